```python
import jax, jax.numpy as jnp
from jax import lax
import numpy as np

D_MODEL = 1024
BATCH = 8
SEQ = 4096
DEPTH = 4

CHUNK = 64
N_LEFT_CHUNKS = 8
BAND_CHUNKS = N_LEFT_CHUNKS + 1
BAND = BAND_CHUNKS * CHUNK
LEFT_PAD = N_LEFT_CHUNKS * CHUNK
MIX_WIDTH = D_MODEL
CONV_WIDTH = MIX_WIDTH // 2
ATTN_WIDTH = MIX_WIDTH - CONV_WIDTH
HEAD_DIM = 64
N_HEADS = ATTN_WIDTH // HEAD_DIM
CONV_GROUPS = 8
CONV_K = 3
REL_CLIP = 128
D_FF = -(-8 * D_MODEL // (3 * 256)) * 256
PROJ_WIDTH = 3 * CONV_WIDTH + 3 * ATTN_WIDTH
EPS = 1e-6
NEG_INF = -1e30

kernel_name = "hybrid_shortconv_chunkattn_sandwich_trunk"


def rms_norm(x, g):
    xf = x.astype(jnp.float32)
    y = xf * lax.rsqrt(jnp.mean(xf * xf, axis=-1, keepdims=True) + EPS)
    return (y * g.astype(jnp.float32)).astype(x.dtype)


def group_rms_norm(y, g, n_groups):
    b, s, w = y.shape
    yf = y.astype(jnp.float32).reshape(b, s, n_groups, w // n_groups)
    yf = yf * lax.rsqrt(jnp.mean(yf * yf, axis=-1, keepdims=True) + EPS)
    return (yf.reshape(b, s, w) * g.astype(jnp.float32)).astype(y.dtype)


def short_gated_conv(h, b_gate, c_gate, w_conv):
    s = h.shape[1]
    u = c_gate * h
    up = jnp.pad(u, ((0, 0), (CONV_K - 1, 0), (0, 0)))
    out = up[:, 0:s] * w_conv[:, 0]
    for k in range(1, CONV_K):
        out = out + up[:, k:k + s] * w_conv[:, k]
    return b_gate * out


def chunked_band_attention(q, k, v, rel_bias):
    b, s, _ = q.shape
    nc = s // CHUNK
    qc = q.reshape(b, nc, CHUNK, N_HEADS, HEAD_DIM) * (HEAD_DIM ** -0.5)
    kp = jnp.pad(k, ((0, 0), (LEFT_PAD, 0), (0, 0))).reshape(b, nc + N_LEFT_CHUNKS, CHUNK, N_HEADS, HEAD_DIM)
    vp = jnp.pad(v, ((0, 0), (LEFT_PAD, 0), (0, 0))).reshape(b, nc + N_LEFT_CHUNKS, CHUNK, N_HEADS, HEAD_DIM)
    kb = jnp.concatenate([kp[:, o:o + nc] for o in range(BAND_CHUNKS)], axis=2)
    vb = jnp.concatenate([vp[:, o:o + nc] for o in range(BAND_CHUNKS)], axis=2)
    scores = jnp.einsum('bcqhd,bckhd->bhcqk', qc, kb).astype(jnp.float32)
    qi = jnp.arange(CHUNK)[:, None]
    kj = jnp.arange(BAND)[None, :]
    rel_idx = jnp.clip(qi - kj + LEFT_PAD, -REL_CLIP, REL_CLIP) + REL_CLIP
    bias = rel_bias.astype(jnp.float32)[:, rel_idx]
    key_pos = jnp.arange(nc)[:, None] * CHUNK - LEFT_PAD + jnp.arange(BAND)[None, :]
    valid = key_pos >= 0
    scores = jnp.where(valid[None, None, :, None, :], scores + bias[None, :, None], NEG_INF)
    p = jax.nn.softmax(scores, axis=-1).astype(v.dtype)
    o = jnp.einsum('bhcqk,bckhd->bcqhd', p, vb)
    return o.reshape(b, s, ATTN_WIDTH)


def _fwd_setup_inputs(seed: int = 0) -> dict:
    key = jax.random.key(seed)
    ks = jax.random.split(key, 14)
    f32 = jnp.float32

    def gain(k, n):
        return 1.0 + 0.1 * jax.random.normal(k, (DEPTH, n), f32)

    return {
        "x": jax.random.normal(ks[0], (BATCH, SEQ, D_MODEL), f32),
        "w_in": jax.random.normal(ks[1], (DEPTH, D_MODEL, PROJ_WIDTH), f32) * D_MODEL ** -0.5,
        "w_conv": jax.random.normal(ks[2], (DEPTH, CONV_WIDTH, CONV_K), f32) * CONV_K ** -0.5,
        "rel_bias": 0.5 * jax.random.normal(ks[3], (DEPTH, N_HEADS, 2 * REL_CLIP + 1), f32),
        "g_conv_out": gain(ks[4], CONV_WIDTH),
        "g_attn_out": gain(ks[5], ATTN_WIDTH),
        "w_out": jax.random.normal(ks[6], (DEPTH, MIX_WIDTH, D_MODEL), f32) * MIX_WIDTH ** -0.5,
        "g_pre_mix": gain(ks[7], D_MODEL),
        "g_post_mix": gain(ks[8], D_MODEL),
        "g_pre_ffn": gain(ks[9], D_MODEL),
        "g_post_ffn": gain(ks[10], D_MODEL),
        "w_ffn_in": jax.random.normal(ks[11], (DEPTH, D_MODEL, 2 * D_FF), f32) * D_MODEL ** -0.5,
        "w_ffn_out": jax.random.normal(ks[12], (DEPTH, D_FF, D_MODEL), f32) * D_FF ** -0.5,
    }


def _fwd_reference(x, w_in, w_conv, rel_bias, g_conv_out, g_attn_out, w_out,
              g_pre_mix, g_post_mix, g_pre_ffn, g_post_ffn, w_ffn_in, w_ffn_out):
    for l in range(DEPTH):
        h = rms_norm(x, g_pre_mix[l])
        proj = jnp.einsum('bsd,dp->bsp', h, w_in[l])
        hc, bg, cg, q, k, v = jnp.split(proj, 6, axis=-1)
        y_conv = short_gated_conv(hc, bg, cg, w_conv[l])
        y_attn = chunked_band_attention(q, k, v, rel_bias[l])
        y = jnp.concatenate([group_rms_norm(y_conv, g_conv_out[l], CONV_GROUPS),
                             group_rms_norm(y_attn, g_attn_out[l], N_HEADS)], axis=-1)
        y = jnp.einsum('bsm,md->bsd', y, w_out[l])
        x = x + rms_norm(y, g_post_mix[l])
        h = rms_norm(x, g_pre_ffn[l])
        gate, up = jnp.split(jnp.einsum('bsd,df->bsf', h, w_ffn_in[l]), 2, axis=-1)
        f = jnp.einsum('bsf,fd->bsd', jax.nn.silu(gate) * up, w_ffn_out[l])
        x = x + rms_norm(f, g_post_ffn[l])
    return x


import jax as _jax
import jax.numpy as _jnp

TWIN_FORMAT = 'train_step'
FWD_PARAMS = ['x', 'w_in', 'w_conv', 'rel_bias', 'g_conv_out', 'g_attn_out', 'w_out', 'g_pre_mix', 'g_post_mix', 'g_pre_ffn', 'g_post_ffn', 'w_ffn_in', 'w_ffn_out']
TWIN_WEIGHTS = ['w_in', 'w_conv', 'rel_bias', 'g_conv_out', 'g_attn_out', 'w_out', 'g_pre_mix', 'g_post_mix', 'g_pre_ffn', 'g_post_ffn', 'w_ffn_in', 'w_ffn_out']
TWIN_DIFF_INPUT = 'x'
TWIN_INPUTS = ['x', 'w_in', 'w_conv', 'rel_bias', 'g_conv_out', 'g_attn_out', 'w_out', 'g_pre_mix', 'g_post_mix', 'g_pre_ffn', 'g_post_ffn', 'w_ffn_in', 'w_ffn_out', 'loss_target', 'm_w_in', 'm_w_conv', 'm_rel_bias', 'm_g_conv_out', 'm_g_attn_out', 'm_w_out', 'm_g_pre_mix', 'm_g_post_mix', 'm_g_pre_ffn', 'm_g_post_ffn', 'm_w_ffn_in', 'm_w_ffn_out', 'v_w_in', 'v_w_conv', 'v_rel_bias', 'v_g_conv_out', 'v_g_attn_out', 'v_w_out', 'v_g_pre_mix', 'v_g_post_mix', 'v_g_pre_ffn', 'v_g_post_ffn', 'v_w_ffn_in', 'v_w_ffn_out']
TWIN_OUTPUTS = ['loss', 'grad_x', 'grad_w_in', 'grad_w_conv', 'grad_rel_bias', 'grad_g_conv_out', 'grad_g_attn_out', 'grad_w_out', 'grad_g_pre_mix', 'grad_g_post_mix', 'grad_g_pre_ffn', 'grad_g_post_ffn', 'grad_w_ffn_in', 'grad_w_ffn_out', 'delta_w_in', 'delta_w_conv', 'delta_rel_bias', 'delta_g_conv_out', 'delta_g_attn_out', 'delta_w_out', 'delta_g_pre_mix', 'delta_g_post_mix', 'delta_g_pre_ffn', 'delta_g_post_ffn', 'delta_w_ffn_in', 'delta_w_ffn_out', 'new_m_w_in', 'new_m_w_conv', 'new_m_rel_bias', 'new_m_g_conv_out', 'new_m_g_attn_out', 'new_m_w_out', 'new_m_g_pre_mix', 'new_m_g_post_mix', 'new_m_g_pre_ffn', 'new_m_g_post_ffn', 'new_m_w_ffn_in', 'new_m_w_ffn_out', 'new_v_w_in', 'new_v_w_conv', 'new_v_rel_bias', 'new_v_g_conv_out', 'new_v_g_attn_out', 'new_v_w_out', 'new_v_g_pre_mix', 'new_v_g_post_mix', 'new_v_g_pre_ffn', 'new_v_g_post_ffn', 'new_v_w_ffn_in', 'new_v_w_ffn_out']
TWIN_LEAF_KINDS = {'loss': 'loss', 'grad_x': 'grad_x', 'grad_w_in': 'grad_w', 'grad_w_conv': 'grad_w', 'grad_rel_bias': 'grad_w', 'grad_g_conv_out': 'grad_w', 'grad_g_attn_out': 'grad_w', 'grad_w_out': 'grad_w', 'grad_g_pre_mix': 'grad_w', 'grad_g_post_mix': 'grad_w', 'grad_g_pre_ffn': 'grad_w', 'grad_g_post_ffn': 'grad_w', 'grad_w_ffn_in': 'grad_w', 'grad_w_ffn_out': 'grad_w', 'delta_w_in': 'delta_w', 'delta_w_conv': 'delta_w', 'delta_rel_bias': 'delta_w', 'delta_g_conv_out': 'delta_w', 'delta_g_attn_out': 'delta_w', 'delta_w_out': 'delta_w', 'delta_g_pre_mix': 'delta_w', 'delta_g_post_mix': 'delta_w', 'delta_g_pre_ffn': 'delta_w', 'delta_g_post_ffn': 'delta_w', 'delta_w_ffn_in': 'delta_w', 'delta_w_ffn_out': 'delta_w', 'new_m_w_in': 'new_m', 'new_m_w_conv': 'new_m', 'new_m_rel_bias': 'new_m', 'new_m_g_conv_out': 'new_m', 'new_m_g_attn_out': 'new_m', 'new_m_w_out': 'new_m', 'new_m_g_pre_mix': 'new_m', 'new_m_g_post_mix': 'new_m', 'new_m_g_pre_ffn': 'new_m', 'new_m_g_post_ffn': 'new_m', 'new_m_w_ffn_in': 'new_m', 'new_m_w_ffn_out': 'new_m', 'new_v_w_in': 'new_v', 'new_v_w_conv': 'new_v', 'new_v_rel_bias': 'new_v', 'new_v_g_conv_out': 'new_v', 'new_v_g_attn_out': 'new_v', 'new_v_w_out': 'new_v', 'new_v_g_pre_mix': 'new_v', 'new_v_g_post_mix': 'new_v', 'new_v_g_pre_ffn': 'new_v', 'new_v_g_post_ffn': 'new_v', 'new_v_w_ffn_in': 'new_v', 'new_v_w_ffn_out': 'new_v'}


def _forward(args):
    return _fwd_reference(*[args[k] for k in FWD_PARAMS])


def _output_shape():
    def fwd():
        inp = _fwd_setup_inputs(0)
        return _fwd_reference(*[inp[k] for k in FWD_PARAMS])
    out = _jax.eval_shape(fwd)
    return out.shape, out.dtype

N_MICROBATCH = 1
ADAM_LR = 0.001
ADAM_B1 = 0.9
ADAM_B2 = 0.999
ADAM_EPS = 1e-08
ADAM_WD = 0.01
ADAM_STEP = 10
PER_EXAMPLE_BATCH_AXIS = {'x': 0, 'loss_target': 0}
SHARED_INPUTS = []
_WEIGHT_DTYPES = {'w_in': _jnp.float32, 'w_conv': _jnp.float32, 'rel_bias': _jnp.float32, 'g_conv_out': _jnp.float32, 'g_attn_out': _jnp.float32, 'w_out': _jnp.float32, 'g_pre_mix': _jnp.float32, 'g_post_mix': _jnp.float32, 'g_pre_ffn': _jnp.float32, 'g_post_ffn': _jnp.float32, 'w_ffn_in': _jnp.float32, 'w_ffn_out': _jnp.float32}
MOMENT_SCALE = {'w_in': 5.292746e+00, 'w_conv': 1.661695e+00, 'rel_bias': 6.596071e-01, 'g_conv_out': 1.641150e+00, 'g_attn_out': 1.218943e+01, 'w_out': 9.242338e+00, 'g_pre_mix': 9.175584e+00, 'g_post_mix': 3.374900e+01, 'g_pre_ffn': 3.343074e+00, 'g_post_ffn': 3.191276e+01, 'w_ffn_in': 1.417408e+00, 'w_ffn_out': 2.755018e+00}


def _to_microbatches(a, axis):
    t = _jnp.moveaxis(a, axis, 0)
    t = t.reshape((N_MICROBATCH, t.shape[0] // N_MICROBATCH) + t.shape[1:])
    return _jnp.moveaxis(t, 1, axis + 1)


def setup_inputs(seed: int = 0) -> dict:
    inp = _fwd_setup_inputs(seed)
    key = _jax.random.fold_in(_jax.random.key(seed), 7919)
    shape, _ = _output_shape()
    out = dict(inp)
    out["loss_target"] = _jax.random.normal(_jax.random.fold_in(key, 0), shape, _jnp.float32)
    for i, name in enumerate(TWIN_WEIGHTS):
        w = inp[name].astype(_jnp.float32)
        if MOMENT_SCALE is None:
            s = _jnp.sqrt(_jnp.mean(_jnp.square(w)) + 1e-30)
        else:
            s = MOMENT_SCALE[name]
        km, kv = _jax.random.split(_jax.random.fold_in(key, i + 1))
        out[name] = w
        out["m_" + name] = s * _jax.random.normal(km, w.shape, _jnp.float32)
        out["v_" + name] = (s * s) * _jax.random.uniform(kv, w.shape, _jnp.float32, 0.5, 1.5)
    if N_MICROBATCH > 1:
        for name, axis in PER_EXAMPLE_BATCH_AXIS.items():
            out[name] = _to_microbatches(out[name], axis)
    return {'x': out['x'], 'w_in': out['w_in'], 'w_conv': out['w_conv'], 'rel_bias': out['rel_bias'], 'g_conv_out': out['g_conv_out'], 'g_attn_out': out['g_attn_out'], 'w_out': out['w_out'], 'g_pre_mix': out['g_pre_mix'], 'g_post_mix': out['g_post_mix'], 'g_pre_ffn': out['g_pre_ffn'], 'g_post_ffn': out['g_post_ffn'], 'w_ffn_in': out['w_ffn_in'], 'w_ffn_out': out['w_ffn_out'], 'loss_target': out['loss_target'], 'm_w_in': out['m_w_in'], 'm_w_conv': out['m_w_conv'], 'm_rel_bias': out['m_rel_bias'], 'm_g_conv_out': out['m_g_conv_out'], 'm_g_attn_out': out['m_g_attn_out'], 'm_w_out': out['m_w_out'], 'm_g_pre_mix': out['m_g_pre_mix'], 'm_g_post_mix': out['m_g_post_mix'], 'm_g_pre_ffn': out['m_g_pre_ffn'], 'm_g_post_ffn': out['m_g_post_ffn'], 'm_w_ffn_in': out['m_w_ffn_in'], 'm_w_ffn_out': out['m_w_ffn_out'], 'v_w_in': out['v_w_in'], 'v_w_conv': out['v_w_conv'], 'v_rel_bias': out['v_rel_bias'], 'v_g_conv_out': out['v_g_conv_out'], 'v_g_attn_out': out['v_g_attn_out'], 'v_w_out': out['v_w_out'], 'v_g_pre_mix': out['v_g_pre_mix'], 'v_g_post_mix': out['v_g_post_mix'], 'v_g_pre_ffn': out['v_g_pre_ffn'], 'v_g_post_ffn': out['v_g_post_ffn'], 'v_w_ffn_in': out['v_w_ffn_in'], 'v_w_ffn_out': out['v_w_ffn_out']}


def _loss(weights, diff, rest, loss_target):
    with _jax.named_scope("forward"):
        args = {**rest, TWIN_DIFF_INPUT: diff, **{k: w.astype(_WEIGHT_DTYPES[k]) for k, w in weights.items()}}
        y = _forward(args)
    with _jax.named_scope("loss_head"):
        err = _jnp.square(y.astype(_jnp.float32) - loss_target)
        return 0.5 * _jnp.sum(_jnp.mean(err, axis=-1)) if err.ndim else 0.5 * err


def _adamw(w, g, m, v):
    m = ADAM_B1 * m + (1.0 - ADAM_B1) * g
    v = ADAM_B2 * v + (1.0 - ADAM_B2) * _jnp.square(g)
    m_hat = m / (1.0 - ADAM_B1 ** ADAM_STEP)
    v_hat = v / (1.0 - ADAM_B2 ** ADAM_STEP)
    delta = -ADAM_LR * (m_hat / (_jnp.sqrt(v_hat) + ADAM_EPS) + ADAM_WD * w)
    return delta, m, v


def reference(x, w_in, w_conv, rel_bias, g_conv_out, g_attn_out, w_out, g_pre_mix, g_post_mix, g_pre_ffn, g_post_ffn, w_ffn_in, w_ffn_out, loss_target, m_w_in, m_w_conv, m_rel_bias, m_g_conv_out, m_g_attn_out, m_w_out, m_g_pre_mix, m_g_post_mix, m_g_pre_ffn, m_g_post_ffn, m_w_ffn_in, m_w_ffn_out, v_w_in, v_w_conv, v_rel_bias, v_g_conv_out, v_g_attn_out, v_w_out, v_g_pre_mix, v_g_post_mix, v_g_pre_ffn, v_g_post_ffn, v_w_ffn_in, v_w_ffn_out):
    given = dict(x=x, w_in=w_in, w_conv=w_conv, rel_bias=rel_bias, g_conv_out=g_conv_out, g_attn_out=g_attn_out, w_out=w_out, g_pre_mix=g_pre_mix, g_post_mix=g_post_mix, g_pre_ffn=g_pre_ffn, g_post_ffn=g_post_ffn, w_ffn_in=w_ffn_in, w_ffn_out=w_ffn_out, loss_target=loss_target, m_w_in=m_w_in, m_w_conv=m_w_conv, m_rel_bias=m_rel_bias, m_g_conv_out=m_g_conv_out, m_g_attn_out=m_g_attn_out, m_w_out=m_w_out, m_g_pre_mix=m_g_pre_mix, m_g_post_mix=m_g_post_mix, m_g_pre_ffn=m_g_pre_ffn, m_g_post_ffn=m_g_post_ffn, m_w_ffn_in=m_w_ffn_in, m_w_ffn_out=m_w_ffn_out, v_w_in=v_w_in, v_w_conv=v_w_conv, v_rel_bias=v_rel_bias, v_g_conv_out=v_g_conv_out, v_g_attn_out=v_g_attn_out, v_w_out=v_w_out, v_g_pre_mix=v_g_pre_mix, v_g_post_mix=v_g_post_mix, v_g_pre_ffn=v_g_pre_ffn, v_g_post_ffn=v_g_post_ffn, v_w_ffn_in=v_w_ffn_in, v_w_ffn_out=v_w_ffn_out)
    weights = {n: given[n] for n in TWIN_WEIGHTS}
    shared = {n: given[n] for n in SHARED_INPUTS}
    per_example = {n: given[n] for n in ['x']}
    grad_fn = _jax.value_and_grad(_loss, argnums=(0, 1))

    def one_microbatch(ex, loss_target):
        ex = dict(ex)
        diff = ex.pop(TWIN_DIFF_INPUT)
        return grad_fn(weights, diff, {**shared, **ex}, loss_target)

    if N_MICROBATCH == 1:
        loss, (grad_w, grad_x) = one_microbatch(per_example, given["loss_target"])
    else:
        def body(carry, xs):
            loss_sum, grad_sum = carry
            l_k, (gw_k, gx_k) = one_microbatch(xs[0], xs[1])
            with _jax.named_scope("update"):
                return (loss_sum + l_k, _jax.tree.map(_jnp.add, grad_sum, gw_k)), gx_k

        init = (_jnp.zeros((), _jnp.float32), _jax.tree.map(_jnp.zeros_like, weights))
        (loss, grad_w), grad_x = _jax.lax.scan(body, init, (per_example, given["loss_target"]))
    with _jax.named_scope("update"):
        delta_w, new_m, new_v = {}, {}, {}
        for n in TWIN_WEIGHTS:
            delta_w[n], new_m[n], new_v[n] = _adamw(weights[n], grad_w[n], given["m_" + n], given["v_" + n])
    return (loss, grad_x, *[grad_w[n] for n in TWIN_WEIGHTS], *[delta_w[n] for n in TWIN_WEIGHTS],
            *[new_m[n] for n in TWIN_WEIGHTS], *[new_v[n] for n in TWIN_WEIGHTS])
```

```python
import functools

import jax
import jax.numpy as jnp
from jax import lax
from jax.experimental import pallas as pl
from jax.experimental.pallas import tpu as pltpu

F32 = jnp.float32
BF16 = jnp.bfloat16

D = 1024
PROJ = 3072
CW = 512
HD = 64
NH = 8
CHUNK = 64
BAND = 576
REL_CLIP = 128
NREL = 2 * REL_CLIP + 1
DFF = 2816
DEPTH = 4
NCHIP = 4
EPS = 1e-6
NEG_INF = -1e30

ADAM_LR = 0.001
ADAM_B1 = 0.9
ADAM_B2 = 0.999
ADAM_EPS = 1e-08
ADAM_WD = 0.01
ADAM_STEP = 10

V7X_VMEM_BYTES = 64 * 1024 * 1024
VMEM_LIMIT = V7X_VMEM_BYTES - 8 * 1024 * 1024
LANES = 128
QG = 2 * CHUNK
KG = QG + BAND - CHUNK
TQ = 512
TM = 256
SMALL_COLS = 1024
MESH = pl.DeviceIdType.MESH
NT = (((1,), (1,)), ((), ()))
TN = (((0,), (0,)), ((), ()))


def _cp(sem=None, vmem=VMEM_LIMIT):
    return pltpu.CompilerParams(dimension_semantics=sem, vmem_limit_bytes=vmem)


def _any():
    return pl.BlockSpec(memory_space=pl.ANY)


def _const(shape):
    nd = len(shape)
    return pl.BlockSpec(shape, lambda *_: (0,) * nd)


def _rms(v, g):
    r = lax.rsqrt(jnp.mean(v * v, axis=-1, keepdims=True) + EPS)
    return v * r * g


def _rms_bwd(dy, v, g):
    r = lax.rsqrt(jnp.mean(v * v, axis=-1, keepdims=True) + EPS)
    vh = v * r
    dg = jnp.sum(dy * vh, axis=0, keepdims=True)
    dvh = dy * g
    dv = r * (dvh - vh * jnp.mean(dvh * vh, axis=-1, keepdims=True))
    return dv, dg


def _group_mean(v, gm):
    hi = v.astype(BF16)
    lo = (v - hi.astype(F32)).astype(BF16)
    return jnp.dot(hi, gm, preferred_element_type=F32) + jnp.dot(lo, gm, preferred_element_type=F32)


def _group_rms_bwd(dy, v, g, gm):
    r = lax.rsqrt(_group_mean(v * v, gm) + EPS)
    vh = v * r
    dg = jnp.sum(dy * vh, axis=0, keepdims=True)
    dvh = dy * g
    dv = r * (dvh - vh * _group_mean(dvh * vh, gm))
    return dv, dg


def _head_masks(scale):
    lane = lax.broadcasted_iota(jnp.int32, (1, LANES), 1)
    return [jnp.where((lane >= HD * a) & (lane < HD * (a + 1)), scale, 0.0).astype(BF16) for a in range(2)]


def _conv_taps(u_prev, u, scr):
    n = u.shape[0]
    scr[0:16, :] = u_prev
    scr[16:16 + n, :] = u
    return scr[15:15 + n, :], scr[14:14 + n, :]


def fwd_inproj(x, g, w_all, layer):
    t = x.shape[0]
    wc = PROJ // NCHIP

    def body(x_ref, g_ref, w_hbm, o_ref, w_v):
        @pl.when(pl.program_id(0) == 0)
        def _():
            pltpu.sync_copy(w_hbm.at[layer], w_v)

        h = _rms(x_ref[...], g_ref[...]).astype(BF16)
        for b in range(NCHIP):
            o_ref[:, wc * b:wc * (b + 1)] = jnp.dot(h, w_v[b], preferred_element_type=F32).astype(BF16)

    return pl.pallas_call(
        body, grid=(t // TQ,),
        in_specs=[pl.BlockSpec((TQ, D), lambda i: (i, 0)), _const((1, D)), _any()],
        out_specs=pl.BlockSpec((TQ, PROJ), lambda i: (i, 0)),
        out_shape=jax.ShapeDtypeStruct((t, PROJ), BF16),
        scratch_shapes=[pltpu.VMEM((NCHIP, D, wc), BF16)],
        compiler_params=_cp(("arbitrary",)), name="fwd_inproj")(x, g, w_all)


def _attn_window_specs():
    return [
        pl.BlockSpec((TQ, CW), lambda i: (i, 3)),
        pl.BlockSpec((TQ, CW), lambda i: (jnp.maximum(i - 1, 0), 4)),
        pl.BlockSpec((TQ, CW), lambda i: (i, 4)),
        pl.BlockSpec((TQ, CW), lambda i: (jnp.maximum(i - 1, 0), 5)),
        pl.BlockSpec((TQ, CW), lambda i: (i, 5)),
    ]


def _conv_specs():
    return [
        pl.BlockSpec((TQ, 3 * CW), lambda i: (i, 0)),
        pl.BlockSpec((16, 3 * CW), lambda i: (jnp.maximum(i * (TQ // 16) - 1, 0), 0)),
    ]


def _conv_fwd(pc_ref, pcp_ref, wc_ref, scr, first):
    pc = pc_ref[...].astype(F32)
    hc, bg, cg = pc[:, :CW], pc[:, CW:2 * CW], pc[:, 2 * CW:]
    u = cg * hc
    pp = pcp_ref[...].astype(F32)
    u_prev = jnp.where(first, 0.0, pp[:, 2 * CW:] * pp[:, :CW])
    u1, u2 = _conv_taps(u_prev, u, scr)
    cout = wc_ref[0:1, :] * u2 + wc_ref[1:2, :] * u1 + wc_ref[2:3, :] * u
    return hc, bg, cg, u, u1, u2, cout


def _key_penalty(first, r0):
    col = lax.broadcasted_iota(jnp.int32, (1, KG), 1)
    limit = jnp.where(first, TQ - r0, 0)
    return jnp.where(col < limit, NEG_INF, 0.0)


def fwd_mix(x, proj, bias2, wconv_t, g_co, g_ao, g_pm, gm, wout_all, layer):
    t = x.shape[0]

    def body(x_ref, pc_ref, pcp_ref, q_ref, kp_ref, kc_ref, vp_ref, vc_ref, b2_ref, wc_ref, gco_ref, gao_ref, gpm_ref,
             gm_ref, wout_hbm, xmid_ref, o_ref, lse_ref, y_ref, z_ref, wout_v, kwin, vwin, cscr):
        i = pl.program_id(0)
        first = i == 0

        @pl.when(first)
        def _():
            pltpu.sync_copy(wout_hbm.at[layer], wout_v)

        kwin[0:TQ, :] = kp_ref[...]
        kwin[TQ:2 * TQ, :] = kc_ref[...]
        vwin[0:TQ, :] = vp_ref[...]
        vwin[TQ:2 * TQ, :] = vc_ref[...]
        qmask = _head_masks(HD ** -0.5)
        vmask = _head_masks(1.0)

        def group(g, carry):
            r0 = pl.multiple_of(g * QG, QG)
            pen = _key_penalty(first, r0)
            for hp in range(NH // 2):
                ls = slice(LANES * hp, LANES * (hp + 1))
                qb = q_ref[pl.ds(r0, QG), ls]
                kw = kwin[pl.ds(r0, KG), ls]
                vw = vwin[pl.ds(r0, KG), ls]
                o_acc = jnp.zeros((QG, LANES), F32)
                lse = jnp.zeros((QG, LANES), F32)
                for a in range(2):
                    s = lax.dot_general(qb * qmask[a], kw, NT, preferred_element_type=F32)
                    s = s + b2_ref[2 * hp + a] + pen
                    m = jnp.max(s, axis=-1, keepdims=True)
                    p = jnp.exp(s - m)
                    l = jnp.sum(p, axis=-1, keepdims=True)
                    o = jnp.dot(p.astype(BF16), vw * vmask[a], preferred_element_type=F32)
                    o_acc = o_acc + o * (1.0 / l)
                    lse = lse + (m + jnp.log(l)) * vmask[a].astype(F32)
                o_ref[pl.ds(r0, QG), ls] = o_acc
                lse_ref[pl.ds(r0, QG), ls] = lse
            return carry

        lax.fori_loop(0, TQ // QG, group, 0)

        _, bg, _, _, _, _, cout = _conv_fwd(pc_ref, pcp_ref, wc_ref, cscr, first)
        yc = bg * cout
        gmv = gm_ref[...]
        ycn = yc * lax.rsqrt(_group_mean(yc * yc, gmv) + EPS) * gco_ref[...]
        oa = o_ref[...]
        oan = oa * lax.rsqrt(_group_mean(oa * oa, gmv) + EPS) * gao_ref[...]
        y_ref[:, 0:CW] = ycn.astype(BF16)
        y_ref[:, CW:2 * CW] = oan.astype(BF16)
        z = jnp.dot(y_ref[...], wout_v[...], preferred_element_type=F32)
        z_ref[...] = z
        xmid_ref[...] = x_ref[...] + _rms(z, gpm_ref[...])

    row = lambda w: pl.BlockSpec((TQ, w), lambda i: (i, 0))
    return pl.pallas_call(
        body, grid=(t // TQ,),
        in_specs=[row(D)] + _conv_specs() + _attn_window_specs() + [
            _const((NH, QG, KG)), _const((8, CW)), _const((1, CW)), _const((1, CW)), _const((1, D)),
            _const((CW, CW)), _any()],
        out_specs=[row(D), row(CW), row(CW), row(D), row(D)],
        out_shape=[jax.ShapeDtypeStruct((t, D), F32), jax.ShapeDtypeStruct((t, CW), F32),
                   jax.ShapeDtypeStruct((t, CW), F32), jax.ShapeDtypeStruct((t, D), BF16),
                   jax.ShapeDtypeStruct((t, D), F32)],
        scratch_shapes=[pltpu.VMEM((D, D), BF16), pltpu.VMEM((2 * TQ, CW), BF16), pltpu.VMEM((2 * TQ, CW), BF16),
                        pltpu.VMEM((TQ + 16, CW), F32)],
        compiler_params=_cp(("arbitrary",)), name="fwd_mix",
    )(x, proj, proj, proj, proj, proj, proj, proj, bias2, wconv_t, g_co, g_ao, g_pm, gm, wout_all)


def fwd_ffn(xmid, g_pre, g_post, wfi_all, wfo_all, layer):
    t = xmid.shape[0]
    hw = DFF // 2

    def body(x_ref, gpre_ref, gpost_ref, wfi_hbm, wfo_hbm, gu_ref, f_ref, xo_ref, wfi_v, wfo_v):
        @pl.when(pl.program_id(0) == 0)
        def _():
            pltpu.sync_copy(wfi_hbm.at[layer], wfi_v)
            pltpu.sync_copy(wfo_hbm.at[layer], wfo_v)

        xv = x_ref[...]
        h = _rms(xv, gpre_ref[...]).astype(BF16)
        f = jnp.zeros((TM, D), F32)
        for j in range(2):
            gate = jnp.dot(h, wfi_v[j], preferred_element_type=F32)
            up = jnp.dot(h, wfi_v[2 + j], preferred_element_type=F32)
            gu_ref[:, hw * j:hw * (j + 1)] = gate.astype(BF16)
            gu_ref[:, DFF + hw * j:DFF + hw * (j + 1)] = up.astype(BF16)
            act = gate * (1.0 / (1.0 + jnp.exp(-gate))) * up
            f = f + jnp.dot(act.astype(BF16), wfo_v[j], preferred_element_type=F32)
        f_ref[...] = f
        xo_ref[...] = xv + _rms(f, gpost_ref[...])

    row = lambda w: pl.BlockSpec((TM, w), lambda i: (i, 0))
    return pl.pallas_call(
        body, grid=(t // TM,),
        in_specs=[row(D), _const((1, D)), _const((1, D)), _any(), _any()],
        out_specs=[row(2 * DFF), row(D), row(D)],
        out_shape=[jax.ShapeDtypeStruct((t, 2 * DFF), BF16), jax.ShapeDtypeStruct((t, D), F32),
                   jax.ShapeDtypeStruct((t, D), F32)],
        scratch_shapes=[pltpu.VMEM((NCHIP, D, hw), BF16), pltpu.VMEM((2, hw, D), BF16)],
        compiler_params=_cp(("arbitrary",)), name="fwd_ffn")(xmid, g_pre, g_post, wfi_all, wfo_all)


def loss_head(y, target):
    t = y.shape[0]

    def body(y_ref, t_ref, dy_ref, l_ref):
        @pl.when(pl.program_id(0) == 0)
        def _():
            l_ref[...] = jnp.zeros_like(l_ref)

        e = y_ref[...] - t_ref[...]
        dy_ref[...] = e * (1.0 / D)
        rows = jnp.sum(e * e, axis=-1, keepdims=True) * (1.0 / D)
        l_ref[...] += 0.5 * jnp.sum(rows, axis=0, keepdims=True)

    row = pl.BlockSpec((TQ, D), lambda i: (i, 0))
    return pl.pallas_call(
        body, grid=(t // TQ,), in_specs=[row, row], out_specs=[row, _const((8, LANES))],
        out_shape=[jax.ShapeDtypeStruct((t, D), F32), jax.ShapeDtypeStruct((8, LANES), F32)],
        compiler_params=_cp(("arbitrary",)), name="loss_head")(y, target)


def bwd_ffn(dx, f, xmid, gu, g_pre, g_post, wfi_all, wfo_all, layer):
    t = dx.shape[0]
    hw = DFF // 2

    def body(dx_ref, f_ref, x_ref, gu_ref, gpre_ref, gpost_ref, wfi_hbm, wfo_hbm,
             dxm_ref, df_ref, act_ref, dgu_ref, h_ref, dgpost_ref, dgpre_ref, wfi_v, wfo_v):
        @pl.when(pl.program_id(0) == 0)
        def _():
            pltpu.sync_copy(wfi_hbm.at[layer], wfi_v)
            pltpu.sync_copy(wfo_hbm.at[layer], wfo_v)
            dgpost_ref[...] = jnp.zeros_like(dgpost_ref)
            dgpre_ref[...] = jnp.zeros_like(dgpre_ref)

        dxo = dx_ref[...]
        df, dgp = _rms_bwd(dxo, f_ref[...], gpost_ref[...])
        dgpost_ref[...] += dgp
        dfb = df.astype(BF16)
        df_ref[...] = dfb
        dh = jnp.zeros((TM, D), F32)
        for j in range(2):
            dact = lax.dot_general(dfb, wfo_v[j], NT, preferred_element_type=F32)
            gate = gu_ref[:, hw * j:hw * (j + 1)].astype(F32)
            up = gu_ref[:, DFF + hw * j:DFF + hw * (j + 1)].astype(F32)
            sig = 1.0 / (1.0 + jnp.exp(-gate))
            silu = gate * sig
            act_ref[:, hw * j:hw * (j + 1)] = (silu * up).astype(BF16)
            dup = (dact * silu).astype(BF16)
            dgate = (dact * up * (sig * (1.0 + gate * (1.0 - sig)))).astype(BF16)
            dgu_ref[:, hw * j:hw * (j + 1)] = dgate
            dgu_ref[:, DFF + hw * j:DFF + hw * (j + 1)] = dup
            dh = dh + lax.dot_general(dgate, wfi_v[j], NT, preferred_element_type=F32)
            dh = dh + lax.dot_general(dup, wfi_v[2 + j], NT, preferred_element_type=F32)
        xv = x_ref[...]
        gpre = gpre_ref[...]
        h_ref[...] = _rms(xv, gpre).astype(BF16)
        dxv, dgq = _rms_bwd(dh, xv, gpre)
        dgpre_ref[...] += dgq
        dxm_ref[...] = dxo + dxv

    row = lambda w: pl.BlockSpec((TM, w), lambda i: (i, 0))
    return pl.pallas_call(
        body, grid=(t // TM,),
        in_specs=[row(D), row(D), row(D), row(2 * DFF), _const((1, D)), _const((1, D)), _any(), _any()],
        out_specs=[row(D), row(D), row(DFF), row(2 * DFF), row(D), _const((1, D)), _const((1, D))],
        out_shape=[jax.ShapeDtypeStruct((t, D), F32), jax.ShapeDtypeStruct((t, D), BF16),
                   jax.ShapeDtypeStruct((t, DFF), BF16), jax.ShapeDtypeStruct((t, 2 * DFF), BF16),
                   jax.ShapeDtypeStruct((t, D), BF16), jax.ShapeDtypeStruct((1, D), F32),
                   jax.ShapeDtypeStruct((1, D), F32)],
        scratch_shapes=[pltpu.VMEM((NCHIP, D, hw), BF16), pltpu.VMEM((2, hw, D), BF16)],
        compiler_params=_cp(("arbitrary",)), name="bwd_ffn")(dx, f, xmid, gu, g_pre, g_post, wfi_all, wfo_all)


def bwd_mix(dxm, z, o, proj, wconv_t, g_co, g_ao, g_pm, gm, wout_all, layer):
    t = dxm.shape[0]

    def body(dx_ref, z_ref, o_ref, pc_ref, pcp_ref, wc_ref, gco_ref, gao_ref, gpm_ref, gm_ref, wout_hbm,
             dz_ref, do_ref, dco_ref, dbg_ref, dgpm_ref, dgco_ref, dgao_ref, wout_v, cscr):
        first = pl.program_id(0) == 0

        @pl.when(first)
        def _():
            pltpu.sync_copy(wout_hbm.at[layer], wout_v)
            dgpm_ref[...] = jnp.zeros_like(dgpm_ref)
            dgco_ref[...] = jnp.zeros_like(dgco_ref)
            dgao_ref[...] = jnp.zeros_like(dgao_ref)

        dz, dgp = _rms_bwd(dx_ref[...], z_ref[...], gpm_ref[...])
        dgpm_ref[...] += dgp
        dzb = dz.astype(BF16)
        dz_ref[...] = dzb
        dy = lax.dot_general(dzb, wout_v[...], NT, preferred_element_type=F32)
        gmv = gm_ref[...]
        _, bg, _, _, _, _, cout = _conv_fwd(pc_ref, pcp_ref, wc_ref, cscr, first)
        dyc, dgc = _group_rms_bwd(dy[:, :CW], bg * cout, gco_ref[...], gmv)
        dgco_ref[...] += dgc
        dbg_ref[...] = (dyc * cout).astype(BF16)
        dco_ref[...] = dyc * bg
        do, dga = _group_rms_bwd(dy[:, CW:], o_ref[...], gao_ref[...], gmv)
        dgao_ref[...] += dga
        do_ref[...] = do.astype(BF16)

    row = lambda w: pl.BlockSpec((TQ, w), lambda i: (i, 0))
    return pl.pallas_call(
        body, grid=(t // TQ,),
        in_specs=[row(D), row(D), row(CW)] + _conv_specs() + [
            _const((8, CW)), _const((1, CW)), _const((1, CW)), _const((1, D)), _const((CW, CW)), _any()],
        out_specs=[row(D), row(CW), row(CW), row(CW), _const((1, D)), _const((1, CW)), _const((1, CW))],
        out_shape=[jax.ShapeDtypeStruct((t, D), BF16), jax.ShapeDtypeStruct((t, CW), BF16),
                   jax.ShapeDtypeStruct((t, CW), F32), jax.ShapeDtypeStruct((t, CW), BF16),
                   jax.ShapeDtypeStruct((1, D), F32), jax.ShapeDtypeStruct((1, CW), F32),
                   jax.ShapeDtypeStruct((1, CW), F32)],
        scratch_shapes=[pltpu.VMEM((D, D), BF16), pltpu.VMEM((TQ + 16, CW), F32)],
        compiler_params=_cp(("arbitrary",)), name="bwd_mix",
    )(dxm, z, o, proj, proj, wconv_t, g_co, g_ao, g_pm, gm, wout_all)


def bwd_conv(dco, proj, wconv_t):
    t = dco.shape[0]
    nt = t // TQ

    def body(d_ref, dn_ref, pc_ref, pcp_ref, wc_ref, dhc_ref, dcg_ref, dw_ref, cscr, dscr):
        i = pl.program_id(0)
        first = i == 0

        @pl.when(first)
        def _():
            dw_ref[...] = jnp.zeros_like(dw_ref)

        hc, _, cg, u, u1, u2, _ = _conv_fwd(pc_ref, pcp_ref, wc_ref, cscr, first)
        d0 = d_ref[...]
        dscr[0:TQ, :] = d0
        dscr[TQ:TQ + 8, :] = jnp.where(i == nt - 1, 0.0, dn_ref[...])
        d1 = dscr[1:TQ + 1, :]
        d2 = dscr[2:TQ + 2, :]
        du = wc_ref[2:3, :] * d0 + wc_ref[1:2, :] * d1 + wc_ref[0:1, :] * d2
        dhc_ref[...] = (du * cg).astype(BF16)
        dcg_ref[...] = (du * hc).astype(BF16)
        dw_ref[0:1, :] += jnp.sum(d0 * u2, axis=0, keepdims=True)
        dw_ref[1:2, :] += jnp.sum(d0 * u1, axis=0, keepdims=True)
        dw_ref[2:3, :] += jnp.sum(d0 * u, axis=0, keepdims=True)

    row = lambda w: pl.BlockSpec((TQ, w), lambda i: (i, 0))
    nxt = pl.BlockSpec((8, CW), lambda i: (jnp.minimum((i + 1) * (TQ // 8), t // 8 - 1), 0))
    return pl.pallas_call(
        body, grid=(nt,),
        in_specs=[row(CW), nxt] + _conv_specs() + [_const((8, CW))],
        out_specs=[row(CW), row(CW), _const((8, CW))],
        out_shape=[jax.ShapeDtypeStruct((t, CW), BF16), jax.ShapeDtypeStruct((t, CW), BF16),
                   jax.ShapeDtypeStruct((8, CW), F32)],
        scratch_shapes=[pltpu.VMEM((TQ + 16, CW), F32), pltpu.VMEM((TQ + 8, CW), F32)],
        compiler_params=_cp(("arbitrary",)), name="bwd_conv")(dco, dco, proj, proj, wconv_t)


def bwd_attn(proj, o, do, lse, bias2):
    t = o.shape[0]
    nt = t // TQ

    def body(q_ref, kp_ref, kc_ref, vp_ref, vc_ref, o_ref, do_ref, lse_ref, b2_ref,
             dq_ref, dk_hbm, dv_hbm, db_hbm, kwin, vwin, dk_acc, dv_acc, db_acc):
        i = pl.program_id(0)
        first = i == 0

        @pl.when(first)
        def _():
            dk_acc[...] = jnp.zeros_like(dk_acc)
            dv_acc[...] = jnp.zeros_like(dv_acc)
            db_acc[...] = jnp.zeros_like(db_acc)

        kwin[0:TQ, :] = kp_ref[...]
        kwin[TQ:2 * TQ, :] = kc_ref[...]
        vwin[0:TQ, :] = vp_ref[...]
        vwin[TQ:2 * TQ, :] = vc_ref[...]
        scale = HD ** -0.5
        qmask = _head_masks(scale)
        vmask = _head_masks(1.0)

        def group(g, carry):
            r0 = pl.multiple_of(g * QG, QG)
            base = pl.multiple_of(i * TQ + r0, QG)
            pen = _key_penalty(first, r0)
            for hp in range(NH // 2):
                ls = slice(LANES * hp, LANES * (hp + 1))
                qb = q_ref[pl.ds(r0, QG), ls]
                kw = kwin[pl.ds(r0, KG), ls]
                vw = vwin[pl.ds(r0, KG), ls]
                dob = do_ref[pl.ds(r0, QG), ls]
                prod = dob.astype(F32) * o_ref[pl.ds(r0, QG), ls]
                lseb = lse_ref[pl.ds(r0, QG), ls]
                dq = jnp.zeros((QG, LANES), F32)
                dk = jnp.zeros((KG, LANES), F32)
                dv = jnp.zeros((KG, LANES), F32)
                for a in range(2):
                    qa = qb * qmask[a]
                    doa = dob * vmask[a]
                    s = lax.dot_general(qa, kw, NT, preferred_element_type=F32)
                    s = s + b2_ref[2 * hp + a] + pen
                    p = jnp.exp(s - lseb[:, HD * a:HD * a + 1])
                    dp = lax.dot_general(doa, vw, NT, preferred_element_type=F32)
                    dsum = jnp.sum(prod * vmask[a].astype(F32), axis=-1, keepdims=True)
                    ds = p * (dp - dsum)
                    db_acc[2 * hp + a] += ds
                    dsb = ds.astype(BF16)
                    dq = dq + jnp.dot(dsb, kw, preferred_element_type=F32) * qmask[a].astype(F32)
                    dk = dk + lax.dot_general(dsb, qa, TN, preferred_element_type=F32)
                    dv = dv + lax.dot_general(p.astype(BF16), doa, TN, preferred_element_type=F32)
                dq_ref[pl.ds(r0, QG), ls] = dq.astype(BF16)
                dk_acc[pl.ds(base, KG), ls] += dk
                dv_acc[pl.ds(base, KG), ls] += dv
            return carry

        lax.fori_loop(0, TQ // QG, group, 0)

        @pl.when(i == nt - 1)
        def _():
            pltpu.sync_copy(dk_acc, dk_hbm)
            pltpu.sync_copy(dv_acc, dv_hbm)
            pltpu.sync_copy(db_acc, db_hbm)

    row = lambda w: pl.BlockSpec((TQ, w), lambda i: (i, 0))
    return pl.pallas_call(
        body, grid=(nt,),
        in_specs=_attn_window_specs() + [row(CW), row(CW), row(CW), _const((NH, QG, KG))],
        out_specs=[row(CW), _any(), _any(), _any()],
        out_shape=[jax.ShapeDtypeStruct((t, CW), BF16), jax.ShapeDtypeStruct((t + TQ, CW), F32),
                   jax.ShapeDtypeStruct((t + TQ, CW), F32), jax.ShapeDtypeStruct((NH, QG, KG), F32)],
        scratch_shapes=[pltpu.VMEM((2 * TQ, CW), BF16), pltpu.VMEM((2 * TQ, CW), BF16),
                        pltpu.VMEM((t + TQ, CW), F32), pltpu.VMEM((t + TQ, CW), F32),
                        pltpu.VMEM((NH, QG, KG), F32)],
        compiler_params=_cp(("arbitrary",)), name="bwd_attn",
    )(proj, proj, proj, proj, proj, o, do, lse, bias2)


def bwd_inproj(dxm, x, dhc, dbg, dcg, dq, dk, dv, g, w_all, layer):
    t = x.shape[0]
    wc = PROJ // NCHIP

    def body(dxm_ref, x_ref, dhc_ref, dbg_ref, dcg_ref, dq_ref, dk_ref, dv_ref, g_ref, w_hbm,
             dx_ref, dp_ref, h_ref, dg_ref, w_v):
        @pl.when(pl.program_id(0) == 0)
        def _():
            pltpu.sync_copy(w_hbm.at[layer], w_v)
            dg_ref[...] = jnp.zeros_like(dg_ref)

        dp_ref[:, 0:CW] = dhc_ref[...]
        dp_ref[:, CW:2 * CW] = dbg_ref[...]
        dp_ref[:, 2 * CW:3 * CW] = dcg_ref[...]
        dp_ref[:, 3 * CW:4 * CW] = dq_ref[...]
        dp_ref[:, 4 * CW:5 * CW] = dk_ref[...].astype(BF16)
        dp_ref[:, 5 * CW:6 * CW] = dv_ref[...].astype(BF16)
        dh = jnp.zeros((TQ, D), F32)
        for b in range(NCHIP):
            dh = dh + lax.dot_general(dp_ref[:, wc * b:wc * (b + 1)], w_v[b], NT, preferred_element_type=F32)
        xv = x_ref[...]
        gv = g_ref[...]
        h_ref[...] = _rms(xv, gv).astype(BF16)
        dxv, dgv = _rms_bwd(dh, xv, gv)
        dg_ref[...] += dgv
        dx_ref[...] = dxm_ref[...] + dxv

    row = lambda w: pl.BlockSpec((TQ, w), lambda i: (i, 0))
    pad = pl.BlockSpec((TQ, CW), lambda i: (i + 1, 0))
    return pl.pallas_call(
        body, grid=(t // TQ,),
        in_specs=[row(D), row(D), row(CW), row(CW), row(CW), row(CW), pad, pad, _const((1, D)), _any()],
        out_specs=[row(D), row(PROJ), row(D), _const((1, D))],
        out_shape=[jax.ShapeDtypeStruct((t, D), F32), jax.ShapeDtypeStruct((t, PROJ), BF16),
                   jax.ShapeDtypeStruct((t, D), BF16), jax.ShapeDtypeStruct((1, D), F32)],
        scratch_shapes=[pltpu.VMEM((NCHIP, D, wc), BF16)],
        compiler_params=_cp(("arbitrary",)), name="bwd_inproj",
    )(dxm, x, dhc, dbg, dcg, dq, dk, dv, g, w_all)


def wgrad(a, b, acc, layer, kb, nb, by_columns, name):
    t, k = a.shape
    n = b.shape[1]
    tk = 512

    def body(a_ref, b_ref, acc_hbm, o_ref):
        del acc_hbm
        o_ref[...] = jnp.zeros_like(o_ref)
        for c in range(t // tk):
            o_ref[...] += lax.dot_general(a_ref[tk * c:tk * (c + 1), :], b_ref[tk * c:tk * (c + 1), :], TN,
                                          preferred_element_type=F32)

    if by_columns:
        assert nb == n // NCHIP
        out_spec = pl.BlockSpec((None, None, kb, nb), lambda ki, ni: (layer, ni, ki, 0))
    else:
        assert nb == n
        out_spec = pl.BlockSpec((None, kb, nb), lambda ki, ni: (layer, ki, 0))
    return pl.pallas_call(
        body, grid=(k // kb, n // nb),
        in_specs=[pl.BlockSpec((t, kb), lambda ki, ni: (0, ki)), pl.BlockSpec((t, nb), lambda ki, ni: (0, ni)), _any()],
        out_specs=out_spec, out_shape=jax.ShapeDtypeStruct(acc.shape, F32), input_output_aliases={2: 0},
        compiler_params=_cp(("arbitrary", "arbitrary")), name=name)(a, b, acc)


NREL_PAD = 384


def _rel_onehot(i):
    r = lax.broadcasted_iota(jnp.int32, (NREL_PAD, BAND), 0)
    j = lax.broadcasted_iota(jnp.int32, (NREL_PAD, BAND), 1)
    idx = jnp.clip(i - j + (BAND - CHUNK), -REL_CLIP, REL_CLIP) + REL_CLIP
    return jnp.where(r == idx, 1.0, 0.0)


def bias_expand(rel_pad):
    def body(r_ref, o_ref):
        def row(i, carry):
            o_ref[i] = jnp.dot(r_ref[...], _rel_onehot(i), preferred_element_type=F32,
                               precision=lax.Precision.HIGHEST)
            return carry
        lax.fori_loop(0, CHUNK, row, 0)

    return pl.pallas_call(body, out_shape=jax.ShapeDtypeStruct((CHUNK, NH, BAND), F32), name="bias_expand")(rel_pad)


def bias_reduce(dbias):
    def body(d_ref, o_ref):
        def row(i, acc):
            return acc + lax.dot_general(d_ref[i], _rel_onehot(i), NT, preferred_element_type=F32,
                                         precision=lax.Precision.HIGHEST)
        o_ref[...] = lax.fori_loop(0, CHUNK, row, jnp.zeros((NH, NREL_PAD), F32))

    return pl.pallas_call(body, out_shape=jax.ShapeDtypeStruct((NH, NREL_PAD), F32), name="bias_reduce")(dbias)


def _bias_to_groups(bias):
    bh = jnp.transpose(bias, (1, 0, 2))
    top = jnp.pad(bh, ((0, 0), (0, 0), (0, CHUNK)), constant_values=NEG_INF)
    bot = jnp.pad(bh, ((0, 0), (0, 0), (CHUNK, 0)), constant_values=NEG_INF)
    return jnp.concatenate([top, bot], axis=1)


def _groups_to_bias(db2):
    dbh = db2[:, :CHUNK, :BAND] + db2[:, CHUNK:, CHUNK:]
    return jnp.transpose(dbh, (1, 0, 2))


def _place():
    x, y, c = lax.axis_index("x"), lax.axis_index("y"), lax.axis_index("c")
    chips = [(1 - x, y), (x, 1 - y), (1 - x, 1 - y)]
    return x, y, c, chips


def _half(ref_rows, c):
    return pl.ds(c * (ref_rows // 2), ref_rows // 2)


def ag_weights(ws):
    n = len(ws)

    def body(*refs):
        ins, outs = refs[:n], refs[n:2 * n]
        send1, recv1, send2, recv2, lsem = refs[2 * n:]
        x, y, c, chips = _place()
        b = 2 * x + y
        local, sends = [], []
        for k in range(n):
            rows = ins[k].shape[1]
            mine = pltpu.make_async_copy(ins[k], outs[k].at[:, b], lsem.at[k])
            mine.start()
            local.append(mine)
            for j, (cx, cy) in enumerate(chips):
                cp = pltpu.make_async_remote_copy(
                    src_ref=ins[k].at[:, _half(rows, c), :], dst_ref=outs[k].at[:, b, _half(rows, c), :],
                    send_sem=send1.at[3 * k + j], recv_sem=recv1.at[3 * k + j],
                    device_id=(cx, cy, c), device_id_type=MESH)
                cp.start()
                sends.append(cp)
        for k in range(n):
            rows = ins[k].shape[1]
            for j, (cx, cy) in enumerate(chips):
                blk = outs[k].at[:, 2 * cx + cy, _half(rows, c), :]
                pltpu.make_async_remote_copy(
                    src_ref=blk, dst_ref=blk, send_sem=send1.at[3 * k + j], recv_sem=recv1.at[3 * k + j],
                    device_id=(cx, cy, c), device_id_type=MESH).wait_recv()
                fw = pltpu.make_async_remote_copy(
                    src_ref=blk, dst_ref=blk, send_sem=send2.at[3 * k + j], recv_sem=recv2.at[3 * k + j],
                    device_id=(x, y, 1 - c), device_id_type=MESH)
                fw.start()
                sends.append(fw)
        for k in range(n):
            rows = ins[k].shape[1]
            for j, (cx, cy) in enumerate(chips):
                blk = outs[k].at[:, 2 * cx + cy, _half(rows, 1 - c), :]
                pltpu.make_async_remote_copy(
                    src_ref=blk, dst_ref=blk, send_sem=send2.at[3 * k + j], recv_sem=recv2.at[3 * k + j],
                    device_id=(x, y, 1 - c), device_id_type=MESH).wait_recv()
        for cp in sends:
            cp.wait_send()
        for cp in local:
            cp.wait()

    out_shape = [jax.ShapeDtypeStruct((w.shape[0], NCHIP) + w.shape[1:], w.dtype) for w in ws]
    return pl.pallas_call(
        body, in_specs=[_any()] * n, out_specs=[_any()] * n, out_shape=out_shape,
        scratch_shapes=[pltpu.SemaphoreType.DMA((3 * n,)), pltpu.SemaphoreType.DMA((3 * n,)),
                        pltpu.SemaphoreType.DMA((3 * n,)), pltpu.SemaphoreType.DMA((3 * n,)),
                        pltpu.SemaphoreType.DMA((n,))],
        name="ag_weights")(*ws)


def pair_exchange(gs):
    n = len(gs)

    def body(*refs):
        ins, outs = refs[:n], refs[n:2 * n]
        send, recv = refs[2 * n:]
        x, y, c, _ = _place()
        cps = []
        for k in range(n):
            rows = ins[k].shape[2]
            cp = pltpu.make_async_remote_copy(
                src_ref=ins[k].at[:, :, _half(rows, 1 - c), :], dst_ref=outs[k],
                send_sem=send.at[k], recv_sem=recv.at[k], device_id=(x, y, 1 - c), device_id_type=MESH)
            cp.start()
            cps.append(cp)
        for cp in cps:
            cp.wait()

    out_shape = [jax.ShapeDtypeStruct(g.shape[:2] + (g.shape[2] // 2, g.shape[3]), g.dtype) for g in gs]
    return pl.pallas_call(
        body, in_specs=[_any()] * n, out_specs=[_any()] * n, out_shape=out_shape,
        scratch_shapes=[pltpu.SemaphoreType.DMA((n,)), pltpu.SemaphoreType.DMA((n,))], name="pair_exchange")(*gs)


def add_pair(g, r1, core):
    nl, ns, rows, cols = g.shape
    hr = rows // 2

    def body(c_ref, g_ref, r_ref, o_ref):
        del c_ref
        o_ref[...] = g_ref[...] + r_ref[...]

    blk = (None, None, hr, cols)
    grid_spec = pltpu.PrefetchScalarGridSpec(
        num_scalar_prefetch=1, grid=(nl, ns),
        in_specs=[pl.BlockSpec(blk, lambda l, s, c: (l, s, c[0], 0)), pl.BlockSpec(blk, lambda l, s, c: (l, s, 0, 0))],
        out_specs=pl.BlockSpec(blk, lambda l, s, c: (l, s, 0, 0)))
    return pl.pallas_call(body, grid_spec=grid_spec, out_shape=jax.ShapeDtypeStruct(r1.shape, F32),
                          compiler_params=_cp(("arbitrary", "arbitrary")), name="add_pair")(core, g, r1)


def ici_scatter(ss):
    n = len(ss)

    def body(*refs):
        ins, outs = refs[:n], refs[n:2 * n]
        send, recv = refs[2 * n:]
        _, _, c, chips = _place()
        cps = []
        for k in range(n):
            for j, (cx, cy) in enumerate(chips):
                cp = pltpu.make_async_remote_copy(
                    src_ref=ins[k].at[:, 2 * cx + cy], dst_ref=outs[k].at[j],
                    send_sem=send.at[3 * k + j], recv_sem=recv.at[3 * k + j],
                    device_id=(cx, cy, c), device_id_type=MESH)
                cp.start()
                cps.append(cp)
        for cp in cps:
            cp.wait()

    out_shape = [jax.ShapeDtypeStruct((3, s.shape[0]) + s.shape[2:], s.dtype) for s in ss]
    return pl.pallas_call(
        body, in_specs=[_any()] * n, out_specs=[_any()] * n, out_shape=out_shape,
        scratch_shapes=[pltpu.SemaphoreType.DMA((3 * n,)), pltpu.SemaphoreType.DMA((3 * n,))],
        name="ici_scatter")(*ss)


def add_chips(s, r2, chip):
    nl, _, hr, cols = s.shape

    def body(b_ref, s_ref, r_ref, o_ref):
        del b_ref
        o_ref[...] = ((s_ref[...] + r_ref[0]) + r_ref[1]) + r_ref[2]

    grid_spec = pltpu.PrefetchScalarGridSpec(
        num_scalar_prefetch=1, grid=(nl,),
        in_specs=[pl.BlockSpec((None, None, hr, cols), lambda l, b: (l, b[0], 0, 0)),
                  pl.BlockSpec((3, None, hr, cols), lambda l, b: (0, l, 0, 0))],
        out_specs=pl.BlockSpec((None, hr, cols), lambda l, b: (l, 0, 0)))
    return pl.pallas_call(body, grid_spec=grid_spec, out_shape=jax.ShapeDtypeStruct((nl, hr, cols), F32),
                          compiler_params=_cp(("arbitrary",)), name="add_chips")(chip, s, r2)


def pair_share(rs):
    n = len(rs)

    def body(*refs):
        ins, outs = refs[:n], refs[n:2 * n]
        send, recv, lsem = refs[2 * n:]
        x, y, c, _ = _place()
        cps = []
        for k in range(n):
            rows = outs[k].shape[1]
            mine = pltpu.make_async_copy(ins[k], outs[k].at[:, _half(rows, c), :], lsem.at[k])
            mine.start()
            cps.append(mine)
            cp = pltpu.make_async_remote_copy(
                src_ref=ins[k], dst_ref=outs[k].at[:, _half(rows, c), :],
                send_sem=send.at[k], recv_sem=recv.at[k], device_id=(x, y, 1 - c), device_id_type=MESH)
            cp.start()
            cps.append(cp)
        for cp in cps:
            cp.wait()

    out_shape = [jax.ShapeDtypeStruct((r.shape[0], 2 * r.shape[1], r.shape[2]), r.dtype) for r in rs]
    return pl.pallas_call(
        body, in_specs=[_any()] * n, out_specs=[_any()] * n, out_shape=out_shape,
        scratch_shapes=[pltpu.SemaphoreType.DMA((n,)), pltpu.SemaphoreType.DMA((n,)), pltpu.SemaphoreType.DMA((n,))],
        name="pair_share")(*rs)


def small_collect(v, reduce, name):
    rows = v.shape[0]
    flips = [(fx, fy, fc) for fx in (0, 1) for fy in (0, 1) for fc in (0, 1)][1:]

    def body(v_ref, o_ref, buf, send, recv):
        x, y, c, _ = _place()
        buf[4 * x + 2 * y + c] = v_ref[...]
        peers = [(jnp.where(fx, 1 - x, x), jnp.where(fy, 1 - y, y), jnp.where(fc, 1 - c, c)) for fx, fy, fc in flips]
        cps = []
        for k, peer in enumerate(peers):
            cp = pltpu.make_async_remote_copy(
                src_ref=v_ref, dst_ref=buf.at[4 * x + 2 * y + c], send_sem=send.at[k], recv_sem=recv.at[k],
                device_id=peer, device_id_type=MESH)
            cp.start()
            cps.append(cp)
        for k, (px, py, pc) in enumerate(peers):
            pltpu.make_async_remote_copy(
                src_ref=v_ref, dst_ref=buf.at[4 * px + 2 * py + pc], send_sem=send.at[k], recv_sem=recv.at[k],
                device_id=(px, py, pc), device_id_type=MESH).wait_recv()
        for cp in cps:
            cp.wait_send()
        if reduce:
            acc = buf[0]
            for s in range(1, 8):
                acc = acc + buf[s]
            o_ref[...] = acc
        else:
            o_ref[...] = buf[...]

    vm = pl.BlockSpec(memory_space=pltpu.VMEM)
    out_shape = jax.ShapeDtypeStruct((rows, SMALL_COLS) if reduce else (8, rows, SMALL_COLS), F32)
    return pl.pallas_call(
        body, in_specs=[vm], out_specs=vm, out_shape=out_shape,
        scratch_shapes=[pltpu.VMEM((8, rows, SMALL_COLS), F32), pltpu.SemaphoreType.DMA((7,)),
                        pltpu.SemaphoreType.DMA((7,))],
        name=name)(v)


def adamw(w, g, m, v, rb, name):
    nl, rows, cols = w.shape

    def body(w_ref, g_ref, m_ref, v_ref, d_ref, nm_ref, nv_ref):
        gv = g_ref[...]
        nm = ADAM_B1 * m_ref[...] + (1.0 - ADAM_B1) * gv
        nv = ADAM_B2 * v_ref[...] + (1.0 - ADAM_B2) * (gv * gv)
        m_hat = nm / (1.0 - ADAM_B1 ** ADAM_STEP)
        v_hat = nv / (1.0 - ADAM_B2 ** ADAM_STEP)
        d_ref[...] = -ADAM_LR * (m_hat / (jnp.sqrt(v_hat) + ADAM_EPS) + ADAM_WD * w_ref[...])
        nm_ref[...] = nm
        nv_ref[...] = nv

    blk = pl.BlockSpec((None, rb, cols), lambda l, r: (l, r, 0))
    shp = jax.ShapeDtypeStruct(w.shape, F32)
    return pl.pallas_call(body, grid=(nl, rows // rb), in_specs=[blk] * 4, out_specs=[blk] * 3, out_shape=[shp] * 3,
                          compiler_params=_cp(("arbitrary", "arbitrary")), name=name)(w, g, m, v)


def _pack(parts, rows):
    flat = jnp.concatenate([p.reshape(-1).astype(F32) for p in parts])
    return jnp.pad(flat, (0, rows * SMALL_COLS - flat.shape[0])).reshape(rows, SMALL_COLS)


def _unpack(vec, shapes):
    flat = vec.reshape(-1)
    out, off = [], 0
    for s in shapes:
        size = 1
        for d in s:
            size *= d
        out.append(flat[off:off + size].reshape(s))
        off += size
    return out


def kernel(x, w_in, w_conv, rel_bias, g_conv_out, g_attn_out, w_out, g_pre_mix, g_post_mix, g_pre_ffn, g_post_ffn, w_ffn_in, w_ffn_out, loss_target, m_w_in, m_w_conv, m_rel_bias, m_g_conv_out, m_g_attn_out, m_w_out, m_g_pre_mix, m_g_post_mix, m_g_pre_ffn, m_g_post_ffn, m_w_ffn_in, m_w_ffn_out, v_w_in, v_w_conv, v_rel_bias, v_g_conv_out, v_g_attn_out, v_w_out, v_g_pre_mix, v_g_post_mix, v_g_pre_ffn, v_g_post_ffn, v_w_ffn_in, v_w_ffn_out):
    xi, yi, ci = lax.axis_index("x"), lax.axis_index("y"), lax.axis_index("c")
    chip = 2 * xi + yi
    nl = w_in.shape[0]
    x0 = x[0]
    target = loss_target[0]
    cwl = CW // NCHIP

    gw_in, gw_out, gw_fi, gw_fo = ag_weights(
        [w_in.astype(BF16), w_out.astype(BF16), w_ffn_in.astype(BF16), w_ffn_out.astype(BF16)])
    wout_all = gw_out.reshape(nl, D, D)
    wfo_all = gw_fo.reshape(nl, 2, DFF // 2, D)
    wc_all = small_collect(_pack([w_conv], 8), False, "gather_w_conv")
    wc_full = wc_all[0::2].reshape(NCHIP, -1)[:, :nl * cwl * 3].reshape(NCHIP, nl, cwl, 3)
    wc_full = jnp.transpose(wc_full, (1, 0, 2, 3)).reshape(nl, CW, 3)
    wconv_t = jnp.pad(jnp.transpose(wc_full, (0, 2, 1)), ((0, 0), (0, 5), (0, 0)))
    rel_pad = jnp.pad(rel_bias, ((0, 0), (0, 0), (0, NREL_PAD - NREL)))
    gm = jnp.kron(jnp.eye(CW // HD, dtype=F32), jnp.full((HD, HD), 1.0 / HD, F32)).astype(BF16)
    row = lambda a, l: a[l][None, :]

    saved = []
    h = x0
    for l in range(nl):
        bias2 = _bias_to_groups(bias_expand(rel_pad[l]))
        proj = fwd_inproj(h, row(g_pre_mix, l), gw_in, l)
        xmid, o, lse, y, z = fwd_mix(h, proj, bias2, wconv_t[l], row(g_conv_out, l), row(g_attn_out, l),
                                     row(g_post_mix, l), gm, wout_all, l)
        gu, f, xout = fwd_ffn(xmid, row(g_pre_ffn, l), row(g_post_ffn, l), gw_fi, wfo_all, l)
        saved.append((h, proj, bias2, xmid, o, lse, y, z, gu, f))
        h = xout
    dx, loss_blk = loss_head(h, target)

    acc_in = jnp.zeros((nl, NCHIP, D, PROJ // NCHIP), F32)
    acc_out = jnp.zeros((nl, D, D), F32)
    acc_fi = jnp.zeros((nl, NCHIP, D, 2 * DFF // NCHIP), F32)
    acc_fo = jnp.zeros((nl, DFF, D), F32)
    small = {k: [None] * nl for k in ("co", "ao", "pm", "qm", "pf", "qf", "rel", "wc")}
    for l in reversed(range(nl)):
        hin, proj, bias2, xmid, o, lse, y, z, gu, f = saved[l]
        dxm, dfb, act, dgu, h2, dg_qf, dg_pf = bwd_ffn(dx, f, xmid, gu, row(g_pre_ffn, l), row(g_post_ffn, l),
                                                        gw_fi, wfo_all, l)
        acc_fo = wgrad(act, dfb, acc_fo, l, 256, D, False, "wgrad_ffn_out")
        acc_fi = wgrad(h2, dgu, acc_fi, l, 512, 2 * DFF // NCHIP, True, "wgrad_ffn_in")
        dzb, do, dco, dbg, dg_qm, dg_co, dg_ao = bwd_mix(dxm, z, o, proj, wconv_t[l], row(g_conv_out, l),
                                                          row(g_attn_out, l), row(g_post_mix, l), gm, wout_all, l)
        acc_out = wgrad(y, dzb, acc_out, l, 512, D, False, "wgrad_out")
        dhc, dcg, dwc = bwd_conv(dco, proj, wconv_t[l])
        dq, dk, dv, db2 = bwd_attn(proj, o, do, lse, bias2)
        dx, dproj, hb, dg_pm = bwd_inproj(dxm, hin, dhc, dbg, dcg, dq, dk, dv, row(g_pre_mix, l), gw_in, l)
        acc_in = wgrad(hb, dproj, acc_in, l, 512, PROJ // NCHIP, True, "wgrad_in")
        small["co"][l], small["ao"][l], small["pm"][l], small["qm"][l] = dg_co, dg_ao, dg_pm, dg_qm
        small["pf"][l], small["qf"][l] = dg_pf, dg_qf
        small["rel"][l] = bias_reduce(_groups_to_bias(db2))[:, :NREL]
        small["wc"][l] = jnp.transpose(dwc[0:3], (1, 0))

    grads = [acc_in, acc_out.reshape(nl, NCHIP, D // NCHIP, D), acc_fi, acc_fo.reshape(nl, NCHIP, DFF // NCHIP, D)]
    core = ci.reshape(1).astype(jnp.int32)
    chip1 = chip.reshape(1).astype(jnp.int32)
    from_sibling = pair_exchange(grads)
    pair_sums = [add_pair(g, r, core) for g, r in zip(grads, from_sibling)]
    from_chips = ici_scatter(pair_sums)
    halves = [add_chips(s, r, chip1) for s, r in zip(pair_sums, from_chips)]
    gr_in, gr_out, gr_fi, gr_fo = pair_share(halves)

    order = ("co", "ao", "pm", "qm", "pf", "qf", "rel", "wc")
    parts = [jnp.stack(small[k]) for k in order] + [loss_blk[0:1, 0:1]]
    shapes = [p.shape for p in parts]
    red = _unpack(small_collect(_pack(parts, 40), True, "reduce_small"), shapes)
    gr_co, gr_ao, gr_pm, gr_qm, gr_pf, gr_qf, gr_rel, gr_wc_full, loss = red
    gr_co, gr_ao, gr_pm, gr_qm, gr_pf, gr_qf = [a.reshape(nl, -1) for a in (gr_co, gr_ao, gr_pm, gr_qm, gr_pf, gr_qf)]
    gr_wc = lax.dynamic_slice_in_dim(gr_wc_full, chip * cwl, cwl, axis=1)
    loss = loss.reshape(())

    big = []
    for w, g, m, v, name in ((w_in, gr_in, m_w_in, v_w_in, "adamw_in"), (w_out, gr_out, m_w_out, v_w_out, "adamw_out"),
                             (w_ffn_in, gr_fi, m_w_ffn_in, v_w_ffn_in, "adamw_ffn_in"),
                             (w_ffn_out, gr_fo, m_w_ffn_out, v_w_ffn_out, "adamw_ffn_out")):
        big.append(adamw(w, g, m, v, w.shape[1] // 4, name))
    sw = [g_conv_out, g_attn_out, g_pre_mix, g_post_mix, g_pre_ffn, g_post_ffn, rel_bias, w_conv]
    sg = [gr_co, gr_ao, gr_pm, gr_qm, gr_pf, gr_qf, gr_rel, gr_wc]
    sm = [m_g_conv_out, m_g_attn_out, m_g_pre_mix, m_g_post_mix, m_g_pre_ffn, m_g_post_ffn, m_rel_bias, m_w_conv]
    sv = [v_g_conv_out, v_g_attn_out, v_g_pre_mix, v_g_post_mix, v_g_pre_ffn, v_g_post_ffn, v_rel_bias, v_w_conv]
    sshapes = [a.shape for a in sw]
    packed = [_pack(a, 32)[None] for a in (sw, sg, sm, sv)]
    s_out = [_unpack(a[0], sshapes) for a in adamw(*packed, 32, "adamw_small")]

    def leaves(big_i, small_i):
        b_in, b_out, b_fi, b_fo = big_i
        s_co, s_ao, s_pm, s_qm, s_pf, s_qf, s_rel, s_wc = small_i
        return [b_in, s_wc, s_rel, s_co, s_ao, b_out, s_pm, s_qm, s_pf, s_qf, b_fi, b_fo]

    out = [loss, dx[None]]
    out += leaves((gr_in, gr_out, gr_fi, gr_fo), sg)
    for i in range(3):
        out += leaves([b[i] for b in big], s_out[i])
    return tuple(out)
```

```python
import functools

import jax
import jax.numpy as jnp
from jax import lax
from jax.experimental import pallas as pl
from jax.experimental.pallas import tpu as pltpu

F32 = jnp.float32
BF16 = jnp.bfloat16

D = 1024
PROJ = 3072
CW = 512
HD = 64
NH = 8
CHUNK = 64
BAND = 576
REL_CLIP = 128
NREL = 2 * REL_CLIP + 1
DFF = 2816
DEPTH = 4
NCHIP = 4
EPS = 1e-6
NEG_INF = -1e30

ADAM_LR = 0.001
ADAM_B1 = 0.9
ADAM_B2 = 0.999
ADAM_EPS = 1e-08
ADAM_WD = 0.01
ADAM_STEP = 10

V7X_VMEM_BYTES = 64 * 1024 * 1024
VMEM_LIMIT = V7X_VMEM_BYTES - 8 * 1024 * 1024
LANES = 128
QG = 2 * CHUNK
KG = QG + BAND - CHUNK
TQ = 512
TM = 256
SMALL_COLS = 1024
MESH = pl.DeviceIdType.MESH
NT = (((1,), (1,)), ((), ()))
TN = (((0,), (0,)), ((), ()))


def _cp(sem=None, vmem=VMEM_LIMIT):
    return pltpu.CompilerParams(dimension_semantics=sem, vmem_limit_bytes=vmem)


def _any():
    return pl.BlockSpec(memory_space=pl.ANY)


def _const(shape):
    nd = len(shape)
    return pl.BlockSpec(shape, lambda *_: (0,) * nd)


def _rms(v, g):
    r = lax.rsqrt(jnp.mean(v * v, axis=-1, keepdims=True) + EPS)
    return v * r * g


def _rms_bwd(dy, v, g):
    r = lax.rsqrt(jnp.mean(v * v, axis=-1, keepdims=True) + EPS)
    vh = v * r
    dg = jnp.sum(dy * vh, axis=0, keepdims=True)
    dvh = dy * g
    dv = r * (dvh - vh * jnp.mean(dvh * vh, axis=-1, keepdims=True))
    return dv, dg


def _group_mean(v, gm):
    hi = v.astype(BF16)
    lo = (v - hi.astype(F32)).astype(BF16)
    return jnp.dot(hi, gm, preferred_element_type=F32) + jnp.dot(lo, gm, preferred_element_type=F32)


def _group_rms_bwd(dy, v, g, gm):
    r = lax.rsqrt(_group_mean(v * v, gm) + EPS)
    vh = v * r
    dg = jnp.sum(dy * vh, axis=0, keepdims=True)
    dvh = dy * g
    dv = r * (dvh - vh * _group_mean(dvh * vh, gm))
    return dv, dg


def _head_masks(scale):
    lane = lax.broadcasted_iota(jnp.int32, (1, LANES), 1)
    return [jnp.where((lane >= HD * a) & (lane < HD * (a + 1)), scale, 0.0).astype(BF16) for a in range(2)]


def _conv_taps(u_prev, u, scr):
    n = u.shape[0]
    scr[0:16, :] = u_prev
    scr[16:16 + n, :] = u
    return scr[15:15 + n, :], scr[14:14 + n, :]


def fwd_inproj(x, g, w_all, layer):
    t = x.shape[0]
    wc = PROJ // NCHIP

    def body(x_ref, g_ref, w_hbm, o_ref, w_v):
        @pl.when(pl.program_id(0) == 0)
        def _():
            pltpu.sync_copy(w_hbm.at[layer], w_v)

        h = _rms(x_ref[...], g_ref[...]).astype(BF16)
        for b in range(NCHIP):
            o_ref[:, wc * b:wc * (b + 1)] = jnp.dot(h, w_v[b], preferred_element_type=F32).astype(BF16)

    return pl.pallas_call(
        body, grid=(t // TQ,),
        in_specs=[pl.BlockSpec((TQ, D), lambda i: (i, 0)), _const((1, D)), _any()],
        out_specs=pl.BlockSpec((TQ, PROJ), lambda i: (i, 0)),
        out_shape=jax.ShapeDtypeStruct((t, PROJ), BF16),
        scratch_shapes=[pltpu.VMEM((NCHIP, D, wc), BF16)],
        compiler_params=_cp(("arbitrary",)), name="fwd_inproj")(x, g, w_all)


def _attn_window_specs():
    return [
        pl.BlockSpec((TQ, CW), lambda i: (i, 3)),
        pl.BlockSpec((TQ, CW), lambda i: (jnp.maximum(i - 1, 0), 4)),
        pl.BlockSpec((TQ, CW), lambda i: (i, 4)),
        pl.BlockSpec((TQ, CW), lambda i: (jnp.maximum(i - 1, 0), 5)),
        pl.BlockSpec((TQ, CW), lambda i: (i, 5)),
    ]


def _conv_specs():
    return [
        pl.BlockSpec((TQ, 3 * CW), lambda i: (i, 0)),
        pl.BlockSpec((16, 3 * CW), lambda i: (jnp.maximum(i * (TQ // 16) - 1, 0), 0)),
    ]


def _conv_fwd(pc_ref, pcp_ref, wc_ref, scr, first):
    pc = pc_ref[...].astype(F32)
    hc, bg, cg = pc[:, :CW], pc[:, CW:2 * CW], pc[:, 2 * CW:]
    u = cg * hc
    pp = pcp_ref[...].astype(F32)
    u_prev = jnp.where(first, 0.0, pp[:, 2 * CW:] * pp[:, :CW])
    u1, u2 = _conv_taps(u_prev, u, scr)
    cout = wc_ref[0:1, :] * u2 + wc_ref[1:2, :] * u1 + wc_ref[2:3, :] * u
    return hc, bg, cg, u, u1, u2, cout


def _key_penalty(first, r0):
    col = lax.broadcasted_iota(jnp.int32, (1, KG), 1)
    limit = jnp.where(first, TQ - r0, 0)
    return jnp.where(col < limit, NEG_INF, 0.0)


def fwd_mix(x, proj, bias2, wconv_t, g_co, g_ao, g_pm, gm, wout_all, layer):
    t = x.shape[0]

    def body(x_ref, pc_ref, pcp_ref, q_ref, kp_ref, kc_ref, vp_ref, vc_ref, b2_ref, wc_ref, gco_ref, gao_ref, gpm_ref,
             gm_ref, wout_hbm, xmid_ref, o_ref, lse_ref, y_ref, z_ref, wout_v, kwin, vwin, cscr):
        i = pl.program_id(0)
        first = i == 0

        @pl.when(first)
        def _():
            pltpu.sync_copy(wout_hbm.at[layer], wout_v)

        kwin[0:TQ, :] = kp_ref[...]
        kwin[TQ:2 * TQ, :] = kc_ref[...]
        vwin[0:TQ, :] = vp_ref[...]
        vwin[TQ:2 * TQ, :] = vc_ref[...]
        qmask = _head_masks(HD ** -0.5)
        vmask = _head_masks(1.0)

        def group(g, carry):
            r0 = pl.multiple_of(g * QG, QG)
            pen = _key_penalty(first, r0)
            for hp in range(NH // 2):
                ls = slice(LANES * hp, LANES * (hp + 1))
                qb = q_ref[pl.ds(r0, QG), ls]
                kw = kwin[pl.ds(r0, KG), ls]
                vw = vwin[pl.ds(r0, KG), ls]
                o_acc = jnp.zeros((QG, LANES), F32)
                lse = jnp.zeros((QG, LANES), F32)
                for a in range(2):
                    s = lax.dot_general(qb * qmask[a], kw, NT, preferred_element_type=F32)
                    s = s + b2_ref[2 * hp + a] + pen
                    m = jnp.max(s, axis=-1, keepdims=True)
                    p = jnp.exp(s - m)
                    l = jnp.sum(p, axis=-1, keepdims=True)
                    o = jnp.dot(p.astype(BF16), vw * vmask[a], preferred_element_type=F32)
                    o_acc = o_acc + o * (1.0 / l)
                    lse = lse + (m + jnp.log(l)) * vmask[a].astype(F32)
                o_ref[pl.ds(r0, QG), ls] = o_acc
                lse_ref[pl.ds(r0, QG), ls] = lse
            return carry

        lax.fori_loop(0, TQ // QG, group, 0)

        _, bg, _, _, _, _, cout = _conv_fwd(pc_ref, pcp_ref, wc_ref, cscr, first)
        yc = bg * cout
        gmv = gm_ref[...]
        ycn = yc * lax.rsqrt(_group_mean(yc * yc, gmv) + EPS) * gco_ref[...]
        oa = o_ref[...]
        oan = oa * lax.rsqrt(_group_mean(oa * oa, gmv) + EPS) * gao_ref[...]
        y_ref[:, 0:CW] = ycn.astype(BF16)
        y_ref[:, CW:2 * CW] = oan.astype(BF16)
        z = jnp.dot(y_ref[...], wout_v[...], preferred_element_type=F32)
        z_ref[...] = z
        xmid_ref[...] = x_ref[...] + _rms(z, gpm_ref[...])

    row = lambda w: pl.BlockSpec((TQ, w), lambda i: (i, 0))
    return pl.pallas_call(
        body, grid=(t // TQ,),
        in_specs=[row(D)] + _conv_specs() + _attn_window_specs() + [
            _const((NH, QG, KG)), _const((8, CW)), _const((1, CW)), _const((1, CW)), _const((1, D)),
            _const((CW, CW)), _any()],
        out_specs=[row(D), row(CW), row(CW), row(D), row(D)],
        out_shape=[jax.ShapeDtypeStruct((t, D), F32), jax.ShapeDtypeStruct((t, CW), F32),
                   jax.ShapeDtypeStruct((t, CW), F32), jax.ShapeDtypeStruct((t, D), BF16),
                   jax.ShapeDtypeStruct((t, D), F32)],
        scratch_shapes=[pltpu.VMEM((D, D), BF16), pltpu.VMEM((2 * TQ, CW), BF16), pltpu.VMEM((2 * TQ, CW), BF16),
                        pltpu.VMEM((TQ + 16, CW), F32)],
        compiler_params=_cp(("arbitrary",)), name="fwd_mix",
    )(x, proj, proj, proj, proj, proj, proj, proj, bias2, wconv_t, g_co, g_ao, g_pm, gm, wout_all)


def fwd_ffn(xmid, g_pre, g_post, wfi_all, wfo_all, layer):
    t = xmid.shape[0]
    hw = DFF // 2

    def body(x_ref, gpre_ref, gpost_ref, wfi_hbm, wfo_hbm, gu_ref, f_ref, xo_ref, wfi_v, wfo_v):
        @pl.when(pl.program_id(0) == 0)
        def _():
            pltpu.sync_copy(wfi_hbm.at[layer], wfi_v)
            pltpu.sync_copy(wfo_hbm.at[layer], wfo_v)

        xv = x_ref[...]
        h = _rms(xv, gpre_ref[...]).astype(BF16)
        f = jnp.zeros((TM, D), F32)
        for j in range(2):
            gate = jnp.dot(h, wfi_v[j], preferred_element_type=F32)
            up = jnp.dot(h, wfi_v[2 + j], preferred_element_type=F32)
            gu_ref[:, hw * j:hw * (j + 1)] = gate.astype(BF16)
            gu_ref[:, DFF + hw * j:DFF + hw * (j + 1)] = up.astype(BF16)
            act = gate * (1.0 / (1.0 + jnp.exp(-gate))) * up
            f = f + jnp.dot(act.astype(BF16), wfo_v[j], preferred_element_type=F32)
        f_ref[...] = f
        xo_ref[...] = xv + _rms(f, gpost_ref[...])

    row = lambda w: pl.BlockSpec((TM, w), lambda i: (i, 0))
    return pl.pallas_call(
        body, grid=(t // TM,),
        in_specs=[row(D), _const((1, D)), _const((1, D)), _any(), _any()],
        out_specs=[row(2 * DFF), row(D), row(D)],
        out_shape=[jax.ShapeDtypeStruct((t, 2 * DFF), BF16), jax.ShapeDtypeStruct((t, D), F32),
                   jax.ShapeDtypeStruct((t, D), F32)],
        scratch_shapes=[pltpu.VMEM((NCHIP, D, hw), BF16), pltpu.VMEM((2, hw, D), BF16)],
        compiler_params=_cp(("arbitrary",)), name="fwd_ffn")(xmid, g_pre, g_post, wfi_all, wfo_all)


def loss_head(y, target):
    t = y.shape[0]

    def body(y_ref, t_ref, dy_ref, l_ref):
        @pl.when(pl.program_id(0) == 0)
        def _():
            l_ref[...] = jnp.zeros_like(l_ref)

        e = y_ref[...] - t_ref[...]
        dy_ref[...] = e * (1.0 / D)
        rows = jnp.sum(e * e, axis=-1, keepdims=True) * (1.0 / D)
        l_ref[...] += 0.5 * jnp.sum(rows, axis=0, keepdims=True)

    row = pl.BlockSpec((TQ, D), lambda i: (i, 0))
    return pl.pallas_call(
        body, grid=(t // TQ,), in_specs=[row, row], out_specs=[row, _const((8, LANES))],
        out_shape=[jax.ShapeDtypeStruct((t, D), F32), jax.ShapeDtypeStruct((8, LANES), F32)],
        compiler_params=_cp(("arbitrary",)), name="loss_head")(y, target)


def bwd_ffn(dx, f, xmid, gu, g_pre, g_post, wfi_all, wfo_all, layer):
    t = dx.shape[0]
    hw = DFF // 2

    def body(dx_ref, f_ref, x_ref, gu_ref, gpre_ref, gpost_ref, wfi_hbm, wfo_hbm,
             dxm_ref, df_ref, act_ref, dgu_ref, h_ref, dgpost_ref, dgpre_ref, wfi_v, wfo_v):
        @pl.when(pl.program_id(0) == 0)
        def _():
            pltpu.sync_copy(wfi_hbm.at[layer], wfi_v)
            pltpu.sync_copy(wfo_hbm.at[layer], wfo_v)
            dgpost_ref[...] = jnp.zeros_like(dgpost_ref)
            dgpre_ref[...] = jnp.zeros_like(dgpre_ref)

        dxo = dx_ref[...]
        df, dgp = _rms_bwd(dxo, f_ref[...], gpost_ref[...])
        dgpost_ref[...] += dgp
        dfb = df.astype(BF16)
        df_ref[...] = dfb
        dh = jnp.zeros((TM, D), F32)
        for j in range(2):
            dact = lax.dot_general(dfb, wfo_v[j], NT, preferred_element_type=F32)
            gate = gu_ref[:, hw * j:hw * (j + 1)].astype(F32)
            up = gu_ref[:, DFF + hw * j:DFF + hw * (j + 1)].astype(F32)
            sig = 1.0 / (1.0 + jnp.exp(-gate))
            silu = gate * sig
            act_ref[:, hw * j:hw * (j + 1)] = (silu * up).astype(BF16)
            dup = (dact * silu).astype(BF16)
            dgate = (dact * up * (sig * (1.0 + gate * (1.0 - sig)))).astype(BF16)
            dgu_ref[:, hw * j:hw * (j + 1)] = dgate
            dgu_ref[:, DFF + hw * j:DFF + hw * (j + 1)] = dup
            dh = dh + lax.dot_general(dgate, wfi_v[j], NT, preferred_element_type=F32)
            dh = dh + lax.dot_general(dup, wfi_v[2 + j], NT, preferred_element_type=F32)
        xv = x_ref[...]
        gpre = gpre_ref[...]
        h_ref[...] = _rms(xv, gpre).astype(BF16)
        dxv, dgq = _rms_bwd(dh, xv, gpre)
        dgpre_ref[...] += dgq
        dxm_ref[...] = dxo + dxv

    row = lambda w: pl.BlockSpec((TM, w), lambda i: (i, 0))
    return pl.pallas_call(
        body, grid=(t // TM,),
        in_specs=[row(D), row(D), row(D), row(2 * DFF), _const((1, D)), _const((1, D)), _any(), _any()],
        out_specs=[row(D), row(D), row(DFF), row(2 * DFF), row(D), _const((1, D)), _const((1, D))],
        out_shape=[jax.ShapeDtypeStruct((t, D), F32), jax.ShapeDtypeStruct((t, D), BF16),
                   jax.ShapeDtypeStruct((t, DFF), BF16), jax.ShapeDtypeStruct((t, 2 * DFF), BF16),
                   jax.ShapeDtypeStruct((t, D), BF16), jax.ShapeDtypeStruct((1, D), F32),
                   jax.ShapeDtypeStruct((1, D), F32)],
        scratch_shapes=[pltpu.VMEM((NCHIP, D, hw), BF16), pltpu.VMEM((2, hw, D), BF16)],
        compiler_params=_cp(("arbitrary",)), name="bwd_ffn")(dx, f, xmid, gu, g_pre, g_post, wfi_all, wfo_all)


def bwd_mix(dxm, z, o, proj, wconv_t, g_co, g_ao, g_pm, gm, wout_all, layer):
    t = dxm.shape[0]

    def body(dx_ref, z_ref, o_ref, pc_ref, pcp_ref, wc_ref, gco_ref, gao_ref, gpm_ref, gm_ref, wout_hbm,
             dz_ref, do_ref, dco_ref, dbg_ref, dgpm_ref, dgco_ref, dgao_ref, wout_v, cscr):
        first = pl.program_id(0) == 0

        @pl.when(first)
        def _():
            pltpu.sync_copy(wout_hbm.at[layer], wout_v)
            dgpm_ref[...] = jnp.zeros_like(dgpm_ref)
            dgco_ref[...] = jnp.zeros_like(dgco_ref)
            dgao_ref[...] = jnp.zeros_like(dgao_ref)

        dz, dgp = _rms_bwd(dx_ref[...], z_ref[...], gpm_ref[...])
        dgpm_ref[...] += dgp
        dzb = dz.astype(BF16)
        dz_ref[...] = dzb
        dy = lax.dot_general(dzb, wout_v[...], NT, preferred_element_type=F32)
        gmv = gm_ref[...]
        _, bg, _, _, _, _, cout = _conv_fwd(pc_ref, pcp_ref, wc_ref, cscr, first)
        dyc, dgc = _group_rms_bwd(dy[:, :CW], bg * cout, gco_ref[...], gmv)
        dgco_ref[...] += dgc
        dbg_ref[...] = (dyc * cout).astype(BF16)
        dco_ref[...] = dyc * bg
        do, dga = _group_rms_bwd(dy[:, CW:], o_ref[...], gao_ref[...], gmv)
        dgao_ref[...] += dga
        do_ref[...] = do.astype(BF16)

    row = lambda w: pl.BlockSpec((TQ, w), lambda i: (i, 0))
    return pl.pallas_call(
        body, grid=(t // TQ,),
        in_specs=[row(D), row(D), row(CW)] + _conv_specs() + [
            _const((8, CW)), _const((1, CW)), _const((1, CW)), _const((1, D)), _const((CW, CW)), _any()],
        out_specs=[row(D), row(CW), row(CW), row(CW), _const((1, D)), _const((1, CW)), _const((1, CW))],
        out_shape=[jax.ShapeDtypeStruct((t, D), BF16), jax.ShapeDtypeStruct((t, CW), BF16),
                   jax.ShapeDtypeStruct((t, CW), F32), jax.ShapeDtypeStruct((t, CW), BF16),
                   jax.ShapeDtypeStruct((1, D), F32), jax.ShapeDtypeStruct((1, CW), F32),
                   jax.ShapeDtypeStruct((1, CW), F32)],
        scratch_shapes=[pltpu.VMEM((D, D), BF16), pltpu.VMEM((TQ + 16, CW), F32)],
        compiler_params=_cp(("arbitrary",)), name="bwd_mix",
    )(dxm, z, o, proj, proj, wconv_t, g_co, g_ao, g_pm, gm, wout_all)


def bwd_conv(dco, proj, wconv_t):
    t = dco.shape[0]
    nt = t // TQ

    def body(d_ref, dn_ref, pc_ref, pcp_ref, wc_ref, dhc_ref, dcg_ref, dw_ref, cscr, dscr):
        i = pl.program_id(0)
        first = i == 0

        @pl.when(first)
        def _():
            dw_ref[...] = jnp.zeros_like(dw_ref)

        hc, _, cg, u, u1, u2, _ = _conv_fwd(pc_ref, pcp_ref, wc_ref, cscr, first)
        d0 = d_ref[...]
        dscr[0:TQ, :] = d0
        dscr[TQ:TQ + 8, :] = jnp.where(i == nt - 1, 0.0, dn_ref[...])
        d1 = dscr[1:TQ + 1, :]
        d2 = dscr[2:TQ + 2, :]
        du = wc_ref[2:3, :] * d0 + wc_ref[1:2, :] * d1 + wc_ref[0:1, :] * d2
        dhc_ref[...] = (du * cg).astype(BF16)
        dcg_ref[...] = (du * hc).astype(BF16)
        dw_ref[0:1, :] += jnp.sum(d0 * u2, axis=0, keepdims=True)
        dw_ref[1:2, :] += jnp.sum(d0 * u1, axis=0, keepdims=True)
        dw_ref[2:3, :] += jnp.sum(d0 * u, axis=0, keepdims=True)

    row = lambda w: pl.BlockSpec((TQ, w), lambda i: (i, 0))
    nxt = pl.BlockSpec((8, CW), lambda i: (jnp.minimum((i + 1) * (TQ // 8), t // 8 - 1), 0))
    return pl.pallas_call(
        body, grid=(nt,),
        in_specs=[row(CW), nxt] + _conv_specs() + [_const((8, CW))],
        out_specs=[row(CW), row(CW), _const((8, CW))],
        out_shape=[jax.ShapeDtypeStruct((t, CW), BF16), jax.ShapeDtypeStruct((t, CW), BF16),
                   jax.ShapeDtypeStruct((8, CW), F32)],
        scratch_shapes=[pltpu.VMEM((TQ + 16, CW), F32), pltpu.VMEM((TQ + 8, CW), F32)],
        compiler_params=_cp(("arbitrary",)), name="bwd_conv")(dco, dco, proj, proj, wconv_t)


def bwd_attn(proj, o, do, lse, bias2):
    t = o.shape[0]
    nt = t // TQ

    def body(q_ref, kp_ref, kc_ref, vp_ref, vc_ref, o_ref, do_ref, lse_ref, b2_ref,
             dq_ref, dk_hbm, dv_hbm, db_hbm, kwin, vwin, dk_acc, dv_acc, db_acc):
        i = pl.program_id(0)
        first = i == 0

        @pl.when(first)
        def _():
            dk_acc[...] = jnp.zeros_like(dk_acc)
            dv_acc[...] = jnp.zeros_like(dv_acc)
            db_acc[...] = jnp.zeros_like(db_acc)

        kwin[0:TQ, :] = kp_ref[...]
        kwin[TQ:2 * TQ, :] = kc_ref[...]
        vwin[0:TQ, :] = vp_ref[...]
        vwin[TQ:2 * TQ, :] = vc_ref[...]
        scale = HD ** -0.5
        qmask = _head_masks(scale)
        vmask = _head_masks(1.0)

        def group(g, carry):
            r0 = pl.multiple_of(g * QG, QG)
            base = pl.multiple_of(i * TQ + r0, QG)
            pen = _key_penalty(first, r0)
            for hp in range(NH // 2):
                ls = slice(LANES * hp, LANES * (hp + 1))
                qb = q_ref[pl.ds(r0, QG), ls]
                kw = kwin[pl.ds(r0, KG), ls]
                vw = vwin[pl.ds(r0, KG), ls]
                dob = do_ref[pl.ds(r0, QG), ls]
                prod = dob.astype(F32) * o_ref[pl.ds(r0, QG), ls]
                lseb = lse_ref[pl.ds(r0, QG), ls]
                dq = jnp.zeros((QG, LANES), F32)
                dk = jnp.zeros((KG, LANES), F32)
                dv = jnp.zeros((KG, LANES), F32)
                for a in range(2):
                    qa = qb * qmask[a]
                    doa = dob * vmask[a]
                    s = lax.dot_general(qa, kw, NT, preferred_element_type=F32)
                    s = s + b2_ref[2 * hp + a] + pen
                    p = jnp.exp(s - lseb[:, HD * a:HD * a + 1])
                    dp = lax.dot_general(doa, vw, NT, preferred_element_type=F32)
                    dsum = jnp.sum(prod * vmask[a].astype(F32), axis=-1, keepdims=True)
                    ds = p * (dp - dsum)
                    db_acc[2 * hp + a] += ds
                    dsb = ds.astype(BF16)
                    dq = dq + jnp.dot(dsb, kw, preferred_element_type=F32) * qmask[a].astype(F32)
                    dk = dk + lax.dot_general(dsb, qa, TN, preferred_element_type=F32)
                    dv = dv + lax.dot_general(p.astype(BF16), doa, TN, preferred_element_type=F32)
                dq_ref[pl.ds(r0, QG), ls] = dq.astype(BF16)
                dk_acc[pl.ds(base, KG), ls] += dk
                dv_acc[pl.ds(base, KG), ls] += dv
            return carry

        lax.fori_loop(0, TQ // QG, group, 0)

        @pl.when(i == nt - 1)
        def _():
            pltpu.sync_copy(dk_acc, dk_hbm)
            pltpu.sync_copy(dv_acc, dv_hbm)
            pltpu.sync_copy(db_acc, db_hbm)

    row = lambda w: pl.BlockSpec((TQ, w), lambda i: (i, 0))
    return pl.pallas_call(
        body, grid=(nt,),
        in_specs=_attn_window_specs() + [row(CW), row(CW), row(CW), _const((NH, QG, KG))],
        out_specs=[row(CW), _any(), _any(), _any()],
        out_shape=[jax.ShapeDtypeStruct((t, CW), BF16), jax.ShapeDtypeStruct((t + TQ, CW), F32),
                   jax.ShapeDtypeStruct((t + TQ, CW), F32), jax.ShapeDtypeStruct((NH, QG, KG), F32)],
        scratch_shapes=[pltpu.VMEM((2 * TQ, CW), BF16), pltpu.VMEM((2 * TQ, CW), BF16),
                        pltpu.VMEM((t + TQ, CW), F32), pltpu.VMEM((t + TQ, CW), F32),
                        pltpu.VMEM((NH, QG, KG), F32)],
        compiler_params=_cp(("arbitrary",)), name="bwd_attn",
    )(proj, proj, proj, proj, proj, o, do, lse, bias2)


def bwd_inproj(dxm, x, dhc, dbg, dcg, dq, dk, dv, g, w_all, layer):
    t = x.shape[0]
    wc = PROJ // NCHIP

    def body(dxm_ref, x_ref, dhc_ref, dbg_ref, dcg_ref, dq_ref, dk_ref, dv_ref, g_ref, w_hbm,
             dx_ref, dp_ref, h_ref, dg_ref, w_v):
        @pl.when(pl.program_id(0) == 0)
        def _():
            pltpu.sync_copy(w_hbm.at[layer], w_v)
            dg_ref[...] = jnp.zeros_like(dg_ref)

        dp_ref[:, 0:CW] = dhc_ref[...]
        dp_ref[:, CW:2 * CW] = dbg_ref[...]
        dp_ref[:, 2 * CW:3 * CW] = dcg_ref[...]
        dp_ref[:, 3 * CW:4 * CW] = dq_ref[...]
        dp_ref[:, 4 * CW:5 * CW] = dk_ref[...].astype(BF16)
        dp_ref[:, 5 * CW:6 * CW] = dv_ref[...].astype(BF16)
        dh = jnp.zeros((TQ, D), F32)
        for b in range(NCHIP):
            dh = dh + lax.dot_general(dp_ref[:, wc * b:wc * (b + 1)], w_v[b], NT, preferred_element_type=F32)
        xv = x_ref[...]
        gv = g_ref[...]
        h_ref[...] = _rms(xv, gv).astype(BF16)
        dxv, dgv = _rms_bwd(dh, xv, gv)
        dg_ref[...] += dgv
        dx_ref[...] = dxm_ref[...] + dxv

    row = lambda w: pl.BlockSpec((TQ, w), lambda i: (i, 0))
    pad = pl.BlockSpec((TQ, CW), lambda i: (i + 1, 0))
    return pl.pallas_call(
        body, grid=(t // TQ,),
        in_specs=[row(D), row(D), row(CW), row(CW), row(CW), row(CW), pad, pad, _const((1, D)), _any()],
        out_specs=[row(D), row(PROJ), row(D), _const((1, D))],
        out_shape=[jax.ShapeDtypeStruct((t, D), F32), jax.ShapeDtypeStruct((t, PROJ), BF16),
                   jax.ShapeDtypeStruct((t, D), BF16), jax.ShapeDtypeStruct((1, D), F32)],
        scratch_shapes=[pltpu.VMEM((NCHIP, D, wc), BF16)],
        compiler_params=_cp(("arbitrary",)), name="bwd_inproj",
    )(dxm, x, dhc, dbg, dcg, dq, dk, dv, g, w_all)


def wgrad(a, b, acc, layer, kb, nb, by_columns, name):
    t, k = a.shape
    n = b.shape[1]
    tk = 512

    def body(a_ref, b_ref, acc_hbm, o_ref):
        del acc_hbm
        o_ref[...] = jnp.zeros_like(o_ref)
        for c in range(t // tk):
            o_ref[...] += lax.dot_general(a_ref[tk * c:tk * (c + 1), :], b_ref[tk * c:tk * (c + 1), :], TN,
                                          preferred_element_type=F32)

    if by_columns:
        assert nb == n // NCHIP
        out_spec = pl.BlockSpec((None, None, kb, nb), lambda ki, ni: (layer, ni, ki, 0))
    else:
        assert nb == n
        out_spec = pl.BlockSpec((None, kb, nb), lambda ki, ni: (layer, ki, 0))
    return pl.pallas_call(
        body, grid=(k // kb, n // nb),
        in_specs=[pl.BlockSpec((t, kb), lambda ki, ni: (0, ki)), pl.BlockSpec((t, nb), lambda ki, ni: (0, ni)), _any()],
        out_specs=out_spec, out_shape=jax.ShapeDtypeStruct(acc.shape, F32), input_output_aliases={2: 0},
        compiler_params=_cp(("arbitrary", "arbitrary")), name=name)(a, b, acc)


LEFT = BAND - CHUNK
TOE = 1024
N_FLAT = LEFT - REL_CLIP + 1
N_VAR = BAND - N_FLAT


def _diag_vector(table):
    last = table[:, 2 * REL_CLIP:]
    var = table[:, 2 * REL_CLIP - N_VAR:2 * REL_CLIP][:, ::-1]
    return jnp.concatenate([jnp.broadcast_to(last, (NH, N_FLAT)), var, jnp.broadcast_to(last, (NH, TOE - BAND))], axis=1)


def _diag_vector_bwd(dvec):
    dlast = jnp.sum(dvec[:, :N_FLAT], axis=1, keepdims=True) + jnp.sum(dvec[:, BAND:], axis=1, keepdims=True)
    dvar = dvec[:, N_FLAT:BAND][:, ::-1]
    return jnp.concatenate([jnp.zeros((NH, 2 * REL_CLIP - N_VAR), F32), dvar, dlast], axis=1)


def _band_valid():
    r = lax.broadcasted_iota(jnp.int32, (QG, KG), 0)
    p = lax.broadcasted_iota(jnp.int32, (QG, KG), 1)
    start = jnp.where(r >= CHUNK, CHUNK, 0)
    return (p >= start) & (p < start + BAND)


def bias_expand(vec):
    def body(v_ref, o_ref):
        valid = _band_valid()
        for h in range(NH):
            rows = jnp.broadcast_to(v_ref[h:h + 1, :], (QG, TOE))
            toe = pltpu.roll(rows, 0, 1, stride=1, stride_axis=0)
            o_ref[h] = jnp.where(valid, toe[:, :KG], NEG_INF)

    return pl.pallas_call(body, out_shape=jax.ShapeDtypeStruct((NH, QG, KG), F32), name="bias_expand")(vec)


def bias_reduce(db2):
    def body(d_ref, o_ref):
        for h in range(NH):
            d = jnp.concatenate([jnp.zeros((QG, TOE - KG), F32), d_ref[h]], axis=1)
            back = pltpu.roll(d, 0, 1, stride=1, stride_axis=0)
            o_ref[h:h + 1, :] = jnp.sum(back, axis=0, keepdims=True)

    rev = pl.pallas_call(body, out_shape=jax.ShapeDtypeStruct((NH, TOE), F32), name="bias_reduce")(db2[:, :, ::-1])
    return rev[:, ::-1]


def _place():
    x, y, c = lax.axis_index("x"), lax.axis_index("y"), lax.axis_index("c")
    chips = [(1 - x, y), (x, 1 - y), (1 - x, 1 - y)]
    return x, y, c, chips


def _half(ref_rows, c):
    return pl.ds(c * (ref_rows // 2), ref_rows // 2)


def ag_weights(ws):
    n = len(ws)

    def body(*refs):
        ins, outs = refs[:n], refs[n:2 * n]
        send1, recv1, send2, recv2, lsem = refs[2 * n:]
        x, y, c, chips = _place()
        b = 2 * x + y
        local, sends = [], []
        for k in range(n):
            rows = ins[k].shape[1]
            mine = pltpu.make_async_copy(ins[k], outs[k].at[:, b], lsem.at[k])
            mine.start()
            local.append(mine)
            for j, (cx, cy) in enumerate(chips):
                cp = pltpu.make_async_remote_copy(
                    src_ref=ins[k].at[:, _half(rows, c), :], dst_ref=outs[k].at[:, b, _half(rows, c), :],
                    send_sem=send1.at[3 * k + j], recv_sem=recv1.at[3 * k + j],
                    device_id=(cx, cy, c), device_id_type=MESH)
                cp.start()
                sends.append(cp)
        for k in range(n):
            rows = ins[k].shape[1]
            for j, (cx, cy) in enumerate(chips):
                blk = outs[k].at[:, 2 * cx + cy, _half(rows, c), :]
                pltpu.make_async_remote_copy(
                    src_ref=blk, dst_ref=blk, send_sem=send1.at[3 * k + j], recv_sem=recv1.at[3 * k + j],
                    device_id=(cx, cy, c), device_id_type=MESH).wait_recv()
                fw = pltpu.make_async_remote_copy(
                    src_ref=blk, dst_ref=blk, send_sem=send2.at[3 * k + j], recv_sem=recv2.at[3 * k + j],
                    device_id=(x, y, 1 - c), device_id_type=MESH)
                fw.start()
                sends.append(fw)
        for k in range(n):
            rows = ins[k].shape[1]
            for j, (cx, cy) in enumerate(chips):
                blk = outs[k].at[:, 2 * cx + cy, _half(rows, 1 - c), :]
                pltpu.make_async_remote_copy(
                    src_ref=blk, dst_ref=blk, send_sem=send2.at[3 * k + j], recv_sem=recv2.at[3 * k + j],
                    device_id=(x, y, 1 - c), device_id_type=MESH).wait_recv()
        for cp in sends:
            cp.wait_send()
        for cp in local:
            cp.wait()

    out_shape = [jax.ShapeDtypeStruct((w.shape[0], NCHIP) + w.shape[1:], w.dtype) for w in ws]
    return pl.pallas_call(
        body, in_specs=[_any()] * n, out_specs=[_any()] * n, out_shape=out_shape,
        scratch_shapes=[pltpu.SemaphoreType.DMA((3 * n,)), pltpu.SemaphoreType.DMA((3 * n,)),
                        pltpu.SemaphoreType.DMA((3 * n,)), pltpu.SemaphoreType.DMA((3 * n,)),
                        pltpu.SemaphoreType.DMA((n,))],
        name="ag_weights")(*ws)


def pair_exchange(gs):
    n = len(gs)

    def body(*refs):
        ins, outs = refs[:n], refs[n:2 * n]
        send, recv = refs[2 * n:]
        x, y, c, _ = _place()
        cps = []
        for k in range(n):
            rows = ins[k].shape[2]
            cp = pltpu.make_async_remote_copy(
                src_ref=ins[k].at[:, :, _half(rows, 1 - c), :], dst_ref=outs[k],
                send_sem=send.at[k], recv_sem=recv.at[k], device_id=(x, y, 1 - c), device_id_type=MESH)
            cp.start()
            cps.append(cp)
        for cp in cps:
            cp.wait()

    out_shape = [jax.ShapeDtypeStruct(g.shape[:2] + (g.shape[2] // 2, g.shape[3]), g.dtype) for g in gs]
    return pl.pallas_call(
        body, in_specs=[_any()] * n, out_specs=[_any()] * n, out_shape=out_shape,
        scratch_shapes=[pltpu.SemaphoreType.DMA((n,)), pltpu.SemaphoreType.DMA((n,))], name="pair_exchange")(*gs)


def add_pair(g, r1, core):
    nl, ns, rows, cols = g.shape
    hr = rows // 2

    def body(c_ref, g_ref, r_ref, o_ref):
        del c_ref
        o_ref[...] = (g_ref[...] + r_ref[...]).astype(BF16)

    blk = (None, None, hr, cols)
    grid_spec = pltpu.PrefetchScalarGridSpec(
        num_scalar_prefetch=1, grid=(nl, ns),
        in_specs=[pl.BlockSpec(blk, lambda l, s, c: (l, s, c[0], 0)), pl.BlockSpec(blk, lambda l, s, c: (l, s, 0, 0))],
        out_specs=pl.BlockSpec(blk, lambda l, s, c: (l, s, 0, 0)))
    return pl.pallas_call(body, grid_spec=grid_spec, out_shape=jax.ShapeDtypeStruct(r1.shape, BF16),
                          compiler_params=_cp(("arbitrary", "arbitrary")), name="add_pair")(core, g, r1)


def ici_scatter(ss):
    n = len(ss)

    def body(*refs):
        ins, outs = refs[:n], refs[n:2 * n]
        send, recv = refs[2 * n:]
        _, _, c, chips = _place()
        cps = []
        for k in range(n):
            for j, (cx, cy) in enumerate(chips):
                cp = pltpu.make_async_remote_copy(
                    src_ref=ins[k].at[:, 2 * cx + cy], dst_ref=outs[k].at[j],
                    send_sem=send.at[3 * k + j], recv_sem=recv.at[3 * k + j],
                    device_id=(cx, cy, c), device_id_type=MESH)
                cp.start()
                cps.append(cp)
        for cp in cps:
            cp.wait()

    out_shape = [jax.ShapeDtypeStruct((3, s.shape[0]) + s.shape[2:], s.dtype) for s in ss]
    return pl.pallas_call(
        body, in_specs=[_any()] * n, out_specs=[_any()] * n, out_shape=out_shape,
        scratch_shapes=[pltpu.SemaphoreType.DMA((3 * n,)), pltpu.SemaphoreType.DMA((3 * n,))],
        name="ici_scatter")(*ss)


def add_chips(g, r1, r2, place):
    nl, _, rows, cols = g.shape
    hr = rows // 2

    def body(p_ref, g_ref, r1_ref, r2_ref, o_ref):
        del p_ref
        own = g_ref[...] + r1_ref[...]
        o_ref[...] = ((own + r2_ref[0].astype(F32)) + r2_ref[1].astype(F32)) + r2_ref[2].astype(F32)

    grid_spec = pltpu.PrefetchScalarGridSpec(
        num_scalar_prefetch=1, grid=(nl,),
        in_specs=[pl.BlockSpec((None, None, hr, cols), lambda l, p: (l, p[1], p[0], 0)),
                  pl.BlockSpec((None, None, hr, cols), lambda l, p: (l, p[1], 0, 0)),
                  pl.BlockSpec((3, None, hr, cols), lambda l, p: (0, l, 0, 0))],
        out_specs=pl.BlockSpec((None, hr, cols), lambda l, p: (l, 0, 0)))
    return pl.pallas_call(body, grid_spec=grid_spec, out_shape=jax.ShapeDtypeStruct((nl, hr, cols), F32),
                          compiler_params=_cp(("arbitrary",)), name="add_chips")(place, g, r1, r2)


def pair_share(rs):
    n = len(rs)

    def body(*refs):
        ins, outs = refs[:n], refs[n:2 * n]
        send, recv, lsem = refs[2 * n:]
        x, y, c, _ = _place()
        cps = []
        for k in range(n):
            rows = outs[k].shape[1]
            mine = pltpu.make_async_copy(ins[k], outs[k].at[:, _half(rows, c), :], lsem.at[k])
            mine.start()
            cps.append(mine)
            cp = pltpu.make_async_remote_copy(
                src_ref=ins[k], dst_ref=outs[k].at[:, _half(rows, c), :],
                send_sem=send.at[k], recv_sem=recv.at[k], device_id=(x, y, 1 - c), device_id_type=MESH)
            cp.start()
            cps.append(cp)
        for cp in cps:
            cp.wait()

    out_shape = [jax.ShapeDtypeStruct((r.shape[0], 2 * r.shape[1], r.shape[2]), r.dtype) for r in rs]
    return pl.pallas_call(
        body, in_specs=[_any()] * n, out_specs=[_any()] * n, out_shape=out_shape,
        scratch_shapes=[pltpu.SemaphoreType.DMA((n,)), pltpu.SemaphoreType.DMA((n,)), pltpu.SemaphoreType.DMA((n,))],
        name="pair_share")(*rs)


def small_collect(v, reduce, name):
    rows = v.shape[0]
    flips = [(fx, fy, fc) for fx in (0, 1) for fy in (0, 1) for fc in (0, 1)][1:]

    def body(v_ref, o_ref, buf, send, recv):
        x, y, c, _ = _place()
        buf[4 * x + 2 * y + c] = v_ref[...]
        peers = [(jnp.where(fx, 1 - x, x), jnp.where(fy, 1 - y, y), jnp.where(fc, 1 - c, c)) for fx, fy, fc in flips]
        cps = []
        for k, peer in enumerate(peers):
            cp = pltpu.make_async_remote_copy(
                src_ref=v_ref, dst_ref=buf.at[4 * x + 2 * y + c], send_sem=send.at[k], recv_sem=recv.at[k],
                device_id=peer, device_id_type=MESH)
            cp.start()
            cps.append(cp)
        for k, (px, py, pc) in enumerate(peers):
            pltpu.make_async_remote_copy(
                src_ref=v_ref, dst_ref=buf.at[4 * px + 2 * py + pc], send_sem=send.at[k], recv_sem=recv.at[k],
                device_id=(px, py, pc), device_id_type=MESH).wait_recv()
        for cp in cps:
            cp.wait_send()
        if reduce:
            acc = buf[0]
            for s in range(1, 8):
                acc = acc + buf[s]
            o_ref[...] = acc
        else:
            o_ref[...] = buf[...]

    vm = pl.BlockSpec(memory_space=pltpu.VMEM)
    out_shape = jax.ShapeDtypeStruct((rows, SMALL_COLS) if reduce else (8, rows, SMALL_COLS), F32)
    return pl.pallas_call(
        body, in_specs=[vm], out_specs=vm, out_shape=out_shape,
        scratch_shapes=[pltpu.VMEM((8, rows, SMALL_COLS), F32), pltpu.SemaphoreType.DMA((7,)),
                        pltpu.SemaphoreType.DMA((7,))],
        name=name)(v)


def adamw(w, g, m, v, rb, name):
    nl, rows, cols = w.shape

    def body(w_ref, g_ref, m_ref, v_ref, d_ref, nm_ref, nv_ref):
        gv = g_ref[...]
        nm = ADAM_B1 * m_ref[...] + (1.0 - ADAM_B1) * gv
        nv = ADAM_B2 * v_ref[...] + (1.0 - ADAM_B2) * (gv * gv)
        m_hat = nm / (1.0 - ADAM_B1 ** ADAM_STEP)
        v_hat = nv / (1.0 - ADAM_B2 ** ADAM_STEP)
        d_ref[...] = -ADAM_LR * (m_hat / (jnp.sqrt(v_hat) + ADAM_EPS) + ADAM_WD * w_ref[...])
        nm_ref[...] = nm
        nv_ref[...] = nv

    blk = pl.BlockSpec((None, rb, cols), lambda l, r: (l, r, 0))
    shp = jax.ShapeDtypeStruct(w.shape, F32)
    return pl.pallas_call(body, grid=(nl, rows // rb), in_specs=[blk] * 4, out_specs=[blk] * 3, out_shape=[shp] * 3,
                          compiler_params=_cp(("arbitrary", "arbitrary")), name=name)(w, g, m, v)


def _pack(parts, rows):
    flat = jnp.concatenate([p.reshape(-1).astype(F32) for p in parts])
    return jnp.pad(flat, (0, rows * SMALL_COLS - flat.shape[0])).reshape(rows, SMALL_COLS)


def _unpack(vec, shapes):
    flat = vec.reshape(-1)
    out, off = [], 0
    for s in shapes:
        size = 1
        for d in s:
            size *= d
        out.append(flat[off:off + size].reshape(s))
        off += size
    return out


def kernel(x, w_in, w_conv, rel_bias, g_conv_out, g_attn_out, w_out, g_pre_mix, g_post_mix, g_pre_ffn, g_post_ffn, w_ffn_in, w_ffn_out, loss_target, m_w_in, m_w_conv, m_rel_bias, m_g_conv_out, m_g_attn_out, m_w_out, m_g_pre_mix, m_g_post_mix, m_g_pre_ffn, m_g_post_ffn, m_w_ffn_in, m_w_ffn_out, v_w_in, v_w_conv, v_rel_bias, v_g_conv_out, v_g_attn_out, v_w_out, v_g_pre_mix, v_g_post_mix, v_g_pre_ffn, v_g_post_ffn, v_w_ffn_in, v_w_ffn_out):
    xi, yi, ci = lax.axis_index("x"), lax.axis_index("y"), lax.axis_index("c")
    chip = 2 * xi + yi
    nl = w_in.shape[0]
    x0 = x[0]
    target = loss_target[0]
    cwl = CW // NCHIP

    gw_in, gw_out, gw_fi, gw_fo = ag_weights(
        [w_in.astype(BF16), w_out.astype(BF16), w_ffn_in.astype(BF16), w_ffn_out.astype(BF16)])
    wout_all = gw_out.reshape(nl, D, D)
    wfo_all = gw_fo.reshape(nl, 2, DFF // 2, D)
    wc_all = small_collect(_pack([w_conv], 8), False, "gather_w_conv")
    wc_full = wc_all[0::2].reshape(NCHIP, -1)[:, :nl * cwl * 3].reshape(NCHIP, nl, cwl, 3)
    wc_full = jnp.transpose(wc_full, (1, 0, 2, 3)).reshape(nl, CW, 3)
    wconv_t = jnp.pad(jnp.transpose(wc_full, (0, 2, 1)), ((0, 0), (0, 5), (0, 0)))
    gm = jnp.kron(jnp.eye(CW // HD, dtype=F32), jnp.full((HD, HD), 1.0 / HD, F32)).astype(BF16)
    row = lambda a, l: a[l][None, :]

    saved = []
    h = x0
    for l in range(nl):
        bias2 = bias_expand(_diag_vector(rel_bias[l]))
        proj = fwd_inproj(h, row(g_pre_mix, l), gw_in, l)
        xmid, o, lse, y, z = fwd_mix(h, proj, bias2, wconv_t[l], row(g_conv_out, l), row(g_attn_out, l),
                                     row(g_post_mix, l), gm, wout_all, l)
        gu, f, xout = fwd_ffn(xmid, row(g_pre_ffn, l), row(g_post_ffn, l), gw_fi, wfo_all, l)
        saved.append((h, proj, bias2, xmid, o, lse, y, z, gu, f))
        h = xout
    dx, loss_blk = loss_head(h, target)

    acc_in = lax.empty((nl, NCHIP, D, PROJ // NCHIP), F32)
    acc_out = lax.empty((nl, D, D), F32)
    acc_fi = lax.empty((nl, NCHIP, D, 2 * DFF // NCHIP), F32)
    acc_fo = lax.empty((nl, DFF, D), F32)
    small = {k: [None] * nl for k in ("co", "ao", "pm", "qm", "pf", "qf", "rel", "wc")}
    for l in reversed(range(nl)):
        hin, proj, bias2, xmid, o, lse, y, z, gu, f = saved[l]
        dxm, dfb, act, dgu, h2, dg_qf, dg_pf = bwd_ffn(dx, f, xmid, gu, row(g_pre_ffn, l), row(g_post_ffn, l),
                                                        gw_fi, wfo_all, l)
        acc_fo = wgrad(act, dfb, acc_fo, l, 256, D, False, "wgrad_ffn_out")
        acc_fi = wgrad(h2, dgu, acc_fi, l, 512, 2 * DFF // NCHIP, True, "wgrad_ffn_in")
        dzb, do, dco, dbg, dg_qm, dg_co, dg_ao = bwd_mix(dxm, z, o, proj, wconv_t[l], row(g_conv_out, l),
                                                          row(g_attn_out, l), row(g_post_mix, l), gm, wout_all, l)
        acc_out = wgrad(y, dzb, acc_out, l, 512, D, False, "wgrad_out")
        dhc, dcg, dwc = bwd_conv(dco, proj, wconv_t[l])
        dq, dk, dv, db2 = bwd_attn(proj, o, do, lse, bias2)
        dx, dproj, hb, dg_pm = bwd_inproj(dxm, hin, dhc, dbg, dcg, dq, dk, dv, row(g_pre_mix, l), gw_in, l)
        acc_in = wgrad(hb, dproj, acc_in, l, 512, PROJ // NCHIP, True, "wgrad_in")
        small["co"][l], small["ao"][l], small["pm"][l], small["qm"][l] = dg_co, dg_ao, dg_pm, dg_qm
        small["pf"][l], small["qf"][l] = dg_pf, dg_qf
        small["rel"][l] = _diag_vector_bwd(bias_reduce(db2))
        small["wc"][l] = jnp.transpose(dwc[0:3], (1, 0))

    grads = [acc_in, acc_out.reshape(nl, NCHIP, D // NCHIP, D), acc_fi, acc_fo.reshape(nl, NCHIP, DFF // NCHIP, D)]
    core = ci.reshape(1).astype(jnp.int32)
    place = jnp.stack([ci, chip]).astype(jnp.int32)
    from_sibling = pair_exchange(grads)
    pair_sums = [add_pair(g, r, core) for g, r in zip(grads, from_sibling)]
    from_chips = ici_scatter(pair_sums)
    halves = [add_chips(g, r1, r2, place) for g, r1, r2 in zip(grads, from_sibling, from_chips)]
    gr_in, gr_out, gr_fi, gr_fo = pair_share(halves)

    order = ("co", "ao", "pm", "qm", "pf", "qf", "rel", "wc")
    parts = [jnp.stack(small[k]) for k in order] + [loss_blk[0:1, 0:1]]
    shapes = [p.shape for p in parts]
    red = _unpack(small_collect(_pack(parts, 40), True, "reduce_small"), shapes)
    gr_co, gr_ao, gr_pm, gr_qm, gr_pf, gr_qf, gr_rel, gr_wc_full, loss = red
    gr_co, gr_ao, gr_pm, gr_qm, gr_pf, gr_qf = [a.reshape(nl, -1) for a in (gr_co, gr_ao, gr_pm, gr_qm, gr_pf, gr_qf)]
    gr_wc = lax.dynamic_slice_in_dim(gr_wc_full, chip * cwl, cwl, axis=1)
    loss = loss.reshape(())

    big = []
    for w, g, m, v, name in ((w_in, gr_in, m_w_in, v_w_in, "adamw_in"), (w_out, gr_out, m_w_out, v_w_out, "adamw_out"),
                             (w_ffn_in, gr_fi, m_w_ffn_in, v_w_ffn_in, "adamw_ffn_in"),
                             (w_ffn_out, gr_fo, m_w_ffn_out, v_w_ffn_out, "adamw_ffn_out")):
        big.append(adamw(w, g, m, v, w.shape[1] // 4, name))
    sw = [g_conv_out, g_attn_out, g_pre_mix, g_post_mix, g_pre_ffn, g_post_ffn, rel_bias, w_conv]
    sg = [gr_co, gr_ao, gr_pm, gr_qm, gr_pf, gr_qf, gr_rel, gr_wc]
    sm = [m_g_conv_out, m_g_attn_out, m_g_pre_mix, m_g_post_mix, m_g_pre_ffn, m_g_post_ffn, m_rel_bias, m_w_conv]
    sv = [v_g_conv_out, v_g_attn_out, v_g_pre_mix, v_g_post_mix, v_g_pre_ffn, v_g_post_ffn, v_rel_bias, v_w_conv]
    sshapes = [a.shape for a in sw]
    packed = [_pack(a, 32)[None] for a in (sw, sg, sm, sv)]
    s_out = [_unpack(a[0], sshapes) for a in adamw(*packed, 32, "adamw_small")]

    def leaves(big_i, small_i):
        b_in, b_out, b_fi, b_fo = big_i
        s_co, s_ao, s_pm, s_qm, s_pf, s_qf, s_rel, s_wc = small_i
        return [b_in, s_wc, s_rel, s_co, s_ao, b_out, s_pm, s_qm, s_pf, s_qf, b_fi, b_fo]

    out = [loss, dx[None]]
    out += leaves((gr_in, gr_out, gr_fi, gr_fo), sg)
    for i in range(3):
        out += leaves([b[i] for b in big], s_out[i])
    return tuple(out)
```

```python
import functools

import jax
import jax.numpy as jnp
from jax import lax
from jax.experimental import pallas as pl
from jax.experimental.pallas import tpu as pltpu

F32 = jnp.float32
BF16 = jnp.bfloat16

D = 1024
PROJ = 3072
CW = 512
HD = 64
NH = 8
CHUNK = 64
BAND = 576
REL_CLIP = 128
NREL = 2 * REL_CLIP + 1
DFF = 2816
DEPTH = 4
NCHIP = 4
EPS = 1e-6
NEG_INF = -1e30

ADAM_LR = 0.001
ADAM_B1 = 0.9
ADAM_B2 = 0.999
ADAM_EPS = 1e-08
ADAM_WD = 0.01
ADAM_STEP = 10

V7X_VMEM_BYTES = 64 * 1024 * 1024
VMEM_LIMIT = V7X_VMEM_BYTES - 8 * 1024 * 1024
LANES = 128
QG = 2 * CHUNK
KG = QG + BAND - CHUNK
TQ = 512
TM = 256
SMALL_COLS = 1024
MESH = pl.DeviceIdType.MESH
NT = (((1,), (1,)), ((), ()))
TN = (((0,), (0,)), ((), ()))


def _cp(sem=None, vmem=VMEM_LIMIT):
    return pltpu.CompilerParams(dimension_semantics=sem, vmem_limit_bytes=vmem)


def _any():
    return pl.BlockSpec(memory_space=pl.ANY)


def _const(shape):
    nd = len(shape)
    return pl.BlockSpec(shape, lambda *_: (0,) * nd)


def _rms(v, g):
    r = lax.rsqrt(jnp.mean(v * v, axis=-1, keepdims=True) + EPS)
    return v * r * g


def _rms_bwd(dy, v, g):
    r = lax.rsqrt(jnp.mean(v * v, axis=-1, keepdims=True) + EPS)
    vh = v * r
    dg = jnp.sum(dy * vh, axis=0, keepdims=True)
    dvh = dy * g
    dv = r * (dvh - vh * jnp.mean(dvh * vh, axis=-1, keepdims=True))
    return dv, dg


def _group_mean(v, gm):
    hi = v.astype(BF16)
    lo = (v - hi.astype(F32)).astype(BF16)
    return jnp.dot(hi, gm, preferred_element_type=F32) + jnp.dot(lo, gm, preferred_element_type=F32)


def _group_rms_bwd(dy, v, g, gm):
    r = lax.rsqrt(_group_mean(v * v, gm) + EPS)
    vh = v * r
    dg = jnp.sum(dy * vh, axis=0, keepdims=True)
    dvh = dy * g
    dv = r * (dvh - vh * _group_mean(dvh * vh, gm))
    return dv, dg


def _head_masks(scale):
    lane = lax.broadcasted_iota(jnp.int32, (1, LANES), 1)
    return [jnp.where((lane >= HD * a) & (lane < HD * (a + 1)), scale, 0.0).astype(BF16) for a in range(2)]


def _conv_taps(u_prev, u, scr):
    n = u.shape[0]
    scr[0:16, :] = u_prev
    scr[16:16 + n, :] = u
    return scr[15:15 + n, :], scr[14:14 + n, :]


def fwd_inproj(x, g, w_all, layer):
    t = x.shape[0]
    wc = PROJ // NCHIP

    def body(x_ref, g_ref, w_hbm, o_ref, w_v):
        @pl.when(pl.program_id(0) == 0)
        def _():
            pltpu.sync_copy(w_hbm.at[layer], w_v)

        h = _rms(x_ref[...], g_ref[...]).astype(BF16)
        for b in range(NCHIP):
            o_ref[:, wc * b:wc * (b + 1)] = jnp.dot(h, w_v[b], preferred_element_type=F32).astype(BF16)

    return pl.pallas_call(
        body, grid=(t // TQ,),
        in_specs=[pl.BlockSpec((TQ, D), lambda i: (i, 0)), _const((1, D)), _any()],
        out_specs=pl.BlockSpec((TQ, PROJ), lambda i: (i, 0)),
        out_shape=jax.ShapeDtypeStruct((t, PROJ), BF16),
        scratch_shapes=[pltpu.VMEM((NCHIP, D, wc), BF16)],
        compiler_params=_cp(("arbitrary",)), name="fwd_inproj")(x, g, w_all)


def _attn_window_specs():
    return [
        pl.BlockSpec((TQ, CW), lambda i: (i, 3)),
        pl.BlockSpec((TQ, CW), lambda i: (jnp.maximum(i - 1, 0), 4)),
        pl.BlockSpec((TQ, CW), lambda i: (i, 4)),
        pl.BlockSpec((TQ, CW), lambda i: (jnp.maximum(i - 1, 0), 5)),
        pl.BlockSpec((TQ, CW), lambda i: (i, 5)),
    ]


def _conv_specs():
    return [
        pl.BlockSpec((TQ, 3 * CW), lambda i: (i, 0)),
        pl.BlockSpec((16, 3 * CW), lambda i: (jnp.maximum(i * (TQ // 16) - 1, 0), 0)),
    ]


def _conv_fwd(pc_ref, pcp_ref, wc_ref, scr, first):
    pc = pc_ref[...].astype(F32)
    hc, bg, cg = pc[:, :CW], pc[:, CW:2 * CW], pc[:, 2 * CW:]
    u = cg * hc
    pp = pcp_ref[...].astype(F32)
    u_prev = jnp.where(first, 0.0, pp[:, 2 * CW:] * pp[:, :CW])
    u1, u2 = _conv_taps(u_prev, u, scr)
    cout = wc_ref[0:1, :] * u2 + wc_ref[1:2, :] * u1 + wc_ref[2:3, :] * u
    return hc, bg, cg, u, u1, u2, cout


def _key_penalty(first, r0):
    col = lax.broadcasted_iota(jnp.int32, (1, KG), 1)
    limit = jnp.where(first, TQ - r0, 0)
    return jnp.where(col < limit, NEG_INF, 0.0)


def fwd_mix(x, proj, bias2, wconv_t, g_co, g_ao, g_pm, gm, wout_all, layer):
    t = x.shape[0]

    def body(x_ref, pc_ref, pcp_ref, q_ref, kp_ref, kc_ref, vp_ref, vc_ref, b2_ref, wc_ref, gco_ref, gao_ref, gpm_ref,
             gm_ref, wout_hbm, xmid_ref, o_ref, lse_ref, y_ref, z_ref, wout_v, kwin, vwin, cscr):
        i = pl.program_id(0)
        first = i == 0

        @pl.when(first)
        def _():
            pltpu.sync_copy(wout_hbm.at[layer], wout_v)

        kwin[0:TQ, :] = kp_ref[...]
        kwin[TQ:2 * TQ, :] = kc_ref[...]
        vwin[0:TQ, :] = vp_ref[...]
        vwin[TQ:2 * TQ, :] = vc_ref[...]
        qmask = _head_masks(HD ** -0.5)
        vmask = _head_masks(1.0)

        def group(g, carry):
            r0 = pl.multiple_of(g * QG, QG)
            pen = _key_penalty(first, r0)
            for hp in range(NH // 2):
                ls = slice(LANES * hp, LANES * (hp + 1))
                qb = q_ref[pl.ds(r0, QG), ls]
                kw = kwin[pl.ds(r0, KG), ls]
                vw = vwin[pl.ds(r0, KG), ls]
                o_acc = jnp.zeros((QG, LANES), F32)
                lse = jnp.zeros((QG, LANES), F32)
                for a in range(2):
                    s = lax.dot_general(qb * qmask[a], kw, NT, preferred_element_type=F32)
                    s = s + b2_ref[2 * hp + a] + pen
                    m = jnp.max(s, axis=-1, keepdims=True)
                    p = jnp.exp(s - m)
                    l = jnp.sum(p, axis=-1, keepdims=True)
                    o = jnp.dot(p.astype(BF16), vw * vmask[a], preferred_element_type=F32)
                    o_acc = o_acc + o * (1.0 / l)
                    lse = lse + (m + jnp.log(l)) * vmask[a].astype(F32)
                o_ref[pl.ds(r0, QG), ls] = o_acc
                lse_ref[pl.ds(r0, QG), ls] = lse
            return carry

        lax.fori_loop(0, TQ // QG, group, 0)

        _, bg, _, _, _, _, cout = _conv_fwd(pc_ref, pcp_ref, wc_ref, cscr, first)
        yc = bg * cout
        gmv = gm_ref[...]
        ycn = yc * lax.rsqrt(_group_mean(yc * yc, gmv) + EPS) * gco_ref[...]
        oa = o_ref[...]
        oan = oa * lax.rsqrt(_group_mean(oa * oa, gmv) + EPS) * gao_ref[...]
        y_ref[:, 0:CW] = ycn.astype(BF16)
        y_ref[:, CW:2 * CW] = oan.astype(BF16)
        z = jnp.dot(y_ref[...], wout_v[...], preferred_element_type=F32)
        z_ref[...] = z
        xmid_ref[...] = x_ref[...] + _rms(z, gpm_ref[...])

    row = lambda w: pl.BlockSpec((TQ, w), lambda i: (i, 0))
    return pl.pallas_call(
        body, grid=(t // TQ,),
        in_specs=[row(D)] + _conv_specs() + _attn_window_specs() + [
            _const((NH, QG, KG)), _const((8, CW)), _const((1, CW)), _const((1, CW)), _const((1, D)),
            _const((CW, CW)), _any()],
        out_specs=[row(D), row(CW), row(CW), row(D), row(D)],
        out_shape=[jax.ShapeDtypeStruct((t, D), F32), jax.ShapeDtypeStruct((t, CW), F32),
                   jax.ShapeDtypeStruct((t, CW), F32), jax.ShapeDtypeStruct((t, D), BF16),
                   jax.ShapeDtypeStruct((t, D), F32)],
        scratch_shapes=[pltpu.VMEM((D, D), BF16), pltpu.VMEM((2 * TQ, CW), BF16), pltpu.VMEM((2 * TQ, CW), BF16),
                        pltpu.VMEM((TQ + 16, CW), F32)],
        compiler_params=_cp(("arbitrary",)), name="fwd_mix",
    )(x, proj, proj, proj, proj, proj, proj, proj, bias2, wconv_t, g_co, g_ao, g_pm, gm, wout_all)


def fwd_ffn(xmid, g_pre, g_post, wfi_all, wfo_all, layer):
    t = xmid.shape[0]
    hw = DFF // 2

    def body(x_ref, gpre_ref, gpost_ref, wfi_hbm, wfo_hbm, gu_ref, f_ref, xo_ref, wfi_v, wfo_v):
        @pl.when(pl.program_id(0) == 0)
        def _():
            pltpu.sync_copy(wfi_hbm.at[layer], wfi_v)
            pltpu.sync_copy(wfo_hbm.at[layer], wfo_v)

        xv = x_ref[...]
        h = _rms(xv, gpre_ref[...]).astype(BF16)
        f = jnp.zeros((TM, D), F32)
        for j in range(2):
            gate = jnp.dot(h, wfi_v[j], preferred_element_type=F32)
            up = jnp.dot(h, wfi_v[2 + j], preferred_element_type=F32)
            gu_ref[:, hw * j:hw * (j + 1)] = gate.astype(BF16)
            gu_ref[:, DFF + hw * j:DFF + hw * (j + 1)] = up.astype(BF16)
            act = gate * (1.0 / (1.0 + jnp.exp(-gate))) * up
            f = f + jnp.dot(act.astype(BF16), wfo_v[j], preferred_element_type=F32)
        f_ref[...] = f
        xo_ref[...] = xv + _rms(f, gpost_ref[...])

    row = lambda w: pl.BlockSpec((TM, w), lambda i: (i, 0))
    return pl.pallas_call(
        body, grid=(t // TM,),
        in_specs=[row(D), _const((1, D)), _const((1, D)), _any(), _any()],
        out_specs=[row(2 * DFF), row(D), row(D)],
        out_shape=[jax.ShapeDtypeStruct((t, 2 * DFF), BF16), jax.ShapeDtypeStruct((t, D), F32),
                   jax.ShapeDtypeStruct((t, D), F32)],
        scratch_shapes=[pltpu.VMEM((NCHIP, D, hw), BF16), pltpu.VMEM((2, hw, D), BF16)],
        compiler_params=_cp(("arbitrary",)), name="fwd_ffn")(xmid, g_pre, g_post, wfi_all, wfo_all)


def loss_head(y, target):
    t = y.shape[0]

    def body(y_ref, t_ref, dy_ref, l_ref):
        @pl.when(pl.program_id(0) == 0)
        def _():
            l_ref[...] = jnp.zeros_like(l_ref)

        e = y_ref[...] - t_ref[...]
        dy_ref[...] = e * (1.0 / D)
        rows = jnp.sum(e * e, axis=-1, keepdims=True) * (1.0 / D)
        l_ref[...] += 0.5 * jnp.sum(rows, axis=0, keepdims=True)

    row = pl.BlockSpec((TQ, D), lambda i: (i, 0))
    return pl.pallas_call(
        body, grid=(t // TQ,), in_specs=[row, row], out_specs=[row, _const((8, LANES))],
        out_shape=[jax.ShapeDtypeStruct((t, D), F32), jax.ShapeDtypeStruct((8, LANES), F32)],
        compiler_params=_cp(("arbitrary",)), name="loss_head")(y, target)


def bwd_ffn(dx, f, xmid, gu, g_pre, g_post, wfi_all, wfo_all, layer):
    t = dx.shape[0]
    hw = DFF // 2

    def body(dx_ref, f_ref, x_ref, gu_ref, gpre_ref, gpost_ref, wfi_hbm, wfo_hbm,
             dxm_ref, df_ref, act_ref, dgu_ref, h_ref, dgpost_ref, dgpre_ref, wfi_v, wfo_v):
        @pl.when(pl.program_id(0) == 0)
        def _():
            pltpu.sync_copy(wfi_hbm.at[layer], wfi_v)
            pltpu.sync_copy(wfo_hbm.at[layer], wfo_v)
            dgpost_ref[...] = jnp.zeros_like(dgpost_ref)
            dgpre_ref[...] = jnp.zeros_like(dgpre_ref)

        dxo = dx_ref[...]
        df, dgp = _rms_bwd(dxo, f_ref[...], gpost_ref[...])
        dgpost_ref[...] += dgp
        dfb = df.astype(BF16)
        df_ref[...] = dfb
        dh = jnp.zeros((TM, D), F32)
        for j in range(2):
            dact = lax.dot_general(dfb, wfo_v[j], NT, preferred_element_type=F32)
            gate = gu_ref[:, hw * j:hw * (j + 1)].astype(F32)
            up = gu_ref[:, DFF + hw * j:DFF + hw * (j + 1)].astype(F32)
            sig = 1.0 / (1.0 + jnp.exp(-gate))
            silu = gate * sig
            act_ref[:, hw * j:hw * (j + 1)] = (silu * up).astype(BF16)
            dup = (dact * silu).astype(BF16)
            dgate = (dact * up * (sig * (1.0 + gate * (1.0 - sig)))).astype(BF16)
            dgu_ref[:, hw * j:hw * (j + 1)] = dgate
            dgu_ref[:, DFF + hw * j:DFF + hw * (j + 1)] = dup
            dh = dh + lax.dot_general(dgate, wfi_v[j], NT, preferred_element_type=F32)
            dh = dh + lax.dot_general(dup, wfi_v[2 + j], NT, preferred_element_type=F32)
        xv = x_ref[...]
        gpre = gpre_ref[...]
        h_ref[...] = _rms(xv, gpre).astype(BF16)
        dxv, dgq = _rms_bwd(dh, xv, gpre)
        dgpre_ref[...] += dgq
        dxm_ref[...] = dxo + dxv

    row = lambda w: pl.BlockSpec((TM, w), lambda i: (i, 0))
    return pl.pallas_call(
        body, grid=(t // TM,),
        in_specs=[row(D), row(D), row(D), row(2 * DFF), _const((1, D)), _const((1, D)), _any(), _any()],
        out_specs=[row(D), row(D), row(DFF), row(2 * DFF), row(D), _const((1, D)), _const((1, D))],
        out_shape=[jax.ShapeDtypeStruct((t, D), F32), jax.ShapeDtypeStruct((t, D), BF16),
                   jax.ShapeDtypeStruct((t, DFF), BF16), jax.ShapeDtypeStruct((t, 2 * DFF), BF16),
                   jax.ShapeDtypeStruct((t, D), BF16), jax.ShapeDtypeStruct((1, D), F32),
                   jax.ShapeDtypeStruct((1, D), F32)],
        scratch_shapes=[pltpu.VMEM((NCHIP, D, hw), BF16), pltpu.VMEM((2, hw, D), BF16)],
        compiler_params=_cp(("arbitrary",)), name="bwd_ffn")(dx, f, xmid, gu, g_pre, g_post, wfi_all, wfo_all)


def bwd_mix(dxm, z, o, proj, wconv_t, g_co, g_ao, g_pm, gm, wout_all, layer):
    t = dxm.shape[0]

    def body(dx_ref, z_ref, o_ref, pc_ref, pcp_ref, wc_ref, gco_ref, gao_ref, gpm_ref, gm_ref, wout_hbm,
             dz_ref, do_ref, dco_ref, dbg_ref, dgpm_ref, dgco_ref, dgao_ref, wout_v, cscr):
        first = pl.program_id(0) == 0

        @pl.when(first)
        def _():
            pltpu.sync_copy(wout_hbm.at[layer], wout_v)
            dgpm_ref[...] = jnp.zeros_like(dgpm_ref)
            dgco_ref[...] = jnp.zeros_like(dgco_ref)
            dgao_ref[...] = jnp.zeros_like(dgao_ref)

        dz, dgp = _rms_bwd(dx_ref[...], z_ref[...], gpm_ref[...])
        dgpm_ref[...] += dgp
        dzb = dz.astype(BF16)
        dz_ref[...] = dzb
        dy = lax.dot_general(dzb, wout_v[...], NT, preferred_element_type=F32)
        gmv = gm_ref[...]
        _, bg, _, _, _, _, cout = _conv_fwd(pc_ref, pcp_ref, wc_ref, cscr, first)
        dyc, dgc = _group_rms_bwd(dy[:, :CW], bg * cout, gco_ref[...], gmv)
        dgco_ref[...] += dgc
        dbg_ref[...] = (dyc * cout).astype(BF16)
        dco_ref[...] = dyc * bg
        do, dga = _group_rms_bwd(dy[:, CW:], o_ref[...], gao_ref[...], gmv)
        dgao_ref[...] += dga
        do_ref[...] = do.astype(BF16)

    row = lambda w: pl.BlockSpec((TQ, w), lambda i: (i, 0))
    return pl.pallas_call(
        body, grid=(t // TQ,),
        in_specs=[row(D), row(D), row(CW)] + _conv_specs() + [
            _const((8, CW)), _const((1, CW)), _const((1, CW)), _const((1, D)), _const((CW, CW)), _any()],
        out_specs=[row(D), row(CW), row(CW), row(CW), _const((1, D)), _const((1, CW)), _const((1, CW))],
        out_shape=[jax.ShapeDtypeStruct((t, D), BF16), jax.ShapeDtypeStruct((t, CW), BF16),
                   jax.ShapeDtypeStruct((t, CW), F32), jax.ShapeDtypeStruct((t, CW), BF16),
                   jax.ShapeDtypeStruct((1, D), F32), jax.ShapeDtypeStruct((1, CW), F32),
                   jax.ShapeDtypeStruct((1, CW), F32)],
        scratch_shapes=[pltpu.VMEM((D, D), BF16), pltpu.VMEM((TQ + 16, CW), F32)],
        compiler_params=_cp(("arbitrary",)), name="bwd_mix",
    )(dxm, z, o, proj, proj, wconv_t, g_co, g_ao, g_pm, gm, wout_all)


def bwd_conv(dco, proj, wconv_t):
    t = dco.shape[0]
    nt = t // TQ

    def body(d_ref, dn_ref, pc_ref, pcp_ref, wc_ref, dhc_ref, dcg_ref, dw_ref, cscr, dscr):
        i = pl.program_id(0)
        first = i == 0

        @pl.when(first)
        def _():
            dw_ref[...] = jnp.zeros_like(dw_ref)

        hc, _, cg, u, u1, u2, _ = _conv_fwd(pc_ref, pcp_ref, wc_ref, cscr, first)
        d0 = d_ref[...]
        dscr[0:TQ, :] = d0
        dscr[TQ:TQ + 8, :] = jnp.where(i == nt - 1, 0.0, dn_ref[...])
        d1 = dscr[1:TQ + 1, :]
        d2 = dscr[2:TQ + 2, :]
        du = wc_ref[2:3, :] * d0 + wc_ref[1:2, :] * d1 + wc_ref[0:1, :] * d2
        dhc_ref[...] = (du * cg).astype(BF16)
        dcg_ref[...] = (du * hc).astype(BF16)
        dw_ref[0:1, :] += jnp.sum(d0 * u2, axis=0, keepdims=True)
        dw_ref[1:2, :] += jnp.sum(d0 * u1, axis=0, keepdims=True)
        dw_ref[2:3, :] += jnp.sum(d0 * u, axis=0, keepdims=True)

    row = lambda w: pl.BlockSpec((TQ, w), lambda i: (i, 0))
    nxt = pl.BlockSpec((8, CW), lambda i: (jnp.minimum((i + 1) * (TQ // 8), t // 8 - 1), 0))
    return pl.pallas_call(
        body, grid=(nt,),
        in_specs=[row(CW), nxt] + _conv_specs() + [_const((8, CW))],
        out_specs=[row(CW), row(CW), _const((8, CW))],
        out_shape=[jax.ShapeDtypeStruct((t, CW), BF16), jax.ShapeDtypeStruct((t, CW), BF16),
                   jax.ShapeDtypeStruct((8, CW), F32)],
        scratch_shapes=[pltpu.VMEM((TQ + 16, CW), F32), pltpu.VMEM((TQ + 8, CW), F32)],
        compiler_params=_cp(("arbitrary",)), name="bwd_conv")(dco, dco, proj, proj, wconv_t)


def bwd_attn(proj, o, do, lse, bias2):
    t = o.shape[0]
    nt = t // TQ

    def body(q_ref, kp_ref, kc_ref, vp_ref, vc_ref, o_ref, do_ref, lse_ref, b2_ref,
             dq_ref, dk_hbm, dv_hbm, db_hbm, kwin, vwin, dk_acc, dv_acc, db_acc):
        i = pl.program_id(0)
        first = i == 0

        @pl.when(first)
        def _():
            dk_acc[...] = jnp.zeros_like(dk_acc)
            dv_acc[...] = jnp.zeros_like(dv_acc)
            db_acc[...] = jnp.zeros_like(db_acc)

        kwin[0:TQ, :] = kp_ref[...]
        kwin[TQ:2 * TQ, :] = kc_ref[...]
        vwin[0:TQ, :] = vp_ref[...]
        vwin[TQ:2 * TQ, :] = vc_ref[...]
        scale = HD ** -0.5
        qmask = _head_masks(scale)
        vmask = _head_masks(1.0)

        def group(g, carry):
            r0 = pl.multiple_of(g * QG, QG)
            base = pl.multiple_of(i * TQ + r0, QG)
            pen = _key_penalty(first, r0)
            for hp in range(NH // 2):
                ls = slice(LANES * hp, LANES * (hp + 1))
                qb = q_ref[pl.ds(r0, QG), ls]
                kw = kwin[pl.ds(r0, KG), ls]
                vw = vwin[pl.ds(r0, KG), ls]
                dob = do_ref[pl.ds(r0, QG), ls]
                prod = dob.astype(F32) * o_ref[pl.ds(r0, QG), ls]
                lseb = lse_ref[pl.ds(r0, QG), ls]
                dq = jnp.zeros((QG, LANES), F32)
                dk = jnp.zeros((KG, LANES), F32)
                dv = jnp.zeros((KG, LANES), F32)
                for a in range(2):
                    qa = qb * qmask[a]
                    doa = dob * vmask[a]
                    s = lax.dot_general(qa, kw, NT, preferred_element_type=F32)
                    s = s + b2_ref[2 * hp + a] + pen
                    p = jnp.exp(s - lseb[:, HD * a:HD * a + 1])
                    dp = lax.dot_general(doa, vw, NT, preferred_element_type=F32)
                    dsum = jnp.sum(prod * vmask[a].astype(F32), axis=-1, keepdims=True)
                    ds = p * (dp - dsum)
                    db_acc[2 * hp + a] += ds
                    dsb = ds.astype(BF16)
                    dq = dq + jnp.dot(dsb, kw, preferred_element_type=F32) * qmask[a].astype(F32)
                    dk = dk + lax.dot_general(dsb, qa, TN, preferred_element_type=F32)
                    dv = dv + lax.dot_general(p.astype(BF16), doa, TN, preferred_element_type=F32)
                dq_ref[pl.ds(r0, QG), ls] = dq.astype(BF16)
                dk_acc[pl.ds(base, KG), ls] += dk
                dv_acc[pl.ds(base, KG), ls] += dv
            return carry

        lax.fori_loop(0, TQ // QG, group, 0)

        @pl.when(i == nt - 1)
        def _():
            pltpu.sync_copy(dk_acc, dk_hbm)
            pltpu.sync_copy(dv_acc, dv_hbm)
            pltpu.sync_copy(db_acc, db_hbm)

    row = lambda w: pl.BlockSpec((TQ, w), lambda i: (i, 0))
    return pl.pallas_call(
        body, grid=(nt,),
        in_specs=_attn_window_specs() + [row(CW), row(CW), row(CW), _const((NH, QG, KG))],
        out_specs=[row(CW), _any(), _any(), _any()],
        out_shape=[jax.ShapeDtypeStruct((t, CW), BF16), jax.ShapeDtypeStruct((t + TQ, CW), F32),
                   jax.ShapeDtypeStruct((t + TQ, CW), F32), jax.ShapeDtypeStruct((NH, QG, KG), F32)],
        scratch_shapes=[pltpu.VMEM((2 * TQ, CW), BF16), pltpu.VMEM((2 * TQ, CW), BF16),
                        pltpu.VMEM((t + TQ, CW), F32), pltpu.VMEM((t + TQ, CW), F32),
                        pltpu.VMEM((NH, QG, KG), F32)],
        compiler_params=_cp(("arbitrary",)), name="bwd_attn",
    )(proj, proj, proj, proj, proj, o, do, lse, bias2)


def bwd_inproj(dxm, x, dhc, dbg, dcg, dq, dk, dv, g, w_all, layer):
    t = x.shape[0]
    wc = PROJ // NCHIP

    def body(dxm_ref, x_ref, dhc_ref, dbg_ref, dcg_ref, dq_ref, dk_ref, dv_ref, g_ref, w_hbm,
             dx_ref, dp_ref, h_ref, dg_ref, w_v):
        @pl.when(pl.program_id(0) == 0)
        def _():
            pltpu.sync_copy(w_hbm.at[layer], w_v)
            dg_ref[...] = jnp.zeros_like(dg_ref)

        dp_ref[:, 0:CW] = dhc_ref[...]
        dp_ref[:, CW:2 * CW] = dbg_ref[...]
        dp_ref[:, 2 * CW:3 * CW] = dcg_ref[...]
        dp_ref[:, 3 * CW:4 * CW] = dq_ref[...]
        dp_ref[:, 4 * CW:5 * CW] = dk_ref[...].astype(BF16)
        dp_ref[:, 5 * CW:6 * CW] = dv_ref[...].astype(BF16)
        dh = jnp.zeros((TQ, D), F32)
        for b in range(NCHIP):
            dh = dh + lax.dot_general(dp_ref[:, wc * b:wc * (b + 1)], w_v[b], NT, preferred_element_type=F32)
        xv = x_ref[...]
        gv = g_ref[...]
        h_ref[...] = _rms(xv, gv).astype(BF16)
        dxv, dgv = _rms_bwd(dh, xv, gv)
        dg_ref[...] += dgv
        dx_ref[...] = dxm_ref[...] + dxv

    row = lambda w: pl.BlockSpec((TQ, w), lambda i: (i, 0))
    pad = pl.BlockSpec((TQ, CW), lambda i: (i + 1, 0))
    return pl.pallas_call(
        body, grid=(t // TQ,),
        in_specs=[row(D), row(D), row(CW), row(CW), row(CW), row(CW), pad, pad, _const((1, D)), _any()],
        out_specs=[row(D), row(PROJ), row(D), _const((1, D))],
        out_shape=[jax.ShapeDtypeStruct((t, D), F32), jax.ShapeDtypeStruct((t, PROJ), BF16),
                   jax.ShapeDtypeStruct((t, D), BF16), jax.ShapeDtypeStruct((1, D), F32)],
        scratch_shapes=[pltpu.VMEM((NCHIP, D, wc), BF16)],
        compiler_params=_cp(("arbitrary",)), name="bwd_inproj",
    )(dxm, x, dhc, dbg, dcg, dq, dk, dv, g, w_all)


def wgrad(a, b, acc, layer, kb, nb, by_columns, name):
    t, k = a.shape
    n = b.shape[1]
    tk = 512

    def body(a_ref, b_ref, acc_hbm, o_ref):
        del acc_hbm
        o_ref[...] = jnp.zeros_like(o_ref)
        for c in range(t // tk):
            o_ref[...] += lax.dot_general(a_ref[tk * c:tk * (c + 1), :], b_ref[tk * c:tk * (c + 1), :], TN,
                                          preferred_element_type=F32)

    if by_columns:
        assert nb == n // NCHIP
        out_spec = pl.BlockSpec((None, None, kb, nb), lambda ki, ni: (layer, ni, ki, 0))
    else:
        assert nb == n
        out_spec = pl.BlockSpec((None, kb, nb), lambda ki, ni: (layer, ki, 0))
    return pl.pallas_call(
        body, grid=(k // kb, n // nb),
        in_specs=[pl.BlockSpec((t, kb), lambda ki, ni: (0, ki)), pl.BlockSpec((t, nb), lambda ki, ni: (0, ni)), _any()],
        out_specs=out_spec, out_shape=jax.ShapeDtypeStruct(acc.shape, F32), input_output_aliases={2: 0},
        compiler_params=_cp(("arbitrary", "arbitrary")), name=name)(a, b, acc)


LEFT = BAND - CHUNK
TOE = 1024
N_FLAT = LEFT - REL_CLIP + 1
N_VAR = BAND - N_FLAT


def _diag_vector(table):
    last = table[:, 2 * REL_CLIP:]
    var = table[:, 2 * REL_CLIP - N_VAR:2 * REL_CLIP][:, ::-1]
    return jnp.concatenate([jnp.broadcast_to(last, (NH, N_FLAT)), var, jnp.broadcast_to(last, (NH, TOE - BAND))], axis=1)


def _diag_vector_bwd(dvec):
    dlast = jnp.sum(dvec[:, :N_FLAT], axis=1, keepdims=True) + jnp.sum(dvec[:, BAND:], axis=1, keepdims=True)
    dvar = dvec[:, N_FLAT:BAND][:, ::-1]
    return jnp.concatenate([jnp.zeros((NH, 2 * REL_CLIP - N_VAR), F32), dvar, dlast], axis=1)


def _band_valid():
    r = lax.broadcasted_iota(jnp.int32, (QG, KG), 0)
    p = lax.broadcasted_iota(jnp.int32, (QG, KG), 1)
    start = jnp.where(r >= CHUNK, CHUNK, 0)
    return (p >= start) & (p < start + BAND)


def bias_expand(vec):
    def body(v_ref, o_ref):
        valid = _band_valid()
        for h in range(NH):
            rows = jnp.broadcast_to(v_ref[h:h + 1, :], (QG, TOE))
            toe = pltpu.roll(rows, 0, 1, stride=1, stride_axis=0)
            o_ref[h] = jnp.where(valid, toe[:, :KG], NEG_INF)

    return pl.pallas_call(body, out_shape=jax.ShapeDtypeStruct((NH, QG, KG), F32), name="bias_expand")(vec)


def bias_reduce(db2):
    def body(d_ref, o_ref):
        for h in range(NH):
            d = jnp.concatenate([jnp.zeros((QG, TOE - KG), F32), d_ref[h]], axis=1)
            back = pltpu.roll(d, 0, 1, stride=1, stride_axis=0)
            o_ref[h:h + 1, :] = jnp.sum(back, axis=0, keepdims=True)

    rev = pl.pallas_call(body, out_shape=jax.ShapeDtypeStruct((NH, TOE), F32), name="bias_reduce")(db2[:, :, ::-1])
    return rev[:, ::-1]


def _place():
    x, y, c = lax.axis_index("x"), lax.axis_index("y"), lax.axis_index("c")
    chips = [(1 - x, y), (x, 1 - y), (1 - x, 1 - y)]
    return x, y, c, chips


def _half(ref_rows, c):
    return pl.ds(c * (ref_rows // 2), ref_rows // 2)


def cast_to_slot(w, chip):
    nl, rows, cols = w.shape
    rb = rows // 4

    def body(b_ref, w_ref, o_ref):
        del b_ref
        o_ref[...] = w_ref[...].astype(BF16)

    grid_spec = pltpu.PrefetchScalarGridSpec(
        num_scalar_prefetch=1, grid=(nl, rows // rb),
        in_specs=[pl.BlockSpec((None, rb, cols), lambda l, r, b: (l, r, 0))],
        out_specs=pl.BlockSpec((None, None, rb, cols), lambda l, r, b: (l, b[0], r, 0)))
    return pl.pallas_call(body, grid_spec=grid_spec, out_shape=jax.ShapeDtypeStruct((nl, NCHIP, rows, cols), BF16),
                          compiler_params=_cp(("arbitrary", "arbitrary")), name="cast_to_slot")(chip, w)


def ag_weights(gs):
    n = len(gs)

    def body(*refs):
        outs = refs[n:2 * n]
        send1, recv1, send2, recv2 = refs[2 * n:]
        x, y, c, chips = _place()
        b = 2 * x + y
        sends = []
        for k in range(n):
            rows = outs[k].shape[2]
            mine = outs[k].at[:, b, _half(rows, c), :]
            for j, (cx, cy) in enumerate(chips):
                cp = pltpu.make_async_remote_copy(
                    src_ref=mine, dst_ref=mine, send_sem=send1.at[3 * k + j], recv_sem=recv1.at[3 * k + j],
                    device_id=(cx, cy, c), device_id_type=MESH)
                cp.start()
                sends.append(cp)
        for k in range(n):
            rows = outs[k].shape[2]
            for j, (cx, cy) in enumerate(chips):
                blk = outs[k].at[:, 2 * cx + cy, _half(rows, c), :]
                pltpu.make_async_remote_copy(
                    src_ref=blk, dst_ref=blk, send_sem=send1.at[3 * k + j], recv_sem=recv1.at[3 * k + j],
                    device_id=(cx, cy, c), device_id_type=MESH).wait_recv()
                fw = pltpu.make_async_remote_copy(
                    src_ref=blk, dst_ref=blk, send_sem=send2.at[3 * k + j], recv_sem=recv2.at[3 * k + j],
                    device_id=(x, y, 1 - c), device_id_type=MESH)
                fw.start()
                sends.append(fw)
        for k in range(n):
            rows = outs[k].shape[2]
            for j, (cx, cy) in enumerate(chips):
                blk = outs[k].at[:, 2 * cx + cy, _half(rows, 1 - c), :]
                pltpu.make_async_remote_copy(
                    src_ref=blk, dst_ref=blk, send_sem=send2.at[3 * k + j], recv_sem=recv2.at[3 * k + j],
                    device_id=(x, y, 1 - c), device_id_type=MESH).wait_recv()
        for cp in sends:
            cp.wait_send()

    return pl.pallas_call(
        body, in_specs=[_any()] * n, out_specs=[_any()] * n,
        out_shape=[jax.ShapeDtypeStruct(g.shape, g.dtype) for g in gs], input_output_aliases={k: k for k in range(n)},
        scratch_shapes=[pltpu.SemaphoreType.DMA((3 * n,)), pltpu.SemaphoreType.DMA((3 * n,)),
                        pltpu.SemaphoreType.DMA((3 * n,)), pltpu.SemaphoreType.DMA((3 * n,))],
        name="ag_weights")(*gs)


def pair_exchange(gs):
    n = len(gs)

    def body(*refs):
        ins, outs = refs[:n], refs[n:2 * n]
        send, recv = refs[2 * n:]
        x, y, c, _ = _place()
        cps = []
        for k in range(n):
            rows = ins[k].shape[2]
            cp = pltpu.make_async_remote_copy(
                src_ref=ins[k].at[:, :, _half(rows, 1 - c), :], dst_ref=outs[k],
                send_sem=send.at[k], recv_sem=recv.at[k], device_id=(x, y, 1 - c), device_id_type=MESH)
            cp.start()
            cps.append(cp)
        for cp in cps:
            cp.wait()

    out_shape = [jax.ShapeDtypeStruct(g.shape[:2] + (g.shape[2] // 2, g.shape[3]), g.dtype) for g in gs]
    return pl.pallas_call(
        body, in_specs=[_any()] * n, out_specs=[_any()] * n, out_shape=out_shape,
        scratch_shapes=[pltpu.SemaphoreType.DMA((n,)), pltpu.SemaphoreType.DMA((n,))], name="pair_exchange")(*gs)


def add_pair(g, r1, core):
    nl, ns, rows, cols = g.shape
    hr = rows // 2

    def body(c_ref, g_ref, r_ref, o_ref):
        del c_ref
        o_ref[...] = (g_ref[...] + r_ref[...]).astype(BF16)

    blk = (None, None, hr, cols)
    grid_spec = pltpu.PrefetchScalarGridSpec(
        num_scalar_prefetch=1, grid=(nl, ns),
        in_specs=[pl.BlockSpec(blk, lambda l, s, c: (l, s, c[0], 0)), pl.BlockSpec(blk, lambda l, s, c: (l, s, 0, 0))],
        out_specs=pl.BlockSpec(blk, lambda l, s, c: (l, s, 0, 0)))
    return pl.pallas_call(body, grid_spec=grid_spec, out_shape=jax.ShapeDtypeStruct(r1.shape, BF16),
                          compiler_params=_cp(("arbitrary", "arbitrary")), name="add_pair")(core, g, r1)


def ici_scatter(ss):
    n = len(ss)

    def body(*refs):
        ins, outs = refs[:n], refs[n:2 * n]
        send, recv = refs[2 * n:]
        _, _, c, chips = _place()
        cps = []
        for k in range(n):
            for j, (cx, cy) in enumerate(chips):
                cp = pltpu.make_async_remote_copy(
                    src_ref=ins[k].at[:, 2 * cx + cy], dst_ref=outs[k].at[j],
                    send_sem=send.at[3 * k + j], recv_sem=recv.at[3 * k + j],
                    device_id=(cx, cy, c), device_id_type=MESH)
                cp.start()
                cps.append(cp)
        for cp in cps:
            cp.wait()

    out_shape = [jax.ShapeDtypeStruct((3, s.shape[0]) + s.shape[2:], s.dtype) for s in ss]
    return pl.pallas_call(
        body, in_specs=[_any()] * n, out_specs=[_any()] * n, out_shape=out_shape,
        scratch_shapes=[pltpu.SemaphoreType.DMA((3 * n,)), pltpu.SemaphoreType.DMA((3 * n,))],
        name="ici_scatter")(*ss)


def add_chips(g, r1, r2, place):
    nl, _, rows, cols = g.shape
    hr = rows // 2

    def body(p_ref, g_ref, r1_ref, r2_ref, o_ref):
        del p_ref
        own = g_ref[...] + r1_ref[...]
        o_ref[...] = ((own + r2_ref[0].astype(F32)) + r2_ref[1].astype(F32)) + r2_ref[2].astype(F32)

    grid_spec = pltpu.PrefetchScalarGridSpec(
        num_scalar_prefetch=1, grid=(nl,),
        in_specs=[pl.BlockSpec((None, None, hr, cols), lambda l, p: (l, p[1], p[0], 0)),
                  pl.BlockSpec((None, None, hr, cols), lambda l, p: (l, p[1], 0, 0)),
                  pl.BlockSpec((3, None, hr, cols), lambda l, p: (0, l, 0, 0))],
        out_specs=pl.BlockSpec((None, hr, cols), lambda l, p: (l, p[0], 0)))
    return pl.pallas_call(body, grid_spec=grid_spec, out_shape=jax.ShapeDtypeStruct((nl, rows, cols), F32),
                          compiler_params=_cp(("arbitrary",)), name="add_chips")(place, g, r1, r2)


def pair_share(gs):
    n = len(gs)

    def body(*refs):
        outs = refs[n:2 * n]
        send, recv = refs[2 * n:]
        x, y, c, _ = _place()
        cps = []
        for k in range(n):
            mine = outs[k].at[:, _half(outs[k].shape[1], c), :]
            cp = pltpu.make_async_remote_copy(
                src_ref=mine, dst_ref=mine, send_sem=send.at[k], recv_sem=recv.at[k],
                device_id=(x, y, 1 - c), device_id_type=MESH)
            cp.start()
            cps.append(cp)
        for k, cp in enumerate(cps):
            cp.wait_send()
            theirs = outs[k].at[:, _half(outs[k].shape[1], 1 - c), :]
            pltpu.make_async_remote_copy(
                src_ref=theirs, dst_ref=theirs, send_sem=send.at[k], recv_sem=recv.at[k],
                device_id=(x, y, 1 - c), device_id_type=MESH).wait_recv()

    return pl.pallas_call(
        body, in_specs=[_any()] * n, out_specs=[_any()] * n,
        out_shape=[jax.ShapeDtypeStruct(g.shape, g.dtype) for g in gs], input_output_aliases={k: k for k in range(n)},
        scratch_shapes=[pltpu.SemaphoreType.DMA((n,)), pltpu.SemaphoreType.DMA((n,))],
        name="pair_share")(*gs)


def small_collect(v, reduce, name):
    rows = v.shape[0]
    flips = [(fx, fy, fc) for fx in (0, 1) for fy in (0, 1) for fc in (0, 1)][1:]

    def body(v_ref, o_ref, buf, send, recv):
        x, y, c, _ = _place()
        buf[4 * x + 2 * y + c] = v_ref[...]
        peers = [(jnp.where(fx, 1 - x, x), jnp.where(fy, 1 - y, y), jnp.where(fc, 1 - c, c)) for fx, fy, fc in flips]
        cps = []
        for k, peer in enumerate(peers):
            cp = pltpu.make_async_remote_copy(
                src_ref=v_ref, dst_ref=buf.at[4 * x + 2 * y + c], send_sem=send.at[k], recv_sem=recv.at[k],
                device_id=peer, device_id_type=MESH)
            cp.start()
            cps.append(cp)
        for k, (px, py, pc) in enumerate(peers):
            pltpu.make_async_remote_copy(
                src_ref=v_ref, dst_ref=buf.at[4 * px + 2 * py + pc], send_sem=send.at[k], recv_sem=recv.at[k],
                device_id=(px, py, pc), device_id_type=MESH).wait_recv()
        for cp in cps:
            cp.wait_send()
        if reduce:
            acc = buf[0]
            for s in range(1, 8):
                acc = acc + buf[s]
            o_ref[...] = acc
        else:
            o_ref[...] = buf[...]

    vm = pl.BlockSpec(memory_space=pltpu.VMEM)
    out_shape = jax.ShapeDtypeStruct((rows, SMALL_COLS) if reduce else (8, rows, SMALL_COLS), F32)
    return pl.pallas_call(
        body, in_specs=[vm], out_specs=vm, out_shape=out_shape,
        scratch_shapes=[pltpu.VMEM((8, rows, SMALL_COLS), F32), pltpu.SemaphoreType.DMA((7,)),
                        pltpu.SemaphoreType.DMA((7,))],
        name=name)(v)


def adamw(w, g, m, v, rb, name):
    nl, rows, cols = w.shape

    def body(w_ref, g_ref, m_ref, v_ref, d_ref, nm_ref, nv_ref):
        gv = g_ref[...]
        nm = ADAM_B1 * m_ref[...] + (1.0 - ADAM_B1) * gv
        nv = ADAM_B2 * v_ref[...] + (1.0 - ADAM_B2) * (gv * gv)
        m_hat = nm / (1.0 - ADAM_B1 ** ADAM_STEP)
        v_hat = nv / (1.0 - ADAM_B2 ** ADAM_STEP)
        d_ref[...] = -ADAM_LR * (m_hat / (jnp.sqrt(v_hat) + ADAM_EPS) + ADAM_WD * w_ref[...])
        nm_ref[...] = nm
        nv_ref[...] = nv

    blk = pl.BlockSpec((None, rb, cols), lambda l, r: (l, r, 0))
    shp = jax.ShapeDtypeStruct(w.shape, F32)
    return pl.pallas_call(body, grid=(nl, rows // rb), in_specs=[blk] * 4, out_specs=[blk] * 3, out_shape=[shp] * 3,
                          compiler_params=_cp(("arbitrary", "arbitrary")), name=name)(w, g, m, v)


def _pack(parts, rows):
    flat = jnp.concatenate([p.reshape(-1).astype(F32) for p in parts])
    return jnp.pad(flat, (0, rows * SMALL_COLS - flat.shape[0])).reshape(rows, SMALL_COLS)


def _unpack(vec, shapes):
    flat = vec.reshape(-1)
    out, off = [], 0
    for s in shapes:
        size = 1
        for d in s:
            size *= d
        out.append(flat[off:off + size].reshape(s))
        off += size
    return out


def kernel(x, w_in, w_conv, rel_bias, g_conv_out, g_attn_out, w_out, g_pre_mix, g_post_mix, g_pre_ffn, g_post_ffn, w_ffn_in, w_ffn_out, loss_target, m_w_in, m_w_conv, m_rel_bias, m_g_conv_out, m_g_attn_out, m_w_out, m_g_pre_mix, m_g_post_mix, m_g_pre_ffn, m_g_post_ffn, m_w_ffn_in, m_w_ffn_out, v_w_in, v_w_conv, v_rel_bias, v_g_conv_out, v_g_attn_out, v_w_out, v_g_pre_mix, v_g_post_mix, v_g_pre_ffn, v_g_post_ffn, v_w_ffn_in, v_w_ffn_out):
    xi, yi, ci = lax.axis_index("x"), lax.axis_index("y"), lax.axis_index("c")
    chip = 2 * xi + yi
    nl = w_in.shape[0]
    x0 = x[0]
    target = loss_target[0]
    cwl = CW // NCHIP

    chip1 = chip.reshape(1).astype(jnp.int32)
    gw_in, gw_out, gw_fi, gw_fo = ag_weights([cast_to_slot(w, chip1) for w in (w_in, w_out, w_ffn_in, w_ffn_out)])
    wout_all = gw_out.reshape(nl, D, D)
    wfo_all = gw_fo.reshape(nl, 2, DFF // 2, D)
    wc_all = small_collect(_pack([w_conv], 8), False, "gather_w_conv")
    wc_full = wc_all[0::2].reshape(NCHIP, -1)[:, :nl * cwl * 3].reshape(NCHIP, nl, cwl, 3)
    wc_full = jnp.transpose(wc_full, (1, 0, 2, 3)).reshape(nl, CW, 3)
    wconv_t = jnp.pad(jnp.transpose(wc_full, (0, 2, 1)), ((0, 0), (0, 5), (0, 0)))
    gm = jnp.kron(jnp.eye(CW // HD, dtype=F32), jnp.full((HD, HD), 1.0 / HD, F32)).astype(BF16)
    row = lambda a, l: a[l][None, :]

    saved = []
    h = x0
    for l in range(nl):
        bias2 = bias_expand(_diag_vector(rel_bias[l]))
        proj = fwd_inproj(h, row(g_pre_mix, l), gw_in, l)
        xmid, o, lse, y, z = fwd_mix(h, proj, bias2, wconv_t[l], row(g_conv_out, l), row(g_attn_out, l),
                                     row(g_post_mix, l), gm, wout_all, l)
        gu, f, xout = fwd_ffn(xmid, row(g_pre_ffn, l), row(g_post_ffn, l), gw_fi, wfo_all, l)
        saved.append((h, proj, bias2, xmid, o, lse, y, z, gu, f))
        h = xout
    dx, loss_blk = loss_head(h, target)

    acc_in = lax.empty((nl, NCHIP, D, PROJ // NCHIP), F32)
    acc_out = lax.empty((nl, D, D), F32)
    acc_fi = lax.empty((nl, NCHIP, D, 2 * DFF // NCHIP), F32)
    acc_fo = lax.empty((nl, DFF, D), F32)
    small = {k: [None] * nl for k in ("co", "ao", "pm", "qm", "pf", "qf", "rel", "wc")}
    for l in reversed(range(nl)):
        hin, proj, bias2, xmid, o, lse, y, z, gu, f = saved[l]
        dxm, dfb, act, dgu, h2, dg_qf, dg_pf = bwd_ffn(dx, f, xmid, gu, row(g_pre_ffn, l), row(g_post_ffn, l),
                                                        gw_fi, wfo_all, l)
        acc_fo = wgrad(act, dfb, acc_fo, l, 256, D, False, "wgrad_ffn_out")
        acc_fi = wgrad(h2, dgu, acc_fi, l, 512, 2 * DFF // NCHIP, True, "wgrad_ffn_in")
        dzb, do, dco, dbg, dg_qm, dg_co, dg_ao = bwd_mix(dxm, z, o, proj, wconv_t[l], row(g_conv_out, l),
                                                          row(g_attn_out, l), row(g_post_mix, l), gm, wout_all, l)
        acc_out = wgrad(y, dzb, acc_out, l, 512, D, False, "wgrad_out")
        dhc, dcg, dwc = bwd_conv(dco, proj, wconv_t[l])
        dq, dk, dv, db2 = bwd_attn(proj, o, do, lse, bias2)
        dx, dproj, hb, dg_pm = bwd_inproj(dxm, hin, dhc, dbg, dcg, dq, dk, dv, row(g_pre_mix, l), gw_in, l)
        acc_in = wgrad(hb, dproj, acc_in, l, 512, PROJ // NCHIP, True, "wgrad_in")
        small["co"][l], small["ao"][l], small["pm"][l], small["qm"][l] = dg_co, dg_ao, dg_pm, dg_qm
        small["pf"][l], small["qf"][l] = dg_pf, dg_qf
        small["rel"][l] = _diag_vector_bwd(bias_reduce(db2))
        small["wc"][l] = jnp.transpose(dwc[0:3], (1, 0))

    grads = [acc_in, acc_out.reshape(nl, NCHIP, D // NCHIP, D), acc_fi, acc_fo.reshape(nl, NCHIP, DFF // NCHIP, D)]
    core = ci.reshape(1).astype(jnp.int32)
    place = jnp.stack([ci, chip]).astype(jnp.int32)
    from_sibling = pair_exchange(grads)
    pair_sums = [add_pair(g, r, core) for g, r in zip(grads, from_sibling)]
    from_chips = ici_scatter(pair_sums)
    halves = [add_chips(g, r1, r2, place) for g, r1, r2 in zip(grads, from_sibling, from_chips)]
    gr_in, gr_out, gr_fi, gr_fo = pair_share(halves)

    order = ("co", "ao", "pm", "qm", "pf", "qf", "rel", "wc")
    parts = [jnp.stack(small[k]) for k in order] + [loss_blk[0:1, 0:1]]
    shapes = [p.shape for p in parts]
    red = _unpack(small_collect(_pack(parts, 40), True, "reduce_small"), shapes)
    gr_co, gr_ao, gr_pm, gr_qm, gr_pf, gr_qf, gr_rel, gr_wc_full, loss = red
    gr_co, gr_ao, gr_pm, gr_qm, gr_pf, gr_qf = [a.reshape(nl, -1) for a in (gr_co, gr_ao, gr_pm, gr_qm, gr_pf, gr_qf)]
    gr_wc = lax.dynamic_slice_in_dim(gr_wc_full, chip * cwl, cwl, axis=1)
    loss = loss.reshape(())

    big = []
    for w, g, m, v, name in ((w_in, gr_in, m_w_in, v_w_in, "adamw_in"), (w_out, gr_out, m_w_out, v_w_out, "adamw_out"),
                             (w_ffn_in, gr_fi, m_w_ffn_in, v_w_ffn_in, "adamw_ffn_in"),
                             (w_ffn_out, gr_fo, m_w_ffn_out, v_w_ffn_out, "adamw_ffn_out")):
        big.append(adamw(w, g, m, v, w.shape[1] // 4, name))
    sw = [g_conv_out, g_attn_out, g_pre_mix, g_post_mix, g_pre_ffn, g_post_ffn, rel_bias, w_conv]
    sg = [gr_co, gr_ao, gr_pm, gr_qm, gr_pf, gr_qf, gr_rel, gr_wc]
    sm = [m_g_conv_out, m_g_attn_out, m_g_pre_mix, m_g_post_mix, m_g_pre_ffn, m_g_post_ffn, m_rel_bias, m_w_conv]
    sv = [v_g_conv_out, v_g_attn_out, v_g_pre_mix, v_g_post_mix, v_g_pre_ffn, v_g_post_ffn, v_rel_bias, v_w_conv]
    sshapes = [a.shape for a in sw]
    packed = [_pack(a, 32)[None] for a in (sw, sg, sm, sv)]
    s_out = [_unpack(a[0], sshapes) for a in adamw(*packed, 32, "adamw_small")]

    def leaves(big_i, small_i):
        b_in, b_out, b_fi, b_fo = big_i
        s_co, s_ao, s_pm, s_qm, s_pf, s_qf, s_rel, s_wc = small_i
        return [b_in, s_wc, s_rel, s_co, s_ao, b_out, s_pm, s_qm, s_pf, s_qf, b_fi, b_fo]

    out = [loss, dx[None]]
    out += leaves((gr_in, gr_out, gr_fi, gr_fo), sg)
    for i in range(3):
        out += leaves([b[i] for b in big], s_out[i])
    return tuple(out)
```

```python
import functools

import jax
import jax.numpy as jnp
from jax import lax
from jax.experimental import pallas as pl
from jax.experimental.pallas import tpu as pltpu

F32 = jnp.float32
BF16 = jnp.bfloat16

D = 1024
PROJ = 3072
CW = 512
HD = 64
NH = 8
CHUNK = 64
BAND = 576
REL_CLIP = 128
NREL = 2 * REL_CLIP + 1
DFF = 2816
DEPTH = 4
NCHIP = 4
EPS = 1e-6
NEG_INF = -1e30

ADAM_LR = 0.001
ADAM_B1 = 0.9
ADAM_B2 = 0.999
ADAM_EPS = 1e-08
ADAM_WD = 0.01
ADAM_STEP = 10

V7X_VMEM_BYTES = 64 * 1024 * 1024
VMEM_LIMIT = V7X_VMEM_BYTES - 8 * 1024 * 1024
LANES = 128
QG = 2 * CHUNK
KG = QG + BAND - CHUNK
TQ = 512
TM = 256
SMALL_COLS = 1024
MESH = pl.DeviceIdType.MESH
NT = (((1,), (1,)), ((), ()))
TN = (((0,), (0,)), ((), ()))


def _cp(sem=None, vmem=VMEM_LIMIT):
    return pltpu.CompilerParams(dimension_semantics=sem, vmem_limit_bytes=vmem)


def _any():
    return pl.BlockSpec(memory_space=pl.ANY)


def _const(shape):
    nd = len(shape)
    return pl.BlockSpec(shape, lambda *_: (0,) * nd)


def _rms(v, g):
    r = lax.rsqrt(jnp.mean(v * v, axis=-1, keepdims=True) + EPS)
    return v * r * g


def _rms_bwd(dy, v, g):
    r = lax.rsqrt(jnp.mean(v * v, axis=-1, keepdims=True) + EPS)
    vh = v * r
    dg = jnp.sum(dy * vh, axis=0, keepdims=True)
    dvh = dy * g
    dv = r * (dvh - vh * jnp.mean(dvh * vh, axis=-1, keepdims=True))
    return dv, dg


def _group_mean(v, gm):
    hi = v.astype(BF16)
    lo = (v - hi.astype(F32)).astype(BF16)
    return jnp.dot(hi, gm, preferred_element_type=F32) + jnp.dot(lo, gm, preferred_element_type=F32)


def _group_rms_bwd(dy, v, g, gm):
    r = lax.rsqrt(_group_mean(v * v, gm) + EPS)
    vh = v * r
    dg = jnp.sum(dy * vh, axis=0, keepdims=True)
    dvh = dy * g
    dv = r * (dvh - vh * _group_mean(dvh * vh, gm))
    return dv, dg


def _head_masks(scale):
    lane = lax.broadcasted_iota(jnp.int32, (1, LANES), 1)
    return [jnp.where((lane >= HD * a) & (lane < HD * (a + 1)), scale, 0.0).astype(BF16) for a in range(2)]


def _conv_taps(u_prev, u, scr):
    n = u.shape[0]
    scr[0:16, :] = u_prev
    scr[16:16 + n, :] = u
    return scr[15:15 + n, :], scr[14:14 + n, :]


def fwd_inproj(x, g, w_all):
    t = x.shape[0]
    wc = PROJ // NCHIP

    def body(x_ref, g_ref, w_hbm, o_ref, w_v):
        @pl.when(pl.program_id(0) == 0)
        def _():
            pltpu.sync_copy(w_hbm, w_v)

        h = _rms(x_ref[...], g_ref[...]).astype(BF16)
        for b in range(NCHIP):
            o_ref[:, wc * b:wc * (b + 1)] = jnp.dot(h, w_v[b], preferred_element_type=F32).astype(BF16)

    return pl.pallas_call(
        body, grid=(t // TQ,),
        in_specs=[pl.BlockSpec((TQ, D), lambda i: (i, 0)), _const((1, D)), _any()],
        out_specs=pl.BlockSpec((TQ, PROJ), lambda i: (i, 0)),
        out_shape=jax.ShapeDtypeStruct((t, PROJ), BF16),
        scratch_shapes=[pltpu.VMEM((NCHIP, D, wc), BF16)],
        compiler_params=_cp(("arbitrary",)), name="fwd_inproj")(x, g, w_all)


def _attn_window_specs():
    return [
        pl.BlockSpec((TQ, CW), lambda i: (i, 3)),
        pl.BlockSpec((TQ, CW), lambda i: (jnp.maximum(i - 1, 0), 4)),
        pl.BlockSpec((TQ, CW), lambda i: (i, 4)),
        pl.BlockSpec((TQ, CW), lambda i: (jnp.maximum(i - 1, 0), 5)),
        pl.BlockSpec((TQ, CW), lambda i: (i, 5)),
    ]


def _conv_specs():
    return [
        pl.BlockSpec((TQ, 3 * CW), lambda i: (i, 0)),
        pl.BlockSpec((16, 3 * CW), lambda i: (jnp.maximum(i * (TQ // 16) - 1, 0), 0)),
    ]


def _conv_fwd(pc_ref, pcp_ref, wc_ref, scr, first):
    pc = pc_ref[...].astype(F32)
    hc, bg, cg = pc[:, :CW], pc[:, CW:2 * CW], pc[:, 2 * CW:]
    u = cg * hc
    pp = pcp_ref[...].astype(F32)
    u_prev = jnp.where(first, 0.0, pp[:, 2 * CW:] * pp[:, :CW])
    u1, u2 = _conv_taps(u_prev, u, scr)
    cout = wc_ref[0:1, :] * u2 + wc_ref[1:2, :] * u1 + wc_ref[2:3, :] * u
    return hc, bg, cg, u, u1, u2, cout


def _key_penalty(first, r0):
    col = lax.broadcasted_iota(jnp.int32, (1, KG), 1)
    limit = jnp.where(first, TQ - r0, 0)
    return jnp.where(col < limit, NEG_INF, 0.0)


def fwd_mix(x, proj, bias2, wconv_t, g_co, g_ao, g_pm, gm, wout_all):
    t = x.shape[0]

    def body(x_ref, pc_ref, pcp_ref, q_ref, kp_ref, kc_ref, vp_ref, vc_ref, b2_ref, wc_ref, gco_ref, gao_ref, gpm_ref,
             gm_ref, wout_hbm, xmid_ref, o_ref, lse_ref, y_ref, z_ref, wout_v, kwin, vwin, cscr):
        i = pl.program_id(0)
        first = i == 0

        @pl.when(first)
        def _():
            pltpu.sync_copy(wout_hbm, wout_v)

        kwin[0:TQ, :] = kp_ref[...]
        kwin[TQ:2 * TQ, :] = kc_ref[...]
        vwin[0:TQ, :] = vp_ref[...]
        vwin[TQ:2 * TQ, :] = vc_ref[...]
        qmask = _head_masks(HD ** -0.5)
        vmask = _head_masks(1.0)

        def group(g, carry):
            r0 = pl.multiple_of(g * QG, QG)
            pen = _key_penalty(first, r0)
            for hp in range(NH // 2):
                ls = slice(LANES * hp, LANES * (hp + 1))
                qb = q_ref[pl.ds(r0, QG), ls]
                kw = kwin[pl.ds(r0, KG), ls]
                vw = vwin[pl.ds(r0, KG), ls]
                o_acc = jnp.zeros((QG, LANES), F32)
                lse = jnp.zeros((QG, LANES), F32)
                for a in range(2):
                    s = lax.dot_general(qb * qmask[a], kw, NT, preferred_element_type=F32)
                    s = s + b2_ref[2 * hp + a] + pen
                    m = jnp.max(s, axis=-1, keepdims=True)
                    p = jnp.exp(s - m)
                    l = jnp.sum(p, axis=-1, keepdims=True)
                    o = jnp.dot(p.astype(BF16), vw * vmask[a], preferred_element_type=F32)
                    o_acc = o_acc + o * (1.0 / l)
                    lse = lse + (m + jnp.log(l)) * vmask[a].astype(F32)
                o_ref[pl.ds(r0, QG), ls] = o_acc
                lse_ref[pl.ds(r0, QG), ls] = lse
            return carry

        lax.fori_loop(0, TQ // QG, group, 0)

        _, bg, _, _, _, _, cout = _conv_fwd(pc_ref, pcp_ref, wc_ref, cscr, first)
        yc = bg * cout
        gmv = gm_ref[...]
        ycn = yc * lax.rsqrt(_group_mean(yc * yc, gmv) + EPS) * gco_ref[...]
        oa = o_ref[...]
        oan = oa * lax.rsqrt(_group_mean(oa * oa, gmv) + EPS) * gao_ref[...]
        y_ref[:, 0:CW] = ycn.astype(BF16)
        y_ref[:, CW:2 * CW] = oan.astype(BF16)
        z = jnp.dot(y_ref[...], wout_v[...], preferred_element_type=F32)
        z_ref[...] = z
        xmid_ref[...] = x_ref[...] + _rms(z, gpm_ref[...])

    row = lambda w: pl.BlockSpec((TQ, w), lambda i: (i, 0))
    return pl.pallas_call(
        body, grid=(t // TQ,),
        in_specs=[row(D)] + _conv_specs() + _attn_window_specs() + [
            _const((NH, QG, KG)), _const((8, CW)), _const((1, CW)), _const((1, CW)), _const((1, D)),
            _const((CW, CW)), _any()],
        out_specs=[row(D), row(CW), row(CW), row(D), row(D)],
        out_shape=[jax.ShapeDtypeStruct((t, D), F32), jax.ShapeDtypeStruct((t, CW), F32),
                   jax.ShapeDtypeStruct((t, CW), F32), jax.ShapeDtypeStruct((t, D), BF16),
                   jax.ShapeDtypeStruct((t, D), F32)],
        scratch_shapes=[pltpu.VMEM((D, D), BF16), pltpu.VMEM((2 * TQ, CW), BF16), pltpu.VMEM((2 * TQ, CW), BF16),
                        pltpu.VMEM((TQ + 16, CW), F32)],
        compiler_params=_cp(("arbitrary",)), name="fwd_mix",
    )(x, proj, proj, proj, proj, proj, proj, proj, bias2, wconv_t, g_co, g_ao, g_pm, gm, wout_all)


def fwd_ffn(xmid, g_pre, g_post, wfi_all, wfo_all):
    t = xmid.shape[0]
    hw = DFF // 2

    def body(x_ref, gpre_ref, gpost_ref, wfi_hbm, wfo_hbm, gu_ref, f_ref, xo_ref, wfi_v, wfo_v):
        @pl.when(pl.program_id(0) == 0)
        def _():
            pltpu.sync_copy(wfi_hbm, wfi_v)
            pltpu.sync_copy(wfo_hbm, wfo_v)

        xv = x_ref[...]
        h = _rms(xv, gpre_ref[...]).astype(BF16)
        f = jnp.zeros((TM, D), F32)
        for j in range(2):
            gate = jnp.dot(h, wfi_v[j], preferred_element_type=F32)
            up = jnp.dot(h, wfi_v[2 + j], preferred_element_type=F32)
            gu_ref[:, hw * j:hw * (j + 1)] = gate.astype(BF16)
            gu_ref[:, DFF + hw * j:DFF + hw * (j + 1)] = up.astype(BF16)
            act = gate * (1.0 / (1.0 + jnp.exp(-gate))) * up
            f = f + jnp.dot(act.astype(BF16), wfo_v[j], preferred_element_type=F32)
        f_ref[...] = f
        xo_ref[...] = xv + _rms(f, gpost_ref[...])

    row = lambda w: pl.BlockSpec((TM, w), lambda i: (i, 0))
    return pl.pallas_call(
        body, grid=(t // TM,),
        in_specs=[row(D), _const((1, D)), _const((1, D)), _any(), _any()],
        out_specs=[row(2 * DFF), row(D), row(D)],
        out_shape=[jax.ShapeDtypeStruct((t, 2 * DFF), BF16), jax.ShapeDtypeStruct((t, D), F32),
                   jax.ShapeDtypeStruct((t, D), F32)],
        scratch_shapes=[pltpu.VMEM((NCHIP, D, hw), BF16), pltpu.VMEM((2, hw, D), BF16)],
        compiler_params=_cp(("arbitrary",)), name="fwd_ffn")(xmid, g_pre, g_post, wfi_all, wfo_all)


def loss_head(y, target):
    t = y.shape[0]

    def body(y_ref, t_ref, dy_ref, l_ref):
        @pl.when(pl.program_id(0) == 0)
        def _():
            l_ref[...] = jnp.zeros_like(l_ref)

        e = y_ref[...] - t_ref[...]
        dy_ref[...] = e * (1.0 / D)
        rows = jnp.sum(e * e, axis=-1, keepdims=True) * (1.0 / D)
        l_ref[...] += 0.5 * jnp.sum(rows, axis=0, keepdims=True)

    row = pl.BlockSpec((TQ, D), lambda i: (i, 0))
    return pl.pallas_call(
        body, grid=(t // TQ,), in_specs=[row, row], out_specs=[row, _const((8, LANES))],
        out_shape=[jax.ShapeDtypeStruct((t, D), F32), jax.ShapeDtypeStruct((8, LANES), F32)],
        compiler_params=_cp(("arbitrary",)), name="loss_head")(y, target)


def bwd_ffn(dx, f, xmid, gu, g_pre, g_post, wfi_all, wfo_all):
    t = dx.shape[0]
    hw = DFF // 2

    def body(dx_ref, f_ref, x_ref, gu_ref, gpre_ref, gpost_ref, wfi_hbm, wfo_hbm,
             dxm_ref, df_ref, act_ref, dgu_ref, h_ref, dgpost_ref, dgpre_ref, wfi_v, wfo_v):
        @pl.when(pl.program_id(0) == 0)
        def _():
            pltpu.sync_copy(wfi_hbm, wfi_v)
            pltpu.sync_copy(wfo_hbm, wfo_v)
            dgpost_ref[...] = jnp.zeros_like(dgpost_ref)
            dgpre_ref[...] = jnp.zeros_like(dgpre_ref)

        dxo = dx_ref[...]
        df, dgp = _rms_bwd(dxo, f_ref[...], gpost_ref[...])
        dgpost_ref[...] += dgp
        dfb = df.astype(BF16)
        df_ref[...] = dfb
        dh = jnp.zeros((TM, D), F32)
        for j in range(2):
            dact = lax.dot_general(dfb, wfo_v[j], NT, preferred_element_type=F32)
            gate = gu_ref[:, hw * j:hw * (j + 1)].astype(F32)
            up = gu_ref[:, DFF + hw * j:DFF + hw * (j + 1)].astype(F32)
            sig = 1.0 / (1.0 + jnp.exp(-gate))
            silu = gate * sig
            act_ref[:, hw * j:hw * (j + 1)] = (silu * up).astype(BF16)
            dup = (dact * silu).astype(BF16)
            dgate = (dact * up * (sig * (1.0 + gate * (1.0 - sig)))).astype(BF16)
            dgu_ref[:, hw * j:hw * (j + 1)] = dgate
            dgu_ref[:, DFF + hw * j:DFF + hw * (j + 1)] = dup
            dh = dh + lax.dot_general(dgate, wfi_v[j], NT, preferred_element_type=F32)
            dh = dh + lax.dot_general(dup, wfi_v[2 + j], NT, preferred_element_type=F32)
        xv = x_ref[...]
        gpre = gpre_ref[...]
        h_ref[...] = _rms(xv, gpre).astype(BF16)
        dxv, dgq = _rms_bwd(dh, xv, gpre)
        dgpre_ref[...] += dgq
        dxm_ref[...] = dxo + dxv

    row = lambda w: pl.BlockSpec((TM, w), lambda i: (i, 0))
    return pl.pallas_call(
        body, grid=(t // TM,),
        in_specs=[row(D), row(D), row(D), row(2 * DFF), _const((1, D)), _const((1, D)), _any(), _any()],
        out_specs=[row(D), row(D), row(DFF), row(2 * DFF), row(D), _const((1, D)), _const((1, D))],
        out_shape=[jax.ShapeDtypeStruct((t, D), F32), jax.ShapeDtypeStruct((t, D), BF16),
                   jax.ShapeDtypeStruct((t, DFF), BF16), jax.ShapeDtypeStruct((t, 2 * DFF), BF16),
                   jax.ShapeDtypeStruct((t, D), BF16), jax.ShapeDtypeStruct((1, D), F32),
                   jax.ShapeDtypeStruct((1, D), F32)],
        scratch_shapes=[pltpu.VMEM((NCHIP, D, hw), BF16), pltpu.VMEM((2, hw, D), BF16)],
        compiler_params=_cp(("arbitrary",)), name="bwd_ffn")(dx, f, xmid, gu, g_pre, g_post, wfi_all, wfo_all)


def bwd_mix(dxm, z, o, proj, wconv_t, g_co, g_ao, g_pm, gm, wout_all):
    t = dxm.shape[0]

    def body(dx_ref, z_ref, o_ref, pc_ref, pcp_ref, wc_ref, gco_ref, gao_ref, gpm_ref, gm_ref, wout_hbm,
             dz_ref, do_ref, dco_ref, dbg_ref, dgpm_ref, dgco_ref, dgao_ref, wout_v, cscr):
        first = pl.program_id(0) == 0

        @pl.when(first)
        def _():
            pltpu.sync_copy(wout_hbm, wout_v)
            dgpm_ref[...] = jnp.zeros_like(dgpm_ref)
            dgco_ref[...] = jnp.zeros_like(dgco_ref)
            dgao_ref[...] = jnp.zeros_like(dgao_ref)

        dz, dgp = _rms_bwd(dx_ref[...], z_ref[...], gpm_ref[...])
        dgpm_ref[...] += dgp
        dzb = dz.astype(BF16)
        dz_ref[...] = dzb
        dy = lax.dot_general(dzb, wout_v[...], NT, preferred_element_type=F32)
        gmv = gm_ref[...]
        _, bg, _, _, _, _, cout = _conv_fwd(pc_ref, pcp_ref, wc_ref, cscr, first)
        dyc, dgc = _group_rms_bwd(dy[:, :CW], bg * cout, gco_ref[...], gmv)
        dgco_ref[...] += dgc
        dbg_ref[...] = (dyc * cout).astype(BF16)
        dco_ref[...] = dyc * bg
        do, dga = _group_rms_bwd(dy[:, CW:], o_ref[...], gao_ref[...], gmv)
        dgao_ref[...] += dga
        do_ref[...] = do.astype(BF16)

    row = lambda w: pl.BlockSpec((TQ, w), lambda i: (i, 0))
    return pl.pallas_call(
        body, grid=(t // TQ,),
        in_specs=[row(D), row(D), row(CW)] + _conv_specs() + [
            _const((8, CW)), _const((1, CW)), _const((1, CW)), _const((1, D)), _const((CW, CW)), _any()],
        out_specs=[row(D), row(CW), row(CW), row(CW), _const((1, D)), _const((1, CW)), _const((1, CW))],
        out_shape=[jax.ShapeDtypeStruct((t, D), BF16), jax.ShapeDtypeStruct((t, CW), BF16),
                   jax.ShapeDtypeStruct((t, CW), F32), jax.ShapeDtypeStruct((t, CW), BF16),
                   jax.ShapeDtypeStruct((1, D), F32), jax.ShapeDtypeStruct((1, CW), F32),
                   jax.ShapeDtypeStruct((1, CW), F32)],
        scratch_shapes=[pltpu.VMEM((D, D), BF16), pltpu.VMEM((TQ + 16, CW), F32)],
        compiler_params=_cp(("arbitrary",)), name="bwd_mix",
    )(dxm, z, o, proj, proj, wconv_t, g_co, g_ao, g_pm, gm, wout_all)


def bwd_conv(dco, proj, wconv_t):
    t = dco.shape[0]
    nt = t // TQ

    def body(d_ref, dn_ref, pc_ref, pcp_ref, wc_ref, dhc_ref, dcg_ref, dw_ref, cscr, dscr):
        i = pl.program_id(0)
        first = i == 0

        @pl.when(first)
        def _():
            dw_ref[...] = jnp.zeros_like(dw_ref)

        hc, _, cg, u, u1, u2, _ = _conv_fwd(pc_ref, pcp_ref, wc_ref, cscr, first)
        d0 = d_ref[...]
        dscr[0:TQ, :] = d0
        dscr[TQ:TQ + 8, :] = jnp.where(i == nt - 1, 0.0, dn_ref[...])
        d1 = dscr[1:TQ + 1, :]
        d2 = dscr[2:TQ + 2, :]
        du = wc_ref[2:3, :] * d0 + wc_ref[1:2, :] * d1 + wc_ref[0:1, :] * d2
        dhc_ref[...] = (du * cg).astype(BF16)
        dcg_ref[...] = (du * hc).astype(BF16)
        dw_ref[0:1, :] += jnp.sum(d0 * u2, axis=0, keepdims=True)
        dw_ref[1:2, :] += jnp.sum(d0 * u1, axis=0, keepdims=True)
        dw_ref[2:3, :] += jnp.sum(d0 * u, axis=0, keepdims=True)

    row = lambda w: pl.BlockSpec((TQ, w), lambda i: (i, 0))
    nxt = pl.BlockSpec((8, CW), lambda i: (jnp.minimum((i + 1) * (TQ // 8), t // 8 - 1), 0))
    return pl.pallas_call(
        body, grid=(nt,),
        in_specs=[row(CW), nxt] + _conv_specs() + [_const((8, CW))],
        out_specs=[row(CW), row(CW), _const((8, CW))],
        out_shape=[jax.ShapeDtypeStruct((t, CW), BF16), jax.ShapeDtypeStruct((t, CW), BF16),
                   jax.ShapeDtypeStruct((8, CW), F32)],
        scratch_shapes=[pltpu.VMEM((TQ + 16, CW), F32), pltpu.VMEM((TQ + 8, CW), F32)],
        compiler_params=_cp(("arbitrary",)), name="bwd_conv")(dco, dco, proj, proj, wconv_t)


def bwd_attn(proj, o, do, lse, bias2):
    t = o.shape[0]
    nt = t // TQ

    def body(q_ref, kp_ref, kc_ref, vp_ref, vc_ref, o_ref, do_ref, lse_ref, b2_ref,
             dq_ref, dk_hbm, dv_hbm, db_hbm, kwin, vwin, dk_acc, dv_acc, db_acc):
        i = pl.program_id(0)
        first = i == 0

        @pl.when(first)
        def _():
            dk_acc[...] = jnp.zeros_like(dk_acc)
            dv_acc[...] = jnp.zeros_like(dv_acc)
            db_acc[...] = jnp.zeros_like(db_acc)

        kwin[0:TQ, :] = kp_ref[...]
        kwin[TQ:2 * TQ, :] = kc_ref[...]
        vwin[0:TQ, :] = vp_ref[...]
        vwin[TQ:2 * TQ, :] = vc_ref[...]
        scale = HD ** -0.5
        qmask = _head_masks(scale)
        vmask = _head_masks(1.0)

        def group(g, carry):
            r0 = pl.multiple_of(g * QG, QG)
            base = pl.multiple_of(i * TQ + r0, QG)
            pen = _key_penalty(first, r0)
            for hp in range(NH // 2):
                ls = slice(LANES * hp, LANES * (hp + 1))
                qb = q_ref[pl.ds(r0, QG), ls]
                kw = kwin[pl.ds(r0, KG), ls]
                vw = vwin[pl.ds(r0, KG), ls]
                dob = do_ref[pl.ds(r0, QG), ls]
                prod = dob.astype(F32) * o_ref[pl.ds(r0, QG), ls]
                lseb = lse_ref[pl.ds(r0, QG), ls]
                dq = jnp.zeros((QG, LANES), F32)
                dk = jnp.zeros((KG, LANES), F32)
                dv = jnp.zeros((KG, LANES), F32)
                for a in range(2):
                    qa = qb * qmask[a]
                    doa = dob * vmask[a]
                    s = lax.dot_general(qa, kw, NT, preferred_element_type=F32)
                    s = s + b2_ref[2 * hp + a] + pen
                    p = jnp.exp(s - lseb[:, HD * a:HD * a + 1])
                    dp = lax.dot_general(doa, vw, NT, preferred_element_type=F32)
                    dsum = jnp.sum(prod * vmask[a].astype(F32), axis=-1, keepdims=True)
                    ds = p * (dp - dsum)
                    db_acc[2 * hp + a] += ds
                    dsb = ds.astype(BF16)
                    dq = dq + jnp.dot(dsb, kw, preferred_element_type=F32) * qmask[a].astype(F32)
                    dk = dk + lax.dot_general(dsb, qa, TN, preferred_element_type=F32)
                    dv = dv + lax.dot_general(p.astype(BF16), doa, TN, preferred_element_type=F32)
                dq_ref[pl.ds(r0, QG), ls] = dq.astype(BF16)
                dk_acc[pl.ds(base, KG), ls] += dk
                dv_acc[pl.ds(base, KG), ls] += dv
            return carry

        lax.fori_loop(0, TQ // QG, group, 0)

        @pl.when(i == nt - 1)
        def _():
            pltpu.sync_copy(dk_acc, dk_hbm)
            pltpu.sync_copy(dv_acc, dv_hbm)
            pltpu.sync_copy(db_acc, db_hbm)

    row = lambda w: pl.BlockSpec((TQ, w), lambda i: (i, 0))
    return pl.pallas_call(
        body, grid=(nt,),
        in_specs=_attn_window_specs() + [row(CW), row(CW), row(CW), _const((NH, QG, KG))],
        out_specs=[row(CW), _any(), _any(), _any()],
        out_shape=[jax.ShapeDtypeStruct((t, CW), BF16), jax.ShapeDtypeStruct((t + TQ, CW), F32),
                   jax.ShapeDtypeStruct((t + TQ, CW), F32), jax.ShapeDtypeStruct((NH, QG, KG), F32)],
        scratch_shapes=[pltpu.VMEM((2 * TQ, CW), BF16), pltpu.VMEM((2 * TQ, CW), BF16),
                        pltpu.VMEM((t + TQ, CW), F32), pltpu.VMEM((t + TQ, CW), F32),
                        pltpu.VMEM((NH, QG, KG), F32)],
        compiler_params=_cp(("arbitrary",)), name="bwd_attn",
    )(proj, proj, proj, proj, proj, o, do, lse, bias2)


def bwd_inproj(dxm, x, dhc, dbg, dcg, dq, dk, dv, g, w_all):
    t = x.shape[0]
    wc = PROJ // NCHIP

    def body(dxm_ref, x_ref, dhc_ref, dbg_ref, dcg_ref, dq_ref, dk_ref, dv_ref, g_ref, w_hbm,
             dx_ref, dp_ref, h_ref, dg_ref, w_v):
        @pl.when(pl.program_id(0) == 0)
        def _():
            pltpu.sync_copy(w_hbm, w_v)
            dg_ref[...] = jnp.zeros_like(dg_ref)

        dp_ref[:, 0:CW] = dhc_ref[...]
        dp_ref[:, CW:2 * CW] = dbg_ref[...]
        dp_ref[:, 2 * CW:3 * CW] = dcg_ref[...]
        dp_ref[:, 3 * CW:4 * CW] = dq_ref[...]
        dp_ref[:, 4 * CW:5 * CW] = dk_ref[...].astype(BF16)
        dp_ref[:, 5 * CW:6 * CW] = dv_ref[...].astype(BF16)
        dh = jnp.zeros((TQ, D), F32)
        for b in range(NCHIP):
            dh = dh + lax.dot_general(dp_ref[:, wc * b:wc * (b + 1)], w_v[b], NT, preferred_element_type=F32)
        xv = x_ref[...]
        gv = g_ref[...]
        h_ref[...] = _rms(xv, gv).astype(BF16)
        dxv, dgv = _rms_bwd(dh, xv, gv)
        dg_ref[...] += dgv
        dx_ref[...] = dxm_ref[...] + dxv

    row = lambda w: pl.BlockSpec((TQ, w), lambda i: (i, 0))
    pad = pl.BlockSpec((TQ, CW), lambda i: (i + 1, 0))
    return pl.pallas_call(
        body, grid=(t // TQ,),
        in_specs=[row(D), row(D), row(CW), row(CW), row(CW), row(CW), pad, pad, _const((1, D)), _any()],
        out_specs=[row(D), row(PROJ), row(D), _const((1, D))],
        out_shape=[jax.ShapeDtypeStruct((t, D), F32), jax.ShapeDtypeStruct((t, PROJ), BF16),
                   jax.ShapeDtypeStruct((t, D), BF16), jax.ShapeDtypeStruct((1, D), F32)],
        scratch_shapes=[pltpu.VMEM((NCHIP, D, wc), BF16)],
        compiler_params=_cp(("arbitrary",)), name="bwd_inproj",
    )(dxm, x, dhc, dbg, dcg, dq, dk, dv, g, w_all)


def wgrad(a, b, kb, nb, by_columns, name):
    t, k = a.shape
    n = b.shape[1]
    tk = 512

    def body(a_ref, b_ref, o_ref):
        o_ref[...] = jnp.zeros_like(o_ref)
        for c in range(t // tk):
            o_ref[...] += lax.dot_general(a_ref[tk * c:tk * (c + 1), :], b_ref[tk * c:tk * (c + 1), :], TN,
                                          preferred_element_type=F32)

    if by_columns:
        assert nb == n // NCHIP
        out_spec = pl.BlockSpec((None, kb, nb), lambda ki, ni: (ni, ki, 0))
        out_shape = jax.ShapeDtypeStruct((NCHIP, k, nb), F32)
    else:
        assert nb == n
        out_spec = pl.BlockSpec((kb, nb), lambda ki, ni: (ki, 0))
        out_shape = jax.ShapeDtypeStruct((k, n), F32)
    return pl.pallas_call(
        body, grid=(k // kb, n // nb),
        in_specs=[pl.BlockSpec((t, kb), lambda ki, ni: (0, ki)), pl.BlockSpec((t, nb), lambda ki, ni: (0, ni))],
        out_specs=out_spec, out_shape=out_shape,
        compiler_params=_cp(("arbitrary", "arbitrary")), name=name)(a, b)


LEFT = BAND - CHUNK
TOE = 1024
N_FLAT = LEFT - REL_CLIP + 1
N_VAR = BAND - N_FLAT


def _diag_vector(table):
    last = table[:, 2 * REL_CLIP:]
    var = table[:, 2 * REL_CLIP - N_VAR:2 * REL_CLIP][:, ::-1]
    return jnp.concatenate([jnp.broadcast_to(last, (NH, N_FLAT)), var, jnp.broadcast_to(last, (NH, TOE - BAND))], axis=1)


def _diag_vector_bwd(dvec):
    dlast = jnp.sum(dvec[:, :N_FLAT], axis=1, keepdims=True) + jnp.sum(dvec[:, BAND:], axis=1, keepdims=True)
    dvar = dvec[:, N_FLAT:BAND][:, ::-1]
    return jnp.concatenate([jnp.zeros((NH, 2 * REL_CLIP - N_VAR), F32), dvar, dlast], axis=1)


def _band_valid():
    r = lax.broadcasted_iota(jnp.int32, (QG, KG), 0)
    p = lax.broadcasted_iota(jnp.int32, (QG, KG), 1)
    start = jnp.where(r >= CHUNK, CHUNK, 0)
    return (p >= start) & (p < start + BAND)


def bias_expand(vec):
    def body(v_ref, o_ref):
        valid = _band_valid()
        for h in range(NH):
            rows = jnp.broadcast_to(v_ref[h:h + 1, :], (QG, TOE))
            toe = pltpu.roll(rows, 0, 1, stride=1, stride_axis=0)
            o_ref[h] = jnp.where(valid, toe[:, :KG], NEG_INF)

    return pl.pallas_call(body, out_shape=jax.ShapeDtypeStruct((NH, QG, KG), F32), name="bias_expand")(vec)


def bias_reduce(db2):
    def body(d_ref, o_ref):
        for h in range(NH):
            d = jnp.concatenate([jnp.zeros((QG, TOE - KG), F32), d_ref[h]], axis=1)
            back = pltpu.roll(d, 0, 1, stride=1, stride_axis=0)
            o_ref[h:h + 1, :] = jnp.sum(back, axis=0, keepdims=True)

    rev = pl.pallas_call(body, out_shape=jax.ShapeDtypeStruct((NH, TOE), F32), name="bias_reduce")(db2[:, :, ::-1])
    return rev[:, ::-1]


def _place():
    x, y, c = lax.axis_index("x"), lax.axis_index("y"), lax.axis_index("c")
    chips = [(1 - x, y), (x, 1 - y), (1 - x, 1 - y)]
    return x, y, c, chips


def _half(ref_rows, c):
    return pl.ds(c * (ref_rows // 2), ref_rows // 2)


HBM_SPEC = pl.BlockSpec(memory_space=pltpu.HBM)
SEM_SPEC = pl.BlockSpec(memory_space=pltpu.SEMAPHORE)
IN_FLIGHT = pltpu.CompilerParams(has_side_effects=pltpu.SideEffectType.DATAFLOW_SIDE_EFFECTING)
N_ICI = 3 * 4


def _in_hbm(a):
    return pltpu.with_memory_space_constraint(a, pltpu.HBM)


def cast_to_slot(w, chip, layer):
    _, rows, cols = w.shape
    rb = rows // 4

    def body(b_ref, w_ref, o_ref):
        del b_ref
        o_ref[...] = w_ref[...].astype(BF16)

    grid_spec = pltpu.PrefetchScalarGridSpec(
        num_scalar_prefetch=1, grid=(rows // rb,),
        in_specs=[pl.BlockSpec((None, rb, cols), lambda r, b: (layer, r, 0))],
        out_specs=pl.BlockSpec((None, rb, cols), lambda r, b: (b[0], r, 0)))
    return pl.pallas_call(body, grid_spec=grid_spec, out_shape=jax.ShapeDtypeStruct((NCHIP, rows, cols), BF16),
                          compiler_params=_cp(("arbitrary",)), name="cast_to_slot")(chip, w)


def _gather_copies(bufs, send, recv):
    x, y, c, chips = _place()
    b = 2 * x + y
    out = []
    for k, buf in enumerate(bufs):
        rows = buf.shape[1]
        mine = buf.at[b, _half(rows, c), :]
        for j, (cx, cy) in enumerate(chips):
            theirs = buf.at[2 * cx + cy, _half(rows, c), :]
            sems = dict(send_sem=send.at[3 * k + j], recv_sem=recv.at[3 * k + j],
                        device_id=(cx, cy, c), device_id_type=MESH)
            out.append((pltpu.make_async_remote_copy(src_ref=mine, dst_ref=mine, **sems),
                        pltpu.make_async_remote_copy(src_ref=theirs, dst_ref=theirs, **sems)))
    return out


def gather_start(bufs, after, layer):
    n = len(bufs)

    def body(*refs):
        ins = refs[:n]
        send, recv = refs[n + 1], refs[n + 2]
        token = refs[-1]
        for start, _ in _gather_copies(ins, send, recv):
            start.start()
        token[...] = jnp.zeros_like(token)

    sems = pltpu.SemaphoreType.DMA((N_ICI,))
    res = pl.pallas_call(
        body, name=f"gather_start_{layer}",
        in_specs=[HBM_SPEC] * n + [_any()],
        out_specs=[SEM_SPEC, SEM_SPEC] + [HBM_SPEC] * n + [pl.BlockSpec(memory_space=pltpu.VMEM)],
        out_shape=[sems, sems] + [pltpu.HBM(b.shape, b.dtype) for b in bufs] + [jax.ShapeDtypeStruct((8, LANES), F32)],
        input_output_aliases={k: 2 + k for k in range(n)}, compiler_params=IN_FLIGHT,
    )(*[_in_hbm(b) for b in bufs], after)
    return res[0], res[1], res[2:2 + n], res[-1]


def gather_wait(send, recv, bufs, after, layer):
    n = len(bufs)

    def body(*refs):
        ins = refs[:n]
        send_ref, recv_ref = refs[n], refs[n + 1]
        for start, arrival in _gather_copies(ins, send_ref, recv_ref):
            start.wait_send()
            arrival.wait_recv()

    return pl.pallas_call(
        body, name=f"gather_wait_{layer}",
        in_specs=[HBM_SPEC] * n + [SEM_SPEC, SEM_SPEC, _any()], out_specs=[HBM_SPEC] * n,
        out_shape=[pltpu.HBM(b.shape, b.dtype) for b in bufs],
        input_output_aliases={k: k for k in range(n)}, compiler_params=IN_FLIGHT,
    )(*bufs, send, recv, after)


def gather_forward(bufs):
    n = len(bufs)

    def body(*refs):
        outs = refs[n:2 * n]
        send, recv = refs[2 * n:]
        x, y, c, chips = _place()
        cps = []
        for k in range(n):
            rows = outs[k].shape[1]
            for j, (cx, cy) in enumerate(chips):
                sems = dict(send_sem=send.at[3 * k + j], recv_sem=recv.at[3 * k + j],
                            device_id=(x, y, 1 - c), device_id_type=MESH)
                mine = outs[k].at[2 * cx + cy, _half(rows, c), :]
                theirs = outs[k].at[2 * cx + cy, _half(rows, 1 - c), :]
                cp = pltpu.make_async_remote_copy(src_ref=mine, dst_ref=mine, **sems)
                cp.start()
                cps.append((cp, pltpu.make_async_remote_copy(src_ref=theirs, dst_ref=theirs, **sems)))
        for cp, arrival in cps:
            cp.wait_send()
            arrival.wait_recv()

    return pl.pallas_call(
        body, in_specs=[_any()] * n, out_specs=[_any()] * n,
        out_shape=[jax.ShapeDtypeStruct(b.shape, b.dtype) for b in bufs], input_output_aliases={k: k for k in range(n)},
        scratch_shapes=[pltpu.SemaphoreType.DMA((N_ICI,)), pltpu.SemaphoreType.DMA((N_ICI,))],
        name="gather_forward")(*bufs)


def pair_exchange(gs):
    n = len(gs)

    def body(*refs):
        ins, outs = refs[:n], refs[n:2 * n]
        send, recv = refs[2 * n:]
        x, y, c, _ = _place()
        cps = []
        for k in range(n):
            rows = ins[k].shape[1]
            cp = pltpu.make_async_remote_copy(
                src_ref=ins[k].at[:, _half(rows, 1 - c), :], dst_ref=outs[k],
                send_sem=send.at[k], recv_sem=recv.at[k], device_id=(x, y, 1 - c), device_id_type=MESH)
            cp.start()
            cps.append(cp)
        for cp in cps:
            cp.wait()

    out_shape = [jax.ShapeDtypeStruct((g.shape[0], g.shape[1] // 2, g.shape[2]), g.dtype) for g in gs]
    return pl.pallas_call(
        body, in_specs=[_any()] * n, out_specs=[_any()] * n, out_shape=out_shape,
        scratch_shapes=[pltpu.SemaphoreType.DMA((n,)), pltpu.SemaphoreType.DMA((n,))], name="pair_exchange")(*gs)


def add_pair(g, r1, core):
    ns, rows, cols = g.shape
    hr = rows // 2

    def body(c_ref, g_ref, r_ref, o_ref):
        del c_ref
        o_ref[...] = (g_ref[...] + r_ref[...]).astype(BF16)

    blk = (None, hr, cols)
    grid_spec = pltpu.PrefetchScalarGridSpec(
        num_scalar_prefetch=1, grid=(ns,),
        in_specs=[pl.BlockSpec(blk, lambda s, c: (s, c[0], 0)), pl.BlockSpec(blk, lambda s, c: (s, 0, 0))],
        out_specs=pl.BlockSpec(blk, lambda s, c: (s, 0, 0)))
    return pl.pallas_call(body, grid_spec=grid_spec, out_shape=jax.ShapeDtypeStruct(r1.shape, BF16),
                          compiler_params=_cp(("arbitrary",)), name="add_pair")(core, g, r1)


def _scatter_copies(srcs, lands, send, recv):
    _, _, c, chips = _place()
    out = []
    for k, (src, land) in enumerate(zip(srcs, lands)):
        for j, (cx, cy) in enumerate(chips):
            out.append(pltpu.make_async_remote_copy(
                src_ref=src.at[2 * cx + cy], dst_ref=land.at[j], send_sem=send.at[3 * k + j],
                recv_sem=recv.at[3 * k + j], device_id=(cx, cy, c), device_id_type=MESH))
    return out


def scatter_start(srcs, layer):
    n = len(srcs)
    lands = [lax.empty((3,) + s.shape[1:], s.dtype) for s in srcs]

    def body(*refs):
        ins, land_refs = refs[:n], refs[n:2 * n]
        send, recv = refs[2 * n], refs[2 * n + 1]
        token = refs[-1]
        for cp in _scatter_copies(ins, land_refs, send, recv):
            cp.start()
        token[...] = jnp.zeros_like(token)

    sems = pltpu.SemaphoreType.DMA((N_ICI,))
    res = pl.pallas_call(
        body, name=f"scatter_start_{layer}",
        in_specs=[HBM_SPEC] * (2 * n),
        out_specs=[SEM_SPEC, SEM_SPEC] + [HBM_SPEC] * (2 * n) + [pl.BlockSpec(memory_space=pltpu.VMEM)],
        out_shape=[sems, sems] + [pltpu.HBM(a.shape, a.dtype) for a in srcs + lands]
        + [jax.ShapeDtypeStruct((8, LANES), F32)],
        input_output_aliases={k: 2 + k for k in range(2 * n)}, compiler_params=IN_FLIGHT,
    )(*[_in_hbm(a) for a in srcs + lands])
    return res[0], res[1], res[2:2 + n], res[2 + n:2 + 2 * n], res[-1]


def scatter_wait(send, recv, srcs, lands, after, layer):
    n = len(srcs)

    def body(*refs):
        ins, land_refs = refs[:n], refs[n:2 * n]
        send_ref, recv_ref = refs[2 * n], refs[2 * n + 1]
        for cp in _scatter_copies(ins, land_refs, send_ref, recv_ref):
            cp.wait_send()
            cp.wait_recv()

    res = pl.pallas_call(
        body, name=f"scatter_wait_{layer}",
        in_specs=[HBM_SPEC] * (2 * n) + [SEM_SPEC, SEM_SPEC, _any()], out_specs=[HBM_SPEC] * (2 * n),
        out_shape=[pltpu.HBM(a.shape, a.dtype) for a in list(srcs) + list(lands)],
        input_output_aliases={k: k for k in range(2 * n)}, compiler_params=IN_FLIGHT,
    )(*srcs, *lands, send, recv, after)
    return res[n:]


def add_chips(g, r1, r2, place, total, layer):
    _, rows, cols = g.shape
    hr = rows // 2

    def body(p_ref, g_ref, r1_ref, r2_ref, t_hbm, o_ref):
        del p_ref, t_hbm
        own = g_ref[...] + r1_ref[...]
        o_ref[...] = ((own + r2_ref[0].astype(F32)) + r2_ref[1].astype(F32)) + r2_ref[2].astype(F32)

    grid_spec = pltpu.PrefetchScalarGridSpec(
        num_scalar_prefetch=1, grid=(1,),
        in_specs=[pl.BlockSpec((None, hr, cols), lambda i, p: (p[1], p[0], 0)),
                  pl.BlockSpec((None, hr, cols), lambda i, p: (p[1], 0, 0)),
                  pl.BlockSpec((3, hr, cols), lambda i, p: (0, 0, 0)), _any()],
        out_specs=pl.BlockSpec((None, hr, cols), lambda i, p: (layer, p[0], 0)))
    return pl.pallas_call(body, grid_spec=grid_spec, out_shape=jax.ShapeDtypeStruct(total.shape, F32),
                          input_output_aliases={4: 0}, compiler_params=_cp(("arbitrary",)),
                          name="add_chips")(place, g, r1, r2, total)


def pair_share(gs):
    n = len(gs)

    def body(*refs):
        outs = refs[n:2 * n]
        send, recv = refs[2 * n:]
        x, y, c, _ = _place()
        cps = []
        for k in range(n):
            mine = outs[k].at[:, _half(outs[k].shape[1], c), :]
            cp = pltpu.make_async_remote_copy(
                src_ref=mine, dst_ref=mine, send_sem=send.at[k], recv_sem=recv.at[k],
                device_id=(x, y, 1 - c), device_id_type=MESH)
            cp.start()
            cps.append(cp)
        for k, cp in enumerate(cps):
            cp.wait_send()
            theirs = outs[k].at[:, _half(outs[k].shape[1], 1 - c), :]
            pltpu.make_async_remote_copy(
                src_ref=theirs, dst_ref=theirs, send_sem=send.at[k], recv_sem=recv.at[k],
                device_id=(x, y, 1 - c), device_id_type=MESH).wait_recv()

    return pl.pallas_call(
        body, in_specs=[_any()] * n, out_specs=[_any()] * n,
        out_shape=[jax.ShapeDtypeStruct(g.shape, g.dtype) for g in gs], input_output_aliases={k: k for k in range(n)},
        scratch_shapes=[pltpu.SemaphoreType.DMA((n,)), pltpu.SemaphoreType.DMA((n,))],
        name="pair_share")(*gs)


def small_collect(v, reduce, name):
    rows = v.shape[0]
    flips = [(fx, fy, fc) for fx in (0, 1) for fy in (0, 1) for fc in (0, 1)][1:]

    def body(v_ref, o_ref, buf, send, recv):
        x, y, c, _ = _place()
        buf[4 * x + 2 * y + c] = v_ref[...]
        peers = [(jnp.where(fx, 1 - x, x), jnp.where(fy, 1 - y, y), jnp.where(fc, 1 - c, c)) for fx, fy, fc in flips]
        cps = []
        for k, peer in enumerate(peers):
            cp = pltpu.make_async_remote_copy(
                src_ref=v_ref, dst_ref=buf.at[4 * x + 2 * y + c], send_sem=send.at[k], recv_sem=recv.at[k],
                device_id=peer, device_id_type=MESH)
            cp.start()
            cps.append(cp)
        for k, (px, py, pc) in enumerate(peers):
            pltpu.make_async_remote_copy(
                src_ref=v_ref, dst_ref=buf.at[4 * px + 2 * py + pc], send_sem=send.at[k], recv_sem=recv.at[k],
                device_id=(px, py, pc), device_id_type=MESH).wait_recv()
        for cp in cps:
            cp.wait_send()
        if reduce:
            acc = buf[0]
            for s in range(1, 8):
                acc = acc + buf[s]
            o_ref[...] = acc
        else:
            o_ref[...] = buf[...]

    vm = pl.BlockSpec(memory_space=pltpu.VMEM)
    out_shape = jax.ShapeDtypeStruct((rows, SMALL_COLS) if reduce else (8, rows, SMALL_COLS), F32)
    return pl.pallas_call(
        body, in_specs=[vm], out_specs=vm, out_shape=out_shape,
        scratch_shapes=[pltpu.VMEM((8, rows, SMALL_COLS), F32), pltpu.SemaphoreType.DMA((7,)),
                        pltpu.SemaphoreType.DMA((7,))],
        name=name)(v)


def adamw(w, g, m, v, rb, name):
    nl, rows, cols = w.shape

    def body(w_ref, g_ref, m_ref, v_ref, d_ref, nm_ref, nv_ref):
        gv = g_ref[...]
        nm = ADAM_B1 * m_ref[...] + (1.0 - ADAM_B1) * gv
        nv = ADAM_B2 * v_ref[...] + (1.0 - ADAM_B2) * (gv * gv)
        m_hat = nm / (1.0 - ADAM_B1 ** ADAM_STEP)
        v_hat = nv / (1.0 - ADAM_B2 ** ADAM_STEP)
        d_ref[...] = -ADAM_LR * (m_hat / (jnp.sqrt(v_hat) + ADAM_EPS) + ADAM_WD * w_ref[...])
        nm_ref[...] = nm
        nv_ref[...] = nv

    blk = pl.BlockSpec((None, rb, cols), lambda l, r: (l, r, 0))
    shp = jax.ShapeDtypeStruct(w.shape, F32)
    return pl.pallas_call(body, grid=(nl, rows // rb), in_specs=[blk] * 4, out_specs=[blk] * 3, out_shape=[shp] * 3,
                          compiler_params=_cp(("arbitrary", "arbitrary")), name=name)(w, g, m, v)


def _pack(parts, rows):
    flat = jnp.concatenate([p.reshape(-1).astype(F32) for p in parts])
    return jnp.pad(flat, (0, rows * SMALL_COLS - flat.shape[0])).reshape(rows, SMALL_COLS)


def _unpack(vec, shapes):
    flat = vec.reshape(-1)
    out, off = [], 0
    for s in shapes:
        size = 1
        for d in s:
            size *= d
        out.append(flat[off:off + size].reshape(s))
        off += size
    return out


def kernel(x, w_in, w_conv, rel_bias, g_conv_out, g_attn_out, w_out, g_pre_mix, g_post_mix, g_pre_ffn, g_post_ffn, w_ffn_in, w_ffn_out, loss_target, m_w_in, m_w_conv, m_rel_bias, m_g_conv_out, m_g_attn_out, m_w_out, m_g_pre_mix, m_g_post_mix, m_g_pre_ffn, m_g_post_ffn, m_w_ffn_in, m_w_ffn_out, v_w_in, v_w_conv, v_rel_bias, v_g_conv_out, v_g_attn_out, v_w_out, v_g_pre_mix, v_g_post_mix, v_g_pre_ffn, v_g_post_ffn, v_w_ffn_in, v_w_ffn_out):
    xi, yi, ci = lax.axis_index("x"), lax.axis_index("y"), lax.axis_index("c")
    chip = 2 * xi + yi
    nl = w_in.shape[0]
    x0 = x[0]
    target = loss_target[0]
    cwl = CW // NCHIP

    chip1 = chip.reshape(1).astype(jnp.int32)
    own = [[cast_to_slot(w, chip1, l) for w in (w_in, w_out, w_ffn_in, w_ffn_out)] for l in range(nl)]
    wc_all = small_collect(_pack([w_conv], 8), False, "gather_w_conv")
    wc_full = wc_all[0::2].reshape(NCHIP, -1)[:, :nl * cwl * 3].reshape(NCHIP, nl, cwl, 3)
    wc_full = jnp.transpose(wc_full, (1, 0, 2, 3)).reshape(nl, CW, 3)
    wconv_t = jnp.pad(jnp.transpose(wc_full, (0, 2, 1)), ((0, 0), (0, 5), (0, 0)))
    gm = jnp.kron(jnp.eye(CW // HD, dtype=F32), jnp.full((HD, HD), 1.0 / HD, F32)).astype(BF16)
    row = lambda a, l: a[l][None, :]

    def gather_finish(flight, after, layer):
        send, recv, bufs, _ = flight
        g_in, g_out, g_fi, g_fo = gather_forward(gather_wait(send, recv, bufs, after, layer))
        return g_in, g_out.reshape(D, D), g_fi, g_fo.reshape(2, DFF // 2, D)

    flight = gather_start(own[0], x0, 0)
    weights = [gather_finish(flight, x0, 0)]
    saved = []
    h = x0
    for l in range(nl):
        gw_in, gw_out, gw_fi, gw_fo = weights[l]
        g_pm = row(g_pre_mix, l)
        if l + 1 < nl:
            flight = gather_start(own[l + 1], gw_fo, l + 1)
            g_pm = g_pm + flight[3][0:1, 0:1]
        bias2 = bias_expand(_diag_vector(rel_bias[l]))
        proj = fwd_inproj(h, g_pm, gw_in)
        xmid, o, lse, y, z = fwd_mix(h, proj, bias2, wconv_t[l], row(g_conv_out, l), row(g_attn_out, l),
                                     row(g_post_mix, l), gm, gw_out)
        gu, f, xout = fwd_ffn(xmid, row(g_pre_ffn, l), row(g_post_ffn, l), gw_fi, gw_fo)
        saved.append((h, proj, bias2, xmid, o, lse, y, z, gu, f))
        h = xout
        if l + 1 < nl:
            weights.append(gather_finish(flight, xout, l + 1))
    dx, loss_blk = loss_head(h, target)

    core = ci.reshape(1).astype(jnp.int32)
    place = jnp.stack([ci, chip]).astype(jnp.int32)
    totals = [lax.empty(w.shape, F32) for w in (w_in, w_out, w_ffn_in, w_ffn_out)]
    small = {k: [None] * nl for k in ("co", "ao", "pm", "qm", "pf", "qf", "rel", "wc")}
    pending = None

    def reduce_finish(pend, after, totals):
        grads, from_sibling, (send, recv, srcs, lands, _), layer = pend
        from_chips = scatter_wait(send, recv, srcs, lands, after, layer)
        return [add_chips(g, r1, r2, place, t, layer)
                for g, r1, r2, t in zip(grads, from_sibling, from_chips, totals)]

    for l in reversed(range(nl)):
        hin, proj, bias2, xmid, o, lse, y, z, gu, f = saved[l]
        gw_in, gw_out, gw_fi, gw_fo = weights[l]
        g_qf = row(g_post_ffn, l)
        if pending is not None:
            g_qf = g_qf + pending[2][4][0:1, 0:1]
        dxm, dfb, act, dgu, h2, dg_qf, dg_pf = bwd_ffn(dx, f, xmid, gu, row(g_pre_ffn, l), g_qf, gw_fi, gw_fo)
        gr_fo = wgrad(act, dfb, 256, D, False, "wgrad_ffn_out")
        gr_fi = wgrad(h2, dgu, 512, 2 * DFF // NCHIP, True, "wgrad_ffn_in")
        dzb, do, dco, dbg, dg_qm, dg_co, dg_ao = bwd_mix(dxm, z, o, proj, wconv_t[l], row(g_conv_out, l),
                                                          row(g_attn_out, l), row(g_post_mix, l), gm, gw_out)
        gr_out = wgrad(y, dzb, 512, D, False, "wgrad_out")
        dhc, dcg, dwc = bwd_conv(dco, proj, wconv_t[l])
        dq, dk, dv, db2 = bwd_attn(proj, o, do, lse, bias2)
        dx, dproj, hb, dg_pm = bwd_inproj(dxm, hin, dhc, dbg, dcg, dq, dk, dv, row(g_pre_mix, l), gw_in)
        gr_in = wgrad(hb, dproj, 512, PROJ // NCHIP, True, "wgrad_in")
        small["co"][l], small["ao"][l], small["pm"][l], small["qm"][l] = dg_co, dg_ao, dg_pm, dg_qm
        small["pf"][l], small["qf"][l] = dg_pf, dg_qf
        small["rel"][l] = _diag_vector_bwd(bias_reduce(db2))
        small["wc"][l] = jnp.transpose(dwc[0:3], (1, 0))
        if pending is not None:
            totals = reduce_finish(pending, dx, totals)
        grads = [gr_in, gr_out.reshape(NCHIP, D // NCHIP, D), gr_fi, gr_fo.reshape(NCHIP, DFF // NCHIP, D)]
        from_sibling = pair_exchange(grads)
        pair_sums = [add_pair(g, r, core) for g, r in zip(grads, from_sibling)]
        pending = (grads, from_sibling, scatter_start(pair_sums, l), l)
    totals = reduce_finish(pending, dx, totals)
    gr_in, gr_out, gr_fi, gr_fo = pair_share(totals)

    order = ("co", "ao", "pm", "qm", "pf", "qf", "rel", "wc")
    parts = [jnp.stack(small[k]) for k in order] + [loss_blk[0:1, 0:1]]
    shapes = [p.shape for p in parts]
    red = _unpack(small_collect(_pack(parts, 40), True, "reduce_small"), shapes)
    gr_co, gr_ao, gr_pm, gr_qm, gr_pf, gr_qf, gr_rel, gr_wc_full, loss = red
    gr_co, gr_ao, gr_pm, gr_qm, gr_pf, gr_qf = [a.reshape(nl, -1) for a in (gr_co, gr_ao, gr_pm, gr_qm, gr_pf, gr_qf)]
    gr_wc = lax.dynamic_slice_in_dim(gr_wc_full, chip * cwl, cwl, axis=1)
    loss = loss.reshape(())

    big = []
    for w, g, m, v, name in ((w_in, gr_in, m_w_in, v_w_in, "adamw_in"), (w_out, gr_out, m_w_out, v_w_out, "adamw_out"),
                             (w_ffn_in, gr_fi, m_w_ffn_in, v_w_ffn_in, "adamw_ffn_in"),
                             (w_ffn_out, gr_fo, m_w_ffn_out, v_w_ffn_out, "adamw_ffn_out")):
        big.append(adamw(w, g, m, v, w.shape[1] // 4, name))
    sw = [g_conv_out, g_attn_out, g_pre_mix, g_post_mix, g_pre_ffn, g_post_ffn, rel_bias, w_conv]
    sg = [gr_co, gr_ao, gr_pm, gr_qm, gr_pf, gr_qf, gr_rel, gr_wc]
    sm = [m_g_conv_out, m_g_attn_out, m_g_pre_mix, m_g_post_mix, m_g_pre_ffn, m_g_post_ffn, m_rel_bias, m_w_conv]
    sv = [v_g_conv_out, v_g_attn_out, v_g_pre_mix, v_g_post_mix, v_g_pre_ffn, v_g_post_ffn, v_rel_bias, v_w_conv]
    sshapes = [a.shape for a in sw]
    packed = [_pack(a, 32)[None] for a in (sw, sg, sm, sv)]
    s_out = [_unpack(a[0], sshapes) for a in adamw(*packed, 32, "adamw_small")]

    def leaves(big_i, small_i):
        b_in, b_out, b_fi, b_fo = big_i
        s_co, s_ao, s_pm, s_qm, s_pf, s_qf, s_rel, s_wc = small_i
        return [b_in, s_wc, s_rel, s_co, s_ao, b_out, s_pm, s_qm, s_pf, s_qf, b_fi, b_fo]

    out = [loss, dx[None]]
    out += leaves((gr_in, gr_out, gr_fi, gr_fo), sg)
    for i in range(3):
        out += leaves([b[i] for b in big], s_out[i])
    return tuple(out)
```

```python
import functools

import jax
import jax.numpy as jnp
from jax import lax
from jax.experimental import pallas as pl
from jax.experimental.pallas import tpu as pltpu

F32 = jnp.float32
BF16 = jnp.bfloat16

D = 1024
PROJ = 3072
CW = 512
HD = 64
NH = 8
CHUNK = 64
BAND = 576
REL_CLIP = 128
NREL = 2 * REL_CLIP + 1
DFF = 2816
DEPTH = 4
NCHIP = 4
EPS = 1e-6
NEG_INF = -1e30

ADAM_LR = 0.001
ADAM_B1 = 0.9
ADAM_B2 = 0.999
ADAM_EPS = 1e-08
ADAM_WD = 0.01
ADAM_STEP = 10

V7X_VMEM_BYTES = 64 * 1024 * 1024
VMEM_LIMIT = V7X_VMEM_BYTES - 8 * 1024 * 1024
LANES = 128
QG = 2 * CHUNK
KG = QG + BAND - CHUNK
TQ = 512
TM = 256
SMALL_COLS = 1024
MESH = pl.DeviceIdType.MESH
NT = (((1,), (1,)), ((), ()))
TN = (((0,), (0,)), ((), ()))


def _cp(sem=None, vmem=VMEM_LIMIT):
    return pltpu.CompilerParams(dimension_semantics=sem, vmem_limit_bytes=vmem)


def _any():
    return pl.BlockSpec(memory_space=pl.ANY)


def _const(shape):
    nd = len(shape)
    return pl.BlockSpec(shape, lambda *_: (0,) * nd)


def _rms(v, g):
    r = lax.rsqrt(jnp.mean(v * v, axis=-1, keepdims=True) + EPS)
    return v * r * g


def _rms_bwd(dy, v, g):
    r = lax.rsqrt(jnp.mean(v * v, axis=-1, keepdims=True) + EPS)
    vh = v * r
    dg = jnp.sum(dy * vh, axis=0, keepdims=True)
    dvh = dy * g
    dv = r * (dvh - vh * jnp.mean(dvh * vh, axis=-1, keepdims=True))
    return dv, dg


def _group_mean(v, gm):
    hi = v.astype(BF16)
    lo = (v - hi.astype(F32)).astype(BF16)
    return jnp.dot(hi, gm, preferred_element_type=F32) + jnp.dot(lo, gm, preferred_element_type=F32)


def _group_rms_bwd(dy, v, g, gm):
    r = lax.rsqrt(_group_mean(v * v, gm) + EPS)
    vh = v * r
    dg = jnp.sum(dy * vh, axis=0, keepdims=True)
    dvh = dy * g
    dv = r * (dvh - vh * _group_mean(dvh * vh, gm))
    return dv, dg


def _head_masks(scale):
    lane = lax.broadcasted_iota(jnp.int32, (1, LANES), 1)
    return [jnp.where((lane >= HD * a) & (lane < HD * (a + 1)), scale, 0.0).astype(BF16) for a in range(2)]


def _conv_taps(u_prev, u, scr):
    n = u.shape[0]
    scr[0:16, :] = u_prev
    scr[16:16 + n, :] = u
    return scr[15:15 + n, :], scr[14:14 + n, :]


def fwd_inproj(x, g, w_all):
    t = x.shape[0]
    wc = PROJ // NCHIP

    def body(x_ref, g_ref, w_hbm, o_ref, w_v):
        @pl.when(pl.program_id(0) == 0)
        def _():
            pltpu.sync_copy(w_hbm, w_v)

        h = _rms(x_ref[...], g_ref[...]).astype(BF16)
        for b in range(NCHIP):
            o_ref[:, wc * b:wc * (b + 1)] = jnp.dot(h, w_v[b], preferred_element_type=F32).astype(BF16)

    return pl.pallas_call(
        body, grid=(t // TQ,),
        in_specs=[pl.BlockSpec((TQ, D), lambda i: (i, 0)), _const((1, D)), _any()],
        out_specs=pl.BlockSpec((TQ, PROJ), lambda i: (i, 0)),
        out_shape=jax.ShapeDtypeStruct((t, PROJ), BF16),
        scratch_shapes=[pltpu.VMEM((NCHIP, D, wc), BF16)],
        compiler_params=_cp(("arbitrary",)), name="fwd_inproj")(x, g, w_all)


def _attn_window_specs():
    return [
        pl.BlockSpec((TQ, CW), lambda i: (i, 3)),
        pl.BlockSpec((TQ, CW), lambda i: (jnp.maximum(i - 1, 0), 4)),
        pl.BlockSpec((TQ, CW), lambda i: (i, 4)),
        pl.BlockSpec((TQ, CW), lambda i: (jnp.maximum(i - 1, 0), 5)),
        pl.BlockSpec((TQ, CW), lambda i: (i, 5)),
    ]


def _conv_specs():
    return [
        pl.BlockSpec((TQ, 3 * CW), lambda i: (i, 0)),
        pl.BlockSpec((16, 3 * CW), lambda i: (jnp.maximum(i * (TQ // 16) - 1, 0), 0)),
    ]


def _conv_fwd(pc_ref, pcp_ref, wc_ref, scr, first):
    pc = pc_ref[...].astype(F32)
    hc, bg, cg = pc[:, :CW], pc[:, CW:2 * CW], pc[:, 2 * CW:]
    u = cg * hc
    pp = pcp_ref[...].astype(F32)
    u_prev = jnp.where(first, 0.0, pp[:, 2 * CW:] * pp[:, :CW])
    u1, u2 = _conv_taps(u_prev, u, scr)
    cout = wc_ref[0:1, :] * u2 + wc_ref[1:2, :] * u1 + wc_ref[2:3, :] * u
    return hc, bg, cg, u, u1, u2, cout


def _key_penalty(first, r0):
    col = lax.broadcasted_iota(jnp.int32, (1, KG), 1)
    limit = jnp.where(first, TQ - r0, 0)
    return jnp.where(col < limit, NEG_INF, 0.0)


def fwd_mix(x, proj, bias2, wconv_t, g_co, g_ao, g_pm, gm, wout_all):
    t = x.shape[0]

    def body(x_ref, pc_ref, pcp_ref, q_ref, kp_ref, kc_ref, vp_ref, vc_ref, b2_ref, wc_ref, gco_ref, gao_ref, gpm_ref,
             gm_ref, wout_hbm, xmid_ref, o_ref, lse_ref, y_ref, z_ref, wout_v, kwin, vwin, cscr):
        i = pl.program_id(0)
        first = i == 0

        @pl.when(first)
        def _():
            pltpu.sync_copy(wout_hbm, wout_v)

        kwin[0:TQ, :] = kp_ref[...]
        kwin[TQ:2 * TQ, :] = kc_ref[...]
        vwin[0:TQ, :] = vp_ref[...]
        vwin[TQ:2 * TQ, :] = vc_ref[...]
        qmask = _head_masks(HD ** -0.5)
        vmask = _head_masks(1.0)

        def group(g, carry):
            r0 = pl.multiple_of(g * QG, QG)
            pen = _key_penalty(first, r0)
            for hp in range(NH // 2):
                ls = slice(LANES * hp, LANES * (hp + 1))
                qb = q_ref[pl.ds(r0, QG), ls]
                kw = kwin[pl.ds(r0, KG), ls]
                vw = vwin[pl.ds(r0, KG), ls]
                o_acc = jnp.zeros((QG, LANES), F32)
                lse = jnp.zeros((QG, LANES), F32)
                for a in range(2):
                    s = lax.dot_general(qb * qmask[a], kw, NT, preferred_element_type=F32)
                    s = s + b2_ref[2 * hp + a] + pen
                    m = jnp.max(s, axis=-1, keepdims=True)
                    p = jnp.exp(s - m)
                    l = jnp.sum(p, axis=-1, keepdims=True)
                    o = jnp.dot(p.astype(BF16), vw * vmask[a], preferred_element_type=F32)
                    o_acc = o_acc + o * (1.0 / l)
                    lse = lse + (m + jnp.log(l)) * vmask[a].astype(F32)
                o_ref[pl.ds(r0, QG), ls] = o_acc
                lse_ref[pl.ds(r0, QG), ls] = lse
            return carry

        lax.fori_loop(0, TQ // QG, group, 0)

        _, bg, _, _, _, _, cout = _conv_fwd(pc_ref, pcp_ref, wc_ref, cscr, first)
        yc = bg * cout
        gmv = gm_ref[...]
        ycn = yc * lax.rsqrt(_group_mean(yc * yc, gmv) + EPS) * gco_ref[...]
        oa = o_ref[...]
        oan = oa * lax.rsqrt(_group_mean(oa * oa, gmv) + EPS) * gao_ref[...]
        y_ref[:, 0:CW] = ycn.astype(BF16)
        y_ref[:, CW:2 * CW] = oan.astype(BF16)
        z = jnp.dot(y_ref[...], wout_v[...], preferred_element_type=F32)
        z_ref[...] = z
        xmid_ref[...] = x_ref[...] + _rms(z, gpm_ref[...])

    row = lambda w: pl.BlockSpec((TQ, w), lambda i: (i, 0))
    return pl.pallas_call(
        body, grid=(t // TQ,),
        in_specs=[row(D)] + _conv_specs() + _attn_window_specs() + [
            _const((NH, QG, KG)), _const((8, CW)), _const((1, CW)), _const((1, CW)), _const((1, D)),
            _const((CW, CW)), _any()],
        out_specs=[row(D), row(CW), row(CW), row(D), row(D)],
        out_shape=[jax.ShapeDtypeStruct((t, D), F32), jax.ShapeDtypeStruct((t, CW), F32),
                   jax.ShapeDtypeStruct((t, CW), F32), jax.ShapeDtypeStruct((t, D), BF16),
                   jax.ShapeDtypeStruct((t, D), F32)],
        scratch_shapes=[pltpu.VMEM((D, D), BF16), pltpu.VMEM((2 * TQ, CW), BF16), pltpu.VMEM((2 * TQ, CW), BF16),
                        pltpu.VMEM((TQ + 16, CW), F32)],
        compiler_params=_cp(("arbitrary",)), name="fwd_mix",
    )(x, proj, proj, proj, proj, proj, proj, proj, bias2, wconv_t, g_co, g_ao, g_pm, gm, wout_all)


def fwd_ffn(xmid, g_pre, g_post, wfi_all, wfo_all):
    t = xmid.shape[0]
    hw = DFF // 2

    def body(x_ref, gpre_ref, gpost_ref, wfi_hbm, wfo_hbm, gu_ref, f_ref, xo_ref, wfi_v, wfo_v):
        @pl.when(pl.program_id(0) == 0)
        def _():
            pltpu.sync_copy(wfi_hbm, wfi_v)
            pltpu.sync_copy(wfo_hbm, wfo_v)

        xv = x_ref[...]
        h = _rms(xv, gpre_ref[...]).astype(BF16)
        f = jnp.zeros((TM, D), F32)
        for j in range(2):
            gate = jnp.dot(h, wfi_v[j], preferred_element_type=F32)
            up = jnp.dot(h, wfi_v[2 + j], preferred_element_type=F32)
            gu_ref[:, hw * j:hw * (j + 1)] = gate.astype(BF16)
            gu_ref[:, DFF + hw * j:DFF + hw * (j + 1)] = up.astype(BF16)
            act = gate * (1.0 / (1.0 + jnp.exp(-gate))) * up
            f = f + jnp.dot(act.astype(BF16), wfo_v[j], preferred_element_type=F32)
        f_ref[...] = f
        xo_ref[...] = xv + _rms(f, gpost_ref[...])

    row = lambda w: pl.BlockSpec((TM, w), lambda i: (i, 0))
    return pl.pallas_call(
        body, grid=(t // TM,),
        in_specs=[row(D), _const((1, D)), _const((1, D)), _any(), _any()],
        out_specs=[row(2 * DFF), row(D), row(D)],
        out_shape=[jax.ShapeDtypeStruct((t, 2 * DFF), BF16), jax.ShapeDtypeStruct((t, D), F32),
                   jax.ShapeDtypeStruct((t, D), F32)],
        scratch_shapes=[pltpu.VMEM((NCHIP, D, hw), BF16), pltpu.VMEM((2, hw, D), BF16)],
        compiler_params=_cp(("arbitrary",)), name="fwd_ffn")(xmid, g_pre, g_post, wfi_all, wfo_all)


def loss_head(y, target):
    t = y.shape[0]

    def body(y_ref, t_ref, dy_ref, l_ref):
        @pl.when(pl.program_id(0) == 0)
        def _():
            l_ref[...] = jnp.zeros_like(l_ref)

        e = y_ref[...] - t_ref[...]
        dy_ref[...] = e * (1.0 / D)
        rows = jnp.sum(e * e, axis=-1, keepdims=True) * (1.0 / D)
        l_ref[...] += 0.5 * jnp.sum(rows, axis=0, keepdims=True)

    row = pl.BlockSpec((TQ, D), lambda i: (i, 0))
    return pl.pallas_call(
        body, grid=(t // TQ,), in_specs=[row, row], out_specs=[row, _const((8, LANES))],
        out_shape=[jax.ShapeDtypeStruct((t, D), F32), jax.ShapeDtypeStruct((8, LANES), F32)],
        compiler_params=_cp(("arbitrary",)), name="loss_head")(y, target)


def bwd_ffn(dx, f, xmid, gu, g_pre, g_post, wfi_all, wfo_all):
    t = dx.shape[0]
    hw = DFF // 2

    def body(dx_ref, f_ref, x_ref, gu_ref, gpre_ref, gpost_ref, wfi_hbm, wfo_hbm,
             dxm_ref, df_ref, act_ref, dgu_ref, h_ref, dgpost_ref, dgpre_ref, wfi_v, wfo_v):
        @pl.when(pl.program_id(0) == 0)
        def _():
            pltpu.sync_copy(wfi_hbm, wfi_v)
            pltpu.sync_copy(wfo_hbm, wfo_v)
            dgpost_ref[...] = jnp.zeros_like(dgpost_ref)
            dgpre_ref[...] = jnp.zeros_like(dgpre_ref)

        dxo = dx_ref[...]
        df, dgp = _rms_bwd(dxo, f_ref[...], gpost_ref[...])
        dgpost_ref[...] += dgp
        dfb = df.astype(BF16)
        df_ref[...] = dfb
        dh = jnp.zeros((TM, D), F32)
        for j in range(2):
            dact = lax.dot_general(dfb, wfo_v[j], NT, preferred_element_type=F32)
            gate = gu_ref[:, hw * j:hw * (j + 1)].astype(F32)
            up = gu_ref[:, DFF + hw * j:DFF + hw * (j + 1)].astype(F32)
            sig = 1.0 / (1.0 + jnp.exp(-gate))
            silu = gate * sig
            act_ref[:, hw * j:hw * (j + 1)] = (silu * up).astype(BF16)
            dup = (dact * silu).astype(BF16)
            dgate = (dact * up * (sig * (1.0 + gate * (1.0 - sig)))).astype(BF16)
            dgu_ref[:, hw * j:hw * (j + 1)] = dgate
            dgu_ref[:, DFF + hw * j:DFF + hw * (j + 1)] = dup
            dh = dh + lax.dot_general(dgate, wfi_v[j], NT, preferred_element_type=F32)
            dh = dh + lax.dot_general(dup, wfi_v[2 + j], NT, preferred_element_type=F32)
        xv = x_ref[...]
        gpre = gpre_ref[...]
        h_ref[...] = _rms(xv, gpre).astype(BF16)
        dxv, dgq = _rms_bwd(dh, xv, gpre)
        dgpre_ref[...] += dgq
        dxm_ref[...] = dxo + dxv

    row = lambda w: pl.BlockSpec((TM, w), lambda i: (i, 0))
    return pl.pallas_call(
        body, grid=(t // TM,),
        in_specs=[row(D), row(D), row(D), row(2 * DFF), _const((1, D)), _const((1, D)), _any(), _any()],
        out_specs=[row(D), row(D), row(DFF), row(2 * DFF), row(D), _const((1, D)), _const((1, D))],
        out_shape=[jax.ShapeDtypeStruct((t, D), F32), jax.ShapeDtypeStruct((t, D), BF16),
                   jax.ShapeDtypeStruct((t, DFF), BF16), jax.ShapeDtypeStruct((t, 2 * DFF), BF16),
                   jax.ShapeDtypeStruct((t, D), BF16), jax.ShapeDtypeStruct((1, D), F32),
                   jax.ShapeDtypeStruct((1, D), F32)],
        scratch_shapes=[pltpu.VMEM((NCHIP, D, hw), BF16), pltpu.VMEM((2, hw, D), BF16)],
        compiler_params=_cp(("arbitrary",)), name="bwd_ffn")(dx, f, xmid, gu, g_pre, g_post, wfi_all, wfo_all)


def bwd_mix(dxm, z, o, proj, wconv_t, g_co, g_ao, g_pm, gm, wout_all):
    t = dxm.shape[0]

    def body(dx_ref, z_ref, o_ref, pc_ref, pcp_ref, wc_ref, gco_ref, gao_ref, gpm_ref, gm_ref, wout_hbm,
             dz_ref, do_ref, dco_ref, dbg_ref, dgpm_ref, dgco_ref, dgao_ref, wout_v, cscr):
        first = pl.program_id(0) == 0

        @pl.when(first)
        def _():
            pltpu.sync_copy(wout_hbm, wout_v)
            dgpm_ref[...] = jnp.zeros_like(dgpm_ref)
            dgco_ref[...] = jnp.zeros_like(dgco_ref)
            dgao_ref[...] = jnp.zeros_like(dgao_ref)

        dz, dgp = _rms_bwd(dx_ref[...], z_ref[...], gpm_ref[...])
        dgpm_ref[...] += dgp
        dzb = dz.astype(BF16)
        dz_ref[...] = dzb
        dy = lax.dot_general(dzb, wout_v[...], NT, preferred_element_type=F32)
        gmv = gm_ref[...]
        _, bg, _, _, _, _, cout = _conv_fwd(pc_ref, pcp_ref, wc_ref, cscr, first)
        dyc, dgc = _group_rms_bwd(dy[:, :CW], bg * cout, gco_ref[...], gmv)
        dgco_ref[...] += dgc
        dbg_ref[...] = (dyc * cout).astype(BF16)
        dco_ref[...] = dyc * bg
        do, dga = _group_rms_bwd(dy[:, CW:], o_ref[...], gao_ref[...], gmv)
        dgao_ref[...] += dga
        do_ref[...] = do.astype(BF16)

    row = lambda w: pl.BlockSpec((TQ, w), lambda i: (i, 0))
    return pl.pallas_call(
        body, grid=(t // TQ,),
        in_specs=[row(D), row(D), row(CW)] + _conv_specs() + [
            _const((8, CW)), _const((1, CW)), _const((1, CW)), _const((1, D)), _const((CW, CW)), _any()],
        out_specs=[row(D), row(CW), row(CW), row(CW), _const((1, D)), _const((1, CW)), _const((1, CW))],
        out_shape=[jax.ShapeDtypeStruct((t, D), BF16), jax.ShapeDtypeStruct((t, CW), BF16),
                   jax.ShapeDtypeStruct((t, CW), F32), jax.ShapeDtypeStruct((t, CW), BF16),
                   jax.ShapeDtypeStruct((1, D), F32), jax.ShapeDtypeStruct((1, CW), F32),
                   jax.ShapeDtypeStruct((1, CW), F32)],
        scratch_shapes=[pltpu.VMEM((D, D), BF16), pltpu.VMEM((TQ + 16, CW), F32)],
        compiler_params=_cp(("arbitrary",)), name="bwd_mix",
    )(dxm, z, o, proj, proj, wconv_t, g_co, g_ao, g_pm, gm, wout_all)


def bwd_conv(dco, proj, wconv_t):
    t = dco.shape[0]
    nt = t // TQ

    def body(d_ref, dn_ref, pc_ref, pcp_ref, wc_ref, dhc_ref, dcg_ref, dw_ref, cscr, dscr):
        i = pl.program_id(0)
        first = i == 0

        @pl.when(first)
        def _():
            dw_ref[...] = jnp.zeros_like(dw_ref)

        hc, _, cg, u, u1, u2, _ = _conv_fwd(pc_ref, pcp_ref, wc_ref, cscr, first)
        d0 = d_ref[...]
        dscr[0:TQ, :] = d0
        dscr[TQ:TQ + 8, :] = jnp.where(i == nt - 1, 0.0, dn_ref[...])
        d1 = dscr[1:TQ + 1, :]
        d2 = dscr[2:TQ + 2, :]
        du = wc_ref[2:3, :] * d0 + wc_ref[1:2, :] * d1 + wc_ref[0:1, :] * d2
        dhc_ref[...] = (du * cg).astype(BF16)
        dcg_ref[...] = (du * hc).astype(BF16)
        dw_ref[0:1, :] += jnp.sum(d0 * u2, axis=0, keepdims=True)
        dw_ref[1:2, :] += jnp.sum(d0 * u1, axis=0, keepdims=True)
        dw_ref[2:3, :] += jnp.sum(d0 * u, axis=0, keepdims=True)

    row = lambda w: pl.BlockSpec((TQ, w), lambda i: (i, 0))
    nxt = pl.BlockSpec((8, CW), lambda i: (jnp.minimum((i + 1) * (TQ // 8), t // 8 - 1), 0))
    return pl.pallas_call(
        body, grid=(nt,),
        in_specs=[row(CW), nxt] + _conv_specs() + [_const((8, CW))],
        out_specs=[row(CW), row(CW), _const((8, CW))],
        out_shape=[jax.ShapeDtypeStruct((t, CW), BF16), jax.ShapeDtypeStruct((t, CW), BF16),
                   jax.ShapeDtypeStruct((8, CW), F32)],
        scratch_shapes=[pltpu.VMEM((TQ + 16, CW), F32), pltpu.VMEM((TQ + 8, CW), F32)],
        compiler_params=_cp(("arbitrary",)), name="bwd_conv")(dco, dco, proj, proj, wconv_t)


def bwd_attn(proj, o, do, lse, bias2):
    t = o.shape[0]
    nt = t // TQ

    def body(q_ref, kp_ref, kc_ref, vp_ref, vc_ref, o_ref, do_ref, lse_ref, b2_ref,
             dq_ref, dk_hbm, dv_hbm, db_hbm, kwin, vwin, dk_acc, dv_acc, db_acc):
        i = pl.program_id(0)
        first = i == 0

        @pl.when(first)
        def _():
            dk_acc[...] = jnp.zeros_like(dk_acc)
            dv_acc[...] = jnp.zeros_like(dv_acc)
            db_acc[...] = jnp.zeros_like(db_acc)

        kwin[0:TQ, :] = kp_ref[...]
        kwin[TQ:2 * TQ, :] = kc_ref[...]
        vwin[0:TQ, :] = vp_ref[...]
        vwin[TQ:2 * TQ, :] = vc_ref[...]
        scale = HD ** -0.5
        qmask = _head_masks(scale)
        vmask = _head_masks(1.0)

        def group(g, carry):
            r0 = pl.multiple_of(g * QG, QG)
            base = pl.multiple_of(i * TQ + r0, QG)
            pen = _key_penalty(first, r0)
            for hp in range(NH // 2):
                ls = slice(LANES * hp, LANES * (hp + 1))
                qb = q_ref[pl.ds(r0, QG), ls]
                kw = kwin[pl.ds(r0, KG), ls]
                vw = vwin[pl.ds(r0, KG), ls]
                dob = do_ref[pl.ds(r0, QG), ls]
                prod = dob.astype(F32) * o_ref[pl.ds(r0, QG), ls]
                lseb = lse_ref[pl.ds(r0, QG), ls]
                dq = jnp.zeros((QG, LANES), F32)
                dk = jnp.zeros((KG, LANES), F32)
                dv = jnp.zeros((KG, LANES), F32)
                for a in range(2):
                    qa = qb * qmask[a]
                    doa = dob * vmask[a]
                    s = lax.dot_general(qa, kw, NT, preferred_element_type=F32)
                    s = s + b2_ref[2 * hp + a] + pen
                    p = jnp.exp(s - lseb[:, HD * a:HD * a + 1])
                    dp = lax.dot_general(doa, vw, NT, preferred_element_type=F32)
                    dsum = jnp.sum(prod * vmask[a].astype(F32), axis=-1, keepdims=True)
                    ds = p * (dp - dsum)
                    db_acc[2 * hp + a] += ds
                    dsb = ds.astype(BF16)
                    dq = dq + jnp.dot(dsb, kw, preferred_element_type=F32) * qmask[a].astype(F32)
                    dk = dk + lax.dot_general(dsb, qa, TN, preferred_element_type=F32)
                    dv = dv + lax.dot_general(p.astype(BF16), doa, TN, preferred_element_type=F32)
                dq_ref[pl.ds(r0, QG), ls] = dq.astype(BF16)
                dk_acc[pl.ds(base, KG), ls] += dk
                dv_acc[pl.ds(base, KG), ls] += dv
            return carry

        lax.fori_loop(0, TQ // QG, group, 0)

        @pl.when(i == nt - 1)
        def _():
            pltpu.sync_copy(dk_acc, dk_hbm)
            pltpu.sync_copy(dv_acc, dv_hbm)
            pltpu.sync_copy(db_acc, db_hbm)

    row = lambda w: pl.BlockSpec((TQ, w), lambda i: (i, 0))
    return pl.pallas_call(
        body, grid=(nt,),
        in_specs=_attn_window_specs() + [row(CW), row(CW), row(CW), _const((NH, QG, KG))],
        out_specs=[row(CW), _any(), _any(), _any()],
        out_shape=[jax.ShapeDtypeStruct((t, CW), BF16), jax.ShapeDtypeStruct((t + TQ, CW), F32),
                   jax.ShapeDtypeStruct((t + TQ, CW), F32), jax.ShapeDtypeStruct((NH, QG, KG), F32)],
        scratch_shapes=[pltpu.VMEM((2 * TQ, CW), BF16), pltpu.VMEM((2 * TQ, CW), BF16),
                        pltpu.VMEM((t + TQ, CW), F32), pltpu.VMEM((t + TQ, CW), F32),
                        pltpu.VMEM((NH, QG, KG), F32)],
        compiler_params=_cp(("arbitrary",)), name="bwd_attn",
    )(proj, proj, proj, proj, proj, o, do, lse, bias2)


def bwd_inproj(dxm, x, dhc, dbg, dcg, dq, dk, dv, g, w_all):
    t = x.shape[0]
    wc = PROJ // NCHIP

    def body(dxm_ref, x_ref, dhc_ref, dbg_ref, dcg_ref, dq_ref, dk_ref, dv_ref, g_ref, w_hbm,
             dx_ref, dp_ref, h_ref, dg_ref, w_v):
        @pl.when(pl.program_id(0) == 0)
        def _():
            pltpu.sync_copy(w_hbm, w_v)
            dg_ref[...] = jnp.zeros_like(dg_ref)

        dp_ref[:, 0:CW] = dhc_ref[...]
        dp_ref[:, CW:2 * CW] = dbg_ref[...]
        dp_ref[:, 2 * CW:3 * CW] = dcg_ref[...]
        dp_ref[:, 3 * CW:4 * CW] = dq_ref[...]
        dp_ref[:, 4 * CW:5 * CW] = dk_ref[...].astype(BF16)
        dp_ref[:, 5 * CW:6 * CW] = dv_ref[...].astype(BF16)
        dh = jnp.zeros((TQ, D), F32)
        for b in range(NCHIP):
            dh = dh + lax.dot_general(dp_ref[:, wc * b:wc * (b + 1)], w_v[b], NT, preferred_element_type=F32)
        xv = x_ref[...]
        gv = g_ref[...]
        h_ref[...] = _rms(xv, gv).astype(BF16)
        dxv, dgv = _rms_bwd(dh, xv, gv)
        dg_ref[...] += dgv
        dx_ref[...] = dxm_ref[...] + dxv

    row = lambda w: pl.BlockSpec((TQ, w), lambda i: (i, 0))
    pad = pl.BlockSpec((TQ, CW), lambda i: (i + 1, 0))
    return pl.pallas_call(
        body, grid=(t // TQ,),
        in_specs=[row(D), row(D), row(CW), row(CW), row(CW), row(CW), pad, pad, _const((1, D)), _any()],
        out_specs=[row(D), row(PROJ), row(D), _const((1, D))],
        out_shape=[jax.ShapeDtypeStruct((t, D), F32), jax.ShapeDtypeStruct((t, PROJ), BF16),
                   jax.ShapeDtypeStruct((t, D), BF16), jax.ShapeDtypeStruct((1, D), F32)],
        scratch_shapes=[pltpu.VMEM((NCHIP, D, wc), BF16)],
        compiler_params=_cp(("arbitrary",)), name="bwd_inproj",
    )(dxm, x, dhc, dbg, dcg, dq, dk, dv, g, w_all)


def wgrad(a, b, kb, nb, by_columns, name):
    t, k = a.shape
    n = b.shape[1]
    tk = 512

    def body(a_ref, b_ref, o_ref):
        o_ref[...] = jnp.zeros_like(o_ref)
        for c in range(t // tk):
            o_ref[...] += lax.dot_general(a_ref[tk * c:tk * (c + 1), :], b_ref[tk * c:tk * (c + 1), :], TN,
                                          preferred_element_type=F32)

    if by_columns:
        assert nb == n // NCHIP
        out_spec = pl.BlockSpec((None, kb, nb), lambda ki, ni: (ni, ki, 0))
        out_shape = jax.ShapeDtypeStruct((NCHIP, k, nb), F32)
    else:
        assert nb == n
        out_spec = pl.BlockSpec((kb, nb), lambda ki, ni: (ki, 0))
        out_shape = jax.ShapeDtypeStruct((k, n), F32)
    return pl.pallas_call(
        body, grid=(k // kb, n // nb),
        in_specs=[pl.BlockSpec((t, kb), lambda ki, ni: (0, ki)), pl.BlockSpec((t, nb), lambda ki, ni: (0, ni))],
        out_specs=out_spec, out_shape=out_shape,
        compiler_params=_cp(("arbitrary", "arbitrary")), name=name)(a, b)


LEFT = BAND - CHUNK
TOE = 1024
N_FLAT = LEFT - REL_CLIP + 1
N_VAR = BAND - N_FLAT


def _diag_vector(table):
    last = table[:, 2 * REL_CLIP:]
    var = table[:, 2 * REL_CLIP - N_VAR:2 * REL_CLIP][:, ::-1]
    return jnp.concatenate([jnp.broadcast_to(last, (NH, N_FLAT)), var, jnp.broadcast_to(last, (NH, TOE - BAND))], axis=1)


def _diag_vector_bwd(dvec):
    dlast = jnp.sum(dvec[:, :N_FLAT], axis=1, keepdims=True) + jnp.sum(dvec[:, BAND:], axis=1, keepdims=True)
    dvar = dvec[:, N_FLAT:BAND][:, ::-1]
    return jnp.concatenate([jnp.zeros((NH, 2 * REL_CLIP - N_VAR), F32), dvar, dlast], axis=1)


def _band_valid():
    r = lax.broadcasted_iota(jnp.int32, (QG, KG), 0)
    p = lax.broadcasted_iota(jnp.int32, (QG, KG), 1)
    start = jnp.where(r >= CHUNK, CHUNK, 0)
    return (p >= start) & (p < start + BAND)


def bias_expand(vec):
    def body(v_ref, o_ref):
        valid = _band_valid()
        for h in range(NH):
            rows = jnp.broadcast_to(v_ref[h:h + 1, :], (QG, TOE))
            toe = pltpu.roll(rows, 0, 1, stride=1, stride_axis=0)
            o_ref[h] = jnp.where(valid, toe[:, :KG], NEG_INF)

    return pl.pallas_call(body, out_shape=jax.ShapeDtypeStruct((NH, QG, KG), F32), name="bias_expand")(vec)


def bias_reduce(db2):
    def body(d_ref, o_ref):
        for h in range(NH):
            d = jnp.concatenate([jnp.zeros((QG, TOE - KG), F32), d_ref[h]], axis=1)
            back = pltpu.roll(d, 0, 1, stride=1, stride_axis=0)
            o_ref[h:h + 1, :] = jnp.sum(back, axis=0, keepdims=True)

    rev = pl.pallas_call(body, out_shape=jax.ShapeDtypeStruct((NH, TOE), F32), name="bias_reduce")(db2[:, :, ::-1])
    return rev[:, ::-1]


def _place():
    x, y, c = lax.axis_index("x"), lax.axis_index("y"), lax.axis_index("c")
    chips = [(1 - x, y), (x, 1 - y), (1 - x, 1 - y)]
    return x, y, c, chips


def _half(ref_rows, c):
    return pl.ds(c * (ref_rows // 2), ref_rows // 2)


HBM_SPEC = pl.BlockSpec(memory_space=pltpu.HBM)
SEM_SPEC = pl.BlockSpec(memory_space=pltpu.SEMAPHORE)
IN_FLIGHT = pltpu.CompilerParams(has_side_effects=pltpu.SideEffectType.DATAFLOW_SIDE_EFFECTING)


def _in_hbm(a):
    return pltpu.with_memory_space_constraint(a, pltpu.HBM)


def cast_to_slot(w, chip, layer):
    _, rows, cols = w.shape
    rb = rows // 4

    def body(b_ref, w_ref, o_ref):
        del b_ref
        o_ref[...] = w_ref[...].astype(BF16)

    grid_spec = pltpu.PrefetchScalarGridSpec(
        num_scalar_prefetch=1, grid=(rows // rb,),
        in_specs=[pl.BlockSpec((None, rb, cols), lambda r, b: (layer, r, 0))],
        out_specs=pl.BlockSpec((None, rb, cols), lambda r, b: (b[0], r, 0)))
    return pl.pallas_call(body, grid_spec=grid_spec, out_shape=jax.ShapeDtypeStruct((NCHIP, rows, cols), BF16),
                          compiler_params=_cp(("arbitrary",)), name="cast_to_slot")(chip, w)


def _gather_copies(bufs, send, recv):
    x, y, c, chips = _place()
    b = 2 * x + y
    out = []
    for k, buf in enumerate(bufs):
        rows = buf.shape[1]
        mine = buf.at[b, _half(rows, c), :]
        for j, (cx, cy) in enumerate(chips):
            theirs = buf.at[2 * cx + cy, _half(rows, c), :]
            sems = dict(send_sem=send.at[3 * k + j], recv_sem=recv.at[3 * k + j],
                        device_id=(cx, cy, c), device_id_type=MESH)
            out.append((pltpu.make_async_remote_copy(src_ref=mine, dst_ref=mine, **sems),
                        pltpu.make_async_remote_copy(src_ref=theirs, dst_ref=theirs, **sems)))
    return out


def gather_start(bufs, after, layer):
    n = len(bufs)

    def body(*refs):
        ins = refs[:n]
        send, recv = refs[n + 1], refs[n + 2]
        token = refs[-1]
        for start, _ in _gather_copies(ins, send, recv):
            start.start()
        token[...] = jnp.zeros_like(token)

    sems = pltpu.SemaphoreType.DMA((3 * n,))
    res = pl.pallas_call(
        body, name=f"gather_start_{layer}",
        in_specs=[HBM_SPEC] * n + [_any()],
        out_specs=[SEM_SPEC, SEM_SPEC] + [HBM_SPEC] * n + [pl.BlockSpec(memory_space=pltpu.VMEM)],
        out_shape=[sems, sems] + [pltpu.HBM(b.shape, b.dtype) for b in bufs] + [jax.ShapeDtypeStruct((8, LANES), F32)],
        input_output_aliases={k: 2 + k for k in range(n)}, compiler_params=IN_FLIGHT,
    )(*[_in_hbm(b) for b in bufs], after)
    return res[0], res[1], res[2:2 + n], res[-1]


def gather_wait(send, recv, bufs, after, layer):
    n = len(bufs)

    def body(*refs):
        ins = refs[:n]
        send_ref, recv_ref = refs[n], refs[n + 1]
        for start, arrival in _gather_copies(ins, send_ref, recv_ref):
            start.wait_send()
            arrival.wait_recv()

    return pl.pallas_call(
        body, name=f"gather_wait_{layer}",
        in_specs=[HBM_SPEC] * n + [SEM_SPEC, SEM_SPEC, _any()], out_specs=[HBM_SPEC] * n,
        out_shape=[pltpu.HBM(b.shape, b.dtype) for b in bufs],
        input_output_aliases={k: k for k in range(n)}, compiler_params=IN_FLIGHT,
    )(*bufs, send, recv, after)


def gather_forward(bufs):
    n = len(bufs)

    def body(*refs):
        outs = refs[n:2 * n]
        send, recv = refs[2 * n:]
        x, y, c, chips = _place()
        cps = []
        for k in range(n):
            rows = outs[k].shape[1]
            for j, (cx, cy) in enumerate(chips):
                sems = dict(send_sem=send.at[3 * k + j], recv_sem=recv.at[3 * k + j],
                            device_id=(x, y, 1 - c), device_id_type=MESH)
                mine = outs[k].at[2 * cx + cy, _half(rows, c), :]
                theirs = outs[k].at[2 * cx + cy, _half(rows, 1 - c), :]
                cp = pltpu.make_async_remote_copy(src_ref=mine, dst_ref=mine, **sems)
                cp.start()
                cps.append((cp, pltpu.make_async_remote_copy(src_ref=theirs, dst_ref=theirs, **sems)))
        for cp, arrival in cps:
            cp.wait_send()
            arrival.wait_recv()

    return pl.pallas_call(
        body, in_specs=[_any()] * n, out_specs=[_any()] * n,
        out_shape=[jax.ShapeDtypeStruct(b.shape, b.dtype) for b in bufs], input_output_aliases={k: k for k in range(n)},
        scratch_shapes=[pltpu.SemaphoreType.DMA((3 * n,)), pltpu.SemaphoreType.DMA((3 * n,))],
        name="gather_forward")(*bufs)


def _exchange_copies(srcs, lands, send, recv):
    x, y, c, _ = _place()
    return [pltpu.make_async_remote_copy(
        src_ref=src.at[:, _half(src.shape[1], 1 - c), :], dst_ref=land, send_sem=send.at[k], recv_sem=recv.at[k],
        device_id=(x, y, 1 - c), device_id_type=MESH) for k, (src, land) in enumerate(zip(srcs, lands))]


def exchange_start(srcs, tag):
    n = len(srcs)
    lands = [lax.empty((s.shape[0], s.shape[1] // 2, s.shape[2]), s.dtype) for s in srcs]

    def body(*refs):
        ins, land_refs = refs[:n], refs[n:2 * n]
        send, recv = refs[2 * n], refs[2 * n + 1]
        token = refs[-1]
        for cp in _exchange_copies(ins, land_refs, send, recv):
            cp.start()
        token[...] = jnp.zeros_like(token)

    sems = pltpu.SemaphoreType.DMA((n,))
    res = pl.pallas_call(
        body, name=f"exchange_start_{tag}",
        in_specs=[HBM_SPEC] * (2 * n),
        out_specs=[SEM_SPEC, SEM_SPEC] + [HBM_SPEC] * (2 * n) + [pl.BlockSpec(memory_space=pltpu.VMEM)],
        out_shape=[sems, sems] + [pltpu.HBM(a.shape, a.dtype) for a in list(srcs) + lands]
        + [jax.ShapeDtypeStruct((8, LANES), F32)],
        input_output_aliases={k: 2 + k for k in range(2 * n)}, compiler_params=IN_FLIGHT,
    )(*[_in_hbm(a) for a in list(srcs) + lands])
    return res[0], res[1], res[2:2 + n], res[2 + n:2 + 2 * n], res[-1]


def exchange_wait(send, recv, srcs, lands, after, tag):
    n = len(srcs)

    def body(*refs):
        ins, land_refs = refs[:n], refs[n:2 * n]
        send_ref, recv_ref = refs[2 * n], refs[2 * n + 1]
        for cp in _exchange_copies(ins, land_refs, send_ref, recv_ref):
            cp.wait_send()
            cp.wait_recv()

    res = pl.pallas_call(
        body, name=f"exchange_wait_{tag}",
        in_specs=[HBM_SPEC] * (2 * n) + [SEM_SPEC, SEM_SPEC, _any()], out_specs=[HBM_SPEC] * (2 * n),
        out_shape=[pltpu.HBM(a.shape, a.dtype) for a in list(srcs) + list(lands)],
        input_output_aliases={k: k for k in range(2 * n)}, compiler_params=IN_FLIGHT,
    )(*srcs, *lands, send, recv, after)
    return res[:n], res[n:]


def add_pair(g, r1, core):
    ns, rows, cols = g.shape
    hr = rows // 2

    def body(c_ref, g_ref, r_ref, o_ref):
        del c_ref
        o_ref[...] = (g_ref[...] + r_ref[...]).astype(BF16)

    blk = (None, hr, cols)
    grid_spec = pltpu.PrefetchScalarGridSpec(
        num_scalar_prefetch=1, grid=(ns,),
        in_specs=[pl.BlockSpec(blk, lambda s, c: (s, c[0], 0)), pl.BlockSpec(blk, lambda s, c: (s, 0, 0))],
        out_specs=pl.BlockSpec(blk, lambda s, c: (s, 0, 0)))
    return pl.pallas_call(body, grid_spec=grid_spec, out_shape=jax.ShapeDtypeStruct(r1.shape, BF16),
                          compiler_params=_cp(("arbitrary",)), name="add_pair")(core, g, r1)


def _scatter_copies(srcs, lands, send, recv):
    _, _, c, chips = _place()
    out = []
    for k, (src, land) in enumerate(zip(srcs, lands)):
        for j, (cx, cy) in enumerate(chips):
            out.append(pltpu.make_async_remote_copy(
                src_ref=src.at[2 * cx + cy], dst_ref=land.at[j], send_sem=send.at[3 * k + j],
                recv_sem=recv.at[3 * k + j], device_id=(cx, cy, c), device_id_type=MESH))
    return out


def scatter_start(srcs, layer):
    n = len(srcs)
    lands = [lax.empty((3,) + s.shape[1:], s.dtype) for s in srcs]

    def body(*refs):
        ins, land_refs = refs[:n], refs[n:2 * n]
        send, recv = refs[2 * n], refs[2 * n + 1]
        token = refs[-1]
        for cp in _scatter_copies(ins, land_refs, send, recv):
            cp.start()
        token[...] = jnp.zeros_like(token)

    sems = pltpu.SemaphoreType.DMA((3 * n,))
    res = pl.pallas_call(
        body, name=f"scatter_start_{layer}",
        in_specs=[HBM_SPEC] * (2 * n),
        out_specs=[SEM_SPEC, SEM_SPEC] + [HBM_SPEC] * (2 * n) + [pl.BlockSpec(memory_space=pltpu.VMEM)],
        out_shape=[sems, sems] + [pltpu.HBM(a.shape, a.dtype) for a in srcs + lands]
        + [jax.ShapeDtypeStruct((8, LANES), F32)],
        input_output_aliases={k: 2 + k for k in range(2 * n)}, compiler_params=IN_FLIGHT,
    )(*[_in_hbm(a) for a in srcs + lands])
    return res[0], res[1], res[2:2 + n], res[2 + n:2 + 2 * n], res[-1]


def scatter_wait(send, recv, srcs, lands, after, layer):
    n = len(srcs)

    def body(*refs):
        ins, land_refs = refs[:n], refs[n:2 * n]
        send_ref, recv_ref = refs[2 * n], refs[2 * n + 1]
        for cp in _scatter_copies(ins, land_refs, send_ref, recv_ref):
            cp.wait_send()
            cp.wait_recv()

    res = pl.pallas_call(
        body, name=f"scatter_wait_{layer}",
        in_specs=[HBM_SPEC] * (2 * n) + [SEM_SPEC, SEM_SPEC, _any()], out_specs=[HBM_SPEC] * (2 * n),
        out_shape=[pltpu.HBM(a.shape, a.dtype) for a in list(srcs) + list(lands)],
        input_output_aliases={k: k for k in range(2 * n)}, compiler_params=IN_FLIGHT,
    )(*srcs, *lands, send, recv, after)
    return res[n:]


def add_chips(g, r1, r2, place, total, layer):
    _, rows, cols = g.shape
    hr = rows // 2

    def body(p_ref, g_ref, r1_ref, r2_ref, t_hbm, o_ref):
        del p_ref, t_hbm
        own = g_ref[...] + r1_ref[...]
        o_ref[...] = ((own + r2_ref[0].astype(F32)) + r2_ref[1].astype(F32)) + r2_ref[2].astype(F32)

    grid_spec = pltpu.PrefetchScalarGridSpec(
        num_scalar_prefetch=1, grid=(1,),
        in_specs=[pl.BlockSpec((None, hr, cols), lambda i, p: (p[1], p[0], 0)),
                  pl.BlockSpec((None, hr, cols), lambda i, p: (p[1], 0, 0)),
                  pl.BlockSpec((3, hr, cols), lambda i, p: (0, 0, 0)), _any()],
        out_specs=pl.BlockSpec((None, hr, cols), lambda i, p: (layer, p[0], 0)))
    return pl.pallas_call(body, grid_spec=grid_spec, out_shape=jax.ShapeDtypeStruct(total.shape, F32),
                          input_output_aliases={4: 0}, compiler_params=_cp(("arbitrary",)),
                          name="add_chips")(place, g, r1, r2, total)


def pair_share(gs):
    n = len(gs)

    def body(*refs):
        outs = refs[n:2 * n]
        send, recv = refs[2 * n:]
        x, y, c, _ = _place()
        cps = []
        for k in range(n):
            mine = outs[k].at[:, _half(outs[k].shape[1], c), :]
            cp = pltpu.make_async_remote_copy(
                src_ref=mine, dst_ref=mine, send_sem=send.at[k], recv_sem=recv.at[k],
                device_id=(x, y, 1 - c), device_id_type=MESH)
            cp.start()
            cps.append(cp)
        for k, cp in enumerate(cps):
            cp.wait_send()
            theirs = outs[k].at[:, _half(outs[k].shape[1], 1 - c), :]
            pltpu.make_async_remote_copy(
                src_ref=theirs, dst_ref=theirs, send_sem=send.at[k], recv_sem=recv.at[k],
                device_id=(x, y, 1 - c), device_id_type=MESH).wait_recv()

    return pl.pallas_call(
        body, in_specs=[_any()] * n, out_specs=[_any()] * n,
        out_shape=[jax.ShapeDtypeStruct(g.shape, g.dtype) for g in gs], input_output_aliases={k: k for k in range(n)},
        scratch_shapes=[pltpu.SemaphoreType.DMA((n,)), pltpu.SemaphoreType.DMA((n,))],
        name="pair_share")(*gs)


def small_collect(v, reduce, name):
    rows = v.shape[0]
    flips = [(fx, fy, fc) for fx in (0, 1) for fy in (0, 1) for fc in (0, 1)][1:]

    def body(v_ref, o_ref, buf, send, recv):
        x, y, c, _ = _place()
        buf[4 * x + 2 * y + c] = v_ref[...]
        peers = [(jnp.where(fx, 1 - x, x), jnp.where(fy, 1 - y, y), jnp.where(fc, 1 - c, c)) for fx, fy, fc in flips]
        cps = []
        for k, peer in enumerate(peers):
            cp = pltpu.make_async_remote_copy(
                src_ref=v_ref, dst_ref=buf.at[4 * x + 2 * y + c], send_sem=send.at[k], recv_sem=recv.at[k],
                device_id=peer, device_id_type=MESH)
            cp.start()
            cps.append(cp)
        for k, (px, py, pc) in enumerate(peers):
            pltpu.make_async_remote_copy(
                src_ref=v_ref, dst_ref=buf.at[4 * px + 2 * py + pc], send_sem=send.at[k], recv_sem=recv.at[k],
                device_id=(px, py, pc), device_id_type=MESH).wait_recv()
        for cp in cps:
            cp.wait_send()
        if reduce:
            acc = buf[0]
            for s in range(1, 8):
                acc = acc + buf[s]
            o_ref[...] = acc
        else:
            o_ref[...] = buf[...]

    vm = pl.BlockSpec(memory_space=pltpu.VMEM)
    out_shape = jax.ShapeDtypeStruct((rows, SMALL_COLS) if reduce else (8, rows, SMALL_COLS), F32)
    return pl.pallas_call(
        body, in_specs=[vm], out_specs=vm, out_shape=out_shape,
        scratch_shapes=[pltpu.VMEM((8, rows, SMALL_COLS), F32), pltpu.SemaphoreType.DMA((7,)),
                        pltpu.SemaphoreType.DMA((7,))],
        name=name)(v)


def adamw(w, g, m, v, rb, name):
    nl, rows, cols = w.shape

    def body(w_ref, g_ref, m_ref, v_ref, go_ref, d_ref, nm_ref, nv_ref):
        gv = g_ref[...]
        go_ref[...] = gv
        nm = ADAM_B1 * m_ref[...] + (1.0 - ADAM_B1) * gv
        nv = ADAM_B2 * v_ref[...] + (1.0 - ADAM_B2) * (gv * gv)
        m_hat = nm / (1.0 - ADAM_B1 ** ADAM_STEP)
        v_hat = nv / (1.0 - ADAM_B2 ** ADAM_STEP)
        d_ref[...] = -ADAM_LR * (m_hat / (jnp.sqrt(v_hat) + ADAM_EPS) + ADAM_WD * w_ref[...])
        nm_ref[...] = nm
        nv_ref[...] = nv

    blk = pl.BlockSpec((None, rb, cols), lambda l, r: (l, r, 0))
    shp = jax.ShapeDtypeStruct(w.shape, F32)
    return pl.pallas_call(body, grid=(nl, rows // rb), in_specs=[blk] * 4, out_specs=[blk] * 4, out_shape=[shp] * 4,
                          compiler_params=_cp(("arbitrary", "arbitrary")), name=name)(w, g, m, v)


def _pack(parts, rows):
    flat = jnp.concatenate([p.reshape(-1).astype(F32) for p in parts])
    return jnp.pad(flat, (0, rows * SMALL_COLS - flat.shape[0])).reshape(rows, SMALL_COLS)


def _unpack(vec, shapes):
    flat = vec.reshape(-1)
    out, off = [], 0
    for s in shapes:
        size = 1
        for d in s:
            size *= d
        out.append(flat[off:off + size].reshape(s))
        off += size
    return out


def kernel(x, w_in, w_conv, rel_bias, g_conv_out, g_attn_out, w_out, g_pre_mix, g_post_mix, g_pre_ffn, g_post_ffn, w_ffn_in, w_ffn_out, loss_target, m_w_in, m_w_conv, m_rel_bias, m_g_conv_out, m_g_attn_out, m_w_out, m_g_pre_mix, m_g_post_mix, m_g_pre_ffn, m_g_post_ffn, m_w_ffn_in, m_w_ffn_out, v_w_in, v_w_conv, v_rel_bias, v_g_conv_out, v_g_attn_out, v_w_out, v_g_pre_mix, v_g_post_mix, v_g_pre_ffn, v_g_post_ffn, v_w_ffn_in, v_w_ffn_out):
    xi, yi, ci = lax.axis_index("x"), lax.axis_index("y"), lax.axis_index("c")
    chip = 2 * xi + yi
    nl = w_in.shape[0]
    x0 = x[0]
    target = loss_target[0]
    cwl = CW // NCHIP

    chip1 = chip.reshape(1).astype(jnp.int32)
    own = [[cast_to_slot(w, chip1, l) for w in (w_in, w_out, w_ffn_in, w_ffn_out)] for l in range(nl)]
    wc_all = small_collect(_pack([w_conv], 8), False, "gather_w_conv")
    wc_full = wc_all[0::2].reshape(NCHIP, -1)[:, :nl * cwl * 3].reshape(NCHIP, nl, cwl, 3)
    wc_full = jnp.transpose(wc_full, (1, 0, 2, 3)).reshape(nl, CW, 3)
    wconv_t = jnp.pad(jnp.transpose(wc_full, (0, 2, 1)), ((0, 0), (0, 5), (0, 0)))
    gm = jnp.kron(jnp.eye(CW // HD, dtype=F32), jnp.full((HD, HD), 1.0 / HD, F32)).astype(BF16)
    row = lambda a, l: a[l][None, :]

    def token(t):
        return t[0:1, 0:1]

    def gather_finish(flight, after, tag):
        send, recv, bufs, _ = flight
        return gather_forward(gather_wait(send, recv, bufs, after, tag))

    first_mix = gather_start(own[0][:2], x0, "0m")
    first_ffn = gather_start(own[0][2:], first_mix[3], "0f")
    flight = None
    saved, weights = [], []
    h = x0
    for l in range(nl):
        if l == 0:
            gw_in, gw_out = gather_finish(first_mix, x0, "0m")
        else:
            gw_in, gw_out, gw_fi, gw_fo = gather_finish(flight, h, l)
        gw_out = gw_out.reshape(D, D)
        g_pm = row(g_pre_mix, l)
        if l + 1 < nl:
            flight = gather_start(own[l + 1], gw_in, l + 1)
            g_pm = g_pm + token(flight[3])
        bias2 = bias_expand(_diag_vector(rel_bias[l]))
        proj = fwd_inproj(h, g_pm, gw_in)
        xmid, o, lse, y, z = fwd_mix(h, proj, bias2, wconv_t[l], row(g_conv_out, l), row(g_attn_out, l),
                                     row(g_post_mix, l), gm, gw_out)
        if l == 0:
            gw_fi, gw_fo = gather_finish(first_ffn, xmid, "0f")
        gw_fo = gw_fo.reshape(2, DFF // 2, D)
        gu, f, xout = fwd_ffn(xmid, row(g_pre_ffn, l), row(g_post_ffn, l), gw_fi, gw_fo)
        saved.append((h, proj, bias2, xmid, o, lse, y, z, gu, f))
        weights.append((gw_in, gw_out, gw_fi, gw_fo))
        h = xout
    dx, loss_blk = loss_head(h, target)

    core = ci.reshape(1).astype(jnp.int32)
    place = jnp.stack([ci, chip]).astype(jnp.int32)
    totals = [lax.empty(w.shape, F32) for w in (w_in, w_out, w_ffn_in, w_ffn_out)]
    small = {k: [None] * nl for k in ("co", "ao", "pm", "qm", "pf", "qf", "rel", "wc")}

    def reduce_begin(kinds, grads, tag):
        return kinds, exchange_start(grads, tag), tag

    def reduce_mid(state, after):
        kinds, (send, recv, srcs, lands, _), tag = state
        grads, from_sibling = exchange_wait(send, recv, srcs, lands, after, tag)
        pair_sums = [add_pair(g, r, core) for g, r in zip(grads, from_sibling)]
        return kinds, grads, from_sibling, scatter_start(pair_sums, tag), tag

    def reduce_end(state, after, totals, layer):
        kinds, grads, from_sibling, (send, recv, srcs, lands, _), tag = state
        from_chips = scatter_wait(send, recv, srcs, lands, after, tag)
        totals = list(totals)
        for i, g, r1, r2 in zip(kinds, grads, from_sibling, from_chips):
            totals[i] = add_chips(g, r1, r2, place, totals[i], layer)
        return totals

    begun = flying = None
    for l in reversed(range(nl)):
        hin, proj, bias2, xmid, o, lse, y, z, gu, f = saved[l]
        gw_in, gw_out, gw_fi, gw_fo = weights[l]
        g_qf, g_qm, wct = row(g_post_ffn, l), row(g_post_mix, l), wconv_t[l]
        if begun is not None:
            g_qf = g_qf + token(begun[1][4])
        dxm, dfb, act, dgu, h2, dg_qf, dg_pf = bwd_ffn(dx, f, xmid, gu, row(g_pre_ffn, l), g_qf, gw_fi, gw_fo)
        if begun is not None:
            flying = reduce_mid(begun, dxm)
            g_qm = g_qm + token(flying[3][4])
        gr_fo = wgrad(act, dfb, 256, D, False, "wgrad_ffn_out").reshape(NCHIP, DFF // NCHIP, D)
        gr_fi = wgrad(h2, dgu, 512, 2 * DFF // NCHIP, True, "wgrad_ffn_in")
        if l == 0:
            begun_ffn = reduce_begin([2, 3], [gr_fi, gr_fo], "0f")
            g_qm = g_qm + token(begun_ffn[1][4])
        dzb, do, dco, dbg, dg_qm, dg_co, dg_ao = bwd_mix(dxm, z, o, proj, wct, row(g_conv_out, l),
                                                          row(g_attn_out, l), g_qm, gm, gw_out)
        if l == 0:
            flying_ffn = reduce_mid(begun_ffn, dzb)
            wct = wct + token(flying_ffn[3][4])
        gr_out = wgrad(y, dzb, 512, D, False, "wgrad_out").reshape(NCHIP, D // NCHIP, D)
        dhc, dcg, dwc = bwd_conv(dco, proj, wct)
        dq, dk, dv, db2 = bwd_attn(proj, o, do, lse, bias2)
        dx, dproj, hb, dg_pm = bwd_inproj(dxm, hin, dhc, dbg, dcg, dq, dk, dv, row(g_pre_mix, l), gw_in)
        if flying is not None:
            totals = reduce_end(flying, dx, totals, l + 1)
        gr_in = wgrad(hb, dproj, 512, PROJ // NCHIP, True, "wgrad_in")
        small["co"][l], small["ao"][l], small["pm"][l], small["qm"][l] = dg_co, dg_ao, dg_pm, dg_qm
        small["pf"][l], small["qf"][l] = dg_pf, dg_qf
        small["rel"][l] = _diag_vector_bwd(bias_reduce(db2))
        small["wc"][l] = jnp.transpose(dwc[0:3], (1, 0))
        if l > 0:
            begun = reduce_begin([0, 1, 2, 3], [gr_in, gr_out, gr_fi, gr_fo], l)
    flying_mix = reduce_mid(reduce_begin([0, 1], [gr_in, gr_out], "0m"), dx)
    totals = reduce_end(flying_ffn, flying_mix[3][4], totals, 0)
    totals = reduce_end(flying_mix, totals[2], totals, 0)
    gr_in, gr_out, gr_fi, gr_fo = pair_share(totals)

    order = ("co", "ao", "pm", "qm", "pf", "qf", "rel", "wc")
    parts = [jnp.stack(small[k]) for k in order] + [loss_blk[0:1, 0:1]]
    shapes = [p.shape for p in parts]
    red = _unpack(small_collect(_pack(parts, 40), True, "reduce_small"), shapes)
    gr_co, gr_ao, gr_pm, gr_qm, gr_pf, gr_qf, gr_rel, gr_wc_full, loss = red
    gr_co, gr_ao, gr_pm, gr_qm, gr_pf, gr_qf = [a.reshape(nl, -1) for a in (gr_co, gr_ao, gr_pm, gr_qm, gr_pf, gr_qf)]
    gr_wc = lax.dynamic_slice_in_dim(gr_wc_full, chip * cwl, cwl, axis=1)
    loss = loss.reshape(())

    big = []
    for w, g, m, v, name in ((w_in, gr_in, m_w_in, v_w_in, "adamw_in"), (w_out, gr_out, m_w_out, v_w_out, "adamw_out"),
                             (w_ffn_in, gr_fi, m_w_ffn_in, v_w_ffn_in, "adamw_ffn_in"),
                             (w_ffn_out, gr_fo, m_w_ffn_out, v_w_ffn_out, "adamw_ffn_out")):
        big.append(adamw(w, g, m, v, w.shape[1] // 4, name))
    sw = [g_conv_out, g_attn_out, g_pre_mix, g_post_mix, g_pre_ffn, g_post_ffn, rel_bias, w_conv]
    sg = [gr_co, gr_ao, gr_pm, gr_qm, gr_pf, gr_qf, gr_rel, gr_wc]
    sm = [m_g_conv_out, m_g_attn_out, m_g_pre_mix, m_g_post_mix, m_g_pre_ffn, m_g_post_ffn, m_rel_bias, m_w_conv]
    sv = [v_g_conv_out, v_g_attn_out, v_g_pre_mix, v_g_post_mix, v_g_pre_ffn, v_g_post_ffn, v_rel_bias, v_w_conv]
    sshapes = [a.shape for a in sw]
    packed = [_pack(a, 32)[None] for a in (sw, sg, sm, sv)]
    s_out = [_unpack(a[0], sshapes) for a in adamw(*packed, 32, "adamw_small")]

    def leaves(big_i, small_i):
        b_in, b_out, b_fi, b_fo = big_i
        s_co, s_ao, s_pm, s_qm, s_pf, s_qf, s_rel, s_wc = small_i
        return [b_in, s_wc, s_rel, s_co, s_ao, b_out, s_pm, s_qm, s_pf, s_qf, b_fi, b_fo]

    out = [loss, dx[None]]
    out += leaves([b[0] for b in big], sg)
    for i in range(1, 4):
        out += leaves([b[i] for b in big], s_out[i])
    return tuple(out)
```

```python
import functools

import jax
import jax.numpy as jnp
from jax import lax
from jax.experimental import pallas as pl
from jax.experimental.pallas import tpu as pltpu

F32 = jnp.float32
BF16 = jnp.bfloat16

D = 1024
PROJ = 3072
CW = 512
HD = 64
NH = 8
CHUNK = 64
BAND = 576
REL_CLIP = 128
NREL = 2 * REL_CLIP + 1
DFF = 2816
DEPTH = 4
NCHIP = 4
EPS = 1e-6
NEG_INF = -1e30

ADAM_LR = 0.001
ADAM_B1 = 0.9
ADAM_B2 = 0.999
ADAM_EPS = 1e-08
ADAM_WD = 0.01
ADAM_STEP = 10

V7X_VMEM_BYTES = 64 * 1024 * 1024
VMEM_LIMIT = V7X_VMEM_BYTES - 8 * 1024 * 1024
LANES = 128
QG = 2 * CHUNK
KG = QG + BAND - CHUNK
TQ = 512
TM = 256
SMALL_COLS = 1024
MESH = pl.DeviceIdType.MESH
NT = (((1,), (1,)), ((), ()))
TN = (((0,), (0,)), ((), ()))


def _cp(sem=None, vmem=VMEM_LIMIT):
    return pltpu.CompilerParams(dimension_semantics=sem, vmem_limit_bytes=vmem)


def _any():
    return pl.BlockSpec(memory_space=pl.ANY)


def _const(shape):
    nd = len(shape)
    return pl.BlockSpec(shape, lambda *_: (0,) * nd)


def _rms(v, g):
    r = lax.rsqrt(jnp.mean(v * v, axis=-1, keepdims=True) + EPS)
    return v * r * g


def _rms_bwd(dy, v, g):
    r = lax.rsqrt(jnp.mean(v * v, axis=-1, keepdims=True) + EPS)
    vh = v * r
    dg = jnp.sum(dy * vh, axis=0, keepdims=True)
    dvh = dy * g
    dv = r * (dvh - vh * jnp.mean(dvh * vh, axis=-1, keepdims=True))
    return dv, dg


def _group_mean(v, gm):
    hi = v.astype(BF16)
    lo = (v - hi.astype(F32)).astype(BF16)
    return jnp.dot(hi, gm, preferred_element_type=F32) + jnp.dot(lo, gm, preferred_element_type=F32)


def _group_rms_bwd(dy, v, g, gm):
    r = lax.rsqrt(_group_mean(v * v, gm) + EPS)
    vh = v * r
    dg = jnp.sum(dy * vh, axis=0, keepdims=True)
    dvh = dy * g
    dv = r * (dvh - vh * _group_mean(dvh * vh, gm))
    return dv, dg


def _head_masks(scale):
    lane = lax.broadcasted_iota(jnp.int32, (1, LANES), 1)
    return [jnp.where((lane >= HD * a) & (lane < HD * (a + 1)), scale, 0.0).astype(BF16) for a in range(2)]


def _conv_taps(u_prev, u, scr):
    n = u.shape[0]
    scr[0:16, :] = u_prev
    scr[16:16 + n, :] = u
    return scr[15:15 + n, :], scr[14:14 + n, :]


def fwd_inproj(x, g, w_all):
    t = x.shape[0]
    wc = PROJ // NCHIP

    def body(x_ref, g_ref, w_hbm, o_ref, w_v):
        @pl.when(pl.program_id(0) == 0)
        def _():
            pltpu.sync_copy(w_hbm, w_v)

        h = _rms(x_ref[...], g_ref[...]).astype(BF16)
        for b in range(NCHIP):
            o_ref[:, wc * b:wc * (b + 1)] = jnp.dot(h, w_v[b], preferred_element_type=F32).astype(BF16)

    return pl.pallas_call(
        body, grid=(t // TQ,),
        in_specs=[pl.BlockSpec((TQ, D), lambda i: (i, 0)), _const((1, D)), _any()],
        out_specs=pl.BlockSpec((TQ, PROJ), lambda i: (i, 0)),
        out_shape=jax.ShapeDtypeStruct((t, PROJ), BF16),
        scratch_shapes=[pltpu.VMEM((NCHIP, D, wc), BF16)],
        compiler_params=_cp(("arbitrary",)), name="fwd_inproj")(x, g, w_all)


def _attn_window_specs():
    return [
        pl.BlockSpec((TQ, CW), lambda i: (i, 3)),
        pl.BlockSpec((TQ, CW), lambda i: (jnp.maximum(i - 1, 0), 4)),
        pl.BlockSpec((TQ, CW), lambda i: (i, 4)),
        pl.BlockSpec((TQ, CW), lambda i: (jnp.maximum(i - 1, 0), 5)),
        pl.BlockSpec((TQ, CW), lambda i: (i, 5)),
    ]


def _conv_specs():
    return [
        pl.BlockSpec((TQ, 3 * CW), lambda i: (i, 0)),
        pl.BlockSpec((16, 3 * CW), lambda i: (jnp.maximum(i * (TQ // 16) - 1, 0), 0)),
    ]


def _conv_fwd(pc_ref, pcp_ref, wc_ref, scr, first):
    pc = pc_ref[...].astype(F32)
    hc, bg, cg = pc[:, :CW], pc[:, CW:2 * CW], pc[:, 2 * CW:]
    u = cg * hc
    pp = pcp_ref[...].astype(F32)
    u_prev = jnp.where(first, 0.0, pp[:, 2 * CW:] * pp[:, :CW])
    u1, u2 = _conv_taps(u_prev, u, scr)
    cout = wc_ref[0:1, :] * u2 + wc_ref[1:2, :] * u1 + wc_ref[2:3, :] * u
    return hc, bg, cg, u, u1, u2, cout


def _key_penalty(first, r0):
    col = lax.broadcasted_iota(jnp.int32, (1, KG), 1)
    limit = jnp.where(first, TQ - r0, 0)
    return jnp.where(col < limit, NEG_INF, 0.0)


def fwd_mix(x, proj, bias2, wconv_t, g_co, g_ao, g_pm, gm, wout_all):
    t = x.shape[0]

    def body(x_ref, pc_ref, pcp_ref, q_ref, kp_ref, kc_ref, vp_ref, vc_ref, b2_ref, wc_ref, gco_ref, gao_ref, gpm_ref,
             gm_ref, wout_hbm, xmid_ref, o_ref, lse_ref, y_ref, z_ref, wout_v, kwin, vwin, cscr):
        i = pl.program_id(0)
        first = i == 0

        @pl.when(first)
        def _():
            pltpu.sync_copy(wout_hbm, wout_v)

        kwin[0:TQ, :] = kp_ref[...]
        kwin[TQ:2 * TQ, :] = kc_ref[...]
        vwin[0:TQ, :] = vp_ref[...]
        vwin[TQ:2 * TQ, :] = vc_ref[...]
        qmask = _head_masks(HD ** -0.5)
        vmask = _head_masks(1.0)

        def group(g, carry):
            r0 = pl.multiple_of(g * QG, QG)
            pen = _key_penalty(first, r0)
            for hp in range(NH // 2):
                ls = slice(LANES * hp, LANES * (hp + 1))
                qb = q_ref[pl.ds(r0, QG), ls]
                kw = kwin[pl.ds(r0, KG), ls]
                vw = vwin[pl.ds(r0, KG), ls]
                o_acc = jnp.zeros((QG, LANES), F32)
                lse = jnp.zeros((QG, LANES), F32)
                for a in range(2):
                    s = lax.dot_general(qb * qmask[a], kw, NT, preferred_element_type=F32)
                    s = s + b2_ref[2 * hp + a] + pen
                    m = jnp.max(s, axis=-1, keepdims=True)
                    p = jnp.exp(s - m)
                    l = jnp.sum(p, axis=-1, keepdims=True)
                    o = jnp.dot(p.astype(BF16), vw * vmask[a], preferred_element_type=F32)
                    o_acc = o_acc + o * (1.0 / l)
                    lse = lse + (m + jnp.log(l)) * vmask[a].astype(F32)
                o_ref[pl.ds(r0, QG), ls] = o_acc
                lse_ref[pl.ds(r0, QG), ls] = lse
            return carry

        lax.fori_loop(0, TQ // QG, group, 0)

        _, bg, _, _, _, _, cout = _conv_fwd(pc_ref, pcp_ref, wc_ref, cscr, first)
        yc = bg * cout
        gmv = gm_ref[...]
        ycn = yc * lax.rsqrt(_group_mean(yc * yc, gmv) + EPS) * gco_ref[...]
        oa = o_ref[...]
        oan = oa * lax.rsqrt(_group_mean(oa * oa, gmv) + EPS) * gao_ref[...]
        y_ref[:, 0:CW] = ycn.astype(BF16)
        y_ref[:, CW:2 * CW] = oan.astype(BF16)
        z = jnp.dot(y_ref[...], wout_v[...], preferred_element_type=F32)
        z_ref[...] = z
        xmid_ref[...] = x_ref[...] + _rms(z, gpm_ref[...])

    row = lambda w: pl.BlockSpec((TQ, w), lambda i: (i, 0))
    return pl.pallas_call(
        body, grid=(t // TQ,),
        in_specs=[row(D)] + _conv_specs() + _attn_window_specs() + [
            _const((NH, QG, KG)), _const((8, CW)), _const((1, CW)), _const((1, CW)), _const((1, D)),
            _const((CW, CW)), _any()],
        out_specs=[row(D), row(CW), row(CW), row(D), row(D)],
        out_shape=[jax.ShapeDtypeStruct((t, D), F32), jax.ShapeDtypeStruct((t, CW), F32),
                   jax.ShapeDtypeStruct((t, CW), F32), jax.ShapeDtypeStruct((t, D), BF16),
                   jax.ShapeDtypeStruct((t, D), F32)],
        scratch_shapes=[pltpu.VMEM((D, D), BF16), pltpu.VMEM((2 * TQ, CW), BF16), pltpu.VMEM((2 * TQ, CW), BF16),
                        pltpu.VMEM((TQ + 16, CW), F32)],
        compiler_params=_cp(("arbitrary",)), name="fwd_mix",
    )(x, proj, proj, proj, proj, proj, proj, proj, bias2, wconv_t, g_co, g_ao, g_pm, gm, wout_all)


def fwd_ffn(xmid, g_pre, g_post, wfi_all, wfo_all):
    t = xmid.shape[0]
    hw = DFF // 2

    def body(x_ref, gpre_ref, gpost_ref, wfi_hbm, wfo_hbm, gu_ref, f_ref, xo_ref, wfi_v, wfo_v):
        @pl.when(pl.program_id(0) == 0)
        def _():
            pltpu.sync_copy(wfi_hbm, wfi_v)
            pltpu.sync_copy(wfo_hbm, wfo_v)

        xv = x_ref[...]
        h = _rms(xv, gpre_ref[...]).astype(BF16)
        f = jnp.zeros((TM, D), F32)
        for j in range(2):
            gate = jnp.dot(h, wfi_v[j], preferred_element_type=F32)
            up = jnp.dot(h, wfi_v[2 + j], preferred_element_type=F32)
            gu_ref[:, hw * j:hw * (j + 1)] = gate.astype(BF16)
            gu_ref[:, DFF + hw * j:DFF + hw * (j + 1)] = up.astype(BF16)
            act = gate * (1.0 / (1.0 + jnp.exp(-gate))) * up
            f = f + jnp.dot(act.astype(BF16), wfo_v[j], preferred_element_type=F32)
        f_ref[...] = f
        xo_ref[...] = xv + _rms(f, gpost_ref[...])

    row = lambda w: pl.BlockSpec((TM, w), lambda i: (i, 0))
    return pl.pallas_call(
        body, grid=(t // TM,),
        in_specs=[row(D), _const((1, D)), _const((1, D)), _any(), _any()],
        out_specs=[row(2 * DFF), row(D), row(D)],
        out_shape=[jax.ShapeDtypeStruct((t, 2 * DFF), BF16), jax.ShapeDtypeStruct((t, D), F32),
                   jax.ShapeDtypeStruct((t, D), F32)],
        scratch_shapes=[pltpu.VMEM((NCHIP, D, hw), BF16), pltpu.VMEM((2, hw, D), BF16)],
        compiler_params=_cp(("arbitrary",)), name="fwd_ffn")(xmid, g_pre, g_post, wfi_all, wfo_all)


def loss_head(y, target):
    t = y.shape[0]

    def body(y_ref, t_ref, dy_ref, l_ref):
        @pl.when(pl.program_id(0) == 0)
        def _():
            l_ref[...] = jnp.zeros_like(l_ref)

        e = y_ref[...] - t_ref[...]
        dy_ref[...] = e * (1.0 / D)
        rows = jnp.sum(e * e, axis=-1, keepdims=True) * (1.0 / D)
        l_ref[...] += 0.5 * jnp.sum(rows, axis=0, keepdims=True)

    row = pl.BlockSpec((TQ, D), lambda i: (i, 0))
    return pl.pallas_call(
        body, grid=(t // TQ,), in_specs=[row, row], out_specs=[row, _const((8, LANES))],
        out_shape=[jax.ShapeDtypeStruct((t, D), F32), jax.ShapeDtypeStruct((8, LANES), F32)],
        compiler_params=_cp(("arbitrary",)), name="loss_head")(y, target)


def bwd_ffn(dx, f, xmid, gu, g_pre, g_post, wfi_all, wfo_all):
    t = dx.shape[0]
    hw = DFF // 2

    def body(dx_ref, f_ref, x_ref, gu_ref, gpre_ref, gpost_ref, wfi_hbm, wfo_hbm,
             dxm_ref, df_ref, act_ref, dgu_ref, h_ref, dgpost_ref, dgpre_ref, wfi_v, wfo_v):
        @pl.when(pl.program_id(0) == 0)
        def _():
            pltpu.sync_copy(wfi_hbm, wfi_v)
            pltpu.sync_copy(wfo_hbm, wfo_v)
            dgpost_ref[...] = jnp.zeros_like(dgpost_ref)
            dgpre_ref[...] = jnp.zeros_like(dgpre_ref)

        dxo = dx_ref[...]
        df, dgp = _rms_bwd(dxo, f_ref[...], gpost_ref[...])
        dgpost_ref[...] += dgp
        dfb = df.astype(BF16)
        df_ref[...] = dfb
        dh = jnp.zeros((TM, D), F32)
        for j in range(2):
            dact = lax.dot_general(dfb, wfo_v[j], NT, preferred_element_type=F32)
            gate = gu_ref[:, hw * j:hw * (j + 1)].astype(F32)
            up = gu_ref[:, DFF + hw * j:DFF + hw * (j + 1)].astype(F32)
            sig = 1.0 / (1.0 + jnp.exp(-gate))
            silu = gate * sig
            act_ref[:, hw * j:hw * (j + 1)] = (silu * up).astype(BF16)
            dup = (dact * silu).astype(BF16)
            dgate = (dact * up * (sig * (1.0 + gate * (1.0 - sig)))).astype(BF16)
            dgu_ref[:, hw * j:hw * (j + 1)] = dgate
            dgu_ref[:, DFF + hw * j:DFF + hw * (j + 1)] = dup
            dh = dh + lax.dot_general(dgate, wfi_v[j], NT, preferred_element_type=F32)
            dh = dh + lax.dot_general(dup, wfi_v[2 + j], NT, preferred_element_type=F32)
        xv = x_ref[...]
        gpre = gpre_ref[...]
        h_ref[...] = _rms(xv, gpre).astype(BF16)
        dxv, dgq = _rms_bwd(dh, xv, gpre)
        dgpre_ref[...] += dgq
        dxm_ref[...] = dxo + dxv

    row = lambda w: pl.BlockSpec((TM, w), lambda i: (i, 0))
    return pl.pallas_call(
        body, grid=(t // TM,),
        in_specs=[row(D), row(D), row(D), row(2 * DFF), _const((1, D)), _const((1, D)), _any(), _any()],
        out_specs=[row(D), row(D), row(DFF), row(2 * DFF), row(D), _const((1, D)), _const((1, D))],
        out_shape=[jax.ShapeDtypeStruct((t, D), F32), jax.ShapeDtypeStruct((t, D), BF16),
                   jax.ShapeDtypeStruct((t, DFF), BF16), jax.ShapeDtypeStruct((t, 2 * DFF), BF16),
                   jax.ShapeDtypeStruct((t, D), BF16), jax.ShapeDtypeStruct((1, D), F32),
                   jax.ShapeDtypeStruct((1, D), F32)],
        scratch_shapes=[pltpu.VMEM((NCHIP, D, hw), BF16), pltpu.VMEM((2, hw, D), BF16)],
        compiler_params=_cp(("arbitrary",)), name="bwd_ffn")(dx, f, xmid, gu, g_pre, g_post, wfi_all, wfo_all)


def bwd_mix(dxm, z, o, proj, wconv_t, g_co, g_ao, g_pm, gm, wout_all):
    t = dxm.shape[0]

    def body(dx_ref, z_ref, o_ref, pc_ref, pcp_ref, wc_ref, gco_ref, gao_ref, gpm_ref, gm_ref, wout_hbm,
             dz_ref, do_ref, dco_ref, dbg_ref, dgpm_ref, dgco_ref, dgao_ref, wout_v, cscr):
        first = pl.program_id(0) == 0

        @pl.when(first)
        def _():
            pltpu.sync_copy(wout_hbm, wout_v)
            dgpm_ref[...] = jnp.zeros_like(dgpm_ref)
            dgco_ref[...] = jnp.zeros_like(dgco_ref)
            dgao_ref[...] = jnp.zeros_like(dgao_ref)

        dz, dgp = _rms_bwd(dx_ref[...], z_ref[...], gpm_ref[...])
        dgpm_ref[...] += dgp
        dzb = dz.astype(BF16)
        dz_ref[...] = dzb
        dy = lax.dot_general(dzb, wout_v[...], NT, preferred_element_type=F32)
        gmv = gm_ref[...]
        _, bg, _, _, _, _, cout = _conv_fwd(pc_ref, pcp_ref, wc_ref, cscr, first)
        dyc, dgc = _group_rms_bwd(dy[:, :CW], bg * cout, gco_ref[...], gmv)
        dgco_ref[...] += dgc
        dbg_ref[...] = (dyc * cout).astype(BF16)
        dco_ref[...] = dyc * bg
        do, dga = _group_rms_bwd(dy[:, CW:], o_ref[...], gao_ref[...], gmv)
        dgao_ref[...] += dga
        do_ref[...] = do.astype(BF16)

    row = lambda w: pl.BlockSpec((TQ, w), lambda i: (i, 0))
    return pl.pallas_call(
        body, grid=(t // TQ,),
        in_specs=[row(D), row(D), row(CW)] + _conv_specs() + [
            _const((8, CW)), _const((1, CW)), _const((1, CW)), _const((1, D)), _const((CW, CW)), _any()],
        out_specs=[row(D), row(CW), row(CW), row(CW), _const((1, D)), _const((1, CW)), _const((1, CW))],
        out_shape=[jax.ShapeDtypeStruct((t, D), BF16), jax.ShapeDtypeStruct((t, CW), BF16),
                   jax.ShapeDtypeStruct((t, CW), F32), jax.ShapeDtypeStruct((t, CW), BF16),
                   jax.ShapeDtypeStruct((1, D), F32), jax.ShapeDtypeStruct((1, CW), F32),
                   jax.ShapeDtypeStruct((1, CW), F32)],
        scratch_shapes=[pltpu.VMEM((D, D), BF16), pltpu.VMEM((TQ + 16, CW), F32)],
        compiler_params=_cp(("arbitrary",)), name="bwd_mix",
    )(dxm, z, o, proj, proj, wconv_t, g_co, g_ao, g_pm, gm, wout_all)


def bwd_conv(dco, proj, wconv_t):
    t = dco.shape[0]
    nt = t // TQ

    def body(d_ref, dn_ref, pc_ref, pcp_ref, wc_ref, dhc_ref, dcg_ref, dw_ref, cscr, dscr):
        i = pl.program_id(0)
        first = i == 0

        @pl.when(first)
        def _():
            dw_ref[...] = jnp.zeros_like(dw_ref)

        hc, _, cg, u, u1, u2, _ = _conv_fwd(pc_ref, pcp_ref, wc_ref, cscr, first)
        d0 = d_ref[...]
        dscr[0:TQ, :] = d0
        dscr[TQ:TQ + 8, :] = jnp.where(i == nt - 1, 0.0, dn_ref[...])
        d1 = dscr[1:TQ + 1, :]
        d2 = dscr[2:TQ + 2, :]
        du = wc_ref[2:3, :] * d0 + wc_ref[1:2, :] * d1 + wc_ref[0:1, :] * d2
        dhc_ref[...] = (du * cg).astype(BF16)
        dcg_ref[...] = (du * hc).astype(BF16)
        dw_ref[0:1, :] += jnp.sum(d0 * u2, axis=0, keepdims=True)
        dw_ref[1:2, :] += jnp.sum(d0 * u1, axis=0, keepdims=True)
        dw_ref[2:3, :] += jnp.sum(d0 * u, axis=0, keepdims=True)

    row = lambda w: pl.BlockSpec((TQ, w), lambda i: (i, 0))
    nxt = pl.BlockSpec((8, CW), lambda i: (jnp.minimum((i + 1) * (TQ // 8), t // 8 - 1), 0))
    return pl.pallas_call(
        body, grid=(nt,),
        in_specs=[row(CW), nxt] + _conv_specs() + [_const((8, CW))],
        out_specs=[row(CW), row(CW), _const((8, CW))],
        out_shape=[jax.ShapeDtypeStruct((t, CW), BF16), jax.ShapeDtypeStruct((t, CW), BF16),
                   jax.ShapeDtypeStruct((8, CW), F32)],
        scratch_shapes=[pltpu.VMEM((TQ + 16, CW), F32), pltpu.VMEM((TQ + 8, CW), F32)],
        compiler_params=_cp(("arbitrary",)), name="bwd_conv")(dco, dco, proj, proj, wconv_t)


def bwd_attn(proj, o, do, lse, bias2):
    t = o.shape[0]
    nt = t // TQ

    def body(q_ref, kp_ref, kc_ref, vp_ref, vc_ref, o_ref, do_ref, lse_ref, b2_ref,
             dq_ref, dk_hbm, dv_hbm, db_hbm, kwin, vwin, dk_acc, dv_acc, db_acc):
        i = pl.program_id(0)
        first = i == 0

        @pl.when(first)
        def _():
            dk_acc[...] = jnp.zeros_like(dk_acc)
            dv_acc[...] = jnp.zeros_like(dv_acc)
            db_acc[...] = jnp.zeros_like(db_acc)

        kwin[0:TQ, :] = kp_ref[...]
        kwin[TQ:2 * TQ, :] = kc_ref[...]
        vwin[0:TQ, :] = vp_ref[...]
        vwin[TQ:2 * TQ, :] = vc_ref[...]
        scale = HD ** -0.5
        qmask = _head_masks(scale)
        vmask = _head_masks(1.0)

        def group(g, carry):
            r0 = pl.multiple_of(g * QG, QG)
            base = pl.multiple_of(i * TQ + r0, QG)
            pen = _key_penalty(first, r0)
            for hp in range(NH // 2):
                ls = slice(LANES * hp, LANES * (hp + 1))
                qb = q_ref[pl.ds(r0, QG), ls]
                kw = kwin[pl.ds(r0, KG), ls]
                vw = vwin[pl.ds(r0, KG), ls]
                dob = do_ref[pl.ds(r0, QG), ls]
                prod = dob.astype(F32) * o_ref[pl.ds(r0, QG), ls]
                lseb = lse_ref[pl.ds(r0, QG), ls]
                dq = jnp.zeros((QG, LANES), F32)
                dk = jnp.zeros((KG, LANES), F32)
                dv = jnp.zeros((KG, LANES), F32)
                for a in range(2):
                    qa = qb * qmask[a]
                    doa = dob * vmask[a]
                    s = lax.dot_general(qa, kw, NT, preferred_element_type=F32)
                    s = s + b2_ref[2 * hp + a] + pen
                    p = jnp.exp(s - lseb[:, HD * a:HD * a + 1])
                    dp = lax.dot_general(doa, vw, NT, preferred_element_type=F32)
                    dsum = jnp.sum(prod * vmask[a].astype(F32), axis=-1, keepdims=True)
                    ds = p * (dp - dsum)
                    db_acc[2 * hp + a] += ds
                    dsb = ds.astype(BF16)
                    dq = dq + jnp.dot(dsb, kw, preferred_element_type=F32) * qmask[a].astype(F32)
                    dk = dk + lax.dot_general(dsb, qa, TN, preferred_element_type=F32)
                    dv = dv + lax.dot_general(p.astype(BF16), doa, TN, preferred_element_type=F32)
                dq_ref[pl.ds(r0, QG), ls] = dq.astype(BF16)
                dk_acc[pl.ds(base, KG), ls] += dk
                dv_acc[pl.ds(base, KG), ls] += dv
            return carry

        lax.fori_loop(0, TQ // QG, group, 0)

        @pl.when(i == nt - 1)
        def _():
            pltpu.sync_copy(dk_acc, dk_hbm)
            pltpu.sync_copy(dv_acc, dv_hbm)
            pltpu.sync_copy(db_acc, db_hbm)

    row = lambda w: pl.BlockSpec((TQ, w), lambda i: (i, 0))
    return pl.pallas_call(
        body, grid=(nt,),
        in_specs=_attn_window_specs() + [row(CW), row(CW), row(CW), _const((NH, QG, KG))],
        out_specs=[row(CW), _any(), _any(), _any()],
        out_shape=[jax.ShapeDtypeStruct((t, CW), BF16), jax.ShapeDtypeStruct((t + TQ, CW), F32),
                   jax.ShapeDtypeStruct((t + TQ, CW), F32), jax.ShapeDtypeStruct((NH, QG, KG), F32)],
        scratch_shapes=[pltpu.VMEM((2 * TQ, CW), BF16), pltpu.VMEM((2 * TQ, CW), BF16),
                        pltpu.VMEM((t + TQ, CW), F32), pltpu.VMEM((t + TQ, CW), F32),
                        pltpu.VMEM((NH, QG, KG), F32)],
        compiler_params=_cp(("arbitrary",)), name="bwd_attn",
    )(proj, proj, proj, proj, proj, o, do, lse, bias2)


def bwd_inproj(dxm, x, dhc, dbg, dcg, dq, dk, dv, g, w_all):
    t = x.shape[0]
    wc = PROJ // NCHIP

    def body(dxm_ref, x_ref, dhc_ref, dbg_ref, dcg_ref, dq_ref, dk_ref, dv_ref, g_ref, w_hbm,
             dx_ref, dp_ref, h_ref, dg_ref, w_v):
        @pl.when(pl.program_id(0) == 0)
        def _():
            pltpu.sync_copy(w_hbm, w_v)
            dg_ref[...] = jnp.zeros_like(dg_ref)

        dp_ref[:, 0:CW] = dhc_ref[...]
        dp_ref[:, CW:2 * CW] = dbg_ref[...]
        dp_ref[:, 2 * CW:3 * CW] = dcg_ref[...]
        dp_ref[:, 3 * CW:4 * CW] = dq_ref[...]
        dp_ref[:, 4 * CW:5 * CW] = dk_ref[...].astype(BF16)
        dp_ref[:, 5 * CW:6 * CW] = dv_ref[...].astype(BF16)
        dh = jnp.zeros((TQ, D), F32)
        for b in range(NCHIP):
            dh = dh + lax.dot_general(dp_ref[:, wc * b:wc * (b + 1)], w_v[b], NT, preferred_element_type=F32)
        xv = x_ref[...]
        gv = g_ref[...]
        h_ref[...] = _rms(xv, gv).astype(BF16)
        dxv, dgv = _rms_bwd(dh, xv, gv)
        dg_ref[...] += dgv
        dx_ref[...] = dxm_ref[...] + dxv

    row = lambda w: pl.BlockSpec((TQ, w), lambda i: (i, 0))
    pad = pl.BlockSpec((TQ, CW), lambda i: (i + 1, 0))
    return pl.pallas_call(
        body, grid=(t // TQ,),
        in_specs=[row(D), row(D), row(CW), row(CW), row(CW), row(CW), pad, pad, _const((1, D)), _any()],
        out_specs=[row(D), row(PROJ), row(D), _const((1, D))],
        out_shape=[jax.ShapeDtypeStruct((t, D), F32), jax.ShapeDtypeStruct((t, PROJ), BF16),
                   jax.ShapeDtypeStruct((t, D), BF16), jax.ShapeDtypeStruct((1, D), F32)],
        scratch_shapes=[pltpu.VMEM((NCHIP, D, wc), BF16)],
        compiler_params=_cp(("arbitrary",)), name="bwd_inproj",
    )(dxm, x, dhc, dbg, dcg, dq, dk, dv, g, w_all)


def wgrad(a, b, kb, nb, by_columns, name):
    t, k = a.shape
    n = b.shape[1]
    tk = 512

    def body(a_ref, b_ref, o_ref):
        o_ref[...] = jnp.zeros_like(o_ref)
        for c in range(t // tk):
            o_ref[...] += lax.dot_general(a_ref[tk * c:tk * (c + 1), :], b_ref[tk * c:tk * (c + 1), :], TN,
                                          preferred_element_type=F32)

    if by_columns:
        assert nb == n // NCHIP
        out_spec = pl.BlockSpec((None, kb, nb), lambda ki, ni: (ni, ki, 0))
        out_shape = jax.ShapeDtypeStruct((NCHIP, k, nb), F32)
    else:
        assert nb == n
        out_spec = pl.BlockSpec((kb, nb), lambda ki, ni: (ki, 0))
        out_shape = jax.ShapeDtypeStruct((k, n), F32)
    return pl.pallas_call(
        body, grid=(k // kb, n // nb),
        in_specs=[pl.BlockSpec((t, kb), lambda ki, ni: (0, ki)), pl.BlockSpec((t, nb), lambda ki, ni: (0, ni))],
        out_specs=out_spec, out_shape=out_shape,
        compiler_params=_cp(("arbitrary", "arbitrary")), name=name)(a, b)


LEFT = BAND - CHUNK
TOE = 1024
N_FLAT = LEFT - REL_CLIP + 1
N_VAR = BAND - N_FLAT


def _diag_vector(table):
    last = table[:, 2 * REL_CLIP:]
    var = table[:, 2 * REL_CLIP - N_VAR:2 * REL_CLIP][:, ::-1]
    return jnp.concatenate([jnp.broadcast_to(last, (NH, N_FLAT)), var, jnp.broadcast_to(last, (NH, TOE - BAND))], axis=1)


def _diag_vector_bwd(dvec):
    dlast = jnp.sum(dvec[:, :N_FLAT], axis=1, keepdims=True) + jnp.sum(dvec[:, BAND:], axis=1, keepdims=True)
    dvar = dvec[:, N_FLAT:BAND][:, ::-1]
    return jnp.concatenate([jnp.zeros((NH, 2 * REL_CLIP - N_VAR), F32), dvar, dlast], axis=1)


def _band_valid():
    r = lax.broadcasted_iota(jnp.int32, (QG, KG), 0)
    p = lax.broadcasted_iota(jnp.int32, (QG, KG), 1)
    start = jnp.where(r >= CHUNK, CHUNK, 0)
    return (p >= start) & (p < start + BAND)


def bias_expand(vec):
    def body(v_ref, o_ref):
        valid = _band_valid()
        for h in range(NH):
            rows = jnp.broadcast_to(v_ref[h:h + 1, :], (QG, TOE))
            toe = pltpu.roll(rows, 0, 1, stride=1, stride_axis=0)
            o_ref[h] = jnp.where(valid, toe[:, :KG], NEG_INF)

    return pl.pallas_call(body, out_shape=jax.ShapeDtypeStruct((NH, QG, KG), F32), name="bias_expand")(vec)


def bias_reduce(db2):
    def body(d_ref, o_ref):
        for h in range(NH):
            d = jnp.concatenate([jnp.zeros((QG, TOE - KG), F32), d_ref[h]], axis=1)
            back = pltpu.roll(d, 0, 1, stride=1, stride_axis=0)
            o_ref[h:h + 1, :] = jnp.sum(back, axis=0, keepdims=True)

    rev = pl.pallas_call(body, out_shape=jax.ShapeDtypeStruct((NH, TOE), F32), name="bias_reduce")(db2[:, :, ::-1])
    return rev[:, ::-1]


def _place():
    x, y, c = lax.axis_index("x"), lax.axis_index("y"), lax.axis_index("c")
    chips = [(1 - x, y), (x, 1 - y), (1 - x, 1 - y)]
    return x, y, c, chips


def _half(ref_rows, c):
    return pl.ds(c * (ref_rows // 2), ref_rows // 2)


HBM_SPEC = pl.BlockSpec(memory_space=pltpu.HBM)
SEM_SPEC = pl.BlockSpec(memory_space=pltpu.SEMAPHORE)
IN_FLIGHT = pltpu.CompilerParams(has_side_effects=pltpu.SideEffectType.DATAFLOW_SIDE_EFFECTING)


def _in_hbm(a):
    return pltpu.with_memory_space_constraint(a, pltpu.HBM)


def cast_to_slot(w, chip, layer):
    _, rows, cols = w.shape
    rb = rows // 4

    def body(b_ref, w_ref, o_ref):
        del b_ref
        o_ref[...] = w_ref[...].astype(BF16)

    grid_spec = pltpu.PrefetchScalarGridSpec(
        num_scalar_prefetch=1, grid=(rows // rb,),
        in_specs=[pl.BlockSpec((None, rb, cols), lambda r, b: (layer, r, 0))],
        out_specs=pl.BlockSpec((None, rb, cols), lambda r, b: (b[0], r, 0)))
    return pl.pallas_call(body, grid_spec=grid_spec, out_shape=jax.ShapeDtypeStruct((NCHIP, rows, cols), BF16),
                          compiler_params=_cp(("arbitrary",)), name="cast_to_slot")(chip, w)


def _gather_copies(bufs, send, recv):
    x, y, c, chips = _place()
    b = 2 * x + y
    out = []
    for k, buf in enumerate(bufs):
        rows = buf.shape[1]
        mine = buf.at[b, _half(rows, c), :]
        for j, (cx, cy) in enumerate(chips):
            theirs = buf.at[2 * cx + cy, _half(rows, c), :]
            sems = dict(send_sem=send.at[3 * k + j], recv_sem=recv.at[3 * k + j],
                        device_id=(cx, cy, c), device_id_type=MESH)
            out.append((pltpu.make_async_remote_copy(src_ref=mine, dst_ref=mine, **sems),
                        pltpu.make_async_remote_copy(src_ref=theirs, dst_ref=theirs, **sems)))
    return out


def gather_start(bufs, after, layer):
    n = len(bufs)

    def body(*refs):
        ins = refs[:n]
        send, recv = refs[n + 1], refs[n + 2]
        token = refs[-1]
        for start, _ in _gather_copies(ins, send, recv):
            start.start()
        token[...] = jnp.zeros_like(token)

    sems = pltpu.SemaphoreType.DMA((3 * n,))
    res = pl.pallas_call(
        body, name=f"gather_start_{layer}",
        in_specs=[HBM_SPEC] * n + [_any()],
        out_specs=[SEM_SPEC, SEM_SPEC] + [HBM_SPEC] * n + [pl.BlockSpec(memory_space=pltpu.VMEM)],
        out_shape=[sems, sems] + [pltpu.HBM(b.shape, b.dtype) for b in bufs] + [jax.ShapeDtypeStruct((8, LANES), F32)],
        input_output_aliases={k: 2 + k for k in range(n)}, compiler_params=IN_FLIGHT,
    )(*[_in_hbm(b) for b in bufs], after)
    return res[0], res[1], res[2:2 + n], res[-1]


def gather_wait(send, recv, bufs, after, layer):
    n = len(bufs)

    def body(*refs):
        ins = refs[:n]
        send_ref, recv_ref = refs[n], refs[n + 1]
        for start, arrival in _gather_copies(ins, send_ref, recv_ref):
            start.wait_send()
            arrival.wait_recv()

    return pl.pallas_call(
        body, name=f"gather_wait_{layer}",
        in_specs=[HBM_SPEC] * n + [SEM_SPEC, SEM_SPEC, _any()], out_specs=[HBM_SPEC] * n,
        out_shape=[pltpu.HBM(b.shape, b.dtype) for b in bufs],
        input_output_aliases={k: k for k in range(n)}, compiler_params=IN_FLIGHT,
    )(*bufs, send, recv, after)


def gather_forward(bufs):
    n = len(bufs)

    def body(*refs):
        outs = refs[n:2 * n]
        send, recv = refs[2 * n:]
        x, y, c, chips = _place()
        cps = []
        for k in range(n):
            rows = outs[k].shape[1]
            for j, (cx, cy) in enumerate(chips):
                sems = dict(send_sem=send.at[3 * k + j], recv_sem=recv.at[3 * k + j],
                            device_id=(x, y, 1 - c), device_id_type=MESH)
                mine = outs[k].at[2 * cx + cy, _half(rows, c), :]
                theirs = outs[k].at[2 * cx + cy, _half(rows, 1 - c), :]
                cp = pltpu.make_async_remote_copy(src_ref=mine, dst_ref=mine, **sems)
                cp.start()
                cps.append((cp, pltpu.make_async_remote_copy(src_ref=theirs, dst_ref=theirs, **sems)))
        for cp, arrival in cps:
            cp.wait_send()
            arrival.wait_recv()

    return pl.pallas_call(
        body, in_specs=[_any()] * n, out_specs=[_any()] * n,
        out_shape=[jax.ShapeDtypeStruct(b.shape, b.dtype) for b in bufs], input_output_aliases={k: k for k in range(n)},
        scratch_shapes=[pltpu.SemaphoreType.DMA((3 * n,)), pltpu.SemaphoreType.DMA((3 * n,))],
        name="gather_forward")(*bufs)


def _forward_copies(bufs, send, recv):
    x, y, c, chips = _place()
    out = []
    for k, buf in enumerate(bufs):
        rows = buf.shape[1]
        for j, (cx, cy) in enumerate(chips):
            sems = dict(send_sem=send.at[3 * k + j], recv_sem=recv.at[3 * k + j],
                        device_id=(x, y, 1 - c), device_id_type=MESH)
            mine = buf.at[2 * cx + cy, _half(rows, c), :]
            theirs = buf.at[2 * cx + cy, _half(rows, 1 - c), :]
            out.append((pltpu.make_async_remote_copy(src_ref=mine, dst_ref=mine, **sems),
                        pltpu.make_async_remote_copy(src_ref=theirs, dst_ref=theirs, **sems)))
    return out


def forward_start(bufs, tag):
    n = len(bufs)

    def body(*refs):
        ins = refs[:n]
        send, recv = refs[n], refs[n + 1]
        token = refs[-1]
        for start, _ in _forward_copies(ins, send, recv):
            start.start()
        token[...] = jnp.zeros_like(token)

    sems = pltpu.SemaphoreType.DMA((3 * n,))
    res = pl.pallas_call(
        body, name=f"forward_start_{tag}", in_specs=[HBM_SPEC] * n,
        out_specs=[SEM_SPEC, SEM_SPEC] + [HBM_SPEC] * n + [pl.BlockSpec(memory_space=pltpu.VMEM)],
        out_shape=[sems, sems] + [pltpu.HBM(b.shape, b.dtype) for b in bufs] + [jax.ShapeDtypeStruct((8, LANES), F32)],
        input_output_aliases={k: 2 + k for k in range(n)}, compiler_params=IN_FLIGHT,
    )(*[_in_hbm(b) for b in bufs])
    return res[0], res[1], res[2:2 + n], res[-1]


def forward_wait(send, recv, bufs, after, tag):
    n = len(bufs)

    def body(*refs):
        ins = refs[:n]
        send_ref, recv_ref = refs[n], refs[n + 1]
        for start, arrival in _forward_copies(ins, send_ref, recv_ref):
            start.wait_send()
            arrival.wait_recv()

    return pl.pallas_call(
        body, name=f"forward_wait_{tag}",
        in_specs=[HBM_SPEC] * n + [SEM_SPEC, SEM_SPEC, _any()], out_specs=[HBM_SPEC] * n,
        out_shape=[pltpu.HBM(b.shape, b.dtype) for b in bufs],
        input_output_aliases={k: k for k in range(n)}, compiler_params=IN_FLIGHT,
    )(*bufs, send, recv, after)


def _exchange_copies(srcs, lands, send, recv):
    x, y, c, _ = _place()
    return [pltpu.make_async_remote_copy(
        src_ref=src.at[:, _half(src.shape[1], 1 - c), :], dst_ref=land, send_sem=send.at[k], recv_sem=recv.at[k],
        device_id=(x, y, 1 - c), device_id_type=MESH) for k, (src, land) in enumerate(zip(srcs, lands))]


def exchange_start(srcs, tag):
    n = len(srcs)
    lands = [lax.empty((s.shape[0], s.shape[1] // 2, s.shape[2]), s.dtype) for s in srcs]

    def body(*refs):
        ins, land_refs = refs[:n], refs[n:2 * n]
        send, recv = refs[2 * n], refs[2 * n + 1]
        token = refs[-1]
        for cp in _exchange_copies(ins, land_refs, send, recv):
            cp.start()
        token[...] = jnp.zeros_like(token)

    sems = pltpu.SemaphoreType.DMA((n,))
    res = pl.pallas_call(
        body, name=f"exchange_start_{tag}",
        in_specs=[HBM_SPEC] * (2 * n),
        out_specs=[SEM_SPEC, SEM_SPEC] + [HBM_SPEC] * (2 * n) + [pl.BlockSpec(memory_space=pltpu.VMEM)],
        out_shape=[sems, sems] + [pltpu.HBM(a.shape, a.dtype) for a in list(srcs) + lands]
        + [jax.ShapeDtypeStruct((8, LANES), F32)],
        input_output_aliases={k: 2 + k for k in range(2 * n)}, compiler_params=IN_FLIGHT,
    )(*[_in_hbm(a) for a in list(srcs) + lands])
    return res[0], res[1], res[2:2 + n], res[2 + n:2 + 2 * n], res[-1]


def exchange_wait(send, recv, srcs, lands, after, tag):
    n = len(srcs)

    def body(*refs):
        ins, land_refs = refs[:n], refs[n:2 * n]
        send_ref, recv_ref = refs[2 * n], refs[2 * n + 1]
        for cp in _exchange_copies(ins, land_refs, send_ref, recv_ref):
            cp.wait_send()
            cp.wait_recv()

    res = pl.pallas_call(
        body, name=f"exchange_wait_{tag}",
        in_specs=[HBM_SPEC] * (2 * n) + [SEM_SPEC, SEM_SPEC, _any()], out_specs=[HBM_SPEC] * (2 * n),
        out_shape=[pltpu.HBM(a.shape, a.dtype) for a in list(srcs) + list(lands)],
        input_output_aliases={k: k for k in range(2 * n)}, compiler_params=IN_FLIGHT,
    )(*srcs, *lands, send, recv, after)
    return res[:n], res[n:]


def add_pair(g, r1, core):
    ns, rows, cols = g.shape
    hr = rows // 2

    def body(c_ref, g_ref, r_ref, o_ref):
        del c_ref
        o_ref[...] = (g_ref[...] + r_ref[...]).astype(BF16)

    blk = (None, hr, cols)
    grid_spec = pltpu.PrefetchScalarGridSpec(
        num_scalar_prefetch=1, grid=(ns,),
        in_specs=[pl.BlockSpec(blk, lambda s, c: (s, c[0], 0)), pl.BlockSpec(blk, lambda s, c: (s, 0, 0))],
        out_specs=pl.BlockSpec(blk, lambda s, c: (s, 0, 0)))
    return pl.pallas_call(body, grid_spec=grid_spec, out_shape=jax.ShapeDtypeStruct(r1.shape, BF16),
                          compiler_params=_cp(("arbitrary",)), name="add_pair")(core, g, r1)


def _scatter_copies(srcs, lands, send, recv):
    _, _, c, chips = _place()
    out = []
    for k, (src, land) in enumerate(zip(srcs, lands)):
        for j, (cx, cy) in enumerate(chips):
            out.append(pltpu.make_async_remote_copy(
                src_ref=src.at[2 * cx + cy], dst_ref=land.at[j], send_sem=send.at[3 * k + j],
                recv_sem=recv.at[3 * k + j], device_id=(cx, cy, c), device_id_type=MESH))
    return out


def scatter_start(srcs, layer):
    n = len(srcs)
    lands = [lax.empty((3,) + s.shape[1:], s.dtype) for s in srcs]

    def body(*refs):
        ins, land_refs = refs[:n], refs[n:2 * n]
        send, recv = refs[2 * n], refs[2 * n + 1]
        token = refs[-1]
        for cp in _scatter_copies(ins, land_refs, send, recv):
            cp.start()
        token[...] = jnp.zeros_like(token)

    sems = pltpu.SemaphoreType.DMA((3 * n,))
    res = pl.pallas_call(
        body, name=f"scatter_start_{layer}",
        in_specs=[HBM_SPEC] * (2 * n),
        out_specs=[SEM_SPEC, SEM_SPEC] + [HBM_SPEC] * (2 * n) + [pl.BlockSpec(memory_space=pltpu.VMEM)],
        out_shape=[sems, sems] + [pltpu.HBM(a.shape, a.dtype) for a in srcs + lands]
        + [jax.ShapeDtypeStruct((8, LANES), F32)],
        input_output_aliases={k: 2 + k for k in range(2 * n)}, compiler_params=IN_FLIGHT,
    )(*[_in_hbm(a) for a in srcs + lands])
    return res[0], res[1], res[2:2 + n], res[2 + n:2 + 2 * n], res[-1]


def scatter_wait(send, recv, srcs, lands, after, layer):
    n = len(srcs)

    def body(*refs):
        ins, land_refs = refs[:n], refs[n:2 * n]
        send_ref, recv_ref = refs[2 * n], refs[2 * n + 1]
        for cp in _scatter_copies(ins, land_refs, send_ref, recv_ref):
            cp.wait_send()
            cp.wait_recv()

    res = pl.pallas_call(
        body, name=f"scatter_wait_{layer}",
        in_specs=[HBM_SPEC] * (2 * n) + [SEM_SPEC, SEM_SPEC, _any()], out_specs=[HBM_SPEC] * (2 * n),
        out_shape=[pltpu.HBM(a.shape, a.dtype) for a in list(srcs) + list(lands)],
        input_output_aliases={k: k for k in range(2 * n)}, compiler_params=IN_FLIGHT,
    )(*srcs, *lands, send, recv, after)
    return res[n:]


def add_chips(g, r1, r2, place, total, layer):
    _, rows, cols = g.shape
    hr = rows // 2

    def body(p_ref, g_ref, r1_ref, r2_ref, t_hbm, o_ref):
        del p_ref, t_hbm
        own = g_ref[...] + r1_ref[...]
        o_ref[...] = ((own + r2_ref[0].astype(F32)) + r2_ref[1].astype(F32)) + r2_ref[2].astype(F32)

    grid_spec = pltpu.PrefetchScalarGridSpec(
        num_scalar_prefetch=1, grid=(1,),
        in_specs=[pl.BlockSpec((None, hr, cols), lambda i, p: (p[1], p[0], 0)),
                  pl.BlockSpec((None, hr, cols), lambda i, p: (p[1], 0, 0)),
                  pl.BlockSpec((3, hr, cols), lambda i, p: (0, 0, 0)), _any()],
        out_specs=pl.BlockSpec((None, hr, cols), lambda i, p: (layer, p[0], 0)))
    return pl.pallas_call(body, grid_spec=grid_spec, out_shape=jax.ShapeDtypeStruct(total.shape, F32),
                          input_output_aliases={4: 0}, compiler_params=_cp(("arbitrary",)),
                          name="add_chips")(place, g, r1, r2, total)


def pair_share(gs):
    n = len(gs)

    def body(*refs):
        outs = refs[n:2 * n]
        send, recv = refs[2 * n:]
        x, y, c, _ = _place()
        cps = []
        for k in range(n):
            mine = outs[k].at[:, _half(outs[k].shape[1], c), :]
            cp = pltpu.make_async_remote_copy(
                src_ref=mine, dst_ref=mine, send_sem=send.at[k], recv_sem=recv.at[k],
                device_id=(x, y, 1 - c), device_id_type=MESH)
            cp.start()
            cps.append(cp)
        for k, cp in enumerate(cps):
            cp.wait_send()
            theirs = outs[k].at[:, _half(outs[k].shape[1], 1 - c), :]
            pltpu.make_async_remote_copy(
                src_ref=theirs, dst_ref=theirs, send_sem=send.at[k], recv_sem=recv.at[k],
                device_id=(x, y, 1 - c), device_id_type=MESH).wait_recv()

    return pl.pallas_call(
        body, in_specs=[_any()] * n, out_specs=[_any()] * n,
        out_shape=[jax.ShapeDtypeStruct(g.shape, g.dtype) for g in gs], input_output_aliases={k: k for k in range(n)},
        scratch_shapes=[pltpu.SemaphoreType.DMA((n,)), pltpu.SemaphoreType.DMA((n,))],
        name="pair_share")(*gs)


def small_collect(v, reduce, name):
    rows = v.shape[0]
    flips = [(fx, fy, fc) for fx in (0, 1) for fy in (0, 1) for fc in (0, 1)][1:]

    def body(v_ref, o_ref, buf, send, recv):
        x, y, c, _ = _place()
        buf[4 * x + 2 * y + c] = v_ref[...]
        peers = [(jnp.where(fx, 1 - x, x), jnp.where(fy, 1 - y, y), jnp.where(fc, 1 - c, c)) for fx, fy, fc in flips]
        cps = []
        for k, peer in enumerate(peers):
            cp = pltpu.make_async_remote_copy(
                src_ref=v_ref, dst_ref=buf.at[4 * x + 2 * y + c], send_sem=send.at[k], recv_sem=recv.at[k],
                device_id=peer, device_id_type=MESH)
            cp.start()
            cps.append(cp)
        for k, (px, py, pc) in enumerate(peers):
            pltpu.make_async_remote_copy(
                src_ref=v_ref, dst_ref=buf.at[4 * px + 2 * py + pc], send_sem=send.at[k], recv_sem=recv.at[k],
                device_id=(px, py, pc), device_id_type=MESH).wait_recv()
        for cp in cps:
            cp.wait_send()
        if reduce:
            acc = buf[0]
            for s in range(1, 8):
                acc = acc + buf[s]
            o_ref[...] = acc
        else:
            o_ref[...] = buf[...]

    vm = pl.BlockSpec(memory_space=pltpu.VMEM)
    out_shape = jax.ShapeDtypeStruct((rows, SMALL_COLS) if reduce else (8, rows, SMALL_COLS), F32)
    return pl.pallas_call(
        body, in_specs=[vm], out_specs=vm, out_shape=out_shape,
        scratch_shapes=[pltpu.VMEM((8, rows, SMALL_COLS), F32), pltpu.SemaphoreType.DMA((7,)),
                        pltpu.SemaphoreType.DMA((7,))],
        name=name)(v)


def adamw(w, g, m, v, rb, name):
    nl, rows, cols = w.shape

    def body(w_ref, g_ref, m_ref, v_ref, go_ref, d_ref, nm_ref, nv_ref):
        gv = g_ref[...]
        go_ref[...] = gv
        nm = ADAM_B1 * m_ref[...] + (1.0 - ADAM_B1) * gv
        nv = ADAM_B2 * v_ref[...] + (1.0 - ADAM_B2) * (gv * gv)
        m_hat = nm / (1.0 - ADAM_B1 ** ADAM_STEP)
        v_hat = nv / (1.0 - ADAM_B2 ** ADAM_STEP)
        d_ref[...] = -ADAM_LR * (m_hat / (jnp.sqrt(v_hat) + ADAM_EPS) + ADAM_WD * w_ref[...])
        nm_ref[...] = nm
        nv_ref[...] = nv

    blk = pl.BlockSpec((None, rb, cols), lambda l, r: (l, r, 0))
    shp = jax.ShapeDtypeStruct(w.shape, F32)
    return pl.pallas_call(body, grid=(nl, rows // rb), in_specs=[blk] * 4, out_specs=[blk] * 4, out_shape=[shp] * 4,
                          compiler_params=_cp(("arbitrary", "arbitrary")), name=name)(w, g, m, v)


def _pack(parts, rows):
    flat = jnp.concatenate([p.reshape(-1).astype(F32) for p in parts])
    return jnp.pad(flat, (0, rows * SMALL_COLS - flat.shape[0])).reshape(rows, SMALL_COLS)


def _unpack(vec, shapes):
    flat = vec.reshape(-1)
    out, off = [], 0
    for s in shapes:
        size = 1
        for d in s:
            size *= d
        out.append(flat[off:off + size].reshape(s))
        off += size
    return out


def kernel(x, w_in, w_conv, rel_bias, g_conv_out, g_attn_out, w_out, g_pre_mix, g_post_mix, g_pre_ffn, g_post_ffn, w_ffn_in, w_ffn_out, loss_target, m_w_in, m_w_conv, m_rel_bias, m_g_conv_out, m_g_attn_out, m_w_out, m_g_pre_mix, m_g_post_mix, m_g_pre_ffn, m_g_post_ffn, m_w_ffn_in, m_w_ffn_out, v_w_in, v_w_conv, v_rel_bias, v_g_conv_out, v_g_attn_out, v_w_out, v_g_pre_mix, v_g_post_mix, v_g_pre_ffn, v_g_post_ffn, v_w_ffn_in, v_w_ffn_out):
    xi, yi, ci = lax.axis_index("x"), lax.axis_index("y"), lax.axis_index("c")
    chip = 2 * xi + yi
    nl = w_in.shape[0]
    x0 = x[0]
    target = loss_target[0]
    cwl = CW // NCHIP

    chip1 = chip.reshape(1).astype(jnp.int32)
    own = [[cast_to_slot(w, chip1, l) for w in (w_in, w_out, w_ffn_in, w_ffn_out)] for l in range(nl)]
    wc_all = small_collect(_pack([w_conv], 8), False, "gather_w_conv")
    wc_full = wc_all[0::2].reshape(NCHIP, -1)[:, :nl * cwl * 3].reshape(NCHIP, nl, cwl, 3)
    wc_full = jnp.transpose(wc_full, (1, 0, 2, 3)).reshape(nl, CW, 3)
    wconv_t = jnp.pad(jnp.transpose(wc_full, (0, 2, 1)), ((0, 0), (0, 5), (0, 0)))
    gm = jnp.kron(jnp.eye(CW // HD, dtype=F32), jnp.full((HD, HD), 1.0 / HD, F32)).astype(BF16)
    row = lambda a, l: a[l][None, :]

    def token(t):
        return t[0:1, 0:1]

    def gather_finish(flight, after, tag):
        send, recv, bufs, _ = flight
        return gather_forward(gather_wait(send, recv, bufs, after, tag))

    first_mix = gather_start(own[0][:2], wc_all, "0m")
    first_ffn = gather_start(own[0][2:], first_mix[3], "0f")
    flight = to_sibling = None
    saved, weights = [], []
    h = x0
    for l in range(nl):
        if l == 0:
            gw_in, gw_out = gather_finish(first_mix, x0, "0m")
        elif l == 1:
            gw_in, gw_out, gw_fi, gw_fo = gather_finish(flight, h, l)
        else:
            gw_in, gw_out, gw_fi, gw_fo = forward_wait(*to_sibling[:3], h, l)
        gw_out = gw_out.reshape(D, D)
        g_pm, g_pf = row(g_pre_mix, l), row(g_pre_ffn, l)
        if l == 0:
            g_pm = g_pm + token(first_ffn[3])
        if l + 1 < nl:
            flight = gather_start(own[l + 1], gw_in, l + 1)
            g_pm = g_pm + token(flight[3])
        bias2 = bias_expand(_diag_vector(rel_bias[l]))
        proj = fwd_inproj(h, g_pm, gw_in)
        xmid, o, lse, y, z = fwd_mix(h, proj, bias2, wconv_t[l], row(g_conv_out, l), row(g_attn_out, l),
                                     row(g_post_mix, l), gm, gw_out)
        if l == 0:
            gw_fi, gw_fo = gather_finish(first_ffn, xmid, "0f")
        elif l + 1 < nl:
            send, recv, bufs, _ = flight
            to_sibling = forward_start(gather_wait(send, recv, bufs, xmid, l + 1), l + 1)
            g_pf = g_pf + token(to_sibling[3])
        gw_fo = gw_fo.reshape(2, DFF // 2, D)
        gu, f, xout = fwd_ffn(xmid, g_pf, row(g_post_ffn, l), gw_fi, gw_fo)
        saved.append((h, proj, bias2, xmid, o, lse, y, z, gu, f))
        weights.append((gw_in, gw_out, gw_fi, gw_fo))
        h = xout
    dx, loss_blk = loss_head(h, target)

    core = ci.reshape(1).astype(jnp.int32)
    place = jnp.stack([ci, chip]).astype(jnp.int32)
    totals = [lax.empty(w.shape, F32) for w in (w_in, w_out, w_ffn_in, w_ffn_out)]
    small = {k: [None] * nl for k in ("co", "ao", "pm", "qm", "pf", "qf", "rel", "wc")}

    def reduce_begin(kinds, grads, tag):
        return kinds, exchange_start(grads, tag), tag

    def reduce_mid(state, after):
        kinds, (send, recv, srcs, lands, _), tag = state
        grads, from_sibling = exchange_wait(send, recv, srcs, lands, after, tag)
        pair_sums = [add_pair(g, r, core) for g, r in zip(grads, from_sibling)]
        return kinds, grads, from_sibling, scatter_start(pair_sums, tag), tag

    def reduce_end(state, after, totals, layer):
        kinds, grads, from_sibling, (send, recv, srcs, lands, _), tag = state
        from_chips = scatter_wait(send, recv, srcs, lands, after, tag)
        totals = list(totals)
        for i, g, r1, r2 in zip(kinds, grads, from_sibling, from_chips):
            totals[i] = add_chips(g, r1, r2, place, totals[i], layer)
        return totals

    begun = flying = None
    for l in reversed(range(nl)):
        hin, proj, bias2, xmid, o, lse, y, z, gu, f = saved[l]
        gw_in, gw_out, gw_fi, gw_fo = weights[l]
        g_qf, g_qm, wct = row(g_post_ffn, l), row(g_post_mix, l), wconv_t[l]
        if begun is not None:
            g_qf = g_qf + token(begun[1][4])
        dxm, dfb, act, dgu, h2, dg_qf, dg_pf = bwd_ffn(dx, f, xmid, gu, row(g_pre_ffn, l), g_qf, gw_fi, gw_fo)
        if begun is not None:
            flying = reduce_mid(begun, dxm)
            g_qm = g_qm + token(flying[3][4])
        gr_fo = wgrad(act, dfb, 256, D, False, "wgrad_ffn_out").reshape(NCHIP, DFF // NCHIP, D)
        gr_fi = wgrad(h2, dgu, 512, 2 * DFF // NCHIP, True, "wgrad_ffn_in")
        if l == 0:
            begun_ffn = reduce_begin([2, 3], [gr_fi, gr_fo], "0f")
            g_qm = g_qm + token(begun_ffn[1][4])
        dzb, do, dco, dbg, dg_qm, dg_co, dg_ao = bwd_mix(dxm, z, o, proj, wct, row(g_conv_out, l),
                                                          row(g_attn_out, l), g_qm, gm, gw_out)
        if l == 0:
            flying_ffn = reduce_mid(begun_ffn, dzb)
            wct = wct + token(flying_ffn[3][4])
        gr_out = wgrad(y, dzb, 512, D, False, "wgrad_out").reshape(NCHIP, D // NCHIP, D)
        dhc, dcg, dwc = bwd_conv(dco, proj, wct)
        dq, dk, dv, db2 = bwd_attn(proj, o, do, lse, bias2)
        dx, dproj, hb, dg_pm = bwd_inproj(dxm, hin, dhc, dbg, dcg, dq, dk, dv, row(g_pre_mix, l), gw_in)
        if flying is not None:
            totals = reduce_end(flying, dx, totals, l + 1)
        gr_in = wgrad(hb, dproj, 512, PROJ // NCHIP, True, "wgrad_in")
        small["co"][l], small["ao"][l], small["pm"][l], small["qm"][l] = dg_co, dg_ao, dg_pm, dg_qm
        small["pf"][l], small["qf"][l] = dg_pf, dg_qf
        small["rel"][l] = _diag_vector_bwd(bias_reduce(db2))
        small["wc"][l] = jnp.transpose(dwc[0:3], (1, 0))
        if l > 0:
            begun = reduce_begin([0, 1, 2, 3], [gr_in, gr_out, gr_fi, gr_fo], l)
    flying_mix = reduce_mid(reduce_begin([0, 1], [gr_in, gr_out], "0m"), dx)
    totals = reduce_end(flying_ffn, flying_mix[3][4], totals, 0)
    totals = reduce_end(flying_mix, totals[2], totals, 0)
    gr_in, gr_out, gr_fi, gr_fo = pair_share(totals)

    order = ("co", "ao", "pm", "qm", "pf", "qf", "rel", "wc")
    parts = [jnp.stack(small[k]) for k in order] + [loss_blk[0:1, 0:1]]
    shapes = [p.shape for p in parts]
    red = _unpack(small_collect(_pack(parts, 40), True, "reduce_small"), shapes)
    gr_co, gr_ao, gr_pm, gr_qm, gr_pf, gr_qf, gr_rel, gr_wc_full, loss = red
    gr_co, gr_ao, gr_pm, gr_qm, gr_pf, gr_qf = [a.reshape(nl, -1) for a in (gr_co, gr_ao, gr_pm, gr_qm, gr_pf, gr_qf)]
    gr_wc = lax.dynamic_slice_in_dim(gr_wc_full, chip * cwl, cwl, axis=1)
    loss = loss.reshape(())

    big = []
    for w, g, m, v, name in ((w_in, gr_in, m_w_in, v_w_in, "adamw_in"), (w_out, gr_out, m_w_out, v_w_out, "adamw_out"),
                             (w_ffn_in, gr_fi, m_w_ffn_in, v_w_ffn_in, "adamw_ffn_in"),
                             (w_ffn_out, gr_fo, m_w_ffn_out, v_w_ffn_out, "adamw_ffn_out")):
        big.append(adamw(w, g, m, v, w.shape[1] // 4, name))
    sw = [g_conv_out, g_attn_out, g_pre_mix, g_post_mix, g_pre_ffn, g_post_ffn, rel_bias, w_conv]
    sg = [gr_co, gr_ao, gr_pm, gr_qm, gr_pf, gr_qf, gr_rel, gr_wc]
    sm = [m_g_conv_out, m_g_attn_out, m_g_pre_mix, m_g_post_mix, m_g_pre_ffn, m_g_post_ffn, m_rel_bias, m_w_conv]
    sv = [v_g_conv_out, v_g_attn_out, v_g_pre_mix, v_g_post_mix, v_g_pre_ffn, v_g_post_ffn, v_rel_bias, v_w_conv]
    sshapes = [a.shape for a in sw]
    packed = [_pack(a, 32)[None] for a in (sw, sg, sm, sv)]
    s_out = [_unpack(a[0], sshapes) for a in adamw(*packed, 32, "adamw_small")]

    def leaves(big_i, small_i):
        b_in, b_out, b_fi, b_fo = big_i
        s_co, s_ao, s_pm, s_qm, s_pf, s_qf, s_rel, s_wc = small_i
        return [b_in, s_wc, s_rel, s_co, s_ao, b_out, s_pm, s_qm, s_pf, s_qf, b_fi, b_fo]

    out = [loss, dx[None]]
    out += leaves([b[0] for b in big], sg)
    for i in range(1, 4):
        out += leaves([b[i] for b in big], s_out[i])
    return tuple(out)
```

```python
import functools

import jax
import jax.numpy as jnp
from jax import lax
from jax.experimental import pallas as pl
from jax.experimental.pallas import tpu as pltpu

F32 = jnp.float32
BF16 = jnp.bfloat16

D = 1024
PROJ = 3072
CW = 512
HD = 64
NH = 8
CHUNK = 64
BAND = 576
REL_CLIP = 128
NREL = 2 * REL_CLIP + 1
DFF = 2816
DEPTH = 4
NCHIP = 4
EPS = 1e-6
NEG_INF = -1e30

ADAM_LR = 0.001
ADAM_B1 = 0.9
ADAM_B2 = 0.999
ADAM_EPS = 1e-08
ADAM_WD = 0.01
ADAM_STEP = 10

V7X_VMEM_BYTES = 64 * 1024 * 1024
VMEM_LIMIT = V7X_VMEM_BYTES - 8 * 1024 * 1024
LANES = 128
QG_FWD = 4 * CHUNK
QG_BWD = 2 * CHUNK
LEFT = BAND - CHUNK
TQ = 512
TM = 256
SMALL_COLS = 1024
MESH = pl.DeviceIdType.MESH
NT = (((1,), (1,)), ((), ()))
TN = (((0,), (0,)), ((), ()))


def _cp(sem=None, vmem=VMEM_LIMIT):
    return pltpu.CompilerParams(dimension_semantics=sem, vmem_limit_bytes=vmem)


def _any():
    return pl.BlockSpec(memory_space=pl.ANY)


def _const(shape):
    nd = len(shape)
    return pl.BlockSpec(shape, lambda *_: (0,) * nd)


def _rms(v, g):
    r = lax.rsqrt(jnp.mean(v * v, axis=-1, keepdims=True) + EPS)
    return v * r * g


def _rms_bwd(dy, v, g):
    r = lax.rsqrt(jnp.mean(v * v, axis=-1, keepdims=True) + EPS)
    vh = v * r
    dg = jnp.sum(dy * vh, axis=0, keepdims=True)
    dvh = dy * g
    dv = r * (dvh - vh * jnp.mean(dvh * vh, axis=-1, keepdims=True))
    return dv, dg


def _group_mean(v, gm):
    hi = v.astype(BF16)
    lo = (v - hi.astype(F32)).astype(BF16)
    return jnp.dot(hi, gm, preferred_element_type=F32) + jnp.dot(lo, gm, preferred_element_type=F32)


def _group_rms_bwd(dy, v, g, gm):
    r = lax.rsqrt(_group_mean(v * v, gm) + EPS)
    vh = v * r
    dg = jnp.sum(dy * vh, axis=0, keepdims=True)
    dvh = dy * g
    dv = r * (dvh - vh * _group_mean(dvh * vh, gm))
    return dv, dg


def _head_masks(scale):
    lane = lax.broadcasted_iota(jnp.int32, (1, LANES), 1)
    return [jnp.where((lane >= HD * a) & (lane < HD * (a + 1)), scale, 0.0).astype(BF16) for a in range(2)]


def _conv_taps(u_prev, u, scr):
    n = u.shape[0]
    scr[0:16, :] = u_prev
    scr[16:16 + n, :] = u
    return scr[15:15 + n, :], scr[14:14 + n, :]


def fwd_inproj(x, g, w_all):
    t = x.shape[0]
    wc = PROJ // NCHIP

    def body(x_ref, g_ref, w_hbm, o_ref, w_v):
        @pl.when(pl.program_id(0) == 0)
        def _():
            pltpu.sync_copy(w_hbm, w_v)

        h = _rms(x_ref[...], g_ref[...]).astype(BF16)
        for b in range(NCHIP):
            o_ref[:, wc * b:wc * (b + 1)] = jnp.dot(h, w_v[b], preferred_element_type=F32).astype(BF16)

    return pl.pallas_call(
        body, grid=(t // TQ,),
        in_specs=[pl.BlockSpec((TQ, D), lambda i: (i, 0)), _const((1, D)), _any()],
        out_specs=pl.BlockSpec((TQ, PROJ), lambda i: (i, 0)),
        out_shape=jax.ShapeDtypeStruct((t, PROJ), BF16),
        scratch_shapes=[pltpu.VMEM((NCHIP, D, wc), BF16)],
        compiler_params=_cp(("arbitrary",)), name="fwd_inproj")(x, g, w_all)


def _attn_window_specs():
    return [
        pl.BlockSpec((TQ, CW), lambda i: (i, 3)),
        pl.BlockSpec((TQ, CW), lambda i: (jnp.maximum(i - 1, 0), 4)),
        pl.BlockSpec((TQ, CW), lambda i: (i, 4)),
        pl.BlockSpec((TQ, CW), lambda i: (jnp.maximum(i - 1, 0), 5)),
        pl.BlockSpec((TQ, CW), lambda i: (i, 5)),
    ]


def _conv_specs():
    return [
        pl.BlockSpec((TQ, 3 * CW), lambda i: (i, 0)),
        pl.BlockSpec((16, 3 * CW), lambda i: (jnp.maximum(i * (TQ // 16) - 1, 0), 0)),
    ]


def _conv_fwd(pc_ref, pcp_ref, wc_ref, scr, first):
    pc = pc_ref[...].astype(F32)
    hc, bg, cg = pc[:, :CW], pc[:, CW:2 * CW], pc[:, 2 * CW:]
    u = cg * hc
    pp = pcp_ref[...].astype(F32)
    u_prev = jnp.where(first, 0.0, pp[:, 2 * CW:] * pp[:, :CW])
    u1, u2 = _conv_taps(u_prev, u, scr)
    cout = wc_ref[0:1, :] * u2 + wc_ref[1:2, :] * u1 + wc_ref[2:3, :] * u
    return hc, bg, cg, u, u1, u2, cout


def _key_penalty(first, r0, kg):
    col = lax.broadcasted_iota(jnp.int32, (1, kg), 1)
    limit = jnp.where(first, TQ - r0, 0)
    return jnp.where(col < limit, NEG_INF, 0.0)


def fwd_mix(x, proj, bias2, wconv_t, g_co, g_ao, g_pm, gm, wout_all):
    t = x.shape[0]
    qg, kg = QG_FWD, QG_FWD + LEFT

    def body(x_ref, pc_ref, pcp_ref, q_ref, kp_ref, kc_ref, vp_ref, vc_ref, b2_ref, wc_ref, gco_ref, gao_ref, gpm_ref,
             gm_ref, wout_hbm, xmid_ref, o_ref, lse_ref, y_ref, z_ref, wout_v, kwin, vwin, cscr):
        i = pl.program_id(0)
        first = i == 0

        @pl.when(first)
        def _():
            pltpu.sync_copy(wout_hbm, wout_v)

        kwin[0:TQ, :] = kp_ref[...]
        kwin[TQ:2 * TQ, :] = kc_ref[...]
        vwin[0:TQ, :] = vp_ref[...]
        vwin[TQ:2 * TQ, :] = vc_ref[...]
        qmask = _head_masks(HD ** -0.5)
        low = lax.broadcasted_iota(jnp.int32, (1, LANES), 1) < HD

        def group(g, carry):
            r0 = pl.multiple_of(g * qg, qg)
            pen = _key_penalty(first, r0, kg)
            for hp in range(NH // 2):
                ls = slice(LANES * hp, LANES * (hp + 1))
                qb = q_ref[pl.ds(r0, qg), ls]
                q2 = jnp.concatenate([qb * qmask[0], qb * qmask[1]], axis=0)
                s = lax.dot_general(q2, kwin[pl.ds(r0, kg), ls], NT, preferred_element_type=F32)
                s = s + b2_ref[hp] + pen
                m = jnp.max(s, axis=-1, keepdims=True)
                p = jnp.exp(s - m)
                l = jnp.sum(p, axis=-1, keepdims=True)
                o2 = jnp.dot(p.astype(BF16), vwin[pl.ds(r0, kg), ls], preferred_element_type=F32) * (1.0 / l)
                lse2 = m + jnp.log(l)
                o_ref[pl.ds(r0, qg), ls] = jnp.where(low, o2[:qg], o2[qg:])
                lse_ref[pl.ds(r0, qg), ls] = jnp.where(low, lse2[:qg], lse2[qg:])
            return carry

        lax.fori_loop(0, TQ // qg, group, 0)

        _, bg, _, _, _, _, cout = _conv_fwd(pc_ref, pcp_ref, wc_ref, cscr, first)
        yc = bg * cout
        gmv = gm_ref[...]
        ycn = yc * lax.rsqrt(_group_mean(yc * yc, gmv) + EPS) * gco_ref[...]
        oa = o_ref[...]
        oan = oa * lax.rsqrt(_group_mean(oa * oa, gmv) + EPS) * gao_ref[...]
        y_ref[:, 0:CW] = ycn.astype(BF16)
        y_ref[:, CW:2 * CW] = oan.astype(BF16)
        z = jnp.dot(y_ref[...], wout_v[...], preferred_element_type=F32)
        z_ref[...] = z
        xmid_ref[...] = x_ref[...] + _rms(z, gpm_ref[...])

    row = lambda w: pl.BlockSpec((TQ, w), lambda i: (i, 0))
    return pl.pallas_call(
        body, grid=(t // TQ,),
        in_specs=[row(D)] + _conv_specs() + _attn_window_specs() + [
            _const((NH // 2, 2 * qg, kg)), _const((8, CW)), _const((1, CW)), _const((1, CW)), _const((1, D)),
            _const((CW, CW)), _any()],
        out_specs=[row(D), row(CW), row(CW), row(D), row(D)],
        out_shape=[jax.ShapeDtypeStruct((t, D), F32), jax.ShapeDtypeStruct((t, CW), F32),
                   jax.ShapeDtypeStruct((t, CW), F32), jax.ShapeDtypeStruct((t, D), BF16),
                   jax.ShapeDtypeStruct((t, D), F32)],
        scratch_shapes=[pltpu.VMEM((D, D), BF16), pltpu.VMEM((2 * TQ, CW), BF16), pltpu.VMEM((2 * TQ, CW), BF16),
                        pltpu.VMEM((TQ + 16, CW), F32)],
        compiler_params=_cp(("arbitrary",)), name="fwd_mix",
    )(x, proj, proj, proj, proj, proj, proj, proj, bias2, wconv_t, g_co, g_ao, g_pm, gm, wout_all)


def fwd_ffn(xmid, g_pre, g_post, wfi_all, wfo_all):
    t = xmid.shape[0]
    hw = DFF // 2

    def body(x_ref, gpre_ref, gpost_ref, wfi_hbm, wfo_hbm, gu_ref, f_ref, xo_ref, wfi_v, wfo_v):
        @pl.when(pl.program_id(0) == 0)
        def _():
            pltpu.sync_copy(wfi_hbm, wfi_v)
            pltpu.sync_copy(wfo_hbm, wfo_v)

        xv = x_ref[...]
        h = _rms(xv, gpre_ref[...]).astype(BF16)
        f = jnp.zeros((TM, D), F32)
        for j in range(2):
            gate = jnp.dot(h, wfi_v[j], preferred_element_type=F32)
            up = jnp.dot(h, wfi_v[2 + j], preferred_element_type=F32)
            gu_ref[:, hw * j:hw * (j + 1)] = gate.astype(BF16)
            gu_ref[:, DFF + hw * j:DFF + hw * (j + 1)] = up.astype(BF16)
            act = gate * (1.0 / (1.0 + jnp.exp(-gate))) * up
            f = f + jnp.dot(act.astype(BF16), wfo_v[j], preferred_element_type=F32)
        f_ref[...] = f
        xo_ref[...] = xv + _rms(f, gpost_ref[...])

    row = lambda w: pl.BlockSpec((TM, w), lambda i: (i, 0))
    return pl.pallas_call(
        body, grid=(t // TM,),
        in_specs=[row(D), _const((1, D)), _const((1, D)), _any(), _any()],
        out_specs=[row(2 * DFF), row(D), row(D)],
        out_shape=[jax.ShapeDtypeStruct((t, 2 * DFF), BF16), jax.ShapeDtypeStruct((t, D), F32),
                   jax.ShapeDtypeStruct((t, D), F32)],
        scratch_shapes=[pltpu.VMEM((NCHIP, D, hw), BF16), pltpu.VMEM((2, hw, D), BF16)],
        compiler_params=_cp(("arbitrary",)), name="fwd_ffn")(xmid, g_pre, g_post, wfi_all, wfo_all)


def loss_head(y, target):
    t = y.shape[0]

    def body(y_ref, t_ref, dy_ref, l_ref):
        @pl.when(pl.program_id(0) == 0)
        def _():
            l_ref[...] = jnp.zeros_like(l_ref)

        e = y_ref[...] - t_ref[...]
        dy_ref[...] = e * (1.0 / D)
        rows = jnp.sum(e * e, axis=-1, keepdims=True) * (1.0 / D)
        l_ref[...] += 0.5 * jnp.sum(rows, axis=0, keepdims=True)

    row = pl.BlockSpec((TQ, D), lambda i: (i, 0))
    return pl.pallas_call(
        body, grid=(t // TQ,), in_specs=[row, row], out_specs=[row, _const((8, LANES))],
        out_shape=[jax.ShapeDtypeStruct((t, D), F32), jax.ShapeDtypeStruct((8, LANES), F32)],
        compiler_params=_cp(("arbitrary",)), name="loss_head")(y, target)


def bwd_ffn(dx, f, xmid, gu, g_pre, g_post, wfi_all, wfo_all):
    t = dx.shape[0]
    hw = DFF // 2

    def body(dx_ref, f_ref, x_ref, gu_ref, gpre_ref, gpost_ref, wfi_hbm, wfo_hbm,
             dxm_ref, df_ref, act_ref, dgu_ref, h_ref, dgpost_ref, dgpre_ref, wfi_v, wfo_v):
        @pl.when(pl.program_id(0) == 0)
        def _():
            pltpu.sync_copy(wfi_hbm, wfi_v)
            pltpu.sync_copy(wfo_hbm, wfo_v)
            dgpost_ref[...] = jnp.zeros_like(dgpost_ref)
            dgpre_ref[...] = jnp.zeros_like(dgpre_ref)

        dxo = dx_ref[...]
        df, dgp = _rms_bwd(dxo, f_ref[...], gpost_ref[...])
        dgpost_ref[...] += dgp
        dfb = df.astype(BF16)
        df_ref[...] = dfb
        dh = jnp.zeros((TM, D), F32)
        for j in range(2):
            dact = lax.dot_general(dfb, wfo_v[j], NT, preferred_element_type=F32)
            gate = gu_ref[:, hw * j:hw * (j + 1)].astype(F32)
            up = gu_ref[:, DFF + hw * j:DFF + hw * (j + 1)].astype(F32)
            sig = 1.0 / (1.0 + jnp.exp(-gate))
            silu = gate * sig
            act_ref[:, hw * j:hw * (j + 1)] = (silu * up).astype(BF16)
            dup = (dact * silu).astype(BF16)
            dgate = (dact * up * (sig * (1.0 + gate * (1.0 - sig)))).astype(BF16)
            dgu_ref[:, hw * j:hw * (j + 1)] = dgate
            dgu_ref[:, DFF + hw * j:DFF + hw * (j + 1)] = dup
            dh = dh + lax.dot_general(dgate, wfi_v[j], NT, preferred_element_type=F32)
            dh = dh + lax.dot_general(dup, wfi_v[2 + j], NT, preferred_element_type=F32)
        xv = x_ref[...]
        gpre = gpre_ref[...]
        h_ref[...] = _rms(xv, gpre).astype(BF16)
        dxv, dgq = _rms_bwd(dh, xv, gpre)
        dgpre_ref[...] += dgq
        dxm_ref[...] = dxo + dxv

    row = lambda w: pl.BlockSpec((TM, w), lambda i: (i, 0))
    return pl.pallas_call(
        body, grid=(t // TM,),
        in_specs=[row(D), row(D), row(D), row(2 * DFF), _const((1, D)), _const((1, D)), _any(), _any()],
        out_specs=[row(D), row(D), row(DFF), row(2 * DFF), row(D), _const((1, D)), _const((1, D))],
        out_shape=[jax.ShapeDtypeStruct((t, D), F32), jax.ShapeDtypeStruct((t, D), BF16),
                   jax.ShapeDtypeStruct((t, DFF), BF16), jax.ShapeDtypeStruct((t, 2 * DFF), BF16),
                   jax.ShapeDtypeStruct((t, D), BF16), jax.ShapeDtypeStruct((1, D), F32),
                   jax.ShapeDtypeStruct((1, D), F32)],
        scratch_shapes=[pltpu.VMEM((NCHIP, D, hw), BF16), pltpu.VMEM((2, hw, D), BF16)],
        compiler_params=_cp(("arbitrary",)), name="bwd_ffn")(dx, f, xmid, gu, g_pre, g_post, wfi_all, wfo_all)


def bwd_mix(dxm, z, o, proj, wconv_t, g_co, g_ao, g_pm, gm, wout_all):
    t = dxm.shape[0]

    def body(dx_ref, z_ref, o_ref, pc_ref, pcp_ref, wc_ref, gco_ref, gao_ref, gpm_ref, gm_ref, wout_hbm,
             dz_ref, do_ref, dco_ref, dbg_ref, dgpm_ref, dgco_ref, dgao_ref, wout_v, cscr):
        first = pl.program_id(0) == 0

        @pl.when(first)
        def _():
            pltpu.sync_copy(wout_hbm, wout_v)
            dgpm_ref[...] = jnp.zeros_like(dgpm_ref)
            dgco_ref[...] = jnp.zeros_like(dgco_ref)
            dgao_ref[...] = jnp.zeros_like(dgao_ref)

        dz, dgp = _rms_bwd(dx_ref[...], z_ref[...], gpm_ref[...])
        dgpm_ref[...] += dgp
        dzb = dz.astype(BF16)
        dz_ref[...] = dzb
        dy = lax.dot_general(dzb, wout_v[...], NT, preferred_element_type=F32)
        gmv = gm_ref[...]
        _, bg, _, _, _, _, cout = _conv_fwd(pc_ref, pcp_ref, wc_ref, cscr, first)
        dyc, dgc = _group_rms_bwd(dy[:, :CW], bg * cout, gco_ref[...], gmv)
        dgco_ref[...] += dgc
        dbg_ref[...] = (dyc * cout).astype(BF16)
        dco_ref[...] = dyc * bg
        do, dga = _group_rms_bwd(dy[:, CW:], o_ref[...], gao_ref[...], gmv)
        dgao_ref[...] += dga
        do_ref[...] = do.astype(BF16)

    row = lambda w: pl.BlockSpec((TQ, w), lambda i: (i, 0))
    return pl.pallas_call(
        body, grid=(t // TQ,),
        in_specs=[row(D), row(D), row(CW)] + _conv_specs() + [
            _const((8, CW)), _const((1, CW)), _const((1, CW)), _const((1, D)), _const((CW, CW)), _any()],
        out_specs=[row(D), row(CW), row(CW), row(CW), _const((1, D)), _const((1, CW)), _const((1, CW))],
        out_shape=[jax.ShapeDtypeStruct((t, D), BF16), jax.ShapeDtypeStruct((t, CW), BF16),
                   jax.ShapeDtypeStruct((t, CW), F32), jax.ShapeDtypeStruct((t, CW), BF16),
                   jax.ShapeDtypeStruct((1, D), F32), jax.ShapeDtypeStruct((1, CW), F32),
                   jax.ShapeDtypeStruct((1, CW), F32)],
        scratch_shapes=[pltpu.VMEM((D, D), BF16), pltpu.VMEM((TQ + 16, CW), F32)],
        compiler_params=_cp(("arbitrary",)), name="bwd_mix",
    )(dxm, z, o, proj, proj, wconv_t, g_co, g_ao, g_pm, gm, wout_all)


def bwd_conv(dco, proj, wconv_t):
    t = dco.shape[0]
    nt = t // TQ

    def body(d_ref, dn_ref, pc_ref, pcp_ref, wc_ref, dhc_ref, dcg_ref, dw_ref, cscr, dscr):
        i = pl.program_id(0)
        first = i == 0

        @pl.when(first)
        def _():
            dw_ref[...] = jnp.zeros_like(dw_ref)

        hc, _, cg, u, u1, u2, _ = _conv_fwd(pc_ref, pcp_ref, wc_ref, cscr, first)
        d0 = d_ref[...]
        dscr[0:TQ, :] = d0
        dscr[TQ:TQ + 8, :] = jnp.where(i == nt - 1, 0.0, dn_ref[...])
        d1 = dscr[1:TQ + 1, :]
        d2 = dscr[2:TQ + 2, :]
        du = wc_ref[2:3, :] * d0 + wc_ref[1:2, :] * d1 + wc_ref[0:1, :] * d2
        dhc_ref[...] = (du * cg).astype(BF16)
        dcg_ref[...] = (du * hc).astype(BF16)
        dw_ref[0:1, :] += jnp.sum(d0 * u2, axis=0, keepdims=True)
        dw_ref[1:2, :] += jnp.sum(d0 * u1, axis=0, keepdims=True)
        dw_ref[2:3, :] += jnp.sum(d0 * u, axis=0, keepdims=True)

    row = lambda w: pl.BlockSpec((TQ, w), lambda i: (i, 0))
    nxt = pl.BlockSpec((8, CW), lambda i: (jnp.minimum((i + 1) * (TQ // 8), t // 8 - 1), 0))
    return pl.pallas_call(
        body, grid=(nt,),
        in_specs=[row(CW), nxt] + _conv_specs() + [_const((8, CW))],
        out_specs=[row(CW), row(CW), _const((8, CW))],
        out_shape=[jax.ShapeDtypeStruct((t, CW), BF16), jax.ShapeDtypeStruct((t, CW), BF16),
                   jax.ShapeDtypeStruct((8, CW), F32)],
        scratch_shapes=[pltpu.VMEM((TQ + 16, CW), F32), pltpu.VMEM((TQ + 8, CW), F32)],
        compiler_params=_cp(("arbitrary",)), name="bwd_conv")(dco, dco, proj, proj, wconv_t)


def bwd_attn(proj, o, do, lse, bias2):
    t = o.shape[0]
    nt = t // TQ
    qg, kg = QG_BWD, QG_BWD + LEFT

    def body(q_ref, kp_ref, kc_ref, vp_ref, vc_ref, o_ref, do_ref, lse_ref, b2_ref,
             dq_ref, dk_hbm, dv_hbm, db_hbm, kwin, vwin, dk_acc, dv_acc, db_acc):
        i = pl.program_id(0)
        first = i == 0

        @pl.when(first)
        def _():
            dk_acc[...] = jnp.zeros_like(dk_acc)
            dv_acc[...] = jnp.zeros_like(dv_acc)
            db_acc[...] = jnp.zeros_like(db_acc)

        kwin[0:TQ, :] = kp_ref[...]
        kwin[TQ:2 * TQ, :] = kc_ref[...]
        vwin[0:TQ, :] = vp_ref[...]
        vwin[TQ:2 * TQ, :] = vc_ref[...]
        scale = HD ** -0.5
        qmask = _head_masks(scale)
        vmask = _head_masks(1.0)
        low = lax.broadcasted_iota(jnp.int32, (1, LANES), 1) < HD

        def group(g, carry):
            r0 = pl.multiple_of(g * qg, qg)
            base = pl.multiple_of(i * TQ + r0, qg)
            pen = _key_penalty(first, r0, kg)
            for hp in range(NH // 2):
                ls = slice(LANES * hp, LANES * (hp + 1))
                qb = q_ref[pl.ds(r0, qg), ls]
                kw = kwin[pl.ds(r0, kg), ls]
                dob = do_ref[pl.ds(r0, qg), ls]
                prod = dob.astype(F32) * o_ref[pl.ds(r0, qg), ls]
                lseb = lse_ref[pl.ds(r0, qg), ls]
                q2 = jnp.concatenate([qb * qmask[0], qb * qmask[1]], axis=0)
                do2 = jnp.concatenate([dob * vmask[0], dob * vmask[1]], axis=0)
                lse2 = jnp.concatenate([lseb[:, 0:1], lseb[:, HD:HD + 1]], axis=0)
                dsum = jnp.concatenate([jnp.sum(jnp.where(low, prod, 0.0), axis=-1, keepdims=True),
                                        jnp.sum(jnp.where(low, 0.0, prod), axis=-1, keepdims=True)], axis=0)
                s = lax.dot_general(q2, kw, NT, preferred_element_type=F32) + b2_ref[hp] + pen
                p = jnp.exp(s - lse2)
                dp = lax.dot_general(do2, vwin[pl.ds(r0, kg), ls], NT, preferred_element_type=F32)
                ds = p * (dp - dsum)
                db_acc[hp] += ds
                dsb = ds.astype(BF16)
                dq2 = jnp.dot(dsb, kw, preferred_element_type=F32)
                dq_ref[pl.ds(r0, qg), ls] = (jnp.where(low, dq2[:qg], dq2[qg:]) * scale).astype(BF16)
                dk_acc[pl.ds(base, kg), ls] += lax.dot_general(dsb, q2, TN, preferred_element_type=F32)
                dv_acc[pl.ds(base, kg), ls] += lax.dot_general(p.astype(BF16), do2, TN, preferred_element_type=F32)
            return carry

        lax.fori_loop(0, TQ // qg, group, 0)

        @pl.when(i == nt - 1)
        def _():
            pltpu.sync_copy(dk_acc, dk_hbm)
            pltpu.sync_copy(dv_acc, dv_hbm)
            pltpu.sync_copy(db_acc, db_hbm)

    row = lambda w: pl.BlockSpec((TQ, w), lambda i: (i, 0))
    return pl.pallas_call(
        body, grid=(nt,),
        in_specs=_attn_window_specs() + [row(CW), row(CW), row(CW), _const((NH // 2, 2 * qg, kg))],
        out_specs=[row(CW), _any(), _any(), _any()],
        out_shape=[jax.ShapeDtypeStruct((t, CW), BF16), jax.ShapeDtypeStruct((t + TQ, CW), F32),
                   jax.ShapeDtypeStruct((t + TQ, CW), F32), jax.ShapeDtypeStruct((NH // 2, 2 * qg, kg), F32)],
        scratch_shapes=[pltpu.VMEM((2 * TQ, CW), BF16), pltpu.VMEM((2 * TQ, CW), BF16),
                        pltpu.VMEM((t + TQ, CW), F32), pltpu.VMEM((t + TQ, CW), F32),
                        pltpu.VMEM((NH // 2, 2 * qg, kg), F32)],
        compiler_params=_cp(("arbitrary",)), name="bwd_attn",
    )(proj, proj, proj, proj, proj, o, do, lse, bias2)


def bwd_inproj(dxm, x, dhc, dbg, dcg, dq, dk, dv, g, w_all):
    t = x.shape[0]
    wc = PROJ // NCHIP

    def body(dxm_ref, x_ref, dhc_ref, dbg_ref, dcg_ref, dq_ref, dk_ref, dv_ref, g_ref, w_hbm,
             dx_ref, dp_ref, h_ref, dg_ref, w_v):
        @pl.when(pl.program_id(0) == 0)
        def _():
            pltpu.sync_copy(w_hbm, w_v)
            dg_ref[...] = jnp.zeros_like(dg_ref)

        dp_ref[:, 0:CW] = dhc_ref[...]
        dp_ref[:, CW:2 * CW] = dbg_ref[...]
        dp_ref[:, 2 * CW:3 * CW] = dcg_ref[...]
        dp_ref[:, 3 * CW:4 * CW] = dq_ref[...]
        dp_ref[:, 4 * CW:5 * CW] = dk_ref[...].astype(BF16)
        dp_ref[:, 5 * CW:6 * CW] = dv_ref[...].astype(BF16)
        dh = jnp.zeros((TQ, D), F32)
        for b in range(NCHIP):
            dh = dh + lax.dot_general(dp_ref[:, wc * b:wc * (b + 1)], w_v[b], NT, preferred_element_type=F32)
        xv = x_ref[...]
        gv = g_ref[...]
        h_ref[...] = _rms(xv, gv).astype(BF16)
        dxv, dgv = _rms_bwd(dh, xv, gv)
        dg_ref[...] += dgv
        dx_ref[...] = dxm_ref[...] + dxv

    row = lambda w: pl.BlockSpec((TQ, w), lambda i: (i, 0))
    pad = pl.BlockSpec((TQ, CW), lambda i: (i + 1, 0))
    return pl.pallas_call(
        body, grid=(t // TQ,),
        in_specs=[row(D), row(D), row(CW), row(CW), row(CW), row(CW), pad, pad, _const((1, D)), _any()],
        out_specs=[row(D), row(PROJ), row(D), _const((1, D))],
        out_shape=[jax.ShapeDtypeStruct((t, D), F32), jax.ShapeDtypeStruct((t, PROJ), BF16),
                   jax.ShapeDtypeStruct((t, D), BF16), jax.ShapeDtypeStruct((1, D), F32)],
        scratch_shapes=[pltpu.VMEM((NCHIP, D, wc), BF16)],
        compiler_params=_cp(("arbitrary",)), name="bwd_inproj",
    )(dxm, x, dhc, dbg, dcg, dq, dk, dv, g, w_all)


def wgrad(a, b, kb, nb, by_columns, name):
    t, k = a.shape
    n = b.shape[1]
    tk = 512

    def body(a_ref, b_ref, o_ref):
        o_ref[...] = jnp.zeros_like(o_ref)
        for c in range(t // tk):
            o_ref[...] += lax.dot_general(a_ref[tk * c:tk * (c + 1), :], b_ref[tk * c:tk * (c + 1), :], TN,
                                          preferred_element_type=F32)

    if by_columns:
        assert nb == n // NCHIP
        out_spec = pl.BlockSpec((None, kb, nb), lambda ki, ni: (ni, ki, 0))
        out_shape = jax.ShapeDtypeStruct((NCHIP, k, nb), F32)
    else:
        assert nb == n
        out_spec = pl.BlockSpec((kb, nb), lambda ki, ni: (ki, 0))
        out_shape = jax.ShapeDtypeStruct((k, n), F32)
    return pl.pallas_call(
        body, grid=(k // kb, n // nb),
        in_specs=[pl.BlockSpec((t, kb), lambda ki, ni: (0, ki)), pl.BlockSpec((t, nb), lambda ki, ni: (0, ni))],
        out_specs=out_spec, out_shape=out_shape,
        compiler_params=_cp(("arbitrary", "arbitrary")), name=name)(a, b)


TOE = 1024
assert 2 * QG_FWD + LEFT <= TOE
N_FLAT = LEFT - REL_CLIP + 1
N_VAR = BAND - N_FLAT


def _diag_vector(table):
    last = table[:, 2 * REL_CLIP:]
    var = table[:, 2 * REL_CLIP - N_VAR:2 * REL_CLIP][:, ::-1]
    return jnp.concatenate([jnp.broadcast_to(last, (NH, N_FLAT)), var, jnp.broadcast_to(last, (NH, TOE - BAND))], axis=1)


def _diag_vector_bwd(dvec):
    dlast = jnp.sum(dvec[:, :N_FLAT], axis=1, keepdims=True) + jnp.sum(dvec[:, BAND:], axis=1, keepdims=True)
    dvar = dvec[:, N_FLAT:BAND][:, ::-1]
    return jnp.concatenate([jnp.zeros((NH, 2 * REL_CLIP - N_VAR), F32), dvar, dlast], axis=1)


def _band_valid(qg):
    r = lax.broadcasted_iota(jnp.int32, (qg, qg + LEFT), 0)
    p = lax.broadcasted_iota(jnp.int32, (qg, qg + LEFT), 1)
    start = lax.shift_left(lax.shift_right_logical(r, 6), 6)
    return (p >= start) & (p < start + BAND)


def bias_expand(vec, qg):
    def body(v_ref, o_ref):
        valid = _band_valid(qg)
        for h in range(NH):
            rows = jnp.broadcast_to(v_ref[h:h + 1, :], (qg, TOE))
            toe = pltpu.roll(rows, 0, 1, stride=1, stride_axis=0)
            o_ref[h // 2, qg * (h % 2):qg * (h % 2 + 1), :] = jnp.where(valid, toe[:, :qg + LEFT], NEG_INF)

    return pl.pallas_call(body, out_shape=jax.ShapeDtypeStruct((NH // 2, 2 * qg, qg + LEFT), F32),
                          name="bias_expand")(vec)


def bias_reduce(db2):
    _, qg, kg = db2.shape

    def body(d_ref, o_ref):
        for h in range(NH):
            d = jnp.concatenate([jnp.zeros((qg, TOE - kg), F32), d_ref[h]], axis=1)
            back = pltpu.roll(d, 0, 1, stride=1, stride_axis=0)
            o_ref[h:h + 1, :] = jnp.sum(back, axis=0, keepdims=True)

    rev = pl.pallas_call(body, out_shape=jax.ShapeDtypeStruct((NH, TOE), F32), name="bias_reduce")(db2[:, :, ::-1])
    return rev[:, ::-1]


def _place():
    x, y, c = lax.axis_index("x"), lax.axis_index("y"), lax.axis_index("c")
    chips = [(1 - x, y), (x, 1 - y), (1 - x, 1 - y)]
    return x, y, c, chips


def _half(ref_rows, c):
    return pl.ds(c * (ref_rows // 2), ref_rows // 2)


HBM_SPEC = pl.BlockSpec(memory_space=pltpu.HBM)
SEM_SPEC = pl.BlockSpec(memory_space=pltpu.SEMAPHORE)
IN_FLIGHT = pltpu.CompilerParams(has_side_effects=pltpu.SideEffectType.DATAFLOW_SIDE_EFFECTING)


def _in_hbm(a):
    return pltpu.with_memory_space_constraint(a, pltpu.HBM)


def cast_to_slot(w, chip, layer):
    _, rows, cols = w.shape
    rb = rows // 4

    def body(b_ref, w_ref, o_ref):
        del b_ref
        o_ref[...] = w_ref[...].astype(BF16)

    grid_spec = pltpu.PrefetchScalarGridSpec(
        num_scalar_prefetch=1, grid=(rows // rb,),
        in_specs=[pl.BlockSpec((None, rb, cols), lambda r, b: (layer, r, 0))],
        out_specs=pl.BlockSpec((None, rb, cols), lambda r, b: (b[0], r, 0)))
    return pl.pallas_call(body, grid_spec=grid_spec, out_shape=jax.ShapeDtypeStruct((NCHIP, rows, cols), BF16),
                          compiler_params=_cp(("arbitrary",)), name="cast_to_slot")(chip, w)


def _gather_copies(bufs, send, recv):
    x, y, c, chips = _place()
    b = 2 * x + y
    out = []
    for k, buf in enumerate(bufs):
        rows = buf.shape[1]
        mine = buf.at[b, _half(rows, c), :]
        for j, (cx, cy) in enumerate(chips):
            theirs = buf.at[2 * cx + cy, _half(rows, c), :]
            sems = dict(send_sem=send.at[3 * k + j], recv_sem=recv.at[3 * k + j],
                        device_id=(cx, cy, c), device_id_type=MESH)
            out.append((pltpu.make_async_remote_copy(src_ref=mine, dst_ref=mine, **sems),
                        pltpu.make_async_remote_copy(src_ref=theirs, dst_ref=theirs, **sems)))
    return out


def gather_start(bufs, after, layer):
    n = len(bufs)

    def body(*refs):
        ins = refs[:n]
        send, recv = refs[n + 1], refs[n + 2]
        token = refs[-1]
        for start, _ in _gather_copies(ins, send, recv):
            start.start()
        token[...] = jnp.zeros_like(token)

    sems = pltpu.SemaphoreType.DMA((3 * n,))
    res = pl.pallas_call(
        body, name=f"gather_start_{layer}",
        in_specs=[HBM_SPEC] * n + [_any()],
        out_specs=[SEM_SPEC, SEM_SPEC] + [HBM_SPEC] * n + [pl.BlockSpec(memory_space=pltpu.VMEM)],
        out_shape=[sems, sems] + [pltpu.HBM(b.shape, b.dtype) for b in bufs] + [jax.ShapeDtypeStruct((8, LANES), F32)],
        input_output_aliases={k: 2 + k for k in range(n)}, compiler_params=IN_FLIGHT,
    )(*[_in_hbm(b) for b in bufs], after)
    return res[0], res[1], res[2:2 + n], res[-1]


def gather_wait(send, recv, bufs, after, layer):
    n = len(bufs)

    def body(*refs):
        ins = refs[:n]
        send_ref, recv_ref = refs[n], refs[n + 1]
        for start, arrival in _gather_copies(ins, send_ref, recv_ref):
            start.wait_send()
            arrival.wait_recv()

    return pl.pallas_call(
        body, name=f"gather_wait_{layer}",
        in_specs=[HBM_SPEC] * n + [SEM_SPEC, SEM_SPEC, _any()], out_specs=[HBM_SPEC] * n,
        out_shape=[pltpu.HBM(b.shape, b.dtype) for b in bufs],
        input_output_aliases={k: k for k in range(n)}, compiler_params=IN_FLIGHT,
    )(*bufs, send, recv, after)


def gather_forward(bufs):
    n = len(bufs)

    def body(*refs):
        outs = refs[n:2 * n]
        send, recv = refs[2 * n:]
        x, y, c, chips = _place()
        cps = []
        for k in range(n):
            rows = outs[k].shape[1]
            for j, (cx, cy) in enumerate(chips):
                sems = dict(send_sem=send.at[3 * k + j], recv_sem=recv.at[3 * k + j],
                            device_id=(x, y, 1 - c), device_id_type=MESH)
                mine = outs[k].at[2 * cx + cy, _half(rows, c), :]
                theirs = outs[k].at[2 * cx + cy, _half(rows, 1 - c), :]
                cp = pltpu.make_async_remote_copy(src_ref=mine, dst_ref=mine, **sems)
                cp.start()
                cps.append((cp, pltpu.make_async_remote_copy(src_ref=theirs, dst_ref=theirs, **sems)))
        for cp, arrival in cps:
            cp.wait_send()
            arrival.wait_recv()

    return pl.pallas_call(
        body, in_specs=[_any()] * n, out_specs=[_any()] * n,
        out_shape=[jax.ShapeDtypeStruct(b.shape, b.dtype) for b in bufs], input_output_aliases={k: k for k in range(n)},
        scratch_shapes=[pltpu.SemaphoreType.DMA((3 * n,)), pltpu.SemaphoreType.DMA((3 * n,))],
        name="gather_forward")(*bufs)


def _forward_copies(bufs, send, recv):
    x, y, c, chips = _place()
    out = []
    for k, buf in enumerate(bufs):
        rows = buf.shape[1]
        for j, (cx, cy) in enumerate(chips):
            sems = dict(send_sem=send.at[3 * k + j], recv_sem=recv.at[3 * k + j],
                        device_id=(x, y, 1 - c), device_id_type=MESH)
            mine = buf.at[2 * cx + cy, _half(rows, c), :]
            theirs = buf.at[2 * cx + cy, _half(rows, 1 - c), :]
            out.append((pltpu.make_async_remote_copy(src_ref=mine, dst_ref=mine, **sems),
                        pltpu.make_async_remote_copy(src_ref=theirs, dst_ref=theirs, **sems)))
    return out


def forward_start(bufs, tag):
    n = len(bufs)

    def body(*refs):
        ins = refs[:n]
        send, recv = refs[n], refs[n + 1]
        token = refs[-1]
        for start, _ in _forward_copies(ins, send, recv):
            start.start()
        token[...] = jnp.zeros_like(token)

    sems = pltpu.SemaphoreType.DMA((3 * n,))
    res = pl.pallas_call(
        body, name=f"forward_start_{tag}", in_specs=[HBM_SPEC] * n,
        out_specs=[SEM_SPEC, SEM_SPEC] + [HBM_SPEC] * n + [pl.BlockSpec(memory_space=pltpu.VMEM)],
        out_shape=[sems, sems] + [pltpu.HBM(b.shape, b.dtype) for b in bufs] + [jax.ShapeDtypeStruct((8, LANES), F32)],
        input_output_aliases={k: 2 + k for k in range(n)}, compiler_params=IN_FLIGHT,
    )(*[_in_hbm(b) for b in bufs])
    return res[0], res[1], res[2:2 + n], res[-1]


def forward_wait(send, recv, bufs, after, tag):
    n = len(bufs)

    def body(*refs):
        ins = refs[:n]
        send_ref, recv_ref = refs[n], refs[n + 1]
        for start, arrival in _forward_copies(ins, send_ref, recv_ref):
            start.wait_send()
            arrival.wait_recv()

    return pl.pallas_call(
        body, name=f"forward_wait_{tag}",
        in_specs=[HBM_SPEC] * n + [SEM_SPEC, SEM_SPEC, _any()], out_specs=[HBM_SPEC] * n,
        out_shape=[pltpu.HBM(b.shape, b.dtype) for b in bufs],
        input_output_aliases={k: k for k in range(n)}, compiler_params=IN_FLIGHT,
    )(*bufs, send, recv, after)


def _exchange_copies(srcs, lands, send, recv):
    x, y, c, _ = _place()
    return [pltpu.make_async_remote_copy(
        src_ref=src.at[:, _half(src.shape[1], 1 - c), :], dst_ref=land, send_sem=send.at[k], recv_sem=recv.at[k],
        device_id=(x, y, 1 - c), device_id_type=MESH) for k, (src, land) in enumerate(zip(srcs, lands))]


def exchange_start(srcs, tag):
    n = len(srcs)
    lands = [lax.empty((s.shape[0], s.shape[1] // 2, s.shape[2]), s.dtype) for s in srcs]

    def body(*refs):
        ins, land_refs = refs[:n], refs[n:2 * n]
        send, recv = refs[2 * n], refs[2 * n + 1]
        token = refs[-1]
        for cp in _exchange_copies(ins, land_refs, send, recv):
            cp.start()
        token[...] = jnp.zeros_like(token)

    sems = pltpu.SemaphoreType.DMA((n,))
    res = pl.pallas_call(
        body, name=f"exchange_start_{tag}",
        in_specs=[HBM_SPEC] * (2 * n),
        out_specs=[SEM_SPEC, SEM_SPEC] + [HBM_SPEC] * (2 * n) + [pl.BlockSpec(memory_space=pltpu.VMEM)],
        out_shape=[sems, sems] + [pltpu.HBM(a.shape, a.dtype) for a in list(srcs) + lands]
        + [jax.ShapeDtypeStruct((8, LANES), F32)],
        input_output_aliases={k: 2 + k for k in range(2 * n)}, compiler_params=IN_FLIGHT,
    )(*[_in_hbm(a) for a in list(srcs) + lands])
    return res[0], res[1], res[2:2 + n], res[2 + n:2 + 2 * n], res[-1]


def exchange_wait(send, recv, srcs, lands, after, tag):
    n = len(srcs)

    def body(*refs):
        ins, land_refs = refs[:n], refs[n:2 * n]
        send_ref, recv_ref = refs[2 * n], refs[2 * n + 1]
        for cp in _exchange_copies(ins, land_refs, send_ref, recv_ref):
            cp.wait_send()
            cp.wait_recv()

    res = pl.pallas_call(
        body, name=f"exchange_wait_{tag}",
        in_specs=[HBM_SPEC] * (2 * n) + [SEM_SPEC, SEM_SPEC, _any()], out_specs=[HBM_SPEC] * (2 * n),
        out_shape=[pltpu.HBM(a.shape, a.dtype) for a in list(srcs) + list(lands)],
        input_output_aliases={k: k for k in range(2 * n)}, compiler_params=IN_FLIGHT,
    )(*srcs, *lands, send, recv, after)
    return res[:n], res[n:]


def add_pair(g, r1, core):
    ns, rows, cols = g.shape
    hr = rows // 2

    def body(c_ref, g_ref, r_ref, o_ref):
        del c_ref
        o_ref[...] = (g_ref[...] + r_ref[...]).astype(BF16)

    blk = (None, hr, cols)
    grid_spec = pltpu.PrefetchScalarGridSpec(
        num_scalar_prefetch=1, grid=(ns,),
        in_specs=[pl.BlockSpec(blk, lambda s, c: (s, c[0], 0)), pl.BlockSpec(blk, lambda s, c: (s, 0, 0))],
        out_specs=pl.BlockSpec(blk, lambda s, c: (s, 0, 0)))
    return pl.pallas_call(body, grid_spec=grid_spec, out_shape=jax.ShapeDtypeStruct(r1.shape, BF16),
                          compiler_params=_cp(("arbitrary",)), name="add_pair")(core, g, r1)


def _scatter_copies(srcs, lands, send, recv):
    _, _, c, chips = _place()
    out = []
    for k, (src, land) in enumerate(zip(srcs, lands)):
        for j, (cx, cy) in enumerate(chips):
            out.append(pltpu.make_async_remote_copy(
                src_ref=src.at[2 * cx + cy], dst_ref=land.at[j], send_sem=send.at[3 * k + j],
                recv_sem=recv.at[3 * k + j], device_id=(cx, cy, c), device_id_type=MESH))
    return out


def scatter_start(srcs, layer):
    n = len(srcs)
    lands = [lax.empty((3,) + s.shape[1:], s.dtype) for s in srcs]

    def body(*refs):
        ins, land_refs = refs[:n], refs[n:2 * n]
        send, recv = refs[2 * n], refs[2 * n + 1]
        token = refs[-1]
        for cp in _scatter_copies(ins, land_refs, send, recv):
            cp.start()
        token[...] = jnp.zeros_like(token)

    sems = pltpu.SemaphoreType.DMA((3 * n,))
    res = pl.pallas_call(
        body, name=f"scatter_start_{layer}",
        in_specs=[HBM_SPEC] * (2 * n),
        out_specs=[SEM_SPEC, SEM_SPEC] + [HBM_SPEC] * (2 * n) + [pl.BlockSpec(memory_space=pltpu.VMEM)],
        out_shape=[sems, sems] + [pltpu.HBM(a.shape, a.dtype) for a in srcs + lands]
        + [jax.ShapeDtypeStruct((8, LANES), F32)],
        input_output_aliases={k: 2 + k for k in range(2 * n)}, compiler_params=IN_FLIGHT,
    )(*[_in_hbm(a) for a in srcs + lands])
    return res[0], res[1], res[2:2 + n], res[2 + n:2 + 2 * n], res[-1]


def scatter_wait(send, recv, srcs, lands, after, layer):
    n = len(srcs)

    def body(*refs):
        ins, land_refs = refs[:n], refs[n:2 * n]
        send_ref, recv_ref = refs[2 * n], refs[2 * n + 1]
        for cp in _scatter_copies(ins, land_refs, send_ref, recv_ref):
            cp.wait_send()
            cp.wait_recv()

    res = pl.pallas_call(
        body, name=f"scatter_wait_{layer}",
        in_specs=[HBM_SPEC] * (2 * n) + [SEM_SPEC, SEM_SPEC, _any()], out_specs=[HBM_SPEC] * (2 * n),
        out_shape=[pltpu.HBM(a.shape, a.dtype) for a in list(srcs) + list(lands)],
        input_output_aliases={k: k for k in range(2 * n)}, compiler_params=IN_FLIGHT,
    )(*srcs, *lands, send, recv, after)
    return res[n:]


def add_chips(g, r1, r2, place, total, layer):
    _, rows, cols = g.shape
    hr = rows // 2

    def body(p_ref, g_ref, r1_ref, r2_ref, t_hbm, o_ref):
        del p_ref, t_hbm
        own = g_ref[...] + r1_ref[...]
        o_ref[...] = ((own + r2_ref[0].astype(F32)) + r2_ref[1].astype(F32)) + r2_ref[2].astype(F32)

    grid_spec = pltpu.PrefetchScalarGridSpec(
        num_scalar_prefetch=1, grid=(1,),
        in_specs=[pl.BlockSpec((None, hr, cols), lambda i, p: (p[1], p[0], 0)),
                  pl.BlockSpec((None, hr, cols), lambda i, p: (p[1], 0, 0)),
                  pl.BlockSpec((3, hr, cols), lambda i, p: (0, 0, 0)), _any()],
        out_specs=pl.BlockSpec((None, hr, cols), lambda i, p: (layer, p[0], 0)))
    return pl.pallas_call(body, grid_spec=grid_spec, out_shape=jax.ShapeDtypeStruct(total.shape, F32),
                          input_output_aliases={4: 0}, compiler_params=_cp(("arbitrary",)),
                          name="add_chips")(place, g, r1, r2, total)


def pair_share(gs):
    n = len(gs)

    def body(*refs):
        outs = refs[n:2 * n]
        send, recv = refs[2 * n:]
        x, y, c, _ = _place()
        cps = []
        for k in range(n):
            mine = outs[k].at[:, _half(outs[k].shape[1], c), :]
            cp = pltpu.make_async_remote_copy(
                src_ref=mine, dst_ref=mine, send_sem=send.at[k], recv_sem=recv.at[k],
                device_id=(x, y, 1 - c), device_id_type=MESH)
            cp.start()
            cps.append(cp)
        for k, cp in enumerate(cps):
            cp.wait_send()
            theirs = outs[k].at[:, _half(outs[k].shape[1], 1 - c), :]
            pltpu.make_async_remote_copy(
                src_ref=theirs, dst_ref=theirs, send_sem=send.at[k], recv_sem=recv.at[k],
                device_id=(x, y, 1 - c), device_id_type=MESH).wait_recv()

    return pl.pallas_call(
        body, in_specs=[_any()] * n, out_specs=[_any()] * n,
        out_shape=[jax.ShapeDtypeStruct(g.shape, g.dtype) for g in gs], input_output_aliases={k: k for k in range(n)},
        scratch_shapes=[pltpu.SemaphoreType.DMA((n,)), pltpu.SemaphoreType.DMA((n,))],
        name="pair_share")(*gs)


def small_collect(v, reduce, name):
    rows = v.shape[0]
    flips = [(fx, fy, fc) for fx in (0, 1) for fy in (0, 1) for fc in (0, 1)][1:]

    def body(v_ref, o_ref, buf, send, recv):
        x, y, c, _ = _place()
        buf[4 * x + 2 * y + c] = v_ref[...]
        peers = [(jnp.where(fx, 1 - x, x), jnp.where(fy, 1 - y, y), jnp.where(fc, 1 - c, c)) for fx, fy, fc in flips]
        cps = []
        for k, peer in enumerate(peers):
            cp = pltpu.make_async_remote_copy(
                src_ref=v_ref, dst_ref=buf.at[4 * x + 2 * y + c], send_sem=send.at[k], recv_sem=recv.at[k],
                device_id=peer, device_id_type=MESH)
            cp.start()
            cps.append(cp)
        for k, (px, py, pc) in enumerate(peers):
            pltpu.make_async_remote_copy(
                src_ref=v_ref, dst_ref=buf.at[4 * px + 2 * py + pc], send_sem=send.at[k], recv_sem=recv.at[k],
                device_id=(px, py, pc), device_id_type=MESH).wait_recv()
        for cp in cps:
            cp.wait_send()
        if reduce:
            acc = buf[0]
            for s in range(1, 8):
                acc = acc + buf[s]
            o_ref[...] = acc
        else:
            o_ref[...] = buf[...]

    vm = pl.BlockSpec(memory_space=pltpu.VMEM)
    out_shape = jax.ShapeDtypeStruct((rows, SMALL_COLS) if reduce else (8, rows, SMALL_COLS), F32)
    return pl.pallas_call(
        body, in_specs=[vm], out_specs=vm, out_shape=out_shape,
        scratch_shapes=[pltpu.VMEM((8, rows, SMALL_COLS), F32), pltpu.SemaphoreType.DMA((7,)),
                        pltpu.SemaphoreType.DMA((7,))],
        name=name)(v)


def adamw(w, g, m, v, rb, name):
    nl, rows, cols = w.shape

    def body(w_ref, g_ref, m_ref, v_ref, go_ref, d_ref, nm_ref, nv_ref):
        gv = g_ref[...]
        go_ref[...] = gv
        nm = ADAM_B1 * m_ref[...] + (1.0 - ADAM_B1) * gv
        nv = ADAM_B2 * v_ref[...] + (1.0 - ADAM_B2) * (gv * gv)
        m_hat = nm / (1.0 - ADAM_B1 ** ADAM_STEP)
        v_hat = nv / (1.0 - ADAM_B2 ** ADAM_STEP)
        d_ref[...] = -ADAM_LR * (m_hat / (jnp.sqrt(v_hat) + ADAM_EPS) + ADAM_WD * w_ref[...])
        nm_ref[...] = nm
        nv_ref[...] = nv

    blk = pl.BlockSpec((None, rb, cols), lambda l, r: (l, r, 0))
    shp = jax.ShapeDtypeStruct(w.shape, F32)
    return pl.pallas_call(body, grid=(nl, rows // rb), in_specs=[blk] * 4, out_specs=[blk] * 4, out_shape=[shp] * 4,
                          compiler_params=_cp(("arbitrary", "arbitrary")), name=name)(w, g, m, v)


def _pack(parts, rows):
    flat = jnp.concatenate([p.reshape(-1).astype(F32) for p in parts])
    return jnp.pad(flat, (0, rows * SMALL_COLS - flat.shape[0])).reshape(rows, SMALL_COLS)


def _unpack(vec, shapes):
    flat = vec.reshape(-1)
    out, off = [], 0
    for s in shapes:
        size = 1
        for d in s:
            size *= d
        out.append(flat[off:off + size].reshape(s))
        off += size
    return out


def kernel(x, w_in, w_conv, rel_bias, g_conv_out, g_attn_out, w_out, g_pre_mix, g_post_mix, g_pre_ffn, g_post_ffn, w_ffn_in, w_ffn_out, loss_target, m_w_in, m_w_conv, m_rel_bias, m_g_conv_out, m_g_attn_out, m_w_out, m_g_pre_mix, m_g_post_mix, m_g_pre_ffn, m_g_post_ffn, m_w_ffn_in, m_w_ffn_out, v_w_in, v_w_conv, v_rel_bias, v_g_conv_out, v_g_attn_out, v_w_out, v_g_pre_mix, v_g_post_mix, v_g_pre_ffn, v_g_post_ffn, v_w_ffn_in, v_w_ffn_out):
    xi, yi, ci = lax.axis_index("x"), lax.axis_index("y"), lax.axis_index("c")
    chip = 2 * xi + yi
    nl = w_in.shape[0]
    x0 = x[0]
    target = loss_target[0]
    cwl = CW // NCHIP

    chip1 = chip.reshape(1).astype(jnp.int32)
    own = [[cast_to_slot(w, chip1, l) for w in (w_in, w_out, w_ffn_in, w_ffn_out)] for l in range(nl)]
    wc_all = small_collect(_pack([w_conv], 8), False, "gather_w_conv")
    wc_full = wc_all[0::2].reshape(NCHIP, -1)[:, :nl * cwl * 3].reshape(NCHIP, nl, cwl, 3)
    wc_full = jnp.transpose(wc_full, (1, 0, 2, 3)).reshape(nl, CW, 3)
    wconv_t = jnp.pad(jnp.transpose(wc_full, (0, 2, 1)), ((0, 0), (0, 5), (0, 0)))
    gm = jnp.kron(jnp.eye(CW // HD, dtype=F32), jnp.full((HD, HD), 1.0 / HD, F32)).astype(BF16)
    row = lambda a, l: a[l][None, :]

    def token(t):
        return t[0:1, 0:1]

    def gather_finish(flight, after, tag):
        send, recv, bufs, _ = flight
        return gather_forward(gather_wait(send, recv, bufs, after, tag))

    first_mix = gather_start(own[0][:2], wc_all, "0m")
    first_ffn = gather_start(own[0][2:], first_mix[3], "0f")
    flight = to_sibling = None
    saved, weights = [], []
    h = x0
    for l in range(nl):
        if l == 0:
            gw_in, gw_out = gather_finish(first_mix, x0, "0m")
        elif l == 1:
            gw_in, gw_out, gw_fi, gw_fo = gather_finish(flight, h, l)
        else:
            gw_in, gw_out, gw_fi, gw_fo = forward_wait(*to_sibling[:3], h, l)
        gw_out = gw_out.reshape(D, D)
        g_pm, g_pf = row(g_pre_mix, l), row(g_pre_ffn, l)
        if l == 0:
            g_pm = g_pm + token(first_ffn[3])
        if l + 1 < nl:
            flight = gather_start(own[l + 1], first_ffn[3] if l == 0 else gw_in, l + 1)
            g_pm = g_pm + token(flight[3])
        bias2 = bias_expand(_diag_vector(rel_bias[l]), QG_FWD)
        proj = fwd_inproj(h, g_pm, gw_in)
        xmid, o, lse, y, z = fwd_mix(h, proj, bias2, wconv_t[l], row(g_conv_out, l), row(g_attn_out, l),
                                     row(g_post_mix, l), gm, gw_out)
        if l == 0:
            gw_fi, gw_fo = gather_finish(first_ffn, xmid, "0f")
        elif l + 1 < nl:
            send, recv, bufs, _ = flight
            to_sibling = forward_start(gather_wait(send, recv, bufs, xmid, l + 1), l + 1)
            g_pf = g_pf + token(to_sibling[3])
        gw_fo = gw_fo.reshape(2, DFF // 2, D)
        gu, f, xout = fwd_ffn(xmid, g_pf, row(g_post_ffn, l), gw_fi, gw_fo)
        saved.append((h, proj, bias2, xmid, o, lse, y, z, gu, f))
        weights.append((gw_in, gw_out, gw_fi, gw_fo))
        h = xout
    dx, loss_blk = loss_head(h, target)

    core = ci.reshape(1).astype(jnp.int32)
    place = jnp.stack([ci, chip]).astype(jnp.int32)
    totals = [lax.empty(w.shape, F32) for w in (w_in, w_out, w_ffn_in, w_ffn_out)]
    small = {k: [None] * nl for k in ("co", "ao", "pm", "qm", "pf", "qf", "rel", "wc")}

    def reduce_begin(kinds, grads, tag):
        return kinds, exchange_start(grads, tag), tag

    def reduce_mid(state, after):
        kinds, (send, recv, srcs, lands, _), tag = state
        grads, from_sibling = exchange_wait(send, recv, srcs, lands, after, tag)
        pair_sums = [add_pair(g, r, core) for g, r in zip(grads, from_sibling)]
        return kinds, grads, from_sibling, scatter_start(pair_sums, tag), tag

    def reduce_end(state, after, totals, layer):
        kinds, grads, from_sibling, (send, recv, srcs, lands, _), tag = state
        from_chips = scatter_wait(send, recv, srcs, lands, after, tag)
        totals = list(totals)
        for i, g, r1, r2 in zip(kinds, grads, from_sibling, from_chips):
            totals[i] = add_chips(g, r1, r2, place, totals[i], layer)
        return totals

    begun = flying = None
    for l in reversed(range(nl)):
        hin, proj, bias2, xmid, o, lse, y, z, gu, f = saved[l]
        gw_in, gw_out, gw_fi, gw_fo = weights[l]
        g_qf, g_qm, wct = row(g_post_ffn, l), row(g_post_mix, l), wconv_t[l]
        if begun is not None:
            g_qf = g_qf + token(begun[1][4])
        dxm, dfb, act, dgu, h2, dg_qf, dg_pf = bwd_ffn(dx, f, xmid, gu, row(g_pre_ffn, l), g_qf, gw_fi, gw_fo)
        if begun is not None:
            flying = reduce_mid(begun, dxm)
            g_qm = g_qm + token(flying[3][4])
        gr_fo = wgrad(act, dfb, 256, D, False, "wgrad_ffn_out").reshape(NCHIP, DFF // NCHIP, D)
        gr_fi = wgrad(h2, dgu, 512, 2 * DFF // NCHIP, True, "wgrad_ffn_in")
        if l == 0:
            begun_ffn = reduce_begin([2, 3], [gr_fi, gr_fo], "0f")
            g_qm = g_qm + token(begun_ffn[1][4])
        dzb, do, dco, dbg, dg_qm, dg_co, dg_ao = bwd_mix(dxm, z, o, proj, wct, row(g_conv_out, l),
                                                          row(g_attn_out, l), g_qm, gm, gw_out)
        if l == 0:
            flying_ffn = reduce_mid(begun_ffn, dzb)
            wct = wct + token(flying_ffn[3][4])
        gr_out = wgrad(y, dzb, 512, D, False, "wgrad_out").reshape(NCHIP, D // NCHIP, D)
        dhc, dcg, dwc = bwd_conv(dco, proj, wct)
        dq, dk, dv, db2 = bwd_attn(proj, o, do, lse, bias_expand(_diag_vector(rel_bias[l]), QG_BWD))
        dx, dproj, hb, dg_pm = bwd_inproj(dxm, hin, dhc, dbg, dcg, dq, dk, dv, row(g_pre_mix, l), gw_in)
        if flying is not None:
            totals = reduce_end(flying, dx, totals, l + 1)
        gr_in = wgrad(hb, dproj, 512, PROJ // NCHIP, True, "wgrad_in")
        small["co"][l], small["ao"][l], small["pm"][l], small["qm"][l] = dg_co, dg_ao, dg_pm, dg_qm
        small["pf"][l], small["qf"][l] = dg_pf, dg_qf
        small["rel"][l] = _diag_vector_bwd(bias_reduce(db2.reshape(NH, QG_BWD, QG_BWD + LEFT)))
        small["wc"][l] = jnp.transpose(dwc[0:3], (1, 0))
        if l > 0:
            begun = reduce_begin([0, 1, 2, 3], [gr_in, gr_out, gr_fi, gr_fo], l)
    flying_mix = reduce_mid(reduce_begin([0, 1], [gr_in, gr_out], "0m"), dx)
    totals = reduce_end(flying_ffn, flying_mix[3][4], totals, 0)
    totals = reduce_end(flying_mix, totals[2], totals, 0)
    gr_in, gr_out, gr_fi, gr_fo = pair_share(totals)

    order = ("co", "ao", "pm", "qm", "pf", "qf", "rel", "wc")
    parts = [jnp.stack(small[k]) for k in order] + [loss_blk[0:1, 0:1]]
    shapes = [p.shape for p in parts]
    red = _unpack(small_collect(_pack(parts, 40), True, "reduce_small"), shapes)
    gr_co, gr_ao, gr_pm, gr_qm, gr_pf, gr_qf, gr_rel, gr_wc_full, loss = red
    gr_co, gr_ao, gr_pm, gr_qm, gr_pf, gr_qf = [a.reshape(nl, -1) for a in (gr_co, gr_ao, gr_pm, gr_qm, gr_pf, gr_qf)]
    gr_wc = lax.dynamic_slice_in_dim(gr_wc_full, chip * cwl, cwl, axis=1)
    loss = loss.reshape(())

    big = []
    for w, g, m, v, name in ((w_in, gr_in, m_w_in, v_w_in, "adamw_in"), (w_out, gr_out, m_w_out, v_w_out, "adamw_out"),
                             (w_ffn_in, gr_fi, m_w_ffn_in, v_w_ffn_in, "adamw_ffn_in"),
                             (w_ffn_out, gr_fo, m_w_ffn_out, v_w_ffn_out, "adamw_ffn_out")):
        big.append(adamw(w, g, m, v, w.shape[1] // 4, name))
    sw = [g_conv_out, g_attn_out, g_pre_mix, g_post_mix, g_pre_ffn, g_post_ffn, rel_bias, w_conv]
    sg = [gr_co, gr_ao, gr_pm, gr_qm, gr_pf, gr_qf, gr_rel, gr_wc]
    sm = [m_g_conv_out, m_g_attn_out, m_g_pre_mix, m_g_post_mix, m_g_pre_ffn, m_g_post_ffn, m_rel_bias, m_w_conv]
    sv = [v_g_conv_out, v_g_attn_out, v_g_pre_mix, v_g_post_mix, v_g_pre_ffn, v_g_post_ffn, v_rel_bias, v_w_conv]
    sshapes = [a.shape for a in sw]
    packed = [_pack(a, 32)[None] for a in (sw, sg, sm, sv)]
    s_out = [_unpack(a[0], sshapes) for a in adamw(*packed, 32, "adamw_small")]

    def leaves(big_i, small_i):
        b_in, b_out, b_fi, b_fo = big_i
        s_co, s_ao, s_pm, s_qm, s_pf, s_qf, s_rel, s_wc = small_i
        return [b_in, s_wc, s_rel, s_co, s_ao, b_out, s_pm, s_qm, s_pf, s_qf, b_fi, b_fo]

    out = [loss, dx[None]]
    out += leaves([b[0] for b in big], sg)
    for i in range(1, 4):
        out += leaves([b[i] for b in big], s_out[i])
    return tuple(out)
```

```python
import functools

import jax
import jax.numpy as jnp
from jax import lax
from jax.experimental import pallas as pl
from jax.experimental.pallas import tpu as pltpu

F32 = jnp.float32
BF16 = jnp.bfloat16

D = 1024
PROJ = 3072
CW = 512
HD = 64
NH = 8
CHUNK = 64
BAND = 576
REL_CLIP = 128
NREL = 2 * REL_CLIP + 1
DFF = 2816
DEPTH = 4
NCHIP = 4
EPS = 1e-6
NEG_INF = -1e30

ADAM_LR = 0.001
ADAM_B1 = 0.9
ADAM_B2 = 0.999
ADAM_EPS = 1e-08
ADAM_WD = 0.01
ADAM_STEP = 10

V7X_VMEM_BYTES = 64 * 1024 * 1024
VMEM_LIMIT = V7X_VMEM_BYTES - 8 * 1024 * 1024
LANES = 128
QG_FWD = 4 * CHUNK
QG_BWD = 2 * CHUNK
LEFT = BAND - CHUNK
TQ = 512
TM = 256
SMALL_COLS = 1024
MESH = pl.DeviceIdType.MESH
NT = (((1,), (1,)), ((), ()))
TN = (((0,), (0,)), ((), ()))


def _cp(sem=None, vmem=VMEM_LIMIT):
    return pltpu.CompilerParams(dimension_semantics=sem, vmem_limit_bytes=vmem)


def _any():
    return pl.BlockSpec(memory_space=pl.ANY)


def _const(shape):
    nd = len(shape)
    return pl.BlockSpec(shape, lambda *_: (0,) * nd)


def _rms(v, g):
    r = lax.rsqrt(jnp.mean(v * v, axis=-1, keepdims=True) + EPS)
    return v * r * g


def _rms_bwd(dy, v, g):
    r = lax.rsqrt(jnp.mean(v * v, axis=-1, keepdims=True) + EPS)
    vh = v * r
    dg = jnp.sum(dy * vh, axis=0, keepdims=True)
    dvh = dy * g
    dv = r * (dvh - vh * jnp.mean(dvh * vh, axis=-1, keepdims=True))
    return dv, dg


def _group_mean(v, gm):
    hi = v.astype(BF16)
    lo = (v - hi.astype(F32)).astype(BF16)
    return jnp.dot(hi, gm, preferred_element_type=F32) + jnp.dot(lo, gm, preferred_element_type=F32)


def _group_rms_bwd(dy, v, g, gm):
    r = lax.rsqrt(_group_mean(v * v, gm) + EPS)
    vh = v * r
    dg = jnp.sum(dy * vh, axis=0, keepdims=True)
    dvh = dy * g
    dv = r * (dvh - vh * _group_mean(dvh * vh, gm))
    return dv, dg


def _head_masks(scale):
    lane = lax.broadcasted_iota(jnp.int32, (1, LANES), 1)
    return [jnp.where((lane >= HD * a) & (lane < HD * (a + 1)), scale, 0.0).astype(BF16) for a in range(2)]


class _Resident:
    def __init__(self, pieces, sems):
        self.first = pl.program_id(0) == 0
        self.copies = [pltpu.make_async_copy(src, dst, sems.at[k]) for k, (src, dst) in enumerate(pieces)]
        self.dst = [dst for _, dst in pieces]

        @pl.when(self.first)
        def _():
            for cp in self.copies:
                cp.start()

    def __getitem__(self, k):
        @pl.when(self.first)
        def _():
            self.copies[k].wait()

        return self.dst[k][...]


def _conv_taps(u_prev, u, scr):
    n = u.shape[0]
    scr[0:16, :] = u_prev
    scr[16:16 + n, :] = u
    return scr[15:15 + n, :], scr[14:14 + n, :]


def fwd_inproj(x, g, w_all):
    t = x.shape[0]
    wc = PROJ // NCHIP

    def body(x_ref, g_ref, w_hbm, o_ref, w_v, sems):
        w = _Resident([(w_hbm.at[b], w_v.at[b]) for b in range(NCHIP)], sems)
        h = _rms(x_ref[...], g_ref[...]).astype(BF16)
        for b in range(NCHIP):
            o_ref[:, wc * b:wc * (b + 1)] = jnp.dot(h, w[b], preferred_element_type=F32).astype(BF16)

    return pl.pallas_call(
        body, grid=(t // TQ,),
        in_specs=[pl.BlockSpec((TQ, D), lambda i: (i, 0)), _const((1, D)), _any()],
        out_specs=pl.BlockSpec((TQ, PROJ), lambda i: (i, 0)),
        out_shape=jax.ShapeDtypeStruct((t, PROJ), BF16),
        scratch_shapes=[pltpu.VMEM((NCHIP, D, wc), BF16), pltpu.SemaphoreType.DMA((NCHIP,))],
        compiler_params=_cp(("arbitrary",)), name="fwd_inproj")(x, g, w_all)


def _attn_window_specs():
    return [
        pl.BlockSpec((TQ, CW), lambda i: (i, 3)),
        pl.BlockSpec((TQ, CW), lambda i: (jnp.maximum(i - 1, 0), 4)),
        pl.BlockSpec((TQ, CW), lambda i: (i, 4)),
        pl.BlockSpec((TQ, CW), lambda i: (jnp.maximum(i - 1, 0), 5)),
        pl.BlockSpec((TQ, CW), lambda i: (i, 5)),
    ]


def _conv_specs():
    return [
        pl.BlockSpec((TQ, 3 * CW), lambda i: (i, 0)),
        pl.BlockSpec((16, 3 * CW), lambda i: (jnp.maximum(i * (TQ // 16) - 1, 0), 0)),
    ]


def _conv_fwd(pc_ref, pcp_ref, wc_ref, scr, first):
    pc = pc_ref[...].astype(F32)
    hc, bg, cg = pc[:, :CW], pc[:, CW:2 * CW], pc[:, 2 * CW:]
    u = cg * hc
    pp = pcp_ref[...].astype(F32)
    u_prev = jnp.where(first, 0.0, pp[:, 2 * CW:] * pp[:, :CW])
    u1, u2 = _conv_taps(u_prev, u, scr)
    cout = wc_ref[0:1, :] * u2 + wc_ref[1:2, :] * u1 + wc_ref[2:3, :] * u
    return hc, bg, cg, u, u1, u2, cout


def _key_penalty(first, r0, kg):
    col = lax.broadcasted_iota(jnp.int32, (1, kg), 1)
    limit = jnp.where(first, TQ - r0, 0)
    return jnp.where(col < limit, NEG_INF, 0.0)


def fwd_mix(x, proj, bias2, wconv_t, g_co, g_ao, g_pm, gm, wout_all):
    t = x.shape[0]
    qg, kg = QG_FWD, QG_FWD + LEFT

    def body(x_ref, pc_ref, pcp_ref, q_ref, kp_ref, kc_ref, vp_ref, vc_ref, b2_ref, wc_ref, gco_ref, gao_ref, gpm_ref,
             gm_ref, wout_hbm, xmid_ref, o_ref, lse_ref, y_ref, z_ref, wout_v, kwin, vwin, cscr, sems):
        i = pl.program_id(0)
        first = i == 0
        wout = _Resident([(wout_hbm, wout_v)], sems)
        kwin[0:TQ, :] = kp_ref[...]
        kwin[TQ:2 * TQ, :] = kc_ref[...]
        vwin[0:TQ, :] = vp_ref[...]
        vwin[TQ:2 * TQ, :] = vc_ref[...]
        qmask = _head_masks(HD ** -0.5)
        low = lax.broadcasted_iota(jnp.int32, (1, LANES), 1) < HD

        def group(g, carry):
            r0 = pl.multiple_of(g * qg, qg)
            pen = _key_penalty(first, r0, kg)
            for hp in range(NH // 2):
                ls = slice(LANES * hp, LANES * (hp + 1))
                qb = q_ref[pl.ds(r0, qg), ls]
                q2 = jnp.concatenate([qb * qmask[0], qb * qmask[1]], axis=0)
                s = lax.dot_general(q2, kwin[pl.ds(r0, kg), ls], NT, preferred_element_type=F32)
                s = s + b2_ref[hp] + pen
                m = jnp.max(s, axis=-1, keepdims=True)
                p = jnp.exp(s - m)
                l = jnp.sum(p, axis=-1, keepdims=True)
                o2 = jnp.dot(p.astype(BF16), vwin[pl.ds(r0, kg), ls], preferred_element_type=F32) * (1.0 / l)
                lse2 = m + jnp.log(l)
                o_ref[pl.ds(r0, qg), ls] = jnp.where(low, o2[:qg], o2[qg:])
                lse_ref[pl.ds(r0, qg), ls] = jnp.where(low, lse2[:qg], lse2[qg:])
            return carry

        lax.fori_loop(0, TQ // qg, group, 0)

        _, bg, _, _, _, _, cout = _conv_fwd(pc_ref, pcp_ref, wc_ref, cscr, first)
        yc = bg * cout
        gmv = gm_ref[...]
        ycn = yc * lax.rsqrt(_group_mean(yc * yc, gmv) + EPS) * gco_ref[...]
        oa = o_ref[...]
        oan = oa * lax.rsqrt(_group_mean(oa * oa, gmv) + EPS) * gao_ref[...]
        y_ref[:, 0:CW] = ycn.astype(BF16)
        y_ref[:, CW:2 * CW] = oan.astype(BF16)
        z = jnp.dot(y_ref[...], wout[0], preferred_element_type=F32)
        z_ref[...] = z
        xmid_ref[...] = x_ref[...] + _rms(z, gpm_ref[...])

    row = lambda w: pl.BlockSpec((TQ, w), lambda i: (i, 0))
    return pl.pallas_call(
        body, grid=(t // TQ,),
        in_specs=[row(D)] + _conv_specs() + _attn_window_specs() + [
            _const((NH // 2, 2 * qg, kg)), _const((8, CW)), _const((1, CW)), _const((1, CW)), _const((1, D)),
            _const((CW, CW)), _any()],
        out_specs=[row(D), row(CW), row(CW), row(D), row(D)],
        out_shape=[jax.ShapeDtypeStruct((t, D), F32), jax.ShapeDtypeStruct((t, CW), F32),
                   jax.ShapeDtypeStruct((t, CW), F32), jax.ShapeDtypeStruct((t, D), BF16),
                   jax.ShapeDtypeStruct((t, D), F32)],
        scratch_shapes=[pltpu.VMEM((D, D), BF16), pltpu.VMEM((2 * TQ, CW), BF16), pltpu.VMEM((2 * TQ, CW), BF16),
                        pltpu.VMEM((TQ + 16, CW), F32), pltpu.SemaphoreType.DMA((1,))],
        compiler_params=_cp(("arbitrary",)), name="fwd_mix",
    )(x, proj, proj, proj, proj, proj, proj, proj, bias2, wconv_t, g_co, g_ao, g_pm, gm, wout_all)


def fwd_ffn(xmid, g_pre, g_post, wfi_all, wfo_all):
    t = xmid.shape[0]
    hw = DFF // 2

    def body(x_ref, gpre_ref, gpost_ref, wfi_hbm, wfo_hbm, gu_ref, f_ref, xo_ref, wfi_v, wfo_v, sems):
        w = _Resident([(src.at[k], dst.at[k]) for j in range(2)
                       for src, dst, k in ((wfi_hbm, wfi_v, j), (wfi_hbm, wfi_v, 2 + j), (wfo_hbm, wfo_v, j))], sems)
        xv = x_ref[...]
        h = _rms(xv, gpre_ref[...]).astype(BF16)
        f = jnp.zeros((TM, D), F32)
        for j in range(2):
            gate = jnp.dot(h, w[3 * j], preferred_element_type=F32)
            up = jnp.dot(h, w[3 * j + 1], preferred_element_type=F32)
            gu_ref[:, hw * j:hw * (j + 1)] = gate.astype(BF16)
            gu_ref[:, DFF + hw * j:DFF + hw * (j + 1)] = up.astype(BF16)
            act = gate * (1.0 / (1.0 + jnp.exp(-gate))) * up
            f = f + jnp.dot(act.astype(BF16), w[3 * j + 2], preferred_element_type=F32)
        f_ref[...] = f
        xo_ref[...] = xv + _rms(f, gpost_ref[...])

    row = lambda w: pl.BlockSpec((TM, w), lambda i: (i, 0))
    return pl.pallas_call(
        body, grid=(t // TM,),
        in_specs=[row(D), _const((1, D)), _const((1, D)), _any(), _any()],
        out_specs=[row(2 * DFF), row(D), row(D)],
        out_shape=[jax.ShapeDtypeStruct((t, 2 * DFF), BF16), jax.ShapeDtypeStruct((t, D), F32),
                   jax.ShapeDtypeStruct((t, D), F32)],
        scratch_shapes=[pltpu.VMEM((NCHIP, D, hw), BF16), pltpu.VMEM((2, hw, D), BF16), pltpu.SemaphoreType.DMA((6,))],
        compiler_params=_cp(("arbitrary",)), name="fwd_ffn")(xmid, g_pre, g_post, wfi_all, wfo_all)


def loss_head(y, target):
    t = y.shape[0]

    def body(y_ref, t_ref, dy_ref, l_ref):
        @pl.when(pl.program_id(0) == 0)
        def _():
            l_ref[...] = jnp.zeros_like(l_ref)

        e = y_ref[...] - t_ref[...]
        dy_ref[...] = e * (1.0 / D)
        rows = jnp.sum(e * e, axis=-1, keepdims=True) * (1.0 / D)
        l_ref[...] += 0.5 * jnp.sum(rows, axis=0, keepdims=True)

    row = pl.BlockSpec((TQ, D), lambda i: (i, 0))
    return pl.pallas_call(
        body, grid=(t // TQ,), in_specs=[row, row], out_specs=[row, _const((8, LANES))],
        out_shape=[jax.ShapeDtypeStruct((t, D), F32), jax.ShapeDtypeStruct((8, LANES), F32)],
        compiler_params=_cp(("arbitrary",)), name="loss_head")(y, target)


def bwd_ffn(dx, f, xmid, gu, g_pre, g_post, wfi_all, wfo_all):
    t = dx.shape[0]
    hw = DFF // 2

    def body(dx_ref, f_ref, x_ref, gu_ref, gpre_ref, gpost_ref, wfi_hbm, wfo_hbm,
             dxm_ref, df_ref, act_ref, dgu_ref, h_ref, dgpost_ref, dgpre_ref, wfi_v, wfo_v, sems):
        w = _Resident([(src.at[k], dst.at[k]) for j in range(2)
                       for src, dst, k in ((wfo_hbm, wfo_v, j), (wfi_hbm, wfi_v, j), (wfi_hbm, wfi_v, 2 + j))], sems)

        @pl.when(pl.program_id(0) == 0)
        def _():
            dgpost_ref[...] = jnp.zeros_like(dgpost_ref)
            dgpre_ref[...] = jnp.zeros_like(dgpre_ref)

        dxo = dx_ref[...]
        df, dgp = _rms_bwd(dxo, f_ref[...], gpost_ref[...])
        dgpost_ref[...] += dgp
        dfb = df.astype(BF16)
        df_ref[...] = dfb
        dh = jnp.zeros((TM, D), F32)
        for j in range(2):
            dact = lax.dot_general(dfb, w[3 * j], NT, preferred_element_type=F32)
            gate = gu_ref[:, hw * j:hw * (j + 1)].astype(F32)
            up = gu_ref[:, DFF + hw * j:DFF + hw * (j + 1)].astype(F32)
            sig = 1.0 / (1.0 + jnp.exp(-gate))
            silu = gate * sig
            act_ref[:, hw * j:hw * (j + 1)] = (silu * up).astype(BF16)
            dup = (dact * silu).astype(BF16)
            dgate = (dact * up * (sig * (1.0 + gate * (1.0 - sig)))).astype(BF16)
            dgu_ref[:, hw * j:hw * (j + 1)] = dgate
            dgu_ref[:, DFF + hw * j:DFF + hw * (j + 1)] = dup
            dh = dh + lax.dot_general(dgate, w[3 * j + 1], NT, preferred_element_type=F32)
            dh = dh + lax.dot_general(dup, w[3 * j + 2], NT, preferred_element_type=F32)
        xv = x_ref[...]
        gpre = gpre_ref[...]
        h_ref[...] = _rms(xv, gpre).astype(BF16)
        dxv, dgq = _rms_bwd(dh, xv, gpre)
        dgpre_ref[...] += dgq
        dxm_ref[...] = dxo + dxv

    row = lambda w: pl.BlockSpec((TM, w), lambda i: (i, 0))
    return pl.pallas_call(
        body, grid=(t // TM,),
        in_specs=[row(D), row(D), row(D), row(2 * DFF), _const((1, D)), _const((1, D)), _any(), _any()],
        out_specs=[row(D), row(D), row(DFF), row(2 * DFF), row(D), _const((1, D)), _const((1, D))],
        out_shape=[jax.ShapeDtypeStruct((t, D), F32), jax.ShapeDtypeStruct((t, D), BF16),
                   jax.ShapeDtypeStruct((t, DFF), BF16), jax.ShapeDtypeStruct((t, 2 * DFF), BF16),
                   jax.ShapeDtypeStruct((t, D), BF16), jax.ShapeDtypeStruct((1, D), F32),
                   jax.ShapeDtypeStruct((1, D), F32)],
        scratch_shapes=[pltpu.VMEM((NCHIP, D, hw), BF16), pltpu.VMEM((2, hw, D), BF16), pltpu.SemaphoreType.DMA((6,))],
        compiler_params=_cp(("arbitrary",)), name="bwd_ffn")(dx, f, xmid, gu, g_pre, g_post, wfi_all, wfo_all)


def bwd_mix(dxm, z, o, proj, wconv_t, g_co, g_ao, g_pm, gm, wout_all):
    t = dxm.shape[0]

    def body(dx_ref, z_ref, o_ref, pc_ref, pcp_ref, wc_ref, gco_ref, gao_ref, gpm_ref, gm_ref, wout_hbm,
             dz_ref, do_ref, dco_ref, dbg_ref, dgpm_ref, dgco_ref, dgao_ref, wout_v, cscr, sems):
        first = pl.program_id(0) == 0
        wout = _Resident([(wout_hbm.at[pl.ds(CW * k, CW)], wout_v.at[pl.ds(CW * k, CW)]) for k in range(2)], sems)

        @pl.when(first)
        def _():
            dgpm_ref[...] = jnp.zeros_like(dgpm_ref)
            dgco_ref[...] = jnp.zeros_like(dgco_ref)
            dgao_ref[...] = jnp.zeros_like(dgao_ref)

        dz, dgp = _rms_bwd(dx_ref[...], z_ref[...], gpm_ref[...])
        dgpm_ref[...] += dgp
        dzb = dz.astype(BF16)
        dz_ref[...] = dzb
        gmv = gm_ref[...]
        _, bg, _, _, _, _, cout = _conv_fwd(pc_ref, pcp_ref, wc_ref, cscr, first)
        dy_conv = lax.dot_general(dzb, wout[0], NT, preferred_element_type=F32)
        dyc, dgc = _group_rms_bwd(dy_conv, bg * cout, gco_ref[...], gmv)
        dgco_ref[...] += dgc
        dbg_ref[...] = (dyc * cout).astype(BF16)
        dco_ref[...] = dyc * bg
        dy_attn = lax.dot_general(dzb, wout[1], NT, preferred_element_type=F32)
        do, dga = _group_rms_bwd(dy_attn, o_ref[...], gao_ref[...], gmv)
        dgao_ref[...] += dga
        do_ref[...] = do.astype(BF16)

    row = lambda w: pl.BlockSpec((TQ, w), lambda i: (i, 0))
    return pl.pallas_call(
        body, grid=(t // TQ,),
        in_specs=[row(D), row(D), row(CW)] + _conv_specs() + [
            _const((8, CW)), _const((1, CW)), _const((1, CW)), _const((1, D)), _const((CW, CW)), _any()],
        out_specs=[row(D), row(CW), row(CW), row(CW), _const((1, D)), _const((1, CW)), _const((1, CW))],
        out_shape=[jax.ShapeDtypeStruct((t, D), BF16), jax.ShapeDtypeStruct((t, CW), BF16),
                   jax.ShapeDtypeStruct((t, CW), F32), jax.ShapeDtypeStruct((t, CW), BF16),
                   jax.ShapeDtypeStruct((1, D), F32), jax.ShapeDtypeStruct((1, CW), F32),
                   jax.ShapeDtypeStruct((1, CW), F32)],
        scratch_shapes=[pltpu.VMEM((D, D), BF16), pltpu.VMEM((TQ + 16, CW), F32), pltpu.SemaphoreType.DMA((2,))],
        compiler_params=_cp(("arbitrary",)), name="bwd_mix",
    )(dxm, z, o, proj, proj, wconv_t, g_co, g_ao, g_pm, gm, wout_all)


def bwd_conv(dco, proj, wconv_t):
    t = dco.shape[0]
    nt = t // TQ

    def body(d_ref, dn_ref, pc_ref, pcp_ref, wc_ref, dhc_ref, dcg_ref, dw_ref, cscr, dscr):
        i = pl.program_id(0)
        first = i == 0

        @pl.when(first)
        def _():
            dw_ref[...] = jnp.zeros_like(dw_ref)

        hc, _, cg, u, u1, u2, _ = _conv_fwd(pc_ref, pcp_ref, wc_ref, cscr, first)
        d0 = d_ref[...]
        dscr[0:TQ, :] = d0
        dscr[TQ:TQ + 8, :] = jnp.where(i == nt - 1, 0.0, dn_ref[...])
        d1 = dscr[1:TQ + 1, :]
        d2 = dscr[2:TQ + 2, :]
        du = wc_ref[2:3, :] * d0 + wc_ref[1:2, :] * d1 + wc_ref[0:1, :] * d2
        dhc_ref[...] = (du * cg).astype(BF16)
        dcg_ref[...] = (du * hc).astype(BF16)
        dw_ref[0:1, :] += jnp.sum(d0 * u2, axis=0, keepdims=True)
        dw_ref[1:2, :] += jnp.sum(d0 * u1, axis=0, keepdims=True)
        dw_ref[2:3, :] += jnp.sum(d0 * u, axis=0, keepdims=True)

    row = lambda w: pl.BlockSpec((TQ, w), lambda i: (i, 0))
    nxt = pl.BlockSpec((8, CW), lambda i: (jnp.minimum((i + 1) * (TQ // 8), t // 8 - 1), 0))
    return pl.pallas_call(
        body, grid=(nt,),
        in_specs=[row(CW), nxt] + _conv_specs() + [_const((8, CW))],
        out_specs=[row(CW), row(CW), _const((8, CW))],
        out_shape=[jax.ShapeDtypeStruct((t, CW), BF16), jax.ShapeDtypeStruct((t, CW), BF16),
                   jax.ShapeDtypeStruct((8, CW), F32)],
        scratch_shapes=[pltpu.VMEM((TQ + 16, CW), F32), pltpu.VMEM((TQ + 8, CW), F32)],
        compiler_params=_cp(("arbitrary",)), name="bwd_conv")(dco, dco, proj, proj, wconv_t)


def bwd_attn(proj, o, do, lse, bias2):
    t = o.shape[0]
    nt = t // TQ
    qg, kg = QG_BWD, QG_BWD + LEFT

    def body(q_ref, kp_ref, kc_ref, vp_ref, vc_ref, o_ref, do_ref, lse_ref, b2_ref,
             dq_ref, dk_hbm, dv_hbm, db_hbm, kwin, vwin, dk_acc, dv_acc, db_acc):
        i = pl.program_id(0)
        first = i == 0

        @pl.when(first)
        def _():
            dk_acc[...] = jnp.zeros_like(dk_acc)
            dv_acc[...] = jnp.zeros_like(dv_acc)
            db_acc[...] = jnp.zeros_like(db_acc)

        kwin[0:TQ, :] = kp_ref[...]
        kwin[TQ:2 * TQ, :] = kc_ref[...]
        vwin[0:TQ, :] = vp_ref[...]
        vwin[TQ:2 * TQ, :] = vc_ref[...]
        scale = HD ** -0.5
        qmask = _head_masks(scale)
        vmask = _head_masks(1.0)
        low = lax.broadcasted_iota(jnp.int32, (1, LANES), 1) < HD

        def group(g, carry):
            r0 = pl.multiple_of(g * qg, qg)
            base = pl.multiple_of(i * TQ + r0, qg)
            pen = _key_penalty(first, r0, kg)
            for hp in range(NH // 2):
                ls = slice(LANES * hp, LANES * (hp + 1))
                qb = q_ref[pl.ds(r0, qg), ls]
                kw = kwin[pl.ds(r0, kg), ls]
                dob = do_ref[pl.ds(r0, qg), ls]
                prod = dob.astype(F32) * o_ref[pl.ds(r0, qg), ls]
                lseb = lse_ref[pl.ds(r0, qg), ls]
                q2 = jnp.concatenate([qb * qmask[0], qb * qmask[1]], axis=0)
                do2 = jnp.concatenate([dob * vmask[0], dob * vmask[1]], axis=0)
                lse2 = jnp.concatenate([lseb[:, 0:1], lseb[:, HD:HD + 1]], axis=0)
                dsum = jnp.concatenate([jnp.sum(jnp.where(low, prod, 0.0), axis=-1, keepdims=True),
                                        jnp.sum(jnp.where(low, 0.0, prod), axis=-1, keepdims=True)], axis=0)
                s = lax.dot_general(q2, kw, NT, preferred_element_type=F32) + b2_ref[hp] + pen
                p = jnp.exp(s - lse2)
                dp = lax.dot_general(do2, vwin[pl.ds(r0, kg), ls], NT, preferred_element_type=F32)
                ds = p * (dp - dsum)
                db_acc[hp] += ds
                dsb = ds.astype(BF16)
                dq2 = jnp.dot(dsb, kw, preferred_element_type=F32)
                dq_ref[pl.ds(r0, qg), ls] = (jnp.where(low, dq2[:qg], dq2[qg:]) * scale).astype(BF16)
                dk_acc[pl.ds(base, kg), ls] += lax.dot_general(dsb, q2, TN, preferred_element_type=F32)
                dv_acc[pl.ds(base, kg), ls] += lax.dot_general(p.astype(BF16), do2, TN, preferred_element_type=F32)
            return carry

        lax.fori_loop(0, TQ // qg, group, 0)

        @pl.when(i == nt - 1)
        def _():
            pltpu.sync_copy(dk_acc, dk_hbm)
            pltpu.sync_copy(dv_acc, dv_hbm)
            pltpu.sync_copy(db_acc, db_hbm)

    row = lambda w: pl.BlockSpec((TQ, w), lambda i: (i, 0))
    return pl.pallas_call(
        body, grid=(nt,),
        in_specs=_attn_window_specs() + [row(CW), row(CW), row(CW), _const((NH // 2, 2 * qg, kg))],
        out_specs=[row(CW), _any(), _any(), _any()],
        out_shape=[jax.ShapeDtypeStruct((t, CW), BF16), jax.ShapeDtypeStruct((t + TQ, CW), F32),
                   jax.ShapeDtypeStruct((t + TQ, CW), F32), jax.ShapeDtypeStruct((NH // 2, 2 * qg, kg), F32)],
        scratch_shapes=[pltpu.VMEM((2 * TQ, CW), BF16), pltpu.VMEM((2 * TQ, CW), BF16),
                        pltpu.VMEM((t + TQ, CW), F32), pltpu.VMEM((t + TQ, CW), F32),
                        pltpu.VMEM((NH // 2, 2 * qg, kg), F32)],
        compiler_params=_cp(("arbitrary",)), name="bwd_attn",
    )(proj, proj, proj, proj, proj, o, do, lse, bias2)


def bwd_inproj(dxm, x, dhc, dbg, dcg, dq, dk, dv, g, w_all):
    t = x.shape[0]
    wc = PROJ // NCHIP

    def body(dxm_ref, x_ref, dhc_ref, dbg_ref, dcg_ref, dq_ref, dk_ref, dv_ref, g_ref, w_hbm,
             dx_ref, dp_ref, h_ref, dg_ref, w_v, sems):
        w = _Resident([(w_hbm.at[b], w_v.at[b]) for b in range(NCHIP)], sems)

        @pl.when(pl.program_id(0) == 0)
        def _():
            dg_ref[...] = jnp.zeros_like(dg_ref)

        dp_ref[:, 0:CW] = dhc_ref[...]
        dp_ref[:, CW:2 * CW] = dbg_ref[...]
        dp_ref[:, 2 * CW:3 * CW] = dcg_ref[...]
        dp_ref[:, 3 * CW:4 * CW] = dq_ref[...]
        dp_ref[:, 4 * CW:5 * CW] = dk_ref[...].astype(BF16)
        dp_ref[:, 5 * CW:6 * CW] = dv_ref[...].astype(BF16)
        dh = jnp.zeros((TQ, D), F32)
        for b in range(NCHIP):
            dh = dh + lax.dot_general(dp_ref[:, wc * b:wc * (b + 1)], w[b], NT, preferred_element_type=F32)
        xv = x_ref[...]
        gv = g_ref[...]
        h_ref[...] = _rms(xv, gv).astype(BF16)
        dxv, dgv = _rms_bwd(dh, xv, gv)
        dg_ref[...] += dgv
        dx_ref[...] = dxm_ref[...] + dxv

    row = lambda w: pl.BlockSpec((TQ, w), lambda i: (i, 0))
    pad = pl.BlockSpec((TQ, CW), lambda i: (i + 1, 0))
    return pl.pallas_call(
        body, grid=(t // TQ,),
        in_specs=[row(D), row(D), row(CW), row(CW), row(CW), row(CW), pad, pad, _const((1, D)), _any()],
        out_specs=[row(D), row(PROJ), row(D), _const((1, D))],
        out_shape=[jax.ShapeDtypeStruct((t, D), F32), jax.ShapeDtypeStruct((t, PROJ), BF16),
                   jax.ShapeDtypeStruct((t, D), BF16), jax.ShapeDtypeStruct((1, D), F32)],
        scratch_shapes=[pltpu.VMEM((NCHIP, D, wc), BF16), pltpu.SemaphoreType.DMA((NCHIP,))],
        compiler_params=_cp(("arbitrary",)), name="bwd_inproj",
    )(dxm, x, dhc, dbg, dcg, dq, dk, dv, g, w_all)


def wgrad(a, b, kb, nb, by_columns, name):
    t, k = a.shape
    n = b.shape[1]
    tk = 512

    def body(a_ref, b_ref, o_ref):
        o_ref[...] = jnp.zeros_like(o_ref)
        for c in range(t // tk):
            o_ref[...] += lax.dot_general(a_ref[tk * c:tk * (c + 1), :], b_ref[tk * c:tk * (c + 1), :], TN,
                                          preferred_element_type=F32)

    if by_columns:
        assert nb == n // NCHIP
        out_spec = pl.BlockSpec((None, kb, nb), lambda ki, ni: (ni, ki, 0))
        out_shape = jax.ShapeDtypeStruct((NCHIP, k, nb), F32)
    else:
        assert nb == n
        out_spec = pl.BlockSpec((kb, nb), lambda ki, ni: (ki, 0))
        out_shape = jax.ShapeDtypeStruct((k, n), F32)
    return pl.pallas_call(
        body, grid=(k // kb, n // nb),
        in_specs=[pl.BlockSpec((t, kb), lambda ki, ni: (0, ki)), pl.BlockSpec((t, nb), lambda ki, ni: (0, ni))],
        out_specs=out_spec, out_shape=out_shape,
        compiler_params=_cp(("arbitrary", "arbitrary")), name=name)(a, b)


TOE = 1024
assert 2 * QG_FWD + LEFT <= TOE
N_FLAT = LEFT - REL_CLIP + 1
N_VAR = BAND - N_FLAT


def _diag_vector(table):
    last = table[:, 2 * REL_CLIP:]
    var = table[:, 2 * REL_CLIP - N_VAR:2 * REL_CLIP][:, ::-1]
    return jnp.concatenate([jnp.broadcast_to(last, (NH, N_FLAT)), var, jnp.broadcast_to(last, (NH, TOE - BAND))], axis=1)


def _diag_vector_bwd(dvec):
    dlast = jnp.sum(dvec[:, :N_FLAT], axis=1, keepdims=True) + jnp.sum(dvec[:, BAND:], axis=1, keepdims=True)
    dvar = dvec[:, N_FLAT:BAND][:, ::-1]
    return jnp.concatenate([jnp.zeros((NH, 2 * REL_CLIP - N_VAR), F32), dvar, dlast], axis=1)


def _band_valid(qg):
    r = lax.broadcasted_iota(jnp.int32, (qg, qg + LEFT), 0)
    p = lax.broadcasted_iota(jnp.int32, (qg, qg + LEFT), 1)
    start = lax.shift_left(lax.shift_right_logical(r, 6), 6)
    return (p >= start) & (p < start + BAND)


def bias_expand(vec, qgs):
    def body(v_ref, *o_refs):
        for qg, o_ref in zip(qgs, o_refs):
            valid = _band_valid(qg)
            for h in range(NH):
                rows = jnp.broadcast_to(v_ref[h:h + 1, :], (qg, TOE))
                toe = pltpu.roll(rows, 0, 1, stride=1, stride_axis=0)
                o_ref[h // 2, qg * (h % 2):qg * (h % 2 + 1), :] = jnp.where(valid, toe[:, :qg + LEFT], NEG_INF)

    return pl.pallas_call(body, out_shape=[jax.ShapeDtypeStruct((NH // 2, 2 * qg, qg + LEFT), F32) for qg in qgs],
                          name="bias_expand")(vec)


def bias_reduce(db2):
    _, qg, kg = db2.shape

    def body(d_ref, o_ref):
        ii = lax.broadcasted_iota(jnp.int32, (kg, kg), 0)
        jj = lax.broadcasted_iota(jnp.int32, (kg, kg), 1)
        flip = jnp.where(ii + jj == kg - 1, 1.0, 0.0).astype(BF16)
        for h in range(NH):
            rest = d_ref[h]
            rev = jnp.zeros((qg, kg), F32)
            for _ in range(3):
                term = rest.astype(BF16)
                rev = rev + jnp.dot(term, flip, preferred_element_type=F32)
                rest = rest - term.astype(F32)
            d = jnp.concatenate([jnp.zeros((qg, TOE - kg), F32), rev], axis=1)
            back = pltpu.roll(d, 0, 1, stride=1, stride_axis=0)
            o_ref[h:h + 1, :] = jnp.sum(back, axis=0, keepdims=True)

    rev = pl.pallas_call(body, out_shape=jax.ShapeDtypeStruct((NH, TOE), F32), name="bias_reduce")(db2)
    return rev[:, ::-1]


def _place():
    x, y, c = lax.axis_index("x"), lax.axis_index("y"), lax.axis_index("c")
    chips = [(1 - x, y), (x, 1 - y), (1 - x, 1 - y)]
    return x, y, c, chips


def _half(ref_rows, c):
    return pl.ds(c * (ref_rows // 2), ref_rows // 2)


HBM_SPEC = pl.BlockSpec(memory_space=pltpu.HBM)
SEM_SPEC = pl.BlockSpec(memory_space=pltpu.SEMAPHORE)
IN_FLIGHT = pltpu.CompilerParams(has_side_effects=pltpu.SideEffectType.DATAFLOW_SIDE_EFFECTING)


def _in_hbm(a):
    return pltpu.with_memory_space_constraint(a, pltpu.HBM)


def cast_to_slot(ws, chip, layer):
    n = len(ws)
    steps = 4

    def body(b_ref, *refs):
        del b_ref
        for w_ref, o_ref in zip(refs[:n], refs[n:]):
            o_ref[...] = w_ref[...].astype(BF16)

    grid_spec = pltpu.PrefetchScalarGridSpec(
        num_scalar_prefetch=1, grid=(steps,),
        in_specs=[pl.BlockSpec((None, w.shape[1] // steps, w.shape[2]), lambda r, b: (layer, r, 0)) for w in ws],
        out_specs=[pl.BlockSpec((None, w.shape[1] // steps, w.shape[2]), lambda r, b: (b[0], r, 0)) for w in ws])
    return pl.pallas_call(body, grid_spec=grid_spec,
                          out_shape=[jax.ShapeDtypeStruct((NCHIP,) + w.shape[1:], BF16) for w in ws],
                          compiler_params=_cp(("arbitrary",)), name="cast_to_slot")(chip, *ws)


def _gather_copies(bufs, send, recv):
    x, y, c, chips = _place()
    b = 2 * x + y
    out = []
    for k, buf in enumerate(bufs):
        rows = buf.shape[1]
        mine = buf.at[b, _half(rows, c), :]
        for j, (cx, cy) in enumerate(chips):
            theirs = buf.at[2 * cx + cy, _half(rows, c), :]
            sems = dict(send_sem=send.at[3 * k + j], recv_sem=recv.at[3 * k + j],
                        device_id=(cx, cy, c), device_id_type=MESH)
            out.append((pltpu.make_async_remote_copy(src_ref=mine, dst_ref=mine, **sems),
                        pltpu.make_async_remote_copy(src_ref=theirs, dst_ref=theirs, **sems)))
    return out


def gather_start(bufs, after, layer):
    n = len(bufs)

    def body(*refs):
        ins = refs[:n]
        send, recv = refs[n + 1], refs[n + 2]
        token = refs[-1]
        for start, _ in _gather_copies(ins, send, recv):
            start.start()
        token[...] = jnp.zeros_like(token)

    sems = pltpu.SemaphoreType.DMA((3 * n,))
    res = pl.pallas_call(
        body, name=f"gather_start_{layer}",
        in_specs=[HBM_SPEC] * n + [_any()],
        out_specs=[SEM_SPEC, SEM_SPEC] + [HBM_SPEC] * n + [pl.BlockSpec(memory_space=pltpu.VMEM)],
        out_shape=[sems, sems] + [pltpu.HBM(b.shape, b.dtype) for b in bufs] + [jax.ShapeDtypeStruct((8, LANES), F32)],
        input_output_aliases={k: 2 + k for k in range(n)}, compiler_params=IN_FLIGHT,
    )(*[_in_hbm(b) for b in bufs], after)
    return res[0], res[1], res[2:2 + n], res[-1]


def gather_wait(send, recv, bufs, after, layer):
    n = len(bufs)

    def body(*refs):
        ins = refs[:n]
        send_ref, recv_ref = refs[n], refs[n + 1]
        for start, arrival in _gather_copies(ins, send_ref, recv_ref):
            start.wait_send()
            arrival.wait_recv()

    return pl.pallas_call(
        body, name=f"gather_wait_{layer}",
        in_specs=[HBM_SPEC] * n + [SEM_SPEC, SEM_SPEC, _any()], out_specs=[HBM_SPEC] * n,
        out_shape=[pltpu.HBM(b.shape, b.dtype) for b in bufs],
        input_output_aliases={k: k for k in range(n)}, compiler_params=IN_FLIGHT,
    )(*bufs, send, recv, after)


def gather_forward(bufs):
    n = len(bufs)

    def body(*refs):
        outs = refs[n:2 * n]
        send, recv = refs[2 * n:]
        x, y, c, chips = _place()
        cps = []
        for k in range(n):
            rows = outs[k].shape[1]
            for j, (cx, cy) in enumerate(chips):
                sems = dict(send_sem=send.at[3 * k + j], recv_sem=recv.at[3 * k + j],
                            device_id=(x, y, 1 - c), device_id_type=MESH)
                mine = outs[k].at[2 * cx + cy, _half(rows, c), :]
                theirs = outs[k].at[2 * cx + cy, _half(rows, 1 - c), :]
                cp = pltpu.make_async_remote_copy(src_ref=mine, dst_ref=mine, **sems)
                cp.start()
                cps.append((cp, pltpu.make_async_remote_copy(src_ref=theirs, dst_ref=theirs, **sems)))
        for cp, arrival in cps:
            cp.wait_send()
            arrival.wait_recv()

    return pl.pallas_call(
        body, in_specs=[_any()] * n, out_specs=[_any()] * n,
        out_shape=[jax.ShapeDtypeStruct(b.shape, b.dtype) for b in bufs], input_output_aliases={k: k for k in range(n)},
        scratch_shapes=[pltpu.SemaphoreType.DMA((3 * n,)), pltpu.SemaphoreType.DMA((3 * n,))],
        name="gather_forward")(*bufs)


def _forward_copies(bufs, send, recv):
    x, y, c, chips = _place()
    out = []
    for k, buf in enumerate(bufs):
        rows = buf.shape[1]
        for j, (cx, cy) in enumerate(chips):
            sems = dict(send_sem=send.at[3 * k + j], recv_sem=recv.at[3 * k + j],
                        device_id=(x, y, 1 - c), device_id_type=MESH)
            mine = buf.at[2 * cx + cy, _half(rows, c), :]
            theirs = buf.at[2 * cx + cy, _half(rows, 1 - c), :]
            out.append((pltpu.make_async_remote_copy(src_ref=mine, dst_ref=mine, **sems),
                        pltpu.make_async_remote_copy(src_ref=theirs, dst_ref=theirs, **sems)))
    return out


def forward_start(bufs, tag):
    n = len(bufs)

    def body(*refs):
        ins = refs[:n]
        send, recv = refs[n], refs[n + 1]
        token = refs[-1]
        for start, _ in _forward_copies(ins, send, recv):
            start.start()
        token[...] = jnp.zeros_like(token)

    sems = pltpu.SemaphoreType.DMA((3 * n,))
    res = pl.pallas_call(
        body, name=f"forward_start_{tag}", in_specs=[HBM_SPEC] * n,
        out_specs=[SEM_SPEC, SEM_SPEC] + [HBM_SPEC] * n + [pl.BlockSpec(memory_space=pltpu.VMEM)],
        out_shape=[sems, sems] + [pltpu.HBM(b.shape, b.dtype) for b in bufs] + [jax.ShapeDtypeStruct((8, LANES), F32)],
        input_output_aliases={k: 2 + k for k in range(n)}, compiler_params=IN_FLIGHT,
    )(*[_in_hbm(b) for b in bufs])
    return res[0], res[1], res[2:2 + n], res[-1]


def forward_wait(send, recv, bufs, after, tag):
    n = len(bufs)

    def body(*refs):
        ins = refs[:n]
        send_ref, recv_ref = refs[n], refs[n + 1]
        for start, arrival in _forward_copies(ins, send_ref, recv_ref):
            start.wait_send()
            arrival.wait_recv()

    return pl.pallas_call(
        body, name=f"forward_wait_{tag}",
        in_specs=[HBM_SPEC] * n + [SEM_SPEC, SEM_SPEC, _any()], out_specs=[HBM_SPEC] * n,
        out_shape=[pltpu.HBM(b.shape, b.dtype) for b in bufs],
        input_output_aliases={k: k for k in range(n)}, compiler_params=IN_FLIGHT,
    )(*bufs, send, recv, after)


def _exchange_copies(srcs, lands, send, recv):
    x, y, c, _ = _place()
    return [pltpu.make_async_remote_copy(
        src_ref=src.at[:, _half(src.shape[1], 1 - c), :], dst_ref=land, send_sem=send.at[k], recv_sem=recv.at[k],
        device_id=(x, y, 1 - c), device_id_type=MESH) for k, (src, land) in enumerate(zip(srcs, lands))]


def exchange_start(srcs, tag):
    n = len(srcs)
    lands = [lax.empty((s.shape[0], s.shape[1] // 2, s.shape[2]), s.dtype) for s in srcs]

    def body(*refs):
        ins, land_refs = refs[:n], refs[n:2 * n]
        send, recv = refs[2 * n], refs[2 * n + 1]
        token = refs[-1]
        for cp in _exchange_copies(ins, land_refs, send, recv):
            cp.start()
        token[...] = jnp.zeros_like(token)

    sems = pltpu.SemaphoreType.DMA((n,))
    res = pl.pallas_call(
        body, name=f"exchange_start_{tag}",
        in_specs=[HBM_SPEC] * (2 * n),
        out_specs=[SEM_SPEC, SEM_SPEC] + [HBM_SPEC] * (2 * n) + [pl.BlockSpec(memory_space=pltpu.VMEM)],
        out_shape=[sems, sems] + [pltpu.HBM(a.shape, a.dtype) for a in list(srcs) + lands]
        + [jax.ShapeDtypeStruct((8, LANES), F32)],
        input_output_aliases={k: 2 + k for k in range(2 * n)}, compiler_params=IN_FLIGHT,
    )(*[_in_hbm(a) for a in list(srcs) + lands])
    return res[0], res[1], res[2:2 + n], res[2 + n:2 + 2 * n], res[-1]


def exchange_wait(send, recv, srcs, lands, after, tag):
    n = len(srcs)

    def body(*refs):
        ins, land_refs = refs[:n], refs[n:2 * n]
        send_ref, recv_ref = refs[2 * n], refs[2 * n + 1]
        for cp in _exchange_copies(ins, land_refs, send_ref, recv_ref):
            cp.wait_send()
            cp.wait_recv()

    res = pl.pallas_call(
        body, name=f"exchange_wait_{tag}",
        in_specs=[HBM_SPEC] * (2 * n) + [SEM_SPEC, SEM_SPEC, _any()], out_specs=[HBM_SPEC] * (2 * n),
        out_shape=[pltpu.HBM(a.shape, a.dtype) for a in list(srcs) + list(lands)],
        input_output_aliases={k: k for k in range(2 * n)}, compiler_params=IN_FLIGHT,
    )(*srcs, *lands, send, recv, after)
    return res[:n], res[n:]


def add_pair(gs, r1s, core):
    n = len(gs)

    def body(c_ref, *refs):
        del c_ref
        for g_ref, r_ref, o_ref in zip(refs[:n], refs[n:2 * n], refs[2 * n:]):
            o_ref[...] = (g_ref[...] + r_ref[...]).astype(BF16)

    blk = lambda r: (None,) + r.shape[1:]
    grid_spec = pltpu.PrefetchScalarGridSpec(
        num_scalar_prefetch=1, grid=(NCHIP,),
        in_specs=[pl.BlockSpec(blk(r), lambda s, c: (s, c[0], 0)) for r in r1s]
        + [pl.BlockSpec(blk(r), lambda s, c: (s, 0, 0)) for r in r1s],
        out_specs=[pl.BlockSpec(blk(r), lambda s, c: (s, 0, 0)) for r in r1s])
    return pl.pallas_call(body, grid_spec=grid_spec, out_shape=[jax.ShapeDtypeStruct(r.shape, BF16) for r in r1s],
                          compiler_params=_cp(("arbitrary",)), name="add_pair")(core, *gs, *r1s)


def _scatter_copies(srcs, lands, send, recv):
    _, _, c, chips = _place()
    out = []
    for k, (src, land) in enumerate(zip(srcs, lands)):
        for j, (cx, cy) in enumerate(chips):
            out.append(pltpu.make_async_remote_copy(
                src_ref=src.at[2 * cx + cy], dst_ref=land.at[j], send_sem=send.at[3 * k + j],
                recv_sem=recv.at[3 * k + j], device_id=(cx, cy, c), device_id_type=MESH))
    return out


def scatter_start(srcs, layer):
    n = len(srcs)
    srcs = list(srcs)
    lands = [lax.empty((3,) + s.shape[1:], s.dtype) for s in srcs]

    def body(*refs):
        ins, land_refs = refs[:n], refs[n:2 * n]
        send, recv = refs[2 * n], refs[2 * n + 1]
        token = refs[-1]
        for cp in _scatter_copies(ins, land_refs, send, recv):
            cp.start()
        token[...] = jnp.zeros_like(token)

    sems = pltpu.SemaphoreType.DMA((3 * n,))
    res = pl.pallas_call(
        body, name=f"scatter_start_{layer}",
        in_specs=[HBM_SPEC] * (2 * n),
        out_specs=[SEM_SPEC, SEM_SPEC] + [HBM_SPEC] * (2 * n) + [pl.BlockSpec(memory_space=pltpu.VMEM)],
        out_shape=[sems, sems] + [pltpu.HBM(a.shape, a.dtype) for a in srcs + lands]
        + [jax.ShapeDtypeStruct((8, LANES), F32)],
        input_output_aliases={k: 2 + k for k in range(2 * n)}, compiler_params=IN_FLIGHT,
    )(*[_in_hbm(a) for a in srcs + lands])
    return res[0], res[1], res[2:2 + n], res[2 + n:2 + 2 * n], res[-1]


def scatter_wait(send, recv, srcs, lands, after, layer):
    n = len(srcs)

    def body(*refs):
        ins, land_refs = refs[:n], refs[n:2 * n]
        send_ref, recv_ref = refs[2 * n], refs[2 * n + 1]
        for cp in _scatter_copies(ins, land_refs, send_ref, recv_ref):
            cp.wait_send()
            cp.wait_recv()

    res = pl.pallas_call(
        body, name=f"scatter_wait_{layer}",
        in_specs=[HBM_SPEC] * (2 * n) + [SEM_SPEC, SEM_SPEC, _any()], out_specs=[HBM_SPEC] * (2 * n),
        out_shape=[pltpu.HBM(a.shape, a.dtype) for a in list(srcs) + list(lands)],
        input_output_aliases={k: k for k in range(2 * n)}, compiler_params=IN_FLIGHT,
    )(*srcs, *lands, send, recv, after)
    return res[n:]


def add_chips(gs, r1s, r2s, place, totals, layer):
    n = len(gs)
    steps = 2

    def body(p_ref, *refs):
        del p_ref
        for g_ref, r1_ref, r2_ref, o_ref in zip(refs[:n], refs[n:2 * n], refs[2 * n:3 * n], refs[4 * n:]):
            own = g_ref[...] + r1_ref[...]
            o_ref[...] = ((own + r2_ref[0].astype(F32)) + r2_ref[1].astype(F32)) + r2_ref[2].astype(F32)

    blk = lambda r: (None, r.shape[1] // steps, r.shape[2])
    grid_spec = pltpu.PrefetchScalarGridSpec(
        num_scalar_prefetch=1, grid=(steps,),
        in_specs=[pl.BlockSpec(blk(r), lambda i, p: (p[1], p[0] * steps + i, 0)) for r in r1s]
        + [pl.BlockSpec(blk(r), lambda i, p: (p[1], i, 0)) for r in r1s]
        + [pl.BlockSpec((3,) + blk(r)[1:], lambda i, p: (0, i, 0)) for r in r1s] + [_any()] * n,
        out_specs=[pl.BlockSpec(blk(r), lambda i, p: (layer, p[0] * steps + i, 0)) for r in r1s])
    return pl.pallas_call(body, grid_spec=grid_spec, out_shape=[jax.ShapeDtypeStruct(t.shape, F32) for t in totals],
                          input_output_aliases={1 + 3 * n + k: k for k in range(n)},
                          compiler_params=_cp(("arbitrary",)), name="add_chips")(place, *gs, *r1s, *r2s, *totals)


def pair_share(gs):
    n = len(gs)

    def body(*refs):
        outs = refs[n:2 * n]
        send, recv = refs[2 * n:]
        x, y, c, _ = _place()
        cps = []
        for k in range(n):
            mine = outs[k].at[:, _half(outs[k].shape[1], c), :]
            cp = pltpu.make_async_remote_copy(
                src_ref=mine, dst_ref=mine, send_sem=send.at[k], recv_sem=recv.at[k],
                device_id=(x, y, 1 - c), device_id_type=MESH)
            cp.start()
            cps.append(cp)
        for k, cp in enumerate(cps):
            cp.wait_send()
            theirs = outs[k].at[:, _half(outs[k].shape[1], 1 - c), :]
            pltpu.make_async_remote_copy(
                src_ref=theirs, dst_ref=theirs, send_sem=send.at[k], recv_sem=recv.at[k],
                device_id=(x, y, 1 - c), device_id_type=MESH).wait_recv()

    return pl.pallas_call(
        body, in_specs=[_any()] * n, out_specs=[_any()] * n,
        out_shape=[jax.ShapeDtypeStruct(g.shape, g.dtype) for g in gs], input_output_aliases={k: k for k in range(n)},
        scratch_shapes=[pltpu.SemaphoreType.DMA((n,)), pltpu.SemaphoreType.DMA((n,))],
        name="pair_share")(*gs)


def small_collect(v, reduce, name):
    rows = v.shape[0]
    flips = [(fx, fy, fc) for fx in (0, 1) for fy in (0, 1) for fc in (0, 1)][1:]

    def body(v_ref, o_ref, buf, send, recv):
        x, y, c, _ = _place()
        buf[4 * x + 2 * y + c] = v_ref[...]
        peers = [(jnp.where(fx, 1 - x, x), jnp.where(fy, 1 - y, y), jnp.where(fc, 1 - c, c)) for fx, fy, fc in flips]
        cps = []
        for k, peer in enumerate(peers):
            cp = pltpu.make_async_remote_copy(
                src_ref=v_ref, dst_ref=buf.at[4 * x + 2 * y + c], send_sem=send.at[k], recv_sem=recv.at[k],
                device_id=peer, device_id_type=MESH)
            cp.start()
            cps.append(cp)
        for k, (px, py, pc) in enumerate(peers):
            pltpu.make_async_remote_copy(
                src_ref=v_ref, dst_ref=buf.at[4 * px + 2 * py + pc], send_sem=send.at[k], recv_sem=recv.at[k],
                device_id=(px, py, pc), device_id_type=MESH).wait_recv()
        for cp in cps:
            cp.wait_send()
        if reduce:
            acc = buf[0]
            for s in range(1, 8):
                acc = acc + buf[s]
            o_ref[...] = acc
        else:
            o_ref[...] = buf[...]

    vm = pl.BlockSpec(memory_space=pltpu.VMEM)
    out_shape = jax.ShapeDtypeStruct((rows, SMALL_COLS) if reduce else (8, rows, SMALL_COLS), F32)
    return pl.pallas_call(
        body, in_specs=[vm], out_specs=vm, out_shape=out_shape,
        scratch_shapes=[pltpu.VMEM((8, rows, SMALL_COLS), F32), pltpu.SemaphoreType.DMA((7,)),
                        pltpu.SemaphoreType.DMA((7,))],
        name=name)(v)


def adamw(w, g, m, v, rb, name):
    nl, rows, cols = w.shape

    def body(w_ref, g_ref, m_ref, v_ref, go_ref, d_ref, nm_ref, nv_ref):
        gv = g_ref[...]
        go_ref[...] = gv
        nm = ADAM_B1 * m_ref[...] + (1.0 - ADAM_B1) * gv
        nv = ADAM_B2 * v_ref[...] + (1.0 - ADAM_B2) * (gv * gv)
        m_hat = nm / (1.0 - ADAM_B1 ** ADAM_STEP)
        v_hat = nv / (1.0 - ADAM_B2 ** ADAM_STEP)
        d_ref[...] = -ADAM_LR * (m_hat / (jnp.sqrt(v_hat) + ADAM_EPS) + ADAM_WD * w_ref[...])
        nm_ref[...] = nm
        nv_ref[...] = nv

    blk = pl.BlockSpec((None, rb, cols), lambda l, r: (l, r, 0))
    shp = jax.ShapeDtypeStruct(w.shape, F32)
    return pl.pallas_call(body, grid=(nl, rows // rb), in_specs=[blk] * 4, out_specs=[blk] * 4, out_shape=[shp] * 4,
                          compiler_params=_cp(("arbitrary", "arbitrary")), name=name)(w, g, m, v)


def _pack(parts, rows):
    flat = jnp.concatenate([p.reshape(-1).astype(F32) for p in parts])
    return jnp.pad(flat, (0, rows * SMALL_COLS - flat.shape[0])).reshape(rows, SMALL_COLS)


def _unpack(vec, shapes):
    flat = vec.reshape(-1)
    out, off = [], 0
    for s in shapes:
        size = 1
        for d in s:
            size *= d
        out.append(flat[off:off + size].reshape(s))
        off += size
    return out


def kernel(x, w_in, w_conv, rel_bias, g_conv_out, g_attn_out, w_out, g_pre_mix, g_post_mix, g_pre_ffn, g_post_ffn, w_ffn_in, w_ffn_out, loss_target, m_w_in, m_w_conv, m_rel_bias, m_g_conv_out, m_g_attn_out, m_w_out, m_g_pre_mix, m_g_post_mix, m_g_pre_ffn, m_g_post_ffn, m_w_ffn_in, m_w_ffn_out, v_w_in, v_w_conv, v_rel_bias, v_g_conv_out, v_g_attn_out, v_w_out, v_g_pre_mix, v_g_post_mix, v_g_pre_ffn, v_g_post_ffn, v_w_ffn_in, v_w_ffn_out):
    xi, yi, ci = lax.axis_index("x"), lax.axis_index("y"), lax.axis_index("c")
    chip = 2 * xi + yi
    nl = w_in.shape[0]
    x0 = x[0]
    target = loss_target[0]
    cwl = CW // NCHIP

    chip1 = chip.reshape(1).astype(jnp.int32)
    own = [cast_to_slot([w_in, w_out, w_ffn_in, w_ffn_out], chip1, l) for l in range(nl)]
    wc_all = small_collect(_pack([w_conv], 8), False, "gather_w_conv")
    wc_full = wc_all[0::2].reshape(NCHIP, -1)[:, :nl * cwl * 3].reshape(NCHIP, nl, cwl, 3)
    wc_full = jnp.transpose(wc_full, (1, 0, 2, 3)).reshape(nl, CW, 3)
    wconv_t = jnp.pad(jnp.transpose(wc_full, (0, 2, 1)), ((0, 0), (0, 5), (0, 0)))
    gm = jnp.kron(jnp.eye(CW // HD, dtype=F32), jnp.full((HD, HD), 1.0 / HD, F32)).astype(BF16)
    row = lambda a, l: a[l][None, :]

    def token(t):
        return t[0:1, 0:1]

    def gather_finish(flight, after, tag):
        send, recv, bufs, _ = flight
        return gather_forward(gather_wait(send, recv, bufs, after, tag))

    first_mix = gather_start(own[0][:2], wc_all, "0m")
    first_ffn = gather_start(own[0][2:], first_mix[3], "0f")
    flight = to_sibling = None
    saved, weights = [], []
    h = x0
    for l in range(nl):
        if l == 0:
            gw_in, gw_out = gather_finish(first_mix, x0, "0m")
        elif l == 1:
            gw_in, gw_out, gw_fi, gw_fo = gather_finish(flight, h, l)
        else:
            gw_in, gw_out, gw_fi, gw_fo = forward_wait(*to_sibling[:3], h, l)
        gw_out = gw_out.reshape(D, D)
        g_pm, g_pf = row(g_pre_mix, l), row(g_pre_ffn, l)
        if l == 0:
            g_pm = g_pm + token(first_ffn[3])
        if l + 1 < nl:
            flight = gather_start(own[l + 1], first_ffn[3] if l == 0 else gw_in, l + 1)
            g_pm = g_pm + token(flight[3])
        bias2, bias2_bwd = bias_expand(_diag_vector(rel_bias[l]), (QG_FWD, QG_BWD))
        proj = fwd_inproj(h, g_pm, gw_in)
        xmid, o, lse, y, z = fwd_mix(h, proj, bias2, wconv_t[l], row(g_conv_out, l), row(g_attn_out, l),
                                     row(g_post_mix, l), gm, gw_out)
        if l == 0:
            gw_fi, gw_fo = gather_finish(first_ffn, xmid, "0f")
        elif l + 1 < nl:
            send, recv, bufs, _ = flight
            to_sibling = forward_start(gather_wait(send, recv, bufs, xmid, l + 1), l + 1)
            g_pf = g_pf + token(to_sibling[3])
        gw_fo = gw_fo.reshape(2, DFF // 2, D)
        gu, f, xout = fwd_ffn(xmid, g_pf, row(g_post_ffn, l), gw_fi, gw_fo)
        saved.append((h, proj, bias2_bwd, xmid, o, lse, y, z, gu, f))
        weights.append((gw_in, gw_out, gw_fi, gw_fo))
        h = xout
    dx, loss_blk = loss_head(h, target)

    core = ci.reshape(1).astype(jnp.int32)
    place = jnp.stack([ci, chip]).astype(jnp.int32)
    totals = [lax.empty(w.shape, F32) for w in (w_in, w_out, w_ffn_in, w_ffn_out)]
    small = {k: [None] * nl for k in ("co", "ao", "pm", "qm", "pf", "qf", "rel", "wc")}

    def reduce_begin(kinds, grads, tag):
        return kinds, exchange_start(grads, tag), tag

    def reduce_mid(state, after):
        kinds, (send, recv, srcs, lands, _), tag = state
        grads, from_sibling = exchange_wait(send, recv, srcs, lands, after, tag)
        return kinds, grads, from_sibling, scatter_start(add_pair(grads, from_sibling, core), tag), tag

    def reduce_end(state, after, totals, layer):
        kinds, grads, from_sibling, (send, recv, srcs, lands, _), tag = state
        from_chips = scatter_wait(send, recv, srcs, lands, after, tag)
        totals = list(totals)
        summed = add_chips(grads, from_sibling, from_chips, place, [totals[i] for i in kinds], layer)
        for i, t in zip(kinds, summed):
            totals[i] = t
        return totals

    begun = flying = None
    for l in reversed(range(nl)):
        hin, proj, bias2, xmid, o, lse, y, z, gu, f = saved[l]
        gw_in, gw_out, gw_fi, gw_fo = weights[l]
        g_qf, g_qm, wct = row(g_post_ffn, l), row(g_post_mix, l), wconv_t[l]
        if begun is not None:
            g_qf = g_qf + token(begun[1][4])
        dxm, dfb, act, dgu, h2, dg_qf, dg_pf = bwd_ffn(dx, f, xmid, gu, row(g_pre_ffn, l), g_qf, gw_fi, gw_fo)
        if begun is not None:
            flying = reduce_mid(begun, dxm)
            g_qm = g_qm + token(flying[3][4])
        gr_fo = wgrad(act, dfb, 256, D, False, "wgrad_ffn_out").reshape(NCHIP, DFF // NCHIP, D)
        gr_fi = wgrad(h2, dgu, 512, 2 * DFF // NCHIP, True, "wgrad_ffn_in")
        if l == 0:
            begun_ffn = reduce_begin([2, 3], [gr_fi, gr_fo], "0f")
            g_qm = g_qm + token(begun_ffn[1][4])
        dzb, do, dco, dbg, dg_qm, dg_co, dg_ao = bwd_mix(dxm, z, o, proj, wct, row(g_conv_out, l),
                                                          row(g_attn_out, l), g_qm, gm, gw_out)
        if l == 0:
            flying_ffn = reduce_mid(begun_ffn, dzb)
            wct = wct + token(flying_ffn[3][4])
        gr_out = wgrad(y, dzb, 512, D, False, "wgrad_out").reshape(NCHIP, D // NCHIP, D)
        dhc, dcg, dwc = bwd_conv(dco, proj, wct)
        dq, dk, dv, db2 = bwd_attn(proj, o, do, lse, bias2)
        dx, dproj, hb, dg_pm = bwd_inproj(dxm, hin, dhc, dbg, dcg, dq, dk, dv, row(g_pre_mix, l), gw_in)
        if flying is not None:
            totals = reduce_end(flying, dx, totals, l + 1)
        gr_in = wgrad(hb, dproj, 512, PROJ // NCHIP, True, "wgrad_in")
        small["co"][l], small["ao"][l], small["pm"][l], small["qm"][l] = dg_co, dg_ao, dg_pm, dg_qm
        small["pf"][l], small["qf"][l] = dg_pf, dg_qf
        small["rel"][l] = _diag_vector_bwd(bias_reduce(db2.reshape(NH, QG_BWD, QG_BWD + LEFT)))
        small["wc"][l] = jnp.transpose(dwc[0:3], (1, 0))
        if l > 0:
            begun = reduce_begin([0, 1, 2, 3], [gr_in, gr_out, gr_fi, gr_fo], l)
    flying_mix = reduce_mid(reduce_begin([0, 1], [gr_in, gr_out], "0m"), dx)
    totals = reduce_end(flying_ffn, flying_mix[3][4], totals, 0)
    totals = reduce_end(flying_mix, totals[2], totals, 0)
    gr_in, gr_out, gr_fi, gr_fo = pair_share(totals)

    order = ("co", "ao", "pm", "qm", "pf", "qf", "rel", "wc")
    parts = [jnp.stack(small[k]) for k in order] + [loss_blk[0:1, 0:1]]
    shapes = [p.shape for p in parts]
    red = _unpack(small_collect(_pack(parts, 40), True, "reduce_small"), shapes)
    gr_co, gr_ao, gr_pm, gr_qm, gr_pf, gr_qf, gr_rel, gr_wc_full, loss = red
    gr_co, gr_ao, gr_pm, gr_qm, gr_pf, gr_qf = [a.reshape(nl, -1) for a in (gr_co, gr_ao, gr_pm, gr_qm, gr_pf, gr_qf)]
    gr_wc = lax.dynamic_slice_in_dim(gr_wc_full, chip * cwl, cwl, axis=1)
    loss = loss.reshape(())

    big = []
    for w, g, m, v, name in ((w_in, gr_in, m_w_in, v_w_in, "adamw_in"), (w_out, gr_out, m_w_out, v_w_out, "adamw_out"),
                             (w_ffn_in, gr_fi, m_w_ffn_in, v_w_ffn_in, "adamw_ffn_in"),
                             (w_ffn_out, gr_fo, m_w_ffn_out, v_w_ffn_out, "adamw_ffn_out")):
        big.append(adamw(w, g, m, v, w.shape[1] // 4, name))
    sw = [g_conv_out, g_attn_out, g_pre_mix, g_post_mix, g_pre_ffn, g_post_ffn, rel_bias, w_conv]
    sg = [gr_co, gr_ao, gr_pm, gr_qm, gr_pf, gr_qf, gr_rel, gr_wc]
    sm = [m_g_conv_out, m_g_attn_out, m_g_pre_mix, m_g_post_mix, m_g_pre_ffn, m_g_post_ffn, m_rel_bias, m_w_conv]
    sv = [v_g_conv_out, v_g_attn_out, v_g_pre_mix, v_g_post_mix, v_g_pre_ffn, v_g_post_ffn, v_rel_bias, v_w_conv]
    sshapes = [a.shape for a in sw]
    packed = [_pack(a, 32)[None] for a in (sw, sg, sm, sv)]
    s_out = [_unpack(a[0], sshapes) for a in adamw(*packed, 32, "adamw_small")]

    def leaves(big_i, small_i):
        b_in, b_out, b_fi, b_fo = big_i
        s_co, s_ao, s_pm, s_qm, s_pf, s_qf, s_rel, s_wc = small_i
        return [b_in, s_wc, s_rel, s_co, s_ao, b_out, s_pm, s_qm, s_pf, s_qf, b_fi, b_fo]

    out = [loss, dx[None]]
    out += leaves([b[0] for b in big], sg)
    for i in range(1, 4):
        out += leaves([b[i] for b in big], s_out[i])
    return tuple(out)
```

```python
import functools

import jax
import jax.numpy as jnp
from jax import lax
from jax.experimental import pallas as pl
from jax.experimental.pallas import tpu as pltpu

F32 = jnp.float32
BF16 = jnp.bfloat16

D = 1024
PROJ = 3072
CW = 512
HD = 64
NH = 8
CHUNK = 64
BAND = 576
REL_CLIP = 128
NREL = 2 * REL_CLIP + 1
DFF = 2816
DEPTH = 4
NCHIP = 4
EPS = 1e-6
NEG_INF = -1e30

ADAM_LR = 0.001
ADAM_B1 = 0.9
ADAM_B2 = 0.999
ADAM_EPS = 1e-08
ADAM_WD = 0.01
ADAM_STEP = 10

V7X_VMEM_BYTES = 64 * 1024 * 1024
VMEM_LIMIT = V7X_VMEM_BYTES - 8 * 1024 * 1024
LANES = 128
QG_FWD = 4 * CHUNK
QG_BWD = 2 * CHUNK
LEFT = BAND - CHUNK
TQ = 512
TM = 256
SMALL_COLS = 1024
MESH = pl.DeviceIdType.MESH
NT = (((1,), (1,)), ((), ()))
TN = (((0,), (0,)), ((), ()))


def _cp(sem=None, vmem=VMEM_LIMIT):
    return pltpu.CompilerParams(dimension_semantics=sem, vmem_limit_bytes=vmem)


def _any():
    return pl.BlockSpec(memory_space=pl.ANY)


def _const(shape):
    nd = len(shape)
    return pl.BlockSpec(shape, lambda *_: (0,) * nd)


def _rms(v, g):
    r = lax.rsqrt(jnp.mean(v * v, axis=-1, keepdims=True) + EPS)
    return v * r * g


def _rms_bwd(dy, v, g):
    r = lax.rsqrt(jnp.mean(v * v, axis=-1, keepdims=True) + EPS)
    vh = v * r
    dg = jnp.sum(dy * vh, axis=0, keepdims=True)
    dvh = dy * g
    dv = r * (dvh - vh * jnp.mean(dvh * vh, axis=-1, keepdims=True))
    return dv, dg


def _group_mean(v, gm):
    hi = v.astype(BF16)
    lo = (v - hi.astype(F32)).astype(BF16)
    return jnp.dot(hi, gm, preferred_element_type=F32) + jnp.dot(lo, gm, preferred_element_type=F32)


def _group_rms_bwd(dy, v, g, gm):
    r = lax.rsqrt(_group_mean(v * v, gm) + EPS)
    vh = v * r
    dg = jnp.sum(dy * vh, axis=0, keepdims=True)
    dvh = dy * g
    dv = r * (dvh - vh * _group_mean(dvh * vh, gm))
    return dv, dg


def _head_masks(scale):
    lane = lax.broadcasted_iota(jnp.int32, (1, LANES), 1)
    return [jnp.where((lane >= HD * a) & (lane < HD * (a + 1)), scale, 0.0).astype(BF16) for a in range(2)]


class _Resident:
    def __init__(self, src, dst, sem):
        self.first = pl.program_id(0) == 0
        self.copy = pltpu.make_async_copy(src, dst, sem)
        self.dst = dst

        @pl.when(self.first)
        def _():
            self.copy.start()

    def read(self):
        @pl.when(self.first)
        def _():
            self.copy.wait()

        return self.dst[...]


def _conv_taps(u_prev, u, scr):
    n = u.shape[0]
    scr[0:16, :] = u_prev
    scr[16:16 + n, :] = u
    return scr[15:15 + n, :], scr[14:14 + n, :]


def fwd_inproj(x, g, w_all):
    t = x.shape[0]
    wc = PROJ // NCHIP

    def body(x_ref, g_ref, w_hbm, o_ref, w_v):
        @pl.when(pl.program_id(0) == 0)
        def _():
            pltpu.sync_copy(w_hbm, w_v)

        h = _rms(x_ref[...], g_ref[...]).astype(BF16)
        for b in range(NCHIP):
            o_ref[:, wc * b:wc * (b + 1)] = jnp.dot(h, w_v[b], preferred_element_type=F32).astype(BF16)

    return pl.pallas_call(
        body, grid=(t // TQ,),
        in_specs=[pl.BlockSpec((TQ, D), lambda i: (i, 0)), _const((1, D)), _any()],
        out_specs=pl.BlockSpec((TQ, PROJ), lambda i: (i, 0)),
        out_shape=jax.ShapeDtypeStruct((t, PROJ), BF16),
        scratch_shapes=[pltpu.VMEM((NCHIP, D, wc), BF16)],
        compiler_params=_cp(("arbitrary",)), name="fwd_inproj")(x, g, w_all)


def _attn_window_specs():
    return [
        pl.BlockSpec((TQ, CW), lambda i: (i, 3)),
        pl.BlockSpec((TQ, CW), lambda i: (jnp.maximum(i - 1, 0), 4)),
        pl.BlockSpec((TQ, CW), lambda i: (i, 4)),
        pl.BlockSpec((TQ, CW), lambda i: (jnp.maximum(i - 1, 0), 5)),
        pl.BlockSpec((TQ, CW), lambda i: (i, 5)),
    ]


def _conv_specs():
    return [
        pl.BlockSpec((TQ, 3 * CW), lambda i: (i, 0)),
        pl.BlockSpec((16, 3 * CW), lambda i: (jnp.maximum(i * (TQ // 16) - 1, 0), 0)),
    ]


def _conv_fwd(pc_ref, pcp_ref, wc_ref, scr, first):
    pc = pc_ref[...].astype(F32)
    hc, bg, cg = pc[:, :CW], pc[:, CW:2 * CW], pc[:, 2 * CW:]
    u = cg * hc
    pp = pcp_ref[...].astype(F32)
    u_prev = jnp.where(first, 0.0, pp[:, 2 * CW:] * pp[:, :CW])
    u1, u2 = _conv_taps(u_prev, u, scr)
    cout = wc_ref[0:1, :] * u2 + wc_ref[1:2, :] * u1 + wc_ref[2:3, :] * u
    return hc, bg, cg, u, u1, u2, cout


def _key_penalty(first, r0, kg):
    col = lax.broadcasted_iota(jnp.int32, (1, kg), 1)
    limit = jnp.where(first, TQ - r0, 0)
    return jnp.where(col < limit, NEG_INF, 0.0)


def fwd_mix(x, proj, bias2, wconv_t, g_co, g_ao, g_pm, gm, wout_all):
    t = x.shape[0]
    qg, kg = QG_FWD, QG_FWD + LEFT

    def body(x_ref, pc_ref, pcp_ref, q_ref, kp_ref, kc_ref, vp_ref, vc_ref, b2_ref, wc_ref, gco_ref, gao_ref, gpm_ref,
             gm_ref, wout_hbm, xmid_ref, o_ref, lse_ref, y_ref, z_ref, wout_v, kwin, vwin, cscr, sems):
        i = pl.program_id(0)
        first = i == 0
        wout = _Resident(wout_hbm, wout_v, sems.at[0])
        kwin[0:TQ, :] = kp_ref[...]
        kwin[TQ:2 * TQ, :] = kc_ref[...]
        vwin[0:TQ, :] = vp_ref[...]
        vwin[TQ:2 * TQ, :] = vc_ref[...]
        qmask = _head_masks(HD ** -0.5)
        low = lax.broadcasted_iota(jnp.int32, (1, LANES), 1) < HD

        def group(g, carry):
            r0 = pl.multiple_of(g * qg, qg)
            pen = _key_penalty(first, r0, kg)
            for hp in range(NH // 2):
                ls = slice(LANES * hp, LANES * (hp + 1))
                qb = q_ref[pl.ds(r0, qg), ls]
                q2 = jnp.concatenate([qb * qmask[0], qb * qmask[1]], axis=0)
                s = lax.dot_general(q2, kwin[pl.ds(r0, kg), ls], NT, preferred_element_type=F32)
                s = s + b2_ref[hp] + pen
                m = jnp.max(s, axis=-1, keepdims=True)
                p = jnp.exp(s - m)
                l = jnp.sum(p, axis=-1, keepdims=True)
                o2 = jnp.dot(p.astype(BF16), vwin[pl.ds(r0, kg), ls], preferred_element_type=F32) * (1.0 / l)
                lse2 = m + jnp.log(l)
                o_ref[pl.ds(r0, qg), ls] = jnp.where(low, o2[:qg], o2[qg:])
                lse_ref[pl.ds(r0, qg), ls] = jnp.where(low, lse2[:qg], lse2[qg:])
            return carry

        lax.fori_loop(0, TQ // qg, group, 0)

        _, bg, _, _, _, _, cout = _conv_fwd(pc_ref, pcp_ref, wc_ref, cscr, first)
        yc = bg * cout
        gmv = gm_ref[...]
        ycn = yc * lax.rsqrt(_group_mean(yc * yc, gmv) + EPS) * gco_ref[...]
        oa = o_ref[...]
        oan = oa * lax.rsqrt(_group_mean(oa * oa, gmv) + EPS) * gao_ref[...]
        y_ref[:, 0:CW] = ycn.astype(BF16)
        y_ref[:, CW:2 * CW] = oan.astype(BF16)
        z = jnp.dot(y_ref[...], wout.read(), preferred_element_type=F32)
        z_ref[...] = z
        xmid_ref[...] = x_ref[...] + _rms(z, gpm_ref[...])

    row = lambda w: pl.BlockSpec((TQ, w), lambda i: (i, 0))
    return pl.pallas_call(
        body, grid=(t // TQ,),
        in_specs=[row(D)] + _conv_specs() + _attn_window_specs() + [
            _const((NH // 2, 2 * qg, kg)), _const((8, CW)), _const((1, CW)), _const((1, CW)), _const((1, D)),
            _const((CW, CW)), _any()],
        out_specs=[row(D), row(CW), row(CW), row(D), row(D)],
        out_shape=[jax.ShapeDtypeStruct((t, D), F32), jax.ShapeDtypeStruct((t, CW), F32),
                   jax.ShapeDtypeStruct((t, CW), F32), jax.ShapeDtypeStruct((t, D), BF16),
                   jax.ShapeDtypeStruct((t, D), F32)],
        scratch_shapes=[pltpu.VMEM((D, D), BF16), pltpu.VMEM((2 * TQ, CW), BF16), pltpu.VMEM((2 * TQ, CW), BF16),
                        pltpu.VMEM((TQ + 16, CW), F32), pltpu.SemaphoreType.DMA((1,))],
        compiler_params=_cp(("arbitrary",)), name="fwd_mix",
    )(x, proj, proj, proj, proj, proj, proj, proj, bias2, wconv_t, g_co, g_ao, g_pm, gm, wout_all)


def fwd_ffn(xmid, g_pre, g_post, wfi_all, wfo_all):
    t = xmid.shape[0]
    hw = DFF // 2

    def body(x_ref, gpre_ref, gpost_ref, wfi_hbm, wfo_hbm, gu_ref, f_ref, xo_ref, wfi_v, wfo_v):
        @pl.when(pl.program_id(0) == 0)
        def _():
            pltpu.sync_copy(wfi_hbm, wfi_v)
            pltpu.sync_copy(wfo_hbm, wfo_v)

        xv = x_ref[...]
        h = _rms(xv, gpre_ref[...]).astype(BF16)
        f = jnp.zeros((TM, D), F32)
        for j in range(2):
            gate = jnp.dot(h, wfi_v[j], preferred_element_type=F32)
            up = jnp.dot(h, wfi_v[2 + j], preferred_element_type=F32)
            gu_ref[:, hw * j:hw * (j + 1)] = gate.astype(BF16)
            gu_ref[:, DFF + hw * j:DFF + hw * (j + 1)] = up.astype(BF16)
            act = gate * (1.0 / (1.0 + jnp.exp(-gate))) * up
            f = f + jnp.dot(act.astype(BF16), wfo_v[j], preferred_element_type=F32)
        f_ref[...] = f
        xo_ref[...] = xv + _rms(f, gpost_ref[...])

    row = lambda w: pl.BlockSpec((TM, w), lambda i: (i, 0))
    return pl.pallas_call(
        body, grid=(t // TM,),
        in_specs=[row(D), _const((1, D)), _const((1, D)), _any(), _any()],
        out_specs=[row(2 * DFF), row(D), row(D)],
        out_shape=[jax.ShapeDtypeStruct((t, 2 * DFF), BF16), jax.ShapeDtypeStruct((t, D), F32),
                   jax.ShapeDtypeStruct((t, D), F32)],
        scratch_shapes=[pltpu.VMEM((NCHIP, D, hw), BF16), pltpu.VMEM((2, hw, D), BF16)],
        compiler_params=_cp(("arbitrary",)), name="fwd_ffn")(xmid, g_pre, g_post, wfi_all, wfo_all)


def loss_head(y, target):
    t = y.shape[0]

    def body(y_ref, t_ref, dy_ref, l_ref):
        @pl.when(pl.program_id(0) == 0)
        def _():
            l_ref[...] = jnp.zeros_like(l_ref)

        e = y_ref[...] - t_ref[...]
        dy_ref[...] = e * (1.0 / D)
        rows = jnp.sum(e * e, axis=-1, keepdims=True) * (1.0 / D)
        l_ref[...] += 0.5 * jnp.sum(rows, axis=0, keepdims=True)

    row = pl.BlockSpec((TQ, D), lambda i: (i, 0))
    return pl.pallas_call(
        body, grid=(t // TQ,), in_specs=[row, row], out_specs=[row, _const((8, LANES))],
        out_shape=[jax.ShapeDtypeStruct((t, D), F32), jax.ShapeDtypeStruct((8, LANES), F32)],
        compiler_params=_cp(("arbitrary",)), name="loss_head")(y, target)


def bwd_ffn(dx, f, xmid, gu, g_pre, g_post, wfi_all, wfo_all):
    t = dx.shape[0]
    hw = DFF // 2

    def body(dx_ref, f_ref, x_ref, gu_ref, gpre_ref, gpost_ref, wfi_hbm, wfo_hbm,
             dxm_ref, df_ref, act_ref, dgu_ref, h_ref, dgpost_ref, dgpre_ref, wfi_v, wfo_v):
        @pl.when(pl.program_id(0) == 0)
        def _():
            pltpu.sync_copy(wfi_hbm, wfi_v)
            pltpu.sync_copy(wfo_hbm, wfo_v)
            dgpost_ref[...] = jnp.zeros_like(dgpost_ref)
            dgpre_ref[...] = jnp.zeros_like(dgpre_ref)

        dxo = dx_ref[...]
        df, dgp = _rms_bwd(dxo, f_ref[...], gpost_ref[...])
        dgpost_ref[...] += dgp
        dfb = df.astype(BF16)
        df_ref[...] = dfb
        dh = jnp.zeros((TM, D), F32)
        for j in range(2):
            dact = lax.dot_general(dfb, wfo_v[j], NT, preferred_element_type=F32)
            gate = gu_ref[:, hw * j:hw * (j + 1)].astype(F32)
            up = gu_ref[:, DFF + hw * j:DFF + hw * (j + 1)].astype(F32)
            sig = 1.0 / (1.0 + jnp.exp(-gate))
            silu = gate * sig
            act_ref[:, hw * j:hw * (j + 1)] = (silu * up).astype(BF16)
            dup = (dact * silu).astype(BF16)
            dgate = (dact * up * (sig * (1.0 + gate * (1.0 - sig)))).astype(BF16)
            dgu_ref[:, hw * j:hw * (j + 1)] = dgate
            dgu_ref[:, DFF + hw * j:DFF + hw * (j + 1)] = dup
            dh = dh + lax.dot_general(dgate, wfi_v[j], NT, preferred_element_type=F32)
            dh = dh + lax.dot_general(dup, wfi_v[2 + j], NT, preferred_element_type=F32)
        xv = x_ref[...]
        gpre = gpre_ref[...]
        h_ref[...] = _rms(xv, gpre).astype(BF16)
        dxv, dgq = _rms_bwd(dh, xv, gpre)
        dgpre_ref[...] += dgq
        dxm_ref[...] = dxo + dxv

    row = lambda w: pl.BlockSpec((TM, w), lambda i: (i, 0))
    return pl.pallas_call(
        body, grid=(t // TM,),
        in_specs=[row(D), row(D), row(D), row(2 * DFF), _const((1, D)), _const((1, D)), _any(), _any()],
        out_specs=[row(D), row(D), row(DFF), row(2 * DFF), row(D), _const((1, D)), _const((1, D))],
        out_shape=[jax.ShapeDtypeStruct((t, D), F32), jax.ShapeDtypeStruct((t, D), BF16),
                   jax.ShapeDtypeStruct((t, DFF), BF16), jax.ShapeDtypeStruct((t, 2 * DFF), BF16),
                   jax.ShapeDtypeStruct((t, D), BF16), jax.ShapeDtypeStruct((1, D), F32),
                   jax.ShapeDtypeStruct((1, D), F32)],
        scratch_shapes=[pltpu.VMEM((NCHIP, D, hw), BF16), pltpu.VMEM((2, hw, D), BF16)],
        compiler_params=_cp(("arbitrary",)), name="bwd_ffn")(dx, f, xmid, gu, g_pre, g_post, wfi_all, wfo_all)


def bwd_mix(dxm, z, o, proj, wconv_t, g_co, g_ao, g_pm, gm, wout_all):
    t = dxm.shape[0]

    def body(dx_ref, z_ref, o_ref, pc_ref, pcp_ref, wc_ref, gco_ref, gao_ref, gpm_ref, gm_ref, wout_hbm,
             dz_ref, do_ref, dco_ref, dbg_ref, dgpm_ref, dgco_ref, dgao_ref, wout_v, cscr):
        first = pl.program_id(0) == 0

        @pl.when(first)
        def _():
            pltpu.sync_copy(wout_hbm, wout_v)
            dgpm_ref[...] = jnp.zeros_like(dgpm_ref)
            dgco_ref[...] = jnp.zeros_like(dgco_ref)
            dgao_ref[...] = jnp.zeros_like(dgao_ref)

        dz, dgp = _rms_bwd(dx_ref[...], z_ref[...], gpm_ref[...])
        dgpm_ref[...] += dgp
        dzb = dz.astype(BF16)
        dz_ref[...] = dzb
        gmv = gm_ref[...]
        _, bg, _, _, _, _, cout = _conv_fwd(pc_ref, pcp_ref, wc_ref, cscr, first)
        dy_conv = lax.dot_general(dzb, wout_v[0:CW, :], NT, preferred_element_type=F32)
        dyc, dgc = _group_rms_bwd(dy_conv, bg * cout, gco_ref[...], gmv)
        dgco_ref[...] += dgc
        dbg_ref[...] = (dyc * cout).astype(BF16)
        dco_ref[...] = dyc * bg
        dy_attn = lax.dot_general(dzb, wout_v[CW:2 * CW, :], NT, preferred_element_type=F32)
        do, dga = _group_rms_bwd(dy_attn, o_ref[...], gao_ref[...], gmv)
        dgao_ref[...] += dga
        do_ref[...] = do.astype(BF16)

    row = lambda w: pl.BlockSpec((TQ, w), lambda i: (i, 0))
    return pl.pallas_call(
        body, grid=(t // TQ,),
        in_specs=[row(D), row(D), row(CW)] + _conv_specs() + [
            _const((8, CW)), _const((1, CW)), _const((1, CW)), _const((1, D)), _const((CW, CW)), _any()],
        out_specs=[row(D), row(CW), row(CW), row(CW), _const((1, D)), _const((1, CW)), _const((1, CW))],
        out_shape=[jax.ShapeDtypeStruct((t, D), BF16), jax.ShapeDtypeStruct((t, CW), BF16),
                   jax.ShapeDtypeStruct((t, CW), F32), jax.ShapeDtypeStruct((t, CW), BF16),
                   jax.ShapeDtypeStruct((1, D), F32), jax.ShapeDtypeStruct((1, CW), F32),
                   jax.ShapeDtypeStruct((1, CW), F32)],
        scratch_shapes=[pltpu.VMEM((D, D), BF16), pltpu.VMEM((TQ + 16, CW), F32)],
        compiler_params=_cp(("arbitrary",)), name="bwd_mix",
    )(dxm, z, o, proj, proj, wconv_t, g_co, g_ao, g_pm, gm, wout_all)


def bwd_conv(dco, proj, wconv_t):
    t = dco.shape[0]
    nt = t // TQ

    def body(d_ref, dn_ref, pc_ref, pcp_ref, wc_ref, dhc_ref, dcg_ref, dw_ref, cscr, dscr):
        i = pl.program_id(0)
        first = i == 0

        @pl.when(first)
        def _():
            dw_ref[...] = jnp.zeros_like(dw_ref)

        hc, _, cg, u, u1, u2, _ = _conv_fwd(pc_ref, pcp_ref, wc_ref, cscr, first)
        d0 = d_ref[...]
        dscr[0:TQ, :] = d0
        dscr[TQ:TQ + 8, :] = jnp.where(i == nt - 1, 0.0, dn_ref[...])
        d1 = dscr[1:TQ + 1, :]
        d2 = dscr[2:TQ + 2, :]
        du = wc_ref[2:3, :] * d0 + wc_ref[1:2, :] * d1 + wc_ref[0:1, :] * d2
        dhc_ref[...] = (du * cg).astype(BF16)
        dcg_ref[...] = (du * hc).astype(BF16)
        dw_ref[0:1, :] += jnp.sum(d0 * u2, axis=0, keepdims=True)
        dw_ref[1:2, :] += jnp.sum(d0 * u1, axis=0, keepdims=True)
        dw_ref[2:3, :] += jnp.sum(d0 * u, axis=0, keepdims=True)

    row = lambda w: pl.BlockSpec((TQ, w), lambda i: (i, 0))
    nxt = pl.BlockSpec((8, CW), lambda i: (jnp.minimum((i + 1) * (TQ // 8), t // 8 - 1), 0))
    return pl.pallas_call(
        body, grid=(nt,),
        in_specs=[row(CW), nxt] + _conv_specs() + [_const((8, CW))],
        out_specs=[row(CW), row(CW), _const((8, CW))],
        out_shape=[jax.ShapeDtypeStruct((t, CW), BF16), jax.ShapeDtypeStruct((t, CW), BF16),
                   jax.ShapeDtypeStruct((8, CW), F32)],
        scratch_shapes=[pltpu.VMEM((TQ + 16, CW), F32), pltpu.VMEM((TQ + 8, CW), F32)],
        compiler_params=_cp(("arbitrary",)), name="bwd_conv")(dco, dco, proj, proj, wconv_t)


def bwd_attn(proj, o, do, lse, bias2):
    t = o.shape[0]
    nt = t // TQ
    qg, kg = QG_BWD, QG_BWD + LEFT

    def body(q_ref, kp_ref, kc_ref, vp_ref, vc_ref, o_ref, do_ref, lse_ref, b2_ref,
             dq_ref, dk_hbm, dv_hbm, db_hbm, kwin, vwin, dk_acc, dv_acc, db_acc):
        i = pl.program_id(0)
        first = i == 0

        @pl.when(first)
        def _():
            dk_acc[...] = jnp.zeros_like(dk_acc)
            dv_acc[...] = jnp.zeros_like(dv_acc)
            db_acc[...] = jnp.zeros_like(db_acc)

        kwin[0:TQ, :] = kp_ref[...]
        kwin[TQ:2 * TQ, :] = kc_ref[...]
        vwin[0:TQ, :] = vp_ref[...]
        vwin[TQ:2 * TQ, :] = vc_ref[...]
        scale = HD ** -0.5
        qmask = _head_masks(scale)
        vmask = _head_masks(1.0)
        low = lax.broadcasted_iota(jnp.int32, (1, LANES), 1) < HD

        def group(g, carry):
            r0 = pl.multiple_of(g * qg, qg)
            base = pl.multiple_of(i * TQ + r0, qg)
            pen = _key_penalty(first, r0, kg)
            for hp in range(NH // 2):
                ls = slice(LANES * hp, LANES * (hp + 1))
                qb = q_ref[pl.ds(r0, qg), ls]
                kw = kwin[pl.ds(r0, kg), ls]
                dob = do_ref[pl.ds(r0, qg), ls]
                prod = dob.astype(F32) * o_ref[pl.ds(r0, qg), ls]
                lseb = lse_ref[pl.ds(r0, qg), ls]
                q2 = jnp.concatenate([qb * qmask[0], qb * qmask[1]], axis=0)
                do2 = jnp.concatenate([dob * vmask[0], dob * vmask[1]], axis=0)
                lse2 = jnp.concatenate([lseb[:, 0:1], lseb[:, HD:HD + 1]], axis=0)
                dsum = jnp.concatenate([jnp.sum(jnp.where(low, prod, 0.0), axis=-1, keepdims=True),
                                        jnp.sum(jnp.where(low, 0.0, prod), axis=-1, keepdims=True)], axis=0)
                s = lax.dot_general(q2, kw, NT, preferred_element_type=F32) + b2_ref[hp] + pen
                p = jnp.exp(s - lse2)
                dp = lax.dot_general(do2, vwin[pl.ds(r0, kg), ls], NT, preferred_element_type=F32)
                ds = p * (dp - dsum)
                db_acc[hp] += ds
                dsb = ds.astype(BF16)
                dq2 = jnp.dot(dsb, kw, preferred_element_type=F32)
                dq_ref[pl.ds(r0, qg), ls] = (jnp.where(low, dq2[:qg], dq2[qg:]) * scale).astype(BF16)
                dk_acc[pl.ds(base, kg), ls] += lax.dot_general(dsb, q2, TN, preferred_element_type=F32)
                dv_acc[pl.ds(base, kg), ls] += lax.dot_general(p.astype(BF16), do2, TN, preferred_element_type=F32)
            return carry

        lax.fori_loop(0, TQ // qg, group, 0)

        @pl.when(i == nt - 1)
        def _():
            pltpu.sync_copy(dk_acc, dk_hbm)
            pltpu.sync_copy(dv_acc, dv_hbm)
            pltpu.sync_copy(db_acc, db_hbm)

    row = lambda w: pl.BlockSpec((TQ, w), lambda i: (i, 0))
    return pl.pallas_call(
        body, grid=(nt,),
        in_specs=_attn_window_specs() + [row(CW), row(CW), row(CW), _const((NH // 2, 2 * qg, kg))],
        out_specs=[row(CW), _any(), _any(), _any()],
        out_shape=[jax.ShapeDtypeStruct((t, CW), BF16), jax.ShapeDtypeStruct((t + TQ, CW), F32),
                   jax.ShapeDtypeStruct((t + TQ, CW), F32), jax.ShapeDtypeStruct((NH // 2, 2 * qg, kg), F32)],
        scratch_shapes=[pltpu.VMEM((2 * TQ, CW), BF16), pltpu.VMEM((2 * TQ, CW), BF16),
                        pltpu.VMEM((t + TQ, CW), F32), pltpu.VMEM((t + TQ, CW), F32),
                        pltpu.VMEM((NH // 2, 2 * qg, kg), F32)],
        compiler_params=_cp(("arbitrary",)), name="bwd_attn",
    )(proj, proj, proj, proj, proj, o, do, lse, bias2)


def bwd_inproj(dxm, x, dhc, dbg, dcg, dq, dk, dv, g, w_all):
    t = x.shape[0]
    wc = PROJ // NCHIP

    def body(dxm_ref, x_ref, dhc_ref, dbg_ref, dcg_ref, dq_ref, dk_ref, dv_ref, g_ref, w_hbm,
             dx_ref, dp_ref, h_ref, dg_ref, w_v):
        @pl.when(pl.program_id(0) == 0)
        def _():
            pltpu.sync_copy(w_hbm, w_v)
            dg_ref[...] = jnp.zeros_like(dg_ref)

        dp_ref[:, 0:CW] = dhc_ref[...]
        dp_ref[:, CW:2 * CW] = dbg_ref[...]
        dp_ref[:, 2 * CW:3 * CW] = dcg_ref[...]
        dp_ref[:, 3 * CW:4 * CW] = dq_ref[...]
        dp_ref[:, 4 * CW:5 * CW] = dk_ref[...].astype(BF16)
        dp_ref[:, 5 * CW:6 * CW] = dv_ref[...].astype(BF16)
        dh = jnp.zeros((TQ, D), F32)
        for b in range(NCHIP):
            dh = dh + lax.dot_general(dp_ref[:, wc * b:wc * (b + 1)], w_v[b], NT, preferred_element_type=F32)
        xv = x_ref[...]
        gv = g_ref[...]
        h_ref[...] = _rms(xv, gv).astype(BF16)
        dxv, dgv = _rms_bwd(dh, xv, gv)
        dg_ref[...] += dgv
        dx_ref[...] = dxm_ref[...] + dxv

    row = lambda w: pl.BlockSpec((TQ, w), lambda i: (i, 0))
    pad = pl.BlockSpec((TQ, CW), lambda i: (i + 1, 0))
    return pl.pallas_call(
        body, grid=(t // TQ,),
        in_specs=[row(D), row(D), row(CW), row(CW), row(CW), row(CW), pad, pad, _const((1, D)), _any()],
        out_specs=[row(D), row(PROJ), row(D), _const((1, D))],
        out_shape=[jax.ShapeDtypeStruct((t, D), F32), jax.ShapeDtypeStruct((t, PROJ), BF16),
                   jax.ShapeDtypeStruct((t, D), BF16), jax.ShapeDtypeStruct((1, D), F32)],
        scratch_shapes=[pltpu.VMEM((NCHIP, D, wc), BF16)],
        compiler_params=_cp(("arbitrary",)), name="bwd_inproj",
    )(dxm, x, dhc, dbg, dcg, dq, dk, dv, g, w_all)


def wgrad(a, b, kb, nb, by_columns, name):
    t, k = a.shape
    n = b.shape[1]
    tk = 512

    def body(a_ref, b_ref, o_ref):
        o_ref[...] = jnp.zeros_like(o_ref)
        for c in range(t // tk):
            o_ref[...] += lax.dot_general(a_ref[tk * c:tk * (c + 1), :], b_ref[tk * c:tk * (c + 1), :], TN,
                                          preferred_element_type=F32)

    if by_columns:
        assert nb == n // NCHIP
        out_spec = pl.BlockSpec((None, kb, nb), lambda ki, ni: (ni, ki, 0))
        out_shape = jax.ShapeDtypeStruct((NCHIP, k, nb), F32)
    else:
        assert nb == n
        out_spec = pl.BlockSpec((kb, nb), lambda ki, ni: (ki, 0))
        out_shape = jax.ShapeDtypeStruct((k, n), F32)
    return pl.pallas_call(
        body, grid=(k // kb, n // nb),
        in_specs=[pl.BlockSpec((t, kb), lambda ki, ni: (0, ki)), pl.BlockSpec((t, nb), lambda ki, ni: (0, ni))],
        out_specs=out_spec, out_shape=out_shape,
        compiler_params=_cp(("arbitrary", "arbitrary")), name=name)(a, b)


TOE = 1024
assert 2 * QG_FWD + LEFT <= TOE
N_FLAT = LEFT - REL_CLIP + 1
N_VAR = BAND - N_FLAT


def _diag_vector(table):
    last = table[:, 2 * REL_CLIP:]
    var = table[:, 2 * REL_CLIP - N_VAR:2 * REL_CLIP][:, ::-1]
    return jnp.concatenate([jnp.broadcast_to(last, (NH, N_FLAT)), var, jnp.broadcast_to(last, (NH, TOE - BAND))], axis=1)


def _diag_vector_bwd(dvec):
    dlast = jnp.sum(dvec[:, :N_FLAT], axis=1, keepdims=True) + jnp.sum(dvec[:, BAND:], axis=1, keepdims=True)
    dvar = dvec[:, N_FLAT:BAND][:, ::-1]
    return jnp.concatenate([jnp.zeros((NH, 2 * REL_CLIP - N_VAR), F32), dvar, dlast], axis=1)


def _band_valid(qg):
    r = lax.broadcasted_iota(jnp.int32, (qg, qg + LEFT), 0)
    p = lax.broadcasted_iota(jnp.int32, (qg, qg + LEFT), 1)
    start = lax.shift_left(lax.shift_right_logical(r, 6), 6)
    return (p >= start) & (p < start + BAND)


def bias_expand(vec, qgs):
    def body(v_ref, *o_refs):
        for qg, o_ref in zip(qgs, o_refs):
            valid = _band_valid(qg)
            for h in range(NH):
                rows = jnp.broadcast_to(v_ref[h:h + 1, :], (qg, TOE))
                toe = pltpu.roll(rows, 0, 1, stride=1, stride_axis=0)
                o_ref[h // 2, qg * (h % 2):qg * (h % 2 + 1), :] = jnp.where(valid, toe[:, :qg + LEFT], NEG_INF)

    return pl.pallas_call(body, out_shape=[jax.ShapeDtypeStruct((NH // 2, 2 * qg, qg + LEFT), F32) for qg in qgs],
                          name="bias_expand")(vec)


def bias_reduce(db2):
    _, qg, kg = db2.shape

    def body(d_ref, o_ref):
        ii = lax.broadcasted_iota(jnp.int32, (kg, kg), 0)
        jj = lax.broadcasted_iota(jnp.int32, (kg, kg), 1)
        flip = jnp.where(ii + jj == kg - 1, 1.0, 0.0).astype(BF16)
        for h in range(NH):
            rest = d_ref[h]
            rev = jnp.zeros((qg, kg), F32)
            for _ in range(3):
                term = rest.astype(BF16)
                rev = rev + jnp.dot(term, flip, preferred_element_type=F32)
                rest = rest - term.astype(F32)
            d = jnp.concatenate([jnp.zeros((qg, TOE - kg), F32), rev], axis=1)
            back = pltpu.roll(d, 0, 1, stride=1, stride_axis=0)
            o_ref[h:h + 1, :] = jnp.sum(back, axis=0, keepdims=True)

    rev = pl.pallas_call(body, out_shape=jax.ShapeDtypeStruct((NH, TOE), F32), name="bias_reduce")(db2)
    return rev[:, ::-1]


def _place():
    x, y, c = lax.axis_index("x"), lax.axis_index("y"), lax.axis_index("c")
    chips = [(1 - x, y), (x, 1 - y), (1 - x, 1 - y)]
    return x, y, c, chips


def _half(ref_rows, c):
    return pl.ds(c * (ref_rows // 2), ref_rows // 2)


HBM_SPEC = pl.BlockSpec(memory_space=pltpu.HBM)
SEM_SPEC = pl.BlockSpec(memory_space=pltpu.SEMAPHORE)
IN_FLIGHT = pltpu.CompilerParams(has_side_effects=pltpu.SideEffectType.DATAFLOW_SIDE_EFFECTING)


def _in_hbm(a):
    return pltpu.with_memory_space_constraint(a, pltpu.HBM)


def cast_to_slot(ws, chip, layer):
    n = len(ws)
    steps = 4

    def body(b_ref, *refs):
        del b_ref
        for w_ref, o_ref in zip(refs[:n], refs[n:]):
            o_ref[...] = w_ref[...].astype(BF16)

    grid_spec = pltpu.PrefetchScalarGridSpec(
        num_scalar_prefetch=1, grid=(steps,),
        in_specs=[pl.BlockSpec((None, w.shape[1] // steps, w.shape[2]), lambda r, b: (layer, r, 0)) for w in ws],
        out_specs=[pl.BlockSpec((None, w.shape[1] // steps, w.shape[2]), lambda r, b: (b[0], r, 0)) for w in ws])
    return pl.pallas_call(body, grid_spec=grid_spec,
                          out_shape=[jax.ShapeDtypeStruct((NCHIP,) + w.shape[1:], BF16) for w in ws],
                          compiler_params=_cp(("arbitrary",)), name="cast_to_slot")(chip, *ws)


def _gather_copies(bufs, send, recv):
    x, y, c, chips = _place()
    b = 2 * x + y
    out = []
    for k, buf in enumerate(bufs):
        rows = buf.shape[1]
        mine = buf.at[b, _half(rows, c), :]
        for j, (cx, cy) in enumerate(chips):
            theirs = buf.at[2 * cx + cy, _half(rows, c), :]
            sems = dict(send_sem=send.at[3 * k + j], recv_sem=recv.at[3 * k + j],
                        device_id=(cx, cy, c), device_id_type=MESH)
            out.append((pltpu.make_async_remote_copy(src_ref=mine, dst_ref=mine, **sems),
                        pltpu.make_async_remote_copy(src_ref=theirs, dst_ref=theirs, **sems)))
    return out


def gather_start(bufs, after, layer):
    n = len(bufs)

    def body(*refs):
        ins = refs[:n]
        send, recv = refs[n + 1], refs[n + 2]
        token = refs[-1]
        for start, _ in _gather_copies(ins, send, recv):
            start.start()
        token[...] = jnp.zeros_like(token)

    sems = pltpu.SemaphoreType.DMA((3 * n,))
    res = pl.pallas_call(
        body, name=f"gather_start_{layer}",
        in_specs=[HBM_SPEC] * n + [_any()],
        out_specs=[SEM_SPEC, SEM_SPEC] + [HBM_SPEC] * n + [pl.BlockSpec(memory_space=pltpu.VMEM)],
        out_shape=[sems, sems] + [pltpu.HBM(b.shape, b.dtype) for b in bufs] + [jax.ShapeDtypeStruct((8, LANES), F32)],
        input_output_aliases={k: 2 + k for k in range(n)}, compiler_params=IN_FLIGHT,
    )(*[_in_hbm(b) for b in bufs], after)
    return res[0], res[1], res[2:2 + n], res[-1]


def gather_wait(send, recv, bufs, after, layer):
    n = len(bufs)

    def body(*refs):
        ins = refs[:n]
        send_ref, recv_ref = refs[n], refs[n + 1]
        for start, arrival in _gather_copies(ins, send_ref, recv_ref):
            start.wait_send()
            arrival.wait_recv()

    return pl.pallas_call(
        body, name=f"gather_wait_{layer}",
        in_specs=[HBM_SPEC] * n + [SEM_SPEC, SEM_SPEC, _any()], out_specs=[HBM_SPEC] * n,
        out_shape=[pltpu.HBM(b.shape, b.dtype) for b in bufs],
        input_output_aliases={k: k for k in range(n)}, compiler_params=IN_FLIGHT,
    )(*bufs, send, recv, after)


def gather_forward(bufs):
    n = len(bufs)

    def body(*refs):
        outs = refs[n:2 * n]
        send, recv = refs[2 * n:]
        x, y, c, chips = _place()
        cps = []
        for k in range(n):
            rows = outs[k].shape[1]
            for j, (cx, cy) in enumerate(chips):
                sems = dict(send_sem=send.at[3 * k + j], recv_sem=recv.at[3 * k + j],
                            device_id=(x, y, 1 - c), device_id_type=MESH)
                mine = outs[k].at[2 * cx + cy, _half(rows, c), :]
                theirs = outs[k].at[2 * cx + cy, _half(rows, 1 - c), :]
                cp = pltpu.make_async_remote_copy(src_ref=mine, dst_ref=mine, **sems)
                cp.start()
                cps.append((cp, pltpu.make_async_remote_copy(src_ref=theirs, dst_ref=theirs, **sems)))
        for cp, arrival in cps:
            cp.wait_send()
            arrival.wait_recv()

    return pl.pallas_call(
        body, in_specs=[_any()] * n, out_specs=[_any()] * n,
        out_shape=[jax.ShapeDtypeStruct(b.shape, b.dtype) for b in bufs], input_output_aliases={k: k for k in range(n)},
        scratch_shapes=[pltpu.SemaphoreType.DMA((3 * n,)), pltpu.SemaphoreType.DMA((3 * n,))],
        name="gather_forward")(*bufs)


def _forward_copies(bufs, send, recv):
    x, y, c, chips = _place()
    out = []
    for k, buf in enumerate(bufs):
        rows = buf.shape[1]
        for j, (cx, cy) in enumerate(chips):
            sems = dict(send_sem=send.at[3 * k + j], recv_sem=recv.at[3 * k + j],
                        device_id=(x, y, 1 - c), device_id_type=MESH)
            mine = buf.at[2 * cx + cy, _half(rows, c), :]
            theirs = buf.at[2 * cx + cy, _half(rows, 1 - c), :]
            out.append((pltpu.make_async_remote_copy(src_ref=mine, dst_ref=mine, **sems),
                        pltpu.make_async_remote_copy(src_ref=theirs, dst_ref=theirs, **sems)))
    return out


def forward_start(bufs, tag):
    n = len(bufs)

    def body(*refs):
        ins = refs[:n]
        send, recv = refs[n], refs[n + 1]
        token = refs[-1]
        for start, _ in _forward_copies(ins, send, recv):
            start.start()
        token[...] = jnp.zeros_like(token)

    sems = pltpu.SemaphoreType.DMA((3 * n,))
    res = pl.pallas_call(
        body, name=f"forward_start_{tag}", in_specs=[HBM_SPEC] * n,
        out_specs=[SEM_SPEC, SEM_SPEC] + [HBM_SPEC] * n + [pl.BlockSpec(memory_space=pltpu.VMEM)],
        out_shape=[sems, sems] + [pltpu.HBM(b.shape, b.dtype) for b in bufs] + [jax.ShapeDtypeStruct((8, LANES), F32)],
        input_output_aliases={k: 2 + k for k in range(n)}, compiler_params=IN_FLIGHT,
    )(*[_in_hbm(b) for b in bufs])
    return res[0], res[1], res[2:2 + n], res[-1]


def forward_wait(send, recv, bufs, after, tag):
    n = len(bufs)

    def body(*refs):
        ins = refs[:n]
        send_ref, recv_ref = refs[n], refs[n + 1]
        for start, arrival in _forward_copies(ins, send_ref, recv_ref):
            start.wait_send()
            arrival.wait_recv()

    return pl.pallas_call(
        body, name=f"forward_wait_{tag}",
        in_specs=[HBM_SPEC] * n + [SEM_SPEC, SEM_SPEC, _any()], out_specs=[HBM_SPEC] * n,
        out_shape=[pltpu.HBM(b.shape, b.dtype) for b in bufs],
        input_output_aliases={k: k for k in range(n)}, compiler_params=IN_FLIGHT,
    )(*bufs, send, recv, after)


def _exchange_copies(srcs, lands, send, recv):
    x, y, c, _ = _place()
    return [pltpu.make_async_remote_copy(
        src_ref=src.at[:, _half(src.shape[1], 1 - c), :], dst_ref=land, send_sem=send.at[k], recv_sem=recv.at[k],
        device_id=(x, y, 1 - c), device_id_type=MESH) for k, (src, land) in enumerate(zip(srcs, lands))]


def exchange_start(srcs, tag):
    n = len(srcs)
    lands = [lax.empty((s.shape[0], s.shape[1] // 2, s.shape[2]), s.dtype) for s in srcs]

    def body(*refs):
        ins, land_refs = refs[:n], refs[n:2 * n]
        send, recv = refs[2 * n], refs[2 * n + 1]
        token = refs[-1]
        for cp in _exchange_copies(ins, land_refs, send, recv):
            cp.start()
        token[...] = jnp.zeros_like(token)

    sems = pltpu.SemaphoreType.DMA((n,))
    res = pl.pallas_call(
        body, name=f"exchange_start_{tag}",
        in_specs=[HBM_SPEC] * (2 * n),
        out_specs=[SEM_SPEC, SEM_SPEC] + [HBM_SPEC] * (2 * n) + [pl.BlockSpec(memory_space=pltpu.VMEM)],
        out_shape=[sems, sems] + [pltpu.HBM(a.shape, a.dtype) for a in list(srcs) + lands]
        + [jax.ShapeDtypeStruct((8, LANES), F32)],
        input_output_aliases={k: 2 + k for k in range(2 * n)}, compiler_params=IN_FLIGHT,
    )(*[_in_hbm(a) for a in list(srcs) + lands])
    return res[0], res[1], res[2:2 + n], res[2 + n:2 + 2 * n], res[-1]


def exchange_wait(send, recv, srcs, lands, after, tag):
    n = len(srcs)

    def body(*refs):
        ins, land_refs = refs[:n], refs[n:2 * n]
        send_ref, recv_ref = refs[2 * n], refs[2 * n + 1]
        for cp in _exchange_copies(ins, land_refs, send_ref, recv_ref):
            cp.wait_send()
            cp.wait_recv()

    res = pl.pallas_call(
        body, name=f"exchange_wait_{tag}",
        in_specs=[HBM_SPEC] * (2 * n) + [SEM_SPEC, SEM_SPEC, _any()], out_specs=[HBM_SPEC] * (2 * n),
        out_shape=[pltpu.HBM(a.shape, a.dtype) for a in list(srcs) + list(lands)],
        input_output_aliases={k: k for k in range(2 * n)}, compiler_params=IN_FLIGHT,
    )(*srcs, *lands, send, recv, after)
    return res[:n], res[n:]


def add_pair(gs, r1s, core):
    n = len(gs)

    def body(c_ref, *refs):
        del c_ref
        for g_ref, r_ref, o_ref in zip(refs[:n], refs[n:2 * n], refs[2 * n:]):
            o_ref[...] = (g_ref[...] + r_ref[...]).astype(BF16)

    blk = lambda r: (None,) + r.shape[1:]
    grid_spec = pltpu.PrefetchScalarGridSpec(
        num_scalar_prefetch=1, grid=(NCHIP,),
        in_specs=[pl.BlockSpec(blk(r), lambda s, c: (s, c[0], 0)) for r in r1s]
        + [pl.BlockSpec(blk(r), lambda s, c: (s, 0, 0)) for r in r1s],
        out_specs=[pl.BlockSpec(blk(r), lambda s, c: (s, 0, 0)) for r in r1s])
    return pl.pallas_call(body, grid_spec=grid_spec, out_shape=[jax.ShapeDtypeStruct(r.shape, BF16) for r in r1s],
                          compiler_params=_cp(("arbitrary",)), name="add_pair")(core, *gs, *r1s)


def _scatter_copies(srcs, lands, send, recv):
    _, _, c, chips = _place()
    out = []
    for k, (src, land) in enumerate(zip(srcs, lands)):
        for j, (cx, cy) in enumerate(chips):
            out.append(pltpu.make_async_remote_copy(
                src_ref=src.at[2 * cx + cy], dst_ref=land.at[j], send_sem=send.at[3 * k + j],
                recv_sem=recv.at[3 * k + j], device_id=(cx, cy, c), device_id_type=MESH))
    return out


def scatter_start(srcs, layer):
    n = len(srcs)
    srcs = list(srcs)
    lands = [lax.empty((3,) + s.shape[1:], s.dtype) for s in srcs]

    def body(*refs):
        ins, land_refs = refs[:n], refs[n:2 * n]
        send, recv = refs[2 * n], refs[2 * n + 1]
        token = refs[-1]
        for cp in _scatter_copies(ins, land_refs, send, recv):
            cp.start()
        token[...] = jnp.zeros_like(token)

    sems = pltpu.SemaphoreType.DMA((3 * n,))
    res = pl.pallas_call(
        body, name=f"scatter_start_{layer}",
        in_specs=[HBM_SPEC] * (2 * n),
        out_specs=[SEM_SPEC, SEM_SPEC] + [HBM_SPEC] * (2 * n) + [pl.BlockSpec(memory_space=pltpu.VMEM)],
        out_shape=[sems, sems] + [pltpu.HBM(a.shape, a.dtype) for a in srcs + lands]
        + [jax.ShapeDtypeStruct((8, LANES), F32)],
        input_output_aliases={k: 2 + k for k in range(2 * n)}, compiler_params=IN_FLIGHT,
    )(*[_in_hbm(a) for a in srcs + lands])
    return res[0], res[1], res[2:2 + n], res[2 + n:2 + 2 * n], res[-1]


def scatter_wait(send, recv, srcs, lands, after, layer):
    n = len(srcs)

    def body(*refs):
        ins, land_refs = refs[:n], refs[n:2 * n]
        send_ref, recv_ref = refs[2 * n], refs[2 * n + 1]
        for cp in _scatter_copies(ins, land_refs, send_ref, recv_ref):
            cp.wait_send()
            cp.wait_recv()

    res = pl.pallas_call(
        body, name=f"scatter_wait_{layer}",
        in_specs=[HBM_SPEC] * (2 * n) + [SEM_SPEC, SEM_SPEC, _any()], out_specs=[HBM_SPEC] * (2 * n),
        out_shape=[pltpu.HBM(a.shape, a.dtype) for a in list(srcs) + list(lands)],
        input_output_aliases={k: k for k in range(2 * n)}, compiler_params=IN_FLIGHT,
    )(*srcs, *lands, send, recv, after)
    return res[n:]


def add_chips(gs, r1s, r2s, place, totals, layer):
    n = len(gs)
    steps = 2

    def body(p_ref, *refs):
        del p_ref
        for g_ref, r1_ref, r2_ref, o_ref in zip(refs[:n], refs[n:2 * n], refs[2 * n:3 * n], refs[4 * n:]):
            own = g_ref[...] + r1_ref[...]
            o_ref[...] = ((own + r2_ref[0].astype(F32)) + r2_ref[1].astype(F32)) + r2_ref[2].astype(F32)

    blk = lambda r: (None, r.shape[1] // steps, r.shape[2])
    grid_spec = pltpu.PrefetchScalarGridSpec(
        num_scalar_prefetch=1, grid=(steps,),
        in_specs=[pl.BlockSpec(blk(r), lambda i, p: (p[1], p[0] * steps + i, 0)) for r in r1s]
        + [pl.BlockSpec(blk(r), lambda i, p: (p[1], i, 0)) for r in r1s]
        + [pl.BlockSpec((3,) + blk(r)[1:], lambda i, p: (0, i, 0)) for r in r1s] + [_any()] * n,
        out_specs=[pl.BlockSpec(blk(r), lambda i, p: (layer, p[0] * steps + i, 0)) for r in r1s])
    return pl.pallas_call(body, grid_spec=grid_spec, out_shape=[jax.ShapeDtypeStruct(t.shape, F32) for t in totals],
                          input_output_aliases={1 + 3 * n + k: k for k in range(n)},
                          compiler_params=_cp(("arbitrary",)), name="add_chips")(place, *gs, *r1s, *r2s, *totals)


def pair_share(gs):
    n = len(gs)

    def body(*refs):
        outs = refs[n:2 * n]
        send, recv = refs[2 * n:]
        x, y, c, _ = _place()
        cps = []
        for k in range(n):
            mine = outs[k].at[:, _half(outs[k].shape[1], c), :]
            cp = pltpu.make_async_remote_copy(
                src_ref=mine, dst_ref=mine, send_sem=send.at[k], recv_sem=recv.at[k],
                device_id=(x, y, 1 - c), device_id_type=MESH)
            cp.start()
            cps.append(cp)
        for k, cp in enumerate(cps):
            cp.wait_send()
            theirs = outs[k].at[:, _half(outs[k].shape[1], 1 - c), :]
            pltpu.make_async_remote_copy(
                src_ref=theirs, dst_ref=theirs, send_sem=send.at[k], recv_sem=recv.at[k],
                device_id=(x, y, 1 - c), device_id_type=MESH).wait_recv()

    return pl.pallas_call(
        body, in_specs=[_any()] * n, out_specs=[_any()] * n,
        out_shape=[jax.ShapeDtypeStruct(g.shape, g.dtype) for g in gs], input_output_aliases={k: k for k in range(n)},
        scratch_shapes=[pltpu.SemaphoreType.DMA((n,)), pltpu.SemaphoreType.DMA((n,))],
        name="pair_share")(*gs)


def small_collect(v, reduce, name):
    rows = v.shape[0]
    flips = [(fx, fy, fc) for fx in (0, 1) for fy in (0, 1) for fc in (0, 1)][1:]

    def body(v_ref, o_ref, buf, send, recv):
        x, y, c, _ = _place()
        buf[4 * x + 2 * y + c] = v_ref[...]
        peers = [(jnp.where(fx, 1 - x, x), jnp.where(fy, 1 - y, y), jnp.where(fc, 1 - c, c)) for fx, fy, fc in flips]
        cps = []
        for k, peer in enumerate(peers):
            cp = pltpu.make_async_remote_copy(
                src_ref=v_ref, dst_ref=buf.at[4 * x + 2 * y + c], send_sem=send.at[k], recv_sem=recv.at[k],
                device_id=peer, device_id_type=MESH)
            cp.start()
            cps.append(cp)
        for k, (px, py, pc) in enumerate(peers):
            pltpu.make_async_remote_copy(
                src_ref=v_ref, dst_ref=buf.at[4 * px + 2 * py + pc], send_sem=send.at[k], recv_sem=recv.at[k],
                device_id=(px, py, pc), device_id_type=MESH).wait_recv()
        for cp in cps:
            cp.wait_send()
        if reduce:
            acc = buf[0]
            for s in range(1, 8):
                acc = acc + buf[s]
            o_ref[...] = acc
        else:
            o_ref[...] = buf[...]

    vm = pl.BlockSpec(memory_space=pltpu.VMEM)
    out_shape = jax.ShapeDtypeStruct((rows, SMALL_COLS) if reduce else (8, rows, SMALL_COLS), F32)
    return pl.pallas_call(
        body, in_specs=[vm], out_specs=vm, out_shape=out_shape,
        scratch_shapes=[pltpu.VMEM((8, rows, SMALL_COLS), F32), pltpu.SemaphoreType.DMA((7,)),
                        pltpu.SemaphoreType.DMA((7,))],
        name=name)(v)


def adamw(w, g, m, v, rb, name):
    nl, rows, cols = w.shape

    def body(w_ref, g_ref, m_ref, v_ref, go_ref, d_ref, nm_ref, nv_ref):
        gv = g_ref[...]
        go_ref[...] = gv
        nm = ADAM_B1 * m_ref[...] + (1.0 - ADAM_B1) * gv
        nv = ADAM_B2 * v_ref[...] + (1.0 - ADAM_B2) * (gv * gv)
        m_hat = nm / (1.0 - ADAM_B1 ** ADAM_STEP)
        v_hat = nv / (1.0 - ADAM_B2 ** ADAM_STEP)
        d_ref[...] = -ADAM_LR * (m_hat / (jnp.sqrt(v_hat) + ADAM_EPS) + ADAM_WD * w_ref[...])
        nm_ref[...] = nm
        nv_ref[...] = nv

    blk = pl.BlockSpec((None, rb, cols), lambda l, r: (l, r, 0))
    shp = jax.ShapeDtypeStruct(w.shape, F32)
    return pl.pallas_call(body, grid=(nl, rows // rb), in_specs=[blk] * 4, out_specs=[blk] * 4, out_shape=[shp] * 4,
                          compiler_params=_cp(("arbitrary", "arbitrary")), name=name)(w, g, m, v)


def _pack(parts, rows):
    flat = jnp.concatenate([p.reshape(-1).astype(F32) for p in parts])
    return jnp.pad(flat, (0, rows * SMALL_COLS - flat.shape[0])).reshape(rows, SMALL_COLS)


def _unpack(vec, shapes):
    flat = vec.reshape(-1)
    out, off = [], 0
    for s in shapes:
        size = 1
        for d in s:
            size *= d
        out.append(flat[off:off + size].reshape(s))
        off += size
    return out


def kernel(x, w_in, w_conv, rel_bias, g_conv_out, g_attn_out, w_out, g_pre_mix, g_post_mix, g_pre_ffn, g_post_ffn, w_ffn_in, w_ffn_out, loss_target, m_w_in, m_w_conv, m_rel_bias, m_g_conv_out, m_g_attn_out, m_w_out, m_g_pre_mix, m_g_post_mix, m_g_pre_ffn, m_g_post_ffn, m_w_ffn_in, m_w_ffn_out, v_w_in, v_w_conv, v_rel_bias, v_g_conv_out, v_g_attn_out, v_w_out, v_g_pre_mix, v_g_post_mix, v_g_pre_ffn, v_g_post_ffn, v_w_ffn_in, v_w_ffn_out):
    xi, yi, ci = lax.axis_index("x"), lax.axis_index("y"), lax.axis_index("c")
    chip = 2 * xi + yi
    nl = w_in.shape[0]
    x0 = x[0]
    target = loss_target[0]
    cwl = CW // NCHIP

    chip1 = chip.reshape(1).astype(jnp.int32)
    own = [cast_to_slot([w_in, w_out, w_ffn_in, w_ffn_out], chip1, l) for l in range(nl)]
    wc_all = small_collect(_pack([w_conv], 8), False, "gather_w_conv")
    wc_full = wc_all[0::2].reshape(NCHIP, -1)[:, :nl * cwl * 3].reshape(NCHIP, nl, cwl, 3)
    wc_full = jnp.transpose(wc_full, (1, 0, 2, 3)).reshape(nl, CW, 3)
    wconv_t = jnp.pad(jnp.transpose(wc_full, (0, 2, 1)), ((0, 0), (0, 5), (0, 0)))
    gm = jnp.kron(jnp.eye(CW // HD, dtype=F32), jnp.full((HD, HD), 1.0 / HD, F32)).astype(BF16)
    row = lambda a, l: a[l][None, :]

    def token(t):
        return t[0:1, 0:1]

    def gather_finish(flight, after, tag):
        send, recv, bufs, _ = flight
        return gather_forward(gather_wait(send, recv, bufs, after, tag))

    first_mix = gather_start(own[0][:2], wc_all, "0m")
    first_ffn = gather_start(own[0][2:], first_mix[3], "0f")
    flight = to_sibling = None
    saved, weights = [], []
    h = x0
    for l in range(nl):
        if l == 0:
            gw_in, gw_out = gather_finish(first_mix, x0, "0m")
        elif l == 1:
            gw_in, gw_out, gw_fi, gw_fo = gather_finish(flight, h, l)
        else:
            gw_in, gw_out, gw_fi, gw_fo = forward_wait(*to_sibling[:3], h, l)
        gw_out = gw_out.reshape(D, D)
        g_pm, g_pf = row(g_pre_mix, l), row(g_pre_ffn, l)
        if l == 0:
            g_pm = g_pm + token(first_ffn[3])
        if l + 1 < nl:
            flight = gather_start(own[l + 1], first_ffn[3] if l == 0 else gw_in, l + 1)
            g_pm = g_pm + token(flight[3])
        bias2, bias2_bwd = bias_expand(_diag_vector(rel_bias[l]), (QG_FWD, QG_BWD))
        proj = fwd_inproj(h, g_pm, gw_in)
        xmid, o, lse, y, z = fwd_mix(h, proj, bias2, wconv_t[l], row(g_conv_out, l), row(g_attn_out, l),
                                     row(g_post_mix, l), gm, gw_out)
        if l == 0:
            gw_fi, gw_fo = gather_finish(first_ffn, xmid, "0f")
        elif l + 1 < nl:
            send, recv, bufs, _ = flight
            to_sibling = forward_start(gather_wait(send, recv, bufs, xmid, l + 1), l + 1)
            g_pf = g_pf + token(to_sibling[3])
        gw_fo = gw_fo.reshape(2, DFF // 2, D)
        gu, f, xout = fwd_ffn(xmid, g_pf, row(g_post_ffn, l), gw_fi, gw_fo)
        saved.append((h, proj, bias2_bwd, xmid, o, lse, y, z, gu, f))
        weights.append((gw_in, gw_out, gw_fi, gw_fo))
        h = xout
    dx, loss_blk = loss_head(h, target)

    core = ci.reshape(1).astype(jnp.int32)
    place = jnp.stack([ci, chip]).astype(jnp.int32)
    totals = [lax.empty(w.shape, F32) for w in (w_in, w_out, w_ffn_in, w_ffn_out)]
    small = {k: [None] * nl for k in ("co", "ao", "pm", "qm", "pf", "qf", "rel", "wc")}

    def reduce_begin(kinds, grads, tag):
        return kinds, exchange_start(grads, tag), tag

    def reduce_mid(state, after):
        kinds, (send, recv, srcs, lands, _), tag = state
        grads, from_sibling = exchange_wait(send, recv, srcs, lands, after, tag)
        return kinds, grads, from_sibling, scatter_start(add_pair(grads, from_sibling, core), tag), tag

    def reduce_end(state, after, totals, layer):
        kinds, grads, from_sibling, (send, recv, srcs, lands, _), tag = state
        from_chips = scatter_wait(send, recv, srcs, lands, after, tag)
        totals = list(totals)
        summed = add_chips(grads, from_sibling, from_chips, place, [totals[i] for i in kinds], layer)
        for i, t in zip(kinds, summed):
            totals[i] = t
        return totals

    begun = flying = None
    for l in reversed(range(nl)):
        hin, proj, bias2, xmid, o, lse, y, z, gu, f = saved[l]
        gw_in, gw_out, gw_fi, gw_fo = weights[l]
        g_qf, g_qm, wct = row(g_post_ffn, l), row(g_post_mix, l), wconv_t[l]
        if begun is not None:
            g_qf = g_qf + token(begun[1][4])
        dxm, dfb, act, dgu, h2, dg_qf, dg_pf = bwd_ffn(dx, f, xmid, gu, row(g_pre_ffn, l), g_qf, gw_fi, gw_fo)
        if begun is not None:
            flying = reduce_mid(begun, dxm)
            g_qm = g_qm + token(flying[3][4])
        gr_fo = wgrad(act, dfb, 256, D, False, "wgrad_ffn_out").reshape(NCHIP, DFF // NCHIP, D)
        gr_fi = wgrad(h2, dgu, 512, 2 * DFF // NCHIP, True, "wgrad_ffn_in")
        if l == 0:
            begun_ffn = reduce_begin([2, 3], [gr_fi, gr_fo], "0f")
            g_qm = g_qm + token(begun_ffn[1][4])
        dzb, do, dco, dbg, dg_qm, dg_co, dg_ao = bwd_mix(dxm, z, o, proj, wct, row(g_conv_out, l),
                                                          row(g_attn_out, l), g_qm, gm, gw_out)
        if l == 0:
            flying_ffn = reduce_mid(begun_ffn, dzb)
            wct = wct + token(flying_ffn[3][4])
        gr_out = wgrad(y, dzb, 512, D, False, "wgrad_out").reshape(NCHIP, D // NCHIP, D)
        dhc, dcg, dwc = bwd_conv(dco, proj, wct)
        dq, dk, dv, db2 = bwd_attn(proj, o, do, lse, bias2)
        dx, dproj, hb, dg_pm = bwd_inproj(dxm, hin, dhc, dbg, dcg, dq, dk, dv, row(g_pre_mix, l), gw_in)
        if flying is not None:
            totals = reduce_end(flying, dx, totals, l + 1)
        gr_in = wgrad(hb, dproj, 512, PROJ // NCHIP, True, "wgrad_in")
        small["co"][l], small["ao"][l], small["pm"][l], small["qm"][l] = dg_co, dg_ao, dg_pm, dg_qm
        small["pf"][l], small["qf"][l] = dg_pf, dg_qf
        small["rel"][l] = _diag_vector_bwd(bias_reduce(db2.reshape(NH, QG_BWD, QG_BWD + LEFT)))
        small["wc"][l] = jnp.transpose(dwc[0:3], (1, 0))
        if l > 0:
            begun = reduce_begin([0, 1, 2, 3], [gr_in, gr_out, gr_fi, gr_fo], l)
    flying_mix = reduce_mid(reduce_begin([0, 1], [gr_in, gr_out], "0m"), dx)
    totals = reduce_end(flying_ffn, flying_mix[3][4], totals, 0)
    totals = reduce_end(flying_mix, totals[2], totals, 0)
    gr_in, gr_out, gr_fi, gr_fo = pair_share(totals)

    order = ("co", "ao", "pm", "qm", "pf", "qf", "rel", "wc")
    parts = [jnp.stack(small[k]) for k in order] + [loss_blk[0:1, 0:1]]
    shapes = [p.shape for p in parts]
    red = _unpack(small_collect(_pack(parts, 40), True, "reduce_small"), shapes)
    gr_co, gr_ao, gr_pm, gr_qm, gr_pf, gr_qf, gr_rel, gr_wc_full, loss = red
    gr_co, gr_ao, gr_pm, gr_qm, gr_pf, gr_qf = [a.reshape(nl, -1) for a in (gr_co, gr_ao, gr_pm, gr_qm, gr_pf, gr_qf)]
    gr_wc = lax.dynamic_slice_in_dim(gr_wc_full, chip * cwl, cwl, axis=1)
    loss = loss.reshape(())

    big = []
    for w, g, m, v, name in ((w_in, gr_in, m_w_in, v_w_in, "adamw_in"), (w_out, gr_out, m_w_out, v_w_out, "adamw_out"),
                             (w_ffn_in, gr_fi, m_w_ffn_in, v_w_ffn_in, "adamw_ffn_in"),
                             (w_ffn_out, gr_fo, m_w_ffn_out, v_w_ffn_out, "adamw_ffn_out")):
        big.append(adamw(w, g, m, v, w.shape[1] // 4, name))
    sw = [g_conv_out, g_attn_out, g_pre_mix, g_post_mix, g_pre_ffn, g_post_ffn, rel_bias, w_conv]
    sg = [gr_co, gr_ao, gr_pm, gr_qm, gr_pf, gr_qf, gr_rel, gr_wc]
    sm = [m_g_conv_out, m_g_attn_out, m_g_pre_mix, m_g_post_mix, m_g_pre_ffn, m_g_post_ffn, m_rel_bias, m_w_conv]
    sv = [v_g_conv_out, v_g_attn_out, v_g_pre_mix, v_g_post_mix, v_g_pre_ffn, v_g_post_ffn, v_rel_bias, v_w_conv]
    sshapes = [a.shape for a in sw]
    packed = [_pack(a, 32)[None] for a in (sw, sg, sm, sv)]
    s_out = [_unpack(a[0], sshapes) for a in adamw(*packed, 32, "adamw_small")]

    def leaves(big_i, small_i):
        b_in, b_out, b_fi, b_fo = big_i
        s_co, s_ao, s_pm, s_qm, s_pf, s_qf, s_rel, s_wc = small_i
        return [b_in, s_wc, s_rel, s_co, s_ao, b_out, s_pm, s_qm, s_pf, s_qf, b_fi, b_fo]

    out = [loss, dx[None]]
    out += leaves([b[0] for b in big], sg)
    for i in range(1, 4):
        out += leaves([b[i] for b in big], s_out[i])
    return tuple(out)
```

```python
import functools

import jax
import jax.numpy as jnp
from jax import lax
from jax.experimental import pallas as pl
from jax.experimental.pallas import tpu as pltpu

F32 = jnp.float32
BF16 = jnp.bfloat16

D = 1024
PROJ = 3072
CW = 512
HD = 64
NH = 8
CHUNK = 64
BAND = 576
REL_CLIP = 128
NREL = 2 * REL_CLIP + 1
DFF = 2816
DEPTH = 4
NCHIP = 4
EPS = 1e-6
NEG_INF = -1e30

ADAM_LR = 0.001
ADAM_B1 = 0.9
ADAM_B2 = 0.999
ADAM_EPS = 1e-08
ADAM_WD = 0.01
ADAM_STEP = 10

V7X_VMEM_BYTES = 64 * 1024 * 1024
VMEM_LIMIT = V7X_VMEM_BYTES - 8 * 1024 * 1024
LANES = 128
QG_FWD = 4 * CHUNK
QG_BWD = 2 * CHUNK
LEFT = BAND - CHUNK
TQ = 512
TM = 256
SMALL_COLS = 1024
MESH = pl.DeviceIdType.MESH
NT = (((1,), (1,)), ((), ()))
TN = (((0,), (0,)), ((), ()))


def _cp(sem=None, vmem=VMEM_LIMIT):
    return pltpu.CompilerParams(dimension_semantics=sem, vmem_limit_bytes=vmem)


def _any():
    return pl.BlockSpec(memory_space=pl.ANY)


def _const(shape):
    nd = len(shape)
    return pl.BlockSpec(shape, lambda *_: (0,) * nd)


def _rms(v, g):
    r = lax.rsqrt(jnp.mean(v * v, axis=-1, keepdims=True) + EPS)
    return v * r * g


def _rms_bwd(dy, v, g):
    r = lax.rsqrt(jnp.mean(v * v, axis=-1, keepdims=True) + EPS)
    vh = v * r
    dg = jnp.sum(dy * vh, axis=0, keepdims=True)
    dvh = dy * g
    dv = r * (dvh - vh * jnp.mean(dvh * vh, axis=-1, keepdims=True))
    return dv, dg


def _group_mean(v, gm):
    hi = v.astype(BF16)
    lo = (v - hi.astype(F32)).astype(BF16)
    return jnp.dot(hi, gm, preferred_element_type=F32) + jnp.dot(lo, gm, preferred_element_type=F32)


def _group_rms_bwd(dy, v, g, gm):
    r = lax.rsqrt(_group_mean(v * v, gm) + EPS)
    vh = v * r
    dg = jnp.sum(dy * vh, axis=0, keepdims=True)
    dvh = dy * g
    dv = r * (dvh - vh * _group_mean(dvh * vh, gm))
    return dv, dg


def _head_masks(scale):
    lane = lax.broadcasted_iota(jnp.int32, (1, LANES), 1)
    return [jnp.where((lane >= HD * a) & (lane < HD * (a + 1)), scale, 0.0).astype(BF16) for a in range(2)]


class _Resident:
    def __init__(self, src, dst, sem):
        self.first = pl.program_id(0) == 0
        self.copy = pltpu.make_async_copy(src, dst, sem)
        self.dst = dst

        @pl.when(self.first)
        def _():
            self.copy.start()

    def read(self):
        @pl.when(self.first)
        def _():
            self.copy.wait()

        return self.dst[...]


def _conv_taps(u_prev, u, scr):
    n = u.shape[0]
    scr[0:16, :] = u_prev
    scr[16:16 + n, :] = u
    return scr[15:15 + n, :], scr[14:14 + n, :]


def fwd_inproj(x, g, w_all):
    t = x.shape[0]
    wc = PROJ // NCHIP

    def body(x_ref, g_ref, w_hbm, o_ref, w_v):
        @pl.when(pl.program_id(0) == 0)
        def _():
            pltpu.sync_copy(w_hbm, w_v)

        h = _rms(x_ref[...], g_ref[...]).astype(BF16)
        for b in range(NCHIP):
            o_ref[:, wc * b:wc * (b + 1)] = jnp.dot(h, w_v[b], preferred_element_type=F32).astype(BF16)

    return pl.pallas_call(
        body, grid=(t // TQ,),
        in_specs=[pl.BlockSpec((TQ, D), lambda i: (i, 0)), _const((1, D)), _any()],
        out_specs=pl.BlockSpec((TQ, PROJ), lambda i: (i, 0)),
        out_shape=jax.ShapeDtypeStruct((t, PROJ), BF16),
        scratch_shapes=[pltpu.VMEM((NCHIP, D, wc), BF16)],
        compiler_params=_cp(("arbitrary",)), name="fwd_inproj")(x, g, w_all)


def _attn_window_specs():
    return [
        pl.BlockSpec((TQ, CW), lambda i: (i, 3)),
        pl.BlockSpec((TQ, CW), lambda i: (jnp.maximum(i - 1, 0), 4)),
        pl.BlockSpec((TQ, CW), lambda i: (i, 4)),
        pl.BlockSpec((TQ, CW), lambda i: (jnp.maximum(i - 1, 0), 5)),
        pl.BlockSpec((TQ, CW), lambda i: (i, 5)),
    ]


def _conv_specs():
    return [
        pl.BlockSpec((TQ, 3 * CW), lambda i: (i, 0)),
        pl.BlockSpec((16, 3 * CW), lambda i: (jnp.maximum(i * (TQ // 16) - 1, 0), 0)),
    ]


def _conv_fwd(pc_ref, pcp_ref, wc_ref, scr, first):
    pc = pc_ref[...].astype(F32)
    hc, bg, cg = pc[:, :CW], pc[:, CW:2 * CW], pc[:, 2 * CW:]
    u = cg * hc
    pp = pcp_ref[...].astype(F32)
    u_prev = jnp.where(first, 0.0, pp[:, 2 * CW:] * pp[:, :CW])
    u1, u2 = _conv_taps(u_prev, u, scr)
    cout = wc_ref[0:1, :] * u2 + wc_ref[1:2, :] * u1 + wc_ref[2:3, :] * u
    return hc, bg, cg, u, u1, u2, cout


def _key_penalty(first, r0, kg):
    col = lax.broadcasted_iota(jnp.int32, (1, kg), 1)
    limit = jnp.where(first, TQ - r0, 0)
    return jnp.where(col < limit, NEG_INF, 0.0)


def fwd_mix(x, proj, bias2, wconv_t, g_co, g_ao, g_pm, gm, wout_all):
    t = x.shape[0]
    qg, kg = QG_FWD, QG_FWD + LEFT

    def body(x_ref, pc_ref, pcp_ref, q_ref, kp_ref, kc_ref, vp_ref, vc_ref, b2_ref, wc_ref, gco_ref, gao_ref, gpm_ref,
             gm_ref, wout_hbm, xmid_ref, o_ref, lse_ref, y_ref, z_ref, wout_v, kwin, vwin, cscr, sems):
        i = pl.program_id(0)
        first = i == 0
        wout = _Resident(wout_hbm, wout_v, sems.at[0])
        kwin[0:TQ, :] = kp_ref[...]
        kwin[TQ:2 * TQ, :] = kc_ref[...]
        vwin[0:TQ, :] = vp_ref[...]
        vwin[TQ:2 * TQ, :] = vc_ref[...]
        qmask = _head_masks(HD ** -0.5)
        low = lax.broadcasted_iota(jnp.int32, (1, LANES), 1) < HD

        def group(g, carry):
            r0 = pl.multiple_of(g * qg, qg)
            pen = _key_penalty(first, r0, kg)
            for hp in range(NH // 2):
                ls = slice(LANES * hp, LANES * (hp + 1))
                qb = q_ref[pl.ds(r0, qg), ls]
                q2 = jnp.concatenate([qb * qmask[0], qb * qmask[1]], axis=0)
                s = lax.dot_general(q2, kwin[pl.ds(r0, kg), ls], NT, preferred_element_type=F32)
                s = s + b2_ref[hp] + pen
                m = jnp.max(s, axis=-1, keepdims=True)
                p = jnp.exp(s - m)
                l = jnp.sum(p, axis=-1, keepdims=True)
                o2 = jnp.dot(p.astype(BF16), vwin[pl.ds(r0, kg), ls], preferred_element_type=F32) * (1.0 / l)
                lse2 = m + jnp.log(l)
                o_ref[pl.ds(r0, qg), ls] = jnp.where(low, o2[:qg], o2[qg:])
                lse_ref[pl.ds(r0, qg), ls] = jnp.where(low, lse2[:qg], lse2[qg:])
            return carry

        lax.fori_loop(0, TQ // qg, group, 0)

        _, bg, _, _, _, _, cout = _conv_fwd(pc_ref, pcp_ref, wc_ref, cscr, first)
        yc = bg * cout
        gmv = gm_ref[...]
        ycn = yc * lax.rsqrt(_group_mean(yc * yc, gmv) + EPS) * gco_ref[...]
        oa = o_ref[...]
        oan = oa * lax.rsqrt(_group_mean(oa * oa, gmv) + EPS) * gao_ref[...]
        y_ref[:, 0:CW] = ycn.astype(BF16)
        y_ref[:, CW:2 * CW] = oan.astype(BF16)
        z = jnp.dot(y_ref[...], wout.read(), preferred_element_type=F32)
        z_ref[...] = z
        xmid_ref[...] = x_ref[...] + _rms(z, gpm_ref[...])

    row = lambda w: pl.BlockSpec((TQ, w), lambda i: (i, 0))
    return pl.pallas_call(
        body, grid=(t // TQ,),
        in_specs=[row(D)] + _conv_specs() + _attn_window_specs() + [
            _const((NH // 2, 2 * qg, kg)), _const((8, CW)), _const((1, CW)), _const((1, CW)), _const((1, D)),
            _const((CW, CW)), _any()],
        out_specs=[row(D), row(CW), row(CW), row(D), row(D)],
        out_shape=[jax.ShapeDtypeStruct((t, D), F32), jax.ShapeDtypeStruct((t, CW), F32),
                   jax.ShapeDtypeStruct((t, CW), F32), jax.ShapeDtypeStruct((t, D), BF16),
                   jax.ShapeDtypeStruct((t, D), F32)],
        scratch_shapes=[pltpu.VMEM((D, D), BF16), pltpu.VMEM((2 * TQ, CW), BF16), pltpu.VMEM((2 * TQ, CW), BF16),
                        pltpu.VMEM((TQ + 16, CW), F32), pltpu.SemaphoreType.DMA((1,))],
        compiler_params=_cp(("arbitrary",)), name="fwd_mix",
    )(x, proj, proj, proj, proj, proj, proj, proj, bias2, wconv_t, g_co, g_ao, g_pm, gm, wout_all)


def fwd_ffn(xmid, g_pre, g_post, wfi_all, wfo_all):
    t = xmid.shape[0]
    hw = DFF // 2

    def body(x_ref, gpre_ref, gpost_ref, wfi_hbm, wfo_hbm, gu_ref, f_ref, xo_ref, wfi_v, wfo_v):
        @pl.when(pl.program_id(0) == 0)
        def _():
            pltpu.sync_copy(wfi_hbm, wfi_v)
            pltpu.sync_copy(wfo_hbm, wfo_v)

        xv = x_ref[...]
        h = _rms(xv, gpre_ref[...]).astype(BF16)
        f = jnp.zeros((TM, D), F32)
        for j in range(2):
            gate = jnp.dot(h, wfi_v[j], preferred_element_type=F32)
            up = jnp.dot(h, wfi_v[2 + j], preferred_element_type=F32)
            gu_ref[:, hw * j:hw * (j + 1)] = gate.astype(BF16)
            gu_ref[:, DFF + hw * j:DFF + hw * (j + 1)] = up.astype(BF16)
            act = gate * (1.0 / (1.0 + jnp.exp(-gate))) * up
            f = f + jnp.dot(act.astype(BF16), wfo_v[j], preferred_element_type=F32)
        f_ref[...] = f
        xo_ref[...] = xv + _rms(f, gpost_ref[...])

    row = lambda w: pl.BlockSpec((TM, w), lambda i: (i, 0))
    return pl.pallas_call(
        body, grid=(t // TM,),
        in_specs=[row(D), _const((1, D)), _const((1, D)), _any(), _any()],
        out_specs=[row(2 * DFF), row(D), row(D)],
        out_shape=[jax.ShapeDtypeStruct((t, 2 * DFF), BF16), jax.ShapeDtypeStruct((t, D), F32),
                   jax.ShapeDtypeStruct((t, D), F32)],
        scratch_shapes=[pltpu.VMEM((NCHIP, D, hw), BF16), pltpu.VMEM((2, hw, D), BF16)],
        compiler_params=_cp(("arbitrary",)), name="fwd_ffn")(xmid, g_pre, g_post, wfi_all, wfo_all)


def loss_head(y, target):
    t = y.shape[0]

    def body(y_ref, t_ref, dy_ref, l_ref):
        @pl.when(pl.program_id(0) == 0)
        def _():
            l_ref[...] = jnp.zeros_like(l_ref)

        e = y_ref[...] - t_ref[...]
        dy_ref[...] = e * (1.0 / D)
        rows = jnp.sum(e * e, axis=-1, keepdims=True) * (1.0 / D)
        l_ref[...] += 0.5 * jnp.sum(rows, axis=0, keepdims=True)

    row = pl.BlockSpec((TQ, D), lambda i: (i, 0))
    return pl.pallas_call(
        body, grid=(t // TQ,), in_specs=[row, row], out_specs=[row, _const((8, LANES))],
        out_shape=[jax.ShapeDtypeStruct((t, D), F32), jax.ShapeDtypeStruct((8, LANES), F32)],
        compiler_params=_cp(("arbitrary",)), name="loss_head")(y, target)


def bwd_ffn(dx, f, xmid, gu, g_pre, g_post, wfi_all, wfo_all):
    t = dx.shape[0]
    hw = DFF // 2

    def body(dx_ref, f_ref, x_ref, gu_ref, gpre_ref, gpost_ref, wfi_hbm, wfo_hbm,
             dxm_ref, df_ref, act_ref, dgu_ref, h_ref, dgpost_ref, dgpre_ref, wfi_v, wfo_v):
        @pl.when(pl.program_id(0) == 0)
        def _():
            pltpu.sync_copy(wfi_hbm, wfi_v)
            pltpu.sync_copy(wfo_hbm, wfo_v)
            dgpost_ref[...] = jnp.zeros_like(dgpost_ref)
            dgpre_ref[...] = jnp.zeros_like(dgpre_ref)

        dxo = dx_ref[...]
        df, dgp = _rms_bwd(dxo, f_ref[...], gpost_ref[...])
        dgpost_ref[...] += dgp
        dfb = df.astype(BF16)
        df_ref[...] = dfb
        dh = jnp.zeros((TM, D), F32)
        for j in range(2):
            dact = lax.dot_general(dfb, wfo_v[j], NT, preferred_element_type=F32)
            gate = gu_ref[:, hw * j:hw * (j + 1)].astype(F32)
            up = gu_ref[:, DFF + hw * j:DFF + hw * (j + 1)].astype(F32)
            sig = 1.0 / (1.0 + jnp.exp(-gate))
            silu = gate * sig
            act_ref[:, hw * j:hw * (j + 1)] = (silu * up).astype(BF16)
            dup = (dact * silu).astype(BF16)
            dgate = (dact * up * (sig * (1.0 + gate * (1.0 - sig)))).astype(BF16)
            dgu_ref[:, hw * j:hw * (j + 1)] = dgate
            dgu_ref[:, DFF + hw * j:DFF + hw * (j + 1)] = dup
            dh = dh + lax.dot_general(dgate, wfi_v[j], NT, preferred_element_type=F32)
            dh = dh + lax.dot_general(dup, wfi_v[2 + j], NT, preferred_element_type=F32)
        xv = x_ref[...]
        gpre = gpre_ref[...]
        h_ref[...] = _rms(xv, gpre).astype(BF16)
        dxv, dgq = _rms_bwd(dh, xv, gpre)
        dgpre_ref[...] += dgq
        dxm_ref[...] = dxo + dxv

    row = lambda w: pl.BlockSpec((TM, w), lambda i: (i, 0))
    return pl.pallas_call(
        body, grid=(t // TM,),
        in_specs=[row(D), row(D), row(D), row(2 * DFF), _const((1, D)), _const((1, D)), _any(), _any()],
        out_specs=[row(D), row(D), row(DFF), row(2 * DFF), row(D), _const((1, D)), _const((1, D))],
        out_shape=[jax.ShapeDtypeStruct((t, D), F32), jax.ShapeDtypeStruct((t, D), BF16),
                   jax.ShapeDtypeStruct((t, DFF), BF16), jax.ShapeDtypeStruct((t, 2 * DFF), BF16),
                   jax.ShapeDtypeStruct((t, D), BF16), jax.ShapeDtypeStruct((1, D), F32),
                   jax.ShapeDtypeStruct((1, D), F32)],
        scratch_shapes=[pltpu.VMEM((NCHIP, D, hw), BF16), pltpu.VMEM((2, hw, D), BF16)],
        compiler_params=_cp(("arbitrary",)), name="bwd_ffn")(dx, f, xmid, gu, g_pre, g_post, wfi_all, wfo_all)


def bwd_mix(dxm, z, o, proj, wconv_t, g_co, g_ao, g_pm, gm, wout_all):
    t = dxm.shape[0]

    def body(dx_ref, z_ref, o_ref, pc_ref, pcp_ref, wc_ref, gco_ref, gao_ref, gpm_ref, gm_ref, wout_hbm,
             dz_ref, do_ref, dco_ref, dbg_ref, dgpm_ref, dgco_ref, dgao_ref, wout_v, cscr):
        first = pl.program_id(0) == 0

        @pl.when(first)
        def _():
            pltpu.sync_copy(wout_hbm, wout_v)
            dgpm_ref[...] = jnp.zeros_like(dgpm_ref)
            dgco_ref[...] = jnp.zeros_like(dgco_ref)
            dgao_ref[...] = jnp.zeros_like(dgao_ref)

        dz, dgp = _rms_bwd(dx_ref[...], z_ref[...], gpm_ref[...])
        dgpm_ref[...] += dgp
        dzb = dz.astype(BF16)
        dz_ref[...] = dzb
        gmv = gm_ref[...]
        _, bg, _, _, _, _, cout = _conv_fwd(pc_ref, pcp_ref, wc_ref, cscr, first)
        dy_conv = lax.dot_general(dzb, wout_v[0:CW, :], NT, preferred_element_type=F32)
        dyc, dgc = _group_rms_bwd(dy_conv, bg * cout, gco_ref[...], gmv)
        dgco_ref[...] += dgc
        dbg_ref[...] = (dyc * cout).astype(BF16)
        dco_ref[...] = dyc * bg
        dy_attn = lax.dot_general(dzb, wout_v[CW:2 * CW, :], NT, preferred_element_type=F32)
        do, dga = _group_rms_bwd(dy_attn, o_ref[...], gao_ref[...], gmv)
        dgao_ref[...] += dga
        do_ref[...] = do.astype(BF16)

    row = lambda w: pl.BlockSpec((TQ, w), lambda i: (i, 0))
    return pl.pallas_call(
        body, grid=(t // TQ,),
        in_specs=[row(D), row(D), row(CW)] + _conv_specs() + [
            _const((8, CW)), _const((1, CW)), _const((1, CW)), _const((1, D)), _const((CW, CW)), _any()],
        out_specs=[row(D), row(CW), row(CW), row(CW), _const((1, D)), _const((1, CW)), _const((1, CW))],
        out_shape=[jax.ShapeDtypeStruct((t, D), BF16), jax.ShapeDtypeStruct((t, CW), BF16),
                   jax.ShapeDtypeStruct((t, CW), F32), jax.ShapeDtypeStruct((t, CW), BF16),
                   jax.ShapeDtypeStruct((1, D), F32), jax.ShapeDtypeStruct((1, CW), F32),
                   jax.ShapeDtypeStruct((1, CW), F32)],
        scratch_shapes=[pltpu.VMEM((D, D), BF16), pltpu.VMEM((TQ + 16, CW), F32)],
        compiler_params=_cp(("arbitrary",)), name="bwd_mix",
    )(dxm, z, o, proj, proj, wconv_t, g_co, g_ao, g_pm, gm, wout_all)


def bwd_conv(dco, proj, wconv_t):
    t = dco.shape[0]
    nt = t // TQ

    def body(d_ref, dn_ref, pc_ref, pcp_ref, wc_ref, dhc_ref, dcg_ref, dw_ref, cscr, dscr):
        i = pl.program_id(0)
        first = i == 0

        @pl.when(first)
        def _():
            dw_ref[...] = jnp.zeros_like(dw_ref)

        hc, _, cg, u, u1, u2, _ = _conv_fwd(pc_ref, pcp_ref, wc_ref, cscr, first)
        d0 = d_ref[...]
        dscr[0:TQ, :] = d0
        dscr[TQ:TQ + 8, :] = jnp.where(i == nt - 1, 0.0, dn_ref[...])
        d1 = dscr[1:TQ + 1, :]
        d2 = dscr[2:TQ + 2, :]
        du = wc_ref[2:3, :] * d0 + wc_ref[1:2, :] * d1 + wc_ref[0:1, :] * d2
        dhc_ref[...] = (du * cg).astype(BF16)
        dcg_ref[...] = (du * hc).astype(BF16)
        dw_ref[0:1, :] += jnp.sum(d0 * u2, axis=0, keepdims=True)
        dw_ref[1:2, :] += jnp.sum(d0 * u1, axis=0, keepdims=True)
        dw_ref[2:3, :] += jnp.sum(d0 * u, axis=0, keepdims=True)

    row = lambda w: pl.BlockSpec((TQ, w), lambda i: (i, 0))
    nxt = pl.BlockSpec((8, CW), lambda i: (jnp.minimum((i + 1) * (TQ // 8), t // 8 - 1), 0))
    return pl.pallas_call(
        body, grid=(nt,),
        in_specs=[row(CW), nxt] + _conv_specs() + [_const((8, CW))],
        out_specs=[row(CW), row(CW), _const((8, CW))],
        out_shape=[jax.ShapeDtypeStruct((t, CW), BF16), jax.ShapeDtypeStruct((t, CW), BF16),
                   jax.ShapeDtypeStruct((8, CW), F32)],
        scratch_shapes=[pltpu.VMEM((TQ + 16, CW), F32), pltpu.VMEM((TQ + 8, CW), F32)],
        compiler_params=_cp(("arbitrary",)), name="bwd_conv")(dco, dco, proj, proj, wconv_t)


def bwd_attn(proj, o, do, lse, bias2):
    t = o.shape[0]
    nt = t // TQ
    qg, kg = QG_BWD, QG_BWD + LEFT

    def body(q_ref, kp_ref, kc_ref, vp_ref, vc_ref, o_ref, do_ref, lse_ref, b2_ref,
             dq_ref, dk_hbm, dv_hbm, db_hbm, kwin, vwin, dk_acc, dv_acc, db_acc):
        i = pl.program_id(0)
        first = i == 0

        @pl.when(first)
        def _():
            dk_acc[...] = jnp.zeros_like(dk_acc)
            dv_acc[...] = jnp.zeros_like(dv_acc)
            db_acc[...] = jnp.zeros_like(db_acc)

        kwin[0:TQ, :] = kp_ref[...]
        kwin[TQ:2 * TQ, :] = kc_ref[...]
        vwin[0:TQ, :] = vp_ref[...]
        vwin[TQ:2 * TQ, :] = vc_ref[...]
        scale = HD ** -0.5
        qmask = _head_masks(scale)
        vmask = _head_masks(1.0)
        low = lax.broadcasted_iota(jnp.int32, (1, LANES), 1) < HD

        def group(g, carry):
            r0 = pl.multiple_of(g * qg, qg)
            base = pl.multiple_of(i * TQ + r0, qg)
            pen = _key_penalty(first, r0, kg)
            for hp in range(NH // 2):
                ls = slice(LANES * hp, LANES * (hp + 1))
                qb = q_ref[pl.ds(r0, qg), ls]
                kw = kwin[pl.ds(r0, kg), ls]
                dob = do_ref[pl.ds(r0, qg), ls]
                prod = dob.astype(F32) * o_ref[pl.ds(r0, qg), ls]
                lseb = lse_ref[pl.ds(r0, qg), ls]
                q2 = jnp.concatenate([qb * qmask[0], qb * qmask[1]], axis=0)
                do2 = jnp.concatenate([dob * vmask[0], dob * vmask[1]], axis=0)
                lse2 = jnp.concatenate([lseb[:, 0:1], lseb[:, HD:HD + 1]], axis=0)
                dsum = jnp.concatenate([jnp.sum(jnp.where(low, prod, 0.0), axis=-1, keepdims=True),
                                        jnp.sum(jnp.where(low, 0.0, prod), axis=-1, keepdims=True)], axis=0)
                s = lax.dot_general(q2, kw, NT, preferred_element_type=F32) + b2_ref[hp] + pen
                p = jnp.exp(s - lse2)
                dp = lax.dot_general(do2, vwin[pl.ds(r0, kg), ls], NT, preferred_element_type=F32)
                ds = p * (dp - dsum)
                db_acc[hp] += ds
                dsb = ds.astype(BF16)
                dq2 = jnp.dot(dsb, kw, preferred_element_type=F32)
                dq_ref[pl.ds(r0, qg), ls] = (jnp.where(low, dq2[:qg], dq2[qg:]) * scale).astype(BF16)
                dk_acc[pl.ds(base, kg), ls] += lax.dot_general(dsb, q2, TN, preferred_element_type=F32)
                dv_acc[pl.ds(base, kg), ls] += lax.dot_general(p.astype(BF16), do2, TN, preferred_element_type=F32)
            return carry

        lax.fori_loop(0, TQ // qg, group, 0)

        @pl.when(i == nt - 1)
        def _():
            pltpu.sync_copy(dk_acc, dk_hbm)
            pltpu.sync_copy(dv_acc, dv_hbm)
            pltpu.sync_copy(db_acc, db_hbm)

    row = lambda w: pl.BlockSpec((TQ, w), lambda i: (i, 0))
    return pl.pallas_call(
        body, grid=(nt,),
        in_specs=_attn_window_specs() + [row(CW), row(CW), row(CW), _const((NH // 2, 2 * qg, kg))],
        out_specs=[row(CW), _any(), _any(), _any()],
        out_shape=[jax.ShapeDtypeStruct((t, CW), BF16), jax.ShapeDtypeStruct((t + TQ, CW), F32),
                   jax.ShapeDtypeStruct((t + TQ, CW), F32), jax.ShapeDtypeStruct((NH // 2, 2 * qg, kg), F32)],
        scratch_shapes=[pltpu.VMEM((2 * TQ, CW), BF16), pltpu.VMEM((2 * TQ, CW), BF16),
                        pltpu.VMEM((t + TQ, CW), F32), pltpu.VMEM((t + TQ, CW), F32),
                        pltpu.VMEM((NH // 2, 2 * qg, kg), F32)],
        compiler_params=_cp(("arbitrary",)), name="bwd_attn",
    )(proj, proj, proj, proj, proj, o, do, lse, bias2)


def bwd_inproj(dxm, x, dhc, dbg, dcg, dq, dk, dv, g, w_all):
    t = x.shape[0]
    wc = PROJ // NCHIP

    def body(dxm_ref, x_ref, dhc_ref, dbg_ref, dcg_ref, dq_ref, dk_ref, dv_ref, g_ref, w_hbm,
             dx_ref, dp_ref, h_ref, dg_ref, w_v):
        @pl.when(pl.program_id(0) == 0)
        def _():
            pltpu.sync_copy(w_hbm, w_v)
            dg_ref[...] = jnp.zeros_like(dg_ref)

        dp_ref[:, 0:CW] = dhc_ref[...]
        dp_ref[:, CW:2 * CW] = dbg_ref[...]
        dp_ref[:, 2 * CW:3 * CW] = dcg_ref[...]
        dp_ref[:, 3 * CW:4 * CW] = dq_ref[...]
        dp_ref[:, 4 * CW:5 * CW] = dk_ref[...].astype(BF16)
        dp_ref[:, 5 * CW:6 * CW] = dv_ref[...].astype(BF16)
        dh = jnp.zeros((TQ, D), F32)
        for b in range(NCHIP):
            dh = dh + lax.dot_general(dp_ref[:, wc * b:wc * (b + 1)], w_v[b], NT, preferred_element_type=F32)
        xv = x_ref[...]
        gv = g_ref[...]
        h_ref[...] = _rms(xv, gv).astype(BF16)
        dxv, dgv = _rms_bwd(dh, xv, gv)
        dg_ref[...] += dgv
        dx_ref[...] = dxm_ref[...] + dxv

    row = lambda w: pl.BlockSpec((TQ, w), lambda i: (i, 0))
    pad = pl.BlockSpec((TQ, CW), lambda i: (i + 1, 0))
    return pl.pallas_call(
        body, grid=(t // TQ,),
        in_specs=[row(D), row(D), row(CW), row(CW), row(CW), row(CW), pad, pad, _const((1, D)), _any()],
        out_specs=[row(D), row(PROJ), row(D), _const((1, D))],
        out_shape=[jax.ShapeDtypeStruct((t, D), F32), jax.ShapeDtypeStruct((t, PROJ), BF16),
                   jax.ShapeDtypeStruct((t, D), BF16), jax.ShapeDtypeStruct((1, D), F32)],
        scratch_shapes=[pltpu.VMEM((NCHIP, D, wc), BF16)],
        compiler_params=_cp(("arbitrary",)), name="bwd_inproj",
    )(dxm, x, dhc, dbg, dcg, dq, dk, dv, g, w_all)


def wgrad(a, b, kb, nb, by_columns, name):
    t, k = a.shape
    n = b.shape[1]
    tk = 512

    def body(a_ref, b_ref, o_ref):
        o_ref[...] = jnp.zeros_like(o_ref)
        for c in range(t // tk):
            o_ref[...] += lax.dot_general(a_ref[tk * c:tk * (c + 1), :], b_ref[tk * c:tk * (c + 1), :], TN,
                                          preferred_element_type=F32)

    if by_columns:
        assert nb == n // NCHIP
        out_spec = pl.BlockSpec((None, kb, nb), lambda ki, ni: (ni, ki, 0))
        out_shape = jax.ShapeDtypeStruct((NCHIP, k, nb), F32)
    else:
        assert nb == n
        out_spec = pl.BlockSpec((kb, nb), lambda ki, ni: (ki, 0))
        out_shape = jax.ShapeDtypeStruct((k, n), F32)
    return pl.pallas_call(
        body, grid=(k // kb, n // nb),
        in_specs=[pl.BlockSpec((t, kb), lambda ki, ni: (0, ki)), pl.BlockSpec((t, nb), lambda ki, ni: (0, ni))],
        out_specs=out_spec, out_shape=out_shape,
        compiler_params=_cp(("arbitrary", "arbitrary")), name=name)(a, b)


TOE = 1024
assert 2 * QG_FWD + LEFT <= TOE
N_FLAT = LEFT - REL_CLIP + 1
N_VAR = BAND - N_FLAT


def _diag_vector(table):
    last = table[:, 2 * REL_CLIP:]
    var = table[:, 2 * REL_CLIP - N_VAR:2 * REL_CLIP][:, ::-1]
    return jnp.concatenate([jnp.broadcast_to(last, (NH, N_FLAT)), var, jnp.broadcast_to(last, (NH, TOE - BAND))], axis=1)


def _diag_vector_bwd(dvec):
    dlast = jnp.sum(dvec[:, :N_FLAT], axis=1, keepdims=True) + jnp.sum(dvec[:, BAND:], axis=1, keepdims=True)
    dvar = dvec[:, N_FLAT:BAND][:, ::-1]
    return jnp.concatenate([jnp.zeros((NH, 2 * REL_CLIP - N_VAR), F32), dvar, dlast], axis=1)


def _band_valid(qg):
    r = lax.broadcasted_iota(jnp.int32, (qg, qg + LEFT), 0)
    p = lax.broadcasted_iota(jnp.int32, (qg, qg + LEFT), 1)
    start = lax.shift_left(lax.shift_right_logical(r, 6), 6)
    return (p >= start) & (p < start + BAND)


def bias_expand(vec, qgs):
    def body(v_ref, *o_refs):
        for qg, o_ref in zip(qgs, o_refs):
            valid = _band_valid(qg)
            for h in range(NH):
                rows = jnp.broadcast_to(v_ref[h:h + 1, :], (qg, TOE))
                toe = pltpu.roll(rows, 0, 1, stride=1, stride_axis=0)
                o_ref[h // 2, qg * (h % 2):qg * (h % 2 + 1), :] = jnp.where(valid, toe[:, :qg + LEFT], NEG_INF)

    return pl.pallas_call(body, out_shape=[jax.ShapeDtypeStruct((NH // 2, 2 * qg, qg + LEFT), F32) for qg in qgs],
                          name="bias_expand")(vec)


def bias_reduce(db2):
    _, qg, kg = db2.shape

    def body(d_ref, o_ref):
        ii = lax.broadcasted_iota(jnp.int32, (kg, kg), 0)
        jj = lax.broadcasted_iota(jnp.int32, (kg, kg), 1)
        flip = jnp.where(ii + jj == kg - 1, 1.0, 0.0).astype(BF16)
        for h in range(NH):
            rest = d_ref[h]
            rev = jnp.zeros((qg, kg), F32)
            for _ in range(3):
                term = rest.astype(BF16)
                rev = rev + jnp.dot(term, flip, preferred_element_type=F32)
                rest = rest - term.astype(F32)
            d = jnp.concatenate([jnp.zeros((qg, TOE - kg), F32), rev], axis=1)
            back = pltpu.roll(d, 0, 1, stride=1, stride_axis=0)
            o_ref[h:h + 1, :] = jnp.sum(back, axis=0, keepdims=True)

    rev = pl.pallas_call(body, out_shape=jax.ShapeDtypeStruct((NH, TOE), F32), name="bias_reduce")(db2)
    return rev[:, ::-1]


def _place():
    x, y, c = lax.axis_index("x"), lax.axis_index("y"), lax.axis_index("c")
    chips = [(1 - x, y), (x, 1 - y), (1 - x, 1 - y)]
    return x, y, c, chips


def _half(ref_rows, c):
    return pl.ds(c * (ref_rows // 2), ref_rows // 2)


HBM_SPEC = pl.BlockSpec(memory_space=pltpu.HBM)
SEM_SPEC = pl.BlockSpec(memory_space=pltpu.SEMAPHORE)
IN_FLIGHT = pltpu.CompilerParams(has_side_effects=pltpu.SideEffectType.DATAFLOW_SIDE_EFFECTING)


def _in_hbm(a):
    return pltpu.with_memory_space_constraint(a, pltpu.HBM)


def cast_to_slot(ws, chip, layer):
    n = len(ws)
    steps = 4

    def body(b_ref, *refs):
        del b_ref
        for w_ref, o_ref in zip(refs[:n], refs[n:]):
            o_ref[...] = w_ref[...].astype(BF16)

    grid_spec = pltpu.PrefetchScalarGridSpec(
        num_scalar_prefetch=1, grid=(steps,),
        in_specs=[pl.BlockSpec((None, w.shape[1] // steps, w.shape[2]), lambda r, b: (layer, r, 0)) for w in ws],
        out_specs=[pl.BlockSpec((None, w.shape[1] // steps, w.shape[2]), lambda r, b: (b[0], r, 0)) for w in ws])
    return pl.pallas_call(body, grid_spec=grid_spec,
                          out_shape=[jax.ShapeDtypeStruct((NCHIP,) + w.shape[1:], BF16) for w in ws],
                          compiler_params=_cp(("arbitrary",)), name="cast_to_slot")(chip, *ws)


def _gather_copies(bufs, send, recv):
    x, y, c, chips = _place()
    b = 2 * x + y
    out = []
    for k, buf in enumerate(bufs):
        rows = buf.shape[1]
        mine = buf.at[b, _half(rows, c), :]
        for j, (cx, cy) in enumerate(chips):
            theirs = buf.at[2 * cx + cy, _half(rows, c), :]
            sems = dict(send_sem=send.at[3 * k + j], recv_sem=recv.at[3 * k + j],
                        device_id=(cx, cy, c), device_id_type=MESH)
            out.append((pltpu.make_async_remote_copy(src_ref=mine, dst_ref=mine, **sems),
                        pltpu.make_async_remote_copy(src_ref=theirs, dst_ref=theirs, **sems)))
    return out


def gather_start(bufs, after, layer):
    n = len(bufs)

    def body(*refs):
        ins = refs[:n]
        send, recv = refs[n + 1], refs[n + 2]
        token = refs[-1]
        for start, _ in _gather_copies(ins, send, recv):
            start.start()
        token[...] = jnp.zeros_like(token)

    sems = pltpu.SemaphoreType.DMA((3 * n,))
    res = pl.pallas_call(
        body, name=f"gather_start_{layer}",
        in_specs=[HBM_SPEC] * n + [_any()],
        out_specs=[SEM_SPEC, SEM_SPEC] + [HBM_SPEC] * n + [pl.BlockSpec(memory_space=pltpu.VMEM)],
        out_shape=[sems, sems] + [pltpu.HBM(b.shape, b.dtype) for b in bufs] + [jax.ShapeDtypeStruct((8, LANES), F32)],
        input_output_aliases={k: 2 + k for k in range(n)}, compiler_params=IN_FLIGHT,
    )(*[_in_hbm(b) for b in bufs], after)
    return res[0], res[1], res[2:2 + n], res[-1]


def gather_wait(send, recv, bufs, after, layer):
    n = len(bufs)

    def body(*refs):
        ins = refs[:n]
        send_ref, recv_ref = refs[n], refs[n + 1]
        for start, arrival in _gather_copies(ins, send_ref, recv_ref):
            start.wait_send()
            arrival.wait_recv()

    return pl.pallas_call(
        body, name=f"gather_wait_{layer}",
        in_specs=[HBM_SPEC] * n + [SEM_SPEC, SEM_SPEC, _any()], out_specs=[HBM_SPEC] * n,
        out_shape=[pltpu.HBM(b.shape, b.dtype) for b in bufs],
        input_output_aliases={k: k for k in range(n)}, compiler_params=IN_FLIGHT,
    )(*bufs, send, recv, after)


def gather_forward(bufs):
    n = len(bufs)

    def body(*refs):
        outs = refs[n:2 * n]
        send, recv = refs[2 * n:]
        x, y, c, chips = _place()
        cps = []
        for k in range(n):
            rows = outs[k].shape[1]
            for j, (cx, cy) in enumerate(chips):
                sems = dict(send_sem=send.at[3 * k + j], recv_sem=recv.at[3 * k + j],
                            device_id=(x, y, 1 - c), device_id_type=MESH)
                mine = outs[k].at[2 * cx + cy, _half(rows, c), :]
                theirs = outs[k].at[2 * cx + cy, _half(rows, 1 - c), :]
                cp = pltpu.make_async_remote_copy(src_ref=mine, dst_ref=mine, **sems)
                cp.start()
                cps.append((cp, pltpu.make_async_remote_copy(src_ref=theirs, dst_ref=theirs, **sems)))
        for cp, arrival in cps:
            cp.wait_send()
            arrival.wait_recv()

    return pl.pallas_call(
        body, in_specs=[_any()] * n, out_specs=[_any()] * n,
        out_shape=[jax.ShapeDtypeStruct(b.shape, b.dtype) for b in bufs], input_output_aliases={k: k for k in range(n)},
        scratch_shapes=[pltpu.SemaphoreType.DMA((3 * n,)), pltpu.SemaphoreType.DMA((3 * n,))],
        name="gather_forward")(*bufs)


def _forward_copies(bufs, send, recv):
    x, y, c, chips = _place()
    out = []
    for k, buf in enumerate(bufs):
        rows = buf.shape[1]
        for j, (cx, cy) in enumerate(chips):
            sems = dict(send_sem=send.at[3 * k + j], recv_sem=recv.at[3 * k + j],
                        device_id=(x, y, 1 - c), device_id_type=MESH)
            mine = buf.at[2 * cx + cy, _half(rows, c), :]
            theirs = buf.at[2 * cx + cy, _half(rows, 1 - c), :]
            out.append((pltpu.make_async_remote_copy(src_ref=mine, dst_ref=mine, **sems),
                        pltpu.make_async_remote_copy(src_ref=theirs, dst_ref=theirs, **sems)))
    return out


def forward_start(bufs, tag):
    n = len(bufs)

    def body(*refs):
        ins = refs[:n]
        send, recv = refs[n], refs[n + 1]
        token = refs[-1]
        for start, _ in _forward_copies(ins, send, recv):
            start.start()
        token[...] = jnp.zeros_like(token)

    sems = pltpu.SemaphoreType.DMA((3 * n,))
    res = pl.pallas_call(
        body, name=f"forward_start_{tag}", in_specs=[HBM_SPEC] * n,
        out_specs=[SEM_SPEC, SEM_SPEC] + [HBM_SPEC] * n + [pl.BlockSpec(memory_space=pltpu.VMEM)],
        out_shape=[sems, sems] + [pltpu.HBM(b.shape, b.dtype) for b in bufs] + [jax.ShapeDtypeStruct((8, LANES), F32)],
        input_output_aliases={k: 2 + k for k in range(n)}, compiler_params=IN_FLIGHT,
    )(*[_in_hbm(b) for b in bufs])
    return res[0], res[1], res[2:2 + n], res[-1]


def forward_wait(send, recv, bufs, after, tag):
    n = len(bufs)

    def body(*refs):
        ins = refs[:n]
        send_ref, recv_ref = refs[n], refs[n + 1]
        for start, arrival in _forward_copies(ins, send_ref, recv_ref):
            start.wait_send()
            arrival.wait_recv()

    return pl.pallas_call(
        body, name=f"forward_wait_{tag}",
        in_specs=[HBM_SPEC] * n + [SEM_SPEC, SEM_SPEC, _any()], out_specs=[HBM_SPEC] * n,
        out_shape=[pltpu.HBM(b.shape, b.dtype) for b in bufs],
        input_output_aliases={k: k for k in range(n)}, compiler_params=IN_FLIGHT,
    )(*bufs, send, recv, after)


def _exchange_copies(srcs, lands, send, recv):
    x, y, c, _ = _place()
    return [pltpu.make_async_remote_copy(
        src_ref=src.at[:, _half(src.shape[1], 1 - c), :], dst_ref=land, send_sem=send.at[k], recv_sem=recv.at[k],
        device_id=(x, y, 1 - c), device_id_type=MESH) for k, (src, land) in enumerate(zip(srcs, lands))]


def exchange_start(srcs, tag):
    n = len(srcs)
    lands = [lax.empty((s.shape[0], s.shape[1] // 2, s.shape[2]), s.dtype) for s in srcs]

    def body(*refs):
        ins, land_refs = refs[:n], refs[n:2 * n]
        send, recv = refs[2 * n], refs[2 * n + 1]
        token = refs[-1]
        for cp in _exchange_copies(ins, land_refs, send, recv):
            cp.start()
        token[...] = jnp.zeros_like(token)

    sems = pltpu.SemaphoreType.DMA((n,))
    res = pl.pallas_call(
        body, name=f"exchange_start_{tag}",
        in_specs=[HBM_SPEC] * (2 * n),
        out_specs=[SEM_SPEC, SEM_SPEC] + [HBM_SPEC] * (2 * n) + [pl.BlockSpec(memory_space=pltpu.VMEM)],
        out_shape=[sems, sems] + [pltpu.HBM(a.shape, a.dtype) for a in list(srcs) + lands]
        + [jax.ShapeDtypeStruct((8, LANES), F32)],
        input_output_aliases={k: 2 + k for k in range(2 * n)}, compiler_params=IN_FLIGHT,
    )(*[_in_hbm(a) for a in list(srcs) + lands])
    return res[0], res[1], res[2:2 + n], res[2 + n:2 + 2 * n], res[-1]


def exchange_wait(send, recv, srcs, lands, after, tag):
    n = len(srcs)

    def body(*refs):
        ins, land_refs = refs[:n], refs[n:2 * n]
        send_ref, recv_ref = refs[2 * n], refs[2 * n + 1]
        for cp in _exchange_copies(ins, land_refs, send_ref, recv_ref):
            cp.wait_send()
            cp.wait_recv()

    res = pl.pallas_call(
        body, name=f"exchange_wait_{tag}",
        in_specs=[HBM_SPEC] * (2 * n) + [SEM_SPEC, SEM_SPEC, _any()], out_specs=[HBM_SPEC] * (2 * n),
        out_shape=[pltpu.HBM(a.shape, a.dtype) for a in list(srcs) + list(lands)],
        input_output_aliases={k: k for k in range(2 * n)}, compiler_params=IN_FLIGHT,
    )(*srcs, *lands, send, recv, after)
    return res[:n], res[n:]


def add_pair(gs, r1s, core):
    n = len(gs)

    def body(c_ref, *refs):
        del c_ref
        for g_ref, r_ref, o_ref in zip(refs[:n], refs[n:2 * n], refs[2 * n:]):
            o_ref[...] = (g_ref[...] + r_ref[...]).astype(BF16)

    blk = lambda r: (None,) + r.shape[1:]
    grid_spec = pltpu.PrefetchScalarGridSpec(
        num_scalar_prefetch=1, grid=(NCHIP,),
        in_specs=[pl.BlockSpec(blk(r), lambda s, c: (s, c[0], 0)) for r in r1s]
        + [pl.BlockSpec(blk(r), lambda s, c: (s, 0, 0)) for r in r1s],
        out_specs=[pl.BlockSpec(blk(r), lambda s, c: (s, 0, 0)) for r in r1s])
    return pl.pallas_call(body, grid_spec=grid_spec, out_shape=[jax.ShapeDtypeStruct(r.shape, BF16) for r in r1s],
                          compiler_params=_cp(("arbitrary",)), name="add_pair")(core, *gs, *r1s)


def _scatter_copies(srcs, lands, send, recv):
    _, _, c, chips = _place()
    out = []
    for k, (src, land) in enumerate(zip(srcs, lands)):
        for j, (cx, cy) in enumerate(chips):
            out.append(pltpu.make_async_remote_copy(
                src_ref=src.at[2 * cx + cy], dst_ref=land.at[j], send_sem=send.at[3 * k + j],
                recv_sem=recv.at[3 * k + j], device_id=(cx, cy, c), device_id_type=MESH))
    return out


def scatter_start(srcs, layer):
    n = len(srcs)
    srcs = list(srcs)
    lands = [lax.empty((3,) + s.shape[1:], s.dtype) for s in srcs]

    def body(*refs):
        ins, land_refs = refs[:n], refs[n:2 * n]
        send, recv = refs[2 * n], refs[2 * n + 1]
        token = refs[-1]
        for cp in _scatter_copies(ins, land_refs, send, recv):
            cp.start()
        token[...] = jnp.zeros_like(token)

    sems = pltpu.SemaphoreType.DMA((3 * n,))
    res = pl.pallas_call(
        body, name=f"scatter_start_{layer}",
        in_specs=[HBM_SPEC] * (2 * n),
        out_specs=[SEM_SPEC, SEM_SPEC] + [HBM_SPEC] * (2 * n) + [pl.BlockSpec(memory_space=pltpu.VMEM)],
        out_shape=[sems, sems] + [pltpu.HBM(a.shape, a.dtype) for a in srcs + lands]
        + [jax.ShapeDtypeStruct((8, LANES), F32)],
        input_output_aliases={k: 2 + k for k in range(2 * n)}, compiler_params=IN_FLIGHT,
    )(*[_in_hbm(a) for a in srcs + lands])
    return res[0], res[1], res[2:2 + n], res[2 + n:2 + 2 * n], res[-1]


def scatter_wait(send, recv, srcs, lands, after, layer):
    n = len(srcs)

    def body(*refs):
        ins, land_refs = refs[:n], refs[n:2 * n]
        send_ref, recv_ref = refs[2 * n], refs[2 * n + 1]
        for cp in _scatter_copies(ins, land_refs, send_ref, recv_ref):
            cp.wait_send()
            cp.wait_recv()

    res = pl.pallas_call(
        body, name=f"scatter_wait_{layer}",
        in_specs=[HBM_SPEC] * (2 * n) + [SEM_SPEC, SEM_SPEC, _any()], out_specs=[HBM_SPEC] * (2 * n),
        out_shape=[pltpu.HBM(a.shape, a.dtype) for a in list(srcs) + list(lands)],
        input_output_aliases={k: k for k in range(2 * n)}, compiler_params=IN_FLIGHT,
    )(*srcs, *lands, send, recv, after)
    return res[n:]


def add_chips(gs, r1s, r2s, place, totals, layer):
    n = len(gs)
    steps = 2

    def body(p_ref, *refs):
        del p_ref
        for g_ref, r1_ref, r2_ref, o_ref in zip(refs[:n], refs[n:2 * n], refs[2 * n:3 * n], refs[4 * n:]):
            own = g_ref[...] + r1_ref[...]
            o_ref[...] = ((own + r2_ref[0].astype(F32)) + r2_ref[1].astype(F32)) + r2_ref[2].astype(F32)

    blk = lambda r: (None, r.shape[1] // steps, r.shape[2])
    grid_spec = pltpu.PrefetchScalarGridSpec(
        num_scalar_prefetch=1, grid=(steps,),
        in_specs=[pl.BlockSpec(blk(r), lambda i, p: (p[1], p[0] * steps + i, 0)) for r in r1s]
        + [pl.BlockSpec(blk(r), lambda i, p: (p[1], i, 0)) for r in r1s]
        + [pl.BlockSpec((3,) + blk(r)[1:], lambda i, p: (0, i, 0)) for r in r1s] + [_any()] * n,
        out_specs=[pl.BlockSpec(blk(r), lambda i, p: (layer, p[0] * steps + i, 0)) for r in r1s])
    return pl.pallas_call(body, grid_spec=grid_spec, out_shape=[jax.ShapeDtypeStruct(t.shape, F32) for t in totals],
                          input_output_aliases={1 + 3 * n + k: k for k in range(n)},
                          compiler_params=_cp(("arbitrary",)), name="add_chips")(place, *gs, *r1s, *r2s, *totals)


def pair_share(gs, tag):
    n = len(gs)

    def body(*refs):
        outs = refs[n:2 * n]
        send, recv = refs[2 * n:]
        x, y, c, _ = _place()
        cps = []
        for k in range(n):
            mine = outs[k].at[:, _half(outs[k].shape[1], c), :]
            cp = pltpu.make_async_remote_copy(
                src_ref=mine, dst_ref=mine, send_sem=send.at[k], recv_sem=recv.at[k],
                device_id=(x, y, 1 - c), device_id_type=MESH)
            cp.start()
            cps.append(cp)
        for k, cp in enumerate(cps):
            cp.wait_send()
            theirs = outs[k].at[:, _half(outs[k].shape[1], 1 - c), :]
            pltpu.make_async_remote_copy(
                src_ref=theirs, dst_ref=theirs, send_sem=send.at[k], recv_sem=recv.at[k],
                device_id=(x, y, 1 - c), device_id_type=MESH).wait_recv()

    return pl.pallas_call(
        body, in_specs=[_any()] * n, out_specs=[_any()] * n,
        out_shape=[jax.ShapeDtypeStruct(g.shape, g.dtype) for g in gs], input_output_aliases={k: k for k in range(n)},
        scratch_shapes=[pltpu.SemaphoreType.DMA((n,)), pltpu.SemaphoreType.DMA((n,))],
        name=f"pair_share_{tag}")(*gs)


def small_collect(v, reduce, name):
    rows = v.shape[0]
    flips = [(fx, fy, fc) for fx in (0, 1) for fy in (0, 1) for fc in (0, 1)][1:]

    def body(v_ref, o_ref, buf, send, recv):
        x, y, c, _ = _place()
        buf[4 * x + 2 * y + c] = v_ref[...]
        peers = [(jnp.where(fx, 1 - x, x), jnp.where(fy, 1 - y, y), jnp.where(fc, 1 - c, c)) for fx, fy, fc in flips]
        cps = []
        for k, peer in enumerate(peers):
            cp = pltpu.make_async_remote_copy(
                src_ref=v_ref, dst_ref=buf.at[4 * x + 2 * y + c], send_sem=send.at[k], recv_sem=recv.at[k],
                device_id=peer, device_id_type=MESH)
            cp.start()
            cps.append(cp)
        for k, (px, py, pc) in enumerate(peers):
            pltpu.make_async_remote_copy(
                src_ref=v_ref, dst_ref=buf.at[4 * px + 2 * py + pc], send_sem=send.at[k], recv_sem=recv.at[k],
                device_id=(px, py, pc), device_id_type=MESH).wait_recv()
        for cp in cps:
            cp.wait_send()
        if reduce:
            acc = buf[0]
            for s in range(1, 8):
                acc = acc + buf[s]
            o_ref[...] = acc
        else:
            o_ref[...] = buf[...]

    vm = pl.BlockSpec(memory_space=pltpu.VMEM)
    out_shape = jax.ShapeDtypeStruct((rows, SMALL_COLS) if reduce else (8, rows, SMALL_COLS), F32)
    return pl.pallas_call(
        body, in_specs=[vm], out_specs=vm, out_shape=out_shape,
        scratch_shapes=[pltpu.VMEM((8, rows, SMALL_COLS), F32), pltpu.SemaphoreType.DMA((7,)),
                        pltpu.SemaphoreType.DMA((7,))],
        name=name)(v)


def adamw(w, g, m, v, rb, name):
    nl, rows, cols = w.shape

    def body(w_ref, g_ref, m_ref, v_ref, go_ref, d_ref, nm_ref, nv_ref):
        gv = g_ref[...]
        go_ref[...] = gv
        nm = ADAM_B1 * m_ref[...] + (1.0 - ADAM_B1) * gv
        nv = ADAM_B2 * v_ref[...] + (1.0 - ADAM_B2) * (gv * gv)
        m_hat = nm / (1.0 - ADAM_B1 ** ADAM_STEP)
        v_hat = nv / (1.0 - ADAM_B2 ** ADAM_STEP)
        d_ref[...] = -ADAM_LR * (m_hat / (jnp.sqrt(v_hat) + ADAM_EPS) + ADAM_WD * w_ref[...])
        nm_ref[...] = nm
        nv_ref[...] = nv

    blk = pl.BlockSpec((None, rb, cols), lambda l, r: (l, r, 0))
    shp = jax.ShapeDtypeStruct(w.shape, F32)
    return pl.pallas_call(body, grid=(nl, rows // rb), in_specs=[blk] * 4, out_specs=[blk] * 4, out_shape=[shp] * 4,
                          compiler_params=_cp(("arbitrary", "arbitrary")), name=name)(w, g, m, v)


def _pack(parts, rows):
    flat = jnp.concatenate([p.reshape(-1).astype(F32) for p in parts])
    return jnp.pad(flat, (0, rows * SMALL_COLS - flat.shape[0])).reshape(rows, SMALL_COLS)


def _unpack(vec, shapes):
    flat = vec.reshape(-1)
    out, off = [], 0
    for s in shapes:
        size = 1
        for d in s:
            size *= d
        out.append(flat[off:off + size].reshape(s))
        off += size
    return out


def kernel(x, w_in, w_conv, rel_bias, g_conv_out, g_attn_out, w_out, g_pre_mix, g_post_mix, g_pre_ffn, g_post_ffn, w_ffn_in, w_ffn_out, loss_target, m_w_in, m_w_conv, m_rel_bias, m_g_conv_out, m_g_attn_out, m_w_out, m_g_pre_mix, m_g_post_mix, m_g_pre_ffn, m_g_post_ffn, m_w_ffn_in, m_w_ffn_out, v_w_in, v_w_conv, v_rel_bias, v_g_conv_out, v_g_attn_out, v_w_out, v_g_pre_mix, v_g_post_mix, v_g_pre_ffn, v_g_post_ffn, v_w_ffn_in, v_w_ffn_out):
    xi, yi, ci = lax.axis_index("x"), lax.axis_index("y"), lax.axis_index("c")
    chip = 2 * xi + yi
    nl = w_in.shape[0]
    x0 = x[0]
    target = loss_target[0]
    cwl = CW // NCHIP

    chip1 = chip.reshape(1).astype(jnp.int32)
    own = [cast_to_slot([w_in, w_out, w_ffn_in, w_ffn_out], chip1, l) for l in range(nl)]
    wc_mine = jnp.pad(w_conv.reshape(-1), (0, 16 * LANES - w_conv.size)).reshape(1, 16, LANES)
    wc_slot = lax.dynamic_update_slice_in_dim(jnp.zeros((NCHIP, 16, LANES), F32), wc_mine, chip, axis=0)
    gm = jnp.kron(jnp.eye(CW // HD, dtype=F32), jnp.full((HD, HD), 1.0 / HD, F32)).astype(BF16)
    row = lambda a, l: a[l][None, :]

    def token(t):
        return t[0:1, 0:1]

    def gather_finish(flight, after, tag):
        send, recv, bufs, _ = flight
        return gather_forward(gather_wait(send, recv, bufs, after, tag))

    first_mix = gather_start(list(own[0][:2]) + [wc_slot], x0, "0m")
    first_ffn = gather_start(own[0][2:], first_mix[3], "0f")
    gw_in, gw_out, wc_all = gather_finish(first_mix, x0, "0m")
    wc_full = wc_all.reshape(NCHIP, -1)[:, :nl * cwl * 3].reshape(NCHIP, nl, cwl, 3)
    wc_full = jnp.transpose(wc_full, (1, 0, 2, 3)).reshape(nl, CW, 3)
    wconv_t = jnp.pad(jnp.transpose(wc_full, (0, 2, 1)), ((0, 0), (0, 5), (0, 0)))
    flight = to_sibling = None
    saved, weights = [], []
    h = x0
    for l in range(nl):
        if l == 0:
            pass
        elif l == 1:
            gw_in, gw_out, gw_fi, gw_fo = gather_finish(flight, h, l)
        else:
            gw_in, gw_out, gw_fi, gw_fo = forward_wait(*to_sibling[:3], h, l)
        gw_out = gw_out.reshape(D, D)
        g_pm, g_pf = row(g_pre_mix, l), row(g_pre_ffn, l)
        if l == 0:
            g_pm = g_pm + token(first_ffn[3])
        if l + 1 < nl:
            flight = gather_start(own[l + 1], first_ffn[3] if l == 0 else gw_in, l + 1)
            g_pm = g_pm + token(flight[3])
        bias2, bias2_bwd = bias_expand(_diag_vector(rel_bias[l]), (QG_FWD, QG_BWD))
        proj = fwd_inproj(h, g_pm, gw_in)
        xmid, o, lse, y, z = fwd_mix(h, proj, bias2, wconv_t[l], row(g_conv_out, l), row(g_attn_out, l),
                                     row(g_post_mix, l), gm, gw_out)
        if l == 0:
            gw_fi, gw_fo = gather_finish(first_ffn, xmid, "0f")
        elif l + 1 < nl:
            send, recv, bufs, _ = flight
            to_sibling = forward_start(gather_wait(send, recv, bufs, xmid, l + 1), l + 1)
            g_pf = g_pf + token(to_sibling[3])
        gw_fo = gw_fo.reshape(2, DFF // 2, D)
        gu, f, xout = fwd_ffn(xmid, g_pf, row(g_post_ffn, l), gw_fi, gw_fo)
        saved.append((h, proj, bias2_bwd, xmid, o, lse, y, z, gu, f))
        weights.append((gw_in, gw_out, gw_fi, gw_fo))
        h = xout
    dx, loss_blk = loss_head(h, target)

    core = ci.reshape(1).astype(jnp.int32)
    place = jnp.stack([ci, chip]).astype(jnp.int32)
    totals = [lax.empty(w.shape, F32) for w in (w_in, w_out, w_ffn_in, w_ffn_out)]
    small = {k: [None] * nl for k in ("co", "ao", "pm", "qm", "pf", "qf", "rel", "wc")}

    def reduce_begin(kinds, grads, tag):
        return kinds, exchange_start(grads, tag), tag

    def reduce_mid(state, after):
        kinds, (send, recv, srcs, lands, _), tag = state
        grads, from_sibling = exchange_wait(send, recv, srcs, lands, after, tag)
        return kinds, grads, from_sibling, scatter_start(add_pair(grads, from_sibling, core), tag), tag

    def reduce_end(state, after, totals, layer):
        kinds, grads, from_sibling, (send, recv, srcs, lands, _), tag = state
        from_chips = scatter_wait(send, recv, srcs, lands, after, tag)
        totals = list(totals)
        summed = add_chips(grads, from_sibling, from_chips, place, [totals[i] for i in kinds], layer)
        for i, t in zip(kinds, summed):
            totals[i] = t
        return totals

    begun = flying = None
    for l in reversed(range(nl)):
        hin, proj, bias2, xmid, o, lse, y, z, gu, f = saved[l]
        gw_in, gw_out, gw_fi, gw_fo = weights[l]
        g_qf, g_qm, wct = row(g_post_ffn, l), row(g_post_mix, l), wconv_t[l]
        if begun is not None:
            g_qf = g_qf + token(begun[1][4])
        dxm, dfb, act, dgu, h2, dg_qf, dg_pf = bwd_ffn(dx, f, xmid, gu, row(g_pre_ffn, l), g_qf, gw_fi, gw_fo)
        if begun is not None:
            flying = reduce_mid(begun, dxm)
            g_qm = g_qm + token(flying[3][4])
        gr_fo = wgrad(act, dfb, 256, D, False, "wgrad_ffn_out").reshape(NCHIP, DFF // NCHIP, D)
        gr_fi = wgrad(h2, dgu, 512, 2 * DFF // NCHIP, True, "wgrad_ffn_in")
        if l == 0:
            begun_ffn = reduce_begin([2, 3], [gr_fi, gr_fo], "0f")
            g_qm = g_qm + token(begun_ffn[1][4])
        dzb, do, dco, dbg, dg_qm, dg_co, dg_ao = bwd_mix(dxm, z, o, proj, wct, row(g_conv_out, l),
                                                          row(g_attn_out, l), g_qm, gm, gw_out)
        if l == 0:
            flying_ffn = reduce_mid(begun_ffn, dzb)
            wct = wct + token(flying_ffn[3][4])
        gr_out = wgrad(y, dzb, 512, D, False, "wgrad_out").reshape(NCHIP, D // NCHIP, D)
        dhc, dcg, dwc = bwd_conv(dco, proj, wct)
        dq, dk, dv, db2 = bwd_attn(proj, o, do, lse, bias2)
        dx, dproj, hb, dg_pm = bwd_inproj(dxm, hin, dhc, dbg, dcg, dq, dk, dv, row(g_pre_mix, l), gw_in)
        if flying is not None:
            totals = reduce_end(flying, dx, totals, l + 1)
        gr_in = wgrad(hb, dproj, 512, PROJ // NCHIP, True, "wgrad_in")
        small["co"][l], small["ao"][l], small["pm"][l], small["qm"][l] = dg_co, dg_ao, dg_pm, dg_qm
        small["pf"][l], small["qf"][l] = dg_pf, dg_qf
        small["rel"][l] = _diag_vector_bwd(bias_reduce(db2.reshape(NH, QG_BWD, QG_BWD + LEFT)))
        small["wc"][l] = jnp.transpose(dwc[0:3], (1, 0))
        if l > 0:
            begun = reduce_begin([0, 1, 2, 3], [gr_in, gr_out, gr_fi, gr_fo], l)
    flying_mix = reduce_mid(reduce_begin([0, 1], [gr_in, gr_out], "0m"), dx)
    totals = reduce_end(flying_ffn, flying_mix[3][4], totals, 0)
    gr_fi, gr_fo = pair_share(totals[2:], "ffn")
    big_fi = adamw(w_ffn_in, gr_fi, m_w_ffn_in, v_w_ffn_in, w_ffn_in.shape[1] // 4, "adamw_ffn_in")
    big_fo = adamw(w_ffn_out, gr_fo, m_w_ffn_out, v_w_ffn_out, w_ffn_out.shape[1] // 4, "adamw_ffn_out")
    totals = reduce_end(flying_mix, big_fo[1], totals, 0)
    gr_in, gr_out = pair_share(totals[:2], "mix")
    big_in = adamw(w_in, gr_in, m_w_in, v_w_in, w_in.shape[1] // 4, "adamw_in")
    big_out = adamw(w_out, gr_out, m_w_out, v_w_out, w_out.shape[1] // 4, "adamw_out")
    big = [big_in, big_out, big_fi, big_fo]

    order = ("co", "ao", "pm", "qm", "pf", "qf", "rel", "wc")
    parts = [jnp.stack(small[k]) for k in order] + [loss_blk[0:1, 0:1]]
    shapes = [p.shape for p in parts]
    red = _unpack(small_collect(_pack(parts, 40), True, "reduce_small"), shapes)
    gr_co, gr_ao, gr_pm, gr_qm, gr_pf, gr_qf, gr_rel, gr_wc_full, loss = red
    gr_co, gr_ao, gr_pm, gr_qm, gr_pf, gr_qf = [a.reshape(nl, -1) for a in (gr_co, gr_ao, gr_pm, gr_qm, gr_pf, gr_qf)]
    gr_wc = lax.dynamic_slice_in_dim(gr_wc_full, chip * cwl, cwl, axis=1)
    loss = loss.reshape(())

    sw = [g_conv_out, g_attn_out, g_pre_mix, g_post_mix, g_pre_ffn, g_post_ffn, rel_bias, w_conv]
    sg = [gr_co, gr_ao, gr_pm, gr_qm, gr_pf, gr_qf, gr_rel, gr_wc]
    sm = [m_g_conv_out, m_g_attn_out, m_g_pre_mix, m_g_post_mix, m_g_pre_ffn, m_g_post_ffn, m_rel_bias, m_w_conv]
    sv = [v_g_conv_out, v_g_attn_out, v_g_pre_mix, v_g_post_mix, v_g_pre_ffn, v_g_post_ffn, v_rel_bias, v_w_conv]
    sshapes = [a.shape for a in sw]
    packed = [_pack(a, 32)[None] for a in (sw, sg, sm, sv)]
    s_out = [_unpack(a[0], sshapes) for a in adamw(*packed, 32, "adamw_small")]

    def leaves(big_i, small_i):
        b_in, b_out, b_fi, b_fo = big_i
        s_co, s_ao, s_pm, s_qm, s_pf, s_qf, s_rel, s_wc = small_i
        return [b_in, s_wc, s_rel, s_co, s_ao, b_out, s_pm, s_qm, s_pf, s_qf, b_fi, b_fo]

    out = [loss, dx[None]]
    out += leaves([b[0] for b in big], sg)
    for i in range(1, 4):
        out += leaves([b[i] for b in big], s_out[i])
    return tuple(out)
```

```python
import functools

import jax
import jax.numpy as jnp
from jax import lax
from jax.experimental import pallas as pl
from jax.experimental.pallas import tpu as pltpu

F32 = jnp.float32
BF16 = jnp.bfloat16

D = 1024
PROJ = 3072
CW = 512
HD = 64
NH = 8
CHUNK = 64
BAND = 576
REL_CLIP = 128
NREL = 2 * REL_CLIP + 1
DFF = 2816
DEPTH = 4
NCHIP = 4
EPS = 1e-6
NEG_INF = -1e30

ADAM_LR = 0.001
ADAM_B1 = 0.9
ADAM_B2 = 0.999
ADAM_EPS = 1e-08
ADAM_WD = 0.01
ADAM_STEP = 10

V7X_VMEM_BYTES = 64 * 1024 * 1024
VMEM_LIMIT = V7X_VMEM_BYTES - 8 * 1024 * 1024
LANES = 128
QG_FWD = 4 * CHUNK
QG_BWD = 2 * CHUNK
LEFT = BAND - CHUNK
TQ = 512
TM = 256
SMALL_COLS = 1024
MESH = pl.DeviceIdType.MESH
NT = (((1,), (1,)), ((), ()))
TN = (((0,), (0,)), ((), ()))


def _cp(sem=None, vmem=VMEM_LIMIT):
    return pltpu.CompilerParams(dimension_semantics=sem, vmem_limit_bytes=vmem)


def _any():
    return pl.BlockSpec(memory_space=pl.ANY)


def _const(shape):
    nd = len(shape)
    return pl.BlockSpec(shape, lambda *_: (0,) * nd)


def _rms(v, g):
    r = lax.rsqrt(jnp.mean(v * v, axis=-1, keepdims=True) + EPS)
    return v * r * g


def _rms_bwd(dy, v, g):
    r = lax.rsqrt(jnp.mean(v * v, axis=-1, keepdims=True) + EPS)
    vh = v * r
    dg = jnp.sum(dy * vh, axis=0, keepdims=True)
    dvh = dy * g
    dv = r * (dvh - vh * jnp.mean(dvh * vh, axis=-1, keepdims=True))
    return dv, dg


def _group_mean(v, gm):
    hi = v.astype(BF16)
    lo = (v - hi.astype(F32)).astype(BF16)
    return jnp.dot(hi, gm, preferred_element_type=F32) + jnp.dot(lo, gm, preferred_element_type=F32)


def _group_rms_bwd(dy, v, g, gm):
    r = lax.rsqrt(_group_mean(v * v, gm) + EPS)
    vh = v * r
    dg = jnp.sum(dy * vh, axis=0, keepdims=True)
    dvh = dy * g
    dv = r * (dvh - vh * _group_mean(dvh * vh, gm))
    return dv, dg


def _head_masks(scale):
    lane = lax.broadcasted_iota(jnp.int32, (1, LANES), 1)
    return [jnp.where((lane >= HD * a) & (lane < HD * (a + 1)), scale, 0.0).astype(BF16) for a in range(2)]


class _Resident:
    def __init__(self, src, dst, sem):
        self.first = pl.program_id(0) == 0
        self.copy = pltpu.make_async_copy(src, dst, sem)
        self.dst = dst

        @pl.when(self.first)
        def _():
            self.copy.start()

    def read(self):
        @pl.when(self.first)
        def _():
            self.copy.wait()

        return self.dst[...]


def _stream_weights_once(pieces, sems, step):
    copies = [pltpu.make_async_copy(src, dst, sems.at[k]) for k, (src, dst) in enumerate(pieces)]
    first = pl.program_id(0) == 0

    @pl.when(first)
    def _():
        for cp in copies:
            cp.start()
        step(lambda k: copies[k].wait())

    @pl.when(jnp.logical_not(first))
    def _():
        step(lambda k: None)


def _conv_taps(u_prev, u, scr):
    n = u.shape[0]
    scr[0:16, :] = u_prev
    scr[16:16 + n, :] = u
    return scr[15:15 + n, :], scr[14:14 + n, :]


def fwd_inproj(x, g, w_all):
    t = x.shape[0]
    wc = PROJ // NCHIP

    def body(x_ref, g_ref, w_hbm, o_ref, w_v, sems):
        def step(ready):
            h = _rms(x_ref[...], g_ref[...]).astype(BF16)
            for b in range(NCHIP):
                ready(b)
                o_ref[:, wc * b:wc * (b + 1)] = jnp.dot(h, w_v[b], preferred_element_type=F32).astype(BF16)

        _stream_weights_once([(w_hbm.at[b], w_v.at[b]) for b in range(NCHIP)], sems, step)

    return pl.pallas_call(
        body, grid=(t // TQ,),
        in_specs=[pl.BlockSpec((TQ, D), lambda i: (i, 0)), _const((1, D)), _any()],
        out_specs=pl.BlockSpec((TQ, PROJ), lambda i: (i, 0)),
        out_shape=jax.ShapeDtypeStruct((t, PROJ), BF16),
        scratch_shapes=[pltpu.VMEM((NCHIP, D, wc), BF16), pltpu.SemaphoreType.DMA((NCHIP,))],
        compiler_params=_cp(("arbitrary",)), name="fwd_inproj")(x, g, w_all)


def _attn_window_specs():
    return [
        pl.BlockSpec((TQ, CW), lambda i: (i, 3)),
        pl.BlockSpec((TQ, CW), lambda i: (jnp.maximum(i - 1, 0), 4)),
        pl.BlockSpec((TQ, CW), lambda i: (i, 4)),
        pl.BlockSpec((TQ, CW), lambda i: (jnp.maximum(i - 1, 0), 5)),
        pl.BlockSpec((TQ, CW), lambda i: (i, 5)),
    ]


def _conv_specs():
    return [
        pl.BlockSpec((TQ, 3 * CW), lambda i: (i, 0)),
        pl.BlockSpec((16, 3 * CW), lambda i: (jnp.maximum(i * (TQ // 16) - 1, 0), 0)),
    ]


def _conv_fwd(pc_ref, pcp_ref, wc_ref, scr, first):
    pc = pc_ref[...].astype(F32)
    hc, bg, cg = pc[:, :CW], pc[:, CW:2 * CW], pc[:, 2 * CW:]
    u = cg * hc
    pp = pcp_ref[...].astype(F32)
    u_prev = jnp.where(first, 0.0, pp[:, 2 * CW:] * pp[:, :CW])
    u1, u2 = _conv_taps(u_prev, u, scr)
    cout = wc_ref[0:1, :] * u2 + wc_ref[1:2, :] * u1 + wc_ref[2:3, :] * u
    return hc, bg, cg, u, u1, u2, cout


def _key_penalty(first, r0, kg):
    col = lax.broadcasted_iota(jnp.int32, (1, kg), 1)
    limit = jnp.where(first, TQ - r0, 0)
    return jnp.where(col < limit, NEG_INF, 0.0)


def fwd_mix(x, proj, bias2, wconv_t, g_co, g_ao, g_pm, gm, wout_all):
    t = x.shape[0]
    qg, kg = QG_FWD, QG_FWD + LEFT

    def body(x_ref, pc_ref, pcp_ref, q_ref, kp_ref, kc_ref, vp_ref, vc_ref, b2_ref, wc_ref, gco_ref, gao_ref, gpm_ref,
             gm_ref, wout_hbm, xmid_ref, o_ref, lse_ref, y_ref, z_ref, wout_v, kwin, vwin, cscr, sems):
        i = pl.program_id(0)
        first = i == 0
        wout = _Resident(wout_hbm, wout_v, sems.at[0])
        kwin[0:TQ, :] = kp_ref[...]
        kwin[TQ:2 * TQ, :] = kc_ref[...]
        vwin[0:TQ, :] = vp_ref[...]
        vwin[TQ:2 * TQ, :] = vc_ref[...]
        qmask = _head_masks(HD ** -0.5)
        low = lax.broadcasted_iota(jnp.int32, (1, LANES), 1) < HD

        def group(g, carry):
            r0 = pl.multiple_of(g * qg, qg)
            pen = _key_penalty(first, r0, kg)
            for hp in range(NH // 2):
                ls = slice(LANES * hp, LANES * (hp + 1))
                qb = q_ref[pl.ds(r0, qg), ls]
                q2 = jnp.concatenate([qb * qmask[0], qb * qmask[1]], axis=0)
                s = lax.dot_general(q2, kwin[pl.ds(r0, kg), ls], NT, preferred_element_type=F32)
                s = s + b2_ref[hp] + pen
                m = jnp.max(s, axis=-1, keepdims=True)
                p = jnp.exp(s - m)
                l = jnp.sum(p, axis=-1, keepdims=True)
                o2 = jnp.dot(p.astype(BF16), vwin[pl.ds(r0, kg), ls], preferred_element_type=F32) * (1.0 / l)
                lse2 = m + jnp.log(l)
                o_ref[pl.ds(r0, qg), ls] = jnp.where(low, o2[:qg], o2[qg:])
                lse_ref[pl.ds(r0, qg), ls] = jnp.where(low, lse2[:qg], lse2[qg:])
            return carry

        lax.fori_loop(0, TQ // qg, group, 0)

        _, bg, _, _, _, _, cout = _conv_fwd(pc_ref, pcp_ref, wc_ref, cscr, first)
        yc = bg * cout
        gmv = gm_ref[...]
        ycn = yc * lax.rsqrt(_group_mean(yc * yc, gmv) + EPS) * gco_ref[...]
        oa = o_ref[...]
        oan = oa * lax.rsqrt(_group_mean(oa * oa, gmv) + EPS) * gao_ref[...]
        y_ref[:, 0:CW] = ycn.astype(BF16)
        y_ref[:, CW:2 * CW] = oan.astype(BF16)
        z = jnp.dot(y_ref[...], wout.read(), preferred_element_type=F32)
        z_ref[...] = z
        xmid_ref[...] = x_ref[...] + _rms(z, gpm_ref[...])

    row = lambda w: pl.BlockSpec((TQ, w), lambda i: (i, 0))
    return pl.pallas_call(
        body, grid=(t // TQ,),
        in_specs=[row(D)] + _conv_specs() + _attn_window_specs() + [
            _const((NH // 2, 2 * qg, kg)), _const((8, CW)), _const((1, CW)), _const((1, CW)), _const((1, D)),
            _const((CW, CW)), _any()],
        out_specs=[row(D), row(CW), row(CW), row(D), row(D)],
        out_shape=[jax.ShapeDtypeStruct((t, D), F32), jax.ShapeDtypeStruct((t, CW), F32),
                   jax.ShapeDtypeStruct((t, CW), F32), jax.ShapeDtypeStruct((t, D), BF16),
                   jax.ShapeDtypeStruct((t, D), F32)],
        scratch_shapes=[pltpu.VMEM((D, D), BF16), pltpu.VMEM((2 * TQ, CW), BF16), pltpu.VMEM((2 * TQ, CW), BF16),
                        pltpu.VMEM((TQ + 16, CW), F32), pltpu.SemaphoreType.DMA((1,))],
        compiler_params=_cp(("arbitrary",)), name="fwd_mix",
    )(x, proj, proj, proj, proj, proj, proj, proj, bias2, wconv_t, g_co, g_ao, g_pm, gm, wout_all)


def fwd_ffn(xmid, g_pre, g_post, wfi_all, wfo_all):
    t = xmid.shape[0]
    hw = DFF // 2

    def body(x_ref, gpre_ref, gpost_ref, wfi_hbm, wfo_hbm, gu_ref, f_ref, xo_ref, wfi_v, wfo_v, sems):
        def step(ready):
            xv = x_ref[...]
            h = _rms(xv, gpre_ref[...]).astype(BF16)
            f = jnp.zeros((TM, D), F32)
            for j in range(2):
                ready(3 * j)
                gate = jnp.dot(h, wfi_v[j], preferred_element_type=F32)
                ready(3 * j + 1)
                up = jnp.dot(h, wfi_v[2 + j], preferred_element_type=F32)
                gu_ref[:, hw * j:hw * (j + 1)] = gate.astype(BF16)
                gu_ref[:, DFF + hw * j:DFF + hw * (j + 1)] = up.astype(BF16)
                act = gate * (1.0 / (1.0 + jnp.exp(-gate))) * up
                ready(3 * j + 2)
                f = f + jnp.dot(act.astype(BF16), wfo_v[j], preferred_element_type=F32)
            f_ref[...] = f
            xo_ref[...] = xv + _rms(f, gpost_ref[...])

        _stream_weights_once([(src.at[k], dst.at[k]) for j in range(2) for src, dst, k in
                              ((wfi_hbm, wfi_v, j), (wfi_hbm, wfi_v, 2 + j), (wfo_hbm, wfo_v, j))], sems, step)

    row = lambda w: pl.BlockSpec((TM, w), lambda i: (i, 0))
    return pl.pallas_call(
        body, grid=(t // TM,),
        in_specs=[row(D), _const((1, D)), _const((1, D)), _any(), _any()],
        out_specs=[row(2 * DFF), row(D), row(D)],
        out_shape=[jax.ShapeDtypeStruct((t, 2 * DFF), BF16), jax.ShapeDtypeStruct((t, D), F32),
                   jax.ShapeDtypeStruct((t, D), F32)],
        scratch_shapes=[pltpu.VMEM((NCHIP, D, hw), BF16), pltpu.VMEM((2, hw, D), BF16), pltpu.SemaphoreType.DMA((6,))],
        compiler_params=_cp(("arbitrary",)), name="fwd_ffn")(xmid, g_pre, g_post, wfi_all, wfo_all)


def loss_head(y, target):
    t = y.shape[0]

    def body(y_ref, t_ref, dy_ref, l_ref):
        @pl.when(pl.program_id(0) == 0)
        def _():
            l_ref[...] = jnp.zeros_like(l_ref)

        e = y_ref[...] - t_ref[...]
        dy_ref[...] = e * (1.0 / D)
        rows = jnp.sum(e * e, axis=-1, keepdims=True) * (1.0 / D)
        l_ref[...] += 0.5 * jnp.sum(rows, axis=0, keepdims=True)

    row = pl.BlockSpec((TQ, D), lambda i: (i, 0))
    return pl.pallas_call(
        body, grid=(t // TQ,), in_specs=[row, row], out_specs=[row, _const((8, LANES))],
        out_shape=[jax.ShapeDtypeStruct((t, D), F32), jax.ShapeDtypeStruct((8, LANES), F32)],
        compiler_params=_cp(("arbitrary",)), name="loss_head")(y, target)


def bwd_ffn(dx, f, xmid, gu, g_pre, g_post, wfi_all, wfo_all):
    t = dx.shape[0]
    hw = DFF // 2

    def body(dx_ref, f_ref, x_ref, gu_ref, gpre_ref, gpost_ref, wfi_hbm, wfo_hbm,
             dxm_ref, df_ref, act_ref, dgu_ref, h_ref, dgpost_ref, dgpre_ref, wfi_v, wfo_v, sems):
        @pl.when(pl.program_id(0) == 0)
        def _():
            dgpost_ref[...] = jnp.zeros_like(dgpost_ref)
            dgpre_ref[...] = jnp.zeros_like(dgpre_ref)

        def step(ready):
            dxo = dx_ref[...]
            df, dgp = _rms_bwd(dxo, f_ref[...], gpost_ref[...])
            dgpost_ref[...] += dgp
            dfb = df.astype(BF16)
            df_ref[...] = dfb
            dh = jnp.zeros((TM, D), F32)
            for j in range(2):
                ready(3 * j)
                dact = lax.dot_general(dfb, wfo_v[j], NT, preferred_element_type=F32)
                gate = gu_ref[:, hw * j:hw * (j + 1)].astype(F32)
                up = gu_ref[:, DFF + hw * j:DFF + hw * (j + 1)].astype(F32)
                sig = 1.0 / (1.0 + jnp.exp(-gate))
                silu = gate * sig
                act_ref[:, hw * j:hw * (j + 1)] = (silu * up).astype(BF16)
                dup = (dact * silu).astype(BF16)
                dgate = (dact * up * (sig * (1.0 + gate * (1.0 - sig)))).astype(BF16)
                dgu_ref[:, hw * j:hw * (j + 1)] = dgate
                dgu_ref[:, DFF + hw * j:DFF + hw * (j + 1)] = dup
                ready(3 * j + 1)
                dh = dh + lax.dot_general(dgate, wfi_v[j], NT, preferred_element_type=F32)
                ready(3 * j + 2)
                dh = dh + lax.dot_general(dup, wfi_v[2 + j], NT, preferred_element_type=F32)
            xv = x_ref[...]
            gpre = gpre_ref[...]
            h_ref[...] = _rms(xv, gpre).astype(BF16)
            dxv, dgq = _rms_bwd(dh, xv, gpre)
            dgpre_ref[...] += dgq
            dxm_ref[...] = dxo + dxv

        _stream_weights_once([(src.at[k], dst.at[k]) for j in range(2) for src, dst, k in
                              ((wfo_hbm, wfo_v, j), (wfi_hbm, wfi_v, j), (wfi_hbm, wfi_v, 2 + j))], sems, step)

    row = lambda w: pl.BlockSpec((TM, w), lambda i: (i, 0))
    return pl.pallas_call(
        body, grid=(t // TM,),
        in_specs=[row(D), row(D), row(D), row(2 * DFF), _const((1, D)), _const((1, D)), _any(), _any()],
        out_specs=[row(D), row(D), row(DFF), row(2 * DFF), row(D), _const((1, D)), _const((1, D))],
        out_shape=[jax.ShapeDtypeStruct((t, D), F32), jax.ShapeDtypeStruct((t, D), BF16),
                   jax.ShapeDtypeStruct((t, DFF), BF16), jax.ShapeDtypeStruct((t, 2 * DFF), BF16),
                   jax.ShapeDtypeStruct((t, D), BF16), jax.ShapeDtypeStruct((1, D), F32),
                   jax.ShapeDtypeStruct((1, D), F32)],
        scratch_shapes=[pltpu.VMEM((NCHIP, D, hw), BF16), pltpu.VMEM((2, hw, D), BF16), pltpu.SemaphoreType.DMA((6,))],
        compiler_params=_cp(("arbitrary",)), name="bwd_ffn")(dx, f, xmid, gu, g_pre, g_post, wfi_all, wfo_all)


def bwd_mix(dxm, z, o, proj, wconv_t, g_co, g_ao, g_pm, gm, wout_all):
    t = dxm.shape[0]

    def body(dx_ref, z_ref, o_ref, pc_ref, pcp_ref, wc_ref, gco_ref, gao_ref, gpm_ref, gm_ref, wout_hbm,
             dz_ref, do_ref, dco_ref, dbg_ref, dgpm_ref, dgco_ref, dgao_ref, wout_v, cscr):
        first = pl.program_id(0) == 0

        @pl.when(first)
        def _():
            pltpu.sync_copy(wout_hbm, wout_v)
            dgpm_ref[...] = jnp.zeros_like(dgpm_ref)
            dgco_ref[...] = jnp.zeros_like(dgco_ref)
            dgao_ref[...] = jnp.zeros_like(dgao_ref)

        dz, dgp = _rms_bwd(dx_ref[...], z_ref[...], gpm_ref[...])
        dgpm_ref[...] += dgp
        dzb = dz.astype(BF16)
        dz_ref[...] = dzb
        gmv = gm_ref[...]
        _, bg, _, _, _, _, cout = _conv_fwd(pc_ref, pcp_ref, wc_ref, cscr, first)
        dy_conv = lax.dot_general(dzb, wout_v[0:CW, :], NT, preferred_element_type=F32)
        dyc, dgc = _group_rms_bwd(dy_conv, bg * cout, gco_ref[...], gmv)
        dgco_ref[...] += dgc
        dbg_ref[...] = (dyc * cout).astype(BF16)
        dco_ref[...] = dyc * bg
        dy_attn = lax.dot_general(dzb, wout_v[CW:2 * CW, :], NT, preferred_element_type=F32)
        do, dga = _group_rms_bwd(dy_attn, o_ref[...], gao_ref[...], gmv)
        dgao_ref[...] += dga
        do_ref[...] = do.astype(BF16)

    row = lambda w: pl.BlockSpec((TQ, w), lambda i: (i, 0))
    return pl.pallas_call(
        body, grid=(t // TQ,),
        in_specs=[row(D), row(D), row(CW)] + _conv_specs() + [
            _const((8, CW)), _const((1, CW)), _const((1, CW)), _const((1, D)), _const((CW, CW)), _any()],
        out_specs=[row(D), row(CW), row(CW), row(CW), _const((1, D)), _const((1, CW)), _const((1, CW))],
        out_shape=[jax.ShapeDtypeStruct((t, D), BF16), jax.ShapeDtypeStruct((t, CW), BF16),
                   jax.ShapeDtypeStruct((t, CW), F32), jax.ShapeDtypeStruct((t, CW), BF16),
                   jax.ShapeDtypeStruct((1, D), F32), jax.ShapeDtypeStruct((1, CW), F32),
                   jax.ShapeDtypeStruct((1, CW), F32)],
        scratch_shapes=[pltpu.VMEM((D, D), BF16), pltpu.VMEM((TQ + 16, CW), F32)],
        compiler_params=_cp(("arbitrary",)), name="bwd_mix",
    )(dxm, z, o, proj, proj, wconv_t, g_co, g_ao, g_pm, gm, wout_all)


def bwd_conv(dco, proj, wconv_t):
    t = dco.shape[0]
    nt = t // TQ

    def body(d_ref, dn_ref, pc_ref, pcp_ref, wc_ref, dhc_ref, dcg_ref, dw_ref, cscr, dscr):
        i = pl.program_id(0)
        first = i == 0

        @pl.when(first)
        def _():
            dw_ref[...] = jnp.zeros_like(dw_ref)

        hc, _, cg, u, u1, u2, _ = _conv_fwd(pc_ref, pcp_ref, wc_ref, cscr, first)
        d0 = d_ref[...]
        dscr[0:TQ, :] = d0
        dscr[TQ:TQ + 8, :] = jnp.where(i == nt - 1, 0.0, dn_ref[...])
        d1 = dscr[1:TQ + 1, :]
        d2 = dscr[2:TQ + 2, :]
        du = wc_ref[2:3, :] * d0 + wc_ref[1:2, :] * d1 + wc_ref[0:1, :] * d2
        dhc_ref[...] = (du * cg).astype(BF16)
        dcg_ref[...] = (du * hc).astype(BF16)
        dw_ref[0:1, :] += jnp.sum(d0 * u2, axis=0, keepdims=True)
        dw_ref[1:2, :] += jnp.sum(d0 * u1, axis=0, keepdims=True)
        dw_ref[2:3, :] += jnp.sum(d0 * u, axis=0, keepdims=True)

    row = lambda w: pl.BlockSpec((TQ, w), lambda i: (i, 0))
    nxt = pl.BlockSpec((8, CW), lambda i: (jnp.minimum((i + 1) * (TQ // 8), t // 8 - 1), 0))
    return pl.pallas_call(
        body, grid=(nt,),
        in_specs=[row(CW), nxt] + _conv_specs() + [_const((8, CW))],
        out_specs=[row(CW), row(CW), _const((8, CW))],
        out_shape=[jax.ShapeDtypeStruct((t, CW), BF16), jax.ShapeDtypeStruct((t, CW), BF16),
                   jax.ShapeDtypeStruct((8, CW), F32)],
        scratch_shapes=[pltpu.VMEM((TQ + 16, CW), F32), pltpu.VMEM((TQ + 8, CW), F32)],
        compiler_params=_cp(("arbitrary",)), name="bwd_conv")(dco, dco, proj, proj, wconv_t)


def bwd_attn(proj, o, do, lse, bias2):
    t = o.shape[0]
    nt = t // TQ
    qg, kg = QG_BWD, QG_BWD + LEFT

    def body(q_ref, kp_ref, kc_ref, vp_ref, vc_ref, o_ref, do_ref, lse_ref, b2_ref,
             dq_ref, dk_hbm, dv_hbm, db_hbm, kwin, vwin, dk_acc, dv_acc, db_acc):
        i = pl.program_id(0)
        first = i == 0

        @pl.when(first)
        def _():
            dk_acc[...] = jnp.zeros_like(dk_acc)
            dv_acc[...] = jnp.zeros_like(dv_acc)
            db_acc[...] = jnp.zeros_like(db_acc)

        kwin[0:TQ, :] = kp_ref[...]
        kwin[TQ:2 * TQ, :] = kc_ref[...]
        vwin[0:TQ, :] = vp_ref[...]
        vwin[TQ:2 * TQ, :] = vc_ref[...]
        scale = HD ** -0.5
        qmask = _head_masks(scale)
        vmask = _head_masks(1.0)
        low = lax.broadcasted_iota(jnp.int32, (1, LANES), 1) < HD

        def group(g, carry):
            r0 = pl.multiple_of(g * qg, qg)
            base = pl.multiple_of(i * TQ + r0, qg)
            pen = _key_penalty(first, r0, kg)
            for hp in range(NH // 2):
                ls = slice(LANES * hp, LANES * (hp + 1))
                qb = q_ref[pl.ds(r0, qg), ls]
                kw = kwin[pl.ds(r0, kg), ls]
                dob = do_ref[pl.ds(r0, qg), ls]
                prod = dob.astype(F32) * o_ref[pl.ds(r0, qg), ls]
                lseb = lse_ref[pl.ds(r0, qg), ls]
                q2 = jnp.concatenate([qb * qmask[0], qb * qmask[1]], axis=0)
                do2 = jnp.concatenate([dob * vmask[0], dob * vmask[1]], axis=0)
                lse2 = jnp.concatenate([lseb[:, 0:1], lseb[:, HD:HD + 1]], axis=0)
                dsum = jnp.concatenate([jnp.sum(jnp.where(low, prod, 0.0), axis=-1, keepdims=True),
                                        jnp.sum(jnp.where(low, 0.0, prod), axis=-1, keepdims=True)], axis=0)
                s = lax.dot_general(q2, kw, NT, preferred_element_type=F32) + b2_ref[hp] + pen
                p = jnp.exp(s - lse2)
                dp = lax.dot_general(do2, vwin[pl.ds(r0, kg), ls], NT, preferred_element_type=F32)
                ds = p * (dp - dsum)
                db_acc[hp] += ds
                dsb = ds.astype(BF16)
                dq2 = jnp.dot(dsb, kw, preferred_element_type=F32)
                dq_ref[pl.ds(r0, qg), ls] = (jnp.where(low, dq2[:qg], dq2[qg:]) * scale).astype(BF16)
                dk_acc[pl.ds(base, kg), ls] += lax.dot_general(dsb, q2, TN, preferred_element_type=F32)
                dv_acc[pl.ds(base, kg), ls] += lax.dot_general(p.astype(BF16), do2, TN, preferred_element_type=F32)
            return carry

        lax.fori_loop(0, TQ // qg, group, 0)

        @pl.when(i == nt - 1)
        def _():
            pltpu.sync_copy(dk_acc, dk_hbm)
            pltpu.sync_copy(dv_acc, dv_hbm)
            pltpu.sync_copy(db_acc, db_hbm)

    row = lambda w: pl.BlockSpec((TQ, w), lambda i: (i, 0))
    return pl.pallas_call(
        body, grid=(nt,),
        in_specs=_attn_window_specs() + [row(CW), row(CW), row(CW), _const((NH // 2, 2 * qg, kg))],
        out_specs=[row(CW), _any(), _any(), _any()],
        out_shape=[jax.ShapeDtypeStruct((t, CW), BF16), jax.ShapeDtypeStruct((t + TQ, CW), F32),
                   jax.ShapeDtypeStruct((t + TQ, CW), F32), jax.ShapeDtypeStruct((NH // 2, 2 * qg, kg), F32)],
        scratch_shapes=[pltpu.VMEM((2 * TQ, CW), BF16), pltpu.VMEM((2 * TQ, CW), BF16),
                        pltpu.VMEM((t + TQ, CW), F32), pltpu.VMEM((t + TQ, CW), F32),
                        pltpu.VMEM((NH // 2, 2 * qg, kg), F32)],
        compiler_params=_cp(("arbitrary",)), name="bwd_attn",
    )(proj, proj, proj, proj, proj, o, do, lse, bias2)


def bwd_inproj(dxm, x, dhc, dbg, dcg, dq, dk, dv, g, w_all):
    t = x.shape[0]
    wc = PROJ // NCHIP

    def body(dxm_ref, x_ref, dhc_ref, dbg_ref, dcg_ref, dq_ref, dk_ref, dv_ref, g_ref, w_hbm,
             dx_ref, dp_ref, h_ref, dg_ref, w_v, sems):
        @pl.when(pl.program_id(0) == 0)
        def _():
            dg_ref[...] = jnp.zeros_like(dg_ref)

        def step(ready):
            dp_ref[:, 0:CW] = dhc_ref[...]
            dp_ref[:, CW:2 * CW] = dbg_ref[...]
            dp_ref[:, 2 * CW:3 * CW] = dcg_ref[...]
            dp_ref[:, 3 * CW:4 * CW] = dq_ref[...]
            dp_ref[:, 4 * CW:5 * CW] = dk_ref[...].astype(BF16)
            dp_ref[:, 5 * CW:6 * CW] = dv_ref[...].astype(BF16)
            dh = jnp.zeros((TQ, D), F32)
            for b in range(NCHIP):
                ready(b)
                dh = dh + lax.dot_general(dp_ref[:, wc * b:wc * (b + 1)], w_v[b], NT, preferred_element_type=F32)
            xv = x_ref[...]
            gv = g_ref[...]
            h_ref[...] = _rms(xv, gv).astype(BF16)
            dxv, dgv = _rms_bwd(dh, xv, gv)
            dg_ref[...] += dgv
            dx_ref[...] = dxm_ref[...] + dxv

        _stream_weights_once([(w_hbm.at[b], w_v.at[b]) for b in range(NCHIP)], sems, step)

    row = lambda w: pl.BlockSpec((TQ, w), lambda i: (i, 0))
    pad = pl.BlockSpec((TQ, CW), lambda i: (i + 1, 0))
    return pl.pallas_call(
        body, grid=(t // TQ,),
        in_specs=[row(D), row(D), row(CW), row(CW), row(CW), row(CW), pad, pad, _const((1, D)), _any()],
        out_specs=[row(D), row(PROJ), row(D), _const((1, D))],
        out_shape=[jax.ShapeDtypeStruct((t, D), F32), jax.ShapeDtypeStruct((t, PROJ), BF16),
                   jax.ShapeDtypeStruct((t, D), BF16), jax.ShapeDtypeStruct((1, D), F32)],
        scratch_shapes=[pltpu.VMEM((NCHIP, D, wc), BF16), pltpu.SemaphoreType.DMA((NCHIP,))],
        compiler_params=_cp(("arbitrary",)), name="bwd_inproj",
    )(dxm, x, dhc, dbg, dcg, dq, dk, dv, g, w_all)


def wgrad(a, b, kb, nb, by_columns, name):
    t, k = a.shape
    n = b.shape[1]
    tk = 512

    def body(a_ref, b_ref, o_ref):
        o_ref[...] = jnp.zeros_like(o_ref)
        for c in range(t // tk):
            o_ref[...] += lax.dot_general(a_ref[tk * c:tk * (c + 1), :], b_ref[tk * c:tk * (c + 1), :], TN,
                                          preferred_element_type=F32)

    if by_columns:
        assert nb == n // NCHIP
        out_spec = pl.BlockSpec((None, kb, nb), lambda ki, ni: (ni, ki, 0))
        out_shape = jax.ShapeDtypeStruct((NCHIP, k, nb), F32)
    else:
        assert nb == n
        out_spec = pl.BlockSpec((kb, nb), lambda ki, ni: (ki, 0))
        out_shape = jax.ShapeDtypeStruct((k, n), F32)
    return pl.pallas_call(
        body, grid=(k // kb, n // nb),
        in_specs=[pl.BlockSpec((t, kb), lambda ki, ni: (0, ki)), pl.BlockSpec((t, nb), lambda ki, ni: (0, ni))],
        out_specs=out_spec, out_shape=out_shape,
        compiler_params=_cp(("arbitrary", "arbitrary")), name=name)(a, b)


TOE = 1024
assert 2 * QG_FWD + LEFT <= TOE
N_FLAT = LEFT - REL_CLIP + 1
N_VAR = BAND - N_FLAT


def _diag_vector(table):
    last = table[:, 2 * REL_CLIP:]
    var = table[:, 2 * REL_CLIP - N_VAR:2 * REL_CLIP][:, ::-1]
    return jnp.concatenate([jnp.broadcast_to(last, (NH, N_FLAT)), var, jnp.broadcast_to(last, (NH, TOE - BAND))], axis=1)


def _diag_vector_bwd(dvec):
    dlast = jnp.sum(dvec[:, :N_FLAT], axis=1, keepdims=True) + jnp.sum(dvec[:, BAND:], axis=1, keepdims=True)
    dvar = dvec[:, N_FLAT:BAND][:, ::-1]
    return jnp.concatenate([jnp.zeros((NH, 2 * REL_CLIP - N_VAR), F32), dvar, dlast], axis=1)


def _band_valid(qg):
    r = lax.broadcasted_iota(jnp.int32, (qg, qg + LEFT), 0)
    p = lax.broadcasted_iota(jnp.int32, (qg, qg + LEFT), 1)
    start = lax.shift_left(lax.shift_right_logical(r, 6), 6)
    return (p >= start) & (p < start + BAND)


def bias_expand(vec, qgs):
    def body(v_ref, *o_refs):
        for qg, o_ref in zip(qgs, o_refs):
            valid = _band_valid(qg)
            for h in range(NH):
                rows = jnp.broadcast_to(v_ref[h:h + 1, :], (qg, TOE))
                toe = pltpu.roll(rows, 0, 1, stride=1, stride_axis=0)
                o_ref[h // 2, qg * (h % 2):qg * (h % 2 + 1), :] = jnp.where(valid, toe[:, :qg + LEFT], NEG_INF)

    return pl.pallas_call(body, out_shape=[jax.ShapeDtypeStruct((NH // 2, 2 * qg, qg + LEFT), F32) for qg in qgs],
                          name="bias_expand")(vec)


def bias_reduce(db2):
    _, qg, kg = db2.shape

    def body(d_ref, o_ref):
        ii = lax.broadcasted_iota(jnp.int32, (kg, kg), 0)
        jj = lax.broadcasted_iota(jnp.int32, (kg, kg), 1)
        flip = jnp.where(ii + jj == kg - 1, 1.0, 0.0).astype(BF16)
        for h in range(NH):
            rest = d_ref[h]
            rev = jnp.zeros((qg, kg), F32)
            for _ in range(3):
                term = rest.astype(BF16)
                rev = rev + jnp.dot(term, flip, preferred_element_type=F32)
                rest = rest - term.astype(F32)
            d = jnp.concatenate([jnp.zeros((qg, TOE - kg), F32), rev], axis=1)
            back = pltpu.roll(d, 0, 1, stride=1, stride_axis=0)
            o_ref[h:h + 1, :] = jnp.sum(back, axis=0, keepdims=True)

    rev = pl.pallas_call(body, out_shape=jax.ShapeDtypeStruct((NH, TOE), F32), name="bias_reduce")(db2)
    return rev[:, ::-1]


def _place():
    x, y, c = lax.axis_index("x"), lax.axis_index("y"), lax.axis_index("c")
    chips = [(1 - x, y), (x, 1 - y), (1 - x, 1 - y)]
    return x, y, c, chips


def _half(ref_rows, c):
    return pl.ds(c * (ref_rows // 2), ref_rows // 2)


HBM_SPEC = pl.BlockSpec(memory_space=pltpu.HBM)
SEM_SPEC = pl.BlockSpec(memory_space=pltpu.SEMAPHORE)
IN_FLIGHT = pltpu.CompilerParams(has_side_effects=pltpu.SideEffectType.DATAFLOW_SIDE_EFFECTING)


def _in_hbm(a):
    return pltpu.with_memory_space_constraint(a, pltpu.HBM)


def cast_to_slot(ws, chip, layer):
    n = len(ws)
    steps = 4

    def body(b_ref, *refs):
        del b_ref
        for w_ref, o_ref in zip(refs[:n], refs[n:]):
            o_ref[...] = w_ref[...].astype(BF16)

    grid_spec = pltpu.PrefetchScalarGridSpec(
        num_scalar_prefetch=1, grid=(steps,),
        in_specs=[pl.BlockSpec((None, w.shape[1] // steps, w.shape[2]), lambda r, b: (layer, r, 0)) for w in ws],
        out_specs=[pl.BlockSpec((None, w.shape[1] // steps, w.shape[2]), lambda r, b: (b[0], r, 0)) for w in ws])
    return pl.pallas_call(body, grid_spec=grid_spec,
                          out_shape=[jax.ShapeDtypeStruct((NCHIP,) + w.shape[1:], BF16) for w in ws],
                          compiler_params=_cp(("arbitrary",)), name="cast_to_slot")(chip, *ws)


def _gather_copies(bufs, send, recv):
    x, y, c, chips = _place()
    b = 2 * x + y
    out = []
    for k, buf in enumerate(bufs):
        rows = buf.shape[1]
        mine = buf.at[b, _half(rows, c), :]
        for j, (cx, cy) in enumerate(chips):
            theirs = buf.at[2 * cx + cy, _half(rows, c), :]
            sems = dict(send_sem=send.at[3 * k + j], recv_sem=recv.at[3 * k + j],
                        device_id=(cx, cy, c), device_id_type=MESH)
            out.append((pltpu.make_async_remote_copy(src_ref=mine, dst_ref=mine, **sems),
                        pltpu.make_async_remote_copy(src_ref=theirs, dst_ref=theirs, **sems)))
    return out


def gather_start(bufs, after, layer):
    n = len(bufs)

    def body(*refs):
        ins = refs[:n]
        send, recv = refs[n + 1], refs[n + 2]
        token = refs[-1]
        for start, _ in _gather_copies(ins, send, recv):
            start.start()
        token[...] = jnp.zeros_like(token)

    sems = pltpu.SemaphoreType.DMA((3 * n,))
    res = pl.pallas_call(
        body, name=f"gather_start_{layer}",
        in_specs=[HBM_SPEC] * n + [_any()],
        out_specs=[SEM_SPEC, SEM_SPEC] + [HBM_SPEC] * n + [pl.BlockSpec(memory_space=pltpu.VMEM)],
        out_shape=[sems, sems] + [pltpu.HBM(b.shape, b.dtype) for b in bufs] + [jax.ShapeDtypeStruct((8, LANES), F32)],
        input_output_aliases={k: 2 + k for k in range(n)}, compiler_params=IN_FLIGHT,
    )(*[_in_hbm(b) for b in bufs], after)
    return res[0], res[1], res[2:2 + n], res[-1]


def gather_wait(send, recv, bufs, after, layer):
    n = len(bufs)

    def body(*refs):
        ins = refs[:n]
        send_ref, recv_ref = refs[n], refs[n + 1]
        for start, arrival in _gather_copies(ins, send_ref, recv_ref):
            start.wait_send()
            arrival.wait_recv()

    return pl.pallas_call(
        body, name=f"gather_wait_{layer}",
        in_specs=[HBM_SPEC] * n + [SEM_SPEC, SEM_SPEC, _any()], out_specs=[HBM_SPEC] * n,
        out_shape=[pltpu.HBM(b.shape, b.dtype) for b in bufs],
        input_output_aliases={k: k for k in range(n)}, compiler_params=IN_FLIGHT,
    )(*bufs, send, recv, after)


def gather_forward(bufs):
    n = len(bufs)

    def body(*refs):
        outs = refs[n:2 * n]
        send, recv = refs[2 * n:]
        x, y, c, chips = _place()
        cps = []
        for k in range(n):
            rows = outs[k].shape[1]
            for j, (cx, cy) in enumerate(chips):
                sems = dict(send_sem=send.at[3 * k + j], recv_sem=recv.at[3 * k + j],
                            device_id=(x, y, 1 - c), device_id_type=MESH)
                mine = outs[k].at[2 * cx + cy, _half(rows, c), :]
                theirs = outs[k].at[2 * cx + cy, _half(rows, 1 - c), :]
                cp = pltpu.make_async_remote_copy(src_ref=mine, dst_ref=mine, **sems)
                cp.start()
                cps.append((cp, pltpu.make_async_remote_copy(src_ref=theirs, dst_ref=theirs, **sems)))
        for cp, arrival in cps:
            cp.wait_send()
            arrival.wait_recv()

    return pl.pallas_call(
        body, in_specs=[_any()] * n, out_specs=[_any()] * n,
        out_shape=[jax.ShapeDtypeStruct(b.shape, b.dtype) for b in bufs], input_output_aliases={k: k for k in range(n)},
        scratch_shapes=[pltpu.SemaphoreType.DMA((3 * n,)), pltpu.SemaphoreType.DMA((3 * n,))],
        name="gather_forward")(*bufs)


def _forward_copies(bufs, send, recv):
    x, y, c, chips = _place()
    out = []
    for k, buf in enumerate(bufs):
        rows = buf.shape[1]
        for j, (cx, cy) in enumerate(chips):
            sems = dict(send_sem=send.at[3 * k + j], recv_sem=recv.at[3 * k + j],
                        device_id=(x, y, 1 - c), device_id_type=MESH)
            mine = buf.at[2 * cx + cy, _half(rows, c), :]
            theirs = buf.at[2 * cx + cy, _half(rows, 1 - c), :]
            out.append((pltpu.make_async_remote_copy(src_ref=mine, dst_ref=mine, **sems),
                        pltpu.make_async_remote_copy(src_ref=theirs, dst_ref=theirs, **sems)))
    return out


def forward_start(bufs, tag):
    n = len(bufs)

    def body(*refs):
        ins = refs[:n]
        send, recv = refs[n], refs[n + 1]
        token = refs[-1]
        for start, _ in _forward_copies(ins, send, recv):
            start.start()
        token[...] = jnp.zeros_like(token)

    sems = pltpu.SemaphoreType.DMA((3 * n,))
    res = pl.pallas_call(
        body, name=f"forward_start_{tag}", in_specs=[HBM_SPEC] * n,
        out_specs=[SEM_SPEC, SEM_SPEC] + [HBM_SPEC] * n + [pl.BlockSpec(memory_space=pltpu.VMEM)],
        out_shape=[sems, sems] + [pltpu.HBM(b.shape, b.dtype) for b in bufs] + [jax.ShapeDtypeStruct((8, LANES), F32)],
        input_output_aliases={k: 2 + k for k in range(n)}, compiler_params=IN_FLIGHT,
    )(*[_in_hbm(b) for b in bufs])
    return res[0], res[1], res[2:2 + n], res[-1]


def forward_wait(send, recv, bufs, after, tag):
    n = len(bufs)

    def body(*refs):
        ins = refs[:n]
        send_ref, recv_ref = refs[n], refs[n + 1]
        for start, arrival in _forward_copies(ins, send_ref, recv_ref):
            start.wait_send()
            arrival.wait_recv()

    return pl.pallas_call(
        body, name=f"forward_wait_{tag}",
        in_specs=[HBM_SPEC] * n + [SEM_SPEC, SEM_SPEC, _any()], out_specs=[HBM_SPEC] * n,
        out_shape=[pltpu.HBM(b.shape, b.dtype) for b in bufs],
        input_output_aliases={k: k for k in range(n)}, compiler_params=IN_FLIGHT,
    )(*bufs, send, recv, after)


def _exchange_copies(srcs, lands, send, recv):
    x, y, c, _ = _place()
    return [pltpu.make_async_remote_copy(
        src_ref=src.at[:, _half(src.shape[1], 1 - c), :], dst_ref=land, send_sem=send.at[k], recv_sem=recv.at[k],
        device_id=(x, y, 1 - c), device_id_type=MESH) for k, (src, land) in enumerate(zip(srcs, lands))]


def exchange_start(srcs, tag):
    n = len(srcs)
    lands = [lax.empty((s.shape[0], s.shape[1] // 2, s.shape[2]), s.dtype) for s in srcs]

    def body(*refs):
        ins, land_refs = refs[:n], refs[n:2 * n]
        send, recv = refs[2 * n], refs[2 * n + 1]
        token = refs[-1]
        for cp in _exchange_copies(ins, land_refs, send, recv):
            cp.start()
        token[...] = jnp.zeros_like(token)

    sems = pltpu.SemaphoreType.DMA((n,))
    res = pl.pallas_call(
        body, name=f"exchange_start_{tag}",
        in_specs=[HBM_SPEC] * (2 * n),
        out_specs=[SEM_SPEC, SEM_SPEC] + [HBM_SPEC] * (2 * n) + [pl.BlockSpec(memory_space=pltpu.VMEM)],
        out_shape=[sems, sems] + [pltpu.HBM(a.shape, a.dtype) for a in list(srcs) + lands]
        + [jax.ShapeDtypeStruct((8, LANES), F32)],
        input_output_aliases={k: 2 + k for k in range(2 * n)}, compiler_params=IN_FLIGHT,
    )(*[_in_hbm(a) for a in list(srcs) + lands])
    return res[0], res[1], res[2:2 + n], res[2 + n:2 + 2 * n], res[-1]


def exchange_wait(send, recv, srcs, lands, after, tag):
    n = len(srcs)

    def body(*refs):
        ins, land_refs = refs[:n], refs[n:2 * n]
        send_ref, recv_ref = refs[2 * n], refs[2 * n + 1]
        for cp in _exchange_copies(ins, land_refs, send_ref, recv_ref):
            cp.wait_send()
            cp.wait_recv()

    res = pl.pallas_call(
        body, name=f"exchange_wait_{tag}",
        in_specs=[HBM_SPEC] * (2 * n) + [SEM_SPEC, SEM_SPEC, _any()], out_specs=[HBM_SPEC] * (2 * n),
        out_shape=[pltpu.HBM(a.shape, a.dtype) for a in list(srcs) + list(lands)],
        input_output_aliases={k: k for k in range(2 * n)}, compiler_params=IN_FLIGHT,
    )(*srcs, *lands, send, recv, after)
    return res[:n], res[n:]


def add_pair(gs, r1s, core):
    n = len(gs)

    def body(c_ref, *refs):
        del c_ref
        for g_ref, r_ref, o_ref in zip(refs[:n], refs[n:2 * n], refs[2 * n:]):
            o_ref[...] = (g_ref[...] + r_ref[...]).astype(BF16)

    blk = lambda r: (None,) + r.shape[1:]
    grid_spec = pltpu.PrefetchScalarGridSpec(
        num_scalar_prefetch=1, grid=(NCHIP,),
        in_specs=[pl.BlockSpec(blk(r), lambda s, c: (s, c[0], 0)) for r in r1s]
        + [pl.BlockSpec(blk(r), lambda s, c: (s, 0, 0)) for r in r1s],
        out_specs=[pl.BlockSpec(blk(r), lambda s, c: (s, 0, 0)) for r in r1s])
    return pl.pallas_call(body, grid_spec=grid_spec, out_shape=[jax.ShapeDtypeStruct(r.shape, BF16) for r in r1s],
                          compiler_params=_cp(("arbitrary",)), name="add_pair")(core, *gs, *r1s)


def _scatter_copies(srcs, lands, send, recv):
    _, _, c, chips = _place()
    out = []
    for k, (src, land) in enumerate(zip(srcs, lands)):
        for j, (cx, cy) in enumerate(chips):
            out.append(pltpu.make_async_remote_copy(
                src_ref=src.at[2 * cx + cy], dst_ref=land.at[j], send_sem=send.at[3 * k + j],
                recv_sem=recv.at[3 * k + j], device_id=(cx, cy, c), device_id_type=MESH))
    return out


def scatter_start(srcs, layer):
    n = len(srcs)
    srcs = list(srcs)
    lands = [lax.empty((3,) + s.shape[1:], s.dtype) for s in srcs]

    def body(*refs):
        ins, land_refs = refs[:n], refs[n:2 * n]
        send, recv = refs[2 * n], refs[2 * n + 1]
        token = refs[-1]
        for cp in _scatter_copies(ins, land_refs, send, recv):
            cp.start()
        token[...] = jnp.zeros_like(token)

    sems = pltpu.SemaphoreType.DMA((3 * n,))
    res = pl.pallas_call(
        body, name=f"scatter_start_{layer}",
        in_specs=[HBM_SPEC] * (2 * n),
        out_specs=[SEM_SPEC, SEM_SPEC] + [HBM_SPEC] * (2 * n) + [pl.BlockSpec(memory_space=pltpu.VMEM)],
        out_shape=[sems, sems] + [pltpu.HBM(a.shape, a.dtype) for a in srcs + lands]
        + [jax.ShapeDtypeStruct((8, LANES), F32)],
        input_output_aliases={k: 2 + k for k in range(2 * n)}, compiler_params=IN_FLIGHT,
    )(*[_in_hbm(a) for a in srcs + lands])
    return res[0], res[1], res[2:2 + n], res[2 + n:2 + 2 * n], res[-1]


def scatter_wait(send, recv, srcs, lands, after, layer):
    n = len(srcs)

    def body(*refs):
        ins, land_refs = refs[:n], refs[n:2 * n]
        send_ref, recv_ref = refs[2 * n], refs[2 * n + 1]
        for cp in _scatter_copies(ins, land_refs, send_ref, recv_ref):
            cp.wait_send()
            cp.wait_recv()

    res = pl.pallas_call(
        body, name=f"scatter_wait_{layer}",
        in_specs=[HBM_SPEC] * (2 * n) + [SEM_SPEC, SEM_SPEC, _any()], out_specs=[HBM_SPEC] * (2 * n),
        out_shape=[pltpu.HBM(a.shape, a.dtype) for a in list(srcs) + list(lands)],
        input_output_aliases={k: k for k in range(2 * n)}, compiler_params=IN_FLIGHT,
    )(*srcs, *lands, send, recv, after)
    return res[n:]


def add_chips(gs, r1s, r2s, place, totals, layer):
    n = len(gs)
    steps = 2

    def body(p_ref, *refs):
        del p_ref
        for g_ref, r1_ref, r2_ref, o_ref in zip(refs[:n], refs[n:2 * n], refs[2 * n:3 * n], refs[4 * n:]):
            own = g_ref[...] + r1_ref[...]
            o_ref[...] = ((own + r2_ref[0].astype(F32)) + r2_ref[1].astype(F32)) + r2_ref[2].astype(F32)

    blk = lambda r: (None, r.shape[1] // steps, r.shape[2])
    grid_spec = pltpu.PrefetchScalarGridSpec(
        num_scalar_prefetch=1, grid=(steps,),
        in_specs=[pl.BlockSpec(blk(r), lambda i, p: (p[1], p[0] * steps + i, 0)) for r in r1s]
        + [pl.BlockSpec(blk(r), lambda i, p: (p[1], i, 0)) for r in r1s]
        + [pl.BlockSpec((3,) + blk(r)[1:], lambda i, p: (0, i, 0)) for r in r1s] + [_any()] * n,
        out_specs=[pl.BlockSpec(blk(r), lambda i, p: (layer, p[0] * steps + i, 0)) for r in r1s])
    return pl.pallas_call(body, grid_spec=grid_spec, out_shape=[jax.ShapeDtypeStruct(t.shape, F32) for t in totals],
                          input_output_aliases={1 + 3 * n + k: k for k in range(n)},
                          compiler_params=_cp(("arbitrary",)), name="add_chips")(place, *gs, *r1s, *r2s, *totals)


def pair_share(gs, tag):
    n = len(gs)

    def body(*refs):
        outs = refs[n:2 * n]
        send, recv = refs[2 * n:]
        x, y, c, _ = _place()
        cps = []
        for k in range(n):
            mine = outs[k].at[:, _half(outs[k].shape[1], c), :]
            cp = pltpu.make_async_remote_copy(
                src_ref=mine, dst_ref=mine, send_sem=send.at[k], recv_sem=recv.at[k],
                device_id=(x, y, 1 - c), device_id_type=MESH)
            cp.start()
            cps.append(cp)
        for k, cp in enumerate(cps):
            cp.wait_send()
            theirs = outs[k].at[:, _half(outs[k].shape[1], 1 - c), :]
            pltpu.make_async_remote_copy(
                src_ref=theirs, dst_ref=theirs, send_sem=send.at[k], recv_sem=recv.at[k],
                device_id=(x, y, 1 - c), device_id_type=MESH).wait_recv()

    return pl.pallas_call(
        body, in_specs=[_any()] * n, out_specs=[_any()] * n,
        out_shape=[jax.ShapeDtypeStruct(g.shape, g.dtype) for g in gs], input_output_aliases={k: k for k in range(n)},
        scratch_shapes=[pltpu.SemaphoreType.DMA((n,)), pltpu.SemaphoreType.DMA((n,))],
        name=f"pair_share_{tag}")(*gs)


def small_collect(v, reduce, name):
    rows = v.shape[0]
    flips = [(fx, fy, fc) for fx in (0, 1) for fy in (0, 1) for fc in (0, 1)][1:]

    def body(v_ref, o_ref, buf, send, recv):
        x, y, c, _ = _place()
        buf[4 * x + 2 * y + c] = v_ref[...]
        peers = [(jnp.where(fx, 1 - x, x), jnp.where(fy, 1 - y, y), jnp.where(fc, 1 - c, c)) for fx, fy, fc in flips]
        cps = []
        for k, peer in enumerate(peers):
            cp = pltpu.make_async_remote_copy(
                src_ref=v_ref, dst_ref=buf.at[4 * x + 2 * y + c], send_sem=send.at[k], recv_sem=recv.at[k],
                device_id=peer, device_id_type=MESH)
            cp.start()
            cps.append(cp)
        for k, (px, py, pc) in enumerate(peers):
            pltpu.make_async_remote_copy(
                src_ref=v_ref, dst_ref=buf.at[4 * px + 2 * py + pc], send_sem=send.at[k], recv_sem=recv.at[k],
                device_id=(px, py, pc), device_id_type=MESH).wait_recv()
        for cp in cps:
            cp.wait_send()
        if reduce:
            acc = buf[0]
            for s in range(1, 8):
                acc = acc + buf[s]
            o_ref[...] = acc
        else:
            o_ref[...] = buf[...]

    vm = pl.BlockSpec(memory_space=pltpu.VMEM)
    out_shape = jax.ShapeDtypeStruct((rows, SMALL_COLS) if reduce else (8, rows, SMALL_COLS), F32)
    return pl.pallas_call(
        body, in_specs=[vm], out_specs=vm, out_shape=out_shape,
        scratch_shapes=[pltpu.VMEM((8, rows, SMALL_COLS), F32), pltpu.SemaphoreType.DMA((7,)),
                        pltpu.SemaphoreType.DMA((7,))],
        name=name)(v)


def adamw(w, g, m, v, rb, name):
    nl, rows, cols = w.shape

    def body(w_ref, g_ref, m_ref, v_ref, go_ref, d_ref, nm_ref, nv_ref):
        gv = g_ref[...]
        go_ref[...] = gv
        nm = ADAM_B1 * m_ref[...] + (1.0 - ADAM_B1) * gv
        nv = ADAM_B2 * v_ref[...] + (1.0 - ADAM_B2) * (gv * gv)
        m_hat = nm / (1.0 - ADAM_B1 ** ADAM_STEP)
        v_hat = nv / (1.0 - ADAM_B2 ** ADAM_STEP)
        d_ref[...] = -ADAM_LR * (m_hat / (jnp.sqrt(v_hat) + ADAM_EPS) + ADAM_WD * w_ref[...])
        nm_ref[...] = nm
        nv_ref[...] = nv

    blk = pl.BlockSpec((None, rb, cols), lambda l, r: (l, r, 0))
    shp = jax.ShapeDtypeStruct(w.shape, F32)
    return pl.pallas_call(body, grid=(nl, rows // rb), in_specs=[blk] * 4, out_specs=[blk] * 4, out_shape=[shp] * 4,
                          compiler_params=_cp(("arbitrary", "arbitrary")), name=name)(w, g, m, v)


def _pack(parts, rows):
    flat = jnp.concatenate([p.reshape(-1).astype(F32) for p in parts])
    return jnp.pad(flat, (0, rows * SMALL_COLS - flat.shape[0])).reshape(rows, SMALL_COLS)


def _unpack(vec, shapes):
    flat = vec.reshape(-1)
    out, off = [], 0
    for s in shapes:
        size = 1
        for d in s:
            size *= d
        out.append(flat[off:off + size].reshape(s))
        off += size
    return out


def kernel(x, w_in, w_conv, rel_bias, g_conv_out, g_attn_out, w_out, g_pre_mix, g_post_mix, g_pre_ffn, g_post_ffn, w_ffn_in, w_ffn_out, loss_target, m_w_in, m_w_conv, m_rel_bias, m_g_conv_out, m_g_attn_out, m_w_out, m_g_pre_mix, m_g_post_mix, m_g_pre_ffn, m_g_post_ffn, m_w_ffn_in, m_w_ffn_out, v_w_in, v_w_conv, v_rel_bias, v_g_conv_out, v_g_attn_out, v_w_out, v_g_pre_mix, v_g_post_mix, v_g_pre_ffn, v_g_post_ffn, v_w_ffn_in, v_w_ffn_out):
    xi, yi, ci = lax.axis_index("x"), lax.axis_index("y"), lax.axis_index("c")
    chip = 2 * xi + yi
    nl = w_in.shape[0]
    x0 = x[0]
    target = loss_target[0]
    cwl = CW // NCHIP

    chip1 = chip.reshape(1).astype(jnp.int32)
    own = [cast_to_slot([w_in, w_out, w_ffn_in, w_ffn_out], chip1, l) for l in range(nl)]
    wc_mine = jnp.pad(w_conv.reshape(-1), (0, 16 * LANES - w_conv.size)).reshape(1, 16, LANES)
    wc_slot = lax.dynamic_update_slice_in_dim(jnp.zeros((NCHIP, 16, LANES), F32), wc_mine, chip, axis=0)
    gm = jnp.kron(jnp.eye(CW // HD, dtype=F32), jnp.full((HD, HD), 1.0 / HD, F32)).astype(BF16)
    row = lambda a, l: a[l][None, :]

    def token(t):
        return t[0:1, 0:1]

    def gather_finish(flight, after, tag):
        send, recv, bufs, _ = flight
        return gather_forward(gather_wait(send, recv, bufs, after, tag))

    first_mix = gather_start(list(own[0][:2]) + [wc_slot], x0, "0m")
    first_ffn = gather_start(own[0][2:], first_mix[3], "0f")
    gw_in, gw_out, wc_all = gather_finish(first_mix, x0, "0m")
    wc_full = wc_all.reshape(NCHIP, -1)[:, :nl * cwl * 3].reshape(NCHIP, nl, cwl, 3)
    wc_full = jnp.transpose(wc_full, (1, 0, 2, 3)).reshape(nl, CW, 3)
    wconv_t = jnp.pad(jnp.transpose(wc_full, (0, 2, 1)), ((0, 0), (0, 5), (0, 0)))
    flight = to_sibling = None
    saved, weights = [], []
    h = x0
    for l in range(nl):
        if l == 0:
            pass
        elif l == 1:
            gw_in, gw_out, gw_fi, gw_fo = gather_finish(flight, h, l)
        else:
            gw_in, gw_out, gw_fi, gw_fo = forward_wait(*to_sibling[:3], h, l)
        gw_out = gw_out.reshape(D, D)
        g_pm, g_pf = row(g_pre_mix, l), row(g_pre_ffn, l)
        if l == 0:
            g_pm = g_pm + token(first_ffn[3])
        if l + 1 < nl:
            flight = gather_start(own[l + 1], first_ffn[3] if l == 0 else gw_in, l + 1)
            g_pm = g_pm + token(flight[3])
        bias2, bias2_bwd = bias_expand(_diag_vector(rel_bias[l]), (QG_FWD, QG_BWD))
        proj = fwd_inproj(h, g_pm, gw_in)
        xmid, o, lse, y, z = fwd_mix(h, proj, bias2, wconv_t[l], row(g_conv_out, l), row(g_attn_out, l),
                                     row(g_post_mix, l), gm, gw_out)
        if l == 0:
            gw_fi, gw_fo = gather_finish(first_ffn, xmid, "0f")
        elif l + 1 < nl:
            send, recv, bufs, _ = flight
            to_sibling = forward_start(gather_wait(send, recv, bufs, xmid, l + 1), l + 1)
            g_pf = g_pf + token(to_sibling[3])
        gw_fo = gw_fo.reshape(2, DFF // 2, D)
        gu, f, xout = fwd_ffn(xmid, g_pf, row(g_post_ffn, l), gw_fi, gw_fo)
        saved.append((h, proj, bias2_bwd, xmid, o, lse, y, z, gu, f))
        weights.append((gw_in, gw_out, gw_fi, gw_fo))
        h = xout
    dx, loss_blk = loss_head(h, target)

    core = ci.reshape(1).astype(jnp.int32)
    place = jnp.stack([ci, chip]).astype(jnp.int32)
    totals = [lax.empty(w.shape, F32) for w in (w_in, w_out, w_ffn_in, w_ffn_out)]
    small = {k: [None] * nl for k in ("co", "ao", "pm", "qm", "pf", "qf", "rel", "wc")}

    def reduce_begin(kinds, grads, tag):
        return kinds, exchange_start(grads, tag), tag

    def reduce_mid(state, after):
        kinds, (send, recv, srcs, lands, _), tag = state
        grads, from_sibling = exchange_wait(send, recv, srcs, lands, after, tag)
        return kinds, grads, from_sibling, scatter_start(add_pair(grads, from_sibling, core), tag), tag

    def reduce_end(state, after, totals, layer):
        kinds, grads, from_sibling, (send, recv, srcs, lands, _), tag = state
        from_chips = scatter_wait(send, recv, srcs, lands, after, tag)
        totals = list(totals)
        summed = add_chips(grads, from_sibling, from_chips, place, [totals[i] for i in kinds], layer)
        for i, t in zip(kinds, summed):
            totals[i] = t
        return totals

    begun = flying = None
    for l in reversed(range(nl)):
        hin, proj, bias2, xmid, o, lse, y, z, gu, f = saved[l]
        gw_in, gw_out, gw_fi, gw_fo = weights[l]
        g_qf, g_qm, wct = row(g_post_ffn, l), row(g_post_mix, l), wconv_t[l]
        if begun is not None:
            g_qf = g_qf + token(begun[1][4])
        dxm, dfb, act, dgu, h2, dg_qf, dg_pf = bwd_ffn(dx, f, xmid, gu, row(g_pre_ffn, l), g_qf, gw_fi, gw_fo)
        if begun is not None:
            flying = reduce_mid(begun, dxm)
            g_qm = g_qm + token(flying[3][4])
        gr_fo = wgrad(act, dfb, 256, D, False, "wgrad_ffn_out").reshape(NCHIP, DFF // NCHIP, D)
        gr_fi = wgrad(h2, dgu, 512, 2 * DFF // NCHIP, True, "wgrad_ffn_in")
        if l == 0:
            begun_ffn = reduce_begin([2, 3], [gr_fi, gr_fo], "0f")
            g_qm = g_qm + token(begun_ffn[1][4])
        dzb, do, dco, dbg, dg_qm, dg_co, dg_ao = bwd_mix(dxm, z, o, proj, wct, row(g_conv_out, l),
                                                          row(g_attn_out, l), g_qm, gm, gw_out)
        if l == 0:
            flying_ffn = reduce_mid(begun_ffn, dzb)
            wct = wct + token(flying_ffn[3][4])
        gr_out = wgrad(y, dzb, 512, D, False, "wgrad_out").reshape(NCHIP, D // NCHIP, D)
        dhc, dcg, dwc = bwd_conv(dco, proj, wct)
        dq, dk, dv, db2 = bwd_attn(proj, o, do, lse, bias2)
        dx, dproj, hb, dg_pm = bwd_inproj(dxm, hin, dhc, dbg, dcg, dq, dk, dv, row(g_pre_mix, l), gw_in)
        if flying is not None:
            totals = reduce_end(flying, dx, totals, l + 1)
        gr_in = wgrad(hb, dproj, 512, PROJ // NCHIP, True, "wgrad_in")
        small["co"][l], small["ao"][l], small["pm"][l], small["qm"][l] = dg_co, dg_ao, dg_pm, dg_qm
        small["pf"][l], small["qf"][l] = dg_pf, dg_qf
        small["rel"][l] = _diag_vector_bwd(bias_reduce(db2.reshape(NH, QG_BWD, QG_BWD + LEFT)))
        small["wc"][l] = jnp.transpose(dwc[0:3], (1, 0))
        if l > 0:
            begun = reduce_begin([0, 1, 2, 3], [gr_in, gr_out, gr_fi, gr_fo], l)
    flying_mix = reduce_mid(reduce_begin([0, 1], [gr_in, gr_out], "0m"), dx)
    totals = reduce_end(flying_ffn, flying_mix[3][4], totals, 0)
    gr_fi, gr_fo = pair_share(totals[2:], "ffn")
    big_fi = adamw(w_ffn_in, gr_fi, m_w_ffn_in, v_w_ffn_in, w_ffn_in.shape[1] // 4, "adamw_ffn_in")
    big_fo = adamw(w_ffn_out, gr_fo, m_w_ffn_out, v_w_ffn_out, w_ffn_out.shape[1] // 4, "adamw_ffn_out")
    totals = reduce_end(flying_mix, big_fo[1], totals, 0)
    gr_in, gr_out = pair_share(totals[:2], "mix")
    big_in = adamw(w_in, gr_in, m_w_in, v_w_in, w_in.shape[1] // 4, "adamw_in")
    big_out = adamw(w_out, gr_out, m_w_out, v_w_out, w_out.shape[1] // 4, "adamw_out")
    big = [big_in, big_out, big_fi, big_fo]

    order = ("co", "ao", "pm", "qm", "pf", "qf", "rel", "wc")
    parts = [jnp.stack(small[k]) for k in order] + [loss_blk[0:1, 0:1]]
    shapes = [p.shape for p in parts]
    red = _unpack(small_collect(_pack(parts, 40), True, "reduce_small"), shapes)
    gr_co, gr_ao, gr_pm, gr_qm, gr_pf, gr_qf, gr_rel, gr_wc_full, loss = red
    gr_co, gr_ao, gr_pm, gr_qm, gr_pf, gr_qf = [a.reshape(nl, -1) for a in (gr_co, gr_ao, gr_pm, gr_qm, gr_pf, gr_qf)]
    gr_wc = lax.dynamic_slice_in_dim(gr_wc_full, chip * cwl, cwl, axis=1)
    loss = loss.reshape(())

    sw = [g_conv_out, g_attn_out, g_pre_mix, g_post_mix, g_pre_ffn, g_post_ffn, rel_bias, w_conv]
    sg = [gr_co, gr_ao, gr_pm, gr_qm, gr_pf, gr_qf, gr_rel, gr_wc]
    sm = [m_g_conv_out, m_g_attn_out, m_g_pre_mix, m_g_post_mix, m_g_pre_ffn, m_g_post_ffn, m_rel_bias, m_w_conv]
    sv = [v_g_conv_out, v_g_attn_out, v_g_pre_mix, v_g_post_mix, v_g_pre_ffn, v_g_post_ffn, v_rel_bias, v_w_conv]
    sshapes = [a.shape for a in sw]
    packed = [_pack(a, 32)[None] for a in (sw, sg, sm, sv)]
    s_out = [_unpack(a[0], sshapes) for a in adamw(*packed, 32, "adamw_small")]

    def leaves(big_i, small_i):
        b_in, b_out, b_fi, b_fo = big_i
        s_co, s_ao, s_pm, s_qm, s_pf, s_qf, s_rel, s_wc = small_i
        return [b_in, s_wc, s_rel, s_co, s_ao, b_out, s_pm, s_qm, s_pf, s_qf, b_fi, b_fo]

    out = [loss, dx[None]]
    out += leaves([b[0] for b in big], sg)
    for i in range(1, 4):
        out += leaves([b[i] for b in big], s_out[i])
    return tuple(out)
```

```python
import functools

import jax
import jax.numpy as jnp
from jax import lax
from jax.experimental import pallas as pl
from jax.experimental.pallas import tpu as pltpu

F32 = jnp.float32
BF16 = jnp.bfloat16

D = 1024
PROJ = 3072
CW = 512
HD = 64
NH = 8
CHUNK = 64
BAND = 576
REL_CLIP = 128
NREL = 2 * REL_CLIP + 1
DFF = 2816
DEPTH = 4
NCHIP = 4
EPS = 1e-6
NEG_INF = -1e30

ADAM_LR = 0.001
ADAM_B1 = 0.9
ADAM_B2 = 0.999
ADAM_EPS = 1e-08
ADAM_WD = 0.01
ADAM_STEP = 10

V7X_VMEM_BYTES = 64 * 1024 * 1024
VMEM_LIMIT = V7X_VMEM_BYTES - 8 * 1024 * 1024
LANES = 128
QG_FWD = 4 * CHUNK
QG_BWD = 2 * CHUNK
LEFT = BAND - CHUNK
TQ = 512
TM = 256
SMALL_COLS = 1024
MESH = pl.DeviceIdType.MESH
NT = (((1,), (1,)), ((), ()))
TN = (((0,), (0,)), ((), ()))


def _cp(sem=None, vmem=VMEM_LIMIT):
    return pltpu.CompilerParams(dimension_semantics=sem, vmem_limit_bytes=vmem)


def _any():
    return pl.BlockSpec(memory_space=pl.ANY)


def _const(shape):
    nd = len(shape)
    return pl.BlockSpec(shape, lambda *_: (0,) * nd)


def _rms(v, g):
    r = lax.rsqrt(jnp.mean(v * v, axis=-1, keepdims=True) + EPS)
    return v * r * g


def _rms_bwd(dy, v, g):
    r = lax.rsqrt(jnp.mean(v * v, axis=-1, keepdims=True) + EPS)
    vh = v * r
    dg = jnp.sum(dy * vh, axis=0, keepdims=True)
    dvh = dy * g
    dv = r * (dvh - vh * jnp.mean(dvh * vh, axis=-1, keepdims=True))
    return dv, dg


def _group_mean(v, gm):
    return jnp.dot(v.astype(BF16), gm, preferred_element_type=F32)


def _group_rms_bwd(dy, v, g, gm):
    r = lax.rsqrt(_group_mean(v * v, gm) + EPS)
    vh = v * r
    dg = jnp.sum(dy * vh, axis=0, keepdims=True)
    dvh = dy * g
    dv = r * (dvh - vh * _group_mean(dvh * vh, gm))
    return dv, dg


def _head_masks(scale):
    lane = lax.broadcasted_iota(jnp.int32, (1, LANES), 1)
    return [jnp.where((lane >= HD * a) & (lane < HD * (a + 1)), scale, 0.0).astype(BF16) for a in range(2)]


class _Resident:
    def __init__(self, src, dst, sem):
        self.first = pl.program_id(0) == 0
        self.copy = pltpu.make_async_copy(src, dst, sem)
        self.dst = dst

        @pl.when(self.first)
        def _():
            self.copy.start()

    def read(self):
        @pl.when(self.first)
        def _():
            self.copy.wait()

        return self.dst[...]


def _stream_weights_once(pieces, sems, step):
    copies = [pltpu.make_async_copy(src, dst, sems.at[k]) for k, (src, dst) in enumerate(pieces)]
    first = pl.program_id(0) == 0

    @pl.when(first)
    def _():
        for cp in copies:
            cp.start()
        step(lambda k: copies[k].wait())

    @pl.when(jnp.logical_not(first))
    def _():
        step(lambda k: None)


def _conv_taps(u_prev, u, scr):
    n = u.shape[0]
    scr[0:16, :] = u_prev
    scr[16:16 + n, :] = u
    return scr[15:15 + n, :], scr[14:14 + n, :]


def fwd_inproj(x, g, w_all):
    t = x.shape[0]
    wc = PROJ // NCHIP

    def body(x_ref, g_ref, w_hbm, o_ref, w_v):
        @pl.when(pl.program_id(0) == 0)
        def _():
            pltpu.sync_copy(w_hbm, w_v)

        h = _rms(x_ref[...], g_ref[...]).astype(BF16)
        for b in range(NCHIP):
            o_ref[:, wc * b:wc * (b + 1)] = jnp.dot(h, w_v[b], preferred_element_type=F32).astype(BF16)

    return pl.pallas_call(
        body, grid=(t // TQ,),
        in_specs=[pl.BlockSpec((TQ, D), lambda i: (i, 0)), _const((1, D)), _any()],
        out_specs=pl.BlockSpec((TQ, PROJ), lambda i: (i, 0)),
        out_shape=jax.ShapeDtypeStruct((t, PROJ), BF16),
        scratch_shapes=[pltpu.VMEM((NCHIP, D, wc), BF16)],
        compiler_params=_cp(("arbitrary",)), name="fwd_inproj")(x, g, w_all)


def _attn_window_specs():
    return [
        pl.BlockSpec((TQ, CW), lambda i: (i, 3)),
        pl.BlockSpec((TQ, CW), lambda i: (jnp.maximum(i - 1, 0), 4)),
        pl.BlockSpec((TQ, CW), lambda i: (i, 4)),
        pl.BlockSpec((TQ, CW), lambda i: (jnp.maximum(i - 1, 0), 5)),
        pl.BlockSpec((TQ, CW), lambda i: (i, 5)),
    ]


def _conv_specs():
    return [
        pl.BlockSpec((TQ, 3 * CW), lambda i: (i, 0)),
        pl.BlockSpec((16, 3 * CW), lambda i: (jnp.maximum(i * (TQ // 16) - 1, 0), 0)),
    ]


def _conv_fwd(pc_ref, pcp_ref, wc_ref, scr, first):
    pc = pc_ref[...].astype(F32)
    hc, bg, cg = pc[:, :CW], pc[:, CW:2 * CW], pc[:, 2 * CW:]
    u = cg * hc
    pp = pcp_ref[...].astype(F32)
    u_prev = jnp.where(first, 0.0, pp[:, 2 * CW:] * pp[:, :CW])
    u1, u2 = _conv_taps(u_prev, u, scr)
    cout = wc_ref[0:1, :] * u2 + wc_ref[1:2, :] * u1 + wc_ref[2:3, :] * u
    return hc, bg, cg, u, u1, u2, cout


def _key_penalty(first, r0, kg):
    col = lax.broadcasted_iota(jnp.int32, (1, kg), 1)
    limit = jnp.where(first, TQ - r0, 0)
    return jnp.where(col < limit, NEG_INF, 0.0)


def fwd_mix(x, proj, bias2, wconv_t, g_co, g_ao, g_pm, gm, wout_all):
    t = x.shape[0]
    qg, kg = QG_FWD, QG_FWD + LEFT

    def body(x_ref, pc_ref, pcp_ref, q_ref, kp_ref, kc_ref, vp_ref, vc_ref, b2_ref, wc_ref, gco_ref, gao_ref, gpm_ref,
             gm_ref, wout_hbm, xmid_ref, o_ref, lse_ref, y_ref, z_ref, wout_v, kwin, vwin, cscr, sems):
        i = pl.program_id(0)
        first = i == 0
        wout = _Resident(wout_hbm, wout_v, sems.at[0])
        kwin[0:TQ, :] = kp_ref[...]
        kwin[TQ:2 * TQ, :] = kc_ref[...]
        vwin[0:TQ, :] = vp_ref[...]
        vwin[TQ:2 * TQ, :] = vc_ref[...]
        qmask = _head_masks(HD ** -0.5)
        low = lax.broadcasted_iota(jnp.int32, (1, LANES), 1) < HD

        def group(g, carry):
            r0 = pl.multiple_of(g * qg, qg)
            pen = _key_penalty(first, r0, kg)
            for hp in range(NH // 2):
                ls = slice(LANES * hp, LANES * (hp + 1))
                qb = q_ref[pl.ds(r0, qg), ls]
                q2 = jnp.concatenate([qb * qmask[0], qb * qmask[1]], axis=0)
                s = lax.dot_general(q2, kwin[pl.ds(r0, kg), ls], NT, preferred_element_type=F32)
                s = s + b2_ref[hp] + pen
                m = jnp.max(s, axis=-1, keepdims=True)
                p = jnp.exp(s - m)
                l = jnp.sum(p, axis=-1, keepdims=True)
                o2 = jnp.dot(p.astype(BF16), vwin[pl.ds(r0, kg), ls], preferred_element_type=F32) * (1.0 / l)
                lse2 = m + jnp.log(l)
                o_ref[pl.ds(r0, qg), ls] = jnp.where(low, o2[:qg], o2[qg:])
                lse_ref[pl.ds(r0, qg), ls] = jnp.where(low, lse2[:qg], lse2[qg:])
            return carry

        lax.fori_loop(0, TQ // qg, group, 0)

        _, bg, _, _, _, _, cout = _conv_fwd(pc_ref, pcp_ref, wc_ref, cscr, first)
        yc = bg * cout
        gmv = gm_ref[...]
        ycn = yc * lax.rsqrt(_group_mean(yc * yc, gmv) + EPS) * gco_ref[...]
        oa = o_ref[...]
        oan = oa * lax.rsqrt(_group_mean(oa * oa, gmv) + EPS) * gao_ref[...]
        y_ref[:, 0:CW] = ycn.astype(BF16)
        y_ref[:, CW:2 * CW] = oan.astype(BF16)
        z = jnp.dot(y_ref[...], wout.read(), preferred_element_type=F32)
        z_ref[...] = z
        xmid_ref[...] = x_ref[...] + _rms(z, gpm_ref[...])

    row = lambda w: pl.BlockSpec((TQ, w), lambda i: (i, 0))
    return pl.pallas_call(
        body, grid=(t // TQ,),
        in_specs=[row(D)] + _conv_specs() + _attn_window_specs() + [
            _const((NH // 2, 2 * qg, kg)), _const((8, CW)), _const((1, CW)), _const((1, CW)), _const((1, D)),
            _const((CW, CW)), _any()],
        out_specs=[row(D), row(CW), row(CW), row(D), row(D)],
        out_shape=[jax.ShapeDtypeStruct((t, D), F32), jax.ShapeDtypeStruct((t, CW), F32),
                   jax.ShapeDtypeStruct((t, CW), F32), jax.ShapeDtypeStruct((t, D), BF16),
                   jax.ShapeDtypeStruct((t, D), F32)],
        scratch_shapes=[pltpu.VMEM((D, D), BF16), pltpu.VMEM((2 * TQ, CW), BF16), pltpu.VMEM((2 * TQ, CW), BF16),
                        pltpu.VMEM((TQ + 16, CW), F32), pltpu.SemaphoreType.DMA((1,))],
        compiler_params=_cp(("arbitrary",)), name="fwd_mix",
    )(x, proj, proj, proj, proj, proj, proj, proj, bias2, wconv_t, g_co, g_ao, g_pm, gm, wout_all)


def fwd_ffn(xmid, g_pre, g_post, wfi_all, wfo_all):
    t = xmid.shape[0]
    hw = DFF // 2

    def body(x_ref, gpre_ref, gpost_ref, wfi_hbm, wfo_hbm, gu_ref, f_ref, xo_ref, wfi_v, wfo_v, sems):
        def step(ready):
            xv = x_ref[...]
            h = _rms(xv, gpre_ref[...]).astype(BF16)
            f = jnp.zeros((TM, D), F32)
            for j in range(2):
                ready(3 * j)
                gate = jnp.dot(h, wfi_v[j], preferred_element_type=F32)
                ready(3 * j + 1)
                up = jnp.dot(h, wfi_v[2 + j], preferred_element_type=F32)
                gu_ref[:, hw * j:hw * (j + 1)] = gate.astype(BF16)
                gu_ref[:, DFF + hw * j:DFF + hw * (j + 1)] = up.astype(BF16)
                act = gate * (1.0 / (1.0 + jnp.exp(-gate))) * up
                ready(3 * j + 2)
                f = f + jnp.dot(act.astype(BF16), wfo_v[j], preferred_element_type=F32)
            f_ref[...] = f
            xo_ref[...] = xv + _rms(f, gpost_ref[...])

        _stream_weights_once([(src.at[k], dst.at[k]) for j in range(2) for src, dst, k in
                              ((wfi_hbm, wfi_v, j), (wfi_hbm, wfi_v, 2 + j), (wfo_hbm, wfo_v, j))], sems, step)

    row = lambda w: pl.BlockSpec((TM, w), lambda i: (i, 0))
    return pl.pallas_call(
        body, grid=(t // TM,),
        in_specs=[row(D), _const((1, D)), _const((1, D)), _any(), _any()],
        out_specs=[row(2 * DFF), row(D), row(D)],
        out_shape=[jax.ShapeDtypeStruct((t, 2 * DFF), BF16), jax.ShapeDtypeStruct((t, D), F32),
                   jax.ShapeDtypeStruct((t, D), F32)],
        scratch_shapes=[pltpu.VMEM((NCHIP, D, hw), BF16), pltpu.VMEM((2, hw, D), BF16), pltpu.SemaphoreType.DMA((6,))],
        compiler_params=_cp(("arbitrary",)), name="fwd_ffn")(xmid, g_pre, g_post, wfi_all, wfo_all)


def loss_head(y, target):
    t = y.shape[0]

    def body(y_ref, t_ref, dy_ref, l_ref):
        @pl.when(pl.program_id(0) == 0)
        def _():
            l_ref[...] = jnp.zeros_like(l_ref)

        e = y_ref[...] - t_ref[...]
        dy_ref[...] = e * (1.0 / D)
        rows = jnp.sum(e * e, axis=-1, keepdims=True) * (1.0 / D)
        l_ref[...] += 0.5 * jnp.sum(rows, axis=0, keepdims=True)

    row = pl.BlockSpec((TQ, D), lambda i: (i, 0))
    return pl.pallas_call(
        body, grid=(t // TQ,), in_specs=[row, row], out_specs=[row, _const((8, LANES))],
        out_shape=[jax.ShapeDtypeStruct((t, D), F32), jax.ShapeDtypeStruct((8, LANES), F32)],
        compiler_params=_cp(("arbitrary",)), name="loss_head")(y, target)


def bwd_ffn(dx, f, xmid, gu, g_pre, g_post, wfi_all, wfo_all):
    t = dx.shape[0]
    hw = DFF // 2

    def body(dx_ref, f_ref, x_ref, gu_ref, gpre_ref, gpost_ref, wfi_hbm, wfo_hbm,
             dxm_ref, df_ref, act_ref, dgu_ref, h_ref, dgpost_ref, dgpre_ref, wfi_v, wfo_v, sems):
        @pl.when(pl.program_id(0) == 0)
        def _():
            dgpost_ref[...] = jnp.zeros_like(dgpost_ref)
            dgpre_ref[...] = jnp.zeros_like(dgpre_ref)

        def step(ready):
            dxo = dx_ref[...]
            df, dgp = _rms_bwd(dxo, f_ref[...], gpost_ref[...])
            dgpost_ref[...] += dgp
            dfb = df.astype(BF16)
            df_ref[...] = dfb
            dh = jnp.zeros((TM, D), F32)
            for j in range(2):
                ready(3 * j)
                dact = lax.dot_general(dfb, wfo_v[j], NT, preferred_element_type=F32)
                gate = gu_ref[:, hw * j:hw * (j + 1)].astype(F32)
                up = gu_ref[:, DFF + hw * j:DFF + hw * (j + 1)].astype(F32)
                sig = 1.0 / (1.0 + jnp.exp(-gate))
                silu = gate * sig
                act_ref[:, hw * j:hw * (j + 1)] = (silu * up).astype(BF16)
                dup = (dact * silu).astype(BF16)
                dgate = (dact * up * (sig * (1.0 + gate * (1.0 - sig)))).astype(BF16)
                dgu_ref[:, hw * j:hw * (j + 1)] = dgate
                dgu_ref[:, DFF + hw * j:DFF + hw * (j + 1)] = dup
                ready(3 * j + 1)
                dh = dh + lax.dot_general(dgate, wfi_v[j], NT, preferred_element_type=F32)
                ready(3 * j + 2)
                dh = dh + lax.dot_general(dup, wfi_v[2 + j], NT, preferred_element_type=F32)
            xv = x_ref[...]
            gpre = gpre_ref[...]
            h_ref[...] = _rms(xv, gpre).astype(BF16)
            dxv, dgq = _rms_bwd(dh, xv, gpre)
            dgpre_ref[...] += dgq
            dxm_ref[...] = dxo + dxv

        _stream_weights_once([(src.at[k], dst.at[k]) for j in range(2) for src, dst, k in
                              ((wfo_hbm, wfo_v, j), (wfi_hbm, wfi_v, j), (wfi_hbm, wfi_v, 2 + j))], sems, step)

    row = lambda w: pl.BlockSpec((TM, w), lambda i: (i, 0))
    return pl.pallas_call(
        body, grid=(t // TM,),
        in_specs=[row(D), row(D), row(D), row(2 * DFF), _const((1, D)), _const((1, D)), _any(), _any()],
        out_specs=[row(D), row(D), row(DFF), row(2 * DFF), row(D), _const((1, D)), _const((1, D))],
        out_shape=[jax.ShapeDtypeStruct((t, D), F32), jax.ShapeDtypeStruct((t, D), BF16),
                   jax.ShapeDtypeStruct((t, DFF), BF16), jax.ShapeDtypeStruct((t, 2 * DFF), BF16),
                   jax.ShapeDtypeStruct((t, D), BF16), jax.ShapeDtypeStruct((1, D), F32),
                   jax.ShapeDtypeStruct((1, D), F32)],
        scratch_shapes=[pltpu.VMEM((NCHIP, D, hw), BF16), pltpu.VMEM((2, hw, D), BF16), pltpu.SemaphoreType.DMA((6,))],
        compiler_params=_cp(("arbitrary",)), name="bwd_ffn")(dx, f, xmid, gu, g_pre, g_post, wfi_all, wfo_all)


def bwd_mix(dxm, z, o, proj, wconv_t, g_co, g_ao, g_pm, gm, wout_all):
    t = dxm.shape[0]

    def body(dx_ref, z_ref, o_ref, pc_ref, pcp_ref, wc_ref, gco_ref, gao_ref, gpm_ref, gm_ref, wout_hbm,
             dz_ref, do_ref, dco_ref, dbg_ref, dgpm_ref, dgco_ref, dgao_ref, wout_v, cscr):
        first = pl.program_id(0) == 0

        @pl.when(first)
        def _():
            pltpu.sync_copy(wout_hbm, wout_v)
            dgpm_ref[...] = jnp.zeros_like(dgpm_ref)
            dgco_ref[...] = jnp.zeros_like(dgco_ref)
            dgao_ref[...] = jnp.zeros_like(dgao_ref)

        dz, dgp = _rms_bwd(dx_ref[...], z_ref[...], gpm_ref[...])
        dgpm_ref[...] += dgp
        dzb = dz.astype(BF16)
        dz_ref[...] = dzb
        gmv = gm_ref[...]
        _, bg, _, _, _, _, cout = _conv_fwd(pc_ref, pcp_ref, wc_ref, cscr, first)
        dy_conv = lax.dot_general(dzb, wout_v[0:CW, :], NT, preferred_element_type=F32)
        dyc, dgc = _group_rms_bwd(dy_conv, bg * cout, gco_ref[...], gmv)
        dgco_ref[...] += dgc
        dbg_ref[...] = (dyc * cout).astype(BF16)
        dco_ref[...] = dyc * bg
        dy_attn = lax.dot_general(dzb, wout_v[CW:2 * CW, :], NT, preferred_element_type=F32)
        do, dga = _group_rms_bwd(dy_attn, o_ref[...], gao_ref[...], gmv)
        dgao_ref[...] += dga
        do_ref[...] = do.astype(BF16)

    row = lambda w: pl.BlockSpec((TQ, w), lambda i: (i, 0))
    return pl.pallas_call(
        body, grid=(t // TQ,),
        in_specs=[row(D), row(D), row(CW)] + _conv_specs() + [
            _const((8, CW)), _const((1, CW)), _const((1, CW)), _const((1, D)), _const((CW, CW)), _any()],
        out_specs=[row(D), row(CW), row(CW), row(CW), _const((1, D)), _const((1, CW)), _const((1, CW))],
        out_shape=[jax.ShapeDtypeStruct((t, D), BF16), jax.ShapeDtypeStruct((t, CW), BF16),
                   jax.ShapeDtypeStruct((t, CW), F32), jax.ShapeDtypeStruct((t, CW), BF16),
                   jax.ShapeDtypeStruct((1, D), F32), jax.ShapeDtypeStruct((1, CW), F32),
                   jax.ShapeDtypeStruct((1, CW), F32)],
        scratch_shapes=[pltpu.VMEM((D, D), BF16), pltpu.VMEM((TQ + 16, CW), F32)],
        compiler_params=_cp(("arbitrary",)), name="bwd_mix",
    )(dxm, z, o, proj, proj, wconv_t, g_co, g_ao, g_pm, gm, wout_all)


def bwd_conv(dco, proj, wconv_t):
    t = dco.shape[0]
    nt = t // TQ

    def body(d_ref, dn_ref, pc_ref, pcp_ref, wc_ref, dhc_ref, dcg_ref, dw_ref, cscr, dscr):
        i = pl.program_id(0)
        first = i == 0

        @pl.when(first)
        def _():
            dw_ref[...] = jnp.zeros_like(dw_ref)

        hc, _, cg, u, u1, u2, _ = _conv_fwd(pc_ref, pcp_ref, wc_ref, cscr, first)
        d0 = d_ref[...]
        dscr[0:TQ, :] = d0
        dscr[TQ:TQ + 8, :] = jnp.where(i == nt - 1, 0.0, dn_ref[...])
        d1 = dscr[1:TQ + 1, :]
        d2 = dscr[2:TQ + 2, :]
        du = wc_ref[2:3, :] * d0 + wc_ref[1:2, :] * d1 + wc_ref[0:1, :] * d2
        dhc_ref[...] = (du * cg).astype(BF16)
        dcg_ref[...] = (du * hc).astype(BF16)
        dw_ref[0:1, :] += jnp.sum(d0 * u2, axis=0, keepdims=True)
        dw_ref[1:2, :] += jnp.sum(d0 * u1, axis=0, keepdims=True)
        dw_ref[2:3, :] += jnp.sum(d0 * u, axis=0, keepdims=True)

    row = lambda w: pl.BlockSpec((TQ, w), lambda i: (i, 0))
    nxt = pl.BlockSpec((8, CW), lambda i: (jnp.minimum((i + 1) * (TQ // 8), t // 8 - 1), 0))
    return pl.pallas_call(
        body, grid=(nt,),
        in_specs=[row(CW), nxt] + _conv_specs() + [_const((8, CW))],
        out_specs=[row(CW), row(CW), _const((8, CW))],
        out_shape=[jax.ShapeDtypeStruct((t, CW), BF16), jax.ShapeDtypeStruct((t, CW), BF16),
                   jax.ShapeDtypeStruct((8, CW), F32)],
        scratch_shapes=[pltpu.VMEM((TQ + 16, CW), F32), pltpu.VMEM((TQ + 8, CW), F32)],
        compiler_params=_cp(("arbitrary",)), name="bwd_conv")(dco, dco, proj, proj, wconv_t)


def bwd_attn(proj, o, do, lse, bias2):
    t = o.shape[0]
    nt = t // TQ
    qg, kg = QG_BWD, QG_BWD + LEFT

    def body(q_ref, kp_ref, kc_ref, vp_ref, vc_ref, o_ref, do_ref, lse_ref, b2_ref,
             dq_ref, dk_hbm, dv_hbm, db_hbm, kwin, vwin, dk_acc, dv_acc, db_acc):
        i = pl.program_id(0)
        first = i == 0

        @pl.when(first)
        def _():
            dk_acc[...] = jnp.zeros_like(dk_acc)
            dv_acc[...] = jnp.zeros_like(dv_acc)
            db_acc[...] = jnp.zeros_like(db_acc)

        kwin[0:TQ, :] = kp_ref[...]
        kwin[TQ:2 * TQ, :] = kc_ref[...]
        vwin[0:TQ, :] = vp_ref[...]
        vwin[TQ:2 * TQ, :] = vc_ref[...]
        scale = HD ** -0.5
        qmask = _head_masks(scale)
        vmask = _head_masks(1.0)
        low = lax.broadcasted_iota(jnp.int32, (1, LANES), 1) < HD

        def group(g, carry):
            r0 = pl.multiple_of(g * qg, qg)
            base = pl.multiple_of(i * TQ + r0, qg)
            pen = _key_penalty(first, r0, kg)
            for hp in range(NH // 2):
                ls = slice(LANES * hp, LANES * (hp + 1))
                qb = q_ref[pl.ds(r0, qg), ls]
                kw = kwin[pl.ds(r0, kg), ls]
                dob = do_ref[pl.ds(r0, qg), ls]
                prod = dob.astype(F32) * o_ref[pl.ds(r0, qg), ls]
                lseb = lse_ref[pl.ds(r0, qg), ls]
                q2 = jnp.concatenate([qb * qmask[0], qb * qmask[1]], axis=0)
                do2 = jnp.concatenate([dob * vmask[0], dob * vmask[1]], axis=0)
                lse2 = jnp.concatenate([lseb[:, 0:1], lseb[:, HD:HD + 1]], axis=0)
                dsum = jnp.concatenate([jnp.sum(jnp.where(low, prod, 0.0), axis=-1, keepdims=True),
                                        jnp.sum(jnp.where(low, 0.0, prod), axis=-1, keepdims=True)], axis=0)
                s = lax.dot_general(q2, kw, NT, preferred_element_type=F32) + b2_ref[hp] + pen
                p = jnp.exp(s - lse2)
                dp = lax.dot_general(do2, vwin[pl.ds(r0, kg), ls], NT, preferred_element_type=F32)
                ds = p * (dp - dsum)
                db_acc[hp] += ds
                dsb = ds.astype(BF16)
                dq2 = jnp.dot(dsb, kw, preferred_element_type=F32)
                dq_ref[pl.ds(r0, qg), ls] = (jnp.where(low, dq2[:qg], dq2[qg:]) * scale).astype(BF16)
                dk_acc[pl.ds(base, kg), ls] += lax.dot_general(dsb, q2, TN, preferred_element_type=F32)
                dv_acc[pl.ds(base, kg), ls] += lax.dot_general(p.astype(BF16), do2, TN, preferred_element_type=F32)
            return carry

        lax.fori_loop(0, TQ // qg, group, 0)

        @pl.when(i == nt - 1)
        def _():
            pltpu.sync_copy(dk_acc, dk_hbm)
            pltpu.sync_copy(dv_acc, dv_hbm)
            pltpu.sync_copy(db_acc, db_hbm)

    row = lambda w: pl.BlockSpec((TQ, w), lambda i: (i, 0))
    return pl.pallas_call(
        body, grid=(nt,),
        in_specs=_attn_window_specs() + [row(CW), row(CW), row(CW), _const((NH // 2, 2 * qg, kg))],
        out_specs=[row(CW), _any(), _any(), _any()],
        out_shape=[jax.ShapeDtypeStruct((t, CW), BF16), jax.ShapeDtypeStruct((t + TQ, CW), F32),
                   jax.ShapeDtypeStruct((t + TQ, CW), F32), jax.ShapeDtypeStruct((NH // 2, 2 * qg, kg), F32)],
        scratch_shapes=[pltpu.VMEM((2 * TQ, CW), BF16), pltpu.VMEM((2 * TQ, CW), BF16),
                        pltpu.VMEM((t + TQ, CW), F32), pltpu.VMEM((t + TQ, CW), F32),
                        pltpu.VMEM((NH // 2, 2 * qg, kg), F32)],
        compiler_params=_cp(("arbitrary",)), name="bwd_attn",
    )(proj, proj, proj, proj, proj, o, do, lse, bias2)


def bwd_inproj(dxm, x, dhc, dbg, dcg, dq, dk, dv, g, w_all):
    t = x.shape[0]
    wc = PROJ // NCHIP

    def body(dxm_ref, x_ref, dhc_ref, dbg_ref, dcg_ref, dq_ref, dk_ref, dv_ref, g_ref, w_hbm,
             dx_ref, dp_ref, h_ref, dg_ref, w_v):
        @pl.when(pl.program_id(0) == 0)
        def _():
            pltpu.sync_copy(w_hbm, w_v)
            dg_ref[...] = jnp.zeros_like(dg_ref)

        dp_ref[:, 0:CW] = dhc_ref[...]
        dp_ref[:, CW:2 * CW] = dbg_ref[...]
        dp_ref[:, 2 * CW:3 * CW] = dcg_ref[...]
        dp_ref[:, 3 * CW:4 * CW] = dq_ref[...]
        dp_ref[:, 4 * CW:5 * CW] = dk_ref[...].astype(BF16)
        dp_ref[:, 5 * CW:6 * CW] = dv_ref[...].astype(BF16)
        dh = jnp.zeros((TQ, D), F32)
        for b in range(NCHIP):
            dh = dh + lax.dot_general(dp_ref[:, wc * b:wc * (b + 1)], w_v[b], NT, preferred_element_type=F32)
        xv = x_ref[...]
        gv = g_ref[...]
        h_ref[...] = _rms(xv, gv).astype(BF16)
        dxv, dgv = _rms_bwd(dh, xv, gv)
        dg_ref[...] += dgv
        dx_ref[...] = dxm_ref[...] + dxv

    row = lambda w: pl.BlockSpec((TQ, w), lambda i: (i, 0))
    pad = pl.BlockSpec((TQ, CW), lambda i: (i + 1, 0))
    return pl.pallas_call(
        body, grid=(t // TQ,),
        in_specs=[row(D), row(D), row(CW), row(CW), row(CW), row(CW), pad, pad, _const((1, D)), _any()],
        out_specs=[row(D), row(PROJ), row(D), _const((1, D))],
        out_shape=[jax.ShapeDtypeStruct((t, D), F32), jax.ShapeDtypeStruct((t, PROJ), BF16),
                   jax.ShapeDtypeStruct((t, D), BF16), jax.ShapeDtypeStruct((1, D), F32)],
        scratch_shapes=[pltpu.VMEM((NCHIP, D, wc), BF16)],
        compiler_params=_cp(("arbitrary",)), name="bwd_inproj",
    )(dxm, x, dhc, dbg, dcg, dq, dk, dv, g, w_all)


def wgrad(a, b, kb, nb, by_columns, name):
    t, k = a.shape
    n = b.shape[1]
    tk = 512

    def body(a_ref, b_ref, o_ref):
        o_ref[...] = jnp.zeros_like(o_ref)
        for c in range(t // tk):
            o_ref[...] += lax.dot_general(a_ref[tk * c:tk * (c + 1), :], b_ref[tk * c:tk * (c + 1), :], TN,
                                          preferred_element_type=F32)

    if by_columns:
        assert nb == n // NCHIP
        out_spec = pl.BlockSpec((None, kb, nb), lambda ki, ni: (ni, ki, 0))
        out_shape = jax.ShapeDtypeStruct((NCHIP, k, nb), F32)
    else:
        assert nb == n
        out_spec = pl.BlockSpec((kb, nb), lambda ki, ni: (ki, 0))
        out_shape = jax.ShapeDtypeStruct((k, n), F32)
    return pl.pallas_call(
        body, grid=(k // kb, n // nb),
        in_specs=[pl.BlockSpec((t, kb), lambda ki, ni: (0, ki)), pl.BlockSpec((t, nb), lambda ki, ni: (0, ni))],
        out_specs=out_spec, out_shape=out_shape,
        compiler_params=_cp(("arbitrary", "arbitrary")), name=name)(a, b)


TOE = 1024
assert 2 * QG_FWD + LEFT <= TOE
N_FLAT = LEFT - REL_CLIP + 1
N_VAR = BAND - N_FLAT


def _diag_vector(table):
    last = table[:, 2 * REL_CLIP:]
    var = table[:, 2 * REL_CLIP - N_VAR:2 * REL_CLIP][:, ::-1]
    return jnp.concatenate([jnp.broadcast_to(last, (NH, N_FLAT)), var, jnp.broadcast_to(last, (NH, TOE - BAND))], axis=1)


def _diag_vector_bwd(dvec):
    dlast = jnp.sum(dvec[:, :N_FLAT], axis=1, keepdims=True) + jnp.sum(dvec[:, BAND:], axis=1, keepdims=True)
    dvar = dvec[:, N_FLAT:BAND][:, ::-1]
    return jnp.concatenate([jnp.zeros((NH, 2 * REL_CLIP - N_VAR), F32), dvar, dlast], axis=1)


def _band_valid(qg):
    r = lax.broadcasted_iota(jnp.int32, (qg, qg + LEFT), 0)
    p = lax.broadcasted_iota(jnp.int32, (qg, qg + LEFT), 1)
    start = lax.shift_left(lax.shift_right_logical(r, 6), 6)
    return (p >= start) & (p < start + BAND)


def bias_expand(vec, qgs):
    def body(v_ref, *o_refs):
        for qg, o_ref in zip(qgs, o_refs):
            valid = _band_valid(qg)
            for h in range(NH):
                rows = jnp.broadcast_to(v_ref[h:h + 1, :], (qg, TOE))
                toe = pltpu.roll(rows, 0, 1, stride=1, stride_axis=0)
                o_ref[h // 2, qg * (h % 2):qg * (h % 2 + 1), :] = jnp.where(valid, toe[:, :qg + LEFT], NEG_INF)

    return pl.pallas_call(body, out_shape=[jax.ShapeDtypeStruct((NH // 2, 2 * qg, qg + LEFT), F32) for qg in qgs],
                          name="bias_expand")(vec)


def bias_reduce(db2):
    _, qg, kg = db2.shape

    def body(d_ref, o_ref):
        ii = lax.broadcasted_iota(jnp.int32, (kg, kg), 0)
        jj = lax.broadcasted_iota(jnp.int32, (kg, kg), 1)
        flip = jnp.where(ii + jj == kg - 1, 1.0, 0.0).astype(BF16)
        for h in range(NH):
            rest = d_ref[h]
            rev = jnp.zeros((qg, kg), F32)
            for _ in range(3):
                term = rest.astype(BF16)
                rev = rev + jnp.dot(term, flip, preferred_element_type=F32)
                rest = rest - term.astype(F32)
            d = jnp.concatenate([jnp.zeros((qg, TOE - kg), F32), rev], axis=1)
            back = pltpu.roll(d, 0, 1, stride=1, stride_axis=0)
            o_ref[h:h + 1, :] = jnp.sum(back, axis=0, keepdims=True)

    rev = pl.pallas_call(body, out_shape=jax.ShapeDtypeStruct((NH, TOE), F32), name="bias_reduce")(db2)
    return rev[:, ::-1]


def _place():
    x, y, c = lax.axis_index("x"), lax.axis_index("y"), lax.axis_index("c")
    chips = [(1 - x, y), (x, 1 - y), (1 - x, 1 - y)]
    return x, y, c, chips


def _half(ref_rows, c):
    return pl.ds(c * (ref_rows // 2), ref_rows // 2)


HBM_SPEC = pl.BlockSpec(memory_space=pltpu.HBM)
SEM_SPEC = pl.BlockSpec(memory_space=pltpu.SEMAPHORE)
IN_FLIGHT = pltpu.CompilerParams(has_side_effects=pltpu.SideEffectType.DATAFLOW_SIDE_EFFECTING)


def _in_hbm(a):
    return pltpu.with_memory_space_constraint(a, pltpu.HBM)


def cast_to_slot(ws, chip, layer):
    n = len(ws)
    steps = 4

    def body(b_ref, *refs):
        del b_ref
        for w_ref, o_ref in zip(refs[:n], refs[n:]):
            o_ref[...] = w_ref[...].astype(BF16)

    grid_spec = pltpu.PrefetchScalarGridSpec(
        num_scalar_prefetch=1, grid=(steps,),
        in_specs=[pl.BlockSpec((None, w.shape[1] // steps, w.shape[2]), lambda r, b: (layer, r, 0)) for w in ws],
        out_specs=[pl.BlockSpec((None, w.shape[1] // steps, w.shape[2]), lambda r, b: (b[0], r, 0)) for w in ws])
    return pl.pallas_call(body, grid_spec=grid_spec,
                          out_shape=[jax.ShapeDtypeStruct((NCHIP,) + w.shape[1:], BF16) for w in ws],
                          compiler_params=_cp(("arbitrary",)), name="cast_to_slot")(chip, *ws)


def _gather_copies(bufs, send, recv):
    x, y, c, chips = _place()
    b = 2 * x + y
    out = []
    for k, buf in enumerate(bufs):
        rows = buf.shape[1]
        mine = buf.at[b, _half(rows, c), :]
        for j, (cx, cy) in enumerate(chips):
            theirs = buf.at[2 * cx + cy, _half(rows, c), :]
            sems = dict(send_sem=send.at[3 * k + j], recv_sem=recv.at[3 * k + j],
                        device_id=(cx, cy, c), device_id_type=MESH)
            out.append((pltpu.make_async_remote_copy(src_ref=mine, dst_ref=mine, **sems),
                        pltpu.make_async_remote_copy(src_ref=theirs, dst_ref=theirs, **sems)))
    return out


def gather_start(bufs, after, layer):
    n = len(bufs)

    def body(*refs):
        ins = refs[:n]
        send, recv = refs[n + 1], refs[n + 2]
        token = refs[-1]
        for start, _ in _gather_copies(ins, send, recv):
            start.start()
        token[...] = jnp.zeros_like(token)

    sems = pltpu.SemaphoreType.DMA((3 * n,))
    res = pl.pallas_call(
        body, name=f"gather_start_{layer}",
        in_specs=[HBM_SPEC] * n + [_any()],
        out_specs=[SEM_SPEC, SEM_SPEC] + [HBM_SPEC] * n + [pl.BlockSpec(memory_space=pltpu.VMEM)],
        out_shape=[sems, sems] + [pltpu.HBM(b.shape, b.dtype) for b in bufs] + [jax.ShapeDtypeStruct((8, LANES), F32)],
        input_output_aliases={k: 2 + k for k in range(n)}, compiler_params=IN_FLIGHT,
    )(*[_in_hbm(b) for b in bufs], after)
    return res[0], res[1], res[2:2 + n], res[-1]


def gather_wait(send, recv, bufs, after, layer):
    n = len(bufs)

    def body(*refs):
        ins = refs[:n]
        send_ref, recv_ref = refs[n], refs[n + 1]
        for start, arrival in _gather_copies(ins, send_ref, recv_ref):
            start.wait_send()
            arrival.wait_recv()

    return pl.pallas_call(
        body, name=f"gather_wait_{layer}",
        in_specs=[HBM_SPEC] * n + [SEM_SPEC, SEM_SPEC, _any()], out_specs=[HBM_SPEC] * n,
        out_shape=[pltpu.HBM(b.shape, b.dtype) for b in bufs],
        input_output_aliases={k: k for k in range(n)}, compiler_params=IN_FLIGHT,
    )(*bufs, send, recv, after)


def gather_forward(bufs):
    n = len(bufs)

    def body(*refs):
        outs = refs[n:2 * n]
        send, recv = refs[2 * n:]
        x, y, c, chips = _place()
        cps = []
        for k in range(n):
            rows = outs[k].shape[1]
            for j, (cx, cy) in enumerate(chips):
                sems = dict(send_sem=send.at[3 * k + j], recv_sem=recv.at[3 * k + j],
                            device_id=(x, y, 1 - c), device_id_type=MESH)
                mine = outs[k].at[2 * cx + cy, _half(rows, c), :]
                theirs = outs[k].at[2 * cx + cy, _half(rows, 1 - c), :]
                cp = pltpu.make_async_remote_copy(src_ref=mine, dst_ref=mine, **sems)
                cp.start()
                cps.append((cp, pltpu.make_async_remote_copy(src_ref=theirs, dst_ref=theirs, **sems)))
        for cp, arrival in cps:
            cp.wait_send()
            arrival.wait_recv()

    return pl.pallas_call(
        body, in_specs=[_any()] * n, out_specs=[_any()] * n,
        out_shape=[jax.ShapeDtypeStruct(b.shape, b.dtype) for b in bufs], input_output_aliases={k: k for k in range(n)},
        scratch_shapes=[pltpu.SemaphoreType.DMA((3 * n,)), pltpu.SemaphoreType.DMA((3 * n,))],
        name="gather_forward")(*bufs)


def _forward_copies(bufs, send, recv):
    x, y, c, chips = _place()
    out = []
    for k, buf in enumerate(bufs):
        rows = buf.shape[1]
        for j, (cx, cy) in enumerate(chips):
            sems = dict(send_sem=send.at[3 * k + j], recv_sem=recv.at[3 * k + j],
                        device_id=(x, y, 1 - c), device_id_type=MESH)
            mine = buf.at[2 * cx + cy, _half(rows, c), :]
            theirs = buf.at[2 * cx + cy, _half(rows, 1 - c), :]
            out.append((pltpu.make_async_remote_copy(src_ref=mine, dst_ref=mine, **sems),
                        pltpu.make_async_remote_copy(src_ref=theirs, dst_ref=theirs, **sems)))
    return out


def forward_start(bufs, tag):
    n = len(bufs)

    def body(*refs):
        ins = refs[:n]
        send, recv = refs[n], refs[n + 1]
        token = refs[-1]
        for start, _ in _forward_copies(ins, send, recv):
            start.start()
        token[...] = jnp.zeros_like(token)

    sems = pltpu.SemaphoreType.DMA((3 * n,))
    res = pl.pallas_call(
        body, name=f"forward_start_{tag}", in_specs=[HBM_SPEC] * n,
        out_specs=[SEM_SPEC, SEM_SPEC] + [HBM_SPEC] * n + [pl.BlockSpec(memory_space=pltpu.VMEM)],
        out_shape=[sems, sems] + [pltpu.HBM(b.shape, b.dtype) for b in bufs] + [jax.ShapeDtypeStruct((8, LANES), F32)],
        input_output_aliases={k: 2 + k for k in range(n)}, compiler_params=IN_FLIGHT,
    )(*[_in_hbm(b) for b in bufs])
    return res[0], res[1], res[2:2 + n], res[-1]


def forward_wait(send, recv, bufs, after, tag):
    n = len(bufs)

    def body(*refs):
        ins = refs[:n]
        send_ref, recv_ref = refs[n], refs[n + 1]
        for start, arrival in _forward_copies(ins, send_ref, recv_ref):
            start.wait_send()
            arrival.wait_recv()

    return pl.pallas_call(
        body, name=f"forward_wait_{tag}",
        in_specs=[HBM_SPEC] * n + [SEM_SPEC, SEM_SPEC, _any()], out_specs=[HBM_SPEC] * n,
        out_shape=[pltpu.HBM(b.shape, b.dtype) for b in bufs],
        input_output_aliases={k: k for k in range(n)}, compiler_params=IN_FLIGHT,
    )(*bufs, send, recv, after)


def _exchange_copies(srcs, lands, send, recv):
    x, y, c, _ = _place()
    return [pltpu.make_async_remote_copy(
        src_ref=src.at[:, _half(src.shape[1], 1 - c), :], dst_ref=land, send_sem=send.at[k], recv_sem=recv.at[k],
        device_id=(x, y, 1 - c), device_id_type=MESH) for k, (src, land) in enumerate(zip(srcs, lands))]


def exchange_start(srcs, tag):
    n = len(srcs)
    lands = [lax.empty((s.shape[0], s.shape[1] // 2, s.shape[2]), s.dtype) for s in srcs]

    def body(*refs):
        ins, land_refs = refs[:n], refs[n:2 * n]
        send, recv = refs[2 * n], refs[2 * n + 1]
        token = refs[-1]
        for cp in _exchange_copies(ins, land_refs, send, recv):
            cp.start()
        token[...] = jnp.zeros_like(token)

    sems = pltpu.SemaphoreType.DMA((n,))
    res = pl.pallas_call(
        body, name=f"exchange_start_{tag}",
        in_specs=[HBM_SPEC] * (2 * n),
        out_specs=[SEM_SPEC, SEM_SPEC] + [HBM_SPEC] * (2 * n) + [pl.BlockSpec(memory_space=pltpu.VMEM)],
        out_shape=[sems, sems] + [pltpu.HBM(a.shape, a.dtype) for a in list(srcs) + lands]
        + [jax.ShapeDtypeStruct((8, LANES), F32)],
        input_output_aliases={k: 2 + k for k in range(2 * n)}, compiler_params=IN_FLIGHT,
    )(*[_in_hbm(a) for a in list(srcs) + lands])
    return res[0], res[1], res[2:2 + n], res[2 + n:2 + 2 * n], res[-1]


def exchange_wait(send, recv, srcs, lands, after, tag):
    n = len(srcs)

    def body(*refs):
        ins, land_refs = refs[:n], refs[n:2 * n]
        send_ref, recv_ref = refs[2 * n], refs[2 * n + 1]
        for cp in _exchange_copies(ins, land_refs, send_ref, recv_ref):
            cp.wait_send()
            cp.wait_recv()

    res = pl.pallas_call(
        body, name=f"exchange_wait_{tag}",
        in_specs=[HBM_SPEC] * (2 * n) + [SEM_SPEC, SEM_SPEC, _any()], out_specs=[HBM_SPEC] * (2 * n),
        out_shape=[pltpu.HBM(a.shape, a.dtype) for a in list(srcs) + list(lands)],
        input_output_aliases={k: k for k in range(2 * n)}, compiler_params=IN_FLIGHT,
    )(*srcs, *lands, send, recv, after)
    return res[:n], res[n:]


def add_pair(gs, r1s, core):
    n = len(gs)

    def body(c_ref, *refs):
        del c_ref
        for g_ref, r_ref, o_ref in zip(refs[:n], refs[n:2 * n], refs[2 * n:]):
            o_ref[...] = (g_ref[...] + r_ref[...]).astype(BF16)

    blk = lambda r: (None,) + r.shape[1:]
    grid_spec = pltpu.PrefetchScalarGridSpec(
        num_scalar_prefetch=1, grid=(NCHIP,),
        in_specs=[pl.BlockSpec(blk(r), lambda s, c: (s, c[0], 0)) for r in r1s]
        + [pl.BlockSpec(blk(r), lambda s, c: (s, 0, 0)) for r in r1s],
        out_specs=[pl.BlockSpec(blk(r), lambda s, c: (s, 0, 0)) for r in r1s])
    return pl.pallas_call(body, grid_spec=grid_spec, out_shape=[jax.ShapeDtypeStruct(r.shape, BF16) for r in r1s],
                          compiler_params=_cp(("arbitrary",)), name="add_pair")(core, *gs, *r1s)


def _scatter_copies(srcs, lands, send, recv):
    _, _, c, chips = _place()
    out = []
    for k, (src, land) in enumerate(zip(srcs, lands)):
        for j, (cx, cy) in enumerate(chips):
            out.append(pltpu.make_async_remote_copy(
                src_ref=src.at[2 * cx + cy], dst_ref=land.at[j], send_sem=send.at[3 * k + j],
                recv_sem=recv.at[3 * k + j], device_id=(cx, cy, c), device_id_type=MESH))
    return out


def scatter_start(srcs, layer):
    n = len(srcs)
    srcs = list(srcs)
    lands = [lax.empty((3,) + s.shape[1:], s.dtype) for s in srcs]

    def body(*refs):
        ins, land_refs = refs[:n], refs[n:2 * n]
        send, recv = refs[2 * n], refs[2 * n + 1]
        token = refs[-1]
        for cp in _scatter_copies(ins, land_refs, send, recv):
            cp.start()
        token[...] = jnp.zeros_like(token)

    sems = pltpu.SemaphoreType.DMA((3 * n,))
    res = pl.pallas_call(
        body, name=f"scatter_start_{layer}",
        in_specs=[HBM_SPEC] * (2 * n),
        out_specs=[SEM_SPEC, SEM_SPEC] + [HBM_SPEC] * (2 * n) + [pl.BlockSpec(memory_space=pltpu.VMEM)],
        out_shape=[sems, sems] + [pltpu.HBM(a.shape, a.dtype) for a in srcs + lands]
        + [jax.ShapeDtypeStruct((8, LANES), F32)],
        input_output_aliases={k: 2 + k for k in range(2 * n)}, compiler_params=IN_FLIGHT,
    )(*[_in_hbm(a) for a in srcs + lands])
    return res[0], res[1], res[2:2 + n], res[2 + n:2 + 2 * n], res[-1]


def scatter_wait(send, recv, srcs, lands, after, layer):
    n = len(srcs)

    def body(*refs):
        ins, land_refs = refs[:n], refs[n:2 * n]
        send_ref, recv_ref = refs[2 * n], refs[2 * n + 1]
        for cp in _scatter_copies(ins, land_refs, send_ref, recv_ref):
            cp.wait_send()
            cp.wait_recv()

    res = pl.pallas_call(
        body, name=f"scatter_wait_{layer}",
        in_specs=[HBM_SPEC] * (2 * n) + [SEM_SPEC, SEM_SPEC, _any()], out_specs=[HBM_SPEC] * (2 * n),
        out_shape=[pltpu.HBM(a.shape, a.dtype) for a in list(srcs) + list(lands)],
        input_output_aliases={k: k for k in range(2 * n)}, compiler_params=IN_FLIGHT,
    )(*srcs, *lands, send, recv, after)
    return res[n:]


def add_chips(gs, r1s, r2s, place, totals, layer):
    n = len(gs)
    steps = 2

    def body(p_ref, *refs):
        del p_ref
        for g_ref, r1_ref, r2_ref, o_ref in zip(refs[:n], refs[n:2 * n], refs[2 * n:3 * n], refs[4 * n:]):
            own = g_ref[...] + r1_ref[...]
            o_ref[...] = ((own + r2_ref[0].astype(F32)) + r2_ref[1].astype(F32)) + r2_ref[2].astype(F32)

    blk = lambda r: (None, r.shape[1] // steps, r.shape[2])
    grid_spec = pltpu.PrefetchScalarGridSpec(
        num_scalar_prefetch=1, grid=(steps,),
        in_specs=[pl.BlockSpec(blk(r), lambda i, p: (p[1], p[0] * steps + i, 0)) for r in r1s]
        + [pl.BlockSpec(blk(r), lambda i, p: (p[1], i, 0)) for r in r1s]
        + [pl.BlockSpec((3,) + blk(r)[1:], lambda i, p: (0, i, 0)) for r in r1s] + [_any()] * n,
        out_specs=[pl.BlockSpec(blk(r), lambda i, p: (layer, p[0] * steps + i, 0)) for r in r1s])
    return pl.pallas_call(body, grid_spec=grid_spec, out_shape=[jax.ShapeDtypeStruct(t.shape, F32) for t in totals],
                          input_output_aliases={1 + 3 * n + k: k for k in range(n)},
                          compiler_params=_cp(("arbitrary",)), name="add_chips")(place, *gs, *r1s, *r2s, *totals)


def pair_share(gs, tag):
    n = len(gs)

    def body(*refs):
        outs = refs[n:2 * n]
        send, recv = refs[2 * n:]
        x, y, c, _ = _place()
        cps = []
        for k in range(n):
            mine = outs[k].at[:, _half(outs[k].shape[1], c), :]
            cp = pltpu.make_async_remote_copy(
                src_ref=mine, dst_ref=mine, send_sem=send.at[k], recv_sem=recv.at[k],
                device_id=(x, y, 1 - c), device_id_type=MESH)
            cp.start()
            cps.append(cp)
        for k, cp in enumerate(cps):
            cp.wait_send()
            theirs = outs[k].at[:, _half(outs[k].shape[1], 1 - c), :]
            pltpu.make_async_remote_copy(
                src_ref=theirs, dst_ref=theirs, send_sem=send.at[k], recv_sem=recv.at[k],
                device_id=(x, y, 1 - c), device_id_type=MESH).wait_recv()

    return pl.pallas_call(
        body, in_specs=[_any()] * n, out_specs=[_any()] * n,
        out_shape=[jax.ShapeDtypeStruct(g.shape, g.dtype) for g in gs], input_output_aliases={k: k for k in range(n)},
        scratch_shapes=[pltpu.SemaphoreType.DMA((n,)), pltpu.SemaphoreType.DMA((n,))],
        name=f"pair_share_{tag}")(*gs)


def small_collect(v, reduce, name):
    rows = v.shape[0]
    flips = [(fx, fy, fc) for fx in (0, 1) for fy in (0, 1) for fc in (0, 1)][1:]

    def body(v_ref, o_ref, buf, send, recv):
        x, y, c, _ = _place()
        buf[4 * x + 2 * y + c] = v_ref[...]
        peers = [(jnp.where(fx, 1 - x, x), jnp.where(fy, 1 - y, y), jnp.where(fc, 1 - c, c)) for fx, fy, fc in flips]
        cps = []
        for k, peer in enumerate(peers):
            cp = pltpu.make_async_remote_copy(
                src_ref=v_ref, dst_ref=buf.at[4 * x + 2 * y + c], send_sem=send.at[k], recv_sem=recv.at[k],
                device_id=peer, device_id_type=MESH)
            cp.start()
            cps.append(cp)
        for k, (px, py, pc) in enumerate(peers):
            pltpu.make_async_remote_copy(
                src_ref=v_ref, dst_ref=buf.at[4 * px + 2 * py + pc], send_sem=send.at[k], recv_sem=recv.at[k],
                device_id=(px, py, pc), device_id_type=MESH).wait_recv()
        for cp in cps:
            cp.wait_send()
        if reduce:
            acc = buf[0]
            for s in range(1, 8):
                acc = acc + buf[s]
            o_ref[...] = acc
        else:
            o_ref[...] = buf[...]

    vm = pl.BlockSpec(memory_space=pltpu.VMEM)
    out_shape = jax.ShapeDtypeStruct((rows, SMALL_COLS) if reduce else (8, rows, SMALL_COLS), F32)
    return pl.pallas_call(
        body, in_specs=[vm], out_specs=vm, out_shape=out_shape,
        scratch_shapes=[pltpu.VMEM((8, rows, SMALL_COLS), F32), pltpu.SemaphoreType.DMA((7,)),
                        pltpu.SemaphoreType.DMA((7,))],
        name=name)(v)


def adamw(w, g, m, v, rb, name):
    nl, rows, cols = w.shape

    def body(w_ref, g_ref, m_ref, v_ref, go_ref, d_ref, nm_ref, nv_ref):
        gv = g_ref[...]
        go_ref[...] = gv
        nm = ADAM_B1 * m_ref[...] + (1.0 - ADAM_B1) * gv
        nv = ADAM_B2 * v_ref[...] + (1.0 - ADAM_B2) * (gv * gv)
        m_hat = nm / (1.0 - ADAM_B1 ** ADAM_STEP)
        v_hat = nv / (1.0 - ADAM_B2 ** ADAM_STEP)
        d_ref[...] = -ADAM_LR * (m_hat / (jnp.sqrt(v_hat) + ADAM_EPS) + ADAM_WD * w_ref[...])
        nm_ref[...] = nm
        nv_ref[...] = nv

    blk = pl.BlockSpec((None, rb, cols), lambda l, r: (l, r, 0))
    shp = jax.ShapeDtypeStruct(w.shape, F32)
    return pl.pallas_call(body, grid=(nl, rows // rb), in_specs=[blk] * 4, out_specs=[blk] * 4, out_shape=[shp] * 4,
                          compiler_params=_cp(("arbitrary", "arbitrary")), name=name)(w, g, m, v)


def _pack(parts, rows):
    flat = jnp.concatenate([p.reshape(-1).astype(F32) for p in parts])
    return jnp.pad(flat, (0, rows * SMALL_COLS - flat.shape[0])).reshape(rows, SMALL_COLS)


def _unpack(vec, shapes):
    flat = vec.reshape(-1)
    out, off = [], 0
    for s in shapes:
        size = 1
        for d in s:
            size *= d
        out.append(flat[off:off + size].reshape(s))
        off += size
    return out


def kernel(x, w_in, w_conv, rel_bias, g_conv_out, g_attn_out, w_out, g_pre_mix, g_post_mix, g_pre_ffn, g_post_ffn, w_ffn_in, w_ffn_out, loss_target, m_w_in, m_w_conv, m_rel_bias, m_g_conv_out, m_g_attn_out, m_w_out, m_g_pre_mix, m_g_post_mix, m_g_pre_ffn, m_g_post_ffn, m_w_ffn_in, m_w_ffn_out, v_w_in, v_w_conv, v_rel_bias, v_g_conv_out, v_g_attn_out, v_w_out, v_g_pre_mix, v_g_post_mix, v_g_pre_ffn, v_g_post_ffn, v_w_ffn_in, v_w_ffn_out):
    xi, yi, ci = lax.axis_index("x"), lax.axis_index("y"), lax.axis_index("c")
    chip = 2 * xi + yi
    nl = w_in.shape[0]
    x0 = x[0]
    target = loss_target[0]
    cwl = CW // NCHIP

    chip1 = chip.reshape(1).astype(jnp.int32)
    own = [cast_to_slot([w_in, w_out, w_ffn_in, w_ffn_out], chip1, l) for l in range(nl)]
    wc_mine = jnp.pad(w_conv.reshape(-1), (0, 16 * LANES - w_conv.size)).reshape(1, 16, LANES)
    wc_slot = lax.dynamic_update_slice_in_dim(jnp.zeros((NCHIP, 16, LANES), F32), wc_mine, chip, axis=0)
    gm = jnp.kron(jnp.eye(CW // HD, dtype=F32), jnp.full((HD, HD), 1.0 / HD, F32)).astype(BF16)
    row = lambda a, l: a[l][None, :]

    def token(t):
        return t[0:1, 0:1]

    def gather_finish(flight, after, tag):
        send, recv, bufs, _ = flight
        return gather_forward(gather_wait(send, recv, bufs, after, tag))

    first_mix = gather_start(list(own[0][:2]) + [wc_slot], x0, "0m")
    first_ffn = gather_start(own[0][2:], first_mix[3], "0f")
    gw_in, gw_out, wc_all = gather_finish(first_mix, x0, "0m")
    wc_full = wc_all.reshape(NCHIP, -1)[:, :nl * cwl * 3].reshape(NCHIP, nl, cwl, 3)
    wc_full = jnp.transpose(wc_full, (1, 0, 2, 3)).reshape(nl, CW, 3)
    wconv_t = jnp.pad(jnp.transpose(wc_full, (0, 2, 1)), ((0, 0), (0, 5), (0, 0)))
    flight = to_sibling = None
    saved, weights = [], []
    h = x0
    for l in range(nl):
        if l == 0:
            pass
        elif l == 1:
            gw_in, gw_out, gw_fi, gw_fo = gather_finish(flight, h, l)
        else:
            gw_in, gw_out, gw_fi, gw_fo = forward_wait(*to_sibling[:3], h, l)
        gw_out = gw_out.reshape(D, D)
        g_pm, g_pf = row(g_pre_mix, l), row(g_pre_ffn, l)
        if l == 0:
            g_pm = g_pm + token(first_ffn[3])
        if l + 1 < nl:
            flight = gather_start(own[l + 1], first_ffn[3] if l == 0 else gw_in, l + 1)
            g_pm = g_pm + token(flight[3])
        bias2, bias2_bwd = bias_expand(_diag_vector(rel_bias[l]), (QG_FWD, QG_BWD))
        proj = fwd_inproj(h, g_pm, gw_in)
        xmid, o, lse, y, z = fwd_mix(h, proj, bias2, wconv_t[l], row(g_conv_out, l), row(g_attn_out, l),
                                     row(g_post_mix, l), gm, gw_out)
        if l == 0:
            gw_fi, gw_fo = gather_finish(first_ffn, xmid, "0f")
        elif l + 1 < nl:
            send, recv, bufs, _ = flight
            to_sibling = forward_start(gather_wait(send, recv, bufs, xmid, l + 1), l + 1)
            g_pf = g_pf + token(to_sibling[3])
        gw_fo = gw_fo.reshape(2, DFF // 2, D)
        gu, f, xout = fwd_ffn(xmid, g_pf, row(g_post_ffn, l), gw_fi, gw_fo)
        saved.append((h, proj, bias2_bwd, xmid, o, lse, y, z, gu, f))
        weights.append((gw_in, gw_out, gw_fi, gw_fo))
        h = xout
    dx, loss_blk = loss_head(h, target)

    core = ci.reshape(1).astype(jnp.int32)
    place = jnp.stack([ci, chip]).astype(jnp.int32)
    totals = [lax.empty(w.shape, F32) for w in (w_in, w_out, w_ffn_in, w_ffn_out)]
    small = {k: [None] * nl for k in ("co", "ao", "pm", "qm", "pf", "qf", "rel", "wc")}

    def reduce_begin(kinds, grads, tag):
        return kinds, exchange_start(grads, tag), tag

    def reduce_mid(state, after):
        kinds, (send, recv, srcs, lands, _), tag = state
        grads, from_sibling = exchange_wait(send, recv, srcs, lands, after, tag)
        return kinds, grads, from_sibling, scatter_start(add_pair(grads, from_sibling, core), tag), tag

    def reduce_end(state, after, totals, layer):
        kinds, grads, from_sibling, (send, recv, srcs, lands, _), tag = state
        from_chips = scatter_wait(send, recv, srcs, lands, after, tag)
        totals = list(totals)
        summed = add_chips(grads, from_sibling, from_chips, place, [totals[i] for i in kinds], layer)
        for i, t in zip(kinds, summed):
            totals[i] = t
        return totals

    begun = flying = None
    for l in reversed(range(nl)):
        hin, proj, bias2, xmid, o, lse, y, z, gu, f = saved[l]
        gw_in, gw_out, gw_fi, gw_fo = weights[l]
        g_qf, g_qm, wct = row(g_post_ffn, l), row(g_post_mix, l), wconv_t[l]
        if begun is not None:
            g_qf = g_qf + token(begun[1][4])
        dxm, dfb, act, dgu, h2, dg_qf, dg_pf = bwd_ffn(dx, f, xmid, gu, row(g_pre_ffn, l), g_qf, gw_fi, gw_fo)
        if begun is not None:
            flying = reduce_mid(begun, dxm)
            g_qm = g_qm + token(flying[3][4])
        gr_fo = wgrad(act, dfb, 256, D, False, "wgrad_ffn_out").reshape(NCHIP, DFF // NCHIP, D)
        gr_fi = wgrad(h2, dgu, 512, 2 * DFF // NCHIP, True, "wgrad_ffn_in")
        if l == 0:
            begun_ffn = reduce_begin([2, 3], [gr_fi, gr_fo], "0f")
            g_qm = g_qm + token(begun_ffn[1][4])
        dzb, do, dco, dbg, dg_qm, dg_co, dg_ao = bwd_mix(dxm, z, o, proj, wct, row(g_conv_out, l),
                                                          row(g_attn_out, l), g_qm, gm, gw_out)
        if l == 0:
            flying_ffn = reduce_mid(begun_ffn, dzb)
            wct = wct + token(flying_ffn[3][4])
        gr_out = wgrad(y, dzb, 512, D, False, "wgrad_out").reshape(NCHIP, D // NCHIP, D)
        dhc, dcg, dwc = bwd_conv(dco, proj, wct)
        dq, dk, dv, db2 = bwd_attn(proj, o, do, lse, bias2)
        dx, dproj, hb, dg_pm = bwd_inproj(dxm, hin, dhc, dbg, dcg, dq, dk, dv, row(g_pre_mix, l), gw_in)
        if flying is not None:
            totals = reduce_end(flying, dx, totals, l + 1)
        gr_in = wgrad(hb, dproj, 512, PROJ // NCHIP, True, "wgrad_in")
        small["co"][l], small["ao"][l], small["pm"][l], small["qm"][l] = dg_co, dg_ao, dg_pm, dg_qm
        small["pf"][l], small["qf"][l] = dg_pf, dg_qf
        small["rel"][l] = _diag_vector_bwd(bias_reduce(db2.reshape(NH, QG_BWD, QG_BWD + LEFT)))
        small["wc"][l] = jnp.transpose(dwc[0:3], (1, 0))
        if l > 0:
            begun = reduce_begin([0, 1, 2, 3], [gr_in, gr_out, gr_fi, gr_fo], l)
    flying_mix = reduce_mid(reduce_begin([0, 1], [gr_in, gr_out], "0m"), dx)
    totals = reduce_end(flying_ffn, flying_mix[3][4], totals, 0)
    gr_fi, gr_fo = pair_share(totals[2:], "ffn")
    big_fi = adamw(w_ffn_in, gr_fi, m_w_ffn_in, v_w_ffn_in, w_ffn_in.shape[1] // 4, "adamw_ffn_in")
    big_fo = adamw(w_ffn_out, gr_fo, m_w_ffn_out, v_w_ffn_out, w_ffn_out.shape[1] // 4, "adamw_ffn_out")
    totals = reduce_end(flying_mix, big_fo[1], totals, 0)
    gr_in, gr_out = pair_share(totals[:2], "mix")
    big_in = adamw(w_in, gr_in, m_w_in, v_w_in, w_in.shape[1] // 4, "adamw_in")
    big_out = adamw(w_out, gr_out, m_w_out, v_w_out, w_out.shape[1] // 4, "adamw_out")
    big = [big_in, big_out, big_fi, big_fo]

    order = ("co", "ao", "pm", "qm", "pf", "qf", "rel", "wc")
    parts = [jnp.stack(small[k]) for k in order] + [loss_blk[0:1, 0:1]]
    shapes = [p.shape for p in parts]
    red = _unpack(small_collect(_pack(parts, 40), True, "reduce_small"), shapes)
    gr_co, gr_ao, gr_pm, gr_qm, gr_pf, gr_qf, gr_rel, gr_wc_full, loss = red
    gr_co, gr_ao, gr_pm, gr_qm, gr_pf, gr_qf = [a.reshape(nl, -1) for a in (gr_co, gr_ao, gr_pm, gr_qm, gr_pf, gr_qf)]
    gr_wc = lax.dynamic_slice_in_dim(gr_wc_full, chip * cwl, cwl, axis=1)
    loss = loss.reshape(())

    sw = [g_conv_out, g_attn_out, g_pre_mix, g_post_mix, g_pre_ffn, g_post_ffn, rel_bias, w_conv]
    sg = [gr_co, gr_ao, gr_pm, gr_qm, gr_pf, gr_qf, gr_rel, gr_wc]
    sm = [m_g_conv_out, m_g_attn_out, m_g_pre_mix, m_g_post_mix, m_g_pre_ffn, m_g_post_ffn, m_rel_bias, m_w_conv]
    sv = [v_g_conv_out, v_g_attn_out, v_g_pre_mix, v_g_post_mix, v_g_pre_ffn, v_g_post_ffn, v_rel_bias, v_w_conv]
    sshapes = [a.shape for a in sw]
    packed = [_pack(a, 32)[None] for a in (sw, sg, sm, sv)]
    s_out = [_unpack(a[0], sshapes) for a in adamw(*packed, 32, "adamw_small")]

    def leaves(big_i, small_i):
        b_in, b_out, b_fi, b_fo = big_i
        s_co, s_ao, s_pm, s_qm, s_pf, s_qf, s_rel, s_wc = small_i
        return [b_in, s_wc, s_rel, s_co, s_ao, b_out, s_pm, s_qm, s_pf, s_qf, b_fi, b_fo]

    out = [loss, dx[None]]
    out += leaves([b[0] for b in big], sg)
    for i in range(1, 4):
        out += leaves([b[i] for b in big], s_out[i])
    return tuple(out)
```

```python
import functools

import jax
import jax.numpy as jnp
from jax import lax
from jax.experimental import pallas as pl
from jax.experimental.pallas import tpu as pltpu

F32 = jnp.float32
BF16 = jnp.bfloat16

D = 1024
PROJ = 3072
CW = 512
HD = 64
NH = 8
CHUNK = 64
BAND = 576
REL_CLIP = 128
NREL = 2 * REL_CLIP + 1
DFF = 2816
DEPTH = 4
NCHIP = 4
EPS = 1e-6
NEG_INF = -1e30

ADAM_LR = 0.001
ADAM_B1 = 0.9
ADAM_B2 = 0.999
ADAM_EPS = 1e-08
ADAM_WD = 0.01
ADAM_STEP = 10

V7X_VMEM_BYTES = 64 * 1024 * 1024
VMEM_LIMIT = V7X_VMEM_BYTES - 8 * 1024 * 1024
LANES = 128
QG_FWD = 4 * CHUNK
QG_BWD = 2 * CHUNK
LEFT = BAND - CHUNK
TQ = 512
TM = 256
SMALL_COLS = 1024
MESH = pl.DeviceIdType.MESH
NT = (((1,), (1,)), ((), ()))
TN = (((0,), (0,)), ((), ()))


def _cp(sem=None, vmem=VMEM_LIMIT):
    return pltpu.CompilerParams(dimension_semantics=sem, vmem_limit_bytes=vmem)


def _any():
    return pl.BlockSpec(memory_space=pl.ANY)


def _const(shape):
    nd = len(shape)
    return pl.BlockSpec(shape, lambda *_: (0,) * nd)


def _rms(v, g):
    r = lax.rsqrt(jnp.mean(v * v, axis=-1, keepdims=True) + EPS)
    return v * r * g


def _rms_bwd(dy, v, g):
    r = lax.rsqrt(jnp.mean(v * v, axis=-1, keepdims=True) + EPS)
    vh = v * r
    dg = jnp.sum(dy * vh, axis=0, keepdims=True)
    dvh = dy * g
    dv = r * (dvh - vh * jnp.mean(dvh * vh, axis=-1, keepdims=True))
    return dv, dg


def _group_mean(v, gm):
    return jnp.dot(v.astype(BF16), gm, preferred_element_type=F32)


def _group_rms_bwd(dy, v, g, gm):
    r = lax.rsqrt(_group_mean(v * v, gm) + EPS)
    vh = v * r
    dg = jnp.sum(dy * vh, axis=0, keepdims=True)
    dvh = dy * g
    dv = r * (dvh - vh * _group_mean(dvh * vh, gm))
    return dv, dg


def _head_masks(scale):
    lane = lax.broadcasted_iota(jnp.int32, (1, LANES), 1)
    return [jnp.where((lane >= HD * a) & (lane < HD * (a + 1)), scale, 0.0).astype(BF16) for a in range(2)]


class _Resident:
    def __init__(self, src, dst, sem):
        self.first = pl.program_id(0) == 0
        self.copy = pltpu.make_async_copy(src, dst, sem)
        self.dst = dst

        @pl.when(self.first)
        def _():
            self.copy.start()

    def read(self):
        @pl.when(self.first)
        def _():
            self.copy.wait()

        return self.dst[...]


FF_CHUNKS = ((0, 1536), (1536, DFF))


def _stream_ffn_weights(wfi_hbm, wfo_hbm, wfi_v, wfo_v, sems, order, step):
    hw = DFF // 2
    per_matrix = {
        0: [(wfi_hbm.at[j], wfi_v.at[0, :, pl.ds(hw * j, hw)]) for j in range(2)],
        1: [(wfi_hbm.at[2 + j], wfi_v.at[1, :, pl.ds(hw * j, hw)]) for j in range(2)],
        2: [(wfo_hbm.at[j], wfo_v.at[pl.ds(hw * j, hw), :]) for j in range(2)],
    }
    pieces = [p for m in order for p in per_matrix[m]]
    slot = {m: 2 * k for k, m in enumerate(order)}

    def make_step(wait):
        def ready(m, chunk):
            if chunk == 0:
                wait(slot[m])
                wait(slot[m] + 1)
        return lambda: step(ready)

    copies = [pltpu.make_async_copy(src, dst, sems.at[k]) for k, (src, dst) in enumerate(pieces)]
    first = pl.program_id(0) == 0

    @pl.when(first)
    def _():
        for cp in copies:
            cp.start()
        make_step(lambda k: copies[k].wait())()

    @pl.when(jnp.logical_not(first))
    def _():
        make_step(lambda k: None)()


def _conv_taps(u_prev, u, scr):
    n = u.shape[0]
    scr[0:16, :] = u_prev
    scr[16:16 + n, :] = u
    return scr[15:15 + n, :], scr[14:14 + n, :]


def fwd_inproj(x, g, w_all):
    t = x.shape[0]
    wc = PROJ // NCHIP

    def body(x_ref, g_ref, w_hbm, o_ref, w_v):
        @pl.when(pl.program_id(0) == 0)
        def _():
            pltpu.sync_copy(w_hbm, w_v)

        h = _rms(x_ref[...], g_ref[...]).astype(BF16)
        for b in range(NCHIP):
            o_ref[:, wc * b:wc * (b + 1)] = jnp.dot(h, w_v[b], preferred_element_type=F32).astype(BF16)

    return pl.pallas_call(
        body, grid=(t // TQ,),
        in_specs=[pl.BlockSpec((TQ, D), lambda i: (i, 0)), _const((1, D)), _any()],
        out_specs=pl.BlockSpec((TQ, PROJ), lambda i: (i, 0)),
        out_shape=jax.ShapeDtypeStruct((t, PROJ), BF16),
        scratch_shapes=[pltpu.VMEM((NCHIP, D, wc), BF16)],
        compiler_params=_cp(("arbitrary",)), name="fwd_inproj")(x, g, w_all)


def _attn_window_specs():
    return [
        pl.BlockSpec((TQ, CW), lambda i: (i, 3)),
        pl.BlockSpec((TQ, CW), lambda i: (jnp.maximum(i - 1, 0), 4)),
        pl.BlockSpec((TQ, CW), lambda i: (i, 4)),
        pl.BlockSpec((TQ, CW), lambda i: (jnp.maximum(i - 1, 0), 5)),
        pl.BlockSpec((TQ, CW), lambda i: (i, 5)),
    ]


def _conv_specs():
    return [
        pl.BlockSpec((TQ, 3 * CW), lambda i: (i, 0)),
        pl.BlockSpec((16, 3 * CW), lambda i: (jnp.maximum(i * (TQ // 16) - 1, 0), 0)),
    ]


def _conv_fwd(pc_ref, pcp_ref, wc_ref, scr, first):
    pc = pc_ref[...].astype(F32)
    hc, bg, cg = pc[:, :CW], pc[:, CW:2 * CW], pc[:, 2 * CW:]
    u = cg * hc
    pp = pcp_ref[...].astype(F32)
    u_prev = jnp.where(first, 0.0, pp[:, 2 * CW:] * pp[:, :CW])
    u1, u2 = _conv_taps(u_prev, u, scr)
    cout = wc_ref[0:1, :] * u2 + wc_ref[1:2, :] * u1 + wc_ref[2:3, :] * u
    return hc, bg, cg, u, u1, u2, cout


def _key_penalty(first, r0, kg):
    col = lax.broadcasted_iota(jnp.int32, (1, kg), 1)
    limit = jnp.where(first, TQ - r0, 0)
    return jnp.where(col < limit, NEG_INF, 0.0)


def fwd_mix(x, proj, bias2, wconv_t, g_co, g_ao, g_pm, gm, wout_all):
    t = x.shape[0]
    qg, kg = QG_FWD, QG_FWD + LEFT

    def body(x_ref, pc_ref, pcp_ref, q_ref, kp_ref, kc_ref, vp_ref, vc_ref, b2_ref, wc_ref, gco_ref, gao_ref, gpm_ref,
             gm_ref, wout_hbm, xmid_ref, o_ref, lse_ref, y_ref, z_ref, wout_v, kwin, vwin, cscr, sems):
        i = pl.program_id(0)
        first = i == 0
        wout = _Resident(wout_hbm, wout_v, sems.at[0])
        kwin[0:TQ, :] = kp_ref[...]
        kwin[TQ:2 * TQ, :] = kc_ref[...]
        vwin[0:TQ, :] = vp_ref[...]
        vwin[TQ:2 * TQ, :] = vc_ref[...]
        qmask = _head_masks(HD ** -0.5)
        low = lax.broadcasted_iota(jnp.int32, (1, LANES), 1) < HD

        def group(g, carry):
            r0 = pl.multiple_of(g * qg, qg)
            pen = _key_penalty(first, r0, kg)
            for hp in range(NH // 2):
                ls = slice(LANES * hp, LANES * (hp + 1))
                qb = q_ref[pl.ds(r0, qg), ls]
                q2 = jnp.concatenate([qb * qmask[0], qb * qmask[1]], axis=0)
                s = lax.dot_general(q2, kwin[pl.ds(r0, kg), ls], NT, preferred_element_type=F32)
                s = s + b2_ref[hp] + pen
                m = jnp.max(s, axis=-1, keepdims=True)
                p = jnp.exp(s - m)
                l = jnp.sum(p, axis=-1, keepdims=True)
                o2 = jnp.dot(p.astype(BF16), vwin[pl.ds(r0, kg), ls], preferred_element_type=F32) * (1.0 / l)
                lse2 = m + jnp.log(l)
                o_ref[pl.ds(r0, qg), ls] = jnp.where(low, o2[:qg], o2[qg:])
                lse_ref[pl.ds(r0, qg), ls] = jnp.where(low, lse2[:qg], lse2[qg:])
            return carry

        lax.fori_loop(0, TQ // qg, group, 0)

        _, bg, _, _, _, _, cout = _conv_fwd(pc_ref, pcp_ref, wc_ref, cscr, first)
        yc = bg * cout
        gmv = gm_ref[...]
        ycn = yc * lax.rsqrt(_group_mean(yc * yc, gmv) + EPS) * gco_ref[...]
        oa = o_ref[...]
        oan = oa * lax.rsqrt(_group_mean(oa * oa, gmv) + EPS) * gao_ref[...]
        y_ref[:, 0:CW] = ycn.astype(BF16)
        y_ref[:, CW:2 * CW] = oan.astype(BF16)
        z = jnp.dot(y_ref[...], wout.read(), preferred_element_type=F32)
        z_ref[...] = z
        xmid_ref[...] = x_ref[...] + _rms(z, gpm_ref[...])

    row = lambda w: pl.BlockSpec((TQ, w), lambda i: (i, 0))
    return pl.pallas_call(
        body, grid=(t // TQ,),
        in_specs=[row(D)] + _conv_specs() + _attn_window_specs() + [
            _const((NH // 2, 2 * qg, kg)), _const((8, CW)), _const((1, CW)), _const((1, CW)), _const((1, D)),
            _const((CW, CW)), _any()],
        out_specs=[row(D), row(CW), row(CW), row(D), row(D)],
        out_shape=[jax.ShapeDtypeStruct((t, D), F32), jax.ShapeDtypeStruct((t, CW), F32),
                   jax.ShapeDtypeStruct((t, CW), F32), jax.ShapeDtypeStruct((t, D), BF16),
                   jax.ShapeDtypeStruct((t, D), F32)],
        scratch_shapes=[pltpu.VMEM((D, D), BF16), pltpu.VMEM((2 * TQ, CW), BF16), pltpu.VMEM((2 * TQ, CW), BF16),
                        pltpu.VMEM((TQ + 16, CW), F32), pltpu.SemaphoreType.DMA((1,))],
        compiler_params=_cp(("arbitrary",)), name="fwd_mix",
    )(x, proj, proj, proj, proj, proj, proj, proj, bias2, wconv_t, g_co, g_ao, g_pm, gm, wout_all)


def fwd_ffn(xmid, g_pre, g_post, wfi_all, wfo_all):
    t = xmid.shape[0]

    def body(x_ref, gpre_ref, gpost_ref, wfi_hbm, wfo_hbm, gu_ref, f_ref, xo_ref, wfi_v, wfo_v, sems):
        def step(ready):
            xv = x_ref[...]
            h = _rms(xv, gpre_ref[...]).astype(BF16)
            f = jnp.zeros((TM, D), F32)
            for ci, (a, b) in enumerate(FF_CHUNKS):
                ready(0, ci)
                gate = jnp.dot(h, wfi_v[0, :, a:b], preferred_element_type=F32)
                ready(1, ci)
                up = jnp.dot(h, wfi_v[1, :, a:b], preferred_element_type=F32)
                gu_ref[:, a:b] = gate.astype(BF16)
                gu_ref[:, DFF + a:DFF + b] = up.astype(BF16)
                act = gate * (1.0 / (1.0 + jnp.exp(-gate))) * up
                ready(2, ci)
                f = f + jnp.dot(act.astype(BF16), wfo_v[a:b, :], preferred_element_type=F32)
            f_ref[...] = f
            xo_ref[...] = xv + _rms(f, gpost_ref[...])

        _stream_ffn_weights(wfi_hbm, wfo_hbm, wfi_v, wfo_v, sems, (0, 1, 2), step)

    row = lambda w: pl.BlockSpec((TM, w), lambda i: (i, 0))
    return pl.pallas_call(
        body, grid=(t // TM,),
        in_specs=[row(D), _const((1, D)), _const((1, D)), _any(), _any()],
        out_specs=[row(2 * DFF), row(D), row(D)],
        out_shape=[jax.ShapeDtypeStruct((t, 2 * DFF), BF16), jax.ShapeDtypeStruct((t, D), F32),
                   jax.ShapeDtypeStruct((t, D), F32)],
        scratch_shapes=[pltpu.VMEM((2, D, DFF), BF16), pltpu.VMEM((DFF, D), BF16), pltpu.SemaphoreType.DMA((6,))],
        compiler_params=_cp(("arbitrary",)), name="fwd_ffn")(xmid, g_pre, g_post, wfi_all, wfo_all)


def loss_head(y, target):
    t = y.shape[0]

    def body(y_ref, t_ref, dy_ref, l_ref):
        @pl.when(pl.program_id(0) == 0)
        def _():
            l_ref[...] = jnp.zeros_like(l_ref)

        e = y_ref[...] - t_ref[...]
        dy_ref[...] = e * (1.0 / D)
        rows = jnp.sum(e * e, axis=-1, keepdims=True) * (1.0 / D)
        l_ref[...] += 0.5 * jnp.sum(rows, axis=0, keepdims=True)

    row = pl.BlockSpec((TQ, D), lambda i: (i, 0))
    return pl.pallas_call(
        body, grid=(t // TQ,), in_specs=[row, row], out_specs=[row, _const((8, LANES))],
        out_shape=[jax.ShapeDtypeStruct((t, D), F32), jax.ShapeDtypeStruct((8, LANES), F32)],
        compiler_params=_cp(("arbitrary",)), name="loss_head")(y, target)


def bwd_ffn(dx, f, xmid, gu, g_pre, g_post, wfi_all, wfo_all):
    t = dx.shape[0]
    hw = DFF // 2

    def body(dx_ref, f_ref, x_ref, gu_ref, gpre_ref, gpost_ref, wfi_hbm, wfo_hbm,
             dxm_ref, df_ref, act_ref, dgu_ref, h_ref, dgpost_ref, dgpre_ref, wfi_v, wfo_v, sems):
        @pl.when(pl.program_id(0) == 0)
        def _():
            dgpost_ref[...] = jnp.zeros_like(dgpost_ref)
            dgpre_ref[...] = jnp.zeros_like(dgpre_ref)

        def step(ready):
            dxo = dx_ref[...]
            df, dgp = _rms_bwd(dxo, f_ref[...], gpost_ref[...])
            dgpost_ref[...] += dgp
            dfb = df.astype(BF16)
            df_ref[...] = dfb
            dh = jnp.zeros((TM, D), F32)
            for ci, (a, b) in enumerate(FF_CHUNKS):
                ready(2, ci)
                dact = lax.dot_general(dfb, wfo_v[a:b, :], NT, preferred_element_type=F32)
                gate = gu_ref[:, a:b].astype(F32)
                up = gu_ref[:, DFF + a:DFF + b].astype(F32)
                sig = 1.0 / (1.0 + jnp.exp(-gate))
                silu = gate * sig
                act_ref[:, a:b] = (silu * up).astype(BF16)
                dup = (dact * silu).astype(BF16)
                dgate = (dact * up * (sig * (1.0 + gate * (1.0 - sig)))).astype(BF16)
                dgu_ref[:, a:b] = dgate
                dgu_ref[:, DFF + a:DFF + b] = dup
                ready(0, ci)
                dh = dh + lax.dot_general(dgate, wfi_v[0, :, a:b], NT, preferred_element_type=F32)
                ready(1, ci)
                dh = dh + lax.dot_general(dup, wfi_v[1, :, a:b], NT, preferred_element_type=F32)
            xv = x_ref[...]
            gpre = gpre_ref[...]
            h_ref[...] = _rms(xv, gpre).astype(BF16)
            dxv, dgq = _rms_bwd(dh, xv, gpre)
            dgpre_ref[...] += dgq
            dxm_ref[...] = dxo + dxv

        _stream_ffn_weights(wfi_hbm, wfo_hbm, wfi_v, wfo_v, sems, (2, 0, 1), step)

    row = lambda w: pl.BlockSpec((TM, w), lambda i: (i, 0))
    return pl.pallas_call(
        body, grid=(t // TM,),
        in_specs=[row(D), row(D), row(D), row(2 * DFF), _const((1, D)), _const((1, D)), _any(), _any()],
        out_specs=[row(D), row(D), row(DFF), row(2 * DFF), row(D), _const((1, D)), _const((1, D))],
        out_shape=[jax.ShapeDtypeStruct((t, D), F32), jax.ShapeDtypeStruct((t, D), BF16),
                   jax.ShapeDtypeStruct((t, DFF), BF16), jax.ShapeDtypeStruct((t, 2 * DFF), BF16),
                   jax.ShapeDtypeStruct((t, D), BF16), jax.ShapeDtypeStruct((1, D), F32),
                   jax.ShapeDtypeStruct((1, D), F32)],
        scratch_shapes=[pltpu.VMEM((2, D, DFF), BF16), pltpu.VMEM((DFF, D), BF16), pltpu.SemaphoreType.DMA((6,))],
        compiler_params=_cp(("arbitrary",)), name="bwd_ffn")(dx, f, xmid, gu, g_pre, g_post, wfi_all, wfo_all)


def bwd_mix(dxm, z, o, proj, wconv_t, g_co, g_ao, g_pm, gm, wout_all):
    t = dxm.shape[0]

    def body(dx_ref, z_ref, o_ref, pc_ref, pcp_ref, wc_ref, gco_ref, gao_ref, gpm_ref, gm_ref, wout_hbm,
             dz_ref, do_ref, dco_ref, dbg_ref, dgpm_ref, dgco_ref, dgao_ref, wout_v, cscr):
        first = pl.program_id(0) == 0

        @pl.when(first)
        def _():
            pltpu.sync_copy(wout_hbm, wout_v)
            dgpm_ref[...] = jnp.zeros_like(dgpm_ref)
            dgco_ref[...] = jnp.zeros_like(dgco_ref)
            dgao_ref[...] = jnp.zeros_like(dgao_ref)

        dz, dgp = _rms_bwd(dx_ref[...], z_ref[...], gpm_ref[...])
        dgpm_ref[...] += dgp
        dzb = dz.astype(BF16)
        dz_ref[...] = dzb
        gmv = gm_ref[...]
        _, bg, _, _, _, _, cout = _conv_fwd(pc_ref, pcp_ref, wc_ref, cscr, first)
        dy_conv = lax.dot_general(dzb, wout_v[0:CW, :], NT, preferred_element_type=F32)
        dyc, dgc = _group_rms_bwd(dy_conv, bg * cout, gco_ref[...], gmv)
        dgco_ref[...] += dgc
        dbg_ref[...] = (dyc * cout).astype(BF16)
        dco_ref[...] = dyc * bg
        dy_attn = lax.dot_general(dzb, wout_v[CW:2 * CW, :], NT, preferred_element_type=F32)
        do, dga = _group_rms_bwd(dy_attn, o_ref[...], gao_ref[...], gmv)
        dgao_ref[...] += dga
        do_ref[...] = do.astype(BF16)

    row = lambda w: pl.BlockSpec((TQ, w), lambda i: (i, 0))
    return pl.pallas_call(
        body, grid=(t // TQ,),
        in_specs=[row(D), row(D), row(CW)] + _conv_specs() + [
            _const((8, CW)), _const((1, CW)), _const((1, CW)), _const((1, D)), _const((CW, CW)), _any()],
        out_specs=[row(D), row(CW), row(CW), row(CW), _const((1, D)), _const((1, CW)), _const((1, CW))],
        out_shape=[jax.ShapeDtypeStruct((t, D), BF16), jax.ShapeDtypeStruct((t, CW), BF16),
                   jax.ShapeDtypeStruct((t, CW), F32), jax.ShapeDtypeStruct((t, CW), BF16),
                   jax.ShapeDtypeStruct((1, D), F32), jax.ShapeDtypeStruct((1, CW), F32),
                   jax.ShapeDtypeStruct((1, CW), F32)],
        scratch_shapes=[pltpu.VMEM((D, D), BF16), pltpu.VMEM((TQ + 16, CW), F32)],
        compiler_params=_cp(("arbitrary",)), name="bwd_mix",
    )(dxm, z, o, proj, proj, wconv_t, g_co, g_ao, g_pm, gm, wout_all)


def bwd_conv(dco, proj, wconv_t):
    t = dco.shape[0]
    nt = t // TQ

    def body(d_ref, dn_ref, pc_ref, pcp_ref, wc_ref, dhc_ref, dcg_ref, dw_ref, cscr, dscr):
        i = pl.program_id(0)
        first = i == 0

        @pl.when(first)
        def _():
            dw_ref[...] = jnp.zeros_like(dw_ref)

        hc, _, cg, u, u1, u2, _ = _conv_fwd(pc_ref, pcp_ref, wc_ref, cscr, first)
        d0 = d_ref[...]
        dscr[0:TQ, :] = d0
        dscr[TQ:TQ + 8, :] = jnp.where(i == nt - 1, 0.0, dn_ref[...])
        d1 = dscr[1:TQ + 1, :]
        d2 = dscr[2:TQ + 2, :]
        du = wc_ref[2:3, :] * d0 + wc_ref[1:2, :] * d1 + wc_ref[0:1, :] * d2
        dhc_ref[...] = (du * cg).astype(BF16)
        dcg_ref[...] = (du * hc).astype(BF16)
        dw_ref[0:1, :] += jnp.sum(d0 * u2, axis=0, keepdims=True)
        dw_ref[1:2, :] += jnp.sum(d0 * u1, axis=0, keepdims=True)
        dw_ref[2:3, :] += jnp.sum(d0 * u, axis=0, keepdims=True)

    row = lambda w: pl.BlockSpec((TQ, w), lambda i: (i, 0))
    nxt = pl.BlockSpec((8, CW), lambda i: (jnp.minimum((i + 1) * (TQ // 8), t // 8 - 1), 0))
    return pl.pallas_call(
        body, grid=(nt,),
        in_specs=[row(CW), nxt] + _conv_specs() + [_const((8, CW))],
        out_specs=[row(CW), row(CW), _const((8, CW))],
        out_shape=[jax.ShapeDtypeStruct((t, CW), BF16), jax.ShapeDtypeStruct((t, CW), BF16),
                   jax.ShapeDtypeStruct((8, CW), F32)],
        scratch_shapes=[pltpu.VMEM((TQ + 16, CW), F32), pltpu.VMEM((TQ + 8, CW), F32)],
        compiler_params=_cp(("arbitrary",)), name="bwd_conv")(dco, dco, proj, proj, wconv_t)


def bwd_attn(proj, o, do, lse, bias2):
    t = o.shape[0]
    nt = t // TQ
    qg, kg = QG_BWD, QG_BWD + LEFT

    def body(q_ref, kp_ref, kc_ref, vp_ref, vc_ref, o_ref, do_ref, lse_ref, b2_ref,
             dq_ref, dk_hbm, dv_hbm, db_hbm, kwin, vwin, dk_acc, dv_acc, db_acc):
        i = pl.program_id(0)
        first = i == 0

        @pl.when(first)
        def _():
            dk_acc[...] = jnp.zeros_like(dk_acc)
            dv_acc[...] = jnp.zeros_like(dv_acc)
            db_acc[...] = jnp.zeros_like(db_acc)

        kwin[0:TQ, :] = kp_ref[...]
        kwin[TQ:2 * TQ, :] = kc_ref[...]
        vwin[0:TQ, :] = vp_ref[...]
        vwin[TQ:2 * TQ, :] = vc_ref[...]
        scale = HD ** -0.5
        qmask = _head_masks(scale)
        vmask = _head_masks(1.0)
        low = lax.broadcasted_iota(jnp.int32, (1, LANES), 1) < HD

        def group(g, carry):
            r0 = pl.multiple_of(g * qg, qg)
            base = pl.multiple_of(i * TQ + r0, qg)
            pen = _key_penalty(first, r0, kg)
            for hp in range(NH // 2):
                ls = slice(LANES * hp, LANES * (hp + 1))
                qb = q_ref[pl.ds(r0, qg), ls]
                kw = kwin[pl.ds(r0, kg), ls]
                dob = do_ref[pl.ds(r0, qg), ls]
                prod = dob.astype(F32) * o_ref[pl.ds(r0, qg), ls]
                lseb = lse_ref[pl.ds(r0, qg), ls]
                q2 = jnp.concatenate([qb * qmask[0], qb * qmask[1]], axis=0)
                do2 = jnp.concatenate([dob * vmask[0], dob * vmask[1]], axis=0)
                lse2 = jnp.concatenate([lseb[:, 0:1], lseb[:, HD:HD + 1]], axis=0)
                dsum = jnp.concatenate([jnp.sum(jnp.where(low, prod, 0.0), axis=-1, keepdims=True),
                                        jnp.sum(jnp.where(low, 0.0, prod), axis=-1, keepdims=True)], axis=0)
                s = lax.dot_general(q2, kw, NT, preferred_element_type=F32) + b2_ref[hp] + pen
                p = jnp.exp(s - lse2)
                dp = lax.dot_general(do2, vwin[pl.ds(r0, kg), ls], NT, preferred_element_type=F32)
                ds = p * (dp - dsum)
                db_acc[hp] += ds
                dsb = ds.astype(BF16)
                dq2 = jnp.dot(dsb, kw, preferred_element_type=F32)
                dq_ref[pl.ds(r0, qg), ls] = (jnp.where(low, dq2[:qg], dq2[qg:]) * scale).astype(BF16)
                dk_acc[pl.ds(base, kg), ls] += lax.dot_general(dsb, q2, TN, preferred_element_type=F32)
                dv_acc[pl.ds(base, kg), ls] += lax.dot_general(p.astype(BF16), do2, TN, preferred_element_type=F32)
            return carry

        lax.fori_loop(0, TQ // qg, group, 0)

        @pl.when(i == nt - 1)
        def _():
            pltpu.sync_copy(dk_acc, dk_hbm)
            pltpu.sync_copy(dv_acc, dv_hbm)
            pltpu.sync_copy(db_acc, db_hbm)

    row = lambda w: pl.BlockSpec((TQ, w), lambda i: (i, 0))
    return pl.pallas_call(
        body, grid=(nt,),
        in_specs=_attn_window_specs() + [row(CW), row(CW), row(CW), _const((NH // 2, 2 * qg, kg))],
        out_specs=[row(CW), _any(), _any(), _any()],
        out_shape=[jax.ShapeDtypeStruct((t, CW), BF16), jax.ShapeDtypeStruct((t + TQ, CW), F32),
                   jax.ShapeDtypeStruct((t + TQ, CW), F32), jax.ShapeDtypeStruct((NH // 2, 2 * qg, kg), F32)],
        scratch_shapes=[pltpu.VMEM((2 * TQ, CW), BF16), pltpu.VMEM((2 * TQ, CW), BF16),
                        pltpu.VMEM((t + TQ, CW), F32), pltpu.VMEM((t + TQ, CW), F32),
                        pltpu.VMEM((NH // 2, 2 * qg, kg), F32)],
        compiler_params=_cp(("arbitrary",)), name="bwd_attn",
    )(proj, proj, proj, proj, proj, o, do, lse, bias2)


def bwd_inproj(dxm, x, dhc, dbg, dcg, dq, dk, dv, g, w_all):
    t = x.shape[0]
    wc = PROJ // NCHIP

    def body(dxm_ref, x_ref, dhc_ref, dbg_ref, dcg_ref, dq_ref, dk_ref, dv_ref, g_ref, w_hbm,
             dx_ref, dp_ref, h_ref, dg_ref, w_v):
        @pl.when(pl.program_id(0) == 0)
        def _():
            pltpu.sync_copy(w_hbm, w_v)
            dg_ref[...] = jnp.zeros_like(dg_ref)

        dp_ref[:, 0:CW] = dhc_ref[...]
        dp_ref[:, CW:2 * CW] = dbg_ref[...]
        dp_ref[:, 2 * CW:3 * CW] = dcg_ref[...]
        dp_ref[:, 3 * CW:4 * CW] = dq_ref[...]
        dp_ref[:, 4 * CW:5 * CW] = dk_ref[...].astype(BF16)
        dp_ref[:, 5 * CW:6 * CW] = dv_ref[...].astype(BF16)
        dh = jnp.zeros((TQ, D), F32)
        for b in range(NCHIP):
            dh = dh + lax.dot_general(dp_ref[:, wc * b:wc * (b + 1)], w_v[b], NT, preferred_element_type=F32)
        xv = x_ref[...]
        gv = g_ref[...]
        h_ref[...] = _rms(xv, gv).astype(BF16)
        dxv, dgv = _rms_bwd(dh, xv, gv)
        dg_ref[...] += dgv
        dx_ref[...] = dxm_ref[...] + dxv

    row = lambda w: pl.BlockSpec((TQ, w), lambda i: (i, 0))
    pad = pl.BlockSpec((TQ, CW), lambda i: (i + 1, 0))
    return pl.pallas_call(
        body, grid=(t // TQ,),
        in_specs=[row(D), row(D), row(CW), row(CW), row(CW), row(CW), pad, pad, _const((1, D)), _any()],
        out_specs=[row(D), row(PROJ), row(D), _const((1, D))],
        out_shape=[jax.ShapeDtypeStruct((t, D), F32), jax.ShapeDtypeStruct((t, PROJ), BF16),
                   jax.ShapeDtypeStruct((t, D), BF16), jax.ShapeDtypeStruct((1, D), F32)],
        scratch_shapes=[pltpu.VMEM((NCHIP, D, wc), BF16)],
        compiler_params=_cp(("arbitrary",)), name="bwd_inproj",
    )(dxm, x, dhc, dbg, dcg, dq, dk, dv, g, w_all)


def wgrad(a, b, kb, nb, by_columns, name):
    t, k = a.shape
    n = b.shape[1]
    tk = 512

    def body(a_ref, b_ref, o_ref):
        o_ref[...] = jnp.zeros_like(o_ref)
        for c in range(t // tk):
            o_ref[...] += lax.dot_general(a_ref[tk * c:tk * (c + 1), :], b_ref[tk * c:tk * (c + 1), :], TN,
                                          preferred_element_type=F32)

    if by_columns:
        assert nb == n // NCHIP
        out_spec = pl.BlockSpec((None, kb, nb), lambda ki, ni: (ni, ki, 0))
        out_shape = jax.ShapeDtypeStruct((NCHIP, k, nb), F32)
    else:
        assert nb == n
        out_spec = pl.BlockSpec((kb, nb), lambda ki, ni: (ki, 0))
        out_shape = jax.ShapeDtypeStruct((k, n), F32)
    return pl.pallas_call(
        body, grid=(k // kb, n // nb),
        in_specs=[pl.BlockSpec((t, kb), lambda ki, ni: (0, ki)), pl.BlockSpec((t, nb), lambda ki, ni: (0, ni))],
        out_specs=out_spec, out_shape=out_shape,
        compiler_params=_cp(("arbitrary", "arbitrary")), name=name)(a, b)


TOE = 1024
assert 2 * QG_FWD + LEFT <= TOE
N_FLAT = LEFT - REL_CLIP + 1
N_VAR = BAND - N_FLAT


def _diag_vector(table):
    last = table[:, 2 * REL_CLIP:]
    var = table[:, 2 * REL_CLIP - N_VAR:2 * REL_CLIP][:, ::-1]
    return jnp.concatenate([jnp.broadcast_to(last, (NH, N_FLAT)), var, jnp.broadcast_to(last, (NH, TOE - BAND))], axis=1)


def _diag_vector_bwd(dvec):
    dlast = jnp.sum(dvec[:, :N_FLAT], axis=1, keepdims=True) + jnp.sum(dvec[:, BAND:], axis=1, keepdims=True)
    dvar = dvec[:, N_FLAT:BAND][:, ::-1]
    return jnp.concatenate([jnp.zeros((NH, 2 * REL_CLIP - N_VAR), F32), dvar, dlast], axis=1)


def _band_valid(qg):
    r = lax.broadcasted_iota(jnp.int32, (qg, qg + LEFT), 0)
    p = lax.broadcasted_iota(jnp.int32, (qg, qg + LEFT), 1)
    start = lax.shift_left(lax.shift_right_logical(r, 6), 6)
    return (p >= start) & (p < start + BAND)


def bias_expand(vec, qgs):
    def body(v_ref, *o_refs):
        for qg, o_ref in zip(qgs, o_refs):
            valid = _band_valid(qg)
            for h in range(NH):
                rows = jnp.broadcast_to(v_ref[h:h + 1, :], (qg, TOE))
                toe = pltpu.roll(rows, 0, 1, stride=1, stride_axis=0)
                o_ref[h // 2, qg * (h % 2):qg * (h % 2 + 1), :] = jnp.where(valid, toe[:, :qg + LEFT], NEG_INF)

    return pl.pallas_call(body, out_shape=[jax.ShapeDtypeStruct((NH // 2, 2 * qg, qg + LEFT), F32) for qg in qgs],
                          name="bias_expand")(vec)


def bias_reduce(db2):
    _, qg, kg = db2.shape

    def body(d_ref, o_ref):
        ii = lax.broadcasted_iota(jnp.int32, (kg, kg), 0)
        jj = lax.broadcasted_iota(jnp.int32, (kg, kg), 1)
        flip = jnp.where(ii + jj == kg - 1, 1.0, 0.0).astype(BF16)
        for h in range(NH):
            rest = d_ref[h]
            rev = jnp.zeros((qg, kg), F32)
            for _ in range(3):
                term = rest.astype(BF16)
                rev = rev + jnp.dot(term, flip, preferred_element_type=F32)
                rest = rest - term.astype(F32)
            d = jnp.concatenate([jnp.zeros((qg, TOE - kg), F32), rev], axis=1)
            back = pltpu.roll(d, 0, 1, stride=1, stride_axis=0)
            o_ref[h:h + 1, :] = jnp.sum(back, axis=0, keepdims=True)

    rev = pl.pallas_call(body, out_shape=jax.ShapeDtypeStruct((NH, TOE), F32), name="bias_reduce")(db2)
    return rev[:, ::-1]


def _place():
    x, y, c = lax.axis_index("x"), lax.axis_index("y"), lax.axis_index("c")
    chips = [(1 - x, y), (x, 1 - y), (1 - x, 1 - y)]
    return x, y, c, chips


def _half(ref_rows, c):
    return pl.ds(c * (ref_rows // 2), ref_rows // 2)


HBM_SPEC = pl.BlockSpec(memory_space=pltpu.HBM)
SEM_SPEC = pl.BlockSpec(memory_space=pltpu.SEMAPHORE)
IN_FLIGHT = pltpu.CompilerParams(has_side_effects=pltpu.SideEffectType.DATAFLOW_SIDE_EFFECTING)


def _in_hbm(a):
    return pltpu.with_memory_space_constraint(a, pltpu.HBM)


def cast_to_slot(ws, chip, layer):
    n = len(ws)
    steps = 4

    def body(b_ref, *refs):
        del b_ref
        for w_ref, o_ref in zip(refs[:n], refs[n:]):
            o_ref[...] = w_ref[...].astype(BF16)

    grid_spec = pltpu.PrefetchScalarGridSpec(
        num_scalar_prefetch=1, grid=(steps,),
        in_specs=[pl.BlockSpec((None, w.shape[1] // steps, w.shape[2]), lambda r, b: (layer, r, 0)) for w in ws],
        out_specs=[pl.BlockSpec((None, w.shape[1] // steps, w.shape[2]), lambda r, b: (b[0], r, 0)) for w in ws])
    return pl.pallas_call(body, grid_spec=grid_spec,
                          out_shape=[jax.ShapeDtypeStruct((NCHIP,) + w.shape[1:], BF16) for w in ws],
                          compiler_params=_cp(("arbitrary",)), name="cast_to_slot")(chip, *ws)


def _gather_copies(bufs, send, recv):
    x, y, c, chips = _place()
    b = 2 * x + y
    out = []
    for k, buf in enumerate(bufs):
        rows = buf.shape[1]
        mine = buf.at[b, _half(rows, c), :]
        for j, (cx, cy) in enumerate(chips):
            theirs = buf.at[2 * cx + cy, _half(rows, c), :]
            sems = dict(send_sem=send.at[3 * k + j], recv_sem=recv.at[3 * k + j],
                        device_id=(cx, cy, c), device_id_type=MESH)
            out.append((pltpu.make_async_remote_copy(src_ref=mine, dst_ref=mine, **sems),
                        pltpu.make_async_remote_copy(src_ref=theirs, dst_ref=theirs, **sems)))
    return out


def gather_start(bufs, after, layer):
    n = len(bufs)

    def body(*refs):
        ins = refs[:n]
        send, recv = refs[n + 1], refs[n + 2]
        token = refs[-1]
        for start, _ in _gather_copies(ins, send, recv):
            start.start()
        token[...] = jnp.zeros_like(token)

    sems = pltpu.SemaphoreType.DMA((3 * n,))
    res = pl.pallas_call(
        body, name=f"gather_start_{layer}",
        in_specs=[HBM_SPEC] * n + [_any()],
        out_specs=[SEM_SPEC, SEM_SPEC] + [HBM_SPEC] * n + [pl.BlockSpec(memory_space=pltpu.VMEM)],
        out_shape=[sems, sems] + [pltpu.HBM(b.shape, b.dtype) for b in bufs] + [jax.ShapeDtypeStruct((8, LANES), F32)],
        input_output_aliases={k: 2 + k for k in range(n)}, compiler_params=IN_FLIGHT,
    )(*[_in_hbm(b) for b in bufs], after)
    return res[0], res[1], res[2:2 + n], res[-1]


def gather_wait(send, recv, bufs, after, layer):
    n = len(bufs)

    def body(*refs):
        ins = refs[:n]
        send_ref, recv_ref = refs[n], refs[n + 1]
        for start, arrival in _gather_copies(ins, send_ref, recv_ref):
            start.wait_send()
            arrival.wait_recv()

    return pl.pallas_call(
        body, name=f"gather_wait_{layer}",
        in_specs=[HBM_SPEC] * n + [SEM_SPEC, SEM_SPEC, _any()], out_specs=[HBM_SPEC] * n,
        out_shape=[pltpu.HBM(b.shape, b.dtype) for b in bufs],
        input_output_aliases={k: k for k in range(n)}, compiler_params=IN_FLIGHT,
    )(*bufs, send, recv, after)


def gather_forward(bufs):
    n = len(bufs)

    def body(*refs):
        outs = refs[n:2 * n]
        send, recv = refs[2 * n:]
        x, y, c, chips = _place()
        cps = []
        for k in range(n):
            rows = outs[k].shape[1]
            for j, (cx, cy) in enumerate(chips):
                sems = dict(send_sem=send.at[3 * k + j], recv_sem=recv.at[3 * k + j],
                            device_id=(x, y, 1 - c), device_id_type=MESH)
                mine = outs[k].at[2 * cx + cy, _half(rows, c), :]
                theirs = outs[k].at[2 * cx + cy, _half(rows, 1 - c), :]
                cp = pltpu.make_async_remote_copy(src_ref=mine, dst_ref=mine, **sems)
                cp.start()
                cps.append((cp, pltpu.make_async_remote_copy(src_ref=theirs, dst_ref=theirs, **sems)))
        for cp, arrival in cps:
            cp.wait_send()
            arrival.wait_recv()

    return pl.pallas_call(
        body, in_specs=[_any()] * n, out_specs=[_any()] * n,
        out_shape=[jax.ShapeDtypeStruct(b.shape, b.dtype) for b in bufs], input_output_aliases={k: k for k in range(n)},
        scratch_shapes=[pltpu.SemaphoreType.DMA((3 * n,)), pltpu.SemaphoreType.DMA((3 * n,))],
        name="gather_forward")(*bufs)


def _forward_copies(bufs, send, recv):
    x, y, c, chips = _place()
    out = []
    for k, buf in enumerate(bufs):
        rows = buf.shape[1]
        for j, (cx, cy) in enumerate(chips):
            sems = dict(send_sem=send.at[3 * k + j], recv_sem=recv.at[3 * k + j],
                        device_id=(x, y, 1 - c), device_id_type=MESH)
            mine = buf.at[2 * cx + cy, _half(rows, c), :]
            theirs = buf.at[2 * cx + cy, _half(rows, 1 - c), :]
            out.append((pltpu.make_async_remote_copy(src_ref=mine, dst_ref=mine, **sems),
                        pltpu.make_async_remote_copy(src_ref=theirs, dst_ref=theirs, **sems)))
    return out


def forward_start(bufs, tag):
    n = len(bufs)

    def body(*refs):
        ins = refs[:n]
        send, recv = refs[n], refs[n + 1]
        token = refs[-1]
        for start, _ in _forward_copies(ins, send, recv):
            start.start()
        token[...] = jnp.zeros_like(token)

    sems = pltpu.SemaphoreType.DMA((3 * n,))
    res = pl.pallas_call(
        body, name=f"forward_start_{tag}", in_specs=[HBM_SPEC] * n,
        out_specs=[SEM_SPEC, SEM_SPEC] + [HBM_SPEC] * n + [pl.BlockSpec(memory_space=pltpu.VMEM)],
        out_shape=[sems, sems] + [pltpu.HBM(b.shape, b.dtype) for b in bufs] + [jax.ShapeDtypeStruct((8, LANES), F32)],
        input_output_aliases={k: 2 + k for k in range(n)}, compiler_params=IN_FLIGHT,
    )(*[_in_hbm(b) for b in bufs])
    return res[0], res[1], res[2:2 + n], res[-1]


def forward_wait(send, recv, bufs, after, tag):
    n = len(bufs)

    def body(*refs):
        ins = refs[:n]
        send_ref, recv_ref = refs[n], refs[n + 1]
        for start, arrival in _forward_copies(ins, send_ref, recv_ref):
            start.wait_send()
            arrival.wait_recv()

    return pl.pallas_call(
        body, name=f"forward_wait_{tag}",
        in_specs=[HBM_SPEC] * n + [SEM_SPEC, SEM_SPEC, _any()], out_specs=[HBM_SPEC] * n,
        out_shape=[pltpu.HBM(b.shape, b.dtype) for b in bufs],
        input_output_aliases={k: k for k in range(n)}, compiler_params=IN_FLIGHT,
    )(*bufs, send, recv, after)


def _exchange_copies(srcs, lands, send, recv):
    x, y, c, _ = _place()
    return [pltpu.make_async_remote_copy(
        src_ref=src.at[:, _half(src.shape[1], 1 - c), :], dst_ref=land, send_sem=send.at[k], recv_sem=recv.at[k],
        device_id=(x, y, 1 - c), device_id_type=MESH) for k, (src, land) in enumerate(zip(srcs, lands))]


def exchange_start(srcs, tag):
    n = len(srcs)
    lands = [lax.empty((s.shape[0], s.shape[1] // 2, s.shape[2]), s.dtype) for s in srcs]

    def body(*refs):
        ins, land_refs = refs[:n], refs[n:2 * n]
        send, recv = refs[2 * n], refs[2 * n + 1]
        token = refs[-1]
        for cp in _exchange_copies(ins, land_refs, send, recv):
            cp.start()
        token[...] = jnp.zeros_like(token)

    sems = pltpu.SemaphoreType.DMA((n,))
    res = pl.pallas_call(
        body, name=f"exchange_start_{tag}",
        in_specs=[HBM_SPEC] * (2 * n),
        out_specs=[SEM_SPEC, SEM_SPEC] + [HBM_SPEC] * (2 * n) + [pl.BlockSpec(memory_space=pltpu.VMEM)],
        out_shape=[sems, sems] + [pltpu.HBM(a.shape, a.dtype) for a in list(srcs) + lands]
        + [jax.ShapeDtypeStruct((8, LANES), F32)],
        input_output_aliases={k: 2 + k for k in range(2 * n)}, compiler_params=IN_FLIGHT,
    )(*[_in_hbm(a) for a in list(srcs) + lands])
    return res[0], res[1], res[2:2 + n], res[2 + n:2 + 2 * n], res[-1]


def exchange_wait(send, recv, srcs, lands, after, tag):
    n = len(srcs)

    def body(*refs):
        ins, land_refs = refs[:n], refs[n:2 * n]
        send_ref, recv_ref = refs[2 * n], refs[2 * n + 1]
        for cp in _exchange_copies(ins, land_refs, send_ref, recv_ref):
            cp.wait_send()
            cp.wait_recv()

    res = pl.pallas_call(
        body, name=f"exchange_wait_{tag}",
        in_specs=[HBM_SPEC] * (2 * n) + [SEM_SPEC, SEM_SPEC, _any()], out_specs=[HBM_SPEC] * (2 * n),
        out_shape=[pltpu.HBM(a.shape, a.dtype) for a in list(srcs) + list(lands)],
        input_output_aliases={k: k for k in range(2 * n)}, compiler_params=IN_FLIGHT,
    )(*srcs, *lands, send, recv, after)
    return res[:n], res[n:]


def add_pair(gs, r1s, core):
    n = len(gs)

    def body(c_ref, *refs):
        del c_ref
        for g_ref, r_ref, o_ref in zip(refs[:n], refs[n:2 * n], refs[2 * n:]):
            o_ref[...] = (g_ref[...] + r_ref[...]).astype(BF16)

    blk = lambda r: (None,) + r.shape[1:]
    grid_spec = pltpu.PrefetchScalarGridSpec(
        num_scalar_prefetch=1, grid=(NCHIP,),
        in_specs=[pl.BlockSpec(blk(r), lambda s, c: (s, c[0], 0)) for r in r1s]
        + [pl.BlockSpec(blk(r), lambda s, c: (s, 0, 0)) for r in r1s],
        out_specs=[pl.BlockSpec(blk(r), lambda s, c: (s, 0, 0)) for r in r1s])
    return pl.pallas_call(body, grid_spec=grid_spec, out_shape=[jax.ShapeDtypeStruct(r.shape, BF16) for r in r1s],
                          compiler_params=_cp(("arbitrary",)), name="add_pair")(core, *gs, *r1s)


def _scatter_copies(srcs, lands, send, recv):
    _, _, c, chips = _place()
    out = []
    for k, (src, land) in enumerate(zip(srcs, lands)):
        for j, (cx, cy) in enumerate(chips):
            out.append(pltpu.make_async_remote_copy(
                src_ref=src.at[2 * cx + cy], dst_ref=land.at[j], send_sem=send.at[3 * k + j],
                recv_sem=recv.at[3 * k + j], device_id=(cx, cy, c), device_id_type=MESH))
    return out


def scatter_start(srcs, layer):
    n = len(srcs)
    srcs = list(srcs)
    lands = [lax.empty((3,) + s.shape[1:], s.dtype) for s in srcs]

    def body(*refs):
        ins, land_refs = refs[:n], refs[n:2 * n]
        send, recv = refs[2 * n], refs[2 * n + 1]
        token = refs[-1]
        for cp in _scatter_copies(ins, land_refs, send, recv):
            cp.start()
        token[...] = jnp.zeros_like(token)

    sems = pltpu.SemaphoreType.DMA((3 * n,))
    res = pl.pallas_call(
        body, name=f"scatter_start_{layer}",
        in_specs=[HBM_SPEC] * (2 * n),
        out_specs=[SEM_SPEC, SEM_SPEC] + [HBM_SPEC] * (2 * n) + [pl.BlockSpec(memory_space=pltpu.VMEM)],
        out_shape=[sems, sems] + [pltpu.HBM(a.shape, a.dtype) for a in srcs + lands]
        + [jax.ShapeDtypeStruct((8, LANES), F32)],
        input_output_aliases={k: 2 + k for k in range(2 * n)}, compiler_params=IN_FLIGHT,
    )(*[_in_hbm(a) for a in srcs + lands])
    return res[0], res[1], res[2:2 + n], res[2 + n:2 + 2 * n], res[-1]


def scatter_wait(send, recv, srcs, lands, after, layer):
    n = len(srcs)

    def body(*refs):
        ins, land_refs = refs[:n], refs[n:2 * n]
        send_ref, recv_ref = refs[2 * n], refs[2 * n + 1]
        for cp in _scatter_copies(ins, land_refs, send_ref, recv_ref):
            cp.wait_send()
            cp.wait_recv()

    res = pl.pallas_call(
        body, name=f"scatter_wait_{layer}",
        in_specs=[HBM_SPEC] * (2 * n) + [SEM_SPEC, SEM_SPEC, _any()], out_specs=[HBM_SPEC] * (2 * n),
        out_shape=[pltpu.HBM(a.shape, a.dtype) for a in list(srcs) + list(lands)],
        input_output_aliases={k: k for k in range(2 * n)}, compiler_params=IN_FLIGHT,
    )(*srcs, *lands, send, recv, after)
    return res[n:]


def add_chips(gs, r1s, r2s, place, totals, layer):
    n = len(gs)
    steps = 2

    def body(p_ref, *refs):
        del p_ref
        for g_ref, r1_ref, r2_ref, o_ref in zip(refs[:n], refs[n:2 * n], refs[2 * n:3 * n], refs[4 * n:]):
            own = g_ref[...] + r1_ref[...]
            o_ref[...] = ((own + r2_ref[0].astype(F32)) + r2_ref[1].astype(F32)) + r2_ref[2].astype(F32)

    blk = lambda r: (None, r.shape[1] // steps, r.shape[2])
    grid_spec = pltpu.PrefetchScalarGridSpec(
        num_scalar_prefetch=1, grid=(steps,),
        in_specs=[pl.BlockSpec(blk(r), lambda i, p: (p[1], p[0] * steps + i, 0)) for r in r1s]
        + [pl.BlockSpec(blk(r), lambda i, p: (p[1], i, 0)) for r in r1s]
        + [pl.BlockSpec((3,) + blk(r)[1:], lambda i, p: (0, i, 0)) for r in r1s] + [_any()] * n,
        out_specs=[pl.BlockSpec(blk(r), lambda i, p: (layer, p[0] * steps + i, 0)) for r in r1s])
    return pl.pallas_call(body, grid_spec=grid_spec, out_shape=[jax.ShapeDtypeStruct(t.shape, F32) for t in totals],
                          input_output_aliases={1 + 3 * n + k: k for k in range(n)},
                          compiler_params=_cp(("arbitrary",)), name="add_chips")(place, *gs, *r1s, *r2s, *totals)


def pair_share(gs, tag):
    n = len(gs)

    def body(*refs):
        outs = refs[n:2 * n]
        send, recv = refs[2 * n:]
        x, y, c, _ = _place()
        cps = []
        for k in range(n):
            mine = outs[k].at[:, _half(outs[k].shape[1], c), :]
            cp = pltpu.make_async_remote_copy(
                src_ref=mine, dst_ref=mine, send_sem=send.at[k], recv_sem=recv.at[k],
                device_id=(x, y, 1 - c), device_id_type=MESH)
            cp.start()
            cps.append(cp)
        for k, cp in enumerate(cps):
            cp.wait_send()
            theirs = outs[k].at[:, _half(outs[k].shape[1], 1 - c), :]
            pltpu.make_async_remote_copy(
                src_ref=theirs, dst_ref=theirs, send_sem=send.at[k], recv_sem=recv.at[k],
                device_id=(x, y, 1 - c), device_id_type=MESH).wait_recv()

    return pl.pallas_call(
        body, in_specs=[_any()] * n, out_specs=[_any()] * n,
        out_shape=[jax.ShapeDtypeStruct(g.shape, g.dtype) for g in gs], input_output_aliases={k: k for k in range(n)},
        scratch_shapes=[pltpu.SemaphoreType.DMA((n,)), pltpu.SemaphoreType.DMA((n,))],
        name=f"pair_share_{tag}")(*gs)


def small_collect(v, reduce, name):
    rows = v.shape[0]
    flips = [(fx, fy, fc) for fx in (0, 1) for fy in (0, 1) for fc in (0, 1)][1:]

    def body(v_ref, o_ref, buf, send, recv):
        x, y, c, _ = _place()
        buf[4 * x + 2 * y + c] = v_ref[...]
        peers = [(jnp.where(fx, 1 - x, x), jnp.where(fy, 1 - y, y), jnp.where(fc, 1 - c, c)) for fx, fy, fc in flips]
        cps = []
        for k, peer in enumerate(peers):
            cp = pltpu.make_async_remote_copy(
                src_ref=v_ref, dst_ref=buf.at[4 * x + 2 * y + c], send_sem=send.at[k], recv_sem=recv.at[k],
                device_id=peer, device_id_type=MESH)
            cp.start()
            cps.append(cp)
        for k, (px, py, pc) in enumerate(peers):
            pltpu.make_async_remote_copy(
                src_ref=v_ref, dst_ref=buf.at[4 * px + 2 * py + pc], send_sem=send.at[k], recv_sem=recv.at[k],
                device_id=(px, py, pc), device_id_type=MESH).wait_recv()
        for cp in cps:
            cp.wait_send()
        if reduce:
            acc = buf[0]
            for s in range(1, 8):
                acc = acc + buf[s]
            o_ref[...] = acc
        else:
            o_ref[...] = buf[...]

    vm = pl.BlockSpec(memory_space=pltpu.VMEM)
    out_shape = jax.ShapeDtypeStruct((rows, SMALL_COLS) if reduce else (8, rows, SMALL_COLS), F32)
    return pl.pallas_call(
        body, in_specs=[vm], out_specs=vm, out_shape=out_shape,
        scratch_shapes=[pltpu.VMEM((8, rows, SMALL_COLS), F32), pltpu.SemaphoreType.DMA((7,)),
                        pltpu.SemaphoreType.DMA((7,))],
        name=name)(v)


def adamw(w, g, m, v, rb, name):
    nl, rows, cols = w.shape

    def body(w_ref, g_ref, m_ref, v_ref, go_ref, d_ref, nm_ref, nv_ref):
        gv = g_ref[...]
        go_ref[...] = gv
        nm = ADAM_B1 * m_ref[...] + (1.0 - ADAM_B1) * gv
        nv = ADAM_B2 * v_ref[...] + (1.0 - ADAM_B2) * (gv * gv)
        m_hat = nm / (1.0 - ADAM_B1 ** ADAM_STEP)
        v_hat = nv / (1.0 - ADAM_B2 ** ADAM_STEP)
        d_ref[...] = -ADAM_LR * (m_hat / (jnp.sqrt(v_hat) + ADAM_EPS) + ADAM_WD * w_ref[...])
        nm_ref[...] = nm
        nv_ref[...] = nv

    blk = pl.BlockSpec((None, rb, cols), lambda l, r: (l, r, 0))
    shp = jax.ShapeDtypeStruct(w.shape, F32)
    return pl.pallas_call(body, grid=(nl, rows // rb), in_specs=[blk] * 4, out_specs=[blk] * 4, out_shape=[shp] * 4,
                          compiler_params=_cp(("arbitrary", "arbitrary")), name=name)(w, g, m, v)


def _pack(parts, rows):
    flat = jnp.concatenate([p.reshape(-1).astype(F32) for p in parts])
    return jnp.pad(flat, (0, rows * SMALL_COLS - flat.shape[0])).reshape(rows, SMALL_COLS)


def _unpack(vec, shapes):
    flat = vec.reshape(-1)
    out, off = [], 0
    for s in shapes:
        size = 1
        for d in s:
            size *= d
        out.append(flat[off:off + size].reshape(s))
        off += size
    return out


def kernel(x, w_in, w_conv, rel_bias, g_conv_out, g_attn_out, w_out, g_pre_mix, g_post_mix, g_pre_ffn, g_post_ffn, w_ffn_in, w_ffn_out, loss_target, m_w_in, m_w_conv, m_rel_bias, m_g_conv_out, m_g_attn_out, m_w_out, m_g_pre_mix, m_g_post_mix, m_g_pre_ffn, m_g_post_ffn, m_w_ffn_in, m_w_ffn_out, v_w_in, v_w_conv, v_rel_bias, v_g_conv_out, v_g_attn_out, v_w_out, v_g_pre_mix, v_g_post_mix, v_g_pre_ffn, v_g_post_ffn, v_w_ffn_in, v_w_ffn_out):
    xi, yi, ci = lax.axis_index("x"), lax.axis_index("y"), lax.axis_index("c")
    chip = 2 * xi + yi
    nl = w_in.shape[0]
    x0 = x[0]
    target = loss_target[0]
    cwl = CW // NCHIP

    chip1 = chip.reshape(1).astype(jnp.int32)
    own = [cast_to_slot([w_in, w_out, w_ffn_in, w_ffn_out], chip1, l) for l in range(nl)]
    wc_mine = jnp.pad(w_conv.reshape(-1), (0, 16 * LANES - w_conv.size)).reshape(1, 16, LANES)
    wc_slot = lax.dynamic_update_slice_in_dim(jnp.zeros((NCHIP, 16, LANES), F32), wc_mine, chip, axis=0)
    gm = jnp.kron(jnp.eye(CW // HD, dtype=F32), jnp.full((HD, HD), 1.0 / HD, F32)).astype(BF16)
    row = lambda a, l: a[l][None, :]

    def token(t):
        return t[0:1, 0:1]

    def gather_finish(flight, after, tag):
        send, recv, bufs, _ = flight
        return gather_forward(gather_wait(send, recv, bufs, after, tag))

    first_mix = gather_start(list(own[0][:2]) + [wc_slot], x0, "0m")
    first_ffn = gather_start(own[0][2:], first_mix[3], "0f")
    gw_in, gw_out, wc_all = gather_finish(first_mix, x0, "0m")
    wc_full = wc_all.reshape(NCHIP, -1)[:, :nl * cwl * 3].reshape(NCHIP, nl, cwl, 3)
    wc_full = jnp.transpose(wc_full, (1, 0, 2, 3)).reshape(nl, CW, 3)
    wconv_t = jnp.pad(jnp.transpose(wc_full, (0, 2, 1)), ((0, 0), (0, 5), (0, 0)))
    flights, to_sibling = {}, None
    saved, weights = [], []
    h = x0
    for l in range(nl):
        if l == 0:
            pass
        elif l == 1:
            gw_in, gw_out, gw_fi, gw_fo = gather_finish(flights[l], h, l)
        else:
            gw_in, gw_out, gw_fi, gw_fo = forward_wait(*to_sibling[:3], h, l)
        gw_out = gw_out.reshape(D, D)
        g_pm, g_pf = row(g_pre_mix, l), row(g_pre_ffn, l)
        if l == 0:
            g_pm = g_pm + token(first_ffn[3])
        if l + 1 < nl and l + 1 not in flights:
            flights[l + 1] = gather_start(own[l + 1], first_ffn[3] if l == 0 else gw_in, l + 1)
            g_pm = g_pm + token(flights[l + 1][3])
        bias2, bias2_bwd = bias_expand(_diag_vector(rel_bias[l]), (QG_FWD, QG_BWD))
        proj = fwd_inproj(h, g_pm, gw_in)
        xmid, o, lse, y, z = fwd_mix(h, proj, bias2, wconv_t[l], row(g_conv_out, l), row(g_attn_out, l),
                                     row(g_post_mix, l), gm, gw_out)
        if l == 0:
            gw_fi, gw_fo = gather_finish(first_ffn, xmid, "0f")
        elif l + 1 < nl:
            send, recv, bufs, _ = flights[l + 1]
            landed = gather_wait(send, recv, bufs, xmid, l + 1)
            to_sibling = forward_start(landed, l + 1)
            g_pf = g_pf + token(to_sibling[3])
            if l + 2 < nl:
                flights[l + 2] = gather_start(own[l + 2], to_sibling[3], l + 2)
                g_pf = g_pf + token(flights[l + 2][3])
        gw_fo = gw_fo.reshape(2, DFF // 2, D)
        gu, f, xout = fwd_ffn(xmid, g_pf, row(g_post_ffn, l), gw_fi, gw_fo)
        saved.append((h, proj, bias2_bwd, xmid, o, lse, y, z, gu, f))
        weights.append((gw_in, gw_out, gw_fi, gw_fo))
        h = xout
    dx, loss_blk = loss_head(h, target)

    core = ci.reshape(1).astype(jnp.int32)
    place = jnp.stack([ci, chip]).astype(jnp.int32)
    totals = [lax.empty(w.shape, F32) for w in (w_in, w_out, w_ffn_in, w_ffn_out)]
    small = {k: [None] * nl for k in ("co", "ao", "pm", "qm", "pf", "qf", "rel", "wc")}

    def reduce_begin(kinds, grads, tag):
        return kinds, exchange_start(grads, tag), tag

    def reduce_mid(state, after):
        kinds, (send, recv, srcs, lands, _), tag = state
        grads, from_sibling = exchange_wait(send, recv, srcs, lands, after, tag)
        return kinds, grads, from_sibling, scatter_start(add_pair(grads, from_sibling, core), tag), tag

    def reduce_end(state, after, totals, layer):
        kinds, grads, from_sibling, (send, recv, srcs, lands, _), tag = state
        from_chips = scatter_wait(send, recv, srcs, lands, after, tag)
        totals = list(totals)
        summed = add_chips(grads, from_sibling, from_chips, place, [totals[i] for i in kinds], layer)
        for i, t in zip(kinds, summed):
            totals[i] = t
        return totals

    begun = flying = None
    for l in reversed(range(nl)):
        hin, proj, bias2, xmid, o, lse, y, z, gu, f = saved[l]
        gw_in, gw_out, gw_fi, gw_fo = weights[l]
        g_qf, g_qm, wct = row(g_post_ffn, l), row(g_post_mix, l), wconv_t[l]
        if begun is not None:
            g_qf = g_qf + token(begun[1][4])
        dxm, dfb, act, dgu, h2, dg_qf, dg_pf = bwd_ffn(dx, f, xmid, gu, row(g_pre_ffn, l), g_qf, gw_fi, gw_fo)
        if begun is not None:
            flying = reduce_mid(begun, dxm)
            g_qm = g_qm + token(flying[3][4])
        gr_fo = wgrad(act, dfb, 256, D, False, "wgrad_ffn_out").reshape(NCHIP, DFF // NCHIP, D)
        gr_fi = wgrad(h2, dgu, 512, 2 * DFF // NCHIP, True, "wgrad_ffn_in")
        if l == 0:
            begun_ffn = reduce_begin([2, 3], [gr_fi, gr_fo], "0f")
            g_qm = g_qm + token(begun_ffn[1][4])
        dzb, do, dco, dbg, dg_qm, dg_co, dg_ao = bwd_mix(dxm, z, o, proj, wct, row(g_conv_out, l),
                                                          row(g_attn_out, l), g_qm, gm, gw_out)
        if l == 0:
            flying_ffn = reduce_mid(begun_ffn, dzb)
            wct = wct + token(flying_ffn[3][4])
        gr_out = wgrad(y, dzb, 512, D, False, "wgrad_out").reshape(NCHIP, D // NCHIP, D)
        dhc, dcg, dwc = bwd_conv(dco, proj, wct)
        dq, dk, dv, db2 = bwd_attn(proj, o, do, lse, bias2)
        dx, dproj, hb, dg_pm = bwd_inproj(dxm, hin, dhc, dbg, dcg, dq, dk, dv, row(g_pre_mix, l), gw_in)
        if flying is not None:
            totals = reduce_end(flying, dx, totals, l + 1)
        gr_in = wgrad(hb, dproj, 512, PROJ // NCHIP, True, "wgrad_in")
        small["co"][l], small["ao"][l], small["pm"][l], small["qm"][l] = dg_co, dg_ao, dg_pm, dg_qm
        small["pf"][l], small["qf"][l] = dg_pf, dg_qf
        small["rel"][l] = _diag_vector_bwd(bias_reduce(db2.reshape(NH, QG_BWD, QG_BWD + LEFT)))
        small["wc"][l] = jnp.transpose(dwc[0:3], (1, 0))
        if l > 0:
            begun = reduce_begin([0, 1, 2, 3], [gr_in, gr_out, gr_fi, gr_fo], l)
    flying_mix = reduce_mid(reduce_begin([0, 1], [gr_in, gr_out], "0m"), dx)
    totals = reduce_end(flying_ffn, flying_mix[3][4], totals, 0)
    gr_fi, gr_fo = pair_share(totals[2:], "ffn")
    big_fi = adamw(w_ffn_in, gr_fi, m_w_ffn_in, v_w_ffn_in, w_ffn_in.shape[1] // 4, "adamw_ffn_in")
    big_fo = adamw(w_ffn_out, gr_fo, m_w_ffn_out, v_w_ffn_out, w_ffn_out.shape[1] // 4, "adamw_ffn_out")
    totals = reduce_end(flying_mix, big_fo[1], totals, 0)
    gr_in, gr_out = pair_share(totals[:2], "mix")
    big_in = adamw(w_in, gr_in, m_w_in, v_w_in, w_in.shape[1] // 4, "adamw_in")
    big_out = adamw(w_out, gr_out, m_w_out, v_w_out, w_out.shape[1] // 4, "adamw_out")
    big = [big_in, big_out, big_fi, big_fo]

    order = ("co", "ao", "pm", "qm", "pf", "qf", "rel", "wc")
    parts = [jnp.stack(small[k]) for k in order] + [loss_blk[0:1, 0:1]]
    shapes = [p.shape for p in parts]
    red = _unpack(small_collect(_pack(parts, 40), True, "reduce_small"), shapes)
    gr_co, gr_ao, gr_pm, gr_qm, gr_pf, gr_qf, gr_rel, gr_wc_full, loss = red
    gr_co, gr_ao, gr_pm, gr_qm, gr_pf, gr_qf = [a.reshape(nl, -1) for a in (gr_co, gr_ao, gr_pm, gr_qm, gr_pf, gr_qf)]
    gr_wc = lax.dynamic_slice_in_dim(gr_wc_full, chip * cwl, cwl, axis=1)
    loss = loss.reshape(())

    sw = [g_conv_out, g_attn_out, g_pre_mix, g_post_mix, g_pre_ffn, g_post_ffn, rel_bias, w_conv]
    sg = [gr_co, gr_ao, gr_pm, gr_qm, gr_pf, gr_qf, gr_rel, gr_wc]
    sm = [m_g_conv_out, m_g_attn_out, m_g_pre_mix, m_g_post_mix, m_g_pre_ffn, m_g_post_ffn, m_rel_bias, m_w_conv]
    sv = [v_g_conv_out, v_g_attn_out, v_g_pre_mix, v_g_post_mix, v_g_pre_ffn, v_g_post_ffn, v_rel_bias, v_w_conv]
    sshapes = [a.shape for a in sw]
    packed = [_pack(a, 32)[None] for a in (sw, sg, sm, sv)]
    s_out = [_unpack(a[0], sshapes) for a in adamw(*packed, 32, "adamw_small")]

    def leaves(big_i, small_i):
        b_in, b_out, b_fi, b_fo = big_i
        s_co, s_ao, s_pm, s_qm, s_pf, s_qf, s_rel, s_wc = small_i
        return [b_in, s_wc, s_rel, s_co, s_ao, b_out, s_pm, s_qm, s_pf, s_qf, b_fi, b_fo]

    out = [loss, dx[None]]
    out += leaves([b[0] for b in big], sg)
    for i in range(1, 4):
        out += leaves([b[i] for b in big], s_out[i])
    return tuple(out)
```

```python
import jax
import jax.numpy as jnp
from jax import lax
from jax.experimental import pallas as pl
from jax.experimental.pallas import tpu as pltpu

F32 = jnp.float32
BF16 = jnp.bfloat16

D = 1024
PROJ = 3072
CW = 512
HD = 64
NH = 8
CHUNK = 64
BAND = 576
REL_CLIP = 128
NREL = 2 * REL_CLIP + 1
DFF = 2816
DEPTH = 4
NCHIP = 4
EPS = 1e-6
NEG_INF = -1e30

ADAM_LR = 0.001
ADAM_B1 = 0.9
ADAM_B2 = 0.999
ADAM_EPS = 1e-08
ADAM_WD = 0.01
ADAM_STEP = 10

V7X_VMEM_BYTES = 64 * 1024 * 1024
VMEM_LIMIT = V7X_VMEM_BYTES - 8 * 1024 * 1024
LANES = 128
QG_FWD = 4 * CHUNK
QG_BWD = 2 * CHUNK
LEFT = BAND - CHUNK
TQ = 512
TM = 256
SMALL_COLS = 1024
MESH = pl.DeviceIdType.MESH
NT = (((1,), (1,)), ((), ()))
TN = (((0,), (0,)), ((), ()))


def _cp(sem=None, vmem=VMEM_LIMIT):
    return pltpu.CompilerParams(dimension_semantics=sem, vmem_limit_bytes=vmem)


def _any():
    return pl.BlockSpec(memory_space=pl.ANY)


def _const(shape):
    nd = len(shape)
    return pl.BlockSpec(shape, lambda *_: (0,) * nd)


def _rms(v, g):
    r = lax.rsqrt(jnp.mean(v * v, axis=-1, keepdims=True) + EPS)
    return v * r * g


def _rms_bwd(dy, v, g):
    r = lax.rsqrt(jnp.mean(v * v, axis=-1, keepdims=True) + EPS)
    vh = v * r
    dg = jnp.sum(dy * vh, axis=0, keepdims=True)
    dvh = dy * g
    dv = r * (dvh - vh * jnp.mean(dvh * vh, axis=-1, keepdims=True))
    return dv, dg


def _group_mean(v, gm):
    return jnp.dot(v.astype(BF16), gm, preferred_element_type=F32)


def _group_rms_bwd(dy, v, g, gm):
    r = lax.rsqrt(_group_mean(v * v, gm) + EPS)
    vh = v * r
    dg = jnp.sum(dy * vh, axis=0, keepdims=True)
    dvh = dy * g
    dv = r * (dvh - vh * _group_mean(dvh * vh, gm))
    return dv, dg


def _head_masks(scale):
    lane = lax.broadcasted_iota(jnp.int32, (1, LANES), 1)
    return [jnp.where((lane >= HD * a) & (lane < HD * (a + 1)), scale, 0.0).astype(BF16) for a in range(2)]


class _Resident:
    def __init__(self, src, dst, sem):
        self.first = pl.program_id(0) == 0
        self.copy = pltpu.make_async_copy(src, dst, sem)
        self.dst = dst

        @pl.when(self.first)
        def _():
            self.copy.start()

    def read(self):
        @pl.when(self.first)
        def _():
            self.copy.wait()

        return self.dst[...]


FF_CHUNKS = ((0, 1536), (1536, DFF))


def _stream_ffn_weights(wfi_hbm, wfo_hbm, wfi_v, wfo_v, sems, order, step):
    hw = DFF // 2
    per_matrix = {
        0: [(wfi_hbm.at[j], wfi_v.at[0, :, pl.ds(hw * j, hw)]) for j in range(2)],
        1: [(wfi_hbm.at[2 + j], wfi_v.at[1, :, pl.ds(hw * j, hw)]) for j in range(2)],
        2: [(wfo_hbm.at[j], wfo_v.at[pl.ds(hw * j, hw), :]) for j in range(2)],
    }
    pieces = [p for m in order for p in per_matrix[m]]
    slot = {m: 2 * k for k, m in enumerate(order)}

    def make_step(wait):
        def ready(m, chunk):
            if chunk == 0:
                wait(slot[m])
                wait(slot[m] + 1)
        return lambda: step(ready)

    copies = [pltpu.make_async_copy(src, dst, sems.at[k]) for k, (src, dst) in enumerate(pieces)]
    first = pl.program_id(0) == 0

    @pl.when(first)
    def _():
        for cp in copies:
            cp.start()
        make_step(lambda k: copies[k].wait())()

    @pl.when(jnp.logical_not(first))
    def _():
        make_step(lambda k: None)()


def _conv_taps(u_prev, u, scr):
    n = u.shape[0]
    scr[0:16, :] = u_prev
    scr[16:16 + n, :] = u
    return scr[15:15 + n, :], scr[14:14 + n, :]


def fwd_inproj(x, g, w_all):
    t = x.shape[0]
    wc = PROJ // NCHIP

    def body(x_ref, g_ref, w_hbm, o_ref, w_v):
        @pl.when(pl.program_id(0) == 0)
        def _():
            pltpu.sync_copy(w_hbm, w_v)

        h = _rms(x_ref[...], g_ref[...]).astype(BF16)
        for b in range(NCHIP):
            o_ref[:, wc * b:wc * (b + 1)] = jnp.dot(h, w_v[b], preferred_element_type=F32).astype(BF16)

    return pl.pallas_call(
        body, grid=(t // TQ,),
        in_specs=[pl.BlockSpec((TQ, D), lambda i: (i, 0)), _const((1, D)), _any()],
        out_specs=pl.BlockSpec((TQ, PROJ), lambda i: (i, 0)),
        out_shape=jax.ShapeDtypeStruct((t, PROJ), BF16),
        scratch_shapes=[pltpu.VMEM((NCHIP, D, wc), BF16)],
        compiler_params=_cp(("arbitrary",)), name="fwd_inproj")(x, g, w_all)


def _attn_window_specs():
    return [
        pl.BlockSpec((TQ, CW), lambda i: (i, 3)),
        pl.BlockSpec((TQ, CW), lambda i: (jnp.maximum(i - 1, 0), 4)),
        pl.BlockSpec((TQ, CW), lambda i: (i, 4)),
        pl.BlockSpec((TQ, CW), lambda i: (jnp.maximum(i - 1, 0), 5)),
        pl.BlockSpec((TQ, CW), lambda i: (i, 5)),
    ]


def _conv_specs():
    return [
        pl.BlockSpec((TQ, 3 * CW), lambda i: (i, 0)),
        pl.BlockSpec((16, 3 * CW), lambda i: (jnp.maximum(i * (TQ // 16) - 1, 0), 0)),
    ]


def _conv_fwd(pc_ref, pcp_ref, wc_ref, scr, first):
    pc = pc_ref[...].astype(F32)
    hc, bg, cg = pc[:, :CW], pc[:, CW:2 * CW], pc[:, 2 * CW:]
    u = cg * hc
    pp = pcp_ref[...].astype(F32)
    u_prev = jnp.where(first, 0.0, pp[:, 2 * CW:] * pp[:, :CW])
    u1, u2 = _conv_taps(u_prev, u, scr)
    cout = wc_ref[0:1, :] * u2 + wc_ref[1:2, :] * u1 + wc_ref[2:3, :] * u
    return hc, bg, cg, u, u1, u2, cout


def _key_penalty(first, r0, kg):
    col = lax.broadcasted_iota(jnp.int32, (1, kg), 1)
    limit = jnp.where(first, TQ - r0, 0)
    return jnp.where(col < limit, NEG_INF, 0.0)


def fwd_mix(x, proj, bias2, wconv_t, g_co, g_ao, g_pm, gm, wout_all):
    t = x.shape[0]
    qg, kg = QG_FWD, QG_FWD + LEFT

    def body(x_ref, pc_ref, pcp_ref, q_ref, kp_ref, kc_ref, vp_ref, vc_ref, b2_ref, wc_ref, gco_ref, gao_ref, gpm_ref,
             gm_ref, wout_hbm, xmid_ref, o_ref, lse_ref, y_ref, z_ref, wout_v, kwin, vwin, cscr, sems):
        i = pl.program_id(0)
        first = i == 0
        wout = _Resident(wout_hbm, wout_v, sems.at[0])
        kwin[0:TQ, :] = kp_ref[...]
        kwin[TQ:2 * TQ, :] = kc_ref[...]
        vwin[0:TQ, :] = vp_ref[...]
        vwin[TQ:2 * TQ, :] = vc_ref[...]
        qmask = _head_masks(HD ** -0.5)
        low = lax.broadcasted_iota(jnp.int32, (1, LANES), 1) < HD

        def group(g, carry):
            r0 = pl.multiple_of(g * qg, qg)
            pen = _key_penalty(first, r0, kg)
            for hp in range(NH // 2):
                ls = slice(LANES * hp, LANES * (hp + 1))
                qb = q_ref[pl.ds(r0, qg), ls]
                q2 = jnp.concatenate([qb * qmask[0], qb * qmask[1]], axis=0)
                s = lax.dot_general(q2, kwin[pl.ds(r0, kg), ls], NT, preferred_element_type=F32)
                s = s + b2_ref[hp] + pen
                m = jnp.max(s, axis=-1, keepdims=True)
                p = jnp.exp(s - m)
                l = jnp.sum(p, axis=-1, keepdims=True)
                o2 = jnp.dot(p.astype(BF16), vwin[pl.ds(r0, kg), ls], preferred_element_type=F32) * (1.0 / l)
                lse2 = m + jnp.log(l)
                o_ref[pl.ds(r0, qg), ls] = jnp.where(low, o2[:qg], o2[qg:])
                lse_ref[pl.ds(r0, qg), ls] = jnp.where(low, lse2[:qg], lse2[qg:])
            return carry

        lax.fori_loop(0, TQ // qg, group, 0)

        _, bg, _, _, _, _, cout = _conv_fwd(pc_ref, pcp_ref, wc_ref, cscr, first)
        yc = bg * cout
        gmv = gm_ref[...]
        ycn = yc * lax.rsqrt(_group_mean(yc * yc, gmv) + EPS) * gco_ref[...]
        oa = o_ref[...]
        oan = oa * lax.rsqrt(_group_mean(oa * oa, gmv) + EPS) * gao_ref[...]
        y_ref[:, 0:CW] = ycn.astype(BF16)
        y_ref[:, CW:2 * CW] = oan.astype(BF16)
        z = jnp.dot(y_ref[...], wout.read(), preferred_element_type=F32)
        z_ref[...] = z
        xmid_ref[...] = x_ref[...] + _rms(z, gpm_ref[...])

    row = lambda w: pl.BlockSpec((TQ, w), lambda i: (i, 0))
    return pl.pallas_call(
        body, grid=(t // TQ,),
        in_specs=[row(D)] + _conv_specs() + _attn_window_specs() + [
            _const((NH // 2, 2 * qg, kg)), _const((8, CW)), _const((1, CW)), _const((1, CW)), _const((1, D)),
            _const((CW, CW)), _any()],
        out_specs=[row(D), row(CW), row(CW), row(D), row(D)],
        out_shape=[jax.ShapeDtypeStruct((t, D), F32), jax.ShapeDtypeStruct((t, CW), F32),
                   jax.ShapeDtypeStruct((t, CW), F32), jax.ShapeDtypeStruct((t, D), BF16),
                   jax.ShapeDtypeStruct((t, D), F32)],
        scratch_shapes=[pltpu.VMEM((D, D), BF16), pltpu.VMEM((2 * TQ, CW), BF16), pltpu.VMEM((2 * TQ, CW), BF16),
                        pltpu.VMEM((TQ + 16, CW), F32), pltpu.SemaphoreType.DMA((1,))],
        compiler_params=_cp(("arbitrary",)), name="fwd_mix",
    )(x, proj, proj, proj, proj, proj, proj, proj, bias2, wconv_t, g_co, g_ao, g_pm, gm, wout_all)


def fwd_ffn(xmid, g_pre, g_post, wfi_all, wfo_all):
    t = xmid.shape[0]

    def body(x_ref, gpre_ref, gpost_ref, wfi_hbm, wfo_hbm, gu_ref, f_ref, xo_ref, wfi_v, wfo_v, sems):
        def step(ready):
            xv = x_ref[...]
            h = _rms(xv, gpre_ref[...]).astype(BF16)
            f = jnp.zeros((TM, D), F32)
            for ci, (a, b) in enumerate(FF_CHUNKS):
                ready(0, ci)
                gate = jnp.dot(h, wfi_v[0, :, a:b], preferred_element_type=F32)
                ready(1, ci)
                up = jnp.dot(h, wfi_v[1, :, a:b], preferred_element_type=F32)
                gu_ref[:, a:b] = gate.astype(BF16)
                gu_ref[:, DFF + a:DFF + b] = up.astype(BF16)
                act = gate * (1.0 / (1.0 + jnp.exp(-gate))) * up
                ready(2, ci)
                f = f + jnp.dot(act.astype(BF16), wfo_v[a:b, :], preferred_element_type=F32)
            f_ref[...] = f
            xo_ref[...] = xv + _rms(f, gpost_ref[...])

        _stream_ffn_weights(wfi_hbm, wfo_hbm, wfi_v, wfo_v, sems, (0, 1, 2), step)

    row = lambda w: pl.BlockSpec((TM, w), lambda i: (i, 0))
    return pl.pallas_call(
        body, grid=(t // TM,),
        in_specs=[row(D), _const((1, D)), _const((1, D)), _any(), _any()],
        out_specs=[row(2 * DFF), row(D), row(D)],
        out_shape=[jax.ShapeDtypeStruct((t, 2 * DFF), BF16), jax.ShapeDtypeStruct((t, D), F32),
                   jax.ShapeDtypeStruct((t, D), F32)],
        scratch_shapes=[pltpu.VMEM((2, D, DFF), BF16), pltpu.VMEM((DFF, D), BF16), pltpu.SemaphoreType.DMA((6,))],
        compiler_params=_cp(("arbitrary",)), name="fwd_ffn")(xmid, g_pre, g_post, wfi_all, wfo_all)


def loss_head(y, target):
    t = y.shape[0]

    def body(y_ref, t_ref, dy_ref, l_ref):
        @pl.when(pl.program_id(0) == 0)
        def _():
            l_ref[...] = jnp.zeros_like(l_ref)

        e = y_ref[...] - t_ref[...]
        dy_ref[...] = e * (1.0 / D)
        rows = jnp.sum(e * e, axis=-1, keepdims=True) * (1.0 / D)
        l_ref[...] += 0.5 * jnp.sum(rows, axis=0, keepdims=True)

    row = pl.BlockSpec((TQ, D), lambda i: (i, 0))
    return pl.pallas_call(
        body, grid=(t // TQ,), in_specs=[row, row], out_specs=[row, _const((8, LANES))],
        out_shape=[jax.ShapeDtypeStruct((t, D), F32), jax.ShapeDtypeStruct((8, LANES), F32)],
        compiler_params=_cp(("arbitrary",)), name="loss_head")(y, target)


def bwd_ffn(dx, f, xmid, gu, g_pre, g_post, wfi_all, wfo_all):
    t = dx.shape[0]
    hw = DFF // 2

    def body(dx_ref, f_ref, x_ref, gu_ref, gpre_ref, gpost_ref, wfi_hbm, wfo_hbm,
             dxm_ref, df_ref, act_ref, dgu_ref, h_ref, dgpost_ref, dgpre_ref, wfi_v, wfo_v, sems):
        @pl.when(pl.program_id(0) == 0)
        def _():
            dgpost_ref[...] = jnp.zeros_like(dgpost_ref)
            dgpre_ref[...] = jnp.zeros_like(dgpre_ref)

        def step(ready):
            dxo = dx_ref[...]
            df, dgp = _rms_bwd(dxo, f_ref[...], gpost_ref[...])
            dgpost_ref[...] += dgp
            dfb = df.astype(BF16)
            df_ref[...] = dfb
            dh = jnp.zeros((TM, D), F32)
            for ci, (a, b) in enumerate(FF_CHUNKS):
                ready(2, ci)
                dact = lax.dot_general(dfb, wfo_v[a:b, :], NT, preferred_element_type=F32)
                gate = gu_ref[:, a:b].astype(F32)
                up = gu_ref[:, DFF + a:DFF + b].astype(F32)
                sig = 1.0 / (1.0 + jnp.exp(-gate))
                silu = gate * sig
                act_ref[:, a:b] = (silu * up).astype(BF16)
                dup = (dact * silu).astype(BF16)
                dgate = (dact * up * (sig * (1.0 + gate * (1.0 - sig)))).astype(BF16)
                dgu_ref[:, a:b] = dgate
                dgu_ref[:, DFF + a:DFF + b] = dup
                ready(0, ci)
                dh = dh + lax.dot_general(dgate, wfi_v[0, :, a:b], NT, preferred_element_type=F32)
                ready(1, ci)
                dh = dh + lax.dot_general(dup, wfi_v[1, :, a:b], NT, preferred_element_type=F32)
            xv = x_ref[...]
            gpre = gpre_ref[...]
            h_ref[...] = _rms(xv, gpre).astype(BF16)
            dxv, dgq = _rms_bwd(dh, xv, gpre)
            dgpre_ref[...] += dgq
            dxm_ref[...] = dxo + dxv

        _stream_ffn_weights(wfi_hbm, wfo_hbm, wfi_v, wfo_v, sems, (2, 0, 1), step)

    row = lambda w: pl.BlockSpec((TM, w), lambda i: (i, 0))
    return pl.pallas_call(
        body, grid=(t // TM,),
        in_specs=[row(D), row(D), row(D), row(2 * DFF), _const((1, D)), _const((1, D)), _any(), _any()],
        out_specs=[row(D), row(D), row(DFF), row(2 * DFF), row(D), _const((1, D)), _const((1, D))],
        out_shape=[jax.ShapeDtypeStruct((t, D), F32), jax.ShapeDtypeStruct((t, D), BF16),
                   jax.ShapeDtypeStruct((t, DFF), BF16), jax.ShapeDtypeStruct((t, 2 * DFF), BF16),
                   jax.ShapeDtypeStruct((t, D), BF16), jax.ShapeDtypeStruct((1, D), F32),
                   jax.ShapeDtypeStruct((1, D), F32)],
        scratch_shapes=[pltpu.VMEM((2, D, DFF), BF16), pltpu.VMEM((DFF, D), BF16), pltpu.SemaphoreType.DMA((6,))],
        compiler_params=_cp(("arbitrary",)), name="bwd_ffn")(dx, f, xmid, gu, g_pre, g_post, wfi_all, wfo_all)


def bwd_mix(dxm, z, o, proj, wconv_t, g_co, g_ao, g_pm, gm, wout_all):
    t = dxm.shape[0]

    def body(dx_ref, z_ref, o_ref, pc_ref, pcp_ref, wc_ref, gco_ref, gao_ref, gpm_ref, gm_ref, wout_hbm,
             dz_ref, do_ref, dco_ref, dbg_ref, dgpm_ref, dgco_ref, dgao_ref, wout_v, cscr):
        first = pl.program_id(0) == 0

        @pl.when(first)
        def _():
            pltpu.sync_copy(wout_hbm, wout_v)
            dgpm_ref[...] = jnp.zeros_like(dgpm_ref)
            dgco_ref[...] = jnp.zeros_like(dgco_ref)
            dgao_ref[...] = jnp.zeros_like(dgao_ref)

        dz, dgp = _rms_bwd(dx_ref[...], z_ref[...], gpm_ref[...])
        dgpm_ref[...] += dgp
        dzb = dz.astype(BF16)
        dz_ref[...] = dzb
        gmv = gm_ref[...]
        _, bg, _, _, _, _, cout = _conv_fwd(pc_ref, pcp_ref, wc_ref, cscr, first)
        dy_conv = lax.dot_general(dzb, wout_v[0:CW, :], NT, preferred_element_type=F32)
        dyc, dgc = _group_rms_bwd(dy_conv, bg * cout, gco_ref[...], gmv)
        dgco_ref[...] += dgc
        dbg_ref[...] = (dyc * cout).astype(BF16)
        dco_ref[...] = dyc * bg
        dy_attn = lax.dot_general(dzb, wout_v[CW:2 * CW, :], NT, preferred_element_type=F32)
        do, dga = _group_rms_bwd(dy_attn, o_ref[...], gao_ref[...], gmv)
        dgao_ref[...] += dga
        do_ref[...] = do.astype(BF16)

    row = lambda w: pl.BlockSpec((TQ, w), lambda i: (i, 0))
    return pl.pallas_call(
        body, grid=(t // TQ,),
        in_specs=[row(D), row(D), row(CW)] + _conv_specs() + [
            _const((8, CW)), _const((1, CW)), _const((1, CW)), _const((1, D)), _const((CW, CW)), _any()],
        out_specs=[row(D), row(CW), row(CW), row(CW), _const((1, D)), _const((1, CW)), _const((1, CW))],
        out_shape=[jax.ShapeDtypeStruct((t, D), BF16), jax.ShapeDtypeStruct((t, CW), BF16),
                   jax.ShapeDtypeStruct((t, CW), F32), jax.ShapeDtypeStruct((t, CW), BF16),
                   jax.ShapeDtypeStruct((1, D), F32), jax.ShapeDtypeStruct((1, CW), F32),
                   jax.ShapeDtypeStruct((1, CW), F32)],
        scratch_shapes=[pltpu.VMEM((D, D), BF16), pltpu.VMEM((TQ + 16, CW), F32)],
        compiler_params=_cp(("arbitrary",)), name="bwd_mix",
    )(dxm, z, o, proj, proj, wconv_t, g_co, g_ao, g_pm, gm, wout_all)


def bwd_conv(dco, proj, wconv_t):
    t = dco.shape[0]
    nt = t // TQ

    def body(d_ref, dn_ref, pc_ref, pcp_ref, wc_ref, dhc_ref, dcg_ref, dw_ref, cscr, dscr):
        i = pl.program_id(0)
        first = i == 0

        @pl.when(first)
        def _():
            dw_ref[...] = jnp.zeros_like(dw_ref)

        hc, _, cg, u, u1, u2, _ = _conv_fwd(pc_ref, pcp_ref, wc_ref, cscr, first)
        d0 = d_ref[...]
        dscr[0:TQ, :] = d0
        dscr[TQ:TQ + 8, :] = jnp.where(i == nt - 1, 0.0, dn_ref[...])
        d1 = dscr[1:TQ + 1, :]
        d2 = dscr[2:TQ + 2, :]
        du = wc_ref[2:3, :] * d0 + wc_ref[1:2, :] * d1 + wc_ref[0:1, :] * d2
        dhc_ref[...] = (du * cg).astype(BF16)
        dcg_ref[...] = (du * hc).astype(BF16)
        dw_ref[0:1, :] += jnp.sum(d0 * u2, axis=0, keepdims=True)
        dw_ref[1:2, :] += jnp.sum(d0 * u1, axis=0, keepdims=True)
        dw_ref[2:3, :] += jnp.sum(d0 * u, axis=0, keepdims=True)

    row = lambda w: pl.BlockSpec((TQ, w), lambda i: (i, 0))
    nxt = pl.BlockSpec((8, CW), lambda i: (jnp.minimum((i + 1) * (TQ // 8), t // 8 - 1), 0))
    return pl.pallas_call(
        body, grid=(nt,),
        in_specs=[row(CW), nxt] + _conv_specs() + [_const((8, CW))],
        out_specs=[row(CW), row(CW), _const((8, CW))],
        out_shape=[jax.ShapeDtypeStruct((t, CW), BF16), jax.ShapeDtypeStruct((t, CW), BF16),
                   jax.ShapeDtypeStruct((8, CW), F32)],
        scratch_shapes=[pltpu.VMEM((TQ + 16, CW), F32), pltpu.VMEM((TQ + 8, CW), F32)],
        compiler_params=_cp(("arbitrary",)), name="bwd_conv")(dco, dco, proj, proj, wconv_t)


def bwd_attn(proj, o, do, lse, bias2):
    t = o.shape[0]
    nt = t // TQ
    qg, kg = QG_BWD, QG_BWD + LEFT
    nkb = (t + TQ) // LANES

    def body(q_ref, kp_ref, kc_ref, vp_ref, vc_ref, o_ref, do_ref, lse_ref, b2_ref,
             dq_ref, dk_hbm, dv_hbm, db_hbm, kwin, vwin, dk_acc, dv_acc, db_acc):
        i = pl.program_id(0)
        first = i == 0

        @pl.when(first)
        def _():
            dk_acc[...] = jnp.zeros_like(dk_acc)
            dv_acc[...] = jnp.zeros_like(dv_acc)
            db_acc[...] = jnp.zeros_like(db_acc)

        kwin[0:TQ, :] = kp_ref[...]
        kwin[TQ:2 * TQ, :] = kc_ref[...]
        vwin[0:TQ, :] = vp_ref[...]
        vwin[TQ:2 * TQ, :] = vc_ref[...]
        scale = HD ** -0.5
        qmask = _head_masks(scale)
        vmask = _head_masks(1.0)
        low = lax.broadcasted_iota(jnp.int32, (1, LANES), 1) < HD

        def group(g, carry):
            r0 = pl.multiple_of(g * qg, qg)
            base = i * (TQ // LANES) + g * (qg // LANES)
            pen = _key_penalty(first, r0, kg)
            for hp in range(NH // 2):
                ls = slice(LANES * hp, LANES * (hp + 1))
                qb = q_ref[pl.ds(r0, qg), ls]
                kw = kwin[pl.ds(r0, kg), ls]
                dob = do_ref[pl.ds(r0, qg), ls]
                prod = dob.astype(F32) * o_ref[pl.ds(r0, qg), ls]
                lseb = lse_ref[pl.ds(r0, qg), ls]
                q2 = jnp.concatenate([qb * qmask[0], qb * qmask[1]], axis=0)
                do2 = jnp.concatenate([dob * vmask[0], dob * vmask[1]], axis=0)
                lse2 = jnp.concatenate([lseb[:, 0:1], lseb[:, HD:HD + 1]], axis=0)
                dsum = jnp.concatenate([jnp.sum(jnp.where(low, prod, 0.0), axis=-1, keepdims=True),
                                        jnp.sum(jnp.where(low, 0.0, prod), axis=-1, keepdims=True)], axis=0)
                s = lax.dot_general(q2, kw, NT, preferred_element_type=F32) + b2_ref[hp] + pen
                p = jnp.exp(s - lse2)
                dp = lax.dot_general(do2, vwin[pl.ds(r0, kg), ls], NT, preferred_element_type=F32)
                ds = p * (dp - dsum)
                db_acc[hp] += ds
                dsb = ds.astype(BF16)
                dq2 = jnp.dot(dsb, kw, preferred_element_type=F32)
                dq_ref[pl.ds(r0, qg), ls] = (jnp.where(low, dq2[:qg], dq2[qg:]) * scale).astype(BF16)
                dkt = lax.dot_general(q2, dsb, TN, preferred_element_type=F32)
                dvt = lax.dot_general(do2, p.astype(BF16), TN, preferred_element_type=F32)
                for kb in range(kg // LANES):
                    dk_acc[base + kb, ls, :] += dkt[:, LANES * kb:LANES * (kb + 1)]
                    dv_acc[base + kb, ls, :] += dvt[:, LANES * kb:LANES * (kb + 1)]
            return carry

        lax.fori_loop(0, TQ // qg, group, 0)

        @pl.when(i == nt - 1)
        def _():
            pltpu.sync_copy(dk_acc, dk_hbm)
            pltpu.sync_copy(dv_acc, dv_hbm)
            pltpu.sync_copy(db_acc, db_hbm)

    row = lambda w: pl.BlockSpec((TQ, w), lambda i: (i, 0))
    return pl.pallas_call(
        body, grid=(nt,),
        in_specs=_attn_window_specs() + [row(CW), row(CW), row(CW), _const((NH // 2, 2 * qg, kg))],
        out_specs=[row(CW), _any(), _any(), _any()],
        out_shape=[jax.ShapeDtypeStruct((t, CW), BF16), jax.ShapeDtypeStruct((nkb, CW, LANES), F32),
                   jax.ShapeDtypeStruct((nkb, CW, LANES), F32), jax.ShapeDtypeStruct((NH // 2, 2 * qg, kg), F32)],
        scratch_shapes=[pltpu.VMEM((2 * TQ, CW), BF16), pltpu.VMEM((2 * TQ, CW), BF16),
                        pltpu.VMEM((nkb, CW, LANES), F32), pltpu.VMEM((nkb, CW, LANES), F32),
                        pltpu.VMEM((NH // 2, 2 * qg, kg), F32)],
        compiler_params=_cp(("arbitrary",)), name="bwd_attn",
    )(proj, proj, proj, proj, proj, o, do, lse, bias2)


def bwd_inproj(dxm, x, dhc, dbg, dcg, dq, dk, dv, g, w_all):
    t = x.shape[0]
    wc = PROJ // NCHIP

    def body(dxm_ref, x_ref, dhc_ref, dbg_ref, dcg_ref, dq_ref, dk_ref, dv_ref, g_ref, w_hbm,
             dx_ref, dp_ref, h_ref, dg_ref, w_v):
        @pl.when(pl.program_id(0) == 0)
        def _():
            pltpu.sync_copy(w_hbm, w_v)
            dg_ref[...] = jnp.zeros_like(dg_ref)

        dp_ref[:, 0:CW] = dhc_ref[...]
        dp_ref[:, CW:2 * CW] = dbg_ref[...]
        dp_ref[:, 2 * CW:3 * CW] = dcg_ref[...]
        dp_ref[:, 3 * CW:4 * CW] = dq_ref[...]
        for kb in range(TQ // LANES):
            rows = slice(LANES * kb, LANES * (kb + 1))
            dp_ref[rows, 4 * CW:5 * CW] = jnp.transpose(dk_ref[kb]).astype(BF16)
            dp_ref[rows, 5 * CW:6 * CW] = jnp.transpose(dv_ref[kb]).astype(BF16)
        dh = jnp.zeros((TQ, D), F32)
        for b in range(NCHIP):
            dh = dh + lax.dot_general(dp_ref[:, wc * b:wc * (b + 1)], w_v[b], NT, preferred_element_type=F32)
        xv = x_ref[...]
        gv = g_ref[...]
        h_ref[...] = _rms(xv, gv).astype(BF16)
        dxv, dgv = _rms_bwd(dh, xv, gv)
        dg_ref[...] += dgv
        dx_ref[...] = dxm_ref[...] + dxv

    row = lambda w: pl.BlockSpec((TQ, w), lambda i: (i, 0))
    pad = pl.BlockSpec((TQ // LANES, CW, LANES), lambda i: (i + 1, 0, 0))
    return pl.pallas_call(
        body, grid=(t // TQ,),
        in_specs=[row(D), row(D), row(CW), row(CW), row(CW), row(CW), pad, pad, _const((1, D)), _any()],
        out_specs=[row(D), row(PROJ), row(D), _const((1, D))],
        out_shape=[jax.ShapeDtypeStruct((t, D), F32), jax.ShapeDtypeStruct((t, PROJ), BF16),
                   jax.ShapeDtypeStruct((t, D), BF16), jax.ShapeDtypeStruct((1, D), F32)],
        scratch_shapes=[pltpu.VMEM((NCHIP, D, wc), BF16)],
        compiler_params=_cp(("arbitrary",)), name="bwd_inproj",
    )(dxm, x, dhc, dbg, dcg, dq, dk, dv, g, w_all)


def wgrad(a, b, kb, nb, by_columns, name):
    t, k = a.shape
    n = b.shape[1]
    tk = 512

    def body(a_ref, b_ref, o_ref):
        o_ref[...] = jnp.zeros_like(o_ref)
        for c in range(t // tk):
            o_ref[...] += lax.dot_general(a_ref[tk * c:tk * (c + 1), :], b_ref[tk * c:tk * (c + 1), :], TN,
                                          preferred_element_type=F32)

    if by_columns:
        assert nb == n // NCHIP
        out_spec = pl.BlockSpec((None, kb, nb), lambda ki, ni: (ni, ki, 0))
        out_shape = jax.ShapeDtypeStruct((NCHIP, k, nb), F32)
    else:
        assert nb == n
        out_spec = pl.BlockSpec((kb, nb), lambda ki, ni: (ki, 0))
        out_shape = jax.ShapeDtypeStruct((k, n), F32)
    return pl.pallas_call(
        body, grid=(k // kb, n // nb),
        in_specs=[pl.BlockSpec((t, kb), lambda ki, ni: (0, ki)), pl.BlockSpec((t, nb), lambda ki, ni: (0, ni))],
        out_specs=out_spec, out_shape=out_shape,
        compiler_params=_cp(("arbitrary", "arbitrary")), name=name)(a, b)


TOE = 1024
assert 2 * QG_FWD + LEFT <= TOE
N_FLAT = LEFT - REL_CLIP + 1
N_VAR = BAND - N_FLAT


def _diag_vector(table):
    last = table[:, 2 * REL_CLIP:]
    var = table[:, 2 * REL_CLIP - N_VAR:2 * REL_CLIP][:, ::-1]
    return jnp.concatenate([jnp.broadcast_to(last, (NH, N_FLAT)), var, jnp.broadcast_to(last, (NH, TOE - BAND))], axis=1)


def _diag_vector_bwd(dvec):
    dlast = jnp.sum(dvec[:, :N_FLAT], axis=1, keepdims=True) + jnp.sum(dvec[:, BAND:], axis=1, keepdims=True)
    dvar = dvec[:, N_FLAT:BAND][:, ::-1]
    return jnp.concatenate([jnp.zeros((NH, 2 * REL_CLIP - N_VAR), F32), dvar, dlast], axis=1)


def _band_valid(qg):
    r = lax.broadcasted_iota(jnp.int32, (qg, qg + LEFT), 0)
    p = lax.broadcasted_iota(jnp.int32, (qg, qg + LEFT), 1)
    start = lax.shift_left(lax.shift_right_logical(r, 6), 6)
    return (p >= start) & (p < start + BAND)


def bias_expand(vec, qgs):
    def body(v_ref, *o_refs):
        for qg, o_ref in zip(qgs, o_refs):
            valid = _band_valid(qg)
            for h in range(NH):
                rows = jnp.broadcast_to(v_ref[h:h + 1, :], (qg, TOE))
                toe = pltpu.roll(rows, 0, 1, stride=1, stride_axis=0)
                o_ref[h // 2, qg * (h % 2):qg * (h % 2 + 1), :] = jnp.where(valid, toe[:, :qg + LEFT], NEG_INF)

    return pl.pallas_call(body, out_shape=[jax.ShapeDtypeStruct((NH // 2, 2 * qg, qg + LEFT), F32) for qg in qgs],
                          name="bias_expand")(vec)


def bias_reduce(db2):
    _, qg, kg = db2.shape

    def body(d_ref, o_ref):
        ii = lax.broadcasted_iota(jnp.int32, (kg, kg), 0)
        jj = lax.broadcasted_iota(jnp.int32, (kg, kg), 1)
        flip = jnp.where(ii + jj == kg - 1, 1.0, 0.0).astype(BF16)
        for h in range(NH):
            rest = d_ref[h]
            rev = jnp.zeros((qg, kg), F32)
            for _ in range(3):
                term = rest.astype(BF16)
                rev = rev + jnp.dot(term, flip, preferred_element_type=F32)
                rest = rest - term.astype(F32)
            d = jnp.concatenate([jnp.zeros((qg, TOE - kg), F32), rev], axis=1)
            back = pltpu.roll(d, 0, 1, stride=1, stride_axis=0)
            o_ref[h:h + 1, :] = jnp.sum(back, axis=0, keepdims=True)

    rev = pl.pallas_call(body, out_shape=jax.ShapeDtypeStruct((NH, TOE), F32), name="bias_reduce")(db2)
    return rev[:, ::-1]


def _place():
    x, y, c = lax.axis_index("x"), lax.axis_index("y"), lax.axis_index("c")
    chips = [(1 - x, y), (x, 1 - y), (1 - x, 1 - y)]
    return x, y, c, chips


def _half(ref_rows, c):
    return pl.ds(c * (ref_rows // 2), ref_rows // 2)


HBM_SPEC = pl.BlockSpec(memory_space=pltpu.HBM)
SEM_SPEC = pl.BlockSpec(memory_space=pltpu.SEMAPHORE)
IN_FLIGHT = pltpu.CompilerParams(has_side_effects=pltpu.SideEffectType.DATAFLOW_SIDE_EFFECTING)


def _in_hbm(a):
    return pltpu.with_memory_space_constraint(a, pltpu.HBM)


def cast_to_slot(ws, chip, layer):
    n = len(ws)
    steps = 4

    def body(b_ref, *refs):
        del b_ref
        for w_ref, o_ref in zip(refs[:n], refs[n:]):
            o_ref[...] = w_ref[...].astype(BF16)

    grid_spec = pltpu.PrefetchScalarGridSpec(
        num_scalar_prefetch=1, grid=(steps,),
        in_specs=[pl.BlockSpec((None, w.shape[1] // steps, w.shape[2]), lambda r, b: (layer, r, 0)) for w in ws],
        out_specs=[pl.BlockSpec((None, w.shape[1] // steps, w.shape[2]), lambda r, b: (b[0], r, 0)) for w in ws])
    return pl.pallas_call(body, grid_spec=grid_spec,
                          out_shape=[jax.ShapeDtypeStruct((NCHIP,) + w.shape[1:], BF16) for w in ws],
                          compiler_params=_cp(("arbitrary",)), name="cast_to_slot")(chip, *ws)


def _gather_copies(bufs, send, recv):
    x, y, c, chips = _place()
    b = 2 * x + y
    out = []
    for k, buf in enumerate(bufs):
        rows = buf.shape[1]
        mine = buf.at[b, _half(rows, c), :]
        for j, (cx, cy) in enumerate(chips):
            theirs = buf.at[2 * cx + cy, _half(rows, c), :]
            sems = dict(send_sem=send.at[3 * k + j], recv_sem=recv.at[3 * k + j],
                        device_id=(cx, cy, c), device_id_type=MESH)
            out.append((pltpu.make_async_remote_copy(src_ref=mine, dst_ref=mine, **sems),
                        pltpu.make_async_remote_copy(src_ref=theirs, dst_ref=theirs, **sems)))
    return out


def gather_start(bufs, after, layer):
    n = len(bufs)

    def body(*refs):
        ins = refs[:n]
        send, recv = refs[n + 1], refs[n + 2]
        token = refs[-1]
        for start, _ in _gather_copies(ins, send, recv):
            start.start()
        token[...] = jnp.zeros_like(token)

    sems = pltpu.SemaphoreType.DMA((3 * n,))
    res = pl.pallas_call(
        body, name=f"gather_start_{layer}",
        in_specs=[HBM_SPEC] * n + [_any()],
        out_specs=[SEM_SPEC, SEM_SPEC] + [HBM_SPEC] * n + [pl.BlockSpec(memory_space=pltpu.VMEM)],
        out_shape=[sems, sems] + [pltpu.HBM(b.shape, b.dtype) for b in bufs] + [jax.ShapeDtypeStruct((8, LANES), F32)],
        input_output_aliases={k: 2 + k for k in range(n)}, compiler_params=IN_FLIGHT,
    )(*[_in_hbm(b) for b in bufs], after)
    return res[0], res[1], res[2:2 + n], res[-1]


def gather_wait(send, recv, bufs, after, layer):
    n = len(bufs)

    def body(*refs):
        ins = refs[:n]
        send_ref, recv_ref = refs[n], refs[n + 1]
        for start, arrival in _gather_copies(ins, send_ref, recv_ref):
            start.wait_send()
            arrival.wait_recv()

    return pl.pallas_call(
        body, name=f"gather_wait_{layer}",
        in_specs=[HBM_SPEC] * n + [SEM_SPEC, SEM_SPEC, _any()], out_specs=[HBM_SPEC] * n,
        out_shape=[pltpu.HBM(b.shape, b.dtype) for b in bufs],
        input_output_aliases={k: k for k in range(n)}, compiler_params=IN_FLIGHT,
    )(*bufs, send, recv, after)


def gather_forward(bufs):
    n = len(bufs)

    def body(*refs):
        outs = refs[n:2 * n]
        send, recv = refs[2 * n:]
        x, y, c, chips = _place()
        cps = []
        for k in range(n):
            rows = outs[k].shape[1]
            for j, (cx, cy) in enumerate(chips):
                sems = dict(send_sem=send.at[3 * k + j], recv_sem=recv.at[3 * k + j],
                            device_id=(x, y, 1 - c), device_id_type=MESH)
                mine = outs[k].at[2 * cx + cy, _half(rows, c), :]
                theirs = outs[k].at[2 * cx + cy, _half(rows, 1 - c), :]
                cp = pltpu.make_async_remote_copy(src_ref=mine, dst_ref=mine, **sems)
                cp.start()
                cps.append((cp, pltpu.make_async_remote_copy(src_ref=theirs, dst_ref=theirs, **sems)))
        for cp, arrival in cps:
            cp.wait_send()
            arrival.wait_recv()

    return pl.pallas_call(
        body, in_specs=[_any()] * n, out_specs=[_any()] * n,
        out_shape=[jax.ShapeDtypeStruct(b.shape, b.dtype) for b in bufs], input_output_aliases={k: k for k in range(n)},
        scratch_shapes=[pltpu.SemaphoreType.DMA((3 * n,)), pltpu.SemaphoreType.DMA((3 * n,))],
        name="gather_forward")(*bufs)


def _forward_copies(bufs, send, recv):
    x, y, c, chips = _place()
    out = []
    for k, buf in enumerate(bufs):
        rows = buf.shape[1]
        for j, (cx, cy) in enumerate(chips):
            sems = dict(send_sem=send.at[3 * k + j], recv_sem=recv.at[3 * k + j],
                        device_id=(x, y, 1 - c), device_id_type=MESH)
            mine = buf.at[2 * cx + cy, _half(rows, c), :]
            theirs = buf.at[2 * cx + cy, _half(rows, 1 - c), :]
            out.append((pltpu.make_async_remote_copy(src_ref=mine, dst_ref=mine, **sems),
                        pltpu.make_async_remote_copy(src_ref=theirs, dst_ref=theirs, **sems)))
    return out


def forward_start(bufs, tag):
    n = len(bufs)

    def body(*refs):
        ins = refs[:n]
        send, recv = refs[n], refs[n + 1]
        token = refs[-1]
        for start, _ in _forward_copies(ins, send, recv):
            start.start()
        token[...] = jnp.zeros_like(token)

    sems = pltpu.SemaphoreType.DMA((3 * n,))
    res = pl.pallas_call(
        body, name=f"forward_start_{tag}", in_specs=[HBM_SPEC] * n,
        out_specs=[SEM_SPEC, SEM_SPEC] + [HBM_SPEC] * n + [pl.BlockSpec(memory_space=pltpu.VMEM)],
        out_shape=[sems, sems] + [pltpu.HBM(b.shape, b.dtype) for b in bufs] + [jax.ShapeDtypeStruct((8, LANES), F32)],
        input_output_aliases={k: 2 + k for k in range(n)}, compiler_params=IN_FLIGHT,
    )(*[_in_hbm(b) for b in bufs])
    return res[0], res[1], res[2:2 + n], res[-1]


def forward_wait(send, recv, bufs, after, tag):
    n = len(bufs)

    def body(*refs):
        ins = refs[:n]
        send_ref, recv_ref = refs[n], refs[n + 1]
        for start, arrival in _forward_copies(ins, send_ref, recv_ref):
            start.wait_send()
            arrival.wait_recv()

    return pl.pallas_call(
        body, name=f"forward_wait_{tag}",
        in_specs=[HBM_SPEC] * n + [SEM_SPEC, SEM_SPEC, _any()], out_specs=[HBM_SPEC] * n,
        out_shape=[pltpu.HBM(b.shape, b.dtype) for b in bufs],
        input_output_aliases={k: k for k in range(n)}, compiler_params=IN_FLIGHT,
    )(*bufs, send, recv, after)


def _exchange_copies(srcs, lands, send, recv):
    x, y, c, _ = _place()
    return [pltpu.make_async_remote_copy(
        src_ref=src.at[:, _half(src.shape[1], 1 - c), :], dst_ref=land, send_sem=send.at[k], recv_sem=recv.at[k],
        device_id=(x, y, 1 - c), device_id_type=MESH) for k, (src, land) in enumerate(zip(srcs, lands))]


def exchange_start(srcs, tag):
    n = len(srcs)
    lands = [lax.empty((s.shape[0], s.shape[1] // 2, s.shape[2]), s.dtype) for s in srcs]

    def body(*refs):
        ins, land_refs = refs[:n], refs[n:2 * n]
        send, recv = refs[2 * n], refs[2 * n + 1]
        token = refs[-1]
        for cp in _exchange_copies(ins, land_refs, send, recv):
            cp.start()
        token[...] = jnp.zeros_like(token)

    sems = pltpu.SemaphoreType.DMA((n,))
    res = pl.pallas_call(
        body, name=f"exchange_start_{tag}",
        in_specs=[HBM_SPEC] * (2 * n),
        out_specs=[SEM_SPEC, SEM_SPEC] + [HBM_SPEC] * (2 * n) + [pl.BlockSpec(memory_space=pltpu.VMEM)],
        out_shape=[sems, sems] + [pltpu.HBM(a.shape, a.dtype) for a in list(srcs) + lands]
        + [jax.ShapeDtypeStruct((8, LANES), F32)],
        input_output_aliases={k: 2 + k for k in range(2 * n)}, compiler_params=IN_FLIGHT,
    )(*[_in_hbm(a) for a in list(srcs) + lands])
    return res[0], res[1], res[2:2 + n], res[2 + n:2 + 2 * n], res[-1]


def exchange_wait(send, recv, srcs, lands, after, tag):
    n = len(srcs)

    def body(*refs):
        ins, land_refs = refs[:n], refs[n:2 * n]
        send_ref, recv_ref = refs[2 * n], refs[2 * n + 1]
        for cp in _exchange_copies(ins, land_refs, send_ref, recv_ref):
            cp.wait_send()
            cp.wait_recv()

    res = pl.pallas_call(
        body, name=f"exchange_wait_{tag}",
        in_specs=[HBM_SPEC] * (2 * n) + [SEM_SPEC, SEM_SPEC, _any()], out_specs=[HBM_SPEC] * (2 * n),
        out_shape=[pltpu.HBM(a.shape, a.dtype) for a in list(srcs) + list(lands)],
        input_output_aliases={k: k for k in range(2 * n)}, compiler_params=IN_FLIGHT,
    )(*srcs, *lands, send, recv, after)
    return res[:n], res[n:]


def add_pair(gs, r1s, core):
    n = len(gs)

    def body(c_ref, *refs):
        del c_ref
        for g_ref, r_ref, o_ref in zip(refs[:n], refs[n:2 * n], refs[2 * n:]):
            o_ref[...] = (g_ref[...] + r_ref[...]).astype(BF16)

    blk = lambda r: (None,) + r.shape[1:]
    grid_spec = pltpu.PrefetchScalarGridSpec(
        num_scalar_prefetch=1, grid=(NCHIP,),
        in_specs=[pl.BlockSpec(blk(r), lambda s, c: (s, c[0], 0)) for r in r1s]
        + [pl.BlockSpec(blk(r), lambda s, c: (s, 0, 0)) for r in r1s],
        out_specs=[pl.BlockSpec(blk(r), lambda s, c: (s, 0, 0)) for r in r1s])
    return pl.pallas_call(body, grid_spec=grid_spec, out_shape=[jax.ShapeDtypeStruct(r.shape, BF16) for r in r1s],
                          compiler_params=_cp(("arbitrary",)), name="add_pair")(core, *gs, *r1s)


def _scatter_copies(srcs, lands, send, recv):
    _, _, c, chips = _place()
    out = []
    for k, (src, land) in enumerate(zip(srcs, lands)):
        for j, (cx, cy) in enumerate(chips):
            out.append(pltpu.make_async_remote_copy(
                src_ref=src.at[2 * cx + cy], dst_ref=land.at[j], send_sem=send.at[3 * k + j],
                recv_sem=recv.at[3 * k + j], device_id=(cx, cy, c), device_id_type=MESH))
    return out


def scatter_start(srcs, layer):
    n = len(srcs)
    srcs = list(srcs)
    lands = [lax.empty((3,) + s.shape[1:], s.dtype) for s in srcs]

    def body(*refs):
        ins, land_refs = refs[:n], refs[n:2 * n]
        send, recv = refs[2 * n], refs[2 * n + 1]
        token = refs[-1]
        for cp in _scatter_copies(ins, land_refs, send, recv):
            cp.start()
        token[...] = jnp.zeros_like(token)

    sems = pltpu.SemaphoreType.DMA((3 * n,))
    res = pl.pallas_call(
        body, name=f"scatter_start_{layer}",
        in_specs=[HBM_SPEC] * (2 * n),
        out_specs=[SEM_SPEC, SEM_SPEC] + [HBM_SPEC] * (2 * n) + [pl.BlockSpec(memory_space=pltpu.VMEM)],
        out_shape=[sems, sems] + [pltpu.HBM(a.shape, a.dtype) for a in srcs + lands]
        + [jax.ShapeDtypeStruct((8, LANES), F32)],
        input_output_aliases={k: 2 + k for k in range(2 * n)}, compiler_params=IN_FLIGHT,
    )(*[_in_hbm(a) for a in srcs + lands])
    return res[0], res[1], res[2:2 + n], res[2 + n:2 + 2 * n], res[-1]


def scatter_wait(send, recv, srcs, lands, after, layer):
    n = len(srcs)

    def body(*refs):
        ins, land_refs = refs[:n], refs[n:2 * n]
        send_ref, recv_ref = refs[2 * n], refs[2 * n + 1]
        for cp in _scatter_copies(ins, land_refs, send_ref, recv_ref):
            cp.wait_send()
            cp.wait_recv()

    res = pl.pallas_call(
        body, name=f"scatter_wait_{layer}",
        in_specs=[HBM_SPEC] * (2 * n) + [SEM_SPEC, SEM_SPEC, _any()], out_specs=[HBM_SPEC] * (2 * n),
        out_shape=[pltpu.HBM(a.shape, a.dtype) for a in list(srcs) + list(lands)],
        input_output_aliases={k: k for k in range(2 * n)}, compiler_params=IN_FLIGHT,
    )(*srcs, *lands, send, recv, after)
    return res[n:]


def add_chips(gs, r1s, r2s, place, totals, layer):
    n = len(gs)
    steps = 2

    def body(p_ref, *refs):
        del p_ref
        for g_ref, r1_ref, r2_ref, o_ref in zip(refs[:n], refs[n:2 * n], refs[2 * n:3 * n], refs[4 * n:]):
            own = g_ref[...] + r1_ref[...]
            o_ref[...] = ((own + r2_ref[0].astype(F32)) + r2_ref[1].astype(F32)) + r2_ref[2].astype(F32)

    blk = lambda r: (None, r.shape[1] // steps, r.shape[2])
    grid_spec = pltpu.PrefetchScalarGridSpec(
        num_scalar_prefetch=1, grid=(steps,),
        in_specs=[pl.BlockSpec(blk(r), lambda i, p: (p[1], p[0] * steps + i, 0)) for r in r1s]
        + [pl.BlockSpec(blk(r), lambda i, p: (p[1], i, 0)) for r in r1s]
        + [pl.BlockSpec((3,) + blk(r)[1:], lambda i, p: (0, i, 0)) for r in r1s] + [_any()] * n,
        out_specs=[pl.BlockSpec(blk(r), lambda i, p: (layer, p[0] * steps + i, 0)) for r in r1s])
    return pl.pallas_call(body, grid_spec=grid_spec, out_shape=[jax.ShapeDtypeStruct(t.shape, F32) for t in totals],
                          input_output_aliases={1 + 3 * n + k: k for k in range(n)},
                          compiler_params=_cp(("arbitrary",)), name="add_chips")(place, *gs, *r1s, *r2s, *totals)


def pair_share(gs, tag):
    n = len(gs)

    def body(*refs):
        outs = refs[n:2 * n]
        send, recv = refs[2 * n:]
        x, y, c, _ = _place()
        cps = []
        for k in range(n):
            mine = outs[k].at[:, _half(outs[k].shape[1], c), :]
            cp = pltpu.make_async_remote_copy(
                src_ref=mine, dst_ref=mine, send_sem=send.at[k], recv_sem=recv.at[k],
                device_id=(x, y, 1 - c), device_id_type=MESH)
            cp.start()
            cps.append(cp)
        for k, cp in enumerate(cps):
            cp.wait_send()
            theirs = outs[k].at[:, _half(outs[k].shape[1], 1 - c), :]
            pltpu.make_async_remote_copy(
                src_ref=theirs, dst_ref=theirs, send_sem=send.at[k], recv_sem=recv.at[k],
                device_id=(x, y, 1 - c), device_id_type=MESH).wait_recv()

    return pl.pallas_call(
        body, in_specs=[_any()] * n, out_specs=[_any()] * n,
        out_shape=[jax.ShapeDtypeStruct(g.shape, g.dtype) for g in gs], input_output_aliases={k: k for k in range(n)},
        scratch_shapes=[pltpu.SemaphoreType.DMA((n,)), pltpu.SemaphoreType.DMA((n,))],
        name=f"pair_share_{tag}")(*gs)


def small_allreduce(v):
    rows = v.shape[0]
    flips = [(fx, fy, fc) for fx in (0, 1) for fy in (0, 1) for fc in (0, 1)][1:]

    def body(v_ref, o_ref, buf, send, recv):
        x, y, c, _ = _place()
        buf[4 * x + 2 * y + c] = v_ref[...]
        peers = [(jnp.where(fx, 1 - x, x), jnp.where(fy, 1 - y, y), jnp.where(fc, 1 - c, c)) for fx, fy, fc in flips]
        cps = []
        for k, peer in enumerate(peers):
            cp = pltpu.make_async_remote_copy(
                src_ref=v_ref, dst_ref=buf.at[4 * x + 2 * y + c], send_sem=send.at[k], recv_sem=recv.at[k],
                device_id=peer, device_id_type=MESH)
            cp.start()
            cps.append(cp)
        for k, (px, py, pc) in enumerate(peers):
            pltpu.make_async_remote_copy(
                src_ref=v_ref, dst_ref=buf.at[4 * px + 2 * py + pc], send_sem=send.at[k], recv_sem=recv.at[k],
                device_id=(px, py, pc), device_id_type=MESH).wait_recv()
        for cp in cps:
            cp.wait_send()
        acc = buf[0]
        for s in range(1, 8):
            acc = acc + buf[s]
        o_ref[...] = acc

    vm = pl.BlockSpec(memory_space=pltpu.VMEM)
    return pl.pallas_call(
        body, in_specs=[vm], out_specs=vm, out_shape=jax.ShapeDtypeStruct((rows, SMALL_COLS), F32),
        scratch_shapes=[pltpu.VMEM((8, rows, SMALL_COLS), F32), pltpu.SemaphoreType.DMA((7,)),
                        pltpu.SemaphoreType.DMA((7,))],
        name="reduce_small")(v)


def adamw(w, g, m, v, rb, name):
    nl, rows, cols = w.shape

    def body(w_ref, g_ref, m_ref, v_ref, go_ref, d_ref, nm_ref, nv_ref):
        gv = g_ref[...]
        go_ref[...] = gv
        nm = ADAM_B1 * m_ref[...] + (1.0 - ADAM_B1) * gv
        nv = ADAM_B2 * v_ref[...] + (1.0 - ADAM_B2) * (gv * gv)
        m_hat = nm / (1.0 - ADAM_B1 ** ADAM_STEP)
        v_hat = nv / (1.0 - ADAM_B2 ** ADAM_STEP)
        d_ref[...] = -ADAM_LR * (m_hat / (jnp.sqrt(v_hat) + ADAM_EPS) + ADAM_WD * w_ref[...])
        nm_ref[...] = nm
        nv_ref[...] = nv

    blk = pl.BlockSpec((None, rb, cols), lambda l, r: (l, r, 0))
    shp = jax.ShapeDtypeStruct(w.shape, F32)
    return pl.pallas_call(body, grid=(nl, rows // rb), in_specs=[blk] * 4, out_specs=[blk] * 4, out_shape=[shp] * 4,
                          compiler_params=_cp(("arbitrary", "arbitrary")), name=name)(w, g, m, v)


def _pack(parts, rows):
    flat = jnp.concatenate([p.reshape(-1).astype(F32) for p in parts])
    return jnp.pad(flat, (0, rows * SMALL_COLS - flat.shape[0])).reshape(rows, SMALL_COLS)


def _unpack(vec, shapes):
    flat = vec.reshape(-1)
    out, off = [], 0
    for s in shapes:
        size = 1
        for d in s:
            size *= d
        out.append(flat[off:off + size].reshape(s))
        off += size
    return out


def kernel(x, w_in, w_conv, rel_bias, g_conv_out, g_attn_out, w_out, g_pre_mix, g_post_mix, g_pre_ffn, g_post_ffn, w_ffn_in, w_ffn_out, loss_target, m_w_in, m_w_conv, m_rel_bias, m_g_conv_out, m_g_attn_out, m_w_out, m_g_pre_mix, m_g_post_mix, m_g_pre_ffn, m_g_post_ffn, m_w_ffn_in, m_w_ffn_out, v_w_in, v_w_conv, v_rel_bias, v_g_conv_out, v_g_attn_out, v_w_out, v_g_pre_mix, v_g_post_mix, v_g_pre_ffn, v_g_post_ffn, v_w_ffn_in, v_w_ffn_out):
    xi, yi, ci = lax.axis_index("x"), lax.axis_index("y"), lax.axis_index("c")
    chip = 2 * xi + yi
    nl = w_in.shape[0]
    x0 = x[0]
    target = loss_target[0]
    cwl = CW // NCHIP

    chip1 = chip.reshape(1).astype(jnp.int32)
    own = [cast_to_slot([w_in, w_out, w_ffn_in, w_ffn_out], chip1, l) for l in range(nl)]
    wc_mine = jnp.pad(w_conv.reshape(-1), (0, 16 * LANES - w_conv.size)).reshape(1, 16, LANES)
    wc_slot = lax.dynamic_update_slice_in_dim(jnp.zeros((NCHIP, 16, LANES), F32), wc_mine, chip, axis=0)
    gm = jnp.kron(jnp.eye(CW // HD, dtype=F32), jnp.full((HD, HD), 1.0 / HD, F32)).astype(BF16)
    row = lambda a, l: a[l][None, :]

    def token(t):
        return t[0:1, 0:1]

    def gather_finish(flight, after, tag):
        send, recv, bufs, _ = flight
        return gather_forward(gather_wait(send, recv, bufs, after, tag))

    first_mix = gather_start(list(own[0][:2]) + [wc_slot], x0, "0m")
    first_ffn = gather_start(own[0][2:], first_mix[3], "0f")
    gw_in, gw_out, wc_all = gather_finish(first_mix, x0, "0m")
    wc_full = wc_all.reshape(NCHIP, -1)[:, :nl * cwl * 3].reshape(NCHIP, nl, cwl, 3)
    wc_full = jnp.transpose(wc_full, (1, 0, 2, 3)).reshape(nl, CW, 3)
    wconv_t = jnp.pad(jnp.transpose(wc_full, (0, 2, 1)), ((0, 0), (0, 5), (0, 0)))
    flights, to_sibling = {}, None
    saved, weights = [], []
    h = x0
    for l in range(nl):
        if l == 0:
            pass
        elif l == 1:
            gw_in, gw_out, gw_fi, gw_fo = gather_finish(flights[l], h, l)
        else:
            gw_in, gw_out, gw_fi, gw_fo = forward_wait(*to_sibling[:3], h, l)
        gw_out = gw_out.reshape(D, D)
        g_pm, g_pf = row(g_pre_mix, l), row(g_pre_ffn, l)
        if l == 0:
            g_pm = g_pm + token(first_ffn[3])
        if l + 1 < nl and l + 1 not in flights:
            flights[l + 1] = gather_start(own[l + 1], first_ffn[3] if l == 0 else gw_in, l + 1)
            g_pm = g_pm + token(flights[l + 1][3])
        bias2, bias2_bwd = bias_expand(_diag_vector(rel_bias[l]), (QG_FWD, QG_BWD))
        proj = fwd_inproj(h, g_pm, gw_in)
        xmid, o, lse, y, z = fwd_mix(h, proj, bias2, wconv_t[l], row(g_conv_out, l), row(g_attn_out, l),
                                     row(g_post_mix, l), gm, gw_out)
        if l == 0:
            gw_fi, gw_fo = gather_finish(first_ffn, xmid, "0f")
        elif l + 1 < nl:
            send, recv, bufs, _ = flights[l + 1]
            landed = gather_wait(send, recv, bufs, xmid, l + 1)
            to_sibling = forward_start(landed, l + 1)
            g_pf = g_pf + token(to_sibling[3])
            if l + 2 < nl:
                flights[l + 2] = gather_start(own[l + 2], to_sibling[3], l + 2)
                g_pf = g_pf + token(flights[l + 2][3])
        gw_fo = gw_fo.reshape(2, DFF // 2, D)
        gu, f, xout = fwd_ffn(xmid, g_pf, row(g_post_ffn, l), gw_fi, gw_fo)
        saved.append((h, proj, bias2_bwd, xmid, o, lse, y, z, gu, f))
        weights.append((gw_in, gw_out, gw_fi, gw_fo))
        h = xout
    dx, loss_blk = loss_head(h, target)

    core = ci.reshape(1).astype(jnp.int32)
    place = jnp.stack([ci, chip]).astype(jnp.int32)
    totals = [lax.empty(w.shape, F32) for w in (w_in, w_out, w_ffn_in, w_ffn_out)]
    small = {k: [None] * nl for k in ("co", "ao", "pm", "qm", "pf", "qf", "rel", "wc")}

    def reduce_begin(kinds, grads, tag):
        return kinds, exchange_start(grads, tag), tag

    def reduce_mid(state, after):
        kinds, (send, recv, srcs, lands, _), tag = state
        grads, from_sibling = exchange_wait(send, recv, srcs, lands, after, tag)
        return kinds, grads, from_sibling, scatter_start(add_pair(grads, from_sibling, core), tag), tag

    def reduce_end(state, after, totals, layer):
        kinds, grads, from_sibling, (send, recv, srcs, lands, _), tag = state
        from_chips = scatter_wait(send, recv, srcs, lands, after, tag)
        totals = list(totals)
        summed = add_chips(grads, from_sibling, from_chips, place, [totals[i] for i in kinds], layer)
        for i, t in zip(kinds, summed):
            totals[i] = t
        return totals

    begun = flying = None
    for l in reversed(range(nl)):
        hin, proj, bias2, xmid, o, lse, y, z, gu, f = saved[l]
        gw_in, gw_out, gw_fi, gw_fo = weights[l]
        g_qf, g_qm, wct = row(g_post_ffn, l), row(g_post_mix, l), wconv_t[l]
        if begun is not None:
            g_qf = g_qf + token(begun[1][4])
        dxm, dfb, act, dgu, h2, dg_qf, dg_pf = bwd_ffn(dx, f, xmid, gu, row(g_pre_ffn, l), g_qf, gw_fi, gw_fo)
        if begun is not None:
            flying = reduce_mid(begun, dxm)
            g_qm = g_qm + token(flying[3][4])
        gr_fo = wgrad(act, dfb, 256, D, False, "wgrad_ffn_out").reshape(NCHIP, DFF // NCHIP, D)
        gr_fi = wgrad(h2, dgu, 512, 2 * DFF // NCHIP, True, "wgrad_ffn_in")
        if l == 0:
            begun_ffn = reduce_begin([2, 3], [gr_fi, gr_fo], "0f")
            g_qm = g_qm + token(begun_ffn[1][4])
        dzb, do, dco, dbg, dg_qm, dg_co, dg_ao = bwd_mix(dxm, z, o, proj, wct, row(g_conv_out, l),
                                                          row(g_attn_out, l), g_qm, gm, gw_out)
        if l == 0:
            flying_ffn = reduce_mid(begun_ffn, dzb)
            wct = wct + token(flying_ffn[3][4])
        gr_out = wgrad(y, dzb, 512, D, False, "wgrad_out").reshape(NCHIP, D // NCHIP, D)
        dhc, dcg, dwc = bwd_conv(dco, proj, wct)
        dq, dk, dv, db2 = bwd_attn(proj, o, do, lse, bias2)
        dx, dproj, hb, dg_pm = bwd_inproj(dxm, hin, dhc, dbg, dcg, dq, dk, dv, row(g_pre_mix, l), gw_in)
        if flying is not None:
            totals = reduce_end(flying, dx, totals, l + 1)
        gr_in = wgrad(hb, dproj, 512, PROJ // NCHIP, True, "wgrad_in")
        small["co"][l], small["ao"][l], small["pm"][l], small["qm"][l] = dg_co, dg_ao, dg_pm, dg_qm
        small["pf"][l], small["qf"][l] = dg_pf, dg_qf
        small["rel"][l] = _diag_vector_bwd(bias_reduce(db2.reshape(NH, QG_BWD, QG_BWD + LEFT)))
        small["wc"][l] = jnp.transpose(dwc[0:3], (1, 0))
        if l > 0:
            begun = reduce_begin([0, 1, 2, 3], [gr_in, gr_out, gr_fi, gr_fo], l)
    flying_mix = reduce_mid(reduce_begin([0, 1], [gr_in, gr_out], "0m"), dx)
    totals = reduce_end(flying_ffn, flying_mix[3][4], totals, 0)
    gr_fi, gr_fo = pair_share(totals[2:], "ffn")
    big_fi = adamw(w_ffn_in, gr_fi, m_w_ffn_in, v_w_ffn_in, w_ffn_in.shape[1] // 4, "adamw_ffn_in")
    big_fo = adamw(w_ffn_out, gr_fo, m_w_ffn_out, v_w_ffn_out, w_ffn_out.shape[1] // 4, "adamw_ffn_out")
    totals = reduce_end(flying_mix, big_fo[1], totals, 0)
    gr_in, gr_out = pair_share(totals[:2], "mix")
    big_in = adamw(w_in, gr_in, m_w_in, v_w_in, w_in.shape[1] // 4, "adamw_in")
    big_out = adamw(w_out, gr_out, m_w_out, v_w_out, w_out.shape[1] // 4, "adamw_out")
    big = [big_in, big_out, big_fi, big_fo]

    order = ("co", "ao", "pm", "qm", "pf", "qf", "rel", "wc")
    parts = [jnp.stack(small[k]) for k in order] + [loss_blk[0:1, 0:1]]
    shapes = [p.shape for p in parts]
    red = _unpack(small_allreduce(_pack(parts, 40)), shapes)
    gr_co, gr_ao, gr_pm, gr_qm, gr_pf, gr_qf, gr_rel, gr_wc_full, loss = red
    gr_co, gr_ao, gr_pm, gr_qm, gr_pf, gr_qf = [a.reshape(nl, -1) for a in (gr_co, gr_ao, gr_pm, gr_qm, gr_pf, gr_qf)]
    gr_wc = lax.dynamic_slice_in_dim(gr_wc_full, chip * cwl, cwl, axis=1)
    loss = loss.reshape(())

    sw = [g_conv_out, g_attn_out, g_pre_mix, g_post_mix, g_pre_ffn, g_post_ffn, rel_bias, w_conv]
    sg = [gr_co, gr_ao, gr_pm, gr_qm, gr_pf, gr_qf, gr_rel, gr_wc]
    sm = [m_g_conv_out, m_g_attn_out, m_g_pre_mix, m_g_post_mix, m_g_pre_ffn, m_g_post_ffn, m_rel_bias, m_w_conv]
    sv = [v_g_conv_out, v_g_attn_out, v_g_pre_mix, v_g_post_mix, v_g_pre_ffn, v_g_post_ffn, v_rel_bias, v_w_conv]
    sshapes = [a.shape for a in sw]
    packed = [_pack(a, 32)[None] for a in (sw, sg, sm, sv)]
    s_out = [_unpack(a[0], sshapes) for a in adamw(*packed, 32, "adamw_small")]

    def leaves(big_i, small_i):
        b_in, b_out, b_fi, b_fo = big_i
        s_co, s_ao, s_pm, s_qm, s_pf, s_qf, s_rel, s_wc = small_i
        return [b_in, s_wc, s_rel, s_co, s_ao, b_out, s_pm, s_qm, s_pf, s_qf, b_fi, b_fo]

    out = [loss, dx[None]]
    out += leaves([b[0] for b in big], sg)
    for i in range(1, 4):
        out += leaves([b[i] for b in big], s_out[i])
    return tuple(out)
```

```python
import jax
import jax.numpy as jnp
from jax import lax
from jax.experimental import pallas as pl
from jax.experimental.pallas import tpu as pltpu

F32 = jnp.float32
BF16 = jnp.bfloat16

D = 1024
PROJ = 3072
CW = 512
HD = 64
NH = 8
CHUNK = 64
BAND = 576
REL_CLIP = 128
NREL = 2 * REL_CLIP + 1
DFF = 2816
DEPTH = 4
NCHIP = 4
EPS = 1e-6
NEG_INF = -1e30

ADAM_LR = 0.001
ADAM_B1 = 0.9
ADAM_B2 = 0.999
ADAM_EPS = 1e-08
ADAM_WD = 0.01
ADAM_STEP = 10

V7X_VMEM_BYTES = 64 * 1024 * 1024
VMEM_LIMIT = V7X_VMEM_BYTES - 8 * 1024 * 1024
LANES = 128
QG_FWD = 4 * CHUNK
QG_BWD = 2 * CHUNK
LEFT = BAND - CHUNK
TQ = 512
TM = 256
SMALL_COLS = 1024
MESH = pl.DeviceIdType.MESH
NT = (((1,), (1,)), ((), ()))
TN = (((0,), (0,)), ((), ()))
TT = (((0,), (1,)), ((), ()))


def _cp(sem=None, vmem=VMEM_LIMIT):
    return pltpu.CompilerParams(dimension_semantics=sem, vmem_limit_bytes=vmem)


def _any():
    return pl.BlockSpec(memory_space=pl.ANY)


def _const(shape):
    nd = len(shape)
    return pl.BlockSpec(shape, lambda *_: (0,) * nd)


def _rms(v, g):
    r = lax.rsqrt(jnp.mean(v * v, axis=-1, keepdims=True) + EPS)
    return v * r * g


def _rms_bwd(dy, v, g):
    r = lax.rsqrt(jnp.mean(v * v, axis=-1, keepdims=True) + EPS)
    vh = v * r
    dg = jnp.sum(dy * vh, axis=0, keepdims=True)
    dvh = dy * g
    dv = r * (dvh - vh * jnp.mean(dvh * vh, axis=-1, keepdims=True))
    return dv, dg


def _group_mean(v, gm):
    return jnp.dot(v.astype(BF16), gm, preferred_element_type=F32)


def _group_rms_bwd(dy, v, g, gm):
    r = lax.rsqrt(_group_mean(v * v, gm) + EPS)
    vh = v * r
    dg = jnp.sum(dy * vh, axis=0, keepdims=True)
    dvh = dy * g
    dv = r * (dvh - vh * _group_mean(dvh * vh, gm))
    return dv, dg


def _head_masks(scale):
    lane = lax.broadcasted_iota(jnp.int32, (1, LANES), 1)
    return [jnp.where((lane >= HD * a) & (lane < HD * (a + 1)), scale, 0.0).astype(BF16) for a in range(2)]


class _Resident:
    def __init__(self, src, dst, sem):
        self.first = pl.program_id(0) == 0
        self.copy = pltpu.make_async_copy(src, dst, sem)
        self.dst = dst

        @pl.when(self.first)
        def _():
            self.copy.start()

    def read(self):
        @pl.when(self.first)
        def _():
            self.copy.wait()

        return self.dst[...]


FF_CHUNKS = ((0, 1536), (1536, DFF))


def _stream_ffn_weights(wfi_hbm, wfo_hbm, wfi_v, wfo_v, sems, order, step):
    hw = DFF // 2
    per_matrix = {
        0: [(wfi_hbm.at[j], wfi_v.at[0, :, pl.ds(hw * j, hw)]) for j in range(2)],
        1: [(wfi_hbm.at[2 + j], wfi_v.at[1, :, pl.ds(hw * j, hw)]) for j in range(2)],
        2: [(wfo_hbm.at[j], wfo_v.at[pl.ds(hw * j, hw), :]) for j in range(2)],
    }
    pieces = [p for m in order for p in per_matrix[m]]
    slot = {m: 2 * k for k, m in enumerate(order)}

    def make_step(wait):
        def ready(m, chunk):
            if chunk == 0:
                wait(slot[m])
                wait(slot[m] + 1)
        return lambda: step(ready)

    copies = [pltpu.make_async_copy(src, dst, sems.at[k]) for k, (src, dst) in enumerate(pieces)]
    first = pl.program_id(0) == 0

    @pl.when(first)
    def _():
        for cp in copies:
            cp.start()
        make_step(lambda k: copies[k].wait())()

    @pl.when(jnp.logical_not(first))
    def _():
        make_step(lambda k: None)()


def _conv_taps(u_prev, u, scr):
    n = u.shape[0]
    scr[0:16, :] = u_prev
    scr[16:16 + n, :] = u
    return scr[15:15 + n, :], scr[14:14 + n, :]


def fwd_inproj(x, g, w_all):
    t = x.shape[0]
    wc = PROJ // NCHIP

    def body(x_ref, g_ref, w_hbm, o_ref, w_v):
        @pl.when(pl.program_id(0) == 0)
        def _():
            pltpu.sync_copy(w_hbm, w_v)

        h = _rms(x_ref[...], g_ref[...]).astype(BF16)
        for b in range(NCHIP):
            o_ref[:, wc * b:wc * (b + 1)] = jnp.dot(h, w_v[b], preferred_element_type=F32).astype(BF16)

    return pl.pallas_call(
        body, grid=(t // TQ,),
        in_specs=[pl.BlockSpec((TQ, D), lambda i: (i, 0)), _const((1, D)), _any()],
        out_specs=pl.BlockSpec((TQ, PROJ), lambda i: (i, 0)),
        out_shape=jax.ShapeDtypeStruct((t, PROJ), BF16),
        scratch_shapes=[pltpu.VMEM((NCHIP, D, wc), BF16)],
        compiler_params=_cp(("arbitrary",)), name="fwd_inproj")(x, g, w_all)


def _attn_window_specs():
    return [
        pl.BlockSpec((TQ, CW), lambda i: (i, 3)),
        pl.BlockSpec((TQ, CW), lambda i: (jnp.maximum(i - 1, 0), 4)),
        pl.BlockSpec((TQ, CW), lambda i: (i, 4)),
        pl.BlockSpec((TQ, CW), lambda i: (jnp.maximum(i - 1, 0), 5)),
        pl.BlockSpec((TQ, CW), lambda i: (i, 5)),
    ]


def _conv_specs():
    return [
        pl.BlockSpec((TQ, 3 * CW), lambda i: (i, 0)),
        pl.BlockSpec((16, 3 * CW), lambda i: (jnp.maximum(i * (TQ // 16) - 1, 0), 0)),
    ]


def _conv_fwd(pc_ref, pcp_ref, wc_ref, scr, first):
    pc = pc_ref[...].astype(F32)
    hc, bg, cg = pc[:, :CW], pc[:, CW:2 * CW], pc[:, 2 * CW:]
    u = cg * hc
    pp = pcp_ref[...].astype(F32)
    u_prev = jnp.where(first, 0.0, pp[:, 2 * CW:] * pp[:, :CW])
    u1, u2 = _conv_taps(u_prev, u, scr)
    cout = wc_ref[0:1, :] * u2 + wc_ref[1:2, :] * u1 + wc_ref[2:3, :] * u
    return hc, bg, cg, u, u1, u2, cout


def _key_penalty(first, r0, kg):
    col = lax.broadcasted_iota(jnp.int32, (1, kg), 1)
    limit = jnp.where(first, TQ - r0, 0)
    return jnp.where(col < limit, NEG_INF, 0.0)


def fwd_mix(x, proj, bias2, wconv_t, g_co, g_ao, g_pm, gm, wout_all):
    t = x.shape[0]
    qg, kg = QG_FWD, QG_FWD + LEFT

    def body(x_ref, pc_ref, pcp_ref, q_ref, kp_ref, kc_ref, vp_ref, vc_ref, b2_ref, wc_ref, gco_ref, gao_ref, gpm_ref,
             gm_ref, wout_hbm, xmid_ref, o_ref, lse_ref, y_ref, z_ref, wout_v, kwin, vwin, cscr, sems):
        i = pl.program_id(0)
        first = i == 0
        wout = _Resident(wout_hbm, wout_v, sems.at[0])
        kwin[0:TQ, :] = kp_ref[...]
        kwin[TQ:2 * TQ, :] = kc_ref[...]
        vwin[0:TQ, :] = vp_ref[...]
        vwin[TQ:2 * TQ, :] = vc_ref[...]
        qmask = _head_masks(HD ** -0.5)
        low = lax.broadcasted_iota(jnp.int32, (1, LANES), 1) < HD

        def group(g, carry):
            r0 = pl.multiple_of(g * qg, qg)
            pen = _key_penalty(first, r0, kg)
            for hp in range(NH // 2):
                ls = slice(LANES * hp, LANES * (hp + 1))
                qb = q_ref[pl.ds(r0, qg), ls]
                q2 = jnp.concatenate([qb * qmask[0], qb * qmask[1]], axis=0)
                s = lax.dot_general(q2, kwin[pl.ds(r0, kg), ls], NT, preferred_element_type=F32)
                s = s + b2_ref[hp] + pen
                m = jnp.max(s, axis=-1, keepdims=True)
                p = jnp.exp(s - m)
                l = jnp.sum(p, axis=-1, keepdims=True)
                o2 = jnp.dot(p.astype(BF16), vwin[pl.ds(r0, kg), ls], preferred_element_type=F32) * (1.0 / l)
                lse2 = m + jnp.log(l)
                o_ref[pl.ds(r0, qg), ls] = jnp.where(low, o2[:qg], o2[qg:])
                lse_ref[pl.ds(r0, qg), ls] = jnp.where(low, lse2[:qg], lse2[qg:])
            return carry

        lax.fori_loop(0, TQ // qg, group, 0)

        _, bg, _, _, _, _, cout = _conv_fwd(pc_ref, pcp_ref, wc_ref, cscr, first)
        yc = bg * cout
        gmv = gm_ref[...]
        ycn = yc * lax.rsqrt(_group_mean(yc * yc, gmv) + EPS) * gco_ref[...]
        oa = o_ref[...]
        oan = oa * lax.rsqrt(_group_mean(oa * oa, gmv) + EPS) * gao_ref[...]
        y_ref[:, 0:CW] = ycn.astype(BF16)
        y_ref[:, CW:2 * CW] = oan.astype(BF16)
        z = jnp.dot(y_ref[...], wout.read(), preferred_element_type=F32)
        z_ref[...] = z
        xmid_ref[...] = x_ref[...] + _rms(z, gpm_ref[...])

    row = lambda w: pl.BlockSpec((TQ, w), lambda i: (i, 0))
    return pl.pallas_call(
        body, grid=(t // TQ,),
        in_specs=[row(D)] + _conv_specs() + _attn_window_specs() + [
            _const((NH // 2, 2 * qg, kg)), _const((8, CW)), _const((1, CW)), _const((1, CW)), _const((1, D)),
            _const((CW, CW)), _any()],
        out_specs=[row(D), row(CW), row(CW), row(D), row(D)],
        out_shape=[jax.ShapeDtypeStruct((t, D), F32), jax.ShapeDtypeStruct((t, CW), F32),
                   jax.ShapeDtypeStruct((t, CW), F32), jax.ShapeDtypeStruct((t, D), BF16),
                   jax.ShapeDtypeStruct((t, D), F32)],
        scratch_shapes=[pltpu.VMEM((D, D), BF16), pltpu.VMEM((2 * TQ, CW), BF16), pltpu.VMEM((2 * TQ, CW), BF16),
                        pltpu.VMEM((TQ + 16, CW), F32), pltpu.SemaphoreType.DMA((1,))],
        compiler_params=_cp(("arbitrary",)), name="fwd_mix",
    )(x, proj, proj, proj, proj, proj, proj, proj, bias2, wconv_t, g_co, g_ao, g_pm, gm, wout_all)


def fwd_ffn(xmid, g_pre, g_post, wfi_all, wfo_all):
    t = xmid.shape[0]

    def body(x_ref, gpre_ref, gpost_ref, wfi_hbm, wfo_hbm, gu_ref, f_ref, xo_ref, wfi_v, wfo_v, sems):
        def step(ready):
            xv = x_ref[...]
            h = _rms(xv, gpre_ref[...]).astype(BF16)
            f = jnp.zeros((TM, D), F32)
            for ci, (a, b) in enumerate(FF_CHUNKS):
                ready(0, ci)
                gate = jnp.dot(h, wfi_v[0, :, a:b], preferred_element_type=F32)
                ready(1, ci)
                up = jnp.dot(h, wfi_v[1, :, a:b], preferred_element_type=F32)
                gu_ref[:, a:b] = gate.astype(BF16)
                gu_ref[:, DFF + a:DFF + b] = up.astype(BF16)
                act = gate * (1.0 / (1.0 + jnp.exp(-gate))) * up
                ready(2, ci)
                f = f + jnp.dot(act.astype(BF16), wfo_v[a:b, :], preferred_element_type=F32)
            f_ref[...] = f
            xo_ref[...] = xv + _rms(f, gpost_ref[...])

        _stream_ffn_weights(wfi_hbm, wfo_hbm, wfi_v, wfo_v, sems, (0, 1, 2), step)

    row = lambda w: pl.BlockSpec((TM, w), lambda i: (i, 0))
    return pl.pallas_call(
        body, grid=(t // TM,),
        in_specs=[row(D), _const((1, D)), _const((1, D)), _any(), _any()],
        out_specs=[row(2 * DFF), row(D), row(D)],
        out_shape=[jax.ShapeDtypeStruct((t, 2 * DFF), BF16), jax.ShapeDtypeStruct((t, D), F32),
                   jax.ShapeDtypeStruct((t, D), F32)],
        scratch_shapes=[pltpu.VMEM((2, D, DFF), BF16), pltpu.VMEM((DFF, D), BF16), pltpu.SemaphoreType.DMA((6,))],
        compiler_params=_cp(("arbitrary",)), name="fwd_ffn")(xmid, g_pre, g_post, wfi_all, wfo_all)


def loss_head(y, target):
    t = y.shape[0]

    def body(y_ref, t_ref, dy_ref, l_ref):
        @pl.when(pl.program_id(0) == 0)
        def _():
            l_ref[...] = jnp.zeros_like(l_ref)

        e = y_ref[...] - t_ref[...]
        dy_ref[...] = e * (1.0 / D)
        rows = jnp.sum(e * e, axis=-1, keepdims=True) * (1.0 / D)
        l_ref[...] += 0.5 * jnp.sum(rows, axis=0, keepdims=True)

    row = pl.BlockSpec((TQ, D), lambda i: (i, 0))
    return pl.pallas_call(
        body, grid=(t // TQ,), in_specs=[row, row], out_specs=[row, _const((8, LANES))],
        out_shape=[jax.ShapeDtypeStruct((t, D), F32), jax.ShapeDtypeStruct((8, LANES), F32)],
        compiler_params=_cp(("arbitrary",)), name="loss_head")(y, target)


def bwd_ffn(dx, f, xmid, gu, g_pre, g_post, wfi_all, wfo_all):
    t = dx.shape[0]
    hw = DFF // 2

    def body(dx_ref, f_ref, x_ref, gu_ref, gpre_ref, gpost_ref, wfi_hbm, wfo_hbm,
             dxm_ref, df_ref, act_ref, dgu_ref, h_ref, dgpost_ref, dgpre_ref, wfi_v, wfo_v, sems):
        @pl.when(pl.program_id(0) == 0)
        def _():
            dgpost_ref[...] = jnp.zeros_like(dgpost_ref)
            dgpre_ref[...] = jnp.zeros_like(dgpre_ref)

        def step(ready):
            dxo = dx_ref[...]
            df, dgp = _rms_bwd(dxo, f_ref[...], gpost_ref[...])
            dgpost_ref[...] += dgp
            dfb = df.astype(BF16)
            df_ref[...] = dfb
            dh = jnp.zeros((TM, D), F32)
            for ci, (a, b) in enumerate(FF_CHUNKS):
                ready(2, ci)
                dact = lax.dot_general(dfb, wfo_v[a:b, :], NT, preferred_element_type=F32)
                gate = gu_ref[:, a:b].astype(F32)
                up = gu_ref[:, DFF + a:DFF + b].astype(F32)
                sig = 1.0 / (1.0 + jnp.exp(-gate))
                silu = gate * sig
                act_ref[:, a:b] = (silu * up).astype(BF16)
                dup = (dact * silu).astype(BF16)
                dgate = (dact * up * (sig * (1.0 + gate * (1.0 - sig)))).astype(BF16)
                dgu_ref[:, a:b] = dgate
                dgu_ref[:, DFF + a:DFF + b] = dup
                ready(0, ci)
                dh = dh + lax.dot_general(dgate, wfi_v[0, :, a:b], NT, preferred_element_type=F32)
                ready(1, ci)
                dh = dh + lax.dot_general(dup, wfi_v[1, :, a:b], NT, preferred_element_type=F32)
            xv = x_ref[...]
            gpre = gpre_ref[...]
            h_ref[...] = _rms(xv, gpre).astype(BF16)
            dxv, dgq = _rms_bwd(dh, xv, gpre)
            dgpre_ref[...] += dgq
            dxm_ref[...] = dxo + dxv

        _stream_ffn_weights(wfi_hbm, wfo_hbm, wfi_v, wfo_v, sems, (2, 0, 1), step)

    row = lambda w: pl.BlockSpec((TM, w), lambda i: (i, 0))
    return pl.pallas_call(
        body, grid=(t // TM,),
        in_specs=[row(D), row(D), row(D), row(2 * DFF), _const((1, D)), _const((1, D)), _any(), _any()],
        out_specs=[row(D), row(D), row(DFF), row(2 * DFF), row(D), _const((1, D)), _const((1, D))],
        out_shape=[jax.ShapeDtypeStruct((t, D), F32), jax.ShapeDtypeStruct((t, D), BF16),
                   jax.ShapeDtypeStruct((t, DFF), BF16), jax.ShapeDtypeStruct((t, 2 * DFF), BF16),
                   jax.ShapeDtypeStruct((t, D), BF16), jax.ShapeDtypeStruct((1, D), F32),
                   jax.ShapeDtypeStruct((1, D), F32)],
        scratch_shapes=[pltpu.VMEM((2, D, DFF), BF16), pltpu.VMEM((DFF, D), BF16), pltpu.SemaphoreType.DMA((6,))],
        compiler_params=_cp(("arbitrary",)), name="bwd_ffn")(dx, f, xmid, gu, g_pre, g_post, wfi_all, wfo_all)


def bwd_mix(dxm, z, o, proj, wconv_t, g_co, g_ao, g_pm, gm, wout_all):
    t = dxm.shape[0]

    def body(dx_ref, z_ref, o_ref, pc_ref, pcp_ref, wc_ref, gco_ref, gao_ref, gpm_ref, gm_ref, wout_hbm,
             dz_ref, do_ref, dco_ref, dbg_ref, dgpm_ref, dgco_ref, dgao_ref, wout_v, cscr):
        first = pl.program_id(0) == 0

        @pl.when(first)
        def _():
            pltpu.sync_copy(wout_hbm, wout_v)
            dgpm_ref[...] = jnp.zeros_like(dgpm_ref)
            dgco_ref[...] = jnp.zeros_like(dgco_ref)
            dgao_ref[...] = jnp.zeros_like(dgao_ref)

        dz, dgp = _rms_bwd(dx_ref[...], z_ref[...], gpm_ref[...])
        dgpm_ref[...] += dgp
        dzb = dz.astype(BF16)
        dz_ref[...] = dzb
        gmv = gm_ref[...]
        _, bg, _, _, _, _, cout = _conv_fwd(pc_ref, pcp_ref, wc_ref, cscr, first)
        dy_conv = lax.dot_general(dzb, wout_v[0:CW, :], NT, preferred_element_type=F32)
        dyc, dgc = _group_rms_bwd(dy_conv, bg * cout, gco_ref[...], gmv)
        dgco_ref[...] += dgc
        dbg_ref[...] = (dyc * cout).astype(BF16)
        dco_ref[...] = dyc * bg
        dy_attn = lax.dot_general(dzb, wout_v[CW:2 * CW, :], NT, preferred_element_type=F32)
        do, dga = _group_rms_bwd(dy_attn, o_ref[...], gao_ref[...], gmv)
        dgao_ref[...] += dga
        do_ref[...] = do.astype(BF16)

    row = lambda w: pl.BlockSpec((TQ, w), lambda i: (i, 0))
    return pl.pallas_call(
        body, grid=(t // TQ,),
        in_specs=[row(D), row(D), row(CW)] + _conv_specs() + [
            _const((8, CW)), _const((1, CW)), _const((1, CW)), _const((1, D)), _const((CW, CW)), _any()],
        out_specs=[row(D), row(CW), row(CW), row(CW), _const((1, D)), _const((1, CW)), _const((1, CW))],
        out_shape=[jax.ShapeDtypeStruct((t, D), BF16), jax.ShapeDtypeStruct((t, CW), BF16),
                   jax.ShapeDtypeStruct((t, CW), F32), jax.ShapeDtypeStruct((t, CW), BF16),
                   jax.ShapeDtypeStruct((1, D), F32), jax.ShapeDtypeStruct((1, CW), F32),
                   jax.ShapeDtypeStruct((1, CW), F32)],
        scratch_shapes=[pltpu.VMEM((D, D), BF16), pltpu.VMEM((TQ + 16, CW), F32)],
        compiler_params=_cp(("arbitrary",)), name="bwd_mix",
    )(dxm, z, o, proj, proj, wconv_t, g_co, g_ao, g_pm, gm, wout_all)


def bwd_conv(dco, proj, wconv_t):
    t = dco.shape[0]
    nt = t // TQ

    def body(d_ref, dn_ref, pc_ref, pcp_ref, wc_ref, dhc_ref, dcg_ref, dw_ref, cscr, dscr):
        i = pl.program_id(0)
        first = i == 0

        @pl.when(first)
        def _():
            dw_ref[...] = jnp.zeros_like(dw_ref)

        hc, _, cg, u, u1, u2, _ = _conv_fwd(pc_ref, pcp_ref, wc_ref, cscr, first)
        d0 = d_ref[...]
        dscr[0:TQ, :] = d0
        dscr[TQ:TQ + 8, :] = jnp.where(i == nt - 1, 0.0, dn_ref[...])
        d1 = dscr[1:TQ + 1, :]
        d2 = dscr[2:TQ + 2, :]
        du = wc_ref[2:3, :] * d0 + wc_ref[1:2, :] * d1 + wc_ref[0:1, :] * d2
        dhc_ref[...] = (du * cg).astype(BF16)
        dcg_ref[...] = (du * hc).astype(BF16)
        dw_ref[0:1, :] += jnp.sum(d0 * u2, axis=0, keepdims=True)
        dw_ref[1:2, :] += jnp.sum(d0 * u1, axis=0, keepdims=True)
        dw_ref[2:3, :] += jnp.sum(d0 * u, axis=0, keepdims=True)

    row = lambda w: pl.BlockSpec((TQ, w), lambda i: (i, 0))
    nxt = pl.BlockSpec((8, CW), lambda i: (jnp.minimum((i + 1) * (TQ // 8), t // 8 - 1), 0))
    return pl.pallas_call(
        body, grid=(nt,),
        in_specs=[row(CW), nxt] + _conv_specs() + [_const((8, CW))],
        out_specs=[row(CW), row(CW), _const((8, CW))],
        out_shape=[jax.ShapeDtypeStruct((t, CW), BF16), jax.ShapeDtypeStruct((t, CW), BF16),
                   jax.ShapeDtypeStruct((8, CW), F32)],
        scratch_shapes=[pltpu.VMEM((TQ + 16, CW), F32), pltpu.VMEM((TQ + 8, CW), F32)],
        compiler_params=_cp(("arbitrary",)), name="bwd_conv")(dco, dco, proj, proj, wconv_t)


def bwd_attn(proj, o, do, lse, bias2):
    t = o.shape[0]
    nt = t // TQ
    qg, kg = QG_BWD, QG_BWD + LEFT
    nkb = (t + TQ) // LANES

    def body(q_ref, kp_ref, kc_ref, vp_ref, vc_ref, o_ref, do_ref, lse_ref, b2_ref,
             dq_ref, dk_hbm, dv_hbm, db_hbm, kwin, vwin, dk_acc, dv_acc, db_acc):
        i = pl.program_id(0)
        first = i == 0

        @pl.when(first)
        def _():
            dk_acc[...] = jnp.zeros_like(dk_acc)
            dv_acc[...] = jnp.zeros_like(dv_acc)
            db_acc[...] = jnp.zeros_like(db_acc)

        kwin[0:TQ, :] = kp_ref[...]
        kwin[TQ:2 * TQ, :] = kc_ref[...]
        vwin[0:TQ, :] = vp_ref[...]
        vwin[TQ:2 * TQ, :] = vc_ref[...]
        scale = HD ** -0.5
        qmask = _head_masks(scale)
        vmask = _head_masks(1.0)
        low = lax.broadcasted_iota(jnp.int32, (1, LANES), 1) < HD
        low_rows = lax.broadcasted_iota(jnp.int32, (LANES, 1), 0) < HD

        def group(g, carry):
            r0 = pl.multiple_of(g * qg, qg)
            base = i * (TQ // LANES) + g * (qg // LANES)
            pen = _key_penalty(first, r0, kg)
            for hp in range(NH // 2):
                ls = slice(LANES * hp, LANES * (hp + 1))
                qb = q_ref[pl.ds(r0, qg), ls]
                kw = kwin[pl.ds(r0, kg), ls]
                dob = do_ref[pl.ds(r0, qg), ls]
                prod = dob.astype(F32) * o_ref[pl.ds(r0, qg), ls]
                lseb = lse_ref[pl.ds(r0, qg), ls]
                q2 = jnp.concatenate([qb * qmask[0], qb * qmask[1]], axis=0)
                do2 = jnp.concatenate([dob * vmask[0], dob * vmask[1]], axis=0)
                lse2 = jnp.concatenate([lseb[:, 0:1], lseb[:, HD:HD + 1]], axis=0)
                dsum = jnp.concatenate([jnp.sum(jnp.where(low, prod, 0.0), axis=-1, keepdims=True),
                                        jnp.sum(jnp.where(low, 0.0, prod), axis=-1, keepdims=True)], axis=0)
                s = lax.dot_general(q2, kw, NT, preferred_element_type=F32) + b2_ref[hp] + pen
                p = jnp.exp(s - lse2)
                dp = lax.dot_general(do2, vwin[pl.ds(r0, kg), ls], NT, preferred_element_type=F32)
                ds = p * (dp - dsum)
                db_acc[hp] += ds
                dsb = ds.astype(BF16)
                dqt = lax.dot_general(kw, dsb, TT, preferred_element_type=F32)
                dq = jnp.transpose(jnp.where(low_rows, dqt[:, :qg], dqt[:, qg:]))
                dq_ref[pl.ds(r0, qg), ls] = (dq * scale).astype(BF16)
                dkt = lax.dot_general(q2, dsb, TN, preferred_element_type=F32)
                dvt = lax.dot_general(do2, p.astype(BF16), TN, preferred_element_type=F32)
                for kb in range(kg // LANES):
                    dk_acc[base + kb, ls, :] += dkt[:, LANES * kb:LANES * (kb + 1)]
                    dv_acc[base + kb, ls, :] += dvt[:, LANES * kb:LANES * (kb + 1)]
            return carry

        lax.fori_loop(0, TQ // qg, group, 0)

        @pl.when(i == nt - 1)
        def _():
            pltpu.sync_copy(dk_acc, dk_hbm)
            pltpu.sync_copy(dv_acc, dv_hbm)
            pltpu.sync_copy(db_acc, db_hbm)

    row = lambda w: pl.BlockSpec((TQ, w), lambda i: (i, 0))
    return pl.pallas_call(
        body, grid=(nt,),
        in_specs=_attn_window_specs() + [row(CW), row(CW), row(CW), _const((NH // 2, 2 * qg, kg))],
        out_specs=[row(CW), _any(), _any(), _any()],
        out_shape=[jax.ShapeDtypeStruct((t, CW), BF16), jax.ShapeDtypeStruct((nkb, CW, LANES), F32),
                   jax.ShapeDtypeStruct((nkb, CW, LANES), F32), jax.ShapeDtypeStruct((NH // 2, 2 * qg, kg), F32)],
        scratch_shapes=[pltpu.VMEM((2 * TQ, CW), BF16), pltpu.VMEM((2 * TQ, CW), BF16),
                        pltpu.VMEM((nkb, CW, LANES), F32), pltpu.VMEM((nkb, CW, LANES), F32),
                        pltpu.VMEM((NH // 2, 2 * qg, kg), F32)],
        compiler_params=_cp(("arbitrary",)), name="bwd_attn",
    )(proj, proj, proj, proj, proj, o, do, lse, bias2)


def bwd_inproj(dxm, x, dhc, dbg, dcg, dq, dk, dv, g, w_all):
    t = x.shape[0]
    wc = PROJ // NCHIP

    def body(dxm_ref, x_ref, dhc_ref, dbg_ref, dcg_ref, dq_ref, dk_ref, dv_ref, g_ref, w_hbm,
             dx_ref, dp_ref, h_ref, dg_ref, w_v):
        @pl.when(pl.program_id(0) == 0)
        def _():
            pltpu.sync_copy(w_hbm, w_v)
            dg_ref[...] = jnp.zeros_like(dg_ref)

        dp_ref[:, 0:CW] = dhc_ref[...]
        dp_ref[:, CW:2 * CW] = dbg_ref[...]
        dp_ref[:, 2 * CW:3 * CW] = dcg_ref[...]
        dp_ref[:, 3 * CW:4 * CW] = dq_ref[...]
        for kb in range(TQ // LANES):
            rows = slice(LANES * kb, LANES * (kb + 1))
            dp_ref[rows, 4 * CW:5 * CW] = jnp.transpose(dk_ref[kb]).astype(BF16)
            dp_ref[rows, 5 * CW:6 * CW] = jnp.transpose(dv_ref[kb]).astype(BF16)
        dh = jnp.zeros((TQ, D), F32)
        for b in range(NCHIP):
            dh = dh + lax.dot_general(dp_ref[:, wc * b:wc * (b + 1)], w_v[b], NT, preferred_element_type=F32)
        xv = x_ref[...]
        gv = g_ref[...]
        h_ref[...] = _rms(xv, gv).astype(BF16)
        dxv, dgv = _rms_bwd(dh, xv, gv)
        dg_ref[...] += dgv
        dx_ref[...] = dxm_ref[...] + dxv

    row = lambda w: pl.BlockSpec((TQ, w), lambda i: (i, 0))
    pad = pl.BlockSpec((TQ // LANES, CW, LANES), lambda i: (i + 1, 0, 0))
    return pl.pallas_call(
        body, grid=(t // TQ,),
        in_specs=[row(D), row(D), row(CW), row(CW), row(CW), row(CW), pad, pad, _const((1, D)), _any()],
        out_specs=[row(D), row(PROJ), row(D), _const((1, D))],
        out_shape=[jax.ShapeDtypeStruct((t, D), F32), jax.ShapeDtypeStruct((t, PROJ), BF16),
                   jax.ShapeDtypeStruct((t, D), BF16), jax.ShapeDtypeStruct((1, D), F32)],
        scratch_shapes=[pltpu.VMEM((NCHIP, D, wc), BF16)],
        compiler_params=_cp(("arbitrary",)), name="bwd_inproj",
    )(dxm, x, dhc, dbg, dcg, dq, dk, dv, g, w_all)


def wgrad(a, b, kb, nb, by_columns, name):
    t, k = a.shape
    n = b.shape[1]
    tk = 512

    def body(a_ref, b_ref, o_ref):
        o_ref[...] = jnp.zeros_like(o_ref)
        for c in range(t // tk):
            o_ref[...] += lax.dot_general(a_ref[tk * c:tk * (c + 1), :], b_ref[tk * c:tk * (c + 1), :], TN,
                                          preferred_element_type=F32)

    if by_columns:
        assert nb == n // NCHIP
        out_spec = pl.BlockSpec((None, kb, nb), lambda ki, ni: (ni, ki, 0))
        out_shape = jax.ShapeDtypeStruct((NCHIP, k, nb), F32)
    else:
        assert nb == n
        out_spec = pl.BlockSpec((kb, nb), lambda ki, ni: (ki, 0))
        out_shape = jax.ShapeDtypeStruct((k, n), F32)
    return pl.pallas_call(
        body, grid=(k // kb, n // nb),
        in_specs=[pl.BlockSpec((t, kb), lambda ki, ni: (0, ki)), pl.BlockSpec((t, nb), lambda ki, ni: (0, ni))],
        out_specs=out_spec, out_shape=out_shape,
        compiler_params=_cp(("arbitrary", "arbitrary")), name=name)(a, b)


TOE = 1024
assert 2 * QG_FWD + LEFT <= TOE
N_FLAT = LEFT - REL_CLIP + 1
N_VAR = BAND - N_FLAT


def _diag_vector(table):
    last = table[:, 2 * REL_CLIP:]
    var = table[:, 2 * REL_CLIP - N_VAR:2 * REL_CLIP][:, ::-1]
    return jnp.concatenate([jnp.broadcast_to(last, (NH, N_FLAT)), var, jnp.broadcast_to(last, (NH, TOE - BAND))], axis=1)


def _diag_vector_bwd(dvec):
    dlast = jnp.sum(dvec[:, :N_FLAT], axis=1, keepdims=True) + jnp.sum(dvec[:, BAND:], axis=1, keepdims=True)
    dvar = dvec[:, N_FLAT:BAND][:, ::-1]
    return jnp.concatenate([jnp.zeros((NH, 2 * REL_CLIP - N_VAR), F32), dvar, dlast], axis=1)


def _band_valid(qg):
    r = lax.broadcasted_iota(jnp.int32, (qg, qg + LEFT), 0)
    p = lax.broadcasted_iota(jnp.int32, (qg, qg + LEFT), 1)
    start = lax.shift_left(lax.shift_right_logical(r, 6), 6)
    return (p >= start) & (p < start + BAND)


def bias_expand(vec, qgs):
    def body(v_ref, *o_refs):
        for qg, o_ref in zip(qgs, o_refs):
            valid = _band_valid(qg)
            for h in range(NH):
                rows = jnp.broadcast_to(v_ref[h:h + 1, :], (qg, TOE))
                toe = pltpu.roll(rows, 0, 1, stride=1, stride_axis=0)
                o_ref[h // 2, qg * (h % 2):qg * (h % 2 + 1), :] = jnp.where(valid, toe[:, :qg + LEFT], NEG_INF)

    return pl.pallas_call(body, out_shape=[jax.ShapeDtypeStruct((NH // 2, 2 * qg, qg + LEFT), F32) for qg in qgs],
                          name="bias_expand")(vec)


def bias_reduce(db2):
    _, qg, kg = db2.shape

    def body(d_ref, o_ref):
        ii = lax.broadcasted_iota(jnp.int32, (kg, kg), 0)
        jj = lax.broadcasted_iota(jnp.int32, (kg, kg), 1)
        flip = jnp.where(ii + jj == kg - 1, 1.0, 0.0).astype(BF16)
        for h in range(NH):
            rest = d_ref[h]
            rev = jnp.zeros((qg, kg), F32)
            for _ in range(3):
                term = rest.astype(BF16)
                rev = rev + jnp.dot(term, flip, preferred_element_type=F32)
                rest = rest - term.astype(F32)
            d = jnp.concatenate([jnp.zeros((qg, TOE - kg), F32), rev], axis=1)
            back = pltpu.roll(d, 0, 1, stride=1, stride_axis=0)
            o_ref[h:h + 1, :] = jnp.sum(back, axis=0, keepdims=True)

    rev = pl.pallas_call(body, out_shape=jax.ShapeDtypeStruct((NH, TOE), F32), name="bias_reduce")(db2)
    return rev[:, ::-1]


def _place():
    x, y, c = lax.axis_index("x"), lax.axis_index("y"), lax.axis_index("c")
    chips = [(1 - x, y), (x, 1 - y), (1 - x, 1 - y)]
    return x, y, c, chips


def _half(ref_rows, c):
    return pl.ds(c * (ref_rows // 2), ref_rows // 2)


HBM_SPEC = pl.BlockSpec(memory_space=pltpu.HBM)
SEM_SPEC = pl.BlockSpec(memory_space=pltpu.SEMAPHORE)
IN_FLIGHT = pltpu.CompilerParams(has_side_effects=pltpu.SideEffectType.DATAFLOW_SIDE_EFFECTING)


def _in_hbm(a):
    return pltpu.with_memory_space_constraint(a, pltpu.HBM)


def cast_to_slot(ws, chip, layer):
    n = len(ws)
    steps = 4

    def body(b_ref, *refs):
        del b_ref
        for w_ref, o_ref in zip(refs[:n], refs[n:]):
            o_ref[...] = w_ref[...].astype(BF16)

    grid_spec = pltpu.PrefetchScalarGridSpec(
        num_scalar_prefetch=1, grid=(steps,),
        in_specs=[pl.BlockSpec((None, w.shape[1] // steps, w.shape[2]), lambda r, b: (layer, r, 0)) for w in ws],
        out_specs=[pl.BlockSpec((None, w.shape[1] // steps, w.shape[2]), lambda r, b: (b[0], r, 0)) for w in ws])
    return pl.pallas_call(body, grid_spec=grid_spec,
                          out_shape=[jax.ShapeDtypeStruct((NCHIP,) + w.shape[1:], BF16) for w in ws],
                          compiler_params=_cp(("arbitrary",)), name="cast_to_slot")(chip, *ws)


def _gather_copies(bufs, send, recv):
    x, y, c, chips = _place()
    b = 2 * x + y
    out = []
    for k, buf in enumerate(bufs):
        rows = buf.shape[1]
        mine = buf.at[b, _half(rows, c), :]
        for j, (cx, cy) in enumerate(chips):
            theirs = buf.at[2 * cx + cy, _half(rows, c), :]
            sems = dict(send_sem=send.at[3 * k + j], recv_sem=recv.at[3 * k + j],
                        device_id=(cx, cy, c), device_id_type=MESH)
            out.append((pltpu.make_async_remote_copy(src_ref=mine, dst_ref=mine, **sems),
                        pltpu.make_async_remote_copy(src_ref=theirs, dst_ref=theirs, **sems)))
    return out


def gather_start(bufs, after, layer):
    n = len(bufs)

    def body(*refs):
        ins = refs[:n]
        send, recv = refs[n + 1], refs[n + 2]
        token = refs[-1]
        for start, _ in _gather_copies(ins, send, recv):
            start.start()
        token[...] = jnp.zeros_like(token)

    sems = pltpu.SemaphoreType.DMA((3 * n,))
    res = pl.pallas_call(
        body, name=f"gather_start_{layer}",
        in_specs=[HBM_SPEC] * n + [_any()],
        out_specs=[SEM_SPEC, SEM_SPEC] + [HBM_SPEC] * n + [pl.BlockSpec(memory_space=pltpu.VMEM)],
        out_shape=[sems, sems] + [pltpu.HBM(b.shape, b.dtype) for b in bufs] + [jax.ShapeDtypeStruct((8, LANES), F32)],
        input_output_aliases={k: 2 + k for k in range(n)}, compiler_params=IN_FLIGHT,
    )(*[_in_hbm(b) for b in bufs], after)
    return res[0], res[1], res[2:2 + n], res[-1]


def gather_wait(send, recv, bufs, after, layer):
    n = len(bufs)

    def body(*refs):
        ins = refs[:n]
        send_ref, recv_ref = refs[n], refs[n + 1]
        for start, arrival in _gather_copies(ins, send_ref, recv_ref):
            start.wait_send()
            arrival.wait_recv()

    return pl.pallas_call(
        body, name=f"gather_wait_{layer}",
        in_specs=[HBM_SPEC] * n + [SEM_SPEC, SEM_SPEC, _any()], out_specs=[HBM_SPEC] * n,
        out_shape=[pltpu.HBM(b.shape, b.dtype) for b in bufs],
        input_output_aliases={k: k for k in range(n)}, compiler_params=IN_FLIGHT,
    )(*bufs, send, recv, after)


def gather_forward(bufs):
    n = len(bufs)

    def body(*refs):
        outs = refs[n:2 * n]
        send, recv = refs[2 * n:]
        x, y, c, chips = _place()
        cps = []
        for k in range(n):
            rows = outs[k].shape[1]
            for j, (cx, cy) in enumerate(chips):
                sems = dict(send_sem=send.at[3 * k + j], recv_sem=recv.at[3 * k + j],
                            device_id=(x, y, 1 - c), device_id_type=MESH)
                mine = outs[k].at[2 * cx + cy, _half(rows, c), :]
                theirs = outs[k].at[2 * cx + cy, _half(rows, 1 - c), :]
                cp = pltpu.make_async_remote_copy(src_ref=mine, dst_ref=mine, **sems)
                cp.start()
                cps.append((cp, pltpu.make_async_remote_copy(src_ref=theirs, dst_ref=theirs, **sems)))
        for cp, arrival in cps:
            cp.wait_send()
            arrival.wait_recv()

    return pl.pallas_call(
        body, in_specs=[_any()] * n, out_specs=[_any()] * n,
        out_shape=[jax.ShapeDtypeStruct(b.shape, b.dtype) for b in bufs], input_output_aliases={k: k for k in range(n)},
        scratch_shapes=[pltpu.SemaphoreType.DMA((3 * n,)), pltpu.SemaphoreType.DMA((3 * n,))],
        name="gather_forward")(*bufs)


def _forward_copies(bufs, send, recv):
    x, y, c, chips = _place()
    out = []
    for k, buf in enumerate(bufs):
        rows = buf.shape[1]
        for j, (cx, cy) in enumerate(chips):
            sems = dict(send_sem=send.at[3 * k + j], recv_sem=recv.at[3 * k + j],
                        device_id=(x, y, 1 - c), device_id_type=MESH)
            mine = buf.at[2 * cx + cy, _half(rows, c), :]
            theirs = buf.at[2 * cx + cy, _half(rows, 1 - c), :]
            out.append((pltpu.make_async_remote_copy(src_ref=mine, dst_ref=mine, **sems),
                        pltpu.make_async_remote_copy(src_ref=theirs, dst_ref=theirs, **sems)))
    return out


def forward_start(bufs, tag):
    n = len(bufs)

    def body(*refs):
        ins = refs[:n]
        send, recv = refs[n], refs[n + 1]
        token = refs[-1]
        for start, _ in _forward_copies(ins, send, recv):
            start.start()
        token[...] = jnp.zeros_like(token)

    sems = pltpu.SemaphoreType.DMA((3 * n,))
    res = pl.pallas_call(
        body, name=f"forward_start_{tag}", in_specs=[HBM_SPEC] * n,
        out_specs=[SEM_SPEC, SEM_SPEC] + [HBM_SPEC] * n + [pl.BlockSpec(memory_space=pltpu.VMEM)],
        out_shape=[sems, sems] + [pltpu.HBM(b.shape, b.dtype) for b in bufs] + [jax.ShapeDtypeStruct((8, LANES), F32)],
        input_output_aliases={k: 2 + k for k in range(n)}, compiler_params=IN_FLIGHT,
    )(*[_in_hbm(b) for b in bufs])
    return res[0], res[1], res[2:2 + n], res[-1]


def forward_wait(send, recv, bufs, after, tag):
    n = len(bufs)

    def body(*refs):
        ins = refs[:n]
        send_ref, recv_ref = refs[n], refs[n + 1]
        for start, arrival in _forward_copies(ins, send_ref, recv_ref):
            start.wait_send()
            arrival.wait_recv()

    return pl.pallas_call(
        body, name=f"forward_wait_{tag}",
        in_specs=[HBM_SPEC] * n + [SEM_SPEC, SEM_SPEC, _any()], out_specs=[HBM_SPEC] * n,
        out_shape=[pltpu.HBM(b.shape, b.dtype) for b in bufs],
        input_output_aliases={k: k for k in range(n)}, compiler_params=IN_FLIGHT,
    )(*bufs, send, recv, after)


def _exchange_copies(srcs, lands, send, recv):
    x, y, c, _ = _place()
    return [pltpu.make_async_remote_copy(
        src_ref=src.at[:, _half(src.shape[1], 1 - c), :], dst_ref=land, send_sem=send.at[k], recv_sem=recv.at[k],
        device_id=(x, y, 1 - c), device_id_type=MESH) for k, (src, land) in enumerate(zip(srcs, lands))]


def exchange_start(srcs, tag):
    n = len(srcs)
    lands = [lax.empty((s.shape[0], s.shape[1] // 2, s.shape[2]), s.dtype) for s in srcs]

    def body(*refs):
        ins, land_refs = refs[:n], refs[n:2 * n]
        send, recv = refs[2 * n], refs[2 * n + 1]
        token = refs[-1]
        for cp in _exchange_copies(ins, land_refs, send, recv):
            cp.start()
        token[...] = jnp.zeros_like(token)

    sems = pltpu.SemaphoreType.DMA((n,))
    res = pl.pallas_call(
        body, name=f"exchange_start_{tag}",
        in_specs=[HBM_SPEC] * (2 * n),
        out_specs=[SEM_SPEC, SEM_SPEC] + [HBM_SPEC] * (2 * n) + [pl.BlockSpec(memory_space=pltpu.VMEM)],
        out_shape=[sems, sems] + [pltpu.HBM(a.shape, a.dtype) for a in list(srcs) + lands]
        + [jax.ShapeDtypeStruct((8, LANES), F32)],
        input_output_aliases={k: 2 + k for k in range(2 * n)}, compiler_params=IN_FLIGHT,
    )(*[_in_hbm(a) for a in list(srcs) + lands])
    return res[0], res[1], res[2:2 + n], res[2 + n:2 + 2 * n], res[-1]


def exchange_wait(send, recv, srcs, lands, after, tag):
    n = len(srcs)

    def body(*refs):
        ins, land_refs = refs[:n], refs[n:2 * n]
        send_ref, recv_ref = refs[2 * n], refs[2 * n + 1]
        for cp in _exchange_copies(ins, land_refs, send_ref, recv_ref):
            cp.wait_send()
            cp.wait_recv()

    res = pl.pallas_call(
        body, name=f"exchange_wait_{tag}",
        in_specs=[HBM_SPEC] * (2 * n) + [SEM_SPEC, SEM_SPEC, _any()], out_specs=[HBM_SPEC] * (2 * n),
        out_shape=[pltpu.HBM(a.shape, a.dtype) for a in list(srcs) + list(lands)],
        input_output_aliases={k: k for k in range(2 * n)}, compiler_params=IN_FLIGHT,
    )(*srcs, *lands, send, recv, after)
    return res[:n], res[n:]


def add_pair(gs, r1s, core):
    n = len(gs)

    def body(c_ref, *refs):
        del c_ref
        for g_ref, r_ref, o_ref in zip(refs[:n], refs[n:2 * n], refs[2 * n:]):
            o_ref[...] = (g_ref[...] + r_ref[...]).astype(BF16)

    blk = lambda r: (None,) + r.shape[1:]
    grid_spec = pltpu.PrefetchScalarGridSpec(
        num_scalar_prefetch=1, grid=(NCHIP,),
        in_specs=[pl.BlockSpec(blk(r), lambda s, c: (s, c[0], 0)) for r in r1s]
        + [pl.BlockSpec(blk(r), lambda s, c: (s, 0, 0)) for r in r1s],
        out_specs=[pl.BlockSpec(blk(r), lambda s, c: (s, 0, 0)) for r in r1s])
    return pl.pallas_call(body, grid_spec=grid_spec, out_shape=[jax.ShapeDtypeStruct(r.shape, BF16) for r in r1s],
                          compiler_params=_cp(("arbitrary",)), name="add_pair")(core, *gs, *r1s)


def _scatter_copies(srcs, lands, send, recv):
    _, _, c, chips = _place()
    out = []
    for k, (src, land) in enumerate(zip(srcs, lands)):
        for j, (cx, cy) in enumerate(chips):
            out.append(pltpu.make_async_remote_copy(
                src_ref=src.at[2 * cx + cy], dst_ref=land.at[j], send_sem=send.at[3 * k + j],
                recv_sem=recv.at[3 * k + j], device_id=(cx, cy, c), device_id_type=MESH))
    return out


def scatter_start(srcs, layer):
    n = len(srcs)
    srcs = list(srcs)
    lands = [lax.empty((3,) + s.shape[1:], s.dtype) for s in srcs]

    def body(*refs):
        ins, land_refs = refs[:n], refs[n:2 * n]
        send, recv = refs[2 * n], refs[2 * n + 1]
        token = refs[-1]
        for cp in _scatter_copies(ins, land_refs, send, recv):
            cp.start()
        token[...] = jnp.zeros_like(token)

    sems = pltpu.SemaphoreType.DMA((3 * n,))
    res = pl.pallas_call(
        body, name=f"scatter_start_{layer}",
        in_specs=[HBM_SPEC] * (2 * n),
        out_specs=[SEM_SPEC, SEM_SPEC] + [HBM_SPEC] * (2 * n) + [pl.BlockSpec(memory_space=pltpu.VMEM)],
        out_shape=[sems, sems] + [pltpu.HBM(a.shape, a.dtype) for a in srcs + lands]
        + [jax.ShapeDtypeStruct((8, LANES), F32)],
        input_output_aliases={k: 2 + k for k in range(2 * n)}, compiler_params=IN_FLIGHT,
    )(*[_in_hbm(a) for a in srcs + lands])
    return res[0], res[1], res[2:2 + n], res[2 + n:2 + 2 * n], res[-1]


def scatter_wait(send, recv, srcs, lands, after, layer):
    n = len(srcs)

    def body(*refs):
        ins, land_refs = refs[:n], refs[n:2 * n]
        send_ref, recv_ref = refs[2 * n], refs[2 * n + 1]
        for cp in _scatter_copies(ins, land_refs, send_ref, recv_ref):
            cp.wait_send()
            cp.wait_recv()

    res = pl.pallas_call(
        body, name=f"scatter_wait_{layer}",
        in_specs=[HBM_SPEC] * (2 * n) + [SEM_SPEC, SEM_SPEC, _any()], out_specs=[HBM_SPEC] * (2 * n),
        out_shape=[pltpu.HBM(a.shape, a.dtype) for a in list(srcs) + list(lands)],
        input_output_aliases={k: k for k in range(2 * n)}, compiler_params=IN_FLIGHT,
    )(*srcs, *lands, send, recv, after)
    return res[n:]


def add_chips(gs, r1s, r2s, place, totals, layer):
    n = len(gs)
    steps = 2

    def body(p_ref, *refs):
        del p_ref
        for g_ref, r1_ref, r2_ref, o_ref in zip(refs[:n], refs[n:2 * n], refs[2 * n:3 * n], refs[4 * n:]):
            own = g_ref[...] + r1_ref[...]
            o_ref[...] = ((own + r2_ref[0].astype(F32)) + r2_ref[1].astype(F32)) + r2_ref[2].astype(F32)

    blk = lambda r: (None, r.shape[1] // steps, r.shape[2])
    grid_spec = pltpu.PrefetchScalarGridSpec(
        num_scalar_prefetch=1, grid=(steps,),
        in_specs=[pl.BlockSpec(blk(r), lambda i, p: (p[1], p[0] * steps + i, 0)) for r in r1s]
        + [pl.BlockSpec(blk(r), lambda i, p: (p[1], i, 0)) for r in r1s]
        + [pl.BlockSpec((3,) + blk(r)[1:], lambda i, p: (0, i, 0)) for r in r1s] + [_any()] * n,
        out_specs=[pl.BlockSpec(blk(r), lambda i, p: (layer, p[0] * steps + i, 0)) for r in r1s])
    return pl.pallas_call(body, grid_spec=grid_spec, out_shape=[jax.ShapeDtypeStruct(t.shape, F32) for t in totals],
                          input_output_aliases={1 + 3 * n + k: k for k in range(n)},
                          compiler_params=_cp(("arbitrary",)), name="add_chips")(place, *gs, *r1s, *r2s, *totals)


def pair_share(gs, tag):
    n = len(gs)

    def body(*refs):
        outs = refs[n:2 * n]
        send, recv = refs[2 * n:]
        x, y, c, _ = _place()
        cps = []
        for k in range(n):
            mine = outs[k].at[:, _half(outs[k].shape[1], c), :]
            cp = pltpu.make_async_remote_copy(
                src_ref=mine, dst_ref=mine, send_sem=send.at[k], recv_sem=recv.at[k],
                device_id=(x, y, 1 - c), device_id_type=MESH)
            cp.start()
            cps.append(cp)
        for k, cp in enumerate(cps):
            cp.wait_send()
            theirs = outs[k].at[:, _half(outs[k].shape[1], 1 - c), :]
            pltpu.make_async_remote_copy(
                src_ref=theirs, dst_ref=theirs, send_sem=send.at[k], recv_sem=recv.at[k],
                device_id=(x, y, 1 - c), device_id_type=MESH).wait_recv()

    return pl.pallas_call(
        body, in_specs=[_any()] * n, out_specs=[_any()] * n,
        out_shape=[jax.ShapeDtypeStruct(g.shape, g.dtype) for g in gs], input_output_aliases={k: k for k in range(n)},
        scratch_shapes=[pltpu.SemaphoreType.DMA((n,)), pltpu.SemaphoreType.DMA((n,))],
        name=f"pair_share_{tag}")(*gs)


def small_allreduce(v):
    rows = v.shape[0]
    flips = [(fx, fy, fc) for fx in (0, 1) for fy in (0, 1) for fc in (0, 1)][1:]

    def body(v_ref, o_ref, buf, send, recv):
        x, y, c, _ = _place()
        buf[4 * x + 2 * y + c] = v_ref[...]
        peers = [(jnp.where(fx, 1 - x, x), jnp.where(fy, 1 - y, y), jnp.where(fc, 1 - c, c)) for fx, fy, fc in flips]
        cps = []
        for k, peer in enumerate(peers):
            cp = pltpu.make_async_remote_copy(
                src_ref=v_ref, dst_ref=buf.at[4 * x + 2 * y + c], send_sem=send.at[k], recv_sem=recv.at[k],
                device_id=peer, device_id_type=MESH)
            cp.start()
            cps.append(cp)
        for k, (px, py, pc) in enumerate(peers):
            pltpu.make_async_remote_copy(
                src_ref=v_ref, dst_ref=buf.at[4 * px + 2 * py + pc], send_sem=send.at[k], recv_sem=recv.at[k],
                device_id=(px, py, pc), device_id_type=MESH).wait_recv()
        for cp in cps:
            cp.wait_send()
        acc = buf[0]
        for s in range(1, 8):
            acc = acc + buf[s]
        o_ref[...] = acc

    vm = pl.BlockSpec(memory_space=pltpu.VMEM)
    return pl.pallas_call(
        body, in_specs=[vm], out_specs=vm, out_shape=jax.ShapeDtypeStruct((rows, SMALL_COLS), F32),
        scratch_shapes=[pltpu.VMEM((8, rows, SMALL_COLS), F32), pltpu.SemaphoreType.DMA((7,)),
                        pltpu.SemaphoreType.DMA((7,))],
        name="reduce_small")(v)


def adamw(w, g, m, v, rb, name):
    nl, rows, cols = w.shape

    def body(w_ref, g_ref, m_ref, v_ref, go_ref, d_ref, nm_ref, nv_ref):
        gv = g_ref[...]
        go_ref[...] = gv
        nm = ADAM_B1 * m_ref[...] + (1.0 - ADAM_B1) * gv
        nv = ADAM_B2 * v_ref[...] + (1.0 - ADAM_B2) * (gv * gv)
        m_hat = nm / (1.0 - ADAM_B1 ** ADAM_STEP)
        v_hat = nv / (1.0 - ADAM_B2 ** ADAM_STEP)
        d_ref[...] = -ADAM_LR * (m_hat / (jnp.sqrt(v_hat) + ADAM_EPS) + ADAM_WD * w_ref[...])
        nm_ref[...] = nm
        nv_ref[...] = nv

    blk = pl.BlockSpec((None, rb, cols), lambda l, r: (l, r, 0))
    shp = jax.ShapeDtypeStruct(w.shape, F32)
    return pl.pallas_call(body, grid=(nl, rows // rb), in_specs=[blk] * 4, out_specs=[blk] * 4, out_shape=[shp] * 4,
                          compiler_params=_cp(("arbitrary", "arbitrary")), name=name)(w, g, m, v)


def _pack(parts, rows):
    flat = jnp.concatenate([p.reshape(-1).astype(F32) for p in parts])
    return jnp.pad(flat, (0, rows * SMALL_COLS - flat.shape[0])).reshape(rows, SMALL_COLS)


def _unpack(vec, shapes):
    flat = vec.reshape(-1)
    out, off = [], 0
    for s in shapes:
        size = 1
        for d in s:
            size *= d
        out.append(flat[off:off + size].reshape(s))
        off += size
    return out


def kernel(x, w_in, w_conv, rel_bias, g_conv_out, g_attn_out, w_out, g_pre_mix, g_post_mix, g_pre_ffn, g_post_ffn, w_ffn_in, w_ffn_out, loss_target, m_w_in, m_w_conv, m_rel_bias, m_g_conv_out, m_g_attn_out, m_w_out, m_g_pre_mix, m_g_post_mix, m_g_pre_ffn, m_g_post_ffn, m_w_ffn_in, m_w_ffn_out, v_w_in, v_w_conv, v_rel_bias, v_g_conv_out, v_g_attn_out, v_w_out, v_g_pre_mix, v_g_post_mix, v_g_pre_ffn, v_g_post_ffn, v_w_ffn_in, v_w_ffn_out):
    xi, yi, ci = lax.axis_index("x"), lax.axis_index("y"), lax.axis_index("c")
    chip = 2 * xi + yi
    nl = w_in.shape[0]
    x0 = x[0]
    target = loss_target[0]
    cwl = CW // NCHIP

    chip1 = chip.reshape(1).astype(jnp.int32)
    own = [cast_to_slot([w_in, w_out, w_ffn_in, w_ffn_out], chip1, l) for l in range(nl)]
    wc_mine = jnp.pad(w_conv.reshape(-1), (0, 16 * LANES - w_conv.size)).reshape(1, 16, LANES)
    wc_slot = lax.dynamic_update_slice_in_dim(jnp.zeros((NCHIP, 16, LANES), F32), wc_mine, chip, axis=0)
    gm = jnp.kron(jnp.eye(CW // HD, dtype=F32), jnp.full((HD, HD), 1.0 / HD, F32)).astype(BF16)
    row = lambda a, l: a[l][None, :]

    def token(t):
        return t[0:1, 0:1]

    def gather_finish(flight, after, tag):
        send, recv, bufs, _ = flight
        return gather_forward(gather_wait(send, recv, bufs, after, tag))

    first_mix = gather_start(list(own[0][:2]) + [wc_slot], x0, "0m")
    first_ffn = gather_start(own[0][2:], first_mix[3], "0f")
    gw_in, gw_out, wc_all = gather_finish(first_mix, x0, "0m")
    wc_full = wc_all.reshape(NCHIP, -1)[:, :nl * cwl * 3].reshape(NCHIP, nl, cwl, 3)
    wc_full = jnp.transpose(wc_full, (1, 0, 2, 3)).reshape(nl, CW, 3)
    wconv_t = jnp.pad(jnp.transpose(wc_full, (0, 2, 1)), ((0, 0), (0, 5), (0, 0)))
    flights, to_sibling = {}, None
    saved, weights = [], []
    h = x0
    for l in range(nl):
        if l == 0:
            pass
        elif l == 1:
            gw_in, gw_out, gw_fi, gw_fo = gather_finish(flights[l], h, l)
        else:
            gw_in, gw_out, gw_fi, gw_fo = forward_wait(*to_sibling[:3], h, l)
        gw_out = gw_out.reshape(D, D)
        g_pm, g_pf = row(g_pre_mix, l), row(g_pre_ffn, l)
        if l == 0:
            g_pm = g_pm + token(first_ffn[3])
        if l + 1 < nl and l + 1 not in flights:
            flights[l + 1] = gather_start(own[l + 1], first_ffn[3] if l == 0 else gw_in, l + 1)
            g_pm = g_pm + token(flights[l + 1][3])
        bias2, bias2_bwd = bias_expand(_diag_vector(rel_bias[l]), (QG_FWD, QG_BWD))
        proj = fwd_inproj(h, g_pm, gw_in)
        xmid, o, lse, y, z = fwd_mix(h, proj, bias2, wconv_t[l], row(g_conv_out, l), row(g_attn_out, l),
                                     row(g_post_mix, l), gm, gw_out)
        if l == 0:
            gw_fi, gw_fo = gather_finish(first_ffn, xmid, "0f")
        elif l + 1 < nl:
            send, recv, bufs, _ = flights[l + 1]
            landed = gather_wait(send, recv, bufs, xmid, l + 1)
            to_sibling = forward_start(landed, l + 1)
            g_pf = g_pf + token(to_sibling[3])
            if l + 2 < nl:
                flights[l + 2] = gather_start(own[l + 2], to_sibling[3], l + 2)
                g_pf = g_pf + token(flights[l + 2][3])
        gw_fo = gw_fo.reshape(2, DFF // 2, D)
        gu, f, xout = fwd_ffn(xmid, g_pf, row(g_post_ffn, l), gw_fi, gw_fo)
        saved.append((h, proj, bias2_bwd, xmid, o, lse, y, z, gu, f))
        weights.append((gw_in, gw_out, gw_fi, gw_fo))
        h = xout
    dx, loss_blk = loss_head(h, target)

    core = ci.reshape(1).astype(jnp.int32)
    place = jnp.stack([ci, chip]).astype(jnp.int32)
    totals = [lax.empty(w.shape, F32) for w in (w_in, w_out, w_ffn_in, w_ffn_out)]
    small = {k: [None] * nl for k in ("co", "ao", "pm", "qm", "pf", "qf", "rel", "wc")}

    def reduce_begin(kinds, grads, tag):
        return kinds, exchange_start(grads, tag), tag

    def reduce_mid(state, after):
        kinds, (send, recv, srcs, lands, _), tag = state
        grads, from_sibling = exchange_wait(send, recv, srcs, lands, after, tag)
        return kinds, grads, from_sibling, scatter_start(add_pair(grads, from_sibling, core), tag), tag

    def reduce_end(state, after, totals, layer):
        kinds, grads, from_sibling, (send, recv, srcs, lands, _), tag = state
        from_chips = scatter_wait(send, recv, srcs, lands, after, tag)
        totals = list(totals)
        summed = add_chips(grads, from_sibling, from_chips, place, [totals[i] for i in kinds], layer)
        for i, t in zip(kinds, summed):
            totals[i] = t
        return totals

    begun = flying = None
    for l in reversed(range(nl)):
        hin, proj, bias2, xmid, o, lse, y, z, gu, f = saved[l]
        gw_in, gw_out, gw_fi, gw_fo = weights[l]
        g_qf, g_qm, wct = row(g_post_ffn, l), row(g_post_mix, l), wconv_t[l]
        if begun is not None:
            g_qf = g_qf + token(begun[1][4])
        dxm, dfb, act, dgu, h2, dg_qf, dg_pf = bwd_ffn(dx, f, xmid, gu, row(g_pre_ffn, l), g_qf, gw_fi, gw_fo)
        if begun is not None:
            flying = reduce_mid(begun, dxm)
            g_qm = g_qm + token(flying[3][4])
        gr_fo = wgrad(act, dfb, 256, D, False, "wgrad_ffn_out").reshape(NCHIP, DFF // NCHIP, D)
        gr_fi = wgrad(h2, dgu, 512, 2 * DFF // NCHIP, True, "wgrad_ffn_in")
        if l == 0:
            begun_ffn = reduce_begin([2, 3], [gr_fi, gr_fo], "0f")
            g_qm = g_qm + token(begun_ffn[1][4])
        dzb, do, dco, dbg, dg_qm, dg_co, dg_ao = bwd_mix(dxm, z, o, proj, wct, row(g_conv_out, l),
                                                          row(g_attn_out, l), g_qm, gm, gw_out)
        if l == 0:
            flying_ffn = reduce_mid(begun_ffn, dzb)
            wct = wct + token(flying_ffn[3][4])
        gr_out = wgrad(y, dzb, 512, D, False, "wgrad_out").reshape(NCHIP, D // NCHIP, D)
        dhc, dcg, dwc = bwd_conv(dco, proj, wct)
        dq, dk, dv, db2 = bwd_attn(proj, o, do, lse, bias2)
        dx, dproj, hb, dg_pm = bwd_inproj(dxm, hin, dhc, dbg, dcg, dq, dk, dv, row(g_pre_mix, l), gw_in)
        if flying is not None:
            totals = reduce_end(flying, dx, totals, l + 1)
        gr_in = wgrad(hb, dproj, 512, PROJ // NCHIP, True, "wgrad_in")
        small["co"][l], small["ao"][l], small["pm"][l], small["qm"][l] = dg_co, dg_ao, dg_pm, dg_qm
        small["pf"][l], small["qf"][l] = dg_pf, dg_qf
        small["rel"][l] = _diag_vector_bwd(bias_reduce(db2.reshape(NH, QG_BWD, QG_BWD + LEFT)))
        small["wc"][l] = jnp.transpose(dwc[0:3], (1, 0))
        if l > 0:
            begun = reduce_begin([0, 1, 2, 3], [gr_in, gr_out, gr_fi, gr_fo], l)
    flying_mix = reduce_mid(reduce_begin([0, 1], [gr_in, gr_out], "0m"), dx)
    totals = reduce_end(flying_ffn, flying_mix[3][4], totals, 0)
    gr_fi, gr_fo = pair_share(totals[2:], "ffn")
    big_fi = adamw(w_ffn_in, gr_fi, m_w_ffn_in, v_w_ffn_in, w_ffn_in.shape[1] // 4, "adamw_ffn_in")
    big_fo = adamw(w_ffn_out, gr_fo, m_w_ffn_out, v_w_ffn_out, w_ffn_out.shape[1] // 4, "adamw_ffn_out")
    totals = reduce_end(flying_mix, big_fo[1], totals, 0)
    gr_in, gr_out = pair_share(totals[:2], "mix")
    big_in = adamw(w_in, gr_in, m_w_in, v_w_in, w_in.shape[1] // 4, "adamw_in")
    big_out = adamw(w_out, gr_out, m_w_out, v_w_out, w_out.shape[1] // 4, "adamw_out")
    big = [big_in, big_out, big_fi, big_fo]

    order = ("co", "ao", "pm", "qm", "pf", "qf", "rel", "wc")
    parts = [jnp.stack(small[k]) for k in order] + [loss_blk[0:1, 0:1]]
    shapes = [p.shape for p in parts]
    red = _unpack(small_allreduce(_pack(parts, 40)), shapes)
    gr_co, gr_ao, gr_pm, gr_qm, gr_pf, gr_qf, gr_rel, gr_wc_full, loss = red
    gr_co, gr_ao, gr_pm, gr_qm, gr_pf, gr_qf = [a.reshape(nl, -1) for a in (gr_co, gr_ao, gr_pm, gr_qm, gr_pf, gr_qf)]
    gr_wc = lax.dynamic_slice_in_dim(gr_wc_full, chip * cwl, cwl, axis=1)
    loss = loss.reshape(())

    sw = [g_conv_out, g_attn_out, g_pre_mix, g_post_mix, g_pre_ffn, g_post_ffn, rel_bias, w_conv]
    sg = [gr_co, gr_ao, gr_pm, gr_qm, gr_pf, gr_qf, gr_rel, gr_wc]
    sm = [m_g_conv_out, m_g_attn_out, m_g_pre_mix, m_g_post_mix, m_g_pre_ffn, m_g_post_ffn, m_rel_bias, m_w_conv]
    sv = [v_g_conv_out, v_g_attn_out, v_g_pre_mix, v_g_post_mix, v_g_pre_ffn, v_g_post_ffn, v_rel_bias, v_w_conv]
    sshapes = [a.shape for a in sw]
    packed = [_pack(a, 32)[None] for a in (sw, sg, sm, sv)]
    s_out = [_unpack(a[0], sshapes) for a in adamw(*packed, 32, "adamw_small")]

    def leaves(big_i, small_i):
        b_in, b_out, b_fi, b_fo = big_i
        s_co, s_ao, s_pm, s_qm, s_pf, s_qf, s_rel, s_wc = small_i
        return [b_in, s_wc, s_rel, s_co, s_ao, b_out, s_pm, s_qm, s_pf, s_qf, b_fi, b_fo]

    out = [loss, dx[None]]
    out += leaves([b[0] for b in big], sg)
    for i in range(1, 4):
        out += leaves([b[i] for b in big], s_out[i])
    return tuple(out)
```

```python
import jax
import jax.numpy as jnp
from jax import lax
from jax.experimental import pallas as pl
from jax.experimental.pallas import tpu as pltpu

F32 = jnp.float32
BF16 = jnp.bfloat16

D = 1024
PROJ = 3072
CW = 512
HD = 64
NH = 8
CHUNK = 64
BAND = 576
REL_CLIP = 128
NREL = 2 * REL_CLIP + 1
DFF = 2816
DEPTH = 4
NCHIP = 4
EPS = 1e-6
NEG_INF = -1e30

ADAM_LR = 0.001
ADAM_B1 = 0.9
ADAM_B2 = 0.999
ADAM_EPS = 1e-08
ADAM_WD = 0.01
ADAM_STEP = 10

V7X_VMEM_BYTES = 64 * 1024 * 1024
VMEM_LIMIT = V7X_VMEM_BYTES - 8 * 1024 * 1024
LANES = 128
QG_FWD = 4 * CHUNK
QG_BWD = 2 * CHUNK
LEFT = BAND - CHUNK
TQ = 512
TM = 256
SMALL_COLS = 1024
MESH = pl.DeviceIdType.MESH
NT = (((1,), (1,)), ((), ()))
TN = (((0,), (0,)), ((), ()))


def _cp(sem=None, vmem=VMEM_LIMIT):
    return pltpu.CompilerParams(dimension_semantics=sem, vmem_limit_bytes=vmem)


def _any():
    return pl.BlockSpec(memory_space=pl.ANY)


def _const(shape):
    nd = len(shape)
    return pl.BlockSpec(shape, lambda *_: (0,) * nd)


def _behind(body, n_in, after):
    def ordered(*refs):
        return body(*refs[:n_in], *refs[n_in + len(after):])
    return ordered


def _rms(v, g):
    r = lax.rsqrt(jnp.mean(v * v, axis=-1, keepdims=True) + EPS)
    return v * r * g


def _rms_bwd(dy, v, g):
    r = lax.rsqrt(jnp.mean(v * v, axis=-1, keepdims=True) + EPS)
    vh = v * r
    dg = jnp.sum(dy * vh, axis=0, keepdims=True)
    dvh = dy * g
    dv = r * (dvh - vh * jnp.mean(dvh * vh, axis=-1, keepdims=True))
    return dv, dg


def _group_mean(v, gm):
    return jnp.dot(v.astype(BF16), gm, preferred_element_type=F32)


def _group_rms_bwd(dy, v, g, gm):
    r = lax.rsqrt(_group_mean(v * v, gm) + EPS)
    vh = v * r
    dg = jnp.sum(dy * vh, axis=0, keepdims=True)
    dvh = dy * g
    dv = r * (dvh - vh * _group_mean(dvh * vh, gm))
    return dv, dg


def _head_masks(scale):
    lane = lax.broadcasted_iota(jnp.int32, (1, LANES), 1)
    return [jnp.where((lane >= HD * a) & (lane < HD * (a + 1)), scale, 0.0).astype(BF16) for a in range(2)]


class _Resident:
    def __init__(self, src, dst, sem):
        self.first = pl.program_id(0) == 0
        self.copy = pltpu.make_async_copy(src, dst, sem)
        self.dst = dst

        @pl.when(self.first)
        def _():
            self.copy.start()

    def read(self):
        @pl.when(self.first)
        def _():
            self.copy.wait()

        return self.dst[...]


FF_CHUNKS = ((0, 1536), (1536, DFF))


def _stream_ffn_weights(wfi_hbm, wfo_hbm, wfi_v, wfo_v, sems, order, step):
    hw = DFF // 2
    per_matrix = {
        0: [(wfi_hbm.at[j], wfi_v.at[0, :, pl.ds(hw * j, hw)]) for j in range(2)],
        1: [(wfi_hbm.at[2 + j], wfi_v.at[1, :, pl.ds(hw * j, hw)]) for j in range(2)],
        2: [(wfo_hbm.at[j], wfo_v.at[pl.ds(hw * j, hw), :]) for j in range(2)],
    }
    pieces = [p for m in order for p in per_matrix[m]]
    slot = {m: 2 * k for k, m in enumerate(order)}

    def make_step(wait):
        def ready(m, chunk):
            if chunk == 0:
                wait(slot[m])
                wait(slot[m] + 1)
        return lambda: step(ready)

    copies = [pltpu.make_async_copy(src, dst, sems.at[k]) for k, (src, dst) in enumerate(pieces)]
    first = pl.program_id(0) == 0

    @pl.when(first)
    def _():
        for cp in copies:
            cp.start()
        make_step(lambda k: copies[k].wait())()

    @pl.when(jnp.logical_not(first))
    def _():
        make_step(lambda k: None)()


def _conv_taps(u_prev, u, scr):
    n = u.shape[0]
    scr[0:16, :] = u_prev
    scr[16:16 + n, :] = u
    return scr[15:15 + n, :], scr[14:14 + n, :]


def fwd_inproj(x, g, w_all, after=()):
    t = x.shape[0]
    wc = PROJ // NCHIP

    def body(x_ref, g_ref, w_hbm, o_ref, w_v):
        @pl.when(pl.program_id(0) == 0)
        def _():
            pltpu.sync_copy(w_hbm, w_v)

        h = _rms(x_ref[...], g_ref[...]).astype(BF16)
        for b in range(NCHIP):
            o_ref[:, wc * b:wc * (b + 1)] = jnp.dot(h, w_v[b], preferred_element_type=F32).astype(BF16)

    return pl.pallas_call(
        _behind(body, 3, after), grid=(t // TQ,),
        in_specs=[pl.BlockSpec((TQ, D), lambda i: (i, 0)), _const((1, D)), _any()] + [_any()] * len(after),
        out_specs=pl.BlockSpec((TQ, PROJ), lambda i: (i, 0)),
        out_shape=jax.ShapeDtypeStruct((t, PROJ), BF16),
        scratch_shapes=[pltpu.VMEM((NCHIP, D, wc), BF16)],
        compiler_params=_cp(("arbitrary",)), name="fwd_inproj")(x, g, w_all, *after)


def _attn_window_specs():
    return [
        pl.BlockSpec((TQ, CW), lambda i: (i, 3)),
        pl.BlockSpec((TQ, CW), lambda i: (jnp.maximum(i - 1, 0), 4)),
        pl.BlockSpec((TQ, CW), lambda i: (i, 4)),
        pl.BlockSpec((TQ, CW), lambda i: (jnp.maximum(i - 1, 0), 5)),
        pl.BlockSpec((TQ, CW), lambda i: (i, 5)),
    ]


def _conv_specs():
    return [
        pl.BlockSpec((TQ, 3 * CW), lambda i: (i, 0)),
        pl.BlockSpec((16, 3 * CW), lambda i: (jnp.maximum(i * (TQ // 16) - 1, 0), 0)),
    ]


def _conv_fwd(pc_ref, pcp_ref, wc_ref, scr, first):
    pc = pc_ref[...].astype(F32)
    hc, bg, cg = pc[:, :CW], pc[:, CW:2 * CW], pc[:, 2 * CW:]
    u = cg * hc
    pp = pcp_ref[...].astype(F32)
    u_prev = jnp.where(first, 0.0, pp[:, 2 * CW:] * pp[:, :CW])
    u1, u2 = _conv_taps(u_prev, u, scr)
    cout = wc_ref[0:1, :] * u2 + wc_ref[1:2, :] * u1 + wc_ref[2:3, :] * u
    return hc, bg, cg, u, u1, u2, cout


def _key_penalty(first, r0, kg):
    col = lax.broadcasted_iota(jnp.int32, (1, kg), 1)
    limit = jnp.where(first, TQ - r0, 0)
    return jnp.where(col < limit, NEG_INF, 0.0)


def fwd_mix(x, proj, bias2, wconv_t, g_co, g_ao, g_pm, gm, wout_all):
    t = x.shape[0]
    qg, kg = QG_FWD, QG_FWD + LEFT

    def body(x_ref, pc_ref, pcp_ref, q_ref, kp_ref, kc_ref, vp_ref, vc_ref, b2_ref, wc_ref, gco_ref, gao_ref, gpm_ref,
             gm_ref, wout_hbm, xmid_ref, o_ref, lse_ref, y_ref, z_ref, wout_v, kwin, vwin, cscr, sems):
        i = pl.program_id(0)
        first = i == 0
        wout = _Resident(wout_hbm, wout_v, sems.at[0])
        kwin[0:TQ, :] = kp_ref[...]
        kwin[TQ:2 * TQ, :] = kc_ref[...]
        vwin[0:TQ, :] = vp_ref[...]
        vwin[TQ:2 * TQ, :] = vc_ref[...]
        qmask = _head_masks(HD ** -0.5)
        low = lax.broadcasted_iota(jnp.int32, (1, LANES), 1) < HD

        def group(g, carry):
            r0 = pl.multiple_of(g * qg, qg)
            pen = _key_penalty(first, r0, kg)
            for hp in range(NH // 2):
                ls = slice(LANES * hp, LANES * (hp + 1))
                qb = q_ref[pl.ds(r0, qg), ls]
                q2 = jnp.concatenate([qb * qmask[0], qb * qmask[1]], axis=0)
                s = lax.dot_general(q2, kwin[pl.ds(r0, kg), ls], NT, preferred_element_type=F32)
                s = s + b2_ref[hp] + pen
                m = jnp.max(s, axis=-1, keepdims=True)
                p = jnp.exp(s - m)
                l = jnp.sum(p, axis=-1, keepdims=True)
                o2 = jnp.dot(p.astype(BF16), vwin[pl.ds(r0, kg), ls], preferred_element_type=F32) * (1.0 / l)
                lse2 = m + jnp.log(l)
                o_ref[pl.ds(r0, qg), ls] = jnp.where(low, o2[:qg], o2[qg:])
                lse_ref[pl.ds(r0, qg), ls] = jnp.where(low, lse2[:qg], lse2[qg:])
            return carry

        lax.fori_loop(0, TQ // qg, group, 0)

        _, bg, _, _, _, _, cout = _conv_fwd(pc_ref, pcp_ref, wc_ref, cscr, first)
        yc = bg * cout
        gmv = gm_ref[...]
        ycn = yc * lax.rsqrt(_group_mean(yc * yc, gmv) + EPS) * gco_ref[...]
        oa = o_ref[...]
        oan = oa * lax.rsqrt(_group_mean(oa * oa, gmv) + EPS) * gao_ref[...]
        y_ref[:, 0:CW] = ycn.astype(BF16)
        y_ref[:, CW:2 * CW] = oan.astype(BF16)
        z = jnp.dot(y_ref[...], wout.read(), preferred_element_type=F32)
        z_ref[...] = z
        xmid_ref[...] = x_ref[...] + _rms(z, gpm_ref[...])

    row = lambda w: pl.BlockSpec((TQ, w), lambda i: (i, 0))
    return pl.pallas_call(
        body, grid=(t // TQ,),
        in_specs=[row(D)] + _conv_specs() + _attn_window_specs() + [
            _const((NH // 2, 2 * qg, kg)), _const((8, CW)), _const((1, CW)), _const((1, CW)), _const((1, D)),
            _const((CW, CW)), _any()],
        out_specs=[row(D), row(CW), row(CW), row(D), row(D)],
        out_shape=[jax.ShapeDtypeStruct((t, D), F32), jax.ShapeDtypeStruct((t, CW), F32),
                   jax.ShapeDtypeStruct((t, CW), F32), jax.ShapeDtypeStruct((t, D), BF16),
                   jax.ShapeDtypeStruct((t, D), F32)],
        scratch_shapes=[pltpu.VMEM((D, D), BF16), pltpu.VMEM((2 * TQ, CW), BF16), pltpu.VMEM((2 * TQ, CW), BF16),
                        pltpu.VMEM((TQ + 16, CW), F32), pltpu.SemaphoreType.DMA((1,))],
        compiler_params=_cp(("arbitrary",)), name="fwd_mix",
    )(x, proj, proj, proj, proj, proj, proj, proj, bias2, wconv_t, g_co, g_ao, g_pm, gm, wout_all)


def fwd_ffn(xmid, g_pre, g_post, wfi_all, wfo_all, after=()):
    t = xmid.shape[0]

    def body(x_ref, gpre_ref, gpost_ref, wfi_hbm, wfo_hbm, gu_ref, f_ref, xo_ref, wfi_v, wfo_v, sems):
        def step(ready):
            xv = x_ref[...]
            h = _rms(xv, gpre_ref[...]).astype(BF16)
            f = jnp.zeros((TM, D), F32)
            for ci, (a, b) in enumerate(FF_CHUNKS):
                ready(0, ci)
                gate = jnp.dot(h, wfi_v[0, :, a:b], preferred_element_type=F32)
                ready(1, ci)
                up = jnp.dot(h, wfi_v[1, :, a:b], preferred_element_type=F32)
                gu_ref[:, a:b] = gate.astype(BF16)
                gu_ref[:, DFF + a:DFF + b] = up.astype(BF16)
                act = gate * (1.0 / (1.0 + jnp.exp(-gate))) * up
                ready(2, ci)
                f = f + jnp.dot(act.astype(BF16), wfo_v[a:b, :], preferred_element_type=F32)
            f_ref[...] = f
            xo_ref[...] = xv + _rms(f, gpost_ref[...])

        _stream_ffn_weights(wfi_hbm, wfo_hbm, wfi_v, wfo_v, sems, (0, 1, 2), step)

    row = lambda w: pl.BlockSpec((TM, w), lambda i: (i, 0))
    return pl.pallas_call(
        _behind(body, 5, after), grid=(t // TM,),
        in_specs=[row(D), _const((1, D)), _const((1, D)), _any(), _any()] + [_any()] * len(after),
        out_specs=[row(2 * DFF), row(D), row(D)],
        out_shape=[jax.ShapeDtypeStruct((t, 2 * DFF), BF16), jax.ShapeDtypeStruct((t, D), F32),
                   jax.ShapeDtypeStruct((t, D), F32)],
        scratch_shapes=[pltpu.VMEM((2, D, DFF), BF16), pltpu.VMEM((DFF, D), BF16), pltpu.SemaphoreType.DMA((6,))],
        compiler_params=_cp(("arbitrary",)), name="fwd_ffn")(xmid, g_pre, g_post, wfi_all, wfo_all, *after)


def loss_head(y, target):
    t = y.shape[0]

    def body(y_ref, t_ref, dy_ref, l_ref):
        @pl.when(pl.program_id(0) == 0)
        def _():
            l_ref[...] = jnp.zeros_like(l_ref)

        e = y_ref[...] - t_ref[...]
        dy_ref[...] = e * (1.0 / D)
        rows = jnp.sum(e * e, axis=-1, keepdims=True) * (1.0 / D)
        l_ref[...] += 0.5 * jnp.sum(rows, axis=0, keepdims=True)

    row = pl.BlockSpec((TQ, D), lambda i: (i, 0))
    return pl.pallas_call(
        body, grid=(t // TQ,), in_specs=[row, row], out_specs=[row, _const((8, LANES))],
        out_shape=[jax.ShapeDtypeStruct((t, D), F32), jax.ShapeDtypeStruct((8, LANES), F32)],
        compiler_params=_cp(("arbitrary",)), name="loss_head")(y, target)


def bwd_ffn(dx, f, xmid, gu, g_pre, g_post, wfi_all, wfo_all, after=()):
    t = dx.shape[0]
    hw = DFF // 2

    def body(dx_ref, f_ref, x_ref, gu_ref, gpre_ref, gpost_ref, wfi_hbm, wfo_hbm,
             dxm_ref, df_ref, act_ref, dgu_ref, h_ref, dgpost_ref, dgpre_ref, wfi_v, wfo_v, sems):
        @pl.when(pl.program_id(0) == 0)
        def _():
            dgpost_ref[...] = jnp.zeros_like(dgpost_ref)
            dgpre_ref[...] = jnp.zeros_like(dgpre_ref)

        def step(ready):
            dxo = dx_ref[...]
            df, dgp = _rms_bwd(dxo, f_ref[...], gpost_ref[...])
            dgpost_ref[...] += dgp
            dfb = df.astype(BF16)
            df_ref[...] = dfb
            dh = jnp.zeros((TM, D), F32)
            for ci, (a, b) in enumerate(FF_CHUNKS):
                ready(2, ci)
                dact = lax.dot_general(dfb, wfo_v[a:b, :], NT, preferred_element_type=F32)
                gate = gu_ref[:, a:b].astype(F32)
                up = gu_ref[:, DFF + a:DFF + b].astype(F32)
                sig = 1.0 / (1.0 + jnp.exp(-gate))
                silu = gate * sig
                act_ref[:, a:b] = (silu * up).astype(BF16)
                dup = (dact * silu).astype(BF16)
                dgate = (dact * up * (sig * (1.0 + gate * (1.0 - sig)))).astype(BF16)
                dgu_ref[:, a:b] = dgate
                dgu_ref[:, DFF + a:DFF + b] = dup
                ready(0, ci)
                dh = dh + lax.dot_general(dgate, wfi_v[0, :, a:b], NT, preferred_element_type=F32)
                ready(1, ci)
                dh = dh + lax.dot_general(dup, wfi_v[1, :, a:b], NT, preferred_element_type=F32)
            xv = x_ref[...]
            gpre = gpre_ref[...]
            h_ref[...] = _rms(xv, gpre).astype(BF16)
            dxv, dgq = _rms_bwd(dh, xv, gpre)
            dgpre_ref[...] += dgq
            dxm_ref[...] = dxo + dxv

        _stream_ffn_weights(wfi_hbm, wfo_hbm, wfi_v, wfo_v, sems, (2, 0, 1), step)

    row = lambda w: pl.BlockSpec((TM, w), lambda i: (i, 0))
    return pl.pallas_call(
        _behind(body, 8, after), grid=(t // TM,),
        in_specs=[row(D), row(D), row(D), row(2 * DFF), _const((1, D)), _const((1, D)), _any(), _any()]
        + [_any()] * len(after),
        out_specs=[row(D), row(D), row(DFF), row(2 * DFF), row(D), _const((1, D)), _const((1, D))],
        out_shape=[jax.ShapeDtypeStruct((t, D), F32), jax.ShapeDtypeStruct((t, D), BF16),
                   jax.ShapeDtypeStruct((t, DFF), BF16), jax.ShapeDtypeStruct((t, 2 * DFF), BF16),
                   jax.ShapeDtypeStruct((t, D), BF16), jax.ShapeDtypeStruct((1, D), F32),
                   jax.ShapeDtypeStruct((1, D), F32)],
        scratch_shapes=[pltpu.VMEM((2, D, DFF), BF16), pltpu.VMEM((DFF, D), BF16), pltpu.SemaphoreType.DMA((6,))],
        compiler_params=_cp(("arbitrary",)), name="bwd_ffn")(dx, f, xmid, gu, g_pre, g_post, wfi_all, wfo_all, *after)


def bwd_mix(dxm, z, o, proj, wconv_t, g_co, g_ao, g_pm, gm, wout_all, after=()):
    t = dxm.shape[0]

    def body(dx_ref, z_ref, o_ref, pc_ref, pcp_ref, wc_ref, gco_ref, gao_ref, gpm_ref, gm_ref, wout_hbm,
             dz_ref, do_ref, dco_ref, dbg_ref, dgpm_ref, dgco_ref, dgao_ref, wout_v, cscr):
        first = pl.program_id(0) == 0

        @pl.when(first)
        def _():
            pltpu.sync_copy(wout_hbm, wout_v)
            dgpm_ref[...] = jnp.zeros_like(dgpm_ref)
            dgco_ref[...] = jnp.zeros_like(dgco_ref)
            dgao_ref[...] = jnp.zeros_like(dgao_ref)

        dz, dgp = _rms_bwd(dx_ref[...], z_ref[...], gpm_ref[...])
        dgpm_ref[...] += dgp
        dzb = dz.astype(BF16)
        dz_ref[...] = dzb
        gmv = gm_ref[...]
        _, bg, _, _, _, _, cout = _conv_fwd(pc_ref, pcp_ref, wc_ref, cscr, first)
        dy_conv = lax.dot_general(dzb, wout_v[0:CW, :], NT, preferred_element_type=F32)
        dyc, dgc = _group_rms_bwd(dy_conv, bg * cout, gco_ref[...], gmv)
        dgco_ref[...] += dgc
        dbg_ref[...] = (dyc * cout).astype(BF16)
        dco_ref[...] = dyc * bg
        dy_attn = lax.dot_general(dzb, wout_v[CW:2 * CW, :], NT, preferred_element_type=F32)
        do, dga = _group_rms_bwd(dy_attn, o_ref[...], gao_ref[...], gmv)
        dgao_ref[...] += dga
        do_ref[...] = do.astype(BF16)

    row = lambda w: pl.BlockSpec((TQ, w), lambda i: (i, 0))
    return pl.pallas_call(
        _behind(body, 11, after), grid=(t // TQ,),
        in_specs=[row(D), row(D), row(CW)] + _conv_specs() + [
            _const((8, CW)), _const((1, CW)), _const((1, CW)), _const((1, D)), _const((CW, CW)), _any()]
        + [_any()] * len(after),
        out_specs=[row(D), row(CW), row(CW), row(CW), _const((1, D)), _const((1, CW)), _const((1, CW))],
        out_shape=[jax.ShapeDtypeStruct((t, D), BF16), jax.ShapeDtypeStruct((t, CW), BF16),
                   jax.ShapeDtypeStruct((t, CW), F32), jax.ShapeDtypeStruct((t, CW), BF16),
                   jax.ShapeDtypeStruct((1, D), F32), jax.ShapeDtypeStruct((1, CW), F32),
                   jax.ShapeDtypeStruct((1, CW), F32)],
        scratch_shapes=[pltpu.VMEM((D, D), BF16), pltpu.VMEM((TQ + 16, CW), F32)],
        compiler_params=_cp(("arbitrary",)), name="bwd_mix",
    )(dxm, z, o, proj, proj, wconv_t, g_co, g_ao, g_pm, gm, wout_all, *after)


def bwd_conv(dco, proj, wconv_t, after=()):
    t = dco.shape[0]
    nt = t // TQ

    def body(d_ref, dn_ref, pc_ref, pcp_ref, wc_ref, dhc_ref, dcg_ref, dw_ref, cscr, dscr):
        i = pl.program_id(0)
        first = i == 0

        @pl.when(first)
        def _():
            dw_ref[...] = jnp.zeros_like(dw_ref)

        hc, _, cg, u, u1, u2, _ = _conv_fwd(pc_ref, pcp_ref, wc_ref, cscr, first)
        d0 = d_ref[...]
        dscr[0:TQ, :] = d0
        dscr[TQ:TQ + 8, :] = jnp.where(i == nt - 1, 0.0, dn_ref[...])
        d1 = dscr[1:TQ + 1, :]
        d2 = dscr[2:TQ + 2, :]
        du = wc_ref[2:3, :] * d0 + wc_ref[1:2, :] * d1 + wc_ref[0:1, :] * d2
        dhc_ref[...] = (du * cg).astype(BF16)
        dcg_ref[...] = (du * hc).astype(BF16)
        dw_ref[0:1, :] += jnp.sum(d0 * u2, axis=0, keepdims=True)
        dw_ref[1:2, :] += jnp.sum(d0 * u1, axis=0, keepdims=True)
        dw_ref[2:3, :] += jnp.sum(d0 * u, axis=0, keepdims=True)

    row = lambda w: pl.BlockSpec((TQ, w), lambda i: (i, 0))
    nxt = pl.BlockSpec((8, CW), lambda i: (jnp.minimum((i + 1) * (TQ // 8), t // 8 - 1), 0))
    return pl.pallas_call(
        _behind(body, 5, after), grid=(nt,),
        in_specs=[row(CW), nxt] + _conv_specs() + [_const((8, CW))] + [_any()] * len(after),
        out_specs=[row(CW), row(CW), _const((8, CW))],
        out_shape=[jax.ShapeDtypeStruct((t, CW), BF16), jax.ShapeDtypeStruct((t, CW), BF16),
                   jax.ShapeDtypeStruct((8, CW), F32)],
        scratch_shapes=[pltpu.VMEM((TQ + 16, CW), F32), pltpu.VMEM((TQ + 8, CW), F32)],
        compiler_params=_cp(("arbitrary",)), name="bwd_conv")(dco, dco, proj, proj, wconv_t, *after)


def bwd_attn(proj, o, do, lse, bias2):
    t = o.shape[0]
    nt = t // TQ
    qg, kg = QG_BWD, QG_BWD + LEFT
    nkb = (t + TQ) // LANES

    def body(q_ref, kp_ref, kc_ref, vp_ref, vc_ref, o_ref, do_ref, lse_ref, b2_ref,
             dq_ref, dk_hbm, dv_hbm, db_hbm, kwin, vwin, dk_acc, dv_acc, db_acc):
        i = pl.program_id(0)
        first = i == 0

        @pl.when(first)
        def _():
            dk_acc[...] = jnp.zeros_like(dk_acc)
            dv_acc[...] = jnp.zeros_like(dv_acc)
            db_acc[...] = jnp.zeros_like(db_acc)

        kwin[0:TQ, :] = kp_ref[...]
        kwin[TQ:2 * TQ, :] = kc_ref[...]
        vwin[0:TQ, :] = vp_ref[...]
        vwin[TQ:2 * TQ, :] = vc_ref[...]
        scale = HD ** -0.5
        qmask = _head_masks(scale)
        vmask = _head_masks(1.0)
        low = lax.broadcasted_iota(jnp.int32, (1, LANES), 1) < HD

        def group(g, carry):
            r0 = pl.multiple_of(g * qg, qg)
            base = i * (TQ // LANES) + g * (qg // LANES)
            pen = _key_penalty(first, r0, kg)
            for hp in range(NH // 2):
                ls = slice(LANES * hp, LANES * (hp + 1))
                qb = q_ref[pl.ds(r0, qg), ls]
                kw = kwin[pl.ds(r0, kg), ls]
                dob = do_ref[pl.ds(r0, qg), ls]
                prod = dob.astype(F32) * o_ref[pl.ds(r0, qg), ls]
                lseb = lse_ref[pl.ds(r0, qg), ls]
                q2 = jnp.concatenate([qb * qmask[0], qb * qmask[1]], axis=0)
                do2 = jnp.concatenate([dob * vmask[0], dob * vmask[1]], axis=0)
                lse2 = jnp.concatenate([lseb[:, 0:1], lseb[:, HD:HD + 1]], axis=0)
                dsum = jnp.concatenate([jnp.sum(jnp.where(low, prod, 0.0), axis=-1, keepdims=True),
                                        jnp.sum(jnp.where(low, 0.0, prod), axis=-1, keepdims=True)], axis=0)
                s = lax.dot_general(q2, kw, NT, preferred_element_type=F32) + b2_ref[hp] + pen
                p = jnp.exp(s - lse2)
                dp = lax.dot_general(do2, vwin[pl.ds(r0, kg), ls], NT, preferred_element_type=F32)
                ds = p * (dp - dsum)
                db_acc[hp] += ds
                dsb = ds.astype(BF16)
                dq2 = jnp.dot(dsb, kw, preferred_element_type=F32)
                dq_ref[pl.ds(r0, qg), ls] = (jnp.where(low, dq2[:qg], dq2[qg:]) * scale).astype(BF16)
                dkt = lax.dot_general(q2, dsb, TN, preferred_element_type=F32)
                dvt = lax.dot_general(do2, p.astype(BF16), TN, preferred_element_type=F32)
                for kb in range(kg // LANES):
                    dk_acc[base + kb, ls, :] += dkt[:, LANES * kb:LANES * (kb + 1)]
                    dv_acc[base + kb, ls, :] += dvt[:, LANES * kb:LANES * (kb + 1)]
            return carry

        lax.fori_loop(0, TQ // qg, group, 0)

        @pl.when(i == nt - 1)
        def _():
            pltpu.sync_copy(dk_acc, dk_hbm)
            pltpu.sync_copy(dv_acc, dv_hbm)
            pltpu.sync_copy(db_acc, db_hbm)

    row = lambda w: pl.BlockSpec((TQ, w), lambda i: (i, 0))
    return pl.pallas_call(
        body, grid=(nt,),
        in_specs=_attn_window_specs() + [row(CW), row(CW), row(CW), _const((NH // 2, 2 * qg, kg))],
        out_specs=[row(CW), _any(), _any(), _any()],
        out_shape=[jax.ShapeDtypeStruct((t, CW), BF16), jax.ShapeDtypeStruct((nkb, CW, LANES), F32),
                   jax.ShapeDtypeStruct((nkb, CW, LANES), F32), jax.ShapeDtypeStruct((NH // 2, 2 * qg, kg), F32)],
        scratch_shapes=[pltpu.VMEM((2 * TQ, CW), BF16), pltpu.VMEM((2 * TQ, CW), BF16),
                        pltpu.VMEM((nkb, CW, LANES), F32), pltpu.VMEM((nkb, CW, LANES), F32),
                        pltpu.VMEM((NH // 2, 2 * qg, kg), F32)],
        compiler_params=_cp(("arbitrary",)), name="bwd_attn",
    )(proj, proj, proj, proj, proj, o, do, lse, bias2)


def bwd_inproj(dxm, x, dhc, dbg, dcg, dq, dk, dv, g, w_all):
    t = x.shape[0]
    wc = PROJ // NCHIP

    def body(dxm_ref, x_ref, dhc_ref, dbg_ref, dcg_ref, dq_ref, dk_ref, dv_ref, g_ref, w_hbm,
             dx_ref, dp_ref, h_ref, dg_ref, w_v):
        @pl.when(pl.program_id(0) == 0)
        def _():
            pltpu.sync_copy(w_hbm, w_v)
            dg_ref[...] = jnp.zeros_like(dg_ref)

        dp_ref[:, 0:CW] = dhc_ref[...]
        dp_ref[:, CW:2 * CW] = dbg_ref[...]
        dp_ref[:, 2 * CW:3 * CW] = dcg_ref[...]
        dp_ref[:, 3 * CW:4 * CW] = dq_ref[...]
        for kb in range(TQ // LANES):
            rows = slice(LANES * kb, LANES * (kb + 1))
            dp_ref[rows, 4 * CW:5 * CW] = jnp.transpose(dk_ref[kb]).astype(BF16)
            dp_ref[rows, 5 * CW:6 * CW] = jnp.transpose(dv_ref[kb]).astype(BF16)
        dh = jnp.zeros((TQ, D), F32)
        for b in range(NCHIP):
            dh = dh + lax.dot_general(dp_ref[:, wc * b:wc * (b + 1)], w_v[b], NT, preferred_element_type=F32)
        xv = x_ref[...]
        gv = g_ref[...]
        h_ref[...] = _rms(xv, gv).astype(BF16)
        dxv, dgv = _rms_bwd(dh, xv, gv)
        dg_ref[...] += dgv
        dx_ref[...] = dxm_ref[...] + dxv

    row = lambda w: pl.BlockSpec((TQ, w), lambda i: (i, 0))
    pad = pl.BlockSpec((TQ // LANES, CW, LANES), lambda i: (i + 1, 0, 0))
    return pl.pallas_call(
        body, grid=(t // TQ,),
        in_specs=[row(D), row(D), row(CW), row(CW), row(CW), row(CW), pad, pad, _const((1, D)), _any()],
        out_specs=[row(D), row(PROJ), row(D), _const((1, D))],
        out_shape=[jax.ShapeDtypeStruct((t, D), F32), jax.ShapeDtypeStruct((t, PROJ), BF16),
                   jax.ShapeDtypeStruct((t, D), BF16), jax.ShapeDtypeStruct((1, D), F32)],
        scratch_shapes=[pltpu.VMEM((NCHIP, D, wc), BF16)],
        compiler_params=_cp(("arbitrary",)), name="bwd_inproj",
    )(dxm, x, dhc, dbg, dcg, dq, dk, dv, g, w_all)


def wgrad(a, b, kb, nb, by_columns, name):
    t, k = a.shape
    n = b.shape[1]
    tk = 512

    def body(a_ref, b_ref, o_ref):
        o_ref[...] = jnp.zeros_like(o_ref)
        for c in range(t // tk):
            o_ref[...] += lax.dot_general(a_ref[tk * c:tk * (c + 1), :], b_ref[tk * c:tk * (c + 1), :], TN,
                                          preferred_element_type=F32)

    if by_columns:
        assert nb == n // NCHIP
        out_spec = pl.BlockSpec((None, kb, nb), lambda ki, ni: (ni, ki, 0))
        out_shape = jax.ShapeDtypeStruct((NCHIP, k, nb), F32)
    else:
        assert nb == n
        out_spec = pl.BlockSpec((kb, nb), lambda ki, ni: (ki, 0))
        out_shape = jax.ShapeDtypeStruct((k, n), F32)
    return pl.pallas_call(
        body, grid=(k // kb, n // nb),
        in_specs=[pl.BlockSpec((t, kb), lambda ki, ni: (0, ki)), pl.BlockSpec((t, nb), lambda ki, ni: (0, ni))],
        out_specs=out_spec, out_shape=out_shape,
        compiler_params=_cp(("arbitrary", "arbitrary")), name=name)(a, b)


TOE = 1024
assert 2 * QG_FWD + LEFT <= TOE
N_FLAT = LEFT - REL_CLIP + 1
N_VAR = BAND - N_FLAT


def _diag_vector(table):
    last = table[:, 2 * REL_CLIP:]
    var = table[:, 2 * REL_CLIP - N_VAR:2 * REL_CLIP][:, ::-1]
    return jnp.concatenate([jnp.broadcast_to(last, (NH, N_FLAT)), var, jnp.broadcast_to(last, (NH, TOE - BAND))], axis=1)


def _diag_vector_bwd(dvec):
    dlast = jnp.sum(dvec[:, :N_FLAT], axis=1, keepdims=True) + jnp.sum(dvec[:, BAND:], axis=1, keepdims=True)
    dvar = dvec[:, N_FLAT:BAND][:, ::-1]
    return jnp.concatenate([jnp.zeros((NH, 2 * REL_CLIP - N_VAR), F32), dvar, dlast], axis=1)


def _band_valid(qg):
    r = lax.broadcasted_iota(jnp.int32, (qg, qg + LEFT), 0)
    p = lax.broadcasted_iota(jnp.int32, (qg, qg + LEFT), 1)
    start = lax.shift_left(lax.shift_right_logical(r, 6), 6)
    return (p >= start) & (p < start + BAND)


def bias_expand(vec, qgs):
    def body(v_ref, *o_refs):
        for qg, o_ref in zip(qgs, o_refs):
            valid = _band_valid(qg)
            for h in range(NH):
                rows = jnp.broadcast_to(v_ref[h:h + 1, :], (qg, TOE))
                toe = pltpu.roll(rows, 0, 1, stride=1, stride_axis=0)
                o_ref[h // 2, qg * (h % 2):qg * (h % 2 + 1), :] = jnp.where(valid, toe[:, :qg + LEFT], NEG_INF)

    return pl.pallas_call(body, out_shape=[jax.ShapeDtypeStruct((NH // 2, 2 * qg, qg + LEFT), F32) for qg in qgs],
                          name="bias_expand")(vec)


def bias_reduce(db2):
    _, qg, kg = db2.shape

    def body(d_ref, o_ref):
        ii = lax.broadcasted_iota(jnp.int32, (kg, kg), 0)
        jj = lax.broadcasted_iota(jnp.int32, (kg, kg), 1)
        flip = jnp.where(ii + jj == kg - 1, 1.0, 0.0).astype(BF16)
        for h in range(NH):
            rest = d_ref[h]
            rev = jnp.zeros((qg, kg), F32)
            for _ in range(3):
                term = rest.astype(BF16)
                rev = rev + jnp.dot(term, flip, preferred_element_type=F32)
                rest = rest - term.astype(F32)
            d = jnp.concatenate([jnp.zeros((qg, TOE - kg), F32), rev], axis=1)
            back = pltpu.roll(d, 0, 1, stride=1, stride_axis=0)
            o_ref[h:h + 1, :] = jnp.sum(back, axis=0, keepdims=True)

    rev = pl.pallas_call(body, out_shape=jax.ShapeDtypeStruct((NH, TOE), F32), name="bias_reduce")(db2)
    return rev[:, ::-1]


def _place():
    x, y, c = lax.axis_index("x"), lax.axis_index("y"), lax.axis_index("c")
    chips = [(1 - x, y), (x, 1 - y), (1 - x, 1 - y)]
    return x, y, c, chips


def _half(ref_rows, c):
    return pl.ds(c * (ref_rows // 2), ref_rows // 2)


HBM_SPEC = pl.BlockSpec(memory_space=pltpu.HBM)
SEM_SPEC = pl.BlockSpec(memory_space=pltpu.SEMAPHORE)
IN_FLIGHT = pltpu.CompilerParams(has_side_effects=pltpu.SideEffectType.DATAFLOW_SIDE_EFFECTING)


def _in_hbm(a):
    return pltpu.with_memory_space_constraint(a, pltpu.HBM)


def cast_to_slot(ws, chip, layer):
    n = len(ws)
    steps = 4

    def body(b_ref, *refs):
        del b_ref
        for w_ref, o_ref in zip(refs[:n], refs[n:]):
            o_ref[...] = w_ref[...].astype(BF16)

    grid_spec = pltpu.PrefetchScalarGridSpec(
        num_scalar_prefetch=1, grid=(steps,),
        in_specs=[pl.BlockSpec((None, w.shape[1] // steps, w.shape[2]), lambda r, b: (layer, r, 0)) for w in ws],
        out_specs=[pl.BlockSpec((None, w.shape[1] // steps, w.shape[2]), lambda r, b: (b[0], r, 0)) for w in ws])
    return pl.pallas_call(body, grid_spec=grid_spec,
                          out_shape=[jax.ShapeDtypeStruct((NCHIP,) + w.shape[1:], BF16) for w in ws],
                          compiler_params=_cp(("arbitrary",)), name="cast_to_slot")(chip, *ws)


def _gather_copies(bufs, send, recv):
    x, y, c, chips = _place()
    b = 2 * x + y
    out = []
    for k, buf in enumerate(bufs):
        rows = buf.shape[1]
        mine = buf.at[b, _half(rows, c), :]
        for j, (cx, cy) in enumerate(chips):
            theirs = buf.at[2 * cx + cy, _half(rows, c), :]
            sems = dict(send_sem=send.at[3 * k + j], recv_sem=recv.at[3 * k + j],
                        device_id=(cx, cy, c), device_id_type=MESH)
            out.append((pltpu.make_async_remote_copy(src_ref=mine, dst_ref=mine, **sems),
                        pltpu.make_async_remote_copy(src_ref=theirs, dst_ref=theirs, **sems)))
    return out


def gather_start(bufs, after, layer):
    n = len(bufs)

    def body(*refs):
        ins = refs[:n]
        send, recv = refs[n + 1], refs[n + 2]
        token = refs[-1]
        for start, _ in _gather_copies(ins, send, recv):
            start.start()
        token[...] = jnp.zeros_like(token)

    sems = pltpu.SemaphoreType.DMA((3 * n,))
    res = pl.pallas_call(
        body, name=f"gather_start_{layer}",
        in_specs=[HBM_SPEC] * n + [_any()],
        out_specs=[SEM_SPEC, SEM_SPEC] + [HBM_SPEC] * n + [pl.BlockSpec(memory_space=pltpu.VMEM)],
        out_shape=[sems, sems] + [pltpu.HBM(b.shape, b.dtype) for b in bufs] + [jax.ShapeDtypeStruct((8, LANES), F32)],
        input_output_aliases={k: 2 + k for k in range(n)}, compiler_params=IN_FLIGHT,
    )(*[_in_hbm(b) for b in bufs], after)
    return res[0], res[1], res[2:2 + n], res[-1]


def gather_wait(send, recv, bufs, after, layer):
    n = len(bufs)

    def body(*refs):
        ins = refs[:n]
        send_ref, recv_ref = refs[n], refs[n + 1]
        for start, arrival in _gather_copies(ins, send_ref, recv_ref):
            start.wait_send()
            arrival.wait_recv()

    return pl.pallas_call(
        body, name=f"gather_wait_{layer}",
        in_specs=[HBM_SPEC] * n + [SEM_SPEC, SEM_SPEC, _any()], out_specs=[HBM_SPEC] * n,
        out_shape=[pltpu.HBM(b.shape, b.dtype) for b in bufs],
        input_output_aliases={k: k for k in range(n)}, compiler_params=IN_FLIGHT,
    )(*bufs, send, recv, after)


def gather_forward(bufs):
    n = len(bufs)

    def body(*refs):
        outs = refs[n:2 * n]
        send, recv = refs[2 * n:]
        x, y, c, chips = _place()
        cps = []
        for k in range(n):
            rows = outs[k].shape[1]
            for j, (cx, cy) in enumerate(chips):
                sems = dict(send_sem=send.at[3 * k + j], recv_sem=recv.at[3 * k + j],
                            device_id=(x, y, 1 - c), device_id_type=MESH)
                mine = outs[k].at[2 * cx + cy, _half(rows, c), :]
                theirs = outs[k].at[2 * cx + cy, _half(rows, 1 - c), :]
                cp = pltpu.make_async_remote_copy(src_ref=mine, dst_ref=mine, **sems)
                cp.start()
                cps.append((cp, pltpu.make_async_remote_copy(src_ref=theirs, dst_ref=theirs, **sems)))
        for cp, arrival in cps:
            cp.wait_send()
            arrival.wait_recv()

    return pl.pallas_call(
        body, in_specs=[_any()] * n, out_specs=[_any()] * n,
        out_shape=[jax.ShapeDtypeStruct(b.shape, b.dtype) for b in bufs], input_output_aliases={k: k for k in range(n)},
        scratch_shapes=[pltpu.SemaphoreType.DMA((3 * n,)), pltpu.SemaphoreType.DMA((3 * n,))],
        name="gather_forward")(*bufs)


def _forward_copies(bufs, send, recv):
    x, y, c, chips = _place()
    out = []
    for k, buf in enumerate(bufs):
        rows = buf.shape[1]
        for j, (cx, cy) in enumerate(chips):
            sems = dict(send_sem=send.at[3 * k + j], recv_sem=recv.at[3 * k + j],
                        device_id=(x, y, 1 - c), device_id_type=MESH)
            mine = buf.at[2 * cx + cy, _half(rows, c), :]
            theirs = buf.at[2 * cx + cy, _half(rows, 1 - c), :]
            out.append((pltpu.make_async_remote_copy(src_ref=mine, dst_ref=mine, **sems),
                        pltpu.make_async_remote_copy(src_ref=theirs, dst_ref=theirs, **sems)))
    return out


def forward_start(bufs, tag):
    n = len(bufs)

    def body(*refs):
        ins = refs[:n]
        send, recv = refs[n], refs[n + 1]
        token = refs[-1]
        for start, _ in _forward_copies(ins, send, recv):
            start.start()
        token[...] = jnp.zeros_like(token)

    sems = pltpu.SemaphoreType.DMA((3 * n,))
    res = pl.pallas_call(
        body, name=f"forward_start_{tag}", in_specs=[HBM_SPEC] * n,
        out_specs=[SEM_SPEC, SEM_SPEC] + [HBM_SPEC] * n + [pl.BlockSpec(memory_space=pltpu.VMEM)],
        out_shape=[sems, sems] + [pltpu.HBM(b.shape, b.dtype) for b in bufs] + [jax.ShapeDtypeStruct((8, LANES), F32)],
        input_output_aliases={k: 2 + k for k in range(n)}, compiler_params=IN_FLIGHT,
    )(*[_in_hbm(b) for b in bufs])
    return res[0], res[1], res[2:2 + n], res[-1]


def forward_wait(send, recv, bufs, after, tag):
    n = len(bufs)

    def body(*refs):
        ins = refs[:n]
        send_ref, recv_ref = refs[n], refs[n + 1]
        for start, arrival in _forward_copies(ins, send_ref, recv_ref):
            start.wait_send()
            arrival.wait_recv()

    return pl.pallas_call(
        body, name=f"forward_wait_{tag}",
        in_specs=[HBM_SPEC] * n + [SEM_SPEC, SEM_SPEC, _any()], out_specs=[HBM_SPEC] * n,
        out_shape=[pltpu.HBM(b.shape, b.dtype) for b in bufs],
        input_output_aliases={k: k for k in range(n)}, compiler_params=IN_FLIGHT,
    )(*bufs, send, recv, after)


def _exchange_copies(srcs, lands, send, recv):
    x, y, c, _ = _place()
    return [pltpu.make_async_remote_copy(
        src_ref=src.at[:, _half(src.shape[1], 1 - c), :], dst_ref=land, send_sem=send.at[k], recv_sem=recv.at[k],
        device_id=(x, y, 1 - c), device_id_type=MESH) for k, (src, land) in enumerate(zip(srcs, lands))]


def exchange_start(srcs, tag):
    n = len(srcs)
    lands = [lax.empty((s.shape[0], s.shape[1] // 2, s.shape[2]), s.dtype) for s in srcs]

    def body(*refs):
        ins, land_refs = refs[:n], refs[n:2 * n]
        send, recv = refs[2 * n], refs[2 * n + 1]
        token = refs[-1]
        for cp in _exchange_copies(ins, land_refs, send, recv):
            cp.start()
        token[...] = jnp.zeros_like(token)

    sems = pltpu.SemaphoreType.DMA((n,))
    res = pl.pallas_call(
        body, name=f"exchange_start_{tag}",
        in_specs=[HBM_SPEC] * (2 * n),
        out_specs=[SEM_SPEC, SEM_SPEC] + [HBM_SPEC] * (2 * n) + [pl.BlockSpec(memory_space=pltpu.VMEM)],
        out_shape=[sems, sems] + [pltpu.HBM(a.shape, a.dtype) for a in list(srcs) + lands]
        + [jax.ShapeDtypeStruct((8, LANES), F32)],
        input_output_aliases={k: 2 + k for k in range(2 * n)}, compiler_params=IN_FLIGHT,
    )(*[_in_hbm(a) for a in list(srcs) + lands])
    return res[0], res[1], res[2:2 + n], res[2 + n:2 + 2 * n], res[-1]


def exchange_wait(send, recv, srcs, lands, after, tag):
    n = len(srcs)

    def body(*refs):
        ins, land_refs = refs[:n], refs[n:2 * n]
        send_ref, recv_ref = refs[2 * n], refs[2 * n + 1]
        for cp in _exchange_copies(ins, land_refs, send_ref, recv_ref):
            cp.wait_send()
            cp.wait_recv()

    res = pl.pallas_call(
        body, name=f"exchange_wait_{tag}",
        in_specs=[HBM_SPEC] * (2 * n) + [SEM_SPEC, SEM_SPEC, _any()], out_specs=[HBM_SPEC] * (2 * n),
        out_shape=[pltpu.HBM(a.shape, a.dtype) for a in list(srcs) + list(lands)],
        input_output_aliases={k: k for k in range(2 * n)}, compiler_params=IN_FLIGHT,
    )(*srcs, *lands, send, recv, after)
    return res[:n], res[n:]


def add_pair(gs, r1s, core):
    n = len(gs)

    def body(c_ref, *refs):
        del c_ref
        for g_ref, r_ref, o_ref in zip(refs[:n], refs[n:2 * n], refs[2 * n:]):
            o_ref[...] = (g_ref[...] + r_ref[...]).astype(BF16)

    blk = lambda r: (None,) + r.shape[1:]
    grid_spec = pltpu.PrefetchScalarGridSpec(
        num_scalar_prefetch=1, grid=(NCHIP,),
        in_specs=[pl.BlockSpec(blk(r), lambda s, c: (s, c[0], 0)) for r in r1s]
        + [pl.BlockSpec(blk(r), lambda s, c: (s, 0, 0)) for r in r1s],
        out_specs=[pl.BlockSpec(blk(r), lambda s, c: (s, 0, 0)) for r in r1s])
    return pl.pallas_call(body, grid_spec=grid_spec, out_shape=[jax.ShapeDtypeStruct(r.shape, BF16) for r in r1s],
                          compiler_params=_cp(("arbitrary",)), name="add_pair")(core, *gs, *r1s)


def _scatter_copies(srcs, lands, send, recv):
    _, _, c, chips = _place()
    out = []
    for k, (src, land) in enumerate(zip(srcs, lands)):
        for j, (cx, cy) in enumerate(chips):
            out.append(pltpu.make_async_remote_copy(
                src_ref=src.at[2 * cx + cy], dst_ref=land.at[j], send_sem=send.at[3 * k + j],
                recv_sem=recv.at[3 * k + j], device_id=(cx, cy, c), device_id_type=MESH))
    return out


def scatter_start(srcs, layer):
    n = len(srcs)
    srcs = list(srcs)
    lands = [lax.empty((3,) + s.shape[1:], s.dtype) for s in srcs]

    def body(*refs):
        ins, land_refs = refs[:n], refs[n:2 * n]
        send, recv = refs[2 * n], refs[2 * n + 1]
        token = refs[-1]
        for cp in _scatter_copies(ins, land_refs, send, recv):
            cp.start()
        token[...] = jnp.zeros_like(token)

    sems = pltpu.SemaphoreType.DMA((3 * n,))
    res = pl.pallas_call(
        body, name=f"scatter_start_{layer}",
        in_specs=[HBM_SPEC] * (2 * n),
        out_specs=[SEM_SPEC, SEM_SPEC] + [HBM_SPEC] * (2 * n) + [pl.BlockSpec(memory_space=pltpu.VMEM)],
        out_shape=[sems, sems] + [pltpu.HBM(a.shape, a.dtype) for a in srcs + lands]
        + [jax.ShapeDtypeStruct((8, LANES), F32)],
        input_output_aliases={k: 2 + k for k in range(2 * n)}, compiler_params=IN_FLIGHT,
    )(*[_in_hbm(a) for a in srcs + lands])
    return res[0], res[1], res[2:2 + n], res[2 + n:2 + 2 * n], res[-1]


def scatter_wait(send, recv, srcs, lands, after, layer):
    n = len(srcs)

    def body(*refs):
        ins, land_refs = refs[:n], refs[n:2 * n]
        send_ref, recv_ref = refs[2 * n], refs[2 * n + 1]
        for cp in _scatter_copies(ins, land_refs, send_ref, recv_ref):
            cp.wait_send()
            cp.wait_recv()

    res = pl.pallas_call(
        body, name=f"scatter_wait_{layer}",
        in_specs=[HBM_SPEC] * (2 * n) + [SEM_SPEC, SEM_SPEC, _any()], out_specs=[HBM_SPEC] * (2 * n),
        out_shape=[pltpu.HBM(a.shape, a.dtype) for a in list(srcs) + list(lands)],
        input_output_aliases={k: k for k in range(2 * n)}, compiler_params=IN_FLIGHT,
    )(*srcs, *lands, send, recv, after)
    return res[n:]


def add_chips(gs, r1s, r2s, place, totals, layer):
    n = len(gs)
    steps = 2

    def body(p_ref, *refs):
        del p_ref
        for g_ref, r1_ref, r2_ref, o_ref in zip(refs[:n], refs[n:2 * n], refs[2 * n:3 * n], refs[4 * n:]):
            own = g_ref[...] + r1_ref[...]
            o_ref[...] = ((own + r2_ref[0].astype(F32)) + r2_ref[1].astype(F32)) + r2_ref[2].astype(F32)

    blk = lambda r: (None, r.shape[1] // steps, r.shape[2])
    grid_spec = pltpu.PrefetchScalarGridSpec(
        num_scalar_prefetch=1, grid=(steps,),
        in_specs=[pl.BlockSpec(blk(r), lambda i, p: (p[1], p[0] * steps + i, 0)) for r in r1s]
        + [pl.BlockSpec(blk(r), lambda i, p: (p[1], i, 0)) for r in r1s]
        + [pl.BlockSpec((3,) + blk(r)[1:], lambda i, p: (0, i, 0)) for r in r1s] + [_any()] * n,
        out_specs=[pl.BlockSpec(blk(r), lambda i, p: (layer, p[0] * steps + i, 0)) for r in r1s])
    return pl.pallas_call(body, grid_spec=grid_spec, out_shape=[jax.ShapeDtypeStruct(t.shape, F32) for t in totals],
                          input_output_aliases={1 + 3 * n + k: k for k in range(n)},
                          compiler_params=_cp(("arbitrary",)), name="add_chips")(place, *gs, *r1s, *r2s, *totals)


def _share_copies(bufs, send, recv):
    x, y, c, _ = _place()
    out = []
    for k, buf in enumerate(bufs):
        sems = dict(send_sem=send.at[k], recv_sem=recv.at[k], device_id=(x, y, 1 - c), device_id_type=MESH)
        mine = buf.at[:, _half(buf.shape[1], c), :]
        theirs = buf.at[:, _half(buf.shape[1], 1 - c), :]
        out.append((pltpu.make_async_remote_copy(src_ref=mine, dst_ref=mine, **sems),
                    pltpu.make_async_remote_copy(src_ref=theirs, dst_ref=theirs, **sems)))
    return out


def share_start(bufs, tag):
    n = len(bufs)

    def body(*refs):
        ins = refs[:n]
        send, recv = refs[n], refs[n + 1]
        token = refs[-1]
        for start, _ in _share_copies(ins, send, recv):
            start.start()
        token[...] = jnp.zeros_like(token)

    sems = pltpu.SemaphoreType.DMA((n,))
    res = pl.pallas_call(
        body, name=f"share_start_{tag}", in_specs=[HBM_SPEC] * n,
        out_specs=[SEM_SPEC, SEM_SPEC] + [HBM_SPEC] * n + [pl.BlockSpec(memory_space=pltpu.VMEM)],
        out_shape=[sems, sems] + [pltpu.HBM(b.shape, b.dtype) for b in bufs] + [jax.ShapeDtypeStruct((8, LANES), F32)],
        input_output_aliases={k: 2 + k for k in range(n)}, compiler_params=IN_FLIGHT,
    )(*[_in_hbm(b) for b in bufs])
    return res[0], res[1], res[2:2 + n], res[-1]


def share_wait(send, recv, bufs, after, tag):
    n = len(bufs)

    def body(*refs):
        ins = refs[:n]
        send_ref, recv_ref = refs[n], refs[n + 1]
        for start, arrival in _share_copies(ins, send_ref, recv_ref):
            start.wait_send()
            arrival.wait_recv()

    return pl.pallas_call(
        body, name=f"share_wait_{tag}",
        in_specs=[HBM_SPEC] * n + [SEM_SPEC, SEM_SPEC, _any()], out_specs=[HBM_SPEC] * n,
        out_shape=[pltpu.HBM(b.shape, b.dtype) for b in bufs],
        input_output_aliases={k: k for k in range(n)}, compiler_params=IN_FLIGHT,
    )(*bufs, send, recv, after)


def small_allreduce(v, after=()):
    rows = v.shape[0]
    flips = [(fx, fy, fc) for fx in (0, 1) for fy in (0, 1) for fc in (0, 1)][1:]

    def body(v_ref, o_ref, buf, send, recv):
        x, y, c, _ = _place()
        buf[4 * x + 2 * y + c] = v_ref[...]
        peers = [(jnp.where(fx, 1 - x, x), jnp.where(fy, 1 - y, y), jnp.where(fc, 1 - c, c)) for fx, fy, fc in flips]
        cps = []
        for k, peer in enumerate(peers):
            cp = pltpu.make_async_remote_copy(
                src_ref=v_ref, dst_ref=buf.at[4 * x + 2 * y + c], send_sem=send.at[k], recv_sem=recv.at[k],
                device_id=peer, device_id_type=MESH)
            cp.start()
            cps.append(cp)
        for k, (px, py, pc) in enumerate(peers):
            pltpu.make_async_remote_copy(
                src_ref=v_ref, dst_ref=buf.at[4 * px + 2 * py + pc], send_sem=send.at[k], recv_sem=recv.at[k],
                device_id=(px, py, pc), device_id_type=MESH).wait_recv()
        for cp in cps:
            cp.wait_send()
        acc = buf[0]
        for s in range(1, 8):
            acc = acc + buf[s]
        o_ref[...] = acc

    vm = pl.BlockSpec(memory_space=pltpu.VMEM)
    return pl.pallas_call(
        _behind(body, 1, after), in_specs=[vm] + [_any()] * len(after), out_specs=vm,
        out_shape=jax.ShapeDtypeStruct((rows, SMALL_COLS), F32),
        scratch_shapes=[pltpu.VMEM((8, rows, SMALL_COLS), F32), pltpu.SemaphoreType.DMA((7,)),
                        pltpu.SemaphoreType.DMA((7,))],
        name="reduce_small")(v, *after)


def adamw(w, g, m, v, rb, name, after=()):
    nl, rows, cols = w.shape

    def body(w_ref, g_ref, m_ref, v_ref, go_ref, d_ref, nm_ref, nv_ref):
        gv = g_ref[...]
        go_ref[...] = gv
        nm = ADAM_B1 * m_ref[...] + (1.0 - ADAM_B1) * gv
        nv = ADAM_B2 * v_ref[...] + (1.0 - ADAM_B2) * (gv * gv)
        m_hat = nm / (1.0 - ADAM_B1 ** ADAM_STEP)
        v_hat = nv / (1.0 - ADAM_B2 ** ADAM_STEP)
        d_ref[...] = -ADAM_LR * (m_hat / (jnp.sqrt(v_hat) + ADAM_EPS) + ADAM_WD * w_ref[...])
        nm_ref[...] = nm
        nv_ref[...] = nv

    blk = pl.BlockSpec((None, rb, cols), lambda l, r: (l, r, 0))
    shp = jax.ShapeDtypeStruct(w.shape, F32)
    return pl.pallas_call(_behind(body, 4, after), grid=(nl, rows // rb), in_specs=[blk] * 4 + [_any()] * len(after),
                          out_specs=[blk] * 4, out_shape=[shp] * 4,
                          compiler_params=_cp(("arbitrary", "arbitrary")), name=name)(w, g, m, v, *after)


def _pack(parts, rows):
    flat = jnp.concatenate([p.reshape(-1).astype(F32) for p in parts])
    return jnp.pad(flat, (0, rows * SMALL_COLS - flat.shape[0])).reshape(rows, SMALL_COLS)


def _unpack(vec, shapes):
    flat = vec.reshape(-1)
    out, off = [], 0
    for s in shapes:
        size = 1
        for d in s:
            size *= d
        out.append(flat[off:off + size].reshape(s))
        off += size
    return out


def kernel(x, w_in, w_conv, rel_bias, g_conv_out, g_attn_out, w_out, g_pre_mix, g_post_mix, g_pre_ffn, g_post_ffn, w_ffn_in, w_ffn_out, loss_target, m_w_in, m_w_conv, m_rel_bias, m_g_conv_out, m_g_attn_out, m_w_out, m_g_pre_mix, m_g_post_mix, m_g_pre_ffn, m_g_post_ffn, m_w_ffn_in, m_w_ffn_out, v_w_in, v_w_conv, v_rel_bias, v_g_conv_out, v_g_attn_out, v_w_out, v_g_pre_mix, v_g_post_mix, v_g_pre_ffn, v_g_post_ffn, v_w_ffn_in, v_w_ffn_out):
    xi, yi, ci = lax.axis_index("x"), lax.axis_index("y"), lax.axis_index("c")
    chip = 2 * xi + yi
    nl = w_in.shape[0]
    x0 = x[0]
    target = loss_target[0]
    cwl = CW // NCHIP

    chip1 = chip.reshape(1).astype(jnp.int32)
    own = [cast_to_slot([w_in, w_out, w_ffn_in, w_ffn_out], chip1, l) for l in range(nl)]
    wc_mine = jnp.pad(w_conv.reshape(-1), (0, 16 * LANES - w_conv.size)).reshape(1, 16, LANES)
    wc_slot = lax.dynamic_update_slice_in_dim(jnp.zeros((NCHIP, 16, LANES), F32), wc_mine, chip, axis=0)
    gm = jnp.kron(jnp.eye(CW // HD, dtype=F32), jnp.full((HD, HD), 1.0 / HD, F32)).astype(BF16)
    row = lambda a, l: a[l][None, :]

    def gather_finish(flight, after, tag):
        send, recv, bufs, _ = flight
        return gather_forward(gather_wait(send, recv, bufs, after, tag))

    first_mix = gather_start(list(own[0][:2]) + [wc_slot], x0, "0m")
    first_ffn = gather_start(own[0][2:], first_mix[3], "0f")
    gw_in, gw_out, wc_all = gather_finish(first_mix, x0, "0m")
    wc_full = wc_all.reshape(NCHIP, -1)[:, :nl * cwl * 3].reshape(NCHIP, nl, cwl, 3)
    wc_full = jnp.transpose(wc_full, (1, 0, 2, 3)).reshape(nl, CW, 3)
    wconv_t = jnp.pad(jnp.transpose(wc_full, (0, 2, 1)), ((0, 0), (0, 5), (0, 0)))
    flights, to_sibling = {}, None
    saved, weights = [], []
    h = x0
    for l in range(nl):
        if l == 0:
            pass
        elif l == 1:
            gw_in, gw_out, gw_fi, gw_fo = gather_finish(flights[l], h, l)
        else:
            gw_in, gw_out, gw_fi, gw_fo = forward_wait(*to_sibling[:3], h, l)
        gw_out = gw_out.reshape(D, D)
        behind_mix, behind_ffn = ([first_ffn[3]] if l == 0 else []), []
        if l + 1 < nl and l + 1 not in flights:
            flights[l + 1] = gather_start(own[l + 1], first_ffn[3] if l == 0 else gw_in, l + 1)
            behind_mix.append(flights[l + 1][3])
        bias2, bias2_bwd = bias_expand(_diag_vector(rel_bias[l]), (QG_FWD, QG_BWD))
        proj = fwd_inproj(h, row(g_pre_mix, l), gw_in, behind_mix)
        xmid, o, lse, y, z = fwd_mix(h, proj, bias2, wconv_t[l], row(g_conv_out, l), row(g_attn_out, l),
                                     row(g_post_mix, l), gm, gw_out)
        if l == 0:
            gw_fi, gw_fo = gather_finish(first_ffn, xmid, "0f")
        elif l + 1 < nl:
            send, recv, bufs, _ = flights[l + 1]
            landed = gather_wait(send, recv, bufs, xmid, l + 1)
            to_sibling = forward_start(landed, l + 1)
            behind_ffn.append(to_sibling[3])
            if l + 2 < nl:
                flights[l + 2] = gather_start(own[l + 2], to_sibling[3], l + 2)
                behind_ffn.append(flights[l + 2][3])
        gw_fo = gw_fo.reshape(2, DFF // 2, D)
        gu, f, xout = fwd_ffn(xmid, row(g_pre_ffn, l), row(g_post_ffn, l), gw_fi, gw_fo, behind_ffn)
        saved.append((h, proj, bias2_bwd, xmid, o, lse, y, z, gu, f))
        weights.append((gw_in, gw_out, gw_fi, gw_fo))
        h = xout
    dx, loss_blk = loss_head(h, target)

    core = ci.reshape(1).astype(jnp.int32)
    place = jnp.stack([ci, chip]).astype(jnp.int32)
    totals = [lax.empty(w.shape, F32) for w in (w_in, w_out, w_ffn_in, w_ffn_out)]
    small = {k: [None] * nl for k in ("co", "ao", "pm", "qm", "pf", "qf", "rel", "wc")}

    def reduce_begin(kinds, grads, tag):
        return kinds, exchange_start(grads, tag), tag

    def reduce_mid(state, after):
        kinds, (send, recv, srcs, lands, _), tag = state
        grads, from_sibling = exchange_wait(send, recv, srcs, lands, after, tag)
        return kinds, grads, from_sibling, scatter_start(add_pair(grads, from_sibling, core), tag), tag

    def reduce_end(state, after, totals, layer):
        kinds, grads, from_sibling, (send, recv, srcs, lands, _), tag = state
        from_chips = scatter_wait(send, recv, srcs, lands, after, tag)
        totals = list(totals)
        summed = add_chips(grads, from_sibling, from_chips, place, [totals[i] for i in kinds], layer)
        for i, t in zip(kinds, summed):
            totals[i] = t
        return totals

    begun = flying = None
    for l in reversed(range(nl)):
        hin, proj, bias2, xmid, o, lse, y, z, gu, f = saved[l]
        gw_in, gw_out, gw_fi, gw_fo = weights[l]
        behind_ffn = [begun[1][4]] if begun is not None else []
        dxm, dfb, act, dgu, h2, dg_qf, dg_pf = bwd_ffn(dx, f, xmid, gu, row(g_pre_ffn, l), row(g_post_ffn, l),
                                                        gw_fi, gw_fo, behind_ffn)
        behind_mix, behind_conv = [], []
        if begun is not None:
            flying = reduce_mid(begun, dxm)
            behind_mix.append(flying[3][4])
        gr_fo = wgrad(act, dfb, 256, D, False, "wgrad_ffn_out").reshape(NCHIP, DFF // NCHIP, D)
        gr_fi = wgrad(h2, dgu, 512, 2 * DFF // NCHIP, True, "wgrad_ffn_in")
        if l == 0:
            begun_ffn = reduce_begin([2, 3], [gr_fi, gr_fo], "0f")
            behind_mix.append(begun_ffn[1][4])
        dzb, do, dco, dbg, dg_qm, dg_co, dg_ao = bwd_mix(dxm, z, o, proj, wconv_t[l], row(g_conv_out, l),
                                                          row(g_attn_out, l), row(g_post_mix, l), gm, gw_out,
                                                          behind_mix)
        if l == 0:
            flying_ffn = reduce_mid(begun_ffn, dzb)
            behind_conv.append(flying_ffn[3][4])
        gr_out = wgrad(y, dzb, 512, D, False, "wgrad_out").reshape(NCHIP, D // NCHIP, D)
        dhc, dcg, dwc = bwd_conv(dco, proj, wconv_t[l], behind_conv)
        dq, dk, dv, db2 = bwd_attn(proj, o, do, lse, bias2)
        dx, dproj, hb, dg_pm = bwd_inproj(dxm, hin, dhc, dbg, dcg, dq, dk, dv, row(g_pre_mix, l), gw_in)
        if flying is not None:
            totals = reduce_end(flying, dx, totals, l + 1)
        gr_in = wgrad(hb, dproj, 512, PROJ // NCHIP, True, "wgrad_in")
        small["co"][l], small["ao"][l], small["pm"][l], small["qm"][l] = dg_co, dg_ao, dg_pm, dg_qm
        small["pf"][l], small["qf"][l] = dg_pf, dg_qf
        small["rel"][l] = _diag_vector_bwd(bias_reduce(db2.reshape(NH, QG_BWD, QG_BWD + LEFT)))
        small["wc"][l] = jnp.transpose(dwc[0:3], (1, 0))
        if l > 0:
            begun = reduce_begin([0, 1, 2, 3], [gr_in, gr_out, gr_fi, gr_fo], l)
    flying_mix = reduce_mid(reduce_begin([0, 1], [gr_in, gr_out], "0m"), dx)
    totals = reduce_end(flying_ffn, flying_mix[3][4], totals, 0)
    share_ffn = share_start(totals[2:], "ffn")

    order = ("co", "ao", "pm", "qm", "pf", "qf", "rel", "wc")
    parts = [jnp.stack(small[k]) for k in order] + [loss_blk[0:1, 0:1]]
    shapes = [p.shape for p in parts]
    red_vec = small_allreduce(_pack(parts, 40), [share_ffn[3]])
    red = _unpack(red_vec, shapes)

    gr_fi, gr_fo = share_wait(*share_ffn[:3], red_vec, "ffn")
    big_fi = adamw(w_ffn_in, gr_fi, m_w_ffn_in, v_w_ffn_in, w_ffn_in.shape[1] // 4, "adamw_ffn_in")
    totals = reduce_end(flying_mix, big_fi[1], totals, 0)
    share_mix = share_start(totals[:2], "mix")
    big_fo = adamw(w_ffn_out, gr_fo, m_w_ffn_out, v_w_ffn_out, w_ffn_out.shape[1] // 4, "adamw_ffn_out",
                   [share_mix[3]])
    gr_in, gr_out = share_wait(*share_mix[:3], big_fo[1], "mix")
    big_in = adamw(w_in, gr_in, m_w_in, v_w_in, w_in.shape[1] // 4, "adamw_in")
    big_out = adamw(w_out, gr_out, m_w_out, v_w_out, w_out.shape[1] // 4, "adamw_out")
    big = [big_in, big_out, big_fi, big_fo]
    gr_co, gr_ao, gr_pm, gr_qm, gr_pf, gr_qf, gr_rel, gr_wc_full, loss = red
    gr_co, gr_ao, gr_pm, gr_qm, gr_pf, gr_qf = [a.reshape(nl, -1) for a in (gr_co, gr_ao, gr_pm, gr_qm, gr_pf, gr_qf)]
    gr_wc = lax.dynamic_slice_in_dim(gr_wc_full, chip * cwl, cwl, axis=1)
    loss = loss.reshape(())

    sw = [g_conv_out, g_attn_out, g_pre_mix, g_post_mix, g_pre_ffn, g_post_ffn, rel_bias, w_conv]
    sg = [gr_co, gr_ao, gr_pm, gr_qm, gr_pf, gr_qf, gr_rel, gr_wc]
    sm = [m_g_conv_out, m_g_attn_out, m_g_pre_mix, m_g_post_mix, m_g_pre_ffn, m_g_post_ffn, m_rel_bias, m_w_conv]
    sv = [v_g_conv_out, v_g_attn_out, v_g_pre_mix, v_g_post_mix, v_g_pre_ffn, v_g_post_ffn, v_rel_bias, v_w_conv]
    sshapes = [a.shape for a in sw]
    packed = [_pack(a, 32)[None] for a in (sw, sg, sm, sv)]
    s_out = [_unpack(a[0], sshapes) for a in adamw(*packed, 32, "adamw_small")]

    def leaves(big_i, small_i):
        b_in, b_out, b_fi, b_fo = big_i
        s_co, s_ao, s_pm, s_qm, s_pf, s_qf, s_rel, s_wc = small_i
        return [b_in, s_wc, s_rel, s_co, s_ao, b_out, s_pm, s_qm, s_pf, s_qf, b_fi, b_fo]

    out = [loss, dx[None]]
    out += leaves([b[0] for b in big], sg)
    for i in range(1, 4):
        out += leaves([b[i] for b in big], s_out[i])
    return tuple(out)
```

```python
import jax
import jax.numpy as jnp
from jax import lax
from jax.experimental import pallas as pl
from jax.experimental.pallas import tpu as pltpu

F32 = jnp.float32
BF16 = jnp.bfloat16

D = 1024
PROJ = 3072
CW = 512
HD = 64
NH = 8
CHUNK = 64
BAND = 576
REL_CLIP = 128
NREL = 2 * REL_CLIP + 1
DFF = 2816
DEPTH = 4
NCHIP = 4
EPS = 1e-6
NEG_INF = -1e30

ADAM_LR = 0.001
ADAM_B1 = 0.9
ADAM_B2 = 0.999
ADAM_EPS = 1e-08
ADAM_WD = 0.01
ADAM_STEP = 10

V7X_VMEM_BYTES = 64 * 1024 * 1024
VMEM_LIMIT = V7X_VMEM_BYTES - 8 * 1024 * 1024
LANES = 128
QG_FWD = 4 * CHUNK
QG_BWD = 2 * CHUNK
LEFT = BAND - CHUNK
TQ = 512
TM = 256
SMALL_COLS = 1024
MESH = pl.DeviceIdType.MESH
NT = (((1,), (1,)), ((), ()))
TN = (((0,), (0,)), ((), ()))


def _cp(sem=None, vmem=VMEM_LIMIT):
    return pltpu.CompilerParams(dimension_semantics=sem, vmem_limit_bytes=vmem)


def _any():
    return pl.BlockSpec(memory_space=pl.ANY)


def _const(shape):
    nd = len(shape)
    return pl.BlockSpec(shape, lambda *_: (0,) * nd)


def _behind(body, n_in, after):
    def ordered(*refs):
        return body(*refs[:n_in], *refs[n_in + len(after):])
    return ordered


def _rms(v, g):
    r = lax.rsqrt(jnp.mean(v * v, axis=-1, keepdims=True) + EPS)
    return v * r * g


def _rms_bwd(dy, v, g):
    r = lax.rsqrt(jnp.mean(v * v, axis=-1, keepdims=True) + EPS)
    vh = v * r
    dg = jnp.sum(dy * vh, axis=0, keepdims=True)
    dvh = dy * g
    dv = r * (dvh - vh * jnp.mean(dvh * vh, axis=-1, keepdims=True))
    return dv, dg


def _group_mean(v, gm):
    return jnp.dot(v.astype(BF16), gm, preferred_element_type=F32)


def _group_rms_bwd(dy, v, g, gm):
    r = lax.rsqrt(_group_mean(v * v, gm) + EPS)
    vh = v * r
    dg = jnp.sum(dy * vh, axis=0, keepdims=True)
    dvh = dy * g
    dv = r * (dvh - vh * _group_mean(dvh * vh, gm))
    return dv, dg


def _head_masks(scale):
    lane = lax.broadcasted_iota(jnp.int32, (1, LANES), 1)
    return [jnp.where((lane >= HD * a) & (lane < HD * (a + 1)), scale, 0.0).astype(BF16) for a in range(2)]


class _Resident:
    def __init__(self, src, dst, sem):
        self.first = pl.program_id(0) == 0
        self.copy = pltpu.make_async_copy(src, dst, sem)
        self.dst = dst

        @pl.when(self.first)
        def _():
            self.copy.start()

    def read(self):
        @pl.when(self.first)
        def _():
            self.copy.wait()

        return self.dst[...]


FF_CHUNKS = ((0, 1536), (1536, DFF))


def _stream_ffn_weights(wfi_hbm, wfo_hbm, wfi_v, wfo_v, sems, order, step):
    hw = DFF // 2
    per_matrix = {
        0: [(wfi_hbm.at[j], wfi_v.at[0, :, pl.ds(hw * j, hw)]) for j in range(2)],
        1: [(wfi_hbm.at[2 + j], wfi_v.at[1, :, pl.ds(hw * j, hw)]) for j in range(2)],
        2: [(wfo_hbm.at[j], wfo_v.at[pl.ds(hw * j, hw), :]) for j in range(2)],
    }
    pieces = [p for m in order for p in per_matrix[m]]
    slot = {m: 2 * k for k, m in enumerate(order)}

    def make_step(wait):
        def ready(m, chunk):
            if chunk == 0:
                wait(slot[m])
                wait(slot[m] + 1)
        return lambda: step(ready)

    copies = [pltpu.make_async_copy(src, dst, sems.at[k]) for k, (src, dst) in enumerate(pieces)]
    first = pl.program_id(0) == 0

    @pl.when(first)
    def _():
        for cp in copies:
            cp.start()
        make_step(lambda k: copies[k].wait())()

    @pl.when(jnp.logical_not(first))
    def _():
        make_step(lambda k: None)()


def _conv_taps(u_prev, u, scr):
    n = u.shape[0]
    scr[0:16, :] = u_prev
    scr[16:16 + n, :] = u
    return scr[15:15 + n, :], scr[14:14 + n, :]


def fwd_inproj(x, g, w_all, after=()):
    t = x.shape[0]
    wc = PROJ // NCHIP

    def body(x_ref, g_ref, w_hbm, o_ref, w_v):
        @pl.when(pl.program_id(0) == 0)
        def _():
            pltpu.sync_copy(w_hbm, w_v)

        h = _rms(x_ref[...], g_ref[...]).astype(BF16)
        for b in range(NCHIP):
            o_ref[:, wc * b:wc * (b + 1)] = jnp.dot(h, w_v[b], preferred_element_type=F32).astype(BF16)

    return pl.pallas_call(
        _behind(body, 3, after), grid=(t // TQ,),
        in_specs=[pl.BlockSpec((TQ, D), lambda i: (i, 0)), _const((1, D)), _any()] + [_any()] * len(after),
        out_specs=pl.BlockSpec((TQ, PROJ), lambda i: (i, 0)),
        out_shape=jax.ShapeDtypeStruct((t, PROJ), BF16),
        scratch_shapes=[pltpu.VMEM((NCHIP, D, wc), BF16)],
        compiler_params=_cp(("arbitrary",)), name="fwd_inproj")(x, g, w_all, *after)


def _attn_window_specs():
    return [
        pl.BlockSpec((TQ, CW), lambda i: (i, 3)),
        pl.BlockSpec((TQ, CW), lambda i: (jnp.maximum(i - 1, 0), 4)),
        pl.BlockSpec((TQ, CW), lambda i: (i, 4)),
        pl.BlockSpec((TQ, CW), lambda i: (jnp.maximum(i - 1, 0), 5)),
        pl.BlockSpec((TQ, CW), lambda i: (i, 5)),
    ]


def _conv_specs():
    return [
        pl.BlockSpec((TQ, 3 * CW), lambda i: (i, 0)),
        pl.BlockSpec((16, 3 * CW), lambda i: (jnp.maximum(i * (TQ // 16) - 1, 0), 0)),
    ]


def _conv_fwd(pc_ref, pcp_ref, wc_ref, scr, first):
    pc = pc_ref[...].astype(F32)
    hc, bg, cg = pc[:, :CW], pc[:, CW:2 * CW], pc[:, 2 * CW:]
    u = cg * hc
    pp = pcp_ref[...].astype(F32)
    u_prev = jnp.where(first, 0.0, pp[:, 2 * CW:] * pp[:, :CW])
    u1, u2 = _conv_taps(u_prev, u, scr)
    cout = wc_ref[0:1, :] * u2 + wc_ref[1:2, :] * u1 + wc_ref[2:3, :] * u
    return hc, bg, cg, u, u1, u2, cout


def _key_penalty(first, r0, kg):
    col = lax.broadcasted_iota(jnp.int32, (1, kg), 1)
    limit = jnp.where(first, TQ - r0, 0)
    return jnp.where(col < limit, NEG_INF, 0.0)


def fwd_mix(x, proj, bias2, wconv_t, g_co, g_ao, g_pm, gm, wout_all):
    t = x.shape[0]
    qg, kg = QG_FWD, QG_FWD + LEFT

    def body(x_ref, pc_ref, pcp_ref, q_ref, kp_ref, kc_ref, vp_ref, vc_ref, b2_ref, wc_ref, gco_ref, gao_ref, gpm_ref,
             gm_ref, wout_hbm, xmid_ref, o_ref, lse_ref, y_ref, z_ref, wout_v, kwin, vwin, cscr, sems):
        i = pl.program_id(0)
        first = i == 0
        wout = _Resident(wout_hbm, wout_v, sems.at[0])
        kwin[0:TQ, :] = kp_ref[...]
        kwin[TQ:2 * TQ, :] = kc_ref[...]
        vwin[0:TQ, :] = vp_ref[...]
        vwin[TQ:2 * TQ, :] = vc_ref[...]
        qmask = _head_masks(HD ** -0.5)
        low = lax.broadcasted_iota(jnp.int32, (1, LANES), 1) < HD

        def group(g, carry):
            r0 = pl.multiple_of(g * qg, qg)
            pen = _key_penalty(first, r0, kg)
            for hp in range(NH // 2):
                ls = slice(LANES * hp, LANES * (hp + 1))
                qb = q_ref[pl.ds(r0, qg), ls]
                q2 = jnp.concatenate([qb * qmask[0], qb * qmask[1]], axis=0)
                s = lax.dot_general(q2, kwin[pl.ds(r0, kg), ls], NT, preferred_element_type=F32)
                s = s + b2_ref[hp] + pen
                m = jnp.max(s, axis=-1, keepdims=True)
                p = jnp.exp(s - m)
                l = jnp.sum(p, axis=-1, keepdims=True)
                o2 = jnp.dot(p.astype(BF16), vwin[pl.ds(r0, kg), ls], preferred_element_type=F32) * (1.0 / l)
                lse2 = m + jnp.log(l)
                o_ref[pl.ds(r0, qg), ls] = jnp.where(low, o2[:qg], o2[qg:])
                lse_ref[pl.ds(r0, qg), ls] = jnp.where(low, lse2[:qg], lse2[qg:])
            return carry

        lax.fori_loop(0, TQ // qg, group, 0)

        _, bg, _, _, _, _, cout = _conv_fwd(pc_ref, pcp_ref, wc_ref, cscr, first)
        yc = bg * cout
        gmv = gm_ref[...]
        ycn = yc * lax.rsqrt(_group_mean(yc * yc, gmv) + EPS) * gco_ref[...]
        oa = o_ref[...]
        oan = oa * lax.rsqrt(_group_mean(oa * oa, gmv) + EPS) * gao_ref[...]
        y_ref[:, 0:CW] = ycn.astype(BF16)
        y_ref[:, CW:2 * CW] = oan.astype(BF16)
        z = jnp.dot(y_ref[...], wout.read(), preferred_element_type=F32)
        z_ref[...] = z
        xmid_ref[...] = x_ref[...] + _rms(z, gpm_ref[...])

    row = lambda w: pl.BlockSpec((TQ, w), lambda i: (i, 0))
    return pl.pallas_call(
        body, grid=(t // TQ,),
        in_specs=[row(D)] + _conv_specs() + _attn_window_specs() + [
            _const((NH // 2, 2 * qg, kg)), _const((8, CW)), _const((1, CW)), _const((1, CW)), _const((1, D)),
            _const((CW, CW)), _any()],
        out_specs=[row(D), row(CW), row(CW), row(D), row(D)],
        out_shape=[jax.ShapeDtypeStruct((t, D), F32), jax.ShapeDtypeStruct((t, CW), F32),
                   jax.ShapeDtypeStruct((t, CW), F32), jax.ShapeDtypeStruct((t, D), BF16),
                   jax.ShapeDtypeStruct((t, D), F32)],
        scratch_shapes=[pltpu.VMEM((D, D), BF16), pltpu.VMEM((2 * TQ, CW), BF16), pltpu.VMEM((2 * TQ, CW), BF16),
                        pltpu.VMEM((TQ + 16, CW), F32), pltpu.SemaphoreType.DMA((1,))],
        compiler_params=_cp(("arbitrary",)), name="fwd_mix",
    )(x, proj, proj, proj, proj, proj, proj, proj, bias2, wconv_t, g_co, g_ao, g_pm, gm, wout_all)


def fwd_ffn(xmid, g_pre, g_post, wfi_all, wfo_all, after=()):
    t = xmid.shape[0]

    def body(x_ref, gpre_ref, gpost_ref, wfi_hbm, wfo_hbm, gu_ref, f_ref, xo_ref, wfi_v, wfo_v, sems):
        def step(ready):
            xv = x_ref[...]
            h = _rms(xv, gpre_ref[...]).astype(BF16)
            f = jnp.zeros((TM, D), F32)
            for ci, (a, b) in enumerate(FF_CHUNKS):
                ready(0, ci)
                gate = jnp.dot(h, wfi_v[0, :, a:b], preferred_element_type=F32)
                ready(1, ci)
                up = jnp.dot(h, wfi_v[1, :, a:b], preferred_element_type=F32)
                gu_ref[:, a:b] = gate.astype(BF16)
                gu_ref[:, DFF + a:DFF + b] = up.astype(BF16)
                act = gate * (1.0 / (1.0 + jnp.exp(-gate))) * up
                ready(2, ci)
                f = f + jnp.dot(act.astype(BF16), wfo_v[a:b, :], preferred_element_type=F32)
            f_ref[...] = f
            xo_ref[...] = xv + _rms(f, gpost_ref[...])

        _stream_ffn_weights(wfi_hbm, wfo_hbm, wfi_v, wfo_v, sems, (0, 1, 2), step)

    row = lambda w: pl.BlockSpec((TM, w), lambda i: (i, 0))
    return pl.pallas_call(
        _behind(body, 5, after), grid=(t // TM,),
        in_specs=[row(D), _const((1, D)), _const((1, D)), _any(), _any()] + [_any()] * len(after),
        out_specs=[row(2 * DFF), row(D), row(D)],
        out_shape=[jax.ShapeDtypeStruct((t, 2 * DFF), BF16), jax.ShapeDtypeStruct((t, D), F32),
                   jax.ShapeDtypeStruct((t, D), F32)],
        scratch_shapes=[pltpu.VMEM((2, D, DFF), BF16), pltpu.VMEM((DFF, D), BF16), pltpu.SemaphoreType.DMA((6,))],
        compiler_params=_cp(("arbitrary",)), name="fwd_ffn")(xmid, g_pre, g_post, wfi_all, wfo_all, *after)


def loss_head(y, target):
    t = y.shape[0]

    def body(y_ref, t_ref, dy_ref, l_ref):
        @pl.when(pl.program_id(0) == 0)
        def _():
            l_ref[...] = jnp.zeros_like(l_ref)

        e = y_ref[...] - t_ref[...]
        dy_ref[...] = e * (1.0 / D)
        rows = jnp.sum(e * e, axis=-1, keepdims=True) * (1.0 / D)
        l_ref[...] += 0.5 * jnp.sum(rows, axis=0, keepdims=True)

    row = pl.BlockSpec((TQ, D), lambda i: (i, 0))
    return pl.pallas_call(
        body, grid=(t // TQ,), in_specs=[row, row], out_specs=[row, _const((8, LANES))],
        out_shape=[jax.ShapeDtypeStruct((t, D), F32), jax.ShapeDtypeStruct((8, LANES), F32)],
        compiler_params=_cp(("arbitrary",)), name="loss_head")(y, target)


def bwd_ffn(dx, f, xmid, gu, g_pre, g_post, wfi_all, wfo_all, after=()):
    t = dx.shape[0]
    hw = DFF // 2

    def body(dx_ref, f_ref, x_ref, gu_ref, gpre_ref, gpost_ref, wfi_hbm, wfo_hbm,
             dxm_ref, df_ref, act_ref, dgu_ref, h_ref, dgpost_ref, dgpre_ref, wfi_v, wfo_v, sems):
        @pl.when(pl.program_id(0) == 0)
        def _():
            dgpost_ref[...] = jnp.zeros_like(dgpost_ref)
            dgpre_ref[...] = jnp.zeros_like(dgpre_ref)

        def step(ready):
            dxo = dx_ref[...]
            df, dgp = _rms_bwd(dxo, f_ref[...], gpost_ref[...])
            dgpost_ref[...] += dgp
            dfb = df.astype(BF16)
            df_ref[...] = dfb
            dh = jnp.zeros((TM, D), F32)
            for ci, (a, b) in enumerate(FF_CHUNKS):
                ready(2, ci)
                dact = lax.dot_general(dfb, wfo_v[a:b, :], NT, preferred_element_type=F32)
                gate = gu_ref[:, a:b].astype(F32)
                up = gu_ref[:, DFF + a:DFF + b].astype(F32)
                sig = 1.0 / (1.0 + jnp.exp(-gate))
                silu = gate * sig
                act_ref[:, a:b] = (silu * up).astype(BF16)
                dup = (dact * silu).astype(BF16)
                dgate = (dact * up * (sig * (1.0 + gate * (1.0 - sig)))).astype(BF16)
                dgu_ref[:, a:b] = dgate
                dgu_ref[:, DFF + a:DFF + b] = dup
                ready(0, ci)
                dh = dh + lax.dot_general(dgate, wfi_v[0, :, a:b], NT, preferred_element_type=F32)
                ready(1, ci)
                dh = dh + lax.dot_general(dup, wfi_v[1, :, a:b], NT, preferred_element_type=F32)
            xv = x_ref[...]
            gpre = gpre_ref[...]
            h_ref[...] = _rms(xv, gpre).astype(BF16)
            dxv, dgq = _rms_bwd(dh, xv, gpre)
            dgpre_ref[...] += dgq
            dxm_ref[...] = dxo + dxv

        _stream_ffn_weights(wfi_hbm, wfo_hbm, wfi_v, wfo_v, sems, (2, 0, 1), step)

    row = lambda w: pl.BlockSpec((TM, w), lambda i: (i, 0))
    return pl.pallas_call(
        _behind(body, 8, after), grid=(t // TM,),
        in_specs=[row(D), row(D), row(D), row(2 * DFF), _const((1, D)), _const((1, D)), _any(), _any()]
        + [_any()] * len(after),
        out_specs=[row(D), row(D), row(DFF), row(2 * DFF), row(D), _const((1, D)), _const((1, D))],
        out_shape=[jax.ShapeDtypeStruct((t, D), F32), jax.ShapeDtypeStruct((t, D), BF16),
                   jax.ShapeDtypeStruct((t, DFF), BF16), jax.ShapeDtypeStruct((t, 2 * DFF), BF16),
                   jax.ShapeDtypeStruct((t, D), BF16), jax.ShapeDtypeStruct((1, D), F32),
                   jax.ShapeDtypeStruct((1, D), F32)],
        scratch_shapes=[pltpu.VMEM((2, D, DFF), BF16), pltpu.VMEM((DFF, D), BF16), pltpu.SemaphoreType.DMA((6,))],
        compiler_params=_cp(("arbitrary",)), name="bwd_ffn")(dx, f, xmid, gu, g_pre, g_post, wfi_all, wfo_all, *after)


def bwd_mix(dxm, z, o, proj, wconv_t, g_co, g_ao, g_pm, gm, wout_all, after=()):
    t = dxm.shape[0]

    def body(dx_ref, z_ref, o_ref, pc_ref, pcp_ref, wc_ref, gco_ref, gao_ref, gpm_ref, gm_ref, wout_hbm,
             dz_ref, do_ref, dco_ref, dbg_ref, dgpm_ref, dgco_ref, dgao_ref, wout_v, cscr):
        first = pl.program_id(0) == 0

        @pl.when(first)
        def _():
            pltpu.sync_copy(wout_hbm, wout_v)
            dgpm_ref[...] = jnp.zeros_like(dgpm_ref)
            dgco_ref[...] = jnp.zeros_like(dgco_ref)
            dgao_ref[...] = jnp.zeros_like(dgao_ref)

        dz, dgp = _rms_bwd(dx_ref[...], z_ref[...], gpm_ref[...])
        dgpm_ref[...] += dgp
        dzb = dz.astype(BF16)
        dz_ref[...] = dzb
        gmv = gm_ref[...]
        _, bg, _, _, _, _, cout = _conv_fwd(pc_ref, pcp_ref, wc_ref, cscr, first)
        dy_conv = lax.dot_general(dzb, wout_v[0:CW, :], NT, preferred_element_type=F32)
        dyc, dgc = _group_rms_bwd(dy_conv, bg * cout, gco_ref[...], gmv)
        dgco_ref[...] += dgc
        dbg_ref[...] = (dyc * cout).astype(BF16)
        dco_ref[...] = dyc * bg
        dy_attn = lax.dot_general(dzb, wout_v[CW:2 * CW, :], NT, preferred_element_type=F32)
        do, dga = _group_rms_bwd(dy_attn, o_ref[...], gao_ref[...], gmv)
        dgao_ref[...] += dga
        do_ref[...] = do.astype(BF16)

    row = lambda w: pl.BlockSpec((TQ, w), lambda i: (i, 0))
    return pl.pallas_call(
        _behind(body, 11, after), grid=(t // TQ,),
        in_specs=[row(D), row(D), row(CW)] + _conv_specs() + [
            _const((8, CW)), _const((1, CW)), _const((1, CW)), _const((1, D)), _const((CW, CW)), _any()]
        + [_any()] * len(after),
        out_specs=[row(D), row(CW), row(CW), row(CW), _const((1, D)), _const((1, CW)), _const((1, CW))],
        out_shape=[jax.ShapeDtypeStruct((t, D), BF16), jax.ShapeDtypeStruct((t, CW), BF16),
                   jax.ShapeDtypeStruct((t, CW), F32), jax.ShapeDtypeStruct((t, CW), BF16),
                   jax.ShapeDtypeStruct((1, D), F32), jax.ShapeDtypeStruct((1, CW), F32),
                   jax.ShapeDtypeStruct((1, CW), F32)],
        scratch_shapes=[pltpu.VMEM((D, D), BF16), pltpu.VMEM((TQ + 16, CW), F32)],
        compiler_params=_cp(("arbitrary",)), name="bwd_mix",
    )(dxm, z, o, proj, proj, wconv_t, g_co, g_ao, g_pm, gm, wout_all, *after)


def bwd_conv(dco, proj, wconv_t, after=()):
    t = dco.shape[0]
    nt = t // TQ

    def body(d_ref, dn_ref, pc_ref, pcp_ref, wc_ref, dhc_ref, dcg_ref, dw_ref, cscr, dscr):
        i = pl.program_id(0)
        first = i == 0

        @pl.when(first)
        def _():
            dw_ref[...] = jnp.zeros_like(dw_ref)

        hc, _, cg, u, u1, u2, _ = _conv_fwd(pc_ref, pcp_ref, wc_ref, cscr, first)
        d0 = d_ref[...]
        dscr[0:TQ, :] = d0
        dscr[TQ:TQ + 8, :] = jnp.where(i == nt - 1, 0.0, dn_ref[...])
        d1 = dscr[1:TQ + 1, :]
        d2 = dscr[2:TQ + 2, :]
        du = wc_ref[2:3, :] * d0 + wc_ref[1:2, :] * d1 + wc_ref[0:1, :] * d2
        dhc_ref[...] = (du * cg).astype(BF16)
        dcg_ref[...] = (du * hc).astype(BF16)
        dw_ref[0:1, :] += jnp.sum(d0 * u2, axis=0, keepdims=True)
        dw_ref[1:2, :] += jnp.sum(d0 * u1, axis=0, keepdims=True)
        dw_ref[2:3, :] += jnp.sum(d0 * u, axis=0, keepdims=True)

    row = lambda w: pl.BlockSpec((TQ, w), lambda i: (i, 0))
    nxt = pl.BlockSpec((8, CW), lambda i: (jnp.minimum((i + 1) * (TQ // 8), t // 8 - 1), 0))
    return pl.pallas_call(
        _behind(body, 5, after), grid=(nt,),
        in_specs=[row(CW), nxt] + _conv_specs() + [_const((8, CW))] + [_any()] * len(after),
        out_specs=[row(CW), row(CW), _const((8, CW))],
        out_shape=[jax.ShapeDtypeStruct((t, CW), BF16), jax.ShapeDtypeStruct((t, CW), BF16),
                   jax.ShapeDtypeStruct((8, CW), F32)],
        scratch_shapes=[pltpu.VMEM((TQ + 16, CW), F32), pltpu.VMEM((TQ + 8, CW), F32)],
        compiler_params=_cp(("arbitrary",)), name="bwd_conv")(dco, dco, proj, proj, wconv_t, *after)


def bwd_attn(proj, o, do, lse, bias2):
    t = o.shape[0]
    nt = t // TQ
    qg, kg = QG_BWD, QG_BWD + LEFT
    nkb = (t + TQ) // LANES

    def body(q_ref, kp_ref, kc_ref, vp_ref, vc_ref, o_ref, do_ref, lse_ref, b2_ref,
             dq_ref, dk_hbm, dv_hbm, db_hbm, kwin, vwin, dk_acc, dv_acc, db_acc):
        i = pl.program_id(0)
        first = i == 0

        @pl.when(first)
        def _():
            dk_acc[...] = jnp.zeros_like(dk_acc)
            dv_acc[...] = jnp.zeros_like(dv_acc)
            db_acc[...] = jnp.zeros_like(db_acc)

        kwin[0:TQ, :] = kp_ref[...]
        kwin[TQ:2 * TQ, :] = kc_ref[...]
        vwin[0:TQ, :] = vp_ref[...]
        vwin[TQ:2 * TQ, :] = vc_ref[...]
        scale = HD ** -0.5
        qmask = _head_masks(scale)
        vmask = _head_masks(1.0)
        low = lax.broadcasted_iota(jnp.int32, (1, LANES), 1) < HD

        def group(g, carry):
            r0 = pl.multiple_of(g * qg, qg)
            base = i * (TQ // LANES) + g * (qg // LANES)
            pen = _key_penalty(first, r0, kg)
            for hp in range(NH // 2):
                ls = slice(LANES * hp, LANES * (hp + 1))
                qb = q_ref[pl.ds(r0, qg), ls]
                kw = kwin[pl.ds(r0, kg), ls]
                dob = do_ref[pl.ds(r0, qg), ls]
                prod = dob.astype(F32) * o_ref[pl.ds(r0, qg), ls]
                lseb = lse_ref[pl.ds(r0, qg), ls]
                q2 = jnp.concatenate([qb * qmask[0], qb * qmask[1]], axis=0)
                do2 = jnp.concatenate([dob * vmask[0], dob * vmask[1]], axis=0)
                lse2 = jnp.concatenate([lseb[:, 0:1], lseb[:, HD:HD + 1]], axis=0)
                dsum = jnp.concatenate([jnp.sum(jnp.where(low, prod, 0.0), axis=-1, keepdims=True),
                                        jnp.sum(jnp.where(low, 0.0, prod), axis=-1, keepdims=True)], axis=0)
                s = lax.dot_general(q2, kw, NT, preferred_element_type=F32) + b2_ref[hp] + pen
                p = jnp.exp(s - lse2)
                dp = lax.dot_general(do2, vwin[pl.ds(r0, kg), ls], NT, preferred_element_type=F32)
                ds = p * (dp - dsum)
                db_acc[hp] += ds
                dsb = ds.astype(BF16)
                dq2 = jnp.dot(dsb, kw, preferred_element_type=F32)
                dq_ref[pl.ds(r0, qg), ls] = (jnp.where(low, dq2[:qg], dq2[qg:]) * scale).astype(BF16)
                dkt = lax.dot_general(q2, dsb, TN, preferred_element_type=F32)
                dvt = lax.dot_general(do2, p.astype(BF16), TN, preferred_element_type=F32)
                for kb in range(kg // LANES):
                    dk_acc[base + kb, ls, :] += dkt[:, LANES * kb:LANES * (kb + 1)]
                    dv_acc[base + kb, ls, :] += dvt[:, LANES * kb:LANES * (kb + 1)]
            return carry

        lax.fori_loop(0, TQ // qg, group, 0)

        @pl.when(i == nt - 1)
        def _():
            pltpu.sync_copy(dk_acc, dk_hbm)
            pltpu.sync_copy(dv_acc, dv_hbm)
            pltpu.sync_copy(db_acc, db_hbm)

    row = lambda w: pl.BlockSpec((TQ, w), lambda i: (i, 0))
    return pl.pallas_call(
        body, grid=(nt,),
        in_specs=_attn_window_specs() + [row(CW), row(CW), row(CW), _const((NH // 2, 2 * qg, kg))],
        out_specs=[row(CW), _any(), _any(), _any()],
        out_shape=[jax.ShapeDtypeStruct((t, CW), BF16), jax.ShapeDtypeStruct((nkb, CW, LANES), F32),
                   jax.ShapeDtypeStruct((nkb, CW, LANES), F32), jax.ShapeDtypeStruct((NH // 2, 2 * qg, kg), F32)],
        scratch_shapes=[pltpu.VMEM((2 * TQ, CW), BF16), pltpu.VMEM((2 * TQ, CW), BF16),
                        pltpu.VMEM((nkb, CW, LANES), F32), pltpu.VMEM((nkb, CW, LANES), F32),
                        pltpu.VMEM((NH // 2, 2 * qg, kg), F32)],
        compiler_params=_cp(("arbitrary",)), name="bwd_attn",
    )(proj, proj, proj, proj, proj, o, do, lse, bias2)


def bwd_inproj(dxm, x, dhc, dbg, dcg, dq, dk, dv, g, w_all):
    t = x.shape[0]
    wc = PROJ // NCHIP

    def body(dxm_ref, x_ref, dhc_ref, dbg_ref, dcg_ref, dq_ref, dk_ref, dv_ref, g_ref, w_hbm,
             dx_ref, dp_ref, h_ref, dg_ref, w_v):
        @pl.when(pl.program_id(0) == 0)
        def _():
            pltpu.sync_copy(w_hbm, w_v)
            dg_ref[...] = jnp.zeros_like(dg_ref)

        dp_ref[:, 0:CW] = dhc_ref[...]
        dp_ref[:, CW:2 * CW] = dbg_ref[...]
        dp_ref[:, 2 * CW:3 * CW] = dcg_ref[...]
        dp_ref[:, 3 * CW:4 * CW] = dq_ref[...]
        for kb in range(TQ // LANES):
            rows = slice(LANES * kb, LANES * (kb + 1))
            dp_ref[rows, 4 * CW:5 * CW] = jnp.transpose(dk_ref[kb]).astype(BF16)
            dp_ref[rows, 5 * CW:6 * CW] = jnp.transpose(dv_ref[kb]).astype(BF16)
        dh = jnp.zeros((TQ, D), F32)
        for b in range(NCHIP):
            dh = dh + lax.dot_general(dp_ref[:, wc * b:wc * (b + 1)], w_v[b], NT, preferred_element_type=F32)
        xv = x_ref[...]
        gv = g_ref[...]
        h_ref[...] = _rms(xv, gv).astype(BF16)
        dxv, dgv = _rms_bwd(dh, xv, gv)
        dg_ref[...] += dgv
        dx_ref[...] = dxm_ref[...] + dxv

    row = lambda w: pl.BlockSpec((TQ, w), lambda i: (i, 0))
    pad = pl.BlockSpec((TQ // LANES, CW, LANES), lambda i: (i + 1, 0, 0))
    return pl.pallas_call(
        body, grid=(t // TQ,),
        in_specs=[row(D), row(D), row(CW), row(CW), row(CW), row(CW), pad, pad, _const((1, D)), _any()],
        out_specs=[row(D), row(PROJ), row(D), _const((1, D))],
        out_shape=[jax.ShapeDtypeStruct((t, D), F32), jax.ShapeDtypeStruct((t, PROJ), BF16),
                   jax.ShapeDtypeStruct((t, D), BF16), jax.ShapeDtypeStruct((1, D), F32)],
        scratch_shapes=[pltpu.VMEM((NCHIP, D, wc), BF16)],
        compiler_params=_cp(("arbitrary",)), name="bwd_inproj",
    )(dxm, x, dhc, dbg, dcg, dq, dk, dv, g, w_all)


def wgrad(a, b, kb, nb, by_columns, name):
    t, k = a.shape
    n = b.shape[1]
    tk = 512

    def body(a_ref, b_ref, o_ref):
        o_ref[...] = jnp.zeros_like(o_ref)
        for c in range(t // tk):
            o_ref[...] += lax.dot_general(a_ref[tk * c:tk * (c + 1), :], b_ref[tk * c:tk * (c + 1), :], TN,
                                          preferred_element_type=F32)

    if by_columns:
        assert nb == n // NCHIP
        out_spec = pl.BlockSpec((None, kb, nb), lambda ki, ni: (ni, ki, 0))
        out_shape = jax.ShapeDtypeStruct((NCHIP, k, nb), F32)
    else:
        assert nb == n
        out_spec = pl.BlockSpec((kb, nb), lambda ki, ni: (ki, 0))
        out_shape = jax.ShapeDtypeStruct((k, n), F32)
    return pl.pallas_call(
        body, grid=(k // kb, n // nb),
        in_specs=[pl.BlockSpec((t, kb), lambda ki, ni: (0, ki)), pl.BlockSpec((t, nb), lambda ki, ni: (0, ni))],
        out_specs=out_spec, out_shape=out_shape,
        compiler_params=_cp(("arbitrary", "arbitrary")), name=name)(a, b)


TOE = 1024
assert 2 * QG_FWD + LEFT <= TOE
N_FLAT = LEFT - REL_CLIP + 1
N_VAR = BAND - N_FLAT


def _diag_vector(table):
    last = table[:, 2 * REL_CLIP:]
    var = table[:, 2 * REL_CLIP - N_VAR:2 * REL_CLIP][:, ::-1]
    return jnp.concatenate([jnp.broadcast_to(last, (NH, N_FLAT)), var, jnp.broadcast_to(last, (NH, TOE - BAND))], axis=1)


def _diag_vector_bwd(dvec):
    dlast = jnp.sum(dvec[:, :N_FLAT], axis=1, keepdims=True) + jnp.sum(dvec[:, BAND:], axis=1, keepdims=True)
    dvar = dvec[:, N_FLAT:BAND][:, ::-1]
    return jnp.concatenate([jnp.zeros((NH, 2 * REL_CLIP - N_VAR), F32), dvar, dlast], axis=1)


def _band_valid(qg):
    r = lax.broadcasted_iota(jnp.int32, (qg, qg + LEFT), 0)
    p = lax.broadcasted_iota(jnp.int32, (qg, qg + LEFT), 1)
    start = lax.shift_left(lax.shift_right_logical(r, 6), 6)
    return (p >= start) & (p < start + BAND)


def bias_expand(vec, qgs, after=()):
    def body(v_ref, *o_refs):
        for qg, o_ref in zip(qgs, o_refs):
            valid = _band_valid(qg)
            for h in range(NH):
                rows = jnp.broadcast_to(v_ref[h:h + 1, :], (qg, TOE))
                toe = pltpu.roll(rows, 0, 1, stride=1, stride_axis=0)
                o_ref[h // 2, qg * (h % 2):qg * (h % 2 + 1), :] = jnp.where(valid, toe[:, :qg + LEFT], NEG_INF)

    vm = pl.BlockSpec(memory_space=pltpu.VMEM)
    return pl.pallas_call(_behind(body, 1, after), in_specs=[vm] + [_any()] * len(after), out_specs=[vm] * len(qgs),
                          out_shape=[jax.ShapeDtypeStruct((NH // 2, 2 * qg, qg + LEFT), F32) for qg in qgs],
                          name="bias_expand")(vec, *after)


def bias_reduce(db2):
    _, qg, kg = db2.shape

    def body(d_ref, o_ref):
        ii = lax.broadcasted_iota(jnp.int32, (kg, kg), 0)
        jj = lax.broadcasted_iota(jnp.int32, (kg, kg), 1)
        flip = jnp.where(ii + jj == kg - 1, 1.0, 0.0).astype(BF16)
        for h in range(NH):
            rest = d_ref[h]
            rev = jnp.zeros((qg, kg), F32)
            for _ in range(3):
                term = rest.astype(BF16)
                rev = rev + jnp.dot(term, flip, preferred_element_type=F32)
                rest = rest - term.astype(F32)
            d = jnp.concatenate([jnp.zeros((qg, TOE - kg), F32), rev], axis=1)
            back = pltpu.roll(d, 0, 1, stride=1, stride_axis=0)
            o_ref[h:h + 1, :] = jnp.sum(back, axis=0, keepdims=True)

    rev = pl.pallas_call(body, out_shape=jax.ShapeDtypeStruct((NH, TOE), F32), name="bias_reduce")(db2)
    return rev[:, ::-1]


def _place():
    x, y, c = lax.axis_index("x"), lax.axis_index("y"), lax.axis_index("c")
    chips = [(1 - x, y), (x, 1 - y), (1 - x, 1 - y)]
    return x, y, c, chips


def _half(ref_rows, c):
    return pl.ds(c * (ref_rows // 2), ref_rows // 2)


HBM_SPEC = pl.BlockSpec(memory_space=pltpu.HBM)
SEM_SPEC = pl.BlockSpec(memory_space=pltpu.SEMAPHORE)
IN_FLIGHT = pltpu.CompilerParams(has_side_effects=pltpu.SideEffectType.DATAFLOW_SIDE_EFFECTING)


def _in_hbm(a):
    return pltpu.with_memory_space_constraint(a, pltpu.HBM)


def cast_to_slot(ws, chip, layer, after=()):
    n = len(ws)
    steps = 4

    def body(b_ref, *refs):
        del b_ref
        for w_ref, o_ref in zip(refs[:n], refs[n + len(after):]):
            o_ref[...] = w_ref[...].astype(BF16)

    grid_spec = pltpu.PrefetchScalarGridSpec(
        num_scalar_prefetch=1, grid=(steps,),
        in_specs=[pl.BlockSpec((None, w.shape[1] // steps, w.shape[2]), lambda r, b: (layer, r, 0)) for w in ws]
        + [_any()] * len(after),
        out_specs=[pl.BlockSpec((None, w.shape[1] // steps, w.shape[2]), lambda r, b: (b[0], r, 0)) for w in ws])
    return pl.pallas_call(body, grid_spec=grid_spec,
                          out_shape=[jax.ShapeDtypeStruct((NCHIP,) + w.shape[1:], BF16) for w in ws],
                          compiler_params=_cp(("arbitrary",)), name="cast_to_slot")(chip, *ws, *after)


def _gather_copies(bufs, send, recv):
    x, y, c, chips = _place()
    b = 2 * x + y
    out = []
    for k, buf in enumerate(bufs):
        rows = buf.shape[1]
        mine = buf.at[b, _half(rows, c), :]
        for j, (cx, cy) in enumerate(chips):
            theirs = buf.at[2 * cx + cy, _half(rows, c), :]
            sems = dict(send_sem=send.at[3 * k + j], recv_sem=recv.at[3 * k + j],
                        device_id=(cx, cy, c), device_id_type=MESH)
            out.append((pltpu.make_async_remote_copy(src_ref=mine, dst_ref=mine, **sems),
                        pltpu.make_async_remote_copy(src_ref=theirs, dst_ref=theirs, **sems)))
    return out


def gather_start(bufs, after, layer):
    n = len(bufs)

    def body(*refs):
        ins = refs[:n]
        send, recv = refs[n + 1], refs[n + 2]
        token = refs[-1]
        for start, _ in _gather_copies(ins, send, recv):
            start.start()
        token[...] = jnp.zeros_like(token)

    sems = pltpu.SemaphoreType.DMA((3 * n,))
    res = pl.pallas_call(
        body, name=f"gather_start_{layer}",
        in_specs=[HBM_SPEC] * n + [_any()],
        out_specs=[SEM_SPEC, SEM_SPEC] + [HBM_SPEC] * n + [pl.BlockSpec(memory_space=pltpu.VMEM)],
        out_shape=[sems, sems] + [pltpu.HBM(b.shape, b.dtype) for b in bufs] + [jax.ShapeDtypeStruct((8, LANES), F32)],
        input_output_aliases={k: 2 + k for k in range(n)}, compiler_params=IN_FLIGHT,
    )(*[_in_hbm(b) for b in bufs], after)
    return res[0], res[1], res[2:2 + n], res[-1]


def gather_wait(send, recv, bufs, after, layer):
    n = len(bufs)

    def body(*refs):
        ins = refs[:n]
        send_ref, recv_ref = refs[n], refs[n + 1]
        for start, arrival in _gather_copies(ins, send_ref, recv_ref):
            start.wait_send()
            arrival.wait_recv()

    return pl.pallas_call(
        body, name=f"gather_wait_{layer}",
        in_specs=[HBM_SPEC] * n + [SEM_SPEC, SEM_SPEC, _any()], out_specs=[HBM_SPEC] * n,
        out_shape=[pltpu.HBM(b.shape, b.dtype) for b in bufs],
        input_output_aliases={k: k for k in range(n)}, compiler_params=IN_FLIGHT,
    )(*bufs, send, recv, after)


def gather_forward(bufs):
    n = len(bufs)

    def body(*refs):
        outs = refs[n:2 * n]
        send, recv = refs[2 * n:]
        x, y, c, chips = _place()
        cps = []
        for k in range(n):
            rows = outs[k].shape[1]
            for j, (cx, cy) in enumerate(chips):
                sems = dict(send_sem=send.at[3 * k + j], recv_sem=recv.at[3 * k + j],
                            device_id=(x, y, 1 - c), device_id_type=MESH)
                mine = outs[k].at[2 * cx + cy, _half(rows, c), :]
                theirs = outs[k].at[2 * cx + cy, _half(rows, 1 - c), :]
                cp = pltpu.make_async_remote_copy(src_ref=mine, dst_ref=mine, **sems)
                cp.start()
                cps.append((cp, pltpu.make_async_remote_copy(src_ref=theirs, dst_ref=theirs, **sems)))
        for cp, arrival in cps:
            cp.wait_send()
            arrival.wait_recv()

    return pl.pallas_call(
        body, in_specs=[_any()] * n, out_specs=[_any()] * n,
        out_shape=[jax.ShapeDtypeStruct(b.shape, b.dtype) for b in bufs], input_output_aliases={k: k for k in range(n)},
        scratch_shapes=[pltpu.SemaphoreType.DMA((3 * n,)), pltpu.SemaphoreType.DMA((3 * n,))],
        name="gather_forward")(*bufs)


def _forward_copies(bufs, send, recv):
    x, y, c, chips = _place()
    out = []
    for k, buf in enumerate(bufs):
        rows = buf.shape[1]
        for j, (cx, cy) in enumerate(chips):
            sems = dict(send_sem=send.at[3 * k + j], recv_sem=recv.at[3 * k + j],
                        device_id=(x, y, 1 - c), device_id_type=MESH)
            mine = buf.at[2 * cx + cy, _half(rows, c), :]
            theirs = buf.at[2 * cx + cy, _half(rows, 1 - c), :]
            out.append((pltpu.make_async_remote_copy(src_ref=mine, dst_ref=mine, **sems),
                        pltpu.make_async_remote_copy(src_ref=theirs, dst_ref=theirs, **sems)))
    return out


def forward_start(bufs, tag):
    n = len(bufs)

    def body(*refs):
        ins = refs[:n]
        send, recv = refs[n], refs[n + 1]
        token = refs[-1]
        for start, _ in _forward_copies(ins, send, recv):
            start.start()
        token[...] = jnp.zeros_like(token)

    sems = pltpu.SemaphoreType.DMA((3 * n,))
    res = pl.pallas_call(
        body, name=f"forward_start_{tag}", in_specs=[HBM_SPEC] * n,
        out_specs=[SEM_SPEC, SEM_SPEC] + [HBM_SPEC] * n + [pl.BlockSpec(memory_space=pltpu.VMEM)],
        out_shape=[sems, sems] + [pltpu.HBM(b.shape, b.dtype) for b in bufs] + [jax.ShapeDtypeStruct((8, LANES), F32)],
        input_output_aliases={k: 2 + k for k in range(n)}, compiler_params=IN_FLIGHT,
    )(*[_in_hbm(b) for b in bufs])
    return res[0], res[1], res[2:2 + n], res[-1]


def forward_wait(send, recv, bufs, after, tag):
    n = len(bufs)

    def body(*refs):
        ins = refs[:n]
        send_ref, recv_ref = refs[n], refs[n + 1]
        for start, arrival in _forward_copies(ins, send_ref, recv_ref):
            start.wait_send()
            arrival.wait_recv()

    return pl.pallas_call(
        body, name=f"forward_wait_{tag}",
        in_specs=[HBM_SPEC] * n + [SEM_SPEC, SEM_SPEC, _any()], out_specs=[HBM_SPEC] * n,
        out_shape=[pltpu.HBM(b.shape, b.dtype) for b in bufs],
        input_output_aliases={k: k for k in range(n)}, compiler_params=IN_FLIGHT,
    )(*bufs, send, recv, after)


def _exchange_copies(srcs, lands, send, recv):
    x, y, c, _ = _place()
    return [pltpu.make_async_remote_copy(
        src_ref=src.at[:, _half(src.shape[1], 1 - c), :], dst_ref=land, send_sem=send.at[k], recv_sem=recv.at[k],
        device_id=(x, y, 1 - c), device_id_type=MESH) for k, (src, land) in enumerate(zip(srcs, lands))]


def exchange_start(srcs, tag):
    n = len(srcs)
    lands = [lax.empty((s.shape[0], s.shape[1] // 2, s.shape[2]), s.dtype) for s in srcs]

    def body(*refs):
        ins, land_refs = refs[:n], refs[n:2 * n]
        send, recv = refs[2 * n], refs[2 * n + 1]
        token = refs[-1]
        for cp in _exchange_copies(ins, land_refs, send, recv):
            cp.start()
        token[...] = jnp.zeros_like(token)

    sems = pltpu.SemaphoreType.DMA((n,))
    res = pl.pallas_call(
        body, name=f"exchange_start_{tag}",
        in_specs=[HBM_SPEC] * (2 * n),
        out_specs=[SEM_SPEC, SEM_SPEC] + [HBM_SPEC] * (2 * n) + [pl.BlockSpec(memory_space=pltpu.VMEM)],
        out_shape=[sems, sems] + [pltpu.HBM(a.shape, a.dtype) for a in list(srcs) + lands]
        + [jax.ShapeDtypeStruct((8, LANES), F32)],
        input_output_aliases={k: 2 + k for k in range(2 * n)}, compiler_params=IN_FLIGHT,
    )(*[_in_hbm(a) for a in list(srcs) + lands])
    return res[0], res[1], res[2:2 + n], res[2 + n:2 + 2 * n], res[-1]


def exchange_wait(send, recv, srcs, lands, after, tag):
    n = len(srcs)

    def body(*refs):
        ins, land_refs = refs[:n], refs[n:2 * n]
        send_ref, recv_ref = refs[2 * n], refs[2 * n + 1]
        for cp in _exchange_copies(ins, land_refs, send_ref, recv_ref):
            cp.wait_send()
            cp.wait_recv()

    res = pl.pallas_call(
        body, name=f"exchange_wait_{tag}",
        in_specs=[HBM_SPEC] * (2 * n) + [SEM_SPEC, SEM_SPEC, _any()], out_specs=[HBM_SPEC] * (2 * n),
        out_shape=[pltpu.HBM(a.shape, a.dtype) for a in list(srcs) + list(lands)],
        input_output_aliases={k: k for k in range(2 * n)}, compiler_params=IN_FLIGHT,
    )(*srcs, *lands, send, recv, after)
    return res[:n], res[n:]


def add_pair(gs, r1s, core):
    n = len(gs)

    def body(c_ref, *refs):
        del c_ref
        for g_ref, r_ref, o_ref in zip(refs[:n], refs[n:2 * n], refs[2 * n:]):
            o_ref[...] = (g_ref[...] + r_ref[...]).astype(BF16)

    blk = lambda r: (None,) + r.shape[1:]
    grid_spec = pltpu.PrefetchScalarGridSpec(
        num_scalar_prefetch=1, grid=(NCHIP,),
        in_specs=[pl.BlockSpec(blk(r), lambda s, c: (s, c[0], 0)) for r in r1s]
        + [pl.BlockSpec(blk(r), lambda s, c: (s, 0, 0)) for r in r1s],
        out_specs=[pl.BlockSpec(blk(r), lambda s, c: (s, 0, 0)) for r in r1s])
    return pl.pallas_call(body, grid_spec=grid_spec, out_shape=[jax.ShapeDtypeStruct(r.shape, BF16) for r in r1s],
                          compiler_params=_cp(("arbitrary",)), name="add_pair")(core, *gs, *r1s)


def _scatter_copies(srcs, lands, send, recv):
    _, _, c, chips = _place()
    out = []
    for k, (src, land) in enumerate(zip(srcs, lands)):
        for j, (cx, cy) in enumerate(chips):
            out.append(pltpu.make_async_remote_copy(
                src_ref=src.at[2 * cx + cy], dst_ref=land.at[j], send_sem=send.at[3 * k + j],
                recv_sem=recv.at[3 * k + j], device_id=(cx, cy, c), device_id_type=MESH))
    return out


def scatter_start(srcs, layer):
    n = len(srcs)
    srcs = list(srcs)
    lands = [lax.empty((3,) + s.shape[1:], s.dtype) for s in srcs]

    def body(*refs):
        ins, land_refs = refs[:n], refs[n:2 * n]
        send, recv = refs[2 * n], refs[2 * n + 1]
        token = refs[-1]
        for cp in _scatter_copies(ins, land_refs, send, recv):
            cp.start()
        token[...] = jnp.zeros_like(token)

    sems = pltpu.SemaphoreType.DMA((3 * n,))
    res = pl.pallas_call(
        body, name=f"scatter_start_{layer}",
        in_specs=[HBM_SPEC] * (2 * n),
        out_specs=[SEM_SPEC, SEM_SPEC] + [HBM_SPEC] * (2 * n) + [pl.BlockSpec(memory_space=pltpu.VMEM)],
        out_shape=[sems, sems] + [pltpu.HBM(a.shape, a.dtype) for a in srcs + lands]
        + [jax.ShapeDtypeStruct((8, LANES), F32)],
        input_output_aliases={k: 2 + k for k in range(2 * n)}, compiler_params=IN_FLIGHT,
    )(*[_in_hbm(a) for a in srcs + lands])
    return res[0], res[1], res[2:2 + n], res[2 + n:2 + 2 * n], res[-1]


def scatter_wait(send, recv, srcs, lands, after, layer):
    n = len(srcs)

    def body(*refs):
        ins, land_refs = refs[:n], refs[n:2 * n]
        send_ref, recv_ref = refs[2 * n], refs[2 * n + 1]
        for cp in _scatter_copies(ins, land_refs, send_ref, recv_ref):
            cp.wait_send()
            cp.wait_recv()

    res = pl.pallas_call(
        body, name=f"scatter_wait_{layer}",
        in_specs=[HBM_SPEC] * (2 * n) + [SEM_SPEC, SEM_SPEC, _any()], out_specs=[HBM_SPEC] * (2 * n),
        out_shape=[pltpu.HBM(a.shape, a.dtype) for a in list(srcs) + list(lands)],
        input_output_aliases={k: k for k in range(2 * n)}, compiler_params=IN_FLIGHT,
    )(*srcs, *lands, send, recv, after)
    return res[n:]


def add_chips(gs, r1s, r2s, place, totals, layer):
    n = len(gs)
    steps = 2

    def body(p_ref, *refs):
        del p_ref
        for g_ref, r1_ref, r2_ref, o_ref in zip(refs[:n], refs[n:2 * n], refs[2 * n:3 * n], refs[4 * n:]):
            own = g_ref[...] + r1_ref[...]
            o_ref[...] = ((own + r2_ref[0].astype(F32)) + r2_ref[1].astype(F32)) + r2_ref[2].astype(F32)

    blk = lambda r: (None, r.shape[1] // steps, r.shape[2])
    grid_spec = pltpu.PrefetchScalarGridSpec(
        num_scalar_prefetch=1, grid=(steps,),
        in_specs=[pl.BlockSpec(blk(r), lambda i, p: (p[1], p[0] * steps + i, 0)) for r in r1s]
        + [pl.BlockSpec(blk(r), lambda i, p: (p[1], i, 0)) for r in r1s]
        + [pl.BlockSpec((3,) + blk(r)[1:], lambda i, p: (0, i, 0)) for r in r1s] + [_any()] * n,
        out_specs=[pl.BlockSpec(blk(r), lambda i, p: (layer, p[0] * steps + i, 0)) for r in r1s])
    return pl.pallas_call(body, grid_spec=grid_spec, out_shape=[jax.ShapeDtypeStruct(t.shape, F32) for t in totals],
                          input_output_aliases={1 + 3 * n + k: k for k in range(n)},
                          compiler_params=_cp(("arbitrary",)), name="add_chips")(place, *gs, *r1s, *r2s, *totals)


def _share_copies(bufs, send, recv):
    x, y, c, _ = _place()
    out = []
    for k, buf in enumerate(bufs):
        sems = dict(send_sem=send.at[k], recv_sem=recv.at[k], device_id=(x, y, 1 - c), device_id_type=MESH)
        mine = buf.at[:, _half(buf.shape[1], c), :]
        theirs = buf.at[:, _half(buf.shape[1], 1 - c), :]
        out.append((pltpu.make_async_remote_copy(src_ref=mine, dst_ref=mine, **sems),
                    pltpu.make_async_remote_copy(src_ref=theirs, dst_ref=theirs, **sems)))
    return out


def share_start(bufs, tag):
    n = len(bufs)

    def body(*refs):
        ins = refs[:n]
        send, recv = refs[n], refs[n + 1]
        token = refs[-1]
        for start, _ in _share_copies(ins, send, recv):
            start.start()
        token[...] = jnp.zeros_like(token)

    sems = pltpu.SemaphoreType.DMA((n,))
    res = pl.pallas_call(
        body, name=f"share_start_{tag}", in_specs=[HBM_SPEC] * n,
        out_specs=[SEM_SPEC, SEM_SPEC] + [HBM_SPEC] * n + [pl.BlockSpec(memory_space=pltpu.VMEM)],
        out_shape=[sems, sems] + [pltpu.HBM(b.shape, b.dtype) for b in bufs] + [jax.ShapeDtypeStruct((8, LANES), F32)],
        input_output_aliases={k: 2 + k for k in range(n)}, compiler_params=IN_FLIGHT,
    )(*[_in_hbm(b) for b in bufs])
    return res[0], res[1], res[2:2 + n], res[-1]


def share_wait(send, recv, bufs, after, tag):
    n = len(bufs)

    def body(*refs):
        ins = refs[:n]
        send_ref, recv_ref = refs[n], refs[n + 1]
        for start, arrival in _share_copies(ins, send_ref, recv_ref):
            start.wait_send()
            arrival.wait_recv()

    return pl.pallas_call(
        body, name=f"share_wait_{tag}",
        in_specs=[HBM_SPEC] * n + [SEM_SPEC, SEM_SPEC, _any()], out_specs=[HBM_SPEC] * n,
        out_shape=[pltpu.HBM(b.shape, b.dtype) for b in bufs],
        input_output_aliases={k: k for k in range(n)}, compiler_params=IN_FLIGHT,
    )(*bufs, send, recv, after)


def small_allreduce(v, after=()):
    rows = v.shape[0]
    flips = [(fx, fy, fc) for fx in (0, 1) for fy in (0, 1) for fc in (0, 1)][1:]

    def body(v_ref, o_ref, buf, send, recv):
        x, y, c, _ = _place()
        buf[4 * x + 2 * y + c] = v_ref[...]
        peers = [(jnp.where(fx, 1 - x, x), jnp.where(fy, 1 - y, y), jnp.where(fc, 1 - c, c)) for fx, fy, fc in flips]
        cps = []
        for k, peer in enumerate(peers):
            cp = pltpu.make_async_remote_copy(
                src_ref=v_ref, dst_ref=buf.at[4 * x + 2 * y + c], send_sem=send.at[k], recv_sem=recv.at[k],
                device_id=peer, device_id_type=MESH)
            cp.start()
            cps.append(cp)
        for k, (px, py, pc) in enumerate(peers):
            pltpu.make_async_remote_copy(
                src_ref=v_ref, dst_ref=buf.at[4 * px + 2 * py + pc], send_sem=send.at[k], recv_sem=recv.at[k],
                device_id=(px, py, pc), device_id_type=MESH).wait_recv()
        for cp in cps:
            cp.wait_send()
        acc = buf[0]
        for s in range(1, 8):
            acc = acc + buf[s]
        o_ref[...] = acc

    vm = pl.BlockSpec(memory_space=pltpu.VMEM)
    return pl.pallas_call(
        _behind(body, 1, after), in_specs=[vm] + [_any()] * len(after), out_specs=vm,
        out_shape=jax.ShapeDtypeStruct((rows, SMALL_COLS), F32),
        scratch_shapes=[pltpu.VMEM((8, rows, SMALL_COLS), F32), pltpu.SemaphoreType.DMA((7,)),
                        pltpu.SemaphoreType.DMA((7,))],
        name="reduce_small")(v, *after)


def adamw(w, g, m, v, rb, name, after=()):
    nl, rows, cols = w.shape

    def body(w_ref, g_ref, m_ref, v_ref, go_ref, d_ref, nm_ref, nv_ref):
        gv = g_ref[...]
        go_ref[...] = gv
        nm = ADAM_B1 * m_ref[...] + (1.0 - ADAM_B1) * gv
        nv = ADAM_B2 * v_ref[...] + (1.0 - ADAM_B2) * (gv * gv)
        m_hat = nm / (1.0 - ADAM_B1 ** ADAM_STEP)
        v_hat = nv / (1.0 - ADAM_B2 ** ADAM_STEP)
        d_ref[...] = -ADAM_LR * (m_hat / (jnp.sqrt(v_hat) + ADAM_EPS) + ADAM_WD * w_ref[...])
        nm_ref[...] = nm
        nv_ref[...] = nv

    blk = pl.BlockSpec((None, rb, cols), lambda l, r: (l, r, 0))
    shp = jax.ShapeDtypeStruct(w.shape, F32)
    return pl.pallas_call(_behind(body, 4, after), grid=(nl, rows // rb), in_specs=[blk] * 4 + [_any()] * len(after),
                          out_specs=[blk] * 4, out_shape=[shp] * 4,
                          compiler_params=_cp(("arbitrary", "arbitrary")), name=name)(w, g, m, v, *after)


def _pack(parts, rows):
    flat = jnp.concatenate([p.reshape(-1).astype(F32) for p in parts])
    return jnp.pad(flat, (0, rows * SMALL_COLS - flat.shape[0])).reshape(rows, SMALL_COLS)


def _unpack(vec, shapes):
    flat = vec.reshape(-1)
    out, off = [], 0
    for s in shapes:
        size = 1
        for d in s:
            size *= d
        out.append(flat[off:off + size].reshape(s))
        off += size
    return out


def kernel(x, w_in, w_conv, rel_bias, g_conv_out, g_attn_out, w_out, g_pre_mix, g_post_mix, g_pre_ffn, g_post_ffn, w_ffn_in, w_ffn_out, loss_target, m_w_in, m_w_conv, m_rel_bias, m_g_conv_out, m_g_attn_out, m_w_out, m_g_pre_mix, m_g_post_mix, m_g_pre_ffn, m_g_post_ffn, m_w_ffn_in, m_w_ffn_out, v_w_in, v_w_conv, v_rel_bias, v_g_conv_out, v_g_attn_out, v_w_out, v_g_pre_mix, v_g_post_mix, v_g_pre_ffn, v_g_post_ffn, v_w_ffn_in, v_w_ffn_out):
    xi, yi, ci = lax.axis_index("x"), lax.axis_index("y"), lax.axis_index("c")
    chip = 2 * xi + yi
    nl = w_in.shape[0]
    x0 = x[0]
    target = loss_target[0]
    cwl = CW // NCHIP

    chip1 = chip.reshape(1).astype(jnp.int32)
    big_weights = [w_in, w_out, w_ffn_in, w_ffn_out]
    own = [cast_to_slot(big_weights, chip1, 0)]
    wc_mine = jnp.pad(w_conv.reshape(-1), (0, 16 * LANES - w_conv.size)).reshape(1, 16, LANES)
    wc_slot = lax.dynamic_update_slice_in_dim(jnp.zeros((NCHIP, 16, LANES), F32), wc_mine, chip, axis=0)
    gm = jnp.kron(jnp.eye(CW // HD, dtype=F32), jnp.full((HD, HD), 1.0 / HD, F32)).astype(BF16)
    row = lambda a, l: a[l][None, :]

    def gather_finish(flight, after, tag):
        send, recv, bufs, _ = flight
        return gather_forward(gather_wait(send, recv, bufs, after, tag))

    first_mix = gather_start(list(own[0][:2]) + [wc_slot], x0, "0m")
    first_ffn = gather_start(own[0][2:], first_mix[3], "0f")
    chain = first_ffn[3]
    biases = []
    for l in range(nl):
        biases.append(bias_expand(_diag_vector(rel_bias[l]), (QG_FWD, QG_BWD), [chain]))
        chain = biases[l][1]
    for l in range(1, nl):
        own.append(cast_to_slot(big_weights, chip1, l, [chain]))
        chain = own[l][0]
    gw_in, gw_out, wc_all = gather_finish(first_mix, chain, "0m")
    wc_full = wc_all.reshape(NCHIP, -1)[:, :nl * cwl * 3].reshape(NCHIP, nl, cwl, 3)
    wc_full = jnp.transpose(wc_full, (1, 0, 2, 3)).reshape(nl, CW, 3)
    wconv_t = jnp.pad(jnp.transpose(wc_full, (0, 2, 1)), ((0, 0), (0, 5), (0, 0)))
    flights, to_sibling = {}, None
    saved, weights = [], []
    h = x0
    for l in range(nl):
        if l == 0:
            pass
        elif l == 1:
            gw_in, gw_out, gw_fi, gw_fo = gather_finish(flights[l], h, l)
        else:
            gw_in, gw_out, gw_fi, gw_fo = forward_wait(*to_sibling[:3], h, l)
        gw_out = gw_out.reshape(D, D)
        behind_mix, behind_ffn = ([first_ffn[3]] if l == 0 else []), []
        if l + 1 < nl and l + 1 not in flights:
            flights[l + 1] = gather_start(own[l + 1], first_ffn[3] if l == 0 else gw_in, l + 1)
            behind_mix.append(flights[l + 1][3])
        bias2, bias2_bwd = biases[l]
        proj = fwd_inproj(h, row(g_pre_mix, l), gw_in, behind_mix)
        xmid, o, lse, y, z = fwd_mix(h, proj, bias2, wconv_t[l], row(g_conv_out, l), row(g_attn_out, l),
                                     row(g_post_mix, l), gm, gw_out)
        if l == 0:
            gw_fi, gw_fo = gather_finish(first_ffn, xmid, "0f")
        elif l + 1 < nl:
            send, recv, bufs, _ = flights[l + 1]
            landed = gather_wait(send, recv, bufs, xmid, l + 1)
            to_sibling = forward_start(landed, l + 1)
            behind_ffn.append(to_sibling[3])
            if l + 2 < nl:
                flights[l + 2] = gather_start(own[l + 2], to_sibling[3], l + 2)
                behind_ffn.append(flights[l + 2][3])
        gw_fo = gw_fo.reshape(2, DFF // 2, D)
        gu, f, xout = fwd_ffn(xmid, row(g_pre_ffn, l), row(g_post_ffn, l), gw_fi, gw_fo, behind_ffn)
        saved.append((h, proj, bias2_bwd, xmid, o, lse, y, z, gu, f))
        weights.append((gw_in, gw_out, gw_fi, gw_fo))
        h = xout
    dx, loss_blk = loss_head(h, target)

    core = ci.reshape(1).astype(jnp.int32)
    place = jnp.stack([ci, chip]).astype(jnp.int32)
    totals = [lax.empty(w.shape, F32) for w in (w_in, w_out, w_ffn_in, w_ffn_out)]
    small = {k: [None] * nl for k in ("co", "ao", "pm", "qm", "pf", "qf", "rel", "wc")}

    def reduce_begin(kinds, grads, tag):
        return kinds, exchange_start(grads, tag), tag

    def reduce_mid(state, after):
        kinds, (send, recv, srcs, lands, _), tag = state
        grads, from_sibling = exchange_wait(send, recv, srcs, lands, after, tag)
        return kinds, grads, from_sibling, scatter_start(add_pair(grads, from_sibling, core), tag), tag

    def reduce_end(state, after, totals, layer):
        kinds, grads, from_sibling, (send, recv, srcs, lands, _), tag = state
        from_chips = scatter_wait(send, recv, srcs, lands, after, tag)
        totals = list(totals)
        summed = add_chips(grads, from_sibling, from_chips, place, [totals[i] for i in kinds], layer)
        for i, t in zip(kinds, summed):
            totals[i] = t
        return totals

    begun = flying = None
    for l in reversed(range(nl)):
        hin, proj, bias2, xmid, o, lse, y, z, gu, f = saved[l]
        gw_in, gw_out, gw_fi, gw_fo = weights[l]
        behind_ffn = [begun[1][4]] if begun is not None else []
        dxm, dfb, act, dgu, h2, dg_qf, dg_pf = bwd_ffn(dx, f, xmid, gu, row(g_pre_ffn, l), row(g_post_ffn, l),
                                                        gw_fi, gw_fo, behind_ffn)
        behind_mix, behind_conv = [], []
        if begun is not None:
            flying = reduce_mid(begun, dxm)
            behind_mix.append(flying[3][4])
        gr_fo = wgrad(act, dfb, 256, D, False, "wgrad_ffn_out").reshape(NCHIP, DFF // NCHIP, D)
        gr_fi = wgrad(h2, dgu, 512, 2 * DFF // NCHIP, True, "wgrad_ffn_in")
        if l == 0:
            begun_ffn = reduce_begin([2, 3], [gr_fi, gr_fo], "0f")
            behind_mix.append(begun_ffn[1][4])
        dzb, do, dco, dbg, dg_qm, dg_co, dg_ao = bwd_mix(dxm, z, o, proj, wconv_t[l], row(g_conv_out, l),
                                                          row(g_attn_out, l), row(g_post_mix, l), gm, gw_out,
                                                          behind_mix)
        if l == 0:
            flying_ffn = reduce_mid(begun_ffn, dzb)
            behind_conv.append(flying_ffn[3][4])
        gr_out = wgrad(y, dzb, 512, D, False, "wgrad_out").reshape(NCHIP, D // NCHIP, D)
        dhc, dcg, dwc = bwd_conv(dco, proj, wconv_t[l], behind_conv)
        dq, dk, dv, db2 = bwd_attn(proj, o, do, lse, bias2)
        dx, dproj, hb, dg_pm = bwd_inproj(dxm, hin, dhc, dbg, dcg, dq, dk, dv, row(g_pre_mix, l), gw_in)
        if flying is not None:
            totals = reduce_end(flying, dx, totals, l + 1)
        gr_in = wgrad(hb, dproj, 512, PROJ // NCHIP, True, "wgrad_in")
        small["co"][l], small["ao"][l], small["pm"][l], small["qm"][l] = dg_co, dg_ao, dg_pm, dg_qm
        small["pf"][l], small["qf"][l] = dg_pf, dg_qf
        small["rel"][l] = _diag_vector_bwd(bias_reduce(db2.reshape(NH, QG_BWD, QG_BWD + LEFT)))
        small["wc"][l] = jnp.transpose(dwc[0:3], (1, 0))
        if l > 0:
            begun = reduce_begin([0, 1, 2, 3], [gr_in, gr_out, gr_fi, gr_fo], l)
    flying_mix = reduce_mid(reduce_begin([0, 1], [gr_in, gr_out], "0m"), dx)
    totals = reduce_end(flying_ffn, flying_mix[3][4], totals, 0)
    share_ffn = share_start(totals[2:], "ffn")

    order = ("co", "ao", "pm", "qm", "pf", "qf", "rel", "wc")
    parts = [jnp.stack(small[k]) for k in order] + [loss_blk[0:1, 0:1]]
    shapes = [p.shape for p in parts]
    red_vec = small_allreduce(_pack(parts, 40), [share_ffn[3]])
    red = _unpack(red_vec, shapes)

    gr_fi, gr_fo = share_wait(*share_ffn[:3], red_vec, "ffn")
    big_fi = adamw(w_ffn_in, gr_fi, m_w_ffn_in, v_w_ffn_in, w_ffn_in.shape[1] // 4, "adamw_ffn_in")
    totals = reduce_end(flying_mix, big_fi[1], totals, 0)
    share_mix = share_start(totals[:2], "mix")
    big_fo = adamw(w_ffn_out, gr_fo, m_w_ffn_out, v_w_ffn_out, w_ffn_out.shape[1] // 4, "adamw_ffn_out",
                   [share_mix[3]])
    gr_in, gr_out = share_wait(*share_mix[:3], big_fo[1], "mix")
    big_in = adamw(w_in, gr_in, m_w_in, v_w_in, w_in.shape[1] // 4, "adamw_in")
    big_out = adamw(w_out, gr_out, m_w_out, v_w_out, w_out.shape[1] // 4, "adamw_out")
    big = [big_in, big_out, big_fi, big_fo]
    gr_co, gr_ao, gr_pm, gr_qm, gr_pf, gr_qf, gr_rel, gr_wc_full, loss = red
    gr_co, gr_ao, gr_pm, gr_qm, gr_pf, gr_qf = [a.reshape(nl, -1) for a in (gr_co, gr_ao, gr_pm, gr_qm, gr_pf, gr_qf)]
    gr_wc = lax.dynamic_slice_in_dim(gr_wc_full, chip * cwl, cwl, axis=1)
    loss = loss.reshape(())

    sw = [g_conv_out, g_attn_out, g_pre_mix, g_post_mix, g_pre_ffn, g_post_ffn, rel_bias, w_conv]
    sg = [gr_co, gr_ao, gr_pm, gr_qm, gr_pf, gr_qf, gr_rel, gr_wc]
    sm = [m_g_conv_out, m_g_attn_out, m_g_pre_mix, m_g_post_mix, m_g_pre_ffn, m_g_post_ffn, m_rel_bias, m_w_conv]
    sv = [v_g_conv_out, v_g_attn_out, v_g_pre_mix, v_g_post_mix, v_g_pre_ffn, v_g_post_ffn, v_rel_bias, v_w_conv]
    sshapes = [a.shape for a in sw]
    packed = [_pack(a, 32)[None] for a in (sw, sg, sm, sv)]
    s_out = [_unpack(a[0], sshapes) for a in adamw(*packed, 32, "adamw_small")]

    def leaves(big_i, small_i):
        b_in, b_out, b_fi, b_fo = big_i
        s_co, s_ao, s_pm, s_qm, s_pf, s_qf, s_rel, s_wc = small_i
        return [b_in, s_wc, s_rel, s_co, s_ao, b_out, s_pm, s_qm, s_pf, s_qf, b_fi, b_fo]

    out = [loss, dx[None]]
    out += leaves([b[0] for b in big], sg)
    for i in range(1, 4):
        out += leaves([b[i] for b in big], s_out[i])
    return tuple(out)
```

```python
import jax
import jax.numpy as jnp
from jax import lax
from jax.experimental import pallas as pl
from jax.experimental.pallas import tpu as pltpu

F32 = jnp.float32
BF16 = jnp.bfloat16

D = 1024
PROJ = 3072
CW = 512
HD = 64
NH = 8
CHUNK = 64
BAND = 576
REL_CLIP = 128
NREL = 2 * REL_CLIP + 1
DFF = 2816
DEPTH = 4
NCHIP = 4
EPS = 1e-6
NEG_INF = -1e30

ADAM_LR = 0.001
ADAM_B1 = 0.9
ADAM_B2 = 0.999
ADAM_EPS = 1e-08
ADAM_WD = 0.01
ADAM_STEP = 10

V7X_VMEM_BYTES = 64 * 1024 * 1024
VMEM_LIMIT = V7X_VMEM_BYTES - 8 * 1024 * 1024
LANES = 128
QG_FWD = 4 * CHUNK
QG_BWD = 2 * CHUNK
LEFT = BAND - CHUNK
TQ = 512
TM = 256
SMALL_COLS = 1024
MESH = pl.DeviceIdType.MESH
NT = (((1,), (1,)), ((), ()))
TN = (((0,), (0,)), ((), ()))


def _cp(sem=None, vmem=VMEM_LIMIT):
    return pltpu.CompilerParams(dimension_semantics=sem, vmem_limit_bytes=vmem)


def _any():
    return pl.BlockSpec(memory_space=pl.ANY)


def _const(shape):
    nd = len(shape)
    return pl.BlockSpec(shape, lambda *_: (0,) * nd)


def _behind(body, n_in, after):
    def ordered(*refs):
        return body(*refs[:n_in], *refs[n_in + len(after):])
    return ordered


def _rms(v, g):
    r = lax.rsqrt(jnp.mean(v * v, axis=-1, keepdims=True) + EPS)
    return v * r * g


def _rms_bwd(dy, v, g):
    r = lax.rsqrt(jnp.mean(v * v, axis=-1, keepdims=True) + EPS)
    vh = v * r
    dg = jnp.sum(dy * vh, axis=0, keepdims=True)
    dvh = dy * g
    dv = r * (dvh - vh * jnp.mean(dvh * vh, axis=-1, keepdims=True))
    return dv, dg


def _group_mean(v, gm):
    return jnp.dot(v.astype(BF16), gm, preferred_element_type=F32)


def _group_rms_bwd(dy, v, g, gm):
    r = lax.rsqrt(_group_mean(v * v, gm) + EPS)
    vh = v * r
    dg = jnp.sum(dy * vh, axis=0, keepdims=True)
    dvh = dy * g
    dv = r * (dvh - vh * _group_mean(dvh * vh, gm))
    return dv, dg


def _head_masks(scale):
    lane = lax.broadcasted_iota(jnp.int32, (1, LANES), 1)
    return [jnp.where((lane >= HD * a) & (lane < HD * (a + 1)), scale, 0.0).astype(BF16) for a in range(2)]


class _Resident:
    def __init__(self, src, dst, sem):
        self.first = pl.program_id(0) == 0
        self.copy = pltpu.make_async_copy(src, dst, sem)
        self.dst = dst

        @pl.when(self.first)
        def _():
            self.copy.start()

    def read(self):
        @pl.when(self.first)
        def _():
            self.copy.wait()

        return self.dst[...]


FF_CHUNKS = ((0, 1536), (1536, DFF))


def _stream_ffn_weights(wfi_hbm, wfo_hbm, wfi_v, wfo_v, sems, order, step):
    hw = DFF // 2
    per_matrix = {
        0: [(wfi_hbm.at[j], wfi_v.at[0, :, pl.ds(hw * j, hw)]) for j in range(2)],
        1: [(wfi_hbm.at[2 + j], wfi_v.at[1, :, pl.ds(hw * j, hw)]) for j in range(2)],
        2: [(wfo_hbm.at[j], wfo_v.at[pl.ds(hw * j, hw), :]) for j in range(2)],
    }
    pieces = [p for m in order for p in per_matrix[m]]
    slot = {m: 2 * k for k, m in enumerate(order)}

    def make_step(wait):
        def ready(m, chunk):
            if chunk == 0:
                wait(slot[m])
                wait(slot[m] + 1)
        return lambda: step(ready)

    copies = [pltpu.make_async_copy(src, dst, sems.at[k]) for k, (src, dst) in enumerate(pieces)]
    first = pl.program_id(0) == 0

    @pl.when(first)
    def _():
        for cp in copies:
            cp.start()
        make_step(lambda k: copies[k].wait())()

    @pl.when(jnp.logical_not(first))
    def _():
        make_step(lambda k: None)()


def _conv_taps(u_prev, u, scr):
    n = u.shape[0]
    scr[0:16, :] = u_prev
    scr[16:16 + n, :] = u
    return scr[15:15 + n, :], scr[14:14 + n, :]


def fwd_inproj(x, g, w_all, after=()):
    t = x.shape[0]
    wc = PROJ // NCHIP

    def body(x_ref, g_ref, w_hbm, o_ref, w_v):
        @pl.when(pl.program_id(0) == 0)
        def _():
            pltpu.sync_copy(w_hbm, w_v)

        h = _rms(x_ref[...], g_ref[...]).astype(BF16)
        for b in range(NCHIP):
            o_ref[:, wc * b:wc * (b + 1)] = jnp.dot(h, w_v[b], preferred_element_type=F32).astype(BF16)

    return pl.pallas_call(
        _behind(body, 3, after), grid=(t // TQ,),
        in_specs=[pl.BlockSpec((TQ, D), lambda i: (i, 0)), _const((1, D)), _any()] + [_any()] * len(after),
        out_specs=pl.BlockSpec((TQ, PROJ), lambda i: (i, 0)),
        out_shape=jax.ShapeDtypeStruct((t, PROJ), BF16),
        scratch_shapes=[pltpu.VMEM((NCHIP, D, wc), BF16)],
        compiler_params=_cp(("arbitrary",)), name="fwd_inproj")(x, g, w_all, *after)


def _attn_window_specs():
    return [
        pl.BlockSpec((TQ, CW), lambda i: (i, 3)),
        pl.BlockSpec((TQ, CW), lambda i: (jnp.maximum(i - 1, 0), 4)),
        pl.BlockSpec((TQ, CW), lambda i: (i, 4)),
        pl.BlockSpec((TQ, CW), lambda i: (jnp.maximum(i - 1, 0), 5)),
        pl.BlockSpec((TQ, CW), lambda i: (i, 5)),
    ]


def _conv_specs():
    return [
        pl.BlockSpec((TQ, 3 * CW), lambda i: (i, 0)),
        pl.BlockSpec((16, 3 * CW), lambda i: (jnp.maximum(i * (TQ // 16) - 1, 0), 0)),
    ]


def _conv_fwd(pc_ref, pcp_ref, wc_ref, scr, first):
    pc = pc_ref[...].astype(F32)
    hc, bg, cg = pc[:, :CW], pc[:, CW:2 * CW], pc[:, 2 * CW:]
    u = cg * hc
    pp = pcp_ref[...].astype(F32)
    u_prev = jnp.where(first, 0.0, pp[:, 2 * CW:] * pp[:, :CW])
    u1, u2 = _conv_taps(u_prev, u, scr)
    cout = wc_ref[0:1, :] * u2 + wc_ref[1:2, :] * u1 + wc_ref[2:3, :] * u
    return hc, bg, cg, u, u1, u2, cout


def _key_penalty(first, r0, kg):
    col = lax.broadcasted_iota(jnp.int32, (1, kg), 1)
    limit = jnp.where(first, TQ - r0, 0)
    return jnp.where(col < limit, NEG_INF, 0.0)


def fwd_mix(x, proj, bias2, wconv_t, g_co, g_ao, g_pm, gm, wout_all):
    t = x.shape[0]
    qg, kg = QG_FWD, QG_FWD + LEFT

    def body(x_ref, pc_ref, pcp_ref, q_ref, kp_ref, kc_ref, vp_ref, vc_ref, b2_ref, wc_ref, gco_ref, gao_ref, gpm_ref,
             gm_ref, wout_hbm, xmid_ref, o_ref, lse_ref, y_ref, z_ref, wout_v, kwin, vwin, cscr, sems):
        i = pl.program_id(0)
        first = i == 0
        wout = _Resident(wout_hbm, wout_v, sems.at[0])
        kwin[0:TQ, :] = kp_ref[...]
        kwin[TQ:2 * TQ, :] = kc_ref[...]
        vwin[0:TQ, :] = vp_ref[...]
        vwin[TQ:2 * TQ, :] = vc_ref[...]
        qmask = _head_masks(HD ** -0.5)
        low = lax.broadcasted_iota(jnp.int32, (1, LANES), 1) < HD

        def group(g, carry):
            r0 = pl.multiple_of(g * qg, qg)
            pen = _key_penalty(first, r0, kg)
            for hp in range(NH // 2):
                ls = slice(LANES * hp, LANES * (hp + 1))
                qb = q_ref[pl.ds(r0, qg), ls]
                q2 = jnp.concatenate([qb * qmask[0], qb * qmask[1]], axis=0)
                s = lax.dot_general(q2, kwin[pl.ds(r0, kg), ls], NT, preferred_element_type=F32)
                s = s + b2_ref[hp] + pen
                m = jnp.max(s, axis=-1, keepdims=True)
                p = jnp.exp(s - m)
                l = jnp.sum(p, axis=-1, keepdims=True)
                o2 = jnp.dot(p.astype(BF16), vwin[pl.ds(r0, kg), ls], preferred_element_type=F32) * (1.0 / l)
                lse2 = m + jnp.log(l)
                o_ref[pl.ds(r0, qg), ls] = jnp.where(low, o2[:qg], o2[qg:])
                lse_ref[pl.ds(r0, qg), ls] = jnp.where(low, lse2[:qg], lse2[qg:])
            return carry

        lax.fori_loop(0, TQ // qg, group, 0)

        _, bg, _, _, _, _, cout = _conv_fwd(pc_ref, pcp_ref, wc_ref, cscr, first)
        yc = bg * cout
        gmv = gm_ref[...]
        ycn = yc * lax.rsqrt(_group_mean(yc * yc, gmv) + EPS) * gco_ref[...]
        oa = o_ref[...]
        oan = oa * lax.rsqrt(_group_mean(oa * oa, gmv) + EPS) * gao_ref[...]
        y_ref[:, 0:CW] = ycn.astype(BF16)
        y_ref[:, CW:2 * CW] = oan.astype(BF16)
        z = jnp.dot(y_ref[...], wout.read(), preferred_element_type=F32)
        z_ref[...] = z
        xmid_ref[...] = x_ref[...] + _rms(z, gpm_ref[...])

    row = lambda w: pl.BlockSpec((TQ, w), lambda i: (i, 0))
    return pl.pallas_call(
        body, grid=(t // TQ,),
        in_specs=[row(D)] + _conv_specs() + _attn_window_specs() + [
            _const((NH // 2, 2 * qg, kg)), _const((8, CW)), _const((1, CW)), _const((1, CW)), _const((1, D)),
            _const((CW, CW)), _any()],
        out_specs=[row(D), row(CW), row(CW), row(D), row(D)],
        out_shape=[jax.ShapeDtypeStruct((t, D), F32), jax.ShapeDtypeStruct((t, CW), F32),
                   jax.ShapeDtypeStruct((t, CW), F32), jax.ShapeDtypeStruct((t, D), BF16),
                   jax.ShapeDtypeStruct((t, D), F32)],
        scratch_shapes=[pltpu.VMEM((D, D), BF16), pltpu.VMEM((2 * TQ, CW), BF16), pltpu.VMEM((2 * TQ, CW), BF16),
                        pltpu.VMEM((TQ + 16, CW), F32), pltpu.SemaphoreType.DMA((1,))],
        compiler_params=_cp(("arbitrary",)), name="fwd_mix",
    )(x, proj, proj, proj, proj, proj, proj, proj, bias2, wconv_t, g_co, g_ao, g_pm, gm, wout_all)


def fwd_ffn(xmid, g_pre, g_post, wfi_all, wfo_all, after=()):
    t = xmid.shape[0]

    def body(x_ref, gpre_ref, gpost_ref, wfi_hbm, wfo_hbm, gu_ref, f_ref, xo_ref, wfi_v, wfo_v, sems):
        def step(ready):
            xv = x_ref[...]
            h = _rms(xv, gpre_ref[...]).astype(BF16)
            f = jnp.zeros((TM, D), F32)
            for ci, (a, b) in enumerate(FF_CHUNKS):
                ready(0, ci)
                gate = jnp.dot(h, wfi_v[0, :, a:b], preferred_element_type=F32)
                ready(1, ci)
                up = jnp.dot(h, wfi_v[1, :, a:b], preferred_element_type=F32)
                gu_ref[:, a:b] = gate.astype(BF16)
                gu_ref[:, DFF + a:DFF + b] = up.astype(BF16)
                act = gate * (1.0 / (1.0 + jnp.exp(-gate))) * up
                ready(2, ci)
                f = f + jnp.dot(act.astype(BF16), wfo_v[a:b, :], preferred_element_type=F32)
            f_ref[...] = f
            xo_ref[...] = xv + _rms(f, gpost_ref[...])

        _stream_ffn_weights(wfi_hbm, wfo_hbm, wfi_v, wfo_v, sems, (0, 1, 2), step)

    row = lambda w: pl.BlockSpec((TM, w), lambda i: (i, 0))
    return pl.pallas_call(
        _behind(body, 5, after), grid=(t // TM,),
        in_specs=[row(D), _const((1, D)), _const((1, D)), _any(), _any()] + [_any()] * len(after),
        out_specs=[row(2 * DFF), row(D), row(D)],
        out_shape=[jax.ShapeDtypeStruct((t, 2 * DFF), BF16), jax.ShapeDtypeStruct((t, D), F32),
                   jax.ShapeDtypeStruct((t, D), F32)],
        scratch_shapes=[pltpu.VMEM((2, D, DFF), BF16), pltpu.VMEM((DFF, D), BF16), pltpu.SemaphoreType.DMA((6,))],
        compiler_params=_cp(("arbitrary",)), name="fwd_ffn")(xmid, g_pre, g_post, wfi_all, wfo_all, *after)


def loss_head(y, target):
    t = y.shape[0]

    def body(y_ref, t_ref, dy_ref, l_ref):
        @pl.when(pl.program_id(0) == 0)
        def _():
            l_ref[...] = jnp.zeros_like(l_ref)

        e = y_ref[...] - t_ref[...]
        dy_ref[...] = e * (1.0 / D)
        rows = jnp.sum(e * e, axis=-1, keepdims=True) * (1.0 / D)
        l_ref[...] += 0.5 * jnp.sum(rows, axis=0, keepdims=True)

    row = pl.BlockSpec((TQ, D), lambda i: (i, 0))
    return pl.pallas_call(
        body, grid=(t // TQ,), in_specs=[row, row], out_specs=[row, _const((8, LANES))],
        out_shape=[jax.ShapeDtypeStruct((t, D), F32), jax.ShapeDtypeStruct((8, LANES), F32)],
        compiler_params=_cp(("arbitrary",)), name="loss_head")(y, target)


def bwd_ffn(dx, f, xmid, gu, g_pre, g_post, wfi_all, wfo_all, after=()):
    t = dx.shape[0]
    hw = DFF // 2

    def body(dx_ref, f_ref, x_ref, gu_ref, gpre_ref, gpost_ref, wfi_hbm, wfo_hbm,
             dxm_ref, df_ref, act_ref, dgu_ref, h_ref, dgpost_ref, dgpre_ref, wfi_v, wfo_v, sems):
        @pl.when(pl.program_id(0) == 0)
        def _():
            dgpost_ref[...] = jnp.zeros_like(dgpost_ref)
            dgpre_ref[...] = jnp.zeros_like(dgpre_ref)

        def step(ready):
            dxo = dx_ref[...]
            df, dgp = _rms_bwd(dxo, f_ref[...], gpost_ref[...])
            dgpost_ref[...] += dgp
            dfb = df.astype(BF16)
            df_ref[...] = dfb
            dh = jnp.zeros((TM, D), F32)
            for ci, (a, b) in enumerate(FF_CHUNKS):
                ready(2, ci)
                dact = lax.dot_general(dfb, wfo_v[a:b, :], NT, preferred_element_type=F32)
                gate = gu_ref[:, a:b].astype(F32)
                up = gu_ref[:, DFF + a:DFF + b].astype(F32)
                sig = 1.0 / (1.0 + jnp.exp(-gate))
                silu = gate * sig
                act_ref[:, a:b] = (silu * up).astype(BF16)
                dup = (dact * silu).astype(BF16)
                dgate = (dact * up * (sig * (1.0 + gate * (1.0 - sig)))).astype(BF16)
                dgu_ref[:, a:b] = dgate
                dgu_ref[:, DFF + a:DFF + b] = dup
                ready(0, ci)
                dh = dh + lax.dot_general(dgate, wfi_v[0, :, a:b], NT, preferred_element_type=F32)
                ready(1, ci)
                dh = dh + lax.dot_general(dup, wfi_v[1, :, a:b], NT, preferred_element_type=F32)
            xv = x_ref[...]
            gpre = gpre_ref[...]
            h_ref[...] = _rms(xv, gpre).astype(BF16)
            dxv, dgq = _rms_bwd(dh, xv, gpre)
            dgpre_ref[...] += dgq
            dxm_ref[...] = dxo + dxv

        _stream_ffn_weights(wfi_hbm, wfo_hbm, wfi_v, wfo_v, sems, (2, 0, 1), step)

    row = lambda w: pl.BlockSpec((TM, w), lambda i: (i, 0))
    return pl.pallas_call(
        _behind(body, 8, after), grid=(t // TM,),
        in_specs=[row(D), row(D), row(D), row(2 * DFF), _const((1, D)), _const((1, D)), _any(), _any()]
        + [_any()] * len(after),
        out_specs=[row(D), row(D), row(DFF), row(2 * DFF), row(D), _const((1, D)), _const((1, D))],
        out_shape=[jax.ShapeDtypeStruct((t, D), F32), jax.ShapeDtypeStruct((t, D), BF16),
                   jax.ShapeDtypeStruct((t, DFF), BF16), jax.ShapeDtypeStruct((t, 2 * DFF), BF16),
                   jax.ShapeDtypeStruct((t, D), BF16), jax.ShapeDtypeStruct((1, D), F32),
                   jax.ShapeDtypeStruct((1, D), F32)],
        scratch_shapes=[pltpu.VMEM((2, D, DFF), BF16), pltpu.VMEM((DFF, D), BF16), pltpu.SemaphoreType.DMA((6,))],
        compiler_params=_cp(("arbitrary",)), name="bwd_ffn")(dx, f, xmid, gu, g_pre, g_post, wfi_all, wfo_all, *after)


def bwd_mix(dxm, z, o, proj, wconv_t, g_co, g_ao, g_pm, gm, wout_all, after=()):
    t = dxm.shape[0]

    def body(dx_ref, z_ref, o_ref, pc_ref, pcp_ref, wc_ref, gco_ref, gao_ref, gpm_ref, gm_ref, wout_hbm,
             dz_ref, do_ref, dco_ref, dbg_ref, dgpm_ref, dgco_ref, dgao_ref, wout_v, cscr):
        first = pl.program_id(0) == 0

        @pl.when(first)
        def _():
            pltpu.sync_copy(wout_hbm, wout_v)
            dgpm_ref[...] = jnp.zeros_like(dgpm_ref)
            dgco_ref[...] = jnp.zeros_like(dgco_ref)
            dgao_ref[...] = jnp.zeros_like(dgao_ref)

        dz, dgp = _rms_bwd(dx_ref[...], z_ref[...], gpm_ref[...])
        dgpm_ref[...] += dgp
        dzb = dz.astype(BF16)
        dz_ref[...] = dzb
        gmv = gm_ref[...]
        _, bg, _, _, _, _, cout = _conv_fwd(pc_ref, pcp_ref, wc_ref, cscr, first)
        dy_conv = lax.dot_general(dzb, wout_v[0:CW, :], NT, preferred_element_type=F32)
        dyc, dgc = _group_rms_bwd(dy_conv, bg * cout, gco_ref[...], gmv)
        dgco_ref[...] += dgc
        dbg_ref[...] = (dyc * cout).astype(BF16)
        dco_ref[...] = dyc * bg
        dy_attn = lax.dot_general(dzb, wout_v[CW:2 * CW, :], NT, preferred_element_type=F32)
        do, dga = _group_rms_bwd(dy_attn, o_ref[...], gao_ref[...], gmv)
        dgao_ref[...] += dga
        do_ref[...] = do.astype(BF16)

    row = lambda w: pl.BlockSpec((TQ, w), lambda i: (i, 0))
    return pl.pallas_call(
        _behind(body, 11, after), grid=(t // TQ,),
        in_specs=[row(D), row(D), row(CW)] + _conv_specs() + [
            _const((8, CW)), _const((1, CW)), _const((1, CW)), _const((1, D)), _const((CW, CW)), _any()]
        + [_any()] * len(after),
        out_specs=[row(D), row(CW), row(CW), row(CW), _const((1, D)), _const((1, CW)), _const((1, CW))],
        out_shape=[jax.ShapeDtypeStruct((t, D), BF16), jax.ShapeDtypeStruct((t, CW), BF16),
                   jax.ShapeDtypeStruct((t, CW), F32), jax.ShapeDtypeStruct((t, CW), BF16),
                   jax.ShapeDtypeStruct((1, D), F32), jax.ShapeDtypeStruct((1, CW), F32),
                   jax.ShapeDtypeStruct((1, CW), F32)],
        scratch_shapes=[pltpu.VMEM((D, D), BF16), pltpu.VMEM((TQ + 16, CW), F32)],
        compiler_params=_cp(("arbitrary",)), name="bwd_mix",
    )(dxm, z, o, proj, proj, wconv_t, g_co, g_ao, g_pm, gm, wout_all, *after)


def bwd_conv(dco, proj, wconv_t, after=()):
    t = dco.shape[0]
    nt = t // TQ

    def body(d_ref, dn_ref, pc_ref, pcp_ref, wc_ref, dhc_ref, dcg_ref, dw_ref, cscr, dscr):
        i = pl.program_id(0)
        first = i == 0

        @pl.when(first)
        def _():
            dw_ref[...] = jnp.zeros_like(dw_ref)

        hc, _, cg, u, u1, u2, _ = _conv_fwd(pc_ref, pcp_ref, wc_ref, cscr, first)
        d0 = d_ref[...]
        dscr[0:TQ, :] = d0
        dscr[TQ:TQ + 8, :] = jnp.where(i == nt - 1, 0.0, dn_ref[...])
        d1 = dscr[1:TQ + 1, :]
        d2 = dscr[2:TQ + 2, :]
        du = wc_ref[2:3, :] * d0 + wc_ref[1:2, :] * d1 + wc_ref[0:1, :] * d2
        dhc_ref[...] = (du * cg).astype(BF16)
        dcg_ref[...] = (du * hc).astype(BF16)
        dw_ref[0:1, :] += jnp.sum(d0 * u2, axis=0, keepdims=True)
        dw_ref[1:2, :] += jnp.sum(d0 * u1, axis=0, keepdims=True)
        dw_ref[2:3, :] += jnp.sum(d0 * u, axis=0, keepdims=True)

    row = lambda w: pl.BlockSpec((TQ, w), lambda i: (i, 0))
    nxt = pl.BlockSpec((8, CW), lambda i: (jnp.minimum((i + 1) * (TQ // 8), t // 8 - 1), 0))
    return pl.pallas_call(
        _behind(body, 5, after), grid=(nt,),
        in_specs=[row(CW), nxt] + _conv_specs() + [_const((8, CW))] + [_any()] * len(after),
        out_specs=[row(CW), row(CW), _const((8, CW))],
        out_shape=[jax.ShapeDtypeStruct((t, CW), BF16), jax.ShapeDtypeStruct((t, CW), BF16),
                   jax.ShapeDtypeStruct((8, CW), F32)],
        scratch_shapes=[pltpu.VMEM((TQ + 16, CW), F32), pltpu.VMEM((TQ + 8, CW), F32)],
        compiler_params=_cp(("arbitrary",)), name="bwd_conv")(dco, dco, proj, proj, wconv_t, *after)


def bwd_attn(proj, o, do, lse, bias2):
    t = o.shape[0]
    nt = t // TQ
    qg, kg = QG_BWD, QG_BWD + LEFT
    nkb = (t + TQ) // LANES

    def body(q_ref, kp_ref, kc_ref, vp_ref, vc_ref, o_ref, do_ref, lse_ref, b2_ref,
             dq_ref, dk_hbm, dv_hbm, db_hbm, kwin, vwin, dk_acc, dv_acc, db_acc):
        i = pl.program_id(0)
        first = i == 0

        @pl.when(first)
        def _():
            dk_acc[...] = jnp.zeros_like(dk_acc)
            dv_acc[...] = jnp.zeros_like(dv_acc)
            db_acc[...] = jnp.zeros_like(db_acc)

        kwin[0:TQ, :] = kp_ref[...]
        kwin[TQ:2 * TQ, :] = kc_ref[...]
        vwin[0:TQ, :] = vp_ref[...]
        vwin[TQ:2 * TQ, :] = vc_ref[...]
        scale = HD ** -0.5
        qmask = _head_masks(scale)
        vmask = _head_masks(1.0)
        low = lax.broadcasted_iota(jnp.int32, (1, LANES), 1) < HD

        def group(g, carry):
            r0 = pl.multiple_of(g * qg, qg)
            base = i * (TQ // LANES) + g * (qg // LANES)
            pen = _key_penalty(first, r0, kg)
            for hp in range(NH // 2):
                ls = slice(LANES * hp, LANES * (hp + 1))
                qb = q_ref[pl.ds(r0, qg), ls]
                kw = kwin[pl.ds(r0, kg), ls]
                dob = do_ref[pl.ds(r0, qg), ls]
                prod = dob.astype(F32) * o_ref[pl.ds(r0, qg), ls]
                lseb = lse_ref[pl.ds(r0, qg), ls]
                q2 = jnp.concatenate([qb * qmask[0], qb * qmask[1]], axis=0)
                do2 = jnp.concatenate([dob * vmask[0], dob * vmask[1]], axis=0)
                lse2 = jnp.concatenate([lseb[:, 0:1], lseb[:, HD:HD + 1]], axis=0)
                dsum = jnp.concatenate([jnp.sum(jnp.where(low, prod, 0.0), axis=-1, keepdims=True),
                                        jnp.sum(jnp.where(low, 0.0, prod), axis=-1, keepdims=True)], axis=0)
                s = lax.dot_general(q2, kw, NT, preferred_element_type=F32) + b2_ref[hp] + pen
                p = jnp.exp(s - lse2)
                dp = lax.dot_general(do2, vwin[pl.ds(r0, kg), ls], NT, preferred_element_type=F32)
                ds = p * (dp - dsum)
                db_acc[hp] += ds
                dsb = ds.astype(BF16)
                dq2 = jnp.dot(dsb, kw, preferred_element_type=F32)
                dq_ref[pl.ds(r0, qg), ls] = (jnp.where(low, dq2[:qg], dq2[qg:]) * scale).astype(BF16)
                dkt = lax.dot_general(q2, dsb, TN, preferred_element_type=F32)
                dvt = lax.dot_general(do2, p.astype(BF16), TN, preferred_element_type=F32)
                for kb in range(kg // LANES):
                    dk_acc[base + kb, ls, :] += dkt[:, LANES * kb:LANES * (kb + 1)]
                    dv_acc[base + kb, ls, :] += dvt[:, LANES * kb:LANES * (kb + 1)]
            return carry

        lax.fori_loop(0, TQ // qg, group, 0)

        @pl.when(i == nt - 1)
        def _():
            pltpu.sync_copy(dk_acc, dk_hbm)
            pltpu.sync_copy(dv_acc, dv_hbm)
            pltpu.sync_copy(db_acc, db_hbm)

    row = lambda w: pl.BlockSpec((TQ, w), lambda i: (i, 0))
    return pl.pallas_call(
        body, grid=(nt,),
        in_specs=_attn_window_specs() + [row(CW), row(CW), row(CW), _const((NH // 2, 2 * qg, kg))],
        out_specs=[row(CW), _any(), _any(), _any()],
        out_shape=[jax.ShapeDtypeStruct((t, CW), BF16), jax.ShapeDtypeStruct((nkb, CW, LANES), F32),
                   jax.ShapeDtypeStruct((nkb, CW, LANES), F32), jax.ShapeDtypeStruct((NH // 2, 2 * qg, kg), F32)],
        scratch_shapes=[pltpu.VMEM((2 * TQ, CW), BF16), pltpu.VMEM((2 * TQ, CW), BF16),
                        pltpu.VMEM((nkb, CW, LANES), F32), pltpu.VMEM((nkb, CW, LANES), F32),
                        pltpu.VMEM((NH // 2, 2 * qg, kg), F32)],
        compiler_params=_cp(("arbitrary",)), name="bwd_attn",
    )(proj, proj, proj, proj, proj, o, do, lse, bias2)


def bwd_inproj(dxm, x, dhc, dbg, dcg, dq, dk, dv, g, w_all):
    t = x.shape[0]
    wc = PROJ // NCHIP

    def body(dxm_ref, x_ref, dhc_ref, dbg_ref, dcg_ref, dq_ref, dk_ref, dv_ref, g_ref, w_hbm,
             dx_ref, dp_ref, h_ref, dg_ref, w_v):
        @pl.when(pl.program_id(0) == 0)
        def _():
            pltpu.sync_copy(w_hbm, w_v)
            dg_ref[...] = jnp.zeros_like(dg_ref)

        dp_ref[:, 0:CW] = dhc_ref[...]
        dp_ref[:, CW:2 * CW] = dbg_ref[...]
        dp_ref[:, 2 * CW:3 * CW] = dcg_ref[...]
        dp_ref[:, 3 * CW:4 * CW] = dq_ref[...]
        for kb in range(TQ // LANES):
            rows = slice(LANES * kb, LANES * (kb + 1))
            dp_ref[rows, 4 * CW:5 * CW] = jnp.transpose(dk_ref[kb]).astype(BF16)
            dp_ref[rows, 5 * CW:6 * CW] = jnp.transpose(dv_ref[kb]).astype(BF16)
        dh = jnp.zeros((TQ, D), F32)
        for b in range(NCHIP):
            dh = dh + lax.dot_general(dp_ref[:, wc * b:wc * (b + 1)], w_v[b], NT, preferred_element_type=F32)
        xv = x_ref[...]
        gv = g_ref[...]
        h_ref[...] = _rms(xv, gv).astype(BF16)
        dxv, dgv = _rms_bwd(dh, xv, gv)
        dg_ref[...] += dgv
        dx_ref[...] = dxm_ref[...] + dxv

    row = lambda w: pl.BlockSpec((TQ, w), lambda i: (i, 0))
    pad = pl.BlockSpec((TQ // LANES, CW, LANES), lambda i: (i + 1, 0, 0))
    return pl.pallas_call(
        body, grid=(t // TQ,),
        in_specs=[row(D), row(D), row(CW), row(CW), row(CW), row(CW), pad, pad, _const((1, D)), _any()],
        out_specs=[row(D), row(PROJ), row(D), _const((1, D))],
        out_shape=[jax.ShapeDtypeStruct((t, D), F32), jax.ShapeDtypeStruct((t, PROJ), BF16),
                   jax.ShapeDtypeStruct((t, D), BF16), jax.ShapeDtypeStruct((1, D), F32)],
        scratch_shapes=[pltpu.VMEM((NCHIP, D, wc), BF16)],
        compiler_params=_cp(("arbitrary",)), name="bwd_inproj",
    )(dxm, x, dhc, dbg, dcg, dq, dk, dv, g, w_all)


def wgrad(a, b, kb, nb, by_columns, name):
    t, k = a.shape
    n = b.shape[1]
    tk = 512

    def body(a_ref, b_ref, o_ref):
        o_ref[...] = jnp.zeros_like(o_ref)
        for c in range(t // tk):
            o_ref[...] += lax.dot_general(a_ref[tk * c:tk * (c + 1), :], b_ref[tk * c:tk * (c + 1), :], TN,
                                          preferred_element_type=F32)

    if by_columns:
        assert nb == n // NCHIP
        out_spec = pl.BlockSpec((None, kb, nb), lambda ki, ni: (ni, ki, 0))
        out_shape = jax.ShapeDtypeStruct((NCHIP, k, nb), F32)
    else:
        assert nb == n
        out_spec = pl.BlockSpec((kb, nb), lambda ki, ni: (ki, 0))
        out_shape = jax.ShapeDtypeStruct((k, n), F32)
    return pl.pallas_call(
        body, grid=(k // kb, n // nb),
        in_specs=[pl.BlockSpec((t, kb), lambda ki, ni: (0, ki)), pl.BlockSpec((t, nb), lambda ki, ni: (0, ni))],
        out_specs=out_spec, out_shape=out_shape,
        compiler_params=_cp(("arbitrary", "arbitrary")), name=name)(a, b)


TOE = 1024
assert 2 * QG_FWD + LEFT <= TOE
N_FLAT = LEFT - REL_CLIP + 1
N_VAR = BAND - N_FLAT


def _diag_vector(table):
    last = table[:, 2 * REL_CLIP:]
    var = table[:, 2 * REL_CLIP - N_VAR:2 * REL_CLIP][:, ::-1]
    return jnp.concatenate([jnp.broadcast_to(last, (NH, N_FLAT)), var, jnp.broadcast_to(last, (NH, TOE - BAND))], axis=1)


def _diag_vector_bwd(dvec):
    dlast = jnp.sum(dvec[:, :N_FLAT], axis=1, keepdims=True) + jnp.sum(dvec[:, BAND:], axis=1, keepdims=True)
    dvar = dvec[:, N_FLAT:BAND][:, ::-1]
    return jnp.concatenate([jnp.zeros((NH, 2 * REL_CLIP - N_VAR), F32), dvar, dlast], axis=1)


def _band_valid(qg):
    r = lax.broadcasted_iota(jnp.int32, (qg, qg + LEFT), 0)
    p = lax.broadcasted_iota(jnp.int32, (qg, qg + LEFT), 1)
    start = lax.shift_left(lax.shift_right_logical(r, 6), 6)
    return (p >= start) & (p < start + BAND)


def bias_expand(vec, qgs, after=()):
    def body(v_ref, *o_refs):
        for qg, o_ref in zip(qgs, o_refs):
            valid = _band_valid(qg)
            for h in range(NH):
                rows = jnp.broadcast_to(v_ref[h:h + 1, :], (qg, TOE))
                toe = pltpu.roll(rows, 0, 1, stride=1, stride_axis=0)
                o_ref[h // 2, qg * (h % 2):qg * (h % 2 + 1), :] = jnp.where(valid, toe[:, :qg + LEFT], NEG_INF)

    vm = pl.BlockSpec(memory_space=pltpu.VMEM)
    return pl.pallas_call(_behind(body, 1, after), in_specs=[vm] + [_any()] * len(after), out_specs=[vm] * len(qgs),
                          out_shape=[jax.ShapeDtypeStruct((NH // 2, 2 * qg, qg + LEFT), F32) for qg in qgs],
                          name="bias_expand")(vec, *after)


def bias_reduce(db2):
    _, qg, kg = db2.shape

    def body(d_ref, o_ref):
        ii = lax.broadcasted_iota(jnp.int32, (kg, kg), 0)
        jj = lax.broadcasted_iota(jnp.int32, (kg, kg), 1)
        flip = jnp.where(ii + jj == kg - 1, 1.0, 0.0).astype(BF16)
        for h in range(NH):
            rest = d_ref[h]
            rev = jnp.zeros((qg, kg), F32)
            for _ in range(3):
                term = rest.astype(BF16)
                rev = rev + jnp.dot(term, flip, preferred_element_type=F32)
                rest = rest - term.astype(F32)
            d = jnp.concatenate([jnp.zeros((qg, TOE - kg), F32), rev], axis=1)
            back = pltpu.roll(d, 0, 1, stride=1, stride_axis=0)
            o_ref[h:h + 1, :] = jnp.sum(back, axis=0, keepdims=True)

    rev = pl.pallas_call(body, out_shape=jax.ShapeDtypeStruct((NH, TOE), F32), name="bias_reduce")(db2)
    return rev[:, ::-1]


def _place():
    x, y, c = lax.axis_index("x"), lax.axis_index("y"), lax.axis_index("c")
    chips = [(1 - x, y), (x, 1 - y), (1 - x, 1 - y)]
    return x, y, c, chips


def _half(ref_rows, c):
    return pl.ds(c * (ref_rows // 2), ref_rows // 2)


HBM_SPEC = pl.BlockSpec(memory_space=pltpu.HBM)
SEM_SPEC = pl.BlockSpec(memory_space=pltpu.SEMAPHORE)
IN_FLIGHT = pltpu.CompilerParams(has_side_effects=pltpu.SideEffectType.DATAFLOW_SIDE_EFFECTING)


def _in_hbm(a):
    return pltpu.with_memory_space_constraint(a, pltpu.HBM)


def cast_to_slot(ws, chip, layer, after=()):
    n = len(ws)
    steps = 4

    def body(b_ref, *refs):
        del b_ref
        for w_ref, o_ref in zip(refs[:n], refs[n + len(after):]):
            o_ref[...] = w_ref[...].astype(BF16)

    grid_spec = pltpu.PrefetchScalarGridSpec(
        num_scalar_prefetch=1, grid=(steps,),
        in_specs=[pl.BlockSpec((None, w.shape[1] // steps, w.shape[2]), lambda r, b: (layer, r, 0)) for w in ws]
        + [_any()] * len(after),
        out_specs=[pl.BlockSpec((None, w.shape[1] // steps, w.shape[2]), lambda r, b: (b[0], r, 0)) for w in ws])
    return pl.pallas_call(body, grid_spec=grid_spec,
                          out_shape=[jax.ShapeDtypeStruct((NCHIP,) + w.shape[1:], BF16) for w in ws],
                          compiler_params=_cp(("arbitrary",)), name="cast_to_slot")(chip, *ws, *after)


def _gather_copies(bufs, send, recv):
    x, y, c, chips = _place()
    b = 2 * x + y
    out = []
    for k, buf in enumerate(bufs):
        rows = buf.shape[1]
        mine = buf.at[b, _half(rows, c), :]
        for j, (cx, cy) in enumerate(chips):
            theirs = buf.at[2 * cx + cy, _half(rows, c), :]
            sems = dict(send_sem=send.at[3 * k + j], recv_sem=recv.at[3 * k + j],
                        device_id=(cx, cy, c), device_id_type=MESH)
            out.append((pltpu.make_async_remote_copy(src_ref=mine, dst_ref=mine, **sems),
                        pltpu.make_async_remote_copy(src_ref=theirs, dst_ref=theirs, **sems)))
    return out


def gather_start(bufs, after, layer):
    n = len(bufs)

    def body(*refs):
        ins = refs[:n]
        send, recv = refs[n + 1], refs[n + 2]
        token = refs[-1]
        for start, _ in _gather_copies(ins, send, recv):
            start.start()
        token[...] = jnp.zeros_like(token)

    sems = pltpu.SemaphoreType.DMA((3 * n,))
    res = pl.pallas_call(
        body, name=f"gather_start_{layer}",
        in_specs=[HBM_SPEC] * n + [_any()],
        out_specs=[SEM_SPEC, SEM_SPEC] + [HBM_SPEC] * n + [pl.BlockSpec(memory_space=pltpu.VMEM)],
        out_shape=[sems, sems] + [pltpu.HBM(b.shape, b.dtype) for b in bufs] + [jax.ShapeDtypeStruct((8, LANES), F32)],
        input_output_aliases={k: 2 + k for k in range(n)}, compiler_params=IN_FLIGHT,
    )(*[_in_hbm(b) for b in bufs], after)
    return res[0], res[1], res[2:2 + n], res[-1]


def gather_wait(send, recv, bufs, after, layer):
    n = len(bufs)

    def body(*refs):
        ins = refs[:n]
        send_ref, recv_ref = refs[n], refs[n + 1]
        for start, arrival in _gather_copies(ins, send_ref, recv_ref):
            start.wait_send()
            arrival.wait_recv()

    return pl.pallas_call(
        body, name=f"gather_wait_{layer}",
        in_specs=[HBM_SPEC] * n + [SEM_SPEC, SEM_SPEC, _any()], out_specs=[HBM_SPEC] * n,
        out_shape=[pltpu.HBM(b.shape, b.dtype) for b in bufs],
        input_output_aliases={k: k for k in range(n)}, compiler_params=IN_FLIGHT,
    )(*bufs, send, recv, after)


def gather_forward(bufs):
    n = len(bufs)

    def body(*refs):
        outs = refs[n:2 * n]
        send, recv = refs[2 * n:]
        x, y, c, chips = _place()
        cps = []
        for k in range(n):
            rows = outs[k].shape[1]
            for j, (cx, cy) in enumerate(chips):
                sems = dict(send_sem=send.at[3 * k + j], recv_sem=recv.at[3 * k + j],
                            device_id=(x, y, 1 - c), device_id_type=MESH)
                mine = outs[k].at[2 * cx + cy, _half(rows, c), :]
                theirs = outs[k].at[2 * cx + cy, _half(rows, 1 - c), :]
                cp = pltpu.make_async_remote_copy(src_ref=mine, dst_ref=mine, **sems)
                cp.start()
                cps.append((cp, pltpu.make_async_remote_copy(src_ref=theirs, dst_ref=theirs, **sems)))
        for cp, arrival in cps:
            cp.wait_send()
            arrival.wait_recv()

    return pl.pallas_call(
        body, in_specs=[_any()] * n, out_specs=[_any()] * n,
        out_shape=[jax.ShapeDtypeStruct(b.shape, b.dtype) for b in bufs], input_output_aliases={k: k for k in range(n)},
        scratch_shapes=[pltpu.SemaphoreType.DMA((3 * n,)), pltpu.SemaphoreType.DMA((3 * n,))],
        name="gather_forward")(*bufs)


def _forward_copies(bufs, send, recv):
    x, y, c, chips = _place()
    out = []
    for k, buf in enumerate(bufs):
        rows = buf.shape[1]
        for j, (cx, cy) in enumerate(chips):
            sems = dict(send_sem=send.at[3 * k + j], recv_sem=recv.at[3 * k + j],
                        device_id=(x, y, 1 - c), device_id_type=MESH)
            mine = buf.at[2 * cx + cy, _half(rows, c), :]
            theirs = buf.at[2 * cx + cy, _half(rows, 1 - c), :]
            out.append((pltpu.make_async_remote_copy(src_ref=mine, dst_ref=mine, **sems),
                        pltpu.make_async_remote_copy(src_ref=theirs, dst_ref=theirs, **sems)))
    return out


def forward_start(bufs, tag):
    n = len(bufs)

    def body(*refs):
        ins = refs[:n]
        send, recv = refs[n], refs[n + 1]
        token = refs[-1]
        for start, _ in _forward_copies(ins, send, recv):
            start.start()
        token[...] = jnp.zeros_like(token)

    sems = pltpu.SemaphoreType.DMA((3 * n,))
    res = pl.pallas_call(
        body, name=f"forward_start_{tag}", in_specs=[HBM_SPEC] * n,
        out_specs=[SEM_SPEC, SEM_SPEC] + [HBM_SPEC] * n + [pl.BlockSpec(memory_space=pltpu.VMEM)],
        out_shape=[sems, sems] + [pltpu.HBM(b.shape, b.dtype) for b in bufs] + [jax.ShapeDtypeStruct((8, LANES), F32)],
        input_output_aliases={k: 2 + k for k in range(n)}, compiler_params=IN_FLIGHT,
    )(*[_in_hbm(b) for b in bufs])
    return res[0], res[1], res[2:2 + n], res[-1]


def forward_wait(send, recv, bufs, after, tag):
    n = len(bufs)

    def body(*refs):
        ins = refs[:n]
        send_ref, recv_ref = refs[n], refs[n + 1]
        for start, arrival in _forward_copies(ins, send_ref, recv_ref):
            start.wait_send()
            arrival.wait_recv()

    return pl.pallas_call(
        body, name=f"forward_wait_{tag}",
        in_specs=[HBM_SPEC] * n + [SEM_SPEC, SEM_SPEC, _any()], out_specs=[HBM_SPEC] * n,
        out_shape=[pltpu.HBM(b.shape, b.dtype) for b in bufs],
        input_output_aliases={k: k for k in range(n)}, compiler_params=IN_FLIGHT,
    )(*bufs, send, recv, after)


def _exchange_copies(srcs, lands, send, recv):
    x, y, c, _ = _place()
    return [pltpu.make_async_remote_copy(
        src_ref=src.at[:, _half(src.shape[1], 1 - c), :], dst_ref=land, send_sem=send.at[k], recv_sem=recv.at[k],
        device_id=(x, y, 1 - c), device_id_type=MESH) for k, (src, land) in enumerate(zip(srcs, lands))]


def exchange_start(srcs, tag):
    n = len(srcs)
    lands = [lax.empty((s.shape[0], s.shape[1] // 2, s.shape[2]), s.dtype) for s in srcs]

    def body(*refs):
        ins, land_refs = refs[:n], refs[n:2 * n]
        send, recv = refs[2 * n], refs[2 * n + 1]
        token = refs[-1]
        for cp in _exchange_copies(ins, land_refs, send, recv):
            cp.start()
        token[...] = jnp.zeros_like(token)

    sems = pltpu.SemaphoreType.DMA((n,))
    res = pl.pallas_call(
        body, name=f"exchange_start_{tag}",
        in_specs=[HBM_SPEC] * (2 * n),
        out_specs=[SEM_SPEC, SEM_SPEC] + [HBM_SPEC] * (2 * n) + [pl.BlockSpec(memory_space=pltpu.VMEM)],
        out_shape=[sems, sems] + [pltpu.HBM(a.shape, a.dtype) for a in list(srcs) + lands]
        + [jax.ShapeDtypeStruct((8, LANES), F32)],
        input_output_aliases={k: 2 + k for k in range(2 * n)}, compiler_params=IN_FLIGHT,
    )(*[_in_hbm(a) for a in list(srcs) + lands])
    return res[0], res[1], res[2:2 + n], res[2 + n:2 + 2 * n], res[-1]


def exchange_wait(send, recv, srcs, lands, after, tag):
    n = len(srcs)

    def body(*refs):
        ins, land_refs = refs[:n], refs[n:2 * n]
        send_ref, recv_ref = refs[2 * n], refs[2 * n + 1]
        for cp in _exchange_copies(ins, land_refs, send_ref, recv_ref):
            cp.wait_send()
            cp.wait_recv()

    res = pl.pallas_call(
        body, name=f"exchange_wait_{tag}",
        in_specs=[HBM_SPEC] * (2 * n) + [SEM_SPEC, SEM_SPEC, _any()], out_specs=[HBM_SPEC] * (2 * n),
        out_shape=[pltpu.HBM(a.shape, a.dtype) for a in list(srcs) + list(lands)],
        input_output_aliases={k: k for k in range(2 * n)}, compiler_params=IN_FLIGHT,
    )(*srcs, *lands, send, recv, after)
    return res[:n], res[n:]


def add_pair(gs, r1s, core):
    n = len(gs)

    def body(c_ref, *refs):
        del c_ref
        for g_ref, r_ref, o_ref in zip(refs[:n], refs[n:2 * n], refs[2 * n:]):
            o_ref[...] = (g_ref[...] + r_ref[...]).astype(BF16)

    blk = lambda r: (None,) + r.shape[1:]
    grid_spec = pltpu.PrefetchScalarGridSpec(
        num_scalar_prefetch=1, grid=(NCHIP,),
        in_specs=[pl.BlockSpec(blk(r), lambda s, c: (s, c[0], 0)) for r in r1s]
        + [pl.BlockSpec(blk(r), lambda s, c: (s, 0, 0)) for r in r1s],
        out_specs=[pl.BlockSpec(blk(r), lambda s, c: (s, 0, 0)) for r in r1s])
    return pl.pallas_call(body, grid_spec=grid_spec, out_shape=[jax.ShapeDtypeStruct(r.shape, BF16) for r in r1s],
                          compiler_params=_cp(("arbitrary",)), name="add_pair")(core, *gs, *r1s)


def _scatter_copies(srcs, lands, send, recv):
    _, _, c, chips = _place()
    out = []
    for k, (src, land) in enumerate(zip(srcs, lands)):
        for j, (cx, cy) in enumerate(chips):
            out.append(pltpu.make_async_remote_copy(
                src_ref=src.at[2 * cx + cy], dst_ref=land.at[j], send_sem=send.at[3 * k + j],
                recv_sem=recv.at[3 * k + j], device_id=(cx, cy, c), device_id_type=MESH))
    return out


def scatter_start(srcs, layer):
    n = len(srcs)
    srcs = list(srcs)
    lands = [lax.empty((3,) + s.shape[1:], s.dtype) for s in srcs]

    def body(*refs):
        ins, land_refs = refs[:n], refs[n:2 * n]
        send, recv = refs[2 * n], refs[2 * n + 1]
        token = refs[-1]
        for cp in _scatter_copies(ins, land_refs, send, recv):
            cp.start()
        token[...] = jnp.zeros_like(token)

    sems = pltpu.SemaphoreType.DMA((3 * n,))
    res = pl.pallas_call(
        body, name=f"scatter_start_{layer}",
        in_specs=[HBM_SPEC] * (2 * n),
        out_specs=[SEM_SPEC, SEM_SPEC] + [HBM_SPEC] * (2 * n) + [pl.BlockSpec(memory_space=pltpu.VMEM)],
        out_shape=[sems, sems] + [pltpu.HBM(a.shape, a.dtype) for a in srcs + lands]
        + [jax.ShapeDtypeStruct((8, LANES), F32)],
        input_output_aliases={k: 2 + k for k in range(2 * n)}, compiler_params=IN_FLIGHT,
    )(*[_in_hbm(a) for a in srcs + lands])
    return res[0], res[1], res[2:2 + n], res[2 + n:2 + 2 * n], res[-1]


def scatter_wait(send, recv, srcs, lands, after, layer):
    n = len(srcs)

    def body(*refs):
        ins, land_refs = refs[:n], refs[n:2 * n]
        send_ref, recv_ref = refs[2 * n], refs[2 * n + 1]
        for cp in _scatter_copies(ins, land_refs, send_ref, recv_ref):
            cp.wait_send()
            cp.wait_recv()

    res = pl.pallas_call(
        body, name=f"scatter_wait_{layer}",
        in_specs=[HBM_SPEC] * (2 * n) + [SEM_SPEC, SEM_SPEC, _any()], out_specs=[HBM_SPEC] * (2 * n),
        out_shape=[pltpu.HBM(a.shape, a.dtype) for a in list(srcs) + list(lands)],
        input_output_aliases={k: k for k in range(2 * n)}, compiler_params=IN_FLIGHT,
    )(*srcs, *lands, send, recv, after)
    return res[n:]


def add_chips(gs, r1s, r2s, place, totals, layer):
    n = len(gs)
    steps = 2

    def body(p_ref, *refs):
        del p_ref
        for g_ref, r1_ref, r2_ref, o_ref in zip(refs[:n], refs[n:2 * n], refs[2 * n:3 * n], refs[4 * n:]):
            own = g_ref[...] + r1_ref[...]
            o_ref[...] = ((own + r2_ref[0].astype(F32)) + r2_ref[1].astype(F32)) + r2_ref[2].astype(F32)

    blk = lambda r: (None, r.shape[1] // steps, r.shape[2])
    grid_spec = pltpu.PrefetchScalarGridSpec(
        num_scalar_prefetch=1, grid=(steps,),
        in_specs=[pl.BlockSpec(blk(r), lambda i, p: (p[1], p[0] * steps + i, 0)) for r in r1s]
        + [pl.BlockSpec(blk(r), lambda i, p: (p[1], i, 0)) for r in r1s]
        + [pl.BlockSpec((3,) + blk(r)[1:], lambda i, p: (0, i, 0)) for r in r1s] + [_any()] * n,
        out_specs=[pl.BlockSpec(blk(r), lambda i, p: (layer, p[0] * steps + i, 0)) for r in r1s])
    return pl.pallas_call(body, grid_spec=grid_spec, out_shape=[jax.ShapeDtypeStruct(t.shape, F32) for t in totals],
                          input_output_aliases={1 + 3 * n + k: k for k in range(n)},
                          compiler_params=_cp(("arbitrary",)), name="add_chips")(place, *gs, *r1s, *r2s, *totals)


def _share_copies(bufs, send, recv):
    x, y, c, _ = _place()
    out = []
    for k, buf in enumerate(bufs):
        sems = dict(send_sem=send.at[k], recv_sem=recv.at[k], device_id=(x, y, 1 - c), device_id_type=MESH)
        mine = buf.at[:, _half(buf.shape[1], c), :]
        theirs = buf.at[:, _half(buf.shape[1], 1 - c), :]
        out.append((pltpu.make_async_remote_copy(src_ref=mine, dst_ref=mine, **sems),
                    pltpu.make_async_remote_copy(src_ref=theirs, dst_ref=theirs, **sems)))
    return out


def share_start(bufs, tag):
    n = len(bufs)

    def body(*refs):
        ins = refs[:n]
        send, recv = refs[n], refs[n + 1]
        token = refs[-1]
        for start, _ in _share_copies(ins, send, recv):
            start.start()
        token[...] = jnp.zeros_like(token)

    sems = pltpu.SemaphoreType.DMA((n,))
    res = pl.pallas_call(
        body, name=f"share_start_{tag}", in_specs=[HBM_SPEC] * n,
        out_specs=[SEM_SPEC, SEM_SPEC] + [HBM_SPEC] * n + [pl.BlockSpec(memory_space=pltpu.VMEM)],
        out_shape=[sems, sems] + [pltpu.HBM(b.shape, b.dtype) for b in bufs] + [jax.ShapeDtypeStruct((8, LANES), F32)],
        input_output_aliases={k: 2 + k for k in range(n)}, compiler_params=IN_FLIGHT,
    )(*[_in_hbm(b) for b in bufs])
    return res[0], res[1], res[2:2 + n], res[-1]


def share_wait(send, recv, bufs, after, tag):
    n = len(bufs)

    def body(*refs):
        ins = refs[:n]
        send_ref, recv_ref = refs[n], refs[n + 1]
        for start, arrival in _share_copies(ins, send_ref, recv_ref):
            start.wait_send()
            arrival.wait_recv()

    return pl.pallas_call(
        body, name=f"share_wait_{tag}",
        in_specs=[HBM_SPEC] * n + [SEM_SPEC, SEM_SPEC, _any()], out_specs=[HBM_SPEC] * n,
        out_shape=[pltpu.HBM(b.shape, b.dtype) for b in bufs],
        input_output_aliases={k: k for k in range(n)}, compiler_params=IN_FLIGHT,
    )(*bufs, send, recv, after)


def small_allreduce(v, after=()):
    rows = v.shape[0]
    flips = [(fx, fy, fc) for fx in (0, 1) for fy in (0, 1) for fc in (0, 1)][1:]

    def body(v_ref, o_ref, buf, send, recv):
        x, y, c, _ = _place()
        buf[4 * x + 2 * y + c] = v_ref[...]
        peers = [(jnp.where(fx, 1 - x, x), jnp.where(fy, 1 - y, y), jnp.where(fc, 1 - c, c)) for fx, fy, fc in flips]
        cps = []
        for k, peer in enumerate(peers):
            cp = pltpu.make_async_remote_copy(
                src_ref=v_ref, dst_ref=buf.at[4 * x + 2 * y + c], send_sem=send.at[k], recv_sem=recv.at[k],
                device_id=peer, device_id_type=MESH)
            cp.start()
            cps.append(cp)
        for k, (px, py, pc) in enumerate(peers):
            pltpu.make_async_remote_copy(
                src_ref=v_ref, dst_ref=buf.at[4 * px + 2 * py + pc], send_sem=send.at[k], recv_sem=recv.at[k],
                device_id=(px, py, pc), device_id_type=MESH).wait_recv()
        for cp in cps:
            cp.wait_send()
        acc = buf[0]
        for s in range(1, 8):
            acc = acc + buf[s]
        o_ref[...] = acc

    vm = pl.BlockSpec(memory_space=pltpu.VMEM)
    return pl.pallas_call(
        _behind(body, 1, after), in_specs=[vm] + [_any()] * len(after), out_specs=vm,
        out_shape=jax.ShapeDtypeStruct((rows, SMALL_COLS), F32),
        scratch_shapes=[pltpu.VMEM((8, rows, SMALL_COLS), F32), pltpu.SemaphoreType.DMA((7,)),
                        pltpu.SemaphoreType.DMA((7,))],
        name="reduce_small")(v, *after)


def adamw(w, g, m, v, rb, name, after=()):
    nl, rows, cols = w.shape

    def body(w_ref, g_ref, m_ref, v_ref, go_ref, d_ref, nm_ref, nv_ref):
        gv = g_ref[...]
        go_ref[...] = gv
        nm = ADAM_B1 * m_ref[...] + (1.0 - ADAM_B1) * gv
        nv = ADAM_B2 * v_ref[...] + (1.0 - ADAM_B2) * (gv * gv)
        m_hat = nm / (1.0 - ADAM_B1 ** ADAM_STEP)
        v_hat = nv / (1.0 - ADAM_B2 ** ADAM_STEP)
        d_ref[...] = -ADAM_LR * (m_hat / (jnp.sqrt(v_hat) + ADAM_EPS) + ADAM_WD * w_ref[...])
        nm_ref[...] = nm
        nv_ref[...] = nv

    blk = pl.BlockSpec((None, rb, cols), lambda l, r: (l, r, 0))
    shp = jax.ShapeDtypeStruct(w.shape, F32)
    return pl.pallas_call(_behind(body, 4, after), grid=(nl, rows // rb), in_specs=[blk] * 4 + [_any()] * len(after),
                          out_specs=[blk] * 4, out_shape=[shp] * 4,
                          compiler_params=_cp(("arbitrary", "arbitrary")), name=name)(w, g, m, v, *after)


def _pack(parts, rows):
    flat = jnp.concatenate([p.reshape(-1).astype(F32) for p in parts])
    return jnp.pad(flat, (0, rows * SMALL_COLS - flat.shape[0])).reshape(rows, SMALL_COLS)


def _unpack(vec, shapes):
    flat = vec.reshape(-1)
    out, off = [], 0
    for s in shapes:
        size = 1
        for d in s:
            size *= d
        out.append(flat[off:off + size].reshape(s))
        off += size
    return out


def kernel(x, w_in, w_conv, rel_bias, g_conv_out, g_attn_out, w_out, g_pre_mix, g_post_mix, g_pre_ffn, g_post_ffn, w_ffn_in, w_ffn_out, loss_target, m_w_in, m_w_conv, m_rel_bias, m_g_conv_out, m_g_attn_out, m_w_out, m_g_pre_mix, m_g_post_mix, m_g_pre_ffn, m_g_post_ffn, m_w_ffn_in, m_w_ffn_out, v_w_in, v_w_conv, v_rel_bias, v_g_conv_out, v_g_attn_out, v_w_out, v_g_pre_mix, v_g_post_mix, v_g_pre_ffn, v_g_post_ffn, v_w_ffn_in, v_w_ffn_out):
    xi, yi, ci = lax.axis_index("x"), lax.axis_index("y"), lax.axis_index("c")
    chip = 2 * xi + yi
    nl = w_in.shape[0]
    x0 = x[0]
    target = loss_target[0]
    cwl = CW // NCHIP

    chip1 = chip.reshape(1).astype(jnp.int32)
    big_weights = [w_in, w_out, w_ffn_in, w_ffn_out]
    own = [cast_to_slot(big_weights, chip1, 0)]
    wc_mine = jnp.pad(w_conv.reshape(-1), (0, 16 * LANES - w_conv.size)).reshape(1, 16, LANES)
    wc_slot = lax.dynamic_update_slice_in_dim(jnp.zeros((NCHIP, 16, LANES), F32), wc_mine, chip, axis=0)
    gm = jnp.kron(jnp.eye(CW // HD, dtype=F32), jnp.full((HD, HD), 1.0 / HD, F32)).astype(BF16)
    row = lambda a, l: a[l][None, :]

    def gather_finish(flight, after, tag):
        send, recv, bufs, _ = flight
        return gather_forward(gather_wait(send, recv, bufs, after, tag))

    first_mix = gather_start(list(own[0][:2]) + [wc_slot], x0, "0m")
    first_ffn = gather_start(own[0][2:], first_mix[3], "0f")
    chain = first_ffn[3]
    biases = []
    for l in range(nl):
        biases.append(bias_expand(_diag_vector(rel_bias[l]), (QG_FWD, QG_BWD), [chain]))
        chain = biases[l][1]
    for l in range(1, nl):
        own.append(cast_to_slot(big_weights, chip1, l, [chain]))
        chain = own[l][0]
    gw_in, gw_out, wc_all = gather_finish(first_mix, chain, "0m")
    wc_full = wc_all.reshape(NCHIP, -1)[:, :nl * cwl * 3].reshape(NCHIP, nl, cwl, 3)
    wc_full = jnp.transpose(wc_full, (1, 0, 2, 3)).reshape(nl, CW, 3)
    wconv_t = jnp.pad(jnp.transpose(wc_full, (0, 2, 1)), ((0, 0), (0, 5), (0, 0)))
    flights, to_sibling = {}, None
    saved, weights = [], []
    h = x0
    for l in range(nl):
        if l == 0:
            pass
        elif l == 1:
            flights[2] = gather_start(own[2], h, 2)
            gw_in, gw_out, gw_fi, gw_fo = gather_finish(flights[l], flights[2][3], l)
        else:
            gw_in, gw_out, gw_fi, gw_fo = forward_wait(*to_sibling[:3], h, l)
        gw_out = gw_out.reshape(D, D)
        behind_mix, behind_ffn = ([first_ffn[3]] if l == 0 else []), []
        if l + 1 < nl and l + 1 not in flights:
            flights[l + 1] = gather_start(own[l + 1], first_ffn[3] if l == 0 else gw_in, l + 1)
            behind_mix.append(flights[l + 1][3])
        bias2, bias2_bwd = biases[l]
        proj = fwd_inproj(h, row(g_pre_mix, l), gw_in, behind_mix)
        xmid, o, lse, y, z = fwd_mix(h, proj, bias2, wconv_t[l], row(g_conv_out, l), row(g_attn_out, l),
                                     row(g_post_mix, l), gm, gw_out)
        if l == 0:
            gw_fi, gw_fo = gather_finish(first_ffn, xmid, "0f")
        elif l + 1 < nl:
            send, recv, bufs, _ = flights[l + 1]
            landed = gather_wait(send, recv, bufs, xmid, l + 1)
            to_sibling = forward_start(landed, l + 1)
            behind_ffn.append(to_sibling[3])
            if l + 2 < nl:
                flights[l + 2] = gather_start(own[l + 2], to_sibling[3], l + 2)
                behind_ffn.append(flights[l + 2][3])
        gw_fo = gw_fo.reshape(2, DFF // 2, D)
        gu, f, xout = fwd_ffn(xmid, row(g_pre_ffn, l), row(g_post_ffn, l), gw_fi, gw_fo, behind_ffn)
        saved.append((h, proj, bias2_bwd, xmid, o, lse, y, z, gu, f))
        weights.append((gw_in, gw_out, gw_fi, gw_fo))
        h = xout
    dx, loss_blk = loss_head(h, target)

    core = ci.reshape(1).astype(jnp.int32)
    place = jnp.stack([ci, chip]).astype(jnp.int32)
    totals = [lax.empty(w.shape, F32) for w in (w_in, w_out, w_ffn_in, w_ffn_out)]
    small = {k: [None] * nl for k in ("co", "ao", "pm", "qm", "pf", "qf", "rel", "wc")}

    def reduce_begin(kinds, grads, tag):
        return kinds, exchange_start(grads, tag), tag

    def reduce_mid(state, after):
        kinds, (send, recv, srcs, lands, _), tag = state
        grads, from_sibling = exchange_wait(send, recv, srcs, lands, after, tag)
        return kinds, grads, from_sibling, scatter_start(add_pair(grads, from_sibling, core), tag), tag

    def reduce_end(state, after, totals, layer):
        kinds, grads, from_sibling, (send, recv, srcs, lands, _), tag = state
        from_chips = scatter_wait(send, recv, srcs, lands, after, tag)
        totals = list(totals)
        summed = add_chips(grads, from_sibling, from_chips, place, [totals[i] for i in kinds], layer)
        for i, t in zip(kinds, summed):
            totals[i] = t
        return totals

    begun = flying = None
    for l in reversed(range(nl)):
        hin, proj, bias2, xmid, o, lse, y, z, gu, f = saved[l]
        gw_in, gw_out, gw_fi, gw_fo = weights[l]
        behind_ffn = [begun[1][4]] if begun is not None else []
        dxm, dfb, act, dgu, h2, dg_qf, dg_pf = bwd_ffn(dx, f, xmid, gu, row(g_pre_ffn, l), row(g_post_ffn, l),
                                                        gw_fi, gw_fo, behind_ffn)
        behind_mix, behind_conv = [], []
        if begun is not None:
            flying = reduce_mid(begun, dxm)
            behind_mix.append(flying[3][4])
        gr_fo = wgrad(act, dfb, 256, D, False, "wgrad_ffn_out").reshape(NCHIP, DFF // NCHIP, D)
        gr_fi = wgrad(h2, dgu, 512, 2 * DFF // NCHIP, True, "wgrad_ffn_in")
        if l == 0:
            begun_ffn = reduce_begin([2, 3], [gr_fi, gr_fo], "0f")
            behind_mix.append(begun_ffn[1][4])
        dzb, do, dco, dbg, dg_qm, dg_co, dg_ao = bwd_mix(dxm, z, o, proj, wconv_t[l], row(g_conv_out, l),
                                                          row(g_attn_out, l), row(g_post_mix, l), gm, gw_out,
                                                          behind_mix)
        if l == 0:
            flying_ffn = reduce_mid(begun_ffn, dzb)
            behind_conv.append(flying_ffn[3][4])
        gr_out = wgrad(y, dzb, 512, D, False, "wgrad_out").reshape(NCHIP, D // NCHIP, D)
        dhc, dcg, dwc = bwd_conv(dco, proj, wconv_t[l], behind_conv)
        dq, dk, dv, db2 = bwd_attn(proj, o, do, lse, bias2)
        dx, dproj, hb, dg_pm = bwd_inproj(dxm, hin, dhc, dbg, dcg, dq, dk, dv, row(g_pre_mix, l), gw_in)
        if flying is not None:
            totals = reduce_end(flying, dx, totals, l + 1)
        gr_in = wgrad(hb, dproj, 512, PROJ // NCHIP, True, "wgrad_in")
        small["co"][l], small["ao"][l], small["pm"][l], small["qm"][l] = dg_co, dg_ao, dg_pm, dg_qm
        small["pf"][l], small["qf"][l] = dg_pf, dg_qf
        small["rel"][l] = _diag_vector_bwd(bias_reduce(db2.reshape(NH, QG_BWD, QG_BWD + LEFT)))
        small["wc"][l] = jnp.transpose(dwc[0:3], (1, 0))
        if l > 0:
            begun = reduce_begin([0, 1, 2, 3], [gr_in, gr_out, gr_fi, gr_fo], l)
    begun_mix = reduce_begin([0, 1], [gr_in, gr_out], "0m")
    totals = reduce_end(flying_ffn, begun_mix[1][4], totals, 0)
    flying_mix = reduce_mid(begun_mix, totals[2])
    share_ffn = share_start(totals[2:], "ffn")

    order = ("co", "ao", "pm", "qm", "pf", "qf", "rel", "wc")
    parts = [jnp.stack(small[k]) for k in order] + [loss_blk[0:1, 0:1]]
    shapes = [p.shape for p in parts]
    red_vec = small_allreduce(_pack(parts, 40), [share_ffn[3]])
    red = _unpack(red_vec, shapes)

    gr_fi, gr_fo = share_wait(*share_ffn[:3], red_vec, "ffn")
    big_fi = adamw(w_ffn_in, gr_fi, m_w_ffn_in, v_w_ffn_in, w_ffn_in.shape[1] // 4, "adamw_ffn_in")
    totals = reduce_end(flying_mix, big_fi[1], totals, 0)
    share_mix = share_start(totals[:2], "mix")
    big_fo = adamw(w_ffn_out, gr_fo, m_w_ffn_out, v_w_ffn_out, w_ffn_out.shape[1] // 4, "adamw_ffn_out",
                   [share_mix[3]])
    gr_in, gr_out = share_wait(*share_mix[:3], big_fo[1], "mix")
    big_in = adamw(w_in, gr_in, m_w_in, v_w_in, w_in.shape[1] // 4, "adamw_in")
    big_out = adamw(w_out, gr_out, m_w_out, v_w_out, w_out.shape[1] // 4, "adamw_out")
    big = [big_in, big_out, big_fi, big_fo]
    gr_co, gr_ao, gr_pm, gr_qm, gr_pf, gr_qf, gr_rel, gr_wc_full, loss = red
    gr_co, gr_ao, gr_pm, gr_qm, gr_pf, gr_qf = [a.reshape(nl, -1) for a in (gr_co, gr_ao, gr_pm, gr_qm, gr_pf, gr_qf)]
    gr_wc = lax.dynamic_slice_in_dim(gr_wc_full, chip * cwl, cwl, axis=1)
    loss = loss.reshape(())

    sw = [g_conv_out, g_attn_out, g_pre_mix, g_post_mix, g_pre_ffn, g_post_ffn, rel_bias, w_conv]
    sg = [gr_co, gr_ao, gr_pm, gr_qm, gr_pf, gr_qf, gr_rel, gr_wc]
    sm = [m_g_conv_out, m_g_attn_out, m_g_pre_mix, m_g_post_mix, m_g_pre_ffn, m_g_post_ffn, m_rel_bias, m_w_conv]
    sv = [v_g_conv_out, v_g_attn_out, v_g_pre_mix, v_g_post_mix, v_g_pre_ffn, v_g_post_ffn, v_rel_bias, v_w_conv]
    sshapes = [a.shape for a in sw]
    packed = [_pack(a, 32)[None] for a in (sw, sg, sm, sv)]
    s_out = [_unpack(a[0], sshapes) for a in adamw(*packed, 32, "adamw_small")]

    def leaves(big_i, small_i):
        b_in, b_out, b_fi, b_fo = big_i
        s_co, s_ao, s_pm, s_qm, s_pf, s_qf, s_rel, s_wc = small_i
        return [b_in, s_wc, s_rel, s_co, s_ao, b_out, s_pm, s_qm, s_pf, s_qf, b_fi, b_fo]

    out = [loss, dx[None]]
    out += leaves([b[0] for b in big], sg)
    for i in range(1, 4):
        out += leaves([b[i] for b in big], s_out[i])
    return tuple(out)
```

```python
import jax
import jax.numpy as jnp
from jax import lax
from jax.experimental import pallas as pl
from jax.experimental.pallas import tpu as pltpu

F32 = jnp.float32
BF16 = jnp.bfloat16

D = 1024
PROJ = 3072
CW = 512
HD = 64
NH = 8
CHUNK = 64
BAND = 576
REL_CLIP = 128
NREL = 2 * REL_CLIP + 1
DFF = 2816
DEPTH = 4
NCHIP = 4
EPS = 1e-6
NEG_INF = -1e30

ADAM_LR = 0.001
ADAM_B1 = 0.9
ADAM_B2 = 0.999
ADAM_EPS = 1e-08
ADAM_WD = 0.01
ADAM_STEP = 10

V7X_VMEM_BYTES = 64 * 1024 * 1024
VMEM_LIMIT = V7X_VMEM_BYTES - 8 * 1024 * 1024
LANES = 128
QG_FWD = 4 * CHUNK
QG_BWD = 2 * CHUNK
LEFT = BAND - CHUNK
TQ = 512
TM = 256
SMALL_COLS = 1024
MESH = pl.DeviceIdType.MESH
NT = (((1,), (1,)), ((), ()))
TN = (((0,), (0,)), ((), ()))


def _cp(sem=None, vmem=VMEM_LIMIT):
    return pltpu.CompilerParams(dimension_semantics=sem, vmem_limit_bytes=vmem)


def _any():
    return pl.BlockSpec(memory_space=pl.ANY)


def _const(shape):
    nd = len(shape)
    return pl.BlockSpec(shape, lambda *_: (0,) * nd)


def _behind(body, n_in, after):
    def ordered(*refs):
        return body(*refs[:n_in], *refs[n_in + len(after):])
    return ordered


def _rms(v, g):
    r = lax.rsqrt(jnp.mean(v * v, axis=-1, keepdims=True) + EPS)
    return v * r * g


def _rms_bwd(dy, v, g):
    r = lax.rsqrt(jnp.mean(v * v, axis=-1, keepdims=True) + EPS)
    vh = v * r
    dg = jnp.sum(dy * vh, axis=0, keepdims=True)
    dvh = dy * g
    dv = r * (dvh - vh * jnp.mean(dvh * vh, axis=-1, keepdims=True))
    return dv, dg


def _group_mean(v, gm):
    return jnp.dot(v.astype(BF16), gm, preferred_element_type=F32)


def _group_rms_bwd(dy, v, g, gm):
    r = lax.rsqrt(_group_mean(v * v, gm) + EPS)
    vh = v * r
    dg = jnp.sum(dy * vh, axis=0, keepdims=True)
    dvh = dy * g
    dv = r * (dvh - vh * _group_mean(dvh * vh, gm))
    return dv, dg


def _head_masks(scale):
    lane = lax.broadcasted_iota(jnp.int32, (1, LANES), 1)
    return [jnp.where((lane >= HD * a) & (lane < HD * (a + 1)), scale, 0.0).astype(BF16) for a in range(2)]


class _Resident:
    def __init__(self, src, dst, sem):
        self.first = pl.program_id(0) == 0
        self.copy = pltpu.make_async_copy(src, dst, sem)
        self.dst = dst

        @pl.when(self.first)
        def _():
            self.copy.start()

    def read(self):
        @pl.when(self.first)
        def _():
            self.copy.wait()

        return self.dst[...]


FF_CHUNKS = ((0, 1536), (1536, DFF))


def _stream_ffn_weights(wfi_hbm, wfo_hbm, wfi_v, wfo_v, sems, order, step):
    hw = DFF // 2
    per_matrix = {
        0: [(wfi_hbm.at[j], wfi_v.at[0, :, pl.ds(hw * j, hw)]) for j in range(2)],
        1: [(wfi_hbm.at[2 + j], wfi_v.at[1, :, pl.ds(hw * j, hw)]) for j in range(2)],
        2: [(wfo_hbm.at[j], wfo_v.at[pl.ds(hw * j, hw), :]) for j in range(2)],
    }
    pieces = [p for m in order for p in per_matrix[m]]
    slot = {m: 2 * k for k, m in enumerate(order)}

    def make_step(wait):
        def ready(m, chunk):
            if chunk == 0:
                wait(slot[m])
                wait(slot[m] + 1)
        return lambda: step(ready)

    copies = [pltpu.make_async_copy(src, dst, sems.at[k]) for k, (src, dst) in enumerate(pieces)]
    first = pl.program_id(0) == 0

    @pl.when(first)
    def _():
        for cp in copies:
            cp.start()
        make_step(lambda k: copies[k].wait())()

    @pl.when(jnp.logical_not(first))
    def _():
        make_step(lambda k: None)()


def _conv_taps(u_prev, u, scr):
    n = u.shape[0]
    scr[0:16, :] = u_prev
    scr[16:16 + n, :] = u
    return scr[15:15 + n, :], scr[14:14 + n, :]


def fwd_inproj(x, g, w_all, after=()):
    t = x.shape[0]
    wc = PROJ // NCHIP

    def body(x_ref, g_ref, w_hbm, o_ref, w_v):
        @pl.when(pl.program_id(0) == 0)
        def _():
            pltpu.sync_copy(w_hbm, w_v)

        h = _rms(x_ref[...], g_ref[...]).astype(BF16)
        for b in range(NCHIP):
            o_ref[:, wc * b:wc * (b + 1)] = jnp.dot(h, w_v[b], preferred_element_type=F32).astype(BF16)

    return pl.pallas_call(
        _behind(body, 3, after), grid=(t // TQ,),
        in_specs=[pl.BlockSpec((TQ, D), lambda i: (i, 0)), _const((1, D)), _any()] + [_any()] * len(after),
        out_specs=pl.BlockSpec((TQ, PROJ), lambda i: (i, 0)),
        out_shape=jax.ShapeDtypeStruct((t, PROJ), BF16),
        scratch_shapes=[pltpu.VMEM((NCHIP, D, wc), BF16)],
        compiler_params=_cp(("arbitrary",)), name="fwd_inproj")(x, g, w_all, *after)


def _attn_window_specs():
    return [
        pl.BlockSpec((TQ, CW), lambda i: (i, 3)),
        pl.BlockSpec((TQ, CW), lambda i: (jnp.maximum(i - 1, 0), 4)),
        pl.BlockSpec((TQ, CW), lambda i: (i, 4)),
        pl.BlockSpec((TQ, CW), lambda i: (jnp.maximum(i - 1, 0), 5)),
        pl.BlockSpec((TQ, CW), lambda i: (i, 5)),
    ]


def _conv_specs():
    return [
        pl.BlockSpec((TQ, 3 * CW), lambda i: (i, 0)),
        pl.BlockSpec((16, 3 * CW), lambda i: (jnp.maximum(i * (TQ // 16) - 1, 0), 0)),
    ]


def _conv_fwd(pc_ref, pcp_ref, wc_ref, scr, first):
    pc = pc_ref[...].astype(F32)
    hc, bg, cg = pc[:, :CW], pc[:, CW:2 * CW], pc[:, 2 * CW:]
    u = cg * hc
    pp = pcp_ref[...].astype(F32)
    u_prev = jnp.where(first, 0.0, pp[:, 2 * CW:] * pp[:, :CW])
    u1, u2 = _conv_taps(u_prev, u, scr)
    cout = wc_ref[0:1, :] * u2 + wc_ref[1:2, :] * u1 + wc_ref[2:3, :] * u
    return hc, bg, cg, u, u1, u2, cout


def _key_penalty(first, r0, kg):
    col = lax.broadcasted_iota(jnp.int32, (1, kg), 1)
    limit = jnp.where(first, TQ - r0, 0)
    return jnp.where(col < limit, NEG_INF, 0.0)


def fwd_mix(x, proj, bias2, wconv_t, g_co, g_ao, g_pm, gm, wout_all):
    t = x.shape[0]
    qg, kg = QG_FWD, QG_FWD + LEFT

    def body(x_ref, pc_ref, pcp_ref, q_ref, kp_ref, kc_ref, vp_ref, vc_ref, b2_ref, wc_ref, gco_ref, gao_ref, gpm_ref,
             gm_ref, wout_hbm, xmid_ref, o_ref, lse_ref, y_ref, z_ref, wout_v, kwin, vwin, cscr, sems):
        i = pl.program_id(0)
        first = i == 0
        wout = _Resident(wout_hbm, wout_v, sems.at[0])
        kwin[0:TQ, :] = kp_ref[...]
        kwin[TQ:2 * TQ, :] = kc_ref[...]
        vwin[0:TQ, :] = vp_ref[...]
        vwin[TQ:2 * TQ, :] = vc_ref[...]
        qmask = _head_masks(HD ** -0.5)
        low = lax.broadcasted_iota(jnp.int32, (1, LANES), 1) < HD

        def group(g, carry):
            r0 = pl.multiple_of(g * qg, qg)
            pen = _key_penalty(first, r0, kg)
            for hp in range(NH // 2):
                ls = slice(LANES * hp, LANES * (hp + 1))
                qb = q_ref[pl.ds(r0, qg), ls]
                q2 = jnp.concatenate([qb * qmask[0], qb * qmask[1]], axis=0)
                s = lax.dot_general(q2, kwin[pl.ds(r0, kg), ls], NT, preferred_element_type=F32)
                s = s + b2_ref[hp] + pen
                m = jnp.max(s, axis=-1, keepdims=True)
                p = jnp.exp(s - m)
                l = jnp.sum(p, axis=-1, keepdims=True)
                o2 = jnp.dot(p.astype(BF16), vwin[pl.ds(r0, kg), ls], preferred_element_type=F32) * (1.0 / l)
                lse2 = m + jnp.log(l)
                o_ref[pl.ds(r0, qg), ls] = jnp.where(low, o2[:qg], o2[qg:])
                lse_ref[pl.ds(r0, qg), ls] = jnp.where(low, lse2[:qg], lse2[qg:])
            return carry

        lax.fori_loop(0, TQ // qg, group, 0)

        _, bg, _, _, _, _, cout = _conv_fwd(pc_ref, pcp_ref, wc_ref, cscr, first)
        yc = bg * cout
        gmv = gm_ref[...]
        ycn = yc * lax.rsqrt(_group_mean(yc * yc, gmv) + EPS) * gco_ref[...]
        oa = o_ref[...]
        oan = oa * lax.rsqrt(_group_mean(oa * oa, gmv) + EPS) * gao_ref[...]
        y_ref[:, 0:CW] = ycn.astype(BF16)
        y_ref[:, CW:2 * CW] = oan.astype(BF16)
        z = jnp.dot(y_ref[...], wout.read(), preferred_element_type=F32)
        z_ref[...] = z
        xmid_ref[...] = x_ref[...] + _rms(z, gpm_ref[...])

    row = lambda w: pl.BlockSpec((TQ, w), lambda i: (i, 0))
    return pl.pallas_call(
        body, grid=(t // TQ,),
        in_specs=[row(D)] + _conv_specs() + _attn_window_specs() + [
            _const((NH // 2, 2 * qg, kg)), _const((8, CW)), _const((1, CW)), _const((1, CW)), _const((1, D)),
            _const((CW, CW)), _any()],
        out_specs=[row(D), row(CW), row(CW), row(D), row(D)],
        out_shape=[jax.ShapeDtypeStruct((t, D), F32), jax.ShapeDtypeStruct((t, CW), F32),
                   jax.ShapeDtypeStruct((t, CW), F32), jax.ShapeDtypeStruct((t, D), BF16),
                   jax.ShapeDtypeStruct((t, D), F32)],
        scratch_shapes=[pltpu.VMEM((D, D), BF16), pltpu.VMEM((2 * TQ, CW), BF16), pltpu.VMEM((2 * TQ, CW), BF16),
                        pltpu.VMEM((TQ + 16, CW), F32), pltpu.SemaphoreType.DMA((1,))],
        compiler_params=_cp(("arbitrary",)), name="fwd_mix",
    )(x, proj, proj, proj, proj, proj, proj, proj, bias2, wconv_t, g_co, g_ao, g_pm, gm, wout_all)


def fwd_ffn(xmid, g_pre, g_post, wfi_all, wfo_all, after=()):
    t = xmid.shape[0]

    def body(x_ref, gpre_ref, gpost_ref, wfi_hbm, wfo_hbm, gu_ref, f_ref, xo_ref, wfi_v, wfo_v, sems):
        def step(ready):
            xv = x_ref[...]
            h = _rms(xv, gpre_ref[...]).astype(BF16)
            f = jnp.zeros((TM, D), F32)
            for ci, (a, b) in enumerate(FF_CHUNKS):
                ready(0, ci)
                gate = jnp.dot(h, wfi_v[0, :, a:b], preferred_element_type=F32)
                ready(1, ci)
                up = jnp.dot(h, wfi_v[1, :, a:b], preferred_element_type=F32)
                gu_ref[:, a:b] = gate.astype(BF16)
                gu_ref[:, DFF + a:DFF + b] = up.astype(BF16)
                act = gate * (1.0 / (1.0 + jnp.exp(-gate))) * up
                ready(2, ci)
                f = f + jnp.dot(act.astype(BF16), wfo_v[a:b, :], preferred_element_type=F32)
            f_ref[...] = f
            xo_ref[...] = xv + _rms(f, gpost_ref[...])

        _stream_ffn_weights(wfi_hbm, wfo_hbm, wfi_v, wfo_v, sems, (0, 1, 2), step)

    row = lambda w: pl.BlockSpec((TM, w), lambda i: (i, 0))
    return pl.pallas_call(
        _behind(body, 5, after), grid=(t // TM,),
        in_specs=[row(D), _const((1, D)), _const((1, D)), _any(), _any()] + [_any()] * len(after),
        out_specs=[row(2 * DFF), row(D), row(D)],
        out_shape=[jax.ShapeDtypeStruct((t, 2 * DFF), BF16), jax.ShapeDtypeStruct((t, D), F32),
                   jax.ShapeDtypeStruct((t, D), F32)],
        scratch_shapes=[pltpu.VMEM((2, D, DFF), BF16), pltpu.VMEM((DFF, D), BF16), pltpu.SemaphoreType.DMA((6,))],
        compiler_params=_cp(("arbitrary",)), name="fwd_ffn")(xmid, g_pre, g_post, wfi_all, wfo_all, *after)


def loss_head(y, target):
    t = y.shape[0]

    def body(y_ref, t_ref, dy_ref, l_ref):
        @pl.when(pl.program_id(0) == 0)
        def _():
            l_ref[...] = jnp.zeros_like(l_ref)

        e = y_ref[...] - t_ref[...]
        dy_ref[...] = e * (1.0 / D)
        rows = jnp.sum(e * e, axis=-1, keepdims=True) * (1.0 / D)
        l_ref[...] += 0.5 * jnp.sum(rows, axis=0, keepdims=True)

    row = pl.BlockSpec((TQ, D), lambda i: (i, 0))
    return pl.pallas_call(
        body, grid=(t // TQ,), in_specs=[row, row], out_specs=[row, _const((8, LANES))],
        out_shape=[jax.ShapeDtypeStruct((t, D), F32), jax.ShapeDtypeStruct((8, LANES), F32)],
        compiler_params=_cp(("arbitrary",)), name="loss_head")(y, target)


def bwd_ffn(dx, f, xmid, gu, g_pre, g_post, wfi_all, wfo_all, after=()):
    t = dx.shape[0]
    hw = DFF // 2

    def body(dx_ref, f_ref, x_ref, gu_ref, gpre_ref, gpost_ref, wfi_hbm, wfo_hbm,
             dxm_ref, df_ref, act_ref, dgu_ref, h_ref, dgpost_ref, dgpre_ref, wfi_v, wfo_v, sems):
        @pl.when(pl.program_id(0) == 0)
        def _():
            dgpost_ref[...] = jnp.zeros_like(dgpost_ref)
            dgpre_ref[...] = jnp.zeros_like(dgpre_ref)

        def step(ready):
            dxo = dx_ref[...]
            df, dgp = _rms_bwd(dxo, f_ref[...], gpost_ref[...])
            dgpost_ref[...] += dgp
            dfb = df.astype(BF16)
            df_ref[...] = dfb
            dh = jnp.zeros((TM, D), F32)
            for ci, (a, b) in enumerate(FF_CHUNKS):
                ready(2, ci)
                dact = lax.dot_general(dfb, wfo_v[a:b, :], NT, preferred_element_type=F32)
                gate = gu_ref[:, a:b].astype(F32)
                up = gu_ref[:, DFF + a:DFF + b].astype(F32)
                sig = 1.0 / (1.0 + jnp.exp(-gate))
                silu = gate * sig
                act_ref[:, a:b] = (silu * up).astype(BF16)
                dup = (dact * silu).astype(BF16)
                dgate = (dact * up * (sig * (1.0 + gate * (1.0 - sig)))).astype(BF16)
                dgu_ref[:, a:b] = dgate
                dgu_ref[:, DFF + a:DFF + b] = dup
                ready(0, ci)
                dh = dh + lax.dot_general(dgate, wfi_v[0, :, a:b], NT, preferred_element_type=F32)
                ready(1, ci)
                dh = dh + lax.dot_general(dup, wfi_v[1, :, a:b], NT, preferred_element_type=F32)
            xv = x_ref[...]
            gpre = gpre_ref[...]
            h_ref[...] = _rms(xv, gpre).astype(BF16)
            dxv, dgq = _rms_bwd(dh, xv, gpre)
            dgpre_ref[...] += dgq
            dxm_ref[...] = dxo + dxv

        _stream_ffn_weights(wfi_hbm, wfo_hbm, wfi_v, wfo_v, sems, (2, 0, 1), step)

    row = lambda w: pl.BlockSpec((TM, w), lambda i: (i, 0))
    return pl.pallas_call(
        _behind(body, 8, after), grid=(t // TM,),
        in_specs=[row(D), row(D), row(D), row(2 * DFF), _const((1, D)), _const((1, D)), _any(), _any()]
        + [_any()] * len(after),
        out_specs=[row(D), row(D), row(DFF), row(2 * DFF), row(D), _const((1, D)), _const((1, D))],
        out_shape=[jax.ShapeDtypeStruct((t, D), F32), jax.ShapeDtypeStruct((t, D), BF16),
                   jax.ShapeDtypeStruct((t, DFF), BF16), jax.ShapeDtypeStruct((t, 2 * DFF), BF16),
                   jax.ShapeDtypeStruct((t, D), BF16), jax.ShapeDtypeStruct((1, D), F32),
                   jax.ShapeDtypeStruct((1, D), F32)],
        scratch_shapes=[pltpu.VMEM((2, D, DFF), BF16), pltpu.VMEM((DFF, D), BF16), pltpu.SemaphoreType.DMA((6,))],
        compiler_params=_cp(("arbitrary",)), name="bwd_ffn")(dx, f, xmid, gu, g_pre, g_post, wfi_all, wfo_all, *after)


def bwd_mix(dxm, z, o, proj, wconv_t, g_co, g_ao, g_pm, gm, wout_all, after=()):
    t = dxm.shape[0]

    def body(dx_ref, z_ref, o_ref, pc_ref, pcp_ref, wc_ref, gco_ref, gao_ref, gpm_ref, gm_ref, wout_hbm,
             dz_ref, do_ref, dco_ref, dbg_ref, dgpm_ref, dgco_ref, dgao_ref, wout_v, cscr):
        first = pl.program_id(0) == 0

        @pl.when(first)
        def _():
            pltpu.sync_copy(wout_hbm, wout_v)
            dgpm_ref[...] = jnp.zeros_like(dgpm_ref)
            dgco_ref[...] = jnp.zeros_like(dgco_ref)
            dgao_ref[...] = jnp.zeros_like(dgao_ref)

        dz, dgp = _rms_bwd(dx_ref[...], z_ref[...], gpm_ref[...])
        dgpm_ref[...] += dgp
        dzb = dz.astype(BF16)
        dz_ref[...] = dzb
        gmv = gm_ref[...]
        _, bg, _, _, _, _, cout = _conv_fwd(pc_ref, pcp_ref, wc_ref, cscr, first)
        dy_conv = lax.dot_general(dzb, wout_v[0:CW, :], NT, preferred_element_type=F32)
        dyc, dgc = _group_rms_bwd(dy_conv, bg * cout, gco_ref[...], gmv)
        dgco_ref[...] += dgc
        dbg_ref[...] = (dyc * cout).astype(BF16)
        dco_ref[...] = dyc * bg
        dy_attn = lax.dot_general(dzb, wout_v[CW:2 * CW, :], NT, preferred_element_type=F32)
        do, dga = _group_rms_bwd(dy_attn, o_ref[...], gao_ref[...], gmv)
        dgao_ref[...] += dga
        do_ref[...] = do.astype(BF16)

    row = lambda w: pl.BlockSpec((TQ, w), lambda i: (i, 0))
    return pl.pallas_call(
        _behind(body, 11, after), grid=(t // TQ,),
        in_specs=[row(D), row(D), row(CW)] + _conv_specs() + [
            _const((8, CW)), _const((1, CW)), _const((1, CW)), _const((1, D)), _const((CW, CW)), _any()]
        + [_any()] * len(after),
        out_specs=[row(D), row(CW), row(CW), row(CW), _const((1, D)), _const((1, CW)), _const((1, CW))],
        out_shape=[jax.ShapeDtypeStruct((t, D), BF16), jax.ShapeDtypeStruct((t, CW), BF16),
                   jax.ShapeDtypeStruct((t, CW), F32), jax.ShapeDtypeStruct((t, CW), BF16),
                   jax.ShapeDtypeStruct((1, D), F32), jax.ShapeDtypeStruct((1, CW), F32),
                   jax.ShapeDtypeStruct((1, CW), F32)],
        scratch_shapes=[pltpu.VMEM((D, D), BF16), pltpu.VMEM((TQ + 16, CW), F32)],
        compiler_params=_cp(("arbitrary",)), name="bwd_mix",
    )(dxm, z, o, proj, proj, wconv_t, g_co, g_ao, g_pm, gm, wout_all, *after)


def bwd_conv(dco, proj, wconv_t, after=()):
    t = dco.shape[0]
    nt = t // TQ

    def body(d_ref, dn_ref, pc_ref, pcp_ref, wc_ref, dhc_ref, dcg_ref, dw_ref, cscr, dscr):
        i = pl.program_id(0)
        first = i == 0

        @pl.when(first)
        def _():
            dw_ref[...] = jnp.zeros_like(dw_ref)

        hc, _, cg, u, u1, u2, _ = _conv_fwd(pc_ref, pcp_ref, wc_ref, cscr, first)
        d0 = d_ref[...]
        dscr[0:TQ, :] = d0
        dscr[TQ:TQ + 8, :] = jnp.where(i == nt - 1, 0.0, dn_ref[...])
        d1 = dscr[1:TQ + 1, :]
        d2 = dscr[2:TQ + 2, :]
        du = wc_ref[2:3, :] * d0 + wc_ref[1:2, :] * d1 + wc_ref[0:1, :] * d2
        dhc_ref[...] = (du * cg).astype(BF16)
        dcg_ref[...] = (du * hc).astype(BF16)
        dw_ref[0:1, :] += jnp.sum(d0 * u2, axis=0, keepdims=True)
        dw_ref[1:2, :] += jnp.sum(d0 * u1, axis=0, keepdims=True)
        dw_ref[2:3, :] += jnp.sum(d0 * u, axis=0, keepdims=True)

    row = lambda w: pl.BlockSpec((TQ, w), lambda i: (i, 0))
    nxt = pl.BlockSpec((8, CW), lambda i: (jnp.minimum((i + 1) * (TQ // 8), t // 8 - 1), 0))
    return pl.pallas_call(
        _behind(body, 5, after), grid=(nt,),
        in_specs=[row(CW), nxt] + _conv_specs() + [_const((8, CW))] + [_any()] * len(after),
        out_specs=[row(CW), row(CW), _const((8, CW))],
        out_shape=[jax.ShapeDtypeStruct((t, CW), BF16), jax.ShapeDtypeStruct((t, CW), BF16),
                   jax.ShapeDtypeStruct((8, CW), F32)],
        scratch_shapes=[pltpu.VMEM((TQ + 16, CW), F32), pltpu.VMEM((TQ + 8, CW), F32)],
        compiler_params=_cp(("arbitrary",)), name="bwd_conv")(dco, dco, proj, proj, wconv_t, *after)


def bwd_attn(proj, o, do, lse, bias2):
    t = o.shape[0]
    nt = t // TQ
    qg, kg = QG_BWD, QG_BWD + LEFT
    nkb = (t + TQ) // LANES

    def body(q_ref, kp_ref, kc_ref, vp_ref, vc_ref, o_ref, do_ref, lse_ref, b2_ref,
             dq_ref, dk_hbm, dv_hbm, db_hbm, kwin, vwin, dk_acc, dv_acc, db_acc):
        i = pl.program_id(0)
        first = i == 0

        @pl.when(first)
        def _():
            dk_acc[...] = jnp.zeros_like(dk_acc)
            dv_acc[...] = jnp.zeros_like(dv_acc)
            db_acc[...] = jnp.zeros_like(db_acc)

        kwin[0:TQ, :] = kp_ref[...]
        kwin[TQ:2 * TQ, :] = kc_ref[...]
        vwin[0:TQ, :] = vp_ref[...]
        vwin[TQ:2 * TQ, :] = vc_ref[...]
        scale = HD ** -0.5
        qmask = _head_masks(scale)
        vmask = _head_masks(1.0)
        low = lax.broadcasted_iota(jnp.int32, (1, LANES), 1) < HD

        def group(g, carry):
            r0 = pl.multiple_of(g * qg, qg)
            base = i * (TQ // LANES) + g * (qg // LANES)
            pen = _key_penalty(first, r0, kg)
            for hp in range(NH // 2):
                ls = slice(LANES * hp, LANES * (hp + 1))
                qb = q_ref[pl.ds(r0, qg), ls]
                kw = kwin[pl.ds(r0, kg), ls]
                dob = do_ref[pl.ds(r0, qg), ls]
                prod = dob.astype(F32) * o_ref[pl.ds(r0, qg), ls]
                lseb = lse_ref[pl.ds(r0, qg), ls]
                q2 = jnp.concatenate([qb * qmask[0], qb * qmask[1]], axis=0)
                do2 = jnp.concatenate([dob * vmask[0], dob * vmask[1]], axis=0)
                lse2 = jnp.concatenate([lseb[:, 0:1], lseb[:, HD:HD + 1]], axis=0)
                dsum = jnp.concatenate([jnp.sum(jnp.where(low, prod, 0.0), axis=-1, keepdims=True),
                                        jnp.sum(jnp.where(low, 0.0, prod), axis=-1, keepdims=True)], axis=0)
                s = lax.dot_general(q2, kw, NT, preferred_element_type=F32) + b2_ref[hp] + pen
                p = jnp.exp(s - lse2)
                dp = lax.dot_general(do2, vwin[pl.ds(r0, kg), ls], NT, preferred_element_type=F32)
                ds = p * (dp - dsum)
                db_acc[hp] += ds
                dsb = ds.astype(BF16)
                dq2 = jnp.dot(dsb, kw, preferred_element_type=F32)
                dq_ref[pl.ds(r0, qg), ls] = (jnp.where(low, dq2[:qg], dq2[qg:]) * scale).astype(BF16)
                dkt = lax.dot_general(q2, dsb, TN, preferred_element_type=F32)
                dvt = lax.dot_general(do2, p.astype(BF16), TN, preferred_element_type=F32)
                for kb in range(kg // LANES):
                    dk_acc[base + kb, ls, :] += dkt[:, LANES * kb:LANES * (kb + 1)]
                    dv_acc[base + kb, ls, :] += dvt[:, LANES * kb:LANES * (kb + 1)]
            return carry

        lax.fori_loop(0, TQ // qg, group, 0)

        @pl.when(i == nt - 1)
        def _():
            pltpu.sync_copy(dk_acc, dk_hbm)
            pltpu.sync_copy(dv_acc, dv_hbm)
            pltpu.sync_copy(db_acc, db_hbm)

    row = lambda w: pl.BlockSpec((TQ, w), lambda i: (i, 0))
    return pl.pallas_call(
        body, grid=(nt,),
        in_specs=_attn_window_specs() + [row(CW), row(CW), row(CW), _const((NH // 2, 2 * qg, kg))],
        out_specs=[row(CW), _any(), _any(), _any()],
        out_shape=[jax.ShapeDtypeStruct((t, CW), BF16), jax.ShapeDtypeStruct((nkb, CW, LANES), F32),
                   jax.ShapeDtypeStruct((nkb, CW, LANES), F32), jax.ShapeDtypeStruct((NH // 2, 2 * qg, kg), F32)],
        scratch_shapes=[pltpu.VMEM((2 * TQ, CW), BF16), pltpu.VMEM((2 * TQ, CW), BF16),
                        pltpu.VMEM((nkb, CW, LANES), F32), pltpu.VMEM((nkb, CW, LANES), F32),
                        pltpu.VMEM((NH // 2, 2 * qg, kg), F32)],
        compiler_params=_cp(("arbitrary",)), name="bwd_attn",
    )(proj, proj, proj, proj, proj, o, do, lse, bias2)


def bwd_inproj(dxm, x, dhc, dbg, dcg, dq, dk, dv, g, w_all):
    t = x.shape[0]
    wc = PROJ // NCHIP

    def body(dxm_ref, x_ref, dhc_ref, dbg_ref, dcg_ref, dq_ref, dk_ref, dv_ref, g_ref, w_hbm,
             dx_ref, dp_ref, h_ref, dg_ref, w_v):
        @pl.when(pl.program_id(0) == 0)
        def _():
            pltpu.sync_copy(w_hbm, w_v)
            dg_ref[...] = jnp.zeros_like(dg_ref)

        dp_ref[:, 0:CW] = dhc_ref[...]
        dp_ref[:, CW:2 * CW] = dbg_ref[...]
        dp_ref[:, 2 * CW:3 * CW] = dcg_ref[...]
        dp_ref[:, 3 * CW:4 * CW] = dq_ref[...]
        for kb in range(TQ // LANES):
            rows = slice(LANES * kb, LANES * (kb + 1))
            dp_ref[rows, 4 * CW:5 * CW] = jnp.transpose(dk_ref[kb]).astype(BF16)
            dp_ref[rows, 5 * CW:6 * CW] = jnp.transpose(dv_ref[kb]).astype(BF16)
        dh = jnp.zeros((TQ, D), F32)
        for b in range(NCHIP):
            dh = dh + lax.dot_general(dp_ref[:, wc * b:wc * (b + 1)], w_v[b], NT, preferred_element_type=F32)
        xv = x_ref[...]
        gv = g_ref[...]
        h_ref[...] = _rms(xv, gv).astype(BF16)
        dxv, dgv = _rms_bwd(dh, xv, gv)
        dg_ref[...] += dgv
        dx_ref[...] = dxm_ref[...] + dxv

    row = lambda w: pl.BlockSpec((TQ, w), lambda i: (i, 0))
    pad = pl.BlockSpec((TQ // LANES, CW, LANES), lambda i: (i + 1, 0, 0))
    return pl.pallas_call(
        body, grid=(t // TQ,),
        in_specs=[row(D), row(D), row(CW), row(CW), row(CW), row(CW), pad, pad, _const((1, D)), _any()],
        out_specs=[row(D), row(PROJ), row(D), _const((1, D))],
        out_shape=[jax.ShapeDtypeStruct((t, D), F32), jax.ShapeDtypeStruct((t, PROJ), BF16),
                   jax.ShapeDtypeStruct((t, D), BF16), jax.ShapeDtypeStruct((1, D), F32)],
        scratch_shapes=[pltpu.VMEM((NCHIP, D, wc), BF16)],
        compiler_params=_cp(("arbitrary",)), name="bwd_inproj",
    )(dxm, x, dhc, dbg, dcg, dq, dk, dv, g, w_all)


def wgrad(a, b, kb, nb, by_columns, name):
    t, k = a.shape
    n = b.shape[1]
    tk = 512

    def body(a_ref, b_ref, o_ref):
        o_ref[...] = jnp.zeros_like(o_ref)
        for c in range(t // tk):
            o_ref[...] += lax.dot_general(a_ref[tk * c:tk * (c + 1), :], b_ref[tk * c:tk * (c + 1), :], TN,
                                          preferred_element_type=F32)

    if by_columns:
        assert nb == n // NCHIP
        out_spec = pl.BlockSpec((None, kb, nb), lambda ki, ni: (ni, ki, 0))
        out_shape = jax.ShapeDtypeStruct((NCHIP, k, nb), F32)
    else:
        assert nb == n
        out_spec = pl.BlockSpec((kb, nb), lambda ki, ni: (ki, 0))
        out_shape = jax.ShapeDtypeStruct((k, n), F32)
    return pl.pallas_call(
        body, grid=(k // kb, n // nb),
        in_specs=[pl.BlockSpec((t, kb), lambda ki, ni: (0, ki)), pl.BlockSpec((t, nb), lambda ki, ni: (0, ni))],
        out_specs=out_spec, out_shape=out_shape,
        compiler_params=_cp(("arbitrary", "arbitrary")), name=name)(a, b)


TOE = 1024
assert 2 * QG_FWD + LEFT <= TOE
N_FLAT = LEFT - REL_CLIP + 1
N_VAR = BAND - N_FLAT


def _diag_vector(table):
    last = table[:, 2 * REL_CLIP:]
    var = table[:, 2 * REL_CLIP - N_VAR:2 * REL_CLIP][:, ::-1]
    return jnp.concatenate([jnp.broadcast_to(last, (NH, N_FLAT)), var, jnp.broadcast_to(last, (NH, TOE - BAND))], axis=1)


def _diag_vector_bwd(dvec):
    dlast = jnp.sum(dvec[:, :N_FLAT], axis=1, keepdims=True) + jnp.sum(dvec[:, BAND:], axis=1, keepdims=True)
    dvar = dvec[:, N_FLAT:BAND][:, ::-1]
    return jnp.concatenate([jnp.zeros((NH, 2 * REL_CLIP - N_VAR), F32), dvar, dlast], axis=1)


def _band_valid(qg):
    r = lax.broadcasted_iota(jnp.int32, (qg, qg + LEFT), 0)
    p = lax.broadcasted_iota(jnp.int32, (qg, qg + LEFT), 1)
    start = lax.shift_left(lax.shift_right_logical(r, 6), 6)
    return (p >= start) & (p < start + BAND)


def bias_expand(vec, qgs, after=()):
    def body(v_ref, *o_refs):
        for qg, o_ref in zip(qgs, o_refs):
            valid = _band_valid(qg)
            for h in range(NH):
                rows = jnp.broadcast_to(v_ref[h:h + 1, :], (qg, TOE))
                toe = pltpu.roll(rows, 0, 1, stride=1, stride_axis=0)
                o_ref[h // 2, qg * (h % 2):qg * (h % 2 + 1), :] = jnp.where(valid, toe[:, :qg + LEFT], NEG_INF)

    vm = pl.BlockSpec(memory_space=pltpu.VMEM)
    return pl.pallas_call(_behind(body, 1, after), in_specs=[vm] + [_any()] * len(after), out_specs=[vm] * len(qgs),
                          out_shape=[jax.ShapeDtypeStruct((NH // 2, 2 * qg, qg + LEFT), F32) for qg in qgs],
                          name="bias_expand")(vec, *after)


def bias_reduce(db2):
    _, qg, kg = db2.shape

    def body(d_ref, o_ref):
        ii = lax.broadcasted_iota(jnp.int32, (kg, kg), 0)
        jj = lax.broadcasted_iota(jnp.int32, (kg, kg), 1)
        flip = jnp.where(ii + jj == kg - 1, 1.0, 0.0).astype(BF16)
        for h in range(NH):
            rest = d_ref[h]
            rev = jnp.zeros((qg, kg), F32)
            for _ in range(3):
                term = rest.astype(BF16)
                rev = rev + jnp.dot(term, flip, preferred_element_type=F32)
                rest = rest - term.astype(F32)
            d = jnp.concatenate([jnp.zeros((qg, TOE - kg), F32), rev], axis=1)
            back = pltpu.roll(d, 0, 1, stride=1, stride_axis=0)
            o_ref[h:h + 1, :] = jnp.sum(back, axis=0, keepdims=True)

    rev = pl.pallas_call(body, out_shape=jax.ShapeDtypeStruct((NH, TOE), F32), name="bias_reduce")(db2)
    return rev[:, ::-1]


def _place():
    x, y, c = lax.axis_index("x"), lax.axis_index("y"), lax.axis_index("c")
    chips = [(1 - x, y), (x, 1 - y), (1 - x, 1 - y)]
    return x, y, c, chips


def _half(ref_rows, c):
    return pl.ds(c * (ref_rows // 2), ref_rows // 2)


HBM_SPEC = pl.BlockSpec(memory_space=pltpu.HBM)
SEM_SPEC = pl.BlockSpec(memory_space=pltpu.SEMAPHORE)
IN_FLIGHT = pltpu.CompilerParams(has_side_effects=pltpu.SideEffectType.DATAFLOW_SIDE_EFFECTING)


def _in_hbm(a):
    return pltpu.with_memory_space_constraint(a, pltpu.HBM)


def cast_to_slot(ws, chip, layer, after=()):
    n = len(ws)
    steps = 4

    def body(b_ref, *refs):
        del b_ref
        for w_ref, o_ref in zip(refs[:n], refs[n + len(after):]):
            o_ref[...] = w_ref[...].astype(BF16)

    grid_spec = pltpu.PrefetchScalarGridSpec(
        num_scalar_prefetch=1, grid=(steps,),
        in_specs=[pl.BlockSpec((None, w.shape[1] // steps, w.shape[2]), lambda r, b: (layer, r, 0)) for w in ws]
        + [_any()] * len(after),
        out_specs=[pl.BlockSpec((None, w.shape[1] // steps, w.shape[2]), lambda r, b: (b[0], r, 0)) for w in ws])
    return pl.pallas_call(body, grid_spec=grid_spec,
                          out_shape=[jax.ShapeDtypeStruct((NCHIP,) + w.shape[1:], BF16) for w in ws],
                          compiler_params=_cp(("arbitrary",)), name="cast_to_slot")(chip, *ws, *after)


def _gather_copies(bufs, send, recv):
    x, y, c, chips = _place()
    b = 2 * x + y
    out = []
    for k, buf in enumerate(bufs):
        rows = buf.shape[1]
        mine = buf.at[b, _half(rows, c), :]
        for j, (cx, cy) in enumerate(chips):
            theirs = buf.at[2 * cx + cy, _half(rows, c), :]
            sems = dict(send_sem=send.at[3 * k + j], recv_sem=recv.at[3 * k + j],
                        device_id=(cx, cy, c), device_id_type=MESH)
            out.append((pltpu.make_async_remote_copy(src_ref=mine, dst_ref=mine, **sems),
                        pltpu.make_async_remote_copy(src_ref=theirs, dst_ref=theirs, **sems)))
    return out


def gather_start(bufs, after, layer):
    n = len(bufs)

    def body(*refs):
        ins = refs[:n]
        send, recv = refs[n + 1], refs[n + 2]
        token = refs[-1]
        for start, _ in _gather_copies(ins, send, recv):
            start.start()
        token[...] = jnp.zeros_like(token)

    sems = pltpu.SemaphoreType.DMA((3 * n,))
    res = pl.pallas_call(
        body, name=f"gather_start_{layer}",
        in_specs=[HBM_SPEC] * n + [_any()],
        out_specs=[SEM_SPEC, SEM_SPEC] + [HBM_SPEC] * n + [pl.BlockSpec(memory_space=pltpu.VMEM)],
        out_shape=[sems, sems] + [pltpu.HBM(b.shape, b.dtype) for b in bufs] + [jax.ShapeDtypeStruct((8, LANES), F32)],
        input_output_aliases={k: 2 + k for k in range(n)}, compiler_params=IN_FLIGHT,
    )(*[_in_hbm(b) for b in bufs], after)
    return res[0], res[1], res[2:2 + n], res[-1]


def gather_wait(send, recv, bufs, after, layer):
    n = len(bufs)

    def body(*refs):
        ins = refs[:n]
        send_ref, recv_ref = refs[n], refs[n + 1]
        for start, arrival in _gather_copies(ins, send_ref, recv_ref):
            start.wait_send()
            arrival.wait_recv()

    return pl.pallas_call(
        body, name=f"gather_wait_{layer}",
        in_specs=[HBM_SPEC] * n + [SEM_SPEC, SEM_SPEC, _any()], out_specs=[HBM_SPEC] * n,
        out_shape=[pltpu.HBM(b.shape, b.dtype) for b in bufs],
        input_output_aliases={k: k for k in range(n)}, compiler_params=IN_FLIGHT,
    )(*bufs, send, recv, after)


def gather_forward(bufs):
    n = len(bufs)

    def body(*refs):
        outs = refs[n:2 * n]
        send, recv = refs[2 * n:]
        x, y, c, chips = _place()
        cps = []
        for k in range(n):
            rows = outs[k].shape[1]
            for j, (cx, cy) in enumerate(chips):
                sems = dict(send_sem=send.at[3 * k + j], recv_sem=recv.at[3 * k + j],
                            device_id=(x, y, 1 - c), device_id_type=MESH)
                mine = outs[k].at[2 * cx + cy, _half(rows, c), :]
                theirs = outs[k].at[2 * cx + cy, _half(rows, 1 - c), :]
                cp = pltpu.make_async_remote_copy(src_ref=mine, dst_ref=mine, **sems)
                cp.start()
                cps.append((cp, pltpu.make_async_remote_copy(src_ref=theirs, dst_ref=theirs, **sems)))
        for cp, arrival in cps:
            cp.wait_send()
            arrival.wait_recv()

    return pl.pallas_call(
        body, in_specs=[_any()] * n, out_specs=[_any()] * n,
        out_shape=[jax.ShapeDtypeStruct(b.shape, b.dtype) for b in bufs], input_output_aliases={k: k for k in range(n)},
        scratch_shapes=[pltpu.SemaphoreType.DMA((3 * n,)), pltpu.SemaphoreType.DMA((3 * n,))],
        name="gather_forward")(*bufs)


def _forward_copies(bufs, send, recv):
    x, y, c, chips = _place()
    out = []
    for k, buf in enumerate(bufs):
        rows = buf.shape[1]
        for j, (cx, cy) in enumerate(chips):
            sems = dict(send_sem=send.at[3 * k + j], recv_sem=recv.at[3 * k + j],
                        device_id=(x, y, 1 - c), device_id_type=MESH)
            mine = buf.at[2 * cx + cy, _half(rows, c), :]
            theirs = buf.at[2 * cx + cy, _half(rows, 1 - c), :]
            out.append((pltpu.make_async_remote_copy(src_ref=mine, dst_ref=mine, **sems),
                        pltpu.make_async_remote_copy(src_ref=theirs, dst_ref=theirs, **sems)))
    return out


def forward_start(bufs, tag):
    n = len(bufs)

    def body(*refs):
        ins = refs[:n]
        send, recv = refs[n], refs[n + 1]
        token = refs[-1]
        for start, _ in _forward_copies(ins, send, recv):
            start.start()
        token[...] = jnp.zeros_like(token)

    sems = pltpu.SemaphoreType.DMA((3 * n,))
    res = pl.pallas_call(
        body, name=f"forward_start_{tag}", in_specs=[HBM_SPEC] * n,
        out_specs=[SEM_SPEC, SEM_SPEC] + [HBM_SPEC] * n + [pl.BlockSpec(memory_space=pltpu.VMEM)],
        out_shape=[sems, sems] + [pltpu.HBM(b.shape, b.dtype) for b in bufs] + [jax.ShapeDtypeStruct((8, LANES), F32)],
        input_output_aliases={k: 2 + k for k in range(n)}, compiler_params=IN_FLIGHT,
    )(*[_in_hbm(b) for b in bufs])
    return res[0], res[1], res[2:2 + n], res[-1]


def forward_wait(send, recv, bufs, after, tag):
    n = len(bufs)

    def body(*refs):
        ins = refs[:n]
        send_ref, recv_ref = refs[n], refs[n + 1]
        for start, arrival in _forward_copies(ins, send_ref, recv_ref):
            start.wait_send()
            arrival.wait_recv()

    return pl.pallas_call(
        body, name=f"forward_wait_{tag}",
        in_specs=[HBM_SPEC] * n + [SEM_SPEC, SEM_SPEC, _any()], out_specs=[HBM_SPEC] * n,
        out_shape=[pltpu.HBM(b.shape, b.dtype) for b in bufs],
        input_output_aliases={k: k for k in range(n)}, compiler_params=IN_FLIGHT,
    )(*bufs, send, recv, after)


def _exchange_copies(srcs, lands, send, recv):
    x, y, c, _ = _place()
    return [pltpu.make_async_remote_copy(
        src_ref=src.at[:, _half(src.shape[1], 1 - c), :], dst_ref=land, send_sem=send.at[k], recv_sem=recv.at[k],
        device_id=(x, y, 1 - c), device_id_type=MESH) for k, (src, land) in enumerate(zip(srcs, lands))]


def exchange_start(srcs, tag):
    n = len(srcs)
    lands = [lax.empty((s.shape[0], s.shape[1] // 2, s.shape[2]), s.dtype) for s in srcs]

    def body(*refs):
        ins, land_refs = refs[:n], refs[n:2 * n]
        send, recv = refs[2 * n], refs[2 * n + 1]
        token = refs[-1]
        for cp in _exchange_copies(ins, land_refs, send, recv):
            cp.start()
        token[...] = jnp.zeros_like(token)

    sems = pltpu.SemaphoreType.DMA((n,))
    res = pl.pallas_call(
        body, name=f"exchange_start_{tag}",
        in_specs=[HBM_SPEC] * (2 * n),
        out_specs=[SEM_SPEC, SEM_SPEC] + [HBM_SPEC] * (2 * n) + [pl.BlockSpec(memory_space=pltpu.VMEM)],
        out_shape=[sems, sems] + [pltpu.HBM(a.shape, a.dtype) for a in list(srcs) + lands]
        + [jax.ShapeDtypeStruct((8, LANES), F32)],
        input_output_aliases={k: 2 + k for k in range(2 * n)}, compiler_params=IN_FLIGHT,
    )(*[_in_hbm(a) for a in list(srcs) + lands])
    return res[0], res[1], res[2:2 + n], res[2 + n:2 + 2 * n], res[-1]


def exchange_wait(send, recv, srcs, lands, after, tag):
    n = len(srcs)

    def body(*refs):
        ins, land_refs = refs[:n], refs[n:2 * n]
        send_ref, recv_ref = refs[2 * n], refs[2 * n + 1]
        for cp in _exchange_copies(ins, land_refs, send_ref, recv_ref):
            cp.wait_send()
            cp.wait_recv()

    res = pl.pallas_call(
        body, name=f"exchange_wait_{tag}",
        in_specs=[HBM_SPEC] * (2 * n) + [SEM_SPEC, SEM_SPEC, _any()], out_specs=[HBM_SPEC] * (2 * n),
        out_shape=[pltpu.HBM(a.shape, a.dtype) for a in list(srcs) + list(lands)],
        input_output_aliases={k: k for k in range(2 * n)}, compiler_params=IN_FLIGHT,
    )(*srcs, *lands, send, recv, after)
    return res[:n], res[n:]


def add_pair(gs, r1s, core):
    n = len(gs)

    def body(c_ref, *refs):
        del c_ref
        for g_ref, r_ref, o_ref in zip(refs[:n], refs[n:2 * n], refs[2 * n:]):
            o_ref[...] = (g_ref[...] + r_ref[...]).astype(BF16)

    blk = lambda r: (None,) + r.shape[1:]
    grid_spec = pltpu.PrefetchScalarGridSpec(
        num_scalar_prefetch=1, grid=(NCHIP,),
        in_specs=[pl.BlockSpec(blk(r), lambda s, c: (s, c[0], 0)) for r in r1s]
        + [pl.BlockSpec(blk(r), lambda s, c: (s, 0, 0)) for r in r1s],
        out_specs=[pl.BlockSpec(blk(r), lambda s, c: (s, 0, 0)) for r in r1s])
    return pl.pallas_call(body, grid_spec=grid_spec, out_shape=[jax.ShapeDtypeStruct(r.shape, BF16) for r in r1s],
                          compiler_params=_cp(("arbitrary",)), name="add_pair")(core, *gs, *r1s)


def _scatter_copies(srcs, lands, send, recv):
    _, _, c, chips = _place()
    out = []
    for k, (src, land) in enumerate(zip(srcs, lands)):
        for j, (cx, cy) in enumerate(chips):
            out.append(pltpu.make_async_remote_copy(
                src_ref=src.at[2 * cx + cy], dst_ref=land.at[j], send_sem=send.at[3 * k + j],
                recv_sem=recv.at[3 * k + j], device_id=(cx, cy, c), device_id_type=MESH))
    return out


def scatter_start(srcs, layer):
    n = len(srcs)
    srcs = list(srcs)
    lands = [lax.empty((3,) + s.shape[1:], s.dtype) for s in srcs]

    def body(*refs):
        ins, land_refs = refs[:n], refs[n:2 * n]
        send, recv = refs[2 * n], refs[2 * n + 1]
        token = refs[-1]
        for cp in _scatter_copies(ins, land_refs, send, recv):
            cp.start()
        token[...] = jnp.zeros_like(token)

    sems = pltpu.SemaphoreType.DMA((3 * n,))
    res = pl.pallas_call(
        body, name=f"scatter_start_{layer}",
        in_specs=[HBM_SPEC] * (2 * n),
        out_specs=[SEM_SPEC, SEM_SPEC] + [HBM_SPEC] * (2 * n) + [pl.BlockSpec(memory_space=pltpu.VMEM)],
        out_shape=[sems, sems] + [pltpu.HBM(a.shape, a.dtype) for a in srcs + lands]
        + [jax.ShapeDtypeStruct((8, LANES), F32)],
        input_output_aliases={k: 2 + k for k in range(2 * n)}, compiler_params=IN_FLIGHT,
    )(*[_in_hbm(a) for a in srcs + lands])
    return res[0], res[1], res[2:2 + n], res[2 + n:2 + 2 * n], res[-1]


def scatter_wait(send, recv, srcs, lands, after, layer):
    n = len(srcs)

    def body(*refs):
        ins, land_refs = refs[:n], refs[n:2 * n]
        send_ref, recv_ref = refs[2 * n], refs[2 * n + 1]
        for cp in _scatter_copies(ins, land_refs, send_ref, recv_ref):
            cp.wait_send()
            cp.wait_recv()

    res = pl.pallas_call(
        body, name=f"scatter_wait_{layer}",
        in_specs=[HBM_SPEC] * (2 * n) + [SEM_SPEC, SEM_SPEC, _any()], out_specs=[HBM_SPEC] * (2 * n),
        out_shape=[pltpu.HBM(a.shape, a.dtype) for a in list(srcs) + list(lands)],
        input_output_aliases={k: k for k in range(2 * n)}, compiler_params=IN_FLIGHT,
    )(*srcs, *lands, send, recv, after)
    return res[n:]


def add_chips(gs, r1s, r2s, place, totals, layer):
    n = len(gs)
    steps = 2

    def body(p_ref, *refs):
        del p_ref
        for g_ref, r1_ref, r2_ref, o_ref in zip(refs[:n], refs[n:2 * n], refs[2 * n:3 * n], refs[4 * n:]):
            own = g_ref[...] + r1_ref[...]
            o_ref[...] = ((own + r2_ref[0].astype(F32)) + r2_ref[1].astype(F32)) + r2_ref[2].astype(F32)

    blk = lambda r: (None, r.shape[1] // steps, r.shape[2])
    grid_spec = pltpu.PrefetchScalarGridSpec(
        num_scalar_prefetch=1, grid=(steps,),
        in_specs=[pl.BlockSpec(blk(r), lambda i, p: (p[1], p[0] * steps + i, 0)) for r in r1s]
        + [pl.BlockSpec(blk(r), lambda i, p: (p[1], i, 0)) for r in r1s]
        + [pl.BlockSpec((3,) + blk(r)[1:], lambda i, p: (0, i, 0)) for r in r1s] + [_any()] * n,
        out_specs=[pl.BlockSpec(blk(r), lambda i, p: (layer, p[0] * steps + i, 0)) for r in r1s])
    return pl.pallas_call(body, grid_spec=grid_spec, out_shape=[jax.ShapeDtypeStruct(t.shape, F32) for t in totals],
                          input_output_aliases={1 + 3 * n + k: k for k in range(n)},
                          compiler_params=_cp(("arbitrary",)), name="add_chips")(place, *gs, *r1s, *r2s, *totals)


def _share_copies(bufs, send, recv):
    x, y, c, _ = _place()
    out = []
    for k, buf in enumerate(bufs):
        sems = dict(send_sem=send.at[k], recv_sem=recv.at[k], device_id=(x, y, 1 - c), device_id_type=MESH)
        mine = buf.at[:, _half(buf.shape[1], c), :]
        theirs = buf.at[:, _half(buf.shape[1], 1 - c), :]
        out.append((pltpu.make_async_remote_copy(src_ref=mine, dst_ref=mine, **sems),
                    pltpu.make_async_remote_copy(src_ref=theirs, dst_ref=theirs, **sems)))
    return out


def share_start(bufs, tag):
    n = len(bufs)

    def body(*refs):
        ins = refs[:n]
        send, recv = refs[n], refs[n + 1]
        token = refs[-1]
        for start, _ in _share_copies(ins, send, recv):
            start.start()
        token[...] = jnp.zeros_like(token)

    sems = pltpu.SemaphoreType.DMA((n,))
    res = pl.pallas_call(
        body, name=f"share_start_{tag}", in_specs=[HBM_SPEC] * n,
        out_specs=[SEM_SPEC, SEM_SPEC] + [HBM_SPEC] * n + [pl.BlockSpec(memory_space=pltpu.VMEM)],
        out_shape=[sems, sems] + [pltpu.HBM(b.shape, b.dtype) for b in bufs] + [jax.ShapeDtypeStruct((8, LANES), F32)],
        input_output_aliases={k: 2 + k for k in range(n)}, compiler_params=IN_FLIGHT,
    )(*[_in_hbm(b) for b in bufs])
    return res[0], res[1], res[2:2 + n], res[-1]


def share_wait(send, recv, bufs, after, tag):
    n = len(bufs)

    def body(*refs):
        ins = refs[:n]
        send_ref, recv_ref = refs[n], refs[n + 1]
        for start, arrival in _share_copies(ins, send_ref, recv_ref):
            start.wait_send()
            arrival.wait_recv()

    return pl.pallas_call(
        body, name=f"share_wait_{tag}",
        in_specs=[HBM_SPEC] * n + [SEM_SPEC, SEM_SPEC, _any()], out_specs=[HBM_SPEC] * n,
        out_shape=[pltpu.HBM(b.shape, b.dtype) for b in bufs],
        input_output_aliases={k: k for k in range(n)}, compiler_params=IN_FLIGHT,
    )(*bufs, send, recv, after)


def small_allreduce(v, after=()):
    rows = v.shape[0]
    flips = [(fx, fy, fc) for fx in (0, 1) for fy in (0, 1) for fc in (0, 1)][1:]

    def body(v_ref, o_ref, buf, send, recv):
        x, y, c, _ = _place()
        buf[4 * x + 2 * y + c] = v_ref[...]
        peers = [(jnp.where(fx, 1 - x, x), jnp.where(fy, 1 - y, y), jnp.where(fc, 1 - c, c)) for fx, fy, fc in flips]
        cps = []
        for k, peer in enumerate(peers):
            cp = pltpu.make_async_remote_copy(
                src_ref=v_ref, dst_ref=buf.at[4 * x + 2 * y + c], send_sem=send.at[k], recv_sem=recv.at[k],
                device_id=peer, device_id_type=MESH)
            cp.start()
            cps.append(cp)
        for k, (px, py, pc) in enumerate(peers):
            pltpu.make_async_remote_copy(
                src_ref=v_ref, dst_ref=buf.at[4 * px + 2 * py + pc], send_sem=send.at[k], recv_sem=recv.at[k],
                device_id=(px, py, pc), device_id_type=MESH).wait_recv()
        for cp in cps:
            cp.wait_send()
        acc = buf[0]
        for s in range(1, 8):
            acc = acc + buf[s]
        o_ref[...] = acc

    vm = pl.BlockSpec(memory_space=pltpu.VMEM)
    return pl.pallas_call(
        _behind(body, 1, after), in_specs=[vm] + [_any()] * len(after), out_specs=vm,
        out_shape=jax.ShapeDtypeStruct((rows, SMALL_COLS), F32),
        scratch_shapes=[pltpu.VMEM((8, rows, SMALL_COLS), F32), pltpu.SemaphoreType.DMA((7,)),
                        pltpu.SemaphoreType.DMA((7,))],
        name="reduce_small")(v, *after)


def adamw(w, g, m, v, rb, name, after=()):
    nl, rows, cols = w.shape

    def body(w_ref, g_ref, m_ref, v_ref, go_ref, d_ref, nm_ref, nv_ref):
        gv = g_ref[...]
        go_ref[...] = gv
        nm = ADAM_B1 * m_ref[...] + (1.0 - ADAM_B1) * gv
        nv = ADAM_B2 * v_ref[...] + (1.0 - ADAM_B2) * (gv * gv)
        m_hat = nm / (1.0 - ADAM_B1 ** ADAM_STEP)
        v_hat = nv / (1.0 - ADAM_B2 ** ADAM_STEP)
        d_ref[...] = -ADAM_LR * (m_hat / (jnp.sqrt(v_hat) + ADAM_EPS) + ADAM_WD * w_ref[...])
        nm_ref[...] = nm
        nv_ref[...] = nv

    blk = pl.BlockSpec((None, rb, cols), lambda l, r: (l, r, 0))
    shp = jax.ShapeDtypeStruct(w.shape, F32)
    return pl.pallas_call(_behind(body, 4, after), grid=(nl, rows // rb), in_specs=[blk] * 4 + [_any()] * len(after),
                          out_specs=[blk] * 4, out_shape=[shp] * 4,
                          compiler_params=_cp(("arbitrary", "arbitrary")), name=name)(w, g, m, v, *after)


def _pack(parts, rows):
    flat = jnp.concatenate([p.reshape(-1).astype(F32) for p in parts])
    return jnp.pad(flat, (0, rows * SMALL_COLS - flat.shape[0])).reshape(rows, SMALL_COLS)


def _unpack(vec, shapes):
    flat = vec.reshape(-1)
    out, off = [], 0
    for s in shapes:
        size = 1
        for d in s:
            size *= d
        out.append(flat[off:off + size].reshape(s))
        off += size
    return out


def kernel(x, w_in, w_conv, rel_bias, g_conv_out, g_attn_out, w_out, g_pre_mix, g_post_mix, g_pre_ffn, g_post_ffn, w_ffn_in, w_ffn_out, loss_target, m_w_in, m_w_conv, m_rel_bias, m_g_conv_out, m_g_attn_out, m_w_out, m_g_pre_mix, m_g_post_mix, m_g_pre_ffn, m_g_post_ffn, m_w_ffn_in, m_w_ffn_out, v_w_in, v_w_conv, v_rel_bias, v_g_conv_out, v_g_attn_out, v_w_out, v_g_pre_mix, v_g_post_mix, v_g_pre_ffn, v_g_post_ffn, v_w_ffn_in, v_w_ffn_out):
    xi, yi, ci = lax.axis_index("x"), lax.axis_index("y"), lax.axis_index("c")
    chip = 2 * xi + yi
    nl = w_in.shape[0]
    x0 = x[0]
    target = loss_target[0]
    cwl = CW // NCHIP

    chip1 = chip.reshape(1).astype(jnp.int32)
    big_weights = [w_in, w_out, w_ffn_in, w_ffn_out]
    own = [cast_to_slot(big_weights, chip1, 0)]
    wc_mine = jnp.pad(w_conv.reshape(-1), (0, 16 * LANES - w_conv.size)).reshape(1, 16, LANES)
    wc_slot = lax.dynamic_update_slice_in_dim(jnp.zeros((NCHIP, 16, LANES), F32), wc_mine, chip, axis=0)
    gm = jnp.kron(jnp.eye(CW // HD, dtype=F32), jnp.full((HD, HD), 1.0 / HD, F32)).astype(BF16)
    row = lambda a, l: a[l][None, :]

    def gather_finish(flight, after, tag):
        send, recv, bufs, _ = flight
        return gather_forward(gather_wait(send, recv, bufs, after, tag))

    first_mix = gather_start(list(own[0][:2]) + [wc_slot], x0, "0m")
    first_ffn = gather_start(own[0][2:], first_mix[3], "0f")
    chain = first_ffn[3]
    biases = []
    for l in range(nl):
        biases.append(bias_expand(_diag_vector(rel_bias[l]), (QG_FWD, QG_BWD), [chain]))
        chain = biases[l][1]
    for l in range(1, nl):
        own.append(cast_to_slot(big_weights, chip1, l, [chain]))
        chain = own[l][0]
    gw_in, gw_out, wc_all = gather_finish(first_mix, chain, "0m")
    wc_full = wc_all.reshape(NCHIP, -1)[:, :nl * cwl * 3].reshape(NCHIP, nl, cwl, 3)
    wc_full = jnp.transpose(wc_full, (1, 0, 2, 3)).reshape(nl, CW, 3)
    wconv_t = jnp.pad(jnp.transpose(wc_full, (0, 2, 1)), ((0, 0), (0, 5), (0, 0)))
    flights, to_sibling = {}, None
    saved, weights = [], []
    h = x0
    for l in range(nl):
        if l == 0:
            pass
        elif l == 1:
            flights[2] = gather_start(own[2], h, 2)
            gw_in, gw_out, gw_fi, gw_fo = gather_finish(flights[l], flights[2][3], l)
        else:
            gw_in, gw_out, gw_fi, gw_fo = forward_wait(*to_sibling[:3], h, l)
        gw_out = gw_out.reshape(D, D)
        behind_mix, behind_ffn = ([first_ffn[3]] if l == 0 else []), []
        if l + 1 < nl and l + 1 not in flights:
            flights[l + 1] = gather_start(own[l + 1], first_ffn[3] if l == 0 else gw_in, l + 1)
            behind_mix.append(flights[l + 1][3])
        bias2, bias2_bwd = biases[l]
        proj = fwd_inproj(h, row(g_pre_mix, l), gw_in, behind_mix)
        xmid, o, lse, y, z = fwd_mix(h, proj, bias2, wconv_t[l], row(g_conv_out, l), row(g_attn_out, l),
                                     row(g_post_mix, l), gm, gw_out)
        if l == 0:
            gw_fi, gw_fo = gather_finish(first_ffn, xmid, "0f")
        elif l + 1 < nl:
            send, recv, bufs, _ = flights[l + 1]
            landed = gather_wait(send, recv, bufs, xmid, l + 1)
            to_sibling = forward_start(landed, l + 1)
            behind_ffn.append(to_sibling[3])
            if l + 2 < nl:
                flights[l + 2] = gather_start(own[l + 2], to_sibling[3], l + 2)
                behind_ffn.append(flights[l + 2][3])
        gw_fo = gw_fo.reshape(2, DFF // 2, D)
        gu, f, xout = fwd_ffn(xmid, row(g_pre_ffn, l), row(g_post_ffn, l), gw_fi, gw_fo, behind_ffn)
        saved.append((h, proj, bias2_bwd, xmid, o, lse, y, z, gu, f))
        weights.append((gw_in, gw_out, gw_fi, gw_fo))
        h = xout
    dx, loss_blk = loss_head(h, target)

    core = ci.reshape(1).astype(jnp.int32)
    place = jnp.stack([ci, chip]).astype(jnp.int32)
    totals = [lax.empty(w.shape, F32) for w in (w_in, w_out, w_ffn_in, w_ffn_out)]
    small = {k: [None] * nl for k in ("co", "ao", "pm", "qm", "pf", "qf", "rel", "wc")}

    def reduce_begin(kinds, grads, tag):
        return kinds, exchange_start(grads, tag), tag

    def reduce_mid(state, after):
        kinds, (send, recv, srcs, lands, _), tag = state
        grads, from_sibling = exchange_wait(send, recv, srcs, lands, after, tag)
        return kinds, grads, from_sibling, scatter_start(add_pair(grads, from_sibling, core), tag), tag

    def reduce_end(state, after, totals, layer):
        kinds, grads, from_sibling, (send, recv, srcs, lands, _), tag = state
        from_chips = scatter_wait(send, recv, srcs, lands, after, tag)
        totals = list(totals)
        summed = add_chips(grads, from_sibling, from_chips, place, [totals[i] for i in kinds], layer)
        for i, t in zip(kinds, summed):
            totals[i] = t
        return totals

    begun = flying = None
    for l in reversed(range(nl)):
        hin, proj, bias2, xmid, o, lse, y, z, gu, f = saved[l]
        gw_in, gw_out, gw_fi, gw_fo = weights[l]
        behind_ffn = [begun[1][4]] if begun is not None else []
        dxm, dfb, act, dgu, h2, dg_qf, dg_pf = bwd_ffn(dx, f, xmid, gu, row(g_pre_ffn, l), row(g_post_ffn, l),
                                                        gw_fi, gw_fo, behind_ffn)
        behind_mix, behind_conv = [], []
        if begun is not None:
            flying = reduce_mid(begun, dxm)
            behind_mix.append(flying[3][4])
        gr_fo = wgrad(act, dfb, 256, D, False, "wgrad_ffn_out").reshape(NCHIP, DFF // NCHIP, D)
        gr_fi = wgrad(h2, dgu, 512, 2 * DFF // NCHIP, True, "wgrad_ffn_in")
        if l == 0:
            begun_ffn = reduce_begin([2, 3], [gr_fi, gr_fo], "0f")
            behind_mix.append(begun_ffn[1][4])
        dzb, do, dco, dbg, dg_qm, dg_co, dg_ao = bwd_mix(dxm, z, o, proj, wconv_t[l], row(g_conv_out, l),
                                                          row(g_attn_out, l), row(g_post_mix, l), gm, gw_out,
                                                          behind_mix)
        if l == 0:
            flying_ffn = reduce_mid(begun_ffn, dzb)
            behind_conv.append(flying_ffn[3][4])
        gr_out = wgrad(y, dzb, 512, D, False, "wgrad_out").reshape(NCHIP, D // NCHIP, D)
        dhc, dcg, dwc = bwd_conv(dco, proj, wconv_t[l], behind_conv)
        dq, dk, dv, db2 = bwd_attn(proj, o, do, lse, bias2)
        dx, dproj, hb, dg_pm = bwd_inproj(dxm, hin, dhc, dbg, dcg, dq, dk, dv, row(g_pre_mix, l), gw_in)
        if flying is not None:
            totals = reduce_end(flying, dx, totals, l + 1)
        gr_in = wgrad(hb, dproj, 512, PROJ // NCHIP, True, "wgrad_in")
        small["co"][l], small["ao"][l], small["pm"][l], small["qm"][l] = dg_co, dg_ao, dg_pm, dg_qm
        small["pf"][l], small["qf"][l] = dg_pf, dg_qf
        small["rel"][l] = _diag_vector_bwd(bias_reduce(db2.reshape(NH, QG_BWD, QG_BWD + LEFT)))
        small["wc"][l] = jnp.transpose(dwc[0:3], (1, 0))
        if l > 0:
            begun = reduce_begin([0, 1, 2, 3], [gr_in, gr_out, gr_fi, gr_fo], l)
    begun_mix = reduce_begin([0, 1], [gr_in, gr_out], "0m")
    totals = reduce_end(flying_ffn, begun_mix[1][4], totals, 0)
    flying_mix = reduce_mid(begun_mix, totals[2])
    share_ffn = share_start(totals[2:], "ffn")

    order = ("co", "ao", "pm", "qm", "pf", "qf", "rel", "wc")
    parts = [jnp.stack(small[k]) for k in order] + [loss_blk[0:1, 0:1]]
    shapes = [p.shape for p in parts]
    red_vec = small_allreduce(_pack(parts, 40), [share_ffn[3], flying_mix[3][4]])
    red = _unpack(red_vec, shapes)

    gr_fi, gr_fo = share_wait(*share_ffn[:3], red_vec, "ffn")
    big_fi = adamw(w_ffn_in, gr_fi, m_w_ffn_in, v_w_ffn_in, w_ffn_in.shape[1] // 4, "adamw_ffn_in")
    totals = reduce_end(flying_mix, big_fi[1], totals, 0)
    share_mix = share_start(totals[:2], "mix")
    big_fo = adamw(w_ffn_out, gr_fo, m_w_ffn_out, v_w_ffn_out, w_ffn_out.shape[1] // 4, "adamw_ffn_out",
                   [share_mix[3]])
    gr_in, gr_out = share_wait(*share_mix[:3], big_fo[1], "mix")
    big_in = adamw(w_in, gr_in, m_w_in, v_w_in, w_in.shape[1] // 4, "adamw_in")
    big_out = adamw(w_out, gr_out, m_w_out, v_w_out, w_out.shape[1] // 4, "adamw_out")
    big = [big_in, big_out, big_fi, big_fo]
    gr_co, gr_ao, gr_pm, gr_qm, gr_pf, gr_qf, gr_rel, gr_wc_full, loss = red
    gr_co, gr_ao, gr_pm, gr_qm, gr_pf, gr_qf = [a.reshape(nl, -1) for a in (gr_co, gr_ao, gr_pm, gr_qm, gr_pf, gr_qf)]
    gr_wc = lax.dynamic_slice_in_dim(gr_wc_full, chip * cwl, cwl, axis=1)
    loss = loss.reshape(())

    sw = [g_conv_out, g_attn_out, g_pre_mix, g_post_mix, g_pre_ffn, g_post_ffn, rel_bias, w_conv]
    sg = [gr_co, gr_ao, gr_pm, gr_qm, gr_pf, gr_qf, gr_rel, gr_wc]
    sm = [m_g_conv_out, m_g_attn_out, m_g_pre_mix, m_g_post_mix, m_g_pre_ffn, m_g_post_ffn, m_rel_bias, m_w_conv]
    sv = [v_g_conv_out, v_g_attn_out, v_g_pre_mix, v_g_post_mix, v_g_pre_ffn, v_g_post_ffn, v_rel_bias, v_w_conv]
    sshapes = [a.shape for a in sw]
    packed = [_pack(a, 32)[None] for a in (sw, sg, sm, sv)]
    s_out = [_unpack(a[0], sshapes) for a in adamw(*packed, 32, "adamw_small")]

    def leaves(big_i, small_i):
        b_in, b_out, b_fi, b_fo = big_i
        s_co, s_ao, s_pm, s_qm, s_pf, s_qf, s_rel, s_wc = small_i
        return [b_in, s_wc, s_rel, s_co, s_ao, b_out, s_pm, s_qm, s_pf, s_qf, b_fi, b_fo]

    out = [loss, dx[None]]
    out += leaves([b[0] for b in big], sg)
    for i in range(1, 4):
        out += leaves([b[i] for b in big], s_out[i])
    return tuple(out)
```

```python
import jax
import jax.numpy as jnp
from jax import lax
from jax.experimental import pallas as pl
from jax.experimental.pallas import tpu as pltpu

F32 = jnp.float32
BF16 = jnp.bfloat16

D = 1024
PROJ = 3072
CW = 512
HD = 64
NH = 8
CHUNK = 64
BAND = 576
REL_CLIP = 128
NREL = 2 * REL_CLIP + 1
DFF = 2816
DEPTH = 4
NCHIP = 4
EPS = 1e-6
NEG_INF = -1e30

ADAM_LR = 0.001
ADAM_B1 = 0.9
ADAM_B2 = 0.999
ADAM_EPS = 1e-08
ADAM_WD = 0.01
ADAM_STEP = 10

V7X_VMEM_BYTES = 64 * 1024 * 1024
VMEM_LIMIT = V7X_VMEM_BYTES - 8 * 1024 * 1024
LANES = 128
QG_FWD = 4 * CHUNK
QG_BWD = 2 * CHUNK
LEFT = BAND - CHUNK
TQ = 512
TM = 256
SMALL_COLS = 1024
MESH = pl.DeviceIdType.MESH
NT = (((1,), (1,)), ((), ()))
TN = (((0,), (0,)), ((), ()))


def _cp(sem=None, vmem=VMEM_LIMIT):
    return pltpu.CompilerParams(dimension_semantics=sem, vmem_limit_bytes=vmem)


def _any():
    return pl.BlockSpec(memory_space=pl.ANY)


def _const(shape):
    nd = len(shape)
    return pl.BlockSpec(shape, lambda *_: (0,) * nd)


def _behind(body, n_in, after):
    def ordered(*refs):
        return body(*refs[:n_in], *refs[n_in + len(after):])
    return ordered


def _rms(v, g):
    r = lax.rsqrt(jnp.mean(v * v, axis=-1, keepdims=True) + EPS)
    return v * r * g


def _rms_bwd(dy, v, g):
    r = lax.rsqrt(jnp.mean(v * v, axis=-1, keepdims=True) + EPS)
    vh = v * r
    dg = jnp.sum(dy * vh, axis=0, keepdims=True)
    dvh = dy * g
    dv = r * (dvh - vh * jnp.mean(dvh * vh, axis=-1, keepdims=True))
    return dv, dg


def _group_mean(v, gm):
    return jnp.dot(v.astype(BF16), gm, preferred_element_type=F32)


def _group_rms_bwd(dy, v, g, gm):
    r = lax.rsqrt(_group_mean(v * v, gm) + EPS)
    vh = v * r
    dg = jnp.sum(dy * vh, axis=0, keepdims=True)
    dvh = dy * g
    dv = r * (dvh - vh * _group_mean(dvh * vh, gm))
    return dv, dg


def _head_masks(scale):
    lane = lax.broadcasted_iota(jnp.int32, (1, LANES), 1)
    return [jnp.where((lane >= HD * a) & (lane < HD * (a + 1)), scale, 0.0).astype(BF16) for a in range(2)]


class _Resident:
    def __init__(self, src, dst, sem):
        self.first = pl.program_id(0) == 0
        self.copy = pltpu.make_async_copy(src, dst, sem)
        self.dst = dst

        @pl.when(self.first)
        def _():
            self.copy.start()

    def read(self):
        @pl.when(self.first)
        def _():
            self.copy.wait()

        return self.dst[...]


FF_CHUNKS = ((0, 1536), (1536, DFF))


def _stream_ffn_weights(wfi_hbm, wfo_hbm, wfi_v, wfo_v, sems, order, step):
    hw = DFF // 2
    per_matrix = {
        0: [(wfi_hbm.at[j], wfi_v.at[0, :, pl.ds(hw * j, hw)]) for j in range(2)],
        1: [(wfi_hbm.at[2 + j], wfi_v.at[1, :, pl.ds(hw * j, hw)]) for j in range(2)],
        2: [(wfo_hbm.at[j], wfo_v.at[pl.ds(hw * j, hw), :]) for j in range(2)],
    }
    pieces = [p for m in order for p in per_matrix[m]]
    slot = {m: 2 * k for k, m in enumerate(order)}

    def make_step(wait):
        def ready(m, chunk):
            if chunk == 0:
                wait(slot[m])
                wait(slot[m] + 1)
        return lambda: step(ready)

    copies = [pltpu.make_async_copy(src, dst, sems.at[k]) for k, (src, dst) in enumerate(pieces)]
    first = pl.program_id(0) == 0

    @pl.when(first)
    def _():
        for cp in copies:
            cp.start()
        make_step(lambda k: copies[k].wait())()

    @pl.when(jnp.logical_not(first))
    def _():
        make_step(lambda k: None)()


def _conv_taps(u_prev, u, scr):
    n = u.shape[0]
    scr[0:16, :] = u_prev
    scr[16:16 + n, :] = u
    return scr[15:15 + n, :], scr[14:14 + n, :]


def fwd_inproj(x, g, w_all, after=()):
    t = x.shape[0]
    wc = PROJ // NCHIP

    def body(x_ref, g_ref, w_hbm, o_ref, w_v):
        @pl.when(pl.program_id(0) == 0)
        def _():
            pltpu.sync_copy(w_hbm, w_v)

        h = _rms(x_ref[...], g_ref[...]).astype(BF16)
        for b in range(NCHIP):
            o_ref[:, wc * b:wc * (b + 1)] = jnp.dot(h, w_v[b], preferred_element_type=F32).astype(BF16)

    return pl.pallas_call(
        _behind(body, 3, after), grid=(t // TQ,),
        in_specs=[pl.BlockSpec((TQ, D), lambda i: (i, 0)), _const((1, D)), _any()] + [_any()] * len(after),
        out_specs=pl.BlockSpec((TQ, PROJ), lambda i: (i, 0)),
        out_shape=jax.ShapeDtypeStruct((t, PROJ), BF16),
        scratch_shapes=[pltpu.VMEM((NCHIP, D, wc), BF16)],
        compiler_params=_cp(("arbitrary",)), name="fwd_inproj")(x, g, w_all, *after)


def _attn_window_specs():
    return [
        pl.BlockSpec((TQ, CW), lambda i: (i, 3)),
        pl.BlockSpec((TQ, CW), lambda i: (jnp.maximum(i - 1, 0), 4)),
        pl.BlockSpec((TQ, CW), lambda i: (i, 4)),
        pl.BlockSpec((TQ, CW), lambda i: (jnp.maximum(i - 1, 0), 5)),
        pl.BlockSpec((TQ, CW), lambda i: (i, 5)),
    ]


def _conv_specs():
    return [
        pl.BlockSpec((TQ, 3 * CW), lambda i: (i, 0)),
        pl.BlockSpec((16, 3 * CW), lambda i: (jnp.maximum(i * (TQ // 16) - 1, 0), 0)),
    ]


def _conv_fwd(pc_ref, pcp_ref, wc_ref, scr, first):
    pc = pc_ref[...].astype(F32)
    hc, bg, cg = pc[:, :CW], pc[:, CW:2 * CW], pc[:, 2 * CW:]
    u = cg * hc
    pp = pcp_ref[...].astype(F32)
    u_prev = jnp.where(first, 0.0, pp[:, 2 * CW:] * pp[:, :CW])
    u1, u2 = _conv_taps(u_prev, u, scr)
    cout = wc_ref[0:1, :] * u2 + wc_ref[1:2, :] * u1 + wc_ref[2:3, :] * u
    return hc, bg, cg, u, u1, u2, cout


def _key_penalty(first, r0, kg):
    col = lax.broadcasted_iota(jnp.int32, (1, kg), 1)
    limit = jnp.where(first, TQ - r0, 0)
    return jnp.where(col < limit, NEG_INF, 0.0)


def fwd_mix(x, proj, bias2, wconv_t, g_co, g_ao, g_pm, gm, wout_all):
    t = x.shape[0]
    qg, kg = QG_FWD, QG_FWD + LEFT

    def body(x_ref, pc_ref, pcp_ref, q_ref, kp_ref, kc_ref, vp_ref, vc_ref, b2_ref, wc_ref, gco_ref, gao_ref, gpm_ref,
             gm_ref, wout_hbm, xmid_ref, o_ref, lse_ref, y_ref, z_ref, wout_v, kwin, vwin, cscr, sems):
        i = pl.program_id(0)
        first = i == 0
        wout = _Resident(wout_hbm, wout_v, sems.at[0])
        kwin[0:TQ, :] = kp_ref[...]
        kwin[TQ:2 * TQ, :] = kc_ref[...]
        vwin[0:TQ, :] = vp_ref[...]
        vwin[TQ:2 * TQ, :] = vc_ref[...]
        qmask = _head_masks(HD ** -0.5)
        low = lax.broadcasted_iota(jnp.int32, (1, LANES), 1) < HD

        def group(g, carry):
            r0 = pl.multiple_of(g * qg, qg)
            pen = _key_penalty(first, r0, kg)
            for hp in range(NH // 2):
                ls = slice(LANES * hp, LANES * (hp + 1))
                qb = q_ref[pl.ds(r0, qg), ls]
                q2 = jnp.concatenate([qb * qmask[0], qb * qmask[1]], axis=0)
                s = lax.dot_general(q2, kwin[pl.ds(r0, kg), ls], NT, preferred_element_type=F32)
                s = s + b2_ref[hp] + pen
                m = jnp.max(s, axis=-1, keepdims=True)
                p = jnp.exp(s - m)
                l = jnp.sum(p, axis=-1, keepdims=True)
                o2 = jnp.dot(p.astype(BF16), vwin[pl.ds(r0, kg), ls], preferred_element_type=F32) * (1.0 / l)
                lse2 = m + jnp.log(l)
                o_ref[pl.ds(r0, qg), ls] = jnp.where(low, o2[:qg], o2[qg:])
                lse_ref[pl.ds(r0, qg), ls] = jnp.where(low, lse2[:qg], lse2[qg:])
            return carry

        lax.fori_loop(0, TQ // qg, group, 0)

        _, bg, _, _, _, _, cout = _conv_fwd(pc_ref, pcp_ref, wc_ref, cscr, first)
        yc = bg * cout
        gmv = gm_ref[...]
        ycn = yc * lax.rsqrt(_group_mean(yc * yc, gmv) + EPS) * gco_ref[...]
        oa = o_ref[...]
        oan = oa * lax.rsqrt(_group_mean(oa * oa, gmv) + EPS) * gao_ref[...]
        y_ref[:, 0:CW] = ycn.astype(BF16)
        y_ref[:, CW:2 * CW] = oan.astype(BF16)
        z = jnp.dot(y_ref[...], wout.read(), preferred_element_type=F32)
        z_ref[...] = z
        xmid_ref[...] = x_ref[...] + _rms(z, gpm_ref[...])

    row = lambda w: pl.BlockSpec((TQ, w), lambda i: (i, 0))
    return pl.pallas_call(
        body, grid=(t // TQ,),
        in_specs=[row(D)] + _conv_specs() + _attn_window_specs() + [
            _const((NH // 2, 2 * qg, kg)), _const((8, CW)), _const((1, CW)), _const((1, CW)), _const((1, D)),
            _const((CW, CW)), _any()],
        out_specs=[row(D), row(CW), row(CW), row(D), row(D)],
        out_shape=[jax.ShapeDtypeStruct((t, D), F32), jax.ShapeDtypeStruct((t, CW), F32),
                   jax.ShapeDtypeStruct((t, CW), F32), jax.ShapeDtypeStruct((t, D), BF16),
                   jax.ShapeDtypeStruct((t, D), F32)],
        scratch_shapes=[pltpu.VMEM((D, D), BF16), pltpu.VMEM((2 * TQ, CW), BF16), pltpu.VMEM((2 * TQ, CW), BF16),
                        pltpu.VMEM((TQ + 16, CW), F32), pltpu.SemaphoreType.DMA((1,))],
        compiler_params=_cp(("arbitrary",)), name="fwd_mix",
    )(x, proj, proj, proj, proj, proj, proj, proj, bias2, wconv_t, g_co, g_ao, g_pm, gm, wout_all)


def fwd_ffn(xmid, g_pre, g_post, wfi_all, wfo_all, after=(), target=None):
    t = xmid.shape[0]
    n_in = 5 if target is None else 6

    def body(*refs):
        x_ref, gpre_ref, gpost_ref, wfi_hbm, wfo_hbm = refs[:5]
        t_ref = None if target is None else refs[5]
        gu_ref, f_ref, xo_ref = refs[n_in:n_in + 3]
        l_ref = None if target is None else refs[n_in + 3]
        wfi_v, wfo_v, sems = refs[-3:]

        if target is not None:
            @pl.when(pl.program_id(0) == 0)
            def _():
                l_ref[...] = jnp.zeros_like(l_ref)

        def step(ready):
            xv = x_ref[...]
            h = _rms(xv, gpre_ref[...]).astype(BF16)
            f = jnp.zeros((TM, D), F32)
            for ci, (a, b) in enumerate(FF_CHUNKS):
                ready(0, ci)
                gate = jnp.dot(h, wfi_v[0, :, a:b], preferred_element_type=F32)
                ready(1, ci)
                up = jnp.dot(h, wfi_v[1, :, a:b], preferred_element_type=F32)
                gu_ref[:, a:b] = gate.astype(BF16)
                gu_ref[:, DFF + a:DFF + b] = up.astype(BF16)
                act = gate * (1.0 / (1.0 + jnp.exp(-gate))) * up
                ready(2, ci)
                f = f + jnp.dot(act.astype(BF16), wfo_v[a:b, :], preferred_element_type=F32)
            f_ref[...] = f
            xo = xv + _rms(f, gpost_ref[...])
            if target is None:
                xo_ref[...] = xo
            else:
                e = xo - t_ref[...]
                xo_ref[...] = e * (1.0 / D)
                rows = jnp.sum(e * e, axis=-1, keepdims=True) * (1.0 / D)
                l_ref[...] += 0.5 * jnp.sum(rows, axis=0, keepdims=True)

        _stream_ffn_weights(wfi_hbm, wfo_hbm, wfi_v, wfo_v, sems, (0, 1, 2), step)

    row = lambda w: pl.BlockSpec((TM, w), lambda i: (i, 0))
    with_loss = target is not None
    return pl.pallas_call(
        _behind(body, n_in, after), grid=(t // TM,),
        in_specs=[row(D), _const((1, D)), _const((1, D)), _any(), _any()] + [row(D)] * with_loss
        + [_any()] * len(after),
        out_specs=[row(2 * DFF), row(D), row(D)] + [_const((8, LANES))] * with_loss,
        out_shape=[jax.ShapeDtypeStruct((t, 2 * DFF), BF16), jax.ShapeDtypeStruct((t, D), F32),
                   jax.ShapeDtypeStruct((t, D), F32)] + [jax.ShapeDtypeStruct((8, LANES), F32)] * with_loss,
        scratch_shapes=[pltpu.VMEM((2, D, DFF), BF16), pltpu.VMEM((DFF, D), BF16), pltpu.SemaphoreType.DMA((6,))],
        compiler_params=_cp(("arbitrary",)), name="fwd_ffn_loss" if with_loss else "fwd_ffn",
    )(xmid, g_pre, g_post, wfi_all, wfo_all, *([target] * with_loss), *after)


def bwd_ffn(dx, f, xmid, gu, g_pre, g_post, wfi_all, wfo_all, after=()):
    t = dx.shape[0]
    hw = DFF // 2

    def body(dx_ref, f_ref, x_ref, gu_ref, gpre_ref, gpost_ref, wfi_hbm, wfo_hbm,
             dxm_ref, df_ref, act_ref, dgu_ref, h_ref, dgpost_ref, dgpre_ref, wfi_v, wfo_v, sems):
        @pl.when(pl.program_id(0) == 0)
        def _():
            dgpost_ref[...] = jnp.zeros_like(dgpost_ref)
            dgpre_ref[...] = jnp.zeros_like(dgpre_ref)

        def step(ready):
            dxo = dx_ref[...]
            df, dgp = _rms_bwd(dxo, f_ref[...], gpost_ref[...])
            dgpost_ref[...] += dgp
            dfb = df.astype(BF16)
            df_ref[...] = dfb
            dh = jnp.zeros((TM, D), F32)
            for ci, (a, b) in enumerate(FF_CHUNKS):
                ready(2, ci)
                dact = lax.dot_general(dfb, wfo_v[a:b, :], NT, preferred_element_type=F32)
                gate = gu_ref[:, a:b].astype(F32)
                up = gu_ref[:, DFF + a:DFF + b].astype(F32)
                sig = 1.0 / (1.0 + jnp.exp(-gate))
                silu = gate * sig
                act_ref[:, a:b] = (silu * up).astype(BF16)
                dup = (dact * silu).astype(BF16)
                dgate = (dact * up * (sig * (1.0 + gate * (1.0 - sig)))).astype(BF16)
                dgu_ref[:, a:b] = dgate
                dgu_ref[:, DFF + a:DFF + b] = dup
                ready(0, ci)
                dh = dh + lax.dot_general(dgate, wfi_v[0, :, a:b], NT, preferred_element_type=F32)
                ready(1, ci)
                dh = dh + lax.dot_general(dup, wfi_v[1, :, a:b], NT, preferred_element_type=F32)
            xv = x_ref[...]
            gpre = gpre_ref[...]
            h_ref[...] = _rms(xv, gpre).astype(BF16)
            dxv, dgq = _rms_bwd(dh, xv, gpre)
            dgpre_ref[...] += dgq
            dxm_ref[...] = dxo + dxv

        _stream_ffn_weights(wfi_hbm, wfo_hbm, wfi_v, wfo_v, sems, (2, 0, 1), step)

    row = lambda w: pl.BlockSpec((TM, w), lambda i: (i, 0))
    return pl.pallas_call(
        _behind(body, 8, after), grid=(t // TM,),
        in_specs=[row(D), row(D), row(D), row(2 * DFF), _const((1, D)), _const((1, D)), _any(), _any()]
        + [_any()] * len(after),
        out_specs=[row(D), row(D), row(DFF), row(2 * DFF), row(D), _const((1, D)), _const((1, D))],
        out_shape=[jax.ShapeDtypeStruct((t, D), F32), jax.ShapeDtypeStruct((t, D), BF16),
                   jax.ShapeDtypeStruct((t, DFF), BF16), jax.ShapeDtypeStruct((t, 2 * DFF), BF16),
                   jax.ShapeDtypeStruct((t, D), BF16), jax.ShapeDtypeStruct((1, D), F32),
                   jax.ShapeDtypeStruct((1, D), F32)],
        scratch_shapes=[pltpu.VMEM((2, D, DFF), BF16), pltpu.VMEM((DFF, D), BF16), pltpu.SemaphoreType.DMA((6,))],
        compiler_params=_cp(("arbitrary",)), name="bwd_ffn")(dx, f, xmid, gu, g_pre, g_post, wfi_all, wfo_all, *after)


def bwd_mix(dxm, z, o, proj, wconv_t, g_co, g_ao, g_pm, gm, wout_all, after=()):
    t = dxm.shape[0]

    def body(dx_ref, z_ref, o_ref, pc_ref, pcp_ref, wc_ref, gco_ref, gao_ref, gpm_ref, gm_ref, wout_hbm,
             dz_ref, do_ref, dco_ref, dbg_ref, dgpm_ref, dgco_ref, dgao_ref, wout_v, cscr):
        first = pl.program_id(0) == 0

        @pl.when(first)
        def _():
            pltpu.sync_copy(wout_hbm, wout_v)
            dgpm_ref[...] = jnp.zeros_like(dgpm_ref)
            dgco_ref[...] = jnp.zeros_like(dgco_ref)
            dgao_ref[...] = jnp.zeros_like(dgao_ref)

        dz, dgp = _rms_bwd(dx_ref[...], z_ref[...], gpm_ref[...])
        dgpm_ref[...] += dgp
        dzb = dz.astype(BF16)
        dz_ref[...] = dzb
        gmv = gm_ref[...]
        _, bg, _, _, _, _, cout = _conv_fwd(pc_ref, pcp_ref, wc_ref, cscr, first)
        dy_conv = lax.dot_general(dzb, wout_v[0:CW, :], NT, preferred_element_type=F32)
        dyc, dgc = _group_rms_bwd(dy_conv, bg * cout, gco_ref[...], gmv)
        dgco_ref[...] += dgc
        dbg_ref[...] = (dyc * cout).astype(BF16)
        dco_ref[...] = dyc * bg
        dy_attn = lax.dot_general(dzb, wout_v[CW:2 * CW, :], NT, preferred_element_type=F32)
        do, dga = _group_rms_bwd(dy_attn, o_ref[...], gao_ref[...], gmv)
        dgao_ref[...] += dga
        do_ref[...] = do.astype(BF16)

    row = lambda w: pl.BlockSpec((TQ, w), lambda i: (i, 0))
    return pl.pallas_call(
        _behind(body, 11, after), grid=(t // TQ,),
        in_specs=[row(D), row(D), row(CW)] + _conv_specs() + [
            _const((8, CW)), _const((1, CW)), _const((1, CW)), _const((1, D)), _const((CW, CW)), _any()]
        + [_any()] * len(after),
        out_specs=[row(D), row(CW), row(CW), row(CW), _const((1, D)), _const((1, CW)), _const((1, CW))],
        out_shape=[jax.ShapeDtypeStruct((t, D), BF16), jax.ShapeDtypeStruct((t, CW), BF16),
                   jax.ShapeDtypeStruct((t, CW), F32), jax.ShapeDtypeStruct((t, CW), BF16),
                   jax.ShapeDtypeStruct((1, D), F32), jax.ShapeDtypeStruct((1, CW), F32),
                   jax.ShapeDtypeStruct((1, CW), F32)],
        scratch_shapes=[pltpu.VMEM((D, D), BF16), pltpu.VMEM((TQ + 16, CW), F32)],
        compiler_params=_cp(("arbitrary",)), name="bwd_mix",
    )(dxm, z, o, proj, proj, wconv_t, g_co, g_ao, g_pm, gm, wout_all, *after)


def bwd_conv(dco, proj, wconv_t, after=()):
    t = dco.shape[0]
    nt = t // TQ

    def body(d_ref, dn_ref, pc_ref, pcp_ref, wc_ref, dhc_ref, dcg_ref, dw_ref, cscr, dscr):
        i = pl.program_id(0)
        first = i == 0

        @pl.when(first)
        def _():
            dw_ref[...] = jnp.zeros_like(dw_ref)

        hc, _, cg, u, u1, u2, _ = _conv_fwd(pc_ref, pcp_ref, wc_ref, cscr, first)
        d0 = d_ref[...]
        dscr[0:TQ, :] = d0
        dscr[TQ:TQ + 8, :] = jnp.where(i == nt - 1, 0.0, dn_ref[...])
        d1 = dscr[1:TQ + 1, :]
        d2 = dscr[2:TQ + 2, :]
        du = wc_ref[2:3, :] * d0 + wc_ref[1:2, :] * d1 + wc_ref[0:1, :] * d2
        dhc_ref[...] = (du * cg).astype(BF16)
        dcg_ref[...] = (du * hc).astype(BF16)
        dw_ref[0:1, :] += jnp.sum(d0 * u2, axis=0, keepdims=True)
        dw_ref[1:2, :] += jnp.sum(d0 * u1, axis=0, keepdims=True)
        dw_ref[2:3, :] += jnp.sum(d0 * u, axis=0, keepdims=True)

    row = lambda w: pl.BlockSpec((TQ, w), lambda i: (i, 0))
    nxt = pl.BlockSpec((8, CW), lambda i: (jnp.minimum((i + 1) * (TQ // 8), t // 8 - 1), 0))
    return pl.pallas_call(
        _behind(body, 5, after), grid=(nt,),
        in_specs=[row(CW), nxt] + _conv_specs() + [_const((8, CW))] + [_any()] * len(after),
        out_specs=[row(CW), row(CW), _const((8, CW))],
        out_shape=[jax.ShapeDtypeStruct((t, CW), BF16), jax.ShapeDtypeStruct((t, CW), BF16),
                   jax.ShapeDtypeStruct((8, CW), F32)],
        scratch_shapes=[pltpu.VMEM((TQ + 16, CW), F32), pltpu.VMEM((TQ + 8, CW), F32)],
        compiler_params=_cp(("arbitrary",)), name="bwd_conv")(dco, dco, proj, proj, wconv_t, *after)


def bwd_attn(proj, o, do, lse, bias2):
    t = o.shape[0]
    nt = t // TQ
    qg, kg = QG_BWD, QG_BWD + LEFT
    nkb = (t + TQ) // LANES

    def body(q_ref, kp_ref, kc_ref, vp_ref, vc_ref, o_ref, do_ref, lse_ref, b2_ref,
             dq_ref, dk_hbm, dv_hbm, db_hbm, kwin, vwin, dk_acc, dv_acc, db_acc):
        i = pl.program_id(0)
        first = i == 0

        @pl.when(first)
        def _():
            dk_acc[...] = jnp.zeros_like(dk_acc)
            dv_acc[...] = jnp.zeros_like(dv_acc)
            db_acc[...] = jnp.zeros_like(db_acc)

        kwin[0:TQ, :] = kp_ref[...]
        kwin[TQ:2 * TQ, :] = kc_ref[...]
        vwin[0:TQ, :] = vp_ref[...]
        vwin[TQ:2 * TQ, :] = vc_ref[...]
        scale = HD ** -0.5
        qmask = _head_masks(scale)
        vmask = _head_masks(1.0)
        low = lax.broadcasted_iota(jnp.int32, (1, LANES), 1) < HD

        def group(g, carry):
            r0 = pl.multiple_of(g * qg, qg)
            base = i * (TQ // LANES) + g * (qg // LANES)
            pen = _key_penalty(first, r0, kg)
            for hp in range(NH // 2):
                ls = slice(LANES * hp, LANES * (hp + 1))
                qb = q_ref[pl.ds(r0, qg), ls]
                kw = kwin[pl.ds(r0, kg), ls]
                dob = do_ref[pl.ds(r0, qg), ls]
                prod = dob.astype(F32) * o_ref[pl.ds(r0, qg), ls]
                lseb = lse_ref[pl.ds(r0, qg), ls]
                q2 = jnp.concatenate([qb * qmask[0], qb * qmask[1]], axis=0)
                do2 = jnp.concatenate([dob * vmask[0], dob * vmask[1]], axis=0)
                lse2 = jnp.concatenate([lseb[:, 0:1], lseb[:, HD:HD + 1]], axis=0)
                dsum = jnp.concatenate([jnp.sum(jnp.where(low, prod, 0.0), axis=-1, keepdims=True),
                                        jnp.sum(jnp.where(low, 0.0, prod), axis=-1, keepdims=True)], axis=0)
                s = lax.dot_general(q2, kw, NT, preferred_element_type=F32) + b2_ref[hp] + pen
                p = jnp.exp(s - lse2)
                dp = lax.dot_general(do2, vwin[pl.ds(r0, kg), ls], NT, preferred_element_type=F32)
                ds = p * (dp - dsum)
                db_acc[hp] += ds
                dsb = ds.astype(BF16)
                dq2 = jnp.dot(dsb, kw, preferred_element_type=F32)
                dq_ref[pl.ds(r0, qg), ls] = (jnp.where(low, dq2[:qg], dq2[qg:]) * scale).astype(BF16)
                dkt = lax.dot_general(q2, dsb, TN, preferred_element_type=F32)
                dvt = lax.dot_general(do2, p.astype(BF16), TN, preferred_element_type=F32)
                for kb in range(kg // LANES):
                    dk_acc[base + kb, ls, :] += dkt[:, LANES * kb:LANES * (kb + 1)]
                    dv_acc[base + kb, ls, :] += dvt[:, LANES * kb:LANES * (kb + 1)]
            return carry

        lax.fori_loop(0, TQ // qg, group, 0)

        @pl.when(i == nt - 1)
        def _():
            pltpu.sync_copy(dk_acc, dk_hbm)
            pltpu.sync_copy(dv_acc, dv_hbm)
            pltpu.sync_copy(db_acc, db_hbm)

    row = lambda w: pl.BlockSpec((TQ, w), lambda i: (i, 0))
    return pl.pallas_call(
        body, grid=(nt,),
        in_specs=_attn_window_specs() + [row(CW), row(CW), row(CW), _const((NH // 2, 2 * qg, kg))],
        out_specs=[row(CW), _any(), _any(), _any()],
        out_shape=[jax.ShapeDtypeStruct((t, CW), BF16), jax.ShapeDtypeStruct((nkb, CW, LANES), F32),
                   jax.ShapeDtypeStruct((nkb, CW, LANES), F32), jax.ShapeDtypeStruct((NH // 2, 2 * qg, kg), F32)],
        scratch_shapes=[pltpu.VMEM((2 * TQ, CW), BF16), pltpu.VMEM((2 * TQ, CW), BF16),
                        pltpu.VMEM((nkb, CW, LANES), F32), pltpu.VMEM((nkb, CW, LANES), F32),
                        pltpu.VMEM((NH // 2, 2 * qg, kg), F32)],
        compiler_params=_cp(("arbitrary",)), name="bwd_attn",
    )(proj, proj, proj, proj, proj, o, do, lse, bias2)


def bwd_inproj(dxm, x, dhc, dbg, dcg, dq, dk, dv, g, w_all):
    t = x.shape[0]
    wc = PROJ // NCHIP

    def body(dxm_ref, x_ref, dhc_ref, dbg_ref, dcg_ref, dq_ref, dk_ref, dv_ref, g_ref, w_hbm,
             dx_ref, dp_ref, h_ref, dg_ref, w_v):
        @pl.when(pl.program_id(0) == 0)
        def _():
            pltpu.sync_copy(w_hbm, w_v)
            dg_ref[...] = jnp.zeros_like(dg_ref)

        dp_ref[:, 0:CW] = dhc_ref[...]
        dp_ref[:, CW:2 * CW] = dbg_ref[...]
        dp_ref[:, 2 * CW:3 * CW] = dcg_ref[...]
        dp_ref[:, 3 * CW:4 * CW] = dq_ref[...]
        for kb in range(TQ // LANES):
            rows = slice(LANES * kb, LANES * (kb + 1))
            dp_ref[rows, 4 * CW:5 * CW] = jnp.transpose(dk_ref[kb]).astype(BF16)
            dp_ref[rows, 5 * CW:6 * CW] = jnp.transpose(dv_ref[kb]).astype(BF16)
        dh = jnp.zeros((TQ, D), F32)
        for b in range(NCHIP):
            dh = dh + lax.dot_general(dp_ref[:, wc * b:wc * (b + 1)], w_v[b], NT, preferred_element_type=F32)
        xv = x_ref[...]
        gv = g_ref[...]
        h_ref[...] = _rms(xv, gv).astype(BF16)
        dxv, dgv = _rms_bwd(dh, xv, gv)
        dg_ref[...] += dgv
        dx_ref[...] = dxm_ref[...] + dxv

    row = lambda w: pl.BlockSpec((TQ, w), lambda i: (i, 0))
    pad = pl.BlockSpec((TQ // LANES, CW, LANES), lambda i: (i + 1, 0, 0))
    return pl.pallas_call(
        body, grid=(t // TQ,),
        in_specs=[row(D), row(D), row(CW), row(CW), row(CW), row(CW), pad, pad, _const((1, D)), _any()],
        out_specs=[row(D), row(PROJ), row(D), _const((1, D))],
        out_shape=[jax.ShapeDtypeStruct((t, D), F32), jax.ShapeDtypeStruct((t, PROJ), BF16),
                   jax.ShapeDtypeStruct((t, D), BF16), jax.ShapeDtypeStruct((1, D), F32)],
        scratch_shapes=[pltpu.VMEM((NCHIP, D, wc), BF16)],
        compiler_params=_cp(("arbitrary",)), name="bwd_inproj",
    )(dxm, x, dhc, dbg, dcg, dq, dk, dv, g, w_all)


def wgrad(a, b, kb, nb, by_columns, name):
    t, k = a.shape
    n = b.shape[1]
    tk = 512

    def body(a_ref, b_ref, o_ref):
        o_ref[...] = jnp.zeros_like(o_ref)
        for c in range(t // tk):
            o_ref[...] += lax.dot_general(a_ref[tk * c:tk * (c + 1), :], b_ref[tk * c:tk * (c + 1), :], TN,
                                          preferred_element_type=F32)

    if by_columns:
        assert nb == n // NCHIP
        out_spec = pl.BlockSpec((None, kb, nb), lambda ki, ni: (ni, ki, 0))
        out_shape = jax.ShapeDtypeStruct((NCHIP, k, nb), F32)
    else:
        assert nb == n
        out_spec = pl.BlockSpec((kb, nb), lambda ki, ni: (ki, 0))
        out_shape = jax.ShapeDtypeStruct((k, n), F32)
    return pl.pallas_call(
        body, grid=(k // kb, n // nb),
        in_specs=[pl.BlockSpec((t, kb), lambda ki, ni: (0, ki)), pl.BlockSpec((t, nb), lambda ki, ni: (0, ni))],
        out_specs=out_spec, out_shape=out_shape,
        compiler_params=_cp(("arbitrary", "arbitrary")), name=name)(a, b)


TOE = 1024
assert 2 * QG_FWD + LEFT <= TOE
N_FLAT = LEFT - REL_CLIP + 1
N_VAR = BAND - N_FLAT


def _diag_vector(table):
    last = table[:, 2 * REL_CLIP:]
    var = table[:, 2 * REL_CLIP - N_VAR:2 * REL_CLIP][:, ::-1]
    return jnp.concatenate([jnp.broadcast_to(last, (NH, N_FLAT)), var, jnp.broadcast_to(last, (NH, TOE - BAND))], axis=1)


def _diag_vector_bwd(dvec):
    dlast = jnp.sum(dvec[:, :N_FLAT], axis=1, keepdims=True) + jnp.sum(dvec[:, BAND:], axis=1, keepdims=True)
    dvar = dvec[:, N_FLAT:BAND][:, ::-1]
    return jnp.concatenate([jnp.zeros((NH, 2 * REL_CLIP - N_VAR), F32), dvar, dlast], axis=1)


def _band_valid(qg):
    r = lax.broadcasted_iota(jnp.int32, (qg, qg + LEFT), 0)
    p = lax.broadcasted_iota(jnp.int32, (qg, qg + LEFT), 1)
    start = lax.shift_left(lax.shift_right_logical(r, 6), 6)
    return (p >= start) & (p < start + BAND)


def bias_expand(vec, qgs, after=()):
    def body(v_ref, *o_refs):
        for qg, o_ref in zip(qgs, o_refs):
            valid = _band_valid(qg)
            for h in range(NH):
                rows = jnp.broadcast_to(v_ref[h:h + 1, :], (qg, TOE))
                toe = pltpu.roll(rows, 0, 1, stride=1, stride_axis=0)
                o_ref[h // 2, qg * (h % 2):qg * (h % 2 + 1), :] = jnp.where(valid, toe[:, :qg + LEFT], NEG_INF)

    vm = pl.BlockSpec(memory_space=pltpu.VMEM)
    return pl.pallas_call(_behind(body, 1, after), in_specs=[vm] + [_any()] * len(after), out_specs=[vm] * len(qgs),
                          out_shape=[jax.ShapeDtypeStruct((NH // 2, 2 * qg, qg + LEFT), F32) for qg in qgs],
                          name="bias_expand")(vec, *after)


def bias_reduce(db2):
    _, qg, kg = db2.shape

    def body(d_ref, o_ref):
        ii = lax.broadcasted_iota(jnp.int32, (kg, kg), 0)
        jj = lax.broadcasted_iota(jnp.int32, (kg, kg), 1)
        flip = jnp.where(ii + jj == kg - 1, 1.0, 0.0).astype(BF16)
        for h in range(NH):
            rest = d_ref[h]
            rev = jnp.zeros((qg, kg), F32)
            for _ in range(3):
                term = rest.astype(BF16)
                rev = rev + jnp.dot(term, flip, preferred_element_type=F32)
                rest = rest - term.astype(F32)
            d = jnp.concatenate([jnp.zeros((qg, TOE - kg), F32), rev], axis=1)
            back = pltpu.roll(d, 0, 1, stride=1, stride_axis=0)
            o_ref[h:h + 1, :] = jnp.sum(back, axis=0, keepdims=True)

    rev = pl.pallas_call(body, out_shape=jax.ShapeDtypeStruct((NH, TOE), F32), name="bias_reduce")(db2)
    return rev[:, ::-1]


def _place():
    x, y, c = lax.axis_index("x"), lax.axis_index("y"), lax.axis_index("c")
    chips = [(1 - x, y), (x, 1 - y), (1 - x, 1 - y)]
    return x, y, c, chips


def _half(ref_rows, c):
    return pl.ds(c * (ref_rows // 2), ref_rows // 2)


HBM_SPEC = pl.BlockSpec(memory_space=pltpu.HBM)
SEM_SPEC = pl.BlockSpec(memory_space=pltpu.SEMAPHORE)
IN_FLIGHT = pltpu.CompilerParams(has_side_effects=pltpu.SideEffectType.DATAFLOW_SIDE_EFFECTING)


def _in_hbm(a):
    return pltpu.with_memory_space_constraint(a, pltpu.HBM)


def cast_to_slot(ws, chip, layer, after=()):
    n = len(ws)
    steps = 4

    def body(b_ref, *refs):
        del b_ref
        for w_ref, o_ref in zip(refs[:n], refs[n + len(after):]):
            o_ref[...] = w_ref[...].astype(BF16)

    grid_spec = pltpu.PrefetchScalarGridSpec(
        num_scalar_prefetch=1, grid=(steps,),
        in_specs=[pl.BlockSpec((None, w.shape[1] // steps, w.shape[2]), lambda r, b: (layer, r, 0)) for w in ws]
        + [_any()] * len(after),
        out_specs=[pl.BlockSpec((None, w.shape[1] // steps, w.shape[2]), lambda r, b: (b[0], r, 0)) for w in ws])
    return pl.pallas_call(body, grid_spec=grid_spec,
                          out_shape=[jax.ShapeDtypeStruct((NCHIP,) + w.shape[1:], BF16) for w in ws],
                          compiler_params=_cp(("arbitrary",)), name="cast_to_slot")(chip, *ws, *after)


def _gather_copies(bufs, send, recv):
    x, y, c, chips = _place()
    b = 2 * x + y
    out = []
    for k, buf in enumerate(bufs):
        rows = buf.shape[1]
        mine = buf.at[b, _half(rows, c), :]
        for j, (cx, cy) in enumerate(chips):
            theirs = buf.at[2 * cx + cy, _half(rows, c), :]
            sems = dict(send_sem=send.at[3 * k + j], recv_sem=recv.at[3 * k + j],
                        device_id=(cx, cy, c), device_id_type=MESH)
            out.append((pltpu.make_async_remote_copy(src_ref=mine, dst_ref=mine, **sems),
                        pltpu.make_async_remote_copy(src_ref=theirs, dst_ref=theirs, **sems)))
    return out


def gather_start(bufs, after, layer):
    n = len(bufs)

    def body(*refs):
        ins = refs[:n]
        send, recv = refs[n + 1], refs[n + 2]
        token = refs[-1]
        for start, _ in _gather_copies(ins, send, recv):
            start.start()
        token[...] = jnp.zeros_like(token)

    sems = pltpu.SemaphoreType.DMA((3 * n,))
    res = pl.pallas_call(
        body, name=f"gather_start_{layer}",
        in_specs=[HBM_SPEC] * n + [_any()],
        out_specs=[SEM_SPEC, SEM_SPEC] + [HBM_SPEC] * n + [pl.BlockSpec(memory_space=pltpu.VMEM)],
        out_shape=[sems, sems] + [pltpu.HBM(b.shape, b.dtype) for b in bufs] + [jax.ShapeDtypeStruct((8, LANES), F32)],
        input_output_aliases={k: 2 + k for k in range(n)}, compiler_params=IN_FLIGHT,
    )(*[_in_hbm(b) for b in bufs], after)
    return res[0], res[1], res[2:2 + n], res[-1]


def gather_wait(send, recv, bufs, after, layer):
    n = len(bufs)

    def body(*refs):
        ins = refs[:n]
        send_ref, recv_ref = refs[n], refs[n + 1]
        for start, arrival in _gather_copies(ins, send_ref, recv_ref):
            start.wait_send()
            arrival.wait_recv()

    return pl.pallas_call(
        body, name=f"gather_wait_{layer}",
        in_specs=[HBM_SPEC] * n + [SEM_SPEC, SEM_SPEC, _any()], out_specs=[HBM_SPEC] * n,
        out_shape=[pltpu.HBM(b.shape, b.dtype) for b in bufs],
        input_output_aliases={k: k for k in range(n)}, compiler_params=IN_FLIGHT,
    )(*bufs, send, recv, after)


def gather_forward(bufs):
    n = len(bufs)

    def body(*refs):
        outs = refs[n:2 * n]
        send, recv = refs[2 * n:]
        x, y, c, chips = _place()
        cps = []
        for k in range(n):
            rows = outs[k].shape[1]
            for j, (cx, cy) in enumerate(chips):
                sems = dict(send_sem=send.at[3 * k + j], recv_sem=recv.at[3 * k + j],
                            device_id=(x, y, 1 - c), device_id_type=MESH)
                mine = outs[k].at[2 * cx + cy, _half(rows, c), :]
                theirs = outs[k].at[2 * cx + cy, _half(rows, 1 - c), :]
                cp = pltpu.make_async_remote_copy(src_ref=mine, dst_ref=mine, **sems)
                cp.start()
                cps.append((cp, pltpu.make_async_remote_copy(src_ref=theirs, dst_ref=theirs, **sems)))
        for cp, arrival in cps:
            cp.wait_send()
            arrival.wait_recv()

    return pl.pallas_call(
        body, in_specs=[_any()] * n, out_specs=[_any()] * n,
        out_shape=[jax.ShapeDtypeStruct(b.shape, b.dtype) for b in bufs], input_output_aliases={k: k for k in range(n)},
        scratch_shapes=[pltpu.SemaphoreType.DMA((3 * n,)), pltpu.SemaphoreType.DMA((3 * n,))],
        name="gather_forward")(*bufs)


def _forward_copies(bufs, send, recv):
    x, y, c, chips = _place()
    out = []
    for k, buf in enumerate(bufs):
        rows = buf.shape[1]
        for j, (cx, cy) in enumerate(chips):
            sems = dict(send_sem=send.at[3 * k + j], recv_sem=recv.at[3 * k + j],
                        device_id=(x, y, 1 - c), device_id_type=MESH)
            mine = buf.at[2 * cx + cy, _half(rows, c), :]
            theirs = buf.at[2 * cx + cy, _half(rows, 1 - c), :]
            out.append((pltpu.make_async_remote_copy(src_ref=mine, dst_ref=mine, **sems),
                        pltpu.make_async_remote_copy(src_ref=theirs, dst_ref=theirs, **sems)))
    return out


def forward_start(bufs, tag):
    n = len(bufs)

    def body(*refs):
        ins = refs[:n]
        send, recv = refs[n], refs[n + 1]
        token = refs[-1]
        for start, _ in _forward_copies(ins, send, recv):
            start.start()
        token[...] = jnp.zeros_like(token)

    sems = pltpu.SemaphoreType.DMA((3 * n,))
    res = pl.pallas_call(
        body, name=f"forward_start_{tag}", in_specs=[HBM_SPEC] * n,
        out_specs=[SEM_SPEC, SEM_SPEC] + [HBM_SPEC] * n + [pl.BlockSpec(memory_space=pltpu.VMEM)],
        out_shape=[sems, sems] + [pltpu.HBM(b.shape, b.dtype) for b in bufs] + [jax.ShapeDtypeStruct((8, LANES), F32)],
        input_output_aliases={k: 2 + k for k in range(n)}, compiler_params=IN_FLIGHT,
    )(*[_in_hbm(b) for b in bufs])
    return res[0], res[1], res[2:2 + n], res[-1]


def forward_wait(send, recv, bufs, after, tag):
    n = len(bufs)

    def body(*refs):
        ins = refs[:n]
        send_ref, recv_ref = refs[n], refs[n + 1]
        for start, arrival in _forward_copies(ins, send_ref, recv_ref):
            start.wait_send()
            arrival.wait_recv()

    return pl.pallas_call(
        body, name=f"forward_wait_{tag}",
        in_specs=[HBM_SPEC] * n + [SEM_SPEC, SEM_SPEC, _any()], out_specs=[HBM_SPEC] * n,
        out_shape=[pltpu.HBM(b.shape, b.dtype) for b in bufs],
        input_output_aliases={k: k for k in range(n)}, compiler_params=IN_FLIGHT,
    )(*bufs, send, recv, after)


def _exchange_copies(srcs, lands, send, recv):
    x, y, c, _ = _place()
    return [pltpu.make_async_remote_copy(
        src_ref=src.at[:, _half(src.shape[1], 1 - c), :], dst_ref=land, send_sem=send.at[k], recv_sem=recv.at[k],
        device_id=(x, y, 1 - c), device_id_type=MESH) for k, (src, land) in enumerate(zip(srcs, lands))]


def exchange_start(srcs, tag):
    n = len(srcs)
    lands = [lax.empty((s.shape[0], s.shape[1] // 2, s.shape[2]), s.dtype) for s in srcs]

    def body(*refs):
        ins, land_refs = refs[:n], refs[n:2 * n]
        send, recv = refs[2 * n], refs[2 * n + 1]
        token = refs[-1]
        for cp in _exchange_copies(ins, land_refs, send, recv):
            cp.start()
        token[...] = jnp.zeros_like(token)

    sems = pltpu.SemaphoreType.DMA((n,))
    res = pl.pallas_call(
        body, name=f"exchange_start_{tag}",
        in_specs=[HBM_SPEC] * (2 * n),
        out_specs=[SEM_SPEC, SEM_SPEC] + [HBM_SPEC] * (2 * n) + [pl.BlockSpec(memory_space=pltpu.VMEM)],
        out_shape=[sems, sems] + [pltpu.HBM(a.shape, a.dtype) for a in list(srcs) + lands]
        + [jax.ShapeDtypeStruct((8, LANES), F32)],
        input_output_aliases={k: 2 + k for k in range(2 * n)}, compiler_params=IN_FLIGHT,
    )(*[_in_hbm(a) for a in list(srcs) + lands])
    return res[0], res[1], res[2:2 + n], res[2 + n:2 + 2 * n], res[-1]


def exchange_wait(send, recv, srcs, lands, after, tag):
    n = len(srcs)

    def body(*refs):
        ins, land_refs = refs[:n], refs[n:2 * n]
        send_ref, recv_ref = refs[2 * n], refs[2 * n + 1]
        for cp in _exchange_copies(ins, land_refs, send_ref, recv_ref):
            cp.wait_send()
            cp.wait_recv()

    res = pl.pallas_call(
        body, name=f"exchange_wait_{tag}",
        in_specs=[HBM_SPEC] * (2 * n) + [SEM_SPEC, SEM_SPEC, _any()], out_specs=[HBM_SPEC] * (2 * n),
        out_shape=[pltpu.HBM(a.shape, a.dtype) for a in list(srcs) + list(lands)],
        input_output_aliases={k: k for k in range(2 * n)}, compiler_params=IN_FLIGHT,
    )(*srcs, *lands, send, recv, after)
    return res[:n], res[n:]


def add_pair(gs, r1s, core):
    n = len(gs)

    def body(c_ref, *refs):
        del c_ref
        for g_ref, r_ref, o_ref in zip(refs[:n], refs[n:2 * n], refs[2 * n:]):
            o_ref[...] = (g_ref[...] + r_ref[...]).astype(BF16)

    blk = lambda r: (None,) + r.shape[1:]
    grid_spec = pltpu.PrefetchScalarGridSpec(
        num_scalar_prefetch=1, grid=(NCHIP,),
        in_specs=[pl.BlockSpec(blk(r), lambda s, c: (s, c[0], 0)) for r in r1s]
        + [pl.BlockSpec(blk(r), lambda s, c: (s, 0, 0)) for r in r1s],
        out_specs=[pl.BlockSpec(blk(r), lambda s, c: (s, 0, 0)) for r in r1s])
    return pl.pallas_call(body, grid_spec=grid_spec, out_shape=[jax.ShapeDtypeStruct(r.shape, BF16) for r in r1s],
                          compiler_params=_cp(("arbitrary",)), name="add_pair")(core, *gs, *r1s)


def _scatter_copies(srcs, lands, send, recv):
    _, _, c, chips = _place()
    out = []
    for k, (src, land) in enumerate(zip(srcs, lands)):
        for j, (cx, cy) in enumerate(chips):
            out.append(pltpu.make_async_remote_copy(
                src_ref=src.at[2 * cx + cy], dst_ref=land.at[j], send_sem=send.at[3 * k + j],
                recv_sem=recv.at[3 * k + j], device_id=(cx, cy, c), device_id_type=MESH))
    return out


def scatter_start(srcs, layer):
    n = len(srcs)
    srcs = list(srcs)
    lands = [lax.empty((3,) + s.shape[1:], s.dtype) for s in srcs]

    def body(*refs):
        ins, land_refs = refs[:n], refs[n:2 * n]
        send, recv = refs[2 * n], refs[2 * n + 1]
        token = refs[-1]
        for cp in _scatter_copies(ins, land_refs, send, recv):
            cp.start()
        token[...] = jnp.zeros_like(token)

    sems = pltpu.SemaphoreType.DMA((3 * n,))
    res = pl.pallas_call(
        body, name=f"scatter_start_{layer}",
        in_specs=[HBM_SPEC] * (2 * n),
        out_specs=[SEM_SPEC, SEM_SPEC] + [HBM_SPEC] * (2 * n) + [pl.BlockSpec(memory_space=pltpu.VMEM)],
        out_shape=[sems, sems] + [pltpu.HBM(a.shape, a.dtype) for a in srcs + lands]
        + [jax.ShapeDtypeStruct((8, LANES), F32)],
        input_output_aliases={k: 2 + k for k in range(2 * n)}, compiler_params=IN_FLIGHT,
    )(*[_in_hbm(a) for a in srcs + lands])
    return res[0], res[1], res[2:2 + n], res[2 + n:2 + 2 * n], res[-1]


def scatter_wait(send, recv, srcs, lands, after, layer):
    n = len(srcs)

    def body(*refs):
        ins, land_refs = refs[:n], refs[n:2 * n]
        send_ref, recv_ref = refs[2 * n], refs[2 * n + 1]
        for cp in _scatter_copies(ins, land_refs, send_ref, recv_ref):
            cp.wait_send()
            cp.wait_recv()

    res = pl.pallas_call(
        body, name=f"scatter_wait_{layer}",
        in_specs=[HBM_SPEC] * (2 * n) + [SEM_SPEC, SEM_SPEC, _any()], out_specs=[HBM_SPEC] * (2 * n),
        out_shape=[pltpu.HBM(a.shape, a.dtype) for a in list(srcs) + list(lands)],
        input_output_aliases={k: k for k in range(2 * n)}, compiler_params=IN_FLIGHT,
    )(*srcs, *lands, send, recv, after)
    return res[n:]


def add_chips(gs, r1s, r2s, place, totals, layer):
    n = len(gs)
    steps = 2

    def body(p_ref, *refs):
        del p_ref
        for g_ref, r1_ref, r2_ref, o_ref in zip(refs[:n], refs[n:2 * n], refs[2 * n:3 * n], refs[4 * n:]):
            own = g_ref[...] + r1_ref[...]
            o_ref[...] = ((own + r2_ref[0].astype(F32)) + r2_ref[1].astype(F32)) + r2_ref[2].astype(F32)

    blk = lambda r: (None, r.shape[1] // steps, r.shape[2])
    grid_spec = pltpu.PrefetchScalarGridSpec(
        num_scalar_prefetch=1, grid=(steps,),
        in_specs=[pl.BlockSpec(blk(r), lambda i, p: (p[1], p[0] * steps + i, 0)) for r in r1s]
        + [pl.BlockSpec(blk(r), lambda i, p: (p[1], i, 0)) for r in r1s]
        + [pl.BlockSpec((3,) + blk(r)[1:], lambda i, p: (0, i, 0)) for r in r1s] + [_any()] * n,
        out_specs=[pl.BlockSpec(blk(r), lambda i, p: (layer, p[0] * steps + i, 0)) for r in r1s])
    return pl.pallas_call(body, grid_spec=grid_spec, out_shape=[jax.ShapeDtypeStruct(t.shape, F32) for t in totals],
                          input_output_aliases={1 + 3 * n + k: k for k in range(n)},
                          compiler_params=_cp(("arbitrary",)), name="add_chips")(place, *gs, *r1s, *r2s, *totals)


def _share_copies(bufs, send, recv):
    x, y, c, _ = _place()
    out = []
    for k, buf in enumerate(bufs):
        sems = dict(send_sem=send.at[k], recv_sem=recv.at[k], device_id=(x, y, 1 - c), device_id_type=MESH)
        mine = buf.at[:, _half(buf.shape[1], c), :]
        theirs = buf.at[:, _half(buf.shape[1], 1 - c), :]
        out.append((pltpu.make_async_remote_copy(src_ref=mine, dst_ref=mine, **sems),
                    pltpu.make_async_remote_copy(src_ref=theirs, dst_ref=theirs, **sems)))
    return out


def share_start(bufs, tag):
    n = len(bufs)

    def body(*refs):
        ins = refs[:n]
        send, recv = refs[n], refs[n + 1]
        token = refs[-1]
        for start, _ in _share_copies(ins, send, recv):
            start.start()
        token[...] = jnp.zeros_like(token)

    sems = pltpu.SemaphoreType.DMA((n,))
    res = pl.pallas_call(
        body, name=f"share_start_{tag}", in_specs=[HBM_SPEC] * n,
        out_specs=[SEM_SPEC, SEM_SPEC] + [HBM_SPEC] * n + [pl.BlockSpec(memory_space=pltpu.VMEM)],
        out_shape=[sems, sems] + [pltpu.HBM(b.shape, b.dtype) for b in bufs] + [jax.ShapeDtypeStruct((8, LANES), F32)],
        input_output_aliases={k: 2 + k for k in range(n)}, compiler_params=IN_FLIGHT,
    )(*[_in_hbm(b) for b in bufs])
    return res[0], res[1], res[2:2 + n], res[-1]


def share_wait(send, recv, bufs, after, tag):
    n = len(bufs)

    def body(*refs):
        ins = refs[:n]
        send_ref, recv_ref = refs[n], refs[n + 1]
        for start, arrival in _share_copies(ins, send_ref, recv_ref):
            start.wait_send()
            arrival.wait_recv()

    return pl.pallas_call(
        body, name=f"share_wait_{tag}",
        in_specs=[HBM_SPEC] * n + [SEM_SPEC, SEM_SPEC, _any()], out_specs=[HBM_SPEC] * n,
        out_shape=[pltpu.HBM(b.shape, b.dtype) for b in bufs],
        input_output_aliases={k: k for k in range(n)}, compiler_params=IN_FLIGHT,
    )(*bufs, send, recv, after)


def small_allreduce(v, after=()):
    rows = v.shape[0]
    flips = [(fx, fy, fc) for fx in (0, 1) for fy in (0, 1) for fc in (0, 1)][1:]

    def body(v_ref, o_ref, buf, send, recv):
        x, y, c, _ = _place()
        buf[4 * x + 2 * y + c] = v_ref[...]
        peers = [(jnp.where(fx, 1 - x, x), jnp.where(fy, 1 - y, y), jnp.where(fc, 1 - c, c)) for fx, fy, fc in flips]
        cps = []
        for k, peer in enumerate(peers):
            cp = pltpu.make_async_remote_copy(
                src_ref=v_ref, dst_ref=buf.at[4 * x + 2 * y + c], send_sem=send.at[k], recv_sem=recv.at[k],
                device_id=peer, device_id_type=MESH)
            cp.start()
            cps.append(cp)
        for k, (px, py, pc) in enumerate(peers):
            pltpu.make_async_remote_copy(
                src_ref=v_ref, dst_ref=buf.at[4 * px + 2 * py + pc], send_sem=send.at[k], recv_sem=recv.at[k],
                device_id=(px, py, pc), device_id_type=MESH).wait_recv()
        for cp in cps:
            cp.wait_send()
        acc = buf[0]
        for s in range(1, 8):
            acc = acc + buf[s]
        o_ref[...] = acc

    vm = pl.BlockSpec(memory_space=pltpu.VMEM)
    return pl.pallas_call(
        _behind(body, 1, after), in_specs=[vm] + [_any()] * len(after), out_specs=vm,
        out_shape=jax.ShapeDtypeStruct((rows, SMALL_COLS), F32),
        scratch_shapes=[pltpu.VMEM((8, rows, SMALL_COLS), F32), pltpu.SemaphoreType.DMA((7,)),
                        pltpu.SemaphoreType.DMA((7,))],
        name="reduce_small")(v, *after)


def adamw(w, g, m, v, rb, name, after=()):
    nl, rows, cols = w.shape

    def body(w_ref, g_ref, m_ref, v_ref, go_ref, d_ref, nm_ref, nv_ref):
        gv = g_ref[...]
        go_ref[...] = gv
        nm = ADAM_B1 * m_ref[...] + (1.0 - ADAM_B1) * gv
        nv = ADAM_B2 * v_ref[...] + (1.0 - ADAM_B2) * (gv * gv)
        m_hat = nm / (1.0 - ADAM_B1 ** ADAM_STEP)
        v_hat = nv / (1.0 - ADAM_B2 ** ADAM_STEP)
        d_ref[...] = -ADAM_LR * (m_hat / (jnp.sqrt(v_hat) + ADAM_EPS) + ADAM_WD * w_ref[...])
        nm_ref[...] = nm
        nv_ref[...] = nv

    blk = pl.BlockSpec((None, rb, cols), lambda l, r: (l, r, 0))
    shp = jax.ShapeDtypeStruct(w.shape, F32)
    return pl.pallas_call(_behind(body, 4, after), grid=(nl, rows // rb), in_specs=[blk] * 4 + [_any()] * len(after),
                          out_specs=[blk] * 4, out_shape=[shp] * 4,
                          compiler_params=_cp(("arbitrary", "arbitrary")), name=name)(w, g, m, v, *after)


def _pack(parts, rows):
    flat = jnp.concatenate([p.reshape(-1).astype(F32) for p in parts])
    return jnp.pad(flat, (0, rows * SMALL_COLS - flat.shape[0])).reshape(rows, SMALL_COLS)


def _unpack(vec, shapes):
    flat = vec.reshape(-1)
    out, off = [], 0
    for s in shapes:
        size = 1
        for d in s:
            size *= d
        out.append(flat[off:off + size].reshape(s))
        off += size
    return out


def kernel(x, w_in, w_conv, rel_bias, g_conv_out, g_attn_out, w_out, g_pre_mix, g_post_mix, g_pre_ffn, g_post_ffn, w_ffn_in, w_ffn_out, loss_target, m_w_in, m_w_conv, m_rel_bias, m_g_conv_out, m_g_attn_out, m_w_out, m_g_pre_mix, m_g_post_mix, m_g_pre_ffn, m_g_post_ffn, m_w_ffn_in, m_w_ffn_out, v_w_in, v_w_conv, v_rel_bias, v_g_conv_out, v_g_attn_out, v_w_out, v_g_pre_mix, v_g_post_mix, v_g_pre_ffn, v_g_post_ffn, v_w_ffn_in, v_w_ffn_out):
    xi, yi, ci = lax.axis_index("x"), lax.axis_index("y"), lax.axis_index("c")
    chip = 2 * xi + yi
    nl = w_in.shape[0]
    x0 = x[0]
    target = loss_target[0]
    cwl = CW // NCHIP

    chip1 = chip.reshape(1).astype(jnp.int32)
    big_weights = [w_in, w_out, w_ffn_in, w_ffn_out]
    own = [cast_to_slot(big_weights, chip1, 0)]
    wc_mine = jnp.pad(w_conv.reshape(-1), (0, 16 * LANES - w_conv.size)).reshape(1, 16, LANES)
    wc_slot = lax.dynamic_update_slice_in_dim(jnp.zeros((NCHIP, 16, LANES), F32), wc_mine, chip, axis=0)
    gm = jnp.kron(jnp.eye(CW // HD, dtype=F32), jnp.full((HD, HD), 1.0 / HD, F32)).astype(BF16)
    row = lambda a, l: a[l][None, :]

    def gather_finish(flight, after, tag):
        send, recv, bufs, _ = flight
        return gather_forward(gather_wait(send, recv, bufs, after, tag))

    first_mix = gather_start(list(own[0][:2]) + [wc_slot], x0, "0m")
    first_ffn = gather_start(own[0][2:], first_mix[3], "0f")
    chain = first_ffn[3]
    biases = []
    for l in range(nl):
        biases.append(bias_expand(_diag_vector(rel_bias[l]), (QG_FWD, QG_BWD), [chain]))
        chain = biases[l][1]
    for l in range(1, nl):
        own.append(cast_to_slot(big_weights, chip1, l, [chain]))
        chain = own[l][0]
    gw_in, gw_out, wc_all = gather_finish(first_mix, chain, "0m")
    wc_full = wc_all.reshape(NCHIP, -1)[:, :nl * cwl * 3].reshape(NCHIP, nl, cwl, 3)
    wc_full = jnp.transpose(wc_full, (1, 0, 2, 3)).reshape(nl, CW, 3)
    wconv_t = jnp.pad(jnp.transpose(wc_full, (0, 2, 1)), ((0, 0), (0, 5), (0, 0)))
    flights, to_sibling = {}, None
    saved, weights = [], []
    h = x0
    for l in range(nl):
        if l == 0:
            pass
        elif l == 1:
            flights[2] = gather_start(own[2], h, 2)
            gw_in, gw_out, gw_fi, gw_fo = gather_finish(flights[l], flights[2][3], l)
        else:
            gw_in, gw_out, gw_fi, gw_fo = forward_wait(*to_sibling[:3], h, l)
        gw_out = gw_out.reshape(D, D)
        behind_mix, behind_ffn = ([first_ffn[3]] if l == 0 else []), []
        if l + 1 < nl and l + 1 not in flights:
            flights[l + 1] = gather_start(own[l + 1], first_ffn[3] if l == 0 else gw_in, l + 1)
            behind_mix.append(flights[l + 1][3])
        bias2, bias2_bwd = biases[l]
        proj = fwd_inproj(h, row(g_pre_mix, l), gw_in, behind_mix)
        xmid, o, lse, y, z = fwd_mix(h, proj, bias2, wconv_t[l], row(g_conv_out, l), row(g_attn_out, l),
                                     row(g_post_mix, l), gm, gw_out)
        if l == 0:
            gw_fi, gw_fo = gather_finish(first_ffn, xmid, "0f")
        elif l + 1 < nl:
            send, recv, bufs, _ = flights[l + 1]
            landed = gather_wait(send, recv, bufs, xmid, l + 1)
            to_sibling = forward_start(landed, l + 1)
            behind_ffn.append(to_sibling[3])
            if l + 2 < nl:
                flights[l + 2] = gather_start(own[l + 2], to_sibling[3], l + 2)
                behind_ffn.append(flights[l + 2][3])
        gw_fo = gw_fo.reshape(2, DFF // 2, D)
        ffn = fwd_ffn(xmid, row(g_pre_ffn, l), row(g_post_ffn, l), gw_fi, gw_fo, behind_ffn,
                      target if l == nl - 1 else None)
        gu, f = ffn[:2]
        saved.append((h, proj, bias2_bwd, xmid, o, lse, y, z, gu, f))
        weights.append((gw_in, gw_out, gw_fi, gw_fo))
        h = ffn[2]
    dx, loss_blk = ffn[2], ffn[3]

    core = ci.reshape(1).astype(jnp.int32)
    place = jnp.stack([ci, chip]).astype(jnp.int32)
    totals = [lax.empty(w.shape, F32) for w in (w_in, w_out, w_ffn_in, w_ffn_out)]
    small = {k: [None] * nl for k in ("co", "ao", "pm", "qm", "pf", "qf", "rel", "wc")}

    def reduce_begin(kinds, grads, tag):
        return kinds, exchange_start(grads, tag), tag

    def reduce_mid(state, after):
        kinds, (send, recv, srcs, lands, _), tag = state
        grads, from_sibling = exchange_wait(send, recv, srcs, lands, after, tag)
        return kinds, grads, from_sibling, scatter_start(add_pair(grads, from_sibling, core), tag), tag

    def reduce_end(state, after, totals, layer):
        kinds, grads, from_sibling, (send, recv, srcs, lands, _), tag = state
        from_chips = scatter_wait(send, recv, srcs, lands, after, tag)
        totals = list(totals)
        summed = add_chips(grads, from_sibling, from_chips, place, [totals[i] for i in kinds], layer)
        for i, t in zip(kinds, summed):
            totals[i] = t
        return totals

    begun = flying = None
    for l in reversed(range(nl)):
        hin, proj, bias2, xmid, o, lse, y, z, gu, f = saved[l]
        gw_in, gw_out, gw_fi, gw_fo = weights[l]
        behind_ffn = [begun[1][4]] if begun is not None else []
        dxm, dfb, act, dgu, h2, dg_qf, dg_pf = bwd_ffn(dx, f, xmid, gu, row(g_pre_ffn, l), row(g_post_ffn, l),
                                                        gw_fi, gw_fo, behind_ffn)
        behind_mix, behind_conv = [], []
        if begun is not None:
            flying = reduce_mid(begun, dxm)
            behind_mix.append(flying[3][4])
        gr_fo = wgrad(act, dfb, 256, D, False, "wgrad_ffn_out").reshape(NCHIP, DFF // NCHIP, D)
        gr_fi = wgrad(h2, dgu, 512, 2 * DFF // NCHIP, True, "wgrad_ffn_in")
        if l == 0:
            begun_ffn = reduce_begin([2, 3], [gr_fi, gr_fo], "0f")
            behind_mix.append(begun_ffn[1][4])
        dzb, do, dco, dbg, dg_qm, dg_co, dg_ao = bwd_mix(dxm, z, o, proj, wconv_t[l], row(g_conv_out, l),
                                                          row(g_attn_out, l), row(g_post_mix, l), gm, gw_out,
                                                          behind_mix)
        if l == 0:
            flying_ffn = reduce_mid(begun_ffn, dzb)
            behind_conv.append(flying_ffn[3][4])
        gr_out = wgrad(y, dzb, 512, D, False, "wgrad_out").reshape(NCHIP, D // NCHIP, D)
        dhc, dcg, dwc = bwd_conv(dco, proj, wconv_t[l], behind_conv)
        dq, dk, dv, db2 = bwd_attn(proj, o, do, lse, bias2)
        dx, dproj, hb, dg_pm = bwd_inproj(dxm, hin, dhc, dbg, dcg, dq, dk, dv, row(g_pre_mix, l), gw_in)
        if flying is not None:
            totals = reduce_end(flying, dx, totals, l + 1)
        gr_in = wgrad(hb, dproj, 512, PROJ // NCHIP, True, "wgrad_in")
        small["co"][l], small["ao"][l], small["pm"][l], small["qm"][l] = dg_co, dg_ao, dg_pm, dg_qm
        small["pf"][l], small["qf"][l] = dg_pf, dg_qf
        small["rel"][l] = _diag_vector_bwd(bias_reduce(db2.reshape(NH, QG_BWD, QG_BWD + LEFT)))
        small["wc"][l] = jnp.transpose(dwc[0:3], (1, 0))
        if l > 0:
            begun = reduce_begin([0, 1, 2, 3], [gr_in, gr_out, gr_fi, gr_fo], l)
    begun_mix = reduce_begin([0, 1], [gr_in, gr_out], "0m")
    totals = reduce_end(flying_ffn, begun_mix[1][4], totals, 0)
    flying_mix = reduce_mid(begun_mix, totals[2])
    share_ffn = share_start(totals[2:], "ffn")

    order = ("co", "ao", "pm", "qm", "pf", "qf", "rel", "wc")
    parts = [jnp.stack(small[k]) for k in order] + [loss_blk[0:1, 0:1]]
    shapes = [p.shape for p in parts]
    red_vec = small_allreduce(_pack(parts, 40), [share_ffn[3], flying_mix[3][4]])
    red = _unpack(red_vec, shapes)

    gr_fi, gr_fo = share_wait(*share_ffn[:3], red_vec, "ffn")
    big_fi = adamw(w_ffn_in, gr_fi, m_w_ffn_in, v_w_ffn_in, w_ffn_in.shape[1] // 4, "adamw_ffn_in")
    totals = reduce_end(flying_mix, big_fi[1], totals, 0)
    share_mix = share_start(totals[:2], "mix")
    big_fo = adamw(w_ffn_out, gr_fo, m_w_ffn_out, v_w_ffn_out, w_ffn_out.shape[1] // 4, "adamw_ffn_out",
                   [share_mix[3]])
    gr_in, gr_out = share_wait(*share_mix[:3], big_fo[1], "mix")
    big_in = adamw(w_in, gr_in, m_w_in, v_w_in, w_in.shape[1] // 4, "adamw_in")
    big_out = adamw(w_out, gr_out, m_w_out, v_w_out, w_out.shape[1] // 4, "adamw_out")
    big = [big_in, big_out, big_fi, big_fo]
    gr_co, gr_ao, gr_pm, gr_qm, gr_pf, gr_qf, gr_rel, gr_wc_full, loss = red
    gr_co, gr_ao, gr_pm, gr_qm, gr_pf, gr_qf = [a.reshape(nl, -1) for a in (gr_co, gr_ao, gr_pm, gr_qm, gr_pf, gr_qf)]
    gr_wc = lax.dynamic_slice_in_dim(gr_wc_full, chip * cwl, cwl, axis=1)
    loss = loss.reshape(())

    sw = [g_conv_out, g_attn_out, g_pre_mix, g_post_mix, g_pre_ffn, g_post_ffn, rel_bias, w_conv]
    sg = [gr_co, gr_ao, gr_pm, gr_qm, gr_pf, gr_qf, gr_rel, gr_wc]
    sm = [m_g_conv_out, m_g_attn_out, m_g_pre_mix, m_g_post_mix, m_g_pre_ffn, m_g_post_ffn, m_rel_bias, m_w_conv]
    sv = [v_g_conv_out, v_g_attn_out, v_g_pre_mix, v_g_post_mix, v_g_pre_ffn, v_g_post_ffn, v_rel_bias, v_w_conv]
    sshapes = [a.shape for a in sw]
    packed = [_pack(a, 32)[None] for a in (sw, sg, sm, sv)]
    s_out = [_unpack(a[0], sshapes) for a in adamw(*packed, 32, "adamw_small")]

    def leaves(big_i, small_i):
        b_in, b_out, b_fi, b_fo = big_i
        s_co, s_ao, s_pm, s_qm, s_pf, s_qf, s_rel, s_wc = small_i
        return [b_in, s_wc, s_rel, s_co, s_ao, b_out, s_pm, s_qm, s_pf, s_qf, b_fi, b_fo]

    out = [loss, dx[None]]
    out += leaves([b[0] for b in big], sg)
    for i in range(1, 4):
        out += leaves([b[i] for b in big], s_out[i])
    return tuple(out)
```

```python
import jax
import jax.numpy as jnp
from jax import lax
from jax.experimental import pallas as pl
from jax.experimental.pallas import tpu as pltpu

F32 = jnp.float32
BF16 = jnp.bfloat16

D = 1024
PROJ = 3072
CW = 512
HD = 64
NH = 8
CHUNK = 64
BAND = 576
REL_CLIP = 128
NREL = 2 * REL_CLIP + 1
DFF = 2816
DEPTH = 4
NCHIP = 4
EPS = 1e-6
NEG_INF = -1e30

ADAM_LR = 0.001
ADAM_B1 = 0.9
ADAM_B2 = 0.999
ADAM_EPS = 1e-08
ADAM_WD = 0.01
ADAM_STEP = 10

V7X_VMEM_BYTES = 64 * 1024 * 1024
VMEM_LIMIT = V7X_VMEM_BYTES - 8 * 1024 * 1024
LANES = 128
QG_FWD = 4 * CHUNK
QG_BWD = 2 * CHUNK
LEFT = BAND - CHUNK
TQ = 512
TM = 256
SMALL_COLS = 1024
MESH = pl.DeviceIdType.MESH
NT = (((1,), (1,)), ((), ()))
TN = (((0,), (0,)), ((), ()))


def _cp(sem=None, vmem=VMEM_LIMIT):
    return pltpu.CompilerParams(dimension_semantics=sem, vmem_limit_bytes=vmem)


def _any():
    return pl.BlockSpec(memory_space=pl.ANY)


def _const(shape):
    nd = len(shape)
    return pl.BlockSpec(shape, lambda *_: (0,) * nd)


def _behind(body, n_in, after):
    def ordered(*refs):
        return body(*refs[:n_in], *refs[n_in + len(after):])
    return ordered


def _rms(v, g):
    r = lax.rsqrt(jnp.mean(v * v, axis=-1, keepdims=True) + EPS)
    return v * r * g


def _rms_bwd(dy, v, g):
    r = lax.rsqrt(jnp.mean(v * v, axis=-1, keepdims=True) + EPS)
    vh = v * r
    dg = jnp.sum(dy * vh, axis=0, keepdims=True)
    dvh = dy * g
    dv = r * (dvh - vh * jnp.mean(dvh * vh, axis=-1, keepdims=True))
    return dv, dg


def _group_mean(v, gm):
    return jnp.dot(v.astype(BF16), gm, preferred_element_type=F32)


def _group_rms_bwd(dy, v, g, gm):
    r = lax.rsqrt(_group_mean(v * v, gm) + EPS)
    vh = v * r
    dg = jnp.sum(dy * vh, axis=0, keepdims=True)
    dvh = dy * g
    dv = r * (dvh - vh * _group_mean(dvh * vh, gm))
    return dv, dg


def _head_masks(scale):
    lane = lax.broadcasted_iota(jnp.int32, (1, LANES), 1)
    return [jnp.where((lane >= HD * a) & (lane < HD * (a + 1)), scale, 0.0).astype(BF16) for a in range(2)]


class _Resident:
    def __init__(self, src, dst, sem):
        self.first = pl.program_id(0) == 0
        self.copy = pltpu.make_async_copy(src, dst, sem)
        self.dst = dst

        @pl.when(self.first)
        def _():
            self.copy.start()

    def read(self):
        @pl.when(self.first)
        def _():
            self.copy.wait()

        return self.dst[...]


FF_CHUNKS = ((0, 1536), (1536, DFF))


def _stream_ffn_weights(wfi_hbm, wfo_hbm, wfi_v, wfo_v, sems, order, step):
    hw = DFF // 2
    per_matrix = {
        0: [(wfi_hbm.at[j], wfi_v.at[0, :, pl.ds(hw * j, hw)]) for j in range(2)],
        1: [(wfi_hbm.at[2 + j], wfi_v.at[1, :, pl.ds(hw * j, hw)]) for j in range(2)],
        2: [(wfo_hbm.at[j], wfo_v.at[pl.ds(hw * j, hw), :]) for j in range(2)],
    }
    pieces = [p for m in order for p in per_matrix[m]]
    slot = {m: 2 * k for k, m in enumerate(order)}

    def make_step(wait):
        def ready(m, chunk):
            if chunk == 0:
                wait(slot[m])
                wait(slot[m] + 1)
        return lambda: step(ready)

    copies = [pltpu.make_async_copy(src, dst, sems.at[k]) for k, (src, dst) in enumerate(pieces)]
    first = pl.program_id(0) == 0

    @pl.when(first)
    def _():
        for cp in copies:
            cp.start()
        make_step(lambda k: copies[k].wait())()

    @pl.when(jnp.logical_not(first))
    def _():
        make_step(lambda k: None)()


def _conv_taps(u_prev, u, scr):
    n = u.shape[0]
    scr[0:16, :] = u_prev
    scr[16:16 + n, :] = u
    return scr[15:15 + n, :], scr[14:14 + n, :]


def fwd_inproj(x, g, w_all, after=()):
    t = x.shape[0]
    wc = PROJ // NCHIP

    def body(x_ref, g_ref, w_hbm, o_ref, w_v):
        @pl.when(pl.program_id(0) == 0)
        def _():
            pltpu.sync_copy(w_hbm, w_v)

        h = _rms(x_ref[...], g_ref[...]).astype(BF16)
        for b in range(NCHIP):
            o_ref[:, wc * b:wc * (b + 1)] = jnp.dot(h, w_v[b], preferred_element_type=F32).astype(BF16)

    return pl.pallas_call(
        _behind(body, 3, after), grid=(t // TQ,),
        in_specs=[pl.BlockSpec((TQ, D), lambda i: (i, 0)), _const((1, D)), _any()] + [_any()] * len(after),
        out_specs=pl.BlockSpec((TQ, PROJ), lambda i: (i, 0)),
        out_shape=jax.ShapeDtypeStruct((t, PROJ), BF16),
        scratch_shapes=[pltpu.VMEM((NCHIP, D, wc), BF16)],
        compiler_params=_cp(("arbitrary",)), name="fwd_inproj")(x, g, w_all, *after)


def _attn_window_specs():
    return [
        pl.BlockSpec((TQ, CW), lambda i: (i, 3)),
        pl.BlockSpec((TQ, CW), lambda i: (jnp.maximum(i - 1, 0), 4)),
        pl.BlockSpec((TQ, CW), lambda i: (i, 4)),
        pl.BlockSpec((TQ, CW), lambda i: (jnp.maximum(i - 1, 0), 5)),
        pl.BlockSpec((TQ, CW), lambda i: (i, 5)),
    ]


def _conv_specs():
    return [
        pl.BlockSpec((TQ, 3 * CW), lambda i: (i, 0)),
        pl.BlockSpec((16, 3 * CW), lambda i: (jnp.maximum(i * (TQ // 16) - 1, 0), 0)),
    ]


def _conv_fwd(pc_ref, pcp_ref, wc_ref, scr, first):
    pc = pc_ref[...].astype(F32)
    hc, bg, cg = pc[:, :CW], pc[:, CW:2 * CW], pc[:, 2 * CW:]
    u = cg * hc
    pp = pcp_ref[...].astype(F32)
    u_prev = jnp.where(first, 0.0, pp[:, 2 * CW:] * pp[:, :CW])
    u1, u2 = _conv_taps(u_prev, u, scr)
    cout = wc_ref[0:1, :] * u2 + wc_ref[1:2, :] * u1 + wc_ref[2:3, :] * u
    return hc, bg, cg, u, u1, u2, cout


def _key_penalty(first, r0, kg):
    col = lax.broadcasted_iota(jnp.int32, (1, kg), 1)
    limit = jnp.where(first, TQ - r0, 0)
    return jnp.where(col < limit, NEG_INF, 0.0)


def fwd_mix(x, proj, bias2, wconv_t, g_co, g_ao, g_pm, gm, wout_all):
    t = x.shape[0]
    qg, kg = QG_FWD, QG_FWD + LEFT

    def body(x_ref, pc_ref, pcp_ref, q_ref, kp_ref, kc_ref, vp_ref, vc_ref, b2_ref, wc_ref, gco_ref, gao_ref, gpm_ref,
             gm_ref, wout_hbm, xmid_ref, o_ref, lse_ref, y_ref, z_ref, wout_v, kwin, vwin, cscr, sems):
        i = pl.program_id(0)
        first = i == 0
        wout = _Resident(wout_hbm, wout_v, sems.at[0])
        kwin[0:TQ, :] = kp_ref[...]
        kwin[TQ:2 * TQ, :] = kc_ref[...]
        vwin[0:TQ, :] = vp_ref[...]
        vwin[TQ:2 * TQ, :] = vc_ref[...]
        qmask = _head_masks(HD ** -0.5)
        low = lax.broadcasted_iota(jnp.int32, (1, LANES), 1) < HD

        def group(g, carry):
            r0 = pl.multiple_of(g * qg, qg)
            pen = _key_penalty(first, r0, kg)
            for hp in range(NH // 2):
                ls = slice(LANES * hp, LANES * (hp + 1))
                qb = q_ref[pl.ds(r0, qg), ls]
                q2 = jnp.concatenate([qb * qmask[0], qb * qmask[1]], axis=0)
                s = lax.dot_general(q2, kwin[pl.ds(r0, kg), ls], NT, preferred_element_type=F32)
                s = s + b2_ref[hp] + pen
                m = jnp.max(s, axis=-1, keepdims=True)
                p = jnp.exp(s - m)
                l = jnp.sum(p, axis=-1, keepdims=True)
                o2 = jnp.dot(p.astype(BF16), vwin[pl.ds(r0, kg), ls], preferred_element_type=F32) * (1.0 / l)
                lse2 = m + jnp.log(l)
                o_ref[pl.ds(r0, qg), ls] = jnp.where(low, o2[:qg], o2[qg:])
                lse_ref[pl.ds(r0, qg), ls] = jnp.where(low, lse2[:qg], lse2[qg:])
            return carry

        lax.fori_loop(0, TQ // qg, group, 0)

        _, bg, _, _, _, _, cout = _conv_fwd(pc_ref, pcp_ref, wc_ref, cscr, first)
        yc = bg * cout
        gmv = gm_ref[...]
        ycn = yc * lax.rsqrt(_group_mean(yc * yc, gmv) + EPS) * gco_ref[...]
        oa = o_ref[...]
        oan = oa * lax.rsqrt(_group_mean(oa * oa, gmv) + EPS) * gao_ref[...]
        y_ref[:, 0:CW] = ycn.astype(BF16)
        y_ref[:, CW:2 * CW] = oan.astype(BF16)
        z = jnp.dot(y_ref[...], wout.read(), preferred_element_type=F32)
        z_ref[...] = z
        xmid_ref[...] = x_ref[...] + _rms(z, gpm_ref[...])

    row = lambda w: pl.BlockSpec((TQ, w), lambda i: (i, 0))
    return pl.pallas_call(
        body, grid=(t // TQ,),
        in_specs=[row(D)] + _conv_specs() + _attn_window_specs() + [
            _const((NH // 2, 2 * qg, kg)), _const((8, CW)), _const((1, CW)), _const((1, CW)), _const((1, D)),
            _const((CW, CW)), _any()],
        out_specs=[row(D), row(CW), row(CW), row(D), row(D)],
        out_shape=[jax.ShapeDtypeStruct((t, D), F32), jax.ShapeDtypeStruct((t, CW), F32),
                   jax.ShapeDtypeStruct((t, CW), F32), jax.ShapeDtypeStruct((t, D), BF16),
                   jax.ShapeDtypeStruct((t, D), F32)],
        scratch_shapes=[pltpu.VMEM((D, D), BF16), pltpu.VMEM((2 * TQ, CW), BF16), pltpu.VMEM((2 * TQ, CW), BF16),
                        pltpu.VMEM((TQ + 16, CW), F32), pltpu.SemaphoreType.DMA((1,))],
        compiler_params=_cp(("arbitrary",)), name="fwd_mix",
    )(x, proj, proj, proj, proj, proj, proj, proj, bias2, wconv_t, g_co, g_ao, g_pm, gm, wout_all)


def fwd_ffn(xmid, g_pre, g_post, wfi_all, wfo_all, after=(), target=None):
    t = xmid.shape[0]
    n_in = 5 if target is None else 6

    def body(*refs):
        x_ref, gpre_ref, gpost_ref, wfi_hbm, wfo_hbm = refs[:5]
        t_ref = None if target is None else refs[5]
        gu_ref, f_ref, xo_ref = refs[n_in:n_in + 3]
        l_ref = None if target is None else refs[n_in + 3]
        wfi_v, wfo_v, sems = refs[-3:]

        if target is not None:
            @pl.when(pl.program_id(0) == 0)
            def _():
                l_ref[...] = jnp.zeros_like(l_ref)

        def step(ready):
            xv = x_ref[...]
            h = _rms(xv, gpre_ref[...]).astype(BF16)
            f = jnp.zeros((TM, D), F32)
            for ci, (a, b) in enumerate(FF_CHUNKS):
                ready(0, ci)
                gate = jnp.dot(h, wfi_v[0, :, a:b], preferred_element_type=F32)
                ready(1, ci)
                up = jnp.dot(h, wfi_v[1, :, a:b], preferred_element_type=F32)
                gu_ref[:, a:b] = gate.astype(BF16)
                gu_ref[:, DFF + a:DFF + b] = up.astype(BF16)
                act = gate * (1.0 / (1.0 + jnp.exp(-gate))) * up
                ready(2, ci)
                f = f + jnp.dot(act.astype(BF16), wfo_v[a:b, :], preferred_element_type=F32)
            f_ref[...] = f
            xo = xv + _rms(f, gpost_ref[...])
            if target is None:
                xo_ref[...] = xo
            else:
                e = xo - t_ref[...]
                xo_ref[...] = e * (1.0 / D)
                rows = jnp.sum(e * e, axis=-1, keepdims=True) * (1.0 / D)
                l_ref[...] += 0.5 * jnp.sum(rows, axis=0, keepdims=True)

        _stream_ffn_weights(wfi_hbm, wfo_hbm, wfi_v, wfo_v, sems, (0, 1, 2), step)

    row = lambda w: pl.BlockSpec((TM, w), lambda i: (i, 0))
    with_loss = target is not None
    return pl.pallas_call(
        _behind(body, n_in, after), grid=(t // TM,),
        in_specs=[row(D), _const((1, D)), _const((1, D)), _any(), _any()] + [row(D)] * with_loss
        + [_any()] * len(after),
        out_specs=[row(2 * DFF), row(D), row(D)] + [_const((8, LANES))] * with_loss,
        out_shape=[jax.ShapeDtypeStruct((t, 2 * DFF), BF16), jax.ShapeDtypeStruct((t, D), F32),
                   jax.ShapeDtypeStruct((t, D), F32)] + [jax.ShapeDtypeStruct((8, LANES), F32)] * with_loss,
        scratch_shapes=[pltpu.VMEM((2, D, DFF), BF16), pltpu.VMEM((DFF, D), BF16), pltpu.SemaphoreType.DMA((6,))],
        compiler_params=_cp(("arbitrary",)), name="fwd_ffn_loss" if with_loss else "fwd_ffn",
    )(xmid, g_pre, g_post, wfi_all, wfo_all, *([target] * with_loss), *after)


def bwd_ffn(dx, f, xmid, gu, g_pre, g_post, wfi_all, wfo_all, after=()):
    t = dx.shape[0]
    hw = DFF // 2

    def body(dx_ref, f_ref, x_ref, gu_ref, gpre_ref, gpost_ref, wfi_hbm, wfo_hbm,
             dxm_ref, df_ref, act_ref, dgu_ref, h_ref, dgpost_ref, dgpre_ref, wfi_v, wfo_v, sems):
        @pl.when(pl.program_id(0) == 0)
        def _():
            dgpost_ref[...] = jnp.zeros_like(dgpost_ref)
            dgpre_ref[...] = jnp.zeros_like(dgpre_ref)

        def step(ready):
            dxo = dx_ref[...]
            df, dgp = _rms_bwd(dxo, f_ref[...], gpost_ref[...])
            dgpost_ref[...] += dgp
            dfb = df.astype(BF16)
            df_ref[...] = dfb
            dh = jnp.zeros((TM, D), F32)
            for ci, (a, b) in enumerate(FF_CHUNKS):
                ready(2, ci)
                dact = lax.dot_general(dfb, wfo_v[a:b, :], NT, preferred_element_type=F32)
                gate = gu_ref[:, a:b].astype(F32)
                up = gu_ref[:, DFF + a:DFF + b].astype(F32)
                sig = 1.0 / (1.0 + jnp.exp(-gate))
                silu = gate * sig
                act_ref[:, a:b] = (silu * up).astype(BF16)
                dup = (dact * silu).astype(BF16)
                dgate = (dact * up * (sig * (1.0 + gate * (1.0 - sig)))).astype(BF16)
                dgu_ref[:, a:b] = dgate
                dgu_ref[:, DFF + a:DFF + b] = dup
                ready(0, ci)
                dh = dh + lax.dot_general(dgate, wfi_v[0, :, a:b], NT, preferred_element_type=F32)
                ready(1, ci)
                dh = dh + lax.dot_general(dup, wfi_v[1, :, a:b], NT, preferred_element_type=F32)
            xv = x_ref[...]
            gpre = gpre_ref[...]
            h_ref[...] = _rms(xv, gpre).astype(BF16)
            dxv, dgq = _rms_bwd(dh, xv, gpre)
            dgpre_ref[...] += dgq
            dxm_ref[...] = dxo + dxv

        _stream_ffn_weights(wfi_hbm, wfo_hbm, wfi_v, wfo_v, sems, (2, 0, 1), step)

    row = lambda w: pl.BlockSpec((TM, w), lambda i: (i, 0))
    return pl.pallas_call(
        _behind(body, 8, after), grid=(t // TM,),
        in_specs=[row(D), row(D), row(D), row(2 * DFF), _const((1, D)), _const((1, D)), _any(), _any()]
        + [_any()] * len(after),
        out_specs=[row(D), row(D), row(DFF), row(2 * DFF), row(D), _const((1, D)), _const((1, D))],
        out_shape=[jax.ShapeDtypeStruct((t, D), F32), jax.ShapeDtypeStruct((t, D), BF16),
                   jax.ShapeDtypeStruct((t, DFF), BF16), jax.ShapeDtypeStruct((t, 2 * DFF), BF16),
                   jax.ShapeDtypeStruct((t, D), BF16), jax.ShapeDtypeStruct((1, D), F32),
                   jax.ShapeDtypeStruct((1, D), F32)],
        scratch_shapes=[pltpu.VMEM((2, D, DFF), BF16), pltpu.VMEM((DFF, D), BF16), pltpu.SemaphoreType.DMA((6,))],
        compiler_params=_cp(("arbitrary",)), name="bwd_ffn")(dx, f, xmid, gu, g_pre, g_post, wfi_all, wfo_all, *after)


def bwd_mix(dxm, z, o, proj, wconv_t, g_co, g_ao, g_pm, gm, wout_all, after=()):
    t = dxm.shape[0]

    def body(dx_ref, z_ref, o_ref, pc_ref, pcp_ref, wc_ref, gco_ref, gao_ref, gpm_ref, gm_ref, wout_hbm,
             dz_ref, do_ref, dco_ref, dbg_ref, dgpm_ref, dgco_ref, dgao_ref, wout_v, cscr):
        first = pl.program_id(0) == 0

        @pl.when(first)
        def _():
            pltpu.sync_copy(wout_hbm, wout_v)
            dgpm_ref[...] = jnp.zeros_like(dgpm_ref)
            dgco_ref[...] = jnp.zeros_like(dgco_ref)
            dgao_ref[...] = jnp.zeros_like(dgao_ref)

        dz, dgp = _rms_bwd(dx_ref[...], z_ref[...], gpm_ref[...])
        dgpm_ref[...] += dgp
        dzb = dz.astype(BF16)
        dz_ref[...] = dzb
        gmv = gm_ref[...]
        _, bg, _, _, _, _, cout = _conv_fwd(pc_ref, pcp_ref, wc_ref, cscr, first)
        dy_conv = lax.dot_general(dzb, wout_v[0:CW, :], NT, preferred_element_type=F32)
        dyc, dgc = _group_rms_bwd(dy_conv, bg * cout, gco_ref[...], gmv)
        dgco_ref[...] += dgc
        dbg_ref[...] = (dyc * cout).astype(BF16)
        dco_ref[...] = dyc * bg
        dy_attn = lax.dot_general(dzb, wout_v[CW:2 * CW, :], NT, preferred_element_type=F32)
        do, dga = _group_rms_bwd(dy_attn, o_ref[...], gao_ref[...], gmv)
        dgao_ref[...] += dga
        do_ref[...] = do.astype(BF16)

    row = lambda w: pl.BlockSpec((TQ, w), lambda i: (i, 0))
    return pl.pallas_call(
        _behind(body, 11, after), grid=(t // TQ,),
        in_specs=[row(D), row(D), row(CW)] + _conv_specs() + [
            _const((8, CW)), _const((1, CW)), _const((1, CW)), _const((1, D)), _const((CW, CW)), _any()]
        + [_any()] * len(after),
        out_specs=[row(D), row(CW), row(CW), row(CW), _const((1, D)), _const((1, CW)), _const((1, CW))],
        out_shape=[jax.ShapeDtypeStruct((t, D), BF16), jax.ShapeDtypeStruct((t, CW), BF16),
                   jax.ShapeDtypeStruct((t, CW), F32), jax.ShapeDtypeStruct((t, CW), BF16),
                   jax.ShapeDtypeStruct((1, D), F32), jax.ShapeDtypeStruct((1, CW), F32),
                   jax.ShapeDtypeStruct((1, CW), F32)],
        scratch_shapes=[pltpu.VMEM((D, D), BF16), pltpu.VMEM((TQ + 16, CW), F32)],
        compiler_params=_cp(("arbitrary",)), name="bwd_mix",
    )(dxm, z, o, proj, proj, wconv_t, g_co, g_ao, g_pm, gm, wout_all, *after)


def bwd_conv(dco, proj, wconv_t, after=()):
    t = dco.shape[0]
    nt = t // TQ

    def body(d_ref, dn_ref, pc_ref, pcp_ref, wc_ref, dhc_ref, dcg_ref, dw_ref, cscr, dscr):
        i = pl.program_id(0)
        first = i == 0

        @pl.when(first)
        def _():
            dw_ref[...] = jnp.zeros_like(dw_ref)

        hc, _, cg, u, u1, u2, _ = _conv_fwd(pc_ref, pcp_ref, wc_ref, cscr, first)
        d0 = d_ref[...]
        dscr[0:TQ, :] = d0
        dscr[TQ:TQ + 8, :] = jnp.where(i == nt - 1, 0.0, dn_ref[...])
        d1 = dscr[1:TQ + 1, :]
        d2 = dscr[2:TQ + 2, :]
        du = wc_ref[2:3, :] * d0 + wc_ref[1:2, :] * d1 + wc_ref[0:1, :] * d2
        dhc_ref[...] = (du * cg).astype(BF16)
        dcg_ref[...] = (du * hc).astype(BF16)
        dw_ref[0:1, :] += jnp.sum(d0 * u2, axis=0, keepdims=True)
        dw_ref[1:2, :] += jnp.sum(d0 * u1, axis=0, keepdims=True)
        dw_ref[2:3, :] += jnp.sum(d0 * u, axis=0, keepdims=True)

    row = lambda w: pl.BlockSpec((TQ, w), lambda i: (i, 0))
    nxt = pl.BlockSpec((8, CW), lambda i: (jnp.minimum((i + 1) * (TQ // 8), t // 8 - 1), 0))
    return pl.pallas_call(
        _behind(body, 5, after), grid=(nt,),
        in_specs=[row(CW), nxt] + _conv_specs() + [_const((8, CW))] + [_any()] * len(after),
        out_specs=[row(CW), row(CW), _const((8, CW))],
        out_shape=[jax.ShapeDtypeStruct((t, CW), BF16), jax.ShapeDtypeStruct((t, CW), BF16),
                   jax.ShapeDtypeStruct((8, CW), F32)],
        scratch_shapes=[pltpu.VMEM((TQ + 16, CW), F32), pltpu.VMEM((TQ + 8, CW), F32)],
        compiler_params=_cp(("arbitrary",)), name="bwd_conv")(dco, dco, proj, proj, wconv_t, *after)


def bwd_attn(proj, o, do, lse, bias2):
    t = o.shape[0]
    nt = t // TQ
    qg, kg = QG_BWD, QG_BWD + LEFT
    nkb = (t + TQ) // LANES

    def body(q_ref, kp_ref, kc_ref, vp_ref, vc_ref, o_ref, do_ref, lse_ref, b2_ref,
             dq_ref, dk_hbm, dv_hbm, db_hbm, kwin, vwin, dk_acc, dv_acc, db_acc):
        i = pl.program_id(0)
        first = i == 0

        @pl.when(first)
        def _():
            dk_acc[...] = jnp.zeros_like(dk_acc)
            dv_acc[...] = jnp.zeros_like(dv_acc)
            db_acc[...] = jnp.zeros_like(db_acc)

        kwin[0:TQ, :] = kp_ref[...]
        kwin[TQ:2 * TQ, :] = kc_ref[...]
        vwin[0:TQ, :] = vp_ref[...]
        vwin[TQ:2 * TQ, :] = vc_ref[...]
        scale = HD ** -0.5
        qmask = _head_masks(scale)
        vmask = _head_masks(1.0)
        low = lax.broadcasted_iota(jnp.int32, (1, LANES), 1) < HD

        def group(g, carry):
            r0 = pl.multiple_of(g * qg, qg)
            base = i * (TQ // LANES) + g * (qg // LANES)
            pen = _key_penalty(first, r0, kg)
            for hp in range(NH // 2):
                ls = slice(LANES * hp, LANES * (hp + 1))
                qb = q_ref[pl.ds(r0, qg), ls]
                kw = kwin[pl.ds(r0, kg), ls]
                dob = do_ref[pl.ds(r0, qg), ls]
                prod = dob.astype(F32) * o_ref[pl.ds(r0, qg), ls]
                lseb = lse_ref[pl.ds(r0, qg), ls]
                q2 = jnp.concatenate([qb * qmask[0], qb * qmask[1]], axis=0)
                do2 = jnp.concatenate([dob * vmask[0], dob * vmask[1]], axis=0)
                lse2 = jnp.concatenate([lseb[:, 0:1], lseb[:, HD:HD + 1]], axis=0)
                dsum = jnp.concatenate([jnp.sum(jnp.where(low, prod, 0.0), axis=-1, keepdims=True),
                                        jnp.sum(jnp.where(low, 0.0, prod), axis=-1, keepdims=True)], axis=0)
                s = lax.dot_general(q2, kw, NT, preferred_element_type=F32) + b2_ref[hp] + pen
                p = jnp.exp(s - lse2)
                dp = lax.dot_general(do2, vwin[pl.ds(r0, kg), ls], NT, preferred_element_type=F32)
                ds = p * (dp - dsum)
                db_acc[hp] += ds
                dsb = ds.astype(BF16)
                dq2 = jnp.dot(dsb, kw, preferred_element_type=F32)
                dq_ref[pl.ds(r0, qg), ls] = (jnp.where(low, dq2[:qg], dq2[qg:]) * scale).astype(BF16)
                dkt = lax.dot_general(q2, dsb, TN, preferred_element_type=F32)
                dvt = lax.dot_general(do2, p.astype(BF16), TN, preferred_element_type=F32)
                for kb in range(kg // LANES):
                    dk_acc[base + kb, ls, :] += dkt[:, LANES * kb:LANES * (kb + 1)]
                    dv_acc[base + kb, ls, :] += dvt[:, LANES * kb:LANES * (kb + 1)]
            return carry

        lax.fori_loop(0, TQ // qg, group, 0)

        @pl.when(i == nt - 1)
        def _():
            pltpu.sync_copy(dk_acc, dk_hbm)
            pltpu.sync_copy(dv_acc, dv_hbm)
            pltpu.sync_copy(db_acc, db_hbm)

    row = lambda w: pl.BlockSpec((TQ, w), lambda i: (i, 0))
    return pl.pallas_call(
        body, grid=(nt,),
        in_specs=_attn_window_specs() + [row(CW), row(CW), row(CW), _const((NH // 2, 2 * qg, kg))],
        out_specs=[row(CW), _any(), _any(), _any()],
        out_shape=[jax.ShapeDtypeStruct((t, CW), BF16), jax.ShapeDtypeStruct((nkb, CW, LANES), F32),
                   jax.ShapeDtypeStruct((nkb, CW, LANES), F32), jax.ShapeDtypeStruct((NH // 2, 2 * qg, kg), F32)],
        scratch_shapes=[pltpu.VMEM((2 * TQ, CW), BF16), pltpu.VMEM((2 * TQ, CW), BF16),
                        pltpu.VMEM((nkb, CW, LANES), F32), pltpu.VMEM((nkb, CW, LANES), F32),
                        pltpu.VMEM((NH // 2, 2 * qg, kg), F32)],
        compiler_params=_cp(("arbitrary",)), name="bwd_attn",
    )(proj, proj, proj, proj, proj, o, do, lse, bias2)


def bwd_inproj(dxm, x, dhc, dbg, dcg, dq, dk, dv, g, w_all):
    t = x.shape[0]
    wc = PROJ // NCHIP

    def body(dxm_ref, x_ref, dhc_ref, dbg_ref, dcg_ref, dq_ref, dk_ref, dv_ref, g_ref, w_hbm,
             dx_ref, dp_ref, h_ref, dg_ref, w_v):
        @pl.when(pl.program_id(0) == 0)
        def _():
            pltpu.sync_copy(w_hbm, w_v)
            dg_ref[...] = jnp.zeros_like(dg_ref)

        dp_ref[:, 0:CW] = dhc_ref[...]
        dp_ref[:, CW:2 * CW] = dbg_ref[...]
        dp_ref[:, 2 * CW:3 * CW] = dcg_ref[...]
        dp_ref[:, 3 * CW:4 * CW] = dq_ref[...]
        for kb in range(TQ // LANES):
            rows = slice(LANES * kb, LANES * (kb + 1))
            dp_ref[rows, 4 * CW:5 * CW] = jnp.transpose(dk_ref[kb]).astype(BF16)
            dp_ref[rows, 5 * CW:6 * CW] = jnp.transpose(dv_ref[kb]).astype(BF16)
        dh = jnp.zeros((TQ, D), F32)
        for b in range(NCHIP):
            dh = dh + lax.dot_general(dp_ref[:, wc * b:wc * (b + 1)], w_v[b], NT, preferred_element_type=F32)
        xv = x_ref[...]
        gv = g_ref[...]
        h_ref[...] = _rms(xv, gv).astype(BF16)
        dxv, dgv = _rms_bwd(dh, xv, gv)
        dg_ref[...] += dgv
        dx_ref[...] = dxm_ref[...] + dxv

    row = lambda w: pl.BlockSpec((TQ, w), lambda i: (i, 0))
    pad = pl.BlockSpec((TQ // LANES, CW, LANES), lambda i: (i + 1, 0, 0))
    return pl.pallas_call(
        body, grid=(t // TQ,),
        in_specs=[row(D), row(D), row(CW), row(CW), row(CW), row(CW), pad, pad, _const((1, D)), _any()],
        out_specs=[row(D), row(PROJ), row(D), _const((1, D))],
        out_shape=[jax.ShapeDtypeStruct((t, D), F32), jax.ShapeDtypeStruct((t, PROJ), BF16),
                   jax.ShapeDtypeStruct((t, D), BF16), jax.ShapeDtypeStruct((1, D), F32)],
        scratch_shapes=[pltpu.VMEM((NCHIP, D, wc), BF16)],
        compiler_params=_cp(("arbitrary",)), name="bwd_inproj",
    )(dxm, x, dhc, dbg, dcg, dq, dk, dv, g, w_all)


def wgrad(a, b, kb, nb, by_columns, name):
    t, k = a.shape
    n = b.shape[1]
    tk = 512

    def body(a_ref, b_ref, o_ref):
        o_ref[...] = jnp.zeros_like(o_ref)
        for c in range(t // tk):
            o_ref[...] += lax.dot_general(a_ref[tk * c:tk * (c + 1), :], b_ref[tk * c:tk * (c + 1), :], TN,
                                          preferred_element_type=F32)

    if by_columns:
        assert nb == n // NCHIP
        out_spec = pl.BlockSpec((None, kb, nb), lambda ki, ni: (ni, ki, 0))
        out_shape = jax.ShapeDtypeStruct((NCHIP, k, nb), F32)
    else:
        assert nb == n
        out_spec = pl.BlockSpec((kb, nb), lambda ki, ni: (ki, 0))
        out_shape = jax.ShapeDtypeStruct((k, n), F32)
    return pl.pallas_call(
        body, grid=(k // kb, n // nb),
        in_specs=[pl.BlockSpec((t, kb), lambda ki, ni: (0, ki)), pl.BlockSpec((t, nb), lambda ki, ni: (0, ni))],
        out_specs=out_spec, out_shape=out_shape,
        compiler_params=_cp(("arbitrary", "arbitrary")), name=name)(a, b)


TOE = 1024
assert 2 * QG_FWD + LEFT <= TOE
N_FLAT = LEFT - REL_CLIP + 1
N_VAR = BAND - N_FLAT


def _diag_vector(table):
    last = table[:, 2 * REL_CLIP:]
    var = table[:, 2 * REL_CLIP - N_VAR:2 * REL_CLIP][:, ::-1]
    return jnp.concatenate([jnp.broadcast_to(last, (NH, N_FLAT)), var, jnp.broadcast_to(last, (NH, TOE - BAND))], axis=1)


def _diag_vector_bwd(dvec):
    dlast = jnp.sum(dvec[:, :N_FLAT], axis=1, keepdims=True) + jnp.sum(dvec[:, BAND:], axis=1, keepdims=True)
    dvar = dvec[:, N_FLAT:BAND][:, ::-1]
    return jnp.concatenate([jnp.zeros((NH, 2 * REL_CLIP - N_VAR), F32), dvar, dlast], axis=1)


def _band_valid(qg):
    r = lax.broadcasted_iota(jnp.int32, (qg, qg + LEFT), 0)
    p = lax.broadcasted_iota(jnp.int32, (qg, qg + LEFT), 1)
    start = lax.shift_left(lax.shift_right_logical(r, 6), 6)
    return (p >= start) & (p < start + BAND)


def bias_expand(vec, qgs, after=()):
    def body(v_ref, *o_refs):
        for qg, o_ref in zip(qgs, o_refs):
            valid = _band_valid(qg)
            for h in range(NH):
                rows = jnp.broadcast_to(v_ref[h:h + 1, :], (qg, TOE))
                toe = pltpu.roll(rows, 0, 1, stride=1, stride_axis=0)
                o_ref[h // 2, qg * (h % 2):qg * (h % 2 + 1), :] = jnp.where(valid, toe[:, :qg + LEFT], NEG_INF)

    vm = pl.BlockSpec(memory_space=pltpu.VMEM)
    return pl.pallas_call(_behind(body, 1, after), in_specs=[vm] + [_any()] * len(after), out_specs=[vm] * len(qgs),
                          out_shape=[jax.ShapeDtypeStruct((NH // 2, 2 * qg, qg + LEFT), F32) for qg in qgs],
                          name="bias_expand")(vec, *after)


def bias_reduce(db2):
    _, qg, kg = db2.shape

    def body(d_ref, o_ref):
        ii = lax.broadcasted_iota(jnp.int32, (kg, kg), 0)
        jj = lax.broadcasted_iota(jnp.int32, (kg, kg), 1)
        flip = jnp.where(ii + jj == kg - 1, 1.0, 0.0).astype(BF16)
        for h in range(NH):
            rest = d_ref[h]
            rev = jnp.zeros((qg, kg), F32)
            for _ in range(3):
                term = rest.astype(BF16)
                rev = rev + jnp.dot(term, flip, preferred_element_type=F32)
                rest = rest - term.astype(F32)
            d = jnp.concatenate([jnp.zeros((qg, TOE - kg), F32), rev], axis=1)
            back = pltpu.roll(d, 0, 1, stride=1, stride_axis=0)
            o_ref[h:h + 1, :] = jnp.sum(back, axis=0, keepdims=True)

    rev = pl.pallas_call(body, out_shape=jax.ShapeDtypeStruct((NH, TOE), F32), name="bias_reduce")(db2)
    return rev[:, ::-1]


def _place():
    x, y, c = lax.axis_index("x"), lax.axis_index("y"), lax.axis_index("c")
    chips = [(1 - x, y), (x, 1 - y), (1 - x, 1 - y)]
    return x, y, c, chips


def _half(ref_rows, c):
    return pl.ds(c * (ref_rows // 2), ref_rows // 2)


HBM_SPEC = pl.BlockSpec(memory_space=pltpu.HBM)
SEM_SPEC = pl.BlockSpec(memory_space=pltpu.SEMAPHORE)
IN_FLIGHT = pltpu.CompilerParams(has_side_effects=pltpu.SideEffectType.DATAFLOW_SIDE_EFFECTING)


def _in_hbm(a):
    return pltpu.with_memory_space_constraint(a, pltpu.HBM)


def cast_to_slot(ws, chip, layer, after=()):
    n = len(ws)
    steps = 4

    def body(b_ref, *refs):
        del b_ref
        for w_ref, o_ref in zip(refs[:n], refs[n + len(after):]):
            o_ref[...] = w_ref[...].astype(BF16)

    grid_spec = pltpu.PrefetchScalarGridSpec(
        num_scalar_prefetch=1, grid=(steps,),
        in_specs=[pl.BlockSpec((None, w.shape[1] // steps, w.shape[2]), lambda r, b: (layer, r, 0)) for w in ws]
        + [_any()] * len(after),
        out_specs=[pl.BlockSpec((None, w.shape[1] // steps, w.shape[2]), lambda r, b: (b[0], r, 0)) for w in ws])
    return pl.pallas_call(body, grid_spec=grid_spec,
                          out_shape=[jax.ShapeDtypeStruct((NCHIP,) + w.shape[1:], BF16) for w in ws],
                          compiler_params=_cp(("arbitrary",)), name="cast_to_slot")(chip, *ws, *after)


def _gather_copies(bufs, send, recv):
    x, y, c, chips = _place()
    b = 2 * x + y
    out = []
    for k, buf in enumerate(bufs):
        rows = buf.shape[1]
        mine = buf.at[b, _half(rows, c), :]
        for j, (cx, cy) in enumerate(chips):
            theirs = buf.at[2 * cx + cy, _half(rows, c), :]
            sems = dict(send_sem=send.at[3 * k + j], recv_sem=recv.at[3 * k + j],
                        device_id=(cx, cy, c), device_id_type=MESH)
            out.append((pltpu.make_async_remote_copy(src_ref=mine, dst_ref=mine, **sems),
                        pltpu.make_async_remote_copy(src_ref=theirs, dst_ref=theirs, **sems)))
    return out


def gather_start(bufs, after, layer):
    n = len(bufs)

    def body(*refs):
        ins = refs[:n]
        send, recv = refs[n + 1], refs[n + 2]
        token = refs[-1]
        for start, _ in _gather_copies(ins, send, recv):
            start.start()
        token[...] = jnp.zeros_like(token)

    sems = pltpu.SemaphoreType.DMA((3 * n,))
    res = pl.pallas_call(
        body, name=f"gather_start_{layer}",
        in_specs=[HBM_SPEC] * n + [_any()],
        out_specs=[SEM_SPEC, SEM_SPEC] + [HBM_SPEC] * n + [pl.BlockSpec(memory_space=pltpu.VMEM)],
        out_shape=[sems, sems] + [pltpu.HBM(b.shape, b.dtype) for b in bufs] + [jax.ShapeDtypeStruct((8, LANES), F32)],
        input_output_aliases={k: 2 + k for k in range(n)}, compiler_params=IN_FLIGHT,
    )(*[_in_hbm(b) for b in bufs], after)
    return res[0], res[1], res[2:2 + n], res[-1]


def gather_wait(send, recv, bufs, after, layer):
    n = len(bufs)

    def body(*refs):
        ins = refs[:n]
        send_ref, recv_ref = refs[n], refs[n + 1]
        for start, arrival in _gather_copies(ins, send_ref, recv_ref):
            start.wait_send()
            arrival.wait_recv()

    return pl.pallas_call(
        body, name=f"gather_wait_{layer}",
        in_specs=[HBM_SPEC] * n + [SEM_SPEC, SEM_SPEC, _any()], out_specs=[HBM_SPEC] * n,
        out_shape=[pltpu.HBM(b.shape, b.dtype) for b in bufs],
        input_output_aliases={k: k for k in range(n)}, compiler_params=IN_FLIGHT,
    )(*bufs, send, recv, after)


def gather_forward(bufs):
    n = len(bufs)

    def body(*refs):
        outs = refs[n:2 * n]
        send, recv = refs[2 * n:]
        x, y, c, chips = _place()
        cps = []
        for k in range(n):
            rows = outs[k].shape[1]
            for j, (cx, cy) in enumerate(chips):
                sems = dict(send_sem=send.at[3 * k + j], recv_sem=recv.at[3 * k + j],
                            device_id=(x, y, 1 - c), device_id_type=MESH)
                mine = outs[k].at[2 * cx + cy, _half(rows, c), :]
                theirs = outs[k].at[2 * cx + cy, _half(rows, 1 - c), :]
                cp = pltpu.make_async_remote_copy(src_ref=mine, dst_ref=mine, **sems)
                cp.start()
                cps.append((cp, pltpu.make_async_remote_copy(src_ref=theirs, dst_ref=theirs, **sems)))
        for cp, arrival in cps:
            cp.wait_send()
            arrival.wait_recv()

    return pl.pallas_call(
        body, in_specs=[_any()] * n, out_specs=[_any()] * n,
        out_shape=[jax.ShapeDtypeStruct(b.shape, b.dtype) for b in bufs], input_output_aliases={k: k for k in range(n)},
        scratch_shapes=[pltpu.SemaphoreType.DMA((3 * n,)), pltpu.SemaphoreType.DMA((3 * n,))],
        name="gather_forward")(*bufs)


def _forward_copies(bufs, send, recv):
    x, y, c, chips = _place()
    out = []
    for k, buf in enumerate(bufs):
        rows = buf.shape[1]
        for j, (cx, cy) in enumerate(chips):
            sems = dict(send_sem=send.at[3 * k + j], recv_sem=recv.at[3 * k + j],
                        device_id=(x, y, 1 - c), device_id_type=MESH)
            mine = buf.at[2 * cx + cy, _half(rows, c), :]
            theirs = buf.at[2 * cx + cy, _half(rows, 1 - c), :]
            out.append((pltpu.make_async_remote_copy(src_ref=mine, dst_ref=mine, **sems),
                        pltpu.make_async_remote_copy(src_ref=theirs, dst_ref=theirs, **sems)))
    return out


def forward_start(bufs, tag):
    n = len(bufs)

    def body(*refs):
        ins = refs[:n]
        send, recv = refs[n], refs[n + 1]
        token = refs[-1]
        for start, _ in _forward_copies(ins, send, recv):
            start.start()
        token[...] = jnp.zeros_like(token)

    sems = pltpu.SemaphoreType.DMA((3 * n,))
    res = pl.pallas_call(
        body, name=f"forward_start_{tag}", in_specs=[HBM_SPEC] * n,
        out_specs=[SEM_SPEC, SEM_SPEC] + [HBM_SPEC] * n + [pl.BlockSpec(memory_space=pltpu.VMEM)],
        out_shape=[sems, sems] + [pltpu.HBM(b.shape, b.dtype) for b in bufs] + [jax.ShapeDtypeStruct((8, LANES), F32)],
        input_output_aliases={k: 2 + k for k in range(n)}, compiler_params=IN_FLIGHT,
    )(*[_in_hbm(b) for b in bufs])
    return res[0], res[1], res[2:2 + n], res[-1]


def forward_wait(send, recv, bufs, after, tag):
    n = len(bufs)

    def body(*refs):
        ins = refs[:n]
        send_ref, recv_ref = refs[n], refs[n + 1]
        for start, arrival in _forward_copies(ins, send_ref, recv_ref):
            start.wait_send()
            arrival.wait_recv()

    return pl.pallas_call(
        body, name=f"forward_wait_{tag}",
        in_specs=[HBM_SPEC] * n + [SEM_SPEC, SEM_SPEC, _any()], out_specs=[HBM_SPEC] * n,
        out_shape=[pltpu.HBM(b.shape, b.dtype) for b in bufs],
        input_output_aliases={k: k for k in range(n)}, compiler_params=IN_FLIGHT,
    )(*bufs, send, recv, after)


def _exchange_copies(srcs, lands, send, recv):
    x, y, c, _ = _place()
    return [pltpu.make_async_remote_copy(
        src_ref=src.at[:, _half(src.shape[1], 1 - c), :], dst_ref=land, send_sem=send.at[k], recv_sem=recv.at[k],
        device_id=(x, y, 1 - c), device_id_type=MESH) for k, (src, land) in enumerate(zip(srcs, lands))]


def exchange_start(srcs, tag):
    n = len(srcs)
    lands = [lax.empty((s.shape[0], s.shape[1] // 2, s.shape[2]), s.dtype) for s in srcs]

    def body(*refs):
        ins, land_refs = refs[:n], refs[n:2 * n]
        send, recv = refs[2 * n], refs[2 * n + 1]
        token = refs[-1]
        for cp in _exchange_copies(ins, land_refs, send, recv):
            cp.start()
        token[...] = jnp.zeros_like(token)

    sems = pltpu.SemaphoreType.DMA((n,))
    res = pl.pallas_call(
        body, name=f"exchange_start_{tag}",
        in_specs=[HBM_SPEC] * (2 * n),
        out_specs=[SEM_SPEC, SEM_SPEC] + [HBM_SPEC] * (2 * n) + [pl.BlockSpec(memory_space=pltpu.VMEM)],
        out_shape=[sems, sems] + [pltpu.HBM(a.shape, a.dtype) for a in list(srcs) + lands]
        + [jax.ShapeDtypeStruct((8, LANES), F32)],
        input_output_aliases={k: 2 + k for k in range(2 * n)}, compiler_params=IN_FLIGHT,
    )(*[_in_hbm(a) for a in list(srcs) + lands])
    return res[0], res[1], res[2:2 + n], res[2 + n:2 + 2 * n], res[-1]


def exchange_wait(send, recv, srcs, lands, after, tag):
    n = len(srcs)

    def body(*refs):
        ins, land_refs = refs[:n], refs[n:2 * n]
        send_ref, recv_ref = refs[2 * n], refs[2 * n + 1]
        for cp in _exchange_copies(ins, land_refs, send_ref, recv_ref):
            cp.wait_send()
            cp.wait_recv()

    res = pl.pallas_call(
        body, name=f"exchange_wait_{tag}",
        in_specs=[HBM_SPEC] * (2 * n) + [SEM_SPEC, SEM_SPEC, _any()], out_specs=[HBM_SPEC] * (2 * n),
        out_shape=[pltpu.HBM(a.shape, a.dtype) for a in list(srcs) + list(lands)],
        input_output_aliases={k: k for k in range(2 * n)}, compiler_params=IN_FLIGHT,
    )(*srcs, *lands, send, recv, after)
    return res[:n], res[n:]


def add_pair(gs, r1s, core):
    n = len(gs)

    def body(c_ref, *refs):
        del c_ref
        for g_ref, r_ref, o_ref in zip(refs[:n], refs[n:2 * n], refs[2 * n:]):
            o_ref[...] = (g_ref[...] + r_ref[...]).astype(BF16)

    blk = lambda r: (None,) + r.shape[1:]
    grid_spec = pltpu.PrefetchScalarGridSpec(
        num_scalar_prefetch=1, grid=(NCHIP,),
        in_specs=[pl.BlockSpec(blk(r), lambda s, c: (s, c[0], 0)) for r in r1s]
        + [pl.BlockSpec(blk(r), lambda s, c: (s, 0, 0)) for r in r1s],
        out_specs=[pl.BlockSpec(blk(r), lambda s, c: (s, 0, 0)) for r in r1s])
    return pl.pallas_call(body, grid_spec=grid_spec, out_shape=[jax.ShapeDtypeStruct(r.shape, BF16) for r in r1s],
                          compiler_params=_cp(("arbitrary",)), name="add_pair")(core, *gs, *r1s)


def _scatter_copies(srcs, lands, send, recv):
    _, _, c, chips = _place()
    out = []
    for k, (src, land) in enumerate(zip(srcs, lands)):
        for j, (cx, cy) in enumerate(chips):
            out.append(pltpu.make_async_remote_copy(
                src_ref=src.at[2 * cx + cy], dst_ref=land.at[j], send_sem=send.at[3 * k + j],
                recv_sem=recv.at[3 * k + j], device_id=(cx, cy, c), device_id_type=MESH))
    return out


def scatter_start(srcs, layer):
    n = len(srcs)
    srcs = list(srcs)
    lands = [lax.empty((3,) + s.shape[1:], s.dtype) for s in srcs]

    def body(*refs):
        ins, land_refs = refs[:n], refs[n:2 * n]
        send, recv = refs[2 * n], refs[2 * n + 1]
        token = refs[-1]
        for cp in _scatter_copies(ins, land_refs, send, recv):
            cp.start()
        token[...] = jnp.zeros_like(token)

    sems = pltpu.SemaphoreType.DMA((3 * n,))
    res = pl.pallas_call(
        body, name=f"scatter_start_{layer}",
        in_specs=[HBM_SPEC] * (2 * n),
        out_specs=[SEM_SPEC, SEM_SPEC] + [HBM_SPEC] * (2 * n) + [pl.BlockSpec(memory_space=pltpu.VMEM)],
        out_shape=[sems, sems] + [pltpu.HBM(a.shape, a.dtype) for a in srcs + lands]
        + [jax.ShapeDtypeStruct((8, LANES), F32)],
        input_output_aliases={k: 2 + k for k in range(2 * n)}, compiler_params=IN_FLIGHT,
    )(*[_in_hbm(a) for a in srcs + lands])
    return res[0], res[1], res[2:2 + n], res[2 + n:2 + 2 * n], res[-1]


def scatter_wait(send, recv, srcs, lands, after, layer):
    n = len(srcs)

    def body(*refs):
        ins, land_refs = refs[:n], refs[n:2 * n]
        send_ref, recv_ref = refs[2 * n], refs[2 * n + 1]
        for cp in _scatter_copies(ins, land_refs, send_ref, recv_ref):
            cp.wait_send()
            cp.wait_recv()

    res = pl.pallas_call(
        body, name=f"scatter_wait_{layer}",
        in_specs=[HBM_SPEC] * (2 * n) + [SEM_SPEC, SEM_SPEC, _any()], out_specs=[HBM_SPEC] * (2 * n),
        out_shape=[pltpu.HBM(a.shape, a.dtype) for a in list(srcs) + list(lands)],
        input_output_aliases={k: k for k in range(2 * n)}, compiler_params=IN_FLIGHT,
    )(*srcs, *lands, send, recv, after)
    return res[n:]


def add_chips(gs, r1s, r2s, place, totals, layer):
    n = len(gs)
    steps = 2

    def body(p_ref, *refs):
        del p_ref
        for g_ref, r1_ref, r2_ref, o_ref in zip(refs[:n], refs[n:2 * n], refs[2 * n:3 * n], refs[4 * n:]):
            own = g_ref[...] + r1_ref[...]
            o_ref[...] = ((own + r2_ref[0].astype(F32)) + r2_ref[1].astype(F32)) + r2_ref[2].astype(F32)

    blk = lambda r: (None, r.shape[1] // steps, r.shape[2])
    grid_spec = pltpu.PrefetchScalarGridSpec(
        num_scalar_prefetch=1, grid=(steps,),
        in_specs=[pl.BlockSpec(blk(r), lambda i, p: (p[1], p[0] * steps + i, 0)) for r in r1s]
        + [pl.BlockSpec(blk(r), lambda i, p: (p[1], i, 0)) for r in r1s]
        + [pl.BlockSpec((3,) + blk(r)[1:], lambda i, p: (0, i, 0)) for r in r1s] + [_any()] * n,
        out_specs=[pl.BlockSpec(blk(r), lambda i, p: (layer, p[0] * steps + i, 0)) for r in r1s])
    return pl.pallas_call(body, grid_spec=grid_spec, out_shape=[jax.ShapeDtypeStruct(t.shape, F32) for t in totals],
                          input_output_aliases={1 + 3 * n + k: k for k in range(n)},
                          compiler_params=_cp(("arbitrary",)), name="add_chips")(place, *gs, *r1s, *r2s, *totals)


def _share_copies(bufs, send, recv):
    x, y, c, _ = _place()
    out = []
    for k, buf in enumerate(bufs):
        sems = dict(send_sem=send.at[k], recv_sem=recv.at[k], device_id=(x, y, 1 - c), device_id_type=MESH)
        mine = buf.at[:, _half(buf.shape[1], c), :]
        theirs = buf.at[:, _half(buf.shape[1], 1 - c), :]
        out.append((pltpu.make_async_remote_copy(src_ref=mine, dst_ref=mine, **sems),
                    pltpu.make_async_remote_copy(src_ref=theirs, dst_ref=theirs, **sems)))
    return out


def share_start(bufs, tag):
    n = len(bufs)

    def body(*refs):
        ins = refs[:n]
        send, recv = refs[n], refs[n + 1]
        token = refs[-1]
        for start, _ in _share_copies(ins, send, recv):
            start.start()
        token[...] = jnp.zeros_like(token)

    sems = pltpu.SemaphoreType.DMA((n,))
    res = pl.pallas_call(
        body, name=f"share_start_{tag}", in_specs=[HBM_SPEC] * n,
        out_specs=[SEM_SPEC, SEM_SPEC] + [HBM_SPEC] * n + [pl.BlockSpec(memory_space=pltpu.VMEM)],
        out_shape=[sems, sems] + [pltpu.HBM(b.shape, b.dtype) for b in bufs] + [jax.ShapeDtypeStruct((8, LANES), F32)],
        input_output_aliases={k: 2 + k for k in range(n)}, compiler_params=IN_FLIGHT,
    )(*[_in_hbm(b) for b in bufs])
    return res[0], res[1], res[2:2 + n], res[-1]


def share_wait(send, recv, bufs, after, tag):
    n = len(bufs)

    def body(*refs):
        ins = refs[:n]
        send_ref, recv_ref = refs[n], refs[n + 1]
        for start, arrival in _share_copies(ins, send_ref, recv_ref):
            start.wait_send()
            arrival.wait_recv()

    return pl.pallas_call(
        body, name=f"share_wait_{tag}",
        in_specs=[HBM_SPEC] * n + [SEM_SPEC, SEM_SPEC, _any()], out_specs=[HBM_SPEC] * n,
        out_shape=[pltpu.HBM(b.shape, b.dtype) for b in bufs],
        input_output_aliases={k: k for k in range(n)}, compiler_params=IN_FLIGHT,
    )(*bufs, send, recv, after)


def small_allreduce(v, after=()):
    rows = v.shape[0]
    flips = [(fx, fy, fc) for fx in (0, 1) for fy in (0, 1) for fc in (0, 1)][1:]

    def body(v_ref, o_ref, buf, send, recv):
        x, y, c, _ = _place()
        buf[4 * x + 2 * y + c] = v_ref[...]
        peers = [(jnp.where(fx, 1 - x, x), jnp.where(fy, 1 - y, y), jnp.where(fc, 1 - c, c)) for fx, fy, fc in flips]
        cps = []
        for k, peer in enumerate(peers):
            cp = pltpu.make_async_remote_copy(
                src_ref=v_ref, dst_ref=buf.at[4 * x + 2 * y + c], send_sem=send.at[k], recv_sem=recv.at[k],
                device_id=peer, device_id_type=MESH)
            cp.start()
            cps.append(cp)
        for k, (px, py, pc) in enumerate(peers):
            pltpu.make_async_remote_copy(
                src_ref=v_ref, dst_ref=buf.at[4 * px + 2 * py + pc], send_sem=send.at[k], recv_sem=recv.at[k],
                device_id=(px, py, pc), device_id_type=MESH).wait_recv()
        for cp in cps:
            cp.wait_send()
        acc = buf[0]
        for s in range(1, 8):
            acc = acc + buf[s]
        o_ref[...] = acc

    vm = pl.BlockSpec(memory_space=pltpu.VMEM)
    return pl.pallas_call(
        _behind(body, 1, after), in_specs=[vm] + [_any()] * len(after), out_specs=vm,
        out_shape=jax.ShapeDtypeStruct((rows, SMALL_COLS), F32),
        scratch_shapes=[pltpu.VMEM((8, rows, SMALL_COLS), F32), pltpu.SemaphoreType.DMA((7,)),
                        pltpu.SemaphoreType.DMA((7,))],
        name="reduce_small")(v, *after)


def adamw(w, g, m, v, rb, name, after=()):
    nl, rows, cols = w.shape

    def body(w_ref, g_ref, m_ref, v_ref, go_ref, d_ref, nm_ref, nv_ref):
        gv = g_ref[...]
        go_ref[...] = gv
        nm = ADAM_B1 * m_ref[...] + (1.0 - ADAM_B1) * gv
        nv = ADAM_B2 * v_ref[...] + (1.0 - ADAM_B2) * (gv * gv)
        m_hat = nm / (1.0 - ADAM_B1 ** ADAM_STEP)
        v_hat = nv / (1.0 - ADAM_B2 ** ADAM_STEP)
        d_ref[...] = -ADAM_LR * (m_hat / (jnp.sqrt(v_hat) + ADAM_EPS) + ADAM_WD * w_ref[...])
        nm_ref[...] = nm
        nv_ref[...] = nv

    blk = pl.BlockSpec((None, rb, cols), lambda l, r: (l, r, 0))
    shp = jax.ShapeDtypeStruct(w.shape, F32)
    return pl.pallas_call(_behind(body, 4, after), grid=(nl, rows // rb), in_specs=[blk] * 4 + [_any()] * len(after),
                          out_specs=[blk] * 4, out_shape=[shp] * 4,
                          compiler_params=_cp(("arbitrary", "arbitrary")), name=name)(w, g, m, v, *after)


def _pack(parts, rows):
    flat = jnp.concatenate([p.reshape(-1).astype(F32) for p in parts])
    return jnp.pad(flat, (0, rows * SMALL_COLS - flat.shape[0])).reshape(rows, SMALL_COLS)


def _unpack(vec, shapes):
    flat = vec.reshape(-1)
    out, off = [], 0
    for s in shapes:
        size = 1
        for d in s:
            size *= d
        out.append(flat[off:off + size].reshape(s))
        off += size
    return out


def kernel(x, w_in, w_conv, rel_bias, g_conv_out, g_attn_out, w_out, g_pre_mix, g_post_mix, g_pre_ffn, g_post_ffn, w_ffn_in, w_ffn_out, loss_target, m_w_in, m_w_conv, m_rel_bias, m_g_conv_out, m_g_attn_out, m_w_out, m_g_pre_mix, m_g_post_mix, m_g_pre_ffn, m_g_post_ffn, m_w_ffn_in, m_w_ffn_out, v_w_in, v_w_conv, v_rel_bias, v_g_conv_out, v_g_attn_out, v_w_out, v_g_pre_mix, v_g_post_mix, v_g_pre_ffn, v_g_post_ffn, v_w_ffn_in, v_w_ffn_out):
    xi, yi, ci = lax.axis_index("x"), lax.axis_index("y"), lax.axis_index("c")
    chip = 2 * xi + yi
    nl = w_in.shape[0]
    x0 = x[0]
    target = loss_target[0]
    cwl = CW // NCHIP

    chip1 = chip.reshape(1).astype(jnp.int32)
    big_weights = [w_in, w_out, w_ffn_in, w_ffn_out]
    own = [cast_to_slot(big_weights, chip1, 0)]
    wc_mine = jnp.pad(w_conv.reshape(-1), (0, 16 * LANES - w_conv.size)).reshape(1, 16, LANES)
    wc_slot = lax.dynamic_update_slice_in_dim(jnp.zeros((NCHIP, 16, LANES), F32), wc_mine, chip, axis=0)
    gm = jnp.kron(jnp.eye(CW // HD, dtype=F32), jnp.full((HD, HD), 1.0 / HD, F32)).astype(BF16)
    row = lambda a, l: a[l][None, :]

    def gather_finish(flight, after, tag):
        send, recv, bufs, _ = flight
        return gather_forward(gather_wait(send, recv, bufs, after, tag))

    first_mix = gather_start(list(own[0][:2]) + [wc_slot], x0, "0m")
    first_ffn = gather_start(own[0][2:], first_mix[3], "0f")
    chain = first_ffn[3]
    biases = []
    for l in range(nl):
        biases.append(bias_expand(_diag_vector(rel_bias[l]), (QG_FWD, QG_BWD), [chain]))
        chain = biases[l][1]
    own.append(cast_to_slot(big_weights, chip1, 1, [chain]))
    send, recv, bufs, _ = first_mix
    first_mix_on = forward_start(gather_wait(send, recv, bufs, own[1][0], "0m"), "0m")
    chain = first_mix_on[3]
    for l in range(2, nl):
        own.append(cast_to_slot(big_weights, chip1, l, [chain]))
        chain = own[l][0]
    gw_in, gw_out, wc_all = forward_wait(*first_mix_on[:3], chain, "0m")
    wc_full = wc_all.reshape(NCHIP, -1)[:, :nl * cwl * 3].reshape(NCHIP, nl, cwl, 3)
    wc_full = jnp.transpose(wc_full, (1, 0, 2, 3)).reshape(nl, CW, 3)
    wconv_t = jnp.pad(jnp.transpose(wc_full, (0, 2, 1)), ((0, 0), (0, 5), (0, 0)))
    flights, to_sibling = {}, None
    saved, weights = [], []
    h = x0
    for l in range(nl):
        behind_mix, behind_ffn = [], []
        if l == 0:
            flights["1m"] = gather_start(own[1][:2], first_ffn[3], "1m")
            flights["1f"] = gather_start(own[1][2:], flights["1m"][3], "1f")
            behind_mix += [first_ffn[3], flights["1m"][3], flights["1f"][3]]
        elif l == 1:
            gw_in, gw_out = forward_wait(*next_mix_on[:3], h, "1m")
            flights[2] = gather_start(own[2], h, 2)
            send, recv, bufs, _ = flights["1f"]
            next_ffn_on = forward_start(gather_wait(send, recv, bufs, flights[2][3], "1f"), "1f")
            behind_mix.append(next_ffn_on[3])
        else:
            gw_in, gw_out, gw_fi, gw_fo = forward_wait(*to_sibling[:3], h, l)
        gw_out = gw_out.reshape(D, D)
        if 1 < l + 1 < nl and l + 1 not in flights:
            flights[l + 1] = gather_start(own[l + 1], gw_in, l + 1)
            behind_mix.append(flights[l + 1][3])
        bias2, bias2_bwd = biases[l]
        proj = fwd_inproj(h, row(g_pre_mix, l), gw_in, behind_mix)
        xmid, o, lse, y, z = fwd_mix(h, proj, bias2, wconv_t[l], row(g_conv_out, l), row(g_attn_out, l),
                                     row(g_post_mix, l), gm, gw_out)
        if l == 0:
            gw_fi, gw_fo = gather_finish(first_ffn, xmid, "0f")
            send, recv, bufs, _ = flights["1m"]
            next_mix_on = forward_start(gather_wait(send, recv, bufs, gw_fi, "1m"), "1m")
            behind_ffn.append(next_mix_on[3])
        elif l == 1:
            gw_fi, gw_fo = forward_wait(*next_ffn_on[:3], xmid, "1f")
        if 1 <= l and l + 1 < nl:
            send, recv, bufs, _ = flights[l + 1]
            landed = gather_wait(send, recv, bufs, xmid, l + 1)
            to_sibling = forward_start(landed, l + 1)
            behind_ffn.append(to_sibling[3])
            if l + 2 < nl:
                flights[l + 2] = gather_start(own[l + 2], to_sibling[3], l + 2)
                behind_ffn.append(flights[l + 2][3])
        gw_fo = gw_fo.reshape(2, DFF // 2, D)
        ffn = fwd_ffn(xmid, row(g_pre_ffn, l), row(g_post_ffn, l), gw_fi, gw_fo, behind_ffn,
                      target if l == nl - 1 else None)
        gu, f = ffn[:2]
        saved.append((h, proj, bias2_bwd, xmid, o, lse, y, z, gu, f))
        weights.append((gw_in, gw_out, gw_fi, gw_fo))
        h = ffn[2]
    dx, loss_blk = ffn[2], ffn[3]

    core = ci.reshape(1).astype(jnp.int32)
    place = jnp.stack([ci, chip]).astype(jnp.int32)
    totals = [lax.empty(w.shape, F32) for w in (w_in, w_out, w_ffn_in, w_ffn_out)]
    small = {k: [None] * nl for k in ("co", "ao", "pm", "qm", "pf", "qf", "rel", "wc")}

    def reduce_begin(kinds, grads, tag):
        return kinds, exchange_start(grads, tag), tag

    def reduce_mid(state, after):
        kinds, (send, recv, srcs, lands, _), tag = state
        grads, from_sibling = exchange_wait(send, recv, srcs, lands, after, tag)
        return kinds, grads, from_sibling, scatter_start(add_pair(grads, from_sibling, core), tag), tag

    def reduce_end(state, after, totals, layer):
        kinds, grads, from_sibling, (send, recv, srcs, lands, _), tag = state
        from_chips = scatter_wait(send, recv, srcs, lands, after, tag)
        totals = list(totals)
        summed = add_chips(grads, from_sibling, from_chips, place, [totals[i] for i in kinds], layer)
        for i, t in zip(kinds, summed):
            totals[i] = t
        return totals

    begun = flying = None
    for l in reversed(range(nl)):
        hin, proj, bias2, xmid, o, lse, y, z, gu, f = saved[l]
        gw_in, gw_out, gw_fi, gw_fo = weights[l]
        behind_ffn = [begun[1][4]] if begun is not None else []
        dxm, dfb, act, dgu, h2, dg_qf, dg_pf = bwd_ffn(dx, f, xmid, gu, row(g_pre_ffn, l), row(g_post_ffn, l),
                                                        gw_fi, gw_fo, behind_ffn)
        behind_mix, behind_conv = [], []
        if begun is not None:
            flying = reduce_mid(begun, dxm)
            behind_mix.append(flying[3][4])
        gr_fo = wgrad(act, dfb, 256, D, False, "wgrad_ffn_out").reshape(NCHIP, DFF // NCHIP, D)
        gr_fi = wgrad(h2, dgu, 512, 2 * DFF // NCHIP, True, "wgrad_ffn_in")
        if l == 0:
            begun_ffn = reduce_begin([2, 3], [gr_fi, gr_fo], "0f")
            behind_mix.append(begun_ffn[1][4])
        dzb, do, dco, dbg, dg_qm, dg_co, dg_ao = bwd_mix(dxm, z, o, proj, wconv_t[l], row(g_conv_out, l),
                                                          row(g_attn_out, l), row(g_post_mix, l), gm, gw_out,
                                                          behind_mix)
        if l == 0:
            flying_ffn = reduce_mid(begun_ffn, dzb)
            behind_conv.append(flying_ffn[3][4])
        gr_out = wgrad(y, dzb, 512, D, False, "wgrad_out").reshape(NCHIP, D // NCHIP, D)
        dhc, dcg, dwc = bwd_conv(dco, proj, wconv_t[l], behind_conv)
        dq, dk, dv, db2 = bwd_attn(proj, o, do, lse, bias2)
        dx, dproj, hb, dg_pm = bwd_inproj(dxm, hin, dhc, dbg, dcg, dq, dk, dv, row(g_pre_mix, l), gw_in)
        if flying is not None:
            totals = reduce_end(flying, dx, totals, l + 1)
        gr_in = wgrad(hb, dproj, 512, PROJ // NCHIP, True, "wgrad_in")
        small["co"][l], small["ao"][l], small["pm"][l], small["qm"][l] = dg_co, dg_ao, dg_pm, dg_qm
        small["pf"][l], small["qf"][l] = dg_pf, dg_qf
        small["rel"][l] = _diag_vector_bwd(bias_reduce(db2.reshape(NH, QG_BWD, QG_BWD + LEFT)))
        small["wc"][l] = jnp.transpose(dwc[0:3], (1, 0))
        if l > 0:
            begun = reduce_begin([0, 1, 2, 3], [gr_in, gr_out, gr_fi, gr_fo], l)
    begun_mix = reduce_begin([0, 1], [gr_in, gr_out], "0m")
    totals = reduce_end(flying_ffn, begun_mix[1][4], totals, 0)
    flying_mix = reduce_mid(begun_mix, totals[2])
    share_ffn = share_start(totals[2:], "ffn")

    order = ("co", "ao", "pm", "qm", "pf", "qf", "rel", "wc")
    parts = [jnp.stack(small[k]) for k in order] + [loss_blk[0:1, 0:1]]
    shapes = [p.shape for p in parts]
    red_vec = small_allreduce(_pack(parts, 40), [share_ffn[3], flying_mix[3][4]])
    red = _unpack(red_vec, shapes)

    gr_fi, gr_fo = share_wait(*share_ffn[:3], red_vec, "ffn")
    big_fi = adamw(w_ffn_in, gr_fi, m_w_ffn_in, v_w_ffn_in, w_ffn_in.shape[1] // 4, "adamw_ffn_in")
    totals = reduce_end(flying_mix, big_fi[1], totals, 0)
    share_mix = share_start(totals[:2], "mix")
    big_fo = adamw(w_ffn_out, gr_fo, m_w_ffn_out, v_w_ffn_out, w_ffn_out.shape[1] // 4, "adamw_ffn_out",
                   [share_mix[3]])
    gr_in, gr_out = share_wait(*share_mix[:3], big_fo[1], "mix")
    big_in = adamw(w_in, gr_in, m_w_in, v_w_in, w_in.shape[1] // 4, "adamw_in")
    big_out = adamw(w_out, gr_out, m_w_out, v_w_out, w_out.shape[1] // 4, "adamw_out")
    big = [big_in, big_out, big_fi, big_fo]
    gr_co, gr_ao, gr_pm, gr_qm, gr_pf, gr_qf, gr_rel, gr_wc_full, loss = red
    gr_co, gr_ao, gr_pm, gr_qm, gr_pf, gr_qf = [a.reshape(nl, -1) for a in (gr_co, gr_ao, gr_pm, gr_qm, gr_pf, gr_qf)]
    gr_wc = lax.dynamic_slice_in_dim(gr_wc_full, chip * cwl, cwl, axis=1)
    loss = loss.reshape(())

    sw = [g_conv_out, g_attn_out, g_pre_mix, g_post_mix, g_pre_ffn, g_post_ffn, rel_bias, w_conv]
    sg = [gr_co, gr_ao, gr_pm, gr_qm, gr_pf, gr_qf, gr_rel, gr_wc]
    sm = [m_g_conv_out, m_g_attn_out, m_g_pre_mix, m_g_post_mix, m_g_pre_ffn, m_g_post_ffn, m_rel_bias, m_w_conv]
    sv = [v_g_conv_out, v_g_attn_out, v_g_pre_mix, v_g_post_mix, v_g_pre_ffn, v_g_post_ffn, v_rel_bias, v_w_conv]
    sshapes = [a.shape for a in sw]
    packed = [_pack(a, 32)[None] for a in (sw, sg, sm, sv)]
    s_out = [_unpack(a[0], sshapes) for a in adamw(*packed, 32, "adamw_small")]

    def leaves(big_i, small_i):
        b_in, b_out, b_fi, b_fo = big_i
        s_co, s_ao, s_pm, s_qm, s_pf, s_qf, s_rel, s_wc = small_i
        return [b_in, s_wc, s_rel, s_co, s_ao, b_out, s_pm, s_qm, s_pf, s_qf, b_fi, b_fo]

    out = [loss, dx[None]]
    out += leaves([b[0] for b in big], sg)
    for i in range(1, 4):
        out += leaves([b[i] for b in big], s_out[i])
    return tuple(out)
```

```python
import jax
import jax.numpy as jnp
from jax import lax
from jax.experimental import pallas as pl
from jax.experimental.pallas import tpu as pltpu

F32 = jnp.float32
BF16 = jnp.bfloat16

D = 1024
PROJ = 3072
CW = 512
HD = 64
NH = 8
CHUNK = 64
BAND = 576
REL_CLIP = 128
NREL = 2 * REL_CLIP + 1
DFF = 2816
DEPTH = 4
NCHIP = 4
EPS = 1e-6
NEG_INF = -1e30

ADAM_LR = 0.001
ADAM_B1 = 0.9
ADAM_B2 = 0.999
ADAM_EPS = 1e-08
ADAM_WD = 0.01
ADAM_STEP = 10

V7X_VMEM_BYTES = 64 * 1024 * 1024
VMEM_LIMIT = V7X_VMEM_BYTES - 8 * 1024 * 1024
LANES = 128
QG_FWD = 4 * CHUNK
QG_BWD = 2 * CHUNK
LEFT = BAND - CHUNK
TQ = 512
TM = 256
SMALL_COLS = 1024
MESH = pl.DeviceIdType.MESH
NT = (((1,), (1,)), ((), ()))
TN = (((0,), (0,)), ((), ()))


def _cp(sem=None, vmem=VMEM_LIMIT):
    return pltpu.CompilerParams(dimension_semantics=sem, vmem_limit_bytes=vmem)


def _any():
    return pl.BlockSpec(memory_space=pl.ANY)


def _const(shape):
    nd = len(shape)
    return pl.BlockSpec(shape, lambda *_: (0,) * nd)


def _behind(body, n_in, after):
    def ordered(*refs):
        return body(*refs[:n_in], *refs[n_in + len(after):])
    return ordered


def _rms(v, g):
    r = lax.rsqrt(jnp.mean(v * v, axis=-1, keepdims=True) + EPS)
    return v * r * g


def _rms_bwd(dy, v, g):
    r = lax.rsqrt(jnp.mean(v * v, axis=-1, keepdims=True) + EPS)
    vh = v * r
    dg = jnp.sum(dy * vh, axis=0, keepdims=True)
    dvh = dy * g
    dv = r * (dvh - vh * jnp.mean(dvh * vh, axis=-1, keepdims=True))
    return dv, dg


def _group_mean(v, gm):
    return jnp.dot(v.astype(BF16), gm, preferred_element_type=F32)


def _group_rms_bwd(dy, v, g, gm):
    r = lax.rsqrt(_group_mean(v * v, gm) + EPS)
    vh = v * r
    dg = jnp.sum(dy * vh, axis=0, keepdims=True)
    dvh = dy * g
    dv = r * (dvh - vh * _group_mean(dvh * vh, gm))
    return dv, dg


def _head_masks(scale):
    lane = lax.broadcasted_iota(jnp.int32, (1, LANES), 1)
    return [jnp.where((lane >= HD * a) & (lane < HD * (a + 1)), scale, 0.0).astype(BF16) for a in range(2)]


class _Resident:
    def __init__(self, src, dst, sem):
        self.first = pl.program_id(0) == 0
        self.copy = pltpu.make_async_copy(src, dst, sem)
        self.dst = dst

        @pl.when(self.first)
        def _():
            self.copy.start()

    def read(self):
        @pl.when(self.first)
        def _():
            self.copy.wait()

        return self.dst[...]


FF_CHUNKS = ((0, 1536), (1536, DFF))


def _stream_ffn_weights(wfi_hbm, wfo_hbm, wfi_v, wfo_v, sems, order, step):
    hw = DFF // 2
    per_matrix = {
        0: [(wfi_hbm.at[j], wfi_v.at[0, :, pl.ds(hw * j, hw)]) for j in range(2)],
        1: [(wfi_hbm.at[2 + j], wfi_v.at[1, :, pl.ds(hw * j, hw)]) for j in range(2)],
        2: [(wfo_hbm.at[j], wfo_v.at[pl.ds(hw * j, hw), :]) for j in range(2)],
    }
    pieces = [p for m in order for p in per_matrix[m]]
    slot = {m: 2 * k for k, m in enumerate(order)}

    def make_step(wait):
        def ready(m, chunk):
            if chunk == 0:
                wait(slot[m])
                wait(slot[m] + 1)
        return lambda: step(ready)

    copies = [pltpu.make_async_copy(src, dst, sems.at[k]) for k, (src, dst) in enumerate(pieces)]
    first = pl.program_id(0) == 0

    @pl.when(first)
    def _():
        for cp in copies:
            cp.start()
        make_step(lambda k: copies[k].wait())()

    @pl.when(jnp.logical_not(first))
    def _():
        make_step(lambda k: None)()


def _conv_taps(u_prev, u, scr):
    n = u.shape[0]
    scr[0:16, :] = u_prev
    scr[16:16 + n, :] = u
    return scr[15:15 + n, :], scr[14:14 + n, :]


def fwd_inproj(x, g, w_all, after=()):
    t = x.shape[0]
    wc = PROJ // NCHIP

    def body(x_ref, g_ref, w_hbm, o_ref, w_v):
        @pl.when(pl.program_id(0) == 0)
        def _():
            pltpu.sync_copy(w_hbm, w_v)

        h = _rms(x_ref[...], g_ref[...]).astype(BF16)
        for b in range(NCHIP):
            o_ref[:, wc * b:wc * (b + 1)] = jnp.dot(h, w_v[b], preferred_element_type=F32).astype(BF16)

    return pl.pallas_call(
        _behind(body, 3, after), grid=(t // TQ,),
        in_specs=[pl.BlockSpec((TQ, D), lambda i: (i, 0)), _const((1, D)), _any()] + [_any()] * len(after),
        out_specs=pl.BlockSpec((TQ, PROJ), lambda i: (i, 0)),
        out_shape=jax.ShapeDtypeStruct((t, PROJ), BF16),
        scratch_shapes=[pltpu.VMEM((NCHIP, D, wc), BF16)],
        compiler_params=_cp(("arbitrary",)), name="fwd_inproj")(x, g, w_all, *after)


def _attn_window_specs():
    return [
        pl.BlockSpec((TQ, CW), lambda i: (i, 3)),
        pl.BlockSpec((TQ, CW), lambda i: (jnp.maximum(i - 1, 0), 4)),
        pl.BlockSpec((TQ, CW), lambda i: (i, 4)),
        pl.BlockSpec((TQ, CW), lambda i: (jnp.maximum(i - 1, 0), 5)),
        pl.BlockSpec((TQ, CW), lambda i: (i, 5)),
    ]


def _conv_specs():
    return [
        pl.BlockSpec((TQ, 3 * CW), lambda i: (i, 0)),
        pl.BlockSpec((16, 3 * CW), lambda i: (jnp.maximum(i * (TQ // 16) - 1, 0), 0)),
    ]


def _conv_fwd(pc_ref, pcp_ref, wc_ref, scr, first):
    pc = pc_ref[...].astype(F32)
    hc, bg, cg = pc[:, :CW], pc[:, CW:2 * CW], pc[:, 2 * CW:]
    u = cg * hc
    pp = pcp_ref[...].astype(F32)
    u_prev = jnp.where(first, 0.0, pp[:, 2 * CW:] * pp[:, :CW])
    u1, u2 = _conv_taps(u_prev, u, scr)
    cout = wc_ref[0:1, :] * u2 + wc_ref[1:2, :] * u1 + wc_ref[2:3, :] * u
    return hc, bg, cg, u, u1, u2, cout


def _key_penalty(first, r0, kg):
    col = lax.broadcasted_iota(jnp.int32, (1, kg), 1)
    limit = jnp.where(first, TQ - r0, 0)
    return jnp.where(col < limit, NEG_INF, 0.0)


def fwd_mix(x, proj, bias2, wconv_t, g_co, g_ao, g_pm, gm, wout_all):
    t = x.shape[0]
    qg, kg = QG_FWD, QG_FWD + LEFT

    def body(x_ref, pc_ref, pcp_ref, q_ref, kp_ref, kc_ref, vp_ref, vc_ref, b2_ref, wc_ref, gco_ref, gao_ref, gpm_ref,
             gm_ref, wout_hbm, xmid_ref, o_ref, lse_ref, y_ref, z_ref, wout_v, kwin, vwin, cscr, sems):
        i = pl.program_id(0)
        first = i == 0
        wout = _Resident(wout_hbm, wout_v, sems.at[0])
        kwin[0:TQ, :] = kp_ref[...]
        kwin[TQ:2 * TQ, :] = kc_ref[...]
        vwin[0:TQ, :] = vp_ref[...]
        vwin[TQ:2 * TQ, :] = vc_ref[...]
        qmask = _head_masks(HD ** -0.5)
        low = lax.broadcasted_iota(jnp.int32, (1, LANES), 1) < HD

        def group(g, carry):
            r0 = pl.multiple_of(g * qg, qg)
            pen = _key_penalty(first, r0, kg)
            for hp in range(NH // 2):
                ls = slice(LANES * hp, LANES * (hp + 1))
                qb = q_ref[pl.ds(r0, qg), ls]
                q2 = jnp.concatenate([qb * qmask[0], qb * qmask[1]], axis=0)
                s = lax.dot_general(q2, kwin[pl.ds(r0, kg), ls], NT, preferred_element_type=F32)
                s = s + b2_ref[hp] + pen
                m = jnp.max(s, axis=-1, keepdims=True)
                p = jnp.exp(s - m)
                l = jnp.sum(p, axis=-1, keepdims=True)
                o2 = jnp.dot(p.astype(BF16), vwin[pl.ds(r0, kg), ls], preferred_element_type=F32) * (1.0 / l)
                lse2 = m + jnp.log(l)
                o_ref[pl.ds(r0, qg), ls] = jnp.where(low, o2[:qg], o2[qg:])
                lse_ref[pl.ds(r0, qg), ls] = jnp.where(low, lse2[:qg], lse2[qg:])
            return carry

        lax.fori_loop(0, TQ // qg, group, 0)

        _, bg, _, _, _, _, cout = _conv_fwd(pc_ref, pcp_ref, wc_ref, cscr, first)
        yc = bg * cout
        gmv = gm_ref[...]
        ycn = yc * lax.rsqrt(_group_mean(yc * yc, gmv) + EPS) * gco_ref[...]
        oa = o_ref[...]
        oan = oa * lax.rsqrt(_group_mean(oa * oa, gmv) + EPS) * gao_ref[...]
        y_ref[:, 0:CW] = ycn.astype(BF16)
        y_ref[:, CW:2 * CW] = oan.astype(BF16)
        z = jnp.dot(y_ref[...], wout.read(), preferred_element_type=F32)
        z_ref[...] = z
        xmid_ref[...] = x_ref[...] + _rms(z, gpm_ref[...])

    row = lambda w: pl.BlockSpec((TQ, w), lambda i: (i, 0))
    return pl.pallas_call(
        body, grid=(t // TQ,),
        in_specs=[row(D)] + _conv_specs() + _attn_window_specs() + [
            _const((NH // 2, 2 * qg, kg)), _const((8, CW)), _const((1, CW)), _const((1, CW)), _const((1, D)),
            _const((CW, CW)), _any()],
        out_specs=[row(D), row(CW), row(CW), row(D), row(D)],
        out_shape=[jax.ShapeDtypeStruct((t, D), F32), jax.ShapeDtypeStruct((t, CW), F32),
                   jax.ShapeDtypeStruct((t, CW), F32), jax.ShapeDtypeStruct((t, D), BF16),
                   jax.ShapeDtypeStruct((t, D), F32)],
        scratch_shapes=[pltpu.VMEM((D, D), BF16), pltpu.VMEM((2 * TQ, CW), BF16), pltpu.VMEM((2 * TQ, CW), BF16),
                        pltpu.VMEM((TQ + 16, CW), F32), pltpu.SemaphoreType.DMA((1,))],
        compiler_params=_cp(("arbitrary",)), name="fwd_mix",
    )(x, proj, proj, proj, proj, proj, proj, proj, bias2, wconv_t, g_co, g_ao, g_pm, gm, wout_all)


def fwd_ffn(xmid, g_pre, g_post, wfi_all, wfo_all, after=(), target=None):
    t = xmid.shape[0]
    n_in = 5 if target is None else 6

    def body(*refs):
        x_ref, gpre_ref, gpost_ref, wfi_hbm, wfo_hbm = refs[:5]
        t_ref = None if target is None else refs[5]
        gu_ref, f_ref, xo_ref = refs[n_in:n_in + 3]
        l_ref = None if target is None else refs[n_in + 3]
        wfi_v, wfo_v, sems = refs[-3:]

        if target is not None:
            @pl.when(pl.program_id(0) == 0)
            def _():
                l_ref[...] = jnp.zeros_like(l_ref)

        def step(ready):
            xv = x_ref[...]
            h = _rms(xv, gpre_ref[...]).astype(BF16)
            f = jnp.zeros((TM, D), F32)
            for ci, (a, b) in enumerate(FF_CHUNKS):
                ready(0, ci)
                gate = jnp.dot(h, wfi_v[0, :, a:b], preferred_element_type=F32)
                ready(1, ci)
                up = jnp.dot(h, wfi_v[1, :, a:b], preferred_element_type=F32)
                gu_ref[:, a:b] = gate.astype(BF16)
                gu_ref[:, DFF + a:DFF + b] = up.astype(BF16)
                act = gate * (1.0 / (1.0 + jnp.exp(-gate))) * up
                ready(2, ci)
                f = f + jnp.dot(act.astype(BF16), wfo_v[a:b, :], preferred_element_type=F32)
            f_ref[...] = f
            xo = xv + _rms(f, gpost_ref[...])
            if target is None:
                xo_ref[...] = xo
            else:
                e = xo - t_ref[...]
                xo_ref[...] = e * (1.0 / D)
                rows = jnp.sum(e * e, axis=-1, keepdims=True) * (1.0 / D)
                l_ref[...] += 0.5 * jnp.sum(rows, axis=0, keepdims=True)

        _stream_ffn_weights(wfi_hbm, wfo_hbm, wfi_v, wfo_v, sems, (0, 1, 2), step)

    row = lambda w: pl.BlockSpec((TM, w), lambda i: (i, 0))
    with_loss = target is not None
    return pl.pallas_call(
        _behind(body, n_in, after), grid=(t // TM,),
        in_specs=[row(D), _const((1, D)), _const((1, D)), _any(), _any()] + [row(D)] * with_loss
        + [_any()] * len(after),
        out_specs=[row(2 * DFF), row(D), row(D)] + [_const((8, LANES))] * with_loss,
        out_shape=[jax.ShapeDtypeStruct((t, 2 * DFF), BF16), jax.ShapeDtypeStruct((t, D), F32),
                   jax.ShapeDtypeStruct((t, D), F32)] + [jax.ShapeDtypeStruct((8, LANES), F32)] * with_loss,
        scratch_shapes=[pltpu.VMEM((2, D, DFF), BF16), pltpu.VMEM((DFF, D), BF16), pltpu.SemaphoreType.DMA((6,))],
        compiler_params=_cp(("arbitrary",)), name="fwd_ffn_loss" if with_loss else "fwd_ffn",
    )(xmid, g_pre, g_post, wfi_all, wfo_all, *([target] * with_loss), *after)


def bwd_ffn(dx, f, xmid, gu, g_pre, g_post, wfi_all, wfo_all, after=()):
    t = dx.shape[0]

    def body(dx_ref, f_ref, x_ref, gu_ref, gpre_ref, gpost_ref, wfi_hbm, wfo_hbm,
             dxm_ref, df_ref, act_ref, dgu_ref, h_ref, dgpost_ref, dgpre_ref, wfi_v, wfo_v, sems):
        @pl.when(pl.program_id(0) == 0)
        def _():
            dgpost_ref[...] = jnp.zeros_like(dgpost_ref)
            dgpre_ref[...] = jnp.zeros_like(dgpre_ref)

        def step(ready):
            dxo = dx_ref[...]
            df, dgp = _rms_bwd(dxo, f_ref[...], gpost_ref[...])
            dgpost_ref[...] += dgp
            dfb = df.astype(BF16)
            df_ref[...] = dfb
            dh = jnp.zeros((TM, D), F32)
            for ci, (a, b) in enumerate(FF_CHUNKS):
                ready(2, ci)
                dact = lax.dot_general(dfb, wfo_v[a:b, :], NT, preferred_element_type=F32)
                gate = gu_ref[:, a:b].astype(F32)
                up = gu_ref[:, DFF + a:DFF + b].astype(F32)
                sig = 1.0 / (1.0 + jnp.exp(-gate))
                silu = gate * sig
                act_ref[:, a:b] = (silu * up).astype(BF16)
                dup = (dact * silu).astype(BF16)
                dgate = (dact * up * (sig * (1.0 + gate * (1.0 - sig)))).astype(BF16)
                dgu_ref[:, a:b] = dgate
                dgu_ref[:, DFF + a:DFF + b] = dup
                ready(0, ci)
                dh = dh + lax.dot_general(dgate, wfi_v[0, :, a:b], NT, preferred_element_type=F32)
                ready(1, ci)
                dh = dh + lax.dot_general(dup, wfi_v[1, :, a:b], NT, preferred_element_type=F32)
            xv = x_ref[...]
            gpre = gpre_ref[...]
            h_ref[...] = _rms(xv, gpre).astype(BF16)
            dxv, dgq = _rms_bwd(dh, xv, gpre)
            dgpre_ref[...] += dgq
            dxm_ref[...] = dxo + dxv

        _stream_ffn_weights(wfi_hbm, wfo_hbm, wfi_v, wfo_v, sems, (2, 0, 1), step)

    row = lambda w: pl.BlockSpec((TM, w), lambda i: (i, 0))
    return pl.pallas_call(
        _behind(body, 8, after), grid=(t // TM,),
        in_specs=[row(D), row(D), row(D), row(2 * DFF), _const((1, D)), _const((1, D)), _any(), _any()]
        + [_any()] * len(after),
        out_specs=[row(D), row(D), row(DFF), row(2 * DFF), row(D), _const((1, D)), _const((1, D))],
        out_shape=[jax.ShapeDtypeStruct((t, D), F32), jax.ShapeDtypeStruct((t, D), BF16),
                   jax.ShapeDtypeStruct((t, DFF), BF16), jax.ShapeDtypeStruct((t, 2 * DFF), BF16),
                   jax.ShapeDtypeStruct((t, D), BF16), jax.ShapeDtypeStruct((1, D), F32),
                   jax.ShapeDtypeStruct((1, D), F32)],
        scratch_shapes=[pltpu.VMEM((2, D, DFF), BF16), pltpu.VMEM((DFF, D), BF16), pltpu.SemaphoreType.DMA((6,))],
        compiler_params=_cp(("arbitrary",)), name="bwd_ffn")(dx, f, xmid, gu, g_pre, g_post, wfi_all, wfo_all, *after)


def bwd_mix(dxm, z, o, y, proj, wconv_t, g_co, g_ao, g_pm, gm, wout_all, after=()):
    t = dxm.shape[0]

    def body(dx_ref, z_ref, o_ref, y_ref, pc_ref, pcp_ref, wc_ref, gco_ref, gao_ref, gpm_ref, gm_ref, wout_hbm,
             dwo_ref, do_ref, dco_ref, dbg_ref, dgpm_ref, dgco_ref, dgao_ref, wout_v, cscr):
        first = pl.program_id(0) == 0

        @pl.when(first)
        def _():
            pltpu.sync_copy(wout_hbm, wout_v)
            dwo_ref[...] = jnp.zeros_like(dwo_ref)
            dgpm_ref[...] = jnp.zeros_like(dgpm_ref)
            dgco_ref[...] = jnp.zeros_like(dgco_ref)
            dgao_ref[...] = jnp.zeros_like(dgao_ref)

        dz, dgp = _rms_bwd(dx_ref[...], z_ref[...], gpm_ref[...])
        dgpm_ref[...] += dgp
        dzb = dz.astype(BF16)
        dwo_ref[...] += lax.dot_general(y_ref[...], dzb, TN, preferred_element_type=F32)
        gmv = gm_ref[...]
        _, bg, _, _, _, _, cout = _conv_fwd(pc_ref, pcp_ref, wc_ref, cscr, first)
        dy_conv = lax.dot_general(dzb, wout_v[0:CW, :], NT, preferred_element_type=F32)
        dyc, dgc = _group_rms_bwd(dy_conv, bg * cout, gco_ref[...], gmv)
        dgco_ref[...] += dgc
        dbg_ref[...] = (dyc * cout).astype(BF16)
        dco_ref[...] = dyc * bg
        dy_attn = lax.dot_general(dzb, wout_v[CW:2 * CW, :], NT, preferred_element_type=F32)
        do, dga = _group_rms_bwd(dy_attn, o_ref[...], gao_ref[...], gmv)
        dgao_ref[...] += dga
        do_ref[...] = do.astype(BF16)

    row = lambda w: pl.BlockSpec((TQ, w), lambda i: (i, 0))
    return pl.pallas_call(
        _behind(body, 12, after), grid=(t // TQ,),
        in_specs=[row(D), row(D), row(CW), row(D)] + _conv_specs() + [
            _const((8, CW)), _const((1, CW)), _const((1, CW)), _const((1, D)), _const((CW, CW)), _any()]
        + [_any()] * len(after),
        out_specs=[_const((D, D)), row(CW), row(CW), row(CW), _const((1, D)), _const((1, CW)), _const((1, CW))],
        out_shape=[jax.ShapeDtypeStruct((D, D), F32), jax.ShapeDtypeStruct((t, CW), BF16),
                   jax.ShapeDtypeStruct((t, CW), F32), jax.ShapeDtypeStruct((t, CW), BF16),
                   jax.ShapeDtypeStruct((1, D), F32), jax.ShapeDtypeStruct((1, CW), F32),
                   jax.ShapeDtypeStruct((1, CW), F32)],
        scratch_shapes=[pltpu.VMEM((D, D), BF16), pltpu.VMEM((TQ + 16, CW), F32)],
        compiler_params=_cp(("arbitrary",)), name="bwd_mix",
    )(dxm, z, o, y, proj, proj, wconv_t, g_co, g_ao, g_pm, gm, wout_all, *after)


def bwd_conv(dco, proj, wconv_t, after=()):
    t = dco.shape[0]
    nt = t // TQ

    def body(d_ref, dn_ref, pc_ref, pcp_ref, wc_ref, dhc_ref, dcg_ref, dw_ref, cscr, dscr):
        i = pl.program_id(0)
        first = i == 0

        @pl.when(first)
        def _():
            dw_ref[...] = jnp.zeros_like(dw_ref)

        hc, _, cg, u, u1, u2, _ = _conv_fwd(pc_ref, pcp_ref, wc_ref, cscr, first)
        d0 = d_ref[...]
        dscr[0:TQ, :] = d0
        dscr[TQ:TQ + 8, :] = jnp.where(i == nt - 1, 0.0, dn_ref[...])
        d1 = dscr[1:TQ + 1, :]
        d2 = dscr[2:TQ + 2, :]
        du = wc_ref[2:3, :] * d0 + wc_ref[1:2, :] * d1 + wc_ref[0:1, :] * d2
        dhc_ref[...] = (du * cg).astype(BF16)
        dcg_ref[...] = (du * hc).astype(BF16)
        dw_ref[0:1, :] += jnp.sum(d0 * u2, axis=0, keepdims=True)
        dw_ref[1:2, :] += jnp.sum(d0 * u1, axis=0, keepdims=True)
        dw_ref[2:3, :] += jnp.sum(d0 * u, axis=0, keepdims=True)

    row = lambda w: pl.BlockSpec((TQ, w), lambda i: (i, 0))
    nxt = pl.BlockSpec((8, CW), lambda i: (jnp.minimum((i + 1) * (TQ // 8), t // 8 - 1), 0))
    return pl.pallas_call(
        _behind(body, 5, after), grid=(nt,),
        in_specs=[row(CW), nxt] + _conv_specs() + [_const((8, CW))] + [_any()] * len(after),
        out_specs=[row(CW), row(CW), _const((8, CW))],
        out_shape=[jax.ShapeDtypeStruct((t, CW), BF16), jax.ShapeDtypeStruct((t, CW), BF16),
                   jax.ShapeDtypeStruct((8, CW), F32)],
        scratch_shapes=[pltpu.VMEM((TQ + 16, CW), F32), pltpu.VMEM((TQ + 8, CW), F32)],
        compiler_params=_cp(("arbitrary",)), name="bwd_conv")(dco, dco, proj, proj, wconv_t, *after)


def bwd_attn(proj, o, do, lse, bias2):
    t = o.shape[0]
    nt = t // TQ
    qg, kg = QG_BWD, QG_BWD + LEFT
    nkb = (t + TQ) // LANES

    def body(q_ref, kp_ref, kc_ref, vp_ref, vc_ref, o_ref, do_ref, lse_ref, b2_ref,
             dq_ref, dk_hbm, dv_hbm, db_hbm, kwin, vwin, dk_acc, dv_acc, db_acc):
        i = pl.program_id(0)
        first = i == 0

        @pl.when(first)
        def _():
            dk_acc[...] = jnp.zeros_like(dk_acc)
            dv_acc[...] = jnp.zeros_like(dv_acc)
            db_acc[...] = jnp.zeros_like(db_acc)

        kwin[0:TQ, :] = kp_ref[...]
        kwin[TQ:2 * TQ, :] = kc_ref[...]
        vwin[0:TQ, :] = vp_ref[...]
        vwin[TQ:2 * TQ, :] = vc_ref[...]
        scale = HD ** -0.5
        qmask = _head_masks(scale)
        vmask = _head_masks(1.0)
        low = lax.broadcasted_iota(jnp.int32, (1, LANES), 1) < HD

        def group(g, carry):
            r0 = pl.multiple_of(g * qg, qg)
            base = i * (TQ // LANES) + g * (qg // LANES)
            pen = _key_penalty(first, r0, kg)
            for hp in range(NH // 2):
                ls = slice(LANES * hp, LANES * (hp + 1))
                qb = q_ref[pl.ds(r0, qg), ls]
                kw = kwin[pl.ds(r0, kg), ls]
                dob = do_ref[pl.ds(r0, qg), ls]
                prod = dob.astype(F32) * o_ref[pl.ds(r0, qg), ls]
                lseb = lse_ref[pl.ds(r0, qg), ls]
                q2 = jnp.concatenate([qb * qmask[0], qb * qmask[1]], axis=0)
                do2 = jnp.concatenate([dob * vmask[0], dob * vmask[1]], axis=0)
                lse2 = jnp.concatenate([lseb[:, 0:1], lseb[:, HD:HD + 1]], axis=0)
                dsum = jnp.concatenate([jnp.sum(jnp.where(low, prod, 0.0), axis=-1, keepdims=True),
                                        jnp.sum(jnp.where(low, 0.0, prod), axis=-1, keepdims=True)], axis=0)
                s = lax.dot_general(q2, kw, NT, preferred_element_type=F32) + b2_ref[hp] + pen
                p = jnp.exp(s - lse2)
                dp = lax.dot_general(do2, vwin[pl.ds(r0, kg), ls], NT, preferred_element_type=F32)
                ds = p * (dp - dsum)
                db_acc[hp] += ds
                dsb = ds.astype(BF16)
                dq2 = jnp.dot(dsb, kw, preferred_element_type=F32)
                dq_ref[pl.ds(r0, qg), ls] = (jnp.where(low, dq2[:qg], dq2[qg:]) * scale).astype(BF16)
                dkt = lax.dot_general(q2, dsb, TN, preferred_element_type=F32)
                dvt = lax.dot_general(do2, p.astype(BF16), TN, preferred_element_type=F32)
                for kb in range(kg // LANES):
                    dk_acc[base + kb, ls, :] += dkt[:, LANES * kb:LANES * (kb + 1)]
                    dv_acc[base + kb, ls, :] += dvt[:, LANES * kb:LANES * (kb + 1)]
            return carry

        lax.fori_loop(0, TQ // qg, group, 0)

        @pl.when(i == nt - 1)
        def _():
            pltpu.sync_copy(dk_acc, dk_hbm)
            pltpu.sync_copy(dv_acc, dv_hbm)
            pltpu.sync_copy(db_acc, db_hbm)

    row = lambda w: pl.BlockSpec((TQ, w), lambda i: (i, 0))
    return pl.pallas_call(
        body, grid=(nt,),
        in_specs=_attn_window_specs() + [row(CW), row(CW), row(CW), _const((NH // 2, 2 * qg, kg))],
        out_specs=[row(CW), _any(), _any(), _any()],
        out_shape=[jax.ShapeDtypeStruct((t, CW), BF16), jax.ShapeDtypeStruct((nkb, CW, LANES), F32),
                   jax.ShapeDtypeStruct((nkb, CW, LANES), F32), jax.ShapeDtypeStruct((NH // 2, 2 * qg, kg), F32)],
        scratch_shapes=[pltpu.VMEM((2 * TQ, CW), BF16), pltpu.VMEM((2 * TQ, CW), BF16),
                        pltpu.VMEM((nkb, CW, LANES), F32), pltpu.VMEM((nkb, CW, LANES), F32),
                        pltpu.VMEM((NH // 2, 2 * qg, kg), F32)],
        compiler_params=_cp(("arbitrary",)), name="bwd_attn",
    )(proj, proj, proj, proj, proj, o, do, lse, bias2)


def bwd_inproj(dxm, x, dhc, dbg, dcg, dq, dk, dv, g, w_all):
    t = x.shape[0]
    wc = PROJ // NCHIP

    def body(dxm_ref, x_ref, dhc_ref, dbg_ref, dcg_ref, dq_ref, dk_ref, dv_ref, g_ref, w_hbm,
             dx_ref, dp_ref, h_ref, dg_ref, w_v):
        @pl.when(pl.program_id(0) == 0)
        def _():
            pltpu.sync_copy(w_hbm, w_v)
            dg_ref[...] = jnp.zeros_like(dg_ref)

        dp_ref[:, 0:CW] = dhc_ref[...]
        dp_ref[:, CW:2 * CW] = dbg_ref[...]
        dp_ref[:, 2 * CW:3 * CW] = dcg_ref[...]
        dp_ref[:, 3 * CW:4 * CW] = dq_ref[...]
        for kb in range(TQ // LANES):
            rows = slice(LANES * kb, LANES * (kb + 1))
            dp_ref[rows, 4 * CW:5 * CW] = jnp.transpose(dk_ref[kb]).astype(BF16)
            dp_ref[rows, 5 * CW:6 * CW] = jnp.transpose(dv_ref[kb]).astype(BF16)
        dh = jnp.zeros((TQ, D), F32)
        for b in range(NCHIP):
            dh = dh + lax.dot_general(dp_ref[:, wc * b:wc * (b + 1)], w_v[b], NT, preferred_element_type=F32)
        xv = x_ref[...]
        gv = g_ref[...]
        h_ref[...] = _rms(xv, gv).astype(BF16)
        dxv, dgv = _rms_bwd(dh, xv, gv)
        dg_ref[...] += dgv
        dx_ref[...] = dxm_ref[...] + dxv

    row = lambda w: pl.BlockSpec((TQ, w), lambda i: (i, 0))
    pad = pl.BlockSpec((TQ // LANES, CW, LANES), lambda i: (i + 1, 0, 0))
    return pl.pallas_call(
        body, grid=(t // TQ,),
        in_specs=[row(D), row(D), row(CW), row(CW), row(CW), row(CW), pad, pad, _const((1, D)), _any()],
        out_specs=[row(D), row(PROJ), row(D), _const((1, D))],
        out_shape=[jax.ShapeDtypeStruct((t, D), F32), jax.ShapeDtypeStruct((t, PROJ), BF16),
                   jax.ShapeDtypeStruct((t, D), BF16), jax.ShapeDtypeStruct((1, D), F32)],
        scratch_shapes=[pltpu.VMEM((NCHIP, D, wc), BF16)],
        compiler_params=_cp(("arbitrary",)), name="bwd_inproj",
    )(dxm, x, dhc, dbg, dcg, dq, dk, dv, g, w_all)


def wgrad(a, b, kb, nb, by_columns, name):
    t, k = a.shape
    n = b.shape[1]
    tk = 512

    def body(a_ref, b_ref, o_ref):
        o_ref[...] = jnp.zeros_like(o_ref)
        for c in range(t // tk):
            o_ref[...] += lax.dot_general(a_ref[tk * c:tk * (c + 1), :], b_ref[tk * c:tk * (c + 1), :], TN,
                                          preferred_element_type=F32)

    if by_columns:
        assert nb == n // NCHIP
        out_spec = pl.BlockSpec((None, kb, nb), lambda ki, ni: (ni, ki, 0))
        out_shape = jax.ShapeDtypeStruct((NCHIP, k, nb), F32)
    else:
        assert nb == n
        out_spec = pl.BlockSpec((kb, nb), lambda ki, ni: (ki, 0))
        out_shape = jax.ShapeDtypeStruct((k, n), F32)
    return pl.pallas_call(
        body, grid=(k // kb, n // nb),
        in_specs=[pl.BlockSpec((t, kb), lambda ki, ni: (0, ki)), pl.BlockSpec((t, nb), lambda ki, ni: (0, ni))],
        out_specs=out_spec, out_shape=out_shape,
        compiler_params=_cp(("arbitrary", "arbitrary")), name=name)(a, b)


TOE = 1024
assert 2 * QG_FWD + LEFT <= TOE
N_FLAT = LEFT - REL_CLIP + 1
N_VAR = BAND - N_FLAT


def _diag_vector(table):
    last = table[:, 2 * REL_CLIP:]
    var = table[:, 2 * REL_CLIP - N_VAR:2 * REL_CLIP][:, ::-1]
    return jnp.concatenate([jnp.broadcast_to(last, (NH, N_FLAT)), var, jnp.broadcast_to(last, (NH, TOE - BAND))], axis=1)


def _diag_vector_bwd(dvec):
    dlast = jnp.sum(dvec[:, :N_FLAT], axis=1, keepdims=True) + jnp.sum(dvec[:, BAND:], axis=1, keepdims=True)
    dvar = dvec[:, N_FLAT:BAND][:, ::-1]
    return jnp.concatenate([jnp.zeros((NH, 2 * REL_CLIP - N_VAR), F32), dvar, dlast], axis=1)


def _band_valid(qg):
    r = lax.broadcasted_iota(jnp.int32, (qg, qg + LEFT), 0)
    p = lax.broadcasted_iota(jnp.int32, (qg, qg + LEFT), 1)
    start = lax.shift_left(lax.shift_right_logical(r, 6), 6)
    return (p >= start) & (p < start + BAND)


def bias_expand(vec, qgs, after=()):
    def body(v_ref, *o_refs):
        for qg, o_ref in zip(qgs, o_refs):
            valid = _band_valid(qg)
            for h in range(NH):
                rows = jnp.broadcast_to(v_ref[h:h + 1, :], (qg, TOE))
                toe = pltpu.roll(rows, 0, 1, stride=1, stride_axis=0)
                o_ref[h // 2, qg * (h % 2):qg * (h % 2 + 1), :] = jnp.where(valid, toe[:, :qg + LEFT], NEG_INF)

    vm = pl.BlockSpec(memory_space=pltpu.VMEM)
    return pl.pallas_call(_behind(body, 1, after), in_specs=[vm] + [_any()] * len(after), out_specs=[vm] * len(qgs),
                          out_shape=[jax.ShapeDtypeStruct((NH // 2, 2 * qg, qg + LEFT), F32) for qg in qgs],
                          name="bias_expand")(vec, *after)


def bias_reduce(db2):
    _, qg, kg = db2.shape

    def body(d_ref, o_ref):
        ii = lax.broadcasted_iota(jnp.int32, (kg, kg), 0)
        jj = lax.broadcasted_iota(jnp.int32, (kg, kg), 1)
        flip = jnp.where(ii + jj == kg - 1, 1.0, 0.0).astype(BF16)
        for h in range(NH):
            rest = d_ref[h]
            rev = jnp.zeros((qg, kg), F32)
            for _ in range(3):
                term = rest.astype(BF16)
                rev = rev + jnp.dot(term, flip, preferred_element_type=F32)
                rest = rest - term.astype(F32)
            d = jnp.concatenate([jnp.zeros((qg, TOE - kg), F32), rev], axis=1)
            back = pltpu.roll(d, 0, 1, stride=1, stride_axis=0)
            o_ref[h:h + 1, :] = jnp.sum(back, axis=0, keepdims=True)

    rev = pl.pallas_call(body, out_shape=jax.ShapeDtypeStruct((NH, TOE), F32), name="bias_reduce")(db2)
    return rev[:, ::-1]


def _place():
    x, y, c = lax.axis_index("x"), lax.axis_index("y"), lax.axis_index("c")
    chips = [(1 - x, y), (x, 1 - y), (1 - x, 1 - y)]
    return x, y, c, chips


def _half(ref_rows, c):
    return pl.ds(c * (ref_rows // 2), ref_rows // 2)


HBM_SPEC = pl.BlockSpec(memory_space=pltpu.HBM)
SEM_SPEC = pl.BlockSpec(memory_space=pltpu.SEMAPHORE)
IN_FLIGHT = pltpu.CompilerParams(has_side_effects=pltpu.SideEffectType.DATAFLOW_SIDE_EFFECTING)


def _in_hbm(a):
    return pltpu.with_memory_space_constraint(a, pltpu.HBM)


def cast_to_slot(ws, chip, layer, after=()):
    n = len(ws)
    steps = 4

    def body(b_ref, *refs):
        del b_ref
        for w_ref, o_ref in zip(refs[:n], refs[n + len(after):]):
            o_ref[...] = w_ref[...].astype(BF16)

    grid_spec = pltpu.PrefetchScalarGridSpec(
        num_scalar_prefetch=1, grid=(steps,),
        in_specs=[pl.BlockSpec((None, w.shape[1] // steps, w.shape[2]), lambda r, b: (layer, r, 0)) for w in ws]
        + [_any()] * len(after),
        out_specs=[pl.BlockSpec((None, w.shape[1] // steps, w.shape[2]), lambda r, b: (b[0], r, 0)) for w in ws])
    return pl.pallas_call(body, grid_spec=grid_spec,
                          out_shape=[jax.ShapeDtypeStruct((NCHIP,) + w.shape[1:], BF16) for w in ws],
                          compiler_params=_cp(("arbitrary",)), name="cast_to_slot")(chip, *ws, *after)


def _gather_copies(bufs, send, recv):
    x, y, c, chips = _place()
    b = 2 * x + y
    out = []
    for k, buf in enumerate(bufs):
        rows = buf.shape[1]
        mine = buf.at[b, _half(rows, c), :]
        for j, (cx, cy) in enumerate(chips):
            theirs = buf.at[2 * cx + cy, _half(rows, c), :]
            sems = dict(send_sem=send.at[3 * k + j], recv_sem=recv.at[3 * k + j],
                        device_id=(cx, cy, c), device_id_type=MESH)
            out.append((pltpu.make_async_remote_copy(src_ref=mine, dst_ref=mine, **sems),
                        pltpu.make_async_remote_copy(src_ref=theirs, dst_ref=theirs, **sems)))
    return out


def gather_start(bufs, after, layer):
    n = len(bufs)

    def body(*refs):
        ins = refs[:n]
        send, recv = refs[n + 1], refs[n + 2]
        token = refs[-1]
        for start, _ in _gather_copies(ins, send, recv):
            start.start()
        token[...] = jnp.zeros_like(token)

    sems = pltpu.SemaphoreType.DMA((3 * n,))
    res = pl.pallas_call(
        body, name=f"gather_start_{layer}",
        in_specs=[HBM_SPEC] * n + [_any()],
        out_specs=[SEM_SPEC, SEM_SPEC] + [HBM_SPEC] * n + [pl.BlockSpec(memory_space=pltpu.VMEM)],
        out_shape=[sems, sems] + [pltpu.HBM(b.shape, b.dtype) for b in bufs] + [jax.ShapeDtypeStruct((8, LANES), F32)],
        input_output_aliases={k: 2 + k for k in range(n)}, compiler_params=IN_FLIGHT,
    )(*[_in_hbm(b) for b in bufs], after)
    return res[0], res[1], res[2:2 + n], res[-1]


def gather_wait(send, recv, bufs, after, layer):
    n = len(bufs)

    def body(*refs):
        ins = refs[:n]
        send_ref, recv_ref = refs[n], refs[n + 1]
        for start, arrival in _gather_copies(ins, send_ref, recv_ref):
            start.wait_send()
            arrival.wait_recv()

    return pl.pallas_call(
        body, name=f"gather_wait_{layer}",
        in_specs=[HBM_SPEC] * n + [SEM_SPEC, SEM_SPEC, _any()], out_specs=[HBM_SPEC] * n,
        out_shape=[pltpu.HBM(b.shape, b.dtype) for b in bufs],
        input_output_aliases={k: k for k in range(n)}, compiler_params=IN_FLIGHT,
    )(*bufs, send, recv, after)


def gather_forward(bufs):
    n = len(bufs)

    def body(*refs):
        outs = refs[n:2 * n]
        send, recv = refs[2 * n:]
        x, y, c, chips = _place()
        cps = []
        for k in range(n):
            rows = outs[k].shape[1]
            for j, (cx, cy) in enumerate(chips):
                sems = dict(send_sem=send.at[3 * k + j], recv_sem=recv.at[3 * k + j],
                            device_id=(x, y, 1 - c), device_id_type=MESH)
                mine = outs[k].at[2 * cx + cy, _half(rows, c), :]
                theirs = outs[k].at[2 * cx + cy, _half(rows, 1 - c), :]
                cp = pltpu.make_async_remote_copy(src_ref=mine, dst_ref=mine, **sems)
                cp.start()
                cps.append((cp, pltpu.make_async_remote_copy(src_ref=theirs, dst_ref=theirs, **sems)))
        for cp, arrival in cps:
            cp.wait_send()
            arrival.wait_recv()

    return pl.pallas_call(
        body, in_specs=[_any()] * n, out_specs=[_any()] * n,
        out_shape=[jax.ShapeDtypeStruct(b.shape, b.dtype) for b in bufs], input_output_aliases={k: k for k in range(n)},
        scratch_shapes=[pltpu.SemaphoreType.DMA((3 * n,)), pltpu.SemaphoreType.DMA((3 * n,))],
        name="gather_forward")(*bufs)


def _forward_copies(bufs, send, recv):
    x, y, c, chips = _place()
    out = []
    for k, buf in enumerate(bufs):
        rows = buf.shape[1]
        for j, (cx, cy) in enumerate(chips):
            sems = dict(send_sem=send.at[3 * k + j], recv_sem=recv.at[3 * k + j],
                        device_id=(x, y, 1 - c), device_id_type=MESH)
            mine = buf.at[2 * cx + cy, _half(rows, c), :]
            theirs = buf.at[2 * cx + cy, _half(rows, 1 - c), :]
            out.append((pltpu.make_async_remote_copy(src_ref=mine, dst_ref=mine, **sems),
                        pltpu.make_async_remote_copy(src_ref=theirs, dst_ref=theirs, **sems)))
    return out


def forward_start(bufs, tag):
    n = len(bufs)

    def body(*refs):
        ins = refs[:n]
        send, recv = refs[n], refs[n + 1]
        token = refs[-1]
        for start, _ in _forward_copies(ins, send, recv):
            start.start()
        token[...] = jnp.zeros_like(token)

    sems = pltpu.SemaphoreType.DMA((3 * n,))
    res = pl.pallas_call(
        body, name=f"forward_start_{tag}", in_specs=[HBM_SPEC] * n,
        out_specs=[SEM_SPEC, SEM_SPEC] + [HBM_SPEC] * n + [pl.BlockSpec(memory_space=pltpu.VMEM)],
        out_shape=[sems, sems] + [pltpu.HBM(b.shape, b.dtype) for b in bufs] + [jax.ShapeDtypeStruct((8, LANES), F32)],
        input_output_aliases={k: 2 + k for k in range(n)}, compiler_params=IN_FLIGHT,
    )(*[_in_hbm(b) for b in bufs])
    return res[0], res[1], res[2:2 + n], res[-1]


def forward_wait(send, recv, bufs, after, tag):
    n = len(bufs)

    def body(*refs):
        ins = refs[:n]
        send_ref, recv_ref = refs[n], refs[n + 1]
        for start, arrival in _forward_copies(ins, send_ref, recv_ref):
            start.wait_send()
            arrival.wait_recv()

    return pl.pallas_call(
        body, name=f"forward_wait_{tag}",
        in_specs=[HBM_SPEC] * n + [SEM_SPEC, SEM_SPEC, _any()], out_specs=[HBM_SPEC] * n,
        out_shape=[pltpu.HBM(b.shape, b.dtype) for b in bufs],
        input_output_aliases={k: k for k in range(n)}, compiler_params=IN_FLIGHT,
    )(*bufs, send, recv, after)


def _exchange_copies(srcs, lands, send, recv):
    x, y, c, _ = _place()
    return [pltpu.make_async_remote_copy(
        src_ref=src.at[:, _half(src.shape[1], 1 - c), :], dst_ref=land, send_sem=send.at[k], recv_sem=recv.at[k],
        device_id=(x, y, 1 - c), device_id_type=MESH) for k, (src, land) in enumerate(zip(srcs, lands))]


def exchange_start(srcs, tag):
    n = len(srcs)
    lands = [lax.empty((s.shape[0], s.shape[1] // 2, s.shape[2]), s.dtype) for s in srcs]

    def body(*refs):
        ins, land_refs = refs[:n], refs[n:2 * n]
        send, recv = refs[2 * n], refs[2 * n + 1]
        token = refs[-1]
        for cp in _exchange_copies(ins, land_refs, send, recv):
            cp.start()
        token[...] = jnp.zeros_like(token)

    sems = pltpu.SemaphoreType.DMA((n,))
    res = pl.pallas_call(
        body, name=f"exchange_start_{tag}",
        in_specs=[HBM_SPEC] * (2 * n),
        out_specs=[SEM_SPEC, SEM_SPEC] + [HBM_SPEC] * (2 * n) + [pl.BlockSpec(memory_space=pltpu.VMEM)],
        out_shape=[sems, sems] + [pltpu.HBM(a.shape, a.dtype) for a in list(srcs) + lands]
        + [jax.ShapeDtypeStruct((8, LANES), F32)],
        input_output_aliases={k: 2 + k for k in range(2 * n)}, compiler_params=IN_FLIGHT,
    )(*[_in_hbm(a) for a in list(srcs) + lands])
    return res[0], res[1], res[2:2 + n], res[2 + n:2 + 2 * n], res[-1]


def exchange_wait(send, recv, srcs, lands, after, tag):
    n = len(srcs)

    def body(*refs):
        ins, land_refs = refs[:n], refs[n:2 * n]
        send_ref, recv_ref = refs[2 * n], refs[2 * n + 1]
        for cp in _exchange_copies(ins, land_refs, send_ref, recv_ref):
            cp.wait_send()
            cp.wait_recv()

    res = pl.pallas_call(
        body, name=f"exchange_wait_{tag}",
        in_specs=[HBM_SPEC] * (2 * n) + [SEM_SPEC, SEM_SPEC, _any()], out_specs=[HBM_SPEC] * (2 * n),
        out_shape=[pltpu.HBM(a.shape, a.dtype) for a in list(srcs) + list(lands)],
        input_output_aliases={k: k for k in range(2 * n)}, compiler_params=IN_FLIGHT,
    )(*srcs, *lands, send, recv, after)
    return res[:n], res[n:]


def add_pair(gs, r1s, core):
    n = len(gs)

    def body(c_ref, *refs):
        del c_ref
        for g_ref, r_ref, o_ref in zip(refs[:n], refs[n:2 * n], refs[2 * n:]):
            o_ref[...] = (g_ref[...] + r_ref[...]).astype(BF16)

    blk = lambda r: (None,) + r.shape[1:]
    grid_spec = pltpu.PrefetchScalarGridSpec(
        num_scalar_prefetch=1, grid=(NCHIP,),
        in_specs=[pl.BlockSpec(blk(r), lambda s, c: (s, c[0], 0)) for r in r1s]
        + [pl.BlockSpec(blk(r), lambda s, c: (s, 0, 0)) for r in r1s],
        out_specs=[pl.BlockSpec(blk(r), lambda s, c: (s, 0, 0)) for r in r1s])
    return pl.pallas_call(body, grid_spec=grid_spec, out_shape=[jax.ShapeDtypeStruct(r.shape, BF16) for r in r1s],
                          compiler_params=_cp(("arbitrary",)), name="add_pair")(core, *gs, *r1s)


def _scatter_copies(srcs, lands, send, recv):
    _, _, c, chips = _place()
    out = []
    for k, (src, land) in enumerate(zip(srcs, lands)):
        for j, (cx, cy) in enumerate(chips):
            out.append(pltpu.make_async_remote_copy(
                src_ref=src.at[2 * cx + cy], dst_ref=land.at[j], send_sem=send.at[3 * k + j],
                recv_sem=recv.at[3 * k + j], device_id=(cx, cy, c), device_id_type=MESH))
    return out


def scatter_start(srcs, layer):
    n = len(srcs)
    srcs = list(srcs)
    lands = [lax.empty((3,) + s.shape[1:], s.dtype) for s in srcs]

    def body(*refs):
        ins, land_refs = refs[:n], refs[n:2 * n]
        send, recv = refs[2 * n], refs[2 * n + 1]
        token = refs[-1]
        for cp in _scatter_copies(ins, land_refs, send, recv):
            cp.start()
        token[...] = jnp.zeros_like(token)

    sems = pltpu.SemaphoreType.DMA((3 * n,))
    res = pl.pallas_call(
        body, name=f"scatter_start_{layer}",
        in_specs=[HBM_SPEC] * (2 * n),
        out_specs=[SEM_SPEC, SEM_SPEC] + [HBM_SPEC] * (2 * n) + [pl.BlockSpec(memory_space=pltpu.VMEM)],
        out_shape=[sems, sems] + [pltpu.HBM(a.shape, a.dtype) for a in srcs + lands]
        + [jax.ShapeDtypeStruct((8, LANES), F32)],
        input_output_aliases={k: 2 + k for k in range(2 * n)}, compiler_params=IN_FLIGHT,
    )(*[_in_hbm(a) for a in srcs + lands])
    return res[0], res[1], res[2:2 + n], res[2 + n:2 + 2 * n], res[-1]


def scatter_wait(send, recv, srcs, lands, after, layer):
    n = len(srcs)

    def body(*refs):
        ins, land_refs = refs[:n], refs[n:2 * n]
        send_ref, recv_ref = refs[2 * n], refs[2 * n + 1]
        for cp in _scatter_copies(ins, land_refs, send_ref, recv_ref):
            cp.wait_send()
            cp.wait_recv()

    res = pl.pallas_call(
        body, name=f"scatter_wait_{layer}",
        in_specs=[HBM_SPEC] * (2 * n) + [SEM_SPEC, SEM_SPEC, _any()], out_specs=[HBM_SPEC] * (2 * n),
        out_shape=[pltpu.HBM(a.shape, a.dtype) for a in list(srcs) + list(lands)],
        input_output_aliases={k: k for k in range(2 * n)}, compiler_params=IN_FLIGHT,
    )(*srcs, *lands, send, recv, after)
    return res[n:]


def add_chips(gs, r1s, r2s, place, totals, layer):
    n = len(gs)
    steps = 2

    def body(p_ref, *refs):
        del p_ref
        for g_ref, r1_ref, r2_ref, o_ref in zip(refs[:n], refs[n:2 * n], refs[2 * n:3 * n], refs[4 * n:]):
            own = g_ref[...] + r1_ref[...]
            o_ref[...] = ((own + r2_ref[0].astype(F32)) + r2_ref[1].astype(F32)) + r2_ref[2].astype(F32)

    blk = lambda r: (None, r.shape[1] // steps, r.shape[2])
    grid_spec = pltpu.PrefetchScalarGridSpec(
        num_scalar_prefetch=1, grid=(steps,),
        in_specs=[pl.BlockSpec(blk(r), lambda i, p: (p[1], p[0] * steps + i, 0)) for r in r1s]
        + [pl.BlockSpec(blk(r), lambda i, p: (p[1], i, 0)) for r in r1s]
        + [pl.BlockSpec((3,) + blk(r)[1:], lambda i, p: (0, i, 0)) for r in r1s] + [_any()] * n,
        out_specs=[pl.BlockSpec(blk(r), lambda i, p: (layer, p[0] * steps + i, 0)) for r in r1s])
    return pl.pallas_call(body, grid_spec=grid_spec, out_shape=[jax.ShapeDtypeStruct(t.shape, F32) for t in totals],
                          input_output_aliases={1 + 3 * n + k: k for k in range(n)},
                          compiler_params=_cp(("arbitrary",)), name="add_chips")(place, *gs, *r1s, *r2s, *totals)


def _share_copies(bufs, send, recv):
    x, y, c, _ = _place()
    out = []
    for k, buf in enumerate(bufs):
        sems = dict(send_sem=send.at[k], recv_sem=recv.at[k], device_id=(x, y, 1 - c), device_id_type=MESH)
        mine = buf.at[:, _half(buf.shape[1], c), :]
        theirs = buf.at[:, _half(buf.shape[1], 1 - c), :]
        out.append((pltpu.make_async_remote_copy(src_ref=mine, dst_ref=mine, **sems),
                    pltpu.make_async_remote_copy(src_ref=theirs, dst_ref=theirs, **sems)))
    return out


def share_start(bufs, tag):
    n = len(bufs)

    def body(*refs):
        ins = refs[:n]
        send, recv = refs[n], refs[n + 1]
        token = refs[-1]
        for start, _ in _share_copies(ins, send, recv):
            start.start()
        token[...] = jnp.zeros_like(token)

    sems = pltpu.SemaphoreType.DMA((n,))
    res = pl.pallas_call(
        body, name=f"share_start_{tag}", in_specs=[HBM_SPEC] * n,
        out_specs=[SEM_SPEC, SEM_SPEC] + [HBM_SPEC] * n + [pl.BlockSpec(memory_space=pltpu.VMEM)],
        out_shape=[sems, sems] + [pltpu.HBM(b.shape, b.dtype) for b in bufs] + [jax.ShapeDtypeStruct((8, LANES), F32)],
        input_output_aliases={k: 2 + k for k in range(n)}, compiler_params=IN_FLIGHT,
    )(*[_in_hbm(b) for b in bufs])
    return res[0], res[1], res[2:2 + n], res[-1]


def share_wait(send, recv, bufs, after, tag):
    n = len(bufs)

    def body(*refs):
        ins = refs[:n]
        send_ref, recv_ref = refs[n], refs[n + 1]
        for start, arrival in _share_copies(ins, send_ref, recv_ref):
            start.wait_send()
            arrival.wait_recv()

    return pl.pallas_call(
        body, name=f"share_wait_{tag}",
        in_specs=[HBM_SPEC] * n + [SEM_SPEC, SEM_SPEC, _any()], out_specs=[HBM_SPEC] * n,
        out_shape=[pltpu.HBM(b.shape, b.dtype) for b in bufs],
        input_output_aliases={k: k for k in range(n)}, compiler_params=IN_FLIGHT,
    )(*bufs, send, recv, after)


def small_allreduce(v, after=()):
    rows = v.shape[0]
    flips = [(fx, fy, fc) for fx in (0, 1) for fy in (0, 1) for fc in (0, 1)][1:]

    def body(v_ref, o_ref, buf, send, recv):
        x, y, c, _ = _place()
        buf[4 * x + 2 * y + c] = v_ref[...]
        peers = [(jnp.where(fx, 1 - x, x), jnp.where(fy, 1 - y, y), jnp.where(fc, 1 - c, c)) for fx, fy, fc in flips]
        cps = []
        for k, peer in enumerate(peers):
            cp = pltpu.make_async_remote_copy(
                src_ref=v_ref, dst_ref=buf.at[4 * x + 2 * y + c], send_sem=send.at[k], recv_sem=recv.at[k],
                device_id=peer, device_id_type=MESH)
            cp.start()
            cps.append(cp)
        for k, (px, py, pc) in enumerate(peers):
            pltpu.make_async_remote_copy(
                src_ref=v_ref, dst_ref=buf.at[4 * px + 2 * py + pc], send_sem=send.at[k], recv_sem=recv.at[k],
                device_id=(px, py, pc), device_id_type=MESH).wait_recv()
        for cp in cps:
            cp.wait_send()
        acc = buf[0]
        for s in range(1, 8):
            acc = acc + buf[s]
        o_ref[...] = acc

    vm = pl.BlockSpec(memory_space=pltpu.VMEM)
    return pl.pallas_call(
        _behind(body, 1, after), in_specs=[vm] + [_any()] * len(after), out_specs=vm,
        out_shape=jax.ShapeDtypeStruct((rows, SMALL_COLS), F32),
        scratch_shapes=[pltpu.VMEM((8, rows, SMALL_COLS), F32), pltpu.SemaphoreType.DMA((7,)),
                        pltpu.SemaphoreType.DMA((7,))],
        name="reduce_small")(v, *after)


def adamw(w, g, m, v, rb, name, after=()):
    nl, rows, cols = w.shape

    def body(w_ref, g_ref, m_ref, v_ref, go_ref, d_ref, nm_ref, nv_ref):
        gv = g_ref[...]
        go_ref[...] = gv
        nm = ADAM_B1 * m_ref[...] + (1.0 - ADAM_B1) * gv
        nv = ADAM_B2 * v_ref[...] + (1.0 - ADAM_B2) * (gv * gv)
        m_hat = nm / (1.0 - ADAM_B1 ** ADAM_STEP)
        v_hat = nv / (1.0 - ADAM_B2 ** ADAM_STEP)
        d_ref[...] = -ADAM_LR * (m_hat / (jnp.sqrt(v_hat) + ADAM_EPS) + ADAM_WD * w_ref[...])
        nm_ref[...] = nm
        nv_ref[...] = nv

    blk = pl.BlockSpec((None, rb, cols), lambda l, r: (l, r, 0))
    shp = jax.ShapeDtypeStruct(w.shape, F32)
    return pl.pallas_call(_behind(body, 4, after), grid=(nl, rows // rb), in_specs=[blk] * 4 + [_any()] * len(after),
                          out_specs=[blk] * 4, out_shape=[shp] * 4,
                          compiler_params=_cp(("arbitrary", "arbitrary")), name=name)(w, g, m, v, *after)


def _pack(parts, rows):
    flat = jnp.concatenate([p.reshape(-1).astype(F32) for p in parts])
    return jnp.pad(flat, (0, rows * SMALL_COLS - flat.shape[0])).reshape(rows, SMALL_COLS)


def _unpack(vec, shapes):
    flat = vec.reshape(-1)
    out, off = [], 0
    for s in shapes:
        size = 1
        for d in s:
            size *= d
        out.append(flat[off:off + size].reshape(s))
        off += size
    return out


def kernel(x, w_in, w_conv, rel_bias, g_conv_out, g_attn_out, w_out, g_pre_mix, g_post_mix, g_pre_ffn, g_post_ffn, w_ffn_in, w_ffn_out, loss_target, m_w_in, m_w_conv, m_rel_bias, m_g_conv_out, m_g_attn_out, m_w_out, m_g_pre_mix, m_g_post_mix, m_g_pre_ffn, m_g_post_ffn, m_w_ffn_in, m_w_ffn_out, v_w_in, v_w_conv, v_rel_bias, v_g_conv_out, v_g_attn_out, v_w_out, v_g_pre_mix, v_g_post_mix, v_g_pre_ffn, v_g_post_ffn, v_w_ffn_in, v_w_ffn_out):
    xi, yi, ci = lax.axis_index("x"), lax.axis_index("y"), lax.axis_index("c")
    chip = 2 * xi + yi
    nl = w_in.shape[0]
    x0 = x[0]
    target = loss_target[0]
    cwl = CW // NCHIP

    chip1 = chip.reshape(1).astype(jnp.int32)
    big_weights = [w_in, w_out, w_ffn_in, w_ffn_out]
    own = [cast_to_slot(big_weights, chip1, 0)]
    wc_mine = jnp.pad(w_conv.reshape(-1), (0, 16 * LANES - w_conv.size)).reshape(1, 16, LANES)
    wc_slot = lax.dynamic_update_slice_in_dim(jnp.zeros((NCHIP, 16, LANES), F32), wc_mine, chip, axis=0)
    gm = jnp.kron(jnp.eye(CW // HD, dtype=F32), jnp.full((HD, HD), 1.0 / HD, F32)).astype(BF16)
    row = lambda a, l: a[l][None, :]

    def gather_finish(flight, after, tag):
        send, recv, bufs, _ = flight
        return gather_forward(gather_wait(send, recv, bufs, after, tag))

    first_mix = gather_start(list(own[0][:2]) + [wc_slot], x0, "0m")
    first_ffn = gather_start(own[0][2:], first_mix[3], "0f")
    chain = first_ffn[3]
    biases = []
    for l in range(nl):
        biases.append(bias_expand(_diag_vector(rel_bias[l]), (QG_FWD, QG_BWD), [chain]))
        chain = biases[l][1]
    for l in range(1, nl):
        own.append(cast_to_slot(big_weights, chip1, l, [chain]))
        chain = own[l][0]
    gw_in, gw_out, wc_all = gather_finish(first_mix, chain, "0m")
    wc_full = wc_all.reshape(NCHIP, -1)[:, :nl * cwl * 3].reshape(NCHIP, nl, cwl, 3)
    wc_full = jnp.transpose(wc_full, (1, 0, 2, 3)).reshape(nl, CW, 3)
    wconv_t = jnp.pad(jnp.transpose(wc_full, (0, 2, 1)), ((0, 0), (0, 5), (0, 0)))
    flights, to_sibling = {}, None
    saved, weights = [], []
    h = x0
    for l in range(nl):
        if l == 0:
            pass
        elif l == 1:
            flights[2] = gather_start(own[2], h, 2)
            gw_in, gw_out, gw_fi, gw_fo = gather_finish(flights[l], flights[2][3], l)
        else:
            gw_in, gw_out, gw_fi, gw_fo = forward_wait(*to_sibling[:3], h, l)
        gw_out = gw_out.reshape(D, D)
        behind_mix, behind_ffn = ([first_ffn[3]] if l == 0 else []), []
        if l + 1 < nl and l + 1 not in flights:
            flights[l + 1] = gather_start(own[l + 1], first_ffn[3] if l == 0 else gw_in, l + 1)
            behind_mix.append(flights[l + 1][3])
        bias2, bias2_bwd = biases[l]
        proj = fwd_inproj(h, row(g_pre_mix, l), gw_in, behind_mix)
        xmid, o, lse, y, z = fwd_mix(h, proj, bias2, wconv_t[l], row(g_conv_out, l), row(g_attn_out, l),
                                     row(g_post_mix, l), gm, gw_out)
        if l == 0:
            gw_fi, gw_fo = gather_finish(first_ffn, xmid, "0f")
        elif l + 1 < nl:
            send, recv, bufs, _ = flights[l + 1]
            landed = gather_wait(send, recv, bufs, xmid, l + 1)
            to_sibling = forward_start(landed, l + 1)
            behind_ffn.append(to_sibling[3])
            if l + 2 < nl:
                flights[l + 2] = gather_start(own[l + 2], to_sibling[3], l + 2)
                behind_ffn.append(flights[l + 2][3])
        gw_fo = gw_fo.reshape(2, DFF // 2, D)
        ffn = fwd_ffn(xmid, row(g_pre_ffn, l), row(g_post_ffn, l), gw_fi, gw_fo, behind_ffn,
                      target if l == nl - 1 else None)
        gu, f = ffn[:2]
        saved.append((h, proj, bias2_bwd, xmid, o, lse, y, z, gu, f))
        weights.append((gw_in, gw_out, gw_fi, gw_fo))
        h = ffn[2]
    dx, loss_blk = ffn[2], ffn[3]

    core = ci.reshape(1).astype(jnp.int32)
    place = jnp.stack([ci, chip]).astype(jnp.int32)
    totals = [lax.empty(w.shape, F32) for w in (w_in, w_out, w_ffn_in, w_ffn_out)]
    small = {k: [None] * nl for k in ("co", "ao", "pm", "qm", "pf", "qf", "rel", "wc")}

    def reduce_begin(kinds, grads, tag):
        return kinds, exchange_start(grads, tag), tag

    def reduce_mid(state, after):
        kinds, (send, recv, srcs, lands, _), tag = state
        grads, from_sibling = exchange_wait(send, recv, srcs, lands, after, tag)
        return kinds, grads, from_sibling, scatter_start(add_pair(grads, from_sibling, core), tag), tag

    def reduce_end(state, after, totals, layer):
        kinds, grads, from_sibling, (send, recv, srcs, lands, _), tag = state
        from_chips = scatter_wait(send, recv, srcs, lands, after, tag)
        totals = list(totals)
        summed = add_chips(grads, from_sibling, from_chips, place, [totals[i] for i in kinds], layer)
        for i, t in zip(kinds, summed):
            totals[i] = t
        return totals

    begun = flying = None
    for l in reversed(range(nl)):
        hin, proj, bias2, xmid, o, lse, y, z, gu, f = saved[l]
        gw_in, gw_out, gw_fi, gw_fo = weights[l]
        behind_ffn = [begun[1][4]] if begun is not None else []
        dxm, dfb, act, dgu, h2, dg_qf, dg_pf = bwd_ffn(dx, f, xmid, gu, row(g_pre_ffn, l), row(g_post_ffn, l),
                                                        gw_fi, gw_fo, behind_ffn)
        behind_mix, behind_conv = [], []
        if begun is not None:
            flying = reduce_mid(begun, dxm)
            behind_mix.append(flying[3][4])
        gr_fo = wgrad(act, dfb, 256, D, False, "wgrad_ffn_out").reshape(NCHIP, DFF // NCHIP, D)
        gr_fi = wgrad(h2, dgu, 512, 2 * DFF // NCHIP, True, "wgrad_ffn_in")
        if l == 0:
            begun_ffn = reduce_begin([2, 3], [gr_fi, gr_fo], "0f")
            behind_mix.append(begun_ffn[1][4])
        gr_out, do, dco, dbg, dg_qm, dg_co, dg_ao = bwd_mix(dxm, z, o, y, proj, wconv_t[l], row(g_conv_out, l),
                                                             row(g_attn_out, l), row(g_post_mix, l), gm, gw_out,
                                                             behind_mix)
        gr_out = gr_out.reshape(NCHIP, D // NCHIP, D)
        if l == 0:
            flying_ffn = reduce_mid(begun_ffn, do)
            behind_conv.append(flying_ffn[3][4])
        dhc, dcg, dwc = bwd_conv(dco, proj, wconv_t[l], behind_conv)
        dq, dk, dv, db2 = bwd_attn(proj, o, do, lse, bias2)
        dx, dproj, hb, dg_pm = bwd_inproj(dxm, hin, dhc, dbg, dcg, dq, dk, dv, row(g_pre_mix, l), gw_in)
        if flying is not None:
            totals = reduce_end(flying, dx, totals, l + 1)
        gr_in = wgrad(hb, dproj, 512, PROJ // NCHIP, True, "wgrad_in")
        small["co"][l], small["ao"][l], small["pm"][l], small["qm"][l] = dg_co, dg_ao, dg_pm, dg_qm
        small["pf"][l], small["qf"][l] = dg_pf, dg_qf
        small["rel"][l] = _diag_vector_bwd(bias_reduce(db2.reshape(NH, QG_BWD, QG_BWD + LEFT)))
        small["wc"][l] = jnp.transpose(dwc[0:3], (1, 0))
        if l > 0:
            begun = reduce_begin([0, 1, 2, 3], [gr_in, gr_out, gr_fi, gr_fo], l)
    begun_mix = reduce_begin([0, 1], [gr_in, gr_out], "0m")
    totals = reduce_end(flying_ffn, begun_mix[1][4], totals, 0)
    flying_mix = reduce_mid(begun_mix, totals[2])
    share_ffn = share_start(totals[2:], "ffn")

    order = ("co", "ao", "pm", "qm", "pf", "qf", "rel", "wc")
    parts = [jnp.stack(small[k]) for k in order] + [loss_blk[0:1, 0:1]]
    shapes = [p.shape for p in parts]
    red_vec = small_allreduce(_pack(parts, 40), [share_ffn[3], flying_mix[3][4]])
    red = _unpack(red_vec, shapes)

    gr_fi, gr_fo = share_wait(*share_ffn[:3], red_vec, "ffn")
    big_fi = adamw(w_ffn_in, gr_fi, m_w_ffn_in, v_w_ffn_in, w_ffn_in.shape[1] // 4, "adamw_ffn_in")
    totals = reduce_end(flying_mix, big_fi[1], totals, 0)
    share_mix = share_start(totals[:2], "mix")
    big_fo = adamw(w_ffn_out, gr_fo, m_w_ffn_out, v_w_ffn_out, w_ffn_out.shape[1] // 4, "adamw_ffn_out",
                   [share_mix[3]])
    gr_in, gr_out = share_wait(*share_mix[:3], big_fo[1], "mix")
    big_in = adamw(w_in, gr_in, m_w_in, v_w_in, w_in.shape[1] // 4, "adamw_in")
    big_out = adamw(w_out, gr_out, m_w_out, v_w_out, w_out.shape[1] // 4, "adamw_out")
    big = [big_in, big_out, big_fi, big_fo]
    gr_co, gr_ao, gr_pm, gr_qm, gr_pf, gr_qf, gr_rel, gr_wc_full, loss = red
    gr_co, gr_ao, gr_pm, gr_qm, gr_pf, gr_qf = [a.reshape(nl, -1) for a in (gr_co, gr_ao, gr_pm, gr_qm, gr_pf, gr_qf)]
    gr_wc = lax.dynamic_slice_in_dim(gr_wc_full, chip * cwl, cwl, axis=1)
    loss = loss.reshape(())

    sw = [g_conv_out, g_attn_out, g_pre_mix, g_post_mix, g_pre_ffn, g_post_ffn, rel_bias, w_conv]
    sg = [gr_co, gr_ao, gr_pm, gr_qm, gr_pf, gr_qf, gr_rel, gr_wc]
    sm = [m_g_conv_out, m_g_attn_out, m_g_pre_mix, m_g_post_mix, m_g_pre_ffn, m_g_post_ffn, m_rel_bias, m_w_conv]
    sv = [v_g_conv_out, v_g_attn_out, v_g_pre_mix, v_g_post_mix, v_g_pre_ffn, v_g_post_ffn, v_rel_bias, v_w_conv]
    sshapes = [a.shape for a in sw]
    packed = [_pack(a, 32)[None] for a in (sw, sg, sm, sv)]
    s_out = [_unpack(a[0], sshapes) for a in adamw(*packed, 32, "adamw_small")]

    def leaves(big_i, small_i):
        b_in, b_out, b_fi, b_fo = big_i
        s_co, s_ao, s_pm, s_qm, s_pf, s_qf, s_rel, s_wc = small_i
        return [b_in, s_wc, s_rel, s_co, s_ao, b_out, s_pm, s_qm, s_pf, s_qf, b_fi, b_fo]

    out = [loss, dx[None]]
    out += leaves([b[0] for b in big], sg)
    for i in range(1, 4):
        out += leaves([b[i] for b in big], s_out[i])
    return tuple(out)
```

```python
import jax
import jax.numpy as jnp
from jax import lax
from jax.experimental import pallas as pl
from jax.experimental.pallas import tpu as pltpu

F32 = jnp.float32
BF16 = jnp.bfloat16

D = 1024
PROJ = 3072
CW = 512
HD = 64
NH = 8
CHUNK = 64
BAND = 576
REL_CLIP = 128
NREL = 2 * REL_CLIP + 1
DFF = 2816
DEPTH = 4
NCHIP = 4
EPS = 1e-6
NEG_INF = -1e30

ADAM_LR = 0.001
ADAM_B1 = 0.9
ADAM_B2 = 0.999
ADAM_EPS = 1e-08
ADAM_WD = 0.01
ADAM_STEP = 10

V7X_VMEM_BYTES = 64 * 1024 * 1024
VMEM_LIMIT = V7X_VMEM_BYTES - 8 * 1024 * 1024
LANES = 128
QG_FWD = 4 * CHUNK
QG_BWD = 2 * CHUNK
LEFT = BAND - CHUNK
TQ = 512
TM = 256
SMALL_COLS = 1024
MESH = pl.DeviceIdType.MESH
NT = (((1,), (1,)), ((), ()))
TN = (((0,), (0,)), ((), ()))


def _cp(sem=None, vmem=VMEM_LIMIT):
    return pltpu.CompilerParams(dimension_semantics=sem, vmem_limit_bytes=vmem)


def _any():
    return pl.BlockSpec(memory_space=pl.ANY)


def _const(shape):
    nd = len(shape)
    return pl.BlockSpec(shape, lambda *_: (0,) * nd)


def _behind(body, n_in, after):
    def ordered(*refs):
        return body(*refs[:n_in], *refs[n_in + len(after):])
    return ordered


def _rms(v, g):
    r = lax.rsqrt(jnp.mean(v * v, axis=-1, keepdims=True) + EPS)
    return v * r * g


def _rms_bwd(dy, v, g):
    r = lax.rsqrt(jnp.mean(v * v, axis=-1, keepdims=True) + EPS)
    vh = v * r
    dg = jnp.sum(dy * vh, axis=0, keepdims=True)
    dvh = dy * g
    dv = r * (dvh - vh * jnp.mean(dvh * vh, axis=-1, keepdims=True))
    return dv, dg


def _group_mean(v, gm):
    return jnp.dot(v.astype(BF16), gm, preferred_element_type=F32)


def _group_rms_bwd(dy, v, g, gm):
    r = lax.rsqrt(_group_mean(v * v, gm) + EPS)
    vh = v * r
    dg = jnp.sum(dy * vh, axis=0, keepdims=True)
    dvh = dy * g
    dv = r * (dvh - vh * _group_mean(dvh * vh, gm))
    return dv, dg


def _head_masks(scale):
    lane = lax.broadcasted_iota(jnp.int32, (1, LANES), 1)
    return [jnp.where((lane >= HD * a) & (lane < HD * (a + 1)), scale, 0.0).astype(BF16) for a in range(2)]


class _Resident:
    def __init__(self, src, dst, sem):
        self.first = pl.program_id(0) == 0
        self.copy = pltpu.make_async_copy(src, dst, sem)
        self.dst = dst

        @pl.when(self.first)
        def _():
            self.copy.start()

    def read(self):
        @pl.when(self.first)
        def _():
            self.copy.wait()

        return self.dst[...]


FF_CHUNKS = ((0, 1536), (1536, DFF))


def _stream_ffn_weights(wfi_hbm, wfo_hbm, wfi_v, wfo_v, sems, order, step):
    hw = DFF // 2
    per_matrix = {
        0: [(wfi_hbm.at[j], wfi_v.at[0, :, pl.ds(hw * j, hw)]) for j in range(2)],
        1: [(wfi_hbm.at[2 + j], wfi_v.at[1, :, pl.ds(hw * j, hw)]) for j in range(2)],
        2: [(wfo_hbm.at[j], wfo_v.at[pl.ds(hw * j, hw), :]) for j in range(2)],
    }
    pieces = [p for m in order for p in per_matrix[m]]
    slot = {m: 2 * k for k, m in enumerate(order)}

    def make_step(wait):
        def ready(m, chunk):
            if chunk == 0:
                wait(slot[m])
                wait(slot[m] + 1)
        return lambda: step(ready)

    copies = [pltpu.make_async_copy(src, dst, sems.at[k]) for k, (src, dst) in enumerate(pieces)]
    first = pl.program_id(0) == 0

    @pl.when(first)
    def _():
        for cp in copies:
            cp.start()
        make_step(lambda k: copies[k].wait())()

    @pl.when(jnp.logical_not(first))
    def _():
        make_step(lambda k: None)()


def _conv_taps(u_prev, u, scr):
    n = u.shape[0]
    scr[0:16, :] = u_prev
    scr[16:16 + n, :] = u
    return scr[15:15 + n, :], scr[14:14 + n, :]


def fwd_inproj(x, g, w_all, after=()):
    t = x.shape[0]
    wc = PROJ // NCHIP

    def body(x_ref, g_ref, w_hbm, o_ref, w_v):
        @pl.when(pl.program_id(0) == 0)
        def _():
            pltpu.sync_copy(w_hbm, w_v)

        h = _rms(x_ref[...], g_ref[...]).astype(BF16)
        for b in range(NCHIP):
            o_ref[:, wc * b:wc * (b + 1)] = jnp.dot(h, w_v[b], preferred_element_type=F32).astype(BF16)

    return pl.pallas_call(
        _behind(body, 3, after), grid=(t // TQ,),
        in_specs=[pl.BlockSpec((TQ, D), lambda i: (i, 0)), _const((1, D)), _any()] + [_any()] * len(after),
        out_specs=pl.BlockSpec((TQ, PROJ), lambda i: (i, 0)),
        out_shape=jax.ShapeDtypeStruct((t, PROJ), BF16),
        scratch_shapes=[pltpu.VMEM((NCHIP, D, wc), BF16)],
        compiler_params=_cp(("arbitrary",)), name="fwd_inproj")(x, g, w_all, *after)


def _attn_window_specs():
    return [
        pl.BlockSpec((TQ, CW), lambda i: (i, 3)),
        pl.BlockSpec((TQ, CW), lambda i: (jnp.maximum(i - 1, 0), 4)),
        pl.BlockSpec((TQ, CW), lambda i: (i, 4)),
        pl.BlockSpec((TQ, CW), lambda i: (jnp.maximum(i - 1, 0), 5)),
        pl.BlockSpec((TQ, CW), lambda i: (i, 5)),
    ]


def _conv_specs():
    return [
        pl.BlockSpec((TQ, 3 * CW), lambda i: (i, 0)),
        pl.BlockSpec((16, 3 * CW), lambda i: (jnp.maximum(i * (TQ // 16) - 1, 0), 0)),
    ]


def _conv_fwd(pc_ref, pcp_ref, wc_ref, scr, first):
    pc = pc_ref[...].astype(F32)
    hc, bg, cg = pc[:, :CW], pc[:, CW:2 * CW], pc[:, 2 * CW:]
    u = cg * hc
    pp = pcp_ref[...].astype(F32)
    u_prev = jnp.where(first, 0.0, pp[:, 2 * CW:] * pp[:, :CW])
    u1, u2 = _conv_taps(u_prev, u, scr)
    cout = wc_ref[0:1, :] * u2 + wc_ref[1:2, :] * u1 + wc_ref[2:3, :] * u
    return hc, bg, cg, u, u1, u2, cout


def _key_penalty(first, r0, kg):
    col = lax.broadcasted_iota(jnp.int32, (1, kg), 1)
    limit = jnp.where(first, TQ - r0, 0)
    return jnp.where(col < limit, NEG_INF, 0.0)


def fwd_mix(x, proj, bias2, wconv_t, g_co, g_ao, g_pm, gm, wout_all):
    t = x.shape[0]
    qg, kg = QG_FWD, QG_FWD + LEFT

    def body(x_ref, pc_ref, pcp_ref, q_ref, kp_ref, kc_ref, vp_ref, vc_ref, b2_ref, wc_ref, gco_ref, gao_ref, gpm_ref,
             gm_ref, wout_hbm, xmid_ref, o_ref, lse_ref, y_ref, z_ref, wout_v, kwin, vwin, cscr, sems):
        i = pl.program_id(0)
        first = i == 0
        wout = _Resident(wout_hbm, wout_v, sems.at[0])
        kwin[0:TQ, :] = kp_ref[...]
        kwin[TQ:2 * TQ, :] = kc_ref[...]
        vwin[0:TQ, :] = vp_ref[...]
        vwin[TQ:2 * TQ, :] = vc_ref[...]
        qmask = _head_masks(HD ** -0.5)
        low = lax.broadcasted_iota(jnp.int32, (1, LANES), 1) < HD

        def group(g, carry):
            r0 = pl.multiple_of(g * qg, qg)
            pen = _key_penalty(first, r0, kg)
            for hp in range(NH // 2):
                ls = slice(LANES * hp, LANES * (hp + 1))
                qb = q_ref[pl.ds(r0, qg), ls]
                q2 = jnp.concatenate([qb * qmask[0], qb * qmask[1]], axis=0)
                s = lax.dot_general(q2, kwin[pl.ds(r0, kg), ls], NT, preferred_element_type=F32)
                s = s + b2_ref[hp] + pen
                m = jnp.max(s, axis=-1, keepdims=True)
                p = jnp.exp(s - m)
                l = jnp.sum(p, axis=-1, keepdims=True)
                o2 = jnp.dot(p.astype(BF16), vwin[pl.ds(r0, kg), ls], preferred_element_type=F32) * (1.0 / l)
                lse2 = m + jnp.log(l)
                o_ref[pl.ds(r0, qg), ls] = jnp.where(low, o2[:qg], o2[qg:])
                lse_ref[pl.ds(r0, qg), ls] = jnp.where(low, lse2[:qg], lse2[qg:])
            return carry

        lax.fori_loop(0, TQ // qg, group, 0)

        _, bg, _, _, _, _, cout = _conv_fwd(pc_ref, pcp_ref, wc_ref, cscr, first)
        yc = bg * cout
        gmv = gm_ref[...]
        ycn = yc * lax.rsqrt(_group_mean(yc * yc, gmv) + EPS) * gco_ref[...]
        oa = o_ref[...]
        oan = oa * lax.rsqrt(_group_mean(oa * oa, gmv) + EPS) * gao_ref[...]
        y_ref[:, 0:CW] = ycn.astype(BF16)
        y_ref[:, CW:2 * CW] = oan.astype(BF16)
        z = jnp.dot(y_ref[...], wout.read(), preferred_element_type=F32)
        z_ref[...] = z
        xmid_ref[...] = x_ref[...] + _rms(z, gpm_ref[...])

    row = lambda w: pl.BlockSpec((TQ, w), lambda i: (i, 0))
    return pl.pallas_call(
        body, grid=(t // TQ,),
        in_specs=[row(D)] + _conv_specs() + _attn_window_specs() + [
            _const((NH // 2, 2 * qg, kg)), _const((8, CW)), _const((1, CW)), _const((1, CW)), _const((1, D)),
            _const((CW, CW)), _any()],
        out_specs=[row(D), row(CW), row(CW), row(D), row(D)],
        out_shape=[jax.ShapeDtypeStruct((t, D), F32), jax.ShapeDtypeStruct((t, CW), F32),
                   jax.ShapeDtypeStruct((t, CW), F32), jax.ShapeDtypeStruct((t, D), BF16),
                   jax.ShapeDtypeStruct((t, D), F32)],
        scratch_shapes=[pltpu.VMEM((D, D), BF16), pltpu.VMEM((2 * TQ, CW), BF16), pltpu.VMEM((2 * TQ, CW), BF16),
                        pltpu.VMEM((TQ + 16, CW), F32), pltpu.SemaphoreType.DMA((1,))],
        compiler_params=_cp(("arbitrary",)), name="fwd_mix",
    )(x, proj, proj, proj, proj, proj, proj, proj, bias2, wconv_t, g_co, g_ao, g_pm, gm, wout_all)


def fwd_ffn(xmid, g_pre, g_post, wfi_all, wfo_all, after=(), target=None):
    t = xmid.shape[0]
    n_in = 5 if target is None else 6

    def body(*refs):
        x_ref, gpre_ref, gpost_ref, wfi_hbm, wfo_hbm = refs[:5]
        t_ref = None if target is None else refs[5]
        gu_ref, f_ref, xo_ref = refs[n_in:n_in + 3]
        l_ref = None if target is None else refs[n_in + 3]
        wfi_v, wfo_v, sems = refs[-3:]

        if target is not None:
            @pl.when(pl.program_id(0) == 0)
            def _():
                l_ref[...] = jnp.zeros_like(l_ref)

        def step(ready):
            xv = x_ref[...]
            h = _rms(xv, gpre_ref[...]).astype(BF16)
            f = jnp.zeros((TM, D), F32)
            for ci, (a, b) in enumerate(FF_CHUNKS):
                ready(0, ci)
                gate = jnp.dot(h, wfi_v[0, :, a:b], preferred_element_type=F32)
                ready(1, ci)
                up = jnp.dot(h, wfi_v[1, :, a:b], preferred_element_type=F32)
                gu_ref[:, a:b] = gate.astype(BF16)
                gu_ref[:, DFF + a:DFF + b] = up.astype(BF16)
                act = gate * (1.0 / (1.0 + jnp.exp(-gate))) * up
                ready(2, ci)
                f = f + jnp.dot(act.astype(BF16), wfo_v[a:b, :], preferred_element_type=F32)
            f_ref[...] = f
            xo = xv + _rms(f, gpost_ref[...])
            if target is None:
                xo_ref[...] = xo
            else:
                e = xo - t_ref[...]
                xo_ref[...] = e * (1.0 / D)
                rows = jnp.sum(e * e, axis=-1, keepdims=True) * (1.0 / D)
                l_ref[...] += 0.5 * jnp.sum(rows, axis=0, keepdims=True)

        _stream_ffn_weights(wfi_hbm, wfo_hbm, wfi_v, wfo_v, sems, (0, 1, 2), step)

    row = lambda w: pl.BlockSpec((TM, w), lambda i: (i, 0))
    with_loss = target is not None
    return pl.pallas_call(
        _behind(body, n_in, after), grid=(t // TM,),
        in_specs=[row(D), _const((1, D)), _const((1, D)), _any(), _any()] + [row(D)] * with_loss
        + [_any()] * len(after),
        out_specs=[row(2 * DFF), row(D), row(D)] + [_const((8, LANES))] * with_loss,
        out_shape=[jax.ShapeDtypeStruct((t, 2 * DFF), BF16), jax.ShapeDtypeStruct((t, D), F32),
                   jax.ShapeDtypeStruct((t, D), F32)] + [jax.ShapeDtypeStruct((8, LANES), F32)] * with_loss,
        scratch_shapes=[pltpu.VMEM((2, D, DFF), BF16), pltpu.VMEM((DFF, D), BF16), pltpu.SemaphoreType.DMA((6,))],
        compiler_params=_cp(("arbitrary",)), name="fwd_ffn_loss" if with_loss else "fwd_ffn",
    )(xmid, g_pre, g_post, wfi_all, wfo_all, *([target] * with_loss), *after)


def bwd_ffn(dx, f, xmid, gu, g_pre, g_post, wfi_all, wfo_all, after=()):
    t = dx.shape[0]

    def body(dx_ref, f_ref, x_ref, gu_ref, gpre_ref, gpost_ref, wfi_hbm, wfo_hbm,
             dxm_ref, df_ref, act_ref, dgu_ref, h_ref, dgpost_ref, dgpre_ref, wfi_v, wfo_v, sems):
        @pl.when(pl.program_id(0) == 0)
        def _():
            dgpost_ref[...] = jnp.zeros_like(dgpost_ref)
            dgpre_ref[...] = jnp.zeros_like(dgpre_ref)

        def step(ready):
            dxo = dx_ref[...]
            df, dgp = _rms_bwd(dxo, f_ref[...], gpost_ref[...])
            dgpost_ref[...] += dgp
            dfb = df.astype(BF16)
            df_ref[...] = dfb
            dh = jnp.zeros((TM, D), F32)
            for ci, (a, b) in enumerate(FF_CHUNKS):
                ready(2, ci)
                dact = lax.dot_general(dfb, wfo_v[a:b, :], NT, preferred_element_type=F32)
                gate = gu_ref[:, a:b].astype(F32)
                up = gu_ref[:, DFF + a:DFF + b].astype(F32)
                sig = 1.0 / (1.0 + jnp.exp(-gate))
                silu = gate * sig
                act_ref[:, a:b] = (silu * up).astype(BF16)
                dup = (dact * silu).astype(BF16)
                dgate = (dact * up * (sig * (1.0 + gate * (1.0 - sig)))).astype(BF16)
                dgu_ref[:, a:b] = dgate
                dgu_ref[:, DFF + a:DFF + b] = dup
                ready(0, ci)
                dh = dh + lax.dot_general(dgate, wfi_v[0, :, a:b], NT, preferred_element_type=F32)
                ready(1, ci)
                dh = dh + lax.dot_general(dup, wfi_v[1, :, a:b], NT, preferred_element_type=F32)
            xv = x_ref[...]
            gpre = gpre_ref[...]
            h_ref[...] = _rms(xv, gpre).astype(BF16)
            dxv, dgq = _rms_bwd(dh, xv, gpre)
            dgpre_ref[...] += dgq
            dxm_ref[...] = dxo + dxv

        _stream_ffn_weights(wfi_hbm, wfo_hbm, wfi_v, wfo_v, sems, (2, 0, 1), step)

    row = lambda w: pl.BlockSpec((TM, w), lambda i: (i, 0))
    return pl.pallas_call(
        _behind(body, 8, after), grid=(t // TM,),
        in_specs=[row(D), row(D), row(D), row(2 * DFF), _const((1, D)), _const((1, D)), _any(), _any()]
        + [_any()] * len(after),
        out_specs=[row(D), row(D), row(DFF), row(2 * DFF), row(D), _const((1, D)), _const((1, D))],
        out_shape=[jax.ShapeDtypeStruct((t, D), F32), jax.ShapeDtypeStruct((t, D), BF16),
                   jax.ShapeDtypeStruct((t, DFF), BF16), jax.ShapeDtypeStruct((t, 2 * DFF), BF16),
                   jax.ShapeDtypeStruct((t, D), BF16), jax.ShapeDtypeStruct((1, D), F32),
                   jax.ShapeDtypeStruct((1, D), F32)],
        scratch_shapes=[pltpu.VMEM((2, D, DFF), BF16), pltpu.VMEM((DFF, D), BF16), pltpu.SemaphoreType.DMA((6,))],
        compiler_params=_cp(("arbitrary",)), name="bwd_ffn")(dx, f, xmid, gu, g_pre, g_post, wfi_all, wfo_all, *after)


def bwd_mix(dxm, z, o, y, proj, wconv_t, g_co, g_ao, g_pm, gm, wout_all, after=()):
    t = dxm.shape[0]

    def body(dx_ref, z_ref, o_ref, y_ref, pc_ref, pcp_ref, wc_ref, gco_ref, gao_ref, gpm_ref, gm_ref, wout_hbm,
             dwo_ref, do_ref, dco_ref, dbg_ref, dgpm_ref, dgco_ref, dgao_ref, wout_v, cscr):
        first = pl.program_id(0) == 0

        @pl.when(first)
        def _():
            pltpu.sync_copy(wout_hbm, wout_v)
            dwo_ref[...] = jnp.zeros_like(dwo_ref)
            dgpm_ref[...] = jnp.zeros_like(dgpm_ref)
            dgco_ref[...] = jnp.zeros_like(dgco_ref)
            dgao_ref[...] = jnp.zeros_like(dgao_ref)

        dz, dgp = _rms_bwd(dx_ref[...], z_ref[...], gpm_ref[...])
        dgpm_ref[...] += dgp
        dzb = dz.astype(BF16)
        dwo_ref[...] += lax.dot_general(y_ref[...], dzb, TN, preferred_element_type=F32)
        gmv = gm_ref[...]
        _, bg, _, _, _, _, cout = _conv_fwd(pc_ref, pcp_ref, wc_ref, cscr, first)
        dy_conv = lax.dot_general(dzb, wout_v[0:CW, :], NT, preferred_element_type=F32)
        dyc, dgc = _group_rms_bwd(dy_conv, bg * cout, gco_ref[...], gmv)
        dgco_ref[...] += dgc
        dbg_ref[...] = (dyc * cout).astype(BF16)
        dco_ref[...] = dyc * bg
        dy_attn = lax.dot_general(dzb, wout_v[CW:2 * CW, :], NT, preferred_element_type=F32)
        do, dga = _group_rms_bwd(dy_attn, o_ref[...], gao_ref[...], gmv)
        dgao_ref[...] += dga
        do_ref[...] = do.astype(BF16)

    row = lambda w: pl.BlockSpec((TQ, w), lambda i: (i, 0))
    return pl.pallas_call(
        _behind(body, 12, after), grid=(t // TQ,),
        in_specs=[row(D), row(D), row(CW), row(D)] + _conv_specs() + [
            _const((8, CW)), _const((1, CW)), _const((1, CW)), _const((1, D)), _const((CW, CW)), _any()]
        + [_any()] * len(after),
        out_specs=[_const((D, D)), row(CW), row(CW), row(CW), _const((1, D)), _const((1, CW)), _const((1, CW))],
        out_shape=[jax.ShapeDtypeStruct((D, D), F32), jax.ShapeDtypeStruct((t, CW), BF16),
                   jax.ShapeDtypeStruct((t, CW), F32), jax.ShapeDtypeStruct((t, CW), BF16),
                   jax.ShapeDtypeStruct((1, D), F32), jax.ShapeDtypeStruct((1, CW), F32),
                   jax.ShapeDtypeStruct((1, CW), F32)],
        scratch_shapes=[pltpu.VMEM((D, D), BF16), pltpu.VMEM((TQ + 16, CW), F32)],
        compiler_params=_cp(("arbitrary",)), name="bwd_mix",
    )(dxm, z, o, y, proj, proj, wconv_t, g_co, g_ao, g_pm, gm, wout_all, *after)


def bwd_conv(dco, proj, wconv_t, after=()):
    t = dco.shape[0]
    nt = t // TQ

    def body(d_ref, dn_ref, pc_ref, pcp_ref, wc_ref, dhc_ref, dcg_ref, dw_ref, cscr, dscr):
        i = pl.program_id(0)
        first = i == 0

        @pl.when(first)
        def _():
            dw_ref[...] = jnp.zeros_like(dw_ref)

        hc, _, cg, u, u1, u2, _ = _conv_fwd(pc_ref, pcp_ref, wc_ref, cscr, first)
        d0 = d_ref[...]
        dscr[0:TQ, :] = d0
        dscr[TQ:TQ + 8, :] = jnp.where(i == nt - 1, 0.0, dn_ref[...])
        d1 = dscr[1:TQ + 1, :]
        d2 = dscr[2:TQ + 2, :]
        du = wc_ref[2:3, :] * d0 + wc_ref[1:2, :] * d1 + wc_ref[0:1, :] * d2
        dhc_ref[...] = (du * cg).astype(BF16)
        dcg_ref[...] = (du * hc).astype(BF16)
        dw_ref[0:1, :] += jnp.sum(d0 * u2, axis=0, keepdims=True)
        dw_ref[1:2, :] += jnp.sum(d0 * u1, axis=0, keepdims=True)
        dw_ref[2:3, :] += jnp.sum(d0 * u, axis=0, keepdims=True)

    row = lambda w: pl.BlockSpec((TQ, w), lambda i: (i, 0))
    nxt = pl.BlockSpec((8, CW), lambda i: (jnp.minimum((i + 1) * (TQ // 8), t // 8 - 1), 0))
    return pl.pallas_call(
        _behind(body, 5, after), grid=(nt,),
        in_specs=[row(CW), nxt] + _conv_specs() + [_const((8, CW))] + [_any()] * len(after),
        out_specs=[row(CW), row(CW), _const((8, CW))],
        out_shape=[jax.ShapeDtypeStruct((t, CW), BF16), jax.ShapeDtypeStruct((t, CW), BF16),
                   jax.ShapeDtypeStruct((8, CW), F32)],
        scratch_shapes=[pltpu.VMEM((TQ + 16, CW), F32), pltpu.VMEM((TQ + 8, CW), F32)],
        compiler_params=_cp(("arbitrary",)), name="bwd_conv")(dco, dco, proj, proj, wconv_t, *after)


def bwd_attn(proj, o, do, lse, bias2):
    t = o.shape[0]
    nt = t // TQ
    qg, kg = QG_BWD, QG_BWD + LEFT
    nkb = (t + TQ) // LANES

    def body(q_ref, kp_ref, kc_ref, vp_ref, vc_ref, o_ref, do_ref, lse_ref, b2_ref,
             dq_ref, dk_hbm, dv_hbm, db_hbm, kwin, vwin, dk_acc, dv_acc, db_acc):
        i = pl.program_id(0)
        first = i == 0

        @pl.when(first)
        def _():
            dk_acc[...] = jnp.zeros_like(dk_acc)
            dv_acc[...] = jnp.zeros_like(dv_acc)
            db_acc[...] = jnp.zeros_like(db_acc)

        kwin[0:TQ, :] = kp_ref[...]
        kwin[TQ:2 * TQ, :] = kc_ref[...]
        vwin[0:TQ, :] = vp_ref[...]
        vwin[TQ:2 * TQ, :] = vc_ref[...]
        scale = HD ** -0.5
        qmask = _head_masks(scale)
        vmask = _head_masks(1.0)
        low = lax.broadcasted_iota(jnp.int32, (1, LANES), 1) < HD

        def group(g, carry):
            r0 = pl.multiple_of(g * qg, qg)
            base = i * (TQ // LANES) + g * (qg // LANES)
            pen = _key_penalty(first, r0, kg)
            for hp in range(NH // 2):
                ls = slice(LANES * hp, LANES * (hp + 1))
                qb = q_ref[pl.ds(r0, qg), ls]
                kw = kwin[pl.ds(r0, kg), ls]
                dob = do_ref[pl.ds(r0, qg), ls]
                prod = dob.astype(F32) * o_ref[pl.ds(r0, qg), ls]
                lseb = lse_ref[pl.ds(r0, qg), ls]
                q2 = jnp.concatenate([qb * qmask[0], qb * qmask[1]], axis=0)
                do2 = jnp.concatenate([dob * vmask[0], dob * vmask[1]], axis=0)
                lse2 = jnp.concatenate([lseb[:, 0:1], lseb[:, HD:HD + 1]], axis=0)
                dsum = jnp.concatenate([jnp.sum(jnp.where(low, prod, 0.0), axis=-1, keepdims=True),
                                        jnp.sum(jnp.where(low, 0.0, prod), axis=-1, keepdims=True)], axis=0)
                s = lax.dot_general(q2, kw, NT, preferred_element_type=F32) + b2_ref[hp] + pen
                p = jnp.exp(s - lse2)
                dp = lax.dot_general(do2, vwin[pl.ds(r0, kg), ls], NT, preferred_element_type=F32)
                ds = p * (dp - dsum)
                db_acc[hp] += ds
                dsb = ds.astype(BF16)
                dq2 = jnp.dot(dsb, kw, preferred_element_type=F32)
                dq_ref[pl.ds(r0, qg), ls] = (jnp.where(low, dq2[:qg], dq2[qg:]) * scale).astype(BF16)
                dkt = lax.dot_general(q2, dsb, TN, preferred_element_type=F32)
                dvt = lax.dot_general(do2, p.astype(BF16), TN, preferred_element_type=F32)
                for kb in range(kg // LANES):
                    dk_acc[base + kb, ls, :] += dkt[:, LANES * kb:LANES * (kb + 1)]
                    dv_acc[base + kb, ls, :] += dvt[:, LANES * kb:LANES * (kb + 1)]
            return carry

        lax.fori_loop(0, TQ // qg, group, 0)

        @pl.when(i == nt - 1)
        def _():
            pltpu.sync_copy(dk_acc, dk_hbm)
            pltpu.sync_copy(dv_acc, dv_hbm)
            pltpu.sync_copy(db_acc, db_hbm)

    row = lambda w: pl.BlockSpec((TQ, w), lambda i: (i, 0))
    return pl.pallas_call(
        body, grid=(nt,),
        in_specs=_attn_window_specs() + [row(CW), row(CW), row(CW), _const((NH // 2, 2 * qg, kg))],
        out_specs=[row(CW), _any(), _any(), _any()],
        out_shape=[jax.ShapeDtypeStruct((t, CW), BF16), jax.ShapeDtypeStruct((nkb, CW, LANES), F32),
                   jax.ShapeDtypeStruct((nkb, CW, LANES), F32), jax.ShapeDtypeStruct((NH // 2, 2 * qg, kg), F32)],
        scratch_shapes=[pltpu.VMEM((2 * TQ, CW), BF16), pltpu.VMEM((2 * TQ, CW), BF16),
                        pltpu.VMEM((nkb, CW, LANES), F32), pltpu.VMEM((nkb, CW, LANES), F32),
                        pltpu.VMEM((NH // 2, 2 * qg, kg), F32)],
        compiler_params=_cp(("arbitrary",)), name="bwd_attn",
    )(proj, proj, proj, proj, proj, o, do, lse, bias2)


def bwd_inproj(dxm, x, dhc, dbg, dcg, dq, dk, dv, g, w_all):
    t = x.shape[0]
    nt = t // TQ
    wc = PROJ // NCHIP

    def body(dxm_ref, x_ref, dhc_ref, dbg_ref, dcg_ref, dq_ref, dk_ref, dv_ref, g_ref, w_hbm,
             dx_ref, dw_hbm, dg_ref, w_v, dp_ref, dw_acc):
        @pl.when(pl.program_id(0) == 0)
        def _():
            pltpu.sync_copy(w_hbm, w_v)
            dg_ref[...] = jnp.zeros_like(dg_ref)
            dw_acc[...] = jnp.zeros_like(dw_acc)

        dp_ref[:, 0:CW] = dhc_ref[...]
        dp_ref[:, CW:2 * CW] = dbg_ref[...]
        dp_ref[:, 2 * CW:3 * CW] = dcg_ref[...]
        dp_ref[:, 3 * CW:4 * CW] = dq_ref[...]
        for kb in range(TQ // LANES):
            rows = slice(LANES * kb, LANES * (kb + 1))
            dp_ref[rows, 4 * CW:5 * CW] = jnp.transpose(dk_ref[kb]).astype(BF16)
            dp_ref[rows, 5 * CW:6 * CW] = jnp.transpose(dv_ref[kb]).astype(BF16)
        dh = jnp.zeros((TQ, D), F32)
        for b in range(NCHIP):
            dh = dh + lax.dot_general(dp_ref[:, wc * b:wc * (b + 1)], w_v[b], NT, preferred_element_type=F32)
        xv = x_ref[...]
        gv = g_ref[...]
        hb = _rms(xv, gv).astype(BF16)
        for b in range(NCHIP):
            dw_acc[b] += lax.dot_general(hb, dp_ref[:, wc * b:wc * (b + 1)], TN, preferred_element_type=F32)
        dxv, dgv = _rms_bwd(dh, xv, gv)
        dg_ref[...] += dgv
        dx_ref[...] = dxm_ref[...] + dxv

        @pl.when(pl.program_id(0) == nt - 1)
        def _():
            pltpu.sync_copy(dw_acc, dw_hbm)

    row = lambda w: pl.BlockSpec((TQ, w), lambda i: (i, 0))
    pad = pl.BlockSpec((TQ // LANES, CW, LANES), lambda i: (i + 1, 0, 0))
    return pl.pallas_call(
        body, grid=(nt,),
        in_specs=[row(D), row(D), row(CW), row(CW), row(CW), row(CW), pad, pad, _const((1, D)), _any()],
        out_specs=[row(D), _any(), _const((1, D))],
        out_shape=[jax.ShapeDtypeStruct((t, D), F32), jax.ShapeDtypeStruct((NCHIP, D, wc), F32),
                   jax.ShapeDtypeStruct((1, D), F32)],
        scratch_shapes=[pltpu.VMEM((NCHIP, D, wc), BF16), pltpu.VMEM((TQ, PROJ), BF16),
                        pltpu.VMEM((NCHIP, D, wc), F32)],
        compiler_params=_cp(("arbitrary",)), name="bwd_inproj",
    )(dxm, x, dhc, dbg, dcg, dq, dk, dv, g, w_all)


def wgrad(a, b, kb, nb, by_columns, name):
    t, k = a.shape
    n = b.shape[1]
    tk = 512

    def body(a_ref, b_ref, o_ref):
        o_ref[...] = jnp.zeros_like(o_ref)
        for c in range(t // tk):
            o_ref[...] += lax.dot_general(a_ref[tk * c:tk * (c + 1), :], b_ref[tk * c:tk * (c + 1), :], TN,
                                          preferred_element_type=F32)

    if by_columns:
        assert nb == n // NCHIP
        out_spec = pl.BlockSpec((None, kb, nb), lambda ki, ni: (ni, ki, 0))
        out_shape = jax.ShapeDtypeStruct((NCHIP, k, nb), F32)
    else:
        assert nb == n
        out_spec = pl.BlockSpec((kb, nb), lambda ki, ni: (ki, 0))
        out_shape = jax.ShapeDtypeStruct((k, n), F32)
    return pl.pallas_call(
        body, grid=(k // kb, n // nb),
        in_specs=[pl.BlockSpec((t, kb), lambda ki, ni: (0, ki)), pl.BlockSpec((t, nb), lambda ki, ni: (0, ni))],
        out_specs=out_spec, out_shape=out_shape,
        compiler_params=_cp(("arbitrary", "arbitrary")), name=name)(a, b)


TOE = 1024
assert 2 * QG_FWD + LEFT <= TOE
N_FLAT = LEFT - REL_CLIP + 1
N_VAR = BAND - N_FLAT


def _diag_vector(table):
    last = table[:, 2 * REL_CLIP:]
    var = table[:, 2 * REL_CLIP - N_VAR:2 * REL_CLIP][:, ::-1]
    return jnp.concatenate([jnp.broadcast_to(last, (NH, N_FLAT)), var, jnp.broadcast_to(last, (NH, TOE - BAND))], axis=1)


def _diag_vector_bwd(dvec):
    dlast = jnp.sum(dvec[:, :N_FLAT], axis=1, keepdims=True) + jnp.sum(dvec[:, BAND:], axis=1, keepdims=True)
    dvar = dvec[:, N_FLAT:BAND][:, ::-1]
    return jnp.concatenate([jnp.zeros((NH, 2 * REL_CLIP - N_VAR), F32), dvar, dlast], axis=1)


def _band_valid(qg):
    r = lax.broadcasted_iota(jnp.int32, (qg, qg + LEFT), 0)
    p = lax.broadcasted_iota(jnp.int32, (qg, qg + LEFT), 1)
    start = lax.shift_left(lax.shift_right_logical(r, 6), 6)
    return (p >= start) & (p < start + BAND)


def bias_expand(vec, qgs, after=()):
    def body(v_ref, *o_refs):
        for qg, o_ref in zip(qgs, o_refs):
            valid = _band_valid(qg)
            for h in range(NH):
                rows = jnp.broadcast_to(v_ref[h:h + 1, :], (qg, TOE))
                toe = pltpu.roll(rows, 0, 1, stride=1, stride_axis=0)
                o_ref[h // 2, qg * (h % 2):qg * (h % 2 + 1), :] = jnp.where(valid, toe[:, :qg + LEFT], NEG_INF)

    vm = pl.BlockSpec(memory_space=pltpu.VMEM)
    return pl.pallas_call(_behind(body, 1, after), in_specs=[vm] + [_any()] * len(after), out_specs=[vm] * len(qgs),
                          out_shape=[jax.ShapeDtypeStruct((NH // 2, 2 * qg, qg + LEFT), F32) for qg in qgs],
                          name="bias_expand")(vec, *after)


def bias_reduce(db2):
    _, qg, kg = db2.shape

    def body(d_ref, o_ref):
        ii = lax.broadcasted_iota(jnp.int32, (kg, kg), 0)
        jj = lax.broadcasted_iota(jnp.int32, (kg, kg), 1)
        flip = jnp.where(ii + jj == kg - 1, 1.0, 0.0).astype(BF16)
        for h in range(NH):
            rest = d_ref[h]
            rev = jnp.zeros((qg, kg), F32)
            for _ in range(3):
                term = rest.astype(BF16)
                rev = rev + jnp.dot(term, flip, preferred_element_type=F32)
                rest = rest - term.astype(F32)
            d = jnp.concatenate([jnp.zeros((qg, TOE - kg), F32), rev], axis=1)
            back = pltpu.roll(d, 0, 1, stride=1, stride_axis=0)
            o_ref[h:h + 1, :] = jnp.sum(back, axis=0, keepdims=True)

    rev = pl.pallas_call(body, out_shape=jax.ShapeDtypeStruct((NH, TOE), F32), name="bias_reduce")(db2)
    return rev[:, ::-1]


def _place():
    x, y, c = lax.axis_index("x"), lax.axis_index("y"), lax.axis_index("c")
    chips = [(1 - x, y), (x, 1 - y), (1 - x, 1 - y)]
    return x, y, c, chips


def _half(ref_rows, c):
    return pl.ds(c * (ref_rows // 2), ref_rows // 2)


HBM_SPEC = pl.BlockSpec(memory_space=pltpu.HBM)
SEM_SPEC = pl.BlockSpec(memory_space=pltpu.SEMAPHORE)
IN_FLIGHT = pltpu.CompilerParams(has_side_effects=pltpu.SideEffectType.DATAFLOW_SIDE_EFFECTING)


def _in_hbm(a):
    return pltpu.with_memory_space_constraint(a, pltpu.HBM)


def cast_to_slot(ws, chip, layer, after=()):
    n = len(ws)
    steps = 4

    def body(b_ref, *refs):
        del b_ref
        for w_ref, o_ref in zip(refs[:n], refs[n + len(after):]):
            o_ref[...] = w_ref[...].astype(BF16)

    grid_spec = pltpu.PrefetchScalarGridSpec(
        num_scalar_prefetch=1, grid=(steps,),
        in_specs=[pl.BlockSpec((None, w.shape[1] // steps, w.shape[2]), lambda r, b: (layer, r, 0)) for w in ws]
        + [_any()] * len(after),
        out_specs=[pl.BlockSpec((None, w.shape[1] // steps, w.shape[2]), lambda r, b: (b[0], r, 0)) for w in ws])
    return pl.pallas_call(body, grid_spec=grid_spec,
                          out_shape=[jax.ShapeDtypeStruct((NCHIP,) + w.shape[1:], BF16) for w in ws],
                          compiler_params=_cp(("arbitrary",)), name="cast_to_slot")(chip, *ws, *after)


def _gather_copies(bufs, send, recv):
    x, y, c, chips = _place()
    b = 2 * x + y
    out = []
    for k, buf in enumerate(bufs):
        rows = buf.shape[1]
        mine = buf.at[b, _half(rows, c), :]
        for j, (cx, cy) in enumerate(chips):
            theirs = buf.at[2 * cx + cy, _half(rows, c), :]
            sems = dict(send_sem=send.at[3 * k + j], recv_sem=recv.at[3 * k + j],
                        device_id=(cx, cy, c), device_id_type=MESH)
            out.append((pltpu.make_async_remote_copy(src_ref=mine, dst_ref=mine, **sems),
                        pltpu.make_async_remote_copy(src_ref=theirs, dst_ref=theirs, **sems)))
    return out


def gather_start(bufs, after, layer):
    n = len(bufs)

    def body(*refs):
        ins = refs[:n]
        send, recv = refs[n + 1], refs[n + 2]
        token = refs[-1]
        for start, _ in _gather_copies(ins, send, recv):
            start.start()
        token[...] = jnp.zeros_like(token)

    sems = pltpu.SemaphoreType.DMA((3 * n,))
    res = pl.pallas_call(
        body, name=f"gather_start_{layer}",
        in_specs=[HBM_SPEC] * n + [_any()],
        out_specs=[SEM_SPEC, SEM_SPEC] + [HBM_SPEC] * n + [pl.BlockSpec(memory_space=pltpu.VMEM)],
        out_shape=[sems, sems] + [pltpu.HBM(b.shape, b.dtype) for b in bufs] + [jax.ShapeDtypeStruct((8, LANES), F32)],
        input_output_aliases={k: 2 + k for k in range(n)}, compiler_params=IN_FLIGHT,
    )(*[_in_hbm(b) for b in bufs], after)
    return res[0], res[1], res[2:2 + n], res[-1]


def gather_wait(send, recv, bufs, after, layer):
    n = len(bufs)

    def body(*refs):
        ins = refs[:n]
        send_ref, recv_ref = refs[n], refs[n + 1]
        for start, arrival in _gather_copies(ins, send_ref, recv_ref):
            start.wait_send()
            arrival.wait_recv()

    return pl.pallas_call(
        body, name=f"gather_wait_{layer}",
        in_specs=[HBM_SPEC] * n + [SEM_SPEC, SEM_SPEC, _any()], out_specs=[HBM_SPEC] * n,
        out_shape=[pltpu.HBM(b.shape, b.dtype) for b in bufs],
        input_output_aliases={k: k for k in range(n)}, compiler_params=IN_FLIGHT,
    )(*bufs, send, recv, after)


def gather_forward(bufs):
    n = len(bufs)

    def body(*refs):
        outs = refs[n:2 * n]
        send, recv = refs[2 * n:]
        x, y, c, chips = _place()
        cps = []
        for k in range(n):
            rows = outs[k].shape[1]
            for j, (cx, cy) in enumerate(chips):
                sems = dict(send_sem=send.at[3 * k + j], recv_sem=recv.at[3 * k + j],
                            device_id=(x, y, 1 - c), device_id_type=MESH)
                mine = outs[k].at[2 * cx + cy, _half(rows, c), :]
                theirs = outs[k].at[2 * cx + cy, _half(rows, 1 - c), :]
                cp = pltpu.make_async_remote_copy(src_ref=mine, dst_ref=mine, **sems)
                cp.start()
                cps.append((cp, pltpu.make_async_remote_copy(src_ref=theirs, dst_ref=theirs, **sems)))
        for cp, arrival in cps:
            cp.wait_send()
            arrival.wait_recv()

    return pl.pallas_call(
        body, in_specs=[_any()] * n, out_specs=[_any()] * n,
        out_shape=[jax.ShapeDtypeStruct(b.shape, b.dtype) for b in bufs], input_output_aliases={k: k for k in range(n)},
        scratch_shapes=[pltpu.SemaphoreType.DMA((3 * n,)), pltpu.SemaphoreType.DMA((3 * n,))],
        name="gather_forward")(*bufs)


def _forward_copies(bufs, send, recv):
    x, y, c, chips = _place()
    out = []
    for k, buf in enumerate(bufs):
        rows = buf.shape[1]
        for j, (cx, cy) in enumerate(chips):
            sems = dict(send_sem=send.at[3 * k + j], recv_sem=recv.at[3 * k + j],
                        device_id=(x, y, 1 - c), device_id_type=MESH)
            mine = buf.at[2 * cx + cy, _half(rows, c), :]
            theirs = buf.at[2 * cx + cy, _half(rows, 1 - c), :]
            out.append((pltpu.make_async_remote_copy(src_ref=mine, dst_ref=mine, **sems),
                        pltpu.make_async_remote_copy(src_ref=theirs, dst_ref=theirs, **sems)))
    return out


def forward_start(bufs, tag):
    n = len(bufs)

    def body(*refs):
        ins = refs[:n]
        send, recv = refs[n], refs[n + 1]
        token = refs[-1]
        for start, _ in _forward_copies(ins, send, recv):
            start.start()
        token[...] = jnp.zeros_like(token)

    sems = pltpu.SemaphoreType.DMA((3 * n,))
    res = pl.pallas_call(
        body, name=f"forward_start_{tag}", in_specs=[HBM_SPEC] * n,
        out_specs=[SEM_SPEC, SEM_SPEC] + [HBM_SPEC] * n + [pl.BlockSpec(memory_space=pltpu.VMEM)],
        out_shape=[sems, sems] + [pltpu.HBM(b.shape, b.dtype) for b in bufs] + [jax.ShapeDtypeStruct((8, LANES), F32)],
        input_output_aliases={k: 2 + k for k in range(n)}, compiler_params=IN_FLIGHT,
    )(*[_in_hbm(b) for b in bufs])
    return res[0], res[1], res[2:2 + n], res[-1]


def forward_wait(send, recv, bufs, after, tag):
    n = len(bufs)

    def body(*refs):
        ins = refs[:n]
        send_ref, recv_ref = refs[n], refs[n + 1]
        for start, arrival in _forward_copies(ins, send_ref, recv_ref):
            start.wait_send()
            arrival.wait_recv()

    return pl.pallas_call(
        body, name=f"forward_wait_{tag}",
        in_specs=[HBM_SPEC] * n + [SEM_SPEC, SEM_SPEC, _any()], out_specs=[HBM_SPEC] * n,
        out_shape=[pltpu.HBM(b.shape, b.dtype) for b in bufs],
        input_output_aliases={k: k for k in range(n)}, compiler_params=IN_FLIGHT,
    )(*bufs, send, recv, after)


def _exchange_copies(srcs, lands, send, recv):
    x, y, c, _ = _place()
    return [pltpu.make_async_remote_copy(
        src_ref=src.at[:, _half(src.shape[1], 1 - c), :], dst_ref=land, send_sem=send.at[k], recv_sem=recv.at[k],
        device_id=(x, y, 1 - c), device_id_type=MESH) for k, (src, land) in enumerate(zip(srcs, lands))]


def exchange_start(srcs, tag):
    n = len(srcs)
    lands = [lax.empty((s.shape[0], s.shape[1] // 2, s.shape[2]), s.dtype) for s in srcs]

    def body(*refs):
        ins, land_refs = refs[:n], refs[n:2 * n]
        send, recv = refs[2 * n], refs[2 * n + 1]
        token = refs[-1]
        for cp in _exchange_copies(ins, land_refs, send, recv):
            cp.start()
        token[...] = jnp.zeros_like(token)

    sems = pltpu.SemaphoreType.DMA((n,))
    res = pl.pallas_call(
        body, name=f"exchange_start_{tag}",
        in_specs=[HBM_SPEC] * (2 * n),
        out_specs=[SEM_SPEC, SEM_SPEC] + [HBM_SPEC] * (2 * n) + [pl.BlockSpec(memory_space=pltpu.VMEM)],
        out_shape=[sems, sems] + [pltpu.HBM(a.shape, a.dtype) for a in list(srcs) + lands]
        + [jax.ShapeDtypeStruct((8, LANES), F32)],
        input_output_aliases={k: 2 + k for k in range(2 * n)}, compiler_params=IN_FLIGHT,
    )(*[_in_hbm(a) for a in list(srcs) + lands])
    return res[0], res[1], res[2:2 + n], res[2 + n:2 + 2 * n], res[-1]


def exchange_wait(send, recv, srcs, lands, after, tag):
    n = len(srcs)

    def body(*refs):
        ins, land_refs = refs[:n], refs[n:2 * n]
        send_ref, recv_ref = refs[2 * n], refs[2 * n + 1]
        for cp in _exchange_copies(ins, land_refs, send_ref, recv_ref):
            cp.wait_send()
            cp.wait_recv()

    res = pl.pallas_call(
        body, name=f"exchange_wait_{tag}",
        in_specs=[HBM_SPEC] * (2 * n) + [SEM_SPEC, SEM_SPEC, _any()], out_specs=[HBM_SPEC] * (2 * n),
        out_shape=[pltpu.HBM(a.shape, a.dtype) for a in list(srcs) + list(lands)],
        input_output_aliases={k: k for k in range(2 * n)}, compiler_params=IN_FLIGHT,
    )(*srcs, *lands, send, recv, after)
    return res[:n], res[n:]


def add_pair(gs, r1s, core):
    n = len(gs)

    def body(c_ref, *refs):
        del c_ref
        for g_ref, r_ref, o_ref in zip(refs[:n], refs[n:2 * n], refs[2 * n:]):
            o_ref[...] = (g_ref[...] + r_ref[...]).astype(BF16)

    blk = lambda r: (None,) + r.shape[1:]
    grid_spec = pltpu.PrefetchScalarGridSpec(
        num_scalar_prefetch=1, grid=(NCHIP,),
        in_specs=[pl.BlockSpec(blk(r), lambda s, c: (s, c[0], 0)) for r in r1s]
        + [pl.BlockSpec(blk(r), lambda s, c: (s, 0, 0)) for r in r1s],
        out_specs=[pl.BlockSpec(blk(r), lambda s, c: (s, 0, 0)) for r in r1s])
    return pl.pallas_call(body, grid_spec=grid_spec, out_shape=[jax.ShapeDtypeStruct(r.shape, BF16) for r in r1s],
                          compiler_params=_cp(("arbitrary",)), name="add_pair")(core, *gs, *r1s)


def _scatter_copies(srcs, lands, send, recv):
    _, _, c, chips = _place()
    out = []
    for k, (src, land) in enumerate(zip(srcs, lands)):
        for j, (cx, cy) in enumerate(chips):
            out.append(pltpu.make_async_remote_copy(
                src_ref=src.at[2 * cx + cy], dst_ref=land.at[j], send_sem=send.at[3 * k + j],
                recv_sem=recv.at[3 * k + j], device_id=(cx, cy, c), device_id_type=MESH))
    return out


def scatter_start(srcs, layer):
    n = len(srcs)
    srcs = list(srcs)
    lands = [lax.empty((3,) + s.shape[1:], s.dtype) for s in srcs]

    def body(*refs):
        ins, land_refs = refs[:n], refs[n:2 * n]
        send, recv = refs[2 * n], refs[2 * n + 1]
        token = refs[-1]
        for cp in _scatter_copies(ins, land_refs, send, recv):
            cp.start()
        token[...] = jnp.zeros_like(token)

    sems = pltpu.SemaphoreType.DMA((3 * n,))
    res = pl.pallas_call(
        body, name=f"scatter_start_{layer}",
        in_specs=[HBM_SPEC] * (2 * n),
        out_specs=[SEM_SPEC, SEM_SPEC] + [HBM_SPEC] * (2 * n) + [pl.BlockSpec(memory_space=pltpu.VMEM)],
        out_shape=[sems, sems] + [pltpu.HBM(a.shape, a.dtype) for a in srcs + lands]
        + [jax.ShapeDtypeStruct((8, LANES), F32)],
        input_output_aliases={k: 2 + k for k in range(2 * n)}, compiler_params=IN_FLIGHT,
    )(*[_in_hbm(a) for a in srcs + lands])
    return res[0], res[1], res[2:2 + n], res[2 + n:2 + 2 * n], res[-1]


def scatter_wait(send, recv, srcs, lands, after, layer):
    n = len(srcs)

    def body(*refs):
        ins, land_refs = refs[:n], refs[n:2 * n]
        send_ref, recv_ref = refs[2 * n], refs[2 * n + 1]
        for cp in _scatter_copies(ins, land_refs, send_ref, recv_ref):
            cp.wait_send()
            cp.wait_recv()

    res = pl.pallas_call(
        body, name=f"scatter_wait_{layer}",
        in_specs=[HBM_SPEC] * (2 * n) + [SEM_SPEC, SEM_SPEC, _any()], out_specs=[HBM_SPEC] * (2 * n),
        out_shape=[pltpu.HBM(a.shape, a.dtype) for a in list(srcs) + list(lands)],
        input_output_aliases={k: k for k in range(2 * n)}, compiler_params=IN_FLIGHT,
    )(*srcs, *lands, send, recv, after)
    return res[n:]


def add_chips(gs, r1s, r2s, place, totals, layer):
    n = len(gs)
    steps = 2

    def body(p_ref, *refs):
        del p_ref
        for g_ref, r1_ref, r2_ref, o_ref in zip(refs[:n], refs[n:2 * n], refs[2 * n:3 * n], refs[4 * n:]):
            own = g_ref[...] + r1_ref[...]
            o_ref[...] = ((own + r2_ref[0].astype(F32)) + r2_ref[1].astype(F32)) + r2_ref[2].astype(F32)

    blk = lambda r: (None, r.shape[1] // steps, r.shape[2])
    grid_spec = pltpu.PrefetchScalarGridSpec(
        num_scalar_prefetch=1, grid=(steps,),
        in_specs=[pl.BlockSpec(blk(r), lambda i, p: (p[1], p[0] * steps + i, 0)) for r in r1s]
        + [pl.BlockSpec(blk(r), lambda i, p: (p[1], i, 0)) for r in r1s]
        + [pl.BlockSpec((3,) + blk(r)[1:], lambda i, p: (0, i, 0)) for r in r1s] + [_any()] * n,
        out_specs=[pl.BlockSpec(blk(r), lambda i, p: (layer, p[0] * steps + i, 0)) for r in r1s])
    return pl.pallas_call(body, grid_spec=grid_spec, out_shape=[jax.ShapeDtypeStruct(t.shape, F32) for t in totals],
                          input_output_aliases={1 + 3 * n + k: k for k in range(n)},
                          compiler_params=_cp(("arbitrary",)), name="add_chips")(place, *gs, *r1s, *r2s, *totals)


def _share_copies(bufs, send, recv):
    x, y, c, _ = _place()
    out = []
    for k, buf in enumerate(bufs):
        sems = dict(send_sem=send.at[k], recv_sem=recv.at[k], device_id=(x, y, 1 - c), device_id_type=MESH)
        mine = buf.at[:, _half(buf.shape[1], c), :]
        theirs = buf.at[:, _half(buf.shape[1], 1 - c), :]
        out.append((pltpu.make_async_remote_copy(src_ref=mine, dst_ref=mine, **sems),
                    pltpu.make_async_remote_copy(src_ref=theirs, dst_ref=theirs, **sems)))
    return out


def share_start(bufs, tag):
    n = len(bufs)

    def body(*refs):
        ins = refs[:n]
        send, recv = refs[n], refs[n + 1]
        token = refs[-1]
        for start, _ in _share_copies(ins, send, recv):
            start.start()
        token[...] = jnp.zeros_like(token)

    sems = pltpu.SemaphoreType.DMA((n,))
    res = pl.pallas_call(
        body, name=f"share_start_{tag}", in_specs=[HBM_SPEC] * n,
        out_specs=[SEM_SPEC, SEM_SPEC] + [HBM_SPEC] * n + [pl.BlockSpec(memory_space=pltpu.VMEM)],
        out_shape=[sems, sems] + [pltpu.HBM(b.shape, b.dtype) for b in bufs] + [jax.ShapeDtypeStruct((8, LANES), F32)],
        input_output_aliases={k: 2 + k for k in range(n)}, compiler_params=IN_FLIGHT,
    )(*[_in_hbm(b) for b in bufs])
    return res[0], res[1], res[2:2 + n], res[-1]


def share_wait(send, recv, bufs, after, tag):
    n = len(bufs)

    def body(*refs):
        ins = refs[:n]
        send_ref, recv_ref = refs[n], refs[n + 1]
        for start, arrival in _share_copies(ins, send_ref, recv_ref):
            start.wait_send()
            arrival.wait_recv()

    return pl.pallas_call(
        body, name=f"share_wait_{tag}",
        in_specs=[HBM_SPEC] * n + [SEM_SPEC, SEM_SPEC, _any()], out_specs=[HBM_SPEC] * n,
        out_shape=[pltpu.HBM(b.shape, b.dtype) for b in bufs],
        input_output_aliases={k: k for k in range(n)}, compiler_params=IN_FLIGHT,
    )(*bufs, send, recv, after)


def small_allreduce(v, after=()):
    rows = v.shape[0]
    flips = [(fx, fy, fc) for fx in (0, 1) for fy in (0, 1) for fc in (0, 1)][1:]

    def body(v_ref, o_ref, buf, send, recv):
        x, y, c, _ = _place()
        buf[4 * x + 2 * y + c] = v_ref[...]
        peers = [(jnp.where(fx, 1 - x, x), jnp.where(fy, 1 - y, y), jnp.where(fc, 1 - c, c)) for fx, fy, fc in flips]
        cps = []
        for k, peer in enumerate(peers):
            cp = pltpu.make_async_remote_copy(
                src_ref=v_ref, dst_ref=buf.at[4 * x + 2 * y + c], send_sem=send.at[k], recv_sem=recv.at[k],
                device_id=peer, device_id_type=MESH)
            cp.start()
            cps.append(cp)
        for k, (px, py, pc) in enumerate(peers):
            pltpu.make_async_remote_copy(
                src_ref=v_ref, dst_ref=buf.at[4 * px + 2 * py + pc], send_sem=send.at[k], recv_sem=recv.at[k],
                device_id=(px, py, pc), device_id_type=MESH).wait_recv()
        for cp in cps:
            cp.wait_send()
        acc = buf[0]
        for s in range(1, 8):
            acc = acc + buf[s]
        o_ref[...] = acc

    vm = pl.BlockSpec(memory_space=pltpu.VMEM)
    return pl.pallas_call(
        _behind(body, 1, after), in_specs=[vm] + [_any()] * len(after), out_specs=vm,
        out_shape=jax.ShapeDtypeStruct((rows, SMALL_COLS), F32),
        scratch_shapes=[pltpu.VMEM((8, rows, SMALL_COLS), F32), pltpu.SemaphoreType.DMA((7,)),
                        pltpu.SemaphoreType.DMA((7,))],
        name="reduce_small")(v, *after)


def adamw(w, g, m, v, rb, name, after=()):
    nl, rows, cols = w.shape

    def body(w_ref, g_ref, m_ref, v_ref, go_ref, d_ref, nm_ref, nv_ref):
        gv = g_ref[...]
        go_ref[...] = gv
        nm = ADAM_B1 * m_ref[...] + (1.0 - ADAM_B1) * gv
        nv = ADAM_B2 * v_ref[...] + (1.0 - ADAM_B2) * (gv * gv)
        m_hat = nm / (1.0 - ADAM_B1 ** ADAM_STEP)
        v_hat = nv / (1.0 - ADAM_B2 ** ADAM_STEP)
        d_ref[...] = -ADAM_LR * (m_hat / (jnp.sqrt(v_hat) + ADAM_EPS) + ADAM_WD * w_ref[...])
        nm_ref[...] = nm
        nv_ref[...] = nv

    blk = pl.BlockSpec((None, rb, cols), lambda l, r: (l, r, 0))
    shp = jax.ShapeDtypeStruct(w.shape, F32)
    return pl.pallas_call(_behind(body, 4, after), grid=(nl, rows // rb), in_specs=[blk] * 4 + [_any()] * len(after),
                          out_specs=[blk] * 4, out_shape=[shp] * 4,
                          compiler_params=_cp(("arbitrary", "arbitrary")), name=name)(w, g, m, v, *after)


def _pack(parts, rows):
    flat = jnp.concatenate([p.reshape(-1).astype(F32) for p in parts])
    return jnp.pad(flat, (0, rows * SMALL_COLS - flat.shape[0])).reshape(rows, SMALL_COLS)


def _unpack(vec, shapes):
    flat = vec.reshape(-1)
    out, off = [], 0
    for s in shapes:
        size = 1
        for d in s:
            size *= d
        out.append(flat[off:off + size].reshape(s))
        off += size
    return out


def kernel(x, w_in, w_conv, rel_bias, g_conv_out, g_attn_out, w_out, g_pre_mix, g_post_mix, g_pre_ffn, g_post_ffn, w_ffn_in, w_ffn_out, loss_target, m_w_in, m_w_conv, m_rel_bias, m_g_conv_out, m_g_attn_out, m_w_out, m_g_pre_mix, m_g_post_mix, m_g_pre_ffn, m_g_post_ffn, m_w_ffn_in, m_w_ffn_out, v_w_in, v_w_conv, v_rel_bias, v_g_conv_out, v_g_attn_out, v_w_out, v_g_pre_mix, v_g_post_mix, v_g_pre_ffn, v_g_post_ffn, v_w_ffn_in, v_w_ffn_out):
    xi, yi, ci = lax.axis_index("x"), lax.axis_index("y"), lax.axis_index("c")
    chip = 2 * xi + yi
    nl = w_in.shape[0]
    x0 = x[0]
    target = loss_target[0]
    cwl = CW // NCHIP

    chip1 = chip.reshape(1).astype(jnp.int32)
    big_weights = [w_in, w_out, w_ffn_in, w_ffn_out]
    own = [cast_to_slot(big_weights, chip1, 0)]
    wc_mine = jnp.pad(w_conv.reshape(-1), (0, 16 * LANES - w_conv.size)).reshape(1, 16, LANES)
    wc_slot = lax.dynamic_update_slice_in_dim(jnp.zeros((NCHIP, 16, LANES), F32), wc_mine, chip, axis=0)
    gm = jnp.kron(jnp.eye(CW // HD, dtype=F32), jnp.full((HD, HD), 1.0 / HD, F32)).astype(BF16)
    row = lambda a, l: a[l][None, :]

    def gather_finish(flight, after, tag):
        send, recv, bufs, _ = flight
        return gather_forward(gather_wait(send, recv, bufs, after, tag))

    first_mix = gather_start(list(own[0][:2]) + [wc_slot], x0, "0m")
    first_ffn = gather_start(own[0][2:], first_mix[3], "0f")
    chain = first_ffn[3]
    biases = []
    for l in range(nl):
        biases.append(bias_expand(_diag_vector(rel_bias[l]), (QG_FWD, QG_BWD), [chain]))
        chain = biases[l][1]
    for l in range(1, nl):
        own.append(cast_to_slot(big_weights, chip1, l, [chain]))
        chain = own[l][0]
    gw_in, gw_out, wc_all = gather_finish(first_mix, chain, "0m")
    wc_full = wc_all.reshape(NCHIP, -1)[:, :nl * cwl * 3].reshape(NCHIP, nl, cwl, 3)
    wc_full = jnp.transpose(wc_full, (1, 0, 2, 3)).reshape(nl, CW, 3)
    wconv_t = jnp.pad(jnp.transpose(wc_full, (0, 2, 1)), ((0, 0), (0, 5), (0, 0)))
    flights, to_sibling = {}, None
    saved, weights = [], []
    h = x0
    for l in range(nl):
        if l == 0:
            pass
        elif l == 1:
            flights[2] = gather_start(own[2], h, 2)
            gw_in, gw_out, gw_fi, gw_fo = gather_finish(flights[l], flights[2][3], l)
        else:
            gw_in, gw_out, gw_fi, gw_fo = forward_wait(*to_sibling[:3], h, l)
        gw_out = gw_out.reshape(D, D)
        behind_mix, behind_ffn = ([first_ffn[3]] if l == 0 else []), []
        if l + 1 < nl and l + 1 not in flights:
            flights[l + 1] = gather_start(own[l + 1], first_ffn[3] if l == 0 else gw_in, l + 1)
            behind_mix.append(flights[l + 1][3])
        bias2, bias2_bwd = biases[l]
        proj = fwd_inproj(h, row(g_pre_mix, l), gw_in, behind_mix)
        xmid, o, lse, y, z = fwd_mix(h, proj, bias2, wconv_t[l], row(g_conv_out, l), row(g_attn_out, l),
                                     row(g_post_mix, l), gm, gw_out)
        if l == 0:
            gw_fi, gw_fo = gather_finish(first_ffn, xmid, "0f")
        elif l + 1 < nl:
            send, recv, bufs, _ = flights[l + 1]
            landed = gather_wait(send, recv, bufs, xmid, l + 1)
            to_sibling = forward_start(landed, l + 1)
            behind_ffn.append(to_sibling[3])
            if l + 2 < nl:
                flights[l + 2] = gather_start(own[l + 2], to_sibling[3], l + 2)
                behind_ffn.append(flights[l + 2][3])
        gw_fo = gw_fo.reshape(2, DFF // 2, D)
        ffn = fwd_ffn(xmid, row(g_pre_ffn, l), row(g_post_ffn, l), gw_fi, gw_fo, behind_ffn,
                      target if l == nl - 1 else None)
        gu, f = ffn[:2]
        saved.append((h, proj, bias2_bwd, xmid, o, lse, y, z, gu, f))
        weights.append((gw_in, gw_out, gw_fi, gw_fo))
        h = ffn[2]
    dx, loss_blk = ffn[2], ffn[3]

    core = ci.reshape(1).astype(jnp.int32)
    place = jnp.stack([ci, chip]).astype(jnp.int32)
    totals = [lax.empty(w.shape, F32) for w in (w_in, w_out, w_ffn_in, w_ffn_out)]
    small = {k: [None] * nl for k in ("co", "ao", "pm", "qm", "pf", "qf", "rel", "wc")}

    def reduce_begin(kinds, grads, tag):
        return kinds, exchange_start(grads, tag), tag

    def reduce_mid(state, after):
        kinds, (send, recv, srcs, lands, _), tag = state
        grads, from_sibling = exchange_wait(send, recv, srcs, lands, after, tag)
        return kinds, grads, from_sibling, scatter_start(add_pair(grads, from_sibling, core), tag), tag

    def reduce_end(state, after, totals, layer):
        kinds, grads, from_sibling, (send, recv, srcs, lands, _), tag = state
        from_chips = scatter_wait(send, recv, srcs, lands, after, tag)
        totals = list(totals)
        summed = add_chips(grads, from_sibling, from_chips, place, [totals[i] for i in kinds], layer)
        for i, t in zip(kinds, summed):
            totals[i] = t
        return totals

    begun = flying = None
    for l in reversed(range(nl)):
        hin, proj, bias2, xmid, o, lse, y, z, gu, f = saved[l]
        gw_in, gw_out, gw_fi, gw_fo = weights[l]
        behind_ffn = [begun[1][4]] if begun is not None else []
        dxm, dfb, act, dgu, h2, dg_qf, dg_pf = bwd_ffn(dx, f, xmid, gu, row(g_pre_ffn, l), row(g_post_ffn, l),
                                                        gw_fi, gw_fo, behind_ffn)
        behind_mix, behind_conv = [], []
        if begun is not None:
            flying = reduce_mid(begun, dxm)
            behind_mix.append(flying[3][4])
        gr_fo = wgrad(act, dfb, 256, D, False, "wgrad_ffn_out").reshape(NCHIP, DFF // NCHIP, D)
        gr_fi = wgrad(h2, dgu, 512, 2 * DFF // NCHIP, True, "wgrad_ffn_in")
        if l == 0:
            begun_ffn = reduce_begin([2, 3], [gr_fi, gr_fo], "0f")
            behind_mix.append(begun_ffn[1][4])
        gr_out, do, dco, dbg, dg_qm, dg_co, dg_ao = bwd_mix(dxm, z, o, y, proj, wconv_t[l], row(g_conv_out, l),
                                                             row(g_attn_out, l), row(g_post_mix, l), gm, gw_out,
                                                             behind_mix)
        gr_out = gr_out.reshape(NCHIP, D // NCHIP, D)
        if l == 0:
            flying_ffn = reduce_mid(begun_ffn, do)
            behind_conv.append(flying_ffn[3][4])
        dhc, dcg, dwc = bwd_conv(dco, proj, wconv_t[l], behind_conv)
        dq, dk, dv, db2 = bwd_attn(proj, o, do, lse, bias2)
        dx, gr_in, dg_pm = bwd_inproj(dxm, hin, dhc, dbg, dcg, dq, dk, dv, row(g_pre_mix, l), gw_in)
        if flying is not None:
            totals = reduce_end(flying, dx, totals, l + 1)
        small["co"][l], small["ao"][l], small["pm"][l], small["qm"][l] = dg_co, dg_ao, dg_pm, dg_qm
        small["pf"][l], small["qf"][l] = dg_pf, dg_qf
        small["rel"][l] = _diag_vector_bwd(bias_reduce(db2.reshape(NH, QG_BWD, QG_BWD + LEFT)))
        small["wc"][l] = jnp.transpose(dwc[0:3], (1, 0))
        if l > 0:
            begun = reduce_begin([0, 1, 2, 3], [gr_in, gr_out, gr_fi, gr_fo], l)
    begun_mix = reduce_begin([0, 1], [gr_in, gr_out], "0m")
    totals = reduce_end(flying_ffn, begun_mix[1][4], totals, 0)
    flying_mix = reduce_mid(begun_mix, totals[2])
    share_ffn = share_start(totals[2:], "ffn")

    order = ("co", "ao", "pm", "qm", "pf", "qf", "rel", "wc")
    parts = [jnp.stack(small[k]) for k in order] + [loss_blk[0:1, 0:1]]
    shapes = [p.shape for p in parts]
    red_vec = small_allreduce(_pack(parts, 40), [share_ffn[3], flying_mix[3][4]])
    red = _unpack(red_vec, shapes)

    gr_fi, gr_fo = share_wait(*share_ffn[:3], red_vec, "ffn")
    big_fi = adamw(w_ffn_in, gr_fi, m_w_ffn_in, v_w_ffn_in, w_ffn_in.shape[1] // 4, "adamw_ffn_in")
    totals = reduce_end(flying_mix, big_fi[1], totals, 0)
    share_mix = share_start(totals[:2], "mix")
    big_fo = adamw(w_ffn_out, gr_fo, m_w_ffn_out, v_w_ffn_out, w_ffn_out.shape[1] // 4, "adamw_ffn_out",
                   [share_mix[3]])
    gr_in, gr_out = share_wait(*share_mix[:3], big_fo[1], "mix")
    big_in = adamw(w_in, gr_in, m_w_in, v_w_in, w_in.shape[1] // 4, "adamw_in")
    big_out = adamw(w_out, gr_out, m_w_out, v_w_out, w_out.shape[1] // 4, "adamw_out")
    big = [big_in, big_out, big_fi, big_fo]
    gr_co, gr_ao, gr_pm, gr_qm, gr_pf, gr_qf, gr_rel, gr_wc_full, loss = red
    gr_co, gr_ao, gr_pm, gr_qm, gr_pf, gr_qf = [a.reshape(nl, -1) for a in (gr_co, gr_ao, gr_pm, gr_qm, gr_pf, gr_qf)]
    gr_wc = lax.dynamic_slice_in_dim(gr_wc_full, chip * cwl, cwl, axis=1)
    loss = loss.reshape(())

    sw = [g_conv_out, g_attn_out, g_pre_mix, g_post_mix, g_pre_ffn, g_post_ffn, rel_bias, w_conv]
    sg = [gr_co, gr_ao, gr_pm, gr_qm, gr_pf, gr_qf, gr_rel, gr_wc]
    sm = [m_g_conv_out, m_g_attn_out, m_g_pre_mix, m_g_post_mix, m_g_pre_ffn, m_g_post_ffn, m_rel_bias, m_w_conv]
    sv = [v_g_conv_out, v_g_attn_out, v_g_pre_mix, v_g_post_mix, v_g_pre_ffn, v_g_post_ffn, v_rel_bias, v_w_conv]
    sshapes = [a.shape for a in sw]
    packed = [_pack(a, 32)[None] for a in (sw, sg, sm, sv)]
    s_out = [_unpack(a[0], sshapes) for a in adamw(*packed, 32, "adamw_small")]

    def leaves(big_i, small_i):
        b_in, b_out, b_fi, b_fo = big_i
        s_co, s_ao, s_pm, s_qm, s_pf, s_qf, s_rel, s_wc = small_i
        return [b_in, s_wc, s_rel, s_co, s_ao, b_out, s_pm, s_qm, s_pf, s_qf, b_fi, b_fo]

    out = [loss, dx[None]]
    out += leaves([b[0] for b in big], sg)
    for i in range(1, 4):
        out += leaves([b[i] for b in big], s_out[i])
    return tuple(out)
```

```python
import jax
import jax.numpy as jnp
from jax import lax
from jax.experimental import pallas as pl
from jax.experimental.pallas import tpu as pltpu

F32 = jnp.float32
BF16 = jnp.bfloat16

D = 1024
PROJ = 3072
CW = 512
HD = 64
NH = 8
CHUNK = 64
BAND = 576
REL_CLIP = 128
NREL = 2 * REL_CLIP + 1
DFF = 2816
DEPTH = 4
NCHIP = 4
EPS = 1e-6
NEG_INF = -1e30

ADAM_LR = 0.001
ADAM_B1 = 0.9
ADAM_B2 = 0.999
ADAM_EPS = 1e-08
ADAM_WD = 0.01
ADAM_STEP = 10

V7X_VMEM_BYTES = 64 * 1024 * 1024
VMEM_LIMIT = V7X_VMEM_BYTES - 8 * 1024 * 1024
LANES = 128
QG_FWD = 4 * CHUNK
QG_BWD = 2 * CHUNK
LEFT = BAND - CHUNK
TQ = 512
TM = 256
SMALL_COLS = 1024
MESH = pl.DeviceIdType.MESH
NT = (((1,), (1,)), ((), ()))
TN = (((0,), (0,)), ((), ()))


def _cp(sem=None, vmem=VMEM_LIMIT):
    return pltpu.CompilerParams(dimension_semantics=sem, vmem_limit_bytes=vmem)


def _any():
    return pl.BlockSpec(memory_space=pl.ANY)


def _const(shape):
    nd = len(shape)
    return pl.BlockSpec(shape, lambda *_: (0,) * nd)


def _behind(body, n_in, after):
    def ordered(*refs):
        return body(*refs[:n_in], *refs[n_in + len(after):])
    return ordered


def _rms(v, g):
    r = lax.rsqrt(jnp.mean(v * v, axis=-1, keepdims=True) + EPS)
    return v * r * g


def _rms_bwd(dy, v, g):
    r = lax.rsqrt(jnp.mean(v * v, axis=-1, keepdims=True) + EPS)
    vh = v * r
    dg = jnp.sum(dy * vh, axis=0, keepdims=True)
    dvh = dy * g
    dv = r * (dvh - vh * jnp.mean(dvh * vh, axis=-1, keepdims=True))
    return dv, dg


def _group_mean(v, gm):
    return jnp.dot(v.astype(BF16), gm, preferred_element_type=F32)


def _group_rms_bwd(dy, v, g, gm):
    r = lax.rsqrt(_group_mean(v * v, gm) + EPS)
    vh = v * r
    dg = jnp.sum(dy * vh, axis=0, keepdims=True)
    dvh = dy * g
    dv = r * (dvh - vh * _group_mean(dvh * vh, gm))
    return dv, dg


def _head_masks(scale):
    lane = lax.broadcasted_iota(jnp.int32, (1, LANES), 1)
    return [jnp.where((lane >= HD * a) & (lane < HD * (a + 1)), scale, 0.0).astype(BF16) for a in range(2)]


class _Resident:
    def __init__(self, src, dst, sem):
        self.first = pl.program_id(0) == 0
        self.copy = pltpu.make_async_copy(src, dst, sem)
        self.dst = dst

        @pl.when(self.first)
        def _():
            self.copy.start()

    def read(self):
        @pl.when(self.first)
        def _():
            self.copy.wait()

        return self.dst[...]


FF_CHUNKS = ((0, 1536), (1536, DFF))


def _stream_ffn_weights(wfi_hbm, wfo_hbm, wfi_v, wfo_v, sems, order, step):
    hw = DFF // 2
    per_matrix = {
        0: [(wfi_hbm.at[j], wfi_v.at[0, :, pl.ds(hw * j, hw)]) for j in range(2)],
        1: [(wfi_hbm.at[2 + j], wfi_v.at[1, :, pl.ds(hw * j, hw)]) for j in range(2)],
        2: [(wfo_hbm.at[j], wfo_v.at[pl.ds(hw * j, hw), :]) for j in range(2)],
    }
    pieces = [p for m in order for p in per_matrix[m]]
    slot = {m: 2 * k for k, m in enumerate(order)}

    def make_step(wait):
        def ready(m, chunk):
            if chunk == 0:
                wait(slot[m])
                wait(slot[m] + 1)
        return lambda: step(ready)

    copies = [pltpu.make_async_copy(src, dst, sems.at[k]) for k, (src, dst) in enumerate(pieces)]
    first = pl.program_id(0) == 0

    @pl.when(first)
    def _():
        for cp in copies:
            cp.start()
        make_step(lambda k: copies[k].wait())()

    @pl.when(jnp.logical_not(first))
    def _():
        make_step(lambda k: None)()


def _conv_taps(u_prev, u, scr):
    n = u.shape[0]
    scr[0:16, :] = u_prev
    scr[16:16 + n, :] = u
    return scr[15:15 + n, :], scr[14:14 + n, :]


def fwd_inproj(x, g, w_all, after=()):
    t = x.shape[0]
    wc = PROJ // NCHIP

    def body(x_ref, g_ref, w_hbm, o_ref, w_v):
        @pl.when(pl.program_id(0) == 0)
        def _():
            pltpu.sync_copy(w_hbm, w_v)

        h = _rms(x_ref[...], g_ref[...]).astype(BF16)
        for b in range(NCHIP):
            o_ref[:, wc * b:wc * (b + 1)] = jnp.dot(h, w_v[b], preferred_element_type=F32).astype(BF16)

    return pl.pallas_call(
        _behind(body, 3, after), grid=(t // TQ,),
        in_specs=[pl.BlockSpec((TQ, D), lambda i: (i, 0)), _const((1, D)), _any()] + [_any()] * len(after),
        out_specs=pl.BlockSpec((TQ, PROJ), lambda i: (i, 0)),
        out_shape=jax.ShapeDtypeStruct((t, PROJ), BF16),
        scratch_shapes=[pltpu.VMEM((NCHIP, D, wc), BF16)],
        compiler_params=_cp(("arbitrary",)), name="fwd_inproj")(x, g, w_all, *after)


def _attn_window_specs():
    return [
        pl.BlockSpec((TQ, CW), lambda i: (i, 3)),
        pl.BlockSpec((TQ, CW), lambda i: (jnp.maximum(i - 1, 0), 4)),
        pl.BlockSpec((TQ, CW), lambda i: (i, 4)),
        pl.BlockSpec((TQ, CW), lambda i: (jnp.maximum(i - 1, 0), 5)),
        pl.BlockSpec((TQ, CW), lambda i: (i, 5)),
    ]


def _conv_specs():
    return [
        pl.BlockSpec((TQ, 3 * CW), lambda i: (i, 0)),
        pl.BlockSpec((16, 3 * CW), lambda i: (jnp.maximum(i * (TQ // 16) - 1, 0), 0)),
    ]


def _conv_fwd(pc_ref, pcp_ref, wc_ref, scr, first):
    pc = pc_ref[...].astype(F32)
    hc, bg, cg = pc[:, :CW], pc[:, CW:2 * CW], pc[:, 2 * CW:]
    u = cg * hc
    pp = pcp_ref[...].astype(F32)
    u_prev = jnp.where(first, 0.0, pp[:, 2 * CW:] * pp[:, :CW])
    u1, u2 = _conv_taps(u_prev, u, scr)
    cout = wc_ref[0:1, :] * u2 + wc_ref[1:2, :] * u1 + wc_ref[2:3, :] * u
    return hc, bg, cg, u, u1, u2, cout


def _key_penalty(first, r0, kg):
    col = lax.broadcasted_iota(jnp.int32, (1, kg), 1)
    limit = jnp.where(first, TQ - r0, 0)
    return jnp.where(col < limit, NEG_INF, 0.0)


def fwd_mix(x, proj, bias2, wconv_t, g_co, g_ao, g_pm, gm, wout_all):
    t = x.shape[0]
    qg, kg = QG_FWD, QG_FWD + LEFT

    def body(x_ref, pc_ref, pcp_ref, q_ref, kp_ref, kc_ref, vp_ref, vc_ref, b2_ref, wc_ref, gco_ref, gao_ref, gpm_ref,
             gm_ref, wout_hbm, xmid_ref, o_ref, lse_ref, y_ref, z_ref, wout_v, kwin, vwin, cscr, sems):
        i = pl.program_id(0)
        first = i == 0
        wout = _Resident(wout_hbm, wout_v, sems.at[0])
        kwin[0:TQ, :] = kp_ref[...]
        kwin[TQ:2 * TQ, :] = kc_ref[...]
        vwin[0:TQ, :] = vp_ref[...]
        vwin[TQ:2 * TQ, :] = vc_ref[...]
        qmask = _head_masks(HD ** -0.5)
        low = lax.broadcasted_iota(jnp.int32, (1, LANES), 1) < HD

        def group(g, carry):
            r0 = pl.multiple_of(g * qg, qg)
            pen = _key_penalty(first, r0, kg)
            for hp in range(NH // 2):
                ls = slice(LANES * hp, LANES * (hp + 1))
                qb = q_ref[pl.ds(r0, qg), ls]
                q2 = jnp.concatenate([qb * qmask[0], qb * qmask[1]], axis=0)
                s = lax.dot_general(q2, kwin[pl.ds(r0, kg), ls], NT, preferred_element_type=F32)
                s = s + b2_ref[hp] + pen
                m = jnp.max(s, axis=-1, keepdims=True)
                p = jnp.exp(s - m)
                l = jnp.sum(p, axis=-1, keepdims=True)
                o2 = jnp.dot(p.astype(BF16), vwin[pl.ds(r0, kg), ls], preferred_element_type=F32) * (1.0 / l)
                lse2 = m + jnp.log(l)
                o_ref[pl.ds(r0, qg), ls] = jnp.where(low, o2[:qg], o2[qg:])
                lse_ref[pl.ds(r0, qg), ls] = jnp.where(low, lse2[:qg], lse2[qg:])
            return carry

        lax.fori_loop(0, TQ // qg, group, 0)

        _, bg, _, _, _, _, cout = _conv_fwd(pc_ref, pcp_ref, wc_ref, cscr, first)
        yc = bg * cout
        gmv = gm_ref[...]
        ycn = yc * lax.rsqrt(_group_mean(yc * yc, gmv) + EPS) * gco_ref[...]
        oa = o_ref[...]
        oan = oa * lax.rsqrt(_group_mean(oa * oa, gmv) + EPS) * gao_ref[...]
        y_ref[:, 0:CW] = ycn.astype(BF16)
        y_ref[:, CW:2 * CW] = oan.astype(BF16)
        z = jnp.dot(y_ref[...], wout.read(), preferred_element_type=F32)
        z_ref[...] = z
        xmid_ref[...] = x_ref[...] + _rms(z, gpm_ref[...])

    row = lambda w: pl.BlockSpec((TQ, w), lambda i: (i, 0))
    return pl.pallas_call(
        body, grid=(t // TQ,),
        in_specs=[row(D)] + _conv_specs() + _attn_window_specs() + [
            _const((NH // 2, 2 * qg, kg)), _const((8, CW)), _const((1, CW)), _const((1, CW)), _const((1, D)),
            _const((CW, CW)), _any()],
        out_specs=[row(D), row(CW), row(CW), row(D), row(D)],
        out_shape=[jax.ShapeDtypeStruct((t, D), F32), jax.ShapeDtypeStruct((t, CW), F32),
                   jax.ShapeDtypeStruct((t, CW), F32), jax.ShapeDtypeStruct((t, D), BF16),
                   jax.ShapeDtypeStruct((t, D), F32)],
        scratch_shapes=[pltpu.VMEM((D, D), BF16), pltpu.VMEM((2 * TQ, CW), BF16), pltpu.VMEM((2 * TQ, CW), BF16),
                        pltpu.VMEM((TQ + 16, CW), F32), pltpu.SemaphoreType.DMA((1,))],
        compiler_params=_cp(("arbitrary",)), name="fwd_mix",
    )(x, proj, proj, proj, proj, proj, proj, proj, bias2, wconv_t, g_co, g_ao, g_pm, gm, wout_all)


def fwd_ffn(xmid, g_pre, g_post, wfi_all, wfo_all, after=(), target=None):
    t = xmid.shape[0]
    n_in = 5 if target is None else 6

    def body(*refs):
        x_ref, gpre_ref, gpost_ref, wfi_hbm, wfo_hbm = refs[:5]
        t_ref = None if target is None else refs[5]
        gu_ref, f_ref, xo_ref = refs[n_in:n_in + 3]
        l_ref = None if target is None else refs[n_in + 3]
        wfi_v, wfo_v, sems = refs[-3:]

        if target is not None:
            @pl.when(pl.program_id(0) == 0)
            def _():
                l_ref[...] = jnp.zeros_like(l_ref)

        def step(ready):
            xv = x_ref[...]
            h = _rms(xv, gpre_ref[...]).astype(BF16)
            f = jnp.zeros((TM, D), F32)
            for ci, (a, b) in enumerate(FF_CHUNKS):
                ready(0, ci)
                gate = jnp.dot(h, wfi_v[0, :, a:b], preferred_element_type=F32)
                ready(1, ci)
                up = jnp.dot(h, wfi_v[1, :, a:b], preferred_element_type=F32)
                gu_ref[:, a:b] = gate.astype(BF16)
                gu_ref[:, DFF + a:DFF + b] = up.astype(BF16)
                act = gate * (1.0 / (1.0 + jnp.exp(-gate))) * up
                ready(2, ci)
                f = f + jnp.dot(act.astype(BF16), wfo_v[a:b, :], preferred_element_type=F32)
            f_ref[...] = f
            xo = xv + _rms(f, gpost_ref[...])
            if target is None:
                xo_ref[...] = xo
            else:
                e = xo - t_ref[...]
                xo_ref[...] = e * (1.0 / D)
                rows = jnp.sum(e * e, axis=-1, keepdims=True) * (1.0 / D)
                l_ref[...] += 0.5 * jnp.sum(rows, axis=0, keepdims=True)

        _stream_ffn_weights(wfi_hbm, wfo_hbm, wfi_v, wfo_v, sems, (0, 1, 2), step)

    row = lambda w: pl.BlockSpec((TM, w), lambda i: (i, 0))
    with_loss = target is not None
    return pl.pallas_call(
        _behind(body, n_in, after), grid=(t // TM,),
        in_specs=[row(D), _const((1, D)), _const((1, D)), _any(), _any()] + [row(D)] * with_loss
        + [_any()] * len(after),
        out_specs=[row(2 * DFF), row(D), row(D)] + [_const((8, LANES))] * with_loss,
        out_shape=[jax.ShapeDtypeStruct((t, 2 * DFF), BF16), jax.ShapeDtypeStruct((t, D), F32),
                   jax.ShapeDtypeStruct((t, D), F32)] + [jax.ShapeDtypeStruct((8, LANES), F32)] * with_loss,
        scratch_shapes=[pltpu.VMEM((2, D, DFF), BF16), pltpu.VMEM((DFF, D), BF16), pltpu.SemaphoreType.DMA((6,))],
        compiler_params=_cp(("arbitrary",)), name="fwd_ffn_loss" if with_loss else "fwd_ffn",
    )(xmid, g_pre, g_post, wfi_all, wfo_all, *([target] * with_loss), *after)


def bwd_ffn(dx, f, xmid, gu, g_pre, g_post, wfi_all, wfo_all, after=()):
    t = dx.shape[0]
    nt = t // TM

    def body(dx_ref, f_ref, x_ref, gu_ref, gpre_ref, gpost_ref, wfi_hbm, wfo_hbm,
             dxm_ref, dwo_hbm, dgu_ref, h_ref, dgpost_ref, dgpre_ref, wfi_v, wfo_v, dwo_acc, sems):
        @pl.when(pl.program_id(0) == 0)
        def _():
            dgpost_ref[...] = jnp.zeros_like(dgpost_ref)
            dgpre_ref[...] = jnp.zeros_like(dgpre_ref)
            dwo_acc[...] = jnp.zeros_like(dwo_acc)

        def step(ready):
            dxo = dx_ref[...]
            df, dgp = _rms_bwd(dxo, f_ref[...], gpost_ref[...])
            dgpost_ref[...] += dgp
            dfb = df.astype(BF16)
            dh = jnp.zeros((TM, D), F32)
            for ci, (a, b) in enumerate(FF_CHUNKS):
                ready(2, ci)
                dact = lax.dot_general(dfb, wfo_v[a:b, :], NT, preferred_element_type=F32)
                gate = gu_ref[:, a:b].astype(F32)
                up = gu_ref[:, DFF + a:DFF + b].astype(F32)
                sig = 1.0 / (1.0 + jnp.exp(-gate))
                silu = gate * sig
                act = (silu * up).astype(BF16)
                dwo_acc[a:b, :] += lax.dot_general(act, dfb, TN, preferred_element_type=F32)
                dup = (dact * silu).astype(BF16)
                dgate = (dact * up * (sig * (1.0 + gate * (1.0 - sig)))).astype(BF16)
                dgu_ref[:, a:b] = dgate
                dgu_ref[:, DFF + a:DFF + b] = dup
                ready(0, ci)
                dh = dh + lax.dot_general(dgate, wfi_v[0, :, a:b], NT, preferred_element_type=F32)
                ready(1, ci)
                dh = dh + lax.dot_general(dup, wfi_v[1, :, a:b], NT, preferred_element_type=F32)
            xv = x_ref[...]
            gpre = gpre_ref[...]
            h_ref[...] = _rms(xv, gpre).astype(BF16)
            dxv, dgq = _rms_bwd(dh, xv, gpre)
            dgpre_ref[...] += dgq
            dxm_ref[...] = dxo + dxv

        _stream_ffn_weights(wfi_hbm, wfo_hbm, wfi_v, wfo_v, sems, (2, 0, 1), step)

        @pl.when(pl.program_id(0) == nt - 1)
        def _():
            pltpu.sync_copy(dwo_acc, dwo_hbm)

    row = lambda w: pl.BlockSpec((TM, w), lambda i: (i, 0))
    return pl.pallas_call(
        _behind(body, 8, after), grid=(nt,),
        in_specs=[row(D), row(D), row(D), row(2 * DFF), _const((1, D)), _const((1, D)), _any(), _any()]
        + [_any()] * len(after),
        out_specs=[row(D), _any(), row(2 * DFF), row(D), _const((1, D)), _const((1, D))],
        out_shape=[jax.ShapeDtypeStruct((t, D), F32), jax.ShapeDtypeStruct((DFF, D), F32),
                   jax.ShapeDtypeStruct((t, 2 * DFF), BF16), jax.ShapeDtypeStruct((t, D), BF16),
                   jax.ShapeDtypeStruct((1, D), F32), jax.ShapeDtypeStruct((1, D), F32)],
        scratch_shapes=[pltpu.VMEM((2, D, DFF), BF16), pltpu.VMEM((DFF, D), BF16), pltpu.VMEM((DFF, D), F32),
                        pltpu.SemaphoreType.DMA((6,))],
        compiler_params=_cp(("arbitrary",)), name="bwd_ffn")(dx, f, xmid, gu, g_pre, g_post, wfi_all, wfo_all, *after)


def bwd_mix(dxm, z, o, y, proj, wconv_t, g_co, g_ao, g_pm, gm, wout_all, after=()):
    t = dxm.shape[0]

    def body(dx_ref, z_ref, o_ref, y_ref, pc_ref, pcp_ref, wc_ref, gco_ref, gao_ref, gpm_ref, gm_ref, wout_hbm,
             dwo_ref, do_ref, dco_ref, dbg_ref, dgpm_ref, dgco_ref, dgao_ref, wout_v, cscr):
        first = pl.program_id(0) == 0

        @pl.when(first)
        def _():
            pltpu.sync_copy(wout_hbm, wout_v)
            dwo_ref[...] = jnp.zeros_like(dwo_ref)
            dgpm_ref[...] = jnp.zeros_like(dgpm_ref)
            dgco_ref[...] = jnp.zeros_like(dgco_ref)
            dgao_ref[...] = jnp.zeros_like(dgao_ref)

        dz, dgp = _rms_bwd(dx_ref[...], z_ref[...], gpm_ref[...])
        dgpm_ref[...] += dgp
        dzb = dz.astype(BF16)
        dwo_ref[...] += lax.dot_general(y_ref[...], dzb, TN, preferred_element_type=F32)
        gmv = gm_ref[...]
        _, bg, _, _, _, _, cout = _conv_fwd(pc_ref, pcp_ref, wc_ref, cscr, first)
        dy_conv = lax.dot_general(dzb, wout_v[0:CW, :], NT, preferred_element_type=F32)
        dyc, dgc = _group_rms_bwd(dy_conv, bg * cout, gco_ref[...], gmv)
        dgco_ref[...] += dgc
        dbg_ref[...] = (dyc * cout).astype(BF16)
        dco_ref[...] = dyc * bg
        dy_attn = lax.dot_general(dzb, wout_v[CW:2 * CW, :], NT, preferred_element_type=F32)
        do, dga = _group_rms_bwd(dy_attn, o_ref[...], gao_ref[...], gmv)
        dgao_ref[...] += dga
        do_ref[...] = do.astype(BF16)

    row = lambda w: pl.BlockSpec((TQ, w), lambda i: (i, 0))
    return pl.pallas_call(
        _behind(body, 12, after), grid=(t // TQ,),
        in_specs=[row(D), row(D), row(CW), row(D)] + _conv_specs() + [
            _const((8, CW)), _const((1, CW)), _const((1, CW)), _const((1, D)), _const((CW, CW)), _any()]
        + [_any()] * len(after),
        out_specs=[_const((D, D)), row(CW), row(CW), row(CW), _const((1, D)), _const((1, CW)), _const((1, CW))],
        out_shape=[jax.ShapeDtypeStruct((D, D), F32), jax.ShapeDtypeStruct((t, CW), BF16),
                   jax.ShapeDtypeStruct((t, CW), F32), jax.ShapeDtypeStruct((t, CW), BF16),
                   jax.ShapeDtypeStruct((1, D), F32), jax.ShapeDtypeStruct((1, CW), F32),
                   jax.ShapeDtypeStruct((1, CW), F32)],
        scratch_shapes=[pltpu.VMEM((D, D), BF16), pltpu.VMEM((TQ + 16, CW), F32)],
        compiler_params=_cp(("arbitrary",)), name="bwd_mix",
    )(dxm, z, o, y, proj, proj, wconv_t, g_co, g_ao, g_pm, gm, wout_all, *after)


def bwd_conv(dco, proj, wconv_t, after=()):
    t = dco.shape[0]
    nt = t // TQ

    def body(d_ref, dn_ref, pc_ref, pcp_ref, wc_ref, dhc_ref, dcg_ref, dw_ref, cscr, dscr):
        i = pl.program_id(0)
        first = i == 0

        @pl.when(first)
        def _():
            dw_ref[...] = jnp.zeros_like(dw_ref)

        hc, _, cg, u, u1, u2, _ = _conv_fwd(pc_ref, pcp_ref, wc_ref, cscr, first)
        d0 = d_ref[...]
        dscr[0:TQ, :] = d0
        dscr[TQ:TQ + 8, :] = jnp.where(i == nt - 1, 0.0, dn_ref[...])
        d1 = dscr[1:TQ + 1, :]
        d2 = dscr[2:TQ + 2, :]
        du = wc_ref[2:3, :] * d0 + wc_ref[1:2, :] * d1 + wc_ref[0:1, :] * d2
        dhc_ref[...] = (du * cg).astype(BF16)
        dcg_ref[...] = (du * hc).astype(BF16)
        dw_ref[0:1, :] += jnp.sum(d0 * u2, axis=0, keepdims=True)
        dw_ref[1:2, :] += jnp.sum(d0 * u1, axis=0, keepdims=True)
        dw_ref[2:3, :] += jnp.sum(d0 * u, axis=0, keepdims=True)

    row = lambda w: pl.BlockSpec((TQ, w), lambda i: (i, 0))
    nxt = pl.BlockSpec((8, CW), lambda i: (jnp.minimum((i + 1) * (TQ // 8), t // 8 - 1), 0))
    return pl.pallas_call(
        _behind(body, 5, after), grid=(nt,),
        in_specs=[row(CW), nxt] + _conv_specs() + [_const((8, CW))] + [_any()] * len(after),
        out_specs=[row(CW), row(CW), _const((8, CW))],
        out_shape=[jax.ShapeDtypeStruct((t, CW), BF16), jax.ShapeDtypeStruct((t, CW), BF16),
                   jax.ShapeDtypeStruct((8, CW), F32)],
        scratch_shapes=[pltpu.VMEM((TQ + 16, CW), F32), pltpu.VMEM((TQ + 8, CW), F32)],
        compiler_params=_cp(("arbitrary",)), name="bwd_conv")(dco, dco, proj, proj, wconv_t, *after)


def bwd_attn(proj, o, do, lse, bias2):
    t = o.shape[0]
    nt = t // TQ
    qg, kg = QG_BWD, QG_BWD + LEFT
    nkb = (t + TQ) // LANES

    def body(q_ref, kp_ref, kc_ref, vp_ref, vc_ref, o_ref, do_ref, lse_ref, b2_ref,
             dq_ref, dk_hbm, dv_hbm, db_hbm, kwin, vwin, dk_acc, dv_acc, db_acc):
        i = pl.program_id(0)
        first = i == 0

        @pl.when(first)
        def _():
            dk_acc[...] = jnp.zeros_like(dk_acc)
            dv_acc[...] = jnp.zeros_like(dv_acc)
            db_acc[...] = jnp.zeros_like(db_acc)

        kwin[0:TQ, :] = kp_ref[...]
        kwin[TQ:2 * TQ, :] = kc_ref[...]
        vwin[0:TQ, :] = vp_ref[...]
        vwin[TQ:2 * TQ, :] = vc_ref[...]
        scale = HD ** -0.5
        qmask = _head_masks(scale)
        vmask = _head_masks(1.0)
        low = lax.broadcasted_iota(jnp.int32, (1, LANES), 1) < HD

        def group(g, carry):
            r0 = pl.multiple_of(g * qg, qg)
            base = i * (TQ // LANES) + g * (qg // LANES)
            pen = _key_penalty(first, r0, kg)
            for hp in range(NH // 2):
                ls = slice(LANES * hp, LANES * (hp + 1))
                qb = q_ref[pl.ds(r0, qg), ls]
                kw = kwin[pl.ds(r0, kg), ls]
                dob = do_ref[pl.ds(r0, qg), ls]
                prod = dob.astype(F32) * o_ref[pl.ds(r0, qg), ls]
                lseb = lse_ref[pl.ds(r0, qg), ls]
                q2 = jnp.concatenate([qb * qmask[0], qb * qmask[1]], axis=0)
                do2 = jnp.concatenate([dob * vmask[0], dob * vmask[1]], axis=0)
                lse2 = jnp.concatenate([lseb[:, 0:1], lseb[:, HD:HD + 1]], axis=0)
                dsum = jnp.concatenate([jnp.sum(jnp.where(low, prod, 0.0), axis=-1, keepdims=True),
                                        jnp.sum(jnp.where(low, 0.0, prod), axis=-1, keepdims=True)], axis=0)
                s = lax.dot_general(q2, kw, NT, preferred_element_type=F32) + b2_ref[hp] + pen
                p = jnp.exp(s - lse2)
                dp = lax.dot_general(do2, vwin[pl.ds(r0, kg), ls], NT, preferred_element_type=F32)
                ds = p * (dp - dsum)
                db_acc[hp] += ds
                dsb = ds.astype(BF16)
                dq2 = jnp.dot(dsb, kw, preferred_element_type=F32)
                dq_ref[pl.ds(r0, qg), ls] = (jnp.where(low, dq2[:qg], dq2[qg:]) * scale).astype(BF16)
                dkt = lax.dot_general(q2, dsb, TN, preferred_element_type=F32)
                dvt = lax.dot_general(do2, p.astype(BF16), TN, preferred_element_type=F32)
                for kb in range(kg // LANES):
                    dk_acc[base + kb, ls, :] += dkt[:, LANES * kb:LANES * (kb + 1)]
                    dv_acc[base + kb, ls, :] += dvt[:, LANES * kb:LANES * (kb + 1)]
            return carry

        lax.fori_loop(0, TQ // qg, group, 0)

        @pl.when(i == nt - 1)
        def _():
            pltpu.sync_copy(dk_acc, dk_hbm)
            pltpu.sync_copy(dv_acc, dv_hbm)
            pltpu.sync_copy(db_acc, db_hbm)

    row = lambda w: pl.BlockSpec((TQ, w), lambda i: (i, 0))
    return pl.pallas_call(
        body, grid=(nt,),
        in_specs=_attn_window_specs() + [row(CW), row(CW), row(CW), _const((NH // 2, 2 * qg, kg))],
        out_specs=[row(CW), _any(), _any(), _any()],
        out_shape=[jax.ShapeDtypeStruct((t, CW), BF16), jax.ShapeDtypeStruct((nkb, CW, LANES), F32),
                   jax.ShapeDtypeStruct((nkb, CW, LANES), F32), jax.ShapeDtypeStruct((NH // 2, 2 * qg, kg), F32)],
        scratch_shapes=[pltpu.VMEM((2 * TQ, CW), BF16), pltpu.VMEM((2 * TQ, CW), BF16),
                        pltpu.VMEM((nkb, CW, LANES), F32), pltpu.VMEM((nkb, CW, LANES), F32),
                        pltpu.VMEM((NH // 2, 2 * qg, kg), F32)],
        compiler_params=_cp(("arbitrary",)), name="bwd_attn",
    )(proj, proj, proj, proj, proj, o, do, lse, bias2)


def bwd_inproj(dxm, x, dhc, dbg, dcg, dq, dk, dv, g, w_all):
    t = x.shape[0]
    nt = t // TQ
    wc = PROJ // NCHIP

    def body(dxm_ref, x_ref, dhc_ref, dbg_ref, dcg_ref, dq_ref, dk_ref, dv_ref, g_ref, w_hbm,
             dx_ref, dw_hbm, dg_ref, w_v, dp_ref, dw_acc):
        @pl.when(pl.program_id(0) == 0)
        def _():
            pltpu.sync_copy(w_hbm, w_v)
            dg_ref[...] = jnp.zeros_like(dg_ref)
            dw_acc[...] = jnp.zeros_like(dw_acc)

        dp_ref[:, 0:CW] = dhc_ref[...]
        dp_ref[:, CW:2 * CW] = dbg_ref[...]
        dp_ref[:, 2 * CW:3 * CW] = dcg_ref[...]
        dp_ref[:, 3 * CW:4 * CW] = dq_ref[...]
        for kb in range(TQ // LANES):
            rows = slice(LANES * kb, LANES * (kb + 1))
            dp_ref[rows, 4 * CW:5 * CW] = jnp.transpose(dk_ref[kb]).astype(BF16)
            dp_ref[rows, 5 * CW:6 * CW] = jnp.transpose(dv_ref[kb]).astype(BF16)
        dh = jnp.zeros((TQ, D), F32)
        for b in range(NCHIP):
            dh = dh + lax.dot_general(dp_ref[:, wc * b:wc * (b + 1)], w_v[b], NT, preferred_element_type=F32)
        xv = x_ref[...]
        gv = g_ref[...]
        hb = _rms(xv, gv).astype(BF16)
        for b in range(NCHIP):
            dw_acc[b] += lax.dot_general(hb, dp_ref[:, wc * b:wc * (b + 1)], TN, preferred_element_type=F32)
        dxv, dgv = _rms_bwd(dh, xv, gv)
        dg_ref[...] += dgv
        dx_ref[...] = dxm_ref[...] + dxv

        @pl.when(pl.program_id(0) == nt - 1)
        def _():
            pltpu.sync_copy(dw_acc, dw_hbm)

    row = lambda w: pl.BlockSpec((TQ, w), lambda i: (i, 0))
    pad = pl.BlockSpec((TQ // LANES, CW, LANES), lambda i: (i + 1, 0, 0))
    return pl.pallas_call(
        body, grid=(nt,),
        in_specs=[row(D), row(D), row(CW), row(CW), row(CW), row(CW), pad, pad, _const((1, D)), _any()],
        out_specs=[row(D), _any(), _const((1, D))],
        out_shape=[jax.ShapeDtypeStruct((t, D), F32), jax.ShapeDtypeStruct((NCHIP, D, wc), F32),
                   jax.ShapeDtypeStruct((1, D), F32)],
        scratch_shapes=[pltpu.VMEM((NCHIP, D, wc), BF16), pltpu.VMEM((TQ, PROJ), BF16),
                        pltpu.VMEM((NCHIP, D, wc), F32)],
        compiler_params=_cp(("arbitrary",)), name="bwd_inproj",
    )(dxm, x, dhc, dbg, dcg, dq, dk, dv, g, w_all)


def wgrad(a, b, kb, nb, by_columns, name):
    t, k = a.shape
    n = b.shape[1]
    tk = 512

    def body(a_ref, b_ref, o_ref):
        o_ref[...] = jnp.zeros_like(o_ref)
        for c in range(t // tk):
            o_ref[...] += lax.dot_general(a_ref[tk * c:tk * (c + 1), :], b_ref[tk * c:tk * (c + 1), :], TN,
                                          preferred_element_type=F32)

    if by_columns:
        assert nb == n // NCHIP
        out_spec = pl.BlockSpec((None, kb, nb), lambda ki, ni: (ni, ki, 0))
        out_shape = jax.ShapeDtypeStruct((NCHIP, k, nb), F32)
    else:
        assert nb == n
        out_spec = pl.BlockSpec((kb, nb), lambda ki, ni: (ki, 0))
        out_shape = jax.ShapeDtypeStruct((k, n), F32)
    return pl.pallas_call(
        body, grid=(k // kb, n // nb),
        in_specs=[pl.BlockSpec((t, kb), lambda ki, ni: (0, ki)), pl.BlockSpec((t, nb), lambda ki, ni: (0, ni))],
        out_specs=out_spec, out_shape=out_shape,
        compiler_params=_cp(("arbitrary", "arbitrary")), name=name)(a, b)


TOE = 1024
assert 2 * QG_FWD + LEFT <= TOE
N_FLAT = LEFT - REL_CLIP + 1
N_VAR = BAND - N_FLAT


def _diag_vector(table):
    last = table[:, 2 * REL_CLIP:]
    var = table[:, 2 * REL_CLIP - N_VAR:2 * REL_CLIP][:, ::-1]
    return jnp.concatenate([jnp.broadcast_to(last, (NH, N_FLAT)), var, jnp.broadcast_to(last, (NH, TOE - BAND))], axis=1)


def _diag_vector_bwd(dvec):
    dlast = jnp.sum(dvec[:, :N_FLAT], axis=1, keepdims=True) + jnp.sum(dvec[:, BAND:], axis=1, keepdims=True)
    dvar = dvec[:, N_FLAT:BAND][:, ::-1]
    return jnp.concatenate([jnp.zeros((NH, 2 * REL_CLIP - N_VAR), F32), dvar, dlast], axis=1)


def _band_valid(qg):
    r = lax.broadcasted_iota(jnp.int32, (qg, qg + LEFT), 0)
    p = lax.broadcasted_iota(jnp.int32, (qg, qg + LEFT), 1)
    start = lax.shift_left(lax.shift_right_logical(r, 6), 6)
    return (p >= start) & (p < start + BAND)


def bias_expand(vec, qgs, after=()):
    def body(v_ref, *o_refs):
        for qg, o_ref in zip(qgs, o_refs):
            valid = _band_valid(qg)
            for h in range(NH):
                rows = jnp.broadcast_to(v_ref[h:h + 1, :], (qg, TOE))
                toe = pltpu.roll(rows, 0, 1, stride=1, stride_axis=0)
                o_ref[h // 2, qg * (h % 2):qg * (h % 2 + 1), :] = jnp.where(valid, toe[:, :qg + LEFT], NEG_INF)

    vm = pl.BlockSpec(memory_space=pltpu.VMEM)
    return pl.pallas_call(_behind(body, 1, after), in_specs=[vm] + [_any()] * len(after), out_specs=[vm] * len(qgs),
                          out_shape=[jax.ShapeDtypeStruct((NH // 2, 2 * qg, qg + LEFT), F32) for qg in qgs],
                          name="bias_expand")(vec, *after)


def bias_reduce(db2):
    _, qg, kg = db2.shape

    def body(d_ref, o_ref):
        ii = lax.broadcasted_iota(jnp.int32, (kg, kg), 0)
        jj = lax.broadcasted_iota(jnp.int32, (kg, kg), 1)
        flip = jnp.where(ii + jj == kg - 1, 1.0, 0.0).astype(BF16)
        for h in range(NH):
            rest = d_ref[h]
            rev = jnp.zeros((qg, kg), F32)
            for _ in range(3):
                term = rest.astype(BF16)
                rev = rev + jnp.dot(term, flip, preferred_element_type=F32)
                rest = rest - term.astype(F32)
            d = jnp.concatenate([jnp.zeros((qg, TOE - kg), F32), rev], axis=1)
            back = pltpu.roll(d, 0, 1, stride=1, stride_axis=0)
            o_ref[h:h + 1, :] = jnp.sum(back, axis=0, keepdims=True)

    rev = pl.pallas_call(body, out_shape=jax.ShapeDtypeStruct((NH, TOE), F32), name="bias_reduce")(db2)
    return rev[:, ::-1]


def _place():
    x, y, c = lax.axis_index("x"), lax.axis_index("y"), lax.axis_index("c")
    chips = [(1 - x, y), (x, 1 - y), (1 - x, 1 - y)]
    return x, y, c, chips


def _half(ref_rows, c):
    return pl.ds(c * (ref_rows // 2), ref_rows // 2)


HBM_SPEC = pl.BlockSpec(memory_space=pltpu.HBM)
SEM_SPEC = pl.BlockSpec(memory_space=pltpu.SEMAPHORE)
IN_FLIGHT = pltpu.CompilerParams(has_side_effects=pltpu.SideEffectType.DATAFLOW_SIDE_EFFECTING)


def _in_hbm(a):
    return pltpu.with_memory_space_constraint(a, pltpu.HBM)


def cast_to_slot(ws, chip, layer, after=()):
    n = len(ws)
    steps = 4

    def body(b_ref, *refs):
        del b_ref
        for w_ref, o_ref in zip(refs[:n], refs[n + len(after):]):
            o_ref[...] = w_ref[...].astype(BF16)

    grid_spec = pltpu.PrefetchScalarGridSpec(
        num_scalar_prefetch=1, grid=(steps,),
        in_specs=[pl.BlockSpec((None, w.shape[1] // steps, w.shape[2]), lambda r, b: (layer, r, 0)) for w in ws]
        + [_any()] * len(after),
        out_specs=[pl.BlockSpec((None, w.shape[1] // steps, w.shape[2]), lambda r, b: (b[0], r, 0)) for w in ws])
    return pl.pallas_call(body, grid_spec=grid_spec,
                          out_shape=[jax.ShapeDtypeStruct((NCHIP,) + w.shape[1:], BF16) for w in ws],
                          compiler_params=_cp(("arbitrary",)), name="cast_to_slot")(chip, *ws, *after)


def _gather_copies(bufs, send, recv):
    x, y, c, chips = _place()
    b = 2 * x + y
    out = []
    for k, buf in enumerate(bufs):
        rows = buf.shape[1]
        mine = buf.at[b, _half(rows, c), :]
        for j, (cx, cy) in enumerate(chips):
            theirs = buf.at[2 * cx + cy, _half(rows, c), :]
            sems = dict(send_sem=send.at[3 * k + j], recv_sem=recv.at[3 * k + j],
                        device_id=(cx, cy, c), device_id_type=MESH)
            out.append((pltpu.make_async_remote_copy(src_ref=mine, dst_ref=mine, **sems),
                        pltpu.make_async_remote_copy(src_ref=theirs, dst_ref=theirs, **sems)))
    return out


def gather_start(bufs, after, layer):
    n = len(bufs)

    def body(*refs):
        ins = refs[:n]
        send, recv = refs[n + 1], refs[n + 2]
        token = refs[-1]
        for start, _ in _gather_copies(ins, send, recv):
            start.start()
        token[...] = jnp.zeros_like(token)

    sems = pltpu.SemaphoreType.DMA((3 * n,))
    res = pl.pallas_call(
        body, name=f"gather_start_{layer}",
        in_specs=[HBM_SPEC] * n + [_any()],
        out_specs=[SEM_SPEC, SEM_SPEC] + [HBM_SPEC] * n + [pl.BlockSpec(memory_space=pltpu.VMEM)],
        out_shape=[sems, sems] + [pltpu.HBM(b.shape, b.dtype) for b in bufs] + [jax.ShapeDtypeStruct((8, LANES), F32)],
        input_output_aliases={k: 2 + k for k in range(n)}, compiler_params=IN_FLIGHT,
    )(*[_in_hbm(b) for b in bufs], after)
    return res[0], res[1], res[2:2 + n], res[-1]


def gather_wait(send, recv, bufs, after, layer):
    n = len(bufs)

    def body(*refs):
        ins = refs[:n]
        send_ref, recv_ref = refs[n], refs[n + 1]
        for start, arrival in _gather_copies(ins, send_ref, recv_ref):
            start.wait_send()
            arrival.wait_recv()

    return pl.pallas_call(
        body, name=f"gather_wait_{layer}",
        in_specs=[HBM_SPEC] * n + [SEM_SPEC, SEM_SPEC, _any()], out_specs=[HBM_SPEC] * n,
        out_shape=[pltpu.HBM(b.shape, b.dtype) for b in bufs],
        input_output_aliases={k: k for k in range(n)}, compiler_params=IN_FLIGHT,
    )(*bufs, send, recv, after)


def gather_forward(bufs):
    n = len(bufs)

    def body(*refs):
        outs = refs[n:2 * n]
        send, recv = refs[2 * n:]
        x, y, c, chips = _place()
        cps = []
        for k in range(n):
            rows = outs[k].shape[1]
            for j, (cx, cy) in enumerate(chips):
                sems = dict(send_sem=send.at[3 * k + j], recv_sem=recv.at[3 * k + j],
                            device_id=(x, y, 1 - c), device_id_type=MESH)
                mine = outs[k].at[2 * cx + cy, _half(rows, c), :]
                theirs = outs[k].at[2 * cx + cy, _half(rows, 1 - c), :]
                cp = pltpu.make_async_remote_copy(src_ref=mine, dst_ref=mine, **sems)
                cp.start()
                cps.append((cp, pltpu.make_async_remote_copy(src_ref=theirs, dst_ref=theirs, **sems)))
        for cp, arrival in cps:
            cp.wait_send()
            arrival.wait_recv()

    return pl.pallas_call(
        body, in_specs=[_any()] * n, out_specs=[_any()] * n,
        out_shape=[jax.ShapeDtypeStruct(b.shape, b.dtype) for b in bufs], input_output_aliases={k: k for k in range(n)},
        scratch_shapes=[pltpu.SemaphoreType.DMA((3 * n,)), pltpu.SemaphoreType.DMA((3 * n,))],
        name="gather_forward")(*bufs)


def _forward_copies(bufs, send, recv):
    x, y, c, chips = _place()
    out = []
    for k, buf in enumerate(bufs):
        rows = buf.shape[1]
        for j, (cx, cy) in enumerate(chips):
            sems = dict(send_sem=send.at[3 * k + j], recv_sem=recv.at[3 * k + j],
                        device_id=(x, y, 1 - c), device_id_type=MESH)
            mine = buf.at[2 * cx + cy, _half(rows, c), :]
            theirs = buf.at[2 * cx + cy, _half(rows, 1 - c), :]
            out.append((pltpu.make_async_remote_copy(src_ref=mine, dst_ref=mine, **sems),
                        pltpu.make_async_remote_copy(src_ref=theirs, dst_ref=theirs, **sems)))
    return out


def forward_start(bufs, tag):
    n = len(bufs)

    def body(*refs):
        ins = refs[:n]
        send, recv = refs[n], refs[n + 1]
        token = refs[-1]
        for start, _ in _forward_copies(ins, send, recv):
            start.start()
        token[...] = jnp.zeros_like(token)

    sems = pltpu.SemaphoreType.DMA((3 * n,))
    res = pl.pallas_call(
        body, name=f"forward_start_{tag}", in_specs=[HBM_SPEC] * n,
        out_specs=[SEM_SPEC, SEM_SPEC] + [HBM_SPEC] * n + [pl.BlockSpec(memory_space=pltpu.VMEM)],
        out_shape=[sems, sems] + [pltpu.HBM(b.shape, b.dtype) for b in bufs] + [jax.ShapeDtypeStruct((8, LANES), F32)],
        input_output_aliases={k: 2 + k for k in range(n)}, compiler_params=IN_FLIGHT,
    )(*[_in_hbm(b) for b in bufs])
    return res[0], res[1], res[2:2 + n], res[-1]


def forward_wait(send, recv, bufs, after, tag):
    n = len(bufs)

    def body(*refs):
        ins = refs[:n]
        send_ref, recv_ref = refs[n], refs[n + 1]
        for start, arrival in _forward_copies(ins, send_ref, recv_ref):
            start.wait_send()
            arrival.wait_recv()

    return pl.pallas_call(
        body, name=f"forward_wait_{tag}",
        in_specs=[HBM_SPEC] * n + [SEM_SPEC, SEM_SPEC, _any()], out_specs=[HBM_SPEC] * n,
        out_shape=[pltpu.HBM(b.shape, b.dtype) for b in bufs],
        input_output_aliases={k: k for k in range(n)}, compiler_params=IN_FLIGHT,
    )(*bufs, send, recv, after)


def _exchange_copies(srcs, lands, send, recv):
    x, y, c, _ = _place()
    return [pltpu.make_async_remote_copy(
        src_ref=src.at[:, _half(src.shape[1], 1 - c), :], dst_ref=land, send_sem=send.at[k], recv_sem=recv.at[k],
        device_id=(x, y, 1 - c), device_id_type=MESH) for k, (src, land) in enumerate(zip(srcs, lands))]


def exchange_start(srcs, tag):
    n = len(srcs)
    lands = [lax.empty((s.shape[0], s.shape[1] // 2, s.shape[2]), s.dtype) for s in srcs]

    def body(*refs):
        ins, land_refs = refs[:n], refs[n:2 * n]
        send, recv = refs[2 * n], refs[2 * n + 1]
        token = refs[-1]
        for cp in _exchange_copies(ins, land_refs, send, recv):
            cp.start()
        token[...] = jnp.zeros_like(token)

    sems = pltpu.SemaphoreType.DMA((n,))
    res = pl.pallas_call(
        body, name=f"exchange_start_{tag}",
        in_specs=[HBM_SPEC] * (2 * n),
        out_specs=[SEM_SPEC, SEM_SPEC] + [HBM_SPEC] * (2 * n) + [pl.BlockSpec(memory_space=pltpu.VMEM)],
        out_shape=[sems, sems] + [pltpu.HBM(a.shape, a.dtype) for a in list(srcs) + lands]
        + [jax.ShapeDtypeStruct((8, LANES), F32)],
        input_output_aliases={k: 2 + k for k in range(2 * n)}, compiler_params=IN_FLIGHT,
    )(*[_in_hbm(a) for a in list(srcs) + lands])
    return res[0], res[1], res[2:2 + n], res[2 + n:2 + 2 * n], res[-1]


def exchange_wait(send, recv, srcs, lands, after, tag):
    n = len(srcs)

    def body(*refs):
        ins, land_refs = refs[:n], refs[n:2 * n]
        send_ref, recv_ref = refs[2 * n], refs[2 * n + 1]
        for cp in _exchange_copies(ins, land_refs, send_ref, recv_ref):
            cp.wait_send()
            cp.wait_recv()

    res = pl.pallas_call(
        body, name=f"exchange_wait_{tag}",
        in_specs=[HBM_SPEC] * (2 * n) + [SEM_SPEC, SEM_SPEC, _any()], out_specs=[HBM_SPEC] * (2 * n),
        out_shape=[pltpu.HBM(a.shape, a.dtype) for a in list(srcs) + list(lands)],
        input_output_aliases={k: k for k in range(2 * n)}, compiler_params=IN_FLIGHT,
    )(*srcs, *lands, send, recv, after)
    return res[:n], res[n:]


def add_pair(gs, r1s, core):
    n = len(gs)

    def body(c_ref, *refs):
        del c_ref
        for g_ref, r_ref, o_ref in zip(refs[:n], refs[n:2 * n], refs[2 * n:]):
            o_ref[...] = (g_ref[...] + r_ref[...]).astype(BF16)

    blk = lambda r: (None,) + r.shape[1:]
    grid_spec = pltpu.PrefetchScalarGridSpec(
        num_scalar_prefetch=1, grid=(NCHIP,),
        in_specs=[pl.BlockSpec(blk(r), lambda s, c: (s, c[0], 0)) for r in r1s]
        + [pl.BlockSpec(blk(r), lambda s, c: (s, 0, 0)) for r in r1s],
        out_specs=[pl.BlockSpec(blk(r), lambda s, c: (s, 0, 0)) for r in r1s])
    return pl.pallas_call(body, grid_spec=grid_spec, out_shape=[jax.ShapeDtypeStruct(r.shape, BF16) for r in r1s],
                          compiler_params=_cp(("arbitrary",)), name="add_pair")(core, *gs, *r1s)


def _scatter_copies(srcs, lands, send, recv):
    _, _, c, chips = _place()
    out = []
    for k, (src, land) in enumerate(zip(srcs, lands)):
        for j, (cx, cy) in enumerate(chips):
            out.append(pltpu.make_async_remote_copy(
                src_ref=src.at[2 * cx + cy], dst_ref=land.at[j], send_sem=send.at[3 * k + j],
                recv_sem=recv.at[3 * k + j], device_id=(cx, cy, c), device_id_type=MESH))
    return out


def scatter_start(srcs, layer):
    n = len(srcs)
    srcs = list(srcs)
    lands = [lax.empty((3,) + s.shape[1:], s.dtype) for s in srcs]

    def body(*refs):
        ins, land_refs = refs[:n], refs[n:2 * n]
        send, recv = refs[2 * n], refs[2 * n + 1]
        token = refs[-1]
        for cp in _scatter_copies(ins, land_refs, send, recv):
            cp.start()
        token[...] = jnp.zeros_like(token)

    sems = pltpu.SemaphoreType.DMA((3 * n,))
    res = pl.pallas_call(
        body, name=f"scatter_start_{layer}",
        in_specs=[HBM_SPEC] * (2 * n),
        out_specs=[SEM_SPEC, SEM_SPEC] + [HBM_SPEC] * (2 * n) + [pl.BlockSpec(memory_space=pltpu.VMEM)],
        out_shape=[sems, sems] + [pltpu.HBM(a.shape, a.dtype) for a in srcs + lands]
        + [jax.ShapeDtypeStruct((8, LANES), F32)],
        input_output_aliases={k: 2 + k for k in range(2 * n)}, compiler_params=IN_FLIGHT,
    )(*[_in_hbm(a) for a in srcs + lands])
    return res[0], res[1], res[2:2 + n], res[2 + n:2 + 2 * n], res[-1]


def scatter_wait(send, recv, srcs, lands, after, layer):
    n = len(srcs)

    def body(*refs):
        ins, land_refs = refs[:n], refs[n:2 * n]
        send_ref, recv_ref = refs[2 * n], refs[2 * n + 1]
        for cp in _scatter_copies(ins, land_refs, send_ref, recv_ref):
            cp.wait_send()
            cp.wait_recv()

    res = pl.pallas_call(
        body, name=f"scatter_wait_{layer}",
        in_specs=[HBM_SPEC] * (2 * n) + [SEM_SPEC, SEM_SPEC, _any()], out_specs=[HBM_SPEC] * (2 * n),
        out_shape=[pltpu.HBM(a.shape, a.dtype) for a in list(srcs) + list(lands)],
        input_output_aliases={k: k for k in range(2 * n)}, compiler_params=IN_FLIGHT,
    )(*srcs, *lands, send, recv, after)
    return res[n:]


def add_chips(gs, r1s, r2s, place, totals, layer):
    n = len(gs)
    steps = 2

    def body(p_ref, *refs):
        del p_ref
        for g_ref, r1_ref, r2_ref, o_ref in zip(refs[:n], refs[n:2 * n], refs[2 * n:3 * n], refs[4 * n:]):
            own = g_ref[...] + r1_ref[...]
            o_ref[...] = ((own + r2_ref[0].astype(F32)) + r2_ref[1].astype(F32)) + r2_ref[2].astype(F32)

    blk = lambda r: (None, r.shape[1] // steps, r.shape[2])
    grid_spec = pltpu.PrefetchScalarGridSpec(
        num_scalar_prefetch=1, grid=(steps,),
        in_specs=[pl.BlockSpec(blk(r), lambda i, p: (p[1], p[0] * steps + i, 0)) for r in r1s]
        + [pl.BlockSpec(blk(r), lambda i, p: (p[1], i, 0)) for r in r1s]
        + [pl.BlockSpec((3,) + blk(r)[1:], lambda i, p: (0, i, 0)) for r in r1s] + [_any()] * n,
        out_specs=[pl.BlockSpec(blk(r), lambda i, p: (layer, p[0] * steps + i, 0)) for r in r1s])
    return pl.pallas_call(body, grid_spec=grid_spec, out_shape=[jax.ShapeDtypeStruct(t.shape, F32) for t in totals],
                          input_output_aliases={1 + 3 * n + k: k for k in range(n)},
                          compiler_params=_cp(("arbitrary",)), name="add_chips")(place, *gs, *r1s, *r2s, *totals)


def _share_copies(bufs, send, recv):
    x, y, c, _ = _place()
    out = []
    for k, buf in enumerate(bufs):
        sems = dict(send_sem=send.at[k], recv_sem=recv.at[k], device_id=(x, y, 1 - c), device_id_type=MESH)
        mine = buf.at[:, _half(buf.shape[1], c), :]
        theirs = buf.at[:, _half(buf.shape[1], 1 - c), :]
        out.append((pltpu.make_async_remote_copy(src_ref=mine, dst_ref=mine, **sems),
                    pltpu.make_async_remote_copy(src_ref=theirs, dst_ref=theirs, **sems)))
    return out


def share_start(bufs, tag):
    n = len(bufs)

    def body(*refs):
        ins = refs[:n]
        send, recv = refs[n], refs[n + 1]
        token = refs[-1]
        for start, _ in _share_copies(ins, send, recv):
            start.start()
        token[...] = jnp.zeros_like(token)

    sems = pltpu.SemaphoreType.DMA((n,))
    res = pl.pallas_call(
        body, name=f"share_start_{tag}", in_specs=[HBM_SPEC] * n,
        out_specs=[SEM_SPEC, SEM_SPEC] + [HBM_SPEC] * n + [pl.BlockSpec(memory_space=pltpu.VMEM)],
        out_shape=[sems, sems] + [pltpu.HBM(b.shape, b.dtype) for b in bufs] + [jax.ShapeDtypeStruct((8, LANES), F32)],
        input_output_aliases={k: 2 + k for k in range(n)}, compiler_params=IN_FLIGHT,
    )(*[_in_hbm(b) for b in bufs])
    return res[0], res[1], res[2:2 + n], res[-1]


def share_wait(send, recv, bufs, after, tag):
    n = len(bufs)

    def body(*refs):
        ins = refs[:n]
        send_ref, recv_ref = refs[n], refs[n + 1]
        for start, arrival in _share_copies(ins, send_ref, recv_ref):
            start.wait_send()
            arrival.wait_recv()

    return pl.pallas_call(
        body, name=f"share_wait_{tag}",
        in_specs=[HBM_SPEC] * n + [SEM_SPEC, SEM_SPEC, _any()], out_specs=[HBM_SPEC] * n,
        out_shape=[pltpu.HBM(b.shape, b.dtype) for b in bufs],
        input_output_aliases={k: k for k in range(n)}, compiler_params=IN_FLIGHT,
    )(*bufs, send, recv, after)


def small_allreduce(v, after=()):
    rows = v.shape[0]
    flips = [(fx, fy, fc) for fx in (0, 1) for fy in (0, 1) for fc in (0, 1)][1:]

    def body(v_ref, o_ref, buf, send, recv):
        x, y, c, _ = _place()
        buf[4 * x + 2 * y + c] = v_ref[...]
        peers = [(jnp.where(fx, 1 - x, x), jnp.where(fy, 1 - y, y), jnp.where(fc, 1 - c, c)) for fx, fy, fc in flips]
        cps = []
        for k, peer in enumerate(peers):
            cp = pltpu.make_async_remote_copy(
                src_ref=v_ref, dst_ref=buf.at[4 * x + 2 * y + c], send_sem=send.at[k], recv_sem=recv.at[k],
                device_id=peer, device_id_type=MESH)
            cp.start()
            cps.append(cp)
        for k, (px, py, pc) in enumerate(peers):
            pltpu.make_async_remote_copy(
                src_ref=v_ref, dst_ref=buf.at[4 * px + 2 * py + pc], send_sem=send.at[k], recv_sem=recv.at[k],
                device_id=(px, py, pc), device_id_type=MESH).wait_recv()
        for cp in cps:
            cp.wait_send()
        acc = buf[0]
        for s in range(1, 8):
            acc = acc + buf[s]
        o_ref[...] = acc

    vm = pl.BlockSpec(memory_space=pltpu.VMEM)
    return pl.pallas_call(
        _behind(body, 1, after), in_specs=[vm] + [_any()] * len(after), out_specs=vm,
        out_shape=jax.ShapeDtypeStruct((rows, SMALL_COLS), F32),
        scratch_shapes=[pltpu.VMEM((8, rows, SMALL_COLS), F32), pltpu.SemaphoreType.DMA((7,)),
                        pltpu.SemaphoreType.DMA((7,))],
        name="reduce_small")(v, *after)


def adamw(w, g, m, v, rb, name, after=()):
    nl, rows, cols = w.shape

    def body(w_ref, g_ref, m_ref, v_ref, go_ref, d_ref, nm_ref, nv_ref):
        gv = g_ref[...]
        go_ref[...] = gv
        nm = ADAM_B1 * m_ref[...] + (1.0 - ADAM_B1) * gv
        nv = ADAM_B2 * v_ref[...] + (1.0 - ADAM_B2) * (gv * gv)
        m_hat = nm / (1.0 - ADAM_B1 ** ADAM_STEP)
        v_hat = nv / (1.0 - ADAM_B2 ** ADAM_STEP)
        d_ref[...] = -ADAM_LR * (m_hat / (jnp.sqrt(v_hat) + ADAM_EPS) + ADAM_WD * w_ref[...])
        nm_ref[...] = nm
        nv_ref[...] = nv

    blk = pl.BlockSpec((None, rb, cols), lambda l, r: (l, r, 0))
    shp = jax.ShapeDtypeStruct(w.shape, F32)
    return pl.pallas_call(_behind(body, 4, after), grid=(nl, rows // rb), in_specs=[blk] * 4 + [_any()] * len(after),
                          out_specs=[blk] * 4, out_shape=[shp] * 4,
                          compiler_params=_cp(("arbitrary", "arbitrary")), name=name)(w, g, m, v, *after)


def _pack(parts, rows):
    flat = jnp.concatenate([p.reshape(-1).astype(F32) for p in parts])
    return jnp.pad(flat, (0, rows * SMALL_COLS - flat.shape[0])).reshape(rows, SMALL_COLS)


def _unpack(vec, shapes):
    flat = vec.reshape(-1)
    out, off = [], 0
    for s in shapes:
        size = 1
        for d in s:
            size *= d
        out.append(flat[off:off + size].reshape(s))
        off += size
    return out


def kernel(x, w_in, w_conv, rel_bias, g_conv_out, g_attn_out, w_out, g_pre_mix, g_post_mix, g_pre_ffn, g_post_ffn, w_ffn_in, w_ffn_out, loss_target, m_w_in, m_w_conv, m_rel_bias, m_g_conv_out, m_g_attn_out, m_w_out, m_g_pre_mix, m_g_post_mix, m_g_pre_ffn, m_g_post_ffn, m_w_ffn_in, m_w_ffn_out, v_w_in, v_w_conv, v_rel_bias, v_g_conv_out, v_g_attn_out, v_w_out, v_g_pre_mix, v_g_post_mix, v_g_pre_ffn, v_g_post_ffn, v_w_ffn_in, v_w_ffn_out):
    xi, yi, ci = lax.axis_index("x"), lax.axis_index("y"), lax.axis_index("c")
    chip = 2 * xi + yi
    nl = w_in.shape[0]
    x0 = x[0]
    target = loss_target[0]
    cwl = CW // NCHIP

    chip1 = chip.reshape(1).astype(jnp.int32)
    big_weights = [w_in, w_out, w_ffn_in, w_ffn_out]
    own = [cast_to_slot(big_weights, chip1, 0)]
    wc_mine = jnp.pad(w_conv.reshape(-1), (0, 16 * LANES - w_conv.size)).reshape(1, 16, LANES)
    wc_slot = lax.dynamic_update_slice_in_dim(jnp.zeros((NCHIP, 16, LANES), F32), wc_mine, chip, axis=0)
    gm = jnp.kron(jnp.eye(CW // HD, dtype=F32), jnp.full((HD, HD), 1.0 / HD, F32)).astype(BF16)
    row = lambda a, l: a[l][None, :]

    def gather_finish(flight, after, tag):
        send, recv, bufs, _ = flight
        return gather_forward(gather_wait(send, recv, bufs, after, tag))

    first_mix = gather_start(list(own[0][:2]) + [wc_slot], x0, "0m")
    first_ffn = gather_start(own[0][2:], first_mix[3], "0f")
    chain = first_ffn[3]
    biases = []
    for l in range(nl):
        biases.append(bias_expand(_diag_vector(rel_bias[l]), (QG_FWD, QG_BWD), [chain]))
        chain = biases[l][1]
    for l in range(1, nl):
        own.append(cast_to_slot(big_weights, chip1, l, [chain]))
        chain = own[l][0]
    gw_in, gw_out, wc_all = gather_finish(first_mix, chain, "0m")
    wc_full = wc_all.reshape(NCHIP, -1)[:, :nl * cwl * 3].reshape(NCHIP, nl, cwl, 3)
    wc_full = jnp.transpose(wc_full, (1, 0, 2, 3)).reshape(nl, CW, 3)
    wconv_t = jnp.pad(jnp.transpose(wc_full, (0, 2, 1)), ((0, 0), (0, 5), (0, 0)))
    flights, to_sibling = {}, None
    saved, weights = [], []
    h = x0
    for l in range(nl):
        if l == 0:
            pass
        elif l == 1:
            flights[2] = gather_start(own[2], h, 2)
            gw_in, gw_out, gw_fi, gw_fo = gather_finish(flights[l], flights[2][3], l)
        else:
            gw_in, gw_out, gw_fi, gw_fo = forward_wait(*to_sibling[:3], h, l)
        gw_out = gw_out.reshape(D, D)
        behind_mix, behind_ffn = ([first_ffn[3]] if l == 0 else []), []
        if l + 1 < nl and l + 1 not in flights:
            flights[l + 1] = gather_start(own[l + 1], first_ffn[3] if l == 0 else gw_in, l + 1)
            behind_mix.append(flights[l + 1][3])
        bias2, bias2_bwd = biases[l]
        proj = fwd_inproj(h, row(g_pre_mix, l), gw_in, behind_mix)
        xmid, o, lse, y, z = fwd_mix(h, proj, bias2, wconv_t[l], row(g_conv_out, l), row(g_attn_out, l),
                                     row(g_post_mix, l), gm, gw_out)
        if l == 0:
            gw_fi, gw_fo = gather_finish(first_ffn, xmid, "0f")
        elif l + 1 < nl:
            send, recv, bufs, _ = flights[l + 1]
            landed = gather_wait(send, recv, bufs, xmid, l + 1)
            to_sibling = forward_start(landed, l + 1)
            behind_ffn.append(to_sibling[3])
            if l + 2 < nl:
                flights[l + 2] = gather_start(own[l + 2], to_sibling[3], l + 2)
                behind_ffn.append(flights[l + 2][3])
        gw_fo = gw_fo.reshape(2, DFF // 2, D)
        ffn = fwd_ffn(xmid, row(g_pre_ffn, l), row(g_post_ffn, l), gw_fi, gw_fo, behind_ffn,
                      target if l == nl - 1 else None)
        gu, f = ffn[:2]
        saved.append((h, proj, bias2_bwd, xmid, o, lse, y, z, gu, f))
        weights.append((gw_in, gw_out, gw_fi, gw_fo))
        h = ffn[2]
    dx, loss_blk = ffn[2], ffn[3]

    core = ci.reshape(1).astype(jnp.int32)
    place = jnp.stack([ci, chip]).astype(jnp.int32)
    totals = [lax.empty(w.shape, F32) for w in (w_in, w_out, w_ffn_in, w_ffn_out)]
    small = {k: [None] * nl for k in ("co", "ao", "pm", "qm", "pf", "qf", "rel", "wc")}

    def reduce_begin(kinds, grads, tag):
        return kinds, exchange_start(grads, tag), tag

    def reduce_mid(state, after):
        kinds, (send, recv, srcs, lands, _), tag = state
        grads, from_sibling = exchange_wait(send, recv, srcs, lands, after, tag)
        return kinds, grads, from_sibling, scatter_start(add_pair(grads, from_sibling, core), tag), tag

    def reduce_end(state, after, totals, layer):
        kinds, grads, from_sibling, (send, recv, srcs, lands, _), tag = state
        from_chips = scatter_wait(send, recv, srcs, lands, after, tag)
        totals = list(totals)
        summed = add_chips(grads, from_sibling, from_chips, place, [totals[i] for i in kinds], layer)
        for i, t in zip(kinds, summed):
            totals[i] = t
        return totals

    begun = flying = None
    for l in reversed(range(nl)):
        hin, proj, bias2, xmid, o, lse, y, z, gu, f = saved[l]
        gw_in, gw_out, gw_fi, gw_fo = weights[l]
        behind_ffn = [begun[1][4]] if begun is not None else []
        dxm, gr_fo, dgu, h2, dg_qf, dg_pf = bwd_ffn(dx, f, xmid, gu, row(g_pre_ffn, l), row(g_post_ffn, l),
                                                     gw_fi, gw_fo, behind_ffn)
        gr_fo = gr_fo.reshape(NCHIP, DFF // NCHIP, D)
        behind_mix, behind_conv = [], []
        if begun is not None:
            flying = reduce_mid(begun, dxm)
            behind_mix.append(flying[3][4])
        gr_fi = wgrad(h2, dgu, 512, 2 * DFF // NCHIP, True, "wgrad_ffn_in")
        if l == 0:
            begun_ffn = reduce_begin([2, 3], [gr_fi, gr_fo], "0f")
            behind_mix.append(begun_ffn[1][4])
        gr_out, do, dco, dbg, dg_qm, dg_co, dg_ao = bwd_mix(dxm, z, o, y, proj, wconv_t[l], row(g_conv_out, l),
                                                             row(g_attn_out, l), row(g_post_mix, l), gm, gw_out,
                                                             behind_mix)
        gr_out = gr_out.reshape(NCHIP, D // NCHIP, D)
        if l == 0:
            flying_ffn = reduce_mid(begun_ffn, do)
            behind_conv.append(flying_ffn[3][4])
        dhc, dcg, dwc = bwd_conv(dco, proj, wconv_t[l], behind_conv)
        dq, dk, dv, db2 = bwd_attn(proj, o, do, lse, bias2)
        dx, gr_in, dg_pm = bwd_inproj(dxm, hin, dhc, dbg, dcg, dq, dk, dv, row(g_pre_mix, l), gw_in)
        if flying is not None:
            totals = reduce_end(flying, dx, totals, l + 1)
        small["co"][l], small["ao"][l], small["pm"][l], small["qm"][l] = dg_co, dg_ao, dg_pm, dg_qm
        small["pf"][l], small["qf"][l] = dg_pf, dg_qf
        small["rel"][l] = _diag_vector_bwd(bias_reduce(db2.reshape(NH, QG_BWD, QG_BWD + LEFT)))
        small["wc"][l] = jnp.transpose(dwc[0:3], (1, 0))
        if l > 0:
            begun = reduce_begin([0, 1, 2, 3], [gr_in, gr_out, gr_fi, gr_fo], l)
    begun_mix = reduce_begin([0, 1], [gr_in, gr_out], "0m")
    totals = reduce_end(flying_ffn, begun_mix[1][4], totals, 0)
    flying_mix = reduce_mid(begun_mix, totals[2])
    share_ffn = share_start(totals[2:], "ffn")

    order = ("co", "ao", "pm", "qm", "pf", "qf", "rel", "wc")
    parts = [jnp.stack(small[k]) for k in order] + [loss_blk[0:1, 0:1]]
    shapes = [p.shape for p in parts]
    red_vec = small_allreduce(_pack(parts, 40), [share_ffn[3], flying_mix[3][4]])
    red = _unpack(red_vec, shapes)

    gr_fi, gr_fo = share_wait(*share_ffn[:3], red_vec, "ffn")
    big_fi = adamw(w_ffn_in, gr_fi, m_w_ffn_in, v_w_ffn_in, w_ffn_in.shape[1] // 4, "adamw_ffn_in")
    totals = reduce_end(flying_mix, big_fi[1], totals, 0)
    share_mix = share_start(totals[:2], "mix")
    big_fo = adamw(w_ffn_out, gr_fo, m_w_ffn_out, v_w_ffn_out, w_ffn_out.shape[1] // 4, "adamw_ffn_out",
                   [share_mix[3]])
    gr_in, gr_out = share_wait(*share_mix[:3], big_fo[1], "mix")
    big_in = adamw(w_in, gr_in, m_w_in, v_w_in, w_in.shape[1] // 4, "adamw_in")
    big_out = adamw(w_out, gr_out, m_w_out, v_w_out, w_out.shape[1] // 4, "adamw_out")
    big = [big_in, big_out, big_fi, big_fo]
    gr_co, gr_ao, gr_pm, gr_qm, gr_pf, gr_qf, gr_rel, gr_wc_full, loss = red
    gr_co, gr_ao, gr_pm, gr_qm, gr_pf, gr_qf = [a.reshape(nl, -1) for a in (gr_co, gr_ao, gr_pm, gr_qm, gr_pf, gr_qf)]
    gr_wc = lax.dynamic_slice_in_dim(gr_wc_full, chip * cwl, cwl, axis=1)
    loss = loss.reshape(())

    sw = [g_conv_out, g_attn_out, g_pre_mix, g_post_mix, g_pre_ffn, g_post_ffn, rel_bias, w_conv]
    sg = [gr_co, gr_ao, gr_pm, gr_qm, gr_pf, gr_qf, gr_rel, gr_wc]
    sm = [m_g_conv_out, m_g_attn_out, m_g_pre_mix, m_g_post_mix, m_g_pre_ffn, m_g_post_ffn, m_rel_bias, m_w_conv]
    sv = [v_g_conv_out, v_g_attn_out, v_g_pre_mix, v_g_post_mix, v_g_pre_ffn, v_g_post_ffn, v_rel_bias, v_w_conv]
    sshapes = [a.shape for a in sw]
    packed = [_pack(a, 32)[None] for a in (sw, sg, sm, sv)]
    s_out = [_unpack(a[0], sshapes) for a in adamw(*packed, 32, "adamw_small")]

    def leaves(big_i, small_i):
        b_in, b_out, b_fi, b_fo = big_i
        s_co, s_ao, s_pm, s_qm, s_pf, s_qf, s_rel, s_wc = small_i
        return [b_in, s_wc, s_rel, s_co, s_ao, b_out, s_pm, s_qm, s_pf, s_qf, b_fi, b_fo]

    out = [loss, dx[None]]
    out += leaves([b[0] for b in big], sg)
    for i in range(1, 4):
        out += leaves([b[i] for b in big], s_out[i])
    return tuple(out)
```

```python
import jax
import jax.numpy as jnp
from jax import lax
from jax.experimental import pallas as pl
from jax.experimental.pallas import tpu as pltpu

F32 = jnp.float32
BF16 = jnp.bfloat16

D = 1024
PROJ = 3072
CW = 512
HD = 64
NH = 8
CHUNK = 64
BAND = 576
REL_CLIP = 128
NREL = 2 * REL_CLIP + 1
DFF = 2816
DEPTH = 4
NCHIP = 4
EPS = 1e-6
NEG_INF = -1e30

ADAM_LR = 0.001
ADAM_B1 = 0.9
ADAM_B2 = 0.999
ADAM_EPS = 1e-08
ADAM_WD = 0.01
ADAM_STEP = 10

V7X_VMEM_BYTES = 64 * 1024 * 1024
VMEM_LIMIT = V7X_VMEM_BYTES - 8 * 1024 * 1024
LANES = 128
QG_FWD = 4 * CHUNK
QG_BWD = 2 * CHUNK
LEFT = BAND - CHUNK
TQ = 512
TM = 256
SMALL_COLS = 1024
MESH = pl.DeviceIdType.MESH
NT = (((1,), (1,)), ((), ()))
TN = (((0,), (0,)), ((), ()))


def _cp(sem=None, vmem=VMEM_LIMIT):
    return pltpu.CompilerParams(dimension_semantics=sem, vmem_limit_bytes=vmem)


def _any():
    return pl.BlockSpec(memory_space=pl.ANY)


def _const(shape):
    nd = len(shape)
    return pl.BlockSpec(shape, lambda *_: (0,) * nd)


def _behind(body, n_in, after):
    def ordered(*refs):
        return body(*refs[:n_in], *refs[n_in + len(after):])
    return ordered


def _rms(v, g):
    r = lax.rsqrt(jnp.mean(v * v, axis=-1, keepdims=True) + EPS)
    return v * r * g


def _sigmoid(v):
    return 0.5 * jnp.tanh(0.5 * v) + 0.5


def _rms_bwd(dy, v, g):
    r = lax.rsqrt(jnp.mean(v * v, axis=-1, keepdims=True) + EPS)
    vh = v * r
    dg = jnp.sum(dy * vh, axis=0, keepdims=True)
    dvh = dy * g
    dv = r * (dvh - vh * jnp.mean(dvh * vh, axis=-1, keepdims=True))
    return dv, dg


def _group_mean(v, gm):
    return jnp.dot(v.astype(BF16), gm, preferred_element_type=F32)


def _group_rms_bwd(dy, v, g, gm):
    r = lax.rsqrt(_group_mean(v * v, gm) + EPS)
    vh = v * r
    dg = jnp.sum(dy * vh, axis=0, keepdims=True)
    dvh = dy * g
    dv = r * (dvh - vh * _group_mean(dvh * vh, gm))
    return dv, dg


def _head_masks(scale):
    lane = lax.broadcasted_iota(jnp.int32, (1, LANES), 1)
    return [jnp.where((lane >= HD * a) & (lane < HD * (a + 1)), scale, 0.0).astype(BF16) for a in range(2)]


class _Resident:
    def __init__(self, src, dst, sem):
        self.first = pl.program_id(0) == 0
        self.copy = pltpu.make_async_copy(src, dst, sem)
        self.dst = dst

        @pl.when(self.first)
        def _():
            self.copy.start()

    def read(self):
        @pl.when(self.first)
        def _():
            self.copy.wait()

        return self.dst[...]


FF_CHUNKS = ((0, 1536), (1536, DFF))


def _stream_ffn_weights(wfi_hbm, wfo_hbm, wfi_v, wfo_v, sems, order, step):
    hw = DFF // 2
    per_matrix = {
        0: [(wfi_hbm.at[j], wfi_v.at[0, :, pl.ds(hw * j, hw)]) for j in range(2)],
        1: [(wfi_hbm.at[2 + j], wfi_v.at[1, :, pl.ds(hw * j, hw)]) for j in range(2)],
        2: [(wfo_hbm.at[j], wfo_v.at[pl.ds(hw * j, hw), :]) for j in range(2)],
    }
    pieces = [p for m in order for p in per_matrix[m]]
    slot = {m: 2 * k for k, m in enumerate(order)}

    def make_step(wait):
        def ready(m, chunk):
            if chunk == 0:
                wait(slot[m])
                wait(slot[m] + 1)
        return lambda: step(ready)

    copies = [pltpu.make_async_copy(src, dst, sems.at[k]) for k, (src, dst) in enumerate(pieces)]
    first = pl.program_id(0) == 0

    @pl.when(first)
    def _():
        for cp in copies:
            cp.start()
        make_step(lambda k: copies[k].wait())()

    @pl.when(jnp.logical_not(first))
    def _():
        make_step(lambda k: None)()


def _conv_taps(u_prev, u, scr):
    n = u.shape[0]
    scr[0:16, :] = u_prev
    scr[16:16 + n, :] = u
    return scr[15:15 + n, :], scr[14:14 + n, :]


def fwd_inproj(x, g, w_all, after=()):
    t = x.shape[0]
    wc = PROJ // NCHIP

    def body(x_ref, g_ref, w_hbm, o_ref, w_v):
        @pl.when(pl.program_id(0) == 0)
        def _():
            pltpu.sync_copy(w_hbm, w_v)

        h = _rms(x_ref[...], g_ref[...]).astype(BF16)
        for b in range(NCHIP):
            o_ref[:, wc * b:wc * (b + 1)] = jnp.dot(h, w_v[b], preferred_element_type=F32).astype(BF16)

    return pl.pallas_call(
        _behind(body, 3, after), grid=(t // TQ,),
        in_specs=[pl.BlockSpec((TQ, D), lambda i: (i, 0)), _const((1, D)), _any()] + [_any()] * len(after),
        out_specs=pl.BlockSpec((TQ, PROJ), lambda i: (i, 0)),
        out_shape=jax.ShapeDtypeStruct((t, PROJ), BF16),
        scratch_shapes=[pltpu.VMEM((NCHIP, D, wc), BF16)],
        compiler_params=_cp(("arbitrary",)), name="fwd_inproj")(x, g, w_all, *after)


def _attn_window_specs():
    return [
        pl.BlockSpec((TQ, CW), lambda i: (i, 3)),
        pl.BlockSpec((TQ, CW), lambda i: (jnp.maximum(i - 1, 0), 4)),
        pl.BlockSpec((TQ, CW), lambda i: (i, 4)),
        pl.BlockSpec((TQ, CW), lambda i: (jnp.maximum(i - 1, 0), 5)),
        pl.BlockSpec((TQ, CW), lambda i: (i, 5)),
    ]


def _conv_specs():
    return [
        pl.BlockSpec((TQ, 3 * CW), lambda i: (i, 0)),
        pl.BlockSpec((16, 3 * CW), lambda i: (jnp.maximum(i * (TQ // 16) - 1, 0), 0)),
    ]


def _conv_fwd(pc_ref, pcp_ref, wc_ref, scr, first):
    pc = pc_ref[...].astype(F32)
    hc, bg, cg = pc[:, :CW], pc[:, CW:2 * CW], pc[:, 2 * CW:]
    u = cg * hc
    pp = pcp_ref[...].astype(F32)
    u_prev = jnp.where(first, 0.0, pp[:, 2 * CW:] * pp[:, :CW])
    u1, u2 = _conv_taps(u_prev, u, scr)
    cout = wc_ref[0:1, :] * u2 + wc_ref[1:2, :] * u1 + wc_ref[2:3, :] * u
    return hc, bg, cg, u, u1, u2, cout


def _key_penalty(first, r0, kg):
    col = lax.broadcasted_iota(jnp.int32, (1, kg), 1)
    limit = jnp.where(first, TQ - r0, 0)
    return jnp.where(col < limit, NEG_INF, 0.0)


def fwd_mix(x, proj, bias2, wconv_t, g_co, g_ao, g_pm, gm, wout_all):
    t = x.shape[0]
    qg, kg = QG_FWD, QG_FWD + LEFT

    def body(x_ref, pc_ref, pcp_ref, q_ref, kp_ref, kc_ref, vp_ref, vc_ref, b2_ref, wc_ref, gco_ref, gao_ref, gpm_ref,
             gm_ref, wout_hbm, xmid_ref, o_ref, lse_ref, y_ref, z_ref, wout_v, kwin, vwin, cscr, sems):
        i = pl.program_id(0)
        first = i == 0
        wout = _Resident(wout_hbm, wout_v, sems.at[0])
        kwin[0:TQ, :] = kp_ref[...]
        kwin[TQ:2 * TQ, :] = kc_ref[...]
        vwin[0:TQ, :] = vp_ref[...]
        vwin[TQ:2 * TQ, :] = vc_ref[...]
        qmask = _head_masks(HD ** -0.5)
        low = lax.broadcasted_iota(jnp.int32, (1, LANES), 1) < HD

        def group(g, carry):
            r0 = pl.multiple_of(g * qg, qg)
            pen = _key_penalty(first, r0, kg)
            for hp in range(NH // 2):
                ls = slice(LANES * hp, LANES * (hp + 1))
                qb = q_ref[pl.ds(r0, qg), ls]
                q2 = jnp.concatenate([qb * qmask[0], qb * qmask[1]], axis=0)
                s = lax.dot_general(q2, kwin[pl.ds(r0, kg), ls], NT, preferred_element_type=F32)
                s = s + b2_ref[hp] + pen
                m = jnp.max(s, axis=-1, keepdims=True)
                p = jnp.exp(s - m)
                l = jnp.sum(p, axis=-1, keepdims=True)
                o2 = jnp.dot(p.astype(BF16), vwin[pl.ds(r0, kg), ls], preferred_element_type=F32) * (1.0 / l)
                lse2 = m + jnp.log(l)
                o_ref[pl.ds(r0, qg), ls] = jnp.where(low, o2[:qg], o2[qg:])
                lse_ref[pl.ds(r0, qg), ls] = jnp.where(low, lse2[:qg], lse2[qg:])
            return carry

        lax.fori_loop(0, TQ // qg, group, 0)

        _, bg, _, _, _, _, cout = _conv_fwd(pc_ref, pcp_ref, wc_ref, cscr, first)
        yc = bg * cout
        gmv = gm_ref[...]
        ycn = yc * lax.rsqrt(_group_mean(yc * yc, gmv) + EPS) * gco_ref[...]
        oa = o_ref[...]
        oan = oa * lax.rsqrt(_group_mean(oa * oa, gmv) + EPS) * gao_ref[...]
        y_ref[:, 0:CW] = ycn.astype(BF16)
        y_ref[:, CW:2 * CW] = oan.astype(BF16)
        z = jnp.dot(y_ref[...], wout.read(), preferred_element_type=F32)
        z_ref[...] = z
        xmid_ref[...] = x_ref[...] + _rms(z, gpm_ref[...])

    row = lambda w: pl.BlockSpec((TQ, w), lambda i: (i, 0))
    return pl.pallas_call(
        body, grid=(t // TQ,),
        in_specs=[row(D)] + _conv_specs() + _attn_window_specs() + [
            _const((NH // 2, 2 * qg, kg)), _const((8, CW)), _const((1, CW)), _const((1, CW)), _const((1, D)),
            _const((CW, CW)), _any()],
        out_specs=[row(D), row(CW), row(CW), row(D), row(D)],
        out_shape=[jax.ShapeDtypeStruct((t, D), F32), jax.ShapeDtypeStruct((t, CW), F32),
                   jax.ShapeDtypeStruct((t, CW), F32), jax.ShapeDtypeStruct((t, D), BF16),
                   jax.ShapeDtypeStruct((t, D), F32)],
        scratch_shapes=[pltpu.VMEM((D, D), BF16), pltpu.VMEM((2 * TQ, CW), BF16), pltpu.VMEM((2 * TQ, CW), BF16),
                        pltpu.VMEM((TQ + 16, CW), F32), pltpu.SemaphoreType.DMA((1,))],
        compiler_params=_cp(("arbitrary",)), name="fwd_mix",
    )(x, proj, proj, proj, proj, proj, proj, proj, bias2, wconv_t, g_co, g_ao, g_pm, gm, wout_all)


def fwd_ffn(xmid, g_pre, g_post, wfi_all, wfo_all, after=(), target=None):
    t = xmid.shape[0]
    n_in = 5 if target is None else 6

    def body(*refs):
        x_ref, gpre_ref, gpost_ref, wfi_hbm, wfo_hbm = refs[:5]
        t_ref = None if target is None else refs[5]
        gu_ref, f_ref, xo_ref = refs[n_in:n_in + 3]
        l_ref = None if target is None else refs[n_in + 3]
        wfi_v, wfo_v, sems = refs[-3:]

        if target is not None:
            @pl.when(pl.program_id(0) == 0)
            def _():
                l_ref[...] = jnp.zeros_like(l_ref)

        def step(ready):
            xv = x_ref[...]
            h = _rms(xv, gpre_ref[...]).astype(BF16)
            f = jnp.zeros((TM, D), F32)
            for ci, (a, b) in enumerate(FF_CHUNKS):
                ready(0, ci)
                gate = jnp.dot(h, wfi_v[0, :, a:b], preferred_element_type=F32)
                ready(1, ci)
                up = jnp.dot(h, wfi_v[1, :, a:b], preferred_element_type=F32)
                gu_ref[:, a:b] = gate.astype(BF16)
                gu_ref[:, DFF + a:DFF + b] = up.astype(BF16)
                act = gate * _sigmoid(gate) * up
                ready(2, ci)
                f = f + jnp.dot(act.astype(BF16), wfo_v[a:b, :], preferred_element_type=F32)
            f_ref[...] = f
            xo = xv + _rms(f, gpost_ref[...])
            if target is None:
                xo_ref[...] = xo
            else:
                e = xo - t_ref[...]
                xo_ref[...] = e * (1.0 / D)
                rows = jnp.sum(e * e, axis=-1, keepdims=True) * (1.0 / D)
                l_ref[...] += 0.5 * jnp.sum(rows, axis=0, keepdims=True)

        _stream_ffn_weights(wfi_hbm, wfo_hbm, wfi_v, wfo_v, sems, (0, 1, 2), step)

    row = lambda w: pl.BlockSpec((TM, w), lambda i: (i, 0))
    with_loss = target is not None
    return pl.pallas_call(
        _behind(body, n_in, after), grid=(t // TM,),
        in_specs=[row(D), _const((1, D)), _const((1, D)), _any(), _any()] + [row(D)] * with_loss
        + [_any()] * len(after),
        out_specs=[row(2 * DFF), row(D), row(D)] + [_const((8, LANES))] * with_loss,
        out_shape=[jax.ShapeDtypeStruct((t, 2 * DFF), BF16), jax.ShapeDtypeStruct((t, D), F32),
                   jax.ShapeDtypeStruct((t, D), F32)] + [jax.ShapeDtypeStruct((8, LANES), F32)] * with_loss,
        scratch_shapes=[pltpu.VMEM((2, D, DFF), BF16), pltpu.VMEM((DFF, D), BF16), pltpu.SemaphoreType.DMA((6,))],
        compiler_params=_cp(("arbitrary",)), name="fwd_ffn_loss" if with_loss else "fwd_ffn",
    )(xmid, g_pre, g_post, wfi_all, wfo_all, *([target] * with_loss), *after)


def bwd_ffn(dx, f, xmid, gu, g_pre, g_post, wfi_all, wfo_all, after=()):
    t = dx.shape[0]

    def body(dx_ref, f_ref, x_ref, gu_ref, gpre_ref, gpost_ref, wfi_hbm, wfo_hbm,
             dxm_ref, df_ref, act_ref, dgu_ref, h_ref, dgpost_ref, dgpre_ref, wfi_v, wfo_v, sems):
        @pl.when(pl.program_id(0) == 0)
        def _():
            dgpost_ref[...] = jnp.zeros_like(dgpost_ref)
            dgpre_ref[...] = jnp.zeros_like(dgpre_ref)

        def step(ready):
            dxo = dx_ref[...]
            df, dgp = _rms_bwd(dxo, f_ref[...], gpost_ref[...])
            dgpost_ref[...] += dgp
            dfb = df.astype(BF16)
            df_ref[...] = dfb
            dh = jnp.zeros((TM, D), F32)
            for ci, (a, b) in enumerate(FF_CHUNKS):
                ready(2, ci)
                dact = lax.dot_general(dfb, wfo_v[a:b, :], NT, preferred_element_type=F32)
                gate = gu_ref[:, a:b].astype(F32)
                up = gu_ref[:, DFF + a:DFF + b].astype(F32)
                sig = _sigmoid(gate)
                silu = gate * sig
                act_ref[:, a:b] = (silu * up).astype(BF16)
                dup = (dact * silu).astype(BF16)
                dgate = (dact * up * (sig * (1.0 + gate - silu))).astype(BF16)
                dgu_ref[:, a:b] = dgate
                dgu_ref[:, DFF + a:DFF + b] = dup
                ready(0, ci)
                dh = dh + lax.dot_general(dgate, wfi_v[0, :, a:b], NT, preferred_element_type=F32)
                ready(1, ci)
                dh = dh + lax.dot_general(dup, wfi_v[1, :, a:b], NT, preferred_element_type=F32)
            xv = x_ref[...]
            gpre = gpre_ref[...]
            h_ref[...] = _rms(xv, gpre).astype(BF16)
            dxv, dgq = _rms_bwd(dh, xv, gpre)
            dgpre_ref[...] += dgq
            dxm_ref[...] = dxo + dxv

        _stream_ffn_weights(wfi_hbm, wfo_hbm, wfi_v, wfo_v, sems, (2, 0, 1), step)

    row = lambda w: pl.BlockSpec((TM, w), lambda i: (i, 0))
    return pl.pallas_call(
        _behind(body, 8, after), grid=(t // TM,),
        in_specs=[row(D), row(D), row(D), row(2 * DFF), _const((1, D)), _const((1, D)), _any(), _any()]
        + [_any()] * len(after),
        out_specs=[row(D), row(D), row(DFF), row(2 * DFF), row(D), _const((1, D)), _const((1, D))],
        out_shape=[jax.ShapeDtypeStruct((t, D), F32), jax.ShapeDtypeStruct((t, D), BF16),
                   jax.ShapeDtypeStruct((t, DFF), BF16), jax.ShapeDtypeStruct((t, 2 * DFF), BF16),
                   jax.ShapeDtypeStruct((t, D), BF16), jax.ShapeDtypeStruct((1, D), F32),
                   jax.ShapeDtypeStruct((1, D), F32)],
        scratch_shapes=[pltpu.VMEM((2, D, DFF), BF16), pltpu.VMEM((DFF, D), BF16), pltpu.SemaphoreType.DMA((6,))],
        compiler_params=_cp(("arbitrary",)), name="bwd_ffn")(dx, f, xmid, gu, g_pre, g_post, wfi_all, wfo_all, *after)


def bwd_mix(dxm, z, o, y, proj, wconv_t, g_co, g_ao, g_pm, gm, wout_all, after=()):
    t = dxm.shape[0]

    def body(dx_ref, z_ref, o_ref, y_ref, pc_ref, pcp_ref, wc_ref, gco_ref, gao_ref, gpm_ref, gm_ref, wout_hbm,
             dwo_ref, do_ref, dco_ref, dbg_ref, dgpm_ref, dgco_ref, dgao_ref, wout_v, cscr):
        first = pl.program_id(0) == 0

        @pl.when(first)
        def _():
            pltpu.sync_copy(wout_hbm, wout_v)
            dwo_ref[...] = jnp.zeros_like(dwo_ref)
            dgpm_ref[...] = jnp.zeros_like(dgpm_ref)
            dgco_ref[...] = jnp.zeros_like(dgco_ref)
            dgao_ref[...] = jnp.zeros_like(dgao_ref)

        dz, dgp = _rms_bwd(dx_ref[...], z_ref[...], gpm_ref[...])
        dgpm_ref[...] += dgp
        dzb = dz.astype(BF16)
        dwo_ref[...] += lax.dot_general(y_ref[...], dzb, TN, preferred_element_type=F32)
        gmv = gm_ref[...]
        _, bg, _, _, _, _, cout = _conv_fwd(pc_ref, pcp_ref, wc_ref, cscr, first)
        dy_conv = lax.dot_general(dzb, wout_v[0:CW, :], NT, preferred_element_type=F32)
        dyc, dgc = _group_rms_bwd(dy_conv, bg * cout, gco_ref[...], gmv)
        dgco_ref[...] += dgc
        dbg_ref[...] = (dyc * cout).astype(BF16)
        dco_ref[...] = dyc * bg
        dy_attn = lax.dot_general(dzb, wout_v[CW:2 * CW, :], NT, preferred_element_type=F32)
        do, dga = _group_rms_bwd(dy_attn, o_ref[...], gao_ref[...], gmv)
        dgao_ref[...] += dga
        do_ref[...] = do.astype(BF16)

    row = lambda w: pl.BlockSpec((TQ, w), lambda i: (i, 0))
    return pl.pallas_call(
        _behind(body, 12, after), grid=(t // TQ,),
        in_specs=[row(D), row(D), row(CW), row(D)] + _conv_specs() + [
            _const((8, CW)), _const((1, CW)), _const((1, CW)), _const((1, D)), _const((CW, CW)), _any()]
        + [_any()] * len(after),
        out_specs=[_const((D, D)), row(CW), row(CW), row(CW), _const((1, D)), _const((1, CW)), _const((1, CW))],
        out_shape=[jax.ShapeDtypeStruct((D, D), F32), jax.ShapeDtypeStruct((t, CW), BF16),
                   jax.ShapeDtypeStruct((t, CW), F32), jax.ShapeDtypeStruct((t, CW), BF16),
                   jax.ShapeDtypeStruct((1, D), F32), jax.ShapeDtypeStruct((1, CW), F32),
                   jax.ShapeDtypeStruct((1, CW), F32)],
        scratch_shapes=[pltpu.VMEM((D, D), BF16), pltpu.VMEM((TQ + 16, CW), F32)],
        compiler_params=_cp(("arbitrary",)), name="bwd_mix",
    )(dxm, z, o, y, proj, proj, wconv_t, g_co, g_ao, g_pm, gm, wout_all, *after)


def bwd_conv(dco, proj, wconv_t, after=()):
    t = dco.shape[0]
    nt = t // TQ

    def body(d_ref, dn_ref, pc_ref, pcp_ref, wc_ref, dhc_ref, dcg_ref, dw_ref, cscr, dscr):
        i = pl.program_id(0)
        first = i == 0

        @pl.when(first)
        def _():
            dw_ref[...] = jnp.zeros_like(dw_ref)

        hc, _, cg, u, u1, u2, _ = _conv_fwd(pc_ref, pcp_ref, wc_ref, cscr, first)
        d0 = d_ref[...]
        dscr[0:TQ, :] = d0
        dscr[TQ:TQ + 8, :] = jnp.where(i == nt - 1, 0.0, dn_ref[...])
        d1 = dscr[1:TQ + 1, :]
        d2 = dscr[2:TQ + 2, :]
        du = wc_ref[2:3, :] * d0 + wc_ref[1:2, :] * d1 + wc_ref[0:1, :] * d2
        dhc_ref[...] = (du * cg).astype(BF16)
        dcg_ref[...] = (du * hc).astype(BF16)
        dw_ref[0:1, :] += jnp.sum(d0 * u2, axis=0, keepdims=True)
        dw_ref[1:2, :] += jnp.sum(d0 * u1, axis=0, keepdims=True)
        dw_ref[2:3, :] += jnp.sum(d0 * u, axis=0, keepdims=True)

    row = lambda w: pl.BlockSpec((TQ, w), lambda i: (i, 0))
    nxt = pl.BlockSpec((8, CW), lambda i: (jnp.minimum((i + 1) * (TQ // 8), t // 8 - 1), 0))
    return pl.pallas_call(
        _behind(body, 5, after), grid=(nt,),
        in_specs=[row(CW), nxt] + _conv_specs() + [_const((8, CW))] + [_any()] * len(after),
        out_specs=[row(CW), row(CW), _const((8, CW))],
        out_shape=[jax.ShapeDtypeStruct((t, CW), BF16), jax.ShapeDtypeStruct((t, CW), BF16),
                   jax.ShapeDtypeStruct((8, CW), F32)],
        scratch_shapes=[pltpu.VMEM((TQ + 16, CW), F32), pltpu.VMEM((TQ + 8, CW), F32)],
        compiler_params=_cp(("arbitrary",)), name="bwd_conv")(dco, dco, proj, proj, wconv_t, *after)


def bwd_attn(proj, o, do, lse, bias2):
    t = o.shape[0]
    nt = t // TQ
    qg, kg = QG_BWD, QG_BWD + LEFT
    nkb = (t + TQ) // LANES

    def body(q_ref, kp_ref, kc_ref, vp_ref, vc_ref, o_ref, do_ref, lse_ref, b2_ref,
             dq_ref, dk_hbm, dv_hbm, db_hbm, kwin, vwin, dk_acc, dv_acc, db_acc):
        i = pl.program_id(0)
        first = i == 0

        @pl.when(first)
        def _():
            dk_acc[...] = jnp.zeros_like(dk_acc)
            dv_acc[...] = jnp.zeros_like(dv_acc)
            db_acc[...] = jnp.zeros_like(db_acc)

        kwin[0:TQ, :] = kp_ref[...]
        kwin[TQ:2 * TQ, :] = kc_ref[...]
        vwin[0:TQ, :] = vp_ref[...]
        vwin[TQ:2 * TQ, :] = vc_ref[...]
        scale = HD ** -0.5
        qmask = _head_masks(scale)
        vmask = _head_masks(1.0)
        low = lax.broadcasted_iota(jnp.int32, (1, LANES), 1) < HD

        def group(g, carry):
            r0 = pl.multiple_of(g * qg, qg)
            base = i * (TQ // LANES) + g * (qg // LANES)
            pen = _key_penalty(first, r0, kg)
            for hp in range(NH // 2):
                ls = slice(LANES * hp, LANES * (hp + 1))
                qb = q_ref[pl.ds(r0, qg), ls]
                kw = kwin[pl.ds(r0, kg), ls]
                dob = do_ref[pl.ds(r0, qg), ls]
                prod = dob.astype(F32) * o_ref[pl.ds(r0, qg), ls]
                lseb = lse_ref[pl.ds(r0, qg), ls]
                q2 = jnp.concatenate([qb * qmask[0], qb * qmask[1]], axis=0)
                do2 = jnp.concatenate([dob * vmask[0], dob * vmask[1]], axis=0)
                lse2 = jnp.concatenate([lseb[:, 0:1], lseb[:, HD:HD + 1]], axis=0)
                dsum = jnp.concatenate([jnp.sum(jnp.where(low, prod, 0.0), axis=-1, keepdims=True),
                                        jnp.sum(jnp.where(low, 0.0, prod), axis=-1, keepdims=True)], axis=0)
                s = lax.dot_general(q2, kw, NT, preferred_element_type=F32) + b2_ref[hp] + pen
                p = jnp.exp(s - lse2)
                dp = lax.dot_general(do2, vwin[pl.ds(r0, kg), ls], NT, preferred_element_type=F32)
                ds = p * (dp - dsum)
                db_acc[hp] += ds
                dsb = ds.astype(BF16)
                dq2 = jnp.dot(dsb, kw, preferred_element_type=F32)
                dq_ref[pl.ds(r0, qg), ls] = (jnp.where(low, dq2[:qg], dq2[qg:]) * scale).astype(BF16)
                dkt = lax.dot_general(q2, dsb, TN, preferred_element_type=F32)
                dvt = lax.dot_general(do2, p.astype(BF16), TN, preferred_element_type=F32)
                for kb in range(kg // LANES):
                    dk_acc[base + kb, ls, :] += dkt[:, LANES * kb:LANES * (kb + 1)]
                    dv_acc[base + kb, ls, :] += dvt[:, LANES * kb:LANES * (kb + 1)]
            return carry

        lax.fori_loop(0, TQ // qg, group, 0)

        @pl.when(i == nt - 1)
        def _():
            pltpu.sync_copy(dk_acc, dk_hbm)
            pltpu.sync_copy(dv_acc, dv_hbm)
            pltpu.sync_copy(db_acc, db_hbm)

    row = lambda w: pl.BlockSpec((TQ, w), lambda i: (i, 0))
    return pl.pallas_call(
        body, grid=(nt,),
        in_specs=_attn_window_specs() + [row(CW), row(CW), row(CW), _const((NH // 2, 2 * qg, kg))],
        out_specs=[row(CW), _any(), _any(), _any()],
        out_shape=[jax.ShapeDtypeStruct((t, CW), BF16), jax.ShapeDtypeStruct((nkb, CW, LANES), F32),
                   jax.ShapeDtypeStruct((nkb, CW, LANES), F32), jax.ShapeDtypeStruct((NH // 2, 2 * qg, kg), F32)],
        scratch_shapes=[pltpu.VMEM((2 * TQ, CW), BF16), pltpu.VMEM((2 * TQ, CW), BF16),
                        pltpu.VMEM((nkb, CW, LANES), F32), pltpu.VMEM((nkb, CW, LANES), F32),
                        pltpu.VMEM((NH // 2, 2 * qg, kg), F32)],
        compiler_params=_cp(("arbitrary",)), name="bwd_attn",
    )(proj, proj, proj, proj, proj, o, do, lse, bias2)


def bwd_inproj(dxm, x, dhc, dbg, dcg, dq, dk, dv, g, w_all):
    t = x.shape[0]
    nt = t // TQ
    wc = PROJ // NCHIP

    def body(dxm_ref, x_ref, dhc_ref, dbg_ref, dcg_ref, dq_ref, dk_ref, dv_ref, g_ref, w_hbm,
             dx_ref, dw_hbm, dg_ref, w_v, dp_ref, dw_acc):
        @pl.when(pl.program_id(0) == 0)
        def _():
            pltpu.sync_copy(w_hbm, w_v)
            dg_ref[...] = jnp.zeros_like(dg_ref)
            dw_acc[...] = jnp.zeros_like(dw_acc)

        dp_ref[:, 0:CW] = dhc_ref[...]
        dp_ref[:, CW:2 * CW] = dbg_ref[...]
        dp_ref[:, 2 * CW:3 * CW] = dcg_ref[...]
        dp_ref[:, 3 * CW:4 * CW] = dq_ref[...]
        for kb in range(TQ // LANES):
            rows = slice(LANES * kb, LANES * (kb + 1))
            dp_ref[rows, 4 * CW:5 * CW] = jnp.transpose(dk_ref[kb]).astype(BF16)
            dp_ref[rows, 5 * CW:6 * CW] = jnp.transpose(dv_ref[kb]).astype(BF16)
        dh = jnp.zeros((TQ, D), F32)
        for b in range(NCHIP):
            dh = dh + lax.dot_general(dp_ref[:, wc * b:wc * (b + 1)], w_v[b], NT, preferred_element_type=F32)
        xv = x_ref[...]
        gv = g_ref[...]
        hb = _rms(xv, gv).astype(BF16)
        for b in range(NCHIP):
            dw_acc[b] += lax.dot_general(hb, dp_ref[:, wc * b:wc * (b + 1)], TN, preferred_element_type=F32)
        dxv, dgv = _rms_bwd(dh, xv, gv)
        dg_ref[...] += dgv
        dx_ref[...] = dxm_ref[...] + dxv

        @pl.when(pl.program_id(0) == nt - 1)
        def _():
            pltpu.sync_copy(dw_acc, dw_hbm)

    row = lambda w: pl.BlockSpec((TQ, w), lambda i: (i, 0))
    pad = pl.BlockSpec((TQ // LANES, CW, LANES), lambda i: (i + 1, 0, 0))
    return pl.pallas_call(
        body, grid=(nt,),
        in_specs=[row(D), row(D), row(CW), row(CW), row(CW), row(CW), pad, pad, _const((1, D)), _any()],
        out_specs=[row(D), _any(), _const((1, D))],
        out_shape=[jax.ShapeDtypeStruct((t, D), F32), jax.ShapeDtypeStruct((NCHIP, D, wc), F32),
                   jax.ShapeDtypeStruct((1, D), F32)],
        scratch_shapes=[pltpu.VMEM((NCHIP, D, wc), BF16), pltpu.VMEM((TQ, PROJ), BF16),
                        pltpu.VMEM((NCHIP, D, wc), F32)],
        compiler_params=_cp(("arbitrary",)), name="bwd_inproj",
    )(dxm, x, dhc, dbg, dcg, dq, dk, dv, g, w_all)


def wgrad(a, b, kb, nb, by_columns, name):
    t, k = a.shape
    n = b.shape[1]
    tk = 512

    def body(a_ref, b_ref, o_ref):
        o_ref[...] = jnp.zeros_like(o_ref)
        for c in range(t // tk):
            o_ref[...] += lax.dot_general(a_ref[tk * c:tk * (c + 1), :], b_ref[tk * c:tk * (c + 1), :], TN,
                                          preferred_element_type=F32)

    if by_columns:
        assert nb == n // NCHIP
        out_spec = pl.BlockSpec((None, kb, nb), lambda ki, ni: (ni, ki, 0))
        out_shape = jax.ShapeDtypeStruct((NCHIP, k, nb), F32)
    else:
        assert nb == n
        out_spec = pl.BlockSpec((kb, nb), lambda ki, ni: (ki, 0))
        out_shape = jax.ShapeDtypeStruct((k, n), F32)
    return pl.pallas_call(
        body, grid=(k // kb, n // nb),
        in_specs=[pl.BlockSpec((t, kb), lambda ki, ni: (0, ki)), pl.BlockSpec((t, nb), lambda ki, ni: (0, ni))],
        out_specs=out_spec, out_shape=out_shape,
        compiler_params=_cp(("arbitrary", "arbitrary")), name=name)(a, b)


TOE = 1024
assert 2 * QG_FWD + LEFT <= TOE
N_FLAT = LEFT - REL_CLIP + 1
N_VAR = BAND - N_FLAT


def _diag_vector(table):
    last = table[:, 2 * REL_CLIP:]
    var = table[:, 2 * REL_CLIP - N_VAR:2 * REL_CLIP][:, ::-1]
    return jnp.concatenate([jnp.broadcast_to(last, (NH, N_FLAT)), var, jnp.broadcast_to(last, (NH, TOE - BAND))], axis=1)


def _diag_vector_bwd(dvec):
    dlast = jnp.sum(dvec[:, :N_FLAT], axis=1, keepdims=True) + jnp.sum(dvec[:, BAND:], axis=1, keepdims=True)
    dvar = dvec[:, N_FLAT:BAND][:, ::-1]
    return jnp.concatenate([jnp.zeros((NH, 2 * REL_CLIP - N_VAR), F32), dvar, dlast], axis=1)


def _band_valid(qg):
    r = lax.broadcasted_iota(jnp.int32, (qg, qg + LEFT), 0)
    p = lax.broadcasted_iota(jnp.int32, (qg, qg + LEFT), 1)
    start = lax.shift_left(lax.shift_right_logical(r, 6), 6)
    return (p >= start) & (p < start + BAND)


def bias_expand(vec, qgs, after=()):
    def body(v_ref, *o_refs):
        for qg, o_ref in zip(qgs, o_refs):
            valid = _band_valid(qg)
            for h in range(NH):
                rows = jnp.broadcast_to(v_ref[h:h + 1, :], (qg, TOE))
                toe = pltpu.roll(rows, 0, 1, stride=1, stride_axis=0)
                o_ref[h // 2, qg * (h % 2):qg * (h % 2 + 1), :] = jnp.where(valid, toe[:, :qg + LEFT], NEG_INF)

    vm = pl.BlockSpec(memory_space=pltpu.VMEM)
    return pl.pallas_call(_behind(body, 1, after), in_specs=[vm] + [_any()] * len(after), out_specs=[vm] * len(qgs),
                          out_shape=[jax.ShapeDtypeStruct((NH // 2, 2 * qg, qg + LEFT), F32) for qg in qgs],
                          name="bias_expand")(vec, *after)


def bias_reduce(db2):
    _, qg, kg = db2.shape

    def body(d_ref, o_ref):
        ii = lax.broadcasted_iota(jnp.int32, (kg, kg), 0)
        jj = lax.broadcasted_iota(jnp.int32, (kg, kg), 1)
        flip = jnp.where(ii + jj == kg - 1, 1.0, 0.0).astype(BF16)
        for h in range(NH):
            rest = d_ref[h]
            rev = jnp.zeros((qg, kg), F32)
            for _ in range(3):
                term = rest.astype(BF16)
                rev = rev + jnp.dot(term, flip, preferred_element_type=F32)
                rest = rest - term.astype(F32)
            d = jnp.concatenate([jnp.zeros((qg, TOE - kg), F32), rev], axis=1)
            back = pltpu.roll(d, 0, 1, stride=1, stride_axis=0)
            o_ref[h:h + 1, :] = jnp.sum(back, axis=0, keepdims=True)

    rev = pl.pallas_call(body, out_shape=jax.ShapeDtypeStruct((NH, TOE), F32), name="bias_reduce")(db2)
    return rev[:, ::-1]


def _place():
    x, y, c = lax.axis_index("x"), lax.axis_index("y"), lax.axis_index("c")
    chips = [(1 - x, y), (x, 1 - y), (1 - x, 1 - y)]
    return x, y, c, chips


def _half(ref_rows, c):
    return pl.ds(c * (ref_rows // 2), ref_rows // 2)


HBM_SPEC = pl.BlockSpec(memory_space=pltpu.HBM)
SEM_SPEC = pl.BlockSpec(memory_space=pltpu.SEMAPHORE)
IN_FLIGHT = pltpu.CompilerParams(has_side_effects=pltpu.SideEffectType.DATAFLOW_SIDE_EFFECTING)


def _in_hbm(a):
    return pltpu.with_memory_space_constraint(a, pltpu.HBM)


def cast_to_slot(ws, chip, layer, after=()):
    n = len(ws)
    steps = 4

    def body(b_ref, *refs):
        del b_ref
        for w_ref, o_ref in zip(refs[:n], refs[n + len(after):]):
            o_ref[...] = w_ref[...].astype(BF16)

    grid_spec = pltpu.PrefetchScalarGridSpec(
        num_scalar_prefetch=1, grid=(steps,),
        in_specs=[pl.BlockSpec((None, w.shape[1] // steps, w.shape[2]), lambda r, b: (layer, r, 0)) for w in ws]
        + [_any()] * len(after),
        out_specs=[pl.BlockSpec((None, w.shape[1] // steps, w.shape[2]), lambda r, b: (b[0], r, 0)) for w in ws])
    return pl.pallas_call(body, grid_spec=grid_spec,
                          out_shape=[jax.ShapeDtypeStruct((NCHIP,) + w.shape[1:], BF16) for w in ws],
                          compiler_params=_cp(("arbitrary",)), name="cast_to_slot")(chip, *ws, *after)


def _gather_copies(bufs, send, recv):
    x, y, c, chips = _place()
    b = 2 * x + y
    out = []
    for k, buf in enumerate(bufs):
        rows = buf.shape[1]
        mine = buf.at[b, _half(rows, c), :]
        for j, (cx, cy) in enumerate(chips):
            theirs = buf.at[2 * cx + cy, _half(rows, c), :]
            sems = dict(send_sem=send.at[3 * k + j], recv_sem=recv.at[3 * k + j],
                        device_id=(cx, cy, c), device_id_type=MESH)
            out.append((pltpu.make_async_remote_copy(src_ref=mine, dst_ref=mine, **sems),
                        pltpu.make_async_remote_copy(src_ref=theirs, dst_ref=theirs, **sems)))
    return out


def gather_start(bufs, after, layer):
    n = len(bufs)

    def body(*refs):
        ins = refs[:n]
        send, recv = refs[n + 1], refs[n + 2]
        token = refs[-1]
        for start, _ in _gather_copies(ins, send, recv):
            start.start()
        token[...] = jnp.zeros_like(token)

    sems = pltpu.SemaphoreType.DMA((3 * n,))
    res = pl.pallas_call(
        body, name=f"gather_start_{layer}",
        in_specs=[HBM_SPEC] * n + [_any()],
        out_specs=[SEM_SPEC, SEM_SPEC] + [HBM_SPEC] * n + [pl.BlockSpec(memory_space=pltpu.VMEM)],
        out_shape=[sems, sems] + [pltpu.HBM(b.shape, b.dtype) for b in bufs] + [jax.ShapeDtypeStruct((8, LANES), F32)],
        input_output_aliases={k: 2 + k for k in range(n)}, compiler_params=IN_FLIGHT,
    )(*[_in_hbm(b) for b in bufs], after)
    return res[0], res[1], res[2:2 + n], res[-1]


def gather_wait(send, recv, bufs, after, layer):
    n = len(bufs)

    def body(*refs):
        ins = refs[:n]
        send_ref, recv_ref = refs[n], refs[n + 1]
        for start, arrival in _gather_copies(ins, send_ref, recv_ref):
            start.wait_send()
            arrival.wait_recv()

    return pl.pallas_call(
        body, name=f"gather_wait_{layer}",
        in_specs=[HBM_SPEC] * n + [SEM_SPEC, SEM_SPEC, _any()], out_specs=[HBM_SPEC] * n,
        out_shape=[pltpu.HBM(b.shape, b.dtype) for b in bufs],
        input_output_aliases={k: k for k in range(n)}, compiler_params=IN_FLIGHT,
    )(*bufs, send, recv, after)


def gather_forward(bufs):
    n = len(bufs)

    def body(*refs):
        outs = refs[n:2 * n]
        send, recv = refs[2 * n:]
        x, y, c, chips = _place()
        cps = []
        for k in range(n):
            rows = outs[k].shape[1]
            for j, (cx, cy) in enumerate(chips):
                sems = dict(send_sem=send.at[3 * k + j], recv_sem=recv.at[3 * k + j],
                            device_id=(x, y, 1 - c), device_id_type=MESH)
                mine = outs[k].at[2 * cx + cy, _half(rows, c), :]
                theirs = outs[k].at[2 * cx + cy, _half(rows, 1 - c), :]
                cp = pltpu.make_async_remote_copy(src_ref=mine, dst_ref=mine, **sems)
                cp.start()
                cps.append((cp, pltpu.make_async_remote_copy(src_ref=theirs, dst_ref=theirs, **sems)))
        for cp, arrival in cps:
            cp.wait_send()
            arrival.wait_recv()

    return pl.pallas_call(
        body, in_specs=[_any()] * n, out_specs=[_any()] * n,
        out_shape=[jax.ShapeDtypeStruct(b.shape, b.dtype) for b in bufs], input_output_aliases={k: k for k in range(n)},
        scratch_shapes=[pltpu.SemaphoreType.DMA((3 * n,)), pltpu.SemaphoreType.DMA((3 * n,))],
        name="gather_forward")(*bufs)


def _forward_copies(bufs, send, recv):
    x, y, c, chips = _place()
    out = []
    for k, buf in enumerate(bufs):
        rows = buf.shape[1]
        for j, (cx, cy) in enumerate(chips):
            sems = dict(send_sem=send.at[3 * k + j], recv_sem=recv.at[3 * k + j],
                        device_id=(x, y, 1 - c), device_id_type=MESH)
            mine = buf.at[2 * cx + cy, _half(rows, c), :]
            theirs = buf.at[2 * cx + cy, _half(rows, 1 - c), :]
            out.append((pltpu.make_async_remote_copy(src_ref=mine, dst_ref=mine, **sems),
                        pltpu.make_async_remote_copy(src_ref=theirs, dst_ref=theirs, **sems)))
    return out


def forward_start(bufs, tag):
    n = len(bufs)

    def body(*refs):
        ins = refs[:n]
        send, recv = refs[n], refs[n + 1]
        token = refs[-1]
        for start, _ in _forward_copies(ins, send, recv):
            start.start()
        token[...] = jnp.zeros_like(token)

    sems = pltpu.SemaphoreType.DMA((3 * n,))
    res = pl.pallas_call(
        body, name=f"forward_start_{tag}", in_specs=[HBM_SPEC] * n,
        out_specs=[SEM_SPEC, SEM_SPEC] + [HBM_SPEC] * n + [pl.BlockSpec(memory_space=pltpu.VMEM)],
        out_shape=[sems, sems] + [pltpu.HBM(b.shape, b.dtype) for b in bufs] + [jax.ShapeDtypeStruct((8, LANES), F32)],
        input_output_aliases={k: 2 + k for k in range(n)}, compiler_params=IN_FLIGHT,
    )(*[_in_hbm(b) for b in bufs])
    return res[0], res[1], res[2:2 + n], res[-1]


def forward_wait(send, recv, bufs, after, tag):
    n = len(bufs)

    def body(*refs):
        ins = refs[:n]
        send_ref, recv_ref = refs[n], refs[n + 1]
        for start, arrival in _forward_copies(ins, send_ref, recv_ref):
            start.wait_send()
            arrival.wait_recv()

    return pl.pallas_call(
        body, name=f"forward_wait_{tag}",
        in_specs=[HBM_SPEC] * n + [SEM_SPEC, SEM_SPEC, _any()], out_specs=[HBM_SPEC] * n,
        out_shape=[pltpu.HBM(b.shape, b.dtype) for b in bufs],
        input_output_aliases={k: k for k in range(n)}, compiler_params=IN_FLIGHT,
    )(*bufs, send, recv, after)


def _exchange_copies(srcs, lands, send, recv):
    x, y, c, _ = _place()
    return [pltpu.make_async_remote_copy(
        src_ref=src.at[:, _half(src.shape[1], 1 - c), :], dst_ref=land, send_sem=send.at[k], recv_sem=recv.at[k],
        device_id=(x, y, 1 - c), device_id_type=MESH) for k, (src, land) in enumerate(zip(srcs, lands))]


def exchange_start(srcs, tag):
    n = len(srcs)
    lands = [lax.empty((s.shape[0], s.shape[1] // 2, s.shape[2]), s.dtype) for s in srcs]

    def body(*refs):
        ins, land_refs = refs[:n], refs[n:2 * n]
        send, recv = refs[2 * n], refs[2 * n + 1]
        token = refs[-1]
        for cp in _exchange_copies(ins, land_refs, send, recv):
            cp.start()
        token[...] = jnp.zeros_like(token)

    sems = pltpu.SemaphoreType.DMA((n,))
    res = pl.pallas_call(
        body, name=f"exchange_start_{tag}",
        in_specs=[HBM_SPEC] * (2 * n),
        out_specs=[SEM_SPEC, SEM_SPEC] + [HBM_SPEC] * (2 * n) + [pl.BlockSpec(memory_space=pltpu.VMEM)],
        out_shape=[sems, sems] + [pltpu.HBM(a.shape, a.dtype) for a in list(srcs) + lands]
        + [jax.ShapeDtypeStruct((8, LANES), F32)],
        input_output_aliases={k: 2 + k for k in range(2 * n)}, compiler_params=IN_FLIGHT,
    )(*[_in_hbm(a) for a in list(srcs) + lands])
    return res[0], res[1], res[2:2 + n], res[2 + n:2 + 2 * n], res[-1]


def exchange_wait(send, recv, srcs, lands, after, tag):
    n = len(srcs)

    def body(*refs):
        ins, land_refs = refs[:n], refs[n:2 * n]
        send_ref, recv_ref = refs[2 * n], refs[2 * n + 1]
        for cp in _exchange_copies(ins, land_refs, send_ref, recv_ref):
            cp.wait_send()
            cp.wait_recv()

    res = pl.pallas_call(
        body, name=f"exchange_wait_{tag}",
        in_specs=[HBM_SPEC] * (2 * n) + [SEM_SPEC, SEM_SPEC, _any()], out_specs=[HBM_SPEC] * (2 * n),
        out_shape=[pltpu.HBM(a.shape, a.dtype) for a in list(srcs) + list(lands)],
        input_output_aliases={k: k for k in range(2 * n)}, compiler_params=IN_FLIGHT,
    )(*srcs, *lands, send, recv, after)
    return res[:n], res[n:]


def add_pair(gs, r1s, core):
    n = len(gs)

    def body(c_ref, *refs):
        del c_ref
        for g_ref, r_ref, o_ref in zip(refs[:n], refs[n:2 * n], refs[2 * n:]):
            o_ref[...] = (g_ref[...] + r_ref[...]).astype(BF16)

    blk = lambda r: (None,) + r.shape[1:]
    grid_spec = pltpu.PrefetchScalarGridSpec(
        num_scalar_prefetch=1, grid=(NCHIP,),
        in_specs=[pl.BlockSpec(blk(r), lambda s, c: (s, c[0], 0)) for r in r1s]
        + [pl.BlockSpec(blk(r), lambda s, c: (s, 0, 0)) for r in r1s],
        out_specs=[pl.BlockSpec(blk(r), lambda s, c: (s, 0, 0)) for r in r1s])
    return pl.pallas_call(body, grid_spec=grid_spec, out_shape=[jax.ShapeDtypeStruct(r.shape, BF16) for r in r1s],
                          compiler_params=_cp(("arbitrary",)), name="add_pair")(core, *gs, *r1s)


def _scatter_copies(srcs, lands, send, recv):
    _, _, c, chips = _place()
    out = []
    for k, (src, land) in enumerate(zip(srcs, lands)):
        for j, (cx, cy) in enumerate(chips):
            out.append(pltpu.make_async_remote_copy(
                src_ref=src.at[2 * cx + cy], dst_ref=land.at[j], send_sem=send.at[3 * k + j],
                recv_sem=recv.at[3 * k + j], device_id=(cx, cy, c), device_id_type=MESH))
    return out


def scatter_start(srcs, layer):
    n = len(srcs)
    srcs = list(srcs)
    lands = [lax.empty((3,) + s.shape[1:], s.dtype) for s in srcs]

    def body(*refs):
        ins, land_refs = refs[:n], refs[n:2 * n]
        send, recv = refs[2 * n], refs[2 * n + 1]
        token = refs[-1]
        for cp in _scatter_copies(ins, land_refs, send, recv):
            cp.start()
        token[...] = jnp.zeros_like(token)

    sems = pltpu.SemaphoreType.DMA((3 * n,))
    res = pl.pallas_call(
        body, name=f"scatter_start_{layer}",
        in_specs=[HBM_SPEC] * (2 * n),
        out_specs=[SEM_SPEC, SEM_SPEC] + [HBM_SPEC] * (2 * n) + [pl.BlockSpec(memory_space=pltpu.VMEM)],
        out_shape=[sems, sems] + [pltpu.HBM(a.shape, a.dtype) for a in srcs + lands]
        + [jax.ShapeDtypeStruct((8, LANES), F32)],
        input_output_aliases={k: 2 + k for k in range(2 * n)}, compiler_params=IN_FLIGHT,
    )(*[_in_hbm(a) for a in srcs + lands])
    return res[0], res[1], res[2:2 + n], res[2 + n:2 + 2 * n], res[-1]


def scatter_wait(send, recv, srcs, lands, after, layer):
    n = len(srcs)

    def body(*refs):
        ins, land_refs = refs[:n], refs[n:2 * n]
        send_ref, recv_ref = refs[2 * n], refs[2 * n + 1]
        for cp in _scatter_copies(ins, land_refs, send_ref, recv_ref):
            cp.wait_send()
            cp.wait_recv()

    res = pl.pallas_call(
        body, name=f"scatter_wait_{layer}",
        in_specs=[HBM_SPEC] * (2 * n) + [SEM_SPEC, SEM_SPEC, _any()], out_specs=[HBM_SPEC] * (2 * n),
        out_shape=[pltpu.HBM(a.shape, a.dtype) for a in list(srcs) + list(lands)],
        input_output_aliases={k: k for k in range(2 * n)}, compiler_params=IN_FLIGHT,
    )(*srcs, *lands, send, recv, after)
    return res[n:]


def add_chips(gs, r1s, r2s, place, totals, layer):
    n = len(gs)
    steps = 2

    def body(p_ref, *refs):
        del p_ref
        for g_ref, r1_ref, r2_ref, o_ref in zip(refs[:n], refs[n:2 * n], refs[2 * n:3 * n], refs[4 * n:]):
            own = g_ref[...] + r1_ref[...]
            o_ref[...] = ((own + r2_ref[0].astype(F32)) + r2_ref[1].astype(F32)) + r2_ref[2].astype(F32)

    blk = lambda r: (None, r.shape[1] // steps, r.shape[2])
    grid_spec = pltpu.PrefetchScalarGridSpec(
        num_scalar_prefetch=1, grid=(steps,),
        in_specs=[pl.BlockSpec(blk(r), lambda i, p: (p[1], p[0] * steps + i, 0)) for r in r1s]
        + [pl.BlockSpec(blk(r), lambda i, p: (p[1], i, 0)) for r in r1s]
        + [pl.BlockSpec((3,) + blk(r)[1:], lambda i, p: (0, i, 0)) for r in r1s] + [_any()] * n,
        out_specs=[pl.BlockSpec(blk(r), lambda i, p: (layer, p[0] * steps + i, 0)) for r in r1s])
    return pl.pallas_call(body, grid_spec=grid_spec, out_shape=[jax.ShapeDtypeStruct(t.shape, F32) for t in totals],
                          input_output_aliases={1 + 3 * n + k: k for k in range(n)},
                          compiler_params=_cp(("arbitrary",)), name="add_chips")(place, *gs, *r1s, *r2s, *totals)


def _share_copies(bufs, send, recv):
    x, y, c, _ = _place()
    out = []
    for k, buf in enumerate(bufs):
        sems = dict(send_sem=send.at[k], recv_sem=recv.at[k], device_id=(x, y, 1 - c), device_id_type=MESH)
        mine = buf.at[:, _half(buf.shape[1], c), :]
        theirs = buf.at[:, _half(buf.shape[1], 1 - c), :]
        out.append((pltpu.make_async_remote_copy(src_ref=mine, dst_ref=mine, **sems),
                    pltpu.make_async_remote_copy(src_ref=theirs, dst_ref=theirs, **sems)))
    return out


def share_start(bufs, tag):
    n = len(bufs)

    def body(*refs):
        ins = refs[:n]
        send, recv = refs[n], refs[n + 1]
        token = refs[-1]
        for start, _ in _share_copies(ins, send, recv):
            start.start()
        token[...] = jnp.zeros_like(token)

    sems = pltpu.SemaphoreType.DMA((n,))
    res = pl.pallas_call(
        body, name=f"share_start_{tag}", in_specs=[HBM_SPEC] * n,
        out_specs=[SEM_SPEC, SEM_SPEC] + [HBM_SPEC] * n + [pl.BlockSpec(memory_space=pltpu.VMEM)],
        out_shape=[sems, sems] + [pltpu.HBM(b.shape, b.dtype) for b in bufs] + [jax.ShapeDtypeStruct((8, LANES), F32)],
        input_output_aliases={k: 2 + k for k in range(n)}, compiler_params=IN_FLIGHT,
    )(*[_in_hbm(b) for b in bufs])
    return res[0], res[1], res[2:2 + n], res[-1]


def share_wait(send, recv, bufs, after, tag):
    n = len(bufs)

    def body(*refs):
        ins = refs[:n]
        send_ref, recv_ref = refs[n], refs[n + 1]
        for start, arrival in _share_copies(ins, send_ref, recv_ref):
            start.wait_send()
            arrival.wait_recv()

    return pl.pallas_call(
        body, name=f"share_wait_{tag}",
        in_specs=[HBM_SPEC] * n + [SEM_SPEC, SEM_SPEC, _any()], out_specs=[HBM_SPEC] * n,
        out_shape=[pltpu.HBM(b.shape, b.dtype) for b in bufs],
        input_output_aliases={k: k for k in range(n)}, compiler_params=IN_FLIGHT,
    )(*bufs, send, recv, after)


def small_allreduce(v, after=()):
    rows = v.shape[0]
    flips = [(fx, fy, fc) for fx in (0, 1) for fy in (0, 1) for fc in (0, 1)][1:]

    def body(v_ref, o_ref, buf, send, recv):
        x, y, c, _ = _place()
        buf[4 * x + 2 * y + c] = v_ref[...]
        peers = [(jnp.where(fx, 1 - x, x), jnp.where(fy, 1 - y, y), jnp.where(fc, 1 - c, c)) for fx, fy, fc in flips]
        cps = []
        for k, peer in enumerate(peers):
            cp = pltpu.make_async_remote_copy(
                src_ref=v_ref, dst_ref=buf.at[4 * x + 2 * y + c], send_sem=send.at[k], recv_sem=recv.at[k],
                device_id=peer, device_id_type=MESH)
            cp.start()
            cps.append(cp)
        for k, (px, py, pc) in enumerate(peers):
            pltpu.make_async_remote_copy(
                src_ref=v_ref, dst_ref=buf.at[4 * px + 2 * py + pc], send_sem=send.at[k], recv_sem=recv.at[k],
                device_id=(px, py, pc), device_id_type=MESH).wait_recv()
        for cp in cps:
            cp.wait_send()
        acc = buf[0]
        for s in range(1, 8):
            acc = acc + buf[s]
        o_ref[...] = acc

    vm = pl.BlockSpec(memory_space=pltpu.VMEM)
    return pl.pallas_call(
        _behind(body, 1, after), in_specs=[vm] + [_any()] * len(after), out_specs=vm,
        out_shape=jax.ShapeDtypeStruct((rows, SMALL_COLS), F32),
        scratch_shapes=[pltpu.VMEM((8, rows, SMALL_COLS), F32), pltpu.SemaphoreType.DMA((7,)),
                        pltpu.SemaphoreType.DMA((7,))],
        name="reduce_small")(v, *after)


def adamw(w, g, m, v, rb, name, after=()):
    nl, rows, cols = w.shape

    def body(w_ref, g_ref, m_ref, v_ref, go_ref, d_ref, nm_ref, nv_ref):
        gv = g_ref[...]
        go_ref[...] = gv
        nm = ADAM_B1 * m_ref[...] + (1.0 - ADAM_B1) * gv
        nv = ADAM_B2 * v_ref[...] + (1.0 - ADAM_B2) * (gv * gv)
        m_hat = nm / (1.0 - ADAM_B1 ** ADAM_STEP)
        v_hat = nv / (1.0 - ADAM_B2 ** ADAM_STEP)
        d_ref[...] = -ADAM_LR * (m_hat / (jnp.sqrt(v_hat) + ADAM_EPS) + ADAM_WD * w_ref[...])
        nm_ref[...] = nm
        nv_ref[...] = nv

    blk = pl.BlockSpec((None, rb, cols), lambda l, r: (l, r, 0))
    shp = jax.ShapeDtypeStruct(w.shape, F32)
    return pl.pallas_call(_behind(body, 4, after), grid=(nl, rows // rb), in_specs=[blk] * 4 + [_any()] * len(after),
                          out_specs=[blk] * 4, out_shape=[shp] * 4,
                          compiler_params=_cp(("arbitrary", "arbitrary")), name=name)(w, g, m, v, *after)


def _pack(parts, rows):
    flat = jnp.concatenate([p.reshape(-1).astype(F32) for p in parts])
    return jnp.pad(flat, (0, rows * SMALL_COLS - flat.shape[0])).reshape(rows, SMALL_COLS)


def _unpack(vec, shapes):
    flat = vec.reshape(-1)
    out, off = [], 0
    for s in shapes:
        size = 1
        for d in s:
            size *= d
        out.append(flat[off:off + size].reshape(s))
        off += size
    return out


def kernel(x, w_in, w_conv, rel_bias, g_conv_out, g_attn_out, w_out, g_pre_mix, g_post_mix, g_pre_ffn, g_post_ffn, w_ffn_in, w_ffn_out, loss_target, m_w_in, m_w_conv, m_rel_bias, m_g_conv_out, m_g_attn_out, m_w_out, m_g_pre_mix, m_g_post_mix, m_g_pre_ffn, m_g_post_ffn, m_w_ffn_in, m_w_ffn_out, v_w_in, v_w_conv, v_rel_bias, v_g_conv_out, v_g_attn_out, v_w_out, v_g_pre_mix, v_g_post_mix, v_g_pre_ffn, v_g_post_ffn, v_w_ffn_in, v_w_ffn_out):
    xi, yi, ci = lax.axis_index("x"), lax.axis_index("y"), lax.axis_index("c")
    chip = 2 * xi + yi
    nl = w_in.shape[0]
    x0 = x[0]
    target = loss_target[0]
    cwl = CW // NCHIP

    chip1 = chip.reshape(1).astype(jnp.int32)
    big_weights = [w_in, w_out, w_ffn_in, w_ffn_out]
    own = [cast_to_slot(big_weights, chip1, 0)]
    wc_mine = jnp.pad(w_conv.reshape(-1), (0, 16 * LANES - w_conv.size)).reshape(1, 16, LANES)
    wc_slot = lax.dynamic_update_slice_in_dim(jnp.zeros((NCHIP, 16, LANES), F32), wc_mine, chip, axis=0)
    gm = jnp.kron(jnp.eye(CW // HD, dtype=F32), jnp.full((HD, HD), 1.0 / HD, F32)).astype(BF16)
    row = lambda a, l: a[l][None, :]

    def gather_finish(flight, after, tag):
        send, recv, bufs, _ = flight
        return gather_forward(gather_wait(send, recv, bufs, after, tag))

    first_mix = gather_start(list(own[0][:2]) + [wc_slot], x0, "0m")
    first_ffn = gather_start(own[0][2:], first_mix[3], "0f")
    chain = first_ffn[3]
    biases = []
    for l in range(nl):
        biases.append(bias_expand(_diag_vector(rel_bias[l]), (QG_FWD, QG_BWD), [chain]))
        chain = biases[l][1]
    for l in range(1, nl):
        own.append(cast_to_slot(big_weights, chip1, l, [chain]))
        chain = own[l][0]
    gw_in, gw_out, wc_all = gather_finish(first_mix, chain, "0m")
    wc_full = wc_all.reshape(NCHIP, -1)[:, :nl * cwl * 3].reshape(NCHIP, nl, cwl, 3)
    wc_full = jnp.transpose(wc_full, (1, 0, 2, 3)).reshape(nl, CW, 3)
    wconv_t = jnp.pad(jnp.transpose(wc_full, (0, 2, 1)), ((0, 0), (0, 5), (0, 0)))
    flights, to_sibling = {}, None
    saved, weights = [], []
    h = x0
    for l in range(nl):
        if l == 0:
            pass
        elif l == 1:
            flights[2] = gather_start(own[2], h, 2)
            gw_in, gw_out, gw_fi, gw_fo = gather_finish(flights[l], flights[2][3], l)
        else:
            gw_in, gw_out, gw_fi, gw_fo = forward_wait(*to_sibling[:3], h, l)
        gw_out = gw_out.reshape(D, D)
        behind_mix, behind_ffn = ([first_ffn[3]] if l == 0 else []), []
        if l + 1 < nl and l + 1 not in flights:
            flights[l + 1] = gather_start(own[l + 1], first_ffn[3] if l == 0 else gw_in, l + 1)
            behind_mix.append(flights[l + 1][3])
        bias2, bias2_bwd = biases[l]
        proj = fwd_inproj(h, row(g_pre_mix, l), gw_in, behind_mix)
        xmid, o, lse, y, z = fwd_mix(h, proj, bias2, wconv_t[l], row(g_conv_out, l), row(g_attn_out, l),
                                     row(g_post_mix, l), gm, gw_out)
        if l == 0:
            gw_fi, gw_fo = gather_finish(first_ffn, xmid, "0f")
        elif l + 1 < nl:
            send, recv, bufs, _ = flights[l + 1]
            landed = gather_wait(send, recv, bufs, xmid, l + 1)
            to_sibling = forward_start(landed, l + 1)
            behind_ffn.append(to_sibling[3])
            if l + 2 < nl:
                flights[l + 2] = gather_start(own[l + 2], to_sibling[3], l + 2)
                behind_ffn.append(flights[l + 2][3])
        gw_fo = gw_fo.reshape(2, DFF // 2, D)
        ffn = fwd_ffn(xmid, row(g_pre_ffn, l), row(g_post_ffn, l), gw_fi, gw_fo, behind_ffn,
                      target if l == nl - 1 else None)
        gu, f = ffn[:2]
        saved.append((h, proj, bias2_bwd, xmid, o, lse, y, z, gu, f))
        weights.append((gw_in, gw_out, gw_fi, gw_fo))
        h = ffn[2]
    dx, loss_blk = ffn[2], ffn[3]

    core = ci.reshape(1).astype(jnp.int32)
    place = jnp.stack([ci, chip]).astype(jnp.int32)
    totals = [lax.empty(w.shape, F32) for w in (w_in, w_out, w_ffn_in, w_ffn_out)]
    small = {k: [None] * nl for k in ("co", "ao", "pm", "qm", "pf", "qf", "rel", "wc")}

    def reduce_begin(kinds, grads, tag):
        return kinds, exchange_start(grads, tag), tag

    def reduce_mid(state, after):
        kinds, (send, recv, srcs, lands, _), tag = state
        grads, from_sibling = exchange_wait(send, recv, srcs, lands, after, tag)
        return kinds, grads, from_sibling, scatter_start(add_pair(grads, from_sibling, core), tag), tag

    def reduce_end(state, after, totals, layer):
        kinds, grads, from_sibling, (send, recv, srcs, lands, _), tag = state
        from_chips = scatter_wait(send, recv, srcs, lands, after, tag)
        totals = list(totals)
        summed = add_chips(grads, from_sibling, from_chips, place, [totals[i] for i in kinds], layer)
        for i, t in zip(kinds, summed):
            totals[i] = t
        return totals

    begun = flying = None
    for l in reversed(range(nl)):
        hin, proj, bias2, xmid, o, lse, y, z, gu, f = saved[l]
        gw_in, gw_out, gw_fi, gw_fo = weights[l]
        behind_ffn = [begun[1][4]] if begun is not None else []
        dxm, dfb, act, dgu, h2, dg_qf, dg_pf = bwd_ffn(dx, f, xmid, gu, row(g_pre_ffn, l), row(g_post_ffn, l),
                                                        gw_fi, gw_fo, behind_ffn)
        behind_mix, behind_conv = [], []
        if begun is not None:
            flying = reduce_mid(begun, dxm)
            behind_mix.append(flying[3][4])
        gr_fo = wgrad(act, dfb, 256, D, False, "wgrad_ffn_out").reshape(NCHIP, DFF // NCHIP, D)
        gr_fi = wgrad(h2, dgu, 512, 2 * DFF // NCHIP, True, "wgrad_ffn_in")
        if l == 0:
            begun_ffn = reduce_begin([2, 3], [gr_fi, gr_fo], "0f")
            behind_mix.append(begun_ffn[1][4])
        gr_out, do, dco, dbg, dg_qm, dg_co, dg_ao = bwd_mix(dxm, z, o, y, proj, wconv_t[l], row(g_conv_out, l),
                                                             row(g_attn_out, l), row(g_post_mix, l), gm, gw_out,
                                                             behind_mix)
        gr_out = gr_out.reshape(NCHIP, D // NCHIP, D)
        if l == 0:
            flying_ffn = reduce_mid(begun_ffn, do)
            behind_conv.append(flying_ffn[3][4])
        dhc, dcg, dwc = bwd_conv(dco, proj, wconv_t[l], behind_conv)
        dq, dk, dv, db2 = bwd_attn(proj, o, do, lse, bias2)
        dx, gr_in, dg_pm = bwd_inproj(dxm, hin, dhc, dbg, dcg, dq, dk, dv, row(g_pre_mix, l), gw_in)
        if flying is not None:
            totals = reduce_end(flying, dx, totals, l + 1)
        small["co"][l], small["ao"][l], small["pm"][l], small["qm"][l] = dg_co, dg_ao, dg_pm, dg_qm
        small["pf"][l], small["qf"][l] = dg_pf, dg_qf
        small["rel"][l] = _diag_vector_bwd(bias_reduce(db2.reshape(NH, QG_BWD, QG_BWD + LEFT)))
        small["wc"][l] = jnp.transpose(dwc[0:3], (1, 0))
        if l > 0:
            begun = reduce_begin([0, 1, 2, 3], [gr_in, gr_out, gr_fi, gr_fo], l)
    begun_mix = reduce_begin([0, 1], [gr_in, gr_out], "0m")
    totals = reduce_end(flying_ffn, begun_mix[1][4], totals, 0)
    flying_mix = reduce_mid(begun_mix, totals[2])
    share_ffn = share_start(totals[2:], "ffn")

    order = ("co", "ao", "pm", "qm", "pf", "qf", "rel", "wc")
    parts = [jnp.stack(small[k]) for k in order] + [loss_blk[0:1, 0:1]]
    shapes = [p.shape for p in parts]
    red_vec = small_allreduce(_pack(parts, 40), [share_ffn[3], flying_mix[3][4]])
    red = _unpack(red_vec, shapes)

    gr_fi, gr_fo = share_wait(*share_ffn[:3], red_vec, "ffn")
    big_fi = adamw(w_ffn_in, gr_fi, m_w_ffn_in, v_w_ffn_in, w_ffn_in.shape[1] // 4, "adamw_ffn_in")
    totals = reduce_end(flying_mix, big_fi[1], totals, 0)
    share_mix = share_start(totals[:2], "mix")
    big_fo = adamw(w_ffn_out, gr_fo, m_w_ffn_out, v_w_ffn_out, w_ffn_out.shape[1] // 4, "adamw_ffn_out",
                   [share_mix[3]])
    gr_in, gr_out = share_wait(*share_mix[:3], big_fo[1], "mix")
    big_in = adamw(w_in, gr_in, m_w_in, v_w_in, w_in.shape[1] // 4, "adamw_in")
    big_out = adamw(w_out, gr_out, m_w_out, v_w_out, w_out.shape[1] // 4, "adamw_out")
    big = [big_in, big_out, big_fi, big_fo]
    gr_co, gr_ao, gr_pm, gr_qm, gr_pf, gr_qf, gr_rel, gr_wc_full, loss = red
    gr_co, gr_ao, gr_pm, gr_qm, gr_pf, gr_qf = [a.reshape(nl, -1) for a in (gr_co, gr_ao, gr_pm, gr_qm, gr_pf, gr_qf)]
    gr_wc = lax.dynamic_slice_in_dim(gr_wc_full, chip * cwl, cwl, axis=1)
    loss = loss.reshape(())

    sw = [g_conv_out, g_attn_out, g_pre_mix, g_post_mix, g_pre_ffn, g_post_ffn, rel_bias, w_conv]
    sg = [gr_co, gr_ao, gr_pm, gr_qm, gr_pf, gr_qf, gr_rel, gr_wc]
    sm = [m_g_conv_out, m_g_attn_out, m_g_pre_mix, m_g_post_mix, m_g_pre_ffn, m_g_post_ffn, m_rel_bias, m_w_conv]
    sv = [v_g_conv_out, v_g_attn_out, v_g_pre_mix, v_g_post_mix, v_g_pre_ffn, v_g_post_ffn, v_rel_bias, v_w_conv]
    sshapes = [a.shape for a in sw]
    packed = [_pack(a, 32)[None] for a in (sw, sg, sm, sv)]
    s_out = [_unpack(a[0], sshapes) for a in adamw(*packed, 32, "adamw_small")]

    def leaves(big_i, small_i):
        b_in, b_out, b_fi, b_fo = big_i
        s_co, s_ao, s_pm, s_qm, s_pf, s_qf, s_rel, s_wc = small_i
        return [b_in, s_wc, s_rel, s_co, s_ao, b_out, s_pm, s_qm, s_pf, s_qf, b_fi, b_fo]

    out = [loss, dx[None]]
    out += leaves([b[0] for b in big], sg)
    for i in range(1, 4):
        out += leaves([b[i] for b in big], s_out[i])
    return tuple(out)
```

```python
import jax
import jax.numpy as jnp
from jax import lax
from jax.experimental import pallas as pl
from jax.experimental.pallas import tpu as pltpu

F32 = jnp.float32
BF16 = jnp.bfloat16

D = 1024
PROJ = 3072
CW = 512
HD = 64
NH = 8
CHUNK = 64
BAND = 576
REL_CLIP = 128
NREL = 2 * REL_CLIP + 1
DFF = 2816
DEPTH = 4
NCHIP = 4
EPS = 1e-6
NEG_INF = -1e30

ADAM_LR = 0.001
ADAM_B1 = 0.9
ADAM_B2 = 0.999
ADAM_EPS = 1e-08
ADAM_WD = 0.01
ADAM_STEP = 10

V7X_VMEM_BYTES = 64 * 1024 * 1024
VMEM_LIMIT = V7X_VMEM_BYTES - 8 * 1024 * 1024
LANES = 128
QG_FWD = 4 * CHUNK
QG_BWD = 2 * CHUNK
LEFT = BAND - CHUNK
TQ = 512
TM = 256
SMALL_COLS = 1024
MESH = pl.DeviceIdType.MESH
NT = (((1,), (1,)), ((), ()))
TN = (((0,), (0,)), ((), ()))


def _cp(sem=None, vmem=VMEM_LIMIT):
    return pltpu.CompilerParams(dimension_semantics=sem, vmem_limit_bytes=vmem)


def _any():
    return pl.BlockSpec(memory_space=pl.ANY)


def _const(shape):
    nd = len(shape)
    return pl.BlockSpec(shape, lambda *_: (0,) * nd)


def _behind(body, n_in, after):
    def ordered(*refs):
        return body(*refs[:n_in], *refs[n_in + len(after):])
    return ordered


def _rms(v, g):
    r = lax.rsqrt(jnp.mean(v * v, axis=-1, keepdims=True) + EPS)
    return v * r * g


def _rms_bwd(dy, v, g):
    r = lax.rsqrt(jnp.mean(v * v, axis=-1, keepdims=True) + EPS)
    vh = v * r
    dg = jnp.sum(dy * vh, axis=0, keepdims=True)
    dvh = dy * g
    dv = r * (dvh - vh * jnp.mean(dvh * vh, axis=-1, keepdims=True))
    return dv, dg


def _group_mean(v, gm):
    return jnp.dot(v.astype(BF16), gm, preferred_element_type=F32)


def _group_rms_bwd(dy, v, g, gm):
    r = lax.rsqrt(_group_mean(v * v, gm) + EPS)
    vh = v * r
    dg = jnp.sum(dy * vh, axis=0, keepdims=True)
    dvh = dy * g
    dv = r * (dvh - vh * _group_mean(dvh * vh, gm))
    return dv, dg


def _head_masks(scale):
    lane = lax.broadcasted_iota(jnp.int32, (1, LANES), 1)
    return [jnp.where((lane >= HD * a) & (lane < HD * (a + 1)), scale, 0.0).astype(BF16) for a in range(2)]


class _Resident:
    def __init__(self, src, dst, sem):
        self.first = pl.program_id(0) == 0
        self.copy = pltpu.make_async_copy(src, dst, sem)
        self.dst = dst

        @pl.when(self.first)
        def _():
            self.copy.start()

    def read(self):
        @pl.when(self.first)
        def _():
            self.copy.wait()

        return self.dst[...]


FF_CHUNKS = ((0, 1536), (1536, DFF))


def _stream_ffn_weights(wfi_hbm, wfo_hbm, wfi_v, wfo_v, sems, order, step):
    hw = DFF // 2
    per_matrix = {
        0: [(wfi_hbm.at[j], wfi_v.at[0, :, pl.ds(hw * j, hw)]) for j in range(2)],
        1: [(wfi_hbm.at[2 + j], wfi_v.at[1, :, pl.ds(hw * j, hw)]) for j in range(2)],
        2: [(wfo_hbm.at[j], wfo_v.at[pl.ds(hw * j, hw), :]) for j in range(2)],
    }
    pieces = [p for m in order for p in per_matrix[m]]
    slot = {m: 2 * k for k, m in enumerate(order)}

    def make_step(wait):
        def ready(m, chunk):
            if chunk == 0:
                wait(slot[m])
                wait(slot[m] + 1)
        return lambda: step(ready)

    copies = [pltpu.make_async_copy(src, dst, sems.at[k]) for k, (src, dst) in enumerate(pieces)]
    first = pl.program_id(0) == 0

    @pl.when(first)
    def _():
        for cp in copies:
            cp.start()
        make_step(lambda k: copies[k].wait())()

    @pl.when(jnp.logical_not(first))
    def _():
        make_step(lambda k: None)()


def _conv_taps(u_prev, u, scr):
    n = u.shape[0]
    scr[0:16, :] = u_prev
    scr[16:16 + n, :] = u
    return scr[15:15 + n, :], scr[14:14 + n, :]


def fwd_inproj(x, g, w_all, after=()):
    t = x.shape[0]
    wc = PROJ // NCHIP

    def body(x_ref, g_ref, w_hbm, o_ref, w_v):
        @pl.when(pl.program_id(0) == 0)
        def _():
            pltpu.sync_copy(w_hbm, w_v)

        h = _rms(x_ref[...], g_ref[...]).astype(BF16)
        for b in range(NCHIP):
            o_ref[:, wc * b:wc * (b + 1)] = jnp.dot(h, w_v[b], preferred_element_type=F32).astype(BF16)

    return pl.pallas_call(
        _behind(body, 3, after), grid=(t // TQ,),
        in_specs=[pl.BlockSpec((TQ, D), lambda i: (i, 0)), _const((1, D)), _any()] + [_any()] * len(after),
        out_specs=pl.BlockSpec((TQ, PROJ), lambda i: (i, 0)),
        out_shape=jax.ShapeDtypeStruct((t, PROJ), BF16),
        scratch_shapes=[pltpu.VMEM((NCHIP, D, wc), BF16)],
        compiler_params=_cp(("arbitrary",)), name="fwd_inproj")(x, g, w_all, *after)


def _attn_window_specs():
    return [
        pl.BlockSpec((TQ, CW), lambda i: (i, 3)),
        pl.BlockSpec((TQ, CW), lambda i: (jnp.maximum(i - 1, 0), 4)),
        pl.BlockSpec((TQ, CW), lambda i: (i, 4)),
        pl.BlockSpec((TQ, CW), lambda i: (jnp.maximum(i - 1, 0), 5)),
        pl.BlockSpec((TQ, CW), lambda i: (i, 5)),
    ]


def _conv_specs():
    return [
        pl.BlockSpec((TQ, 3 * CW), lambda i: (i, 0)),
        pl.BlockSpec((16, 3 * CW), lambda i: (jnp.maximum(i * (TQ // 16) - 1, 0), 0)),
    ]


def _conv_fwd(pc_ref, pcp_ref, wc_ref, scr, first):
    pc = pc_ref[...].astype(F32)
    hc, bg, cg = pc[:, :CW], pc[:, CW:2 * CW], pc[:, 2 * CW:]
    u = cg * hc
    pp = pcp_ref[...].astype(F32)
    u_prev = jnp.where(first, 0.0, pp[:, 2 * CW:] * pp[:, :CW])
    u1, u2 = _conv_taps(u_prev, u, scr)
    cout = wc_ref[0:1, :] * u2 + wc_ref[1:2, :] * u1 + wc_ref[2:3, :] * u
    return hc, bg, cg, u, u1, u2, cout


def _key_penalty(first, r0, kg):
    col = lax.broadcasted_iota(jnp.int32, (1, kg), 1)
    limit = jnp.where(first, TQ - r0, 0)
    return jnp.where(col < limit, NEG_INF, 0.0)


def fwd_mix(x, proj, bias2, wconv_t, g_co, g_ao, g_pm, gm, wout_all):
    t = x.shape[0]
    qg, kg = QG_FWD, QG_FWD + LEFT

    def body(x_ref, pc_ref, pcp_ref, q_ref, kp_ref, kc_ref, vp_ref, vc_ref, b2_ref, wc_ref, gco_ref, gao_ref, gpm_ref,
             gm_ref, wout_hbm, xmid_ref, o_ref, lse_ref, y_ref, z_ref, wout_v, kwin, vwin, cscr, sems):
        i = pl.program_id(0)
        first = i == 0
        wout = _Resident(wout_hbm, wout_v, sems.at[0])
        kwin[0:TQ, :] = kp_ref[...]
        kwin[TQ:2 * TQ, :] = kc_ref[...]
        vwin[0:TQ, :] = vp_ref[...]
        vwin[TQ:2 * TQ, :] = vc_ref[...]
        qmask = _head_masks(HD ** -0.5)
        low = lax.broadcasted_iota(jnp.int32, (1, LANES), 1) < HD

        def group(g, carry):
            r0 = pl.multiple_of(g * qg, qg)
            pen = _key_penalty(first, r0, kg)
            for hp in range(NH // 2):
                ls = slice(LANES * hp, LANES * (hp + 1))
                qb = q_ref[pl.ds(r0, qg), ls]
                q2 = jnp.concatenate([qb * qmask[0], qb * qmask[1]], axis=0)
                s = lax.dot_general(q2, kwin[pl.ds(r0, kg), ls], NT, preferred_element_type=F32)
                s = s + b2_ref[hp] + pen
                m = jnp.max(s, axis=-1, keepdims=True)
                p = jnp.exp(s - m)
                l = jnp.sum(p, axis=-1, keepdims=True)
                o2 = jnp.dot(p.astype(BF16), vwin[pl.ds(r0, kg), ls], preferred_element_type=F32) * (1.0 / l)
                lse2 = m + jnp.log(l)
                o_ref[pl.ds(r0, qg), ls] = jnp.where(low, o2[:qg], o2[qg:])
                lse_ref[pl.ds(r0, qg), ls] = jnp.where(low, lse2[:qg], lse2[qg:])
            return carry

        lax.fori_loop(0, TQ // qg, group, 0)

        _, bg, _, _, _, _, cout = _conv_fwd(pc_ref, pcp_ref, wc_ref, cscr, first)
        yc = bg * cout
        gmv = gm_ref[...]
        ycn = yc * lax.rsqrt(_group_mean(yc * yc, gmv) + EPS) * gco_ref[...]
        oa = o_ref[...]
        oan = oa * lax.rsqrt(_group_mean(oa * oa, gmv) + EPS) * gao_ref[...]
        y_ref[:, 0:CW] = ycn.astype(BF16)
        y_ref[:, CW:2 * CW] = oan.astype(BF16)
        z = jnp.dot(y_ref[...], wout.read(), preferred_element_type=F32)
        z_ref[...] = z
        xmid_ref[...] = x_ref[...] + _rms(z, gpm_ref[...])

    row = lambda w: pl.BlockSpec((TQ, w), lambda i: (i, 0))
    return pl.pallas_call(
        body, grid=(t // TQ,),
        in_specs=[row(D)] + _conv_specs() + _attn_window_specs() + [
            _const((NH // 2, 2 * qg, kg)), _const((8, CW)), _const((1, CW)), _const((1, CW)), _const((1, D)),
            _const((CW, CW)), _any()],
        out_specs=[row(D), row(CW), row(CW), row(D), row(D)],
        out_shape=[jax.ShapeDtypeStruct((t, D), F32), jax.ShapeDtypeStruct((t, CW), F32),
                   jax.ShapeDtypeStruct((t, CW), F32), jax.ShapeDtypeStruct((t, D), BF16),
                   jax.ShapeDtypeStruct((t, D), F32)],
        scratch_shapes=[pltpu.VMEM((D, D), BF16), pltpu.VMEM((2 * TQ, CW), BF16), pltpu.VMEM((2 * TQ, CW), BF16),
                        pltpu.VMEM((TQ + 16, CW), F32), pltpu.SemaphoreType.DMA((1,))],
        compiler_params=_cp(("arbitrary",)), name="fwd_mix",
    )(x, proj, proj, proj, proj, proj, proj, proj, bias2, wconv_t, g_co, g_ao, g_pm, gm, wout_all)


def fwd_ffn(xmid, g_pre, g_post, wfi_all, wfo_all, after=(), target=None):
    t = xmid.shape[0]
    n_in = 5 if target is None else 6

    def body(*refs):
        x_ref, gpre_ref, gpost_ref, wfi_hbm, wfo_hbm = refs[:5]
        t_ref = None if target is None else refs[5]
        gu_ref, f_ref, xo_ref = refs[n_in:n_in + 3]
        l_ref = None if target is None else refs[n_in + 3]
        wfi_v, wfo_v, sems = refs[-3:]

        if target is not None:
            @pl.when(pl.program_id(0) == 0)
            def _():
                l_ref[...] = jnp.zeros_like(l_ref)

        def step(ready):
            xv = x_ref[...]
            h = _rms(xv, gpre_ref[...]).astype(BF16)
            f = jnp.zeros((TQ, D), F32)
            for ci, (a, b) in enumerate(FF_CHUNKS):
                ready(0, ci)
                gate = jnp.dot(h, wfi_v[0, :, a:b], preferred_element_type=F32)
                ready(1, ci)
                up = jnp.dot(h, wfi_v[1, :, a:b], preferred_element_type=F32)
                gu_ref[:, a:b] = gate.astype(BF16)
                gu_ref[:, DFF + a:DFF + b] = up.astype(BF16)
                act = gate * (1.0 / (1.0 + jnp.exp(-gate))) * up
                ready(2, ci)
                f = f + jnp.dot(act.astype(BF16), wfo_v[a:b, :], preferred_element_type=F32)
            f_ref[...] = f
            xo = xv + _rms(f, gpost_ref[...])
            if target is None:
                xo_ref[...] = xo
            else:
                e = xo - t_ref[...]
                xo_ref[...] = e * (1.0 / D)
                rows = jnp.sum(e * e, axis=-1, keepdims=True) * (1.0 / D)
                l_ref[...] += 0.5 * jnp.sum(rows, axis=0, keepdims=True)

        _stream_ffn_weights(wfi_hbm, wfo_hbm, wfi_v, wfo_v, sems, (0, 1, 2), step)

    row = lambda w: pl.BlockSpec((TQ, w), lambda i: (i, 0))
    with_loss = target is not None
    return pl.pallas_call(
        _behind(body, n_in, after), grid=(t // TQ,),
        in_specs=[row(D), _const((1, D)), _const((1, D)), _any(), _any()] + [row(D)] * with_loss
        + [_any()] * len(after),
        out_specs=[row(2 * DFF), row(D), row(D)] + [_const((8, LANES))] * with_loss,
        out_shape=[jax.ShapeDtypeStruct((t, 2 * DFF), BF16), jax.ShapeDtypeStruct((t, D), F32),
                   jax.ShapeDtypeStruct((t, D), F32)] + [jax.ShapeDtypeStruct((8, LANES), F32)] * with_loss,
        scratch_shapes=[pltpu.VMEM((2, D, DFF), BF16), pltpu.VMEM((DFF, D), BF16), pltpu.SemaphoreType.DMA((6,))],
        compiler_params=_cp(("arbitrary",)), name="fwd_ffn_loss" if with_loss else "fwd_ffn",
    )(xmid, g_pre, g_post, wfi_all, wfo_all, *([target] * with_loss), *after)


def bwd_ffn(dx, f, xmid, gu, g_pre, g_post, wfi_all, wfo_all, after=()):
    t = dx.shape[0]

    def body(dx_ref, f_ref, x_ref, gu_ref, gpre_ref, gpost_ref, wfi_hbm, wfo_hbm,
             dxm_ref, df_ref, act_ref, dgu_ref, h_ref, dgpost_ref, dgpre_ref, wfi_v, wfo_v, sems):
        @pl.when(pl.program_id(0) == 0)
        def _():
            dgpost_ref[...] = jnp.zeros_like(dgpost_ref)
            dgpre_ref[...] = jnp.zeros_like(dgpre_ref)

        def step(ready):
            dxo = dx_ref[...]
            df, dgp = _rms_bwd(dxo, f_ref[...], gpost_ref[...])
            dgpost_ref[...] += dgp
            dfb = df.astype(BF16)
            df_ref[...] = dfb
            dh = jnp.zeros((TM, D), F32)
            for ci, (a, b) in enumerate(FF_CHUNKS):
                ready(2, ci)
                dact = lax.dot_general(dfb, wfo_v[a:b, :], NT, preferred_element_type=F32)
                gate = gu_ref[:, a:b].astype(F32)
                up = gu_ref[:, DFF + a:DFF + b].astype(F32)
                sig = 1.0 / (1.0 + jnp.exp(-gate))
                silu = gate * sig
                act_ref[:, a:b] = (silu * up).astype(BF16)
                dup = (dact * silu).astype(BF16)
                dgate = (dact * up * (sig * (1.0 + gate * (1.0 - sig)))).astype(BF16)
                dgu_ref[:, a:b] = dgate
                dgu_ref[:, DFF + a:DFF + b] = dup
                ready(0, ci)
                dh = dh + lax.dot_general(dgate, wfi_v[0, :, a:b], NT, preferred_element_type=F32)
                ready(1, ci)
                dh = dh + lax.dot_general(dup, wfi_v[1, :, a:b], NT, preferred_element_type=F32)
            xv = x_ref[...]
            gpre = gpre_ref[...]
            h_ref[...] = _rms(xv, gpre).astype(BF16)
            dxv, dgq = _rms_bwd(dh, xv, gpre)
            dgpre_ref[...] += dgq
            dxm_ref[...] = dxo + dxv

        _stream_ffn_weights(wfi_hbm, wfo_hbm, wfi_v, wfo_v, sems, (2, 0, 1), step)

    row = lambda w: pl.BlockSpec((TM, w), lambda i: (i, 0))
    return pl.pallas_call(
        _behind(body, 8, after), grid=(t // TM,),
        in_specs=[row(D), row(D), row(D), row(2 * DFF), _const((1, D)), _const((1, D)), _any(), _any()]
        + [_any()] * len(after),
        out_specs=[row(D), row(D), row(DFF), row(2 * DFF), row(D), _const((1, D)), _const((1, D))],
        out_shape=[jax.ShapeDtypeStruct((t, D), F32), jax.ShapeDtypeStruct((t, D), BF16),
                   jax.ShapeDtypeStruct((t, DFF), BF16), jax.ShapeDtypeStruct((t, 2 * DFF), BF16),
                   jax.ShapeDtypeStruct((t, D), BF16), jax.ShapeDtypeStruct((1, D), F32),
                   jax.ShapeDtypeStruct((1, D), F32)],
        scratch_shapes=[pltpu.VMEM((2, D, DFF), BF16), pltpu.VMEM((DFF, D), BF16), pltpu.SemaphoreType.DMA((6,))],
        compiler_params=_cp(("arbitrary",)), name="bwd_ffn")(dx, f, xmid, gu, g_pre, g_post, wfi_all, wfo_all, *after)


def bwd_mix(dxm, z, o, y, proj, wconv_t, g_co, g_ao, g_pm, gm, wout_all, after=()):
    t = dxm.shape[0]

    def body(dx_ref, z_ref, o_ref, y_ref, pc_ref, pcp_ref, wc_ref, gco_ref, gao_ref, gpm_ref, gm_ref, wout_hbm,
             dwo_ref, do_ref, dco_ref, dbg_ref, dgpm_ref, dgco_ref, dgao_ref, wout_v, cscr):
        first = pl.program_id(0) == 0

        @pl.when(first)
        def _():
            pltpu.sync_copy(wout_hbm, wout_v)
            dwo_ref[...] = jnp.zeros_like(dwo_ref)
            dgpm_ref[...] = jnp.zeros_like(dgpm_ref)
            dgco_ref[...] = jnp.zeros_like(dgco_ref)
            dgao_ref[...] = jnp.zeros_like(dgao_ref)

        dz, dgp = _rms_bwd(dx_ref[...], z_ref[...], gpm_ref[...])
        dgpm_ref[...] += dgp
        dzb = dz.astype(BF16)
        dwo_ref[...] += lax.dot_general(y_ref[...], dzb, TN, preferred_element_type=F32)
        gmv = gm_ref[...]
        _, bg, _, _, _, _, cout = _conv_fwd(pc_ref, pcp_ref, wc_ref, cscr, first)
        dy_conv = lax.dot_general(dzb, wout_v[0:CW, :], NT, preferred_element_type=F32)
        dyc, dgc = _group_rms_bwd(dy_conv, bg * cout, gco_ref[...], gmv)
        dgco_ref[...] += dgc
        dbg_ref[...] = (dyc * cout).astype(BF16)
        dco_ref[...] = dyc * bg
        dy_attn = lax.dot_general(dzb, wout_v[CW:2 * CW, :], NT, preferred_element_type=F32)
        do, dga = _group_rms_bwd(dy_attn, o_ref[...], gao_ref[...], gmv)
        dgao_ref[...] += dga
        do_ref[...] = do.astype(BF16)

    row = lambda w: pl.BlockSpec((TQ, w), lambda i: (i, 0))
    return pl.pallas_call(
        _behind(body, 12, after), grid=(t // TQ,),
        in_specs=[row(D), row(D), row(CW), row(D)] + _conv_specs() + [
            _const((8, CW)), _const((1, CW)), _const((1, CW)), _const((1, D)), _const((CW, CW)), _any()]
        + [_any()] * len(after),
        out_specs=[_const((D, D)), row(CW), row(CW), row(CW), _const((1, D)), _const((1, CW)), _const((1, CW))],
        out_shape=[jax.ShapeDtypeStruct((D, D), F32), jax.ShapeDtypeStruct((t, CW), BF16),
                   jax.ShapeDtypeStruct((t, CW), F32), jax.ShapeDtypeStruct((t, CW), BF16),
                   jax.ShapeDtypeStruct((1, D), F32), jax.ShapeDtypeStruct((1, CW), F32),
                   jax.ShapeDtypeStruct((1, CW), F32)],
        scratch_shapes=[pltpu.VMEM((D, D), BF16), pltpu.VMEM((TQ + 16, CW), F32)],
        compiler_params=_cp(("arbitrary",)), name="bwd_mix",
    )(dxm, z, o, y, proj, proj, wconv_t, g_co, g_ao, g_pm, gm, wout_all, *after)


def bwd_conv(dco, proj, wconv_t, after=()):
    t = dco.shape[0]
    nt = t // TQ

    def body(d_ref, dn_ref, pc_ref, pcp_ref, wc_ref, dhc_ref, dcg_ref, dw_ref, cscr, dscr):
        i = pl.program_id(0)
        first = i == 0

        @pl.when(first)
        def _():
            dw_ref[...] = jnp.zeros_like(dw_ref)

        hc, _, cg, u, u1, u2, _ = _conv_fwd(pc_ref, pcp_ref, wc_ref, cscr, first)
        d0 = d_ref[...]
        dscr[0:TQ, :] = d0
        dscr[TQ:TQ + 8, :] = jnp.where(i == nt - 1, 0.0, dn_ref[...])
        d1 = dscr[1:TQ + 1, :]
        d2 = dscr[2:TQ + 2, :]
        du = wc_ref[2:3, :] * d0 + wc_ref[1:2, :] * d1 + wc_ref[0:1, :] * d2
        dhc_ref[...] = (du * cg).astype(BF16)
        dcg_ref[...] = (du * hc).astype(BF16)
        dw_ref[0:1, :] += jnp.sum(d0 * u2, axis=0, keepdims=True)
        dw_ref[1:2, :] += jnp.sum(d0 * u1, axis=0, keepdims=True)
        dw_ref[2:3, :] += jnp.sum(d0 * u, axis=0, keepdims=True)

    row = lambda w: pl.BlockSpec((TQ, w), lambda i: (i, 0))
    nxt = pl.BlockSpec((8, CW), lambda i: (jnp.minimum((i + 1) * (TQ // 8), t // 8 - 1), 0))
    return pl.pallas_call(
        _behind(body, 5, after), grid=(nt,),
        in_specs=[row(CW), nxt] + _conv_specs() + [_const((8, CW))] + [_any()] * len(after),
        out_specs=[row(CW), row(CW), _const((8, CW))],
        out_shape=[jax.ShapeDtypeStruct((t, CW), BF16), jax.ShapeDtypeStruct((t, CW), BF16),
                   jax.ShapeDtypeStruct((8, CW), F32)],
        scratch_shapes=[pltpu.VMEM((TQ + 16, CW), F32), pltpu.VMEM((TQ + 8, CW), F32)],
        compiler_params=_cp(("arbitrary",)), name="bwd_conv")(dco, dco, proj, proj, wconv_t, *after)


def bwd_attn(proj, o, do, lse, bias2):
    t = o.shape[0]
    nt = t // TQ
    qg, kg = QG_BWD, QG_BWD + LEFT
    nkb = (t + TQ) // LANES

    def body(q_ref, kp_ref, kc_ref, vp_ref, vc_ref, o_ref, do_ref, lse_ref, b2_ref,
             dq_ref, dk_hbm, dv_hbm, db_hbm, kwin, vwin, dk_acc, dv_acc, db_acc):
        i = pl.program_id(0)
        first = i == 0

        @pl.when(first)
        def _():
            dk_acc[...] = jnp.zeros_like(dk_acc)
            dv_acc[...] = jnp.zeros_like(dv_acc)
            db_acc[...] = jnp.zeros_like(db_acc)

        kwin[0:TQ, :] = kp_ref[...]
        kwin[TQ:2 * TQ, :] = kc_ref[...]
        vwin[0:TQ, :] = vp_ref[...]
        vwin[TQ:2 * TQ, :] = vc_ref[...]
        scale = HD ** -0.5
        qmask = _head_masks(scale)
        vmask = _head_masks(1.0)
        low = lax.broadcasted_iota(jnp.int32, (1, LANES), 1) < HD

        def group(g, carry):
            r0 = pl.multiple_of(g * qg, qg)
            base = i * (TQ // LANES) + g * (qg // LANES)
            pen = _key_penalty(first, r0, kg)
            for hp in range(NH // 2):
                ls = slice(LANES * hp, LANES * (hp + 1))
                qb = q_ref[pl.ds(r0, qg), ls]
                kw = kwin[pl.ds(r0, kg), ls]
                dob = do_ref[pl.ds(r0, qg), ls]
                prod = dob.astype(F32) * o_ref[pl.ds(r0, qg), ls]
                lseb = lse_ref[pl.ds(r0, qg), ls]
                q2 = jnp.concatenate([qb * qmask[0], qb * qmask[1]], axis=0)
                do2 = jnp.concatenate([dob * vmask[0], dob * vmask[1]], axis=0)
                lse2 = jnp.concatenate([lseb[:, 0:1], lseb[:, HD:HD + 1]], axis=0)
                dsum = jnp.concatenate([jnp.sum(jnp.where(low, prod, 0.0), axis=-1, keepdims=True),
                                        jnp.sum(jnp.where(low, 0.0, prod), axis=-1, keepdims=True)], axis=0)
                s = lax.dot_general(q2, kw, NT, preferred_element_type=F32) + b2_ref[hp] + pen
                p = jnp.exp(s - lse2)
                dp = lax.dot_general(do2, vwin[pl.ds(r0, kg), ls], NT, preferred_element_type=F32)
                ds = p * (dp - dsum)
                db_acc[hp] += ds
                dsb = ds.astype(BF16)
                dq2 = jnp.dot(dsb, kw, preferred_element_type=F32)
                dq_ref[pl.ds(r0, qg), ls] = (jnp.where(low, dq2[:qg], dq2[qg:]) * scale).astype(BF16)
                dkt = lax.dot_general(q2, dsb, TN, preferred_element_type=F32)
                dvt = lax.dot_general(do2, p.astype(BF16), TN, preferred_element_type=F32)
                for kb in range(kg // LANES):
                    dk_acc[base + kb, ls, :] += dkt[:, LANES * kb:LANES * (kb + 1)]
                    dv_acc[base + kb, ls, :] += dvt[:, LANES * kb:LANES * (kb + 1)]
            return carry

        lax.fori_loop(0, TQ // qg, group, 0)

        @pl.when(i == nt - 1)
        def _():
            pltpu.sync_copy(dk_acc, dk_hbm)
            pltpu.sync_copy(dv_acc, dv_hbm)
            pltpu.sync_copy(db_acc, db_hbm)

    row = lambda w: pl.BlockSpec((TQ, w), lambda i: (i, 0))
    return pl.pallas_call(
        body, grid=(nt,),
        in_specs=_attn_window_specs() + [row(CW), row(CW), row(CW), _const((NH // 2, 2 * qg, kg))],
        out_specs=[row(CW), _any(), _any(), _any()],
        out_shape=[jax.ShapeDtypeStruct((t, CW), BF16), jax.ShapeDtypeStruct((nkb, CW, LANES), F32),
                   jax.ShapeDtypeStruct((nkb, CW, LANES), F32), jax.ShapeDtypeStruct((NH // 2, 2 * qg, kg), F32)],
        scratch_shapes=[pltpu.VMEM((2 * TQ, CW), BF16), pltpu.VMEM((2 * TQ, CW), BF16),
                        pltpu.VMEM((nkb, CW, LANES), F32), pltpu.VMEM((nkb, CW, LANES), F32),
                        pltpu.VMEM((NH // 2, 2 * qg, kg), F32)],
        compiler_params=_cp(("arbitrary",)), name="bwd_attn",
    )(proj, proj, proj, proj, proj, o, do, lse, bias2)


def bwd_inproj(dxm, x, dhc, dbg, dcg, dq, dk, dv, g, w_all):
    t = x.shape[0]
    nt = t // TQ
    wc = PROJ // NCHIP

    def body(dxm_ref, x_ref, dhc_ref, dbg_ref, dcg_ref, dq_ref, dk_ref, dv_ref, g_ref, w_hbm,
             dx_ref, dw_hbm, dg_ref, w_v, dp_ref, dw_acc):
        @pl.when(pl.program_id(0) == 0)
        def _():
            pltpu.sync_copy(w_hbm, w_v)
            dg_ref[...] = jnp.zeros_like(dg_ref)
            dw_acc[...] = jnp.zeros_like(dw_acc)

        dp_ref[:, 0:CW] = dhc_ref[...]
        dp_ref[:, CW:2 * CW] = dbg_ref[...]
        dp_ref[:, 2 * CW:3 * CW] = dcg_ref[...]
        dp_ref[:, 3 * CW:4 * CW] = dq_ref[...]
        for kb in range(TQ // LANES):
            rows = slice(LANES * kb, LANES * (kb + 1))
            dp_ref[rows, 4 * CW:5 * CW] = jnp.transpose(dk_ref[kb]).astype(BF16)
            dp_ref[rows, 5 * CW:6 * CW] = jnp.transpose(dv_ref[kb]).astype(BF16)
        dh = jnp.zeros((TQ, D), F32)
        for b in range(NCHIP):
            dh = dh + lax.dot_general(dp_ref[:, wc * b:wc * (b + 1)], w_v[b], NT, preferred_element_type=F32)
        xv = x_ref[...]
        gv = g_ref[...]
        hb = _rms(xv, gv).astype(BF16)
        for b in range(NCHIP):
            dw_acc[b] += lax.dot_general(hb, dp_ref[:, wc * b:wc * (b + 1)], TN, preferred_element_type=F32)
        dxv, dgv = _rms_bwd(dh, xv, gv)
        dg_ref[...] += dgv
        dx_ref[...] = dxm_ref[...] + dxv

        @pl.when(pl.program_id(0) == nt - 1)
        def _():
            pltpu.sync_copy(dw_acc, dw_hbm)

    row = lambda w: pl.BlockSpec((TQ, w), lambda i: (i, 0))
    pad = pl.BlockSpec((TQ // LANES, CW, LANES), lambda i: (i + 1, 0, 0))
    return pl.pallas_call(
        body, grid=(nt,),
        in_specs=[row(D), row(D), row(CW), row(CW), row(CW), row(CW), pad, pad, _const((1, D)), _any()],
        out_specs=[row(D), _any(), _const((1, D))],
        out_shape=[jax.ShapeDtypeStruct((t, D), F32), jax.ShapeDtypeStruct((NCHIP, D, wc), F32),
                   jax.ShapeDtypeStruct((1, D), F32)],
        scratch_shapes=[pltpu.VMEM((NCHIP, D, wc), BF16), pltpu.VMEM((TQ, PROJ), BF16),
                        pltpu.VMEM((NCHIP, D, wc), F32)],
        compiler_params=_cp(("arbitrary",)), name="bwd_inproj",
    )(dxm, x, dhc, dbg, dcg, dq, dk, dv, g, w_all)


def wgrad(a, b, kb, nb, by_columns, name):
    t, k = a.shape
    n = b.shape[1]
    tk = 512

    def body(a_ref, b_ref, o_ref):
        o_ref[...] = jnp.zeros_like(o_ref)
        for c in range(t // tk):
            o_ref[...] += lax.dot_general(a_ref[tk * c:tk * (c + 1), :], b_ref[tk * c:tk * (c + 1), :], TN,
                                          preferred_element_type=F32)

    if by_columns:
        assert nb == n // NCHIP
        out_spec = pl.BlockSpec((None, kb, nb), lambda ki, ni: (ni, ki, 0))
        out_shape = jax.ShapeDtypeStruct((NCHIP, k, nb), F32)
    else:
        assert nb == n
        out_spec = pl.BlockSpec((kb, nb), lambda ki, ni: (ki, 0))
        out_shape = jax.ShapeDtypeStruct((k, n), F32)
    return pl.pallas_call(
        body, grid=(k // kb, n // nb),
        in_specs=[pl.BlockSpec((t, kb), lambda ki, ni: (0, ki)), pl.BlockSpec((t, nb), lambda ki, ni: (0, ni))],
        out_specs=out_spec, out_shape=out_shape,
        compiler_params=_cp(("arbitrary", "arbitrary")), name=name)(a, b)


TOE = 1024
assert 2 * QG_FWD + LEFT <= TOE
N_FLAT = LEFT - REL_CLIP + 1
N_VAR = BAND - N_FLAT


def _diag_vector(table):
    last = table[:, 2 * REL_CLIP:]
    var = table[:, 2 * REL_CLIP - N_VAR:2 * REL_CLIP][:, ::-1]
    return jnp.concatenate([jnp.broadcast_to(last, (NH, N_FLAT)), var, jnp.broadcast_to(last, (NH, TOE - BAND))], axis=1)


def _diag_vector_bwd(dvec):
    dlast = jnp.sum(dvec[:, :N_FLAT], axis=1, keepdims=True) + jnp.sum(dvec[:, BAND:], axis=1, keepdims=True)
    dvar = dvec[:, N_FLAT:BAND][:, ::-1]
    return jnp.concatenate([jnp.zeros((NH, 2 * REL_CLIP - N_VAR), F32), dvar, dlast], axis=1)


def _band_valid(qg):
    r = lax.broadcasted_iota(jnp.int32, (qg, qg + LEFT), 0)
    p = lax.broadcasted_iota(jnp.int32, (qg, qg + LEFT), 1)
    start = lax.shift_left(lax.shift_right_logical(r, 6), 6)
    return (p >= start) & (p < start + BAND)


def bias_expand(vec, qgs, after=()):
    def body(v_ref, *o_refs):
        for qg, o_ref in zip(qgs, o_refs):
            valid = _band_valid(qg)
            for h in range(NH):
                rows = jnp.broadcast_to(v_ref[h:h + 1, :], (qg, TOE))
                toe = pltpu.roll(rows, 0, 1, stride=1, stride_axis=0)
                o_ref[h // 2, qg * (h % 2):qg * (h % 2 + 1), :] = jnp.where(valid, toe[:, :qg + LEFT], NEG_INF)

    vm = pl.BlockSpec(memory_space=pltpu.VMEM)
    return pl.pallas_call(_behind(body, 1, after), in_specs=[vm] + [_any()] * len(after), out_specs=[vm] * len(qgs),
                          out_shape=[jax.ShapeDtypeStruct((NH // 2, 2 * qg, qg + LEFT), F32) for qg in qgs],
                          name="bias_expand")(vec, *after)


def bias_reduce(db2):
    _, qg, kg = db2.shape

    def body(d_ref, o_ref):
        ii = lax.broadcasted_iota(jnp.int32, (kg, kg), 0)
        jj = lax.broadcasted_iota(jnp.int32, (kg, kg), 1)
        flip = jnp.where(ii + jj == kg - 1, 1.0, 0.0).astype(BF16)
        for h in range(NH):
            rest = d_ref[h]
            rev = jnp.zeros((qg, kg), F32)
            for _ in range(3):
                term = rest.astype(BF16)
                rev = rev + jnp.dot(term, flip, preferred_element_type=F32)
                rest = rest - term.astype(F32)
            d = jnp.concatenate([jnp.zeros((qg, TOE - kg), F32), rev], axis=1)
            back = pltpu.roll(d, 0, 1, stride=1, stride_axis=0)
            o_ref[h:h + 1, :] = jnp.sum(back, axis=0, keepdims=True)

    rev = pl.pallas_call(body, out_shape=jax.ShapeDtypeStruct((NH, TOE), F32), name="bias_reduce")(db2)
    return rev[:, ::-1]


def _place():
    x, y, c = lax.axis_index("x"), lax.axis_index("y"), lax.axis_index("c")
    chips = [(1 - x, y), (x, 1 - y), (1 - x, 1 - y)]
    return x, y, c, chips


def _half(ref_rows, c):
    return pl.ds(c * (ref_rows // 2), ref_rows // 2)


HBM_SPEC = pl.BlockSpec(memory_space=pltpu.HBM)
SEM_SPEC = pl.BlockSpec(memory_space=pltpu.SEMAPHORE)
IN_FLIGHT = pltpu.CompilerParams(has_side_effects=pltpu.SideEffectType.DATAFLOW_SIDE_EFFECTING)


def _in_hbm(a):
    return pltpu.with_memory_space_constraint(a, pltpu.HBM)


def cast_to_slot(ws, chip, layer, after=()):
    n = len(ws)
    steps = 4

    def body(b_ref, *refs):
        del b_ref
        for w_ref, o_ref in zip(refs[:n], refs[n + len(after):]):
            o_ref[...] = w_ref[...].astype(BF16)

    grid_spec = pltpu.PrefetchScalarGridSpec(
        num_scalar_prefetch=1, grid=(steps,),
        in_specs=[pl.BlockSpec((None, w.shape[1] // steps, w.shape[2]), lambda r, b: (layer, r, 0)) for w in ws]
        + [_any()] * len(after),
        out_specs=[pl.BlockSpec((None, w.shape[1] // steps, w.shape[2]), lambda r, b: (b[0], r, 0)) for w in ws])
    return pl.pallas_call(body, grid_spec=grid_spec,
                          out_shape=[jax.ShapeDtypeStruct((NCHIP,) + w.shape[1:], BF16) for w in ws],
                          compiler_params=_cp(("arbitrary",)), name="cast_to_slot")(chip, *ws, *after)


def _gather_copies(bufs, send, recv):
    x, y, c, chips = _place()
    b = 2 * x + y
    out = []
    for k, buf in enumerate(bufs):
        rows = buf.shape[1]
        mine = buf.at[b, _half(rows, c), :]
        for j, (cx, cy) in enumerate(chips):
            theirs = buf.at[2 * cx + cy, _half(rows, c), :]
            sems = dict(send_sem=send.at[3 * k + j], recv_sem=recv.at[3 * k + j],
                        device_id=(cx, cy, c), device_id_type=MESH)
            out.append((pltpu.make_async_remote_copy(src_ref=mine, dst_ref=mine, **sems),
                        pltpu.make_async_remote_copy(src_ref=theirs, dst_ref=theirs, **sems)))
    return out


def gather_start(bufs, after, layer):
    n = len(bufs)

    def body(*refs):
        ins = refs[:n]
        send, recv = refs[n + 1], refs[n + 2]
        token = refs[-1]
        for start, _ in _gather_copies(ins, send, recv):
            start.start()
        token[...] = jnp.zeros_like(token)

    sems = pltpu.SemaphoreType.DMA((3 * n,))
    res = pl.pallas_call(
        body, name=f"gather_start_{layer}",
        in_specs=[HBM_SPEC] * n + [_any()],
        out_specs=[SEM_SPEC, SEM_SPEC] + [HBM_SPEC] * n + [pl.BlockSpec(memory_space=pltpu.VMEM)],
        out_shape=[sems, sems] + [pltpu.HBM(b.shape, b.dtype) for b in bufs] + [jax.ShapeDtypeStruct((8, LANES), F32)],
        input_output_aliases={k: 2 + k for k in range(n)}, compiler_params=IN_FLIGHT,
    )(*[_in_hbm(b) for b in bufs], after)
    return res[0], res[1], res[2:2 + n], res[-1]


def gather_wait(send, recv, bufs, after, layer):
    n = len(bufs)

    def body(*refs):
        ins = refs[:n]
        send_ref, recv_ref = refs[n], refs[n + 1]
        for start, arrival in _gather_copies(ins, send_ref, recv_ref):
            start.wait_send()
            arrival.wait_recv()

    return pl.pallas_call(
        body, name=f"gather_wait_{layer}",
        in_specs=[HBM_SPEC] * n + [SEM_SPEC, SEM_SPEC, _any()], out_specs=[HBM_SPEC] * n,
        out_shape=[pltpu.HBM(b.shape, b.dtype) for b in bufs],
        input_output_aliases={k: k for k in range(n)}, compiler_params=IN_FLIGHT,
    )(*bufs, send, recv, after)


def gather_forward(bufs):
    n = len(bufs)

    def body(*refs):
        outs = refs[n:2 * n]
        send, recv = refs[2 * n:]
        x, y, c, chips = _place()
        cps = []
        for k in range(n):
            rows = outs[k].shape[1]
            for j, (cx, cy) in enumerate(chips):
                sems = dict(send_sem=send.at[3 * k + j], recv_sem=recv.at[3 * k + j],
                            device_id=(x, y, 1 - c), device_id_type=MESH)
                mine = outs[k].at[2 * cx + cy, _half(rows, c), :]
                theirs = outs[k].at[2 * cx + cy, _half(rows, 1 - c), :]
                cp = pltpu.make_async_remote_copy(src_ref=mine, dst_ref=mine, **sems)
                cp.start()
                cps.append((cp, pltpu.make_async_remote_copy(src_ref=theirs, dst_ref=theirs, **sems)))
        for cp, arrival in cps:
            cp.wait_send()
            arrival.wait_recv()

    return pl.pallas_call(
        body, in_specs=[_any()] * n, out_specs=[_any()] * n,
        out_shape=[jax.ShapeDtypeStruct(b.shape, b.dtype) for b in bufs], input_output_aliases={k: k for k in range(n)},
        scratch_shapes=[pltpu.SemaphoreType.DMA((3 * n,)), pltpu.SemaphoreType.DMA((3 * n,))],
        name="gather_forward")(*bufs)


def _forward_copies(bufs, send, recv):
    x, y, c, chips = _place()
    out = []
    for k, buf in enumerate(bufs):
        rows = buf.shape[1]
        for j, (cx, cy) in enumerate(chips):
            sems = dict(send_sem=send.at[3 * k + j], recv_sem=recv.at[3 * k + j],
                        device_id=(x, y, 1 - c), device_id_type=MESH)
            mine = buf.at[2 * cx + cy, _half(rows, c), :]
            theirs = buf.at[2 * cx + cy, _half(rows, 1 - c), :]
            out.append((pltpu.make_async_remote_copy(src_ref=mine, dst_ref=mine, **sems),
                        pltpu.make_async_remote_copy(src_ref=theirs, dst_ref=theirs, **sems)))
    return out


def forward_start(bufs, tag):
    n = len(bufs)

    def body(*refs):
        ins = refs[:n]
        send, recv = refs[n], refs[n + 1]
        token = refs[-1]
        for start, _ in _forward_copies(ins, send, recv):
            start.start()
        token[...] = jnp.zeros_like(token)

    sems = pltpu.SemaphoreType.DMA((3 * n,))
    res = pl.pallas_call(
        body, name=f"forward_start_{tag}", in_specs=[HBM_SPEC] * n,
        out_specs=[SEM_SPEC, SEM_SPEC] + [HBM_SPEC] * n + [pl.BlockSpec(memory_space=pltpu.VMEM)],
        out_shape=[sems, sems] + [pltpu.HBM(b.shape, b.dtype) for b in bufs] + [jax.ShapeDtypeStruct((8, LANES), F32)],
        input_output_aliases={k: 2 + k for k in range(n)}, compiler_params=IN_FLIGHT,
    )(*[_in_hbm(b) for b in bufs])
    return res[0], res[1], res[2:2 + n], res[-1]


def forward_wait(send, recv, bufs, after, tag):
    n = len(bufs)

    def body(*refs):
        ins = refs[:n]
        send_ref, recv_ref = refs[n], refs[n + 1]
        for start, arrival in _forward_copies(ins, send_ref, recv_ref):
            start.wait_send()
            arrival.wait_recv()

    return pl.pallas_call(
        body, name=f"forward_wait_{tag}",
        in_specs=[HBM_SPEC] * n + [SEM_SPEC, SEM_SPEC, _any()], out_specs=[HBM_SPEC] * n,
        out_shape=[pltpu.HBM(b.shape, b.dtype) for b in bufs],
        input_output_aliases={k: k for k in range(n)}, compiler_params=IN_FLIGHT,
    )(*bufs, send, recv, after)


def _exchange_copies(srcs, lands, send, recv):
    x, y, c, _ = _place()
    return [pltpu.make_async_remote_copy(
        src_ref=src.at[:, _half(src.shape[1], 1 - c), :], dst_ref=land, send_sem=send.at[k], recv_sem=recv.at[k],
        device_id=(x, y, 1 - c), device_id_type=MESH) for k, (src, land) in enumerate(zip(srcs, lands))]


def exchange_start(srcs, tag):
    n = len(srcs)
    lands = [lax.empty((s.shape[0], s.shape[1] // 2, s.shape[2]), s.dtype) for s in srcs]

    def body(*refs):
        ins, land_refs = refs[:n], refs[n:2 * n]
        send, recv = refs[2 * n], refs[2 * n + 1]
        token = refs[-1]
        for cp in _exchange_copies(ins, land_refs, send, recv):
            cp.start()
        token[...] = jnp.zeros_like(token)

    sems = pltpu.SemaphoreType.DMA((n,))
    res = pl.pallas_call(
        body, name=f"exchange_start_{tag}",
        in_specs=[HBM_SPEC] * (2 * n),
        out_specs=[SEM_SPEC, SEM_SPEC] + [HBM_SPEC] * (2 * n) + [pl.BlockSpec(memory_space=pltpu.VMEM)],
        out_shape=[sems, sems] + [pltpu.HBM(a.shape, a.dtype) for a in list(srcs) + lands]
        + [jax.ShapeDtypeStruct((8, LANES), F32)],
        input_output_aliases={k: 2 + k for k in range(2 * n)}, compiler_params=IN_FLIGHT,
    )(*[_in_hbm(a) for a in list(srcs) + lands])
    return res[0], res[1], res[2:2 + n], res[2 + n:2 + 2 * n], res[-1]


def exchange_wait(send, recv, srcs, lands, after, tag):
    n = len(srcs)

    def body(*refs):
        ins, land_refs = refs[:n], refs[n:2 * n]
        send_ref, recv_ref = refs[2 * n], refs[2 * n + 1]
        for cp in _exchange_copies(ins, land_refs, send_ref, recv_ref):
            cp.wait_send()
            cp.wait_recv()

    res = pl.pallas_call(
        body, name=f"exchange_wait_{tag}",
        in_specs=[HBM_SPEC] * (2 * n) + [SEM_SPEC, SEM_SPEC, _any()], out_specs=[HBM_SPEC] * (2 * n),
        out_shape=[pltpu.HBM(a.shape, a.dtype) for a in list(srcs) + list(lands)],
        input_output_aliases={k: k for k in range(2 * n)}, compiler_params=IN_FLIGHT,
    )(*srcs, *lands, send, recv, after)
    return res[:n], res[n:]


def add_pair(gs, r1s, core):
    n = len(gs)

    def body(c_ref, *refs):
        del c_ref
        for g_ref, r_ref, o_ref in zip(refs[:n], refs[n:2 * n], refs[2 * n:]):
            o_ref[...] = (g_ref[...] + r_ref[...]).astype(BF16)

    blk = lambda r: (None,) + r.shape[1:]
    grid_spec = pltpu.PrefetchScalarGridSpec(
        num_scalar_prefetch=1, grid=(NCHIP,),
        in_specs=[pl.BlockSpec(blk(r), lambda s, c: (s, c[0], 0)) for r in r1s]
        + [pl.BlockSpec(blk(r), lambda s, c: (s, 0, 0)) for r in r1s],
        out_specs=[pl.BlockSpec(blk(r), lambda s, c: (s, 0, 0)) for r in r1s])
    return pl.pallas_call(body, grid_spec=grid_spec, out_shape=[jax.ShapeDtypeStruct(r.shape, BF16) for r in r1s],
                          compiler_params=_cp(("arbitrary",)), name="add_pair")(core, *gs, *r1s)


def _scatter_copies(srcs, lands, send, recv):
    _, _, c, chips = _place()
    out = []
    for k, (src, land) in enumerate(zip(srcs, lands)):
        for j, (cx, cy) in enumerate(chips):
            out.append(pltpu.make_async_remote_copy(
                src_ref=src.at[2 * cx + cy], dst_ref=land.at[j], send_sem=send.at[3 * k + j],
                recv_sem=recv.at[3 * k + j], device_id=(cx, cy, c), device_id_type=MESH))
    return out


def scatter_start(srcs, layer):
    n = len(srcs)
    srcs = list(srcs)
    lands = [lax.empty((3,) + s.shape[1:], s.dtype) for s in srcs]

    def body(*refs):
        ins, land_refs = refs[:n], refs[n:2 * n]
        send, recv = refs[2 * n], refs[2 * n + 1]
        token = refs[-1]
        for cp in _scatter_copies(ins, land_refs, send, recv):
            cp.start()
        token[...] = jnp.zeros_like(token)

    sems = pltpu.SemaphoreType.DMA((3 * n,))
    res = pl.pallas_call(
        body, name=f"scatter_start_{layer}",
        in_specs=[HBM_SPEC] * (2 * n),
        out_specs=[SEM_SPEC, SEM_SPEC] + [HBM_SPEC] * (2 * n) + [pl.BlockSpec(memory_space=pltpu.VMEM)],
        out_shape=[sems, sems] + [pltpu.HBM(a.shape, a.dtype) for a in srcs + lands]
        + [jax.ShapeDtypeStruct((8, LANES), F32)],
        input_output_aliases={k: 2 + k for k in range(2 * n)}, compiler_params=IN_FLIGHT,
    )(*[_in_hbm(a) for a in srcs + lands])
    return res[0], res[1], res[2:2 + n], res[2 + n:2 + 2 * n], res[-1]


def scatter_wait(send, recv, srcs, lands, after, layer):
    n = len(srcs)

    def body(*refs):
        ins, land_refs = refs[:n], refs[n:2 * n]
        send_ref, recv_ref = refs[2 * n], refs[2 * n + 1]
        for cp in _scatter_copies(ins, land_refs, send_ref, recv_ref):
            cp.wait_send()
            cp.wait_recv()

    res = pl.pallas_call(
        body, name=f"scatter_wait_{layer}",
        in_specs=[HBM_SPEC] * (2 * n) + [SEM_SPEC, SEM_SPEC, _any()], out_specs=[HBM_SPEC] * (2 * n),
        out_shape=[pltpu.HBM(a.shape, a.dtype) for a in list(srcs) + list(lands)],
        input_output_aliases={k: k for k in range(2 * n)}, compiler_params=IN_FLIGHT,
    )(*srcs, *lands, send, recv, after)
    return res[n:]


def add_chips(gs, r1s, r2s, place, totals, layer):
    n = len(gs)
    steps = 2

    def body(p_ref, *refs):
        del p_ref
        for g_ref, r1_ref, r2_ref, o_ref in zip(refs[:n], refs[n:2 * n], refs[2 * n:3 * n], refs[4 * n:]):
            own = g_ref[...] + r1_ref[...]
            o_ref[...] = ((own + r2_ref[0].astype(F32)) + r2_ref[1].astype(F32)) + r2_ref[2].astype(F32)

    blk = lambda r: (None, r.shape[1] // steps, r.shape[2])
    grid_spec = pltpu.PrefetchScalarGridSpec(
        num_scalar_prefetch=1, grid=(steps,),
        in_specs=[pl.BlockSpec(blk(r), lambda i, p: (p[1], p[0] * steps + i, 0)) for r in r1s]
        + [pl.BlockSpec(blk(r), lambda i, p: (p[1], i, 0)) for r in r1s]
        + [pl.BlockSpec((3,) + blk(r)[1:], lambda i, p: (0, i, 0)) for r in r1s] + [_any()] * n,
        out_specs=[pl.BlockSpec(blk(r), lambda i, p: (layer, p[0] * steps + i, 0)) for r in r1s])
    return pl.pallas_call(body, grid_spec=grid_spec, out_shape=[jax.ShapeDtypeStruct(t.shape, F32) for t in totals],
                          input_output_aliases={1 + 3 * n + k: k for k in range(n)},
                          compiler_params=_cp(("arbitrary",)), name="add_chips")(place, *gs, *r1s, *r2s, *totals)


def _share_copies(bufs, send, recv):
    x, y, c, _ = _place()
    out = []
    for k, buf in enumerate(bufs):
        sems = dict(send_sem=send.at[k], recv_sem=recv.at[k], device_id=(x, y, 1 - c), device_id_type=MESH)
        mine = buf.at[:, _half(buf.shape[1], c), :]
        theirs = buf.at[:, _half(buf.shape[1], 1 - c), :]
        out.append((pltpu.make_async_remote_copy(src_ref=mine, dst_ref=mine, **sems),
                    pltpu.make_async_remote_copy(src_ref=theirs, dst_ref=theirs, **sems)))
    return out


def share_start(bufs, tag):
    n = len(bufs)

    def body(*refs):
        ins = refs[:n]
        send, recv = refs[n], refs[n + 1]
        token = refs[-1]
        for start, _ in _share_copies(ins, send, recv):
            start.start()
        token[...] = jnp.zeros_like(token)

    sems = pltpu.SemaphoreType.DMA((n,))
    res = pl.pallas_call(
        body, name=f"share_start_{tag}", in_specs=[HBM_SPEC] * n,
        out_specs=[SEM_SPEC, SEM_SPEC] + [HBM_SPEC] * n + [pl.BlockSpec(memory_space=pltpu.VMEM)],
        out_shape=[sems, sems] + [pltpu.HBM(b.shape, b.dtype) for b in bufs] + [jax.ShapeDtypeStruct((8, LANES), F32)],
        input_output_aliases={k: 2 + k for k in range(n)}, compiler_params=IN_FLIGHT,
    )(*[_in_hbm(b) for b in bufs])
    return res[0], res[1], res[2:2 + n], res[-1]


def share_wait(send, recv, bufs, after, tag):
    n = len(bufs)

    def body(*refs):
        ins = refs[:n]
        send_ref, recv_ref = refs[n], refs[n + 1]
        for start, arrival in _share_copies(ins, send_ref, recv_ref):
            start.wait_send()
            arrival.wait_recv()

    return pl.pallas_call(
        body, name=f"share_wait_{tag}",
        in_specs=[HBM_SPEC] * n + [SEM_SPEC, SEM_SPEC, _any()], out_specs=[HBM_SPEC] * n,
        out_shape=[pltpu.HBM(b.shape, b.dtype) for b in bufs],
        input_output_aliases={k: k for k in range(n)}, compiler_params=IN_FLIGHT,
    )(*bufs, send, recv, after)


def small_allreduce(v, after=()):
    rows = v.shape[0]
    flips = [(fx, fy, fc) for fx in (0, 1) for fy in (0, 1) for fc in (0, 1)][1:]

    def body(v_ref, o_ref, buf, send, recv):
        x, y, c, _ = _place()
        buf[4 * x + 2 * y + c] = v_ref[...]
        peers = [(jnp.where(fx, 1 - x, x), jnp.where(fy, 1 - y, y), jnp.where(fc, 1 - c, c)) for fx, fy, fc in flips]
        cps = []
        for k, peer in enumerate(peers):
            cp = pltpu.make_async_remote_copy(
                src_ref=v_ref, dst_ref=buf.at[4 * x + 2 * y + c], send_sem=send.at[k], recv_sem=recv.at[k],
                device_id=peer, device_id_type=MESH)
            cp.start()
            cps.append(cp)
        for k, (px, py, pc) in enumerate(peers):
            pltpu.make_async_remote_copy(
                src_ref=v_ref, dst_ref=buf.at[4 * px + 2 * py + pc], send_sem=send.at[k], recv_sem=recv.at[k],
                device_id=(px, py, pc), device_id_type=MESH).wait_recv()
        for cp in cps:
            cp.wait_send()
        acc = buf[0]
        for s in range(1, 8):
            acc = acc + buf[s]
        o_ref[...] = acc

    vm = pl.BlockSpec(memory_space=pltpu.VMEM)
    return pl.pallas_call(
        _behind(body, 1, after), in_specs=[vm] + [_any()] * len(after), out_specs=vm,
        out_shape=jax.ShapeDtypeStruct((rows, SMALL_COLS), F32),
        scratch_shapes=[pltpu.VMEM((8, rows, SMALL_COLS), F32), pltpu.SemaphoreType.DMA((7,)),
                        pltpu.SemaphoreType.DMA((7,))],
        name="reduce_small")(v, *after)


def adamw(w, g, m, v, rb, name, after=()):
    nl, rows, cols = w.shape

    def body(w_ref, g_ref, m_ref, v_ref, go_ref, d_ref, nm_ref, nv_ref):
        gv = g_ref[...]
        go_ref[...] = gv
        nm = ADAM_B1 * m_ref[...] + (1.0 - ADAM_B1) * gv
        nv = ADAM_B2 * v_ref[...] + (1.0 - ADAM_B2) * (gv * gv)
        m_hat = nm / (1.0 - ADAM_B1 ** ADAM_STEP)
        v_hat = nv / (1.0 - ADAM_B2 ** ADAM_STEP)
        d_ref[...] = -ADAM_LR * (m_hat / (jnp.sqrt(v_hat) + ADAM_EPS) + ADAM_WD * w_ref[...])
        nm_ref[...] = nm
        nv_ref[...] = nv

    blk = pl.BlockSpec((None, rb, cols), lambda l, r: (l, r, 0))
    shp = jax.ShapeDtypeStruct(w.shape, F32)
    return pl.pallas_call(_behind(body, 4, after), grid=(nl, rows // rb), in_specs=[blk] * 4 + [_any()] * len(after),
                          out_specs=[blk] * 4, out_shape=[shp] * 4,
                          compiler_params=_cp(("arbitrary", "arbitrary")), name=name)(w, g, m, v, *after)


def _pack(parts, rows):
    flat = jnp.concatenate([p.reshape(-1).astype(F32) for p in parts])
    return jnp.pad(flat, (0, rows * SMALL_COLS - flat.shape[0])).reshape(rows, SMALL_COLS)


def _unpack(vec, shapes):
    flat = vec.reshape(-1)
    out, off = [], 0
    for s in shapes:
        size = 1
        for d in s:
            size *= d
        out.append(flat[off:off + size].reshape(s))
        off += size
    return out


def kernel(x, w_in, w_conv, rel_bias, g_conv_out, g_attn_out, w_out, g_pre_mix, g_post_mix, g_pre_ffn, g_post_ffn, w_ffn_in, w_ffn_out, loss_target, m_w_in, m_w_conv, m_rel_bias, m_g_conv_out, m_g_attn_out, m_w_out, m_g_pre_mix, m_g_post_mix, m_g_pre_ffn, m_g_post_ffn, m_w_ffn_in, m_w_ffn_out, v_w_in, v_w_conv, v_rel_bias, v_g_conv_out, v_g_attn_out, v_w_out, v_g_pre_mix, v_g_post_mix, v_g_pre_ffn, v_g_post_ffn, v_w_ffn_in, v_w_ffn_out):
    xi, yi, ci = lax.axis_index("x"), lax.axis_index("y"), lax.axis_index("c")
    chip = 2 * xi + yi
    nl = w_in.shape[0]
    x0 = x[0]
    target = loss_target[0]
    cwl = CW // NCHIP

    chip1 = chip.reshape(1).astype(jnp.int32)
    big_weights = [w_in, w_out, w_ffn_in, w_ffn_out]
    own = [cast_to_slot(big_weights, chip1, 0)]
    wc_mine = jnp.pad(w_conv.reshape(-1), (0, 16 * LANES - w_conv.size)).reshape(1, 16, LANES)
    wc_slot = lax.dynamic_update_slice_in_dim(jnp.zeros((NCHIP, 16, LANES), F32), wc_mine, chip, axis=0)
    gm = jnp.kron(jnp.eye(CW // HD, dtype=F32), jnp.full((HD, HD), 1.0 / HD, F32)).astype(BF16)
    row = lambda a, l: a[l][None, :]

    def gather_finish(flight, after, tag):
        send, recv, bufs, _ = flight
        return gather_forward(gather_wait(send, recv, bufs, after, tag))

    first_mix = gather_start(list(own[0][:2]) + [wc_slot], x0, "0m")
    first_ffn = gather_start(own[0][2:], first_mix[3], "0f")
    chain = first_ffn[3]
    biases = []
    for l in range(nl):
        biases.append(bias_expand(_diag_vector(rel_bias[l]), (QG_FWD, QG_BWD), [chain]))
        chain = biases[l][1]
    for l in range(1, nl):
        own.append(cast_to_slot(big_weights, chip1, l, [chain]))
        chain = own[l][0]
    gw_in, gw_out, wc_all = gather_finish(first_mix, chain, "0m")
    wc_full = wc_all.reshape(NCHIP, -1)[:, :nl * cwl * 3].reshape(NCHIP, nl, cwl, 3)
    wc_full = jnp.transpose(wc_full, (1, 0, 2, 3)).reshape(nl, CW, 3)
    wconv_t = jnp.pad(jnp.transpose(wc_full, (0, 2, 1)), ((0, 0), (0, 5), (0, 0)))
    flights, to_sibling = {}, None
    saved, weights = [], []
    h = x0
    for l in range(nl):
        if l == 0:
            pass
        elif l == 1:
            flights[2] = gather_start(own[2], h, 2)
            gw_in, gw_out, gw_fi, gw_fo = gather_finish(flights[l], flights[2][3], l)
        else:
            gw_in, gw_out, gw_fi, gw_fo = forward_wait(*to_sibling[:3], h, l)
        gw_out = gw_out.reshape(D, D)
        behind_mix, behind_ffn = ([first_ffn[3]] if l == 0 else []), []
        if l + 1 < nl and l + 1 not in flights:
            flights[l + 1] = gather_start(own[l + 1], first_ffn[3] if l == 0 else gw_in, l + 1)
            behind_mix.append(flights[l + 1][3])
        bias2, bias2_bwd = biases[l]
        proj = fwd_inproj(h, row(g_pre_mix, l), gw_in, behind_mix)
        xmid, o, lse, y, z = fwd_mix(h, proj, bias2, wconv_t[l], row(g_conv_out, l), row(g_attn_out, l),
                                     row(g_post_mix, l), gm, gw_out)
        if l == 0:
            gw_fi, gw_fo = gather_finish(first_ffn, xmid, "0f")
        elif l + 1 < nl:
            send, recv, bufs, _ = flights[l + 1]
            landed = gather_wait(send, recv, bufs, xmid, l + 1)
            to_sibling = forward_start(landed, l + 1)
            behind_ffn.append(to_sibling[3])
            if l + 2 < nl:
                flights[l + 2] = gather_start(own[l + 2], to_sibling[3], l + 2)
                behind_ffn.append(flights[l + 2][3])
        gw_fo = gw_fo.reshape(2, DFF // 2, D)
        ffn = fwd_ffn(xmid, row(g_pre_ffn, l), row(g_post_ffn, l), gw_fi, gw_fo, behind_ffn,
                      target if l == nl - 1 else None)
        gu, f = ffn[:2]
        saved.append((h, proj, bias2_bwd, xmid, o, lse, y, z, gu, f))
        weights.append((gw_in, gw_out, gw_fi, gw_fo))
        h = ffn[2]
    dx, loss_blk = ffn[2], ffn[3]

    core = ci.reshape(1).astype(jnp.int32)
    place = jnp.stack([ci, chip]).astype(jnp.int32)
    totals = [lax.empty(w.shape, F32) for w in (w_in, w_out, w_ffn_in, w_ffn_out)]
    small = {k: [None] * nl for k in ("co", "ao", "pm", "qm", "pf", "qf", "rel", "wc")}

    def reduce_begin(kinds, grads, tag):
        return kinds, exchange_start(grads, tag), tag

    def reduce_mid(state, after):
        kinds, (send, recv, srcs, lands, _), tag = state
        grads, from_sibling = exchange_wait(send, recv, srcs, lands, after, tag)
        return kinds, grads, from_sibling, scatter_start(add_pair(grads, from_sibling, core), tag), tag

    def reduce_end(state, after, totals, layer):
        kinds, grads, from_sibling, (send, recv, srcs, lands, _), tag = state
        from_chips = scatter_wait(send, recv, srcs, lands, after, tag)
        totals = list(totals)
        summed = add_chips(grads, from_sibling, from_chips, place, [totals[i] for i in kinds], layer)
        for i, t in zip(kinds, summed):
            totals[i] = t
        return totals

    begun = flying = None
    for l in reversed(range(nl)):
        hin, proj, bias2, xmid, o, lse, y, z, gu, f = saved[l]
        gw_in, gw_out, gw_fi, gw_fo = weights[l]
        behind_ffn = [begun[1][4]] if begun is not None else []
        dxm, dfb, act, dgu, h2, dg_qf, dg_pf = bwd_ffn(dx, f, xmid, gu, row(g_pre_ffn, l), row(g_post_ffn, l),
                                                        gw_fi, gw_fo, behind_ffn)
        behind_mix, behind_conv = [], []
        if begun is not None:
            flying = reduce_mid(begun, dxm)
            behind_mix.append(flying[3][4])
        gr_fo = wgrad(act, dfb, 256, D, False, "wgrad_ffn_out").reshape(NCHIP, DFF // NCHIP, D)
        gr_fi = wgrad(h2, dgu, 512, 2 * DFF // NCHIP, True, "wgrad_ffn_in")
        if l == 0:
            begun_ffn = reduce_begin([2, 3], [gr_fi, gr_fo], "0f")
            behind_mix.append(begun_ffn[1][4])
        gr_out, do, dco, dbg, dg_qm, dg_co, dg_ao = bwd_mix(dxm, z, o, y, proj, wconv_t[l], row(g_conv_out, l),
                                                             row(g_attn_out, l), row(g_post_mix, l), gm, gw_out,
                                                             behind_mix)
        gr_out = gr_out.reshape(NCHIP, D // NCHIP, D)
        if l == 0:
            flying_ffn = reduce_mid(begun_ffn, do)
            behind_conv.append(flying_ffn[3][4])
        dhc, dcg, dwc = bwd_conv(dco, proj, wconv_t[l], behind_conv)
        dq, dk, dv, db2 = bwd_attn(proj, o, do, lse, bias2)
        dx, gr_in, dg_pm = bwd_inproj(dxm, hin, dhc, dbg, dcg, dq, dk, dv, row(g_pre_mix, l), gw_in)
        if flying is not None:
            totals = reduce_end(flying, dx, totals, l + 1)
        small["co"][l], small["ao"][l], small["pm"][l], small["qm"][l] = dg_co, dg_ao, dg_pm, dg_qm
        small["pf"][l], small["qf"][l] = dg_pf, dg_qf
        small["rel"][l] = _diag_vector_bwd(bias_reduce(db2.reshape(NH, QG_BWD, QG_BWD + LEFT)))
        small["wc"][l] = jnp.transpose(dwc[0:3], (1, 0))
        if l > 0:
            begun = reduce_begin([0, 1, 2, 3], [gr_in, gr_out, gr_fi, gr_fo], l)
    begun_mix = reduce_begin([0, 1], [gr_in, gr_out], "0m")
    totals = reduce_end(flying_ffn, begun_mix[1][4], totals, 0)
    flying_mix = reduce_mid(begun_mix, totals[2])
    share_ffn = share_start(totals[2:], "ffn")

    order = ("co", "ao", "pm", "qm", "pf", "qf", "rel", "wc")
    parts = [jnp.stack(small[k]) for k in order] + [loss_blk[0:1, 0:1]]
    shapes = [p.shape for p in parts]
    red_vec = small_allreduce(_pack(parts, 40), [share_ffn[3], flying_mix[3][4]])
    red = _unpack(red_vec, shapes)

    gr_fi, gr_fo = share_wait(*share_ffn[:3], red_vec, "ffn")
    big_fi = adamw(w_ffn_in, gr_fi, m_w_ffn_in, v_w_ffn_in, w_ffn_in.shape[1] // 4, "adamw_ffn_in")
    totals = reduce_end(flying_mix, big_fi[1], totals, 0)
    share_mix = share_start(totals[:2], "mix")
    big_fo = adamw(w_ffn_out, gr_fo, m_w_ffn_out, v_w_ffn_out, w_ffn_out.shape[1] // 4, "adamw_ffn_out",
                   [share_mix[3]])
    gr_in, gr_out = share_wait(*share_mix[:3], big_fo[1], "mix")
    big_in = adamw(w_in, gr_in, m_w_in, v_w_in, w_in.shape[1] // 4, "adamw_in")
    big_out = adamw(w_out, gr_out, m_w_out, v_w_out, w_out.shape[1] // 4, "adamw_out")
    big = [big_in, big_out, big_fi, big_fo]
    gr_co, gr_ao, gr_pm, gr_qm, gr_pf, gr_qf, gr_rel, gr_wc_full, loss = red
    gr_co, gr_ao, gr_pm, gr_qm, gr_pf, gr_qf = [a.reshape(nl, -1) for a in (gr_co, gr_ao, gr_pm, gr_qm, gr_pf, gr_qf)]
    gr_wc = lax.dynamic_slice_in_dim(gr_wc_full, chip * cwl, cwl, axis=1)
    loss = loss.reshape(())

    sw = [g_conv_out, g_attn_out, g_pre_mix, g_post_mix, g_pre_ffn, g_post_ffn, rel_bias, w_conv]
    sg = [gr_co, gr_ao, gr_pm, gr_qm, gr_pf, gr_qf, gr_rel, gr_wc]
    sm = [m_g_conv_out, m_g_attn_out, m_g_pre_mix, m_g_post_mix, m_g_pre_ffn, m_g_post_ffn, m_rel_bias, m_w_conv]
    sv = [v_g_conv_out, v_g_attn_out, v_g_pre_mix, v_g_post_mix, v_g_pre_ffn, v_g_post_ffn, v_rel_bias, v_w_conv]
    sshapes = [a.shape for a in sw]
    packed = [_pack(a, 32)[None] for a in (sw, sg, sm, sv)]
    s_out = [_unpack(a[0], sshapes) for a in adamw(*packed, 32, "adamw_small")]

    def leaves(big_i, small_i):
        b_in, b_out, b_fi, b_fo = big_i
        s_co, s_ao, s_pm, s_qm, s_pf, s_qf, s_rel, s_wc = small_i
        return [b_in, s_wc, s_rel, s_co, s_ao, b_out, s_pm, s_qm, s_pf, s_qf, b_fi, b_fo]

    out = [loss, dx[None]]
    out += leaves([b[0] for b in big], sg)
    for i in range(1, 4):
        out += leaves([b[i] for b in big], s_out[i])
    return tuple(out)
```

```python
import jax
import jax.numpy as jnp
from jax import lax
from jax.experimental import pallas as pl
from jax.experimental.pallas import tpu as pltpu

F32 = jnp.float32
BF16 = jnp.bfloat16

D = 1024
PROJ = 3072
CW = 512
HD = 64
NH = 8
CHUNK = 64
BAND = 576
REL_CLIP = 128
NREL = 2 * REL_CLIP + 1
DFF = 2816
DEPTH = 4
NCHIP = 4
EPS = 1e-6
NEG_INF = -1e30

ADAM_LR = 0.001
ADAM_B1 = 0.9
ADAM_B2 = 0.999
ADAM_EPS = 1e-08
ADAM_WD = 0.01
ADAM_STEP = 10

V7X_VMEM_BYTES = 64 * 1024 * 1024
VMEM_LIMIT = V7X_VMEM_BYTES - 8 * 1024 * 1024
LANES = 128
QG_FWD = 4 * CHUNK
QG_BWD = 2 * CHUNK
LEFT = BAND - CHUNK
TQ = 512
TM = 256
SMALL_COLS = 1024
MESH = pl.DeviceIdType.MESH
NT = (((1,), (1,)), ((), ()))
TN = (((0,), (0,)), ((), ()))


def _cp(sem=None, vmem=VMEM_LIMIT):
    return pltpu.CompilerParams(dimension_semantics=sem, vmem_limit_bytes=vmem)


def _any():
    return pl.BlockSpec(memory_space=pl.ANY)


def _const(shape):
    nd = len(shape)
    return pl.BlockSpec(shape, lambda *_: (0,) * nd)


def _behind(body, n_in, after):
    def ordered(*refs):
        return body(*refs[:n_in], *refs[n_in + len(after):])
    return ordered


def _rms(v, g):
    r = lax.rsqrt(jnp.mean(v * v, axis=-1, keepdims=True) + EPS)
    return v * r * g


def _rms_bwd(dy, v, g):
    r = lax.rsqrt(jnp.mean(v * v, axis=-1, keepdims=True) + EPS)
    vh = v * r
    dg = jnp.sum(dy * vh, axis=0, keepdims=True)
    dvh = dy * g
    dv = r * (dvh - vh * jnp.mean(dvh * vh, axis=-1, keepdims=True))
    return dv, dg


def _group_mean(v, gm):
    return jnp.dot(v.astype(BF16), gm, preferred_element_type=F32)


def _group_rms_bwd(dy, v, g, gm):
    r = lax.rsqrt(_group_mean(v * v, gm) + EPS)
    vh = v * r
    dg = jnp.sum(dy * vh, axis=0, keepdims=True)
    dvh = dy * g
    dv = r * (dvh - vh * _group_mean(dvh * vh, gm))
    return dv, dg


def _head_masks(scale):
    lane = lax.broadcasted_iota(jnp.int32, (1, LANES), 1)
    return [jnp.where((lane >= HD * a) & (lane < HD * (a + 1)), scale, 0.0).astype(BF16) for a in range(2)]


class _Resident:
    def __init__(self, src, dst, sem):
        self.first = pl.program_id(0) == 0
        self.copy = pltpu.make_async_copy(src, dst, sem)
        self.dst = dst

        @pl.when(self.first)
        def _():
            self.copy.start()

    def read(self):
        @pl.when(self.first)
        def _():
            self.copy.wait()

        return self.dst[...]


FF_CHUNKS = ((0, 1536), (1536, DFF))


def _stream_ffn_weights(wfi_hbm, wfo_hbm, wfi_v, wfo_v, sems, order, step):
    hw = DFF // 2
    per_matrix = {
        0: [(wfi_hbm.at[j], wfi_v.at[0, :, pl.ds(hw * j, hw)]) for j in range(2)],
        1: [(wfi_hbm.at[2 + j], wfi_v.at[1, :, pl.ds(hw * j, hw)]) for j in range(2)],
        2: [(wfo_hbm.at[j], wfo_v.at[pl.ds(hw * j, hw), :]) for j in range(2)],
    }
    pieces = [p for m in order for p in per_matrix[m]]
    slot = {m: 2 * k for k, m in enumerate(order)}

    def make_step(wait):
        def ready(m, chunk):
            if chunk == 0:
                wait(slot[m])
                wait(slot[m] + 1)
        return lambda: step(ready)

    copies = [pltpu.make_async_copy(src, dst, sems.at[k]) for k, (src, dst) in enumerate(pieces)]
    first = pl.program_id(0) == 0

    @pl.when(first)
    def _():
        for cp in copies:
            cp.start()
        make_step(lambda k: copies[k].wait())()

    @pl.when(jnp.logical_not(first))
    def _():
        make_step(lambda k: None)()


def _stream_shards(w_hbm, w_v, sems, step):
    copies = [pltpu.make_async_copy(w_hbm.at[b], w_v.at[b], sems.at[b]) for b in range(NCHIP)]
    first = pl.program_id(0) == 0

    @pl.when(first)
    def _():
        for cp in copies:
            cp.start()
        step(lambda b: copies[b].wait())

    @pl.when(jnp.logical_not(first))
    def _():
        step(lambda b: None)


def _conv_taps(u_prev, u, scr):
    n = u.shape[0]
    scr[0:16, :] = u_prev
    scr[16:16 + n, :] = u
    return scr[15:15 + n, :], scr[14:14 + n, :]


def fwd_inproj(x, g, w_all, after=()):
    t = x.shape[0]
    wc = PROJ // NCHIP

    def body(x_ref, g_ref, w_hbm, o_ref, w_v, sems):
        def step(ready):
            h = _rms(x_ref[...], g_ref[...]).astype(BF16)
            for b in range(NCHIP):
                ready(b)
                o_ref[:, wc * b:wc * (b + 1)] = jnp.dot(h, w_v[b], preferred_element_type=F32).astype(BF16)

        _stream_shards(w_hbm, w_v, sems, step)

    return pl.pallas_call(
        _behind(body, 3, after), grid=(t // TQ,),
        in_specs=[pl.BlockSpec((TQ, D), lambda i: (i, 0)), _const((1, D)), _any()] + [_any()] * len(after),
        out_specs=pl.BlockSpec((TQ, PROJ), lambda i: (i, 0)),
        out_shape=jax.ShapeDtypeStruct((t, PROJ), BF16),
        scratch_shapes=[pltpu.VMEM((NCHIP, D, wc), BF16), pltpu.SemaphoreType.DMA((NCHIP,))],
        compiler_params=_cp(("arbitrary",)), name="fwd_inproj")(x, g, w_all, *after)


def _attn_window_specs():
    return [
        pl.BlockSpec((TQ, CW), lambda i: (i, 3)),
        pl.BlockSpec((TQ, CW), lambda i: (jnp.maximum(i - 1, 0), 4)),
        pl.BlockSpec((TQ, CW), lambda i: (i, 4)),
        pl.BlockSpec((TQ, CW), lambda i: (jnp.maximum(i - 1, 0), 5)),
        pl.BlockSpec((TQ, CW), lambda i: (i, 5)),
    ]


def _conv_specs():
    return [
        pl.BlockSpec((TQ, 3 * CW), lambda i: (i, 0)),
        pl.BlockSpec((16, 3 * CW), lambda i: (jnp.maximum(i * (TQ // 16) - 1, 0), 0)),
    ]


def _conv_fwd(pc_ref, pcp_ref, wc_ref, scr, first):
    pc = pc_ref[...].astype(F32)
    hc, bg, cg = pc[:, :CW], pc[:, CW:2 * CW], pc[:, 2 * CW:]
    u = cg * hc
    pp = pcp_ref[...].astype(F32)
    u_prev = jnp.where(first, 0.0, pp[:, 2 * CW:] * pp[:, :CW])
    u1, u2 = _conv_taps(u_prev, u, scr)
    cout = wc_ref[0:1, :] * u2 + wc_ref[1:2, :] * u1 + wc_ref[2:3, :] * u
    return hc, bg, cg, u, u1, u2, cout


def _key_penalty(first, r0, kg):
    col = lax.broadcasted_iota(jnp.int32, (1, kg), 1)
    limit = jnp.where(first, TQ - r0, 0)
    return jnp.where(col < limit, NEG_INF, 0.0)


def fwd_mix(x, proj, bias2, wconv_t, g_co, g_ao, g_pm, gm, wout_all):
    t = x.shape[0]
    qg, kg = QG_FWD, QG_FWD + LEFT

    def body(x_ref, pc_ref, pcp_ref, q_ref, kp_ref, kc_ref, vp_ref, vc_ref, b2_ref, wc_ref, gco_ref, gao_ref, gpm_ref,
             gm_ref, wout_hbm, xmid_ref, o_ref, lse_ref, y_ref, z_ref, wout_v, kwin, vwin, cscr, sems):
        i = pl.program_id(0)
        first = i == 0
        wout = _Resident(wout_hbm, wout_v, sems.at[0])
        kwin[0:TQ, :] = kp_ref[...]
        kwin[TQ:2 * TQ, :] = kc_ref[...]
        vwin[0:TQ, :] = vp_ref[...]
        vwin[TQ:2 * TQ, :] = vc_ref[...]
        qmask = _head_masks(HD ** -0.5)
        low = lax.broadcasted_iota(jnp.int32, (1, LANES), 1) < HD

        def group(g, carry):
            r0 = pl.multiple_of(g * qg, qg)
            pen = _key_penalty(first, r0, kg)
            for hp in range(NH // 2):
                ls = slice(LANES * hp, LANES * (hp + 1))
                qb = q_ref[pl.ds(r0, qg), ls]
                q2 = jnp.concatenate([qb * qmask[0], qb * qmask[1]], axis=0)
                s = lax.dot_general(q2, kwin[pl.ds(r0, kg), ls], NT, preferred_element_type=F32)
                s = s + b2_ref[hp] + pen
                m = jnp.max(s, axis=-1, keepdims=True)
                p = jnp.exp(s - m)
                l = jnp.sum(p, axis=-1, keepdims=True)
                o2 = jnp.dot(p.astype(BF16), vwin[pl.ds(r0, kg), ls], preferred_element_type=F32) * (1.0 / l)
                lse2 = m + jnp.log(l)
                o_ref[pl.ds(r0, qg), ls] = jnp.where(low, o2[:qg], o2[qg:])
                lse_ref[pl.ds(r0, qg), ls] = jnp.where(low, lse2[:qg], lse2[qg:])
            return carry

        lax.fori_loop(0, TQ // qg, group, 0)

        _, bg, _, _, _, _, cout = _conv_fwd(pc_ref, pcp_ref, wc_ref, cscr, first)
        yc = bg * cout
        gmv = gm_ref[...]
        ycn = yc * lax.rsqrt(_group_mean(yc * yc, gmv) + EPS) * gco_ref[...]
        oa = o_ref[...]
        oan = oa * lax.rsqrt(_group_mean(oa * oa, gmv) + EPS) * gao_ref[...]
        y_ref[:, 0:CW] = ycn.astype(BF16)
        y_ref[:, CW:2 * CW] = oan.astype(BF16)
        z = jnp.dot(y_ref[...], wout.read(), preferred_element_type=F32)
        z_ref[...] = z
        xmid_ref[...] = x_ref[...] + _rms(z, gpm_ref[...])

    row = lambda w: pl.BlockSpec((TQ, w), lambda i: (i, 0))
    return pl.pallas_call(
        body, grid=(t // TQ,),
        in_specs=[row(D)] + _conv_specs() + _attn_window_specs() + [
            _const((NH // 2, 2 * qg, kg)), _const((8, CW)), _const((1, CW)), _const((1, CW)), _const((1, D)),
            _const((CW, CW)), _any()],
        out_specs=[row(D), row(CW), row(CW), row(D), row(D)],
        out_shape=[jax.ShapeDtypeStruct((t, D), F32), jax.ShapeDtypeStruct((t, CW), F32),
                   jax.ShapeDtypeStruct((t, CW), F32), jax.ShapeDtypeStruct((t, D), BF16),
                   jax.ShapeDtypeStruct((t, D), F32)],
        scratch_shapes=[pltpu.VMEM((D, D), BF16), pltpu.VMEM((2 * TQ, CW), BF16), pltpu.VMEM((2 * TQ, CW), BF16),
                        pltpu.VMEM((TQ + 16, CW), F32), pltpu.SemaphoreType.DMA((1,))],
        compiler_params=_cp(("arbitrary",)), name="fwd_mix",
    )(x, proj, proj, proj, proj, proj, proj, proj, bias2, wconv_t, g_co, g_ao, g_pm, gm, wout_all)


def fwd_ffn(xmid, g_pre, g_post, wfi_all, wfo_all, after=(), target=None):
    t = xmid.shape[0]
    n_in = 5 if target is None else 6

    def body(*refs):
        x_ref, gpre_ref, gpost_ref, wfi_hbm, wfo_hbm = refs[:5]
        t_ref = None if target is None else refs[5]
        gu_ref, f_ref, xo_ref = refs[n_in:n_in + 3]
        l_ref = None if target is None else refs[n_in + 3]
        wfi_v, wfo_v, sems = refs[-3:]

        if target is not None:
            @pl.when(pl.program_id(0) == 0)
            def _():
                l_ref[...] = jnp.zeros_like(l_ref)

        def step(ready):
            xv = x_ref[...]
            h = _rms(xv, gpre_ref[...]).astype(BF16)
            f = jnp.zeros((TQ, D), F32)
            for ci, (a, b) in enumerate(FF_CHUNKS):
                ready(0, ci)
                gate = jnp.dot(h, wfi_v[0, :, a:b], preferred_element_type=F32)
                ready(1, ci)
                up = jnp.dot(h, wfi_v[1, :, a:b], preferred_element_type=F32)
                gu_ref[:, a:b] = gate.astype(BF16)
                gu_ref[:, DFF + a:DFF + b] = up.astype(BF16)
                act = gate * (1.0 / (1.0 + jnp.exp(-gate))) * up
                ready(2, ci)
                f = f + jnp.dot(act.astype(BF16), wfo_v[a:b, :], preferred_element_type=F32)
            f_ref[...] = f
            xo = xv + _rms(f, gpost_ref[...])
            if target is None:
                xo_ref[...] = xo
            else:
                e = xo - t_ref[...]
                xo_ref[...] = e * (1.0 / D)
                rows = jnp.sum(e * e, axis=-1, keepdims=True) * (1.0 / D)
                l_ref[...] += 0.5 * jnp.sum(rows, axis=0, keepdims=True)

        _stream_ffn_weights(wfi_hbm, wfo_hbm, wfi_v, wfo_v, sems, (0, 1, 2), step)

    row = lambda w: pl.BlockSpec((TQ, w), lambda i: (i, 0))
    with_loss = target is not None
    return pl.pallas_call(
        _behind(body, n_in, after), grid=(t // TQ,),
        in_specs=[row(D), _const((1, D)), _const((1, D)), _any(), _any()] + [row(D)] * with_loss
        + [_any()] * len(after),
        out_specs=[row(2 * DFF), row(D), row(D)] + [_const((8, LANES))] * with_loss,
        out_shape=[jax.ShapeDtypeStruct((t, 2 * DFF), BF16), jax.ShapeDtypeStruct((t, D), F32),
                   jax.ShapeDtypeStruct((t, D), F32)] + [jax.ShapeDtypeStruct((8, LANES), F32)] * with_loss,
        scratch_shapes=[pltpu.VMEM((2, D, DFF), BF16), pltpu.VMEM((DFF, D), BF16), pltpu.SemaphoreType.DMA((6,))],
        compiler_params=_cp(("arbitrary",)), name="fwd_ffn_loss" if with_loss else "fwd_ffn",
    )(xmid, g_pre, g_post, wfi_all, wfo_all, *([target] * with_loss), *after)


def bwd_ffn(dx, f, xmid, gu, g_pre, g_post, wfi_all, wfo_all, after=()):
    t = dx.shape[0]

    def body(dx_ref, f_ref, x_ref, gu_ref, gpre_ref, gpost_ref, wfi_hbm, wfo_hbm,
             dxm_ref, df_ref, act_ref, dgu_ref, h_ref, dgpost_ref, dgpre_ref, wfi_v, wfo_v, sems):
        @pl.when(pl.program_id(0) == 0)
        def _():
            dgpost_ref[...] = jnp.zeros_like(dgpost_ref)
            dgpre_ref[...] = jnp.zeros_like(dgpre_ref)

        def step(ready):
            dxo = dx_ref[...]
            df, dgp = _rms_bwd(dxo, f_ref[...], gpost_ref[...])
            dgpost_ref[...] += dgp
            dfb = df.astype(BF16)
            df_ref[...] = dfb
            dh = jnp.zeros((TM, D), F32)
            for ci, (a, b) in enumerate(FF_CHUNKS):
                ready(2, ci)
                dact = lax.dot_general(dfb, wfo_v[a:b, :], NT, preferred_element_type=F32)
                gate = gu_ref[:, a:b].astype(F32)
                up = gu_ref[:, DFF + a:DFF + b].astype(F32)
                sig = 1.0 / (1.0 + jnp.exp(-gate))
                silu = gate * sig
                act_ref[:, a:b] = (silu * up).astype(BF16)
                dup = (dact * silu).astype(BF16)
                dgate = (dact * up * (sig * (1.0 + gate * (1.0 - sig)))).astype(BF16)
                dgu_ref[:, a:b] = dgate
                dgu_ref[:, DFF + a:DFF + b] = dup
                ready(0, ci)
                dh = dh + lax.dot_general(dgate, wfi_v[0, :, a:b], NT, preferred_element_type=F32)
                ready(1, ci)
                dh = dh + lax.dot_general(dup, wfi_v[1, :, a:b], NT, preferred_element_type=F32)
            xv = x_ref[...]
            gpre = gpre_ref[...]
            h_ref[...] = _rms(xv, gpre).astype(BF16)
            dxv, dgq = _rms_bwd(dh, xv, gpre)
            dgpre_ref[...] += dgq
            dxm_ref[...] = dxo + dxv

        _stream_ffn_weights(wfi_hbm, wfo_hbm, wfi_v, wfo_v, sems, (2, 0, 1), step)

    row = lambda w: pl.BlockSpec((TM, w), lambda i: (i, 0))
    return pl.pallas_call(
        _behind(body, 8, after), grid=(t // TM,),
        in_specs=[row(D), row(D), row(D), row(2 * DFF), _const((1, D)), _const((1, D)), _any(), _any()]
        + [_any()] * len(after),
        out_specs=[row(D), row(D), row(DFF), row(2 * DFF), row(D), _const((1, D)), _const((1, D))],
        out_shape=[jax.ShapeDtypeStruct((t, D), F32), jax.ShapeDtypeStruct((t, D), BF16),
                   jax.ShapeDtypeStruct((t, DFF), BF16), jax.ShapeDtypeStruct((t, 2 * DFF), BF16),
                   jax.ShapeDtypeStruct((t, D), BF16), jax.ShapeDtypeStruct((1, D), F32),
                   jax.ShapeDtypeStruct((1, D), F32)],
        scratch_shapes=[pltpu.VMEM((2, D, DFF), BF16), pltpu.VMEM((DFF, D), BF16), pltpu.SemaphoreType.DMA((6,))],
        compiler_params=_cp(("arbitrary",)), name="bwd_ffn")(dx, f, xmid, gu, g_pre, g_post, wfi_all, wfo_all, *after)


def bwd_mix(dxm, z, o, y, proj, wconv_t, g_co, g_ao, g_pm, gm, wout_all, after=()):
    t = dxm.shape[0]

    def body(dx_ref, z_ref, o_ref, y_ref, pc_ref, pcp_ref, wc_ref, gco_ref, gao_ref, gpm_ref, gm_ref, wout_hbm,
             dwo_ref, do_ref, dco_ref, dbg_ref, dgpm_ref, dgco_ref, dgao_ref, wout_v, cscr):
        first = pl.program_id(0) == 0

        @pl.when(first)
        def _():
            pltpu.sync_copy(wout_hbm, wout_v)
            dwo_ref[...] = jnp.zeros_like(dwo_ref)
            dgpm_ref[...] = jnp.zeros_like(dgpm_ref)
            dgco_ref[...] = jnp.zeros_like(dgco_ref)
            dgao_ref[...] = jnp.zeros_like(dgao_ref)

        dz, dgp = _rms_bwd(dx_ref[...], z_ref[...], gpm_ref[...])
        dgpm_ref[...] += dgp
        dzb = dz.astype(BF16)
        dwo_ref[...] += lax.dot_general(y_ref[...], dzb, TN, preferred_element_type=F32)
        gmv = gm_ref[...]
        _, bg, _, _, _, _, cout = _conv_fwd(pc_ref, pcp_ref, wc_ref, cscr, first)
        dy_conv = lax.dot_general(dzb, wout_v[0:CW, :], NT, preferred_element_type=F32)
        dyc, dgc = _group_rms_bwd(dy_conv, bg * cout, gco_ref[...], gmv)
        dgco_ref[...] += dgc
        dbg_ref[...] = (dyc * cout).astype(BF16)
        dco_ref[...] = dyc * bg
        dy_attn = lax.dot_general(dzb, wout_v[CW:2 * CW, :], NT, preferred_element_type=F32)
        do, dga = _group_rms_bwd(dy_attn, o_ref[...], gao_ref[...], gmv)
        dgao_ref[...] += dga
        do_ref[...] = do.astype(BF16)

    row = lambda w: pl.BlockSpec((TQ, w), lambda i: (i, 0))
    return pl.pallas_call(
        _behind(body, 12, after), grid=(t // TQ,),
        in_specs=[row(D), row(D), row(CW), row(D)] + _conv_specs() + [
            _const((8, CW)), _const((1, CW)), _const((1, CW)), _const((1, D)), _const((CW, CW)), _any()]
        + [_any()] * len(after),
        out_specs=[_const((D, D)), row(CW), row(CW), row(CW), _const((1, D)), _const((1, CW)), _const((1, CW))],
        out_shape=[jax.ShapeDtypeStruct((D, D), F32), jax.ShapeDtypeStruct((t, CW), BF16),
                   jax.ShapeDtypeStruct((t, CW), F32), jax.ShapeDtypeStruct((t, CW), BF16),
                   jax.ShapeDtypeStruct((1, D), F32), jax.ShapeDtypeStruct((1, CW), F32),
                   jax.ShapeDtypeStruct((1, CW), F32)],
        scratch_shapes=[pltpu.VMEM((D, D), BF16), pltpu.VMEM((TQ + 16, CW), F32)],
        compiler_params=_cp(("arbitrary",)), name="bwd_mix",
    )(dxm, z, o, y, proj, proj, wconv_t, g_co, g_ao, g_pm, gm, wout_all, *after)


def bwd_conv(dco, proj, wconv_t, after=()):
    t = dco.shape[0]
    nt = t // TQ

    def body(d_ref, dn_ref, pc_ref, pcp_ref, wc_ref, dhc_ref, dcg_ref, dw_ref, cscr, dscr):
        i = pl.program_id(0)
        first = i == 0

        @pl.when(first)
        def _():
            dw_ref[...] = jnp.zeros_like(dw_ref)

        hc, _, cg, u, u1, u2, _ = _conv_fwd(pc_ref, pcp_ref, wc_ref, cscr, first)
        d0 = d_ref[...]
        dscr[0:TQ, :] = d0
        dscr[TQ:TQ + 8, :] = jnp.where(i == nt - 1, 0.0, dn_ref[...])
        d1 = dscr[1:TQ + 1, :]
        d2 = dscr[2:TQ + 2, :]
        du = wc_ref[2:3, :] * d0 + wc_ref[1:2, :] * d1 + wc_ref[0:1, :] * d2
        dhc_ref[...] = (du * cg).astype(BF16)
        dcg_ref[...] = (du * hc).astype(BF16)
        dw_ref[0:1, :] += jnp.sum(d0 * u2, axis=0, keepdims=True)
        dw_ref[1:2, :] += jnp.sum(d0 * u1, axis=0, keepdims=True)
        dw_ref[2:3, :] += jnp.sum(d0 * u, axis=0, keepdims=True)

    row = lambda w: pl.BlockSpec((TQ, w), lambda i: (i, 0))
    nxt = pl.BlockSpec((8, CW), lambda i: (jnp.minimum((i + 1) * (TQ // 8), t // 8 - 1), 0))
    return pl.pallas_call(
        _behind(body, 5, after), grid=(nt,),
        in_specs=[row(CW), nxt] + _conv_specs() + [_const((8, CW))] + [_any()] * len(after),
        out_specs=[row(CW), row(CW), _const((8, CW))],
        out_shape=[jax.ShapeDtypeStruct((t, CW), BF16), jax.ShapeDtypeStruct((t, CW), BF16),
                   jax.ShapeDtypeStruct((8, CW), F32)],
        scratch_shapes=[pltpu.VMEM((TQ + 16, CW), F32), pltpu.VMEM((TQ + 8, CW), F32)],
        compiler_params=_cp(("arbitrary",)), name="bwd_conv")(dco, dco, proj, proj, wconv_t, *after)


def bwd_attn(proj, o, do, lse, bias2):
    t = o.shape[0]
    nt = t // TQ
    qg, kg = QG_BWD, QG_BWD + LEFT
    nkb = (t + TQ) // LANES

    def body(q_ref, kp_ref, kc_ref, vp_ref, vc_ref, o_ref, do_ref, lse_ref, b2_ref,
             dq_ref, dk_hbm, dv_hbm, db_hbm, kwin, vwin, dk_acc, dv_acc, db_acc):
        i = pl.program_id(0)
        first = i == 0

        @pl.when(first)
        def _():
            dk_acc[...] = jnp.zeros_like(dk_acc)
            dv_acc[...] = jnp.zeros_like(dv_acc)
            db_acc[...] = jnp.zeros_like(db_acc)

        kwin[0:TQ, :] = kp_ref[...]
        kwin[TQ:2 * TQ, :] = kc_ref[...]
        vwin[0:TQ, :] = vp_ref[...]
        vwin[TQ:2 * TQ, :] = vc_ref[...]
        scale = HD ** -0.5
        qmask = _head_masks(scale)
        vmask = _head_masks(1.0)
        low = lax.broadcasted_iota(jnp.int32, (1, LANES), 1) < HD

        def group(g, carry):
            r0 = pl.multiple_of(g * qg, qg)
            base = i * (TQ // LANES) + g * (qg // LANES)
            pen = _key_penalty(first, r0, kg)
            for hp in range(NH // 2):
                ls = slice(LANES * hp, LANES * (hp + 1))
                qb = q_ref[pl.ds(r0, qg), ls]
                kw = kwin[pl.ds(r0, kg), ls]
                dob = do_ref[pl.ds(r0, qg), ls]
                prod = dob.astype(F32) * o_ref[pl.ds(r0, qg), ls]
                lseb = lse_ref[pl.ds(r0, qg), ls]
                q2 = jnp.concatenate([qb * qmask[0], qb * qmask[1]], axis=0)
                do2 = jnp.concatenate([dob * vmask[0], dob * vmask[1]], axis=0)
                lse2 = jnp.concatenate([lseb[:, 0:1], lseb[:, HD:HD + 1]], axis=0)
                dsum = jnp.concatenate([jnp.sum(jnp.where(low, prod, 0.0), axis=-1, keepdims=True),
                                        jnp.sum(jnp.where(low, 0.0, prod), axis=-1, keepdims=True)], axis=0)
                s = lax.dot_general(q2, kw, NT, preferred_element_type=F32) + b2_ref[hp] + pen
                p = jnp.exp(s - lse2)
                dp = lax.dot_general(do2, vwin[pl.ds(r0, kg), ls], NT, preferred_element_type=F32)
                ds = p * (dp - dsum)
                db_acc[hp] += ds
                dsb = ds.astype(BF16)
                dq2 = jnp.dot(dsb, kw, preferred_element_type=F32)
                dq_ref[pl.ds(r0, qg), ls] = (jnp.where(low, dq2[:qg], dq2[qg:]) * scale).astype(BF16)
                dkt = lax.dot_general(q2, dsb, TN, preferred_element_type=F32)
                dvt = lax.dot_general(do2, p.astype(BF16), TN, preferred_element_type=F32)
                for kb in range(kg // LANES):
                    dk_acc[base + kb, ls, :] += dkt[:, LANES * kb:LANES * (kb + 1)]
                    dv_acc[base + kb, ls, :] += dvt[:, LANES * kb:LANES * (kb + 1)]
            return carry

        lax.fori_loop(0, TQ // qg, group, 0)

        @pl.when(i == nt - 1)
        def _():
            pltpu.sync_copy(dk_acc, dk_hbm)
            pltpu.sync_copy(dv_acc, dv_hbm)
            pltpu.sync_copy(db_acc, db_hbm)

    row = lambda w: pl.BlockSpec((TQ, w), lambda i: (i, 0))
    return pl.pallas_call(
        body, grid=(nt,),
        in_specs=_attn_window_specs() + [row(CW), row(CW), row(CW), _const((NH // 2, 2 * qg, kg))],
        out_specs=[row(CW), _any(), _any(), _any()],
        out_shape=[jax.ShapeDtypeStruct((t, CW), BF16), jax.ShapeDtypeStruct((nkb, CW, LANES), F32),
                   jax.ShapeDtypeStruct((nkb, CW, LANES), F32), jax.ShapeDtypeStruct((NH // 2, 2 * qg, kg), F32)],
        scratch_shapes=[pltpu.VMEM((2 * TQ, CW), BF16), pltpu.VMEM((2 * TQ, CW), BF16),
                        pltpu.VMEM((nkb, CW, LANES), F32), pltpu.VMEM((nkb, CW, LANES), F32),
                        pltpu.VMEM((NH // 2, 2 * qg, kg), F32)],
        compiler_params=_cp(("arbitrary",)), name="bwd_attn",
    )(proj, proj, proj, proj, proj, o, do, lse, bias2)


def bwd_inproj(dxm, x, dhc, dbg, dcg, dq, dk, dv, g, w_all):
    t = x.shape[0]
    nt = t // TQ
    wc = PROJ // NCHIP

    def body(dxm_ref, x_ref, dhc_ref, dbg_ref, dcg_ref, dq_ref, dk_ref, dv_ref, g_ref, w_hbm,
             dx_ref, dw_hbm, dg_ref, w_v, dp_ref, dw_acc, sems):
        @pl.when(pl.program_id(0) == 0)
        def _():
            dg_ref[...] = jnp.zeros_like(dg_ref)
            dw_acc[...] = jnp.zeros_like(dw_acc)

        def step(ready):
            dp_ref[:, 0:CW] = dhc_ref[...]
            dp_ref[:, CW:2 * CW] = dbg_ref[...]
            dp_ref[:, 2 * CW:3 * CW] = dcg_ref[...]
            dp_ref[:, 3 * CW:4 * CW] = dq_ref[...]
            for kb in range(TQ // LANES):
                rows = slice(LANES * kb, LANES * (kb + 1))
                dp_ref[rows, 4 * CW:5 * CW] = jnp.transpose(dk_ref[kb]).astype(BF16)
                dp_ref[rows, 5 * CW:6 * CW] = jnp.transpose(dv_ref[kb]).astype(BF16)
            xv = x_ref[...]
            gv = g_ref[...]
            hb = _rms(xv, gv).astype(BF16)
            for b in range(NCHIP):
                dw_acc[b] += lax.dot_general(hb, dp_ref[:, wc * b:wc * (b + 1)], TN, preferred_element_type=F32)
            dh = jnp.zeros((TQ, D), F32)
            for b in range(NCHIP):
                ready(b)
                dh = dh + lax.dot_general(dp_ref[:, wc * b:wc * (b + 1)], w_v[b], NT, preferred_element_type=F32)
            dxv, dgv = _rms_bwd(dh, xv, gv)
            dg_ref[...] += dgv
            dx_ref[...] = dxm_ref[...] + dxv

        _stream_shards(w_hbm, w_v, sems, step)

        @pl.when(pl.program_id(0) == nt - 1)
        def _():
            pltpu.sync_copy(dw_acc, dw_hbm)

    row = lambda w: pl.BlockSpec((TQ, w), lambda i: (i, 0))
    pad = pl.BlockSpec((TQ // LANES, CW, LANES), lambda i: (i + 1, 0, 0))
    return pl.pallas_call(
        body, grid=(nt,),
        in_specs=[row(D), row(D), row(CW), row(CW), row(CW), row(CW), pad, pad, _const((1, D)), _any()],
        out_specs=[row(D), _any(), _const((1, D))],
        out_shape=[jax.ShapeDtypeStruct((t, D), F32), jax.ShapeDtypeStruct((NCHIP, D, wc), F32),
                   jax.ShapeDtypeStruct((1, D), F32)],
        scratch_shapes=[pltpu.VMEM((NCHIP, D, wc), BF16), pltpu.VMEM((TQ, PROJ), BF16),
                        pltpu.VMEM((NCHIP, D, wc), F32), pltpu.SemaphoreType.DMA((NCHIP,))],
        compiler_params=_cp(("arbitrary",)), name="bwd_inproj",
    )(dxm, x, dhc, dbg, dcg, dq, dk, dv, g, w_all)


def wgrad(a, b, kb, nb, by_columns, name):
    t, k = a.shape
    n = b.shape[1]
    tk = 512

    def body(a_ref, b_ref, o_ref):
        o_ref[...] = jnp.zeros_like(o_ref)
        for c in range(t // tk):
            o_ref[...] += lax.dot_general(a_ref[tk * c:tk * (c + 1), :], b_ref[tk * c:tk * (c + 1), :], TN,
                                          preferred_element_type=F32)

    if by_columns:
        assert nb == n // NCHIP
        out_spec = pl.BlockSpec((None, kb, nb), lambda ki, ni: (ni, ki, 0))
        out_shape = jax.ShapeDtypeStruct((NCHIP, k, nb), F32)
    else:
        assert nb == n
        out_spec = pl.BlockSpec((kb, nb), lambda ki, ni: (ki, 0))
        out_shape = jax.ShapeDtypeStruct((k, n), F32)
    return pl.pallas_call(
        body, grid=(k // kb, n // nb),
        in_specs=[pl.BlockSpec((t, kb), lambda ki, ni: (0, ki)), pl.BlockSpec((t, nb), lambda ki, ni: (0, ni))],
        out_specs=out_spec, out_shape=out_shape,
        compiler_params=_cp(("arbitrary", "arbitrary")), name=name)(a, b)


TOE = 1024
assert 2 * QG_FWD + LEFT <= TOE
N_FLAT = LEFT - REL_CLIP + 1
N_VAR = BAND - N_FLAT


def _diag_vector(table):
    last = table[:, 2 * REL_CLIP:]
    var = table[:, 2 * REL_CLIP - N_VAR:2 * REL_CLIP][:, ::-1]
    return jnp.concatenate([jnp.broadcast_to(last, (NH, N_FLAT)), var, jnp.broadcast_to(last, (NH, TOE - BAND))], axis=1)


def _diag_vector_bwd(dvec):
    dlast = jnp.sum(dvec[:, :N_FLAT], axis=1, keepdims=True) + jnp.sum(dvec[:, BAND:], axis=1, keepdims=True)
    dvar = dvec[:, N_FLAT:BAND][:, ::-1]
    return jnp.concatenate([jnp.zeros((NH, 2 * REL_CLIP - N_VAR), F32), dvar, dlast], axis=1)


def _band_valid(qg):
    r = lax.broadcasted_iota(jnp.int32, (qg, qg + LEFT), 0)
    p = lax.broadcasted_iota(jnp.int32, (qg, qg + LEFT), 1)
    start = lax.shift_left(lax.shift_right_logical(r, 6), 6)
    return (p >= start) & (p < start + BAND)


def bias_expand(vec, qgs, after=()):
    def body(v_ref, *o_refs):
        for qg, o_ref in zip(qgs, o_refs):
            valid = _band_valid(qg)
            for h in range(NH):
                rows = jnp.broadcast_to(v_ref[h:h + 1, :], (qg, TOE))
                toe = pltpu.roll(rows, 0, 1, stride=1, stride_axis=0)
                o_ref[h // 2, qg * (h % 2):qg * (h % 2 + 1), :] = jnp.where(valid, toe[:, :qg + LEFT], NEG_INF)

    vm = pl.BlockSpec(memory_space=pltpu.VMEM)
    return pl.pallas_call(_behind(body, 1, after), in_specs=[vm] + [_any()] * len(after), out_specs=[vm] * len(qgs),
                          out_shape=[jax.ShapeDtypeStruct((NH // 2, 2 * qg, qg + LEFT), F32) for qg in qgs],
                          name="bias_expand")(vec, *after)


def bias_reduce(db2):
    _, qg, kg = db2.shape

    def body(d_ref, o_ref):
        ii = lax.broadcasted_iota(jnp.int32, (kg, kg), 0)
        jj = lax.broadcasted_iota(jnp.int32, (kg, kg), 1)
        flip = jnp.where(ii + jj == kg - 1, 1.0, 0.0).astype(BF16)
        for h in range(NH):
            rest = d_ref[h]
            rev = jnp.zeros((qg, kg), F32)
            for _ in range(3):
                term = rest.astype(BF16)
                rev = rev + jnp.dot(term, flip, preferred_element_type=F32)
                rest = rest - term.astype(F32)
            d = jnp.concatenate([jnp.zeros((qg, TOE - kg), F32), rev], axis=1)
            back = pltpu.roll(d, 0, 1, stride=1, stride_axis=0)
            o_ref[h:h + 1, :] = jnp.sum(back, axis=0, keepdims=True)

    rev = pl.pallas_call(body, out_shape=jax.ShapeDtypeStruct((NH, TOE), F32), name="bias_reduce")(db2)
    return rev[:, ::-1]


def _place():
    x, y, c = lax.axis_index("x"), lax.axis_index("y"), lax.axis_index("c")
    chips = [(1 - x, y), (x, 1 - y), (1 - x, 1 - y)]
    return x, y, c, chips


def _half(ref_rows, c):
    return pl.ds(c * (ref_rows // 2), ref_rows // 2)


HBM_SPEC = pl.BlockSpec(memory_space=pltpu.HBM)
SEM_SPEC = pl.BlockSpec(memory_space=pltpu.SEMAPHORE)
IN_FLIGHT = pltpu.CompilerParams(has_side_effects=pltpu.SideEffectType.DATAFLOW_SIDE_EFFECTING)


def _in_hbm(a):
    return pltpu.with_memory_space_constraint(a, pltpu.HBM)


def cast_to_slot(ws, chip, layer, after=()):
    n = len(ws)
    steps = 4

    def body(b_ref, *refs):
        del b_ref
        for w_ref, o_ref in zip(refs[:n], refs[n + len(after):]):
            o_ref[...] = w_ref[...].astype(BF16)

    grid_spec = pltpu.PrefetchScalarGridSpec(
        num_scalar_prefetch=1, grid=(steps,),
        in_specs=[pl.BlockSpec((None, w.shape[1] // steps, w.shape[2]), lambda r, b: (layer, r, 0)) for w in ws]
        + [_any()] * len(after),
        out_specs=[pl.BlockSpec((None, w.shape[1] // steps, w.shape[2]), lambda r, b: (b[0], r, 0)) for w in ws])
    return pl.pallas_call(body, grid_spec=grid_spec,
                          out_shape=[jax.ShapeDtypeStruct((NCHIP,) + w.shape[1:], BF16) for w in ws],
                          compiler_params=_cp(("arbitrary",)), name="cast_to_slot")(chip, *ws, *after)


def _gather_copies(bufs, send, recv):
    x, y, c, chips = _place()
    b = 2 * x + y
    out = []
    for k, buf in enumerate(bufs):
        rows = buf.shape[1]
        mine = buf.at[b, _half(rows, c), :]
        for j, (cx, cy) in enumerate(chips):
            theirs = buf.at[2 * cx + cy, _half(rows, c), :]
            sems = dict(send_sem=send.at[3 * k + j], recv_sem=recv.at[3 * k + j],
                        device_id=(cx, cy, c), device_id_type=MESH)
            out.append((pltpu.make_async_remote_copy(src_ref=mine, dst_ref=mine, **sems),
                        pltpu.make_async_remote_copy(src_ref=theirs, dst_ref=theirs, **sems)))
    return out


def gather_start(bufs, after, layer):
    n = len(bufs)

    def body(*refs):
        ins = refs[:n]
        send, recv = refs[n + 1], refs[n + 2]
        token = refs[-1]
        for start, _ in _gather_copies(ins, send, recv):
            start.start()
        token[...] = jnp.zeros_like(token)

    sems = pltpu.SemaphoreType.DMA((3 * n,))
    res = pl.pallas_call(
        body, name=f"gather_start_{layer}",
        in_specs=[HBM_SPEC] * n + [_any()],
        out_specs=[SEM_SPEC, SEM_SPEC] + [HBM_SPEC] * n + [pl.BlockSpec(memory_space=pltpu.VMEM)],
        out_shape=[sems, sems] + [pltpu.HBM(b.shape, b.dtype) for b in bufs] + [jax.ShapeDtypeStruct((8, LANES), F32)],
        input_output_aliases={k: 2 + k for k in range(n)}, compiler_params=IN_FLIGHT,
    )(*[_in_hbm(b) for b in bufs], after)
    return res[0], res[1], res[2:2 + n], res[-1]


def gather_wait(send, recv, bufs, after, layer):
    n = len(bufs)

    def body(*refs):
        ins = refs[:n]
        send_ref, recv_ref = refs[n], refs[n + 1]
        for start, arrival in _gather_copies(ins, send_ref, recv_ref):
            start.wait_send()
            arrival.wait_recv()

    return pl.pallas_call(
        body, name=f"gather_wait_{layer}",
        in_specs=[HBM_SPEC] * n + [SEM_SPEC, SEM_SPEC, _any()], out_specs=[HBM_SPEC] * n,
        out_shape=[pltpu.HBM(b.shape, b.dtype) for b in bufs],
        input_output_aliases={k: k for k in range(n)}, compiler_params=IN_FLIGHT,
    )(*bufs, send, recv, after)


def gather_forward(bufs):
    n = len(bufs)

    def body(*refs):
        outs = refs[n:2 * n]
        send, recv = refs[2 * n:]
        x, y, c, chips = _place()
        cps = []
        for k in range(n):
            rows = outs[k].shape[1]
            for j, (cx, cy) in enumerate(chips):
                sems = dict(send_sem=send.at[3 * k + j], recv_sem=recv.at[3 * k + j],
                            device_id=(x, y, 1 - c), device_id_type=MESH)
                mine = outs[k].at[2 * cx + cy, _half(rows, c), :]
                theirs = outs[k].at[2 * cx + cy, _half(rows, 1 - c), :]
                cp = pltpu.make_async_remote_copy(src_ref=mine, dst_ref=mine, **sems)
                cp.start()
                cps.append((cp, pltpu.make_async_remote_copy(src_ref=theirs, dst_ref=theirs, **sems)))
        for cp, arrival in cps:
            cp.wait_send()
            arrival.wait_recv()

    return pl.pallas_call(
        body, in_specs=[_any()] * n, out_specs=[_any()] * n,
        out_shape=[jax.ShapeDtypeStruct(b.shape, b.dtype) for b in bufs], input_output_aliases={k: k for k in range(n)},
        scratch_shapes=[pltpu.SemaphoreType.DMA((3 * n,)), pltpu.SemaphoreType.DMA((3 * n,))],
        name="gather_forward")(*bufs)


def _forward_copies(bufs, send, recv):
    x, y, c, chips = _place()
    out = []
    for k, buf in enumerate(bufs):
        rows = buf.shape[1]
        for j, (cx, cy) in enumerate(chips):
            sems = dict(send_sem=send.at[3 * k + j], recv_sem=recv.at[3 * k + j],
                        device_id=(x, y, 1 - c), device_id_type=MESH)
            mine = buf.at[2 * cx + cy, _half(rows, c), :]
            theirs = buf.at[2 * cx + cy, _half(rows, 1 - c), :]
            out.append((pltpu.make_async_remote_copy(src_ref=mine, dst_ref=mine, **sems),
                        pltpu.make_async_remote_copy(src_ref=theirs, dst_ref=theirs, **sems)))
    return out


def forward_start(bufs, tag):
    n = len(bufs)

    def body(*refs):
        ins = refs[:n]
        send, recv = refs[n], refs[n + 1]
        token = refs[-1]
        for start, _ in _forward_copies(ins, send, recv):
            start.start()
        token[...] = jnp.zeros_like(token)

    sems = pltpu.SemaphoreType.DMA((3 * n,))
    res = pl.pallas_call(
        body, name=f"forward_start_{tag}", in_specs=[HBM_SPEC] * n,
        out_specs=[SEM_SPEC, SEM_SPEC] + [HBM_SPEC] * n + [pl.BlockSpec(memory_space=pltpu.VMEM)],
        out_shape=[sems, sems] + [pltpu.HBM(b.shape, b.dtype) for b in bufs] + [jax.ShapeDtypeStruct((8, LANES), F32)],
        input_output_aliases={k: 2 + k for k in range(n)}, compiler_params=IN_FLIGHT,
    )(*[_in_hbm(b) for b in bufs])
    return res[0], res[1], res[2:2 + n], res[-1]


def forward_wait(send, recv, bufs, after, tag):
    n = len(bufs)

    def body(*refs):
        ins = refs[:n]
        send_ref, recv_ref = refs[n], refs[n + 1]
        for start, arrival in _forward_copies(ins, send_ref, recv_ref):
            start.wait_send()
            arrival.wait_recv()

    return pl.pallas_call(
        body, name=f"forward_wait_{tag}",
        in_specs=[HBM_SPEC] * n + [SEM_SPEC, SEM_SPEC, _any()], out_specs=[HBM_SPEC] * n,
        out_shape=[pltpu.HBM(b.shape, b.dtype) for b in bufs],
        input_output_aliases={k: k for k in range(n)}, compiler_params=IN_FLIGHT,
    )(*bufs, send, recv, after)


def _exchange_copies(srcs, lands, send, recv):
    x, y, c, _ = _place()
    return [pltpu.make_async_remote_copy(
        src_ref=src.at[:, _half(src.shape[1], 1 - c), :], dst_ref=land, send_sem=send.at[k], recv_sem=recv.at[k],
        device_id=(x, y, 1 - c), device_id_type=MESH) for k, (src, land) in enumerate(zip(srcs, lands))]


def exchange_start(srcs, tag):
    n = len(srcs)
    lands = [lax.empty((s.shape[0], s.shape[1] // 2, s.shape[2]), s.dtype) for s in srcs]

    def body(*refs):
        ins, land_refs = refs[:n], refs[n:2 * n]
        send, recv = refs[2 * n], refs[2 * n + 1]
        token = refs[-1]
        for cp in _exchange_copies(ins, land_refs, send, recv):
            cp.start()
        token[...] = jnp.zeros_like(token)

    sems = pltpu.SemaphoreType.DMA((n,))
    res = pl.pallas_call(
        body, name=f"exchange_start_{tag}",
        in_specs=[HBM_SPEC] * (2 * n),
        out_specs=[SEM_SPEC, SEM_SPEC] + [HBM_SPEC] * (2 * n) + [pl.BlockSpec(memory_space=pltpu.VMEM)],
        out_shape=[sems, sems] + [pltpu.HBM(a.shape, a.dtype) for a in list(srcs) + lands]
        + [jax.ShapeDtypeStruct((8, LANES), F32)],
        input_output_aliases={k: 2 + k for k in range(2 * n)}, compiler_params=IN_FLIGHT,
    )(*[_in_hbm(a) for a in list(srcs) + lands])
    return res[0], res[1], res[2:2 + n], res[2 + n:2 + 2 * n], res[-1]


def exchange_wait(send, recv, srcs, lands, after, tag):
    n = len(srcs)

    def body(*refs):
        ins, land_refs = refs[:n], refs[n:2 * n]
        send_ref, recv_ref = refs[2 * n], refs[2 * n + 1]
        for cp in _exchange_copies(ins, land_refs, send_ref, recv_ref):
            cp.wait_send()
            cp.wait_recv()

    res = pl.pallas_call(
        body, name=f"exchange_wait_{tag}",
        in_specs=[HBM_SPEC] * (2 * n) + [SEM_SPEC, SEM_SPEC, _any()], out_specs=[HBM_SPEC] * (2 * n),
        out_shape=[pltpu.HBM(a.shape, a.dtype) for a in list(srcs) + list(lands)],
        input_output_aliases={k: k for k in range(2 * n)}, compiler_params=IN_FLIGHT,
    )(*srcs, *lands, send, recv, after)
    return res[:n], res[n:]


def add_pair(gs, r1s, core):
    n = len(gs)

    def body(c_ref, *refs):
        del c_ref
        for g_ref, r_ref, o_ref in zip(refs[:n], refs[n:2 * n], refs[2 * n:]):
            o_ref[...] = (g_ref[...] + r_ref[...]).astype(BF16)

    blk = lambda r: (None,) + r.shape[1:]
    grid_spec = pltpu.PrefetchScalarGridSpec(
        num_scalar_prefetch=1, grid=(NCHIP,),
        in_specs=[pl.BlockSpec(blk(r), lambda s, c: (s, c[0], 0)) for r in r1s]
        + [pl.BlockSpec(blk(r), lambda s, c: (s, 0, 0)) for r in r1s],
        out_specs=[pl.BlockSpec(blk(r), lambda s, c: (s, 0, 0)) for r in r1s])
    return pl.pallas_call(body, grid_spec=grid_spec, out_shape=[jax.ShapeDtypeStruct(r.shape, BF16) for r in r1s],
                          compiler_params=_cp(("arbitrary",)), name="add_pair")(core, *gs, *r1s)


def _scatter_copies(srcs, lands, send, recv):
    _, _, c, chips = _place()
    out = []
    for k, (src, land) in enumerate(zip(srcs, lands)):
        for j, (cx, cy) in enumerate(chips):
            out.append(pltpu.make_async_remote_copy(
                src_ref=src.at[2 * cx + cy], dst_ref=land.at[j], send_sem=send.at[3 * k + j],
                recv_sem=recv.at[3 * k + j], device_id=(cx, cy, c), device_id_type=MESH))
    return out


def scatter_start(srcs, layer):
    n = len(srcs)
    srcs = list(srcs)
    lands = [lax.empty((3,) + s.shape[1:], s.dtype) for s in srcs]

    def body(*refs):
        ins, land_refs = refs[:n], refs[n:2 * n]
        send, recv = refs[2 * n], refs[2 * n + 1]
        token = refs[-1]
        for cp in _scatter_copies(ins, land_refs, send, recv):
            cp.start()
        token[...] = jnp.zeros_like(token)

    sems = pltpu.SemaphoreType.DMA((3 * n,))
    res = pl.pallas_call(
        body, name=f"scatter_start_{layer}",
        in_specs=[HBM_SPEC] * (2 * n),
        out_specs=[SEM_SPEC, SEM_SPEC] + [HBM_SPEC] * (2 * n) + [pl.BlockSpec(memory_space=pltpu.VMEM)],
        out_shape=[sems, sems] + [pltpu.HBM(a.shape, a.dtype) for a in srcs + lands]
        + [jax.ShapeDtypeStruct((8, LANES), F32)],
        input_output_aliases={k: 2 + k for k in range(2 * n)}, compiler_params=IN_FLIGHT,
    )(*[_in_hbm(a) for a in srcs + lands])
    return res[0], res[1], res[2:2 + n], res[2 + n:2 + 2 * n], res[-1]


def scatter_wait(send, recv, srcs, lands, after, layer):
    n = len(srcs)

    def body(*refs):
        ins, land_refs = refs[:n], refs[n:2 * n]
        send_ref, recv_ref = refs[2 * n], refs[2 * n + 1]
        for cp in _scatter_copies(ins, land_refs, send_ref, recv_ref):
            cp.wait_send()
            cp.wait_recv()

    res = pl.pallas_call(
        body, name=f"scatter_wait_{layer}",
        in_specs=[HBM_SPEC] * (2 * n) + [SEM_SPEC, SEM_SPEC, _any()], out_specs=[HBM_SPEC] * (2 * n),
        out_shape=[pltpu.HBM(a.shape, a.dtype) for a in list(srcs) + list(lands)],
        input_output_aliases={k: k for k in range(2 * n)}, compiler_params=IN_FLIGHT,
    )(*srcs, *lands, send, recv, after)
    return res[n:]


def add_chips(gs, r1s, r2s, place, totals, layer):
    n = len(gs)
    steps = 2

    def body(p_ref, *refs):
        del p_ref
        for g_ref, r1_ref, r2_ref, o_ref in zip(refs[:n], refs[n:2 * n], refs[2 * n:3 * n], refs[4 * n:]):
            own = g_ref[...] + r1_ref[...]
            o_ref[...] = ((own + r2_ref[0].astype(F32)) + r2_ref[1].astype(F32)) + r2_ref[2].astype(F32)

    blk = lambda r: (None, r.shape[1] // steps, r.shape[2])
    grid_spec = pltpu.PrefetchScalarGridSpec(
        num_scalar_prefetch=1, grid=(steps,),
        in_specs=[pl.BlockSpec(blk(r), lambda i, p: (p[1], p[0] * steps + i, 0)) for r in r1s]
        + [pl.BlockSpec(blk(r), lambda i, p: (p[1], i, 0)) for r in r1s]
        + [pl.BlockSpec((3,) + blk(r)[1:], lambda i, p: (0, i, 0)) for r in r1s] + [_any()] * n,
        out_specs=[pl.BlockSpec(blk(r), lambda i, p: (layer, p[0] * steps + i, 0)) for r in r1s])
    return pl.pallas_call(body, grid_spec=grid_spec, out_shape=[jax.ShapeDtypeStruct(t.shape, F32) for t in totals],
                          input_output_aliases={1 + 3 * n + k: k for k in range(n)},
                          compiler_params=_cp(("arbitrary",)), name="add_chips")(place, *gs, *r1s, *r2s, *totals)


def _share_copies(bufs, send, recv):
    x, y, c, _ = _place()
    out = []
    for k, buf in enumerate(bufs):
        sems = dict(send_sem=send.at[k], recv_sem=recv.at[k], device_id=(x, y, 1 - c), device_id_type=MESH)
        mine = buf.at[:, _half(buf.shape[1], c), :]
        theirs = buf.at[:, _half(buf.shape[1], 1 - c), :]
        out.append((pltpu.make_async_remote_copy(src_ref=mine, dst_ref=mine, **sems),
                    pltpu.make_async_remote_copy(src_ref=theirs, dst_ref=theirs, **sems)))
    return out


def share_start(bufs, tag):
    n = len(bufs)

    def body(*refs):
        ins = refs[:n]
        send, recv = refs[n], refs[n + 1]
        token = refs[-1]
        for start, _ in _share_copies(ins, send, recv):
            start.start()
        token[...] = jnp.zeros_like(token)

    sems = pltpu.SemaphoreType.DMA((n,))
    res = pl.pallas_call(
        body, name=f"share_start_{tag}", in_specs=[HBM_SPEC] * n,
        out_specs=[SEM_SPEC, SEM_SPEC] + [HBM_SPEC] * n + [pl.BlockSpec(memory_space=pltpu.VMEM)],
        out_shape=[sems, sems] + [pltpu.HBM(b.shape, b.dtype) for b in bufs] + [jax.ShapeDtypeStruct((8, LANES), F32)],
        input_output_aliases={k: 2 + k for k in range(n)}, compiler_params=IN_FLIGHT,
    )(*[_in_hbm(b) for b in bufs])
    return res[0], res[1], res[2:2 + n], res[-1]


def share_wait(send, recv, bufs, after, tag):
    n = len(bufs)

    def body(*refs):
        ins = refs[:n]
        send_ref, recv_ref = refs[n], refs[n + 1]
        for start, arrival in _share_copies(ins, send_ref, recv_ref):
            start.wait_send()
            arrival.wait_recv()

    return pl.pallas_call(
        body, name=f"share_wait_{tag}",
        in_specs=[HBM_SPEC] * n + [SEM_SPEC, SEM_SPEC, _any()], out_specs=[HBM_SPEC] * n,
        out_shape=[pltpu.HBM(b.shape, b.dtype) for b in bufs],
        input_output_aliases={k: k for k in range(n)}, compiler_params=IN_FLIGHT,
    )(*bufs, send, recv, after)


def small_allreduce(v, after=()):
    rows = v.shape[0]
    flips = [(fx, fy, fc) for fx in (0, 1) for fy in (0, 1) for fc in (0, 1)][1:]

    def body(v_ref, o_ref, buf, send, recv):
        x, y, c, _ = _place()
        buf[4 * x + 2 * y + c] = v_ref[...]
        peers = [(jnp.where(fx, 1 - x, x), jnp.where(fy, 1 - y, y), jnp.where(fc, 1 - c, c)) for fx, fy, fc in flips]
        cps = []
        for k, peer in enumerate(peers):
            cp = pltpu.make_async_remote_copy(
                src_ref=v_ref, dst_ref=buf.at[4 * x + 2 * y + c], send_sem=send.at[k], recv_sem=recv.at[k],
                device_id=peer, device_id_type=MESH)
            cp.start()
            cps.append(cp)
        for k, (px, py, pc) in enumerate(peers):
            pltpu.make_async_remote_copy(
                src_ref=v_ref, dst_ref=buf.at[4 * px + 2 * py + pc], send_sem=send.at[k], recv_sem=recv.at[k],
                device_id=(px, py, pc), device_id_type=MESH).wait_recv()
        for cp in cps:
            cp.wait_send()
        acc = buf[0]
        for s in range(1, 8):
            acc = acc + buf[s]
        o_ref[...] = acc

    vm = pl.BlockSpec(memory_space=pltpu.VMEM)
    return pl.pallas_call(
        _behind(body, 1, after), in_specs=[vm] + [_any()] * len(after), out_specs=vm,
        out_shape=jax.ShapeDtypeStruct((rows, SMALL_COLS), F32),
        scratch_shapes=[pltpu.VMEM((8, rows, SMALL_COLS), F32), pltpu.SemaphoreType.DMA((7,)),
                        pltpu.SemaphoreType.DMA((7,))],
        name="reduce_small")(v, *after)


def adamw(w, g, m, v, rb, name, after=()):
    nl, rows, cols = w.shape

    def body(w_ref, g_ref, m_ref, v_ref, go_ref, d_ref, nm_ref, nv_ref):
        gv = g_ref[...]
        go_ref[...] = gv
        nm = ADAM_B1 * m_ref[...] + (1.0 - ADAM_B1) * gv
        nv = ADAM_B2 * v_ref[...] + (1.0 - ADAM_B2) * (gv * gv)
        m_hat = nm / (1.0 - ADAM_B1 ** ADAM_STEP)
        v_hat = nv / (1.0 - ADAM_B2 ** ADAM_STEP)
        d_ref[...] = -ADAM_LR * (m_hat / (jnp.sqrt(v_hat) + ADAM_EPS) + ADAM_WD * w_ref[...])
        nm_ref[...] = nm
        nv_ref[...] = nv

    blk = pl.BlockSpec((None, rb, cols), lambda l, r: (l, r, 0))
    shp = jax.ShapeDtypeStruct(w.shape, F32)
    return pl.pallas_call(_behind(body, 4, after), grid=(nl, rows // rb), in_specs=[blk] * 4 + [_any()] * len(after),
                          out_specs=[blk] * 4, out_shape=[shp] * 4,
                          compiler_params=_cp(("arbitrary", "arbitrary")), name=name)(w, g, m, v, *after)


def _pack(parts, rows):
    flat = jnp.concatenate([p.reshape(-1).astype(F32) for p in parts])
    return jnp.pad(flat, (0, rows * SMALL_COLS - flat.shape[0])).reshape(rows, SMALL_COLS)


def _unpack(vec, shapes):
    flat = vec.reshape(-1)
    out, off = [], 0
    for s in shapes:
        size = 1
        for d in s:
            size *= d
        out.append(flat[off:off + size].reshape(s))
        off += size
    return out


def kernel(x, w_in, w_conv, rel_bias, g_conv_out, g_attn_out, w_out, g_pre_mix, g_post_mix, g_pre_ffn, g_post_ffn, w_ffn_in, w_ffn_out, loss_target, m_w_in, m_w_conv, m_rel_bias, m_g_conv_out, m_g_attn_out, m_w_out, m_g_pre_mix, m_g_post_mix, m_g_pre_ffn, m_g_post_ffn, m_w_ffn_in, m_w_ffn_out, v_w_in, v_w_conv, v_rel_bias, v_g_conv_out, v_g_attn_out, v_w_out, v_g_pre_mix, v_g_post_mix, v_g_pre_ffn, v_g_post_ffn, v_w_ffn_in, v_w_ffn_out):
    xi, yi, ci = lax.axis_index("x"), lax.axis_index("y"), lax.axis_index("c")
    chip = 2 * xi + yi
    nl = w_in.shape[0]
    x0 = x[0]
    target = loss_target[0]
    cwl = CW // NCHIP

    chip1 = chip.reshape(1).astype(jnp.int32)
    big_weights = [w_in, w_out, w_ffn_in, w_ffn_out]
    own = [cast_to_slot(big_weights, chip1, 0)]
    wc_mine = jnp.pad(w_conv.reshape(-1), (0, 16 * LANES - w_conv.size)).reshape(1, 16, LANES)
    wc_slot = lax.dynamic_update_slice_in_dim(jnp.zeros((NCHIP, 16, LANES), F32), wc_mine, chip, axis=0)
    gm = jnp.kron(jnp.eye(CW // HD, dtype=F32), jnp.full((HD, HD), 1.0 / HD, F32)).astype(BF16)
    row = lambda a, l: a[l][None, :]

    def gather_finish(flight, after, tag):
        send, recv, bufs, _ = flight
        return gather_forward(gather_wait(send, recv, bufs, after, tag))

    first_mix = gather_start(list(own[0][:2]) + [wc_slot], x0, "0m")
    first_ffn = gather_start(own[0][2:], first_mix[3], "0f")
    chain = first_ffn[3]
    biases = []
    for l in range(nl):
        biases.append(bias_expand(_diag_vector(rel_bias[l]), (QG_FWD, QG_BWD), [chain]))
        chain = biases[l][1]
    for l in range(1, nl):
        own.append(cast_to_slot(big_weights, chip1, l, [chain]))
        chain = own[l][0]
    gw_in, gw_out, wc_all = gather_finish(first_mix, chain, "0m")
    wc_full = wc_all.reshape(NCHIP, -1)[:, :nl * cwl * 3].reshape(NCHIP, nl, cwl, 3)
    wc_full = jnp.transpose(wc_full, (1, 0, 2, 3)).reshape(nl, CW, 3)
    wconv_t = jnp.pad(jnp.transpose(wc_full, (0, 2, 1)), ((0, 0), (0, 5), (0, 0)))
    flights, to_sibling = {}, None
    saved, weights = [], []
    h = x0
    for l in range(nl):
        if l == 0:
            pass
        elif l == 1:
            flights[2] = gather_start(own[2], h, 2)
            gw_in, gw_out, gw_fi, gw_fo = gather_finish(flights[l], flights[2][3], l)
        else:
            gw_in, gw_out, gw_fi, gw_fo = forward_wait(*to_sibling[:3], h, l)
        gw_out = gw_out.reshape(D, D)
        behind_mix, behind_ffn = ([first_ffn[3]] if l == 0 else []), []
        if l + 1 < nl and l + 1 not in flights:
            flights[l + 1] = gather_start(own[l + 1], first_ffn[3] if l == 0 else gw_in, l + 1)
            behind_mix.append(flights[l + 1][3])
        bias2, bias2_bwd = biases[l]
        proj = fwd_inproj(h, row(g_pre_mix, l), gw_in, behind_mix)
        xmid, o, lse, y, z = fwd_mix(h, proj, bias2, wconv_t[l], row(g_conv_out, l), row(g_attn_out, l),
                                     row(g_post_mix, l), gm, gw_out)
        if l == 0:
            gw_fi, gw_fo = gather_finish(first_ffn, xmid, "0f")
        elif l + 1 < nl:
            send, recv, bufs, _ = flights[l + 1]
            landed = gather_wait(send, recv, bufs, xmid, l + 1)
            to_sibling = forward_start(landed, l + 1)
            behind_ffn.append(to_sibling[3])
            if l + 2 < nl:
                flights[l + 2] = gather_start(own[l + 2], to_sibling[3], l + 2)
                behind_ffn.append(flights[l + 2][3])
        gw_fo = gw_fo.reshape(2, DFF // 2, D)
        ffn = fwd_ffn(xmid, row(g_pre_ffn, l), row(g_post_ffn, l), gw_fi, gw_fo, behind_ffn,
                      target if l == nl - 1 else None)
        gu, f = ffn[:2]
        saved.append((h, proj, bias2_bwd, xmid, o, lse, y, z, gu, f))
        weights.append((gw_in, gw_out, gw_fi, gw_fo))
        h = ffn[2]
    dx, loss_blk = ffn[2], ffn[3]

    core = ci.reshape(1).astype(jnp.int32)
    place = jnp.stack([ci, chip]).astype(jnp.int32)
    totals = [lax.empty(w.shape, F32) for w in (w_in, w_out, w_ffn_in, w_ffn_out)]
    small = {k: [None] * nl for k in ("co", "ao", "pm", "qm", "pf", "qf", "rel", "wc")}

    def reduce_begin(kinds, grads, tag):
        return kinds, exchange_start(grads, tag), tag

    def reduce_mid(state, after):
        kinds, (send, recv, srcs, lands, _), tag = state
        grads, from_sibling = exchange_wait(send, recv, srcs, lands, after, tag)
        return kinds, grads, from_sibling, scatter_start(add_pair(grads, from_sibling, core), tag), tag

    def reduce_end(state, after, totals, layer):
        kinds, grads, from_sibling, (send, recv, srcs, lands, _), tag = state
        from_chips = scatter_wait(send, recv, srcs, lands, after, tag)
        totals = list(totals)
        summed = add_chips(grads, from_sibling, from_chips, place, [totals[i] for i in kinds], layer)
        for i, t in zip(kinds, summed):
            totals[i] = t
        return totals

    begun = flying = None
    for l in reversed(range(nl)):
        hin, proj, bias2, xmid, o, lse, y, z, gu, f = saved[l]
        gw_in, gw_out, gw_fi, gw_fo = weights[l]
        behind_ffn = [begun[1][4]] if begun is not None else []
        dxm, dfb, act, dgu, h2, dg_qf, dg_pf = bwd_ffn(dx, f, xmid, gu, row(g_pre_ffn, l), row(g_post_ffn, l),
                                                        gw_fi, gw_fo, behind_ffn)
        behind_mix, behind_conv = [], []
        if begun is not None:
            flying = reduce_mid(begun, dxm)
            behind_mix.append(flying[3][4])
        gr_fo = wgrad(act, dfb, 256, D, False, "wgrad_ffn_out").reshape(NCHIP, DFF // NCHIP, D)
        gr_fi = wgrad(h2, dgu, 512, 2 * DFF // NCHIP, True, "wgrad_ffn_in")
        if l == 0:
            begun_ffn = reduce_begin([2, 3], [gr_fi, gr_fo], "0f")
            behind_mix.append(begun_ffn[1][4])
        gr_out, do, dco, dbg, dg_qm, dg_co, dg_ao = bwd_mix(dxm, z, o, y, proj, wconv_t[l], row(g_conv_out, l),
                                                             row(g_attn_out, l), row(g_post_mix, l), gm, gw_out,
                                                             behind_mix)
        gr_out = gr_out.reshape(NCHIP, D // NCHIP, D)
        if l == 0:
            flying_ffn = reduce_mid(begun_ffn, do)
            behind_conv.append(flying_ffn[3][4])
        dhc, dcg, dwc = bwd_conv(dco, proj, wconv_t[l], behind_conv)
        dq, dk, dv, db2 = bwd_attn(proj, o, do, lse, bias2)
        dx, gr_in, dg_pm = bwd_inproj(dxm, hin, dhc, dbg, dcg, dq, dk, dv, row(g_pre_mix, l), gw_in)
        if flying is not None:
            totals = reduce_end(flying, dx, totals, l + 1)
        small["co"][l], small["ao"][l], small["pm"][l], small["qm"][l] = dg_co, dg_ao, dg_pm, dg_qm
        small["pf"][l], small["qf"][l] = dg_pf, dg_qf
        small["rel"][l] = _diag_vector_bwd(bias_reduce(db2.reshape(NH, QG_BWD, QG_BWD + LEFT)))
        small["wc"][l] = jnp.transpose(dwc[0:3], (1, 0))
        if l > 0:
            begun = reduce_begin([0, 1, 2, 3], [gr_in, gr_out, gr_fi, gr_fo], l)
    begun_mix = reduce_begin([0, 1], [gr_in, gr_out], "0m")
    totals = reduce_end(flying_ffn, begun_mix[1][4], totals, 0)
    flying_mix = reduce_mid(begun_mix, totals[2])
    share_ffn = share_start(totals[2:], "ffn")

    order = ("co", "ao", "pm", "qm", "pf", "qf", "rel", "wc")
    parts = [jnp.stack(small[k]) for k in order] + [loss_blk[0:1, 0:1]]
    shapes = [p.shape for p in parts]
    red_vec = small_allreduce(_pack(parts, 40), [share_ffn[3], flying_mix[3][4]])
    red = _unpack(red_vec, shapes)

    gr_fi, gr_fo = share_wait(*share_ffn[:3], red_vec, "ffn")
    big_fi = adamw(w_ffn_in, gr_fi, m_w_ffn_in, v_w_ffn_in, w_ffn_in.shape[1] // 4, "adamw_ffn_in")
    totals = reduce_end(flying_mix, big_fi[1], totals, 0)
    share_mix = share_start(totals[:2], "mix")
    big_fo = adamw(w_ffn_out, gr_fo, m_w_ffn_out, v_w_ffn_out, w_ffn_out.shape[1] // 4, "adamw_ffn_out",
                   [share_mix[3]])
    gr_in, gr_out = share_wait(*share_mix[:3], big_fo[1], "mix")
    big_in = adamw(w_in, gr_in, m_w_in, v_w_in, w_in.shape[1] // 4, "adamw_in")
    big_out = adamw(w_out, gr_out, m_w_out, v_w_out, w_out.shape[1] // 4, "adamw_out")
    big = [big_in, big_out, big_fi, big_fo]
    gr_co, gr_ao, gr_pm, gr_qm, gr_pf, gr_qf, gr_rel, gr_wc_full, loss = red
    gr_co, gr_ao, gr_pm, gr_qm, gr_pf, gr_qf = [a.reshape(nl, -1) for a in (gr_co, gr_ao, gr_pm, gr_qm, gr_pf, gr_qf)]
    gr_wc = lax.dynamic_slice_in_dim(gr_wc_full, chip * cwl, cwl, axis=1)
    loss = loss.reshape(())

    sw = [g_conv_out, g_attn_out, g_pre_mix, g_post_mix, g_pre_ffn, g_post_ffn, rel_bias, w_conv]
    sg = [gr_co, gr_ao, gr_pm, gr_qm, gr_pf, gr_qf, gr_rel, gr_wc]
    sm = [m_g_conv_out, m_g_attn_out, m_g_pre_mix, m_g_post_mix, m_g_pre_ffn, m_g_post_ffn, m_rel_bias, m_w_conv]
    sv = [v_g_conv_out, v_g_attn_out, v_g_pre_mix, v_g_post_mix, v_g_pre_ffn, v_g_post_ffn, v_rel_bias, v_w_conv]
    sshapes = [a.shape for a in sw]
    packed = [_pack(a, 32)[None] for a in (sw, sg, sm, sv)]
    s_out = [_unpack(a[0], sshapes) for a in adamw(*packed, 32, "adamw_small")]

    def leaves(big_i, small_i):
        b_in, b_out, b_fi, b_fo = big_i
        s_co, s_ao, s_pm, s_qm, s_pf, s_qf, s_rel, s_wc = small_i
        return [b_in, s_wc, s_rel, s_co, s_ao, b_out, s_pm, s_qm, s_pf, s_qf, b_fi, b_fo]

    out = [loss, dx[None]]
    out += leaves([b[0] for b in big], sg)
    for i in range(1, 4):
        out += leaves([b[i] for b in big], s_out[i])
    return tuple(out)
```

```python
import jax
import jax.numpy as jnp
from jax import lax
from jax.experimental import pallas as pl
from jax.experimental.pallas import tpu as pltpu

F32 = jnp.float32
BF16 = jnp.bfloat16

D = 1024
PROJ = 3072
CW = 512
HD = 64
NH = 8
CHUNK = 64
BAND = 576
REL_CLIP = 128
NREL = 2 * REL_CLIP + 1
DFF = 2816
DEPTH = 4
NCHIP = 4
EPS = 1e-6
NEG_INF = -1e30

ADAM_LR = 0.001
ADAM_B1 = 0.9
ADAM_B2 = 0.999
ADAM_EPS = 1e-08
ADAM_WD = 0.01
ADAM_STEP = 10

V7X_VMEM_BYTES = 64 * 1024 * 1024
VMEM_LIMIT = V7X_VMEM_BYTES - 8 * 1024 * 1024
LANES = 128
QG_FWD = 4 * CHUNK
QG_BWD = 2 * CHUNK
LEFT = BAND - CHUNK
TQ = 512
TM = 256
SMALL_COLS = 1024
MESH = pl.DeviceIdType.MESH
NT = (((1,), (1,)), ((), ()))
TN = (((0,), (0,)), ((), ()))


def _cp(sem=None, vmem=VMEM_LIMIT):
    return pltpu.CompilerParams(dimension_semantics=sem, vmem_limit_bytes=vmem)


def _any():
    return pl.BlockSpec(memory_space=pl.ANY)


def _const(shape):
    nd = len(shape)
    return pl.BlockSpec(shape, lambda *_: (0,) * nd)


def _behind(body, n_in, after):
    def ordered(*refs):
        return body(*refs[:n_in], *refs[n_in + len(after):])
    return ordered


def _rms(v, g):
    r = lax.rsqrt(jnp.mean(v * v, axis=-1, keepdims=True) + EPS)
    return v * r * g


def _rms_bwd(dy, v, g):
    r = lax.rsqrt(jnp.mean(v * v, axis=-1, keepdims=True) + EPS)
    vh = v * r
    dg = jnp.sum(dy * vh, axis=0, keepdims=True)
    dvh = dy * g
    dv = r * (dvh - vh * jnp.mean(dvh * vh, axis=-1, keepdims=True))
    return dv, dg


def _group_mean(v, gm):
    return jnp.dot(v.astype(BF16), gm, preferred_element_type=F32)


def _group_rms_bwd(dy, v, g, gm):
    r = lax.rsqrt(_group_mean(v * v, gm) + EPS)
    vh = v * r
    dg = jnp.sum(dy * vh, axis=0, keepdims=True)
    dvh = dy * g
    dv = r * (dvh - vh * _group_mean(dvh * vh, gm))
    return dv, dg


def _head_masks(scale):
    lane = lax.broadcasted_iota(jnp.int32, (1, LANES), 1)
    return [jnp.where((lane >= HD * a) & (lane < HD * (a + 1)), scale, 0.0).astype(BF16) for a in range(2)]


class _Resident:
    def __init__(self, src, dst, sem):
        self.first = pl.program_id(0) == 0
        self.copy = pltpu.make_async_copy(src, dst, sem)
        self.dst = dst

        @pl.when(self.first)
        def _():
            self.copy.start()

    def read(self):
        @pl.when(self.first)
        def _():
            self.copy.wait()

        return self.dst[...]


FF_CHUNKS = ((0, 1536), (1536, DFF))


def _stream_ffn_weights(wfi_hbm, wfo_hbm, wfi_v, wfo_v, sems, order, step):
    hw = DFF // 2
    per_matrix = {
        0: [(wfi_hbm.at[j], wfi_v.at[0, :, pl.ds(hw * j, hw)]) for j in range(2)],
        1: [(wfi_hbm.at[2 + j], wfi_v.at[1, :, pl.ds(hw * j, hw)]) for j in range(2)],
        2: [(wfo_hbm.at[j], wfo_v.at[pl.ds(hw * j, hw), :]) for j in range(2)],
    }
    pieces = [p for m in order for p in per_matrix[m]]
    slot = {m: 2 * k for k, m in enumerate(order)}

    def make_step(wait):
        def ready(m, chunk):
            if chunk == 0:
                wait(slot[m])
                wait(slot[m] + 1)
        return lambda: step(ready)

    copies = [pltpu.make_async_copy(src, dst, sems.at[k]) for k, (src, dst) in enumerate(pieces)]
    first = pl.program_id(0) == 0

    @pl.when(first)
    def _():
        for cp in copies:
            cp.start()
        make_step(lambda k: copies[k].wait())()

    @pl.when(jnp.logical_not(first))
    def _():
        make_step(lambda k: None)()


def _stream_shards(w_hbm, w_v, sems, step):
    copies = [pltpu.make_async_copy(w_hbm.at[b], w_v.at[b], sems.at[b]) for b in range(NCHIP)]
    first = pl.program_id(0) == 0

    @pl.when(first)
    def _():
        for cp in copies:
            cp.start()
        step(lambda b: copies[b].wait())

    @pl.when(jnp.logical_not(first))
    def _():
        step(lambda b: None)


def _conv_taps(u_prev, u, scr):
    n = u.shape[0]
    scr[0:16, :] = u_prev
    scr[16:16 + n, :] = u
    return scr[15:15 + n, :], scr[14:14 + n, :]


def fwd_inproj(x, g, w_all, after=()):
    t = x.shape[0]
    wc = PROJ // NCHIP

    def body(x_ref, g_ref, w_hbm, o_ref, w_v, sems):
        def step(ready):
            h = _rms(x_ref[...], g_ref[...]).astype(BF16)
            for b in range(NCHIP):
                ready(b)
                o_ref[:, wc * b:wc * (b + 1)] = jnp.dot(h, w_v[b], preferred_element_type=F32).astype(BF16)

        _stream_shards(w_hbm, w_v, sems, step)

    return pl.pallas_call(
        _behind(body, 3, after), grid=(t // TQ,),
        in_specs=[pl.BlockSpec((TQ, D), lambda i: (i, 0)), _const((1, D)), _any()] + [_any()] * len(after),
        out_specs=pl.BlockSpec((TQ, PROJ), lambda i: (i, 0)),
        out_shape=jax.ShapeDtypeStruct((t, PROJ), BF16),
        scratch_shapes=[pltpu.VMEM((NCHIP, D, wc), BF16), pltpu.SemaphoreType.DMA((NCHIP,))],
        compiler_params=_cp(("arbitrary",)), name="fwd_inproj")(x, g, w_all, *after)


def _attn_window_specs():
    return [
        pl.BlockSpec((TQ, CW), lambda i: (i, 3)),
        pl.BlockSpec((TQ, CW), lambda i: (jnp.maximum(i - 1, 0), 4)),
        pl.BlockSpec((TQ, CW), lambda i: (i, 4)),
        pl.BlockSpec((TQ, CW), lambda i: (jnp.maximum(i - 1, 0), 5)),
        pl.BlockSpec((TQ, CW), lambda i: (i, 5)),
    ]


def _conv_specs():
    return [
        pl.BlockSpec((TQ, 3 * CW), lambda i: (i, 0)),
        pl.BlockSpec((16, 3 * CW), lambda i: (jnp.maximum(i * (TQ // 16) - 1, 0), 0)),
    ]


def _conv_fwd(pc_ref, pcp_ref, wc_ref, scr, first):
    pc = pc_ref[...].astype(F32)
    hc, bg, cg = pc[:, :CW], pc[:, CW:2 * CW], pc[:, 2 * CW:]
    u = cg * hc
    pp = pcp_ref[...].astype(F32)
    u_prev = jnp.where(first, 0.0, pp[:, 2 * CW:] * pp[:, :CW])
    u1, u2 = _conv_taps(u_prev, u, scr)
    cout = wc_ref[0:1, :] * u2 + wc_ref[1:2, :] * u1 + wc_ref[2:3, :] * u
    return hc, bg, cg, u, u1, u2, cout


def _key_penalty(first, r0, kg):
    col = lax.broadcasted_iota(jnp.int32, (1, kg), 1)
    limit = jnp.where(first, TQ - r0, 0)
    return jnp.where(col < limit, NEG_INF, 0.0)


def fwd_mix(x, proj, bias2, wconv_t, g_co, g_ao, g_pm, gm, wout_all):
    t = x.shape[0]
    qg, kg = QG_FWD, QG_FWD + LEFT

    def body(x_ref, pc_ref, pcp_ref, q_ref, kp_ref, kc_ref, vp_ref, vc_ref, b2_ref, wc_ref, gco_ref, gao_ref, gpm_ref,
             gm_ref, wout_hbm, xmid_ref, o_ref, lse_ref, y_ref, z_ref, wout_v, kwin, vwin, cscr, sems):
        i = pl.program_id(0)
        first = i == 0
        wout = _Resident(wout_hbm, wout_v, sems.at[0])
        kwin[0:TQ, :] = kp_ref[...]
        kwin[TQ:2 * TQ, :] = kc_ref[...]
        vwin[0:TQ, :] = vp_ref[...]
        vwin[TQ:2 * TQ, :] = vc_ref[...]
        qmask = _head_masks(HD ** -0.5)
        low = lax.broadcasted_iota(jnp.int32, (1, LANES), 1) < HD

        def group(g, carry):
            r0 = pl.multiple_of(g * qg, qg)
            pen = _key_penalty(first, r0, kg)
            for hp in range(NH // 2):
                ls = slice(LANES * hp, LANES * (hp + 1))
                qb = q_ref[pl.ds(r0, qg), ls]
                q2 = jnp.concatenate([qb * qmask[0], qb * qmask[1]], axis=0)
                s = lax.dot_general(q2, kwin[pl.ds(r0, kg), ls], NT, preferred_element_type=F32)
                s = s + b2_ref[hp] + pen
                m = jnp.max(s, axis=-1, keepdims=True)
                p = jnp.exp(s - m)
                l = jnp.sum(p, axis=-1, keepdims=True)
                o2 = jnp.dot(p.astype(BF16), vwin[pl.ds(r0, kg), ls], preferred_element_type=F32) * (1.0 / l)
                lse2 = m + jnp.log(l)
                o_ref[pl.ds(r0, qg), ls] = jnp.where(low, o2[:qg], o2[qg:])
                lse_ref[pl.ds(r0, qg), ls] = jnp.where(low, lse2[:qg], lse2[qg:])
            return carry

        lax.fori_loop(0, TQ // qg, group, 0)

        _, bg, _, _, _, _, cout = _conv_fwd(pc_ref, pcp_ref, wc_ref, cscr, first)
        yc = bg * cout
        gmv = gm_ref[...]
        ycn = yc * lax.rsqrt(_group_mean(yc * yc, gmv) + EPS) * gco_ref[...]
        oa = o_ref[...]
        oan = oa * lax.rsqrt(_group_mean(oa * oa, gmv) + EPS) * gao_ref[...]
        y_ref[:, 0:CW] = ycn.astype(BF16)
        y_ref[:, CW:2 * CW] = oan.astype(BF16)
        z = jnp.dot(y_ref[...], wout.read(), preferred_element_type=F32)
        z_ref[...] = z
        xmid_ref[...] = x_ref[...] + _rms(z, gpm_ref[...])

    row = lambda w: pl.BlockSpec((TQ, w), lambda i: (i, 0))
    return pl.pallas_call(
        body, grid=(t // TQ,),
        in_specs=[row(D)] + _conv_specs() + _attn_window_specs() + [
            _const((NH // 2, 2 * qg, kg)), _const((8, CW)), _const((1, CW)), _const((1, CW)), _const((1, D)),
            _const((CW, CW)), _any()],
        out_specs=[row(D), row(CW), row(CW), row(D), row(D)],
        out_shape=[jax.ShapeDtypeStruct((t, D), F32), jax.ShapeDtypeStruct((t, CW), F32),
                   jax.ShapeDtypeStruct((t, CW), F32), jax.ShapeDtypeStruct((t, D), BF16),
                   jax.ShapeDtypeStruct((t, D), F32)],
        scratch_shapes=[pltpu.VMEM((D, D), BF16), pltpu.VMEM((2 * TQ, CW), BF16), pltpu.VMEM((2 * TQ, CW), BF16),
                        pltpu.VMEM((TQ + 16, CW), F32), pltpu.SemaphoreType.DMA((1,))],
        compiler_params=_cp(("arbitrary",)), name="fwd_mix",
    )(x, proj, proj, proj, proj, proj, proj, proj, bias2, wconv_t, g_co, g_ao, g_pm, gm, wout_all)


def fwd_ffn(xmid, g_pre, g_post, wfi_all, wfo_all, after=(), target=None):
    t = xmid.shape[0]
    n_in = 5 if target is None else 6

    def body(*refs):
        x_ref, gpre_ref, gpost_ref, wfi_hbm, wfo_hbm = refs[:5]
        t_ref = None if target is None else refs[5]
        gu_ref, f_ref, xo_ref = refs[n_in:n_in + 3]
        l_ref = None if target is None else refs[n_in + 3]
        wfi_v, wfo_v, sems = refs[-3:]

        if target is not None:
            @pl.when(pl.program_id(0) == 0)
            def _():
                l_ref[...] = jnp.zeros_like(l_ref)

        def step(ready):
            xv = x_ref[...]
            h = _rms(xv, gpre_ref[...]).astype(BF16)
            f = jnp.zeros((TQ, D), F32)
            for ci, (a, b) in enumerate(FF_CHUNKS):
                ready(0, ci)
                gate = jnp.dot(h, wfi_v[0, :, a:b], preferred_element_type=F32)
                ready(1, ci)
                up = jnp.dot(h, wfi_v[1, :, a:b], preferred_element_type=F32)
                gu_ref[:, a:b] = gate.astype(BF16)
                gu_ref[:, DFF + a:DFF + b] = up.astype(BF16)
                act = gate * (1.0 / (1.0 + jnp.exp(-gate))) * up
                ready(2, ci)
                f = f + jnp.dot(act.astype(BF16), wfo_v[a:b, :], preferred_element_type=F32)
            f_ref[...] = f
            xo = xv + _rms(f, gpost_ref[...])
            if target is None:
                xo_ref[...] = xo
            else:
                e = xo - t_ref[...]
                xo_ref[...] = e * (1.0 / D)
                rows = jnp.sum(e * e, axis=-1, keepdims=True) * (1.0 / D)
                l_ref[...] += 0.5 * jnp.sum(rows, axis=0, keepdims=True)

        _stream_ffn_weights(wfi_hbm, wfo_hbm, wfi_v, wfo_v, sems, (0, 1, 2), step)

    row = lambda w: pl.BlockSpec((TQ, w), lambda i: (i, 0))
    with_loss = target is not None
    return pl.pallas_call(
        _behind(body, n_in, after), grid=(t // TQ,),
        in_specs=[row(D), _const((1, D)), _const((1, D)), _any(), _any()] + [row(D)] * with_loss
        + [_any()] * len(after),
        out_specs=[row(2 * DFF), row(D), row(D)] + [_const((8, LANES))] * with_loss,
        out_shape=[jax.ShapeDtypeStruct((t, 2 * DFF), BF16), jax.ShapeDtypeStruct((t, D), F32),
                   jax.ShapeDtypeStruct((t, D), F32)] + [jax.ShapeDtypeStruct((8, LANES), F32)] * with_loss,
        scratch_shapes=[pltpu.VMEM((2, D, DFF), BF16), pltpu.VMEM((DFF, D), BF16), pltpu.SemaphoreType.DMA((6,))],
        compiler_params=_cp(("arbitrary",)), name="fwd_ffn_loss" if with_loss else "fwd_ffn",
    )(xmid, g_pre, g_post, wfi_all, wfo_all, *([target] * with_loss), *after)


def bwd_ffn(dx, f, xmid, gu, g_pre, g_post, wfi_all, wfo_all, after=()):
    t = dx.shape[0]

    def body(dx_ref, f_ref, x_ref, gu_ref, gpre_ref, gpost_ref, wfi_hbm, wfo_hbm,
             dxm_ref, df_ref, act_ref, dgu_ref, h_ref, dgpost_ref, dgpre_ref, wfi_v, wfo_v, sems):
        @pl.when(pl.program_id(0) == 0)
        def _():
            dgpost_ref[...] = jnp.zeros_like(dgpost_ref)
            dgpre_ref[...] = jnp.zeros_like(dgpre_ref)

        def step(ready):
            dxo = dx_ref[...]
            df, dgp = _rms_bwd(dxo, f_ref[...], gpost_ref[...])
            dgpost_ref[...] += dgp
            dfb = df.astype(BF16)
            df_ref[...] = dfb
            dh = jnp.zeros((TM, D), F32)
            for ci, (a, b) in enumerate(FF_CHUNKS):
                ready(2, ci)
                dact = lax.dot_general(dfb, wfo_v[a:b, :], NT, preferred_element_type=F32)
                gate = gu_ref[:, a:b].astype(F32)
                up = gu_ref[:, DFF + a:DFF + b].astype(F32)
                sig = 1.0 / (1.0 + jnp.exp(-gate))
                silu = gate * sig
                act_ref[:, a:b] = (silu * up).astype(BF16)
                dup = (dact * silu).astype(BF16)
                dgate = (dact * up * (sig * (1.0 + gate * (1.0 - sig)))).astype(BF16)
                dgu_ref[:, a:b] = dgate
                dgu_ref[:, DFF + a:DFF + b] = dup
                ready(0, ci)
                dh = dh + lax.dot_general(dgate, wfi_v[0, :, a:b], NT, preferred_element_type=F32)
                ready(1, ci)
                dh = dh + lax.dot_general(dup, wfi_v[1, :, a:b], NT, preferred_element_type=F32)
            xv = x_ref[...]
            gpre = gpre_ref[...]
            h_ref[...] = _rms(xv, gpre).astype(BF16)
            dxv, dgq = _rms_bwd(dh, xv, gpre)
            dgpre_ref[...] += dgq
            dxm_ref[...] = dxo + dxv

        _stream_ffn_weights(wfi_hbm, wfo_hbm, wfi_v, wfo_v, sems, (2, 0, 1), step)

    row = lambda w: pl.BlockSpec((TM, w), lambda i: (i, 0))
    return pl.pallas_call(
        _behind(body, 8, after), grid=(t // TM,),
        in_specs=[row(D), row(D), row(D), row(2 * DFF), _const((1, D)), _const((1, D)), _any(), _any()]
        + [_any()] * len(after),
        out_specs=[row(D), row(D), row(DFF), row(2 * DFF), row(D), _const((1, D)), _const((1, D))],
        out_shape=[jax.ShapeDtypeStruct((t, D), F32), jax.ShapeDtypeStruct((t, D), BF16),
                   jax.ShapeDtypeStruct((t, DFF), BF16), jax.ShapeDtypeStruct((t, 2 * DFF), BF16),
                   jax.ShapeDtypeStruct((t, D), BF16), jax.ShapeDtypeStruct((1, D), F32),
                   jax.ShapeDtypeStruct((1, D), F32)],
        scratch_shapes=[pltpu.VMEM((2, D, DFF), BF16), pltpu.VMEM((DFF, D), BF16), pltpu.SemaphoreType.DMA((6,))],
        compiler_params=_cp(("arbitrary",)), name="bwd_ffn")(dx, f, xmid, gu, g_pre, g_post, wfi_all, wfo_all, *after)


def bwd_mix(dxm, z, o, y, proj, wconv_t, g_co, g_ao, g_pm, gm, wout_all, after=()):
    t = dxm.shape[0]

    def body(dx_ref, z_ref, o_ref, y_ref, pc_ref, pcp_ref, wc_ref, gco_ref, gao_ref, gpm_ref, gm_ref, wout_hbm,
             dwo_ref, do_ref, dco_ref, dbg_ref, dgpm_ref, dgco_ref, dgao_ref, wout_v, cscr):
        first = pl.program_id(0) == 0

        @pl.when(first)
        def _():
            pltpu.sync_copy(wout_hbm, wout_v)
            dwo_ref[...] = jnp.zeros_like(dwo_ref)
            dgpm_ref[...] = jnp.zeros_like(dgpm_ref)
            dgco_ref[...] = jnp.zeros_like(dgco_ref)
            dgao_ref[...] = jnp.zeros_like(dgao_ref)

        dz, dgp = _rms_bwd(dx_ref[...], z_ref[...], gpm_ref[...])
        dgpm_ref[...] += dgp
        dzb = dz.astype(BF16)
        dwo_ref[...] += lax.dot_general(y_ref[...], dzb, TN, preferred_element_type=F32)
        gmv = gm_ref[...]
        _, bg, _, _, _, _, cout = _conv_fwd(pc_ref, pcp_ref, wc_ref, cscr, first)
        dy_conv = lax.dot_general(dzb, wout_v[0:CW, :], NT, preferred_element_type=F32)
        dyc, dgc = _group_rms_bwd(dy_conv, bg * cout, gco_ref[...], gmv)
        dgco_ref[...] += dgc
        dbg_ref[...] = (dyc * cout).astype(BF16)
        dco_ref[...] = dyc * bg
        dy_attn = lax.dot_general(dzb, wout_v[CW:2 * CW, :], NT, preferred_element_type=F32)
        do, dga = _group_rms_bwd(dy_attn, o_ref[...], gao_ref[...], gmv)
        dgao_ref[...] += dga
        do_ref[...] = do.astype(BF16)

    row = lambda w: pl.BlockSpec((TQ, w), lambda i: (i, 0))
    return pl.pallas_call(
        _behind(body, 12, after), grid=(t // TQ,),
        in_specs=[row(D), row(D), row(CW), row(D)] + _conv_specs() + [
            _const((8, CW)), _const((1, CW)), _const((1, CW)), _const((1, D)), _const((CW, CW)), _any()]
        + [_any()] * len(after),
        out_specs=[_const((D, D)), row(CW), row(CW), row(CW), _const((1, D)), _const((1, CW)), _const((1, CW))],
        out_shape=[jax.ShapeDtypeStruct((D, D), F32), jax.ShapeDtypeStruct((t, CW), BF16),
                   jax.ShapeDtypeStruct((t, CW), F32), jax.ShapeDtypeStruct((t, CW), BF16),
                   jax.ShapeDtypeStruct((1, D), F32), jax.ShapeDtypeStruct((1, CW), F32),
                   jax.ShapeDtypeStruct((1, CW), F32)],
        scratch_shapes=[pltpu.VMEM((D, D), BF16), pltpu.VMEM((TQ + 16, CW), F32)],
        compiler_params=_cp(("arbitrary",)), name="bwd_mix",
    )(dxm, z, o, y, proj, proj, wconv_t, g_co, g_ao, g_pm, gm, wout_all, *after)


def bwd_conv(dco, proj, wconv_t, after=()):
    t = dco.shape[0]
    nt = t // TQ

    def body(d_ref, dn_ref, pc_ref, pcp_ref, wc_ref, dhc_ref, dcg_ref, dw_ref, cscr, dscr):
        i = pl.program_id(0)
        first = i == 0

        @pl.when(first)
        def _():
            dw_ref[...] = jnp.zeros_like(dw_ref)

        hc, _, cg, u, u1, u2, _ = _conv_fwd(pc_ref, pcp_ref, wc_ref, cscr, first)
        d0 = d_ref[...]
        dscr[0:TQ, :] = d0
        dscr[TQ:TQ + 8, :] = jnp.where(i == nt - 1, 0.0, dn_ref[...])
        d1 = dscr[1:TQ + 1, :]
        d2 = dscr[2:TQ + 2, :]
        du = wc_ref[2:3, :] * d0 + wc_ref[1:2, :] * d1 + wc_ref[0:1, :] * d2
        dhc_ref[...] = (du * cg).astype(BF16)
        dcg_ref[...] = (du * hc).astype(BF16)
        dw_ref[0:1, :] += jnp.sum(d0 * u2, axis=0, keepdims=True)
        dw_ref[1:2, :] += jnp.sum(d0 * u1, axis=0, keepdims=True)
        dw_ref[2:3, :] += jnp.sum(d0 * u, axis=0, keepdims=True)

    row = lambda w: pl.BlockSpec((TQ, w), lambda i: (i, 0))
    nxt = pl.BlockSpec((8, CW), lambda i: (jnp.minimum((i + 1) * (TQ // 8), t // 8 - 1), 0))
    return pl.pallas_call(
        _behind(body, 5, after), grid=(nt,),
        in_specs=[row(CW), nxt] + _conv_specs() + [_const((8, CW))] + [_any()] * len(after),
        out_specs=[row(CW), row(CW), _const((8, CW))],
        out_shape=[jax.ShapeDtypeStruct((t, CW), BF16), jax.ShapeDtypeStruct((t, CW), BF16),
                   jax.ShapeDtypeStruct((8, CW), F32)],
        scratch_shapes=[pltpu.VMEM((TQ + 16, CW), F32), pltpu.VMEM((TQ + 8, CW), F32)],
        compiler_params=_cp(("arbitrary",)), name="bwd_conv")(dco, dco, proj, proj, wconv_t, *after)


def bwd_attn(proj, o, do, lse, bias2):
    t = o.shape[0]
    nt = t // TQ
    qg, kg = QG_BWD, QG_BWD + LEFT
    nkb = (t + TQ) // LANES

    def body(q_ref, kp_ref, kc_ref, vp_ref, vc_ref, o_ref, do_ref, lse_ref, b2_ref,
             dq_ref, dk_hbm, dv_hbm, db_hbm, kwin, vwin, dk_acc, dv_acc, db_acc, sems):
        i = pl.program_id(0)
        first = i == 0

        @pl.when(first)
        def _():
            dk_acc[...] = jnp.zeros_like(dk_acc)
            dv_acc[...] = jnp.zeros_like(dv_acc)
            db_acc[...] = jnp.zeros_like(db_acc)

        kwin[0:TQ, :] = kp_ref[...]
        kwin[TQ:2 * TQ, :] = kc_ref[...]
        vwin[0:TQ, :] = vp_ref[...]
        vwin[TQ:2 * TQ, :] = vc_ref[...]
        scale = HD ** -0.5
        qmask = _head_masks(scale)
        vmask = _head_masks(1.0)
        low = lax.broadcasted_iota(jnp.int32, (1, LANES), 1) < HD

        def group(g, carry):
            r0 = pl.multiple_of(g * qg, qg)
            base = i * (TQ // LANES) + g * (qg // LANES)
            pen = _key_penalty(first, r0, kg)
            for hp in range(NH // 2):
                ls = slice(LANES * hp, LANES * (hp + 1))
                qb = q_ref[pl.ds(r0, qg), ls]
                kw = kwin[pl.ds(r0, kg), ls]
                dob = do_ref[pl.ds(r0, qg), ls]
                prod = dob.astype(F32) * o_ref[pl.ds(r0, qg), ls]
                lseb = lse_ref[pl.ds(r0, qg), ls]
                q2 = jnp.concatenate([qb * qmask[0], qb * qmask[1]], axis=0)
                do2 = jnp.concatenate([dob * vmask[0], dob * vmask[1]], axis=0)
                lse2 = jnp.concatenate([lseb[:, 0:1], lseb[:, HD:HD + 1]], axis=0)
                dsum = jnp.concatenate([jnp.sum(jnp.where(low, prod, 0.0), axis=-1, keepdims=True),
                                        jnp.sum(jnp.where(low, 0.0, prod), axis=-1, keepdims=True)], axis=0)
                s = lax.dot_general(q2, kw, NT, preferred_element_type=F32) + b2_ref[hp] + pen
                p = jnp.exp(s - lse2)
                dp = lax.dot_general(do2, vwin[pl.ds(r0, kg), ls], NT, preferred_element_type=F32)
                ds = p * (dp - dsum)
                db_acc[hp] += ds
                dsb = ds.astype(BF16)
                dq2 = jnp.dot(dsb, kw, preferred_element_type=F32)
                dq_ref[pl.ds(r0, qg), ls] = (jnp.where(low, dq2[:qg], dq2[qg:]) * scale).astype(BF16)
                dkt = lax.dot_general(q2, dsb, TN, preferred_element_type=F32)
                dvt = lax.dot_general(do2, p.astype(BF16), TN, preferred_element_type=F32)
                for kb in range(kg // LANES):
                    dk_acc[base + kb, ls, :] += dkt[:, LANES * kb:LANES * (kb + 1)]
                    dv_acc[base + kb, ls, :] += dvt[:, LANES * kb:LANES * (kb + 1)]
            return carry

        lax.fori_loop(0, TQ // qg, group, 0)

        blocks = TQ // LANES

        def flush(step, n):
            sl = pl.ds(step * blocks, n)
            return [pltpu.make_async_copy(acc.at[sl], hbm.at[sl], sems.at[k])
                    for k, (acc, hbm) in enumerate(((dk_acc, dk_hbm), (dv_acc, dv_hbm)))]

        @pl.when(i > 0)
        def _():
            for cp in flush(i - 1, blocks):
                cp.wait()

        @pl.when(i < nt - 1)
        def _():
            for cp in flush(i, blocks):
                cp.start()

        @pl.when(i == nt - 1)
        def _():
            last = flush(i, 2 * blocks)
            for cp in last:
                cp.start()
            pltpu.sync_copy(db_acc, db_hbm)
            for cp in last:
                cp.wait()

    row = lambda w: pl.BlockSpec((TQ, w), lambda i: (i, 0))
    return pl.pallas_call(
        body, grid=(nt,),
        in_specs=_attn_window_specs() + [row(CW), row(CW), row(CW), _const((NH // 2, 2 * qg, kg))],
        out_specs=[row(CW), _any(), _any(), _any()],
        out_shape=[jax.ShapeDtypeStruct((t, CW), BF16), jax.ShapeDtypeStruct((nkb, CW, LANES), F32),
                   jax.ShapeDtypeStruct((nkb, CW, LANES), F32), jax.ShapeDtypeStruct((NH // 2, 2 * qg, kg), F32)],
        scratch_shapes=[pltpu.VMEM((2 * TQ, CW), BF16), pltpu.VMEM((2 * TQ, CW), BF16),
                        pltpu.VMEM((nkb, CW, LANES), F32), pltpu.VMEM((nkb, CW, LANES), F32),
                        pltpu.VMEM((NH // 2, 2 * qg, kg), F32), pltpu.SemaphoreType.DMA((2,))],
        compiler_params=_cp(("arbitrary",)), name="bwd_attn",
    )(proj, proj, proj, proj, proj, o, do, lse, bias2)


def bwd_inproj(dxm, x, dhc, dbg, dcg, dq, dk, dv, g, w_all):
    t = x.shape[0]
    nt = t // TQ
    wc = PROJ // NCHIP

    def body(dxm_ref, x_ref, dhc_ref, dbg_ref, dcg_ref, dq_ref, dk_ref, dv_ref, g_ref, w_hbm,
             dx_ref, dw_hbm, dg_ref, w_v, dp_ref, dw_acc, sems):
        @pl.when(pl.program_id(0) == 0)
        def _():
            dg_ref[...] = jnp.zeros_like(dg_ref)
            dw_acc[...] = jnp.zeros_like(dw_acc)

        def step(ready):
            dp_ref[:, 0:CW] = dhc_ref[...]
            dp_ref[:, CW:2 * CW] = dbg_ref[...]
            dp_ref[:, 2 * CW:3 * CW] = dcg_ref[...]
            dp_ref[:, 3 * CW:4 * CW] = dq_ref[...]
            for kb in range(TQ // LANES):
                rows = slice(LANES * kb, LANES * (kb + 1))
                dp_ref[rows, 4 * CW:5 * CW] = jnp.transpose(dk_ref[kb]).astype(BF16)
                dp_ref[rows, 5 * CW:6 * CW] = jnp.transpose(dv_ref[kb]).astype(BF16)
            xv = x_ref[...]
            gv = g_ref[...]
            hb = _rms(xv, gv).astype(BF16)
            for b in range(NCHIP):
                dw_acc[b] += lax.dot_general(hb, dp_ref[:, wc * b:wc * (b + 1)], TN, preferred_element_type=F32)
            dh = jnp.zeros((TQ, D), F32)
            for b in range(NCHIP):
                ready(b)
                dh = dh + lax.dot_general(dp_ref[:, wc * b:wc * (b + 1)], w_v[b], NT, preferred_element_type=F32)
            dxv, dgv = _rms_bwd(dh, xv, gv)
            dg_ref[...] += dgv
            dx_ref[...] = dxm_ref[...] + dxv

        _stream_shards(w_hbm, w_v, sems, step)

        @pl.when(pl.program_id(0) == nt - 1)
        def _():
            pltpu.sync_copy(dw_acc, dw_hbm)

    row = lambda w: pl.BlockSpec((TQ, w), lambda i: (i, 0))
    pad = pl.BlockSpec((TQ // LANES, CW, LANES), lambda i: (i + 1, 0, 0))
    return pl.pallas_call(
        body, grid=(nt,),
        in_specs=[row(D), row(D), row(CW), row(CW), row(CW), row(CW), pad, pad, _const((1, D)), _any()],
        out_specs=[row(D), _any(), _const((1, D))],
        out_shape=[jax.ShapeDtypeStruct((t, D), F32), jax.ShapeDtypeStruct((NCHIP, D, wc), F32),
                   jax.ShapeDtypeStruct((1, D), F32)],
        scratch_shapes=[pltpu.VMEM((NCHIP, D, wc), BF16), pltpu.VMEM((TQ, PROJ), BF16),
                        pltpu.VMEM((NCHIP, D, wc), F32), pltpu.SemaphoreType.DMA((NCHIP,))],
        compiler_params=_cp(("arbitrary",)), name="bwd_inproj",
    )(dxm, x, dhc, dbg, dcg, dq, dk, dv, g, w_all)


def wgrad(a, b, kb, nb, by_columns, name):
    t, k = a.shape
    n = b.shape[1]
    tk = 512

    def body(a_ref, b_ref, o_ref):
        o_ref[...] = jnp.zeros_like(o_ref)
        for c in range(t // tk):
            o_ref[...] += lax.dot_general(a_ref[tk * c:tk * (c + 1), :], b_ref[tk * c:tk * (c + 1), :], TN,
                                          preferred_element_type=F32)

    if by_columns:
        assert nb == n // NCHIP
        out_spec = pl.BlockSpec((None, kb, nb), lambda ki, ni: (ni, ki, 0))
        out_shape = jax.ShapeDtypeStruct((NCHIP, k, nb), F32)
    else:
        assert nb == n
        out_spec = pl.BlockSpec((kb, nb), lambda ki, ni: (ki, 0))
        out_shape = jax.ShapeDtypeStruct((k, n), F32)
    return pl.pallas_call(
        body, grid=(k // kb, n // nb),
        in_specs=[pl.BlockSpec((t, kb), lambda ki, ni: (0, ki)), pl.BlockSpec((t, nb), lambda ki, ni: (0, ni))],
        out_specs=out_spec, out_shape=out_shape,
        compiler_params=_cp(("arbitrary", "arbitrary")), name=name)(a, b)


TOE = 1024
assert 2 * QG_FWD + LEFT <= TOE
N_FLAT = LEFT - REL_CLIP + 1
N_VAR = BAND - N_FLAT


def _diag_vector(table):
    last = table[:, 2 * REL_CLIP:]
    var = table[:, 2 * REL_CLIP - N_VAR:2 * REL_CLIP][:, ::-1]
    return jnp.concatenate([jnp.broadcast_to(last, (NH, N_FLAT)), var, jnp.broadcast_to(last, (NH, TOE - BAND))], axis=1)


def _diag_vector_bwd(dvec):
    dlast = jnp.sum(dvec[:, :N_FLAT], axis=1, keepdims=True) + jnp.sum(dvec[:, BAND:], axis=1, keepdims=True)
    dvar = dvec[:, N_FLAT:BAND][:, ::-1]
    return jnp.concatenate([jnp.zeros((NH, 2 * REL_CLIP - N_VAR), F32), dvar, dlast], axis=1)


def _band_valid(qg):
    r = lax.broadcasted_iota(jnp.int32, (qg, qg + LEFT), 0)
    p = lax.broadcasted_iota(jnp.int32, (qg, qg + LEFT), 1)
    start = lax.shift_left(lax.shift_right_logical(r, 6), 6)
    return (p >= start) & (p < start + BAND)


def bias_expand(vec, qgs, after=()):
    def body(v_ref, *o_refs):
        for qg, o_ref in zip(qgs, o_refs):
            valid = _band_valid(qg)
            for h in range(NH):
                rows = jnp.broadcast_to(v_ref[h:h + 1, :], (qg, TOE))
                toe = pltpu.roll(rows, 0, 1, stride=1, stride_axis=0)
                o_ref[h // 2, qg * (h % 2):qg * (h % 2 + 1), :] = jnp.where(valid, toe[:, :qg + LEFT], NEG_INF)

    vm = pl.BlockSpec(memory_space=pltpu.VMEM)
    return pl.pallas_call(_behind(body, 1, after), in_specs=[vm] + [_any()] * len(after), out_specs=[vm] * len(qgs),
                          out_shape=[jax.ShapeDtypeStruct((NH // 2, 2 * qg, qg + LEFT), F32) for qg in qgs],
                          name="bias_expand")(vec, *after)


def bias_reduce(db2):
    _, qg, kg = db2.shape

    def body(d_ref, o_ref):
        ii = lax.broadcasted_iota(jnp.int32, (kg, kg), 0)
        jj = lax.broadcasted_iota(jnp.int32, (kg, kg), 1)
        flip = jnp.where(ii + jj == kg - 1, 1.0, 0.0).astype(BF16)
        for h in range(NH):
            rest = d_ref[h]
            rev = jnp.zeros((qg, kg), F32)
            for _ in range(3):
                term = rest.astype(BF16)
                rev = rev + jnp.dot(term, flip, preferred_element_type=F32)
                rest = rest - term.astype(F32)
            d = jnp.concatenate([jnp.zeros((qg, TOE - kg), F32), rev], axis=1)
            back = pltpu.roll(d, 0, 1, stride=1, stride_axis=0)
            o_ref[h:h + 1, :] = jnp.sum(back, axis=0, keepdims=True)

    rev = pl.pallas_call(body, out_shape=jax.ShapeDtypeStruct((NH, TOE), F32), name="bias_reduce")(db2)
    return rev[:, ::-1]


def _place():
    x, y, c = lax.axis_index("x"), lax.axis_index("y"), lax.axis_index("c")
    chips = [(1 - x, y), (x, 1 - y), (1 - x, 1 - y)]
    return x, y, c, chips


def _half(ref_rows, c):
    return pl.ds(c * (ref_rows // 2), ref_rows // 2)


HBM_SPEC = pl.BlockSpec(memory_space=pltpu.HBM)
SEM_SPEC = pl.BlockSpec(memory_space=pltpu.SEMAPHORE)
IN_FLIGHT = pltpu.CompilerParams(has_side_effects=pltpu.SideEffectType.DATAFLOW_SIDE_EFFECTING)


def _in_hbm(a):
    return pltpu.with_memory_space_constraint(a, pltpu.HBM)


def cast_to_slot(ws, chip, layer, after=()):
    n = len(ws)
    steps = 4

    def body(b_ref, *refs):
        del b_ref
        for w_ref, o_ref in zip(refs[:n], refs[n + len(after):]):
            o_ref[...] = w_ref[...].astype(BF16)

    grid_spec = pltpu.PrefetchScalarGridSpec(
        num_scalar_prefetch=1, grid=(steps,),
        in_specs=[pl.BlockSpec((None, w.shape[1] // steps, w.shape[2]), lambda r, b: (layer, r, 0)) for w in ws]
        + [_any()] * len(after),
        out_specs=[pl.BlockSpec((None, w.shape[1] // steps, w.shape[2]), lambda r, b: (b[0], r, 0)) for w in ws])
    return pl.pallas_call(body, grid_spec=grid_spec,
                          out_shape=[jax.ShapeDtypeStruct((NCHIP,) + w.shape[1:], BF16) for w in ws],
                          compiler_params=_cp(("arbitrary",)), name="cast_to_slot")(chip, *ws, *after)


def _gather_copies(bufs, send, recv):
    x, y, c, chips = _place()
    b = 2 * x + y
    out = []
    for k, buf in enumerate(bufs):
        rows = buf.shape[1]
        mine = buf.at[b, _half(rows, c), :]
        for j, (cx, cy) in enumerate(chips):
            theirs = buf.at[2 * cx + cy, _half(rows, c), :]
            sems = dict(send_sem=send.at[3 * k + j], recv_sem=recv.at[3 * k + j],
                        device_id=(cx, cy, c), device_id_type=MESH)
            out.append((pltpu.make_async_remote_copy(src_ref=mine, dst_ref=mine, **sems),
                        pltpu.make_async_remote_copy(src_ref=theirs, dst_ref=theirs, **sems)))
    return out


def gather_start(bufs, after, layer):
    n = len(bufs)

    def body(*refs):
        ins = refs[:n]
        send, recv = refs[n + 1], refs[n + 2]
        token = refs[-1]
        for start, _ in _gather_copies(ins, send, recv):
            start.start()
        token[...] = jnp.zeros_like(token)

    sems = pltpu.SemaphoreType.DMA((3 * n,))
    res = pl.pallas_call(
        body, name=f"gather_start_{layer}",
        in_specs=[HBM_SPEC] * n + [_any()],
        out_specs=[SEM_SPEC, SEM_SPEC] + [HBM_SPEC] * n + [pl.BlockSpec(memory_space=pltpu.VMEM)],
        out_shape=[sems, sems] + [pltpu.HBM(b.shape, b.dtype) for b in bufs] + [jax.ShapeDtypeStruct((8, LANES), F32)],
        input_output_aliases={k: 2 + k for k in range(n)}, compiler_params=IN_FLIGHT,
    )(*[_in_hbm(b) for b in bufs], after)
    return res[0], res[1], res[2:2 + n], res[-1]


def gather_wait(send, recv, bufs, after, layer):
    n = len(bufs)

    def body(*refs):
        ins = refs[:n]
        send_ref, recv_ref = refs[n], refs[n + 1]
        for start, arrival in _gather_copies(ins, send_ref, recv_ref):
            start.wait_send()
            arrival.wait_recv()

    return pl.pallas_call(
        body, name=f"gather_wait_{layer}",
        in_specs=[HBM_SPEC] * n + [SEM_SPEC, SEM_SPEC, _any()], out_specs=[HBM_SPEC] * n,
        out_shape=[pltpu.HBM(b.shape, b.dtype) for b in bufs],
        input_output_aliases={k: k for k in range(n)}, compiler_params=IN_FLIGHT,
    )(*bufs, send, recv, after)


def gather_forward(bufs):
    n = len(bufs)

    def body(*refs):
        outs = refs[n:2 * n]
        send, recv = refs[2 * n:]
        x, y, c, chips = _place()
        cps = []
        for k in range(n):
            rows = outs[k].shape[1]
            for j, (cx, cy) in enumerate(chips):
                sems = dict(send_sem=send.at[3 * k + j], recv_sem=recv.at[3 * k + j],
                            device_id=(x, y, 1 - c), device_id_type=MESH)
                mine = outs[k].at[2 * cx + cy, _half(rows, c), :]
                theirs = outs[k].at[2 * cx + cy, _half(rows, 1 - c), :]
                cp = pltpu.make_async_remote_copy(src_ref=mine, dst_ref=mine, **sems)
                cp.start()
                cps.append((cp, pltpu.make_async_remote_copy(src_ref=theirs, dst_ref=theirs, **sems)))
        for cp, arrival in cps:
            cp.wait_send()
            arrival.wait_recv()

    return pl.pallas_call(
        body, in_specs=[_any()] * n, out_specs=[_any()] * n,
        out_shape=[jax.ShapeDtypeStruct(b.shape, b.dtype) for b in bufs], input_output_aliases={k: k for k in range(n)},
        scratch_shapes=[pltpu.SemaphoreType.DMA((3 * n,)), pltpu.SemaphoreType.DMA((3 * n,))],
        name="gather_forward")(*bufs)


def _forward_copies(bufs, send, recv):
    x, y, c, chips = _place()
    out = []
    for k, buf in enumerate(bufs):
        rows = buf.shape[1]
        for j, (cx, cy) in enumerate(chips):
            sems = dict(send_sem=send.at[3 * k + j], recv_sem=recv.at[3 * k + j],
                        device_id=(x, y, 1 - c), device_id_type=MESH)
            mine = buf.at[2 * cx + cy, _half(rows, c), :]
            theirs = buf.at[2 * cx + cy, _half(rows, 1 - c), :]
            out.append((pltpu.make_async_remote_copy(src_ref=mine, dst_ref=mine, **sems),
                        pltpu.make_async_remote_copy(src_ref=theirs, dst_ref=theirs, **sems)))
    return out


def forward_start(bufs, tag):
    n = len(bufs)

    def body(*refs):
        ins = refs[:n]
        send, recv = refs[n], refs[n + 1]
        token = refs[-1]
        for start, _ in _forward_copies(ins, send, recv):
            start.start()
        token[...] = jnp.zeros_like(token)

    sems = pltpu.SemaphoreType.DMA((3 * n,))
    res = pl.pallas_call(
        body, name=f"forward_start_{tag}", in_specs=[HBM_SPEC] * n,
        out_specs=[SEM_SPEC, SEM_SPEC] + [HBM_SPEC] * n + [pl.BlockSpec(memory_space=pltpu.VMEM)],
        out_shape=[sems, sems] + [pltpu.HBM(b.shape, b.dtype) for b in bufs] + [jax.ShapeDtypeStruct((8, LANES), F32)],
        input_output_aliases={k: 2 + k for k in range(n)}, compiler_params=IN_FLIGHT,
    )(*[_in_hbm(b) for b in bufs])
    return res[0], res[1], res[2:2 + n], res[-1]


def forward_wait(send, recv, bufs, after, tag):
    n = len(bufs)

    def body(*refs):
        ins = refs[:n]
        send_ref, recv_ref = refs[n], refs[n + 1]
        for start, arrival in _forward_copies(ins, send_ref, recv_ref):
            start.wait_send()
            arrival.wait_recv()

    return pl.pallas_call(
        body, name=f"forward_wait_{tag}",
        in_specs=[HBM_SPEC] * n + [SEM_SPEC, SEM_SPEC, _any()], out_specs=[HBM_SPEC] * n,
        out_shape=[pltpu.HBM(b.shape, b.dtype) for b in bufs],
        input_output_aliases={k: k for k in range(n)}, compiler_params=IN_FLIGHT,
    )(*bufs, send, recv, after)


def _exchange_copies(srcs, lands, send, recv):
    x, y, c, _ = _place()
    return [pltpu.make_async_remote_copy(
        src_ref=src.at[:, _half(src.shape[1], 1 - c), :], dst_ref=land, send_sem=send.at[k], recv_sem=recv.at[k],
        device_id=(x, y, 1 - c), device_id_type=MESH) for k, (src, land) in enumerate(zip(srcs, lands))]


def exchange_start(srcs, tag):
    n = len(srcs)
    lands = [lax.empty((s.shape[0], s.shape[1] // 2, s.shape[2]), s.dtype) for s in srcs]

    def body(*refs):
        ins, land_refs = refs[:n], refs[n:2 * n]
        send, recv = refs[2 * n], refs[2 * n + 1]
        token = refs[-1]
        for cp in _exchange_copies(ins, land_refs, send, recv):
            cp.start()
        token[...] = jnp.zeros_like(token)

    sems = pltpu.SemaphoreType.DMA((n,))
    res = pl.pallas_call(
        body, name=f"exchange_start_{tag}",
        in_specs=[HBM_SPEC] * (2 * n),
        out_specs=[SEM_SPEC, SEM_SPEC] + [HBM_SPEC] * (2 * n) + [pl.BlockSpec(memory_space=pltpu.VMEM)],
        out_shape=[sems, sems] + [pltpu.HBM(a.shape, a.dtype) for a in list(srcs) + lands]
        + [jax.ShapeDtypeStruct((8, LANES), F32)],
        input_output_aliases={k: 2 + k for k in range(2 * n)}, compiler_params=IN_FLIGHT,
    )(*[_in_hbm(a) for a in list(srcs) + lands])
    return res[0], res[1], res[2:2 + n], res[2 + n:2 + 2 * n], res[-1]


def exchange_wait(send, recv, srcs, lands, after, tag):
    n = len(srcs)

    def body(*refs):
        ins, land_refs = refs[:n], refs[n:2 * n]
        send_ref, recv_ref = refs[2 * n], refs[2 * n + 1]
        for cp in _exchange_copies(ins, land_refs, send_ref, recv_ref):
            cp.wait_send()
            cp.wait_recv()

    res = pl.pallas_call(
        body, name=f"exchange_wait_{tag}",
        in_specs=[HBM_SPEC] * (2 * n) + [SEM_SPEC, SEM_SPEC, _any()], out_specs=[HBM_SPEC] * (2 * n),
        out_shape=[pltpu.HBM(a.shape, a.dtype) for a in list(srcs) + list(lands)],
        input_output_aliases={k: k for k in range(2 * n)}, compiler_params=IN_FLIGHT,
    )(*srcs, *lands, send, recv, after)
    return res[:n], res[n:]


def add_pair(gs, r1s, core):
    n = len(gs)

    def body(c_ref, *refs):
        del c_ref
        for g_ref, r_ref, o_ref in zip(refs[:n], refs[n:2 * n], refs[2 * n:]):
            o_ref[...] = (g_ref[...] + r_ref[...]).astype(BF16)

    blk = lambda r: (None,) + r.shape[1:]
    grid_spec = pltpu.PrefetchScalarGridSpec(
        num_scalar_prefetch=1, grid=(NCHIP,),
        in_specs=[pl.BlockSpec(blk(r), lambda s, c: (s, c[0], 0)) for r in r1s]
        + [pl.BlockSpec(blk(r), lambda s, c: (s, 0, 0)) for r in r1s],
        out_specs=[pl.BlockSpec(blk(r), lambda s, c: (s, 0, 0)) for r in r1s])
    return pl.pallas_call(body, grid_spec=grid_spec, out_shape=[jax.ShapeDtypeStruct(r.shape, BF16) for r in r1s],
                          compiler_params=_cp(("arbitrary",)), name="add_pair")(core, *gs, *r1s)


def _scatter_copies(srcs, lands, send, recv):
    _, _, c, chips = _place()
    out = []
    for k, (src, land) in enumerate(zip(srcs, lands)):
        for j, (cx, cy) in enumerate(chips):
            out.append(pltpu.make_async_remote_copy(
                src_ref=src.at[2 * cx + cy], dst_ref=land.at[j], send_sem=send.at[3 * k + j],
                recv_sem=recv.at[3 * k + j], device_id=(cx, cy, c), device_id_type=MESH))
    return out


def scatter_start(srcs, layer):
    n = len(srcs)
    srcs = list(srcs)
    lands = [lax.empty((3,) + s.shape[1:], s.dtype) for s in srcs]

    def body(*refs):
        ins, land_refs = refs[:n], refs[n:2 * n]
        send, recv = refs[2 * n], refs[2 * n + 1]
        token = refs[-1]
        for cp in _scatter_copies(ins, land_refs, send, recv):
            cp.start()
        token[...] = jnp.zeros_like(token)

    sems = pltpu.SemaphoreType.DMA((3 * n,))
    res = pl.pallas_call(
        body, name=f"scatter_start_{layer}",
        in_specs=[HBM_SPEC] * (2 * n),
        out_specs=[SEM_SPEC, SEM_SPEC] + [HBM_SPEC] * (2 * n) + [pl.BlockSpec(memory_space=pltpu.VMEM)],
        out_shape=[sems, sems] + [pltpu.HBM(a.shape, a.dtype) for a in srcs + lands]
        + [jax.ShapeDtypeStruct((8, LANES), F32)],
        input_output_aliases={k: 2 + k for k in range(2 * n)}, compiler_params=IN_FLIGHT,
    )(*[_in_hbm(a) for a in srcs + lands])
    return res[0], res[1], res[2:2 + n], res[2 + n:2 + 2 * n], res[-1]


def scatter_wait(send, recv, srcs, lands, after, layer):
    n = len(srcs)

    def body(*refs):
        ins, land_refs = refs[:n], refs[n:2 * n]
        send_ref, recv_ref = refs[2 * n], refs[2 * n + 1]
        for cp in _scatter_copies(ins, land_refs, send_ref, recv_ref):
            cp.wait_send()
            cp.wait_recv()

    res = pl.pallas_call(
        body, name=f"scatter_wait_{layer}",
        in_specs=[HBM_SPEC] * (2 * n) + [SEM_SPEC, SEM_SPEC, _any()], out_specs=[HBM_SPEC] * (2 * n),
        out_shape=[pltpu.HBM(a.shape, a.dtype) for a in list(srcs) + list(lands)],
        input_output_aliases={k: k for k in range(2 * n)}, compiler_params=IN_FLIGHT,
    )(*srcs, *lands, send, recv, after)
    return res[n:]


def add_chips(gs, r1s, r2s, place, totals, layer):
    n = len(gs)
    steps = 2

    def body(p_ref, *refs):
        del p_ref
        for g_ref, r1_ref, r2_ref, o_ref in zip(refs[:n], refs[n:2 * n], refs[2 * n:3 * n], refs[4 * n:]):
            own = g_ref[...] + r1_ref[...]
            o_ref[...] = ((own + r2_ref[0].astype(F32)) + r2_ref[1].astype(F32)) + r2_ref[2].astype(F32)

    blk = lambda r: (None, r.shape[1] // steps, r.shape[2])
    grid_spec = pltpu.PrefetchScalarGridSpec(
        num_scalar_prefetch=1, grid=(steps,),
        in_specs=[pl.BlockSpec(blk(r), lambda i, p: (p[1], p[0] * steps + i, 0)) for r in r1s]
        + [pl.BlockSpec(blk(r), lambda i, p: (p[1], i, 0)) for r in r1s]
        + [pl.BlockSpec((3,) + blk(r)[1:], lambda i, p: (0, i, 0)) for r in r1s] + [_any()] * n,
        out_specs=[pl.BlockSpec(blk(r), lambda i, p: (layer, p[0] * steps + i, 0)) for r in r1s])
    return pl.pallas_call(body, grid_spec=grid_spec, out_shape=[jax.ShapeDtypeStruct(t.shape, F32) for t in totals],
                          input_output_aliases={1 + 3 * n + k: k for k in range(n)},
                          compiler_params=_cp(("arbitrary",)), name="add_chips")(place, *gs, *r1s, *r2s, *totals)


def _share_copies(bufs, send, recv):
    x, y, c, _ = _place()
    out = []
    for k, buf in enumerate(bufs):
        sems = dict(send_sem=send.at[k], recv_sem=recv.at[k], device_id=(x, y, 1 - c), device_id_type=MESH)
        mine = buf.at[:, _half(buf.shape[1], c), :]
        theirs = buf.at[:, _half(buf.shape[1], 1 - c), :]
        out.append((pltpu.make_async_remote_copy(src_ref=mine, dst_ref=mine, **sems),
                    pltpu.make_async_remote_copy(src_ref=theirs, dst_ref=theirs, **sems)))
    return out


def share_start(bufs, tag):
    n = len(bufs)

    def body(*refs):
        ins = refs[:n]
        send, recv = refs[n], refs[n + 1]
        token = refs[-1]
        for start, _ in _share_copies(ins, send, recv):
            start.start()
        token[...] = jnp.zeros_like(token)

    sems = pltpu.SemaphoreType.DMA((n,))
    res = pl.pallas_call(
        body, name=f"share_start_{tag}", in_specs=[HBM_SPEC] * n,
        out_specs=[SEM_SPEC, SEM_SPEC] + [HBM_SPEC] * n + [pl.BlockSpec(memory_space=pltpu.VMEM)],
        out_shape=[sems, sems] + [pltpu.HBM(b.shape, b.dtype) for b in bufs] + [jax.ShapeDtypeStruct((8, LANES), F32)],
        input_output_aliases={k: 2 + k for k in range(n)}, compiler_params=IN_FLIGHT,
    )(*[_in_hbm(b) for b in bufs])
    return res[0], res[1], res[2:2 + n], res[-1]


def share_wait(send, recv, bufs, after, tag):
    n = len(bufs)

    def body(*refs):
        ins = refs[:n]
        send_ref, recv_ref = refs[n], refs[n + 1]
        for start, arrival in _share_copies(ins, send_ref, recv_ref):
            start.wait_send()
            arrival.wait_recv()

    return pl.pallas_call(
        body, name=f"share_wait_{tag}",
        in_specs=[HBM_SPEC] * n + [SEM_SPEC, SEM_SPEC, _any()], out_specs=[HBM_SPEC] * n,
        out_shape=[pltpu.HBM(b.shape, b.dtype) for b in bufs],
        input_output_aliases={k: k for k in range(n)}, compiler_params=IN_FLIGHT,
    )(*bufs, send, recv, after)


def small_allreduce(v, after=()):
    rows = v.shape[0]
    flips = [(fx, fy, fc) for fx in (0, 1) for fy in (0, 1) for fc in (0, 1)][1:]

    def body(v_ref, o_ref, buf, send, recv):
        x, y, c, _ = _place()
        buf[4 * x + 2 * y + c] = v_ref[...]
        peers = [(jnp.where(fx, 1 - x, x), jnp.where(fy, 1 - y, y), jnp.where(fc, 1 - c, c)) for fx, fy, fc in flips]
        cps = []
        for k, peer in enumerate(peers):
            cp = pltpu.make_async_remote_copy(
                src_ref=v_ref, dst_ref=buf.at[4 * x + 2 * y + c], send_sem=send.at[k], recv_sem=recv.at[k],
                device_id=peer, device_id_type=MESH)
            cp.start()
            cps.append(cp)
        for k, (px, py, pc) in enumerate(peers):
            pltpu.make_async_remote_copy(
                src_ref=v_ref, dst_ref=buf.at[4 * px + 2 * py + pc], send_sem=send.at[k], recv_sem=recv.at[k],
                device_id=(px, py, pc), device_id_type=MESH).wait_recv()
        for cp in cps:
            cp.wait_send()
        acc = buf[0]
        for s in range(1, 8):
            acc = acc + buf[s]
        o_ref[...] = acc

    vm = pl.BlockSpec(memory_space=pltpu.VMEM)
    return pl.pallas_call(
        _behind(body, 1, after), in_specs=[vm] + [_any()] * len(after), out_specs=vm,
        out_shape=jax.ShapeDtypeStruct((rows, SMALL_COLS), F32),
        scratch_shapes=[pltpu.VMEM((8, rows, SMALL_COLS), F32), pltpu.SemaphoreType.DMA((7,)),
                        pltpu.SemaphoreType.DMA((7,))],
        name="reduce_small")(v, *after)


def adamw(w, g, m, v, rb, name, after=()):
    nl, rows, cols = w.shape

    def body(w_ref, g_ref, m_ref, v_ref, go_ref, d_ref, nm_ref, nv_ref):
        gv = g_ref[...]
        go_ref[...] = gv
        nm = ADAM_B1 * m_ref[...] + (1.0 - ADAM_B1) * gv
        nv = ADAM_B2 * v_ref[...] + (1.0 - ADAM_B2) * (gv * gv)
        m_hat = nm / (1.0 - ADAM_B1 ** ADAM_STEP)
        v_hat = nv / (1.0 - ADAM_B2 ** ADAM_STEP)
        d_ref[...] = -ADAM_LR * (m_hat / (jnp.sqrt(v_hat) + ADAM_EPS) + ADAM_WD * w_ref[...])
        nm_ref[...] = nm
        nv_ref[...] = nv

    blk = pl.BlockSpec((None, rb, cols), lambda l, r: (l, r, 0))
    shp = jax.ShapeDtypeStruct(w.shape, F32)
    return pl.pallas_call(_behind(body, 4, after), grid=(nl, rows // rb), in_specs=[blk] * 4 + [_any()] * len(after),
                          out_specs=[blk] * 4, out_shape=[shp] * 4,
                          compiler_params=_cp(("arbitrary", "arbitrary")), name=name)(w, g, m, v, *after)


def _pack(parts, rows):
    flat = jnp.concatenate([p.reshape(-1).astype(F32) for p in parts])
    return jnp.pad(flat, (0, rows * SMALL_COLS - flat.shape[0])).reshape(rows, SMALL_COLS)


def _unpack(vec, shapes):
    flat = vec.reshape(-1)
    out, off = [], 0
    for s in shapes:
        size = 1
        for d in s:
            size *= d
        out.append(flat[off:off + size].reshape(s))
        off += size
    return out


def kernel(x, w_in, w_conv, rel_bias, g_conv_out, g_attn_out, w_out, g_pre_mix, g_post_mix, g_pre_ffn, g_post_ffn, w_ffn_in, w_ffn_out, loss_target, m_w_in, m_w_conv, m_rel_bias, m_g_conv_out, m_g_attn_out, m_w_out, m_g_pre_mix, m_g_post_mix, m_g_pre_ffn, m_g_post_ffn, m_w_ffn_in, m_w_ffn_out, v_w_in, v_w_conv, v_rel_bias, v_g_conv_out, v_g_attn_out, v_w_out, v_g_pre_mix, v_g_post_mix, v_g_pre_ffn, v_g_post_ffn, v_w_ffn_in, v_w_ffn_out):
    xi, yi, ci = lax.axis_index("x"), lax.axis_index("y"), lax.axis_index("c")
    chip = 2 * xi + yi
    nl = w_in.shape[0]
    x0 = x[0]
    target = loss_target[0]
    cwl = CW // NCHIP

    chip1 = chip.reshape(1).astype(jnp.int32)
    big_weights = [w_in, w_out, w_ffn_in, w_ffn_out]
    own = [cast_to_slot(big_weights, chip1, 0)]
    wc_mine = jnp.pad(w_conv.reshape(-1), (0, 16 * LANES - w_conv.size)).reshape(1, 16, LANES)
    wc_slot = lax.dynamic_update_slice_in_dim(jnp.zeros((NCHIP, 16, LANES), F32), wc_mine, chip, axis=0)
    gm = jnp.kron(jnp.eye(CW // HD, dtype=F32), jnp.full((HD, HD), 1.0 / HD, F32)).astype(BF16)
    row = lambda a, l: a[l][None, :]

    def gather_finish(flight, after, tag):
        send, recv, bufs, _ = flight
        return gather_forward(gather_wait(send, recv, bufs, after, tag))

    first_mix = gather_start(list(own[0][:2]) + [wc_slot], x0, "0m")
    first_ffn = gather_start(own[0][2:], first_mix[3], "0f")
    chain = first_ffn[3]
    biases = []
    for l in range(nl):
        biases.append(bias_expand(_diag_vector(rel_bias[l]), (QG_FWD, QG_BWD), [chain]))
        chain = biases[l][1]
    for l in range(1, nl):
        own.append(cast_to_slot(big_weights, chip1, l, [chain]))
        chain = own[l][0]
    gw_in, gw_out, wc_all = gather_finish(first_mix, chain, "0m")
    wc_full = wc_all.reshape(NCHIP, -1)[:, :nl * cwl * 3].reshape(NCHIP, nl, cwl, 3)
    wc_full = jnp.transpose(wc_full, (1, 0, 2, 3)).reshape(nl, CW, 3)
    wconv_t = jnp.pad(jnp.transpose(wc_full, (0, 2, 1)), ((0, 0), (0, 5), (0, 0)))
    flights, to_sibling = {}, None
    saved, weights = [], []
    h = x0
    for l in range(nl):
        if l == 0:
            pass
        elif l == 1:
            flights[2] = gather_start(own[2], h, 2)
            gw_in, gw_out, gw_fi, gw_fo = gather_finish(flights[l], flights[2][3], l)
        else:
            gw_in, gw_out, gw_fi, gw_fo = forward_wait(*to_sibling[:3], h, l)
        gw_out = gw_out.reshape(D, D)
        behind_mix, behind_ffn = ([first_ffn[3]] if l == 0 else []), []
        if l + 1 < nl and l + 1 not in flights:
            flights[l + 1] = gather_start(own[l + 1], first_ffn[3] if l == 0 else gw_in, l + 1)
            behind_mix.append(flights[l + 1][3])
        bias2, bias2_bwd = biases[l]
        proj = fwd_inproj(h, row(g_pre_mix, l), gw_in, behind_mix)
        xmid, o, lse, y, z = fwd_mix(h, proj, bias2, wconv_t[l], row(g_conv_out, l), row(g_attn_out, l),
                                     row(g_post_mix, l), gm, gw_out)
        if l == 0:
            gw_fi, gw_fo = gather_finish(first_ffn, xmid, "0f")
        elif l + 1 < nl:
            send, recv, bufs, _ = flights[l + 1]
            landed = gather_wait(send, recv, bufs, xmid, l + 1)
            to_sibling = forward_start(landed, l + 1)
            behind_ffn.append(to_sibling[3])
            if l + 2 < nl:
                flights[l + 2] = gather_start(own[l + 2], to_sibling[3], l + 2)
                behind_ffn.append(flights[l + 2][3])
        gw_fo = gw_fo.reshape(2, DFF // 2, D)
        ffn = fwd_ffn(xmid, row(g_pre_ffn, l), row(g_post_ffn, l), gw_fi, gw_fo, behind_ffn,
                      target if l == nl - 1 else None)
        gu, f = ffn[:2]
        saved.append((h, proj, bias2_bwd, xmid, o, lse, y, z, gu, f))
        weights.append((gw_in, gw_out, gw_fi, gw_fo))
        h = ffn[2]
    dx, loss_blk = ffn[2], ffn[3]

    core = ci.reshape(1).astype(jnp.int32)
    place = jnp.stack([ci, chip]).astype(jnp.int32)
    totals = [lax.empty(w.shape, F32) for w in (w_in, w_out, w_ffn_in, w_ffn_out)]
    small = {k: [None] * nl for k in ("co", "ao", "pm", "qm", "pf", "qf", "rel", "wc")}

    def reduce_begin(kinds, grads, tag):
        return kinds, exchange_start(grads, tag), tag

    def reduce_mid(state, after):
        kinds, (send, recv, srcs, lands, _), tag = state
        grads, from_sibling = exchange_wait(send, recv, srcs, lands, after, tag)
        return kinds, grads, from_sibling, scatter_start(add_pair(grads, from_sibling, core), tag), tag

    def reduce_end(state, after, totals, layer):
        kinds, grads, from_sibling, (send, recv, srcs, lands, _), tag = state
        from_chips = scatter_wait(send, recv, srcs, lands, after, tag)
        totals = list(totals)
        summed = add_chips(grads, from_sibling, from_chips, place, [totals[i] for i in kinds], layer)
        for i, t in zip(kinds, summed):
            totals[i] = t
        return totals

    begun = flying = None
    for l in reversed(range(nl)):
        hin, proj, bias2, xmid, o, lse, y, z, gu, f = saved[l]
        gw_in, gw_out, gw_fi, gw_fo = weights[l]
        behind_ffn = [begun[1][4]] if begun is not None else []
        dxm, dfb, act, dgu, h2, dg_qf, dg_pf = bwd_ffn(dx, f, xmid, gu, row(g_pre_ffn, l), row(g_post_ffn, l),
                                                        gw_fi, gw_fo, behind_ffn)
        behind_mix, behind_conv = [], []
        if begun is not None:
            flying = reduce_mid(begun, dxm)
            behind_mix.append(flying[3][4])
        gr_fo = wgrad(act, dfb, 256, D, False, "wgrad_ffn_out").reshape(NCHIP, DFF // NCHIP, D)
        gr_fi = wgrad(h2, dgu, 512, 2 * DFF // NCHIP, True, "wgrad_ffn_in")
        if l == 0:
            begun_ffn = reduce_begin([2, 3], [gr_fi, gr_fo], "0f")
            behind_mix.append(begun_ffn[1][4])
        gr_out, do, dco, dbg, dg_qm, dg_co, dg_ao = bwd_mix(dxm, z, o, y, proj, wconv_t[l], row(g_conv_out, l),
                                                             row(g_attn_out, l), row(g_post_mix, l), gm, gw_out,
                                                             behind_mix)
        gr_out = gr_out.reshape(NCHIP, D // NCHIP, D)
        if l == 0:
            flying_ffn = reduce_mid(begun_ffn, do)
            behind_conv.append(flying_ffn[3][4])
        dhc, dcg, dwc = bwd_conv(dco, proj, wconv_t[l], behind_conv)
        dq, dk, dv, db2 = bwd_attn(proj, o, do, lse, bias2)
        dx, gr_in, dg_pm = bwd_inproj(dxm, hin, dhc, dbg, dcg, dq, dk, dv, row(g_pre_mix, l), gw_in)
        if flying is not None:
            totals = reduce_end(flying, dx, totals, l + 1)
        small["co"][l], small["ao"][l], small["pm"][l], small["qm"][l] = dg_co, dg_ao, dg_pm, dg_qm
        small["pf"][l], small["qf"][l] = dg_pf, dg_qf
        small["rel"][l] = _diag_vector_bwd(bias_reduce(db2.reshape(NH, QG_BWD, QG_BWD + LEFT)))
        small["wc"][l] = jnp.transpose(dwc[0:3], (1, 0))
        if l > 0:
            begun = reduce_begin([0, 1, 2, 3], [gr_in, gr_out, gr_fi, gr_fo], l)
    begun_mix = reduce_begin([0, 1], [gr_in, gr_out], "0m")
    totals = reduce_end(flying_ffn, begun_mix[1][4], totals, 0)
    flying_mix = reduce_mid(begun_mix, totals[2])
    share_ffn = share_start(totals[2:], "ffn")

    order = ("co", "ao", "pm", "qm", "pf", "qf", "rel", "wc")
    parts = [jnp.stack(small[k]) for k in order] + [loss_blk[0:1, 0:1]]
    shapes = [p.shape for p in parts]
    red_vec = small_allreduce(_pack(parts, 40), [share_ffn[3], flying_mix[3][4]])
    red = _unpack(red_vec, shapes)

    gr_fi, gr_fo = share_wait(*share_ffn[:3], red_vec, "ffn")
    big_fi = adamw(w_ffn_in, gr_fi, m_w_ffn_in, v_w_ffn_in, w_ffn_in.shape[1] // 4, "adamw_ffn_in")
    totals = reduce_end(flying_mix, big_fi[1], totals, 0)
    share_mix = share_start(totals[:2], "mix")
    big_fo = adamw(w_ffn_out, gr_fo, m_w_ffn_out, v_w_ffn_out, w_ffn_out.shape[1] // 4, "adamw_ffn_out",
                   [share_mix[3]])
    gr_in, gr_out = share_wait(*share_mix[:3], big_fo[1], "mix")
    big_in = adamw(w_in, gr_in, m_w_in, v_w_in, w_in.shape[1] // 4, "adamw_in")
    big_out = adamw(w_out, gr_out, m_w_out, v_w_out, w_out.shape[1] // 4, "adamw_out")
    big = [big_in, big_out, big_fi, big_fo]
    gr_co, gr_ao, gr_pm, gr_qm, gr_pf, gr_qf, gr_rel, gr_wc_full, loss = red
    gr_co, gr_ao, gr_pm, gr_qm, gr_pf, gr_qf = [a.reshape(nl, -1) for a in (gr_co, gr_ao, gr_pm, gr_qm, gr_pf, gr_qf)]
    gr_wc = lax.dynamic_slice_in_dim(gr_wc_full, chip * cwl, cwl, axis=1)
    loss = loss.reshape(())

    sw = [g_conv_out, g_attn_out, g_pre_mix, g_post_mix, g_pre_ffn, g_post_ffn, rel_bias, w_conv]
    sg = [gr_co, gr_ao, gr_pm, gr_qm, gr_pf, gr_qf, gr_rel, gr_wc]
    sm = [m_g_conv_out, m_g_attn_out, m_g_pre_mix, m_g_post_mix, m_g_pre_ffn, m_g_post_ffn, m_rel_bias, m_w_conv]
    sv = [v_g_conv_out, v_g_attn_out, v_g_pre_mix, v_g_post_mix, v_g_pre_ffn, v_g_post_ffn, v_rel_bias, v_w_conv]
    sshapes = [a.shape for a in sw]
    packed = [_pack(a, 32)[None] for a in (sw, sg, sm, sv)]
    s_out = [_unpack(a[0], sshapes) for a in adamw(*packed, 32, "adamw_small")]

    def leaves(big_i, small_i):
        b_in, b_out, b_fi, b_fo = big_i
        s_co, s_ao, s_pm, s_qm, s_pf, s_qf, s_rel, s_wc = small_i
        return [b_in, s_wc, s_rel, s_co, s_ao, b_out, s_pm, s_qm, s_pf, s_qf, b_fi, b_fo]

    out = [loss, dx[None]]
    out += leaves([b[0] for b in big], sg)
    for i in range(1, 4):
        out += leaves([b[i] for b in big], s_out[i])
    return tuple(out)
```

```python
import jax
import jax.numpy as jnp
from jax import lax
from jax.experimental import pallas as pl
from jax.experimental.pallas import tpu as pltpu

F32 = jnp.float32
BF16 = jnp.bfloat16

D = 1024
PROJ = 3072
CW = 512
HD = 64
NH = 8
CHUNK = 64
BAND = 576
REL_CLIP = 128
NREL = 2 * REL_CLIP + 1
DFF = 2816
DEPTH = 4
NCHIP = 4
EPS = 1e-6
NEG_INF = -1e30

ADAM_LR = 0.001
ADAM_B1 = 0.9
ADAM_B2 = 0.999
ADAM_EPS = 1e-08
ADAM_WD = 0.01
ADAM_STEP = 10

V7X_VMEM_BYTES = 64 * 1024 * 1024
VMEM_LIMIT = V7X_VMEM_BYTES - 8 * 1024 * 1024
LANES = 128
QG_FWD = 4 * CHUNK
QG_BWD = 2 * CHUNK
LEFT = BAND - CHUNK
TQ = 512
TM = 256
SMALL_COLS = 1024
MESH = pl.DeviceIdType.MESH
NT = (((1,), (1,)), ((), ()))
TN = (((0,), (0,)), ((), ()))


def _cp(sem=None, vmem=VMEM_LIMIT):
    return pltpu.CompilerParams(dimension_semantics=sem, vmem_limit_bytes=vmem)


def _any():
    return pl.BlockSpec(memory_space=pl.ANY)


def _const(shape):
    nd = len(shape)
    return pl.BlockSpec(shape, lambda *_: (0,) * nd)


def _behind(body, n_in, after):
    def ordered(*refs):
        return body(*refs[:n_in], *refs[n_in + len(after):])
    return ordered


def _rms(v, g):
    r = lax.rsqrt(jnp.mean(v * v, axis=-1, keepdims=True) + EPS)
    return v * r * g


def _rms_bwd(dy, v, g):
    r = lax.rsqrt(jnp.mean(v * v, axis=-1, keepdims=True) + EPS)
    vh = v * r
    dg = jnp.sum(dy * vh, axis=0, keepdims=True)
    dvh = dy * g
    dv = r * (dvh - vh * jnp.mean(dvh * vh, axis=-1, keepdims=True))
    return dv, dg


def _group_mean(v, gm):
    return jnp.dot(v.astype(BF16), gm, preferred_element_type=F32)


def _group_rms_bwd(dy, v, g, gm):
    r = lax.rsqrt(_group_mean(v * v, gm) + EPS)
    vh = v * r
    dg = jnp.sum(dy * vh, axis=0, keepdims=True)
    dvh = dy * g
    dv = r * (dvh - vh * _group_mean(dvh * vh, gm))
    return dv, dg


def _head_masks(scale):
    lane = lax.broadcasted_iota(jnp.int32, (1, LANES), 1)
    return [jnp.where((lane >= HD * a) & (lane < HD * (a + 1)), scale, 0.0).astype(BF16) for a in range(2)]


class _Resident:
    def __init__(self, src, dst, sem):
        self.first = pl.program_id(0) == 0
        self.copy = pltpu.make_async_copy(src, dst, sem)
        self.dst = dst

        @pl.when(self.first)
        def _():
            self.copy.start()

    def read(self):
        @pl.when(self.first)
        def _():
            self.copy.wait()

        return self.dst[...]


FF_CHUNKS = ((0, 1536), (1536, DFF))


def _stream_ffn_weights(wfi_hbm, wfo_hbm, wfi_v, wfo_v, sems, order, step):
    hw = DFF // 2
    per_matrix = {
        0: [(wfi_hbm.at[j], wfi_v.at[0, :, pl.ds(hw * j, hw)]) for j in range(2)],
        1: [(wfi_hbm.at[2 + j], wfi_v.at[1, :, pl.ds(hw * j, hw)]) for j in range(2)],
        2: [(wfo_hbm.at[j], wfo_v.at[pl.ds(hw * j, hw), :]) for j in range(2)],
    }
    pieces = [p for m in order for p in per_matrix[m]]
    slot = {m: 2 * k for k, m in enumerate(order)}

    def make_step(wait):
        def ready(m, chunk):
            if chunk == 0:
                wait(slot[m])
                wait(slot[m] + 1)
        return lambda: step(ready)

    copies = [pltpu.make_async_copy(src, dst, sems.at[k]) for k, (src, dst) in enumerate(pieces)]
    first = pl.program_id(0) == 0

    @pl.when(first)
    def _():
        for cp in copies:
            cp.start()
        make_step(lambda k: copies[k].wait())()

    @pl.when(jnp.logical_not(first))
    def _():
        make_step(lambda k: None)()


def _stream_shards(w_hbm, w_v, sems, step):
    copies = [pltpu.make_async_copy(w_hbm.at[b], w_v.at[b], sems.at[b]) for b in range(NCHIP)]
    first = pl.program_id(0) == 0

    @pl.when(first)
    def _():
        for cp in copies:
            cp.start()
        step(lambda b: copies[b].wait())

    @pl.when(jnp.logical_not(first))
    def _():
        step(lambda b: None)


def _conv_taps(u_prev, u, scr):
    n = u.shape[0]
    scr[0:16, :] = u_prev
    scr[16:16 + n, :] = u
    return scr[15:15 + n, :], scr[14:14 + n, :]


def fwd_inproj(x, g, w_all, after=()):
    t = x.shape[0]
    wc = PROJ // NCHIP

    def body(x_ref, g_ref, w_hbm, o_ref, w_v, sems):
        def step(ready):
            h = _rms(x_ref[...], g_ref[...]).astype(BF16)
            for b in range(NCHIP):
                ready(b)
                o_ref[:, wc * b:wc * (b + 1)] = jnp.dot(h, w_v[b], preferred_element_type=F32).astype(BF16)

        _stream_shards(w_hbm, w_v, sems, step)

    return pl.pallas_call(
        _behind(body, 3, after), grid=(t // TQ,),
        in_specs=[pl.BlockSpec((TQ, D), lambda i: (i, 0)), _const((1, D)), _any()] + [_any()] * len(after),
        out_specs=pl.BlockSpec((TQ, PROJ), lambda i: (i, 0)),
        out_shape=jax.ShapeDtypeStruct((t, PROJ), BF16),
        scratch_shapes=[pltpu.VMEM((NCHIP, D, wc), BF16), pltpu.SemaphoreType.DMA((NCHIP,))],
        compiler_params=_cp(("arbitrary",)), name="fwd_inproj")(x, g, w_all, *after)


def _attn_window_specs():
    return [
        pl.BlockSpec((TQ, CW), lambda i: (i, 3)),
        pl.BlockSpec((TQ, CW), lambda i: (jnp.maximum(i - 1, 0), 4)),
        pl.BlockSpec((TQ, CW), lambda i: (i, 4)),
        pl.BlockSpec((TQ, CW), lambda i: (jnp.maximum(i - 1, 0), 5)),
        pl.BlockSpec((TQ, CW), lambda i: (i, 5)),
    ]


def _conv_specs():
    return [
        pl.BlockSpec((TQ, 3 * CW), lambda i: (i, 0)),
        pl.BlockSpec((16, 3 * CW), lambda i: (jnp.maximum(i * (TQ // 16) - 1, 0), 0)),
    ]


def _conv_fwd(pc_ref, pcp_ref, wc_ref, scr, first):
    pc = pc_ref[...].astype(F32)
    hc, bg, cg = pc[:, :CW], pc[:, CW:2 * CW], pc[:, 2 * CW:]
    u = cg * hc
    pp = pcp_ref[...].astype(F32)
    u_prev = jnp.where(first, 0.0, pp[:, 2 * CW:] * pp[:, :CW])
    u1, u2 = _conv_taps(u_prev, u, scr)
    cout = wc_ref[0:1, :] * u2 + wc_ref[1:2, :] * u1 + wc_ref[2:3, :] * u
    return hc, bg, cg, u, u1, u2, cout


def _key_penalty(first, r0, kg):
    col = lax.broadcasted_iota(jnp.int32, (1, kg), 1)
    limit = jnp.where(first, TQ - r0, 0)
    return jnp.where(col < limit, NEG_INF, 0.0)


def fwd_mix(x, proj, bias2, wconv_t, g_co, g_ao, g_pm, gm, wout_all):
    t = x.shape[0]
    qg, kg = QG_FWD, QG_FWD + LEFT

    def body(x_ref, pc_ref, pcp_ref, q_ref, kp_ref, kc_ref, vp_ref, vc_ref, b2_ref, wc_ref, gco_ref, gao_ref, gpm_ref,
             gm_ref, wout_hbm, xmid_ref, o_ref, lse_ref, y_ref, z_ref, wout_v, kwin, vwin, cscr, sems):
        i = pl.program_id(0)
        first = i == 0
        wout = _Resident(wout_hbm, wout_v, sems.at[0])
        kwin[0:TQ, :] = kp_ref[...]
        kwin[TQ:2 * TQ, :] = kc_ref[...]
        vwin[0:TQ, :] = vp_ref[...]
        vwin[TQ:2 * TQ, :] = vc_ref[...]
        qmask = _head_masks(HD ** -0.5)
        low = lax.broadcasted_iota(jnp.int32, (1, LANES), 1) < HD

        def group(g, carry):
            r0 = pl.multiple_of(g * qg, qg)
            pen = _key_penalty(first, r0, kg)
            for hp in range(NH // 2):
                ls = slice(LANES * hp, LANES * (hp + 1))
                qb = q_ref[pl.ds(r0, qg), ls]
                q2 = jnp.concatenate([qb * qmask[0], qb * qmask[1]], axis=0)
                s = lax.dot_general(q2, kwin[pl.ds(r0, kg), ls], NT, preferred_element_type=F32)
                s = s + b2_ref[hp] + pen
                m = jnp.max(s, axis=-1, keepdims=True)
                p = jnp.exp(s - m)
                l = jnp.sum(p, axis=-1, keepdims=True)
                o2 = jnp.dot(p.astype(BF16), vwin[pl.ds(r0, kg), ls], preferred_element_type=F32) * (1.0 / l)
                lse2 = m + jnp.log(l)
                o_ref[pl.ds(r0, qg), ls] = jnp.where(low, o2[:qg], o2[qg:])
                lse_ref[pl.ds(r0, qg), ls] = jnp.where(low, lse2[:qg], lse2[qg:])
            return carry

        lax.fori_loop(0, TQ // qg, group, 0)

        _, bg, _, _, _, _, cout = _conv_fwd(pc_ref, pcp_ref, wc_ref, cscr, first)
        yc = bg * cout
        gmv = gm_ref[...]
        ycn = yc * lax.rsqrt(_group_mean(yc * yc, gmv) + EPS) * gco_ref[...]
        oa = o_ref[...]
        oan = oa * lax.rsqrt(_group_mean(oa * oa, gmv) + EPS) * gao_ref[...]
        y_ref[:, 0:CW] = ycn.astype(BF16)
        y_ref[:, CW:2 * CW] = oan.astype(BF16)
        z = jnp.dot(y_ref[...], wout.read(), preferred_element_type=F32)
        z_ref[...] = z
        xmid_ref[...] = x_ref[...] + _rms(z, gpm_ref[...])

    row = lambda w: pl.BlockSpec((TQ, w), lambda i: (i, 0))
    return pl.pallas_call(
        body, grid=(t // TQ,),
        in_specs=[row(D)] + _conv_specs() + _attn_window_specs() + [
            _const((NH // 2, 2 * qg, kg)), _const((8, CW)), _const((1, CW)), _const((1, CW)), _const((1, D)),
            _const((CW, CW)), _any()],
        out_specs=[row(D), row(CW), row(CW), row(D), row(D)],
        out_shape=[jax.ShapeDtypeStruct((t, D), F32), jax.ShapeDtypeStruct((t, CW), F32),
                   jax.ShapeDtypeStruct((t, CW), F32), jax.ShapeDtypeStruct((t, D), BF16),
                   jax.ShapeDtypeStruct((t, D), F32)],
        scratch_shapes=[pltpu.VMEM((D, D), BF16), pltpu.VMEM((2 * TQ, CW), BF16), pltpu.VMEM((2 * TQ, CW), BF16),
                        pltpu.VMEM((TQ + 16, CW), F32), pltpu.SemaphoreType.DMA((1,))],
        compiler_params=_cp(("arbitrary",)), name="fwd_mix",
    )(x, proj, proj, proj, proj, proj, proj, proj, bias2, wconv_t, g_co, g_ao, g_pm, gm, wout_all)


def fwd_ffn(xmid, g_pre, g_post, wfi_all, wfo_all, after=(), target=None):
    t = xmid.shape[0]
    n_in = 5 if target is None else 6

    def body(*refs):
        x_ref, gpre_ref, gpost_ref, wfi_hbm, wfo_hbm = refs[:5]
        t_ref = None if target is None else refs[5]
        gu_ref, f_ref, xo_ref = refs[n_in:n_in + 3]
        l_ref = None if target is None else refs[n_in + 3]
        wfi_v, wfo_v, sems = refs[-3:]

        if target is not None:
            @pl.when(pl.program_id(0) == 0)
            def _():
                l_ref[...] = jnp.zeros_like(l_ref)

        def step(ready):
            xv = x_ref[...]
            h = _rms(xv, gpre_ref[...]).astype(BF16)
            f = jnp.zeros((TQ, D), F32)
            for ci, (a, b) in enumerate(FF_CHUNKS):
                ready(0, ci)
                gate = jnp.dot(h, wfi_v[0, :, a:b], preferred_element_type=F32)
                ready(1, ci)
                up = jnp.dot(h, wfi_v[1, :, a:b], preferred_element_type=F32)
                gu_ref[:, a:b] = gate.astype(BF16)
                gu_ref[:, DFF + a:DFF + b] = up.astype(BF16)
                act = gate * (1.0 / (1.0 + jnp.exp(-gate))) * up
                ready(2, ci)
                f = f + jnp.dot(act.astype(BF16), wfo_v[a:b, :], preferred_element_type=F32)
            f_ref[...] = f
            xo = xv + _rms(f, gpost_ref[...])
            if target is None:
                xo_ref[...] = xo
            else:
                e = xo - t_ref[...]
                xo_ref[...] = e * (1.0 / D)
                rows = jnp.sum(e * e, axis=-1, keepdims=True) * (1.0 / D)
                l_ref[...] += 0.5 * jnp.sum(rows, axis=0, keepdims=True)

        _stream_ffn_weights(wfi_hbm, wfo_hbm, wfi_v, wfo_v, sems, (0, 1, 2), step)

    row = lambda w: pl.BlockSpec((TQ, w), lambda i: (i, 0))
    with_loss = target is not None
    return pl.pallas_call(
        _behind(body, n_in, after), grid=(t // TQ,),
        in_specs=[row(D), _const((1, D)), _const((1, D)), _any(), _any()] + [row(D)] * with_loss
        + [_any()] * len(after),
        out_specs=[row(2 * DFF), row(D), row(D)] + [_const((8, LANES))] * with_loss,
        out_shape=[jax.ShapeDtypeStruct((t, 2 * DFF), BF16), jax.ShapeDtypeStruct((t, D), F32),
                   jax.ShapeDtypeStruct((t, D), F32)] + [jax.ShapeDtypeStruct((8, LANES), F32)] * with_loss,
        scratch_shapes=[pltpu.VMEM((2, D, DFF), BF16), pltpu.VMEM((DFF, D), BF16), pltpu.SemaphoreType.DMA((6,))],
        compiler_params=_cp(("arbitrary",)), name="fwd_ffn_loss" if with_loss else "fwd_ffn",
    )(xmid, g_pre, g_post, wfi_all, wfo_all, *([target] * with_loss), *after)


def bwd_ffn(dx, f, xmid, gu, g_pre, g_post, wfi_all, wfo_all, after=()):
    t = dx.shape[0]

    def body(dx_ref, f_ref, x_ref, gu_ref, gpre_ref, gpost_ref, wfi_hbm, wfo_hbm,
             dxm_ref, df_ref, act_ref, dgu_ref, h_ref, dgpost_ref, dgpre_ref, wfi_v, wfo_v, sems):
        @pl.when(pl.program_id(0) == 0)
        def _():
            dgpost_ref[...] = jnp.zeros_like(dgpost_ref)
            dgpre_ref[...] = jnp.zeros_like(dgpre_ref)

        def step(ready):
            dxo = dx_ref[...]
            df, dgp = _rms_bwd(dxo, f_ref[...], gpost_ref[...])
            dgpost_ref[...] += dgp
            dfb = df.astype(BF16)
            df_ref[...] = dfb
            dh = jnp.zeros((TM, D), F32)
            for ci, (a, b) in enumerate(FF_CHUNKS):
                ready(2, ci)
                dact = lax.dot_general(dfb, wfo_v[a:b, :], NT, preferred_element_type=F32)
                gate = gu_ref[:, a:b].astype(F32)
                up = gu_ref[:, DFF + a:DFF + b].astype(F32)
                sig = 1.0 / (1.0 + jnp.exp(-gate))
                silu = gate * sig
                act_ref[:, a:b] = (silu * up).astype(BF16)
                dup = (dact * silu).astype(BF16)
                dgate = (dact * up * (sig * (1.0 + gate * (1.0 - sig)))).astype(BF16)
                dgu_ref[:, a:b] = dgate
                dgu_ref[:, DFF + a:DFF + b] = dup
                ready(0, ci)
                dh = dh + lax.dot_general(dgate, wfi_v[0, :, a:b], NT, preferred_element_type=F32)
                ready(1, ci)
                dh = dh + lax.dot_general(dup, wfi_v[1, :, a:b], NT, preferred_element_type=F32)
            xv = x_ref[...]
            gpre = gpre_ref[...]
            h_ref[...] = _rms(xv, gpre).astype(BF16)
            dxv, dgq = _rms_bwd(dh, xv, gpre)
            dgpre_ref[...] += dgq
            dxm_ref[...] = dxo + dxv

        _stream_ffn_weights(wfi_hbm, wfo_hbm, wfi_v, wfo_v, sems, (2, 0, 1), step)

    row = lambda w: pl.BlockSpec((TM, w), lambda i: (i, 0))
    return pl.pallas_call(
        _behind(body, 8, after), grid=(t // TM,),
        in_specs=[row(D), row(D), row(D), row(2 * DFF), _const((1, D)), _const((1, D)), _any(), _any()]
        + [_any()] * len(after),
        out_specs=[row(D), row(D), row(DFF), row(2 * DFF), row(D), _const((1, D)), _const((1, D))],
        out_shape=[jax.ShapeDtypeStruct((t, D), F32), jax.ShapeDtypeStruct((t, D), BF16),
                   jax.ShapeDtypeStruct((t, DFF), BF16), jax.ShapeDtypeStruct((t, 2 * DFF), BF16),
                   jax.ShapeDtypeStruct((t, D), BF16), jax.ShapeDtypeStruct((1, D), F32),
                   jax.ShapeDtypeStruct((1, D), F32)],
        scratch_shapes=[pltpu.VMEM((2, D, DFF), BF16), pltpu.VMEM((DFF, D), BF16), pltpu.SemaphoreType.DMA((6,))],
        compiler_params=_cp(("arbitrary",)), name="bwd_ffn")(dx, f, xmid, gu, g_pre, g_post, wfi_all, wfo_all, *after)


def bwd_mix(dxm, z, o, y, proj, wconv_t, g_co, g_ao, g_pm, gm, wout_all, after=()):
    t = dxm.shape[0]

    def body(dx_ref, z_ref, o_ref, y_ref, pc_ref, pcp_ref, wc_ref, gco_ref, gao_ref, gpm_ref, gm_ref, wout_hbm,
             dwo_ref, do_ref, dco_ref, dbg_ref, dgpm_ref, dgco_ref, dgao_ref, wout_v, cscr):
        first = pl.program_id(0) == 0

        @pl.when(first)
        def _():
            pltpu.sync_copy(wout_hbm, wout_v)
            dwo_ref[...] = jnp.zeros_like(dwo_ref)
            dgpm_ref[...] = jnp.zeros_like(dgpm_ref)
            dgco_ref[...] = jnp.zeros_like(dgco_ref)
            dgao_ref[...] = jnp.zeros_like(dgao_ref)

        dz, dgp = _rms_bwd(dx_ref[...], z_ref[...], gpm_ref[...])
        dgpm_ref[...] += dgp
        dzb = dz.astype(BF16)
        dwo_ref[...] += lax.dot_general(y_ref[...], dzb, TN, preferred_element_type=F32)
        gmv = gm_ref[...]
        _, bg, _, _, _, _, cout = _conv_fwd(pc_ref, pcp_ref, wc_ref, cscr, first)
        dy_conv = lax.dot_general(dzb, wout_v[0:CW, :], NT, preferred_element_type=F32)
        dyc, dgc = _group_rms_bwd(dy_conv, bg * cout, gco_ref[...], gmv)
        dgco_ref[...] += dgc
        dbg_ref[...] = (dyc * cout).astype(BF16)
        dco_ref[...] = dyc * bg
        dy_attn = lax.dot_general(dzb, wout_v[CW:2 * CW, :], NT, preferred_element_type=F32)
        do, dga = _group_rms_bwd(dy_attn, o_ref[...], gao_ref[...], gmv)
        dgao_ref[...] += dga
        do_ref[...] = do.astype(BF16)

    row = lambda w: pl.BlockSpec((TQ, w), lambda i: (i, 0))
    return pl.pallas_call(
        _behind(body, 12, after), grid=(t // TQ,),
        in_specs=[row(D), row(D), row(CW), row(D)] + _conv_specs() + [
            _const((8, CW)), _const((1, CW)), _const((1, CW)), _const((1, D)), _const((CW, CW)), _any()]
        + [_any()] * len(after),
        out_specs=[_const((D, D)), row(CW), row(CW), row(CW), _const((1, D)), _const((1, CW)), _const((1, CW))],
        out_shape=[jax.ShapeDtypeStruct((D, D), F32), jax.ShapeDtypeStruct((t, CW), BF16),
                   jax.ShapeDtypeStruct((t, CW), F32), jax.ShapeDtypeStruct((t, CW), BF16),
                   jax.ShapeDtypeStruct((1, D), F32), jax.ShapeDtypeStruct((1, CW), F32),
                   jax.ShapeDtypeStruct((1, CW), F32)],
        scratch_shapes=[pltpu.VMEM((D, D), BF16), pltpu.VMEM((TQ + 16, CW), F32)],
        compiler_params=_cp(("arbitrary",)), name="bwd_mix",
    )(dxm, z, o, y, proj, proj, wconv_t, g_co, g_ao, g_pm, gm, wout_all, *after)


def bwd_conv(dco, proj, wconv_t, after=()):
    t = dco.shape[0]
    nt = t // TQ

    def body(d_ref, dn_ref, pc_ref, pcp_ref, wc_ref, dhc_ref, dcg_ref, dw_ref, cscr, dscr):
        i = pl.program_id(0)
        first = i == 0

        @pl.when(first)
        def _():
            dw_ref[...] = jnp.zeros_like(dw_ref)

        hc, _, cg, u, u1, u2, _ = _conv_fwd(pc_ref, pcp_ref, wc_ref, cscr, first)
        d0 = d_ref[...]
        dscr[0:TQ, :] = d0
        dscr[TQ:TQ + 8, :] = jnp.where(i == nt - 1, 0.0, dn_ref[...])
        d1 = dscr[1:TQ + 1, :]
        d2 = dscr[2:TQ + 2, :]
        du = wc_ref[2:3, :] * d0 + wc_ref[1:2, :] * d1 + wc_ref[0:1, :] * d2
        dhc_ref[...] = (du * cg).astype(BF16)
        dcg_ref[...] = (du * hc).astype(BF16)
        dw_ref[0:1, :] += jnp.sum(d0 * u2, axis=0, keepdims=True)
        dw_ref[1:2, :] += jnp.sum(d0 * u1, axis=0, keepdims=True)
        dw_ref[2:3, :] += jnp.sum(d0 * u, axis=0, keepdims=True)

    row = lambda w: pl.BlockSpec((TQ, w), lambda i: (i, 0))
    nxt = pl.BlockSpec((8, CW), lambda i: (jnp.minimum((i + 1) * (TQ // 8), t // 8 - 1), 0))
    return pl.pallas_call(
        _behind(body, 5, after), grid=(nt,),
        in_specs=[row(CW), nxt] + _conv_specs() + [_const((8, CW))] + [_any()] * len(after),
        out_specs=[row(CW), row(CW), _const((8, CW))],
        out_shape=[jax.ShapeDtypeStruct((t, CW), BF16), jax.ShapeDtypeStruct((t, CW), BF16),
                   jax.ShapeDtypeStruct((8, CW), F32)],
        scratch_shapes=[pltpu.VMEM((TQ + 16, CW), F32), pltpu.VMEM((TQ + 8, CW), F32)],
        compiler_params=_cp(("arbitrary",)), name="bwd_conv")(dco, dco, proj, proj, wconv_t, *after)


def bwd_attn(proj, o, do, lse, bias2):
    t = o.shape[0]
    nt = t // TQ
    qg, kg = QG_BWD, QG_BWD + LEFT
    nkb = (t + TQ) // LANES

    def body(q_ref, kp_ref, kc_ref, vp_ref, vc_ref, o_ref, do_ref, lse_ref, b2_ref,
             dq_ref, dk_hbm, dv_hbm, db_hbm, kwin, vwin, dk_acc, dv_acc, db_acc, sems):
        i = pl.program_id(0)
        first = i == 0

        @pl.when(first)
        def _():
            dk_acc[...] = jnp.zeros_like(dk_acc)
            dv_acc[...] = jnp.zeros_like(dv_acc)
            db_acc[...] = jnp.zeros_like(db_acc)

        kwin[0:TQ, :] = kp_ref[...]
        kwin[TQ:2 * TQ, :] = kc_ref[...]
        vwin[0:TQ, :] = vp_ref[...]
        vwin[TQ:2 * TQ, :] = vc_ref[...]
        scale = HD ** -0.5
        qmask = _head_masks(scale)
        vmask = _head_masks(1.0)
        low = lax.broadcasted_iota(jnp.int32, (1, LANES), 1) < HD

        def group(g, carry):
            r0 = pl.multiple_of(g * qg, qg)
            base = i * (TQ // LANES) + g * (qg // LANES)
            pen = _key_penalty(first, r0, kg)
            for hp in range(NH // 2):
                ls = slice(LANES * hp, LANES * (hp + 1))
                qb = q_ref[pl.ds(r0, qg), ls]
                kw = kwin[pl.ds(r0, kg), ls]
                dob = do_ref[pl.ds(r0, qg), ls]
                prod = dob.astype(F32) * o_ref[pl.ds(r0, qg), ls]
                lseb = lse_ref[pl.ds(r0, qg), ls]
                q2 = jnp.concatenate([qb * qmask[0], qb * qmask[1]], axis=0)
                do2 = jnp.concatenate([dob * vmask[0], dob * vmask[1]], axis=0)
                lse2 = jnp.concatenate([lseb[:, 0:1], lseb[:, HD:HD + 1]], axis=0)
                dsum = jnp.concatenate([jnp.sum(jnp.where(low, prod, 0.0), axis=-1, keepdims=True),
                                        jnp.sum(jnp.where(low, 0.0, prod), axis=-1, keepdims=True)], axis=0)
                s = lax.dot_general(q2, kw, NT, preferred_element_type=F32) + b2_ref[hp] + pen
                p = jnp.exp(s - lse2)
                dp = lax.dot_general(do2, vwin[pl.ds(r0, kg), ls], NT, preferred_element_type=F32)
                ds = p * (dp - dsum)
                db_acc[hp] += ds
                dsb = ds.astype(BF16)
                dq2 = jnp.dot(dsb, kw, preferred_element_type=F32)
                dq_ref[pl.ds(r0, qg), ls] = (jnp.where(low, dq2[:qg], dq2[qg:]) * scale).astype(BF16)
                dkt = lax.dot_general(q2, dsb, TN, preferred_element_type=F32)
                dvt = lax.dot_general(do2, p.astype(BF16), TN, preferred_element_type=F32)
                for kb in range(kg // LANES):
                    dk_acc[base + kb, ls, :] += dkt[:, LANES * kb:LANES * (kb + 1)]
                    dv_acc[base + kb, ls, :] += dvt[:, LANES * kb:LANES * (kb + 1)]
            return carry

        lax.fori_loop(0, TQ // qg, group, 0)

        blocks = TQ // LANES

        def flush(step, n):
            sl = pl.ds(step * blocks, n)
            return [pltpu.make_async_copy(acc.at[sl], hbm.at[sl], sems.at[k])
                    for k, (acc, hbm) in enumerate(((dk_acc, dk_hbm), (dv_acc, dv_hbm)))]

        @pl.when(i > 0)
        def _():
            for cp in flush(i - 1, blocks):
                cp.wait()

        @pl.when(i < nt - 1)
        def _():
            for cp in flush(i, blocks):
                cp.start()

        @pl.when(i == nt - 1)
        def _():
            last = flush(i, 2 * blocks)
            for cp in last:
                cp.start()
            pltpu.sync_copy(db_acc, db_hbm)
            for cp in last:
                cp.wait()

    row = lambda w: pl.BlockSpec((TQ, w), lambda i: (i, 0))
    return pl.pallas_call(
        body, grid=(nt,),
        in_specs=_attn_window_specs() + [row(CW), row(CW), row(CW), _const((NH // 2, 2 * qg, kg))],
        out_specs=[row(CW), _any(), _any(), _any()],
        out_shape=[jax.ShapeDtypeStruct((t, CW), BF16), jax.ShapeDtypeStruct((nkb, CW, LANES), F32),
                   jax.ShapeDtypeStruct((nkb, CW, LANES), F32), jax.ShapeDtypeStruct((NH // 2, 2 * qg, kg), F32)],
        scratch_shapes=[pltpu.VMEM((2 * TQ, CW), BF16), pltpu.VMEM((2 * TQ, CW), BF16),
                        pltpu.VMEM((nkb, CW, LANES), F32), pltpu.VMEM((nkb, CW, LANES), F32),
                        pltpu.VMEM((NH // 2, 2 * qg, kg), F32), pltpu.SemaphoreType.DMA((2,))],
        compiler_params=_cp(("arbitrary",)), name="bwd_attn",
    )(proj, proj, proj, proj, proj, o, do, lse, bias2)


def bwd_inproj(dxm, x, dhc, dbg, dcg, dq, dk, dv, g, w_all):
    t = x.shape[0]
    nt = t // TQ
    assert nt >= 2
    wc = PROJ // NCHIP

    def body(dxm_ref, x_ref, dhc_ref, dbg_ref, dcg_ref, dq_ref, dk_ref, dv_ref, g_ref, w_hbm,
             dx_ref, dw_hbm, dg_ref, w_v, dp_ref, dw_acc, sems):
        @pl.when(pl.program_id(0) == 0)
        def _():
            dg_ref[...] = jnp.zeros_like(dg_ref)
            dw_acc[...] = jnp.zeros_like(dw_acc)

        def step(ready, done):
            dp_ref[:, 0:CW] = dhc_ref[...]
            dp_ref[:, CW:2 * CW] = dbg_ref[...]
            dp_ref[:, 2 * CW:3 * CW] = dcg_ref[...]
            dp_ref[:, 3 * CW:4 * CW] = dq_ref[...]
            for kb in range(TQ // LANES):
                rows = slice(LANES * kb, LANES * (kb + 1))
                dp_ref[rows, 4 * CW:5 * CW] = jnp.transpose(dk_ref[kb]).astype(BF16)
                dp_ref[rows, 5 * CW:6 * CW] = jnp.transpose(dv_ref[kb]).astype(BF16)
            xv = x_ref[...]
            gv = g_ref[...]
            hb = _rms(xv, gv).astype(BF16)
            for b in range(NCHIP):
                dw_acc[b] += lax.dot_general(hb, dp_ref[:, wc * b:wc * (b + 1)], TN, preferred_element_type=F32)
                done(b)
            dh = jnp.zeros((TQ, D), F32)
            for b in range(NCHIP):
                ready(b)
                dh = dh + lax.dot_general(dp_ref[:, wc * b:wc * (b + 1)], w_v[b], NT, preferred_element_type=F32)
            dxv, dgv = _rms_bwd(dh, xv, gv)
            dg_ref[...] += dgv
            dx_ref[...] = dxm_ref[...] + dxv

        i = pl.program_id(0)
        loads = [pltpu.make_async_copy(w_hbm.at[b], w_v.at[b], sems.at[b]) for b in range(NCHIP)]
        stores = [pltpu.make_async_copy(dw_acc.at[b], dw_hbm.at[b], sems.at[NCHIP + b]) for b in range(NCHIP)]
        nothing = lambda b: None

        @pl.when(i == 0)
        def _():
            for cp in loads:
                cp.start()
            step(lambda b: loads[b].wait(), nothing)

        @pl.when(jnp.logical_and(i > 0, i < nt - 1))
        def _():
            step(nothing, nothing)

        @pl.when(i == nt - 1)
        def _():
            step(nothing, lambda b: stores[b].start())
            for cp in stores:
                cp.wait()

    row = lambda w: pl.BlockSpec((TQ, w), lambda i: (i, 0))
    pad = pl.BlockSpec((TQ // LANES, CW, LANES), lambda i: (i + 1, 0, 0))
    return pl.pallas_call(
        body, grid=(nt,),
        in_specs=[row(D), row(D), row(CW), row(CW), row(CW), row(CW), pad, pad, _const((1, D)), _any()],
        out_specs=[row(D), _any(), _const((1, D))],
        out_shape=[jax.ShapeDtypeStruct((t, D), F32), jax.ShapeDtypeStruct((NCHIP, D, wc), F32),
                   jax.ShapeDtypeStruct((1, D), F32)],
        scratch_shapes=[pltpu.VMEM((NCHIP, D, wc), BF16), pltpu.VMEM((TQ, PROJ), BF16),
                        pltpu.VMEM((NCHIP, D, wc), F32), pltpu.SemaphoreType.DMA((2 * NCHIP,))],
        compiler_params=_cp(("arbitrary",)), name="bwd_inproj",
    )(dxm, x, dhc, dbg, dcg, dq, dk, dv, g, w_all)


def wgrad(a, b, kb, nb, by_columns, name):
    t, k = a.shape
    n = b.shape[1]
    tk = 512

    def body(a_ref, b_ref, o_ref):
        o_ref[...] = jnp.zeros_like(o_ref)
        for c in range(t // tk):
            o_ref[...] += lax.dot_general(a_ref[tk * c:tk * (c + 1), :], b_ref[tk * c:tk * (c + 1), :], TN,
                                          preferred_element_type=F32)

    if by_columns:
        assert nb == n // NCHIP
        out_spec = pl.BlockSpec((None, kb, nb), lambda ki, ni: (ni, ki, 0))
        out_shape = jax.ShapeDtypeStruct((NCHIP, k, nb), F32)
    else:
        assert nb == n
        out_spec = pl.BlockSpec((kb, nb), lambda ki, ni: (ki, 0))
        out_shape = jax.ShapeDtypeStruct((k, n), F32)
    return pl.pallas_call(
        body, grid=(k // kb, n // nb),
        in_specs=[pl.BlockSpec((t, kb), lambda ki, ni: (0, ki)), pl.BlockSpec((t, nb), lambda ki, ni: (0, ni))],
        out_specs=out_spec, out_shape=out_shape,
        compiler_params=_cp(("arbitrary", "arbitrary")), name=name)(a, b)


TOE = 1024
assert 2 * QG_FWD + LEFT <= TOE
N_FLAT = LEFT - REL_CLIP + 1
N_VAR = BAND - N_FLAT


def _diag_vector(table):
    last = table[:, 2 * REL_CLIP:]
    var = table[:, 2 * REL_CLIP - N_VAR:2 * REL_CLIP][:, ::-1]
    return jnp.concatenate([jnp.broadcast_to(last, (NH, N_FLAT)), var, jnp.broadcast_to(last, (NH, TOE - BAND))], axis=1)


def _diag_vector_bwd(dvec):
    dlast = jnp.sum(dvec[:, :N_FLAT], axis=1, keepdims=True) + jnp.sum(dvec[:, BAND:], axis=1, keepdims=True)
    dvar = dvec[:, N_FLAT:BAND][:, ::-1]
    return jnp.concatenate([jnp.zeros((NH, 2 * REL_CLIP - N_VAR), F32), dvar, dlast], axis=1)


def _band_valid(qg):
    r = lax.broadcasted_iota(jnp.int32, (qg, qg + LEFT), 0)
    p = lax.broadcasted_iota(jnp.int32, (qg, qg + LEFT), 1)
    start = lax.shift_left(lax.shift_right_logical(r, 6), 6)
    return (p >= start) & (p < start + BAND)


def bias_expand(vec, qgs, after=()):
    def body(v_ref, *o_refs):
        for qg, o_ref in zip(qgs, o_refs):
            valid = _band_valid(qg)
            for h in range(NH):
                rows = jnp.broadcast_to(v_ref[h:h + 1, :], (qg, TOE))
                toe = pltpu.roll(rows, 0, 1, stride=1, stride_axis=0)
                o_ref[h // 2, qg * (h % 2):qg * (h % 2 + 1), :] = jnp.where(valid, toe[:, :qg + LEFT], NEG_INF)

    vm = pl.BlockSpec(memory_space=pltpu.VMEM)
    return pl.pallas_call(_behind(body, 1, after), in_specs=[vm] + [_any()] * len(after), out_specs=[vm] * len(qgs),
                          out_shape=[jax.ShapeDtypeStruct((NH // 2, 2 * qg, qg + LEFT), F32) for qg in qgs],
                          name="bias_expand")(vec, *after)


def bias_reduce(db2):
    _, qg, kg = db2.shape

    def body(d_ref, o_ref):
        ii = lax.broadcasted_iota(jnp.int32, (kg, kg), 0)
        jj = lax.broadcasted_iota(jnp.int32, (kg, kg), 1)
        flip = jnp.where(ii + jj == kg - 1, 1.0, 0.0).astype(BF16)
        for h in range(NH):
            rest = d_ref[h]
            rev = jnp.zeros((qg, kg), F32)
            for _ in range(3):
                term = rest.astype(BF16)
                rev = rev + jnp.dot(term, flip, preferred_element_type=F32)
                rest = rest - term.astype(F32)
            d = jnp.concatenate([jnp.zeros((qg, TOE - kg), F32), rev], axis=1)
            back = pltpu.roll(d, 0, 1, stride=1, stride_axis=0)
            o_ref[h:h + 1, :] = jnp.sum(back, axis=0, keepdims=True)

    rev = pl.pallas_call(body, out_shape=jax.ShapeDtypeStruct((NH, TOE), F32), name="bias_reduce")(db2)
    return rev[:, ::-1]


def _place():
    x, y, c = lax.axis_index("x"), lax.axis_index("y"), lax.axis_index("c")
    chips = [(1 - x, y), (x, 1 - y), (1 - x, 1 - y)]
    return x, y, c, chips


def _half(ref_rows, c):
    return pl.ds(c * (ref_rows // 2), ref_rows // 2)


HBM_SPEC = pl.BlockSpec(memory_space=pltpu.HBM)
SEM_SPEC = pl.BlockSpec(memory_space=pltpu.SEMAPHORE)
IN_FLIGHT = pltpu.CompilerParams(has_side_effects=pltpu.SideEffectType.DATAFLOW_SIDE_EFFECTING)


def _in_hbm(a):
    return pltpu.with_memory_space_constraint(a, pltpu.HBM)


def cast_to_slot(ws, chip, layer, after=()):
    n = len(ws)
    steps = 4

    def body(b_ref, *refs):
        del b_ref
        for w_ref, o_ref in zip(refs[:n], refs[n + len(after):]):
            o_ref[...] = w_ref[...].astype(BF16)

    grid_spec = pltpu.PrefetchScalarGridSpec(
        num_scalar_prefetch=1, grid=(steps,),
        in_specs=[pl.BlockSpec((None, w.shape[1] // steps, w.shape[2]), lambda r, b: (layer, r, 0)) for w in ws]
        + [_any()] * len(after),
        out_specs=[pl.BlockSpec((None, w.shape[1] // steps, w.shape[2]), lambda r, b: (b[0], r, 0)) for w in ws])
    return pl.pallas_call(body, grid_spec=grid_spec,
                          out_shape=[jax.ShapeDtypeStruct((NCHIP,) + w.shape[1:], BF16) for w in ws],
                          compiler_params=_cp(("arbitrary",)), name="cast_to_slot")(chip, *ws, *after)


def _gather_copies(bufs, send, recv):
    x, y, c, chips = _place()
    b = 2 * x + y
    out = []
    for k, buf in enumerate(bufs):
        rows = buf.shape[1]
        mine = buf.at[b, _half(rows, c), :]
        for j, (cx, cy) in enumerate(chips):
            theirs = buf.at[2 * cx + cy, _half(rows, c), :]
            sems = dict(send_sem=send.at[3 * k + j], recv_sem=recv.at[3 * k + j],
                        device_id=(cx, cy, c), device_id_type=MESH)
            out.append((pltpu.make_async_remote_copy(src_ref=mine, dst_ref=mine, **sems),
                        pltpu.make_async_remote_copy(src_ref=theirs, dst_ref=theirs, **sems)))
    return out


def gather_start(bufs, after, layer):
    n = len(bufs)

    def body(*refs):
        ins = refs[:n]
        send, recv = refs[n + 1], refs[n + 2]
        token = refs[-1]
        for start, _ in _gather_copies(ins, send, recv):
            start.start()
        token[...] = jnp.zeros_like(token)

    sems = pltpu.SemaphoreType.DMA((3 * n,))
    res = pl.pallas_call(
        body, name=f"gather_start_{layer}",
        in_specs=[HBM_SPEC] * n + [_any()],
        out_specs=[SEM_SPEC, SEM_SPEC] + [HBM_SPEC] * n + [pl.BlockSpec(memory_space=pltpu.VMEM)],
        out_shape=[sems, sems] + [pltpu.HBM(b.shape, b.dtype) for b in bufs] + [jax.ShapeDtypeStruct((8, LANES), F32)],
        input_output_aliases={k: 2 + k for k in range(n)}, compiler_params=IN_FLIGHT,
    )(*[_in_hbm(b) for b in bufs], after)
    return res[0], res[1], res[2:2 + n], res[-1]


def gather_wait(send, recv, bufs, after, layer):
    n = len(bufs)

    def body(*refs):
        ins = refs[:n]
        send_ref, recv_ref = refs[n], refs[n + 1]
        for start, arrival in _gather_copies(ins, send_ref, recv_ref):
            start.wait_send()
            arrival.wait_recv()

    return pl.pallas_call(
        body, name=f"gather_wait_{layer}",
        in_specs=[HBM_SPEC] * n + [SEM_SPEC, SEM_SPEC, _any()], out_specs=[HBM_SPEC] * n,
        out_shape=[pltpu.HBM(b.shape, b.dtype) for b in bufs],
        input_output_aliases={k: k for k in range(n)}, compiler_params=IN_FLIGHT,
    )(*bufs, send, recv, after)


def gather_forward(bufs):
    n = len(bufs)

    def body(*refs):
        outs = refs[n:2 * n]
        send, recv = refs[2 * n:]
        x, y, c, chips = _place()
        cps = []
        for k in range(n):
            rows = outs[k].shape[1]
            for j, (cx, cy) in enumerate(chips):
                sems = dict(send_sem=send.at[3 * k + j], recv_sem=recv.at[3 * k + j],
                            device_id=(x, y, 1 - c), device_id_type=MESH)
                mine = outs[k].at[2 * cx + cy, _half(rows, c), :]
                theirs = outs[k].at[2 * cx + cy, _half(rows, 1 - c), :]
                cp = pltpu.make_async_remote_copy(src_ref=mine, dst_ref=mine, **sems)
                cp.start()
                cps.append((cp, pltpu.make_async_remote_copy(src_ref=theirs, dst_ref=theirs, **sems)))
        for cp, arrival in cps:
            cp.wait_send()
            arrival.wait_recv()

    return pl.pallas_call(
        body, in_specs=[_any()] * n, out_specs=[_any()] * n,
        out_shape=[jax.ShapeDtypeStruct(b.shape, b.dtype) for b in bufs], input_output_aliases={k: k for k in range(n)},
        scratch_shapes=[pltpu.SemaphoreType.DMA((3 * n,)), pltpu.SemaphoreType.DMA((3 * n,))],
        name="gather_forward")(*bufs)


def _forward_copies(bufs, send, recv):
    x, y, c, chips = _place()
    out = []
    for k, buf in enumerate(bufs):
        rows = buf.shape[1]
        for j, (cx, cy) in enumerate(chips):
            sems = dict(send_sem=send.at[3 * k + j], recv_sem=recv.at[3 * k + j],
                        device_id=(x, y, 1 - c), device_id_type=MESH)
            mine = buf.at[2 * cx + cy, _half(rows, c), :]
            theirs = buf.at[2 * cx + cy, _half(rows, 1 - c), :]
            out.append((pltpu.make_async_remote_copy(src_ref=mine, dst_ref=mine, **sems),
                        pltpu.make_async_remote_copy(src_ref=theirs, dst_ref=theirs, **sems)))
    return out


def forward_start(bufs, tag):
    n = len(bufs)

    def body(*refs):
        ins = refs[:n]
        send, recv = refs[n], refs[n + 1]
        token = refs[-1]
        for start, _ in _forward_copies(ins, send, recv):
            start.start()
        token[...] = jnp.zeros_like(token)

    sems = pltpu.SemaphoreType.DMA((3 * n,))
    res = pl.pallas_call(
        body, name=f"forward_start_{tag}", in_specs=[HBM_SPEC] * n,
        out_specs=[SEM_SPEC, SEM_SPEC] + [HBM_SPEC] * n + [pl.BlockSpec(memory_space=pltpu.VMEM)],
        out_shape=[sems, sems] + [pltpu.HBM(b.shape, b.dtype) for b in bufs] + [jax.ShapeDtypeStruct((8, LANES), F32)],
        input_output_aliases={k: 2 + k for k in range(n)}, compiler_params=IN_FLIGHT,
    )(*[_in_hbm(b) for b in bufs])
    return res[0], res[1], res[2:2 + n], res[-1]


def forward_wait(send, recv, bufs, after, tag):
    n = len(bufs)

    def body(*refs):
        ins = refs[:n]
        send_ref, recv_ref = refs[n], refs[n + 1]
        for start, arrival in _forward_copies(ins, send_ref, recv_ref):
            start.wait_send()
            arrival.wait_recv()

    return pl.pallas_call(
        body, name=f"forward_wait_{tag}",
        in_specs=[HBM_SPEC] * n + [SEM_SPEC, SEM_SPEC, _any()], out_specs=[HBM_SPEC] * n,
        out_shape=[pltpu.HBM(b.shape, b.dtype) for b in bufs],
        input_output_aliases={k: k for k in range(n)}, compiler_params=IN_FLIGHT,
    )(*bufs, send, recv, after)


def _exchange_copies(srcs, lands, send, recv):
    x, y, c, _ = _place()
    return [pltpu.make_async_remote_copy(
        src_ref=src.at[:, _half(src.shape[1], 1 - c), :], dst_ref=land, send_sem=send.at[k], recv_sem=recv.at[k],
        device_id=(x, y, 1 - c), device_id_type=MESH) for k, (src, land) in enumerate(zip(srcs, lands))]


def exchange_start(srcs, tag):
    n = len(srcs)
    lands = [lax.empty((s.shape[0], s.shape[1] // 2, s.shape[2]), s.dtype) for s in srcs]

    def body(*refs):
        ins, land_refs = refs[:n], refs[n:2 * n]
        send, recv = refs[2 * n], refs[2 * n + 1]
        token = refs[-1]
        for cp in _exchange_copies(ins, land_refs, send, recv):
            cp.start()
        token[...] = jnp.zeros_like(token)

    sems = pltpu.SemaphoreType.DMA((n,))
    res = pl.pallas_call(
        body, name=f"exchange_start_{tag}",
        in_specs=[HBM_SPEC] * (2 * n),
        out_specs=[SEM_SPEC, SEM_SPEC] + [HBM_SPEC] * (2 * n) + [pl.BlockSpec(memory_space=pltpu.VMEM)],
        out_shape=[sems, sems] + [pltpu.HBM(a.shape, a.dtype) for a in list(srcs) + lands]
        + [jax.ShapeDtypeStruct((8, LANES), F32)],
        input_output_aliases={k: 2 + k for k in range(2 * n)}, compiler_params=IN_FLIGHT,
    )(*[_in_hbm(a) for a in list(srcs) + lands])
    return res[0], res[1], res[2:2 + n], res[2 + n:2 + 2 * n], res[-1]


def exchange_wait(send, recv, srcs, lands, after, tag):
    n = len(srcs)

    def body(*refs):
        ins, land_refs = refs[:n], refs[n:2 * n]
        send_ref, recv_ref = refs[2 * n], refs[2 * n + 1]
        for cp in _exchange_copies(ins, land_refs, send_ref, recv_ref):
            cp.wait_send()
            cp.wait_recv()

    res = pl.pallas_call(
        body, name=f"exchange_wait_{tag}",
        in_specs=[HBM_SPEC] * (2 * n) + [SEM_SPEC, SEM_SPEC, _any()], out_specs=[HBM_SPEC] * (2 * n),
        out_shape=[pltpu.HBM(a.shape, a.dtype) for a in list(srcs) + list(lands)],
        input_output_aliases={k: k for k in range(2 * n)}, compiler_params=IN_FLIGHT,
    )(*srcs, *lands, send, recv, after)
    return res[:n], res[n:]


def add_pair(gs, r1s, core):
    n = len(gs)

    def body(c_ref, *refs):
        del c_ref
        for g_ref, r_ref, o_ref in zip(refs[:n], refs[n:2 * n], refs[2 * n:]):
            o_ref[...] = (g_ref[...] + r_ref[...]).astype(BF16)

    blk = lambda r: (None,) + r.shape[1:]
    grid_spec = pltpu.PrefetchScalarGridSpec(
        num_scalar_prefetch=1, grid=(NCHIP,),
        in_specs=[pl.BlockSpec(blk(r), lambda s, c: (s, c[0], 0)) for r in r1s]
        + [pl.BlockSpec(blk(r), lambda s, c: (s, 0, 0)) for r in r1s],
        out_specs=[pl.BlockSpec(blk(r), lambda s, c: (s, 0, 0)) for r in r1s])
    return pl.pallas_call(body, grid_spec=grid_spec, out_shape=[jax.ShapeDtypeStruct(r.shape, BF16) for r in r1s],
                          compiler_params=_cp(("arbitrary",)), name="add_pair")(core, *gs, *r1s)


def _scatter_copies(srcs, lands, send, recv):
    _, _, c, chips = _place()
    out = []
    for k, (src, land) in enumerate(zip(srcs, lands)):
        for j, (cx, cy) in enumerate(chips):
            out.append(pltpu.make_async_remote_copy(
                src_ref=src.at[2 * cx + cy], dst_ref=land.at[j], send_sem=send.at[3 * k + j],
                recv_sem=recv.at[3 * k + j], device_id=(cx, cy, c), device_id_type=MESH))
    return out


def scatter_start(srcs, layer):
    n = len(srcs)
    srcs = list(srcs)
    lands = [lax.empty((3,) + s.shape[1:], s.dtype) for s in srcs]

    def body(*refs):
        ins, land_refs = refs[:n], refs[n:2 * n]
        send, recv = refs[2 * n], refs[2 * n + 1]
        token = refs[-1]
        for cp in _scatter_copies(ins, land_refs, send, recv):
            cp.start()
        token[...] = jnp.zeros_like(token)

    sems = pltpu.SemaphoreType.DMA((3 * n,))
    res = pl.pallas_call(
        body, name=f"scatter_start_{layer}",
        in_specs=[HBM_SPEC] * (2 * n),
        out_specs=[SEM_SPEC, SEM_SPEC] + [HBM_SPEC] * (2 * n) + [pl.BlockSpec(memory_space=pltpu.VMEM)],
        out_shape=[sems, sems] + [pltpu.HBM(a.shape, a.dtype) for a in srcs + lands]
        + [jax.ShapeDtypeStruct((8, LANES), F32)],
        input_output_aliases={k: 2 + k for k in range(2 * n)}, compiler_params=IN_FLIGHT,
    )(*[_in_hbm(a) for a in srcs + lands])
    return res[0], res[1], res[2:2 + n], res[2 + n:2 + 2 * n], res[-1]


def scatter_wait(send, recv, srcs, lands, after, layer):
    n = len(srcs)

    def body(*refs):
        ins, land_refs = refs[:n], refs[n:2 * n]
        send_ref, recv_ref = refs[2 * n], refs[2 * n + 1]
        for cp in _scatter_copies(ins, land_refs, send_ref, recv_ref):
            cp.wait_send()
            cp.wait_recv()

    res = pl.pallas_call(
        body, name=f"scatter_wait_{layer}",
        in_specs=[HBM_SPEC] * (2 * n) + [SEM_SPEC, SEM_SPEC, _any()], out_specs=[HBM_SPEC] * (2 * n),
        out_shape=[pltpu.HBM(a.shape, a.dtype) for a in list(srcs) + list(lands)],
        input_output_aliases={k: k for k in range(2 * n)}, compiler_params=IN_FLIGHT,
    )(*srcs, *lands, send, recv, after)
    return res[n:]


def add_chips(gs, r1s, r2s, place, totals, layer):
    n = len(gs)
    steps = 2

    def body(p_ref, *refs):
        del p_ref
        for g_ref, r1_ref, r2_ref, o_ref in zip(refs[:n], refs[n:2 * n], refs[2 * n:3 * n], refs[4 * n:]):
            own = g_ref[...] + r1_ref[...]
            o_ref[...] = ((own + r2_ref[0].astype(F32)) + r2_ref[1].astype(F32)) + r2_ref[2].astype(F32)

    blk = lambda r: (None, r.shape[1] // steps, r.shape[2])
    grid_spec = pltpu.PrefetchScalarGridSpec(
        num_scalar_prefetch=1, grid=(steps,),
        in_specs=[pl.BlockSpec(blk(r), lambda i, p: (p[1], p[0] * steps + i, 0)) for r in r1s]
        + [pl.BlockSpec(blk(r), lambda i, p: (p[1], i, 0)) for r in r1s]
        + [pl.BlockSpec((3,) + blk(r)[1:], lambda i, p: (0, i, 0)) for r in r1s] + [_any()] * n,
        out_specs=[pl.BlockSpec(blk(r), lambda i, p: (layer, p[0] * steps + i, 0)) for r in r1s])
    return pl.pallas_call(body, grid_spec=grid_spec, out_shape=[jax.ShapeDtypeStruct(t.shape, F32) for t in totals],
                          input_output_aliases={1 + 3 * n + k: k for k in range(n)},
                          compiler_params=_cp(("arbitrary",)), name="add_chips")(place, *gs, *r1s, *r2s, *totals)


def _share_copies(bufs, send, recv):
    x, y, c, _ = _place()
    out = []
    for k, buf in enumerate(bufs):
        sems = dict(send_sem=send.at[k], recv_sem=recv.at[k], device_id=(x, y, 1 - c), device_id_type=MESH)
        mine = buf.at[:, _half(buf.shape[1], c), :]
        theirs = buf.at[:, _half(buf.shape[1], 1 - c), :]
        out.append((pltpu.make_async_remote_copy(src_ref=mine, dst_ref=mine, **sems),
                    pltpu.make_async_remote_copy(src_ref=theirs, dst_ref=theirs, **sems)))
    return out


def share_start(bufs, tag):
    n = len(bufs)

    def body(*refs):
        ins = refs[:n]
        send, recv = refs[n], refs[n + 1]
        token = refs[-1]
        for start, _ in _share_copies(ins, send, recv):
            start.start()
        token[...] = jnp.zeros_like(token)

    sems = pltpu.SemaphoreType.DMA((n,))
    res = pl.pallas_call(
        body, name=f"share_start_{tag}", in_specs=[HBM_SPEC] * n,
        out_specs=[SEM_SPEC, SEM_SPEC] + [HBM_SPEC] * n + [pl.BlockSpec(memory_space=pltpu.VMEM)],
        out_shape=[sems, sems] + [pltpu.HBM(b.shape, b.dtype) for b in bufs] + [jax.ShapeDtypeStruct((8, LANES), F32)],
        input_output_aliases={k: 2 + k for k in range(n)}, compiler_params=IN_FLIGHT,
    )(*[_in_hbm(b) for b in bufs])
    return res[0], res[1], res[2:2 + n], res[-1]


def share_wait(send, recv, bufs, after, tag):
    n = len(bufs)

    def body(*refs):
        ins = refs[:n]
        send_ref, recv_ref = refs[n], refs[n + 1]
        for start, arrival in _share_copies(ins, send_ref, recv_ref):
            start.wait_send()
            arrival.wait_recv()

    return pl.pallas_call(
        body, name=f"share_wait_{tag}",
        in_specs=[HBM_SPEC] * n + [SEM_SPEC, SEM_SPEC, _any()], out_specs=[HBM_SPEC] * n,
        out_shape=[pltpu.HBM(b.shape, b.dtype) for b in bufs],
        input_output_aliases={k: k for k in range(n)}, compiler_params=IN_FLIGHT,
    )(*bufs, send, recv, after)


def small_allreduce(v, after=()):
    rows = v.shape[0]
    flips = [(fx, fy, fc) for fx in (0, 1) for fy in (0, 1) for fc in (0, 1)][1:]

    def body(v_ref, o_ref, buf, send, recv):
        x, y, c, _ = _place()
        buf[4 * x + 2 * y + c] = v_ref[...]
        peers = [(jnp.where(fx, 1 - x, x), jnp.where(fy, 1 - y, y), jnp.where(fc, 1 - c, c)) for fx, fy, fc in flips]
        cps = []
        for k, peer in enumerate(peers):
            cp = pltpu.make_async_remote_copy(
                src_ref=v_ref, dst_ref=buf.at[4 * x + 2 * y + c], send_sem=send.at[k], recv_sem=recv.at[k],
                device_id=peer, device_id_type=MESH)
            cp.start()
            cps.append(cp)
        for k, (px, py, pc) in enumerate(peers):
            pltpu.make_async_remote_copy(
                src_ref=v_ref, dst_ref=buf.at[4 * px + 2 * py + pc], send_sem=send.at[k], recv_sem=recv.at[k],
                device_id=(px, py, pc), device_id_type=MESH).wait_recv()
        for cp in cps:
            cp.wait_send()
        acc = buf[0]
        for s in range(1, 8):
            acc = acc + buf[s]
        o_ref[...] = acc

    vm = pl.BlockSpec(memory_space=pltpu.VMEM)
    return pl.pallas_call(
        _behind(body, 1, after), in_specs=[vm] + [_any()] * len(after), out_specs=vm,
        out_shape=jax.ShapeDtypeStruct((rows, SMALL_COLS), F32),
        scratch_shapes=[pltpu.VMEM((8, rows, SMALL_COLS), F32), pltpu.SemaphoreType.DMA((7,)),
                        pltpu.SemaphoreType.DMA((7,))],
        name="reduce_small")(v, *after)


def adamw(w, g, m, v, rb, name, after=()):
    nl, rows, cols = w.shape

    def body(w_ref, g_ref, m_ref, v_ref, go_ref, d_ref, nm_ref, nv_ref):
        gv = g_ref[...]
        go_ref[...] = gv
        nm = ADAM_B1 * m_ref[...] + (1.0 - ADAM_B1) * gv
        nv = ADAM_B2 * v_ref[...] + (1.0 - ADAM_B2) * (gv * gv)
        m_hat = nm / (1.0 - ADAM_B1 ** ADAM_STEP)
        v_hat = nv / (1.0 - ADAM_B2 ** ADAM_STEP)
        d_ref[...] = -ADAM_LR * (m_hat / (jnp.sqrt(v_hat) + ADAM_EPS) + ADAM_WD * w_ref[...])
        nm_ref[...] = nm
        nv_ref[...] = nv

    blk = pl.BlockSpec((None, rb, cols), lambda l, r: (l, r, 0))
    shp = jax.ShapeDtypeStruct(w.shape, F32)
    return pl.pallas_call(_behind(body, 4, after), grid=(nl, rows // rb), in_specs=[blk] * 4 + [_any()] * len(after),
                          out_specs=[blk] * 4, out_shape=[shp] * 4,
                          compiler_params=_cp(("arbitrary", "arbitrary")), name=name)(w, g, m, v, *after)


def _pack(parts, rows):
    flat = jnp.concatenate([p.reshape(-1).astype(F32) for p in parts])
    return jnp.pad(flat, (0, rows * SMALL_COLS - flat.shape[0])).reshape(rows, SMALL_COLS)


def _unpack(vec, shapes):
    flat = vec.reshape(-1)
    out, off = [], 0
    for s in shapes:
        size = 1
        for d in s:
            size *= d
        out.append(flat[off:off + size].reshape(s))
        off += size
    return out


def kernel(x, w_in, w_conv, rel_bias, g_conv_out, g_attn_out, w_out, g_pre_mix, g_post_mix, g_pre_ffn, g_post_ffn, w_ffn_in, w_ffn_out, loss_target, m_w_in, m_w_conv, m_rel_bias, m_g_conv_out, m_g_attn_out, m_w_out, m_g_pre_mix, m_g_post_mix, m_g_pre_ffn, m_g_post_ffn, m_w_ffn_in, m_w_ffn_out, v_w_in, v_w_conv, v_rel_bias, v_g_conv_out, v_g_attn_out, v_w_out, v_g_pre_mix, v_g_post_mix, v_g_pre_ffn, v_g_post_ffn, v_w_ffn_in, v_w_ffn_out):
    xi, yi, ci = lax.axis_index("x"), lax.axis_index("y"), lax.axis_index("c")
    chip = 2 * xi + yi
    nl = w_in.shape[0]
    x0 = x[0]
    target = loss_target[0]
    cwl = CW // NCHIP

    chip1 = chip.reshape(1).astype(jnp.int32)
    big_weights = [w_in, w_out, w_ffn_in, w_ffn_out]
    own = [cast_to_slot(big_weights, chip1, 0)]
    wc_mine = jnp.pad(w_conv.reshape(-1), (0, 16 * LANES - w_conv.size)).reshape(1, 16, LANES)
    wc_slot = lax.dynamic_update_slice_in_dim(jnp.zeros((NCHIP, 16, LANES), F32), wc_mine, chip, axis=0)
    gm = jnp.kron(jnp.eye(CW // HD, dtype=F32), jnp.full((HD, HD), 1.0 / HD, F32)).astype(BF16)
    row = lambda a, l: a[l][None, :]

    def gather_finish(flight, after, tag):
        send, recv, bufs, _ = flight
        return gather_forward(gather_wait(send, recv, bufs, after, tag))

    first_mix = gather_start(list(own[0][:2]) + [wc_slot], x0, "0m")
    first_ffn = gather_start(own[0][2:], first_mix[3], "0f")
    chain = first_ffn[3]
    biases = []
    for l in range(nl):
        biases.append(bias_expand(_diag_vector(rel_bias[l]), (QG_FWD, QG_BWD), [chain]))
        chain = biases[l][1]
    for l in range(1, nl):
        own.append(cast_to_slot(big_weights, chip1, l, [chain]))
        chain = own[l][0]
    gw_in, gw_out, wc_all = gather_finish(first_mix, chain, "0m")
    wc_full = wc_all.reshape(NCHIP, -1)[:, :nl * cwl * 3].reshape(NCHIP, nl, cwl, 3)
    wc_full = jnp.transpose(wc_full, (1, 0, 2, 3)).reshape(nl, CW, 3)
    wconv_t = jnp.pad(jnp.transpose(wc_full, (0, 2, 1)), ((0, 0), (0, 5), (0, 0)))
    flights, to_sibling = {}, None
    saved, weights = [], []
    h = x0
    for l in range(nl):
        if l == 0:
            pass
        elif l == 1:
            flights[2] = gather_start(own[2], h, 2)
            gw_in, gw_out, gw_fi, gw_fo = gather_finish(flights[l], flights[2][3], l)
        else:
            gw_in, gw_out, gw_fi, gw_fo = forward_wait(*to_sibling[:3], h, l)
        gw_out = gw_out.reshape(D, D)
        behind_mix, behind_ffn = ([first_ffn[3]] if l == 0 else []), []
        if l + 1 < nl and l + 1 not in flights:
            flights[l + 1] = gather_start(own[l + 1], first_ffn[3] if l == 0 else gw_in, l + 1)
            behind_mix.append(flights[l + 1][3])
        bias2, bias2_bwd = biases[l]
        proj = fwd_inproj(h, row(g_pre_mix, l), gw_in, behind_mix)
        xmid, o, lse, y, z = fwd_mix(h, proj, bias2, wconv_t[l], row(g_conv_out, l), row(g_attn_out, l),
                                     row(g_post_mix, l), gm, gw_out)
        if l == 0:
            gw_fi, gw_fo = gather_finish(first_ffn, xmid, "0f")
        elif l + 1 < nl:
            send, recv, bufs, _ = flights[l + 1]
            landed = gather_wait(send, recv, bufs, xmid, l + 1)
            to_sibling = forward_start(landed, l + 1)
            behind_ffn.append(to_sibling[3])
            if l + 2 < nl:
                flights[l + 2] = gather_start(own[l + 2], to_sibling[3], l + 2)
                behind_ffn.append(flights[l + 2][3])
        gw_fo = gw_fo.reshape(2, DFF // 2, D)
        ffn = fwd_ffn(xmid, row(g_pre_ffn, l), row(g_post_ffn, l), gw_fi, gw_fo, behind_ffn,
                      target if l == nl - 1 else None)
        gu, f = ffn[:2]
        saved.append((h, proj, bias2_bwd, xmid, o, lse, y, z, gu, f))
        weights.append((gw_in, gw_out, gw_fi, gw_fo))
        h = ffn[2]
    dx, loss_blk = ffn[2], ffn[3]

    core = ci.reshape(1).astype(jnp.int32)
    place = jnp.stack([ci, chip]).astype(jnp.int32)
    totals = [lax.empty(w.shape, F32) for w in (w_in, w_out, w_ffn_in, w_ffn_out)]
    small = {k: [None] * nl for k in ("co", "ao", "pm", "qm", "pf", "qf", "rel", "wc")}

    def reduce_begin(kinds, grads, tag):
        return kinds, exchange_start(grads, tag), tag

    def reduce_mid(state, after):
        kinds, (send, recv, srcs, lands, _), tag = state
        grads, from_sibling = exchange_wait(send, recv, srcs, lands, after, tag)
        return kinds, grads, from_sibling, scatter_start(add_pair(grads, from_sibling, core), tag), tag

    def reduce_end(state, after, totals, layer):
        kinds, grads, from_sibling, (send, recv, srcs, lands, _), tag = state
        from_chips = scatter_wait(send, recv, srcs, lands, after, tag)
        totals = list(totals)
        summed = add_chips(grads, from_sibling, from_chips, place, [totals[i] for i in kinds], layer)
        for i, t in zip(kinds, summed):
            totals[i] = t
        return totals

    begun = flying = None
    for l in reversed(range(nl)):
        hin, proj, bias2, xmid, o, lse, y, z, gu, f = saved[l]
        gw_in, gw_out, gw_fi, gw_fo = weights[l]
        behind_ffn = [begun[1][4]] if begun is not None else []
        dxm, dfb, act, dgu, h2, dg_qf, dg_pf = bwd_ffn(dx, f, xmid, gu, row(g_pre_ffn, l), row(g_post_ffn, l),
                                                        gw_fi, gw_fo, behind_ffn)
        behind_mix, behind_conv = [], []
        if begun is not None:
            flying = reduce_mid(begun, dxm)
            behind_mix.append(flying[3][4])
        gr_fo = wgrad(act, dfb, 256, D, False, "wgrad_ffn_out").reshape(NCHIP, DFF // NCHIP, D)
        gr_fi = wgrad(h2, dgu, 512, 2 * DFF // NCHIP, True, "wgrad_ffn_in")
        if l == 0:
            begun_ffn = reduce_begin([2, 3], [gr_fi, gr_fo], "0f")
            behind_mix.append(begun_ffn[1][4])
        gr_out, do, dco, dbg, dg_qm, dg_co, dg_ao = bwd_mix(dxm, z, o, y, proj, wconv_t[l], row(g_conv_out, l),
                                                             row(g_attn_out, l), row(g_post_mix, l), gm, gw_out,
                                                             behind_mix)
        gr_out = gr_out.reshape(NCHIP, D // NCHIP, D)
        if l == 0:
            flying_ffn = reduce_mid(begun_ffn, do)
            behind_conv.append(flying_ffn[3][4])
        dhc, dcg, dwc = bwd_conv(dco, proj, wconv_t[l], behind_conv)
        dq, dk, dv, db2 = bwd_attn(proj, o, do, lse, bias2)
        dx, gr_in, dg_pm = bwd_inproj(dxm, hin, dhc, dbg, dcg, dq, dk, dv, row(g_pre_mix, l), gw_in)
        if flying is not None:
            totals = reduce_end(flying, dx, totals, l + 1)
        small["co"][l], small["ao"][l], small["pm"][l], small["qm"][l] = dg_co, dg_ao, dg_pm, dg_qm
        small["pf"][l], small["qf"][l] = dg_pf, dg_qf
        small["rel"][l] = _diag_vector_bwd(bias_reduce(db2.reshape(NH, QG_BWD, QG_BWD + LEFT)))
        small["wc"][l] = jnp.transpose(dwc[0:3], (1, 0))
        if l > 0:
            begun = reduce_begin([0, 1, 2, 3], [gr_in, gr_out, gr_fi, gr_fo], l)
    begun_mix = reduce_begin([0, 1], [gr_in, gr_out], "0m")
    totals = reduce_end(flying_ffn, begun_mix[1][4], totals, 0)
    flying_mix = reduce_mid(begun_mix, totals[2])
    share_ffn = share_start(totals[2:], "ffn")

    order = ("co", "ao", "pm", "qm", "pf", "qf", "rel", "wc")
    parts = [jnp.stack(small[k]) for k in order] + [loss_blk[0:1, 0:1]]
    shapes = [p.shape for p in parts]
    red_vec = small_allreduce(_pack(parts, 40), [share_ffn[3], flying_mix[3][4]])
    red = _unpack(red_vec, shapes)

    gr_fi, gr_fo = share_wait(*share_ffn[:3], red_vec, "ffn")
    big_fi = adamw(w_ffn_in, gr_fi, m_w_ffn_in, v_w_ffn_in, w_ffn_in.shape[1] // 4, "adamw_ffn_in")
    totals = reduce_end(flying_mix, big_fi[1], totals, 0)
    share_mix = share_start(totals[:2], "mix")
    big_fo = adamw(w_ffn_out, gr_fo, m_w_ffn_out, v_w_ffn_out, w_ffn_out.shape[1] // 4, "adamw_ffn_out",
                   [share_mix[3]])
    gr_in, gr_out = share_wait(*share_mix[:3], big_fo[1], "mix")
    big_in = adamw(w_in, gr_in, m_w_in, v_w_in, w_in.shape[1] // 4, "adamw_in")
    big_out = adamw(w_out, gr_out, m_w_out, v_w_out, w_out.shape[1] // 4, "adamw_out")
    big = [big_in, big_out, big_fi, big_fo]
    gr_co, gr_ao, gr_pm, gr_qm, gr_pf, gr_qf, gr_rel, gr_wc_full, loss = red
    gr_co, gr_ao, gr_pm, gr_qm, gr_pf, gr_qf = [a.reshape(nl, -1) for a in (gr_co, gr_ao, gr_pm, gr_qm, gr_pf, gr_qf)]
    gr_wc = lax.dynamic_slice_in_dim(gr_wc_full, chip * cwl, cwl, axis=1)
    loss = loss.reshape(())

    sw = [g_conv_out, g_attn_out, g_pre_mix, g_post_mix, g_pre_ffn, g_post_ffn, rel_bias, w_conv]
    sg = [gr_co, gr_ao, gr_pm, gr_qm, gr_pf, gr_qf, gr_rel, gr_wc]
    sm = [m_g_conv_out, m_g_attn_out, m_g_pre_mix, m_g_post_mix, m_g_pre_ffn, m_g_post_ffn, m_rel_bias, m_w_conv]
    sv = [v_g_conv_out, v_g_attn_out, v_g_pre_mix, v_g_post_mix, v_g_pre_ffn, v_g_post_ffn, v_rel_bias, v_w_conv]
    sshapes = [a.shape for a in sw]
    packed = [_pack(a, 32)[None] for a in (sw, sg, sm, sv)]
    s_out = [_unpack(a[0], sshapes) for a in adamw(*packed, 32, "adamw_small")]

    def leaves(big_i, small_i):
        b_in, b_out, b_fi, b_fo = big_i
        s_co, s_ao, s_pm, s_qm, s_pf, s_qf, s_rel, s_wc = small_i
        return [b_in, s_wc, s_rel, s_co, s_ao, b_out, s_pm, s_qm, s_pf, s_qf, b_fi, b_fo]

    out = [loss, dx[None]]
    out += leaves([b[0] for b in big], sg)
    for i in range(1, 4):
        out += leaves([b[i] for b in big], s_out[i])
    return tuple(out)
```

```python
import jax
import jax.numpy as jnp
from jax import lax
from jax.experimental import pallas as pl
from jax.experimental.pallas import tpu as pltpu

F32 = jnp.float32
BF16 = jnp.bfloat16

D = 1024
PROJ = 3072
CW = 512
HD = 64
NH = 8
CHUNK = 64
BAND = 576
REL_CLIP = 128
NREL = 2 * REL_CLIP + 1
DFF = 2816
DEPTH = 4
NCHIP = 4
EPS = 1e-6
NEG_INF = -1e30

ADAM_LR = 0.001
ADAM_B1 = 0.9
ADAM_B2 = 0.999
ADAM_EPS = 1e-08
ADAM_WD = 0.01
ADAM_STEP = 10

V7X_VMEM_BYTES = 64 * 1024 * 1024
VMEM_LIMIT = V7X_VMEM_BYTES - 8 * 1024 * 1024
LANES = 128
QG_FWD = 4 * CHUNK
QG_BWD = 2 * CHUNK
LEFT = BAND - CHUNK
TQ = 512
TM = 256
SMALL_COLS = 1024
MESH = pl.DeviceIdType.MESH
NT = (((1,), (1,)), ((), ()))
TN = (((0,), (0,)), ((), ()))


def _cp(sem=None, vmem=VMEM_LIMIT):
    return pltpu.CompilerParams(dimension_semantics=sem, vmem_limit_bytes=vmem)


def _any():
    return pl.BlockSpec(memory_space=pl.ANY)


def _const(shape):
    nd = len(shape)
    return pl.BlockSpec(shape, lambda *_: (0,) * nd)


def _behind(body, n_in, after):
    def ordered(*refs):
        return body(*refs[:n_in], *refs[n_in + len(after):])
    return ordered


def _rms(v, g):
    r = lax.rsqrt(jnp.mean(v * v, axis=-1, keepdims=True) + EPS)
    return v * r * g


def _rms_bwd(dy, v, g):
    r = lax.rsqrt(jnp.mean(v * v, axis=-1, keepdims=True) + EPS)
    vh = v * r
    dg = jnp.sum(dy * vh, axis=0, keepdims=True)
    dvh = dy * g
    dv = r * (dvh - vh * jnp.mean(dvh * vh, axis=-1, keepdims=True))
    return dv, dg


def _group_mean(v, gm):
    return jnp.dot(v.astype(BF16), gm, preferred_element_type=F32)


def _group_rms_bwd(dy, v, g, gm):
    r = lax.rsqrt(_group_mean(v * v, gm) + EPS)
    vh = v * r
    dg = jnp.sum(dy * vh, axis=0, keepdims=True)
    dvh = dy * g
    dv = r * (dvh - vh * _group_mean(dvh * vh, gm))
    return dv, dg


def _head_masks(scale):
    lane = lax.broadcasted_iota(jnp.int32, (1, LANES), 1)
    return [jnp.where((lane >= HD * a) & (lane < HD * (a + 1)), scale, 0.0).astype(BF16) for a in range(2)]


class _Resident:
    def __init__(self, src, dst, sem):
        self.first = pl.program_id(0) == 0
        self.copy = pltpu.make_async_copy(src, dst, sem)
        self.dst = dst

        @pl.when(self.first)
        def _():
            self.copy.start()

    def read(self):
        @pl.when(self.first)
        def _():
            self.copy.wait()

        return self.dst[...]


FF_CHUNKS = ((0, 1536), (1536, DFF))


def _stream_ffn_weights(wfi_hbm, wfo_hbm, wfi_v, wfo_v, sems, order, step):
    hw = DFF // 2
    per_matrix = {
        0: [(wfi_hbm.at[j], wfi_v.at[0, :, pl.ds(hw * j, hw)]) for j in range(2)],
        1: [(wfi_hbm.at[2 + j], wfi_v.at[1, :, pl.ds(hw * j, hw)]) for j in range(2)],
        2: [(wfo_hbm.at[j], wfo_v.at[pl.ds(hw * j, hw), :]) for j in range(2)],
    }
    pieces = [p for m in order for p in per_matrix[m]]
    slot = {m: 2 * k for k, m in enumerate(order)}

    def make_step(wait):
        def ready(m, chunk):
            if chunk == 0:
                wait(slot[m])
                wait(slot[m] + 1)
        return lambda: step(ready)

    copies = [pltpu.make_async_copy(src, dst, sems.at[k]) for k, (src, dst) in enumerate(pieces)]
    first = pl.program_id(0) == 0

    @pl.when(first)
    def _():
        for cp in copies:
            cp.start()
        make_step(lambda k: copies[k].wait())()

    @pl.when(jnp.logical_not(first))
    def _():
        make_step(lambda k: None)()


def _stream_shards(w_hbm, w_v, sems, step):
    copies = [pltpu.make_async_copy(w_hbm.at[b], w_v.at[b], sems.at[b]) for b in range(NCHIP)]
    first = pl.program_id(0) == 0

    @pl.when(first)
    def _():
        for cp in copies:
            cp.start()
        step(lambda b: copies[b].wait())

    @pl.when(jnp.logical_not(first))
    def _():
        step(lambda b: None)


def _conv_taps(u_prev, u, scr):
    n = u.shape[0]
    scr[0:16, :] = u_prev
    scr[16:16 + n, :] = u
    return scr[15:15 + n, :], scr[14:14 + n, :]


def fwd_inproj(x, g, w_all, after=()):
    t = x.shape[0]
    wc = PROJ // NCHIP

    def body(x_ref, g_ref, w_hbm, o_ref, w_v, sems):
        def step(ready):
            h = _rms(x_ref[...], g_ref[...]).astype(BF16)
            for b in range(NCHIP):
                ready(b)
                o_ref[:, wc * b:wc * (b + 1)] = jnp.dot(h, w_v[b], preferred_element_type=F32).astype(BF16)

        _stream_shards(w_hbm, w_v, sems, step)

    return pl.pallas_call(
        _behind(body, 3, after), grid=(t // TQ,),
        in_specs=[pl.BlockSpec((TQ, D), lambda i: (i, 0)), _const((1, D)), _any()] + [_any()] * len(after),
        out_specs=pl.BlockSpec((TQ, PROJ), lambda i: (i, 0)),
        out_shape=jax.ShapeDtypeStruct((t, PROJ), BF16),
        scratch_shapes=[pltpu.VMEM((NCHIP, D, wc), BF16), pltpu.SemaphoreType.DMA((NCHIP,))],
        compiler_params=_cp(("arbitrary",)), name="fwd_inproj")(x, g, w_all, *after)


def _attn_window_specs():
    return [
        pl.BlockSpec((TQ, CW), lambda i: (i, 3)),
        pl.BlockSpec((TQ, CW), lambda i: (jnp.maximum(i - 1, 0), 4)),
        pl.BlockSpec((TQ, CW), lambda i: (i, 4)),
        pl.BlockSpec((TQ, CW), lambda i: (jnp.maximum(i - 1, 0), 5)),
        pl.BlockSpec((TQ, CW), lambda i: (i, 5)),
    ]


def _conv_specs():
    return [
        pl.BlockSpec((TQ, 3 * CW), lambda i: (i, 0)),
        pl.BlockSpec((16, 3 * CW), lambda i: (jnp.maximum(i * (TQ // 16) - 1, 0), 0)),
    ]


def _conv_fwd(pc_ref, pcp_ref, wc_ref, scr, first):
    pc = pc_ref[...].astype(F32)
    hc, bg, cg = pc[:, :CW], pc[:, CW:2 * CW], pc[:, 2 * CW:]
    u = cg * hc
    pp = pcp_ref[...].astype(F32)
    u_prev = jnp.where(first, 0.0, pp[:, 2 * CW:] * pp[:, :CW])
    u1, u2 = _conv_taps(u_prev, u, scr)
    cout = wc_ref[0:1, :] * u2 + wc_ref[1:2, :] * u1 + wc_ref[2:3, :] * u
    return hc, bg, cg, u, u1, u2, cout


def _key_penalty(first, r0, kg):
    col = lax.broadcasted_iota(jnp.int32, (1, kg), 1)
    limit = jnp.where(first, TQ - r0, 0)
    return jnp.where(col < limit, NEG_INF, 0.0)


def fwd_mix(x, proj, bias2, wconv_t, g_co, g_ao, g_pm, gm, wout_all):
    t = x.shape[0]
    qg, kg = QG_FWD, QG_FWD + LEFT

    def body(x_ref, pc_ref, pcp_ref, q_ref, kp_ref, kc_ref, vp_ref, vc_ref, b2_ref, wc_ref, gco_ref, gao_ref, gpm_ref,
             gm_ref, wout_hbm, xmid_ref, o_ref, lse_ref, y_ref, z_ref, wout_v, kwin, vwin, cscr, sems):
        i = pl.program_id(0)
        first = i == 0
        wout = _Resident(wout_hbm, wout_v, sems.at[0])
        kwin[0:TQ, :] = kp_ref[...]
        kwin[TQ:2 * TQ, :] = kc_ref[...]
        vwin[0:TQ, :] = vp_ref[...]
        vwin[TQ:2 * TQ, :] = vc_ref[...]
        qmask = _head_masks(HD ** -0.5)
        low = lax.broadcasted_iota(jnp.int32, (1, LANES), 1) < HD

        def group(g, carry):
            r0 = pl.multiple_of(g * qg, qg)
            pen = _key_penalty(first, r0, kg)
            for hp in range(NH // 2):
                ls = slice(LANES * hp, LANES * (hp + 1))
                qb = q_ref[pl.ds(r0, qg), ls]
                q2 = jnp.concatenate([qb * qmask[0], qb * qmask[1]], axis=0)
                s = lax.dot_general(q2, kwin[pl.ds(r0, kg), ls], NT, preferred_element_type=F32)
                s = s + b2_ref[hp] + pen
                m = jnp.max(s, axis=-1, keepdims=True)
                p = jnp.exp(s - m)
                l = jnp.sum(p, axis=-1, keepdims=True)
                o2 = jnp.dot(p.astype(BF16), vwin[pl.ds(r0, kg), ls], preferred_element_type=F32) * (1.0 / l)
                lse2 = m + jnp.log(l)
                o_ref[pl.ds(r0, qg), ls] = jnp.where(low, o2[:qg], o2[qg:])
                lse_ref[pl.ds(r0, qg), ls] = jnp.where(low, lse2[:qg], lse2[qg:])
            return carry

        lax.fori_loop(0, TQ // qg, group, 0)

        _, bg, _, _, _, _, cout = _conv_fwd(pc_ref, pcp_ref, wc_ref, cscr, first)
        yc = bg * cout
        gmv = gm_ref[...]
        ycn = yc * lax.rsqrt(_group_mean(yc * yc, gmv) + EPS) * gco_ref[...]
        oa = o_ref[...]
        oan = oa * lax.rsqrt(_group_mean(oa * oa, gmv) + EPS) * gao_ref[...]
        y_ref[:, 0:CW] = ycn.astype(BF16)
        y_ref[:, CW:2 * CW] = oan.astype(BF16)
        z = jnp.dot(y_ref[...], wout.read(), preferred_element_type=F32)
        z_ref[...] = z
        xmid_ref[...] = x_ref[...] + _rms(z, gpm_ref[...])

    row = lambda w: pl.BlockSpec((TQ, w), lambda i: (i, 0))
    return pl.pallas_call(
        body, grid=(t // TQ,),
        in_specs=[row(D)] + _conv_specs() + _attn_window_specs() + [
            _const((NH // 2, 2 * qg, kg)), _const((8, CW)), _const((1, CW)), _const((1, CW)), _const((1, D)),
            _const((CW, CW)), _any()],
        out_specs=[row(D), row(CW), row(CW), row(D), row(D)],
        out_shape=[jax.ShapeDtypeStruct((t, D), F32), jax.ShapeDtypeStruct((t, CW), F32),
                   jax.ShapeDtypeStruct((t, CW), F32), jax.ShapeDtypeStruct((t, D), BF16),
                   jax.ShapeDtypeStruct((t, D), F32)],
        scratch_shapes=[pltpu.VMEM((D, D), BF16), pltpu.VMEM((2 * TQ, CW), BF16), pltpu.VMEM((2 * TQ, CW), BF16),
                        pltpu.VMEM((TQ + 16, CW), F32), pltpu.SemaphoreType.DMA((1,))],
        compiler_params=_cp(("arbitrary",)), name="fwd_mix",
    )(x, proj, proj, proj, proj, proj, proj, proj, bias2, wconv_t, g_co, g_ao, g_pm, gm, wout_all)


def fwd_ffn(xmid, g_pre, g_post, wfi_all, wfo_all, after=(), target=None):
    t = xmid.shape[0]
    n_in = 5 if target is None else 6

    def body(*refs):
        x_ref, gpre_ref, gpost_ref, wfi_hbm, wfo_hbm = refs[:5]
        t_ref = None if target is None else refs[5]
        gu_ref, f_ref, xo_ref = refs[n_in:n_in + 3]
        l_ref = None if target is None else refs[n_in + 3]
        wfi_v, wfo_v, sems = refs[-3:]

        if target is not None:
            @pl.when(pl.program_id(0) == 0)
            def _():
                l_ref[...] = jnp.zeros_like(l_ref)

        def step(ready):
            xv = x_ref[...]
            h = _rms(xv, gpre_ref[...]).astype(BF16)
            f = jnp.zeros((TQ, D), F32)
            for ci, (a, b) in enumerate(FF_CHUNKS):
                ready(0, ci)
                gate = jnp.dot(h, wfi_v[0, :, a:b], preferred_element_type=F32)
                ready(1, ci)
                up = jnp.dot(h, wfi_v[1, :, a:b], preferred_element_type=F32)
                gu_ref[:, a:b] = gate.astype(BF16)
                gu_ref[:, DFF + a:DFF + b] = up.astype(BF16)
                act = gate * (1.0 / (1.0 + jnp.exp(-gate))) * up
                ready(2, ci)
                f = f + jnp.dot(act.astype(BF16), wfo_v[a:b, :], preferred_element_type=F32)
            f_ref[...] = f
            xo = xv + _rms(f, gpost_ref[...])
            if target is None:
                xo_ref[...] = xo
            else:
                e = xo - t_ref[...]
                xo_ref[...] = e * (1.0 / D)
                rows = jnp.sum(e * e, axis=-1, keepdims=True) * (1.0 / D)
                l_ref[...] += 0.5 * jnp.sum(rows, axis=0, keepdims=True)

        _stream_ffn_weights(wfi_hbm, wfo_hbm, wfi_v, wfo_v, sems, (0, 1, 2), step)

    row = lambda w: pl.BlockSpec((TQ, w), lambda i: (i, 0))
    with_loss = target is not None
    return pl.pallas_call(
        _behind(body, n_in, after), grid=(t // TQ,),
        in_specs=[row(D), _const((1, D)), _const((1, D)), _any(), _any()] + [row(D)] * with_loss
        + [_any()] * len(after),
        out_specs=[row(2 * DFF), row(D), row(D)] + [_const((8, LANES))] * with_loss,
        out_shape=[jax.ShapeDtypeStruct((t, 2 * DFF), BF16), jax.ShapeDtypeStruct((t, D), F32),
                   jax.ShapeDtypeStruct((t, D), F32)] + [jax.ShapeDtypeStruct((8, LANES), F32)] * with_loss,
        scratch_shapes=[pltpu.VMEM((2, D, DFF), BF16), pltpu.VMEM((DFF, D), BF16), pltpu.SemaphoreType.DMA((6,))],
        compiler_params=_cp(("arbitrary",)), name="fwd_ffn_loss" if with_loss else "fwd_ffn",
    )(xmid, g_pre, g_post, wfi_all, wfo_all, *([target] * with_loss), *after)


def bwd_ffn(dx, f, xmid, gu, g_pre, g_post, wfi_all, wfo_all, after=()):
    t = dx.shape[0]

    def body(dx_ref, f_ref, x_ref, gu_ref, gpre_ref, gpost_ref, wfi_hbm, wfo_hbm,
             dxm_ref, df_ref, act_ref, dgu_ref, h_ref, dgpost_ref, dgpre_ref, wfi_v, wfo_v, sems):
        @pl.when(pl.program_id(0) == 0)
        def _():
            dgpost_ref[...] = jnp.zeros_like(dgpost_ref)
            dgpre_ref[...] = jnp.zeros_like(dgpre_ref)

        def step(ready):
            dxo = dx_ref[...]
            df, dgp = _rms_bwd(dxo, f_ref[...], gpost_ref[...])
            dgpost_ref[...] += dgp
            dfb = df.astype(BF16)
            df_ref[...] = dfb
            dh = jnp.zeros((TM, D), F32)
            for ci, (a, b) in enumerate(FF_CHUNKS):
                ready(2, ci)
                dact = lax.dot_general(dfb, wfo_v[a:b, :], NT, preferred_element_type=F32)
                gate = gu_ref[:, a:b].astype(F32)
                up = gu_ref[:, DFF + a:DFF + b].astype(F32)
                sig = 1.0 / (1.0 + jnp.exp(-gate))
                silu = gate * sig
                act_ref[:, a:b] = (silu * up).astype(BF16)
                dup = (dact * silu).astype(BF16)
                dgate = (dact * up * (sig * (1.0 + gate * (1.0 - sig)))).astype(BF16)
                dgu_ref[:, a:b] = dgate
                dgu_ref[:, DFF + a:DFF + b] = dup
                ready(0, ci)
                dh = dh + lax.dot_general(dgate, wfi_v[0, :, a:b], NT, preferred_element_type=F32)
                ready(1, ci)
                dh = dh + lax.dot_general(dup, wfi_v[1, :, a:b], NT, preferred_element_type=F32)
            xv = x_ref[...]
            gpre = gpre_ref[...]
            h_ref[...] = _rms(xv, gpre).astype(BF16)
            dxv, dgq = _rms_bwd(dh, xv, gpre)
            dgpre_ref[...] += dgq
            dxm_ref[...] = dxo + dxv

        _stream_ffn_weights(wfi_hbm, wfo_hbm, wfi_v, wfo_v, sems, (2, 0, 1), step)

    row = lambda w: pl.BlockSpec((TM, w), lambda i: (i, 0))
    return pl.pallas_call(
        _behind(body, 8, after), grid=(t // TM,),
        in_specs=[row(D), row(D), row(D), row(2 * DFF), _const((1, D)), _const((1, D)), _any(), _any()]
        + [_any()] * len(after),
        out_specs=[row(D), row(D), row(DFF), row(2 * DFF), row(D), _const((1, D)), _const((1, D))],
        out_shape=[jax.ShapeDtypeStruct((t, D), F32), jax.ShapeDtypeStruct((t, D), BF16),
                   jax.ShapeDtypeStruct((t, DFF), BF16), jax.ShapeDtypeStruct((t, 2 * DFF), BF16),
                   jax.ShapeDtypeStruct((t, D), BF16), jax.ShapeDtypeStruct((1, D), F32),
                   jax.ShapeDtypeStruct((1, D), F32)],
        scratch_shapes=[pltpu.VMEM((2, D, DFF), BF16), pltpu.VMEM((DFF, D), BF16), pltpu.SemaphoreType.DMA((6,))],
        compiler_params=_cp(("arbitrary",)), name="bwd_ffn")(dx, f, xmid, gu, g_pre, g_post, wfi_all, wfo_all, *after)


def bwd_mix(dxm, z, o, y, proj, wconv_t, g_co, g_ao, g_pm, gm, wout_all, after=()):
    t = dxm.shape[0]

    def body(dx_ref, z_ref, o_ref, y_ref, pc_ref, pcp_ref, wc_ref, gco_ref, gao_ref, gpm_ref, gm_ref, wout_hbm,
             dwo_ref, do_ref, dco_ref, dbg_ref, dgpm_ref, dgco_ref, dgao_ref, wout_v, cscr):
        first = pl.program_id(0) == 0

        @pl.when(first)
        def _():
            pltpu.sync_copy(wout_hbm, wout_v)
            dwo_ref[...] = jnp.zeros_like(dwo_ref)
            dgpm_ref[...] = jnp.zeros_like(dgpm_ref)
            dgco_ref[...] = jnp.zeros_like(dgco_ref)
            dgao_ref[...] = jnp.zeros_like(dgao_ref)

        dz, dgp = _rms_bwd(dx_ref[...], z_ref[...], gpm_ref[...])
        dgpm_ref[...] += dgp
        dzb = dz.astype(BF16)
        dwo_ref[...] += lax.dot_general(y_ref[...], dzb, TN, preferred_element_type=F32)
        gmv = gm_ref[...]
        _, bg, _, _, _, _, cout = _conv_fwd(pc_ref, pcp_ref, wc_ref, cscr, first)
        dy_conv = lax.dot_general(dzb, wout_v[0:CW, :], NT, preferred_element_type=F32)
        dyc, dgc = _group_rms_bwd(dy_conv, bg * cout, gco_ref[...], gmv)
        dgco_ref[...] += dgc
        dbg_ref[...] = (dyc * cout).astype(BF16)
        dco_ref[...] = dyc * bg
        dy_attn = lax.dot_general(dzb, wout_v[CW:2 * CW, :], NT, preferred_element_type=F32)
        do, dga = _group_rms_bwd(dy_attn, o_ref[...], gao_ref[...], gmv)
        dgao_ref[...] += dga
        do_ref[...] = do.astype(BF16)

    row = lambda w: pl.BlockSpec((TQ, w), lambda i: (i, 0))
    return pl.pallas_call(
        _behind(body, 12, after), grid=(t // TQ,),
        in_specs=[row(D), row(D), row(CW), row(D)] + _conv_specs() + [
            _const((8, CW)), _const((1, CW)), _const((1, CW)), _const((1, D)), _const((CW, CW)), _any()]
        + [_any()] * len(after),
        out_specs=[_const((D, D)), row(CW), row(CW), row(CW), _const((1, D)), _const((1, CW)), _const((1, CW))],
        out_shape=[jax.ShapeDtypeStruct((D, D), F32), jax.ShapeDtypeStruct((t, CW), BF16),
                   jax.ShapeDtypeStruct((t, CW), F32), jax.ShapeDtypeStruct((t, CW), BF16),
                   jax.ShapeDtypeStruct((1, D), F32), jax.ShapeDtypeStruct((1, CW), F32),
                   jax.ShapeDtypeStruct((1, CW), F32)],
        scratch_shapes=[pltpu.VMEM((D, D), BF16), pltpu.VMEM((TQ + 16, CW), F32)],
        compiler_params=_cp(("arbitrary",)), name="bwd_mix",
    )(dxm, z, o, y, proj, proj, wconv_t, g_co, g_ao, g_pm, gm, wout_all, *after)


def bwd_conv(dco, proj, wconv_t, after=()):
    t = dco.shape[0]
    nt = t // TQ

    def body(d_ref, dn_ref, pc_ref, pcp_ref, wc_ref, dhc_ref, dcg_ref, dw_ref, cscr, dscr):
        i = pl.program_id(0)
        first = i == 0

        @pl.when(first)
        def _():
            dw_ref[...] = jnp.zeros_like(dw_ref)

        hc, _, cg, u, u1, u2, _ = _conv_fwd(pc_ref, pcp_ref, wc_ref, cscr, first)
        d0 = d_ref[...]
        dscr[0:TQ, :] = d0
        dscr[TQ:TQ + 8, :] = jnp.where(i == nt - 1, 0.0, dn_ref[...])
        d1 = dscr[1:TQ + 1, :]
        d2 = dscr[2:TQ + 2, :]
        du = wc_ref[2:3, :] * d0 + wc_ref[1:2, :] * d1 + wc_ref[0:1, :] * d2
        dhc_ref[...] = (du * cg).astype(BF16)
        dcg_ref[...] = (du * hc).astype(BF16)
        dw_ref[0:1, :] += jnp.sum(d0 * u2, axis=0, keepdims=True)
        dw_ref[1:2, :] += jnp.sum(d0 * u1, axis=0, keepdims=True)
        dw_ref[2:3, :] += jnp.sum(d0 * u, axis=0, keepdims=True)

    row = lambda w: pl.BlockSpec((TQ, w), lambda i: (i, 0))
    nxt = pl.BlockSpec((8, CW), lambda i: (jnp.minimum((i + 1) * (TQ // 8), t // 8 - 1), 0))
    return pl.pallas_call(
        _behind(body, 5, after), grid=(nt,),
        in_specs=[row(CW), nxt] + _conv_specs() + [_const((8, CW))] + [_any()] * len(after),
        out_specs=[row(CW), row(CW), _const((8, CW))],
        out_shape=[jax.ShapeDtypeStruct((t, CW), BF16), jax.ShapeDtypeStruct((t, CW), BF16),
                   jax.ShapeDtypeStruct((8, CW), F32)],
        scratch_shapes=[pltpu.VMEM((TQ + 16, CW), F32), pltpu.VMEM((TQ + 8, CW), F32)],
        compiler_params=_cp(("arbitrary",)), name="bwd_conv")(dco, dco, proj, proj, wconv_t, *after)


def bwd_attn(proj, o, do, lse, bias2):
    t = o.shape[0]
    nt = t // TQ
    qg, kg = QG_BWD, QG_BWD + LEFT
    nkb = (t + TQ) // LANES

    def body(q_ref, kp_ref, kc_ref, vp_ref, vc_ref, o_ref, do_ref, lse_ref, b2_ref,
             dq_ref, dk_hbm, dv_hbm, db_hbm, kwin, vwin, dk_acc, dv_acc, db_acc, sems):
        i = pl.program_id(0)
        first = i == 0

        @pl.when(first)
        def _():
            dk_acc[...] = jnp.zeros_like(dk_acc)
            dv_acc[...] = jnp.zeros_like(dv_acc)
            db_acc[...] = jnp.zeros_like(db_acc)

        kwin[0:TQ, :] = kp_ref[...]
        kwin[TQ:2 * TQ, :] = kc_ref[...]
        vwin[0:TQ, :] = vp_ref[...]
        vwin[TQ:2 * TQ, :] = vc_ref[...]
        scale = HD ** -0.5
        qmask = _head_masks(scale)
        vmask = _head_masks(1.0)
        low = lax.broadcasted_iota(jnp.int32, (1, LANES), 1) < HD

        def group(g, carry):
            r0 = pl.multiple_of(g * qg, qg)
            base = i * (TQ // LANES) + g * (qg // LANES)
            pen = _key_penalty(first, r0, kg)
            for hp in range(NH // 2):
                ls = slice(LANES * hp, LANES * (hp + 1))
                qb = q_ref[pl.ds(r0, qg), ls]
                kw = kwin[pl.ds(r0, kg), ls]
                dob = do_ref[pl.ds(r0, qg), ls]
                prod = dob.astype(F32) * o_ref[pl.ds(r0, qg), ls]
                lseb = lse_ref[pl.ds(r0, qg), ls]
                q2 = jnp.concatenate([qb * qmask[0], qb * qmask[1]], axis=0)
                do2 = jnp.concatenate([dob * vmask[0], dob * vmask[1]], axis=0)
                lse2 = jnp.concatenate([lseb[:, 0:1], lseb[:, HD:HD + 1]], axis=0)
                dsum = jnp.concatenate([jnp.sum(jnp.where(low, prod, 0.0), axis=-1, keepdims=True),
                                        jnp.sum(jnp.where(low, 0.0, prod), axis=-1, keepdims=True)], axis=0)
                s = lax.dot_general(q2, kw, NT, preferred_element_type=F32) + b2_ref[hp] + pen
                p = jnp.exp(s - lse2)
                dp = lax.dot_general(do2, vwin[pl.ds(r0, kg), ls], NT, preferred_element_type=F32)
                ds = p * (dp - dsum)
                db_acc[hp] += ds
                dsb = ds.astype(BF16)
                dq2 = jnp.dot(dsb, kw, preferred_element_type=F32)
                dq_ref[pl.ds(r0, qg), ls] = (jnp.where(low, dq2[:qg], dq2[qg:]) * scale).astype(BF16)
                dkt = lax.dot_general(q2, dsb, TN, preferred_element_type=F32)
                dvt = lax.dot_general(do2, p.astype(BF16), TN, preferred_element_type=F32)
                for kb in range(kg // LANES):
                    dk_acc[base + kb, ls, :] += dkt[:, LANES * kb:LANES * (kb + 1)]
                    dv_acc[base + kb, ls, :] += dvt[:, LANES * kb:LANES * (kb + 1)]
            return carry

        lax.fori_loop(0, TQ // qg, group, 0)

        blocks = TQ // LANES

        def flush(step, n):
            sl = pl.ds(step * blocks, n)
            return [pltpu.make_async_copy(acc.at[sl], hbm.at[sl], sems.at[k])
                    for k, (acc, hbm) in enumerate(((dk_acc, dk_hbm), (dv_acc, dv_hbm)))]

        @pl.when(i > 0)
        def _():
            for cp in flush(i - 1, blocks):
                cp.wait()

        @pl.when(i < nt - 1)
        def _():
            for cp in flush(i, blocks):
                cp.start()

        @pl.when(i == nt - 1)
        def _():
            last = flush(i, 2 * blocks)
            for cp in last:
                cp.start()
            pltpu.sync_copy(db_acc, db_hbm)
            for cp in last:
                cp.wait()

    row = lambda w: pl.BlockSpec((TQ, w), lambda i: (i, 0))
    return pl.pallas_call(
        body, grid=(nt,),
        in_specs=_attn_window_specs() + [row(CW), row(CW), row(CW), _const((NH // 2, 2 * qg, kg))],
        out_specs=[row(CW), _any(), _any(), _any()],
        out_shape=[jax.ShapeDtypeStruct((t, CW), BF16), jax.ShapeDtypeStruct((nkb, CW, LANES), F32),
                   jax.ShapeDtypeStruct((nkb, CW, LANES), F32), jax.ShapeDtypeStruct((NH // 2, 2 * qg, kg), F32)],
        scratch_shapes=[pltpu.VMEM((2 * TQ, CW), BF16), pltpu.VMEM((2 * TQ, CW), BF16),
                        pltpu.VMEM((nkb, CW, LANES), F32), pltpu.VMEM((nkb, CW, LANES), F32),
                        pltpu.VMEM((NH // 2, 2 * qg, kg), F32), pltpu.SemaphoreType.DMA((2,))],
        compiler_params=_cp(("arbitrary",)), name="bwd_attn",
    )(proj, proj, proj, proj, proj, o, do, lse, bias2)


def bwd_inproj(dxm, x, dhc, dbg, dcg, dq, dk, dv, g, w_all):
    t = x.shape[0]
    nt = t // TQ
    assert nt >= 2
    wc = PROJ // NCHIP

    def body(dxm_ref, x_ref, dhc_ref, dbg_ref, dcg_ref, dq_ref, dk_ref, dv_ref, g_ref, w_hbm,
             dx_ref, dw_hbm, dg_ref, w_v, dp_ref, dw_acc, sems):
        @pl.when(pl.program_id(0) == 0)
        def _():
            dg_ref[...] = jnp.zeros_like(dg_ref)
            dw_acc[...] = jnp.zeros_like(dw_acc)

        def step(ready, done):
            dp_ref[:, 0:CW] = dhc_ref[...]
            dp_ref[:, CW:2 * CW] = dbg_ref[...]
            dp_ref[:, 2 * CW:3 * CW] = dcg_ref[...]
            dp_ref[:, 3 * CW:4 * CW] = dq_ref[...]
            for kb in range(TQ // LANES):
                rows = slice(LANES * kb, LANES * (kb + 1))
                dp_ref[rows, 4 * CW:5 * CW] = jnp.transpose(dk_ref[kb]).astype(BF16)
                dp_ref[rows, 5 * CW:6 * CW] = jnp.transpose(dv_ref[kb]).astype(BF16)
            xv = x_ref[...]
            gv = g_ref[...]
            hb = _rms(xv, gv).astype(BF16)
            for b in range(NCHIP):
                dw_acc[b] += lax.dot_general(hb, dp_ref[:, wc * b:wc * (b + 1)], TN, preferred_element_type=F32)
                done(b)
            dh = jnp.zeros((TQ, D), F32)
            for b in range(NCHIP):
                ready(b)
                dh = dh + lax.dot_general(dp_ref[:, wc * b:wc * (b + 1)], w_v[b], NT, preferred_element_type=F32)
            dxv, dgv = _rms_bwd(dh, xv, gv)
            dg_ref[...] += dgv
            dx_ref[...] = dxm_ref[...] + dxv

        i = pl.program_id(0)
        loads = [pltpu.make_async_copy(w_hbm.at[b], w_v.at[b], sems.at[b]) for b in range(NCHIP)]
        stores = [pltpu.make_async_copy(dw_acc.at[b], dw_hbm.at[b], sems.at[NCHIP + b]) for b in range(NCHIP)]
        nothing = lambda b: None

        @pl.when(i == 0)
        def _():
            for cp in loads:
                cp.start()
            step(lambda b: loads[b].wait(), nothing)

        @pl.when(jnp.logical_and(i > 0, i < nt - 1))
        def _():
            step(nothing, nothing)

        @pl.when(i == nt - 1)
        def _():
            step(nothing, lambda b: stores[b].start())
            for cp in stores:
                cp.wait()

    row = lambda w: pl.BlockSpec((TQ, w), lambda i: (i, 0))
    pad = pl.BlockSpec((TQ // LANES, CW, LANES), lambda i: (i + 1, 0, 0))
    return pl.pallas_call(
        body, grid=(nt,),
        in_specs=[row(D), row(D), row(CW), row(CW), row(CW), row(CW), pad, pad, _const((1, D)), _any()],
        out_specs=[row(D), _any(), _const((1, D))],
        out_shape=[jax.ShapeDtypeStruct((t, D), F32), jax.ShapeDtypeStruct((NCHIP, D, wc), F32),
                   jax.ShapeDtypeStruct((1, D), F32)],
        scratch_shapes=[pltpu.VMEM((NCHIP, D, wc), BF16), pltpu.VMEM((TQ, PROJ), BF16),
                        pltpu.VMEM((NCHIP, D, wc), F32), pltpu.SemaphoreType.DMA((2 * NCHIP,))],
        compiler_params=_cp(("arbitrary",)), name="bwd_inproj",
    )(dxm, x, dhc, dbg, dcg, dq, dk, dv, g, w_all)


def wgrad(a, b, kb, nb, by_columns, name):
    t, k = a.shape
    n = b.shape[1]
    tk = 512

    def body(a_ref, b_ref, o_ref):
        o_ref[...] = jnp.zeros_like(o_ref)
        for c in range(t // tk):
            o_ref[...] += lax.dot_general(a_ref[tk * c:tk * (c + 1), :], b_ref[tk * c:tk * (c + 1), :], TN,
                                          preferred_element_type=F32)

    if by_columns:
        assert nb == n // NCHIP
        out_spec = pl.BlockSpec((None, kb, nb), lambda ki, ni: (ni, ki, 0))
        out_shape = jax.ShapeDtypeStruct((NCHIP, k, nb), F32)
    else:
        assert nb == n
        out_spec = pl.BlockSpec((kb, nb), lambda ki, ni: (ki, 0))
        out_shape = jax.ShapeDtypeStruct((k, n), F32)
    return pl.pallas_call(
        body, grid=(k // kb, n // nb),
        in_specs=[pl.BlockSpec((t, kb), lambda ki, ni: (0, ki)), pl.BlockSpec((t, nb), lambda ki, ni: (0, ni))],
        out_specs=out_spec, out_shape=out_shape,
        compiler_params=_cp(("arbitrary", "arbitrary")), name=name)(a, b)


TOE = 1024
assert 2 * QG_FWD + LEFT <= TOE
N_FLAT = LEFT - REL_CLIP + 1
N_VAR = BAND - N_FLAT


def _diag_vector(table):
    last = table[:, 2 * REL_CLIP:]
    var = table[:, 2 * REL_CLIP - N_VAR:2 * REL_CLIP][:, ::-1]
    return jnp.concatenate([jnp.broadcast_to(last, (NH, N_FLAT)), var, jnp.broadcast_to(last, (NH, TOE - BAND))], axis=1)


def _diag_vector_bwd(dvec):
    dlast = jnp.sum(dvec[:, :N_FLAT], axis=1, keepdims=True) + jnp.sum(dvec[:, BAND:], axis=1, keepdims=True)
    dvar = dvec[:, N_FLAT:BAND][:, ::-1]
    return jnp.concatenate([jnp.zeros((NH, 2 * REL_CLIP - N_VAR), F32), dvar, dlast], axis=1)


def _band_valid(qg):
    r = lax.broadcasted_iota(jnp.int32, (qg, qg + LEFT), 0)
    p = lax.broadcasted_iota(jnp.int32, (qg, qg + LEFT), 1)
    start = lax.shift_left(lax.shift_right_logical(r, 6), 6)
    return (p >= start) & (p < start + BAND)


def bias_expand(vec, qgs, after=()):
    def body(v_ref, *o_refs):
        for qg, o_ref in zip(qgs, o_refs):
            valid = _band_valid(qg)
            for h in range(NH):
                rows = jnp.broadcast_to(v_ref[h:h + 1, :], (qg, TOE))
                toe = pltpu.roll(rows, 0, 1, stride=1, stride_axis=0)
                o_ref[h // 2, qg * (h % 2):qg * (h % 2 + 1), :] = jnp.where(valid, toe[:, :qg + LEFT], NEG_INF)

    vm = pl.BlockSpec(memory_space=pltpu.VMEM)
    return pl.pallas_call(_behind(body, 1, after), in_specs=[vm] + [_any()] * len(after), out_specs=[vm] * len(qgs),
                          out_shape=[jax.ShapeDtypeStruct((NH // 2, 2 * qg, qg + LEFT), F32) for qg in qgs],
                          name="bias_expand")(vec, *after)


def bias_reduce(db2):
    _, qg, kg = db2.shape

    def body(d_ref, o_ref):
        ii = lax.broadcasted_iota(jnp.int32, (kg, kg), 0)
        jj = lax.broadcasted_iota(jnp.int32, (kg, kg), 1)
        flip = jnp.where(ii + jj == kg - 1, 1.0, 0.0).astype(BF16)
        for h in range(NH):
            rest = d_ref[h]
            rev = jnp.zeros((qg, kg), F32)
            for _ in range(3):
                term = rest.astype(BF16)
                rev = rev + jnp.dot(term, flip, preferred_element_type=F32)
                rest = rest - term.astype(F32)
            d = jnp.concatenate([jnp.zeros((qg, TOE - kg), F32), rev], axis=1)
            back = pltpu.roll(d, 0, 1, stride=1, stride_axis=0)
            o_ref[h:h + 1, :] = jnp.sum(back, axis=0, keepdims=True)

    rev = pl.pallas_call(body, out_shape=jax.ShapeDtypeStruct((NH, TOE), F32), name="bias_reduce")(db2)
    return rev[:, ::-1]


def _place():
    x, y, c = lax.axis_index("x"), lax.axis_index("y"), lax.axis_index("c")
    chips = [(1 - x, y), (x, 1 - y), (1 - x, 1 - y)]
    return x, y, c, chips


def _half(ref_rows, c):
    return pl.ds(c * (ref_rows // 2), ref_rows // 2)


HBM_SPEC = pl.BlockSpec(memory_space=pltpu.HBM)
SEM_SPEC = pl.BlockSpec(memory_space=pltpu.SEMAPHORE)
IN_FLIGHT = pltpu.CompilerParams(has_side_effects=pltpu.SideEffectType.DATAFLOW_SIDE_EFFECTING)


def _in_hbm(a):
    return pltpu.with_memory_space_constraint(a, pltpu.HBM)


def cast_to_slot(ws, chip, layer, after=()):
    n = len(ws)
    steps = 4

    def body(b_ref, *refs):
        del b_ref
        for w_ref, o_ref in zip(refs[:n], refs[n + len(after):]):
            o_ref[...] = w_ref[...].astype(BF16)

    grid_spec = pltpu.PrefetchScalarGridSpec(
        num_scalar_prefetch=1, grid=(steps,),
        in_specs=[pl.BlockSpec((None, w.shape[1] // steps, w.shape[2]), lambda r, b: (layer, r, 0)) for w in ws]
        + [_any()] * len(after),
        out_specs=[pl.BlockSpec((None, w.shape[1] // steps, w.shape[2]), lambda r, b: (b[0], r, 0)) for w in ws])
    return pl.pallas_call(body, grid_spec=grid_spec,
                          out_shape=[jax.ShapeDtypeStruct((NCHIP,) + w.shape[1:], BF16) for w in ws],
                          compiler_params=_cp(("arbitrary",)), name="cast_to_slot")(chip, *ws, *after)


def _gather_copies(bufs, send, recv):
    x, y, c, chips = _place()
    b = 2 * x + y
    out = []
    for k, buf in enumerate(bufs):
        rows = buf.shape[1]
        mine = buf.at[b, _half(rows, c), :]
        for j, (cx, cy) in enumerate(chips):
            theirs = buf.at[2 * cx + cy, _half(rows, c), :]
            sems = dict(send_sem=send.at[3 * k + j], recv_sem=recv.at[3 * k + j],
                        device_id=(cx, cy, c), device_id_type=MESH)
            out.append((pltpu.make_async_remote_copy(src_ref=mine, dst_ref=mine, **sems),
                        pltpu.make_async_remote_copy(src_ref=theirs, dst_ref=theirs, **sems)))
    return out


def gather_start(bufs, after, layer):
    n = len(bufs)

    def body(*refs):
        ins = refs[:n]
        send, recv = refs[n + 1], refs[n + 2]
        token = refs[-1]
        for start, _ in _gather_copies(ins, send, recv):
            start.start()
        token[...] = jnp.zeros_like(token)

    sems = pltpu.SemaphoreType.DMA((3 * n,))
    res = pl.pallas_call(
        body, name=f"gather_start_{layer}",
        in_specs=[HBM_SPEC] * n + [_any()],
        out_specs=[SEM_SPEC, SEM_SPEC] + [HBM_SPEC] * n + [pl.BlockSpec(memory_space=pltpu.VMEM)],
        out_shape=[sems, sems] + [pltpu.HBM(b.shape, b.dtype) for b in bufs] + [jax.ShapeDtypeStruct((8, LANES), F32)],
        input_output_aliases={k: 2 + k for k in range(n)}, compiler_params=IN_FLIGHT,
    )(*[_in_hbm(b) for b in bufs], after)
    return res[0], res[1], res[2:2 + n], res[-1]


def gather_wait(send, recv, bufs, after, layer):
    n = len(bufs)

    def body(*refs):
        ins = refs[:n]
        send_ref, recv_ref = refs[n], refs[n + 1]
        for start, arrival in _gather_copies(ins, send_ref, recv_ref):
            start.wait_send()
            arrival.wait_recv()

    return pl.pallas_call(
        body, name=f"gather_wait_{layer}",
        in_specs=[HBM_SPEC] * n + [SEM_SPEC, SEM_SPEC, _any()], out_specs=[HBM_SPEC] * n,
        out_shape=[pltpu.HBM(b.shape, b.dtype) for b in bufs],
        input_output_aliases={k: k for k in range(n)}, compiler_params=IN_FLIGHT,
    )(*bufs, send, recv, after)


def gather_forward(bufs):
    n = len(bufs)

    def body(*refs):
        outs = refs[n:2 * n]
        send, recv = refs[2 * n:]
        x, y, c, chips = _place()
        cps = []
        for k in range(n):
            rows = outs[k].shape[1]
            for j, (cx, cy) in enumerate(chips):
                sems = dict(send_sem=send.at[3 * k + j], recv_sem=recv.at[3 * k + j],
                            device_id=(x, y, 1 - c), device_id_type=MESH)
                mine = outs[k].at[2 * cx + cy, _half(rows, c), :]
                theirs = outs[k].at[2 * cx + cy, _half(rows, 1 - c), :]
                cp = pltpu.make_async_remote_copy(src_ref=mine, dst_ref=mine, **sems)
                cp.start()
                cps.append((cp, pltpu.make_async_remote_copy(src_ref=theirs, dst_ref=theirs, **sems)))
        for cp, arrival in cps:
            cp.wait_send()
            arrival.wait_recv()

    return pl.pallas_call(
        body, in_specs=[_any()] * n, out_specs=[_any()] * n,
        out_shape=[jax.ShapeDtypeStruct(b.shape, b.dtype) for b in bufs], input_output_aliases={k: k for k in range(n)},
        scratch_shapes=[pltpu.SemaphoreType.DMA((3 * n,)), pltpu.SemaphoreType.DMA((3 * n,))],
        name="gather_forward")(*bufs)


def _forward_copies(bufs, send, recv):
    x, y, c, chips = _place()
    out = []
    for k, buf in enumerate(bufs):
        rows = buf.shape[1]
        for j, (cx, cy) in enumerate(chips):
            sems = dict(send_sem=send.at[3 * k + j], recv_sem=recv.at[3 * k + j],
                        device_id=(x, y, 1 - c), device_id_type=MESH)
            mine = buf.at[2 * cx + cy, _half(rows, c), :]
            theirs = buf.at[2 * cx + cy, _half(rows, 1 - c), :]
            out.append((pltpu.make_async_remote_copy(src_ref=mine, dst_ref=mine, **sems),
                        pltpu.make_async_remote_copy(src_ref=theirs, dst_ref=theirs, **sems)))
    return out


def forward_start(bufs, tag):
    n = len(bufs)

    def body(*refs):
        ins = refs[:n]
        send, recv = refs[n], refs[n + 1]
        token = refs[-1]
        for start, _ in _forward_copies(ins, send, recv):
            start.start()
        token[...] = jnp.zeros_like(token)

    sems = pltpu.SemaphoreType.DMA((3 * n,))
    res = pl.pallas_call(
        body, name=f"forward_start_{tag}", in_specs=[HBM_SPEC] * n,
        out_specs=[SEM_SPEC, SEM_SPEC] + [HBM_SPEC] * n + [pl.BlockSpec(memory_space=pltpu.VMEM)],
        out_shape=[sems, sems] + [pltpu.HBM(b.shape, b.dtype) for b in bufs] + [jax.ShapeDtypeStruct((8, LANES), F32)],
        input_output_aliases={k: 2 + k for k in range(n)}, compiler_params=IN_FLIGHT,
    )(*[_in_hbm(b) for b in bufs])
    return res[0], res[1], res[2:2 + n], res[-1]


def forward_wait(send, recv, bufs, after, tag):
    n = len(bufs)

    def body(*refs):
        ins = refs[:n]
        send_ref, recv_ref = refs[n], refs[n + 1]
        for start, arrival in _forward_copies(ins, send_ref, recv_ref):
            start.wait_send()
            arrival.wait_recv()

    return pl.pallas_call(
        body, name=f"forward_wait_{tag}",
        in_specs=[HBM_SPEC] * n + [SEM_SPEC, SEM_SPEC, _any()], out_specs=[HBM_SPEC] * n,
        out_shape=[pltpu.HBM(b.shape, b.dtype) for b in bufs],
        input_output_aliases={k: k for k in range(n)}, compiler_params=IN_FLIGHT,
    )(*bufs, send, recv, after)


def _exchange_copies(srcs, lands, send, recv):
    x, y, c, _ = _place()
    return [pltpu.make_async_remote_copy(
        src_ref=src.at[:, _half(src.shape[1], 1 - c), :], dst_ref=land, send_sem=send.at[k], recv_sem=recv.at[k],
        device_id=(x, y, 1 - c), device_id_type=MESH) for k, (src, land) in enumerate(zip(srcs, lands))]


def exchange_start(srcs, tag):
    n = len(srcs)
    lands = [lax.empty((s.shape[0], s.shape[1] // 2, s.shape[2]), s.dtype) for s in srcs]

    def body(*refs):
        ins, land_refs = refs[:n], refs[n:2 * n]
        send, recv = refs[2 * n], refs[2 * n + 1]
        token = refs[-1]
        for cp in _exchange_copies(ins, land_refs, send, recv):
            cp.start()
        token[...] = jnp.zeros_like(token)

    sems = pltpu.SemaphoreType.DMA((n,))
    res = pl.pallas_call(
        body, name=f"exchange_start_{tag}",
        in_specs=[HBM_SPEC] * (2 * n),
        out_specs=[SEM_SPEC, SEM_SPEC] + [HBM_SPEC] * (2 * n) + [pl.BlockSpec(memory_space=pltpu.VMEM)],
        out_shape=[sems, sems] + [pltpu.HBM(a.shape, a.dtype) for a in list(srcs) + lands]
        + [jax.ShapeDtypeStruct((8, LANES), F32)],
        input_output_aliases={k: 2 + k for k in range(2 * n)}, compiler_params=IN_FLIGHT,
    )(*[_in_hbm(a) for a in list(srcs) + lands])
    return res[0], res[1], res[2:2 + n], res[2 + n:2 + 2 * n], res[-1]


def exchange_wait(send, recv, srcs, lands, after, tag):
    n = len(srcs)

    def body(*refs):
        ins, land_refs = refs[:n], refs[n:2 * n]
        send_ref, recv_ref = refs[2 * n], refs[2 * n + 1]
        for cp in _exchange_copies(ins, land_refs, send_ref, recv_ref):
            cp.wait_send()
            cp.wait_recv()

    res = pl.pallas_call(
        body, name=f"exchange_wait_{tag}",
        in_specs=[HBM_SPEC] * (2 * n) + [SEM_SPEC, SEM_SPEC, _any()], out_specs=[HBM_SPEC] * (2 * n),
        out_shape=[pltpu.HBM(a.shape, a.dtype) for a in list(srcs) + list(lands)],
        input_output_aliases={k: k for k in range(2 * n)}, compiler_params=IN_FLIGHT,
    )(*srcs, *lands, send, recv, after)
    return res[:n], res[n:]


def add_pair(gs, r1s, core):
    n = len(gs)

    def body(c_ref, *refs):
        del c_ref
        for g_ref, r_ref, o_ref in zip(refs[:n], refs[n:2 * n], refs[2 * n:]):
            o_ref[...] = (g_ref[...] + r_ref[...]).astype(BF16)

    blk = lambda r: (None,) + r.shape[1:]
    grid_spec = pltpu.PrefetchScalarGridSpec(
        num_scalar_prefetch=1, grid=(NCHIP,),
        in_specs=[pl.BlockSpec(blk(r), lambda s, c: (s, c[0], 0)) for r in r1s]
        + [pl.BlockSpec(blk(r), lambda s, c: (s, 0, 0)) for r in r1s],
        out_specs=[pl.BlockSpec(blk(r), lambda s, c: (s, 0, 0)) for r in r1s])
    return pl.pallas_call(body, grid_spec=grid_spec, out_shape=[jax.ShapeDtypeStruct(r.shape, BF16) for r in r1s],
                          compiler_params=_cp(("arbitrary",)), name="add_pair")(core, *gs, *r1s)


def _scatter_copies(srcs, lands, send, recv):
    _, _, c, chips = _place()
    out = []
    for k, (src, land) in enumerate(zip(srcs, lands)):
        for j, (cx, cy) in enumerate(chips):
            out.append(pltpu.make_async_remote_copy(
                src_ref=src.at[2 * cx + cy], dst_ref=land.at[j], send_sem=send.at[3 * k + j],
                recv_sem=recv.at[3 * k + j], device_id=(cx, cy, c), device_id_type=MESH))
    return out


def scatter_start(srcs, layer):
    n = len(srcs)
    srcs = list(srcs)
    lands = [lax.empty((3,) + s.shape[1:], s.dtype) for s in srcs]

    def body(*refs):
        ins, land_refs = refs[:n], refs[n:2 * n]
        send, recv = refs[2 * n], refs[2 * n + 1]
        token = refs[-1]
        for cp in _scatter_copies(ins, land_refs, send, recv):
            cp.start()
        token[...] = jnp.zeros_like(token)

    sems = pltpu.SemaphoreType.DMA((3 * n,))
    res = pl.pallas_call(
        body, name=f"scatter_start_{layer}",
        in_specs=[HBM_SPEC] * (2 * n),
        out_specs=[SEM_SPEC, SEM_SPEC] + [HBM_SPEC] * (2 * n) + [pl.BlockSpec(memory_space=pltpu.VMEM)],
        out_shape=[sems, sems] + [pltpu.HBM(a.shape, a.dtype) for a in srcs + lands]
        + [jax.ShapeDtypeStruct((8, LANES), F32)],
        input_output_aliases={k: 2 + k for k in range(2 * n)}, compiler_params=IN_FLIGHT,
    )(*[_in_hbm(a) for a in srcs + lands])
    return res[0], res[1], res[2:2 + n], res[2 + n:2 + 2 * n], res[-1]


def scatter_wait(send, recv, srcs, lands, after, layer):
    n = len(srcs)

    def body(*refs):
        ins, land_refs = refs[:n], refs[n:2 * n]
        send_ref, recv_ref = refs[2 * n], refs[2 * n + 1]
        for cp in _scatter_copies(ins, land_refs, send_ref, recv_ref):
            cp.wait_send()
            cp.wait_recv()

    res = pl.pallas_call(
        body, name=f"scatter_wait_{layer}",
        in_specs=[HBM_SPEC] * (2 * n) + [SEM_SPEC, SEM_SPEC, _any()], out_specs=[HBM_SPEC] * (2 * n),
        out_shape=[pltpu.HBM(a.shape, a.dtype) for a in list(srcs) + list(lands)],
        input_output_aliases={k: k for k in range(2 * n)}, compiler_params=IN_FLIGHT,
    )(*srcs, *lands, send, recv, after)
    return res[n:]


def add_chips(gs, r1s, r2s, place, totals, layer):
    n = len(gs)
    steps = 2

    def body(p_ref, *refs):
        del p_ref
        for g_ref, r1_ref, r2_ref, o_ref in zip(refs[:n], refs[n:2 * n], refs[2 * n:3 * n], refs[4 * n:]):
            own = g_ref[...] + r1_ref[...]
            o_ref[...] = ((own + r2_ref[0].astype(F32)) + r2_ref[1].astype(F32)) + r2_ref[2].astype(F32)

    blk = lambda r: (None, r.shape[1] // steps, r.shape[2])
    grid_spec = pltpu.PrefetchScalarGridSpec(
        num_scalar_prefetch=1, grid=(steps,),
        in_specs=[pl.BlockSpec(blk(r), lambda i, p: (p[1], p[0] * steps + i, 0)) for r in r1s]
        + [pl.BlockSpec(blk(r), lambda i, p: (p[1], i, 0)) for r in r1s]
        + [pl.BlockSpec((3,) + blk(r)[1:], lambda i, p: (0, i, 0)) for r in r1s] + [_any()] * n,
        out_specs=[pl.BlockSpec(blk(r), lambda i, p: (layer, p[0] * steps + i, 0)) for r in r1s])
    return pl.pallas_call(body, grid_spec=grid_spec, out_shape=[jax.ShapeDtypeStruct(t.shape, F32) for t in totals],
                          input_output_aliases={1 + 3 * n + k: k for k in range(n)},
                          compiler_params=_cp(("arbitrary",)), name="add_chips")(place, *gs, *r1s, *r2s, *totals)


def _share_copies(bufs, send, recv):
    x, y, c, _ = _place()
    out = []
    for k, buf in enumerate(bufs):
        sems = dict(send_sem=send.at[k], recv_sem=recv.at[k], device_id=(x, y, 1 - c), device_id_type=MESH)
        mine = buf.at[:, _half(buf.shape[1], c), :]
        theirs = buf.at[:, _half(buf.shape[1], 1 - c), :]
        out.append((pltpu.make_async_remote_copy(src_ref=mine, dst_ref=mine, **sems),
                    pltpu.make_async_remote_copy(src_ref=theirs, dst_ref=theirs, **sems)))
    return out


def share_start(bufs, tag):
    n = len(bufs)

    def body(*refs):
        ins = refs[:n]
        send, recv = refs[n], refs[n + 1]
        token = refs[-1]
        for start, _ in _share_copies(ins, send, recv):
            start.start()
        token[...] = jnp.zeros_like(token)

    sems = pltpu.SemaphoreType.DMA((n,))
    res = pl.pallas_call(
        body, name=f"share_start_{tag}", in_specs=[HBM_SPEC] * n,
        out_specs=[SEM_SPEC, SEM_SPEC] + [HBM_SPEC] * n + [pl.BlockSpec(memory_space=pltpu.VMEM)],
        out_shape=[sems, sems] + [pltpu.HBM(b.shape, b.dtype) for b in bufs] + [jax.ShapeDtypeStruct((8, LANES), F32)],
        input_output_aliases={k: 2 + k for k in range(n)}, compiler_params=IN_FLIGHT,
    )(*[_in_hbm(b) for b in bufs])
    return res[0], res[1], res[2:2 + n], res[-1]


def share_wait(send, recv, bufs, after, tag):
    n = len(bufs)

    def body(*refs):
        ins = refs[:n]
        send_ref, recv_ref = refs[n], refs[n + 1]
        for start, arrival in _share_copies(ins, send_ref, recv_ref):
            start.wait_send()
            arrival.wait_recv()

    return pl.pallas_call(
        body, name=f"share_wait_{tag}",
        in_specs=[HBM_SPEC] * n + [SEM_SPEC, SEM_SPEC, _any()], out_specs=[HBM_SPEC] * n,
        out_shape=[pltpu.HBM(b.shape, b.dtype) for b in bufs],
        input_output_aliases={k: k for k in range(n)}, compiler_params=IN_FLIGHT,
    )(*bufs, send, recv, after)


def small_allreduce(v, after=()):
    rows = v.shape[0]
    flips = [(fx, fy, fc) for fx in (0, 1) for fy in (0, 1) for fc in (0, 1)][1:]

    def body(v_ref, o_ref, buf, send, recv):
        x, y, c, _ = _place()
        buf[4 * x + 2 * y + c] = v_ref[...]
        peers = [(jnp.where(fx, 1 - x, x), jnp.where(fy, 1 - y, y), jnp.where(fc, 1 - c, c)) for fx, fy, fc in flips]
        cps = []
        for k, peer in enumerate(peers):
            cp = pltpu.make_async_remote_copy(
                src_ref=v_ref, dst_ref=buf.at[4 * x + 2 * y + c], send_sem=send.at[k], recv_sem=recv.at[k],
                device_id=peer, device_id_type=MESH)
            cp.start()
            cps.append(cp)
        for k, (px, py, pc) in enumerate(peers):
            pltpu.make_async_remote_copy(
                src_ref=v_ref, dst_ref=buf.at[4 * px + 2 * py + pc], send_sem=send.at[k], recv_sem=recv.at[k],
                device_id=(px, py, pc), device_id_type=MESH).wait_recv()
        for cp in cps:
            cp.wait_send()
        acc = buf[0]
        for s in range(1, 8):
            acc = acc + buf[s]
        o_ref[...] = acc

    vm = pl.BlockSpec(memory_space=pltpu.VMEM)
    return pl.pallas_call(
        _behind(body, 1, after), in_specs=[vm] + [_any()] * len(after), out_specs=vm,
        out_shape=jax.ShapeDtypeStruct((rows, SMALL_COLS), F32),
        scratch_shapes=[pltpu.VMEM((8, rows, SMALL_COLS), F32), pltpu.SemaphoreType.DMA((7,)),
                        pltpu.SemaphoreType.DMA((7,))],
        name="reduce_small")(v, *after)


def adamw(w, g, m, v, rb, name, after=()):
    nl, rows, cols = w.shape

    def body(w_ref, g_ref, m_ref, v_ref, go_ref, d_ref, nm_ref, nv_ref):
        gv = g_ref[...]
        go_ref[...] = gv
        nm = ADAM_B1 * m_ref[...] + (1.0 - ADAM_B1) * gv
        nv = ADAM_B2 * v_ref[...] + (1.0 - ADAM_B2) * (gv * gv)
        m_hat = nm / (1.0 - ADAM_B1 ** ADAM_STEP)
        v_hat = nv / (1.0 - ADAM_B2 ** ADAM_STEP)
        d_ref[...] = -ADAM_LR * (m_hat / (jnp.sqrt(v_hat) + ADAM_EPS) + ADAM_WD * w_ref[...])
        nm_ref[...] = nm
        nv_ref[...] = nv

    blk = pl.BlockSpec((None, rb, cols), lambda l, r: (l, r, 0))
    shp = jax.ShapeDtypeStruct(w.shape, F32)
    return pl.pallas_call(_behind(body, 4, after), grid=(nl, rows // rb), in_specs=[blk] * 4 + [_any()] * len(after),
                          out_specs=[blk] * 4, out_shape=[shp] * 4,
                          compiler_params=_cp(("arbitrary", "arbitrary")), name=name)(w, g, m, v, *after)


def _pack(parts, rows):
    flat = jnp.concatenate([p.reshape(-1).astype(F32) for p in parts])
    return jnp.pad(flat, (0, rows * SMALL_COLS - flat.shape[0])).reshape(rows, SMALL_COLS)


def _unpack(vec, shapes):
    flat = vec.reshape(-1)
    out, off = [], 0
    for s in shapes:
        size = 1
        for d in s:
            size *= d
        out.append(flat[off:off + size].reshape(s))
        off += size
    return out


def kernel(x, w_in, w_conv, rel_bias, g_conv_out, g_attn_out, w_out, g_pre_mix, g_post_mix, g_pre_ffn, g_post_ffn, w_ffn_in, w_ffn_out, loss_target, m_w_in, m_w_conv, m_rel_bias, m_g_conv_out, m_g_attn_out, m_w_out, m_g_pre_mix, m_g_post_mix, m_g_pre_ffn, m_g_post_ffn, m_w_ffn_in, m_w_ffn_out, v_w_in, v_w_conv, v_rel_bias, v_g_conv_out, v_g_attn_out, v_w_out, v_g_pre_mix, v_g_post_mix, v_g_pre_ffn, v_g_post_ffn, v_w_ffn_in, v_w_ffn_out):
    xi, yi, ci = lax.axis_index("x"), lax.axis_index("y"), lax.axis_index("c")
    chip = 2 * xi + yi
    nl = w_in.shape[0]
    x0 = x[0]
    target = loss_target[0]
    cwl = CW // NCHIP

    chip1 = chip.reshape(1).astype(jnp.int32)
    big_weights = [w_in, w_out, w_ffn_in, w_ffn_out]
    own = [cast_to_slot(big_weights, chip1, 0)]
    wc_mine = jnp.pad(w_conv.reshape(-1), (0, 16 * LANES - w_conv.size)).reshape(1, 16, LANES)
    wc_slot = lax.dynamic_update_slice_in_dim(jnp.zeros((NCHIP, 16, LANES), F32), wc_mine, chip, axis=0)
    gm = jnp.kron(jnp.eye(CW // HD, dtype=F32), jnp.full((HD, HD), 1.0 / HD, F32)).astype(BF16)
    row = lambda a, l: a[l][None, :]

    def gather_finish(flight, after, tag):
        send, recv, bufs, _ = flight
        return gather_forward(gather_wait(send, recv, bufs, after, tag))

    first_mix = gather_start(list(own[0][:2]) + [wc_slot], x0, "0m")
    first_ffn = gather_start(own[0][2:], first_mix[3], "0f")
    chain = first_ffn[3]
    biases = []
    for l in range(nl):
        biases.append(bias_expand(_diag_vector(rel_bias[l]), (QG_FWD, QG_BWD), [chain]))
        chain = biases[l][1]
    for l in range(1, nl):
        own.append(cast_to_slot(big_weights, chip1, l, [chain]))
        chain = own[l][0]
    gw_in, gw_out, wc_all = gather_finish(first_mix, chain, "0m")
    wc_full = wc_all.reshape(NCHIP, -1)[:, :nl * cwl * 3].reshape(NCHIP, nl, cwl, 3)
    wc_full = jnp.transpose(wc_full, (1, 0, 2, 3)).reshape(nl, CW, 3)
    wconv_t = jnp.pad(jnp.transpose(wc_full, (0, 2, 1)), ((0, 0), (0, 5), (0, 0)))
    flights, to_sibling = {}, None
    saved, weights = [], []
    h = x0
    for l in range(nl):
        if l == 0:
            pass
        elif l == 1:
            flights[2] = gather_start(own[2], h, 2)
            gw_in, gw_out, gw_fi, gw_fo = gather_finish(flights[l], flights[2][3], l)
        else:
            gw_in, gw_out, gw_fi, gw_fo = forward_wait(*to_sibling[:3], h, l)
        gw_out = gw_out.reshape(D, D)
        behind_mix, behind_ffn = ([first_ffn[3]] if l == 0 else []), []
        if l + 1 < nl and l + 1 not in flights:
            flights[l + 1] = gather_start(own[l + 1], first_ffn[3] if l == 0 else gw_in, l + 1)
            behind_mix.append(flights[l + 1][3])
        bias2, bias2_bwd = biases[l]
        proj = fwd_inproj(h, row(g_pre_mix, l), gw_in, behind_mix)
        xmid, o, lse, y, z = fwd_mix(h, proj, bias2, wconv_t[l], row(g_conv_out, l), row(g_attn_out, l),
                                     row(g_post_mix, l), gm, gw_out)
        if l == 0:
            gw_fi, gw_fo = gather_finish(first_ffn, xmid, "0f")
        elif l + 1 < nl:
            send, recv, bufs, _ = flights[l + 1]
            landed = gather_wait(send, recv, bufs, xmid, l + 1)
            to_sibling = forward_start(landed, l + 1)
            behind_ffn.append(to_sibling[3])
            if l + 2 < nl:
                flights[l + 2] = gather_start(own[l + 2], to_sibling[3], l + 2)
                behind_ffn.append(flights[l + 2][3])
        gw_fo = gw_fo.reshape(2, DFF // 2, D)
        ffn = fwd_ffn(xmid, row(g_pre_ffn, l), row(g_post_ffn, l), gw_fi, gw_fo, behind_ffn,
                      target if l == nl - 1 else None)
        gu, f = ffn[:2]
        saved.append((h, proj, bias2_bwd, xmid, o, lse, y, z, gu, f))
        weights.append((gw_in, gw_out, gw_fi, gw_fo))
        h = ffn[2]
    dx, loss_blk = ffn[2], ffn[3]

    core = ci.reshape(1).astype(jnp.int32)
    place = jnp.stack([ci, chip]).astype(jnp.int32)
    totals = [lax.empty(w.shape, F32) for w in (w_in, w_out, w_ffn_in, w_ffn_out)]
    small = {k: [None] * nl for k in ("co", "ao", "pm", "qm", "pf", "qf", "rel", "wc")}

    def reduce_begin(kinds, grads, tag):
        return kinds, exchange_start(grads, tag), tag

    def reduce_mid(state, after):
        kinds, (send, recv, srcs, lands, _), tag = state
        grads, from_sibling = exchange_wait(send, recv, srcs, lands, after, tag)
        return kinds, grads, from_sibling, scatter_start(add_pair(grads, from_sibling, core), tag), tag

    def reduce_end(state, after, totals, layer):
        kinds, grads, from_sibling, (send, recv, srcs, lands, _), tag = state
        from_chips = scatter_wait(send, recv, srcs, lands, after, tag)
        totals = list(totals)
        summed = add_chips(grads, from_sibling, from_chips, place, [totals[i] for i in kinds], layer)
        for i, t in zip(kinds, summed):
            totals[i] = t
        return totals

    begun = flying = None
    for l in reversed(range(nl)):
        hin, proj, bias2, xmid, o, lse, y, z, gu, f = saved[l]
        gw_in, gw_out, gw_fi, gw_fo = weights[l]
        behind_ffn = [begun[1][4]] if begun is not None else []
        dxm, dfb, act, dgu, h2, dg_qf, dg_pf = bwd_ffn(dx, f, xmid, gu, row(g_pre_ffn, l), row(g_post_ffn, l),
                                                        gw_fi, gw_fo, behind_ffn)
        behind_mix, behind_conv = [], []
        if begun is not None:
            flying = reduce_mid(begun, dxm)
            behind_mix.append(flying[3][4])
        gr_fo = wgrad(act, dfb, 256, D, False, "wgrad_ffn_out").reshape(NCHIP, DFF // NCHIP, D)
        gr_fi = wgrad(h2, dgu, D, 2 * DFF // NCHIP, True, "wgrad_ffn_in")
        if l == 0:
            begun_ffn = reduce_begin([2, 3], [gr_fi, gr_fo], "0f")
            behind_mix.append(begun_ffn[1][4])
        gr_out, do, dco, dbg, dg_qm, dg_co, dg_ao = bwd_mix(dxm, z, o, y, proj, wconv_t[l], row(g_conv_out, l),
                                                             row(g_attn_out, l), row(g_post_mix, l), gm, gw_out,
                                                             behind_mix)
        gr_out = gr_out.reshape(NCHIP, D // NCHIP, D)
        if l == 0:
            flying_ffn = reduce_mid(begun_ffn, do)
            behind_conv.append(flying_ffn[3][4])
        dhc, dcg, dwc = bwd_conv(dco, proj, wconv_t[l], behind_conv)
        dq, dk, dv, db2 = bwd_attn(proj, o, do, lse, bias2)
        dx, gr_in, dg_pm = bwd_inproj(dxm, hin, dhc, dbg, dcg, dq, dk, dv, row(g_pre_mix, l), gw_in)
        if flying is not None:
            totals = reduce_end(flying, dx, totals, l + 1)
        small["co"][l], small["ao"][l], small["pm"][l], small["qm"][l] = dg_co, dg_ao, dg_pm, dg_qm
        small["pf"][l], small["qf"][l] = dg_pf, dg_qf
        small["rel"][l] = _diag_vector_bwd(bias_reduce(db2.reshape(NH, QG_BWD, QG_BWD + LEFT)))
        small["wc"][l] = jnp.transpose(dwc[0:3], (1, 0))
        if l > 0:
            begun = reduce_begin([0, 1, 2, 3], [gr_in, gr_out, gr_fi, gr_fo], l)
    begun_mix = reduce_begin([0, 1], [gr_in, gr_out], "0m")
    totals = reduce_end(flying_ffn, begun_mix[1][4], totals, 0)
    flying_mix = reduce_mid(begun_mix, totals[2])
    share_ffn = share_start(totals[2:], "ffn")

    order = ("co", "ao", "pm", "qm", "pf", "qf", "rel", "wc")
    parts = [jnp.stack(small[k]) for k in order] + [loss_blk[0:1, 0:1]]
    shapes = [p.shape for p in parts]
    red_vec = small_allreduce(_pack(parts, 40), [share_ffn[3], flying_mix[3][4]])
    red = _unpack(red_vec, shapes)

    gr_fi, gr_fo = share_wait(*share_ffn[:3], red_vec, "ffn")
    big_fi = adamw(w_ffn_in, gr_fi, m_w_ffn_in, v_w_ffn_in, w_ffn_in.shape[1] // 4, "adamw_ffn_in")
    totals = reduce_end(flying_mix, big_fi[1], totals, 0)
    share_mix = share_start(totals[:2], "mix")
    big_fo = adamw(w_ffn_out, gr_fo, m_w_ffn_out, v_w_ffn_out, w_ffn_out.shape[1] // 4, "adamw_ffn_out",
                   [share_mix[3]])
    gr_in, gr_out = share_wait(*share_mix[:3], big_fo[1], "mix")
    big_in = adamw(w_in, gr_in, m_w_in, v_w_in, w_in.shape[1] // 4, "adamw_in")
    big_out = adamw(w_out, gr_out, m_w_out, v_w_out, w_out.shape[1] // 4, "adamw_out")
    big = [big_in, big_out, big_fi, big_fo]
    gr_co, gr_ao, gr_pm, gr_qm, gr_pf, gr_qf, gr_rel, gr_wc_full, loss = red
    gr_co, gr_ao, gr_pm, gr_qm, gr_pf, gr_qf = [a.reshape(nl, -1) for a in (gr_co, gr_ao, gr_pm, gr_qm, gr_pf, gr_qf)]
    gr_wc = lax.dynamic_slice_in_dim(gr_wc_full, chip * cwl, cwl, axis=1)
    loss = loss.reshape(())

    sw = [g_conv_out, g_attn_out, g_pre_mix, g_post_mix, g_pre_ffn, g_post_ffn, rel_bias, w_conv]
    sg = [gr_co, gr_ao, gr_pm, gr_qm, gr_pf, gr_qf, gr_rel, gr_wc]
    sm = [m_g_conv_out, m_g_attn_out, m_g_pre_mix, m_g_post_mix, m_g_pre_ffn, m_g_post_ffn, m_rel_bias, m_w_conv]
    sv = [v_g_conv_out, v_g_attn_out, v_g_pre_mix, v_g_post_mix, v_g_pre_ffn, v_g_post_ffn, v_rel_bias, v_w_conv]
    sshapes = [a.shape for a in sw]
    packed = [_pack(a, 32)[None] for a in (sw, sg, sm, sv)]
    s_out = [_unpack(a[0], sshapes) for a in adamw(*packed, 32, "adamw_small")]

    def leaves(big_i, small_i):
        b_in, b_out, b_fi, b_fo = big_i
        s_co, s_ao, s_pm, s_qm, s_pf, s_qf, s_rel, s_wc = small_i
        return [b_in, s_wc, s_rel, s_co, s_ao, b_out, s_pm, s_qm, s_pf, s_qf, b_fi, b_fo]

    out = [loss, dx[None]]
    out += leaves([b[0] for b in big], sg)
    for i in range(1, 4):
        out += leaves([b[i] for b in big], s_out[i])
    return tuple(out)
```
